```python
import math
import jax, jax.numpy as jnp
from jax import lax
import numpy as np

D_MODEL = 1024
BATCH = 8
SEQ = 4096
DEPTH = 1

S5_GROUP = 16
S5_WIDTH = D_MODEL // 2
S5_GROUPS = S5_WIDTH // S5_GROUP
S5_STATE = 64
S5_DT_MIN = 0.001
S5_DT_MAX = 0.1
GLA_HEADS = 4
GLA_VAL_WIDTH = D_MODEL // 2
GLA_DV = GLA_VAL_WIDTH // GLA_HEADS
GLA_DK = GLA_DV // 2
GLA_KEY_WIDTH = GLA_HEADS * GLA_DK
GLA_GATE_RANK = 16
GLA_TAU = 16.0
GLA_CHUNK = 64
D_FF = ((8 * D_MODEL // 3 + 255) // 256) * 256
EPS = 1e-6
IN_SIZES = (S5_WIDTH, GLA_KEY_WIDTH, GLA_KEY_WIDTH, GLA_VAL_WIDTH, GLA_VAL_WIDTH,
            GLA_GATE_RANK, D_MODEL, D_MODEL)
IN_COLS = sum(IN_SIZES)

kernel_name = "hybrid_s5_gla_macaron_block"


def rms_norm(x, g):
    xf = x.astype(jnp.float32)
    y = xf * lax.rsqrt(jnp.mean(xf * xf, axis=-1, keepdims=True) + EPS)
    return (y * g.astype(jnp.float32)).astype(x.dtype)


def swiglu(x, w1, w3, w2):
    return (jax.nn.silu(x @ w1) * (x @ w3)) @ w2


def _ssm_combine(e1, e2):
    ar1, ai1, br1, bi1 = e1
    ar2, ai2, br2, bi2 = e2
    return (ar1 * ar2 - ai1 * ai2,
            ar1 * ai2 + ai1 * ar2,
            ar2 * br1 - ai2 * bi1 + br2,
            ar2 * bi1 + ai2 * br1 + bi2)


def s5_mixer(u, lam_re, lam_im, log_dt, b_re, b_im, c_re, c_im, d_skip, w_glu, b_glu):
    bsz, seq, _ = u.shape
    f32 = jnp.float32
    lam_re = lam_re.astype(f32)
    lam_im = lam_im.astype(f32)
    dt = jnp.exp(log_dt.astype(f32))[:, None]
    mag = jnp.exp(lam_re * dt)
    ar = mag * jnp.cos(lam_im * dt)
    ai = mag * jnp.sin(lam_im * dt)
    den = lam_re * lam_re + lam_im * lam_im
    nr = ar - 1.0
    fr = (nr * lam_re + ai * lam_im) / den
    fi = (ai * lam_re - nr * lam_im) / den
    b_re = b_re.astype(f32)
    b_im = b_im.astype(f32)
    bbar_re = fr[:, :, None] * b_re - fi[:, :, None] * b_im
    bbar_im = fr[:, :, None] * b_im + fi[:, :, None] * b_re
    ug = u.astype(f32).reshape(bsz, seq, S5_GROUPS, S5_GROUP)
    bu_re = jnp.einsum('blgh,gph->lbgp', ug, bbar_re)
    bu_im = jnp.einsum('blgh,gph->lbgp', ug, bbar_im)
    a_re = jnp.broadcast_to(ar, (seq, 1, S5_GROUPS, S5_STATE))
    a_im = jnp.broadcast_to(ai, (seq, 1, S5_GROUPS, S5_STATE))
    _, _, xr, xi = lax.associative_scan(_ssm_combine, (a_re, a_im, bu_re, bu_im), axis=0)
    y = (jnp.einsum('ghp,lbgp->blgh', c_re.astype(f32), xr)
         - jnp.einsum('ghp,lbgp->blgh', c_im.astype(f32), xi)
         + d_skip.astype(f32) * ug)
    y = y.reshape(bsz, seq, S5_WIDTH).astype(u.dtype)
    z = jax.nn.gelu(y)
    return z * jax.nn.sigmoid(z @ w_glu + b_glu)


def gla_mixer(q, k, v, r, a_low, w_a_up, b_a_up, g_norm):
    bsz, seq, _ = q.shape
    n_chunks = seq // GLA_CHUNK
    f32 = jnp.float32
    shp_k = (bsz, n_chunks, GLA_CHUNK, GLA_HEADS, GLA_DK)
    shp_v = (bsz, n_chunks, GLA_CHUNK, GLA_HEADS, GLA_DV)
    qc = q.astype(f32).reshape(shp_k) * (GLA_DK ** -0.5)
    kc = k.astype(f32).reshape(shp_k)
    vc = v.astype(f32).reshape(shp_v)
    log_a = jax.nn.log_sigmoid((a_low @ w_a_up + b_a_up).astype(f32)) / GLA_TAU
    bcum = jnp.cumsum(log_a.reshape(shp_k), axis=2)
    b_last = bcum[:, :, -1]
    q_t = qc * jnp.exp(bcum)
    k_t = kc * jnp.exp(-bcum)
    scores = jnp.einsum('bnthd,bnshd->bnhts', q_t, k_t)
    causal = jnp.tril(jnp.ones((GLA_CHUNK, GLA_CHUNK), dtype=bool))
    scores = jnp.where(causal, scores, 0.0)
    o_intra = jnp.einsum('bnhts,bnshv->bnthv', scores, vc)
    k_end = kc * jnp.exp(b_last[:, :, None] - bcum)
    d_state = jnp.einsum('bnshd,bnshv->bnhdv', k_end, vc)
    decay = jnp.exp(b_last)

    def step(state, inp):
        dec, ds = inp
        return dec[..., None] * state + ds, state

    s0 = jnp.zeros((bsz, GLA_HEADS, GLA_DK, GLA_DV), f32)
    _, s_prev = lax.scan(step, s0, (jnp.moveaxis(decay, 1, 0), jnp.moveaxis(d_state, 1, 0)))
    s_prev = jnp.moveaxis(s_prev, 0, 1)
    o_inter = jnp.einsum('bnthd,bnhdv->bnthv', q_t, s_prev)
    o = (o_intra + o_inter).reshape(bsz, seq, GLA_HEADS, GLA_DV)
    o = o * lax.rsqrt(jnp.mean(o * o, axis=-1, keepdims=True) + EPS)
    o = o.reshape(bsz, seq, GLA_VAL_WIDTH) * g_norm.astype(f32)
    return (o * jax.nn.silu(r.astype(f32))).astype(q.dtype)


def _fwd_setup_inputs(seed: int = 0) -> dict:
    key = jax.random.key(seed)
    ks = jax.random.split(key, 32)
    f32 = jnp.float32
    L, G, P, H = DEPTH, S5_GROUPS, S5_STATE, S5_GROUP

    def nrm(k, shape, scale):
        return jax.random.normal(k, shape, f32) * scale

    def gain(k, shape):
        return 1.0 + 0.01 * jax.random.normal(k, shape, f32)

    n_idx = jnp.arange(P, dtype=f32)
    return {
        "x": nrm(ks[0], (BATCH, SEQ, D_MODEL), 1.0),
        "ffn1_norm": gain(ks[1], (L, D_MODEL)),
        "ffn1_w1": nrm(ks[2], (L, D_MODEL, D_FF), D_MODEL ** -0.5),
        "ffn1_w3": nrm(ks[3], (L, D_MODEL, D_FF), D_MODEL ** -0.5),
        "ffn1_w2": nrm(ks[4], (L, D_FF, D_MODEL), D_FF ** -0.5),
        "mix_norm": gain(ks[5], (L, D_MODEL)),
        "w_in": nrm(ks[6], (L, D_MODEL, IN_COLS), D_MODEL ** -0.5),
        "s5_lambda_re": -0.5 + 0.01 * jax.random.normal(ks[7], (L, G, P), f32),
        "s5_lambda_im": math.pi * n_idx + 0.01 * jax.random.normal(ks[8], (L, G, P), f32),
        "s5_log_dt": jax.random.uniform(ks[9], (L, G), f32, math.log(S5_DT_MIN), math.log(S5_DT_MAX)),
        "s5_b_re": nrm(ks[10], (L, G, P, H), (2.0 * H) ** -0.5),
        "s5_b_im": nrm(ks[11], (L, G, P, H), (2.0 * H) ** -0.5),
        "s5_c_re": nrm(ks[12], (L, G, H, P), (2.0 * P) ** -0.5),
        "s5_c_im": nrm(ks[13], (L, G, H, P), (2.0 * P) ** -0.5),
        "s5_d": nrm(ks[14], (L, G, H), 1.0),
        "s5_glu_w": nrm(ks[15], (L, S5_WIDTH, S5_WIDTH), S5_WIDTH ** -0.5),
        "s5_glu_b": nrm(ks[16], (L, S5_WIDTH), 0.01),
        "gla_a_up_w": nrm(ks[17], (L, GLA_GATE_RANK, GLA_KEY_WIDTH), GLA_GATE_RANK ** -0.5),
        "gla_a_up_b": nrm(ks[18], (L, GLA_KEY_WIDTH), 0.1),
        "gla_out_norm": gain(ks[19], (L, GLA_VAL_WIDTH)),
        "proj_s5": nrm(ks[20], (L, S5_WIDTH, D_MODEL), S5_WIDTH ** -0.5),
        "proj_gla": nrm(ks[21], (L, GLA_VAL_WIDTH, D_MODEL), GLA_VAL_WIDTH ** -0.5),
        "w_out": nrm(ks[22], (L, D_MODEL, D_MODEL), D_MODEL ** -0.5),
        "ffn2_norm": gain(ks[23], (L, D_MODEL)),
        "ffn2_w1": nrm(ks[24], (L, D_MODEL, D_FF), D_MODEL ** -0.5),
        "ffn2_w3": nrm(ks[25], (L, D_MODEL, D_FF), D_MODEL ** -0.5),
        "ffn2_w2": nrm(ks[26], (L, D_FF, D_MODEL), D_FF ** -0.5),
        "final_norm": gain(ks[27], (D_MODEL,)),
    }


def _fwd_reference(x, ffn1_norm, ffn1_w1, ffn1_w3, ffn1_w2, mix_norm, w_in,
              s5_lambda_re, s5_lambda_im, s5_log_dt, s5_b_re, s5_b_im, s5_c_re, s5_c_im,
              s5_d, s5_glu_w, s5_glu_b, gla_a_up_w, gla_a_up_b, gla_out_norm,
              proj_s5, proj_gla, w_out, ffn2_norm, ffn2_w1, ffn2_w3, ffn2_w2, final_norm):
    split_idx = list(np.cumsum(IN_SIZES)[:-1])
    h = x
    for l in range(DEPTH):
        h = h + 0.5 * swiglu(rms_norm(h, ffn1_norm[l]), ffn1_w1[l], ffn1_w3[l], ffn1_w2[l])
        u = rms_norm(h, mix_norm[l])
        s5_in, q, k, v, r, a_low, g_s5, g_gla = jnp.split(u @ w_in[l], split_idx, axis=-1)
        y_s5 = s5_mixer(s5_in, s5_lambda_re[l], s5_lambda_im[l], s5_log_dt[l],
                        s5_b_re[l], s5_b_im[l], s5_c_re[l], s5_c_im[l], s5_d[l],
                        s5_glu_w[l], s5_glu_b[l])
        y_gla = gla_mixer(q, k, v, r, a_low, gla_a_up_w[l], gla_a_up_b[l], gla_out_norm[l])
        merged = (jax.nn.sigmoid(g_s5) * (y_s5 @ proj_s5[l])
                  + jax.nn.sigmoid(g_gla) * (y_gla @ proj_gla[l]))
        h = h + merged @ w_out[l]
        h = h + 0.5 * swiglu(rms_norm(h, ffn2_norm[l]), ffn2_w1[l], ffn2_w3[l], ffn2_w2[l])
    return rms_norm(h, final_norm).astype(x.dtype)


import jax as _jax
import jax.numpy as _jnp

TWIN_FORMAT = 'train_step'
FWD_PARAMS = ['x', 'ffn1_norm', 'ffn1_w1', 'ffn1_w3', 'ffn1_w2', 'mix_norm', 'w_in', 's5_lambda_re', 's5_lambda_im', 's5_log_dt', 's5_b_re', 's5_b_im', 's5_c_re', 's5_c_im', 's5_d', 's5_glu_w', 's5_glu_b', 'gla_a_up_w', 'gla_a_up_b', 'gla_out_norm', 'proj_s5', 'proj_gla', 'w_out', 'ffn2_norm', 'ffn2_w1', 'ffn2_w3', 'ffn2_w2', 'final_norm']
TWIN_WEIGHTS = ['ffn1_norm', 'ffn1_w1', 'ffn1_w3', 'ffn1_w2', 'mix_norm', 'w_in', 's5_lambda_re', 's5_lambda_im', 's5_log_dt', 's5_b_re', 's5_b_im', 's5_c_re', 's5_c_im', 's5_d', 's5_glu_w', 's5_glu_b', 'gla_a_up_w', 'gla_a_up_b', 'gla_out_norm', 'proj_s5', 'proj_gla', 'w_out', 'ffn2_norm', 'ffn2_w1', 'ffn2_w3', 'ffn2_w2', 'final_norm']
TWIN_DIFF_INPUT = 'x'
TWIN_INPUTS = ['x', 'ffn1_norm', 'ffn1_w1', 'ffn1_w3', 'ffn1_w2', 'mix_norm', 'w_in', 's5_lambda_re', 's5_lambda_im', 's5_log_dt', 's5_b_re', 's5_b_im', 's5_c_re', 's5_c_im', 's5_d', 's5_glu_w', 's5_glu_b', 'gla_a_up_w', 'gla_a_up_b', 'gla_out_norm', 'proj_s5', 'proj_gla', 'w_out', 'ffn2_norm', 'ffn2_w1', 'ffn2_w3', 'ffn2_w2', 'final_norm', 'loss_target', 'm_ffn1_norm', 'm_ffn1_w1', 'm_ffn1_w3', 'm_ffn1_w2', 'm_mix_norm', 'm_w_in', 'm_s5_lambda_re', 'm_s5_lambda_im', 'm_s5_log_dt', 'm_s5_b_re', 'm_s5_b_im', 'm_s5_c_re', 'm_s5_c_im', 'm_s5_d', 'm_s5_glu_w', 'm_s5_glu_b', 'm_gla_a_up_w', 'm_gla_a_up_b', 'm_gla_out_norm', 'm_proj_s5', 'm_proj_gla', 'm_w_out', 'm_ffn2_norm', 'm_ffn2_w1', 'm_ffn2_w3', 'm_ffn2_w2', 'm_final_norm', 'v_ffn1_norm', 'v_ffn1_w1', 'v_ffn1_w3', 'v_ffn1_w2', 'v_mix_norm', 'v_w_in', 'v_s5_lambda_re', 'v_s5_lambda_im', 'v_s5_log_dt', 'v_s5_b_re', 'v_s5_b_im', 'v_s5_c_re', 'v_s5_c_im', 'v_s5_d', 'v_s5_glu_w', 'v_s5_glu_b', 'v_gla_a_up_w', 'v_gla_a_up_b', 'v_gla_out_norm', 'v_proj_s5', 'v_proj_gla', 'v_w_out', 'v_ffn2_norm', 'v_ffn2_w1', 'v_ffn2_w3', 'v_ffn2_w2', 'v_final_norm']
TWIN_OUTPUTS = ['loss', 'grad_x', 'grad_ffn1_norm', 'grad_ffn1_w1', 'grad_ffn1_w3', 'grad_ffn1_w2', 'grad_mix_norm', 'grad_w_in', 'grad_s5_lambda_re', 'grad_s5_lambda_im', 'grad_s5_log_dt', 'grad_s5_b_re', 'grad_s5_b_im', 'grad_s5_c_re', 'grad_s5_c_im', 'grad_s5_d', 'grad_s5_glu_w', 'grad_s5_glu_b', 'grad_gla_a_up_w', 'grad_gla_a_up_b', 'grad_gla_out_norm', 'grad_proj_s5', 'grad_proj_gla', 'grad_w_out', 'grad_ffn2_norm', 'grad_ffn2_w1', 'grad_ffn2_w3', 'grad_ffn2_w2', 'grad_final_norm', 'delta_ffn1_norm', 'delta_ffn1_w1', 'delta_ffn1_w3', 'delta_ffn1_w2', 'delta_mix_norm', 'delta_w_in', 'delta_s5_lambda_re', 'delta_s5_lambda_im', 'delta_s5_log_dt', 'delta_s5_b_re', 'delta_s5_b_im', 'delta_s5_c_re', 'delta_s5_c_im', 'delta_s5_d', 'delta_s5_glu_w', 'delta_s5_glu_b', 'delta_gla_a_up_w', 'delta_gla_a_up_b', 'delta_gla_out_norm', 'delta_proj_s5', 'delta_proj_gla', 'delta_w_out', 'delta_ffn2_norm', 'delta_ffn2_w1', 'delta_ffn2_w3', 'delta_ffn2_w2', 'delta_final_norm', 'new_m_ffn1_norm', 'new_m_ffn1_w1', 'new_m_ffn1_w3', 'new_m_ffn1_w2', 'new_m_mix_norm', 'new_m_w_in', 'new_m_s5_lambda_re', 'new_m_s5_lambda_im', 'new_m_s5_log_dt', 'new_m_s5_b_re', 'new_m_s5_b_im', 'new_m_s5_c_re', 'new_m_s5_c_im', 'new_m_s5_d', 'new_m_s5_glu_w', 'new_m_s5_glu_b', 'new_m_gla_a_up_w', 'new_m_gla_a_up_b', 'new_m_gla_out_norm', 'new_m_proj_s5', 'new_m_proj_gla', 'new_m_w_out', 'new_m_ffn2_norm', 'new_m_ffn2_w1', 'new_m_ffn2_w3', 'new_m_ffn2_w2', 'new_m_final_norm', 'new_v_ffn1_norm', 'new_v_ffn1_w1', 'new_v_ffn1_w3', 'new_v_ffn1_w2', 'new_v_mix_norm', 'new_v_w_in', 'new_v_s5_lambda_re', 'new_v_s5_lambda_im', 'new_v_s5_log_dt', 'new_v_s5_b_re', 'new_v_s5_b_im', 'new_v_s5_c_re', 'new_v_s5_c_im', 'new_v_s5_d', 'new_v_s5_glu_w', 'new_v_s5_glu_b', 'new_v_gla_a_up_w', 'new_v_gla_a_up_b', 'new_v_gla_out_norm', 'new_v_proj_s5', 'new_v_proj_gla', 'new_v_w_out', 'new_v_ffn2_norm', 'new_v_ffn2_w1', 'new_v_ffn2_w3', 'new_v_ffn2_w2', 'new_v_final_norm']
TWIN_LEAF_KINDS = {'loss': 'loss', 'grad_x': 'grad_x', 'grad_ffn1_norm': 'grad_w', 'grad_ffn1_w1': 'grad_w', 'grad_ffn1_w3': 'grad_w', 'grad_ffn1_w2': 'grad_w', 'grad_mix_norm': 'grad_w', 'grad_w_in': 'grad_w', 'grad_s5_lambda_re': 'grad_w', 'grad_s5_lambda_im': 'grad_w', 'grad_s5_log_dt': 'grad_w', 'grad_s5_b_re': 'grad_w', 'grad_s5_b_im': 'grad_w', 'grad_s5_c_re': 'grad_w', 'grad_s5_c_im': 'grad_w', 'grad_s5_d': 'grad_w', 'grad_s5_glu_w': 'grad_w', 'grad_s5_glu_b': 'grad_w', 'grad_gla_a_up_w': 'grad_w', 'grad_gla_a_up_b': 'grad_w', 'grad_gla_out_norm': 'grad_w', 'grad_proj_s5': 'grad_w', 'grad_proj_gla': 'grad_w', 'grad_w_out': 'grad_w', 'grad_ffn2_norm': 'grad_w', 'grad_ffn2_w1': 'grad_w', 'grad_ffn2_w3': 'grad_w', 'grad_ffn2_w2': 'grad_w', 'grad_final_norm': 'grad_w', 'delta_ffn1_norm': 'delta_w', 'delta_ffn1_w1': 'delta_w', 'delta_ffn1_w3': 'delta_w', 'delta_ffn1_w2': 'delta_w', 'delta_mix_norm': 'delta_w', 'delta_w_in': 'delta_w', 'delta_s5_lambda_re': 'delta_w', 'delta_s5_lambda_im': 'delta_w', 'delta_s5_log_dt': 'delta_w', 'delta_s5_b_re': 'delta_w', 'delta_s5_b_im': 'delta_w', 'delta_s5_c_re': 'delta_w', 'delta_s5_c_im': 'delta_w', 'delta_s5_d': 'delta_w', 'delta_s5_glu_w': 'delta_w', 'delta_s5_glu_b': 'delta_w', 'delta_gla_a_up_w': 'delta_w', 'delta_gla_a_up_b': 'delta_w', 'delta_gla_out_norm': 'delta_w', 'delta_proj_s5': 'delta_w', 'delta_proj_gla': 'delta_w', 'delta_w_out': 'delta_w', 'delta_ffn2_norm': 'delta_w', 'delta_ffn2_w1': 'delta_w', 'delta_ffn2_w3': 'delta_w', 'delta_ffn2_w2': 'delta_w', 'delta_final_norm': 'delta_w', 'new_m_ffn1_norm': 'new_m', 'new_m_ffn1_w1': 'new_m', 'new_m_ffn1_w3': 'new_m', 'new_m_ffn1_w2': 'new_m', 'new_m_mix_norm': 'new_m', 'new_m_w_in': 'new_m', 'new_m_s5_lambda_re': 'new_m', 'new_m_s5_lambda_im': 'new_m', 'new_m_s5_log_dt': 'new_m', 'new_m_s5_b_re': 'new_m', 'new_m_s5_b_im': 'new_m', 'new_m_s5_c_re': 'new_m', 'new_m_s5_c_im': 'new_m', 'new_m_s5_d': 'new_m', 'new_m_s5_glu_w': 'new_m', 'new_m_s5_glu_b': 'new_m', 'new_m_gla_a_up_w': 'new_m', 'new_m_gla_a_up_b': 'new_m', 'new_m_gla_out_norm': 'new_m', 'new_m_proj_s5': 'new_m', 'new_m_proj_gla': 'new_m', 'new_m_w_out': 'new_m', 'new_m_ffn2_norm': 'new_m', 'new_m_ffn2_w1': 'new_m', 'new_m_ffn2_w3': 'new_m', 'new_m_ffn2_w2': 'new_m', 'new_m_final_norm': 'new_m', 'new_v_ffn1_norm': 'new_v', 'new_v_ffn1_w1': 'new_v', 'new_v_ffn1_w3': 'new_v', 'new_v_ffn1_w2': 'new_v', 'new_v_mix_norm': 'new_v', 'new_v_w_in': 'new_v', 'new_v_s5_lambda_re': 'new_v', 'new_v_s5_lambda_im': 'new_v', 'new_v_s5_log_dt': 'new_v', 'new_v_s5_b_re': 'new_v', 'new_v_s5_b_im': 'new_v', 'new_v_s5_c_re': 'new_v', 'new_v_s5_c_im': 'new_v', 'new_v_s5_d': 'new_v', 'new_v_s5_glu_w': 'new_v', 'new_v_s5_glu_b': 'new_v', 'new_v_gla_a_up_w': 'new_v', 'new_v_gla_a_up_b': 'new_v', 'new_v_gla_out_norm': 'new_v', 'new_v_proj_s5': 'new_v', 'new_v_proj_gla': 'new_v', 'new_v_w_out': 'new_v', 'new_v_ffn2_norm': 'new_v', 'new_v_ffn2_w1': 'new_v', 'new_v_ffn2_w3': 'new_v', 'new_v_ffn2_w2': 'new_v', 'new_v_final_norm': 'new_v'}


def _forward(args):
    return _fwd_reference(*[args[k] for k in FWD_PARAMS])


def _output_shape():
    out = _jax.eval_shape(lambda: _forward(_fwd_setup_inputs(0)))
    return out.shape, out.dtype

N_MICROBATCH = 1
ADAM_LR = 0.001
ADAM_B1 = 0.9
ADAM_B2 = 0.999
ADAM_EPS = 1e-08
ADAM_WD = 0.01
ADAM_STEP = 10
PER_EXAMPLE_BATCH_AXIS = {'x': 0, 'loss_target': 0}
SHARED_INPUTS = []
_WEIGHT_DTYPES = {'ffn1_norm': _jnp.float32, 'ffn1_w1': _jnp.float32, 'ffn1_w3': _jnp.float32, 'ffn1_w2': _jnp.float32, 'mix_norm': _jnp.float32, 'w_in': _jnp.float32, 's5_lambda_re': _jnp.float32, 's5_lambda_im': _jnp.float32, 's5_log_dt': _jnp.float32, 's5_b_re': _jnp.float32, 's5_b_im': _jnp.float32, 's5_c_re': _jnp.float32, 's5_c_im': _jnp.float32, 's5_d': _jnp.float32, 's5_glu_w': _jnp.float32, 's5_glu_b': _jnp.float32, 'gla_a_up_w': _jnp.float32, 'gla_a_up_b': _jnp.float32, 'gla_out_norm': _jnp.float32, 'proj_s5': _jnp.float32, 'proj_gla': _jnp.float32, 'w_out': _jnp.float32, 'ffn2_norm': _jnp.float32, 'ffn2_w1': _jnp.float32, 'ffn2_w3': _jnp.float32, 'ffn2_w2': _jnp.float32, 'final_norm': _jnp.float32}
MOMENT_SCALE = {'ffn1_norm': 9.232127e-02, 'ffn1_w1': 3.704842e-02, 'ffn1_w3': 3.581759e-02, 'ffn1_w2': 5.937287e-02, 'mix_norm': 1.167838e-01, 'w_in': 5.845855e-02, 's5_lambda_re': 2.641174e-03, 's5_lambda_im': 2.290390e-03, 's5_log_dt': 1.272509e+00, 's5_b_re': 1.546994e-03, 's5_b_im': 1.546664e-03, 's5_c_re': 2.998197e-03, 's5_c_im': 3.119140e-03, 's5_d': 4.759337e-02, 's5_glu_w': 1.324131e-02, 's5_glu_b': 2.049186e-02, 'gla_a_up_w': 1.218798e-02, 'gla_a_up_b': 4.861657e-02, 'gla_out_norm': 7.622912e-02, 'proj_s5': 3.147598e-02, 'proj_gla': 5.327613e-02, 'w_out': 6.141284e-02, 'ffn2_norm': 6.639623e-02, 'ffn2_w1': 2.865660e-02, 'ffn2_w3': 2.774896e-02, 'ffn2_w2': 4.594341e-02, 'final_norm': 3.194043e+01}


def _to_microbatches(a, axis):
    t = _jnp.moveaxis(a, axis, 0)
    t = t.reshape((N_MICROBATCH, t.shape[0] // N_MICROBATCH) + t.shape[1:])
    return _jnp.moveaxis(t, 1, axis + 1)


def setup_inputs(seed: int = 0) -> dict:
    inp = _fwd_setup_inputs(seed)
    key = _jax.random.fold_in(_jax.random.key(seed), 7919)
    shape, _ = _output_shape()
    out = dict(inp)
    out["loss_target"] = _jax.random.normal(_jax.random.fold_in(key, 0), shape, _jnp.float32)
    for i, name in enumerate(TWIN_WEIGHTS):
        w = inp[name].astype(_jnp.float32)
        if MOMENT_SCALE is None:
            s = _jnp.sqrt(_jnp.mean(_jnp.square(w)) + 1e-30)
        else:
            s = MOMENT_SCALE[name]
        km, kv = _jax.random.split(_jax.random.fold_in(key, i + 1))
        out[name] = w
        out["m_" + name] = s * _jax.random.normal(km, w.shape, _jnp.float32)
        out["v_" + name] = (s * s) * _jax.random.uniform(kv, w.shape, _jnp.float32, 0.5, 1.5)
    if N_MICROBATCH > 1:
        for name, axis in PER_EXAMPLE_BATCH_AXIS.items():
            out[name] = _to_microbatches(out[name], axis)
    return {'x': out['x'], 'ffn1_norm': out['ffn1_norm'], 'ffn1_w1': out['ffn1_w1'], 'ffn1_w3': out['ffn1_w3'], 'ffn1_w2': out['ffn1_w2'], 'mix_norm': out['mix_norm'], 'w_in': out['w_in'], 's5_lambda_re': out['s5_lambda_re'], 's5_lambda_im': out['s5_lambda_im'], 's5_log_dt': out['s5_log_dt'], 's5_b_re': out['s5_b_re'], 's5_b_im': out['s5_b_im'], 's5_c_re': out['s5_c_re'], 's5_c_im': out['s5_c_im'], 's5_d': out['s5_d'], 's5_glu_w': out['s5_glu_w'], 's5_glu_b': out['s5_glu_b'], 'gla_a_up_w': out['gla_a_up_w'], 'gla_a_up_b': out['gla_a_up_b'], 'gla_out_norm': out['gla_out_norm'], 'proj_s5': out['proj_s5'], 'proj_gla': out['proj_gla'], 'w_out': out['w_out'], 'ffn2_norm': out['ffn2_norm'], 'ffn2_w1': out['ffn2_w1'], 'ffn2_w3': out['ffn2_w3'], 'ffn2_w2': out['ffn2_w2'], 'final_norm': out['final_norm'], 'loss_target': out['loss_target'], 'm_ffn1_norm': out['m_ffn1_norm'], 'm_ffn1_w1': out['m_ffn1_w1'], 'm_ffn1_w3': out['m_ffn1_w3'], 'm_ffn1_w2': out['m_ffn1_w2'], 'm_mix_norm': out['m_mix_norm'], 'm_w_in': out['m_w_in'], 'm_s5_lambda_re': out['m_s5_lambda_re'], 'm_s5_lambda_im': out['m_s5_lambda_im'], 'm_s5_log_dt': out['m_s5_log_dt'], 'm_s5_b_re': out['m_s5_b_re'], 'm_s5_b_im': out['m_s5_b_im'], 'm_s5_c_re': out['m_s5_c_re'], 'm_s5_c_im': out['m_s5_c_im'], 'm_s5_d': out['m_s5_d'], 'm_s5_glu_w': out['m_s5_glu_w'], 'm_s5_glu_b': out['m_s5_glu_b'], 'm_gla_a_up_w': out['m_gla_a_up_w'], 'm_gla_a_up_b': out['m_gla_a_up_b'], 'm_gla_out_norm': out['m_gla_out_norm'], 'm_proj_s5': out['m_proj_s5'], 'm_proj_gla': out['m_proj_gla'], 'm_w_out': out['m_w_out'], 'm_ffn2_norm': out['m_ffn2_norm'], 'm_ffn2_w1': out['m_ffn2_w1'], 'm_ffn2_w3': out['m_ffn2_w3'], 'm_ffn2_w2': out['m_ffn2_w2'], 'm_final_norm': out['m_final_norm'], 'v_ffn1_norm': out['v_ffn1_norm'], 'v_ffn1_w1': out['v_ffn1_w1'], 'v_ffn1_w3': out['v_ffn1_w3'], 'v_ffn1_w2': out['v_ffn1_w2'], 'v_mix_norm': out['v_mix_norm'], 'v_w_in': out['v_w_in'], 'v_s5_lambda_re': out['v_s5_lambda_re'], 'v_s5_lambda_im': out['v_s5_lambda_im'], 'v_s5_log_dt': out['v_s5_log_dt'], 'v_s5_b_re': out['v_s5_b_re'], 'v_s5_b_im': out['v_s5_b_im'], 'v_s5_c_re': out['v_s5_c_re'], 'v_s5_c_im': out['v_s5_c_im'], 'v_s5_d': out['v_s5_d'], 'v_s5_glu_w': out['v_s5_glu_w'], 'v_s5_glu_b': out['v_s5_glu_b'], 'v_gla_a_up_w': out['v_gla_a_up_w'], 'v_gla_a_up_b': out['v_gla_a_up_b'], 'v_gla_out_norm': out['v_gla_out_norm'], 'v_proj_s5': out['v_proj_s5'], 'v_proj_gla': out['v_proj_gla'], 'v_w_out': out['v_w_out'], 'v_ffn2_norm': out['v_ffn2_norm'], 'v_ffn2_w1': out['v_ffn2_w1'], 'v_ffn2_w3': out['v_ffn2_w3'], 'v_ffn2_w2': out['v_ffn2_w2'], 'v_final_norm': out['v_final_norm']}


def _loss(weights, diff, rest, loss_target):
    with _jax.named_scope("forward"):
        args = {**rest, TWIN_DIFF_INPUT: diff, **{k: w.astype(_WEIGHT_DTYPES[k]) for k, w in weights.items()}}
        y = _forward(args)
    with _jax.named_scope("loss_head"):
        err = _jnp.square(y.astype(_jnp.float32) - loss_target)
        return 0.5 * _jnp.sum(_jnp.mean(err, axis=-1)) if err.ndim else 0.5 * err


def _adamw(w, g, m, v):
    m = ADAM_B1 * m + (1.0 - ADAM_B1) * g
    v = ADAM_B2 * v + (1.0 - ADAM_B2) * _jnp.square(g)
    m_hat = m / (1.0 - ADAM_B1 ** ADAM_STEP)
    v_hat = v / (1.0 - ADAM_B2 ** ADAM_STEP)
    delta = -ADAM_LR * (m_hat / (_jnp.sqrt(v_hat) + ADAM_EPS) + ADAM_WD * w)
    return delta, m, v


def reference(x, ffn1_norm, ffn1_w1, ffn1_w3, ffn1_w2, mix_norm, w_in, s5_lambda_re, s5_lambda_im, s5_log_dt, s5_b_re, s5_b_im, s5_c_re, s5_c_im, s5_d, s5_glu_w, s5_glu_b, gla_a_up_w, gla_a_up_b, gla_out_norm, proj_s5, proj_gla, w_out, ffn2_norm, ffn2_w1, ffn2_w3, ffn2_w2, final_norm, loss_target, m_ffn1_norm, m_ffn1_w1, m_ffn1_w3, m_ffn1_w2, m_mix_norm, m_w_in, m_s5_lambda_re, m_s5_lambda_im, m_s5_log_dt, m_s5_b_re, m_s5_b_im, m_s5_c_re, m_s5_c_im, m_s5_d, m_s5_glu_w, m_s5_glu_b, m_gla_a_up_w, m_gla_a_up_b, m_gla_out_norm, m_proj_s5, m_proj_gla, m_w_out, m_ffn2_norm, m_ffn2_w1, m_ffn2_w3, m_ffn2_w2, m_final_norm, v_ffn1_norm, v_ffn1_w1, v_ffn1_w3, v_ffn1_w2, v_mix_norm, v_w_in, v_s5_lambda_re, v_s5_lambda_im, v_s5_log_dt, v_s5_b_re, v_s5_b_im, v_s5_c_re, v_s5_c_im, v_s5_d, v_s5_glu_w, v_s5_glu_b, v_gla_a_up_w, v_gla_a_up_b, v_gla_out_norm, v_proj_s5, v_proj_gla, v_w_out, v_ffn2_norm, v_ffn2_w1, v_ffn2_w3, v_ffn2_w2, v_final_norm):
    given = dict(x=x, ffn1_norm=ffn1_norm, ffn1_w1=ffn1_w1, ffn1_w3=ffn1_w3, ffn1_w2=ffn1_w2, mix_norm=mix_norm, w_in=w_in, s5_lambda_re=s5_lambda_re, s5_lambda_im=s5_lambda_im, s5_log_dt=s5_log_dt, s5_b_re=s5_b_re, s5_b_im=s5_b_im, s5_c_re=s5_c_re, s5_c_im=s5_c_im, s5_d=s5_d, s5_glu_w=s5_glu_w, s5_glu_b=s5_glu_b, gla_a_up_w=gla_a_up_w, gla_a_up_b=gla_a_up_b, gla_out_norm=gla_out_norm, proj_s5=proj_s5, proj_gla=proj_gla, w_out=w_out, ffn2_norm=ffn2_norm, ffn2_w1=ffn2_w1, ffn2_w3=ffn2_w3, ffn2_w2=ffn2_w2, final_norm=final_norm, loss_target=loss_target, m_ffn1_norm=m_ffn1_norm, m_ffn1_w1=m_ffn1_w1, m_ffn1_w3=m_ffn1_w3, m_ffn1_w2=m_ffn1_w2, m_mix_norm=m_mix_norm, m_w_in=m_w_in, m_s5_lambda_re=m_s5_lambda_re, m_s5_lambda_im=m_s5_lambda_im, m_s5_log_dt=m_s5_log_dt, m_s5_b_re=m_s5_b_re, m_s5_b_im=m_s5_b_im, m_s5_c_re=m_s5_c_re, m_s5_c_im=m_s5_c_im, m_s5_d=m_s5_d, m_s5_glu_w=m_s5_glu_w, m_s5_glu_b=m_s5_glu_b, m_gla_a_up_w=m_gla_a_up_w, m_gla_a_up_b=m_gla_a_up_b, m_gla_out_norm=m_gla_out_norm, m_proj_s5=m_proj_s5, m_proj_gla=m_proj_gla, m_w_out=m_w_out, m_ffn2_norm=m_ffn2_norm, m_ffn2_w1=m_ffn2_w1, m_ffn2_w3=m_ffn2_w3, m_ffn2_w2=m_ffn2_w2, m_final_norm=m_final_norm, v_ffn1_norm=v_ffn1_norm, v_ffn1_w1=v_ffn1_w1, v_ffn1_w3=v_ffn1_w3, v_ffn1_w2=v_ffn1_w2, v_mix_norm=v_mix_norm, v_w_in=v_w_in, v_s5_lambda_re=v_s5_lambda_re, v_s5_lambda_im=v_s5_lambda_im, v_s5_log_dt=v_s5_log_dt, v_s5_b_re=v_s5_b_re, v_s5_b_im=v_s5_b_im, v_s5_c_re=v_s5_c_re, v_s5_c_im=v_s5_c_im, v_s5_d=v_s5_d, v_s5_glu_w=v_s5_glu_w, v_s5_glu_b=v_s5_glu_b, v_gla_a_up_w=v_gla_a_up_w, v_gla_a_up_b=v_gla_a_up_b, v_gla_out_norm=v_gla_out_norm, v_proj_s5=v_proj_s5, v_proj_gla=v_proj_gla, v_w_out=v_w_out, v_ffn2_norm=v_ffn2_norm, v_ffn2_w1=v_ffn2_w1, v_ffn2_w3=v_ffn2_w3, v_ffn2_w2=v_ffn2_w2, v_final_norm=v_final_norm)
    weights = {n: given[n] for n in TWIN_WEIGHTS}
    shared = {n: given[n] for n in SHARED_INPUTS}
    per_example = {n: given[n] for n in ['x']}
    grad_fn = _jax.value_and_grad(_loss, argnums=(0, 1))

    def one_microbatch(ex, loss_target):
        ex = dict(ex)
        diff = ex.pop(TWIN_DIFF_INPUT)
        return grad_fn(weights, diff, {**shared, **ex}, loss_target)

    if N_MICROBATCH == 1:
        loss, (grad_w, grad_x) = one_microbatch(per_example, given["loss_target"])
    else:
        def body(carry, xs):
            loss_sum, grad_sum = carry
            l_k, (gw_k, gx_k) = one_microbatch(xs[0], xs[1])
            with _jax.named_scope("update"):
                return (loss_sum + l_k, _jax.tree.map(_jnp.add, grad_sum, gw_k)), gx_k

        init = (_jnp.zeros((), _jnp.float32), _jax.tree.map(_jnp.zeros_like, weights))
        (loss, grad_w), grad_x = _jax.lax.scan(body, init, (per_example, given["loss_target"]))
    with _jax.named_scope("update"):
        delta_w, new_m, new_v = {}, {}, {}
        for n in TWIN_WEIGHTS:
            delta_w[n], new_m[n], new_v[n] = _adamw(weights[n], grad_w[n], given["m_" + n], given["v_" + n])
    return (loss, grad_x, *[grad_w[n] for n in TWIN_WEIGHTS], *[delta_w[n] for n in TWIN_WEIGHTS],
            *[new_m[n] for n in TWIN_WEIGHTS], *[new_v[n] for n in TWIN_WEIGHTS])
```

```python
import functools

import jax
import jax.numpy as jnp
from jax import lax
from jax.experimental import pallas as pl
from jax.experimental.pallas import tpu as pltpu

F32 = jnp.float32
BF16 = jnp.bfloat16
HI = lax.Precision.HIGHEST
MESH_ID = pl.DeviceIdType.MESH

D_MODEL = 1024
EPS = 1e-6
S5_G, S5_P, S5_H = 32, 64, 16
S5_W = S5_G * S5_H
S5_GP = S5_G * S5_P
SEG = 8
SCAN_ROWS = 256
GLA_HEADS, GLA_DK, GLA_DV = 4, 64, 128
GLA_CHUNK = 64
GLA_TAU = 16.0
GLA_RANK = 16
ADAM_LR, ADAM_B1, ADAM_B2, ADAM_EPS, ADAM_WD, ADAM_STEP = 0.001, 0.9, 0.999, 1e-08, 0.01, 10
V7X_VMEM_LIMIT = 56 * 1024 * 1024
LANE = 128

WEIGHTS = ['ffn1_norm', 'ffn1_w1', 'ffn1_w3', 'ffn1_w2', 'mix_norm', 'w_in', 's5_lambda_re', 's5_lambda_im',
           's5_log_dt', 's5_b_re', 's5_b_im', 's5_c_re', 's5_c_im', 's5_d', 's5_glu_w', 's5_glu_b', 'gla_a_up_w',
           'gla_a_up_b', 'gla_out_norm', 'proj_s5', 'proj_gla', 'w_out', 'ffn2_norm', 'ffn2_w1', 'ffn2_w3',
           'ffn2_w2', 'final_norm']
SHARDED = ['ffn1_w1', 'ffn1_w3', 'ffn1_w2', 'w_in', 's5_glu_w', 'proj_s5', 'proj_gla', 'w_out',
           'ffn2_w1', 'ffn2_w3', 'ffn2_w2']
COL_SHARDED = ['ffn1_w1', 'ffn1_w3', 'w_in', 'proj_s5', 'proj_gla', 'ffn2_w1', 'ffn2_w3', 'gla_a_up_w']
SMALL = [n for n in WEIGHTS if n not in SHARDED]


def _pcall(body, **kw):
    return pl.pallas_call(body, **kw)


def _params(**kw):
    return pltpu.CompilerParams(vmem_limit_bytes=V7X_VMEM_LIMIT, **kw)


def _pick(n, cap, quantum):
    if n <= cap:
        return n
    best = None
    for t in range(quantum, cap + 1, quantum):
        if n % t == 0:
            best = t
    assert best is not None, (n, cap, quantum)
    return best


def _sigmoid(x):
    return jax.nn.sigmoid(x)


def _mm(a, b, *, name, ta=False, tb=False, out_dtype=F32, alpha=1.0, res=None, bias=None, exact=False):
    (k_a, m) = a.shape if ta else a.shape[::-1]
    (k_b, n) = b.shape[::-1] if tb else b.shape
    assert k_a == k_b, (a.shape, b.shape, ta, tb)
    k = k_a
    tm = _pick(m, 512, 128)
    tn = _pick(n, 1408, 128)
    tk = _pick(k, 1408, 128)
    nk = k // tk
    dims = (((0,) if ta else (1,), (1,) if tb else (0,)), ((), ()))
    op_dtype = F32 if exact else BF16

    def body(*refs):
        a_ref, b_ref = refs[0], refs[1]
        pos = 2
        res_ref = bias_ref = None
        if res is not None:
            res_ref = refs[pos]
            pos += 1
        if bias is not None:
            bias_ref = refs[pos]
            pos += 1
        o_ref, acc_ref = refs[pos], refs[pos + 1]
        kk = pl.program_id(2)

        @pl.when(kk == 0)
        def _():
            acc_ref[...] = jnp.zeros_like(acc_ref)

        acc_ref[...] += lax.dot_general(a_ref[...].astype(op_dtype), b_ref[...].astype(op_dtype), dims,
                                        precision=HI if exact else None, preferred_element_type=F32)

        @pl.when(kk == nk - 1)
        def _():
            o = acc_ref[...]
            if alpha != 1.0:
                o = o * alpha
            if bias_ref is not None:
                o = o + bias_ref[...]
            if res_ref is not None:
                o = o + res_ref[...]
            o_ref[...] = o.astype(out_dtype)

    a_spec = pl.BlockSpec((tk, tm), lambda i, j, kk: (kk, i)) if ta else pl.BlockSpec((tm, tk), lambda i, j, kk: (i, kk))
    b_spec = pl.BlockSpec((tn, tk), lambda i, j, kk: (j, kk)) if tb else pl.BlockSpec((tk, tn), lambda i, j, kk: (kk, j))
    ins, in_specs = [a, b], [a_spec, b_spec]
    if res is not None:
        ins.append(res)
        in_specs.append(pl.BlockSpec((tm, tn), lambda i, j, kk: (i, j)))
    if bias is not None:
        ins.append(bias)
        in_specs.append(pl.BlockSpec((1, tn), lambda i, j, kk: (0, j)))
    return _pcall(body, name=name, grid=(m // tm, n // tn, nk), in_specs=in_specs,
                  out_specs=pl.BlockSpec((tm, tn), lambda i, j, kk: (i, j)),
                  out_shape=jax.ShapeDtypeStruct((m, n), out_dtype),
                  scratch_shapes=[pltpu.VMEM((tm, tn), F32)], compiler_params=_params())(*ins)


def _rows(body, ins, outs, *, n, name, tm=256):
    tm = _pick(n, tm, 16)
    in_specs = []
    for arr, kind in ins:
        if kind == 'r':
            in_specs.append(pl.BlockSpec((tm, arr.shape[1]), lambda i: (i, 0)))
        else:
            in_specs.append(pl.BlockSpec(arr.shape, lambda i: (0, 0)))
    out_specs, out_shape = [], []
    for cols, dtype, kind in outs:
        if kind == 'r':
            out_specs.append(pl.BlockSpec((tm, cols), lambda i: (i, 0)))
            out_shape.append(jax.ShapeDtypeStruct((n, cols), dtype))
        else:
            out_specs.append(pl.BlockSpec((1, cols), lambda i: (0, 0)))
            out_shape.append(jax.ShapeDtypeStruct((1, cols), dtype))
    n_in = len(ins)
    acc_ids = [j for j, o in enumerate(outs) if o[2] == 'a']

    def wrapped(*refs):
        if acc_ids:
            @pl.when(pl.program_id(0) == 0)
            def _():
                for j in acc_ids:
                    refs[n_in + j][...] = jnp.zeros_like(refs[n_in + j])
        body(*refs)

    res = _pcall(wrapped, name=name, grid=(n // tm,), in_specs=in_specs, out_specs=out_specs, out_shape=out_shape,
                 compiler_params=_params())(*[a for a, _ in ins])
    return res


def _rms_fwd(x, g, name):
    def body(x_ref, g_ref, o_ref):
        xv = x_ref[...]
        rstd = lax.rsqrt(jnp.mean(xv * xv, axis=-1, keepdims=True) + EPS)
        o_ref[...] = (xv * rstd * g_ref[...]).astype(BF16)
    return _rows(body, [(x, 'r'), (g, 'f')], [(x.shape[1], BF16, 'r')], n=x.shape[0], name=name)[0]


def _rms_bwd(x, g, dn, dres, name):
    def body(x_ref, g_ref, dn_ref, dres_ref, dx_ref, dg_ref):
        xv = x_ref[...]
        rstd = lax.rsqrt(jnp.mean(xv * xv, axis=-1, keepdims=True) + EPS)
        xh = xv * rstd
        dn = dn_ref[...]
        dg_ref[...] += jnp.sum(dn * xh, axis=0, keepdims=True)
        dxh = dn * g_ref[...]
        dx_ref[...] = dres_ref[...] + rstd * (dxh - xh * jnp.mean(dxh * xh, axis=-1, keepdims=True))
    d = x.shape[1]
    return _rows(body, [(x, 'r'), (g, 'f'), (dn, 'r'), (dres, 'r')], [(d, F32, 'r'), (d, F32, 'a')],
                 n=x.shape[0], name=name)


def _swiglu_fwd(a, b, name):
    def body(a_ref, b_ref, o_ref):
        av = a_ref[...]
        o_ref[...] = (av * _sigmoid(av) * b_ref[...]).astype(BF16)
    return _rows(body, [(a, 'r'), (b, 'r')], [(a.shape[1], BF16, 'r')], n=a.shape[0], name=name)[0]


def _swiglu_bwd(dhm, a, b, name):
    def body(d_ref, a_ref, b_ref, da_ref, db_ref):
        dv, av, bv = d_ref[...], a_ref[...], b_ref[...]
        sg = _sigmoid(av)
        da_ref[...] = (dv * bv * (sg * (1.0 + av * (1.0 - sg)))).astype(BF16)
        db_ref[...] = (dv * av * sg).astype(BF16)
    f = a.shape[1]
    return _rows(body, [(dhm, 'r'), (a, 'r'), (b, 'r')], [(f, BF16, 'r'), (f, BF16, 'r')], n=a.shape[0], name=name)


def _gelu_parts(y):
    c0 = 0.7978845608028654
    inner = c0 * (y + 0.044715 * y * y * y)
    th = jnp.tanh(inner)
    return th, c0 * (1.0 + 3.0 * 0.044715 * y * y)


def _gelu_fwd(y, name):
    def body(y_ref, o_ref):
        yv = y_ref[...]
        th, _ = _gelu_parts(yv)
        o_ref[...] = 0.5 * yv * (1.0 + th)
    return _rows(body, [(y, 'r')], [(y.shape[1], F32, 'r')], n=y.shape[0], name=name)[0]


def _glu_fwd(zg, t, name):
    def body(z_ref, t_ref, o_ref):
        o_ref[...] = (z_ref[...] * _sigmoid(t_ref[...])).astype(BF16)
    return _rows(body, [(zg, 'r'), (t, 'r')], [(zg.shape[1], BF16, 'r')], n=zg.shape[0], name=name)[0]


def _glu_bwd1(dy, zg, t, name):
    def body(dy_ref, z_ref, t_ref, dz_ref, dt_ref, db_ref):
        dyv, zv = dy_ref[...], z_ref[...]
        sg = _sigmoid(t_ref[...])
        dz_ref[...] = dyv * sg
        dt = dyv * zv * sg * (1.0 - sg)
        dt_ref[...] = dt.astype(BF16)
        db_ref[...] += jnp.sum(dt, axis=0, keepdims=True)
    w = zg.shape[1]
    return _rows(body, [(dy, 'r'), (zg, 'r'), (t, 'r')], [(w, F32, 'r'), (w, BF16, 'r'), (w, F32, 'a')],
                 n=zg.shape[0], name=name)


def _glu_bwd2(dzg, ys, u, dskip, name):
    def body(dz_ref, y_ref, u_ref, d_ref, dy_ref, du_ref, dd_ref):
        yv = y_ref[...]
        th, dinner = _gelu_parts(yv)
        dy = dz_ref[...] * (0.5 * (1.0 + th) + 0.5 * yv * (1.0 - th * th) * dinner)
        dy_ref[...] = dy
        du_ref[...] = dy * d_ref[...]
        dd_ref[...] += jnp.sum(dy * u_ref[...], axis=0, keepdims=True)
    w = ys.shape[1]
    return _rows(body, [(dzg, 'r'), (ys, 'r'), (u, 'r'), (dskip, 'f')], [(w, F32, 'r'), (w, F32, 'r'), (w, F32, 'a')],
                 n=ys.shape[0], name=name)


def _scale_rows(u, dskip, name):
    def body(u_ref, d_ref, o_ref):
        o_ref[...] = u_ref[...] * d_ref[...]
    return _rows(body, [(u, 'r'), (dskip, 'f')], [(u.shape[1], F32, 'r')], n=u.shape[0], name=name)[0]


def _merge_fwd(zg, ps, pg, name):
    def body(z_ref, ps_ref, pg_ref, o_ref):
        zv = z_ref[...]
        o_ref[...] = (_sigmoid(zv[:, :D_MODEL]) * ps_ref[...] + _sigmoid(zv[:, D_MODEL:]) * pg_ref[...]).astype(BF16)
    return _rows(body, [(zg, 'r'), (ps, 'r'), (pg, 'r')], [(D_MODEL, BF16, 'r')], n=zg.shape[0], name=name)[0]


def _merge_bwd(dm, zg, ps, pg, name):
    def body(dm_ref, z_ref, ps_ref, pg_ref, dps_ref, dpg_ref, dz_ref):
        dmv, zv = dm_ref[...], z_ref[...]
        s1, s2 = _sigmoid(zv[:, :D_MODEL]), _sigmoid(zv[:, D_MODEL:])
        dps_ref[...] = (dmv * s1).astype(BF16)
        dpg_ref[...] = (dmv * s2).astype(BF16)
        dz_ref[:, :D_MODEL] = dmv * ps_ref[...] * s1 * (1.0 - s1)
        dz_ref[:, D_MODEL:] = dmv * pg_ref[...] * s2 * (1.0 - s2)
    return _rows(body, [(dm, 'r'), (zg, 'r'), (ps, 'r'), (pg, 'r')],
                 [(D_MODEL, BF16, 'r'), (D_MODEL, BF16, 'r'), (2 * D_MODEL, F32, 'r')], n=zg.shape[0], name=name)


def _final_loss(h, g, tgt, name):
    def body(h_ref, g_ref, t_ref, loss_ref, dh_ref, dg_ref):
        hv = h_ref[...]
        rstd = lax.rsqrt(jnp.mean(hv * hv, axis=-1, keepdims=True) + EPS)
        xh = hv * rstd
        err = xh * g_ref[...] - t_ref[...]
        part = 0.5 * jnp.sum(jnp.mean(err * err, axis=-1, keepdims=True), axis=0, keepdims=True)
        loss_ref[...] += jnp.broadcast_to(part, loss_ref.shape)
        dout = err * (1.0 / hv.shape[1])
        dg_ref[...] += jnp.sum(dout * xh, axis=0, keepdims=True)
        dxh = dout * g_ref[...]
        dh_ref[...] = rstd * (dxh - xh * jnp.mean(dxh * xh, axis=-1, keepdims=True))
    d = h.shape[1]
    return _rows(body, [(h, 'r'), (g, 'f'), (tgt, 'r')], [(LANE, F32, 'a'), (d, F32, 'r'), (d, F32, 'a')],
                 n=h.shape[0], name=name)


def _adamw(w, g, m, v, name):
    def body(w_ref, g_ref, m_ref, v_ref, d_ref, nm_ref, nv_ref):
        gv = g_ref[...]
        nm = ADAM_B1 * m_ref[...] + (1.0 - ADAM_B1) * gv
        nv = ADAM_B2 * v_ref[...] + (1.0 - ADAM_B2) * (gv * gv)
        m_hat = nm / (1.0 - ADAM_B1 ** ADAM_STEP)
        v_hat = nv / (1.0 - ADAM_B2 ** ADAM_STEP)
        d_ref[...] = -ADAM_LR * (m_hat / (jnp.sqrt(v_hat) + ADAM_EPS) + ADAM_WD * w_ref[...])
        nm_ref[...] = nm
        nv_ref[...] = nv
    c = w.shape[1]
    return _rows(body, [(w, 'r'), (g, 'r'), (m, 'r'), (v, 'r')], [(c, F32, 'r')] * 3, n=w.shape[0], name=name)


def _shift_rows(v, sh, down):
    rolled = pltpu.roll(v, sh if down else v.shape[0] - sh, axis=0)
    row = lax.broadcasted_iota(jnp.int32, v.shape, 0)
    keep = (row >= sh) if down else (row < v.shape[0] - sh)
    return jnp.where(keep, rolled, 0.0)


def _chain_segments(st_r, st_i, pw_r_ref, pw_i_ref, conj, down):
    vr, vi = st_r[...], st_i[...]
    sh, k = 1, 0
    while sh < SEG:
        pr, pi = pw_r_ref[k:k + 1, :], pw_i_ref[k:k + 1, :]
        if conj:
            pi = -pi
        sr, si = _shift_rows(vr, sh, down), _shift_rows(vi, sh, down)
        vr, vi = vr + pr * sr - pi * si, vi + pr * si + pi * sr
        sh, k = sh * 2, k + 1
    st_r[...] = _shift_rows(vr, 1, down)
    st_i[...] = _shift_rows(vi, 1, down)


def _s5_scan(bu, ar8, ai8, pw_r, pw_i, name):
    n = bu.shape[0]
    rb = SCAN_ROWS
    nb, steps, lc = n // rb, rb // SEG, 512

    def body(bu_ref, ar_ref, ai_ref, pwr_ref, pwi_ref, x_ref, st_r, st_i):
        ph, b = pl.program_id(0), pl.program_id(1)

        @pl.when((ph == 0) & (b == 0))
        def _():
            st_r[...] = jnp.zeros_like(st_r)
            st_i[...] = jnp.zeros_like(st_i)

        def scan(store):
            for c in range(S5_GP // lc):
                re, im = slice(c * lc, (c + 1) * lc), slice(S5_GP + c * lc, S5_GP + (c + 1) * lc)
                a_r, a_i = ar_ref[:, re], ai_ref[:, re]

                def step(s, carry):
                    xr, xi = carry
                    rows = pl.ds(pl.multiple_of(s * SEG, SEG), SEG)
                    nr = a_r * xr - a_i * xi + bu_ref[rows, re]
                    ni = a_r * xi + a_i * xr + bu_ref[rows, im]
                    if store:
                        x_ref[rows, re] = nr
                        x_ref[rows, im] = ni
                    return nr, ni

                xr, xi = lax.fori_loop(0, steps, step, (st_r[:, re], st_i[:, re]), unroll=4)
                st_r[:, re] = xr
                st_i[:, re] = xi

        @pl.when(ph == 0)
        def _():
            scan(False)

        @pl.when((ph == 0) & (b == nb - 1))
        def _():
            _chain_segments(st_r, st_i, pwr_ref, pwi_ref, conj=False, down=True)

        @pl.when(ph == 1)
        def _():
            scan(True)

    full = lambda a: pl.BlockSpec(a.shape, lambda ph, b: (0, 0))
    return _pcall(body, name=name, grid=(2, nb),
                  in_specs=[pl.BlockSpec((rb, 2 * S5_GP), lambda ph, b: (b, 0)), full(ar8), full(ai8), full(pw_r), full(pw_i)],
                  out_specs=pl.BlockSpec((rb, 2 * S5_GP), lambda ph, b: (b * ph, 0)),
                  out_shape=jax.ShapeDtypeStruct((n, 2 * S5_GP), F32),
                  scratch_shapes=[pltpu.VMEM((SEG, S5_GP), F32), pltpu.VMEM((SEG, S5_GP), F32)],
                  compiler_params=_params())(bu, ar8, ai8, pw_r, pw_i)


def _s5_scan_bwd(gx, xs, ar8, ai8, pw_r, pw_i, name):
    n = gx.shape[0]
    rb = SCAN_ROWS
    nb, steps, lc = n // rb, rb // SEG, 256

    def body(gx_ref, x_ref, ar_ref, ai_ref, pwr_ref, pwi_ref, lam_ref, da_ref, st_r, st_i):
        ph, b = pl.program_id(0), pl.program_id(1)

        @pl.when((ph == 0) & (b == 0))
        def _():
            st_r[...] = jnp.zeros_like(st_r)
            st_i[...] = jnp.zeros_like(st_i)
            da_ref[...] = jnp.zeros_like(da_ref)

        def scan(store):
            for c in range(S5_GP // lc):
                re, im = slice(c * lc, (c + 1) * lc), slice(S5_GP + c * lc, S5_GP + (c + 1) * lc)
                a_r, a_i = ar_ref[:, re], ai_ref[:, re]

                def step(s, carry):
                    rows = pl.ds(pl.multiple_of((steps - 1 - s) * SEG, SEG), SEG)
                    if store:
                        lr, li, dr, di = carry
                        xr, xi = x_ref[rows, re], x_ref[rows, im]
                        dr = dr + lr * xr + li * xi
                        di = di + li * xr - lr * xi
                    else:
                        lr, li = carry
                    nr = a_r * lr + a_i * li + gx_ref[rows, re]
                    ni = a_r * li - a_i * lr + gx_ref[rows, im]
                    if store:
                        lam_ref[rows, re] = nr
                        lam_ref[rows, im] = ni
                        return nr, ni, dr, di
                    return nr, ni

                if store:
                    lr, li, dr, di = lax.fori_loop(0, steps, step, (st_r[:, re], st_i[:, re], da_ref[:, re], da_ref[:, im]),
                                                   unroll=4)
                    da_ref[:, re] = dr
                    da_ref[:, im] = di
                else:
                    lr, li = lax.fori_loop(0, steps, step, (st_r[:, re], st_i[:, re]), unroll=4)
                st_r[:, re] = lr
                st_i[:, re] = li

        @pl.when(ph == 0)
        def _():
            scan(False)

        @pl.when((ph == 0) & (b == nb - 1))
        def _():
            _chain_segments(st_r, st_i, pwr_ref, pwi_ref, conj=True, down=False)

        @pl.when(ph == 1)
        def _():
            scan(True)

    full = lambda a: pl.BlockSpec(a.shape, lambda ph, b: (0, 0))
    rev = lambda ph, b: (nb - 1 - b, 0)
    return _pcall(body, name=name, grid=(2, nb),
                  in_specs=[pl.BlockSpec((rb, 2 * S5_GP), rev), pl.BlockSpec((rb, 2 * S5_GP), lambda ph, b: ((nb - 1 - b) * ph, 0)),
                            full(ar8), full(ai8), full(pw_r), full(pw_i)],
                  out_specs=[pl.BlockSpec((rb, 2 * S5_GP), lambda ph, b: (nb - 1 - b * ph, 0)),
                             pl.BlockSpec((SEG, 2 * S5_GP), lambda ph, b: (0, 0))],
                  out_shape=[jax.ShapeDtypeStruct((n, 2 * S5_GP), F32), jax.ShapeDtypeStruct((SEG, 2 * S5_GP), F32)],
                  scratch_shapes=[pltpu.VMEM((SEG, S5_GP), F32), pltpu.VMEM((SEG, S5_GP), F32)],
                  compiler_params=_params())(gx, xs, ar8, ai8, pw_r, pw_i)


def _s5_discretize(lam_re, lam_im, log_dt, b_re, b_im):
    dt = jnp.exp(log_dt)[:, None]
    mag = jnp.exp(lam_re * dt)
    ar = mag * jnp.cos(lam_im * dt)
    ai = mag * jnp.sin(lam_im * dt)
    den = lam_re * lam_re + lam_im * lam_im
    nr = ar - 1.0
    fr = (nr * lam_re + ai * lam_im) / den
    fi = (ai * lam_re - nr * lam_im) / den
    bbar_re = fr[:, :, None] * b_re - fi[:, :, None] * b_im
    bbar_im = fr[:, :, None] * b_im + fi[:, :, None] * b_re
    return ar, ai, bbar_re, bbar_im


def _block_diag(t):
    g, a, b = t.shape
    eye = jnp.eye(g, dtype=t.dtype)
    return (t[:, :, None, :] * eye[:, None, :, None]).reshape(g * a, g * b)


def _diag_blocks(m, a, b):
    g = S5_G
    return jnp.einsum('gagb->gab', m.reshape(g, a, g, b))


def _permute_rows(t):
    n = t.shape[0]
    return t.reshape(SEG, n // SEG, t.shape[1]).transpose(1, 0, 2).reshape(n, t.shape[1])


def _unpermute_rows(t):
    n = t.shape[0]
    return t.reshape(n // SEG, SEG, t.shape[1]).transpose(1, 0, 2).reshape(n, t.shape[1])


def _segment_powers(ar, ai, seg_steps):
    pr, pi = ar.reshape(1, S5_GP), ai.reshape(1, S5_GP)
    e = 1
    while e < seg_steps:
        pr, pi = pr * pr - pi * pi, 2.0 * pr * pi
        e *= 2
    assert e == seg_steps, "segment length must be a power of two"
    rows_r, rows_i = [], []
    for _ in range(3):
        rows_r.append(pr)
        rows_i.append(pi)
        pr, pi = pr * pr - pi * pi, 2.0 * pr * pi
    pad = jnp.zeros((SEG - 3, S5_GP), F32)
    return jnp.concatenate(rows_r + [pad], axis=0), jnp.concatenate(rows_i + [pad], axis=0)


NT = (((1,), (1,)), ((), ()))
TN = (((0,), (0,)), ((), ()))


def _dot(a, b, dims=None, exact=False):
    dims = (((1,), (0,)), ((), ())) if dims is None else dims
    if exact:
        return lax.dot_general(a, b, dims, precision=HI, preferred_element_type=F32)
    return lax.dot_general(a.astype(BF16), b.astype(BF16), dims, preferred_element_type=F32)


def _gla_chunk_fwd(qc, kc, vc, al, wup, bup, s_prev, tril):
    z = _dot(al, wup) + bup
    la = (jnp.minimum(z, 0.0) - jnp.log(1.0 + jnp.exp(-jnp.abs(z)))) * (1.0 / GLA_TAU)
    bc = _dot(tril, la, exact=True)
    blb = _dot(la, jnp.ones((GLA_CHUNK, GLA_DV), F32), TN, exact=True)
    bl = bc[GLA_CHUNK - 1:GLA_CHUNK, :]
    ebc = jnp.exp(bc)
    qt = qc * (GLA_DK ** -0.5) * ebc
    kt = kc * jnp.exp(-bc)
    ke = kc * jnp.exp(bl - bc)
    sc = _dot(qt, kt, NT) * tril
    o = _dot(sc, vc) + _dot(qt, s_prev)
    return z, bc, bl, blb, ebc, qt, kt, ke, sc, o


def _gla_specs(n, arrs):
    specs = []
    for a in arrs:
        if a.ndim == 3:
            specs.append(pl.BlockSpec((None,) + a.shape[1:], lambda h: (h, 0, 0)))
        else:
            specs.append(pl.BlockSpec(a.shape, lambda h: (0, 0)))
    return specs


def _gla_fwd(q, k, v, r, al, wup, bup, gn, name):
    n = q.shape[1]
    nc = n // GLA_CHUNK

    def body(q_ref, k_ref, v_ref, r_ref, al_ref, wup_ref, bup_ref, gn_ref, y_ref, sp_ref):
        ri = lax.broadcasted_iota(jnp.int32, (GLA_CHUNK, GLA_CHUNK), 0)
        ci = lax.broadcasted_iota(jnp.int32, (GLA_CHUNK, GLA_CHUNK), 1)
        tril = (ri >= ci).astype(F32)

        def chunk(c, s_prev):
            rows = pl.ds(pl.multiple_of(c * GLA_CHUNK, GLA_CHUNK), GLA_CHUNK)
            vc, rc = v_ref[rows, :], r_ref[rows, :]
            _, _, _, blb, _, _, _, ke, _, o = _gla_chunk_fwd(q_ref[rows, :], k_ref[rows, :], vc, al_ref[rows, :],
                                                              wup_ref[...], bup_ref[...], s_prev, tril)
            sp_ref[c] = s_prev
            rstd = lax.rsqrt(jnp.mean(o * o, axis=-1, keepdims=True) + EPS)
            y_ref[rows, :] = o * rstd * gn_ref[...] * (rc * _sigmoid(rc))
            return jnp.exp(blb) * s_prev + _dot(ke, vc, TN)

        lax.fori_loop(0, nc, chunk, jnp.zeros((GLA_DK, GLA_DV), F32))

    ins = [q, k, v, r, al, wup, bup, gn]
    return _pcall(body, name=name, grid=(GLA_HEADS,), in_specs=_gla_specs(n, ins),
                  out_specs=[pl.BlockSpec((None, n, GLA_DV), lambda h: (h, 0, 0)),
                             pl.BlockSpec((None, nc, GLA_DK, GLA_DV), lambda h: (h, 0, 0, 0))],
                  out_shape=[jax.ShapeDtypeStruct((GLA_HEADS, n, GLA_DV), F32),
                             jax.ShapeDtypeStruct((GLA_HEADS, nc, GLA_DK, GLA_DV), F32)],
                  compiler_params=_params())(*ins)


def _gla_bwd(q, k, v, r, al, wup, bup, gn, sp, dy, name):
    n = q.shape[1]
    nc = n // GLA_CHUNK

    def body(q_ref, k_ref, v_ref, r_ref, al_ref, wup_ref, bup_ref, gn_ref, sp_ref, dy_ref,
             dq_ref, dk_ref, dv_ref, dr_ref, dz_ref, dgn_ref, dbup_ref):
        ri = lax.broadcasted_iota(jnp.int32, (GLA_CHUNK, GLA_CHUNK), 0)
        ci = lax.broadcasted_iota(jnp.int32, (GLA_CHUNK, GLA_CHUNK), 1)
        tril = (ri >= ci).astype(F32)
        triu = (ri <= ci).astype(F32)

        def chunk(i, carry):
            ds, dgn, dbup = carry
            c = nc - 1 - i
            rows = pl.ds(pl.multiple_of(c * GLA_CHUNK, GLA_CHUNK), GLA_CHUNK)
            qc, kc, vc, rc = q_ref[rows, :], k_ref[rows, :], v_ref[rows, :], r_ref[rows, :]
            s_prev = sp_ref[c]
            z, bc, bl, blb, ebc, qt, kt, ke, sc, o = _gla_chunk_fwd(qc, kc, vc, al_ref[rows, :], wup_ref[...],
                                                                     bup_ref[...], s_prev, tril)
            rstd = lax.rsqrt(jnp.mean(o * o, axis=-1, keepdims=True) + EPS)
            on = o * rstd
            sr = _sigmoid(rc)
            sil = rc * sr
            dyv, gnv = dy_ref[rows, :], gn_ref[...]
            dgn = dgn + jnp.sum(dyv * on * sil, axis=0, keepdims=True)
            dr_ref[rows, :] = dyv * on * gnv * (sr * (1.0 + rc * (1.0 - sr)))
            don = dyv * gnv * sil
            do = rstd * (don - on * jnp.mean(don * on, axis=-1, keepdims=True))
            dp = _dot(do, vc, NT) * tril
            dv_ref[rows, :] = _dot(sc, do, TN) + _dot(ke, ds)
            dqt = _dot(dp, kt) + _dot(do, s_prev, NT)
            dkt = _dot(dp, qt, TN)
            dke = _dot(vc, ds, NT)
            ddec = _dot(jnp.ones((8, GLA_DV), F32), ds * s_prev, NT, exact=True)[0:1, :]
            ds_new = jnp.exp(blb) * ds + _dot(qt, do, TN)
            dq_ref[rows, :] = dqt * (GLA_DK ** -0.5) * ebc
            dk_ref[rows, :] = dkt * jnp.exp(-bc) + dke * jnp.exp(bl - bc)
            dbc = dqt * qt - dkt * kt - dke * ke
            dbl = jnp.sum(dke * ke, axis=0, keepdims=True) + ddec * jnp.exp(bl)
            dla = _dot(triu, dbc, exact=True) + dbl
            dz = dla * (1.0 - _sigmoid(z)) * (1.0 / GLA_TAU)
            dz_ref[rows, :] = dz
            dbup = dbup + jnp.sum(dz, axis=0, keepdims=True)
            return ds_new, dgn, dbup

        _, dgn, dbup = lax.fori_loop(0, nc, chunk, (jnp.zeros((GLA_DK, GLA_DV), F32), jnp.zeros((1, GLA_DV), F32),
                                                   jnp.zeros((1, GLA_DK), F32)))
        dgn_ref[...] = dgn
        dbup_ref[...] = dbup

    ins = [q, k, v, r, al, wup, bup, gn, sp, dy]
    in_specs = _gla_specs(n, ins[:8]) + [pl.BlockSpec((None, nc, GLA_DK, GLA_DV), lambda h: (h, 0, 0, 0)),
                                         pl.BlockSpec((None, n, GLA_DV), lambda h: (h, 0, 0))]
    hs = lambda w: pl.BlockSpec((None, n, w), lambda h: (h, 0, 0))
    h1 = lambda w: pl.BlockSpec((None, 1, w), lambda h: (h, 0, 0))
    sh = lambda w: jax.ShapeDtypeStruct((GLA_HEADS, n, w), F32)
    s1 = lambda w: jax.ShapeDtypeStruct((GLA_HEADS, 1, w), F32)
    return _pcall(body, name=name, grid=(GLA_HEADS,), in_specs=in_specs,
                  out_specs=[hs(GLA_DK), hs(GLA_DK), hs(GLA_DV), hs(GLA_DV), hs(GLA_DK), h1(GLA_DV), h1(GLA_DK)],
                  out_shape=[sh(GLA_DK), sh(GLA_DK), sh(GLA_DV), sh(GLA_DV), sh(GLA_DK), s1(GLA_DV), s1(GLA_DK)],
                  compiler_params=_params())(*ins)


def _heads(t, w):
    return t.reshape(t.shape[0], GLA_HEADS, w).transpose(1, 0, 2)


def _unheads(t):
    return t.transpose(1, 0, 2).reshape(t.shape[1], GLA_HEADS * t.shape[2])


ANY = pl.BlockSpec(memory_space=pl.ANY)


def _place():
    x, y, c = lax.axis_index("x"), lax.axis_index("y"), lax.axis_index("c")
    chips = [(1 - x, y), (x, 1 - y), (1 - x, 1 - y)]
    return x, y, c, chips


def _remote(src, dst, ssem, rsem, dev):
    return pltpu.make_async_remote_copy(src_ref=src, dst_ref=dst, send_sem=ssem, recv_sem=rsem, device_id=dev,
                                        device_id_type=MESH_ID)


def _half(c, rows):
    h = rows // 2
    return pl.ds(pl.multiple_of(c * h, 8), h)


def _gather_shards(shards, name):
    nw = len(shards)

    def body(*refs):
        ins, outs = refs[:nw], refs[nw:2 * nw]
        ssem, rsem, lsem = refs[2 * nw:]
        x, y, c, chips = _place()
        mine = 2 * x + y
        sibling = (x, y, 1 - c)
        local, sends = [], []
        for w in range(nw):
            half = _half(c, ins[w].shape[0])
            cp = pltpu.make_async_copy(ins[w], outs[w].at[mine], lsem.at[w])
            cp.start()
            local.append(cp)
            for k, (px, py) in enumerate(chips):
                cp = _remote(ins[w].at[half], outs[w].at[mine, half], ssem.at[6 * w + k], rsem.at[6 * w + k], (px, py, c))
                cp.start()
                sends.append(cp)
        for w in range(nw):
            half = _half(c, ins[w].shape[0])
            for k, (px, py) in enumerate(chips):
                theirs = outs[w].at[2 * px + py, half]
                _remote(theirs, theirs, ssem.at[6 * w + k], rsem.at[6 * w + k], (px, py, c)).wait_recv()
                cp = _remote(theirs, theirs, ssem.at[6 * w + 3 + k], rsem.at[6 * w + 3 + k], sibling)
                cp.start()
                sends.append(cp)
        for w in range(nw):
            other = _half(1 - c, ins[w].shape[0])
            for k, (px, py) in enumerate(chips):
                theirs = outs[w].at[2 * px + py, other]
                _remote(theirs, theirs, ssem.at[6 * w + 3 + k], rsem.at[6 * w + 3 + k], sibling).wait_recv()
        for cp in sends:
            cp.wait_send()
        for cp in local:
            cp.wait()

    return _pcall(body, name=name, in_specs=[ANY] * nw, out_specs=[ANY] * nw,
                  out_shape=[jax.ShapeDtypeStruct((4,) + s.shape, s.dtype) for s in shards],
                  scratch_shapes=[pltpu.SemaphoreType.DMA((6 * nw,)), pltpu.SemaphoreType.DMA((6 * nw,)),
                                  pltpu.SemaphoreType.DMA((nw,))],
                  compiler_params=pltpu.CompilerParams(has_side_effects=True))(*shards)


def _swap_halves(grads, name):
    nw = len(grads)

    def body(*refs):
        ins, outs = refs[:nw], refs[nw:2 * nw]
        ssem, rsem = refs[2 * nw:]
        x, y, c, _ = _place()
        cps = []
        for w in range(nw):
            other = _half(1 - c, ins[w].shape[1])
            cp = _remote(ins[w].at[:, other, :], outs[w], ssem.at[w], rsem.at[w], (x, y, 1 - c))
            cp.start()
            cps.append(cp)
        for cp in cps:
            cp.wait()

    return _pcall(body, name=name, in_specs=[ANY] * nw, out_specs=[ANY] * nw,
                  out_shape=[jax.ShapeDtypeStruct((4, g.shape[1] // 2, g.shape[2]), g.dtype) for g in grads],
                  scratch_shapes=[pltpu.SemaphoreType.DMA((nw,)), pltpu.SemaphoreType.DMA((nw,))],
                  compiler_params=pltpu.CompilerParams(has_side_effects=True))(*grads)


def _scatter_chip_sums(sums, name):
    nw = len(sums)

    def body(*refs):
        ins, outs = refs[:nw], refs[nw:2 * nw]
        ssem, rsem = refs[2 * nw:]
        x, y, c, chips = _place()
        cps = []
        for w in range(nw):
            for k, (px, py) in enumerate(chips):
                cp = _remote(ins[w].at[2 * px + py], outs[w].at[k], ssem.at[3 * w + k], rsem.at[3 * w + k], (px, py, c))
                cp.start()
                cps.append(cp)
        for cp in cps:
            cp.wait()

    return _pcall(body, name=name, in_specs=[ANY] * nw, out_specs=[ANY] * nw,
                  out_shape=[jax.ShapeDtypeStruct((3,) + s.shape[1:], s.dtype) for s in sums],
                  scratch_shapes=[pltpu.SemaphoreType.DMA((3 * nw,)), pltpu.SemaphoreType.DMA((3 * nw,))],
                  compiler_params=pltpu.CompilerParams(has_side_effects=True))(*sums)


def _join_halves(halves, name):
    nw = len(halves)

    def body(*refs):
        ins, outs = refs[:nw], refs[nw:2 * nw]
        ssem, rsem, lsem = refs[2 * nw:]
        x, y, c, _ = _place()
        cps = []
        for w in range(nw):
            half = _half(c, 2 * ins[w].shape[0])
            loc = pltpu.make_async_copy(ins[w], outs[w].at[half], lsem.at[w])
            loc.start()
            cp = _remote(ins[w], outs[w].at[half], ssem.at[w], rsem.at[w], (x, y, 1 - c))
            cp.start()
            cps.append((loc, cp))
        for w, (loc, cp) in enumerate(cps):
            other = outs[w].at[_half(1 - c, 2 * ins[w].shape[0])]
            cp.wait_send()
            _remote(other, other, ssem.at[w], rsem.at[w], (x, y, 1 - c)).wait_recv()
            loc.wait()

    return _pcall(body, name=name, in_specs=[ANY] * nw, out_specs=[ANY] * nw,
                  out_shape=[jax.ShapeDtypeStruct((2 * h.shape[0], h.shape[1]), h.dtype) for h in halves],
                  scratch_shapes=[pltpu.SemaphoreType.DMA((nw,)), pltpu.SemaphoreType.DMA((nw,)),
                                  pltpu.SemaphoreType.DMA((nw,))],
                  compiler_params=pltpu.CompilerParams(has_side_effects=True))(*halves)


def _chip_sum(g, recv, c_arr, name):
    _, r, cols = g.shape
    h = r // 2
    tr = _pick(h, 256, 16)
    g4 = g.reshape(4, 2, h, cols)

    def body(c_ref, g_ref, r_ref, o_ref):
        o_ref[...] = (g_ref[...] + r_ref[...]).astype(BF16)

    grid_spec = pltpu.PrefetchScalarGridSpec(
        num_scalar_prefetch=1, grid=(4, h // tr),
        in_specs=[pl.BlockSpec((None, None, tr, cols), lambda s, i, c_ref: (s, c_ref[0], i, 0)),
                  pl.BlockSpec((None, tr, cols), lambda s, i, c_ref: (s, i, 0))],
        out_specs=pl.BlockSpec((None, tr, cols), lambda s, i, c_ref: (s, i, 0)))
    return _pcall(body, name=name, grid_spec=grid_spec, out_shape=jax.ShapeDtypeStruct((4, h, cols), BF16),
                  compiler_params=_params())(c_arr, g4, recv)


def _owner_sum(sums, others, s_arr, name):
    _, h, cols = sums.shape
    tr = _pick(h, 256, 16)

    def body(s_ref, a_ref, o_ref, out_ref):
        f = lambda v: v.astype(F32)
        out_ref[...] = (f(a_ref[...]) + f(o_ref[0])) + (f(o_ref[1]) + f(o_ref[2]))

    grid_spec = pltpu.PrefetchScalarGridSpec(
        num_scalar_prefetch=1, grid=(h // tr,),
        in_specs=[pl.BlockSpec((None, tr, cols), lambda i, s_ref: (s_ref[0], i, 0)),
                  pl.BlockSpec((3, tr, cols), lambda i, s_ref: (0, i, 0))],
        out_specs=pl.BlockSpec((tr, cols), lambda i, s_ref: (i, 0)))
    return _pcall(body, name=name, grid_spec=grid_spec, out_shape=jax.ShapeDtypeStruct((h, cols), F32),
                  compiler_params=_params())(s_arr, sums, others)


def _allreduce_small(v, name):
    def body(v_ref, o_ref, r0, r1, ssem, rsem):
        x, y, c, chips = _place()
        cp = _remote(v_ref, r0, ssem.at[0], rsem.at[0], (x, y, 1 - c))
        cp.start()
        cp.wait()
        o_ref[...] = v_ref[...] + r0[...]
        cps = []
        for k, (px, py) in enumerate(chips):
            cp = _remote(o_ref, r1.at[k], ssem.at[1 + k], rsem.at[1 + k], (px, py, c))
            cp.start()
            cps.append(cp)
        for cp in cps:
            cp.wait()
        o_ref[...] = (o_ref[...] + r1[0]) + (r1[1] + r1[2])

    vm = pl.BlockSpec(memory_space=pltpu.VMEM)
    return _pcall(body, name=name, in_specs=[vm], out_specs=vm, out_shape=jax.ShapeDtypeStruct(v.shape, F32),
                  scratch_shapes=[pltpu.VMEM(v.shape, F32), pltpu.VMEM((3,) + v.shape, F32),
                                  pltpu.SemaphoreType.DMA((4,)), pltpu.SemaphoreType.DMA((4,))],
                  compiler_params=_params(has_side_effects=True))(v)


def _pack_small(parts):
    flat = jnp.concatenate([p.reshape(-1).astype(F32) for p in parts])
    pad = (-flat.shape[0]) % (64 * LANE)
    return jnp.pad(flat, (0, pad)).reshape(-1, LANE)


def _unpack_small(packed, like):
    flat, out, pos = packed.reshape(-1), [], 0
    for p in like:
        out.append(flat[pos:pos + p.size].reshape(p.shape))
        pos += p.size
    return out


def _ffn_fwd(h, g, w1, w3, w2, tag):
    n1 = _rms_fwd(h, g, f"{tag}_rms")
    a = _mm(n1, w1, name=f"{tag}_a")
    b = _mm(n1, w3, name=f"{tag}_b")
    hm = _swiglu_fwd(a, b, f"{tag}_act")
    out = _mm(hm, w2, alpha=0.5, res=h, name=f"{tag}_out")
    return out, (h, n1, a, b, hm)


def _ffn_bwd(dout, saved, g, w1, w3, w2, tag):
    h, n1, a, b, hm = saved
    dy = dout.astype(BF16)
    dhm = _mm(dy, w2, tb=True, alpha=0.5, name=f"{tag}_dhm")
    gw2 = _mm(hm, dy, ta=True, alpha=0.5, name=f"{tag}_gw2")
    da, db = _swiglu_bwd(dhm, a, b, f"{tag}_dact")
    gw1 = _mm(n1, da, ta=True, name=f"{tag}_gw1")
    gw3 = _mm(n1, db, ta=True, name=f"{tag}_gw3")
    dn = _mm(da, w1, tb=True, name=f"{tag}_dn1")
    dn = _mm(db, w3, tb=True, res=dn, name=f"{tag}_dn2")
    dh, dg = _rms_bwd(h, g, dn, dout, f"{tag}_drms")
    return dh, dg, gw1, gw3, gw2


def _local_step(x, tgt, p):
    n = x.shape[0]
    f = lambda name: p[name].reshape(1, D_MODEL) if name.endswith('_norm') and name != 'gla_out_norm' else p[name]
    h1, ffn1 = _ffn_fwd(x, f('ffn1_norm'), f('ffn1_w1'), f('ffn1_w3'), f('ffn1_w2'), "ffn1")
    u = _rms_fwd(h1, f('mix_norm'), "mix_rms")
    w_in = f('w_in')
    w_a = w_in[:, :2048]
    w_al = jnp.pad(w_in[:, 2048:2048 + GLA_RANK], ((0, 0), (0, LANE - GLA_RANK)))
    w_g = w_in[:, 2048 + GLA_RANK:]
    za = _mm(u, w_a, name="in_a")
    zg = _mm(u, w_g, name="in_g")
    al = _mm(u, w_al, name="in_al")
    ar, ai, bbar_re, bbar_im = _s5_discretize(f('s5_lambda_re'), f('s5_lambda_im'), f('s5_log_dt'), f('s5_b_re'), f('s5_b_im'))
    b_blk = jnp.concatenate([_block_diag(bbar_re.transpose(0, 2, 1)), _block_diag(bbar_im.transpose(0, 2, 1))], axis=1)
    c_blk = jnp.concatenate([_block_diag(f('s5_c_re').transpose(0, 2, 1)), -_block_diag(f('s5_c_im').transpose(0, 2, 1))], axis=0)
    b_blk, c_blk = b_blk.astype(BF16), c_blk.astype(BF16)
    ar8 = jnp.broadcast_to(ar.reshape(1, S5_GP), (SEG, S5_GP))
    ai8 = jnp.broadcast_to(ai.reshape(1, S5_GP), (SEG, S5_GP))
    pw_r, pw_i = _segment_powers(ar, ai, n // SEG)
    dskip = f('s5_d').reshape(1, S5_W)
    u_s5 = _permute_rows(za[:, :S5_W])
    bu = _mm(u_s5, b_blk, name="s5_bu")
    xs = _s5_scan(bu, ar8, ai8, pw_r, pw_i, "s5_scan")
    ys = _mm(xs, c_blk, res=_scale_rows(u_s5, dskip, "s5_skip"), name="s5_y")
    ys = _unpermute_rows(ys)
    zgelu = _gelu_fwd(ys, "s5_gelu")
    t_glu = _mm(zgelu, f('s5_glu_w'), bias=f('s5_glu_b').reshape(1, S5_W), name="s5_glu_t")
    y_s5 = _glu_fwd(zgelu, t_glu, "s5_glu")
    q, k = _heads(za[:, 512:768], GLA_DK), _heads(za[:, 768:1024], GLA_DK)
    v, r = _heads(za[:, 1024:1536], GLA_DV), _heads(za[:, 1536:2048], GLA_DV)
    wup = jnp.pad(f('gla_a_up_w'), ((0, LANE - GLA_RANK), (0, 0)))
    wup_h = wup.reshape(LANE, GLA_HEADS, GLA_DK).transpose(1, 0, 2)
    bup_h = f('gla_a_up_b').reshape(GLA_HEADS, 1, GLA_DK)
    gn_h = f('gla_out_norm').reshape(GLA_HEADS, 1, GLA_DV)
    y_gla_h, s_prev = _gla_fwd(q, k, v, r, al, wup_h, bup_h, gn_h, "gla_fwd")
    y_gla = _unheads(y_gla_h).astype(BF16)
    ps = _mm(y_s5, f('proj_s5'), name="proj_s5")
    pg = _mm(y_gla, f('proj_gla'), name="proj_gla")
    merged = _merge_fwd(zg, ps, pg, "merge")
    h2 = _mm(merged, f('w_out'), res=h1, name="w_out")
    h3, ffn2 = _ffn_fwd(h2, f('ffn2_norm'), f('ffn2_w1'), f('ffn2_w3'), f('ffn2_w2'), "ffn2")
    loss, dh3, g_final = _final_loss(h3, f('final_norm').reshape(1, D_MODEL), tgt, "loss")
    grads = {'final_norm': g_final.reshape(D_MODEL)}
    dh2, grads['ffn2_norm'], grads['ffn2_w1'], grads['ffn2_w3'], grads['ffn2_w2'] = _ffn_bwd(
        dh3, ffn2, f('ffn2_norm'), f('ffn2_w1'), f('ffn2_w3'), f('ffn2_w2'), "ffn2")
    dh2b = dh2.astype(BF16)
    dm = _mm(dh2b, f('w_out'), tb=True, name="d_merged")
    grads['w_out'] = _mm(merged, dh2b, ta=True, name="g_w_out")
    dps, dpg, dzg = _merge_bwd(dm, zg, ps, pg, "d_merge")
    grads['proj_s5'] = _mm(y_s5, dps, ta=True, name="g_proj_s5")
    grads['proj_gla'] = _mm(y_gla, dpg, ta=True, name="g_proj_gla")
    dy_s5 = _mm(dps, f('proj_s5'), tb=True, name="d_y_s5")
    dy_gla = _mm(dpg, f('proj_gla'), tb=True, name="d_y_gla")
    dzgelu, dt_glu, g_glu_b = _glu_bwd1(dy_s5, zgelu, t_glu, "d_glu")
    grads['s5_glu_b'] = g_glu_b.reshape(S5_W)
    grads['s5_glu_w'] = _mm(zgelu, dt_glu, ta=True, name="g_glu_w")
    dzgelu = _mm(dt_glu, f('s5_glu_w'), tb=True, res=dzgelu, name="d_gelu")
    dys, du_skip, g_d = _glu_bwd2(_permute_rows(dzgelu), _permute_rows(ys), u_s5, dskip, "d_s5_y")
    grads['s5_d'] = g_d.reshape(S5_G, S5_H)
    gx = _mm(dys, c_blk, tb=True, name="s5_gx")
    lam, da8 = _s5_scan_bwd(gx, xs, ar8, ai8, pw_r, pw_i, "s5_scan_bwd")
    g_c = _mm(dys, xs, ta=True, name="g_s5_c")
    grads['s5_c_re'] = _diag_blocks(g_c[:, :S5_GP], S5_H, S5_P)
    grads['s5_c_im'] = -_diag_blocks(g_c[:, S5_GP:], S5_H, S5_P)
    g_b = _mm(lam, u_s5, ta=True, name="g_s5_b")
    g_bbar_re = _diag_blocks(g_b[:S5_GP], S5_P, S5_H)
    g_bbar_im = _diag_blocks(g_b[S5_GP:], S5_P, S5_H)
    da = jnp.sum(da8, axis=0)
    g_ar, g_ai = da[:S5_GP].reshape(S5_G, S5_P), da[S5_GP:].reshape(S5_G, S5_P)
    _, disc_vjp = jax.vjp(_s5_discretize, f('s5_lambda_re'), f('s5_lambda_im'), f('s5_log_dt'), f('s5_b_re'), f('s5_b_im'))
    (grads['s5_lambda_re'], grads['s5_lambda_im'], grads['s5_log_dt'], grads['s5_b_re'],
     grads['s5_b_im']) = disc_vjp((g_ar, g_ai, g_bbar_re, g_bbar_im))
    du_s5 = _unpermute_rows(_mm(lam, b_blk, tb=True, res=du_skip, name="d_s5_u"))
    dq, dk, dv, dr, dz, dgn, dbup = _gla_bwd(q, k, v, r, al, wup_h, bup_h, gn_h, s_prev, _heads(dy_gla, GLA_DV), "gla_bwd")
    grads['gla_out_norm'] = dgn.reshape(GLA_HEADS * GLA_DV)
    grads['gla_a_up_b'] = dbup.reshape(GLA_HEADS * GLA_DK)
    dz = _unheads(dz)
    grads['gla_a_up_w'] = _mm(al, dz, ta=True, name="g_a_up")[:GLA_RANK]
    dal = _mm(dz, wup, tb=True, name="d_a_low")
    dza = jnp.concatenate([du_s5, _unheads(dq), _unheads(dk), _unheads(dv), _unheads(dr)], axis=1)
    g_wa = _mm(u, dza, ta=True, name="g_in_a")
    g_wg = _mm(u, dzg, ta=True, name="g_in_g")
    g_wal = _mm(u, dal, ta=True, name="g_in_al")
    grads['w_in'] = jnp.concatenate([g_wa, g_wal[:, :GLA_RANK], g_wg], axis=1)
    du = _mm(dza, w_a, tb=True, name="d_u_a")
    du = _mm(dzg, w_g, tb=True, res=du, name="d_u_g")
    du = _mm(dal, w_al, tb=True, res=du, name="d_u_al")
    dh1, g_mix = _rms_bwd(h1, f('mix_norm'), du, dh2, "d_mix_rms")
    grads['mix_norm'] = g_mix
    dx, grads['ffn1_norm'], grads['ffn1_w1'], grads['ffn1_w3'], grads['ffn1_w2'] = _ffn_bwd(
        dh1, ffn1, f('ffn1_norm'), f('ffn1_w1'), f('ffn1_w3'), f('ffn1_w2'), "ffn1")
    return loss[0, 0], dx, grads


def _train_step(a):
    x = a['x'][0]
    tgt = a['loss_target'][0]
    xi, yi, ci = lax.axis_index("x"), lax.axis_index("y"), lax.axis_index("c")
    c_arr = jnp.reshape(ci, (1,)).astype(jnp.int32)
    s_arr = jnp.reshape(2 * xi + yi, (1,)).astype(jnp.int32)
    gathered_names = SHARDED + ['gla_a_up_w']
    shards = [a[nm][0].astype(F32 if nm == 'gla_a_up_w' else BF16) for nm in gathered_names]
    gathered = dict(zip(gathered_names, _gather_shards(shards, "gather_weights")))
    p = {}
    for nm in gathered_names:
        g4 = gathered[nm]
        if nm in COL_SHARDED:
            p[nm] = jnp.concatenate([g4[s] for s in range(4)], axis=1)
        else:
            p[nm] = g4.reshape(4 * g4.shape[1], g4.shape[2])
    for nm in SMALL:
        if nm != 'gla_a_up_w':
            p[nm] = a[nm] if nm == 'final_norm' else a[nm][0]
    loss, dx, grads = _local_step(x, tgt, p)
    loss = lax.psum(loss, ("x", "y", "c"))
    g4s = []
    for nm in SHARDED:
        g = grads[nm]
        if nm in COL_SHARDED:
            g4s.append(jnp.stack(jnp.split(g, 4, axis=1)))
        else:
            g4s.append(g.reshape(4, g.shape[0] // 4, g.shape[1]))
    from_sibling = _swap_halves(g4s, "grad_swap_halves")
    chip_sums = [_chip_sum(g, r, c_arr, f"chip_sum_{nm}") for nm, g, r in zip(SHARDED, g4s, from_sibling)]
    from_chips = _scatter_chip_sums(chip_sums, "grad_scatter")
    halves = [_owner_sum(s, o, s_arr, f"owner_sum_{nm}") for nm, s, o in zip(SHARDED, chip_sums, from_chips)]
    red = dict(zip(SHARDED, _join_halves(halves, "grad_join_halves")))
    small_parts = [grads[nm].reshape(a[nm].shape) for nm in SMALL if nm != 'gla_a_up_w'] + [grads['gla_a_up_w']]
    small_sum = _unpack_small(_allreduce_small(_pack_small(small_parts), "allreduce_small"), small_parts)
    small_names = [nm for nm in SMALL if nm != 'gla_a_up_w']
    for nm, g in zip(small_names, small_sum[:-1]):
        red[nm] = g
    g_up = small_sum[-1]
    red['gla_a_up_w'] = lax.dynamic_slice(g_up, (0, (2 * xi + yi) * GLA_DK), (GLA_RANK, GLA_DK))
    out_g, out_d, out_m, out_v = {}, {}, {}, {}
    for nm in SHARDED:
        w = a[nm][0]
        d, nm_, nv_ = _adamw(w, red[nm], a['m_' + nm][0], a['v_' + nm][0], f"adamw_{nm}")
        shape = a[nm].shape
        out_g[nm], out_d[nm], out_m[nm], out_v[nm] = (t.reshape(shape) for t in (red[nm], d, nm_, nv_))
    rest = [nm for nm in WEIGHTS if nm not in SHARDED]
    pk = lambda pre: _pack_small([a[pre + nm] for nm in rest])
    d, nm_, nv_ = _adamw(pk(''), _pack_small([red[nm] for nm in rest]), pk('m_'), pk('v_'), "adamw_small")
    like = [a[nm] for nm in rest]
    for nm, g, dd, mm_, vv_ in zip(rest, [red[nm].reshape(a[nm].shape) for nm in rest], _unpack_small(d, like),
                                   _unpack_small(nm_, like), _unpack_small(nv_, like)):
        out_g[nm], out_d[nm], out_m[nm], out_v[nm] = g, dd, mm_, vv_
    return (loss, dx[None], *[out_g[nm] for nm in WEIGHTS], *[out_d[nm] for nm in WEIGHTS],
            *[out_m[nm] for nm in WEIGHTS], *[out_v[nm] for nm in WEIGHTS])


def kernel(x, ffn1_norm, ffn1_w1, ffn1_w3, ffn1_w2, mix_norm, w_in, s5_lambda_re, s5_lambda_im, s5_log_dt, s5_b_re, s5_b_im, s5_c_re, s5_c_im, s5_d, s5_glu_w, s5_glu_b, gla_a_up_w, gla_a_up_b, gla_out_norm, proj_s5, proj_gla, w_out, ffn2_norm, ffn2_w1, ffn2_w3, ffn2_w2, final_norm, loss_target, m_ffn1_norm, m_ffn1_w1, m_ffn1_w3, m_ffn1_w2, m_mix_norm, m_w_in, m_s5_lambda_re, m_s5_lambda_im, m_s5_log_dt, m_s5_b_re, m_s5_b_im, m_s5_c_re, m_s5_c_im, m_s5_d, m_s5_glu_w, m_s5_glu_b, m_gla_a_up_w, m_gla_a_up_b, m_gla_out_norm, m_proj_s5, m_proj_gla, m_w_out, m_ffn2_norm, m_ffn2_w1, m_ffn2_w3, m_ffn2_w2, m_final_norm, v_ffn1_norm, v_ffn1_w1, v_ffn1_w3, v_ffn1_w2, v_mix_norm, v_w_in, v_s5_lambda_re, v_s5_lambda_im, v_s5_log_dt, v_s5_b_re, v_s5_b_im, v_s5_c_re, v_s5_c_im, v_s5_d, v_s5_glu_w, v_s5_glu_b, v_gla_a_up_w, v_gla_a_up_b, v_gla_out_norm, v_proj_s5, v_proj_gla, v_w_out, v_ffn2_norm, v_ffn2_w1, v_ffn2_w3, v_ffn2_w2, v_final_norm):
    return _train_step(dict(locals()))
```

```python
import functools

import jax
import jax.numpy as jnp
from jax import lax
from jax.experimental import pallas as pl
from jax.experimental.pallas import tpu as pltpu

F32 = jnp.float32
BF16 = jnp.bfloat16
HI = lax.Precision.HIGHEST
MESH_ID = pl.DeviceIdType.MESH

D_MODEL = 1024
EPS = 1e-6
S5_G, S5_P, S5_H = 32, 64, 16
S5_W = S5_G * S5_H
S5_GP = S5_G * S5_P
SEG = 8
SCAN_ROWS = 256
GLA_HEADS, GLA_DK, GLA_DV = 4, 64, 128
GLA_CHUNK = 64
GLA_TAU = 16.0
GLA_RANK = 16
ADAM_LR, ADAM_B1, ADAM_B2, ADAM_EPS, ADAM_WD, ADAM_STEP = 0.001, 0.9, 0.999, 1e-08, 0.01, 10
V7X_VMEM_LIMIT = 56 * 1024 * 1024
LANE = 128

WEIGHTS = ['ffn1_norm', 'ffn1_w1', 'ffn1_w3', 'ffn1_w2', 'mix_norm', 'w_in', 's5_lambda_re', 's5_lambda_im',
           's5_log_dt', 's5_b_re', 's5_b_im', 's5_c_re', 's5_c_im', 's5_d', 's5_glu_w', 's5_glu_b', 'gla_a_up_w',
           'gla_a_up_b', 'gla_out_norm', 'proj_s5', 'proj_gla', 'w_out', 'ffn2_norm', 'ffn2_w1', 'ffn2_w3',
           'ffn2_w2', 'final_norm']
SHARDED = ['ffn1_w1', 'ffn1_w3', 'ffn1_w2', 'w_in', 's5_glu_w', 'proj_s5', 'proj_gla', 'w_out',
           'ffn2_w1', 'ffn2_w3', 'ffn2_w2']
COL_SHARDED = ['ffn1_w1', 'ffn1_w3', 'w_in', 'proj_s5', 'proj_gla', 'ffn2_w1', 'ffn2_w3', 'gla_a_up_w']
SMALL = [n for n in WEIGHTS if n not in SHARDED]
FFN_WEIGHTS = ['ffn1_w1', 'ffn1_w3', 'ffn1_w2', 'ffn2_w1', 'ffn2_w3', 'ffn2_w2']


def _pcall(body, **kw):
    return pl.pallas_call(body, **kw)


def _params(**kw):
    return pltpu.CompilerParams(vmem_limit_bytes=V7X_VMEM_LIMIT, **kw)


def _pick(n, cap, quantum):
    if n <= cap:
        return n
    best = None
    for t in range(quantum, cap + 1, quantum):
        if n % t == 0:
            best = t
    assert best is not None, (n, cap, quantum)
    return best


def _sigmoid(x):
    return jax.nn.sigmoid(x)


def _mm(a, b, *, name, ta=False, tb=False, out_dtype=F32, alpha=1.0, res=None, bias=None, exact=False, shard=None):
    ns = 4
    (k_a, m) = a.shape[-2:] if ta else a.shape[-2:][::-1]
    (k_b, n) = b.shape[-2:][::-1] if tb else b.shape[-2:]
    assert k_a == k_b, (a.shape, b.shape, ta, tb)
    assert (a.ndim == 3) == (shard in ('k', 'm')) and (b.ndim == 3) == (shard in ('n', 'k'))
    k = k_a
    tm = _pick(m, 1024, 128)
    tn = _pick(n, 1024, 128)
    tk = _pick(k, 1024, 128)
    pm, pn, pk = m // tm, n // tn, k // tk
    gm = pm * (ns if shard == 'm' else 1)
    gn = pn * (ns if shard == 'n' else 1)
    gk = pk * (ns if shard == 'k' else 1)
    dims = (((0,) if ta else (1,), (1,) if tb else (0,)), ((), ()))
    op_dtype = F32 if exact else BF16

    def body(*refs):
        a_ref, b_ref = refs[0], refs[1]
        pos = 2
        res_ref = bias_ref = None
        if res is not None:
            res_ref = refs[pos]
            pos += 1
        if bias is not None:
            bias_ref = refs[pos]
            pos += 1
        o_ref, acc_ref = refs[pos], refs[pos + 1]
        kk = pl.program_id(2)

        @pl.when(kk == 0)
        def _():
            acc_ref[...] = jnp.zeros_like(acc_ref)

        acc_ref[...] += lax.dot_general(a_ref[...].astype(op_dtype), b_ref[...].astype(op_dtype), dims,
                                        precision=HI if exact else None, preferred_element_type=F32)

        @pl.when(kk == gk - 1)
        def _():
            o = acc_ref[...]
            if alpha != 1.0:
                o = o * alpha
            if bias_ref is not None:
                o = o + bias_ref[...]
            if res_ref is not None:
                o = o + res_ref[...]
            o_ref[...] = o.astype(out_dtype)

    def spec(block, sharded_on, order):
        per = {'m': pm, 'n': pn, 'k': pk}

        def index(i, j, kk):
            g = {'m': i, 'n': j, 'k': kk}
            r, c = order(i % pm if shard == 'm' else i, j % pn if shard == 'n' else j, kk % pk if shard == 'k' else kk)
            if sharded_on is None:
                return (r, c)
            return (g[sharded_on] // per[sharded_on], r, c)

        return pl.BlockSpec(block if sharded_on is None else (None,) + block, index)

    a_sh = shard if shard in ('k', 'm') else None
    b_sh = shard if shard in ('n', 'k') else None
    o_sh = shard if shard in ('n', 'm') else None
    a_spec = spec((tk, tm), a_sh, lambda i, j, kk: (kk, i)) if ta else spec((tm, tk), a_sh, lambda i, j, kk: (i, kk))
    b_spec = spec((tn, tk), b_sh, lambda i, j, kk: (j, kk)) if tb else spec((tk, tn), b_sh, lambda i, j, kk: (kk, j))
    ins, in_specs = [a, b], [a_spec, b_spec]
    if res is not None:
        assert o_sh is None
        ins.append(res)
        in_specs.append(pl.BlockSpec((tm, tn), lambda i, j, kk: (i, j)))
    if bias is not None:
        assert o_sh is None
        ins.append(bias)
        in_specs.append(pl.BlockSpec((1, tn), lambda i, j, kk: (0, j)))
    out_shape = (m, n) if o_sh is None else (ns, m, n)
    return _pcall(body, name=name, grid=(gm, gn, gk), in_specs=in_specs,
                  out_specs=spec((tm, tn), o_sh, lambda i, j, kk: (i, j)),
                  out_shape=jax.ShapeDtypeStruct(out_shape, out_dtype),
                  scratch_shapes=[pltpu.VMEM((tm, tn), F32)], compiler_params=_params())(*ins)


def _rows(body, ins, outs, *, n, name, tm=256):
    tm = _pick(n, tm, 16)
    in_specs = []
    for arr, kind in ins:
        if kind == 'r':
            in_specs.append(pl.BlockSpec((tm, arr.shape[1]), lambda i: (i, 0)))
        else:
            in_specs.append(pl.BlockSpec(arr.shape, lambda i: (0, 0)))
    out_specs, out_shape = [], []
    for cols, dtype, kind in outs:
        if kind == 'r':
            out_specs.append(pl.BlockSpec((tm, cols), lambda i: (i, 0)))
            out_shape.append(jax.ShapeDtypeStruct((n, cols), dtype))
        else:
            out_specs.append(pl.BlockSpec((1, cols), lambda i: (0, 0)))
            out_shape.append(jax.ShapeDtypeStruct((1, cols), dtype))
    n_in = len(ins)
    acc_ids = [j for j, o in enumerate(outs) if o[2] == 'a']

    def wrapped(*refs):
        if acc_ids:
            @pl.when(pl.program_id(0) == 0)
            def _():
                for j in acc_ids:
                    refs[n_in + j][...] = jnp.zeros_like(refs[n_in + j])
        body(*refs)

    res = _pcall(wrapped, name=name, grid=(n // tm,), in_specs=in_specs, out_specs=out_specs, out_shape=out_shape,
                 compiler_params=_params())(*[a for a, _ in ins])
    return res


def _rms_fwd(x, g, name):
    def body(x_ref, g_ref, o_ref):
        xv = x_ref[...]
        rstd = lax.rsqrt(jnp.mean(xv * xv, axis=-1, keepdims=True) + EPS)
        o_ref[...] = (xv * rstd * g_ref[...]).astype(BF16)
    return _rows(body, [(x, 'r'), (g, 'f')], [(x.shape[1], BF16, 'r')], n=x.shape[0], name=name)[0]


def _rms_bwd(x, g, dn, dres, name):
    def body(x_ref, g_ref, dn_ref, dres_ref, dx_ref, dg_ref):
        xv = x_ref[...]
        rstd = lax.rsqrt(jnp.mean(xv * xv, axis=-1, keepdims=True) + EPS)
        xh = xv * rstd
        dn = dn_ref[...]
        dg_ref[...] += jnp.sum(dn * xh, axis=0, keepdims=True)
        dxh = dn * g_ref[...]
        dx_ref[...] = dres_ref[...] + rstd * (dxh - xh * jnp.mean(dxh * xh, axis=-1, keepdims=True))
    d = x.shape[1]
    return _rows(body, [(x, 'r'), (g, 'f'), (dn, 'r'), (dres, 'r')], [(d, F32, 'r'), (d, F32, 'a')],
                 n=x.shape[0], name=name)


def _swiglu_fwd(a, b, name):
    def body(a_ref, b_ref, o_ref):
        av = a_ref[...]
        o_ref[...] = (av * _sigmoid(av) * b_ref[...]).astype(BF16)
    return _rows(body, [(a, 'r'), (b, 'r')], [(a.shape[1], BF16, 'r')], n=a.shape[0], name=name)[0]


def _swiglu_bwd(dhm, a, b, name):
    def body(d_ref, a_ref, b_ref, da_ref, db_ref):
        dv, av, bv = d_ref[...], a_ref[...], b_ref[...]
        sg = _sigmoid(av)
        da_ref[...] = (dv * bv * (sg * (1.0 + av * (1.0 - sg)))).astype(BF16)
        db_ref[...] = (dv * av * sg).astype(BF16)
    f = a.shape[1]
    return _rows(body, [(dhm, 'r'), (a, 'r'), (b, 'r')], [(f, BF16, 'r'), (f, BF16, 'r')], n=a.shape[0], name=name)


def _gelu_parts(y):
    c0 = 0.7978845608028654
    inner = c0 * (y + 0.044715 * y * y * y)
    th = jnp.tanh(inner)
    return th, c0 * (1.0 + 3.0 * 0.044715 * y * y)


def _gelu_fwd(y, name):
    def body(y_ref, o_ref):
        yv = y_ref[...]
        th, _ = _gelu_parts(yv)
        o_ref[...] = 0.5 * yv * (1.0 + th)
    return _rows(body, [(y, 'r')], [(y.shape[1], F32, 'r')], n=y.shape[0], name=name)[0]


def _glu_fwd(zg, t, name):
    def body(z_ref, t_ref, o_ref):
        o_ref[...] = (z_ref[...] * _sigmoid(t_ref[...])).astype(BF16)
    return _rows(body, [(zg, 'r'), (t, 'r')], [(zg.shape[1], BF16, 'r')], n=zg.shape[0], name=name)[0]


def _glu_bwd1(dy, zg, t, name):
    def body(dy_ref, z_ref, t_ref, dz_ref, dt_ref, db_ref):
        dyv, zv = dy_ref[...], z_ref[...]
        sg = _sigmoid(t_ref[...])
        dz_ref[...] = dyv * sg
        dt = dyv * zv * sg * (1.0 - sg)
        dt_ref[...] = dt.astype(BF16)
        db_ref[...] += jnp.sum(dt, axis=0, keepdims=True)
    w = zg.shape[1]
    return _rows(body, [(dy, 'r'), (zg, 'r'), (t, 'r')], [(w, F32, 'r'), (w, BF16, 'r'), (w, F32, 'a')],
                 n=zg.shape[0], name=name)


def _glu_bwd2(dzg, ys, u, dskip, name):
    def body(dz_ref, y_ref, u_ref, d_ref, dy_ref, du_ref, dd_ref):
        yv = y_ref[...]
        th, dinner = _gelu_parts(yv)
        dy = dz_ref[...] * (0.5 * (1.0 + th) + 0.5 * yv * (1.0 - th * th) * dinner)
        dy_ref[...] = dy
        du_ref[...] = dy * d_ref[...]
        dd_ref[...] += jnp.sum(dy * u_ref[...], axis=0, keepdims=True)
    w = ys.shape[1]
    return _rows(body, [(dzg, 'r'), (ys, 'r'), (u, 'r'), (dskip, 'f')], [(w, F32, 'r'), (w, F32, 'r'), (w, F32, 'a')],
                 n=ys.shape[0], name=name)


def _scale_rows(u, dskip, name):
    def body(u_ref, d_ref, o_ref):
        o_ref[...] = u_ref[...] * d_ref[...]
    return _rows(body, [(u, 'r'), (dskip, 'f')], [(u.shape[1], F32, 'r')], n=u.shape[0], name=name)[0]


def _merge_fwd(zg, ps, pg, name):
    def body(z_ref, ps_ref, pg_ref, o_ref):
        zv = z_ref[...]
        o_ref[...] = (_sigmoid(zv[:, :D_MODEL]) * ps_ref[...] + _sigmoid(zv[:, D_MODEL:]) * pg_ref[...]).astype(BF16)
    return _rows(body, [(zg, 'r'), (ps, 'r'), (pg, 'r')], [(D_MODEL, BF16, 'r')], n=zg.shape[0], name=name)[0]


def _merge_bwd(dm, zg, ps, pg, name):
    def body(dm_ref, z_ref, ps_ref, pg_ref, dps_ref, dpg_ref, dz_ref):
        dmv, zv = dm_ref[...], z_ref[...]
        s1, s2 = _sigmoid(zv[:, :D_MODEL]), _sigmoid(zv[:, D_MODEL:])
        dps_ref[...] = (dmv * s1).astype(BF16)
        dpg_ref[...] = (dmv * s2).astype(BF16)
        dz_ref[:, :D_MODEL] = dmv * ps_ref[...] * s1 * (1.0 - s1)
        dz_ref[:, D_MODEL:] = dmv * pg_ref[...] * s2 * (1.0 - s2)
    return _rows(body, [(dm, 'r'), (zg, 'r'), (ps, 'r'), (pg, 'r')],
                 [(D_MODEL, BF16, 'r'), (D_MODEL, BF16, 'r'), (2 * D_MODEL, F32, 'r')], n=zg.shape[0], name=name)


def _final_loss(h, g, tgt, name):
    def body(h_ref, g_ref, t_ref, loss_ref, dh_ref, dg_ref):
        hv = h_ref[...]
        rstd = lax.rsqrt(jnp.mean(hv * hv, axis=-1, keepdims=True) + EPS)
        xh = hv * rstd
        err = xh * g_ref[...] - t_ref[...]
        part = 0.5 * jnp.sum(jnp.mean(err * err, axis=-1, keepdims=True), axis=0, keepdims=True)
        loss_ref[...] += jnp.broadcast_to(part, loss_ref.shape)
        dout = err * (1.0 / hv.shape[1])
        dg_ref[...] += jnp.sum(dout * xh, axis=0, keepdims=True)
        dxh = dout * g_ref[...]
        dh_ref[...] = rstd * (dxh - xh * jnp.mean(dxh * xh, axis=-1, keepdims=True))
    d = h.shape[1]
    return _rows(body, [(h, 'r'), (g, 'f'), (tgt, 'r')], [(LANE, F32, 'a'), (d, F32, 'r'), (d, F32, 'a')],
                 n=h.shape[0], name=name)


def _adamw_math(wv, gv, mv, vv):
    nm = ADAM_B1 * mv + (1.0 - ADAM_B1) * gv
    nv = ADAM_B2 * vv + (1.0 - ADAM_B2) * (gv * gv)
    m_hat = nm / (1.0 - ADAM_B1 ** ADAM_STEP)
    v_hat = nv / (1.0 - ADAM_B2 ** ADAM_STEP)
    return -ADAM_LR * (m_hat / (jnp.sqrt(v_hat) + ADAM_EPS) + ADAM_WD * wv), nm, nv


def _adamw(w, g, m, v, name):
    def body(w_ref, g_ref, m_ref, v_ref, d_ref, nm_ref, nv_ref):
        d_ref[...], nm_ref[...], nv_ref[...] = _adamw_math(w_ref[...], g_ref[...], m_ref[...], v_ref[...])
    c = w.shape[1]
    return _rows(body, [(w, 'r'), (g, 'r'), (m, 'r'), (v, 'r')], [(c, F32, 'r')] * 3, n=w.shape[0], name=name)


def _adamw_halves(w, g_own, g_sib, m, v, c_arr, name):
    r, cols = w.shape
    h = r // 2
    tr = _pick(h, 256, 8)
    per = h // tr

    def body(c_ref, w_ref, go_ref, gs_ref, m_ref, v_ref, g_ref, d_ref, nm_ref, nv_ref):
        mine = (pl.program_id(0) // per) == c_ref[0]
        gv = jnp.where(mine, go_ref[...], gs_ref[...])
        g_ref[...] = gv
        d_ref[...], nm_ref[...], nv_ref[...] = _adamw_math(w_ref[...], gv, m_ref[...], v_ref[...])

    full = pl.BlockSpec((tr, cols), lambda i, c_ref: (i, 0))
    half = pl.BlockSpec((tr, cols), lambda i, c_ref: (i % per, 0))
    grid_spec = pltpu.PrefetchScalarGridSpec(num_scalar_prefetch=1, grid=(2 * per,),
                                             in_specs=[full, half, half, full, full], out_specs=[full] * 4)
    return _pcall(body, name=name, grid_spec=grid_spec, out_shape=[jax.ShapeDtypeStruct((r, cols), F32)] * 4,
                  compiler_params=_params())(c_arr, w, g_own, g_sib, m, v)


def _shift_rows(v, sh, down):
    rolled = pltpu.roll(v, sh if down else v.shape[0] - sh, axis=0)
    row = lax.broadcasted_iota(jnp.int32, v.shape, 0)
    keep = (row >= sh) if down else (row < v.shape[0] - sh)
    return jnp.where(keep, rolled, 0.0)


def _chain_segments(st_r, st_i, pw_r_ref, pw_i_ref, conj, down):
    vr, vi = st_r[...], st_i[...]
    sh, k = 1, 0
    while sh < SEG:
        pr, pi = pw_r_ref[k:k + 1, :], pw_i_ref[k:k + 1, :]
        if conj:
            pi = -pi
        sr, si = _shift_rows(vr, sh, down), _shift_rows(vi, sh, down)
        vr, vi = vr + pr * sr - pi * si, vi + pr * si + pi * sr
        sh, k = sh * 2, k + 1
    st_r[...] = _shift_rows(vr, 1, down)
    st_i[...] = _shift_rows(vi, 1, down)


def _s5_scan(bu, ar8, ai8, pw_r, pw_i, name):
    n = bu.shape[0]
    rb = SCAN_ROWS
    nb, steps, lc = n // rb, rb // SEG, 512

    def body(bu_ref, ar_ref, ai_ref, pwr_ref, pwi_ref, x_ref, st_r, st_i):
        ph, b = pl.program_id(0), pl.program_id(1)

        @pl.when((ph == 0) & (b == 0))
        def _():
            st_r[...] = jnp.zeros_like(st_r)
            st_i[...] = jnp.zeros_like(st_i)

        def scan(store):
            for c in range(S5_GP // lc):
                re, im = slice(c * lc, (c + 1) * lc), slice(S5_GP + c * lc, S5_GP + (c + 1) * lc)
                a_r, a_i = ar_ref[:, re], ai_ref[:, re]

                def step(s, carry):
                    xr, xi = carry
                    rows = pl.ds(pl.multiple_of(s * SEG, SEG), SEG)
                    nr = a_r * xr - a_i * xi + bu_ref[rows, re]
                    ni = a_r * xi + a_i * xr + bu_ref[rows, im]
                    if store:
                        x_ref[rows, re] = nr
                        x_ref[rows, im] = ni
                    return nr, ni

                xr, xi = lax.fori_loop(0, steps, step, (st_r[:, re], st_i[:, re]), unroll=4)
                st_r[:, re] = xr
                st_i[:, re] = xi

        @pl.when(ph == 0)
        def _():
            scan(False)

        @pl.when((ph == 0) & (b == nb - 1))
        def _():
            _chain_segments(st_r, st_i, pwr_ref, pwi_ref, conj=False, down=True)

        @pl.when(ph == 1)
        def _():
            scan(True)

    full = lambda a: pl.BlockSpec(a.shape, lambda ph, b: (0, 0))
    return _pcall(body, name=name, grid=(2, nb),
                  in_specs=[pl.BlockSpec((rb, 2 * S5_GP), lambda ph, b: (b, 0)), full(ar8), full(ai8), full(pw_r), full(pw_i)],
                  out_specs=pl.BlockSpec((rb, 2 * S5_GP), lambda ph, b: (b * ph, 0)),
                  out_shape=jax.ShapeDtypeStruct((n, 2 * S5_GP), F32),
                  scratch_shapes=[pltpu.VMEM((SEG, S5_GP), F32), pltpu.VMEM((SEG, S5_GP), F32)],
                  compiler_params=_params())(bu, ar8, ai8, pw_r, pw_i)


def _s5_scan_bwd(gx, xs, ar8, ai8, pw_r, pw_i, name):
    n = gx.shape[0]
    rb = SCAN_ROWS
    nb, steps, lc = n // rb, rb // SEG, 256

    def body(gx_ref, x_ref, ar_ref, ai_ref, pwr_ref, pwi_ref, lam_ref, da_ref, st_r, st_i):
        ph, b = pl.program_id(0), pl.program_id(1)

        @pl.when((ph == 0) & (b == 0))
        def _():
            st_r[...] = jnp.zeros_like(st_r)
            st_i[...] = jnp.zeros_like(st_i)
            da_ref[...] = jnp.zeros_like(da_ref)

        def scan(store):
            for c in range(S5_GP // lc):
                re, im = slice(c * lc, (c + 1) * lc), slice(S5_GP + c * lc, S5_GP + (c + 1) * lc)
                a_r, a_i = ar_ref[:, re], ai_ref[:, re]

                def step(s, carry):
                    rows = pl.ds(pl.multiple_of((steps - 1 - s) * SEG, SEG), SEG)
                    if store:
                        lr, li, dr, di = carry
                        xr, xi = x_ref[rows, re], x_ref[rows, im]
                        dr = dr + lr * xr + li * xi
                        di = di + li * xr - lr * xi
                    else:
                        lr, li = carry
                    nr = a_r * lr + a_i * li + gx_ref[rows, re]
                    ni = a_r * li - a_i * lr + gx_ref[rows, im]
                    if store:
                        lam_ref[rows, re] = nr
                        lam_ref[rows, im] = ni
                        return nr, ni, dr, di
                    return nr, ni

                if store:
                    lr, li, dr, di = lax.fori_loop(0, steps, step, (st_r[:, re], st_i[:, re], da_ref[:, re], da_ref[:, im]),
                                                   unroll=4)
                    da_ref[:, re] = dr
                    da_ref[:, im] = di
                else:
                    lr, li = lax.fori_loop(0, steps, step, (st_r[:, re], st_i[:, re]), unroll=4)
                st_r[:, re] = lr
                st_i[:, re] = li

        @pl.when(ph == 0)
        def _():
            scan(False)

        @pl.when((ph == 0) & (b == nb - 1))
        def _():
            _chain_segments(st_r, st_i, pwr_ref, pwi_ref, conj=True, down=False)

        @pl.when(ph == 1)
        def _():
            scan(True)

    full = lambda a: pl.BlockSpec(a.shape, lambda ph, b: (0, 0))
    rev = lambda ph, b: (nb - 1 - b, 0)
    return _pcall(body, name=name, grid=(2, nb),
                  in_specs=[pl.BlockSpec((rb, 2 * S5_GP), rev), pl.BlockSpec((rb, 2 * S5_GP), lambda ph, b: ((nb - 1 - b) * ph, 0)),
                            full(ar8), full(ai8), full(pw_r), full(pw_i)],
                  out_specs=[pl.BlockSpec((rb, 2 * S5_GP), lambda ph, b: (nb - 1 - b * ph, 0)),
                             pl.BlockSpec((SEG, 2 * S5_GP), lambda ph, b: (0, 0))],
                  out_shape=[jax.ShapeDtypeStruct((n, 2 * S5_GP), F32), jax.ShapeDtypeStruct((SEG, 2 * S5_GP), F32)],
                  scratch_shapes=[pltpu.VMEM((SEG, S5_GP), F32), pltpu.VMEM((SEG, S5_GP), F32)],
                  compiler_params=_params())(gx, xs, ar8, ai8, pw_r, pw_i)


def _s5_discretize(lam_re, lam_im, log_dt, b_re, b_im):
    dt = jnp.exp(log_dt)[:, None]
    mag = jnp.exp(lam_re * dt)
    ar = mag * jnp.cos(lam_im * dt)
    ai = mag * jnp.sin(lam_im * dt)
    den = lam_re * lam_re + lam_im * lam_im
    nr = ar - 1.0
    fr = (nr * lam_re + ai * lam_im) / den
    fi = (ai * lam_re - nr * lam_im) / den
    bbar_re = fr[:, :, None] * b_re - fi[:, :, None] * b_im
    bbar_im = fr[:, :, None] * b_im + fi[:, :, None] * b_re
    return ar, ai, bbar_re, bbar_im


def _block_diag(t):
    g, a, b = t.shape
    eye = jnp.eye(g, dtype=t.dtype)
    return (t[:, :, None, :] * eye[:, None, :, None]).reshape(g * a, g * b)


def _diag_blocks(m, a, b):
    g = S5_G
    return jnp.einsum('gagb->gab', m.reshape(g, a, g, b))


def _permute_rows(t):
    n = t.shape[0]
    return t.reshape(SEG, n // SEG, t.shape[1]).transpose(1, 0, 2).reshape(n, t.shape[1])


def _unpermute_rows(t):
    n = t.shape[0]
    return t.reshape(n // SEG, SEG, t.shape[1]).transpose(1, 0, 2).reshape(n, t.shape[1])


def _segment_powers(ar, ai, seg_steps):
    pr, pi = ar.reshape(1, S5_GP), ai.reshape(1, S5_GP)
    e = 1
    while e < seg_steps:
        pr, pi = pr * pr - pi * pi, 2.0 * pr * pi
        e *= 2
    assert e == seg_steps, "segment length must be a power of two"
    rows_r, rows_i = [], []
    for _ in range(3):
        rows_r.append(pr)
        rows_i.append(pi)
        pr, pi = pr * pr - pi * pi, 2.0 * pr * pi
    pad = jnp.zeros((SEG - 3, S5_GP), F32)
    return jnp.concatenate(rows_r + [pad], axis=0), jnp.concatenate(rows_i + [pad], axis=0)


NT = (((1,), (1,)), ((), ()))
TN = (((0,), (0,)), ((), ()))


def _dot(a, b, dims=None, exact=False):
    dims = (((1,), (0,)), ((), ())) if dims is None else dims
    if exact:
        return lax.dot_general(a, b, dims, precision=HI, preferred_element_type=F32)
    return lax.dot_general(a.astype(BF16), b.astype(BF16), dims, preferred_element_type=F32)


def _gla_chunk_fwd(qc, kc, vc, al, wup, bup, s_prev, tril):
    z = _dot(al, wup) + bup
    la = (jnp.minimum(z, 0.0) - jnp.log(1.0 + jnp.exp(-jnp.abs(z)))) * (1.0 / GLA_TAU)
    bc = _dot(tril, la, exact=True)
    blb = _dot(la, jnp.ones((GLA_CHUNK, GLA_DV), F32), TN, exact=True)
    bl = bc[GLA_CHUNK - 1:GLA_CHUNK, :]
    ebc = jnp.exp(bc)
    qt = qc * (GLA_DK ** -0.5) * ebc
    kt = kc * jnp.exp(-bc)
    ke = kc * jnp.exp(bl - bc)
    sc = _dot(qt, kt, NT) * tril
    o = _dot(sc, vc) + _dot(qt, s_prev)
    return z, bc, bl, blb, ebc, qt, kt, ke, sc, o


def _gla_specs(n, arrs):
    specs = []
    for a in arrs:
        if a.ndim == 3:
            specs.append(pl.BlockSpec((None,) + a.shape[1:], lambda h: (h, 0, 0)))
        else:
            specs.append(pl.BlockSpec(a.shape, lambda h: (0, 0)))
    return specs


def _gla_fwd(q, k, v, r, al, wup, bup, gn, name):
    n = q.shape[1]
    nc = n // GLA_CHUNK

    def body(q_ref, k_ref, v_ref, r_ref, al_ref, wup_ref, bup_ref, gn_ref, y_ref, sp_ref):
        ri = lax.broadcasted_iota(jnp.int32, (GLA_CHUNK, GLA_CHUNK), 0)
        ci = lax.broadcasted_iota(jnp.int32, (GLA_CHUNK, GLA_CHUNK), 1)
        tril = (ri >= ci).astype(F32)

        def chunk(c, s_prev):
            rows = pl.ds(pl.multiple_of(c * GLA_CHUNK, GLA_CHUNK), GLA_CHUNK)
            vc, rc = v_ref[rows, :], r_ref[rows, :]
            _, _, _, blb, _, _, _, ke, _, o = _gla_chunk_fwd(q_ref[rows, :], k_ref[rows, :], vc, al_ref[rows, :],
                                                              wup_ref[...], bup_ref[...], s_prev, tril)
            sp_ref[c] = s_prev
            rstd = lax.rsqrt(jnp.mean(o * o, axis=-1, keepdims=True) + EPS)
            y_ref[rows, :] = o * rstd * gn_ref[...] * (rc * _sigmoid(rc))
            return jnp.exp(blb) * s_prev + _dot(ke, vc, TN)

        lax.fori_loop(0, nc, chunk, jnp.zeros((GLA_DK, GLA_DV), F32))

    ins = [q, k, v, r, al, wup, bup, gn]
    return _pcall(body, name=name, grid=(GLA_HEADS,), in_specs=_gla_specs(n, ins),
                  out_specs=[pl.BlockSpec((None, n, GLA_DV), lambda h: (h, 0, 0)),
                             pl.BlockSpec((None, nc, GLA_DK, GLA_DV), lambda h: (h, 0, 0, 0))],
                  out_shape=[jax.ShapeDtypeStruct((GLA_HEADS, n, GLA_DV), F32),
                             jax.ShapeDtypeStruct((GLA_HEADS, nc, GLA_DK, GLA_DV), F32)],
                  compiler_params=_params())(*ins)


def _gla_bwd(q, k, v, r, al, wup, bup, gn, sp, dy, name):
    n = q.shape[1]
    nc = n // GLA_CHUNK

    def body(q_ref, k_ref, v_ref, r_ref, al_ref, wup_ref, bup_ref, gn_ref, sp_ref, dy_ref,
             dq_ref, dk_ref, dv_ref, dr_ref, dz_ref, dgn_ref, dbup_ref):
        ri = lax.broadcasted_iota(jnp.int32, (GLA_CHUNK, GLA_CHUNK), 0)
        ci = lax.broadcasted_iota(jnp.int32, (GLA_CHUNK, GLA_CHUNK), 1)
        tril = (ri >= ci).astype(F32)
        triu = (ri <= ci).astype(F32)

        def chunk(i, carry):
            ds, dgn, dbup = carry
            c = nc - 1 - i
            rows = pl.ds(pl.multiple_of(c * GLA_CHUNK, GLA_CHUNK), GLA_CHUNK)
            qc, kc, vc, rc = q_ref[rows, :], k_ref[rows, :], v_ref[rows, :], r_ref[rows, :]
            s_prev = sp_ref[c]
            z, bc, bl, blb, ebc, qt, kt, ke, sc, o = _gla_chunk_fwd(qc, kc, vc, al_ref[rows, :], wup_ref[...],
                                                                     bup_ref[...], s_prev, tril)
            rstd = lax.rsqrt(jnp.mean(o * o, axis=-1, keepdims=True) + EPS)
            on = o * rstd
            sr = _sigmoid(rc)
            sil = rc * sr
            dyv, gnv = dy_ref[rows, :], gn_ref[...]
            dgn = dgn + jnp.sum(dyv * on * sil, axis=0, keepdims=True)
            dr_ref[rows, :] = dyv * on * gnv * (sr * (1.0 + rc * (1.0 - sr)))
            don = dyv * gnv * sil
            do = rstd * (don - on * jnp.mean(don * on, axis=-1, keepdims=True))
            dp = _dot(do, vc, NT) * tril
            dv_ref[rows, :] = _dot(sc, do, TN) + _dot(ke, ds)
            dqt = _dot(dp, kt) + _dot(do, s_prev, NT)
            dkt = _dot(dp, qt, TN)
            dke = _dot(vc, ds, NT)
            ddec = _dot(jnp.ones((8, GLA_DV), F32), ds * s_prev, NT, exact=True)[0:1, :]
            ds_new = jnp.exp(blb) * ds + _dot(qt, do, TN)
            dq_ref[rows, :] = dqt * (GLA_DK ** -0.5) * ebc
            dk_ref[rows, :] = dkt * jnp.exp(-bc) + dke * jnp.exp(bl - bc)
            dbc = dqt * qt - dkt * kt - dke * ke
            dbl = jnp.sum(dke * ke, axis=0, keepdims=True) + ddec * jnp.exp(bl)
            dla = _dot(triu, dbc, exact=True) + dbl
            dz = dla * (1.0 - _sigmoid(z)) * (1.0 / GLA_TAU)
            dz_ref[rows, :] = dz
            dbup = dbup + jnp.sum(dz, axis=0, keepdims=True)
            return ds_new, dgn, dbup

        _, dgn, dbup = lax.fori_loop(0, nc, chunk, (jnp.zeros((GLA_DK, GLA_DV), F32), jnp.zeros((1, GLA_DV), F32),
                                                   jnp.zeros((1, GLA_DK), F32)))
        dgn_ref[...] = dgn
        dbup_ref[...] = dbup

    ins = [q, k, v, r, al, wup, bup, gn, sp, dy]
    in_specs = _gla_specs(n, ins[:8]) + [pl.BlockSpec((None, nc, GLA_DK, GLA_DV), lambda h: (h, 0, 0, 0)),
                                         pl.BlockSpec((None, n, GLA_DV), lambda h: (h, 0, 0))]
    hs = lambda w: pl.BlockSpec((None, n, w), lambda h: (h, 0, 0))
    h1 = lambda w: pl.BlockSpec((None, 1, w), lambda h: (h, 0, 0))
    sh = lambda w: jax.ShapeDtypeStruct((GLA_HEADS, n, w), F32)
    s1 = lambda w: jax.ShapeDtypeStruct((GLA_HEADS, 1, w), F32)
    return _pcall(body, name=name, grid=(GLA_HEADS,), in_specs=in_specs,
                  out_specs=[hs(GLA_DK), hs(GLA_DK), hs(GLA_DV), hs(GLA_DV), hs(GLA_DK), h1(GLA_DV), h1(GLA_DK)],
                  out_shape=[sh(GLA_DK), sh(GLA_DK), sh(GLA_DV), sh(GLA_DV), sh(GLA_DK), s1(GLA_DV), s1(GLA_DK)],
                  compiler_params=_params())(*ins)


def _heads(t, w):
    return t.reshape(t.shape[0], GLA_HEADS, w).transpose(1, 0, 2)


def _unheads(t):
    return t.transpose(1, 0, 2).reshape(t.shape[1], GLA_HEADS * t.shape[2])


ANY = pl.BlockSpec(memory_space=pl.ANY)


def _place():
    x, y, c = lax.axis_index("x"), lax.axis_index("y"), lax.axis_index("c")
    chips = [(1 - x, y), (x, 1 - y), (1 - x, 1 - y)]
    return x, y, c, chips


def _remote(src, dst, ssem, rsem, dev):
    return pltpu.make_async_remote_copy(src_ref=src, dst_ref=dst, send_sem=ssem, recv_sem=rsem, device_id=dev,
                                        device_id_type=MESH_ID)


def _half(c, rows):
    h = rows // 2
    return pl.ds(pl.multiple_of(c * h, 8), h)


def _gather_shards(shards, name):
    nw = len(shards)

    def body(*refs):
        ins, outs = refs[:nw], refs[nw:2 * nw]
        ssem, rsem, osem_s, osem_r = refs[2 * nw:]
        x, y, c, chips = _place()
        mine = 2 * x + y
        sibling = (x, y, 1 - c)
        own, sends = [], []
        for w in range(nw):
            half = _half(c, ins[w].shape[0])
            cp = _remote(ins[w], outs[w].at[mine], osem_s.at[w], osem_r.at[w], sibling)
            cp.start()
            own.append(cp)
            for k, (px, py) in enumerate(chips):
                cp = _remote(ins[w].at[half], outs[w].at[mine, half], ssem.at[6 * w + k], rsem.at[6 * w + k], (px, py, c))
                cp.start()
                sends.append(cp)
        for w in range(nw):
            half = _half(c, ins[w].shape[0])
            for k, (px, py) in enumerate(chips):
                theirs = outs[w].at[2 * px + py, half]
                _remote(theirs, theirs, ssem.at[6 * w + k], rsem.at[6 * w + k], (px, py, c)).wait_recv()
                cp = _remote(theirs, theirs, ssem.at[6 * w + 3 + k], rsem.at[6 * w + 3 + k], sibling)
                cp.start()
                sends.append(cp)
        for w in range(nw):
            other = _half(1 - c, ins[w].shape[0])
            for k, (px, py) in enumerate(chips):
                theirs = outs[w].at[2 * px + py, other]
                _remote(theirs, theirs, ssem.at[6 * w + 3 + k], rsem.at[6 * w + 3 + k], sibling).wait_recv()
        for cp in sends:
            cp.wait_send()
        for cp in own:
            cp.wait()

    return _pcall(body, name=name, in_specs=[ANY] * nw, out_specs=[ANY] * nw,
                  out_shape=[jax.ShapeDtypeStruct((4,) + s.shape, s.dtype) for s in shards],
                  scratch_shapes=[pltpu.SemaphoreType.DMA((6 * nw,)), pltpu.SemaphoreType.DMA((6 * nw,)),
                                  pltpu.SemaphoreType.DMA((nw,)), pltpu.SemaphoreType.DMA((nw,))],
                  compiler_params=pltpu.CompilerParams(has_side_effects=True))(*shards)


def _swap_halves(grads, name):
    nw = len(grads)

    def body(*refs):
        ins, outs = refs[:nw], refs[nw:2 * nw]
        ssem, rsem = refs[2 * nw:]
        x, y, c, _ = _place()
        cps = []
        for w in range(nw):
            other = _half(1 - c, ins[w].shape[1])
            cp = _remote(ins[w].at[:, other, :], outs[w], ssem.at[w], rsem.at[w], (x, y, 1 - c))
            cp.start()
            cps.append(cp)
        for cp in cps:
            cp.wait()

    return _pcall(body, name=name, in_specs=[ANY] * nw, out_specs=[ANY] * nw,
                  out_shape=[jax.ShapeDtypeStruct((4, g.shape[1] // 2, g.shape[2]), g.dtype) for g in grads],
                  scratch_shapes=[pltpu.SemaphoreType.DMA((nw,)), pltpu.SemaphoreType.DMA((nw,))],
                  compiler_params=pltpu.CompilerParams(has_side_effects=True))(*grads)


def _scatter_chip_sums(sums, name):
    nw = len(sums)

    def body(*refs):
        ins, outs = refs[:nw], refs[nw:2 * nw]
        ssem, rsem = refs[2 * nw:]
        x, y, c, chips = _place()
        cps = []
        for w in range(nw):
            for k, (px, py) in enumerate(chips):
                cp = _remote(ins[w].at[2 * px + py], outs[w].at[k], ssem.at[3 * w + k], rsem.at[3 * w + k], (px, py, c))
                cp.start()
                cps.append(cp)
        for cp in cps:
            cp.wait()

    return _pcall(body, name=name, in_specs=[ANY] * nw, out_specs=[ANY] * nw,
                  out_shape=[jax.ShapeDtypeStruct((3,) + s.shape[1:], s.dtype) for s in sums],
                  scratch_shapes=[pltpu.SemaphoreType.DMA((3 * nw,)), pltpu.SemaphoreType.DMA((3 * nw,))],
                  compiler_params=pltpu.CompilerParams(has_side_effects=True))(*sums)


def _swap_reduced(halves, name):
    nw = len(halves)

    def body(*refs):
        ins, outs = refs[:nw], refs[nw:2 * nw]
        ssem, rsem = refs[2 * nw:]
        x, y, c, _ = _place()
        cps = []
        for w in range(nw):
            cp = _remote(ins[w], outs[w], ssem.at[w], rsem.at[w], (x, y, 1 - c))
            cp.start()
            cps.append(cp)
        for cp in cps:
            cp.wait()

    return _pcall(body, name=name, in_specs=[ANY] * nw, out_specs=[ANY] * nw,
                  out_shape=[jax.ShapeDtypeStruct(h.shape, h.dtype) for h in halves],
                  scratch_shapes=[pltpu.SemaphoreType.DMA((nw,)), pltpu.SemaphoreType.DMA((nw,))],
                  compiler_params=pltpu.CompilerParams(has_side_effects=True))(*halves)


def _chip_sum(g, recv, c_arr, name):
    _, r, cols = g.shape
    h = r // 2
    tr = _pick(h, 256, 16)
    g4 = g.reshape(4, 2, h, cols)

    def body(c_ref, g_ref, r_ref, o_ref):
        o_ref[...] = (g_ref[...] + r_ref[...]).astype(BF16)

    grid_spec = pltpu.PrefetchScalarGridSpec(
        num_scalar_prefetch=1, grid=(4, h // tr),
        in_specs=[pl.BlockSpec((None, None, tr, cols), lambda s, i, c_ref: (s, c_ref[0], i, 0)),
                  pl.BlockSpec((None, tr, cols), lambda s, i, c_ref: (s, i, 0))],
        out_specs=pl.BlockSpec((None, tr, cols), lambda s, i, c_ref: (s, i, 0)))
    return _pcall(body, name=name, grid_spec=grid_spec, out_shape=jax.ShapeDtypeStruct((4, h, cols), BF16),
                  compiler_params=_params())(c_arr, g4, recv)


def _owner_sum(sums, others, s_arr, name):
    _, h, cols = sums.shape
    tr = _pick(h, 256, 16)

    def body(s_ref, a_ref, o_ref, out_ref):
        f = lambda v: v.astype(F32)
        out_ref[...] = (f(a_ref[...]) + f(o_ref[0])) + (f(o_ref[1]) + f(o_ref[2]))

    grid_spec = pltpu.PrefetchScalarGridSpec(
        num_scalar_prefetch=1, grid=(h // tr,),
        in_specs=[pl.BlockSpec((None, tr, cols), lambda i, s_ref: (s_ref[0], i, 0)),
                  pl.BlockSpec((3, tr, cols), lambda i, s_ref: (0, i, 0))],
        out_specs=pl.BlockSpec((tr, cols), lambda i, s_ref: (i, 0)))
    return _pcall(body, name=name, grid_spec=grid_spec, out_shape=jax.ShapeDtypeStruct((h, cols), F32),
                  compiler_params=_params())(s_arr, sums, others)


def _allreduce_small(v, name):
    def body(v_ref, o_ref, r0, r1, ssem, rsem):
        x, y, c, chips = _place()
        cp = _remote(v_ref, r0, ssem.at[0], rsem.at[0], (x, y, 1 - c))
        cp.start()
        cp.wait()
        o_ref[...] = v_ref[...] + r0[...]
        cps = []
        for k, (px, py) in enumerate(chips):
            cp = _remote(o_ref, r1.at[k], ssem.at[1 + k], rsem.at[1 + k], (px, py, c))
            cp.start()
            cps.append(cp)
        for cp in cps:
            cp.wait()
        o_ref[...] = (o_ref[...] + r1[0]) + (r1[1] + r1[2])

    vm = pl.BlockSpec(memory_space=pltpu.VMEM)
    return _pcall(body, name=name, in_specs=[vm], out_specs=vm, out_shape=jax.ShapeDtypeStruct(v.shape, F32),
                  scratch_shapes=[pltpu.VMEM(v.shape, F32), pltpu.VMEM((3,) + v.shape, F32),
                                  pltpu.SemaphoreType.DMA((4,)), pltpu.SemaphoreType.DMA((4,))],
                  compiler_params=_params(has_side_effects=True))(v)


def _pack_small(parts):
    flat = jnp.concatenate([p.reshape(-1).astype(F32) for p in parts])
    pad = (-flat.shape[0]) % (64 * LANE)
    return jnp.pad(flat, (0, pad)).reshape(-1, LANE)


def _unpack_small(packed, like):
    flat, out, pos = packed.reshape(-1), [], 0
    for p in like:
        out.append(flat[pos:pos + p.size].reshape(p.shape))
        pos += p.size
    return out


def _ffn_fwd(h, g, w1, w3, w2, tag):
    n = h.shape[0]
    flat = lambda t: t.reshape(4 * n, t.shape[2])
    n1 = _rms_fwd(h, g, f"{tag}_rms")
    a = _mm(n1, w1, shard='n', name=f"{tag}_a")
    b = _mm(n1, w3, shard='n', name=f"{tag}_b")
    hm = _swiglu_fwd(flat(a), flat(b), f"{tag}_act").reshape(a.shape)
    out = _mm(hm, w2, shard='k', alpha=0.5, res=h, name=f"{tag}_out")
    return out, (h, n1, a, b, hm)


def _ffn_bwd(dout, saved, g, w1, w3, w2, tag):
    h, n1, a, b, hm = saved
    n = h.shape[0]
    flat = lambda t: t.reshape(4 * n, t.shape[2])
    dhm = _mm(dout, w2, tb=True, shard='n', alpha=0.5, name=f"{tag}_dhm")
    gw2 = _mm(hm, dout, ta=True, shard='m', alpha=0.5, name=f"{tag}_gw2")
    da, db = _swiglu_bwd(flat(dhm), flat(a), flat(b), f"{tag}_dact")
    da, db = da.reshape(a.shape), db.reshape(a.shape)
    gw1 = _mm(n1, da, ta=True, shard='n', name=f"{tag}_gw1")
    gw3 = _mm(n1, db, ta=True, shard='n', name=f"{tag}_gw3")
    dn = _mm(da, w1, tb=True, shard='k', name=f"{tag}_dn1")
    dn = _mm(db, w3, tb=True, shard='k', res=dn, name=f"{tag}_dn2")
    dh, dg = _rms_bwd(h, g, dn, dout, f"{tag}_drms")
    return dh, dg, gw1, gw3, gw2


def _local_step(x, tgt, p):
    n = x.shape[0]
    f = lambda name: p[name].reshape(1, D_MODEL) if name.endswith('_norm') and name != 'gla_out_norm' else p[name]
    h1, ffn1 = _ffn_fwd(x, f('ffn1_norm'), f('ffn1_w1'), f('ffn1_w3'), f('ffn1_w2'), "ffn1")
    u = _rms_fwd(h1, f('mix_norm'), "mix_rms")
    w_in = f('w_in')
    w_a = w_in[:, :2048]
    w_al = jnp.pad(w_in[:, 2048:2048 + GLA_RANK], ((0, 0), (0, LANE - GLA_RANK)))
    w_g = w_in[:, 2048 + GLA_RANK:]
    za = _mm(u, w_a, name="in_a")
    zg = _mm(u, w_g, name="in_g")
    al = _mm(u, w_al, name="in_al")
    ar, ai, bbar_re, bbar_im = _s5_discretize(f('s5_lambda_re'), f('s5_lambda_im'), f('s5_log_dt'), f('s5_b_re'), f('s5_b_im'))
    b_blk = jnp.concatenate([_block_diag(bbar_re.transpose(0, 2, 1)), _block_diag(bbar_im.transpose(0, 2, 1))], axis=1)
    c_blk = jnp.concatenate([_block_diag(f('s5_c_re').transpose(0, 2, 1)), -_block_diag(f('s5_c_im').transpose(0, 2, 1))], axis=0)
    b_blk, c_blk = b_blk.astype(BF16), c_blk.astype(BF16)
    ar8 = jnp.broadcast_to(ar.reshape(1, S5_GP), (SEG, S5_GP))
    ai8 = jnp.broadcast_to(ai.reshape(1, S5_GP), (SEG, S5_GP))
    pw_r, pw_i = _segment_powers(ar, ai, n // SEG)
    dskip = f('s5_d').reshape(1, S5_W)
    u_s5 = _permute_rows(za[:, :S5_W])
    bu = _mm(u_s5, b_blk, name="s5_bu")
    xs = _s5_scan(bu, ar8, ai8, pw_r, pw_i, "s5_scan")
    ys_p = _mm(xs, c_blk, res=_scale_rows(u_s5, dskip, "s5_skip"), name="s5_y")
    ys = _unpermute_rows(ys_p)
    zgelu = _gelu_fwd(ys, "s5_gelu")
    t_glu = _mm(zgelu, f('s5_glu_w'), bias=f('s5_glu_b').reshape(1, S5_W), name="s5_glu_t")
    y_s5 = _glu_fwd(zgelu, t_glu, "s5_glu")
    q, k = _heads(za[:, 512:768], GLA_DK), _heads(za[:, 768:1024], GLA_DK)
    v, r = _heads(za[:, 1024:1536], GLA_DV), _heads(za[:, 1536:2048], GLA_DV)
    wup = jnp.pad(f('gla_a_up_w'), ((0, LANE - GLA_RANK), (0, 0)))
    wup_h = wup.reshape(LANE, GLA_HEADS, GLA_DK).transpose(1, 0, 2)
    bup_h = f('gla_a_up_b').reshape(GLA_HEADS, 1, GLA_DK)
    gn_h = f('gla_out_norm').reshape(GLA_HEADS, 1, GLA_DV)
    y_gla_h, s_prev = _gla_fwd(q, k, v, r, al, wup_h, bup_h, gn_h, "gla_fwd")
    y_gla = _unheads(y_gla_h).astype(BF16)
    ps = _mm(y_s5, f('proj_s5'), name="proj_s5")
    pg = _mm(y_gla, f('proj_gla'), name="proj_gla")
    merged = _merge_fwd(zg, ps, pg, "merge")
    h2 = _mm(merged, f('w_out'), res=h1, name="w_out")
    h3, ffn2 = _ffn_fwd(h2, f('ffn2_norm'), f('ffn2_w1'), f('ffn2_w3'), f('ffn2_w2'), "ffn2")
    loss, dh3, g_final = _final_loss(h3, f('final_norm').reshape(1, D_MODEL), tgt, "loss")
    grads = {'final_norm': g_final.reshape(D_MODEL)}
    dh2, grads['ffn2_norm'], grads['ffn2_w1'], grads['ffn2_w3'], grads['ffn2_w2'] = _ffn_bwd(
        dh3, ffn2, f('ffn2_norm'), f('ffn2_w1'), f('ffn2_w3'), f('ffn2_w2'), "ffn2")
    dh2b = dh2.astype(BF16)
    dm = _mm(dh2b, f('w_out'), tb=True, name="d_merged")
    grads['w_out'] = _mm(merged, dh2b, ta=True, name="g_w_out")
    dps, dpg, dzg = _merge_bwd(dm, zg, ps, pg, "d_merge")
    grads['proj_s5'] = _mm(y_s5, dps, ta=True, name="g_proj_s5")
    grads['proj_gla'] = _mm(y_gla, dpg, ta=True, name="g_proj_gla")
    dy_s5 = _mm(dps, f('proj_s5'), tb=True, name="d_y_s5")
    dy_gla = _mm(dpg, f('proj_gla'), tb=True, name="d_y_gla")
    dzgelu, dt_glu, g_glu_b = _glu_bwd1(dy_s5, zgelu, t_glu, "d_glu")
    grads['s5_glu_b'] = g_glu_b.reshape(S5_W)
    grads['s5_glu_w'] = _mm(zgelu, dt_glu, ta=True, name="g_glu_w")
    dzgelu = _mm(dt_glu, f('s5_glu_w'), tb=True, res=dzgelu, name="d_gelu")
    dys, du_skip, g_d = _glu_bwd2(_permute_rows(dzgelu), ys_p, u_s5, dskip, "d_s5_y")
    grads['s5_d'] = g_d.reshape(S5_G, S5_H)
    gx = _mm(dys, c_blk, tb=True, name="s5_gx")
    lam, da8 = _s5_scan_bwd(gx, xs, ar8, ai8, pw_r, pw_i, "s5_scan_bwd")
    g_c = _mm(dys, xs, ta=True, name="g_s5_c")
    grads['s5_c_re'] = _diag_blocks(g_c[:, :S5_GP], S5_H, S5_P)
    grads['s5_c_im'] = -_diag_blocks(g_c[:, S5_GP:], S5_H, S5_P)
    g_b = _mm(lam, u_s5, ta=True, name="g_s5_b")
    g_bbar_re = _diag_blocks(g_b[:S5_GP], S5_P, S5_H)
    g_bbar_im = _diag_blocks(g_b[S5_GP:], S5_P, S5_H)
    da = jnp.sum(da8, axis=0)
    g_ar, g_ai = da[:S5_GP].reshape(S5_G, S5_P), da[S5_GP:].reshape(S5_G, S5_P)
    _, disc_vjp = jax.vjp(_s5_discretize, f('s5_lambda_re'), f('s5_lambda_im'), f('s5_log_dt'), f('s5_b_re'), f('s5_b_im'))
    (grads['s5_lambda_re'], grads['s5_lambda_im'], grads['s5_log_dt'], grads['s5_b_re'],
     grads['s5_b_im']) = disc_vjp((g_ar, g_ai, g_bbar_re, g_bbar_im))
    du_s5 = _unpermute_rows(_mm(lam, b_blk, tb=True, res=du_skip, name="d_s5_u"))
    dq, dk, dv, dr, dz, dgn, dbup = _gla_bwd(q, k, v, r, al, wup_h, bup_h, gn_h, s_prev, _heads(dy_gla, GLA_DV), "gla_bwd")
    grads['gla_out_norm'] = dgn.reshape(GLA_HEADS * GLA_DV)
    grads['gla_a_up_b'] = dbup.reshape(GLA_HEADS * GLA_DK)
    dz = _unheads(dz)
    grads['gla_a_up_w'] = _mm(al, dz, ta=True, name="g_a_up")[:GLA_RANK]
    dal = _mm(dz, wup, tb=True, name="d_a_low")
    dza = jnp.concatenate([du_s5, _unheads(dq), _unheads(dk), _unheads(dv), _unheads(dr)], axis=1)
    g_wa = _mm(u, dza, ta=True, name="g_in_a")
    g_wg = _mm(u, dzg, ta=True, name="g_in_g")
    g_wal = _mm(u, dal, ta=True, name="g_in_al")
    grads['w_in'] = jnp.concatenate([g_wa, g_wal[:, :GLA_RANK], g_wg], axis=1)
    du = _mm(dza, w_a, tb=True, name="d_u_a")
    du = _mm(dzg, w_g, tb=True, res=du, name="d_u_g")
    du = _mm(dal, w_al, tb=True, res=du, name="d_u_al")
    dh1, g_mix = _rms_bwd(h1, f('mix_norm'), du, dh2, "d_mix_rms")
    grads['mix_norm'] = g_mix
    dx, grads['ffn1_norm'], grads['ffn1_w1'], grads['ffn1_w3'], grads['ffn1_w2'] = _ffn_bwd(
        dh1, ffn1, f('ffn1_norm'), f('ffn1_w1'), f('ffn1_w3'), f('ffn1_w2'), "ffn1")
    return loss[0, 0], dx, grads


def _train_step(a):
    x = a['x'][0]
    tgt = a['loss_target'][0]
    xi, yi, ci = lax.axis_index("x"), lax.axis_index("y"), lax.axis_index("c")
    c_arr = jnp.reshape(ci, (1,)).astype(jnp.int32)
    s_arr = jnp.reshape(2 * xi + yi, (1,)).astype(jnp.int32)
    gathered_names = SHARDED + ['gla_a_up_w']
    shards = [a[nm][0].astype(F32 if nm == 'gla_a_up_w' else BF16) for nm in gathered_names]
    gathered = dict(zip(gathered_names, _gather_shards(shards, "gather_weights")))
    p = {}
    for nm in gathered_names:
        g4 = gathered[nm]
        if nm in FFN_WEIGHTS:
            p[nm] = g4
        elif nm in COL_SHARDED:
            p[nm] = jnp.concatenate([g4[s] for s in range(4)], axis=1)
        else:
            p[nm] = g4.reshape(4 * g4.shape[1], g4.shape[2])
    for nm in SMALL:
        if nm != 'gla_a_up_w':
            p[nm] = a[nm] if nm == 'final_norm' else a[nm][0]
    loss, dx, grads = _local_step(x, tgt, p)
    loss = lax.psum(loss, ("x", "y", "c"))
    g4s = []
    for nm in SHARDED:
        g = grads[nm]
        if nm in FFN_WEIGHTS:
            g4s.append(g)
        elif nm in COL_SHARDED:
            g4s.append(jnp.stack(jnp.split(g, 4, axis=1)))
        else:
            g4s.append(g.reshape(4, g.shape[0] // 4, g.shape[1]))
    from_sibling = _swap_halves(g4s, "grad_swap_halves")
    chip_sums = [_chip_sum(g, r, c_arr, f"chip_sum_{nm}") for nm, g, r in zip(SHARDED, g4s, from_sibling)]
    from_chips = _scatter_chip_sums(chip_sums, "grad_scatter")
    halves = [_owner_sum(s, o, s_arr, f"owner_sum_{nm}") for nm, s, o in zip(SHARDED, chip_sums, from_chips)]
    sib_halves = _swap_reduced(halves, "grad_swap_reduced")
    red = {}
    small_parts = [grads[nm].reshape(a[nm].shape) for nm in SMALL if nm != 'gla_a_up_w'] + [grads['gla_a_up_w']]
    small_sum = _unpack_small(_allreduce_small(_pack_small(small_parts), "allreduce_small"), small_parts)
    small_names = [nm for nm in SMALL if nm != 'gla_a_up_w']
    for nm, g in zip(small_names, small_sum[:-1]):
        red[nm] = g
    g_up = small_sum[-1]
    red['gla_a_up_w'] = lax.dynamic_slice(g_up, (0, (2 * xi + yi) * GLA_DK), (GLA_RANK, GLA_DK))
    out_g, out_d, out_m, out_v = {}, {}, {}, {}
    for nm, own, sib in zip(SHARDED, halves, sib_halves):
        g, d, nm_, nv_ = _adamw_halves(a[nm][0], own, sib, a['m_' + nm][0], a['v_' + nm][0], c_arr, f"adamw_{nm}")
        shape = a[nm].shape
        out_g[nm], out_d[nm], out_m[nm], out_v[nm] = (t.reshape(shape) for t in (g, d, nm_, nv_))
    rest = [nm for nm in WEIGHTS if nm not in SHARDED]
    pk = lambda pre: _pack_small([a[pre + nm] for nm in rest])
    d, nm_, nv_ = _adamw(pk(''), _pack_small([red[nm] for nm in rest]), pk('m_'), pk('v_'), "adamw_small")
    like = [a[nm] for nm in rest]
    for nm, g, dd, mm_, vv_ in zip(rest, [red[nm].reshape(a[nm].shape) for nm in rest], _unpack_small(d, like),
                                   _unpack_small(nm_, like), _unpack_small(nv_, like)):
        out_g[nm], out_d[nm], out_m[nm], out_v[nm] = g, dd, mm_, vv_
    return (loss, dx[None], *[out_g[nm] for nm in WEIGHTS], *[out_d[nm] for nm in WEIGHTS],
            *[out_m[nm] for nm in WEIGHTS], *[out_v[nm] for nm in WEIGHTS])


def kernel(x, ffn1_norm, ffn1_w1, ffn1_w3, ffn1_w2, mix_norm, w_in, s5_lambda_re, s5_lambda_im, s5_log_dt, s5_b_re, s5_b_im, s5_c_re, s5_c_im, s5_d, s5_glu_w, s5_glu_b, gla_a_up_w, gla_a_up_b, gla_out_norm, proj_s5, proj_gla, w_out, ffn2_norm, ffn2_w1, ffn2_w3, ffn2_w2, final_norm, loss_target, m_ffn1_norm, m_ffn1_w1, m_ffn1_w3, m_ffn1_w2, m_mix_norm, m_w_in, m_s5_lambda_re, m_s5_lambda_im, m_s5_log_dt, m_s5_b_re, m_s5_b_im, m_s5_c_re, m_s5_c_im, m_s5_d, m_s5_glu_w, m_s5_glu_b, m_gla_a_up_w, m_gla_a_up_b, m_gla_out_norm, m_proj_s5, m_proj_gla, m_w_out, m_ffn2_norm, m_ffn2_w1, m_ffn2_w3, m_ffn2_w2, m_final_norm, v_ffn1_norm, v_ffn1_w1, v_ffn1_w3, v_ffn1_w2, v_mix_norm, v_w_in, v_s5_lambda_re, v_s5_lambda_im, v_s5_log_dt, v_s5_b_re, v_s5_b_im, v_s5_c_re, v_s5_c_im, v_s5_d, v_s5_glu_w, v_s5_glu_b, v_gla_a_up_w, v_gla_a_up_b, v_gla_out_norm, v_proj_s5, v_proj_gla, v_w_out, v_ffn2_norm, v_ffn2_w1, v_ffn2_w3, v_ffn2_w2, v_final_norm):
    return _train_step(dict(locals()))
```

```python
import functools

import jax
import jax.numpy as jnp
from jax import lax
from jax.experimental import pallas as pl
from jax.experimental.pallas import tpu as pltpu

F32 = jnp.float32
BF16 = jnp.bfloat16
HI = lax.Precision.HIGHEST
MESH_ID = pl.DeviceIdType.MESH

D_MODEL = 1024
EPS = 1e-6
S5_G, S5_P, S5_H = 32, 64, 16
S5_W = S5_G * S5_H
S5_GP = S5_G * S5_P
SEG = 8
SCAN_ROWS = 256
GLA_HEADS, GLA_DK, GLA_DV = 4, 64, 128
GLA_CHUNK = 64
GLA_TAU = 16.0
GLA_RANK = 16
ADAM_LR, ADAM_B1, ADAM_B2, ADAM_EPS, ADAM_WD, ADAM_STEP = 0.001, 0.9, 0.999, 1e-08, 0.01, 10
V7X_VMEM_LIMIT = 56 * 1024 * 1024
LANE = 128

WEIGHTS = ['ffn1_norm', 'ffn1_w1', 'ffn1_w3', 'ffn1_w2', 'mix_norm', 'w_in', 's5_lambda_re', 's5_lambda_im',
           's5_log_dt', 's5_b_re', 's5_b_im', 's5_c_re', 's5_c_im', 's5_d', 's5_glu_w', 's5_glu_b', 'gla_a_up_w',
           'gla_a_up_b', 'gla_out_norm', 'proj_s5', 'proj_gla', 'w_out', 'ffn2_norm', 'ffn2_w1', 'ffn2_w3',
           'ffn2_w2', 'final_norm']
SHARDED = ['ffn1_w1', 'ffn1_w3', 'ffn1_w2', 'w_in', 's5_glu_w', 'proj_s5', 'proj_gla', 'w_out',
           'ffn2_w1', 'ffn2_w3', 'ffn2_w2']
COL_SHARDED = ['ffn1_w1', 'ffn1_w3', 'w_in', 'proj_s5', 'proj_gla', 'ffn2_w1', 'ffn2_w3', 'gla_a_up_w']
SMALL = [n for n in WEIGHTS if n not in SHARDED]
FFN_WEIGHTS = ['ffn1_w1', 'ffn1_w3', 'ffn1_w2', 'ffn2_w1', 'ffn2_w3', 'ffn2_w2']


def _pcall(body, **kw):
    return pl.pallas_call(body, **kw)


def _params(**kw):
    return pltpu.CompilerParams(vmem_limit_bytes=V7X_VMEM_LIMIT, **kw)


def _pick(n, cap, quantum):
    if n <= cap:
        return n
    best = None
    for t in range(quantum, cap + 1, quantum):
        if n % t == 0:
            best = t
    assert best is not None, (n, cap, quantum)
    return best


def _sigmoid(x):
    return jax.nn.sigmoid(x)


def _mm(a, b, *, name, ta=False, tb=False, out_dtype=F32, alpha=1.0, res=None, bias=None, exact=False, shard=None):
    ns = 4
    (k_a, m) = a.shape[-2:] if ta else a.shape[-2:][::-1]
    (k_b, n) = b.shape[-2:][::-1] if tb else b.shape[-2:]
    assert k_a == k_b, (a.shape, b.shape, ta, tb)
    assert (a.ndim == 3) == (shard in ('k', 'm')) and (b.ndim == 3) == (shard in ('n', 'k'))
    k = k_a
    tm = _pick(m, 1024, 128)
    tn = _pick(n, 1024, 128)
    tk = _pick(k, 1024, 128)
    pm, pn, pk = m // tm, n // tn, k // tk
    gm = pm * (ns if shard == 'm' else 1)
    gn = pn * (ns if shard == 'n' else 1)
    gk = pk * (ns if shard == 'k' else 1)
    dims = (((0,) if ta else (1,), (1,) if tb else (0,)), ((), ()))
    op_dtype = F32 if exact else BF16

    def body(*refs):
        a_ref, b_ref = refs[0], refs[1]
        pos = 2
        res_ref = bias_ref = None
        if res is not None:
            res_ref = refs[pos]
            pos += 1
        if bias is not None:
            bias_ref = refs[pos]
            pos += 1
        o_ref, acc_ref = refs[pos], refs[pos + 1]
        kk = pl.program_id(2)

        @pl.when(kk == 0)
        def _():
            acc_ref[...] = jnp.zeros_like(acc_ref)

        acc_ref[...] += lax.dot_general(a_ref[...].astype(op_dtype), b_ref[...].astype(op_dtype), dims,
                                        precision=HI if exact else None, preferred_element_type=F32)

        @pl.when(kk == gk - 1)
        def _():
            o = acc_ref[...]
            if alpha != 1.0:
                o = o * alpha
            if bias_ref is not None:
                o = o + bias_ref[...]
            if res_ref is not None:
                o = o + res_ref[...]
            o_ref[...] = o.astype(out_dtype)

    def spec(block, sharded_on, order):
        per = {'m': pm, 'n': pn, 'k': pk}

        def index(i, j, kk):
            g = {'m': i, 'n': j, 'k': kk}
            r, c = order(i % pm if shard == 'm' else i, j % pn if shard == 'n' else j, kk % pk if shard == 'k' else kk)
            if sharded_on is None:
                return (r, c)
            return (g[sharded_on] // per[sharded_on], r, c)

        return pl.BlockSpec(block if sharded_on is None else (None,) + block, index)

    a_sh = shard if shard in ('k', 'm') else None
    b_sh = shard if shard in ('n', 'k') else None
    o_sh = shard if shard in ('n', 'm') else None
    a_spec = spec((tk, tm), a_sh, lambda i, j, kk: (kk, i)) if ta else spec((tm, tk), a_sh, lambda i, j, kk: (i, kk))
    b_spec = spec((tn, tk), b_sh, lambda i, j, kk: (j, kk)) if tb else spec((tk, tn), b_sh, lambda i, j, kk: (kk, j))
    ins, in_specs = [a, b], [a_spec, b_spec]
    if res is not None:
        assert o_sh is None
        ins.append(res)
        in_specs.append(pl.BlockSpec((tm, tn), lambda i, j, kk: (i, j)))
    if bias is not None:
        assert o_sh is None
        ins.append(bias)
        in_specs.append(pl.BlockSpec((1, tn), lambda i, j, kk: (0, j)))
    out_shape = (m, n) if o_sh is None else (ns, m, n)
    return _pcall(body, name=name, grid=(gm, gn, gk), in_specs=in_specs,
                  out_specs=spec((tm, tn), o_sh, lambda i, j, kk: (i, j)),
                  out_shape=jax.ShapeDtypeStruct(out_shape, out_dtype),
                  scratch_shapes=[pltpu.VMEM((tm, tn), F32)], compiler_params=_params())(*ins)


def _rows(body, ins, outs, *, n, name, tm=256):
    tm = _pick(n, tm, 16)
    in_specs = []
    for arr, kind in ins:
        if kind == 'r':
            in_specs.append(pl.BlockSpec((tm, arr.shape[1]), lambda i: (i, 0)))
        else:
            in_specs.append(pl.BlockSpec(arr.shape, lambda i: (0, 0)))
    out_specs, out_shape = [], []
    for cols, dtype, kind in outs:
        if kind == 'r':
            out_specs.append(pl.BlockSpec((tm, cols), lambda i: (i, 0)))
            out_shape.append(jax.ShapeDtypeStruct((n, cols), dtype))
        else:
            out_specs.append(pl.BlockSpec((1, cols), lambda i: (0, 0)))
            out_shape.append(jax.ShapeDtypeStruct((1, cols), dtype))
    n_in = len(ins)
    acc_ids = [j for j, o in enumerate(outs) if o[2] == 'a']

    def wrapped(*refs):
        if acc_ids:
            @pl.when(pl.program_id(0) == 0)
            def _():
                for j in acc_ids:
                    refs[n_in + j][...] = jnp.zeros_like(refs[n_in + j])
        body(*refs)

    res = _pcall(wrapped, name=name, grid=(n // tm,), in_specs=in_specs, out_specs=out_specs, out_shape=out_shape,
                 compiler_params=_params())(*[a for a, _ in ins])
    return res


def _rms_fwd(x, g, name):
    def body(x_ref, g_ref, o_ref):
        xv = x_ref[...]
        rstd = lax.rsqrt(jnp.mean(xv * xv, axis=-1, keepdims=True) + EPS)
        o_ref[...] = (xv * rstd * g_ref[...]).astype(BF16)
    return _rows(body, [(x, 'r'), (g, 'f')], [(x.shape[1], BF16, 'r')], n=x.shape[0], name=name)[0]


def _rms_bwd(x, g, dn, dres, name):
    def body(x_ref, g_ref, dn_ref, dres_ref, dx_ref, dg_ref):
        xv = x_ref[...]
        rstd = lax.rsqrt(jnp.mean(xv * xv, axis=-1, keepdims=True) + EPS)
        xh = xv * rstd
        dn = dn_ref[...]
        dg_ref[...] += jnp.sum(dn * xh, axis=0, keepdims=True)
        dxh = dn * g_ref[...]
        dx_ref[...] = dres_ref[...] + rstd * (dxh - xh * jnp.mean(dxh * xh, axis=-1, keepdims=True))
    d = x.shape[1]
    return _rows(body, [(x, 'r'), (g, 'f'), (dn, 'r'), (dres, 'r')], [(d, F32, 'r'), (d, F32, 'a')],
                 n=x.shape[0], name=name)


def _gelu_parts(y):
    c0 = 0.7978845608028654
    inner = c0 * (y + 0.044715 * y * y * y)
    th = jnp.tanh(inner)
    return th, c0 * (1.0 + 3.0 * 0.044715 * y * y)


def _gelu_fwd(y, name):
    def body(y_ref, o_ref):
        yv = y_ref[...]
        th, _ = _gelu_parts(yv)
        o_ref[...] = 0.5 * yv * (1.0 + th)
    return _rows(body, [(y, 'r')], [(y.shape[1], F32, 'r')], n=y.shape[0], name=name)[0]


def _glu_fwd(zg, t, name):
    def body(z_ref, t_ref, o_ref):
        o_ref[...] = (z_ref[...] * _sigmoid(t_ref[...])).astype(BF16)
    return _rows(body, [(zg, 'r'), (t, 'r')], [(zg.shape[1], BF16, 'r')], n=zg.shape[0], name=name)[0]


def _glu_bwd1(dy, zg, t, name):
    def body(dy_ref, z_ref, t_ref, dz_ref, dt_ref, db_ref):
        dyv, zv = dy_ref[...], z_ref[...]
        sg = _sigmoid(t_ref[...])
        dz_ref[...] = dyv * sg
        dt = dyv * zv * sg * (1.0 - sg)
        dt_ref[...] = dt.astype(BF16)
        db_ref[...] += jnp.sum(dt, axis=0, keepdims=True)
    w = zg.shape[1]
    return _rows(body, [(dy, 'r'), (zg, 'r'), (t, 'r')], [(w, F32, 'r'), (w, BF16, 'r'), (w, F32, 'a')],
                 n=zg.shape[0], name=name)


def _glu_bwd2(dzg, ys, u, dskip, name):
    def body(dz_ref, y_ref, u_ref, d_ref, dy_ref, du_ref, dd_ref):
        yv = y_ref[...]
        th, dinner = _gelu_parts(yv)
        dy = dz_ref[...] * (0.5 * (1.0 + th) + 0.5 * yv * (1.0 - th * th) * dinner)
        dy_ref[...] = dy
        du_ref[...] = dy * d_ref[...]
        dd_ref[...] += jnp.sum(dy * u_ref[...], axis=0, keepdims=True)
    w = ys.shape[1]
    return _rows(body, [(dzg, 'r'), (ys, 'r'), (u, 'r'), (dskip, 'f')], [(w, F32, 'r'), (w, F32, 'r'), (w, F32, 'a')],
                 n=ys.shape[0], name=name)


def _scale_rows(u, dskip, name):
    def body(u_ref, d_ref, o_ref):
        o_ref[...] = u_ref[...] * d_ref[...]
    return _rows(body, [(u, 'r'), (dskip, 'f')], [(u.shape[1], F32, 'r')], n=u.shape[0], name=name)[0]


def _merge_fwd(zg, ps, pg, name):
    def body(z_ref, ps_ref, pg_ref, o_ref):
        zv = z_ref[...]
        o_ref[...] = (_sigmoid(zv[:, :D_MODEL]) * ps_ref[...] + _sigmoid(zv[:, D_MODEL:]) * pg_ref[...]).astype(BF16)
    return _rows(body, [(zg, 'r'), (ps, 'r'), (pg, 'r')], [(D_MODEL, BF16, 'r')], n=zg.shape[0], name=name)[0]


def _merge_bwd(dm, zg, ps, pg, name):
    def body(dm_ref, z_ref, ps_ref, pg_ref, dps_ref, dpg_ref, dz_ref):
        dmv, zv = dm_ref[...], z_ref[...]
        s1, s2 = _sigmoid(zv[:, :D_MODEL]), _sigmoid(zv[:, D_MODEL:])
        dps_ref[...] = (dmv * s1).astype(BF16)
        dpg_ref[...] = (dmv * s2).astype(BF16)
        dz_ref[:, :D_MODEL] = dmv * ps_ref[...] * s1 * (1.0 - s1)
        dz_ref[:, D_MODEL:] = dmv * pg_ref[...] * s2 * (1.0 - s2)
    return _rows(body, [(dm, 'r'), (zg, 'r'), (ps, 'r'), (pg, 'r')],
                 [(D_MODEL, BF16, 'r'), (D_MODEL, BF16, 'r'), (2 * D_MODEL, F32, 'r')], n=zg.shape[0], name=name)


def _final_loss(h, g, tgt, name):
    def body(h_ref, g_ref, t_ref, loss_ref, dh_ref, dg_ref):
        hv = h_ref[...]
        rstd = lax.rsqrt(jnp.mean(hv * hv, axis=-1, keepdims=True) + EPS)
        xh = hv * rstd
        err = xh * g_ref[...] - t_ref[...]
        part = 0.5 * jnp.sum(jnp.mean(err * err, axis=-1, keepdims=True), axis=0, keepdims=True)
        loss_ref[...] += jnp.broadcast_to(part, loss_ref.shape)
        dout = err * (1.0 / hv.shape[1])
        dg_ref[...] += jnp.sum(dout * xh, axis=0, keepdims=True)
        dxh = dout * g_ref[...]
        dh_ref[...] = rstd * (dxh - xh * jnp.mean(dxh * xh, axis=-1, keepdims=True))
    d = h.shape[1]
    return _rows(body, [(h, 'r'), (g, 'f'), (tgt, 'r')], [(LANE, F32, 'a'), (d, F32, 'r'), (d, F32, 'a')],
                 n=h.shape[0], name=name)


def _adamw_math(wv, gv, mv, vv):
    nm = ADAM_B1 * mv + (1.0 - ADAM_B1) * gv
    nv = ADAM_B2 * vv + (1.0 - ADAM_B2) * (gv * gv)
    m_hat = nm / (1.0 - ADAM_B1 ** ADAM_STEP)
    v_hat = nv / (1.0 - ADAM_B2 ** ADAM_STEP)
    return -ADAM_LR * (m_hat / (jnp.sqrt(v_hat) + ADAM_EPS) + ADAM_WD * wv), nm, nv


def _adamw(w, g, m, v, name):
    def body(w_ref, g_ref, m_ref, v_ref, d_ref, nm_ref, nv_ref):
        d_ref[...], nm_ref[...], nv_ref[...] = _adamw_math(w_ref[...], g_ref[...], m_ref[...], v_ref[...])
    c = w.shape[1]
    return _rows(body, [(w, 'r'), (g, 'r'), (m, 'r'), (v, 'r')], [(c, F32, 'r')] * 3, n=w.shape[0], name=name)


def _adamw_halves(w, g_own, g_sib, m, v, c_arr, name):
    r, cols = w.shape
    h = r // 2
    tr = _pick(h, 256, 8)
    per = h // tr

    def body(c_ref, w_ref, go_ref, gs_ref, m_ref, v_ref, g_ref, d_ref, nm_ref, nv_ref):
        mine = (pl.program_id(0) // per) == c_ref[0]
        gv = jnp.where(mine, go_ref[...], gs_ref[...])
        g_ref[...] = gv
        d_ref[...], nm_ref[...], nv_ref[...] = _adamw_math(w_ref[...], gv, m_ref[...], v_ref[...])

    full = pl.BlockSpec((tr, cols), lambda i, c_ref: (i, 0))
    half = pl.BlockSpec((tr, cols), lambda i, c_ref: (i % per, 0))
    grid_spec = pltpu.PrefetchScalarGridSpec(num_scalar_prefetch=1, grid=(2 * per,),
                                             in_specs=[full, half, half, full, full], out_specs=[full] * 4)
    return _pcall(body, name=name, grid_spec=grid_spec, out_shape=[jax.ShapeDtypeStruct((r, cols), F32)] * 4,
                  compiler_params=_params())(c_arr, w, g_own, g_sib, m, v)


def _shift_rows(v, sh, down):
    rolled = pltpu.roll(v, sh if down else v.shape[0] - sh, axis=0)
    row = lax.broadcasted_iota(jnp.int32, v.shape, 0)
    keep = (row >= sh) if down else (row < v.shape[0] - sh)
    return jnp.where(keep, rolled, 0.0)


def _chain_segments(st_r, st_i, pw_r_ref, pw_i_ref, conj, down):
    vr, vi = st_r[...], st_i[...]
    sh, k = 1, 0
    while sh < SEG:
        pr, pi = pw_r_ref[k:k + 1, :], pw_i_ref[k:k + 1, :]
        if conj:
            pi = -pi
        sr, si = _shift_rows(vr, sh, down), _shift_rows(vi, sh, down)
        vr, vi = vr + pr * sr - pi * si, vi + pr * si + pi * sr
        sh, k = sh * 2, k + 1
    st_r[...] = _shift_rows(vr, 1, down)
    st_i[...] = _shift_rows(vi, 1, down)


def _s5_scan(bu, ar8, ai8, pw_r, pw_i, name):
    n = bu.shape[0]
    rb = SCAN_ROWS
    nb, steps, lc = n // rb, rb // SEG, 512

    def body(bu_ref, ar_ref, ai_ref, pwr_ref, pwi_ref, x_ref, st_r, st_i):
        ph, b = pl.program_id(0), pl.program_id(1)

        @pl.when((ph == 0) & (b == 0))
        def _():
            st_r[...] = jnp.zeros_like(st_r)
            st_i[...] = jnp.zeros_like(st_i)

        def scan(store):
            for c in range(S5_GP // lc):
                re, im = slice(c * lc, (c + 1) * lc), slice(S5_GP + c * lc, S5_GP + (c + 1) * lc)
                a_r, a_i = ar_ref[:, re], ai_ref[:, re]

                def step(s, carry):
                    xr, xi = carry
                    rows = pl.ds(pl.multiple_of(s * SEG, SEG), SEG)
                    nr = a_r * xr - a_i * xi + bu_ref[rows, re]
                    ni = a_r * xi + a_i * xr + bu_ref[rows, im]
                    if store:
                        x_ref[rows, re] = nr
                        x_ref[rows, im] = ni
                    return nr, ni

                xr, xi = lax.fori_loop(0, steps, step, (st_r[:, re], st_i[:, re]), unroll=4)
                st_r[:, re] = xr
                st_i[:, re] = xi

        @pl.when(ph == 0)
        def _():
            scan(False)

        @pl.when((ph == 0) & (b == nb - 1))
        def _():
            _chain_segments(st_r, st_i, pwr_ref, pwi_ref, conj=False, down=True)

        @pl.when(ph == 1)
        def _():
            scan(True)

    full = lambda a: pl.BlockSpec(a.shape, lambda ph, b: (0, 0))
    return _pcall(body, name=name, grid=(2, nb),
                  in_specs=[pl.BlockSpec((rb, 2 * S5_GP), lambda ph, b: (b, 0)), full(ar8), full(ai8), full(pw_r), full(pw_i)],
                  out_specs=pl.BlockSpec((rb, 2 * S5_GP), lambda ph, b: (b * ph, 0)),
                  out_shape=jax.ShapeDtypeStruct((n, 2 * S5_GP), F32),
                  scratch_shapes=[pltpu.VMEM((SEG, S5_GP), F32), pltpu.VMEM((SEG, S5_GP), F32)],
                  compiler_params=_params())(bu, ar8, ai8, pw_r, pw_i)


def _s5_scan_bwd(gx, xs, ar8, ai8, pw_r, pw_i, name):
    n = gx.shape[0]
    rb = SCAN_ROWS
    nb, steps, lc = n // rb, rb // SEG, 256

    def body(gx_ref, x_ref, ar_ref, ai_ref, pwr_ref, pwi_ref, lam_ref, da_ref, st_r, st_i):
        ph, b = pl.program_id(0), pl.program_id(1)

        @pl.when((ph == 0) & (b == 0))
        def _():
            st_r[...] = jnp.zeros_like(st_r)
            st_i[...] = jnp.zeros_like(st_i)
            da_ref[...] = jnp.zeros_like(da_ref)

        def scan(store):
            for c in range(S5_GP // lc):
                re, im = slice(c * lc, (c + 1) * lc), slice(S5_GP + c * lc, S5_GP + (c + 1) * lc)
                a_r, a_i = ar_ref[:, re], ai_ref[:, re]

                def step(s, carry):
                    rows = pl.ds(pl.multiple_of((steps - 1 - s) * SEG, SEG), SEG)
                    if store:
                        lr, li, dr, di = carry
                        xr, xi = x_ref[rows, re], x_ref[rows, im]
                        dr = dr + lr * xr + li * xi
                        di = di + li * xr - lr * xi
                    else:
                        lr, li = carry
                    nr = a_r * lr + a_i * li + gx_ref[rows, re]
                    ni = a_r * li - a_i * lr + gx_ref[rows, im]
                    if store:
                        lam_ref[rows, re] = nr
                        lam_ref[rows, im] = ni
                        return nr, ni, dr, di
                    return nr, ni

                if store:
                    lr, li, dr, di = lax.fori_loop(0, steps, step, (st_r[:, re], st_i[:, re], da_ref[:, re], da_ref[:, im]),
                                                   unroll=4)
                    da_ref[:, re] = dr
                    da_ref[:, im] = di
                else:
                    lr, li = lax.fori_loop(0, steps, step, (st_r[:, re], st_i[:, re]), unroll=4)
                st_r[:, re] = lr
                st_i[:, re] = li

        @pl.when(ph == 0)
        def _():
            scan(False)

        @pl.when((ph == 0) & (b == nb - 1))
        def _():
            _chain_segments(st_r, st_i, pwr_ref, pwi_ref, conj=True, down=False)

        @pl.when(ph == 1)
        def _():
            scan(True)

    full = lambda a: pl.BlockSpec(a.shape, lambda ph, b: (0, 0))
    rev = lambda ph, b: (nb - 1 - b, 0)
    return _pcall(body, name=name, grid=(2, nb),
                  in_specs=[pl.BlockSpec((rb, 2 * S5_GP), rev), pl.BlockSpec((rb, 2 * S5_GP), lambda ph, b: ((nb - 1 - b) * ph, 0)),
                            full(ar8), full(ai8), full(pw_r), full(pw_i)],
                  out_specs=[pl.BlockSpec((rb, 2 * S5_GP), lambda ph, b: (nb - 1 - b * ph, 0)),
                             pl.BlockSpec((SEG, 2 * S5_GP), lambda ph, b: (0, 0))],
                  out_shape=[jax.ShapeDtypeStruct((n, 2 * S5_GP), F32), jax.ShapeDtypeStruct((SEG, 2 * S5_GP), F32)],
                  scratch_shapes=[pltpu.VMEM((SEG, S5_GP), F32), pltpu.VMEM((SEG, S5_GP), F32)],
                  compiler_params=_params())(gx, xs, ar8, ai8, pw_r, pw_i)


def _s5_discretize(lam_re, lam_im, log_dt, b_re, b_im):
    dt = jnp.exp(log_dt)[:, None]
    mag = jnp.exp(lam_re * dt)
    ar = mag * jnp.cos(lam_im * dt)
    ai = mag * jnp.sin(lam_im * dt)
    den = lam_re * lam_re + lam_im * lam_im
    nr = ar - 1.0
    fr = (nr * lam_re + ai * lam_im) / den
    fi = (ai * lam_re - nr * lam_im) / den
    bbar_re = fr[:, :, None] * b_re - fi[:, :, None] * b_im
    bbar_im = fr[:, :, None] * b_im + fi[:, :, None] * b_re
    return ar, ai, bbar_re, bbar_im


def _block_diag(t):
    g, a, b = t.shape
    eye = jnp.eye(g, dtype=t.dtype)
    return (t[:, :, None, :] * eye[:, None, :, None]).reshape(g * a, g * b)


def _diag_blocks(m, a, b):
    g = S5_G
    return jnp.einsum('gagb->gab', m.reshape(g, a, g, b))


def _permute_rows(t):
    n = t.shape[0]
    return t.reshape(SEG, n // SEG, t.shape[1]).transpose(1, 0, 2).reshape(n, t.shape[1])


def _unpermute_rows(t):
    n = t.shape[0]
    return t.reshape(n // SEG, SEG, t.shape[1]).transpose(1, 0, 2).reshape(n, t.shape[1])


def _segment_powers(ar, ai, seg_steps):
    pr, pi = ar.reshape(1, S5_GP), ai.reshape(1, S5_GP)
    e = 1
    while e < seg_steps:
        pr, pi = pr * pr - pi * pi, 2.0 * pr * pi
        e *= 2
    assert e == seg_steps, "segment length must be a power of two"
    rows_r, rows_i = [], []
    for _ in range(3):
        rows_r.append(pr)
        rows_i.append(pi)
        pr, pi = pr * pr - pi * pi, 2.0 * pr * pi
    pad = jnp.zeros((SEG - 3, S5_GP), F32)
    return jnp.concatenate(rows_r + [pad], axis=0), jnp.concatenate(rows_i + [pad], axis=0)


NT = (((1,), (1,)), ((), ()))
TN = (((0,), (0,)), ((), ()))


def _dot(a, b, dims=None, exact=False):
    dims = (((1,), (0,)), ((), ())) if dims is None else dims
    if exact:
        return lax.dot_general(a, b, dims, precision=HI, preferred_element_type=F32)
    return lax.dot_general(a.astype(BF16), b.astype(BF16), dims, preferred_element_type=F32)


def _gla_chunk_fwd(qc, kc, vc, al, wup, bup, s_prev, tril):
    z = _dot(al, wup) + bup
    la = (jnp.minimum(z, 0.0) - jnp.log(1.0 + jnp.exp(-jnp.abs(z)))) * (1.0 / GLA_TAU)
    bc = _dot(tril, la, exact=True)
    blb = _dot(la, jnp.ones((GLA_CHUNK, GLA_DV), F32), TN, exact=True)
    bl = bc[GLA_CHUNK - 1:GLA_CHUNK, :]
    ebc = jnp.exp(bc)
    qt = qc * (GLA_DK ** -0.5) * ebc
    kt = kc * jnp.exp(-bc)
    ke = kc * jnp.exp(bl - bc)
    sc = _dot(qt, kt, NT) * tril
    o = _dot(sc, vc) + _dot(qt, s_prev)
    return z, bc, bl, blb, ebc, qt, kt, ke, sc, o


GLA_ROWS = 512
GLA_CPB = GLA_ROWS // GLA_CHUNK


def _gla_in_specs(arrs, blk):
    specs = []
    for a in arrs:
        if a.ndim == 3 and a.shape[1] > LANE:
            specs.append(pl.BlockSpec((GLA_HEADS, GLA_ROWS, a.shape[2]), lambda j: (0, blk(j), 0)))
        elif a.ndim == 3:
            specs.append(pl.BlockSpec(a.shape, lambda j: (0, 0, 0)))
        else:
            specs.append(pl.BlockSpec((GLA_ROWS, a.shape[1]), lambda j: (blk(j), 0)))
    return specs


def _tri(lower):
    ri = lax.broadcasted_iota(jnp.int32, (GLA_CHUNK, GLA_CHUNK), 0)
    ci = lax.broadcasted_iota(jnp.int32, (GLA_CHUNK, GLA_CHUNK), 1)
    return ((ri >= ci) if lower else (ri <= ci)).astype(F32)


def _gla_fwd(q, k, v, r, al, wup, bup, gn, name):
    n = q.shape[1]
    nc = n // GLA_CHUNK

    def body(q_ref, k_ref, v_ref, r_ref, al_ref, wup_ref, bup_ref, gn_ref, y_ref, sp_ref, s_ref):
        @pl.when(pl.program_id(0) == 0)
        def _():
            s_ref[...] = jnp.zeros_like(s_ref)

        tril = _tri(True)

        def chunk(c, carry):
            rows = pl.ds(pl.multiple_of(c * GLA_CHUNK, GLA_CHUNK), GLA_CHUNK)
            alc = al_ref[rows, :]
            for h in range(GLA_HEADS):
                vc, rc, s_prev = v_ref[h, rows, :], r_ref[h, rows, :], s_ref[h]
                _, _, _, blb, _, _, _, ke, _, o = _gla_chunk_fwd(q_ref[h, rows, :], k_ref[h, rows, :], vc, alc,
                                                                  wup_ref[h], bup_ref[h], s_prev, tril)
                sp_ref[h, c] = s_prev
                rstd = lax.rsqrt(jnp.mean(o * o, axis=-1, keepdims=True) + EPS)
                y_ref[h, rows, :] = o * rstd * gn_ref[h] * (rc * _sigmoid(rc))
                s_ref[h] = jnp.exp(blb) * s_prev + _dot(ke, vc, TN)
            return carry

        lax.fori_loop(0, GLA_CPB, chunk, 0)

    ins = [q, k, v, r, al, wup, bup, gn]
    return _pcall(body, name=name, grid=(n // GLA_ROWS,), in_specs=_gla_in_specs(ins, lambda j: j),
                  out_specs=[pl.BlockSpec((GLA_HEADS, GLA_ROWS, GLA_DV), lambda j: (0, j, 0)),
                             pl.BlockSpec((GLA_HEADS, GLA_CPB, GLA_DK, GLA_DV), lambda j: (0, j, 0, 0))],
                  out_shape=[jax.ShapeDtypeStruct((GLA_HEADS, n, GLA_DV), F32),
                             jax.ShapeDtypeStruct((GLA_HEADS, nc, GLA_DK, GLA_DV), F32)],
                  scratch_shapes=[pltpu.VMEM((GLA_HEADS, GLA_DK, GLA_DV), F32)],
                  compiler_params=_params())(*ins)


def _gla_bwd(q, k, v, r, al, wup, bup, gn, sp, dy, name):
    n = q.shape[1]
    nc = n // GLA_CHUNK

    nb = n // GLA_ROWS

    def body(q_ref, k_ref, v_ref, r_ref, al_ref, wup_ref, bup_ref, gn_ref, dy_ref, sp_ref,
             dq_ref, dk_ref, dv_ref, dr_ref, dz_ref, dgn_ref, dbup_ref, ds_ref):
        @pl.when(pl.program_id(0) == 0)
        def _():
            ds_ref[...] = jnp.zeros_like(ds_ref)
            dgn_ref[...] = jnp.zeros_like(dgn_ref)
            dbup_ref[...] = jnp.zeros_like(dbup_ref)

        tril, triu = _tri(True), _tri(False)

        def chunk(i, carry):
            c = GLA_CPB - 1 - i
            rows = pl.ds(pl.multiple_of(c * GLA_CHUNK, GLA_CHUNK), GLA_CHUNK)
            alc = al_ref[rows, :]
            for h in range(GLA_HEADS):
                qc, kc, vc, rc = q_ref[h, rows, :], k_ref[h, rows, :], v_ref[h, rows, :], r_ref[h, rows, :]
                s_prev, ds = sp_ref[h, c], ds_ref[h]
                z, bc, bl, blb, ebc, qt, kt, ke, sc, o = _gla_chunk_fwd(qc, kc, vc, alc, wup_ref[h], bup_ref[h], s_prev, tril)
                rstd = lax.rsqrt(jnp.mean(o * o, axis=-1, keepdims=True) + EPS)
                on = o * rstd
                sr = _sigmoid(rc)
                sil = rc * sr
                dyv, gnv = dy_ref[h, rows, :], gn_ref[h]
                dgn_ref[h] += jnp.sum(dyv * on * sil, axis=0, keepdims=True)
                dr_ref[h, rows, :] = dyv * on * gnv * (sr * (1.0 + rc * (1.0 - sr)))
                don = dyv * gnv * sil
                do = rstd * (don - on * jnp.mean(don * on, axis=-1, keepdims=True))
                dp = _dot(do, vc, NT) * tril
                dv_ref[h, rows, :] = _dot(sc, do, TN) + _dot(ke, ds)
                dqt = _dot(dp, kt) + _dot(do, s_prev, NT)
                dkt = _dot(dp, qt, TN)
                dke = _dot(vc, ds, NT)
                ddec = _dot(jnp.ones((8, GLA_DV), F32), ds * s_prev, NT, exact=True)[0:1, :]
                ds_ref[h] = jnp.exp(blb) * ds + _dot(qt, do, TN)
                dq_ref[h, rows, :] = dqt * (GLA_DK ** -0.5) * ebc
                dk_ref[h, rows, :] = dkt * jnp.exp(-bc) + dke * jnp.exp(bl - bc)
                dbc = dqt * qt - dkt * kt - dke * ke
                dbl = jnp.sum(dke * ke, axis=0, keepdims=True) + ddec * jnp.exp(bl)
                dla = _dot(triu, dbc, exact=True) + dbl
                dz = dla * (1.0 - _sigmoid(z)) * (1.0 / GLA_TAU)
                dz_ref[h, rows, :] = dz
                dbup_ref[h] += jnp.sum(dz, axis=0, keepdims=True)
            return carry

        lax.fori_loop(0, GLA_CPB, chunk, 0)

    rev = lambda j: nb - 1 - j
    ins = [q, k, v, r, al, wup, bup, gn, dy, sp]
    in_specs = _gla_in_specs(ins[:9], rev) + [pl.BlockSpec((GLA_HEADS, GLA_CPB, GLA_DK, GLA_DV), lambda j: (0, rev(j), 0, 0))]
    hs = lambda w: pl.BlockSpec((GLA_HEADS, GLA_ROWS, w), lambda j: (0, rev(j), 0))
    h1 = lambda w: pl.BlockSpec((GLA_HEADS, 1, w), lambda j: (0, 0, 0))
    sh = lambda w: jax.ShapeDtypeStruct((GLA_HEADS, n, w), F32)
    s1 = lambda w: jax.ShapeDtypeStruct((GLA_HEADS, 1, w), F32)
    return _pcall(body, name=name, grid=(nb,), in_specs=in_specs,
                  out_specs=[hs(GLA_DK), hs(GLA_DK), hs(GLA_DV), hs(GLA_DV), hs(GLA_DK), h1(GLA_DV), h1(GLA_DK)],
                  out_shape=[sh(GLA_DK), sh(GLA_DK), sh(GLA_DV), sh(GLA_DV), sh(GLA_DK), s1(GLA_DV), s1(GLA_DK)],
                  scratch_shapes=[pltpu.VMEM((GLA_HEADS, GLA_DK, GLA_DV), F32)],
                  compiler_params=_params())(*ins)


def _heads(t, w):
    return t.reshape(t.shape[0], GLA_HEADS, w).transpose(1, 0, 2)


def _unheads(t):
    return t.transpose(1, 0, 2).reshape(t.shape[1], GLA_HEADS * t.shape[2])


ANY = pl.BlockSpec(memory_space=pl.ANY)


def _place():
    x, y, c = lax.axis_index("x"), lax.axis_index("y"), lax.axis_index("c")
    chips = [(1 - x, y), (x, 1 - y), (1 - x, 1 - y)]
    return x, y, c, chips


def _remote(src, dst, ssem, rsem, dev):
    return pltpu.make_async_remote_copy(src_ref=src, dst_ref=dst, send_sem=ssem, recv_sem=rsem, device_id=dev,
                                        device_id_type=MESH_ID)


def _half(c, rows):
    h = rows // 2
    return pl.ds(pl.multiple_of(c * h, 8), h)


def _gather_shards(shards, name):
    nw = len(shards)

    def body(*refs):
        ins, outs = refs[:nw], refs[nw:2 * nw]
        ssem, rsem, osem_s, osem_r = refs[2 * nw:]
        x, y, c, chips = _place()
        mine = 2 * x + y
        sibling = (x, y, 1 - c)
        own, sends = [], []
        for w in range(nw):
            half = _half(c, ins[w].shape[0])
            cp = _remote(ins[w], outs[w].at[mine], osem_s.at[w], osem_r.at[w], sibling)
            cp.start()
            own.append(cp)
            for k, (px, py) in enumerate(chips):
                cp = _remote(ins[w].at[half], outs[w].at[mine, half], ssem.at[6 * w + k], rsem.at[6 * w + k], (px, py, c))
                cp.start()
                sends.append(cp)
        for w in range(nw):
            half = _half(c, ins[w].shape[0])
            for k, (px, py) in enumerate(chips):
                theirs = outs[w].at[2 * px + py, half]
                _remote(theirs, theirs, ssem.at[6 * w + k], rsem.at[6 * w + k], (px, py, c)).wait_recv()
                cp = _remote(theirs, theirs, ssem.at[6 * w + 3 + k], rsem.at[6 * w + 3 + k], sibling)
                cp.start()
                sends.append(cp)
        for w in range(nw):
            other = _half(1 - c, ins[w].shape[0])
            for k, (px, py) in enumerate(chips):
                theirs = outs[w].at[2 * px + py, other]
                _remote(theirs, theirs, ssem.at[6 * w + 3 + k], rsem.at[6 * w + 3 + k], sibling).wait_recv()
        for cp in sends:
            cp.wait_send()
        for cp in own:
            cp.wait()

    return _pcall(body, name=name, in_specs=[ANY] * nw, out_specs=[ANY] * nw,
                  out_shape=[jax.ShapeDtypeStruct((4,) + s.shape, s.dtype) for s in shards],
                  scratch_shapes=[pltpu.SemaphoreType.DMA((6 * nw,)), pltpu.SemaphoreType.DMA((6 * nw,)),
                                  pltpu.SemaphoreType.DMA((nw,)), pltpu.SemaphoreType.DMA((nw,))],
                  compiler_params=pltpu.CompilerParams(has_side_effects=True))(*shards)


def _swap_halves(grads, name):
    nw = len(grads)

    def body(*refs):
        ins, outs = refs[:nw], refs[nw:2 * nw]
        ssem, rsem = refs[2 * nw:]
        x, y, c, _ = _place()
        cps = []
        for w in range(nw):
            other = _half(1 - c, ins[w].shape[1])
            cp = _remote(ins[w].at[:, other, :], outs[w], ssem.at[w], rsem.at[w], (x, y, 1 - c))
            cp.start()
            cps.append(cp)
        for cp in cps:
            cp.wait()

    return _pcall(body, name=name, in_specs=[ANY] * nw, out_specs=[ANY] * nw,
                  out_shape=[jax.ShapeDtypeStruct((4, g.shape[1] // 2, g.shape[2]), g.dtype) for g in grads],
                  scratch_shapes=[pltpu.SemaphoreType.DMA((nw,)), pltpu.SemaphoreType.DMA((nw,))],
                  compiler_params=pltpu.CompilerParams(has_side_effects=True))(*grads)


def _scatter_chip_sums(sums, name):
    nw = len(sums)

    def body(*refs):
        ins, outs = refs[:nw], refs[nw:2 * nw]
        ssem, rsem = refs[2 * nw:]
        x, y, c, chips = _place()
        cps = []
        for w in range(nw):
            for k, (px, py) in enumerate(chips):
                cp = _remote(ins[w].at[2 * px + py], outs[w].at[k], ssem.at[3 * w + k], rsem.at[3 * w + k], (px, py, c))
                cp.start()
                cps.append(cp)
        for cp in cps:
            cp.wait()

    return _pcall(body, name=name, in_specs=[ANY] * nw, out_specs=[ANY] * nw,
                  out_shape=[jax.ShapeDtypeStruct((3,) + s.shape[1:], s.dtype) for s in sums],
                  scratch_shapes=[pltpu.SemaphoreType.DMA((3 * nw,)), pltpu.SemaphoreType.DMA((3 * nw,))],
                  compiler_params=pltpu.CompilerParams(has_side_effects=True))(*sums)


def _swap_reduced(halves, name):
    nw = len(halves)

    def body(*refs):
        ins, outs = refs[:nw], refs[nw:2 * nw]
        ssem, rsem = refs[2 * nw:]
        x, y, c, _ = _place()
        cps = []
        for w in range(nw):
            cp = _remote(ins[w], outs[w], ssem.at[w], rsem.at[w], (x, y, 1 - c))
            cp.start()
            cps.append(cp)
        for cp in cps:
            cp.wait()

    return _pcall(body, name=name, in_specs=[ANY] * nw, out_specs=[ANY] * nw,
                  out_shape=[jax.ShapeDtypeStruct(h.shape, h.dtype) for h in halves],
                  scratch_shapes=[pltpu.SemaphoreType.DMA((nw,)), pltpu.SemaphoreType.DMA((nw,))],
                  compiler_params=pltpu.CompilerParams(has_side_effects=True))(*halves)


def _chip_sum(g, recv, c_arr, name):
    _, r, cols = g.shape
    h = r // 2
    tr = _pick(h, 256, 16)
    g4 = g.reshape(4, 2, h, cols)

    def body(c_ref, g_ref, r_ref, o_ref):
        o_ref[...] = (g_ref[...] + r_ref[...]).astype(BF16)

    grid_spec = pltpu.PrefetchScalarGridSpec(
        num_scalar_prefetch=1, grid=(4, h // tr),
        in_specs=[pl.BlockSpec((None, None, tr, cols), lambda s, i, c_ref: (s, c_ref[0], i, 0)),
                  pl.BlockSpec((None, tr, cols), lambda s, i, c_ref: (s, i, 0))],
        out_specs=pl.BlockSpec((None, tr, cols), lambda s, i, c_ref: (s, i, 0)))
    return _pcall(body, name=name, grid_spec=grid_spec, out_shape=jax.ShapeDtypeStruct((4, h, cols), BF16),
                  compiler_params=_params())(c_arr, g4, recv)


def _owner_sum(sums, others, s_arr, name):
    _, h, cols = sums.shape
    tr = _pick(h, 256, 16)

    def body(s_ref, a_ref, o_ref, out_ref):
        f = lambda v: v.astype(F32)
        out_ref[...] = (f(a_ref[...]) + f(o_ref[0])) + (f(o_ref[1]) + f(o_ref[2]))

    grid_spec = pltpu.PrefetchScalarGridSpec(
        num_scalar_prefetch=1, grid=(h // tr,),
        in_specs=[pl.BlockSpec((None, tr, cols), lambda i, s_ref: (s_ref[0], i, 0)),
                  pl.BlockSpec((3, tr, cols), lambda i, s_ref: (0, i, 0))],
        out_specs=pl.BlockSpec((tr, cols), lambda i, s_ref: (i, 0)))
    return _pcall(body, name=name, grid_spec=grid_spec, out_shape=jax.ShapeDtypeStruct((h, cols), F32),
                  compiler_params=_params())(s_arr, sums, others)


def _allreduce_small(v, name):
    def body(v_ref, o_ref, r0, r1, ssem, rsem):
        x, y, c, chips = _place()
        cp = _remote(v_ref, r0, ssem.at[0], rsem.at[0], (x, y, 1 - c))
        cp.start()
        cp.wait()
        o_ref[...] = v_ref[...] + r0[...]
        cps = []
        for k, (px, py) in enumerate(chips):
            cp = _remote(o_ref, r1.at[k], ssem.at[1 + k], rsem.at[1 + k], (px, py, c))
            cp.start()
            cps.append(cp)
        for cp in cps:
            cp.wait()
        o_ref[...] = (o_ref[...] + r1[0]) + (r1[1] + r1[2])

    vm = pl.BlockSpec(memory_space=pltpu.VMEM)
    return _pcall(body, name=name, in_specs=[vm], out_specs=vm, out_shape=jax.ShapeDtypeStruct(v.shape, F32),
                  scratch_shapes=[pltpu.VMEM(v.shape, F32), pltpu.VMEM((3,) + v.shape, F32),
                                  pltpu.SemaphoreType.DMA((4,)), pltpu.SemaphoreType.DMA((4,))],
                  compiler_params=_params(has_side_effects=True))(v)


def _pack_small(parts):
    flat = jnp.concatenate([p.reshape(-1).astype(F32) for p in parts])
    pad = (-flat.shape[0]) % (64 * LANE)
    return jnp.pad(flat, (0, pad)).reshape(-1, LANE)


def _unpack_small(packed, like):
    flat, out, pos = packed.reshape(-1), [], 0
    for p in like:
        out.append(flat[pos:pos + p.size].reshape(p.shape))
        pos += p.size
    return out


FFN_ROWS = 512


def _ffn_specs(n, d, fs):
    row = pl.BlockSpec((FFN_ROWS, d), lambda i, s: (i, 0))
    gain = pl.BlockSpec((1, d), lambda i, s: (0, 0))
    w_col = pl.BlockSpec((None, d, fs), lambda i, s: (s, 0, 0))
    w_row = pl.BlockSpec((None, fs, d), lambda i, s: (s, 0, 0))
    hid = pl.BlockSpec((None, FFN_ROWS, fs), lambda i, s: (s, i, 0))
    return row, gain, w_col, w_row, hid


def _ffn_fwd(h, g, w1, w3, w2, tag):
    n, d = h.shape
    ns, _, fs = w1.shape
    row, gain, w_col, w_row, hid = _ffn_specs(n, d, fs)

    def body(h_ref, g_ref, w1_ref, w3_ref, w2_ref, out_ref, n1_ref, a_ref, b_ref, hm_ref, acc_ref):
        s = pl.program_id(1)

        @pl.when(s == 0)
        def _():
            xv = h_ref[...]
            rstd = lax.rsqrt(jnp.mean(xv * xv, axis=-1, keepdims=True) + EPS)
            n1_ref[...] = (xv * rstd * g_ref[...]).astype(BF16)
            acc_ref[...] = jnp.zeros_like(acc_ref)

        n1 = n1_ref[...]
        a = jnp.dot(n1, w1_ref[...], preferred_element_type=F32)
        b = jnp.dot(n1, w3_ref[...], preferred_element_type=F32)
        hm = (a * _sigmoid(a) * b).astype(BF16)
        a_ref[...] = a.astype(BF16)
        b_ref[...] = b.astype(BF16)
        hm_ref[...] = hm
        acc_ref[...] += jnp.dot(hm, w2_ref[...], preferred_element_type=F32)

        @pl.when(s == ns - 1)
        def _():
            out_ref[...] = h_ref[...] + 0.5 * acc_ref[...]

    hid_shape = jax.ShapeDtypeStruct((ns, n, fs), BF16)
    out, n1, a, b, hm = _pcall(
        body, name=f"{tag}_fwd", grid=(n // FFN_ROWS, ns), in_specs=[row, gain, w_col, w_col, w_row],
        out_specs=[row, row, hid, hid, hid],
        out_shape=[jax.ShapeDtypeStruct((n, d), F32), jax.ShapeDtypeStruct((n, d), BF16), hid_shape, hid_shape, hid_shape],
        scratch_shapes=[pltpu.VMEM((FFN_ROWS, d), F32)], compiler_params=_params())(h, g, w1, w3, w2)
    return out, (h, n1, a, b, hm)


def _ffn_bwd(dout, saved, g, w1, w3, w2, tag):
    h, n1, a, b, hm = saved
    n, d = h.shape
    ns, _, fs = w1.shape
    row, gain, w_col, w_row, hid = _ffn_specs(n, d, fs)

    def body(do_ref, h_ref, g_ref, a_ref, b_ref, w1_ref, w3_ref, w2_ref, dh_ref, da_ref, db_ref, dg_ref, acc_ref):
        i, s = pl.program_id(0), pl.program_id(1)

        @pl.when(s == 0)
        def _():
            acc_ref[...] = jnp.zeros_like(acc_ref)

        @pl.when((s == 0) & (i == 0))
        def _():
            dg_ref[...] = jnp.zeros_like(dg_ref)

        dhm = _dot(0.5 * do_ref[...], w2_ref[...], NT)
        av, bv = a_ref[...].astype(F32), b_ref[...].astype(F32)
        sg = _sigmoid(av)
        da = (dhm * bv * (sg * (1.0 + av * (1.0 - sg)))).astype(BF16)
        db = (dhm * av * sg).astype(BF16)
        da_ref[...] = da
        db_ref[...] = db
        acc_ref[...] += _dot(da, w1_ref[...], NT) + _dot(db, w3_ref[...], NT)

        @pl.when(s == ns - 1)
        def _():
            xv, dn = h_ref[...], acc_ref[...]
            rstd = lax.rsqrt(jnp.mean(xv * xv, axis=-1, keepdims=True) + EPS)
            xh = xv * rstd
            dg_ref[...] += jnp.sum(dn * xh, axis=0, keepdims=True)
            dxh = dn * g_ref[...]
            dh_ref[...] = do_ref[...] + rstd * (dxh - xh * jnp.mean(dxh * xh, axis=-1, keepdims=True))

    hid_shape = jax.ShapeDtypeStruct((ns, n, fs), BF16)
    dh, da, db, dg = _pcall(
        body, name=f"{tag}_bwd", grid=(n // FFN_ROWS, ns), in_specs=[row, row, gain, hid, hid, w_col, w_col, w_row],
        out_specs=[row, hid, hid, gain],
        out_shape=[jax.ShapeDtypeStruct((n, d), F32), hid_shape, hid_shape, jax.ShapeDtypeStruct((1, d), F32)],
        scratch_shapes=[pltpu.VMEM((FFN_ROWS, d), F32)], compiler_params=_params())(dout, h, g, a, b, w1, w3, w2)
    gw2 = _mm(hm, dout, ta=True, shard='m', alpha=0.5, name=f"{tag}_gw2")
    gw1 = _mm(n1, da, ta=True, shard='n', name=f"{tag}_gw1")
    gw3 = _mm(n1, db, ta=True, shard='n', name=f"{tag}_gw3")
    return dh, dg, gw1, gw3, gw2


def _local_step(x, tgt, p):
    n = x.shape[0]
    f = lambda name: p[name].reshape(1, D_MODEL) if name.endswith('_norm') and name != 'gla_out_norm' else p[name]
    h1, ffn1 = _ffn_fwd(x, f('ffn1_norm'), f('ffn1_w1'), f('ffn1_w3'), f('ffn1_w2'), "ffn1")
    u = _rms_fwd(h1, f('mix_norm'), "mix_rms")
    w_in = f('w_in')
    w_a = w_in[:, :2048]
    w_al = jnp.pad(w_in[:, 2048:2048 + GLA_RANK], ((0, 0), (0, LANE - GLA_RANK)))
    w_g = w_in[:, 2048 + GLA_RANK:]
    za = _mm(u, w_a, name="in_a")
    zg = _mm(u, w_g, name="in_g")
    al = _mm(u, w_al, name="in_al")
    ar, ai, bbar_re, bbar_im = _s5_discretize(f('s5_lambda_re'), f('s5_lambda_im'), f('s5_log_dt'), f('s5_b_re'), f('s5_b_im'))
    b_blk = jnp.concatenate([_block_diag(bbar_re.transpose(0, 2, 1)), _block_diag(bbar_im.transpose(0, 2, 1))], axis=1)
    c_blk = jnp.concatenate([_block_diag(f('s5_c_re').transpose(0, 2, 1)), -_block_diag(f('s5_c_im').transpose(0, 2, 1))], axis=0)
    b_blk, c_blk = b_blk.astype(BF16), c_blk.astype(BF16)
    ar8 = jnp.broadcast_to(ar.reshape(1, S5_GP), (SEG, S5_GP))
    ai8 = jnp.broadcast_to(ai.reshape(1, S5_GP), (SEG, S5_GP))
    pw_r, pw_i = _segment_powers(ar, ai, n // SEG)
    dskip = f('s5_d').reshape(1, S5_W)
    u_s5 = _permute_rows(za[:, :S5_W])
    bu = _mm(u_s5, b_blk, name="s5_bu")
    xs = _s5_scan(bu, ar8, ai8, pw_r, pw_i, "s5_scan")
    ys_p = _mm(xs, c_blk, res=_scale_rows(u_s5, dskip, "s5_skip"), name="s5_y")
    ys = _unpermute_rows(ys_p)
    zgelu = _gelu_fwd(ys, "s5_gelu")
    t_glu = _mm(zgelu, f('s5_glu_w'), bias=f('s5_glu_b').reshape(1, S5_W), name="s5_glu_t")
    y_s5 = _glu_fwd(zgelu, t_glu, "s5_glu")
    q, k = _heads(za[:, 512:768], GLA_DK), _heads(za[:, 768:1024], GLA_DK)
    v, r = _heads(za[:, 1024:1536], GLA_DV), _heads(za[:, 1536:2048], GLA_DV)
    wup = jnp.pad(f('gla_a_up_w'), ((0, LANE - GLA_RANK), (0, 0)))
    wup_h = wup.reshape(LANE, GLA_HEADS, GLA_DK).transpose(1, 0, 2)
    bup_h = f('gla_a_up_b').reshape(GLA_HEADS, 1, GLA_DK)
    gn_h = f('gla_out_norm').reshape(GLA_HEADS, 1, GLA_DV)
    y_gla_h, s_prev = _gla_fwd(q, k, v, r, al, wup_h, bup_h, gn_h, "gla_fwd")
    y_gla = _unheads(y_gla_h).astype(BF16)
    ps = _mm(y_s5, f('proj_s5'), name="proj_s5")
    pg = _mm(y_gla, f('proj_gla'), name="proj_gla")
    merged = _merge_fwd(zg, ps, pg, "merge")
    h2 = _mm(merged, f('w_out'), res=h1, name="w_out")
    h3, ffn2 = _ffn_fwd(h2, f('ffn2_norm'), f('ffn2_w1'), f('ffn2_w3'), f('ffn2_w2'), "ffn2")
    loss, dh3, g_final = _final_loss(h3, f('final_norm').reshape(1, D_MODEL), tgt, "loss")
    grads = {'final_norm': g_final.reshape(D_MODEL)}
    dh2, grads['ffn2_norm'], grads['ffn2_w1'], grads['ffn2_w3'], grads['ffn2_w2'] = _ffn_bwd(
        dh3, ffn2, f('ffn2_norm'), f('ffn2_w1'), f('ffn2_w3'), f('ffn2_w2'), "ffn2")
    dh2b = dh2.astype(BF16)
    dm = _mm(dh2b, f('w_out'), tb=True, name="d_merged")
    grads['w_out'] = _mm(merged, dh2b, ta=True, name="g_w_out")
    dps, dpg, dzg = _merge_bwd(dm, zg, ps, pg, "d_merge")
    grads['proj_s5'] = _mm(y_s5, dps, ta=True, name="g_proj_s5")
    grads['proj_gla'] = _mm(y_gla, dpg, ta=True, name="g_proj_gla")
    dy_s5 = _mm(dps, f('proj_s5'), tb=True, name="d_y_s5")
    dy_gla = _mm(dpg, f('proj_gla'), tb=True, name="d_y_gla")
    dzgelu, dt_glu, g_glu_b = _glu_bwd1(dy_s5, zgelu, t_glu, "d_glu")
    grads['s5_glu_b'] = g_glu_b.reshape(S5_W)
    grads['s5_glu_w'] = _mm(zgelu, dt_glu, ta=True, name="g_glu_w")
    dzgelu = _mm(dt_glu, f('s5_glu_w'), tb=True, res=dzgelu, name="d_gelu")
    dys, du_skip, g_d = _glu_bwd2(_permute_rows(dzgelu), ys_p, u_s5, dskip, "d_s5_y")
    grads['s5_d'] = g_d.reshape(S5_G, S5_H)
    gx = _mm(dys, c_blk, tb=True, name="s5_gx")
    lam, da8 = _s5_scan_bwd(gx, xs, ar8, ai8, pw_r, pw_i, "s5_scan_bwd")
    g_c = _mm(dys, xs, ta=True, name="g_s5_c")
    grads['s5_c_re'] = _diag_blocks(g_c[:, :S5_GP], S5_H, S5_P)
    grads['s5_c_im'] = -_diag_blocks(g_c[:, S5_GP:], S5_H, S5_P)
    g_b = _mm(lam, u_s5, ta=True, name="g_s5_b")
    g_bbar_re = _diag_blocks(g_b[:S5_GP], S5_P, S5_H)
    g_bbar_im = _diag_blocks(g_b[S5_GP:], S5_P, S5_H)
    da = jnp.sum(da8, axis=0)
    g_ar, g_ai = da[:S5_GP].reshape(S5_G, S5_P), da[S5_GP:].reshape(S5_G, S5_P)
    _, disc_vjp = jax.vjp(_s5_discretize, f('s5_lambda_re'), f('s5_lambda_im'), f('s5_log_dt'), f('s5_b_re'), f('s5_b_im'))
    (grads['s5_lambda_re'], grads['s5_lambda_im'], grads['s5_log_dt'], grads['s5_b_re'],
     grads['s5_b_im']) = disc_vjp((g_ar, g_ai, g_bbar_re, g_bbar_im))
    du_s5 = _unpermute_rows(_mm(lam, b_blk, tb=True, res=du_skip, name="d_s5_u"))
    dq, dk, dv, dr, dz, dgn, dbup = _gla_bwd(q, k, v, r, al, wup_h, bup_h, gn_h, s_prev, _heads(dy_gla, GLA_DV), "gla_bwd")
    grads['gla_out_norm'] = dgn.reshape(GLA_HEADS * GLA_DV)
    grads['gla_a_up_b'] = dbup.reshape(GLA_HEADS * GLA_DK)
    dz = _unheads(dz)
    grads['gla_a_up_w'] = _mm(al, dz, ta=True, name="g_a_up")[:GLA_RANK]
    dal = _mm(dz, wup, tb=True, name="d_a_low")
    dza = jnp.concatenate([du_s5, _unheads(dq), _unheads(dk), _unheads(dv), _unheads(dr)], axis=1)
    g_wa = _mm(u, dza, ta=True, name="g_in_a")
    g_wg = _mm(u, dzg, ta=True, name="g_in_g")
    g_wal = _mm(u, dal, ta=True, name="g_in_al")
    grads['w_in'] = jnp.concatenate([g_wa, g_wal[:, :GLA_RANK], g_wg], axis=1)
    du = _mm(dza, w_a, tb=True, name="d_u_a")
    du = _mm(dzg, w_g, tb=True, res=du, name="d_u_g")
    du = _mm(dal, w_al, tb=True, res=du, name="d_u_al")
    dh1, g_mix = _rms_bwd(h1, f('mix_norm'), du, dh2, "d_mix_rms")
    grads['mix_norm'] = g_mix
    dx, grads['ffn1_norm'], grads['ffn1_w1'], grads['ffn1_w3'], grads['ffn1_w2'] = _ffn_bwd(
        dh1, ffn1, f('ffn1_norm'), f('ffn1_w1'), f('ffn1_w3'), f('ffn1_w2'), "ffn1")
    return loss[0, 0], dx, grads


def _train_step(a):
    x = a['x'][0]
    tgt = a['loss_target'][0]
    xi, yi, ci = lax.axis_index("x"), lax.axis_index("y"), lax.axis_index("c")
    c_arr = jnp.reshape(ci, (1,)).astype(jnp.int32)
    s_arr = jnp.reshape(2 * xi + yi, (1,)).astype(jnp.int32)
    gathered_names = SHARDED + ['gla_a_up_w']
    shards = [a[nm][0].astype(F32 if nm == 'gla_a_up_w' else BF16) for nm in gathered_names]
    gathered = dict(zip(gathered_names, _gather_shards(shards, "gather_weights")))
    p = {}
    for nm in gathered_names:
        g4 = gathered[nm]
        if nm in FFN_WEIGHTS:
            p[nm] = g4
        elif nm in COL_SHARDED:
            p[nm] = jnp.concatenate([g4[s] for s in range(4)], axis=1)
        else:
            p[nm] = g4.reshape(4 * g4.shape[1], g4.shape[2])
    for nm in SMALL:
        if nm != 'gla_a_up_w':
            p[nm] = a[nm] if nm == 'final_norm' else a[nm][0]
    loss, dx, grads = _local_step(x, tgt, p)
    loss = lax.psum(loss, ("x", "y", "c"))
    g4s = []
    for nm in SHARDED:
        g = grads[nm]
        if nm in FFN_WEIGHTS:
            g4s.append(g)
        elif nm in COL_SHARDED:
            g4s.append(jnp.stack(jnp.split(g, 4, axis=1)))
        else:
            g4s.append(g.reshape(4, g.shape[0] // 4, g.shape[1]))
    from_sibling = _swap_halves(g4s, "grad_swap_halves")
    chip_sums = [_chip_sum(g, r, c_arr, f"chip_sum_{nm}") for nm, g, r in zip(SHARDED, g4s, from_sibling)]
    from_chips = _scatter_chip_sums(chip_sums, "grad_scatter")
    halves = [_owner_sum(s, o, s_arr, f"owner_sum_{nm}") for nm, s, o in zip(SHARDED, chip_sums, from_chips)]
    sib_halves = _swap_reduced(halves, "grad_swap_reduced")
    red = {}
    small_parts = [grads[nm].reshape(a[nm].shape) for nm in SMALL if nm != 'gla_a_up_w'] + [grads['gla_a_up_w']]
    small_sum = _unpack_small(_allreduce_small(_pack_small(small_parts), "allreduce_small"), small_parts)
    small_names = [nm for nm in SMALL if nm != 'gla_a_up_w']
    for nm, g in zip(small_names, small_sum[:-1]):
        red[nm] = g
    g_up = small_sum[-1]
    red['gla_a_up_w'] = lax.dynamic_slice(g_up, (0, (2 * xi + yi) * GLA_DK), (GLA_RANK, GLA_DK))
    out_g, out_d, out_m, out_v = {}, {}, {}, {}
    for nm, own, sib in zip(SHARDED, halves, sib_halves):
        g, d, nm_, nv_ = _adamw_halves(a[nm][0], own, sib, a['m_' + nm][0], a['v_' + nm][0], c_arr, f"adamw_{nm}")
        shape = a[nm].shape
        out_g[nm], out_d[nm], out_m[nm], out_v[nm] = (t.reshape(shape) for t in (g, d, nm_, nv_))
    rest = [nm for nm in WEIGHTS if nm not in SHARDED]
    pk = lambda pre: _pack_small([a[pre + nm] for nm in rest])
    d, nm_, nv_ = _adamw(pk(''), _pack_small([red[nm] for nm in rest]), pk('m_'), pk('v_'), "adamw_small")
    like = [a[nm] for nm in rest]
    for nm, g, dd, mm_, vv_ in zip(rest, [red[nm].reshape(a[nm].shape) for nm in rest], _unpack_small(d, like),
                                   _unpack_small(nm_, like), _unpack_small(nv_, like)):
        out_g[nm], out_d[nm], out_m[nm], out_v[nm] = g, dd, mm_, vv_
    return (loss, dx[None], *[out_g[nm] for nm in WEIGHTS], *[out_d[nm] for nm in WEIGHTS],
            *[out_m[nm] for nm in WEIGHTS], *[out_v[nm] for nm in WEIGHTS])


def kernel(x, ffn1_norm, ffn1_w1, ffn1_w3, ffn1_w2, mix_norm, w_in, s5_lambda_re, s5_lambda_im, s5_log_dt, s5_b_re, s5_b_im, s5_c_re, s5_c_im, s5_d, s5_glu_w, s5_glu_b, gla_a_up_w, gla_a_up_b, gla_out_norm, proj_s5, proj_gla, w_out, ffn2_norm, ffn2_w1, ffn2_w3, ffn2_w2, final_norm, loss_target, m_ffn1_norm, m_ffn1_w1, m_ffn1_w3, m_ffn1_w2, m_mix_norm, m_w_in, m_s5_lambda_re, m_s5_lambda_im, m_s5_log_dt, m_s5_b_re, m_s5_b_im, m_s5_c_re, m_s5_c_im, m_s5_d, m_s5_glu_w, m_s5_glu_b, m_gla_a_up_w, m_gla_a_up_b, m_gla_out_norm, m_proj_s5, m_proj_gla, m_w_out, m_ffn2_norm, m_ffn2_w1, m_ffn2_w3, m_ffn2_w2, m_final_norm, v_ffn1_norm, v_ffn1_w1, v_ffn1_w3, v_ffn1_w2, v_mix_norm, v_w_in, v_s5_lambda_re, v_s5_lambda_im, v_s5_log_dt, v_s5_b_re, v_s5_b_im, v_s5_c_re, v_s5_c_im, v_s5_d, v_s5_glu_w, v_s5_glu_b, v_gla_a_up_w, v_gla_a_up_b, v_gla_out_norm, v_proj_s5, v_proj_gla, v_w_out, v_ffn2_norm, v_ffn2_w1, v_ffn2_w3, v_ffn2_w2, v_final_norm):
    return _train_step(dict(locals()))
```

```python
import functools

import jax
import jax.numpy as jnp
from jax import lax
from jax.experimental import pallas as pl
from jax.experimental.pallas import tpu as pltpu

F32 = jnp.float32
BF16 = jnp.bfloat16
HI = lax.Precision.HIGHEST
MESH_ID = pl.DeviceIdType.MESH

D_MODEL = 1024
EPS = 1e-6
S5_G, S5_P, S5_H = 32, 64, 16
S5_W = S5_G * S5_H
S5_GP = S5_G * S5_P
SEG = 8
SCAN_ROWS = 256
GLA_HEADS, GLA_DK, GLA_DV = 4, 64, 128
GLA_CHUNK = 64
GLA_TAU = 16.0
GLA_RANK = 16
ADAM_LR, ADAM_B1, ADAM_B2, ADAM_EPS, ADAM_WD, ADAM_STEP = 0.001, 0.9, 0.999, 1e-08, 0.01, 10
V7X_VMEM_LIMIT = 56 * 1024 * 1024
LANE = 128

WEIGHTS = ['ffn1_norm', 'ffn1_w1', 'ffn1_w3', 'ffn1_w2', 'mix_norm', 'w_in', 's5_lambda_re', 's5_lambda_im',
           's5_log_dt', 's5_b_re', 's5_b_im', 's5_c_re', 's5_c_im', 's5_d', 's5_glu_w', 's5_glu_b', 'gla_a_up_w',
           'gla_a_up_b', 'gla_out_norm', 'proj_s5', 'proj_gla', 'w_out', 'ffn2_norm', 'ffn2_w1', 'ffn2_w3',
           'ffn2_w2', 'final_norm']
SHARDED = ['ffn1_w1', 'ffn1_w3', 'ffn1_w2', 'w_in', 's5_glu_w', 'proj_s5', 'proj_gla', 'w_out',
           'ffn2_w1', 'ffn2_w3', 'ffn2_w2']
COL_SHARDED = ['ffn1_w1', 'ffn1_w3', 'w_in', 'proj_s5', 'proj_gla', 'ffn2_w1', 'ffn2_w3', 'gla_a_up_w']
SMALL = [n for n in WEIGHTS if n not in SHARDED]
FFN_WEIGHTS = ['ffn1_w1', 'ffn1_w3', 'ffn1_w2', 'ffn2_w1', 'ffn2_w3', 'ffn2_w2']


def _params(**kw):
    return pltpu.CompilerParams(vmem_limit_bytes=V7X_VMEM_LIMIT, **kw)


class _Side:
    def __init__(self, ins, out_shapes, nsem, copies, aliased=False):
        self.ins, self.out_shapes, self.nsem, self.copies, self.aliased = list(ins), list(out_shapes), nsem, copies, aliased
        self.outs = None


_RIDER = []


def _pcall(body, **kw):
    if _RIDER:
        return _carry(body, _RIDER.pop(), **kw)
    return pl.pallas_call(body, **kw)


def _carry(body, side, *, name, grid, in_specs, out_specs, out_shape, scratch_shapes=(), compiler_params=None):
    del compiler_params
    single = not isinstance(out_shape, (list, tuple))
    out_specs = [out_specs] if single else list(out_specs)
    out_shape = [out_shape] if single else list(out_shape)
    n_in, n_out, n_scr = len(in_specs), len(out_shape), len(scratch_shapes)
    s_in, s_out = len(side.ins), len(side.out_shapes)
    any_spec = pl.BlockSpec(memory_space=pl.ANY)

    def wrapped(*refs):
        cuts = [n_in, s_in, n_out, s_out, n_scr]
        parts, pos = [], 0
        for c in cuts:
            parts.append(refs[pos:pos + c])
            pos += c
        ins, sins, outs, souts, scr = parts
        ssem, rsem = refs[pos], refs[pos + 1]
        first = last = None
        for d, g in enumerate(grid):
            i = pl.program_id(d)
            first = (i == 0) if first is None else first & (i == 0)
            last = (i == g - 1) if last is None else last & (i == g - 1)

        @pl.when(first)
        def _():
            for cp in side.copies(sins, souts, ssem, rsem):
                cp.start()

        body(*ins, *outs, *scr)

        @pl.when(last)
        def _():
            for cp in side.copies(sins, souts, ssem, rsem):
                cp.wait()

    call = pl.pallas_call(
        wrapped, name=name, grid=grid, in_specs=list(in_specs) + [any_spec] * s_in,
        out_specs=out_specs + [any_spec] * s_out, out_shape=out_shape + side.out_shapes,
        scratch_shapes=list(scratch_shapes) + [pltpu.SemaphoreType.DMA((side.nsem,)), pltpu.SemaphoreType.DMA((side.nsem,))],
        input_output_aliases={n_in + j: n_out + j for j in range(s_in)} if side.aliased else {},
        compiler_params=_params(has_side_effects=True))

    def run(*args):
        res = call(*args, *side.ins)
        side.outs = list(res[n_out:])
        return res[0] if single else list(res[:n_out])

    return run


def _run_side(side, name):
    s_in, s_out = len(side.ins), len(side.out_shapes)
    any_spec = pl.BlockSpec(memory_space=pl.ANY)

    def body(*refs):
        sins, souts = refs[:s_in], refs[s_in:s_in + s_out]
        ssem, rsem = refs[s_in + s_out:]
        cps = side.copies(sins, souts, ssem, rsem)
        for cp in cps:
            cp.start()
        for cp in cps:
            cp.wait()

    side.outs = list(pl.pallas_call(
        body, name=name, in_specs=[any_spec] * s_in, out_specs=[any_spec] * s_out, out_shape=side.out_shapes,
        scratch_shapes=[pltpu.SemaphoreType.DMA((side.nsem,)), pltpu.SemaphoreType.DMA((side.nsem,))],
        input_output_aliases={j: j for j in range(s_in)} if side.aliased else {},
        compiler_params=pltpu.CompilerParams(has_side_effects=True))(*side.ins))
    return side.outs


def _pick(n, cap, quantum):
    if n <= cap:
        return n
    best = None
    for t in range(quantum, cap + 1, quantum):
        if n % t == 0:
            best = t
    assert best is not None, (n, cap, quantum)
    return best


def _sigmoid(x):
    return jax.nn.sigmoid(x)


def _mm(a, b, *, name, ta=False, tb=False, out_dtype=F32, alpha=1.0, res=None, bias=None, exact=False, shard=None):
    ns = 4
    (k_a, m) = a.shape[-2:] if ta else a.shape[-2:][::-1]
    (k_b, n) = b.shape[-2:][::-1] if tb else b.shape[-2:]
    assert k_a == k_b, (a.shape, b.shape, ta, tb)
    assert (a.ndim == 3) == (shard in ('k', 'm')) and (b.ndim == 3) == (shard in ('n', 'k'))
    k = k_a
    tm = _pick(m, 1024, 128)
    tn = _pick(n, 1024, 128)
    tk = _pick(k, 1024, 128)
    pm, pn, pk = m // tm, n // tn, k // tk
    gm = pm * (ns if shard == 'm' else 1)
    gn = pn * (ns if shard == 'n' else 1)
    gk = pk * (ns if shard == 'k' else 1)
    dims = (((0,) if ta else (1,), (1,) if tb else (0,)), ((), ()))
    op_dtype = F32 if exact else BF16

    def body(*refs):
        a_ref, b_ref = refs[0], refs[1]
        pos = 2
        res_ref = bias_ref = None
        if res is not None:
            res_ref = refs[pos]
            pos += 1
        if bias is not None:
            bias_ref = refs[pos]
            pos += 1
        o_ref, acc_ref = refs[pos], refs[pos + 1]
        kk = pl.program_id(2)

        @pl.when(kk == 0)
        def _():
            acc_ref[...] = jnp.zeros_like(acc_ref)

        acc_ref[...] += lax.dot_general(a_ref[...].astype(op_dtype), b_ref[...].astype(op_dtype), dims,
                                        precision=HI if exact else None, preferred_element_type=F32)

        @pl.when(kk == gk - 1)
        def _():
            o = acc_ref[...]
            if alpha != 1.0:
                o = o * alpha
            if bias_ref is not None:
                o = o + bias_ref[...]
            if res_ref is not None:
                o = o + res_ref[...]
            o_ref[...] = o.astype(out_dtype)

    def spec(block, sharded_on, order):
        per = {'m': pm, 'n': pn, 'k': pk}

        def index(i, j, kk):
            g = {'m': i, 'n': j, 'k': kk}
            r, c = order(i % pm if shard == 'm' else i, j % pn if shard == 'n' else j, kk % pk if shard == 'k' else kk)
            if sharded_on is None:
                return (r, c)
            return (g[sharded_on] // per[sharded_on], r, c)

        return pl.BlockSpec(block if sharded_on is None else (None,) + block, index)

    a_sh = shard if shard in ('k', 'm') else None
    b_sh = shard if shard in ('n', 'k') else None
    o_sh = shard if shard in ('n', 'm') else None
    a_spec = spec((tk, tm), a_sh, lambda i, j, kk: (kk, i)) if ta else spec((tm, tk), a_sh, lambda i, j, kk: (i, kk))
    b_spec = spec((tn, tk), b_sh, lambda i, j, kk: (j, kk)) if tb else spec((tk, tn), b_sh, lambda i, j, kk: (kk, j))
    ins, in_specs = [a, b], [a_spec, b_spec]
    if res is not None:
        assert o_sh is None
        ins.append(res)
        in_specs.append(pl.BlockSpec((tm, tn), lambda i, j, kk: (i, j)))
    if bias is not None:
        assert o_sh is None
        ins.append(bias)
        in_specs.append(pl.BlockSpec((1, tn), lambda i, j, kk: (0, j)))
    out_shape = (m, n) if o_sh is None else (ns, m, n)
    return _pcall(body, name=name, grid=(gm, gn, gk), in_specs=in_specs,
                  out_specs=spec((tm, tn), o_sh, lambda i, j, kk: (i, j)),
                  out_shape=jax.ShapeDtypeStruct(out_shape, out_dtype),
                  scratch_shapes=[pltpu.VMEM((tm, tn), F32)], compiler_params=_params())(*ins)


def _rows(body, ins, outs, *, n, name, tm=256):
    tm = _pick(n, tm, 16)
    in_specs = []
    for arr, kind in ins:
        if kind == 'r':
            in_specs.append(pl.BlockSpec((tm, arr.shape[1]), lambda i: (i, 0)))
        else:
            in_specs.append(pl.BlockSpec(arr.shape, lambda i: (0, 0)))
    out_specs, out_shape = [], []
    for cols, dtype, kind in outs:
        if kind == 'r':
            out_specs.append(pl.BlockSpec((tm, cols), lambda i: (i, 0)))
            out_shape.append(jax.ShapeDtypeStruct((n, cols), dtype))
        else:
            out_specs.append(pl.BlockSpec((1, cols), lambda i: (0, 0)))
            out_shape.append(jax.ShapeDtypeStruct((1, cols), dtype))
    n_in = len(ins)
    acc_ids = [j for j, o in enumerate(outs) if o[2] == 'a']

    def wrapped(*refs):
        if acc_ids:
            @pl.when(pl.program_id(0) == 0)
            def _():
                for j in acc_ids:
                    refs[n_in + j][...] = jnp.zeros_like(refs[n_in + j])
        body(*refs)

    res = _pcall(wrapped, name=name, grid=(n // tm,), in_specs=in_specs, out_specs=out_specs, out_shape=out_shape,
                 compiler_params=_params())(*[a for a, _ in ins])
    return res


def _rms_fwd(x, g, name):
    def body(x_ref, g_ref, o_ref):
        xv = x_ref[...]
        rstd = lax.rsqrt(jnp.mean(xv * xv, axis=-1, keepdims=True) + EPS)
        o_ref[...] = (xv * rstd * g_ref[...]).astype(BF16)
    return _rows(body, [(x, 'r'), (g, 'f')], [(x.shape[1], BF16, 'r')], n=x.shape[0], name=name)[0]


def _rms_bwd(x, g, dn, dres, name):
    def body(x_ref, g_ref, dn_ref, dres_ref, dx_ref, dg_ref):
        xv = x_ref[...]
        rstd = lax.rsqrt(jnp.mean(xv * xv, axis=-1, keepdims=True) + EPS)
        xh = xv * rstd
        dn = dn_ref[...]
        dg_ref[...] += jnp.sum(dn * xh, axis=0, keepdims=True)
        dxh = dn * g_ref[...]
        dx_ref[...] = dres_ref[...] + rstd * (dxh - xh * jnp.mean(dxh * xh, axis=-1, keepdims=True))
    d = x.shape[1]
    return _rows(body, [(x, 'r'), (g, 'f'), (dn, 'r'), (dres, 'r')], [(d, F32, 'r'), (d, F32, 'a')],
                 n=x.shape[0], name=name)


def _gelu_parts(y):
    c0 = 0.7978845608028654
    inner = c0 * (y + 0.044715 * y * y * y)
    th = jnp.tanh(inner)
    return th, c0 * (1.0 + 3.0 * 0.044715 * y * y)


def _gelu_fwd(y, name):
    def body(y_ref, o_ref):
        yv = y_ref[...]
        th, _ = _gelu_parts(yv)
        o_ref[...] = 0.5 * yv * (1.0 + th)
    return _rows(body, [(y, 'r')], [(y.shape[1], F32, 'r')], n=y.shape[0], name=name)[0]


def _glu_fwd(zg, t, name):
    def body(z_ref, t_ref, o_ref):
        o_ref[...] = (z_ref[...] * _sigmoid(t_ref[...])).astype(BF16)
    return _rows(body, [(zg, 'r'), (t, 'r')], [(zg.shape[1], BF16, 'r')], n=zg.shape[0], name=name)[0]


def _glu_bwd1(dy, zg, t, name):
    def body(dy_ref, z_ref, t_ref, dz_ref, dt_ref, db_ref):
        dyv, zv = dy_ref[...], z_ref[...]
        sg = _sigmoid(t_ref[...])
        dz_ref[...] = dyv * sg
        dt = dyv * zv * sg * (1.0 - sg)
        dt_ref[...] = dt.astype(BF16)
        db_ref[...] += jnp.sum(dt, axis=0, keepdims=True)
    w = zg.shape[1]
    return _rows(body, [(dy, 'r'), (zg, 'r'), (t, 'r')], [(w, F32, 'r'), (w, BF16, 'r'), (w, F32, 'a')],
                 n=zg.shape[0], name=name)


def _glu_bwd2(dzg, ys, u, dskip, name):
    def body(dz_ref, y_ref, u_ref, d_ref, dy_ref, du_ref, dd_ref):
        yv = y_ref[...]
        th, dinner = _gelu_parts(yv)
        dy = dz_ref[...] * (0.5 * (1.0 + th) + 0.5 * yv * (1.0 - th * th) * dinner)
        dy_ref[...] = dy
        du_ref[...] = dy * d_ref[...]
        dd_ref[...] += jnp.sum(dy * u_ref[...], axis=0, keepdims=True)
    w = ys.shape[1]
    return _rows(body, [(dzg, 'r'), (ys, 'r'), (u, 'r'), (dskip, 'f')], [(w, F32, 'r'), (w, F32, 'r'), (w, F32, 'a')],
                 n=ys.shape[0], name=name)


def _scale_rows(u, dskip, name):
    def body(u_ref, d_ref, o_ref):
        o_ref[...] = u_ref[...] * d_ref[...]
    return _rows(body, [(u, 'r'), (dskip, 'f')], [(u.shape[1], F32, 'r')], n=u.shape[0], name=name)[0]


def _merge_fwd(zg, ps, pg, name):
    def body(z_ref, ps_ref, pg_ref, o_ref):
        zv = z_ref[...]
        o_ref[...] = (_sigmoid(zv[:, :D_MODEL]) * ps_ref[...] + _sigmoid(zv[:, D_MODEL:]) * pg_ref[...]).astype(BF16)
    return _rows(body, [(zg, 'r'), (ps, 'r'), (pg, 'r')], [(D_MODEL, BF16, 'r')], n=zg.shape[0], name=name)[0]


def _merge_bwd(dm, zg, ps, pg, name):
    def body(dm_ref, z_ref, ps_ref, pg_ref, dps_ref, dpg_ref, dz_ref):
        dmv, zv = dm_ref[...], z_ref[...]
        s1, s2 = _sigmoid(zv[:, :D_MODEL]), _sigmoid(zv[:, D_MODEL:])
        dps_ref[...] = (dmv * s1).astype(BF16)
        dpg_ref[...] = (dmv * s2).astype(BF16)
        dz_ref[:, :D_MODEL] = dmv * ps_ref[...] * s1 * (1.0 - s1)
        dz_ref[:, D_MODEL:] = dmv * pg_ref[...] * s2 * (1.0 - s2)
    return _rows(body, [(dm, 'r'), (zg, 'r'), (ps, 'r'), (pg, 'r')],
                 [(D_MODEL, BF16, 'r'), (D_MODEL, BF16, 'r'), (2 * D_MODEL, F32, 'r')], n=zg.shape[0], name=name)


def _final_loss(h, g, tgt, name):
    def body(h_ref, g_ref, t_ref, loss_ref, dh_ref, dg_ref):
        hv = h_ref[...]
        rstd = lax.rsqrt(jnp.mean(hv * hv, axis=-1, keepdims=True) + EPS)
        xh = hv * rstd
        err = xh * g_ref[...] - t_ref[...]
        part = 0.5 * jnp.sum(jnp.mean(err * err, axis=-1, keepdims=True), axis=0, keepdims=True)
        loss_ref[...] += jnp.broadcast_to(part, loss_ref.shape)
        dout = err * (1.0 / hv.shape[1])
        dg_ref[...] += jnp.sum(dout * xh, axis=0, keepdims=True)
        dxh = dout * g_ref[...]
        dh_ref[...] = rstd * (dxh - xh * jnp.mean(dxh * xh, axis=-1, keepdims=True))
    d = h.shape[1]
    return _rows(body, [(h, 'r'), (g, 'f'), (tgt, 'r')], [(LANE, F32, 'a'), (d, F32, 'r'), (d, F32, 'a')],
                 n=h.shape[0], name=name)


def _adamw_math(wv, gv, mv, vv):
    nm = ADAM_B1 * mv + (1.0 - ADAM_B1) * gv
    nv = ADAM_B2 * vv + (1.0 - ADAM_B2) * (gv * gv)
    m_hat = nm / (1.0 - ADAM_B1 ** ADAM_STEP)
    v_hat = nv / (1.0 - ADAM_B2 ** ADAM_STEP)
    return -ADAM_LR * (m_hat / (jnp.sqrt(v_hat) + ADAM_EPS) + ADAM_WD * wv), nm, nv


def _adamw(w, g, m, v, name):
    def body(w_ref, g_ref, m_ref, v_ref, d_ref, nm_ref, nv_ref):
        d_ref[...], nm_ref[...], nv_ref[...] = _adamw_math(w_ref[...], g_ref[...], m_ref[...], v_ref[...])
    c = w.shape[1]
    return _rows(body, [(w, 'r'), (g, 'r'), (m, 'r'), (v, 'r')], [(c, F32, 'r')] * 3, n=w.shape[0], name=name)


def _adamw_halves(w, g_own, g_sib, m, v, c_arr, name):
    r, cols = w.shape
    h = r // 2
    tr = _pick(h, 256, 8)
    per = h // tr

    def body(c_ref, w_ref, go_ref, gs_ref, m_ref, v_ref, g_ref, d_ref, nm_ref, nv_ref):
        mine = (pl.program_id(0) // per) == c_ref[0]
        gv = jnp.where(mine, go_ref[...], gs_ref[...])
        g_ref[...] = gv
        d_ref[...], nm_ref[...], nv_ref[...] = _adamw_math(w_ref[...], gv, m_ref[...], v_ref[...])

    full = pl.BlockSpec((tr, cols), lambda i, c_ref: (i, 0))
    half = pl.BlockSpec((tr, cols), lambda i, c_ref: (i % per, 0))
    grid_spec = pltpu.PrefetchScalarGridSpec(num_scalar_prefetch=1, grid=(2 * per,),
                                             in_specs=[full, half, half, full, full], out_specs=[full] * 4)
    return _pcall(body, name=name, grid_spec=grid_spec, out_shape=[jax.ShapeDtypeStruct((r, cols), F32)] * 4,
                  compiler_params=_params())(c_arr, w, g_own, g_sib, m, v)


def _shift_rows(v, sh, down):
    rolled = pltpu.roll(v, sh if down else v.shape[0] - sh, axis=0)
    row = lax.broadcasted_iota(jnp.int32, v.shape, 0)
    keep = (row >= sh) if down else (row < v.shape[0] - sh)
    return jnp.where(keep, rolled, 0.0)


def _chain_segments(st_r, st_i, pw_r_ref, pw_i_ref, conj, down):
    vr, vi = st_r[...], st_i[...]
    sh, k = 1, 0
    while sh < SEG:
        pr, pi = pw_r_ref[k:k + 1, :], pw_i_ref[k:k + 1, :]
        if conj:
            pi = -pi
        sr, si = _shift_rows(vr, sh, down), _shift_rows(vi, sh, down)
        vr, vi = vr + pr * sr - pi * si, vi + pr * si + pi * sr
        sh, k = sh * 2, k + 1
    st_r[...] = _shift_rows(vr, 1, down)
    st_i[...] = _shift_rows(vi, 1, down)


def _s5_scan(bu, ar8, ai8, pw_r, pw_i, name):
    n = bu.shape[0]
    rb = SCAN_ROWS
    nb, steps, lc = n // rb, rb // SEG, 512

    def body(bu_ref, ar_ref, ai_ref, pwr_ref, pwi_ref, x_ref, st_r, st_i):
        ph, b = pl.program_id(0), pl.program_id(1)

        @pl.when((ph == 0) & (b == 0))
        def _():
            st_r[...] = jnp.zeros_like(st_r)
            st_i[...] = jnp.zeros_like(st_i)

        def scan(store):
            for c in range(S5_GP // lc):
                re, im = slice(c * lc, (c + 1) * lc), slice(S5_GP + c * lc, S5_GP + (c + 1) * lc)
                a_r, a_i = ar_ref[:, re], ai_ref[:, re]

                def step(s, carry):
                    xr, xi = carry
                    rows = pl.ds(pl.multiple_of(s * SEG, SEG), SEG)
                    nr = a_r * xr - a_i * xi + bu_ref[rows, re]
                    ni = a_r * xi + a_i * xr + bu_ref[rows, im]
                    if store:
                        x_ref[rows, re] = nr
                        x_ref[rows, im] = ni
                    return nr, ni

                xr, xi = lax.fori_loop(0, steps, step, (st_r[:, re], st_i[:, re]), unroll=4)
                st_r[:, re] = xr
                st_i[:, re] = xi

        @pl.when(ph == 0)
        def _():
            scan(False)

        @pl.when((ph == 0) & (b == nb - 1))
        def _():
            _chain_segments(st_r, st_i, pwr_ref, pwi_ref, conj=False, down=True)

        @pl.when(ph == 1)
        def _():
            scan(True)

    full = lambda a: pl.BlockSpec(a.shape, lambda ph, b: (0, 0))
    return _pcall(body, name=name, grid=(2, nb),
                  in_specs=[pl.BlockSpec((rb, 2 * S5_GP), lambda ph, b: (b, 0)), full(ar8), full(ai8), full(pw_r), full(pw_i)],
                  out_specs=pl.BlockSpec((rb, 2 * S5_GP), lambda ph, b: (b * ph, 0)),
                  out_shape=jax.ShapeDtypeStruct((n, 2 * S5_GP), F32),
                  scratch_shapes=[pltpu.VMEM((SEG, S5_GP), F32), pltpu.VMEM((SEG, S5_GP), F32)],
                  compiler_params=_params())(bu, ar8, ai8, pw_r, pw_i)


def _s5_scan_bwd(gx, xs, ar8, ai8, pw_r, pw_i, name):
    n = gx.shape[0]
    rb = SCAN_ROWS
    nb, steps, lc = n // rb, rb // SEG, 256

    def body(gx_ref, x_ref, ar_ref, ai_ref, pwr_ref, pwi_ref, lam_ref, da_ref, st_r, st_i):
        ph, b = pl.program_id(0), pl.program_id(1)

        @pl.when((ph == 0) & (b == 0))
        def _():
            st_r[...] = jnp.zeros_like(st_r)
            st_i[...] = jnp.zeros_like(st_i)
            da_ref[...] = jnp.zeros_like(da_ref)

        def scan(store):
            for c in range(S5_GP // lc):
                re, im = slice(c * lc, (c + 1) * lc), slice(S5_GP + c * lc, S5_GP + (c + 1) * lc)
                a_r, a_i = ar_ref[:, re], ai_ref[:, re]

                def step(s, carry):
                    rows = pl.ds(pl.multiple_of((steps - 1 - s) * SEG, SEG), SEG)
                    if store:
                        lr, li, dr, di = carry
                        xr, xi = x_ref[rows, re], x_ref[rows, im]
                        dr = dr + lr * xr + li * xi
                        di = di + li * xr - lr * xi
                    else:
                        lr, li = carry
                    nr = a_r * lr + a_i * li + gx_ref[rows, re]
                    ni = a_r * li - a_i * lr + gx_ref[rows, im]
                    if store:
                        lam_ref[rows, re] = nr
                        lam_ref[rows, im] = ni
                        return nr, ni, dr, di
                    return nr, ni

                if store:
                    lr, li, dr, di = lax.fori_loop(0, steps, step, (st_r[:, re], st_i[:, re], da_ref[:, re], da_ref[:, im]),
                                                   unroll=4)
                    da_ref[:, re] = dr
                    da_ref[:, im] = di
                else:
                    lr, li = lax.fori_loop(0, steps, step, (st_r[:, re], st_i[:, re]), unroll=4)
                st_r[:, re] = lr
                st_i[:, re] = li

        @pl.when(ph == 0)
        def _():
            scan(False)

        @pl.when((ph == 0) & (b == nb - 1))
        def _():
            _chain_segments(st_r, st_i, pwr_ref, pwi_ref, conj=True, down=False)

        @pl.when(ph == 1)
        def _():
            scan(True)

    full = lambda a: pl.BlockSpec(a.shape, lambda ph, b: (0, 0))
    rev = lambda ph, b: (nb - 1 - b, 0)
    return _pcall(body, name=name, grid=(2, nb),
                  in_specs=[pl.BlockSpec((rb, 2 * S5_GP), rev), pl.BlockSpec((rb, 2 * S5_GP), lambda ph, b: ((nb - 1 - b) * ph, 0)),
                            full(ar8), full(ai8), full(pw_r), full(pw_i)],
                  out_specs=[pl.BlockSpec((rb, 2 * S5_GP), lambda ph, b: (nb - 1 - b * ph, 0)),
                             pl.BlockSpec((SEG, 2 * S5_GP), lambda ph, b: (0, 0))],
                  out_shape=[jax.ShapeDtypeStruct((n, 2 * S5_GP), F32), jax.ShapeDtypeStruct((SEG, 2 * S5_GP), F32)],
                  scratch_shapes=[pltpu.VMEM((SEG, S5_GP), F32), pltpu.VMEM((SEG, S5_GP), F32)],
                  compiler_params=_params())(gx, xs, ar8, ai8, pw_r, pw_i)


def _s5_discretize(lam_re, lam_im, log_dt, b_re, b_im):
    dt = jnp.exp(log_dt)[:, None]
    mag = jnp.exp(lam_re * dt)
    ar = mag * jnp.cos(lam_im * dt)
    ai = mag * jnp.sin(lam_im * dt)
    den = lam_re * lam_re + lam_im * lam_im
    nr = ar - 1.0
    fr = (nr * lam_re + ai * lam_im) / den
    fi = (ai * lam_re - nr * lam_im) / den
    bbar_re = fr[:, :, None] * b_re - fi[:, :, None] * b_im
    bbar_im = fr[:, :, None] * b_im + fi[:, :, None] * b_re
    return ar, ai, bbar_re, bbar_im


def _block_diag(t):
    g, a, b = t.shape
    eye = jnp.eye(g, dtype=t.dtype)
    return (t[:, :, None, :] * eye[:, None, :, None]).reshape(g * a, g * b)


def _diag_blocks(m, a, b):
    g = S5_G
    return jnp.einsum('gagb->gab', m.reshape(g, a, g, b))


def _permute_rows(t):
    n = t.shape[0]
    return t.reshape(SEG, n // SEG, t.shape[1]).transpose(1, 0, 2).reshape(n, t.shape[1])


def _unpermute_rows(t):
    n = t.shape[0]
    return t.reshape(n // SEG, SEG, t.shape[1]).transpose(1, 0, 2).reshape(n, t.shape[1])


def _segment_powers(ar, ai, seg_steps):
    pr, pi = ar.reshape(1, S5_GP), ai.reshape(1, S5_GP)
    e = 1
    while e < seg_steps:
        pr, pi = pr * pr - pi * pi, 2.0 * pr * pi
        e *= 2
    assert e == seg_steps, "segment length must be a power of two"
    rows_r, rows_i = [], []
    for _ in range(3):
        rows_r.append(pr)
        rows_i.append(pi)
        pr, pi = pr * pr - pi * pi, 2.0 * pr * pi
    pad = jnp.zeros((SEG - 3, S5_GP), F32)
    return jnp.concatenate(rows_r + [pad], axis=0), jnp.concatenate(rows_i + [pad], axis=0)


NT = (((1,), (1,)), ((), ()))
TN = (((0,), (0,)), ((), ()))


def _dot(a, b, dims=None, exact=False):
    dims = (((1,), (0,)), ((), ())) if dims is None else dims
    if exact:
        return lax.dot_general(a, b, dims, precision=HI, preferred_element_type=F32)
    return lax.dot_general(a.astype(BF16), b.astype(BF16), dims, preferred_element_type=F32)


def _gla_chunk_fwd(qc, kc, vc, al, wup, bup, s_prev, tril):
    z = _dot(al, wup) + bup
    la = (jnp.minimum(z, 0.0) - jnp.log(1.0 + jnp.exp(-jnp.abs(z)))) * (1.0 / GLA_TAU)
    bc = _dot(tril, la, exact=True)
    blb = _dot(la, jnp.ones((GLA_CHUNK, GLA_DV), F32), TN, exact=True)
    bl = bc[GLA_CHUNK - 1:GLA_CHUNK, :]
    ebc = jnp.exp(bc)
    qt = qc * (GLA_DK ** -0.5) * ebc
    kt = kc * jnp.exp(-bc)
    ke = kc * jnp.exp(bl - bc)
    sc = _dot(qt, kt, NT) * tril
    o = _dot(sc, vc) + _dot(qt, s_prev)
    return z, bc, bl, blb, ebc, qt, kt, ke, sc, o


GLA_ROWS = 512
GLA_CPB = GLA_ROWS // GLA_CHUNK


def _gla_in_specs(arrs, blk):
    specs = []
    for a in arrs:
        if a.ndim == 3 and a.shape[1] > LANE:
            specs.append(pl.BlockSpec((GLA_HEADS, GLA_ROWS, a.shape[2]), lambda j: (0, blk(j), 0)))
        elif a.ndim == 3:
            specs.append(pl.BlockSpec(a.shape, lambda j: (0, 0, 0)))
        else:
            specs.append(pl.BlockSpec((GLA_ROWS, a.shape[1]), lambda j: (blk(j), 0)))
    return specs


def _tri(lower):
    ri = lax.broadcasted_iota(jnp.int32, (GLA_CHUNK, GLA_CHUNK), 0)
    ci = lax.broadcasted_iota(jnp.int32, (GLA_CHUNK, GLA_CHUNK), 1)
    return ((ri >= ci) if lower else (ri <= ci)).astype(F32)


def _gla_fwd(q, k, v, r, al, wup, bup, gn, name):
    n = q.shape[1]
    nc = n // GLA_CHUNK

    def body(q_ref, k_ref, v_ref, r_ref, al_ref, wup_ref, bup_ref, gn_ref, y_ref, sp_ref, s_ref):
        @pl.when(pl.program_id(0) == 0)
        def _():
            s_ref[...] = jnp.zeros_like(s_ref)

        tril = _tri(True)

        def chunk(c, carry):
            rows = pl.ds(pl.multiple_of(c * GLA_CHUNK, GLA_CHUNK), GLA_CHUNK)
            alc = al_ref[rows, :]
            for h in range(GLA_HEADS):
                vc, rc, s_prev = v_ref[h, rows, :], r_ref[h, rows, :], s_ref[h]
                _, _, _, blb, _, _, _, ke, _, o = _gla_chunk_fwd(q_ref[h, rows, :], k_ref[h, rows, :], vc, alc,
                                                                  wup_ref[h], bup_ref[h], s_prev, tril)
                sp_ref[h, c] = s_prev
                rstd = lax.rsqrt(jnp.mean(o * o, axis=-1, keepdims=True) + EPS)
                y_ref[h, rows, :] = o * rstd * gn_ref[h] * (rc * _sigmoid(rc))
                s_ref[h] = jnp.exp(blb) * s_prev + _dot(ke, vc, TN)
            return carry

        lax.fori_loop(0, GLA_CPB, chunk, 0)

    ins = [q, k, v, r, al, wup, bup, gn]
    return _pcall(body, name=name, grid=(n // GLA_ROWS,), in_specs=_gla_in_specs(ins, lambda j: j),
                  out_specs=[pl.BlockSpec((GLA_HEADS, GLA_ROWS, GLA_DV), lambda j: (0, j, 0)),
                             pl.BlockSpec((GLA_HEADS, GLA_CPB, GLA_DK, GLA_DV), lambda j: (0, j, 0, 0))],
                  out_shape=[jax.ShapeDtypeStruct((GLA_HEADS, n, GLA_DV), F32),
                             jax.ShapeDtypeStruct((GLA_HEADS, nc, GLA_DK, GLA_DV), F32)],
                  scratch_shapes=[pltpu.VMEM((GLA_HEADS, GLA_DK, GLA_DV), F32)],
                  compiler_params=_params())(*ins)


def _gla_bwd(q, k, v, r, al, wup, bup, gn, sp, dy, name):
    n = q.shape[1]
    nc = n // GLA_CHUNK

    nb = n // GLA_ROWS

    def body(q_ref, k_ref, v_ref, r_ref, al_ref, wup_ref, bup_ref, gn_ref, dy_ref, sp_ref,
             dq_ref, dk_ref, dv_ref, dr_ref, dz_ref, dgn_ref, dbup_ref, ds_ref):
        @pl.when(pl.program_id(0) == 0)
        def _():
            ds_ref[...] = jnp.zeros_like(ds_ref)
            dgn_ref[...] = jnp.zeros_like(dgn_ref)
            dbup_ref[...] = jnp.zeros_like(dbup_ref)

        tril, triu = _tri(True), _tri(False)

        def chunk(i, carry):
            c = GLA_CPB - 1 - i
            rows = pl.ds(pl.multiple_of(c * GLA_CHUNK, GLA_CHUNK), GLA_CHUNK)
            alc = al_ref[rows, :]
            for h in range(GLA_HEADS):
                qc, kc, vc, rc = q_ref[h, rows, :], k_ref[h, rows, :], v_ref[h, rows, :], r_ref[h, rows, :]
                s_prev, ds = sp_ref[h, c], ds_ref[h]
                z, bc, bl, blb, ebc, qt, kt, ke, sc, o = _gla_chunk_fwd(qc, kc, vc, alc, wup_ref[h], bup_ref[h], s_prev, tril)
                rstd = lax.rsqrt(jnp.mean(o * o, axis=-1, keepdims=True) + EPS)
                on = o * rstd
                sr = _sigmoid(rc)
                sil = rc * sr
                dyv, gnv = dy_ref[h, rows, :], gn_ref[h]
                dgn_ref[h] += jnp.sum(dyv * on * sil, axis=0, keepdims=True)
                dr_ref[h, rows, :] = dyv * on * gnv * (sr * (1.0 + rc * (1.0 - sr)))
                don = dyv * gnv * sil
                do = rstd * (don - on * jnp.mean(don * on, axis=-1, keepdims=True))
                dp = _dot(do, vc, NT) * tril
                dv_ref[h, rows, :] = _dot(sc, do, TN) + _dot(ke, ds)
                dqt = _dot(dp, kt) + _dot(do, s_prev, NT)
                dkt = _dot(dp, qt, TN)
                dke = _dot(vc, ds, NT)
                ddec = _dot(jnp.ones((8, GLA_DV), F32), ds * s_prev, NT, exact=True)[0:1, :]
                ds_ref[h] = jnp.exp(blb) * ds + _dot(qt, do, TN)
                dq_ref[h, rows, :] = dqt * (GLA_DK ** -0.5) * ebc
                dk_ref[h, rows, :] = dkt * jnp.exp(-bc) + dke * jnp.exp(bl - bc)
                dbc = dqt * qt - dkt * kt - dke * ke
                dbl = jnp.sum(dke * ke, axis=0, keepdims=True) + ddec * jnp.exp(bl)
                dla = _dot(triu, dbc, exact=True) + dbl
                dz = dla * (1.0 - _sigmoid(z)) * (1.0 / GLA_TAU)
                dz_ref[h, rows, :] = dz
                dbup_ref[h] += jnp.sum(dz, axis=0, keepdims=True)
            return carry

        lax.fori_loop(0, GLA_CPB, chunk, 0)

    rev = lambda j: nb - 1 - j
    ins = [q, k, v, r, al, wup, bup, gn, dy, sp]
    in_specs = _gla_in_specs(ins[:9], rev) + [pl.BlockSpec((GLA_HEADS, GLA_CPB, GLA_DK, GLA_DV), lambda j: (0, rev(j), 0, 0))]
    hs = lambda w: pl.BlockSpec((GLA_HEADS, GLA_ROWS, w), lambda j: (0, rev(j), 0))
    h1 = lambda w: pl.BlockSpec((GLA_HEADS, 1, w), lambda j: (0, 0, 0))
    sh = lambda w: jax.ShapeDtypeStruct((GLA_HEADS, n, w), F32)
    s1 = lambda w: jax.ShapeDtypeStruct((GLA_HEADS, 1, w), F32)
    return _pcall(body, name=name, grid=(nb,), in_specs=in_specs,
                  out_specs=[hs(GLA_DK), hs(GLA_DK), hs(GLA_DV), hs(GLA_DV), hs(GLA_DK), h1(GLA_DV), h1(GLA_DK)],
                  out_shape=[sh(GLA_DK), sh(GLA_DK), sh(GLA_DV), sh(GLA_DV), sh(GLA_DK), s1(GLA_DV), s1(GLA_DK)],
                  scratch_shapes=[pltpu.VMEM((GLA_HEADS, GLA_DK, GLA_DV), F32)],
                  compiler_params=_params())(*ins)


def _heads(t, w):
    return t.reshape(t.shape[0], GLA_HEADS, w).transpose(1, 0, 2)


def _unheads(t):
    return t.transpose(1, 0, 2).reshape(t.shape[1], GLA_HEADS * t.shape[2])


ANY = pl.BlockSpec(memory_space=pl.ANY)


def _place():
    x, y, c = lax.axis_index("x"), lax.axis_index("y"), lax.axis_index("c")
    chips = [(1 - x, y), (x, 1 - y), (1 - x, 1 - y)]
    return x, y, c, chips


def _remote(src, dst, ssem, rsem, dev):
    return pltpu.make_async_remote_copy(src_ref=src, dst_ref=dst, send_sem=ssem, recv_sem=rsem, device_id=dev,
                                        device_id_type=MESH_ID)


def _half(c, rows):
    h = rows // 2
    return pl.ds(pl.multiple_of(c * h, 8), h)


def _side_gather_ici(shards):
    def copies(ins, outs, ssem, rsem):
        x, y, c, chips = _place()
        mine = 2 * x + y
        cps = []
        for w in range(len(ins)):
            half = _half(c, ins[w].shape[0])
            cps.append(_remote(ins[w], outs[w].at[mine], ssem.at[4 * w], rsem.at[4 * w], (x, y, 1 - c)))
            for k, (px, py) in enumerate(chips):
                cps.append(_remote(ins[w].at[half], outs[w].at[mine, half], ssem.at[4 * w + 1 + k], rsem.at[4 * w + 1 + k],
                                   (px, py, c)))
        return cps

    return _Side(shards, [jax.ShapeDtypeStruct((4,) + s.shape, s.dtype) for s in shards], 4 * len(shards), copies)


def _side_gather_d2d(gathered):
    def copies(ins, outs, ssem, rsem):
        x, y, c, chips = _place()
        cps = []
        for w in range(len(outs)):
            half = _half(c, outs[w].shape[1])
            for k, (px, py) in enumerate(chips):
                theirs = outs[w].at[2 * px + py, half]
                cps.append(_remote(theirs, theirs, ssem.at[3 * w + k], rsem.at[3 * w + k], (x, y, 1 - c)))
        return cps

    return _Side(gathered, [jax.ShapeDtypeStruct(g.shape, g.dtype) for g in gathered], 3 * len(gathered), copies,
                 aliased=True)


def _side_swap_halves(grads):
    def copies(ins, outs, ssem, rsem):
        x, y, c, _ = _place()
        return [_remote(ins[w].at[:, _half(1 - c, ins[w].shape[1]), :], outs[w], ssem.at[w], rsem.at[w], (x, y, 1 - c))
                for w in range(len(ins))]

    return _Side(grads, [jax.ShapeDtypeStruct((4, g.shape[1] // 2, g.shape[2]), g.dtype) for g in grads], len(grads), copies)


def _side_scatter(sums):
    def copies(ins, outs, ssem, rsem):
        x, y, c, chips = _place()
        return [_remote(ins[w].at[2 * px + py], outs[w].at[k], ssem.at[3 * w + k], rsem.at[3 * w + k], (px, py, c))
                for w in range(len(ins)) for k, (px, py) in enumerate(chips)]

    return _Side(sums, [jax.ShapeDtypeStruct((3,) + s.shape[1:], s.dtype) for s in sums], 3 * len(sums), copies)


def _side_swap_reduced(halves):
    def copies(ins, outs, ssem, rsem):
        x, y, c, _ = _place()
        return [_remote(ins[w], outs[w], ssem.at[w], rsem.at[w], (x, y, 1 - c)) for w in range(len(ins))]

    return _Side(halves, [jax.ShapeDtypeStruct(h.shape, h.dtype) for h in halves], len(halves), copies)


def _chip_sum(g, recv, c_arr, name):
    _, r, cols = g.shape
    h = r // 2
    tr = _pick(h, 256, 16)
    g4 = g.reshape(4, 2, h, cols)

    def body(c_ref, g_ref, r_ref, o_ref):
        o_ref[...] = (g_ref[...] + r_ref[...]).astype(BF16)

    grid_spec = pltpu.PrefetchScalarGridSpec(
        num_scalar_prefetch=1, grid=(4, h // tr),
        in_specs=[pl.BlockSpec((None, None, tr, cols), lambda s, i, c_ref: (s, c_ref[0], i, 0)),
                  pl.BlockSpec((None, tr, cols), lambda s, i, c_ref: (s, i, 0))],
        out_specs=pl.BlockSpec((None, tr, cols), lambda s, i, c_ref: (s, i, 0)))
    return _pcall(body, name=name, grid_spec=grid_spec, out_shape=jax.ShapeDtypeStruct((4, h, cols), BF16),
                  compiler_params=_params())(c_arr, g4, recv)


def _owner_sum(sums, others, s_arr, name):
    _, h, cols = sums.shape
    tr = _pick(h, 256, 16)

    def body(s_ref, a_ref, o_ref, out_ref):
        f = lambda v: v.astype(F32)
        out_ref[...] = (f(a_ref[...]) + f(o_ref[0])) + (f(o_ref[1]) + f(o_ref[2]))

    grid_spec = pltpu.PrefetchScalarGridSpec(
        num_scalar_prefetch=1, grid=(h // tr,),
        in_specs=[pl.BlockSpec((None, tr, cols), lambda i, s_ref: (s_ref[0], i, 0)),
                  pl.BlockSpec((3, tr, cols), lambda i, s_ref: (0, i, 0))],
        out_specs=pl.BlockSpec((tr, cols), lambda i, s_ref: (i, 0)))
    return _pcall(body, name=name, grid_spec=grid_spec, out_shape=jax.ShapeDtypeStruct((h, cols), F32),
                  compiler_params=_params())(s_arr, sums, others)


def _allreduce_small(v, name):
    def body(v_ref, o_ref, r0, r1, ssem, rsem):
        x, y, c, chips = _place()
        cp = _remote(v_ref, r0, ssem.at[0], rsem.at[0], (x, y, 1 - c))
        cp.start()
        cp.wait()
        o_ref[...] = v_ref[...] + r0[...]
        cps = []
        for k, (px, py) in enumerate(chips):
            cp = _remote(o_ref, r1.at[k], ssem.at[1 + k], rsem.at[1 + k], (px, py, c))
            cp.start()
            cps.append(cp)
        for cp in cps:
            cp.wait()
        o_ref[...] = (o_ref[...] + r1[0]) + (r1[1] + r1[2])

    vm = pl.BlockSpec(memory_space=pltpu.VMEM)
    return _pcall(body, name=name, in_specs=[vm], out_specs=vm, out_shape=jax.ShapeDtypeStruct(v.shape, F32),
                  scratch_shapes=[pltpu.VMEM(v.shape, F32), pltpu.VMEM((3,) + v.shape, F32),
                                  pltpu.SemaphoreType.DMA((4,)), pltpu.SemaphoreType.DMA((4,))],
                  compiler_params=_params(has_side_effects=True))(v)


def _pack_small(parts):
    flat = jnp.concatenate([p.reshape(-1).astype(F32) for p in parts])
    pad = (-flat.shape[0]) % (64 * LANE)
    return jnp.pad(flat, (0, pad)).reshape(-1, LANE)


def _unpack_small(packed, like):
    flat, out, pos = packed.reshape(-1), [], 0
    for p in like:
        out.append(flat[pos:pos + p.size].reshape(p.shape))
        pos += p.size
    return out


FFN_ROWS = 512


def _ffn_specs(n, d, fs):
    row = pl.BlockSpec((FFN_ROWS, d), lambda i, s: (i, 0))
    gain = pl.BlockSpec((1, d), lambda i, s: (0, 0))
    w_col = pl.BlockSpec((None, d, fs), lambda i, s: (s, 0, 0))
    w_row = pl.BlockSpec((None, fs, d), lambda i, s: (s, 0, 0))
    hid = pl.BlockSpec((None, FFN_ROWS, fs), lambda i, s: (s, i, 0))
    return row, gain, w_col, w_row, hid


def _ffn_fwd(h, g, w1, w3, w2, tag, plan):
    n, d = h.shape
    ns, _, fs = w1.shape
    row, gain, w_col, w_row, hid = _ffn_specs(n, d, fs)

    def body(h_ref, g_ref, w1_ref, w3_ref, w2_ref, out_ref, n1_ref, a_ref, b_ref, hm_ref, acc_ref):
        s = pl.program_id(1)

        @pl.when(s == 0)
        def _():
            xv = h_ref[...]
            rstd = lax.rsqrt(jnp.mean(xv * xv, axis=-1, keepdims=True) + EPS)
            n1_ref[...] = (xv * rstd * g_ref[...]).astype(BF16)
            acc_ref[...] = jnp.zeros_like(acc_ref)

        n1 = n1_ref[...]
        a = jnp.dot(n1, w1_ref[...], preferred_element_type=F32)
        b = jnp.dot(n1, w3_ref[...], preferred_element_type=F32)
        hm = (a * _sigmoid(a) * b).astype(BF16)
        a_ref[...] = a.astype(BF16)
        b_ref[...] = b.astype(BF16)
        hm_ref[...] = hm
        acc_ref[...] += jnp.dot(hm, w2_ref[...], preferred_element_type=F32)

        @pl.when(s == ns - 1)
        def _():
            out_ref[...] = h_ref[...] + 0.5 * acc_ref[...]

    hid_shape = jax.ShapeDtypeStruct((ns, n, fs), BF16)
    plan.before(f"{tag}_fwd")
    out, n1, a, b, hm = _pcall(
        body, name=f"{tag}_fwd", grid=(n // FFN_ROWS, ns), in_specs=[row, gain, w_col, w_col, w_row],
        out_specs=[row, row, hid, hid, hid],
        out_shape=[jax.ShapeDtypeStruct((n, d), F32), jax.ShapeDtypeStruct((n, d), BF16), hid_shape, hid_shape, hid_shape],
        scratch_shapes=[pltpu.VMEM((FFN_ROWS, d), F32)], compiler_params=_params())(h, g, w1, w3, w2)
    plan.after(f"{tag}_fwd")
    return out, (h, n1, a, b, hm)


def _ffn_bwd(dout, saved, g, w1, w3, w2, tag, plan):
    h, n1, a, b, hm = saved
    n, d = h.shape
    ns, _, fs = w1.shape
    row, gain, w_col, w_row, hid = _ffn_specs(n, d, fs)

    def body(do_ref, h_ref, g_ref, a_ref, b_ref, w1_ref, w3_ref, w2_ref, dh_ref, da_ref, db_ref, dg_ref, acc_ref):
        i, s = pl.program_id(0), pl.program_id(1)

        @pl.when(s == 0)
        def _():
            acc_ref[...] = jnp.zeros_like(acc_ref)

        @pl.when((s == 0) & (i == 0))
        def _():
            dg_ref[...] = jnp.zeros_like(dg_ref)

        dhm = _dot(0.5 * do_ref[...], w2_ref[...], NT)
        av, bv = a_ref[...].astype(F32), b_ref[...].astype(F32)
        sg = _sigmoid(av)
        da = (dhm * bv * (sg * (1.0 + av * (1.0 - sg)))).astype(BF16)
        db = (dhm * av * sg).astype(BF16)
        da_ref[...] = da
        db_ref[...] = db
        acc_ref[...] += _dot(da, w1_ref[...], NT) + _dot(db, w3_ref[...], NT)

        @pl.when(s == ns - 1)
        def _():
            xv, dn = h_ref[...], acc_ref[...]
            rstd = lax.rsqrt(jnp.mean(xv * xv, axis=-1, keepdims=True) + EPS)
            xh = xv * rstd
            dg_ref[...] += jnp.sum(dn * xh, axis=0, keepdims=True)
            dxh = dn * g_ref[...]
            dh_ref[...] = do_ref[...] + rstd * (dxh - xh * jnp.mean(dxh * xh, axis=-1, keepdims=True))

    hid_shape = jax.ShapeDtypeStruct((ns, n, fs), BF16)
    plan.before(f"{tag}_bwd")
    dh, da, db, dg = _pcall(
        body, name=f"{tag}_bwd", grid=(n // FFN_ROWS, ns), in_specs=[row, row, gain, hid, hid, w_col, w_col, w_row],
        out_specs=[row, hid, hid, gain],
        out_shape=[jax.ShapeDtypeStruct((n, d), F32), hid_shape, hid_shape, jax.ShapeDtypeStruct((1, d), F32)],
        scratch_shapes=[pltpu.VMEM((FFN_ROWS, d), F32)], compiler_params=_params())(dout, h, g, a, b, w1, w3, w2)
    plan.after(f"{tag}_bwd")
    plan.before(f"{tag}_gw2")
    gw2 = _mm(hm, dout, ta=True, shard='m', alpha=0.5, name=f"{tag}_gw2")
    plan.after(f"{tag}_gw2")
    gw1 = _mm(n1, da, ta=True, shard='n', name=f"{tag}_gw1")
    gw3 = _mm(n1, db, ta=True, shard='n', name=f"{tag}_gw3")
    return dh, dg, gw1, gw3, gw2


def _local_step(x, tgt, plan):
    n = x.shape[0]
    grads = plan.grads

    def f(name):
        w = plan.get(name)
        return w.reshape(1, D_MODEL) if name.endswith('_norm') and name != 'gla_out_norm' else w

    def carried(tag, fn, *args, **kw):
        plan.before(tag)
        out = fn(*args, **kw)
        plan.after(tag)
        return out

    h1, ffn1 = _ffn_fwd(x, f('ffn1_norm'), f('ffn1_w1'), f('ffn1_w3'), f('ffn1_w2'), "ffn1", plan)
    u = carried("mix_rms", _rms_fwd, h1, f('mix_norm'), "mix_rms")
    w_in = f('w_in')
    w_a = w_in[:, :2048]
    w_al = jnp.pad(w_in[:, 2048:2048 + GLA_RANK], ((0, 0), (0, LANE - GLA_RANK)))
    w_g = w_in[:, 2048 + GLA_RANK:]
    za = _mm(u, w_a, name="in_a")
    zg = _mm(u, w_g, name="in_g")
    al = _mm(u, w_al, name="in_al")
    ar, ai, bbar_re, bbar_im = _s5_discretize(f('s5_lambda_re'), f('s5_lambda_im'), f('s5_log_dt'), f('s5_b_re'), f('s5_b_im'))
    b_blk = jnp.concatenate([_block_diag(bbar_re.transpose(0, 2, 1)), _block_diag(bbar_im.transpose(0, 2, 1))], axis=1)
    c_blk = jnp.concatenate([_block_diag(f('s5_c_re').transpose(0, 2, 1)), -_block_diag(f('s5_c_im').transpose(0, 2, 1))], axis=0)
    b_blk, c_blk = b_blk.astype(BF16), c_blk.astype(BF16)
    ar8 = jnp.broadcast_to(ar.reshape(1, S5_GP), (SEG, S5_GP))
    ai8 = jnp.broadcast_to(ai.reshape(1, S5_GP), (SEG, S5_GP))
    pw_r, pw_i = _segment_powers(ar, ai, n // SEG)
    dskip = f('s5_d').reshape(1, S5_W)
    u_s5 = _permute_rows(za[:, :S5_W])
    bu = _mm(u_s5, b_blk, name="s5_bu")
    xs = _s5_scan(bu, ar8, ai8, pw_r, pw_i, "s5_scan")
    ys_p = _mm(xs, c_blk, res=_scale_rows(u_s5, dskip, "s5_skip"), name="s5_y")
    ys = _unpermute_rows(ys_p)
    zgelu = _gelu_fwd(ys, "s5_gelu")
    t_glu = _mm(zgelu, f('s5_glu_w'), bias=f('s5_glu_b').reshape(1, S5_W), name="s5_glu_t")
    y_s5 = _glu_fwd(zgelu, t_glu, "s5_glu")
    q, k = _heads(za[:, 512:768], GLA_DK), _heads(za[:, 768:1024], GLA_DK)
    v, r = _heads(za[:, 1024:1536], GLA_DV), _heads(za[:, 1536:2048], GLA_DV)
    wup = jnp.pad(f('gla_a_up_w'), ((0, LANE - GLA_RANK), (0, 0)))
    wup_h = wup.reshape(LANE, GLA_HEADS, GLA_DK).transpose(1, 0, 2)
    bup_h = f('gla_a_up_b').reshape(GLA_HEADS, 1, GLA_DK)
    gn_h = f('gla_out_norm').reshape(GLA_HEADS, 1, GLA_DV)
    y_gla_h, s_prev = carried("gla_fwd", _gla_fwd, q, k, v, r, al, wup_h, bup_h, gn_h, "gla_fwd")
    y_gla = _unheads(y_gla_h).astype(BF16)
    ps = _mm(y_s5, f('proj_s5'), name="proj_s5")
    pg = carried("proj_gla", _mm, y_gla, f('proj_gla'), name="proj_gla")
    merged = _merge_fwd(zg, ps, pg, "merge")
    h2 = _mm(merged, f('w_out'), res=h1, name="w_out")
    h3, ffn2 = _ffn_fwd(h2, f('ffn2_norm'), f('ffn2_w1'), f('ffn2_w3'), f('ffn2_w2'), "ffn2", plan)
    loss, dh3, g_final = _final_loss(h3, f('final_norm').reshape(1, D_MODEL), tgt, "loss")
    grads['final_norm'] = g_final.reshape(D_MODEL)
    dh2, grads['ffn2_norm'], grads['ffn2_w1'], grads['ffn2_w3'], grads['ffn2_w2'] = _ffn_bwd(
        dh3, ffn2, f('ffn2_norm'), f('ffn2_w1'), f('ffn2_w3'), f('ffn2_w2'), "ffn2", plan)
    dh2b = dh2.astype(BF16)
    dm = _mm(dh2b, f('w_out'), tb=True, name="d_merged")
    grads['w_out'] = _mm(merged, dh2b, ta=True, name="g_w_out")
    dps, dpg, dzg = carried("d_merge", _merge_bwd, dm, zg, ps, pg, "d_merge")
    grads['proj_s5'] = _mm(y_s5, dps, ta=True, name="g_proj_s5")
    grads['proj_gla'] = _mm(y_gla, dpg, ta=True, name="g_proj_gla")
    dy_s5 = _mm(dps, f('proj_s5'), tb=True, name="d_y_s5")
    dy_gla = _mm(dpg, f('proj_gla'), tb=True, name="d_y_gla")
    dzgelu, dt_glu, g_glu_b = _glu_bwd1(dy_s5, zgelu, t_glu, "d_glu")
    grads['s5_glu_b'] = g_glu_b.reshape(S5_W)
    grads['s5_glu_w'] = _mm(zgelu, dt_glu, ta=True, name="g_glu_w")
    dzgelu = _mm(dt_glu, f('s5_glu_w'), tb=True, res=dzgelu, name="d_gelu")
    dys, du_skip, g_d = _glu_bwd2(_permute_rows(dzgelu), ys_p, u_s5, dskip, "d_s5_y")
    grads['s5_d'] = g_d.reshape(S5_G, S5_H)
    gx = _mm(dys, c_blk, tb=True, name="s5_gx")
    lam, da8 = _s5_scan_bwd(gx, xs, ar8, ai8, pw_r, pw_i, "s5_scan_bwd")
    g_c = _mm(dys, xs, ta=True, name="g_s5_c")
    grads['s5_c_re'] = _diag_blocks(g_c[:, :S5_GP], S5_H, S5_P)
    grads['s5_c_im'] = -_diag_blocks(g_c[:, S5_GP:], S5_H, S5_P)
    g_b = _mm(lam, u_s5, ta=True, name="g_s5_b")
    g_bbar_re = _diag_blocks(g_b[:S5_GP], S5_P, S5_H)
    g_bbar_im = _diag_blocks(g_b[S5_GP:], S5_P, S5_H)
    da = jnp.sum(da8, axis=0)
    g_ar, g_ai = da[:S5_GP].reshape(S5_G, S5_P), da[S5_GP:].reshape(S5_G, S5_P)
    _, disc_vjp = jax.vjp(_s5_discretize, f('s5_lambda_re'), f('s5_lambda_im'), f('s5_log_dt'), f('s5_b_re'), f('s5_b_im'))
    (grads['s5_lambda_re'], grads['s5_lambda_im'], grads['s5_log_dt'], grads['s5_b_re'],
     grads['s5_b_im']) = disc_vjp((g_ar, g_ai, g_bbar_re, g_bbar_im))
    du_s5 = _unpermute_rows(_mm(lam, b_blk, tb=True, res=du_skip, name="d_s5_u"))
    dq, dk, dv, dr, dz, dgn, dbup = carried("gla_bwd", _gla_bwd, q, k, v, r, al, wup_h, bup_h, gn_h, s_prev,
                                            _heads(dy_gla, GLA_DV), "gla_bwd")
    grads['gla_out_norm'] = dgn.reshape(GLA_HEADS * GLA_DV)
    grads['gla_a_up_b'] = dbup.reshape(GLA_HEADS * GLA_DK)
    dz = _unheads(dz)
    grads['gla_a_up_w'] = _mm(al, dz, ta=True, name="g_a_up")[:GLA_RANK]
    dal = _mm(dz, wup, tb=True, name="d_a_low")
    dza = jnp.concatenate([du_s5, _unheads(dq), _unheads(dk), _unheads(dv), _unheads(dr)], axis=1)
    g_wa = _mm(u, dza, ta=True, name="g_in_a")
    g_wg = _mm(u, dzg, ta=True, name="g_in_g")
    g_wal = _mm(u, dal, ta=True, name="g_in_al")
    grads['w_in'] = jnp.concatenate([g_wa, g_wal[:, :GLA_RANK], g_wg], axis=1)
    du = carried("d_u_a", _mm, dza, w_a, tb=True, name="d_u_a")
    du = _mm(dzg, w_g, tb=True, res=du, name="d_u_g")
    du = _mm(dal, w_al, tb=True, res=du, name="d_u_al")
    dh1, g_mix = carried("d_mix_rms", _rms_bwd, h1, f('mix_norm'), du, dh2, "d_mix_rms")
    grads['mix_norm'] = g_mix
    dx, grads['ffn1_norm'], grads['ffn1_w1'], grads['ffn1_w3'], grads['ffn1_w2'] = _ffn_bwd(
        dh1, ffn1, f('ffn1_norm'), f('ffn1_w1'), f('ffn1_w3'), f('ffn1_w2'), "ffn1", plan)
    return loss[0, 0], dx


MIXER_WEIGHTS = ['w_in', 's5_glu_w', 'proj_s5', 'proj_gla', 'w_out', 'gla_a_up_w']
FFN1_WEIGHTS, FFN2_WEIGHTS = FFN_WEIGHTS[:3], FFN_WEIGHTS[3:]
GRAD_GROUPS = {'ffn2': FFN2_WEIGHTS, 'mixer': ['w_out', 'proj_s5', 'proj_gla', 's5_glu_w', 'w_in'], 'ffn1': FFN1_WEIGHTS}


class _Plan:
    def __init__(self, a, c_arr, s_arr):
        self.a, self.c_arr, self.s_arr = a, c_arr, s_arr
        self.grads, self.weights, self.riding = {}, {}, {}
        self.g4s, self.chip_sums, self.halves, self.sib_halves = {}, {}, {}, {}
        for nm in SMALL:
            if nm != 'gla_a_up_w':
                self.weights[nm] = a[nm] if nm == 'final_norm' else a[nm][0]
        ici = _side_gather_ici(self._shards(FFN1_WEIGHTS))
        _run_side(ici, "gather_ffn1_ici")
        self._gathered(FFN1_WEIGHTS, _run_side(_side_gather_d2d(ici.outs), "gather_ffn1_d2d"))

    def _shards(self, names):
        return [self.a[nm][0].astype(F32 if nm == 'gla_a_up_w' else BF16) for nm in names]

    def _gathered(self, names, arrs):
        for nm, g4 in zip(names, arrs):
            if nm in FFN_WEIGHTS:
                self.weights[nm] = g4
            elif nm in COL_SHARDED:
                self.weights[nm] = jnp.concatenate([g4[s] for s in range(4)], axis=1)
            else:
                self.weights[nm] = g4.reshape(4 * g4.shape[1], g4.shape[2])

    def get(self, name):
        return self.weights[name]

    def _shard_major(self, nm):
        g = self.grads[nm]
        if nm in FFN_WEIGHTS:
            return g
        if nm in COL_SHARDED:
            return jnp.stack(jnp.split(g, 4, axis=1))
        return g.reshape(4, g.shape[0] // 4, g.shape[1])

    def _schedule(self, tag):
        grp = GRAD_GROUPS
        if tag == "ffn1_fwd":
            return _side_gather_ici(self._shards(MIXER_WEIGHTS)), lambda outs: self.riding.update(mixer_ici=outs)
        if tag == "mix_rms":
            return _side_gather_d2d(self.riding['mixer_ici']), lambda outs: self._gathered(MIXER_WEIGHTS, outs)
        if tag == "gla_fwd":
            return _side_gather_ici(self._shards(FFN2_WEIGHTS)), lambda outs: self.riding.update(ffn2_ici=outs)
        if tag == "proj_gla":
            return _side_gather_d2d(self.riding['ffn2_ici']), lambda outs: self._gathered(FFN2_WEIGHTS, outs)
        steps = {"d_merge": ('ffn2', 0), "gla_bwd": ('ffn2', 1), "d_mix_rms": ('ffn2', 2),
                 "d_u_a": ('mixer', 0), "ffn1_bwd": ('mixer', 1), "ffn1_gw2": ('mixer', 2)}
        if tag in steps:
            group, stage = steps[tag]
            return self._reduce_stage(grp[group], stage)
        return None

    def _reduce_stage(self, names, stage):
        if stage == 0:
            for nm in names:
                self.g4s[nm] = self._shard_major(nm)

            def done(outs):
                for nm, r in zip(names, outs):
                    self.chip_sums[nm] = _chip_sum(self.g4s[nm], r, self.c_arr, f"chip_sum_{nm}")
            return _side_swap_halves([self.g4s[nm] for nm in names]), done
        if stage == 1:
            def done(outs):
                for nm, o in zip(names, outs):
                    self.halves[nm] = _owner_sum(self.chip_sums[nm], o, self.s_arr, f"owner_sum_{nm}")
            return _side_scatter([self.chip_sums[nm] for nm in names]), done

        def done(outs):
            self.sib_halves.update(zip(names, outs))
        return _side_swap_reduced([self.halves[nm] for nm in names]), done

    def before(self, tag):
        entry = self._schedule(tag)
        if entry is not None:
            side, done = entry
            self.riding[tag] = (side, done)
            _RIDER.append(side)

    def after(self, tag):
        if tag in self.riding:
            side, done = self.riding.pop(tag)
            assert not _RIDER and side.outs is not None, tag
            done(side.outs)

    def finish(self):
        names = GRAD_GROUPS['ffn1']
        for stage in range(3):
            side, done = self._reduce_stage(names, stage)
            done(_run_side(side, f"grad_ffn1_stage{stage}"))


def _train_step(a):
    x = a['x'][0]
    tgt = a['loss_target'][0]
    xi, yi, ci = lax.axis_index("x"), lax.axis_index("y"), lax.axis_index("c")
    c_arr = jnp.reshape(ci, (1,)).astype(jnp.int32)
    s_arr = jnp.reshape(2 * xi + yi, (1,)).astype(jnp.int32)
    plan = _Plan(a, c_arr, s_arr)
    loss, dx = _local_step(x, tgt, plan)
    plan.finish()
    grads = plan.grads
    loss = lax.psum(loss, ("x", "y", "c"))
    halves = [plan.halves[nm] for nm in SHARDED]
    sib_halves = [plan.sib_halves[nm] for nm in SHARDED]
    red = {}
    small_parts = [grads[nm].reshape(a[nm].shape) for nm in SMALL if nm != 'gla_a_up_w'] + [grads['gla_a_up_w']]
    small_sum = _unpack_small(_allreduce_small(_pack_small(small_parts), "allreduce_small"), small_parts)
    small_names = [nm for nm in SMALL if nm != 'gla_a_up_w']
    for nm, g in zip(small_names, small_sum[:-1]):
        red[nm] = g
    g_up = small_sum[-1]
    red['gla_a_up_w'] = lax.dynamic_slice(g_up, (0, (2 * xi + yi) * GLA_DK), (GLA_RANK, GLA_DK))
    out_g, out_d, out_m, out_v = {}, {}, {}, {}
    for nm, own, sib in zip(SHARDED, halves, sib_halves):
        g, d, nm_, nv_ = _adamw_halves(a[nm][0], own, sib, a['m_' + nm][0], a['v_' + nm][0], c_arr, f"adamw_{nm}")
        shape = a[nm].shape
        out_g[nm], out_d[nm], out_m[nm], out_v[nm] = (t.reshape(shape) for t in (g, d, nm_, nv_))
    rest = [nm for nm in WEIGHTS if nm not in SHARDED]
    pk = lambda pre: _pack_small([a[pre + nm] for nm in rest])
    d, nm_, nv_ = _adamw(pk(''), _pack_small([red[nm] for nm in rest]), pk('m_'), pk('v_'), "adamw_small")
    like = [a[nm] for nm in rest]
    for nm, g, dd, mm_, vv_ in zip(rest, [red[nm].reshape(a[nm].shape) for nm in rest], _unpack_small(d, like),
                                   _unpack_small(nm_, like), _unpack_small(nv_, like)):
        out_g[nm], out_d[nm], out_m[nm], out_v[nm] = g, dd, mm_, vv_
    return (loss, dx[None], *[out_g[nm] for nm in WEIGHTS], *[out_d[nm] for nm in WEIGHTS],
            *[out_m[nm] for nm in WEIGHTS], *[out_v[nm] for nm in WEIGHTS])


def kernel(x, ffn1_norm, ffn1_w1, ffn1_w3, ffn1_w2, mix_norm, w_in, s5_lambda_re, s5_lambda_im, s5_log_dt, s5_b_re, s5_b_im, s5_c_re, s5_c_im, s5_d, s5_glu_w, s5_glu_b, gla_a_up_w, gla_a_up_b, gla_out_norm, proj_s5, proj_gla, w_out, ffn2_norm, ffn2_w1, ffn2_w3, ffn2_w2, final_norm, loss_target, m_ffn1_norm, m_ffn1_w1, m_ffn1_w3, m_ffn1_w2, m_mix_norm, m_w_in, m_s5_lambda_re, m_s5_lambda_im, m_s5_log_dt, m_s5_b_re, m_s5_b_im, m_s5_c_re, m_s5_c_im, m_s5_d, m_s5_glu_w, m_s5_glu_b, m_gla_a_up_w, m_gla_a_up_b, m_gla_out_norm, m_proj_s5, m_proj_gla, m_w_out, m_ffn2_norm, m_ffn2_w1, m_ffn2_w3, m_ffn2_w2, m_final_norm, v_ffn1_norm, v_ffn1_w1, v_ffn1_w3, v_ffn1_w2, v_mix_norm, v_w_in, v_s5_lambda_re, v_s5_lambda_im, v_s5_log_dt, v_s5_b_re, v_s5_b_im, v_s5_c_re, v_s5_c_im, v_s5_d, v_s5_glu_w, v_s5_glu_b, v_gla_a_up_w, v_gla_a_up_b, v_gla_out_norm, v_proj_s5, v_proj_gla, v_w_out, v_ffn2_norm, v_ffn2_w1, v_ffn2_w3, v_ffn2_w2, v_final_norm):
    return _train_step(dict(locals()))
```

```python
import functools

import jax
import jax.numpy as jnp
from jax import lax
from jax.experimental import pallas as pl
from jax.experimental.pallas import tpu as pltpu

F32 = jnp.float32
BF16 = jnp.bfloat16
HI = lax.Precision.HIGHEST
MESH_ID = pl.DeviceIdType.MESH

D_MODEL = 1024
EPS = 1e-6
S5_G, S5_P, S5_H = 32, 64, 16
S5_W = S5_G * S5_H
S5_GP = S5_G * S5_P
SEG = 8
SCAN_ROWS = 256
GLA_HEADS, GLA_DK, GLA_DV = 4, 64, 128
GLA_CHUNK = 64
GLA_TAU = 16.0
GLA_RANK = 16
ADAM_LR, ADAM_B1, ADAM_B2, ADAM_EPS, ADAM_WD, ADAM_STEP = 0.001, 0.9, 0.999, 1e-08, 0.01, 10
V7X_VMEM_LIMIT = 56 * 1024 * 1024
LANE = 128

WEIGHTS = ['ffn1_norm', 'ffn1_w1', 'ffn1_w3', 'ffn1_w2', 'mix_norm', 'w_in', 's5_lambda_re', 's5_lambda_im',
           's5_log_dt', 's5_b_re', 's5_b_im', 's5_c_re', 's5_c_im', 's5_d', 's5_glu_w', 's5_glu_b', 'gla_a_up_w',
           'gla_a_up_b', 'gla_out_norm', 'proj_s5', 'proj_gla', 'w_out', 'ffn2_norm', 'ffn2_w1', 'ffn2_w3',
           'ffn2_w2', 'final_norm']
SHARDED = ['ffn1_w1', 'ffn1_w3', 'ffn1_w2', 'w_in', 's5_glu_w', 'proj_s5', 'proj_gla', 'w_out',
           'ffn2_w1', 'ffn2_w3', 'ffn2_w2']
COL_SHARDED = ['ffn1_w1', 'ffn1_w3', 'w_in', 'proj_s5', 'proj_gla', 'ffn2_w1', 'ffn2_w3', 'gla_a_up_w']
SMALL = [n for n in WEIGHTS if n not in SHARDED]
FFN_WEIGHTS = ['ffn1_w1', 'ffn1_w3', 'ffn1_w2', 'ffn2_w1', 'ffn2_w3', 'ffn2_w2']


def _params(**kw):
    return pltpu.CompilerParams(vmem_limit_bytes=V7X_VMEM_LIMIT, **kw)


class _Side:
    def __init__(self, ins, out_shapes, nsem, copies, aliased=False):
        self.ins, self.out_shapes, self.nsem, self.copies, self.aliased = list(ins), list(out_shapes), nsem, copies, aliased
        self.outs = None


_RIDER = []


def _pcall(body, **kw):
    if _RIDER:
        return _carry(body, _RIDER.pop(), **kw)
    return pl.pallas_call(body, **kw)


def _carry(body, side, *, name, grid, in_specs, out_specs, out_shape, scratch_shapes=(), compiler_params=None):
    del compiler_params
    single = not isinstance(out_shape, (list, tuple))
    out_specs = [out_specs] if single else list(out_specs)
    out_shape = [out_shape] if single else list(out_shape)
    n_in, n_out, n_scr = len(in_specs), len(out_shape), len(scratch_shapes)
    s_in, s_out = len(side.ins), len(side.out_shapes)
    any_spec = pl.BlockSpec(memory_space=pl.ANY)

    def wrapped(*refs):
        cuts = [n_in, s_in, n_out, s_out, n_scr]
        parts, pos = [], 0
        for c in cuts:
            parts.append(refs[pos:pos + c])
            pos += c
        ins, sins, outs, souts, scr = parts
        ssem, rsem = refs[pos], refs[pos + 1]
        first = last = None
        for d, g in enumerate(grid):
            i = pl.program_id(d)
            first = (i == 0) if first is None else first & (i == 0)
            last = (i == g - 1) if last is None else last & (i == g - 1)

        @pl.when(first)
        def _():
            for cp in side.copies(sins, souts, ssem, rsem):
                cp.start()

        body(*ins, *outs, *scr)

        @pl.when(last)
        def _():
            for cp in side.copies(sins, souts, ssem, rsem):
                cp.wait()

    call = pl.pallas_call(
        wrapped, name=name, grid=grid, in_specs=list(in_specs) + [any_spec] * s_in,
        out_specs=out_specs + [any_spec] * s_out, out_shape=out_shape + side.out_shapes,
        scratch_shapes=list(scratch_shapes) + [pltpu.SemaphoreType.DMA((side.nsem,)), pltpu.SemaphoreType.DMA((side.nsem,))],
        input_output_aliases={n_in + j: n_out + j for j in range(s_in)} if side.aliased else {},
        compiler_params=_params(has_side_effects=True))

    def run(*args):
        res = call(*args, *side.ins)
        side.outs = list(res[n_out:])
        return res[0] if single else list(res[:n_out])

    return run


def _run_side(side, name):
    s_in, s_out = len(side.ins), len(side.out_shapes)
    any_spec = pl.BlockSpec(memory_space=pl.ANY)

    def body(*refs):
        sins, souts = refs[:s_in], refs[s_in:s_in + s_out]
        ssem, rsem = refs[s_in + s_out:]
        cps = side.copies(sins, souts, ssem, rsem)
        for cp in cps:
            cp.start()
        for cp in cps:
            cp.wait()

    side.outs = list(pl.pallas_call(
        body, name=name, in_specs=[any_spec] * s_in, out_specs=[any_spec] * s_out, out_shape=side.out_shapes,
        scratch_shapes=[pltpu.SemaphoreType.DMA((side.nsem,)), pltpu.SemaphoreType.DMA((side.nsem,))],
        input_output_aliases={j: j for j in range(s_in)} if side.aliased else {},
        compiler_params=pltpu.CompilerParams(has_side_effects=True))(*side.ins))
    return side.outs


def _pick(n, cap, quantum):
    if n <= cap:
        return n
    best = None
    for t in range(quantum, cap + 1, quantum):
        if n % t == 0:
            best = t
    assert best is not None, (n, cap, quantum)
    return best


def _sigmoid(x):
    return jax.nn.sigmoid(x)


def _mm(a, b, *, name, ta=False, tb=False, out_dtype=F32, alpha=1.0, res=None, bias=None, exact=False, shard=None):
    ns = 4
    (k_a, m) = a.shape[-2:] if ta else a.shape[-2:][::-1]
    (k_b, n) = b.shape[-2:][::-1] if tb else b.shape[-2:]
    assert k_a == k_b, (a.shape, b.shape, ta, tb)
    assert (a.ndim == 3) == (shard in ('k', 'm')) and (b.ndim == 3) == (shard in ('n', 'k'))
    k = k_a
    tm = _pick(m, 1024, 128)
    tn = _pick(n, 1024, 128)
    tk = _pick(k, 1024, 128)
    pm, pn, pk = m // tm, n // tn, k // tk
    gm = pm * (ns if shard == 'm' else 1)
    gn = pn * (ns if shard == 'n' else 1)
    gk = pk * (ns if shard == 'k' else 1)
    dims = (((0,) if ta else (1,), (1,) if tb else (0,)), ((), ()))
    op_dtype = F32 if exact else BF16

    def body(*refs):
        a_ref, b_ref = refs[0], refs[1]
        pos = 2
        res_ref = bias_ref = None
        if res is not None:
            res_ref = refs[pos]
            pos += 1
        if bias is not None:
            bias_ref = refs[pos]
            pos += 1
        o_ref, acc_ref = refs[pos], refs[pos + 1]
        kk = pl.program_id(2)

        @pl.when(kk == 0)
        def _():
            acc_ref[...] = jnp.zeros_like(acc_ref)

        acc_ref[...] += lax.dot_general(a_ref[...].astype(op_dtype), b_ref[...].astype(op_dtype), dims,
                                        precision=HI if exact else None, preferred_element_type=F32)

        @pl.when(kk == gk - 1)
        def _():
            o = acc_ref[...]
            if alpha != 1.0:
                o = o * alpha
            if bias_ref is not None:
                o = o + bias_ref[...]
            if res_ref is not None:
                o = o + res_ref[...]
            o_ref[...] = o.astype(out_dtype)

    def spec(block, sharded_on, order):
        per = {'m': pm, 'n': pn, 'k': pk}

        def index(i, j, kk):
            g = {'m': i, 'n': j, 'k': kk}
            r, c = order(i % pm if shard == 'm' else i, j % pn if shard == 'n' else j, kk % pk if shard == 'k' else kk)
            if sharded_on is None:
                return (r, c)
            return (g[sharded_on] // per[sharded_on], r, c)

        return pl.BlockSpec(block if sharded_on is None else (None,) + block, index)

    a_sh = shard if shard in ('k', 'm') else None
    b_sh = shard if shard in ('n', 'k') else None
    o_sh = shard if shard in ('n', 'm') else None
    a_spec = spec((tk, tm), a_sh, lambda i, j, kk: (kk, i)) if ta else spec((tm, tk), a_sh, lambda i, j, kk: (i, kk))
    b_spec = spec((tn, tk), b_sh, lambda i, j, kk: (j, kk)) if tb else spec((tk, tn), b_sh, lambda i, j, kk: (kk, j))
    ins, in_specs = [a, b], [a_spec, b_spec]
    if res is not None:
        assert o_sh is None
        ins.append(res)
        in_specs.append(pl.BlockSpec((tm, tn), lambda i, j, kk: (i, j)))
    if bias is not None:
        assert o_sh is None
        ins.append(bias)
        in_specs.append(pl.BlockSpec((1, tn), lambda i, j, kk: (0, j)))
    out_shape = (m, n) if o_sh is None else (ns, m, n)
    return _pcall(body, name=name, grid=(gm, gn, gk), in_specs=in_specs,
                  out_specs=spec((tm, tn), o_sh, lambda i, j, kk: (i, j)),
                  out_shape=jax.ShapeDtypeStruct(out_shape, out_dtype),
                  scratch_shapes=[pltpu.VMEM((tm, tn), F32)], compiler_params=_params())(*ins)


def _rows(body, ins, outs, *, n, name, tm=256):
    tm = _pick(n, tm, 16)
    in_specs = []
    for arr, kind in ins:
        if kind == 'r':
            in_specs.append(pl.BlockSpec((tm, arr.shape[1]), lambda i: (i, 0)))
        else:
            in_specs.append(pl.BlockSpec(arr.shape, lambda i: (0, 0)))
    out_specs, out_shape = [], []
    for cols, dtype, kind in outs:
        if kind == 'r':
            out_specs.append(pl.BlockSpec((tm, cols), lambda i: (i, 0)))
            out_shape.append(jax.ShapeDtypeStruct((n, cols), dtype))
        else:
            out_specs.append(pl.BlockSpec((1, cols), lambda i: (0, 0)))
            out_shape.append(jax.ShapeDtypeStruct((1, cols), dtype))
    n_in = len(ins)
    acc_ids = [j for j, o in enumerate(outs) if o[2] == 'a']

    def wrapped(*refs):
        if acc_ids:
            @pl.when(pl.program_id(0) == 0)
            def _():
                for j in acc_ids:
                    refs[n_in + j][...] = jnp.zeros_like(refs[n_in + j])
        body(*refs)

    res = _pcall(wrapped, name=name, grid=(n // tm,), in_specs=in_specs, out_specs=out_specs, out_shape=out_shape,
                 compiler_params=_params())(*[a for a, _ in ins])
    return res


def _rms_fwd(x, g, name):
    def body(x_ref, g_ref, o_ref):
        xv = x_ref[...]
        rstd = lax.rsqrt(jnp.mean(xv * xv, axis=-1, keepdims=True) + EPS)
        o_ref[...] = (xv * rstd * g_ref[...]).astype(BF16)
    return _rows(body, [(x, 'r'), (g, 'f')], [(x.shape[1], BF16, 'r')], n=x.shape[0], name=name)[0]


def _rms_bwd(x, g, dn, dres, name):
    def body(x_ref, g_ref, dn_ref, dres_ref, dx_ref, dg_ref):
        xv = x_ref[...]
        rstd = lax.rsqrt(jnp.mean(xv * xv, axis=-1, keepdims=True) + EPS)
        xh = xv * rstd
        dn = dn_ref[...]
        dg_ref[...] += jnp.sum(dn * xh, axis=0, keepdims=True)
        dxh = dn * g_ref[...]
        dx_ref[...] = dres_ref[...] + rstd * (dxh - xh * jnp.mean(dxh * xh, axis=-1, keepdims=True))
    d = x.shape[1]
    return _rows(body, [(x, 'r'), (g, 'f'), (dn, 'r'), (dres, 'r')], [(d, F32, 'r'), (d, F32, 'a')],
                 n=x.shape[0], name=name)


def _gelu_parts(y):
    c0 = 0.7978845608028654
    inner = c0 * (y + 0.044715 * y * y * y)
    th = jnp.tanh(inner)
    return th, c0 * (1.0 + 3.0 * 0.044715 * y * y)


def _gelu_fwd(y, name):
    def body(y_ref, o_ref):
        yv = y_ref[...]
        th, _ = _gelu_parts(yv)
        o_ref[...] = 0.5 * yv * (1.0 + th)
    return _rows(body, [(y, 'r')], [(y.shape[1], F32, 'r')], n=y.shape[0], name=name)[0]


def _glu_fwd(zg, t, name):
    def body(z_ref, t_ref, o_ref):
        o_ref[...] = (z_ref[...] * _sigmoid(t_ref[...])).astype(BF16)
    return _rows(body, [(zg, 'r'), (t, 'r')], [(zg.shape[1], BF16, 'r')], n=zg.shape[0], name=name)[0]


def _glu_bwd1(dy, zg, t, name):
    def body(dy_ref, z_ref, t_ref, dz_ref, dt_ref, db_ref):
        dyv, zv = dy_ref[...], z_ref[...]
        sg = _sigmoid(t_ref[...])
        dz_ref[...] = dyv * sg
        dt = dyv * zv * sg * (1.0 - sg)
        dt_ref[...] = dt.astype(BF16)
        db_ref[...] += jnp.sum(dt, axis=0, keepdims=True)
    w = zg.shape[1]
    return _rows(body, [(dy, 'r'), (zg, 'r'), (t, 'r')], [(w, F32, 'r'), (w, BF16, 'r'), (w, F32, 'a')],
                 n=zg.shape[0], name=name)


def _glu_bwd2(dzg, ys, u, dskip, name):
    def body(dz_ref, y_ref, u_ref, d_ref, dy_ref, du_ref, dd_ref):
        yv = y_ref[...]
        th, dinner = _gelu_parts(yv)
        dy = dz_ref[...] * (0.5 * (1.0 + th) + 0.5 * yv * (1.0 - th * th) * dinner)
        dy_ref[...] = dy
        du_ref[...] = dy * d_ref[...]
        dd_ref[...] += jnp.sum(dy * u_ref[...], axis=0, keepdims=True)
    w = ys.shape[1]
    return _rows(body, [(dzg, 'r'), (ys, 'r'), (u, 'r'), (dskip, 'f')], [(w, F32, 'r'), (w, F32, 'r'), (w, F32, 'a')],
                 n=ys.shape[0], name=name)


def _scale_rows(u, dskip, name):
    def body(u_ref, d_ref, o_ref):
        o_ref[...] = u_ref[...] * d_ref[...]
    return _rows(body, [(u, 'r'), (dskip, 'f')], [(u.shape[1], F32, 'r')], n=u.shape[0], name=name)[0]


def _merge_fwd(zg, ps, pg, name):
    def body(z_ref, ps_ref, pg_ref, o_ref):
        zv = z_ref[...]
        o_ref[...] = (_sigmoid(zv[:, :D_MODEL]) * ps_ref[...] + _sigmoid(zv[:, D_MODEL:]) * pg_ref[...]).astype(BF16)
    return _rows(body, [(zg, 'r'), (ps, 'r'), (pg, 'r')], [(D_MODEL, BF16, 'r')], n=zg.shape[0], name=name)[0]


def _merge_bwd(dm, zg, ps, pg, name):
    def body(dm_ref, z_ref, ps_ref, pg_ref, dps_ref, dpg_ref, dz_ref):
        dmv, zv = dm_ref[...], z_ref[...]
        s1, s2 = _sigmoid(zv[:, :D_MODEL]), _sigmoid(zv[:, D_MODEL:])
        dps_ref[...] = (dmv * s1).astype(BF16)
        dpg_ref[...] = (dmv * s2).astype(BF16)
        dz_ref[:, :D_MODEL] = dmv * ps_ref[...] * s1 * (1.0 - s1)
        dz_ref[:, D_MODEL:] = dmv * pg_ref[...] * s2 * (1.0 - s2)
    return _rows(body, [(dm, 'r'), (zg, 'r'), (ps, 'r'), (pg, 'r')],
                 [(D_MODEL, BF16, 'r'), (D_MODEL, BF16, 'r'), (2 * D_MODEL, F32, 'r')], n=zg.shape[0], name=name)


def _final_loss(h, g, tgt, name):
    def body(h_ref, g_ref, t_ref, loss_ref, dh_ref, dg_ref):
        hv = h_ref[...]
        rstd = lax.rsqrt(jnp.mean(hv * hv, axis=-1, keepdims=True) + EPS)
        xh = hv * rstd
        err = xh * g_ref[...] - t_ref[...]
        part = 0.5 * jnp.sum(jnp.mean(err * err, axis=-1, keepdims=True), axis=0, keepdims=True)
        loss_ref[...] += jnp.broadcast_to(part, loss_ref.shape)
        dout = err * (1.0 / hv.shape[1])
        dg_ref[...] += jnp.sum(dout * xh, axis=0, keepdims=True)
        dxh = dout * g_ref[...]
        dh_ref[...] = rstd * (dxh - xh * jnp.mean(dxh * xh, axis=-1, keepdims=True))
    d = h.shape[1]
    return _rows(body, [(h, 'r'), (g, 'f'), (tgt, 'r')], [(LANE, F32, 'a'), (d, F32, 'r'), (d, F32, 'a')],
                 n=h.shape[0], name=name)


def _adamw_math(wv, gv, mv, vv):
    nm = ADAM_B1 * mv + (1.0 - ADAM_B1) * gv
    nv = ADAM_B2 * vv + (1.0 - ADAM_B2) * (gv * gv)
    m_hat = nm / (1.0 - ADAM_B1 ** ADAM_STEP)
    v_hat = nv / (1.0 - ADAM_B2 ** ADAM_STEP)
    return -ADAM_LR * (m_hat / (jnp.sqrt(v_hat) + ADAM_EPS) + ADAM_WD * wv), nm, nv


def _adamw(w, g, m, v, name):
    def body(w_ref, g_ref, m_ref, v_ref, d_ref, nm_ref, nv_ref):
        d_ref[...], nm_ref[...], nv_ref[...] = _adamw_math(w_ref[...], g_ref[...], m_ref[...], v_ref[...])
    c = w.shape[1]
    return _rows(body, [(w, 'r'), (g, 'r'), (m, 'r'), (v, 'r')], [(c, F32, 'r')] * 3, n=w.shape[0], name=name)


def _adamw_halves(w, g_own, g_sib, m, v, c_arr, name):
    r, cols = w.shape
    h = r // 2
    tr = _pick(h, 256, 8)
    per = h // tr

    def body(c_ref, w_ref, go_ref, gs_ref, m_ref, v_ref, g_ref, d_ref, nm_ref, nv_ref):
        mine = (pl.program_id(0) // per) == c_ref[0]
        gv = jnp.where(mine, go_ref[...], gs_ref[...])
        g_ref[...] = gv
        d_ref[...], nm_ref[...], nv_ref[...] = _adamw_math(w_ref[...], gv, m_ref[...], v_ref[...])

    full = pl.BlockSpec((tr, cols), lambda i, c_ref: (i, 0))
    half = pl.BlockSpec((tr, cols), lambda i, c_ref: (i % per, 0))
    grid_spec = pltpu.PrefetchScalarGridSpec(num_scalar_prefetch=1, grid=(2 * per,),
                                             in_specs=[full, half, half, full, full], out_specs=[full] * 4)
    return _pcall(body, name=name, grid_spec=grid_spec, out_shape=[jax.ShapeDtypeStruct((r, cols), F32)] * 4,
                  compiler_params=_params())(c_arr, w, g_own, g_sib, m, v)


def _shift_rows(v, sh, down):
    rolled = pltpu.roll(v, sh if down else v.shape[0] - sh, axis=0)
    row = lax.broadcasted_iota(jnp.int32, v.shape, 0)
    keep = (row >= sh) if down else (row < v.shape[0] - sh)
    return jnp.where(keep, rolled, 0.0)


def _chain_segments(st_r, st_i, pw_r_ref, pw_i_ref, conj, down):
    vr, vi = st_r[...], st_i[...]
    sh, k = 1, 0
    while sh < SEG:
        pr, pi = pw_r_ref[k:k + 1, :], pw_i_ref[k:k + 1, :]
        if conj:
            pi = -pi
        sr, si = _shift_rows(vr, sh, down), _shift_rows(vi, sh, down)
        vr, vi = vr + pr * sr - pi * si, vi + pr * si + pi * sr
        sh, k = sh * 2, k + 1
    st_r[...] = _shift_rows(vr, 1, down)
    st_i[...] = _shift_rows(vi, 1, down)


def _s5_scan(bu, ar8, ai8, pw_r, pw_i, name):
    n = bu.shape[0]
    rb = SCAN_ROWS
    nb, steps, lc = n // rb, rb // SEG, 512

    def body(bu_ref, ar_ref, ai_ref, pwr_ref, pwi_ref, x_ref, st_r, st_i):
        ph, b = pl.program_id(0), pl.program_id(1)

        @pl.when((ph == 0) & (b == 0))
        def _():
            st_r[...] = jnp.zeros_like(st_r)
            st_i[...] = jnp.zeros_like(st_i)

        def scan(store):
            for c in range(S5_GP // lc):
                re, im = slice(c * lc, (c + 1) * lc), slice(S5_GP + c * lc, S5_GP + (c + 1) * lc)
                a_r, a_i = ar_ref[:, re], ai_ref[:, re]

                def step(s, carry):
                    xr, xi = carry
                    rows = pl.ds(pl.multiple_of(s * SEG, SEG), SEG)
                    nr = a_r * xr - a_i * xi + bu_ref[rows, re]
                    ni = a_r * xi + a_i * xr + bu_ref[rows, im]
                    if store:
                        x_ref[rows, re] = nr
                        x_ref[rows, im] = ni
                    return nr, ni

                xr, xi = lax.fori_loop(0, steps, step, (st_r[:, re], st_i[:, re]), unroll=4)
                st_r[:, re] = xr
                st_i[:, re] = xi

        @pl.when(ph == 0)
        def _():
            scan(False)

        @pl.when((ph == 0) & (b == nb - 1))
        def _():
            _chain_segments(st_r, st_i, pwr_ref, pwi_ref, conj=False, down=True)

        @pl.when(ph == 1)
        def _():
            scan(True)

    full = lambda a: pl.BlockSpec(a.shape, lambda ph, b: (0, 0))
    return _pcall(body, name=name, grid=(2, nb),
                  in_specs=[pl.BlockSpec((rb, 2 * S5_GP), lambda ph, b: (b, 0)), full(ar8), full(ai8), full(pw_r), full(pw_i)],
                  out_specs=pl.BlockSpec((rb, 2 * S5_GP), lambda ph, b: (b * ph, 0)),
                  out_shape=jax.ShapeDtypeStruct((n, 2 * S5_GP), F32),
                  scratch_shapes=[pltpu.VMEM((SEG, S5_GP), F32), pltpu.VMEM((SEG, S5_GP), F32)],
                  compiler_params=_params())(bu, ar8, ai8, pw_r, pw_i)


def _s5_scan_bwd(gx, xs, ar8, ai8, pw_r, pw_i, name):
    n = gx.shape[0]
    rb = SCAN_ROWS
    nb, steps, lc = n // rb, rb // SEG, 256

    def body(gx_ref, x_ref, ar_ref, ai_ref, pwr_ref, pwi_ref, lam_ref, da_ref, st_r, st_i):
        ph, b = pl.program_id(0), pl.program_id(1)

        @pl.when((ph == 0) & (b == 0))
        def _():
            st_r[...] = jnp.zeros_like(st_r)
            st_i[...] = jnp.zeros_like(st_i)
            da_ref[...] = jnp.zeros_like(da_ref)

        def scan(store):
            for c in range(S5_GP // lc):
                re, im = slice(c * lc, (c + 1) * lc), slice(S5_GP + c * lc, S5_GP + (c + 1) * lc)
                a_r, a_i = ar_ref[:, re], ai_ref[:, re]

                def step(s, carry):
                    rows = pl.ds(pl.multiple_of((steps - 1 - s) * SEG, SEG), SEG)
                    if store:
                        lr, li, dr, di = carry
                        xr, xi = x_ref[rows, re], x_ref[rows, im]
                        dr = dr + lr * xr + li * xi
                        di = di + li * xr - lr * xi
                    else:
                        lr, li = carry
                    nr = a_r * lr + a_i * li + gx_ref[rows, re]
                    ni = a_r * li - a_i * lr + gx_ref[rows, im]
                    if store:
                        lam_ref[rows, re] = nr
                        lam_ref[rows, im] = ni
                        return nr, ni, dr, di
                    return nr, ni

                if store:
                    lr, li, dr, di = lax.fori_loop(0, steps, step, (st_r[:, re], st_i[:, re], da_ref[:, re], da_ref[:, im]),
                                                   unroll=4)
                    da_ref[:, re] = dr
                    da_ref[:, im] = di
                else:
                    lr, li = lax.fori_loop(0, steps, step, (st_r[:, re], st_i[:, re]), unroll=4)
                st_r[:, re] = lr
                st_i[:, re] = li

        @pl.when(ph == 0)
        def _():
            scan(False)

        @pl.when((ph == 0) & (b == nb - 1))
        def _():
            _chain_segments(st_r, st_i, pwr_ref, pwi_ref, conj=True, down=False)

        @pl.when(ph == 1)
        def _():
            scan(True)

    full = lambda a: pl.BlockSpec(a.shape, lambda ph, b: (0, 0))
    rev = lambda ph, b: (nb - 1 - b, 0)
    return _pcall(body, name=name, grid=(2, nb),
                  in_specs=[pl.BlockSpec((rb, 2 * S5_GP), rev), pl.BlockSpec((rb, 2 * S5_GP), lambda ph, b: ((nb - 1 - b) * ph, 0)),
                            full(ar8), full(ai8), full(pw_r), full(pw_i)],
                  out_specs=[pl.BlockSpec((rb, 2 * S5_GP), lambda ph, b: (nb - 1 - b * ph, 0)),
                             pl.BlockSpec((SEG, 2 * S5_GP), lambda ph, b: (0, 0))],
                  out_shape=[jax.ShapeDtypeStruct((n, 2 * S5_GP), F32), jax.ShapeDtypeStruct((SEG, 2 * S5_GP), F32)],
                  scratch_shapes=[pltpu.VMEM((SEG, S5_GP), F32), pltpu.VMEM((SEG, S5_GP), F32)],
                  compiler_params=_params())(gx, xs, ar8, ai8, pw_r, pw_i)


def _s5_discretize(lam_re, lam_im, log_dt, b_re, b_im):
    dt = jnp.exp(log_dt)[:, None]
    mag = jnp.exp(lam_re * dt)
    ar = mag * jnp.cos(lam_im * dt)
    ai = mag * jnp.sin(lam_im * dt)
    den = lam_re * lam_re + lam_im * lam_im
    nr = ar - 1.0
    fr = (nr * lam_re + ai * lam_im) / den
    fi = (ai * lam_re - nr * lam_im) / den
    bbar_re = fr[:, :, None] * b_re - fi[:, :, None] * b_im
    bbar_im = fr[:, :, None] * b_im + fi[:, :, None] * b_re
    return ar, ai, bbar_re, bbar_im


def _block_diag(t):
    g, a, b = t.shape
    eye = jnp.eye(g, dtype=t.dtype)
    return (t[:, :, None, :] * eye[:, None, :, None]).reshape(g * a, g * b)


def _diag_blocks(m, a, b):
    g = S5_G
    return jnp.einsum('gagb->gab', m.reshape(g, a, g, b))


def _permute_rows(t):
    n = t.shape[0]
    return t.reshape(SEG, n // SEG, t.shape[1]).transpose(1, 0, 2).reshape(n, t.shape[1])


def _unpermute_rows(t):
    n = t.shape[0]
    return t.reshape(n // SEG, SEG, t.shape[1]).transpose(1, 0, 2).reshape(n, t.shape[1])


def _segment_powers(ar, ai, seg_steps):
    pr, pi = ar.reshape(1, S5_GP), ai.reshape(1, S5_GP)
    e = 1
    while e < seg_steps:
        pr, pi = pr * pr - pi * pi, 2.0 * pr * pi
        e *= 2
    assert e == seg_steps, "segment length must be a power of two"
    rows_r, rows_i = [], []
    for _ in range(3):
        rows_r.append(pr)
        rows_i.append(pi)
        pr, pi = pr * pr - pi * pi, 2.0 * pr * pi
    pad = jnp.zeros((SEG - 3, S5_GP), F32)
    return jnp.concatenate(rows_r + [pad], axis=0), jnp.concatenate(rows_i + [pad], axis=0)


NT = (((1,), (1,)), ((), ()))
TN = (((0,), (0,)), ((), ()))


def _dot(a, b, dims=None, exact=False):
    dims = (((1,), (0,)), ((), ())) if dims is None else dims
    if exact:
        return lax.dot_general(a, b, dims, precision=HI, preferred_element_type=F32)
    return lax.dot_general(a.astype(BF16), b.astype(BF16), dims, preferred_element_type=F32)


def _dot01(a, b, dims=None, ones_first=True):
    x = b if ones_first else a
    hi = x.astype(BF16)
    r1 = x - hi.astype(F32)
    mid = r1.astype(BF16)
    lo = (r1 - mid.astype(F32)).astype(BF16)
    parts = [(_dot(a, p, dims) if ones_first else _dot(p, b, dims)) for p in (lo, mid, hi)]
    return (parts[0] + parts[1]) + parts[2]


HEADS = range(4)


def _gla_chunk_fwd(qc, kc, vc, al, wup, bup, s_prev, tril):
    ones = jnp.ones((GLA_CHUNK, GLA_DV), F32)
    z = [_dot(al, wup[h]) + bup[h] for h in HEADS]
    la = [(jnp.minimum(z[h], 0.0) - jnp.log(1.0 + jnp.exp(-jnp.abs(z[h])))) * (1.0 / GLA_TAU) for h in HEADS]
    bc = [_dot01(tril, la[h]) for h in HEADS]
    blb = [_dot01(la[h], ones, TN, ones_first=False) for h in HEADS]
    bl = [bc[h][GLA_CHUNK - 1:GLA_CHUNK, :] for h in HEADS]
    ebc = [jnp.exp(bc[h]) for h in HEADS]
    qt = [qc[h] * (GLA_DK ** -0.5) * ebc[h] for h in HEADS]
    kt = [kc[h] * jnp.exp(-bc[h]) for h in HEADS]
    ke = [kc[h] * jnp.exp(bl[h] - bc[h]) for h in HEADS]
    sc = [_dot(qt[h], kt[h], NT) * tril for h in HEADS]
    oi = [_dot(sc[h], vc[h]) for h in HEADS]
    oo = [_dot(qt[h], s_prev[h]) for h in HEADS]
    o = [oi[h] + oo[h] for h in HEADS]
    return z, bc, bl, blb, ebc, qt, kt, ke, sc, o


GLA_ROWS = 512
GLA_CPB = GLA_ROWS // GLA_CHUNK


def _gla_in_specs(arrs, blk):
    specs = []
    for a in arrs:
        if a.ndim == 3 and a.shape[1] > LANE:
            specs.append(pl.BlockSpec((GLA_HEADS, GLA_ROWS, a.shape[2]), lambda j: (0, blk(j), 0)))
        elif a.ndim == 3:
            specs.append(pl.BlockSpec(a.shape, lambda j: (0, 0, 0)))
        else:
            specs.append(pl.BlockSpec((GLA_ROWS, a.shape[1]), lambda j: (blk(j), 0)))
    return specs


def _tri(lower):
    ri = lax.broadcasted_iota(jnp.int32, (GLA_CHUNK, GLA_CHUNK), 0)
    ci = lax.broadcasted_iota(jnp.int32, (GLA_CHUNK, GLA_CHUNK), 1)
    return ((ri >= ci) if lower else (ri <= ci)).astype(F32)


def _gla_fwd(q, k, v, r, al, wup, bup, gn, name):
    n = q.shape[1]
    nc = n // GLA_CHUNK

    def body(q_ref, k_ref, v_ref, r_ref, al_ref, wup_ref, bup_ref, gn_ref, y_ref, sp_ref, s_ref):
        @pl.when(pl.program_id(0) == 0)
        def _():
            s_ref[...] = jnp.zeros_like(s_ref)

        tril = _tri(True)

        def chunk(c, carry):
            rows = pl.ds(pl.multiple_of(c * GLA_CHUNK, GLA_CHUNK), GLA_CHUNK)
            alc = al_ref[rows, :]
            vc = [v_ref[h, rows, :] for h in HEADS]
            s_prev = [s_ref[h] for h in HEADS]
            _, _, _, blb, _, _, _, ke, _, o = _gla_chunk_fwd(
                [q_ref[h, rows, :] for h in HEADS], [k_ref[h, rows, :] for h in HEADS], vc, alc,
                [wup_ref[h] for h in HEADS], [bup_ref[h] for h in HEADS], s_prev, tril)
            ds = [_dot(ke[h], vc[h], TN) for h in HEADS]
            for h in HEADS:
                rc = r_ref[h, rows, :]
                sp_ref[h, c] = s_prev[h]
                rstd = lax.rsqrt(jnp.mean(o[h] * o[h], axis=-1, keepdims=True) + EPS)
                y_ref[h, rows, :] = o[h] * rstd * gn_ref[h] * (rc * _sigmoid(rc))
                s_ref[h] = jnp.exp(blb[h]) * s_prev[h] + ds[h]
            return carry

        lax.fori_loop(0, GLA_CPB, chunk, 0)

    ins = [q, k, v, r, al, wup, bup, gn]
    return _pcall(body, name=name, grid=(n // GLA_ROWS,), in_specs=_gla_in_specs(ins, lambda j: j),
                  out_specs=[pl.BlockSpec((GLA_HEADS, GLA_ROWS, GLA_DV), lambda j: (0, j, 0)),
                             pl.BlockSpec((GLA_HEADS, GLA_CPB, GLA_DK, GLA_DV), lambda j: (0, j, 0, 0))],
                  out_shape=[jax.ShapeDtypeStruct((GLA_HEADS, n, GLA_DV), F32),
                             jax.ShapeDtypeStruct((GLA_HEADS, nc, GLA_DK, GLA_DV), F32)],
                  scratch_shapes=[pltpu.VMEM((GLA_HEADS, GLA_DK, GLA_DV), F32)],
                  compiler_params=_params())(*ins)


def _gla_bwd(q, k, v, r, al, wup, bup, gn, sp, dy, name):
    n = q.shape[1]
    nc = n // GLA_CHUNK

    nb = n // GLA_ROWS

    def body(q_ref, k_ref, v_ref, r_ref, al_ref, wup_ref, bup_ref, gn_ref, dy_ref, sp_ref,
             dq_ref, dk_ref, dv_ref, dr_ref, dz_ref, dgn_ref, dbup_ref, ds_ref):
        @pl.when(pl.program_id(0) == 0)
        def _():
            ds_ref[...] = jnp.zeros_like(ds_ref)
            dgn_ref[...] = jnp.zeros_like(dgn_ref)
            dbup_ref[...] = jnp.zeros_like(dbup_ref)

        tril, triu = _tri(True), _tri(False)

        def chunk(i, carry):
            c = GLA_CPB - 1 - i
            rows = pl.ds(pl.multiple_of(c * GLA_CHUNK, GLA_CHUNK), GLA_CHUNK)
            alc = al_ref[rows, :]
            qc = [q_ref[h, rows, :] for h in HEADS]
            kc = [k_ref[h, rows, :] for h in HEADS]
            vc = [v_ref[h, rows, :] for h in HEADS]
            s_prev = [sp_ref[h, c] for h in HEADS]
            ds = [ds_ref[h] for h in HEADS]
            z, bc, bl, blb, ebc, qt, kt, ke, sc, o = _gla_chunk_fwd(
                qc, kc, vc, alc, [wup_ref[h] for h in HEADS], [bup_ref[h] for h in HEADS], s_prev, tril)
            do, rstd = [], []
            for h in HEADS:
                rc = r_ref[h, rows, :]
                rs = lax.rsqrt(jnp.mean(o[h] * o[h], axis=-1, keepdims=True) + EPS)
                on = o[h] * rs
                sr = _sigmoid(rc)
                sil = rc * sr
                dyv, gnv = dy_ref[h, rows, :], gn_ref[h]
                dgn_ref[h] += jnp.sum(dyv * on * sil, axis=0, keepdims=True)
                dr_ref[h, rows, :] = dyv * on * gnv * (sr * (1.0 + rc * (1.0 - sr)))
                don = dyv * gnv * sil
                do.append(rs * (don - on * jnp.mean(don * on, axis=-1, keepdims=True)))
            dp = [_dot(do[h], vc[h], NT) * tril for h in HEADS]
            dv1 = [_dot(sc[h], do[h], TN) for h in HEADS]
            dv2 = [_dot(ke[h], ds[h]) for h in HEADS]
            dq2 = [_dot(do[h], s_prev[h], NT) for h in HEADS]
            dke = [_dot(vc[h], ds[h], NT) for h in HEADS]
            ddec = [_dot01(jnp.ones((8, GLA_DV), F32), ds[h] * s_prev[h], NT)[0:1, :] for h in HEADS]
            dsn = [_dot(qt[h], do[h], TN) for h in HEADS]
            dq1 = [_dot(dp[h], kt[h]) for h in HEADS]
            dkt = [_dot(dp[h], qt[h], TN) for h in HEADS]
            dbc, dbl = [], []
            for h in HEADS:
                dqt = dq1[h] + dq2[h]
                dv_ref[h, rows, :] = dv1[h] + dv2[h]
                ds_ref[h] = jnp.exp(blb[h]) * ds[h] + dsn[h]
                dq_ref[h, rows, :] = dqt * (GLA_DK ** -0.5) * ebc[h]
                dk_ref[h, rows, :] = dkt[h] * jnp.exp(-bc[h]) + dke[h] * jnp.exp(bl[h] - bc[h])
                dbc.append(dqt * qt[h] - dkt[h] * kt[h] - dke[h] * ke[h])
                dbl.append(jnp.sum(dke[h] * ke[h], axis=0, keepdims=True) + ddec[h] * jnp.exp(bl[h]))
            dla = [_dot01(triu, dbc[h]) + dbl[h] for h in HEADS]
            for h in HEADS:
                dz = dla[h] * (1.0 - _sigmoid(z[h])) * (1.0 / GLA_TAU)
                dz_ref[h, rows, :] = dz
                dbup_ref[h] += jnp.sum(dz, axis=0, keepdims=True)
            return carry

        lax.fori_loop(0, GLA_CPB, chunk, 0)

    rev = lambda j: nb - 1 - j
    ins = [q, k, v, r, al, wup, bup, gn, dy, sp]
    in_specs = _gla_in_specs(ins[:9], rev) + [pl.BlockSpec((GLA_HEADS, GLA_CPB, GLA_DK, GLA_DV), lambda j: (0, rev(j), 0, 0))]
    hs = lambda w: pl.BlockSpec((GLA_HEADS, GLA_ROWS, w), lambda j: (0, rev(j), 0))
    h1 = lambda w: pl.BlockSpec((GLA_HEADS, 1, w), lambda j: (0, 0, 0))
    sh = lambda w: jax.ShapeDtypeStruct((GLA_HEADS, n, w), F32)
    s1 = lambda w: jax.ShapeDtypeStruct((GLA_HEADS, 1, w), F32)
    return _pcall(body, name=name, grid=(nb,), in_specs=in_specs,
                  out_specs=[hs(GLA_DK), hs(GLA_DK), hs(GLA_DV), hs(GLA_DV), hs(GLA_DK), h1(GLA_DV), h1(GLA_DK)],
                  out_shape=[sh(GLA_DK), sh(GLA_DK), sh(GLA_DV), sh(GLA_DV), sh(GLA_DK), s1(GLA_DV), s1(GLA_DK)],
                  scratch_shapes=[pltpu.VMEM((GLA_HEADS, GLA_DK, GLA_DV), F32)],
                  compiler_params=_params())(*ins)


def _heads(t, w):
    return t.reshape(t.shape[0], GLA_HEADS, w).transpose(1, 0, 2)


def _unheads(t):
    return t.transpose(1, 0, 2).reshape(t.shape[1], GLA_HEADS * t.shape[2])


ANY = pl.BlockSpec(memory_space=pl.ANY)


def _place():
    x, y, c = lax.axis_index("x"), lax.axis_index("y"), lax.axis_index("c")
    chips = [(1 - x, y), (x, 1 - y), (1 - x, 1 - y)]
    return x, y, c, chips


def _remote(src, dst, ssem, rsem, dev):
    return pltpu.make_async_remote_copy(src_ref=src, dst_ref=dst, send_sem=ssem, recv_sem=rsem, device_id=dev,
                                        device_id_type=MESH_ID)


def _half(c, rows):
    h = rows // 2
    return pl.ds(pl.multiple_of(c * h, 8), h)


def _side_gather_ici(shards):
    def copies(ins, outs, ssem, rsem):
        x, y, c, chips = _place()
        mine = 2 * x + y
        cps = []
        for w in range(len(ins)):
            half = _half(c, ins[w].shape[0])
            cps.append(_remote(ins[w], outs[w].at[mine], ssem.at[4 * w], rsem.at[4 * w], (x, y, 1 - c)))
            for k, (px, py) in enumerate(chips):
                cps.append(_remote(ins[w].at[half], outs[w].at[mine, half], ssem.at[4 * w + 1 + k], rsem.at[4 * w + 1 + k],
                                   (px, py, c)))
        return cps

    return _Side(shards, [jax.ShapeDtypeStruct((4,) + s.shape, s.dtype) for s in shards], 4 * len(shards), copies)


def _side_gather_d2d(gathered):
    def copies(ins, outs, ssem, rsem):
        x, y, c, chips = _place()
        cps = []
        for w in range(len(outs)):
            half = _half(c, outs[w].shape[1])
            for k, (px, py) in enumerate(chips):
                theirs = outs[w].at[2 * px + py, half]
                cps.append(_remote(theirs, theirs, ssem.at[3 * w + k], rsem.at[3 * w + k], (x, y, 1 - c)))
        return cps

    return _Side(gathered, [jax.ShapeDtypeStruct(g.shape, g.dtype) for g in gathered], 3 * len(gathered), copies,
                 aliased=True)


def _side_swap_halves(grads):
    def copies(ins, outs, ssem, rsem):
        x, y, c, _ = _place()
        return [_remote(ins[w].at[:, _half(1 - c, ins[w].shape[1]), :], outs[w], ssem.at[w], rsem.at[w], (x, y, 1 - c))
                for w in range(len(ins))]

    return _Side(grads, [jax.ShapeDtypeStruct((4, g.shape[1] // 2, g.shape[2]), g.dtype) for g in grads], len(grads), copies)


def _side_scatter(sums):
    def copies(ins, outs, ssem, rsem):
        x, y, c, chips = _place()
        return [_remote(ins[w].at[2 * px + py], outs[w].at[k], ssem.at[3 * w + k], rsem.at[3 * w + k], (px, py, c))
                for w in range(len(ins)) for k, (px, py) in enumerate(chips)]

    return _Side(sums, [jax.ShapeDtypeStruct((3,) + s.shape[1:], s.dtype) for s in sums], 3 * len(sums), copies)


def _side_swap_reduced(halves):
    def copies(ins, outs, ssem, rsem):
        x, y, c, _ = _place()
        return [_remote(ins[w], outs[w], ssem.at[w], rsem.at[w], (x, y, 1 - c)) for w in range(len(ins))]

    return _Side(halves, [jax.ShapeDtypeStruct(h.shape, h.dtype) for h in halves], len(halves), copies)


def _chip_sum(g, recv, c_arr, name):
    _, r, cols = g.shape
    h = r // 2
    tr = _pick(h, 256, 16)
    g4 = g.reshape(4, 2, h, cols)

    def body(c_ref, g_ref, r_ref, o_ref):
        o_ref[...] = (g_ref[...] + r_ref[...]).astype(BF16)

    grid_spec = pltpu.PrefetchScalarGridSpec(
        num_scalar_prefetch=1, grid=(4, h // tr),
        in_specs=[pl.BlockSpec((None, None, tr, cols), lambda s, i, c_ref: (s, c_ref[0], i, 0)),
                  pl.BlockSpec((None, tr, cols), lambda s, i, c_ref: (s, i, 0))],
        out_specs=pl.BlockSpec((None, tr, cols), lambda s, i, c_ref: (s, i, 0)))
    return _pcall(body, name=name, grid_spec=grid_spec, out_shape=jax.ShapeDtypeStruct((4, h, cols), BF16),
                  compiler_params=_params())(c_arr, g4, recv)


def _owner_sum(sums, others, s_arr, name):
    _, h, cols = sums.shape
    tr = _pick(h, 256, 16)

    def body(s_ref, a_ref, o_ref, out_ref):
        f = lambda v: v.astype(F32)
        out_ref[...] = (f(a_ref[...]) + f(o_ref[0])) + (f(o_ref[1]) + f(o_ref[2]))

    grid_spec = pltpu.PrefetchScalarGridSpec(
        num_scalar_prefetch=1, grid=(h // tr,),
        in_specs=[pl.BlockSpec((None, tr, cols), lambda i, s_ref: (s_ref[0], i, 0)),
                  pl.BlockSpec((3, tr, cols), lambda i, s_ref: (0, i, 0))],
        out_specs=pl.BlockSpec((tr, cols), lambda i, s_ref: (i, 0)))
    return _pcall(body, name=name, grid_spec=grid_spec, out_shape=jax.ShapeDtypeStruct((h, cols), F32),
                  compiler_params=_params())(s_arr, sums, others)


def _allreduce_small(v, name):
    def body(v_ref, o_ref, r0, r1, ssem, rsem):
        x, y, c, chips = _place()
        cp = _remote(v_ref, r0, ssem.at[0], rsem.at[0], (x, y, 1 - c))
        cp.start()
        cp.wait()
        o_ref[...] = v_ref[...] + r0[...]
        cps = []
        for k, (px, py) in enumerate(chips):
            cp = _remote(o_ref, r1.at[k], ssem.at[1 + k], rsem.at[1 + k], (px, py, c))
            cp.start()
            cps.append(cp)
        for cp in cps:
            cp.wait()
        o_ref[...] = (o_ref[...] + r1[0]) + (r1[1] + r1[2])

    vm = pl.BlockSpec(memory_space=pltpu.VMEM)
    return _pcall(body, name=name, in_specs=[vm], out_specs=vm, out_shape=jax.ShapeDtypeStruct(v.shape, F32),
                  scratch_shapes=[pltpu.VMEM(v.shape, F32), pltpu.VMEM((3,) + v.shape, F32),
                                  pltpu.SemaphoreType.DMA((4,)), pltpu.SemaphoreType.DMA((4,))],
                  compiler_params=_params(has_side_effects=True))(v)


def _pack_small(parts):
    flat = jnp.concatenate([p.reshape(-1).astype(F32) for p in parts])
    pad = (-flat.shape[0]) % (64 * LANE)
    return jnp.pad(flat, (0, pad)).reshape(-1, LANE)


def _unpack_small(packed, like):
    flat, out, pos = packed.reshape(-1), [], 0
    for p in like:
        out.append(flat[pos:pos + p.size].reshape(p.shape))
        pos += p.size
    return out


FFN_FWD_ROWS, FFN_BWD_ROWS = 1024, 512


def _ffn_specs(n, d, fs, cap):
    rows = _pick(n, cap, 16)
    row = pl.BlockSpec((rows, d), lambda i, s: (i, 0))
    gain = pl.BlockSpec((1, d), lambda i, s: (0, 0))
    w_col = pl.BlockSpec((None, d, fs), lambda i, s: (s, 0, 0))
    w_row = pl.BlockSpec((None, fs, d), lambda i, s: (s, 0, 0))
    hid = pl.BlockSpec((None, rows, fs), lambda i, s: (s, i, 0))
    return rows, row, gain, w_col, w_row, hid


def _ffn_fwd(h, g, w1, w3, w2, tag, plan):
    n, d = h.shape
    ns, _, fs = w1.shape
    rows, row, gain, w_col, w_row, hid = _ffn_specs(n, d, fs, FFN_FWD_ROWS)

    def body(h_ref, g_ref, w1_ref, w3_ref, w2_ref, out_ref, n1_ref, a_ref, b_ref, hm_ref, acc_ref):
        s = pl.program_id(1)

        @pl.when(s == 0)
        def _():
            xv = h_ref[...]
            rstd = lax.rsqrt(jnp.mean(xv * xv, axis=-1, keepdims=True) + EPS)
            n1_ref[...] = (xv * rstd * g_ref[...]).astype(BF16)
            acc_ref[...] = jnp.zeros_like(acc_ref)

        n1 = n1_ref[...]
        a = jnp.dot(n1, w1_ref[...], preferred_element_type=F32)
        b = jnp.dot(n1, w3_ref[...], preferred_element_type=F32)
        hm = (a * _sigmoid(a) * b).astype(BF16)
        a_ref[...] = a.astype(BF16)
        b_ref[...] = b.astype(BF16)
        hm_ref[...] = hm
        acc_ref[...] += jnp.dot(hm, w2_ref[...], preferred_element_type=F32)

        @pl.when(s == ns - 1)
        def _():
            out_ref[...] = h_ref[...] + 0.5 * acc_ref[...]

    hid_shape = jax.ShapeDtypeStruct((ns, n, fs), BF16)
    plan.before(f"{tag}_fwd")
    out, n1, a, b, hm = _pcall(
        body, name=f"{tag}_fwd", grid=(n // rows, ns), in_specs=[row, gain, w_col, w_col, w_row],
        out_specs=[row, row, hid, hid, hid],
        out_shape=[jax.ShapeDtypeStruct((n, d), F32), jax.ShapeDtypeStruct((n, d), BF16), hid_shape, hid_shape, hid_shape],
        scratch_shapes=[pltpu.VMEM((rows, d), F32)], compiler_params=_params())(h, g, w1, w3, w2)
    plan.after(f"{tag}_fwd")
    return out, (h, n1, a, b, hm)


def _ffn_bwd(dout, saved, g, w1, w3, w2, tag, plan):
    h, n1, a, b, hm = saved
    n, d = h.shape
    ns, _, fs = w1.shape
    rows, row, gain, w_col, w_row, hid = _ffn_specs(n, d, fs, FFN_BWD_ROWS)

    def body(do_ref, h_ref, g_ref, a_ref, b_ref, w1_ref, w3_ref, w2_ref, dh_ref, da_ref, db_ref, dg_ref, acc_ref):
        i, s = pl.program_id(0), pl.program_id(1)

        @pl.when(s == 0)
        def _():
            acc_ref[...] = jnp.zeros_like(acc_ref)

        @pl.when((s == 0) & (i == 0))
        def _():
            dg_ref[...] = jnp.zeros_like(dg_ref)

        dhm = _dot(0.5 * do_ref[...], w2_ref[...], NT)
        av, bv = a_ref[...].astype(F32), b_ref[...].astype(F32)
        sg = _sigmoid(av)
        da = (dhm * bv * (sg * (1.0 + av * (1.0 - sg)))).astype(BF16)
        db = (dhm * av * sg).astype(BF16)
        da_ref[...] = da
        db_ref[...] = db
        acc_ref[...] += _dot(da, w1_ref[...], NT) + _dot(db, w3_ref[...], NT)

        @pl.when(s == ns - 1)
        def _():
            xv, dn = h_ref[...], acc_ref[...]
            rstd = lax.rsqrt(jnp.mean(xv * xv, axis=-1, keepdims=True) + EPS)
            xh = xv * rstd
            dg_ref[...] += jnp.sum(dn * xh, axis=0, keepdims=True)
            dxh = dn * g_ref[...]
            dh_ref[...] = do_ref[...] + rstd * (dxh - xh * jnp.mean(dxh * xh, axis=-1, keepdims=True))

    hid_shape = jax.ShapeDtypeStruct((ns, n, fs), BF16)
    plan.before(f"{tag}_bwd")
    dh, da, db, dg = _pcall(
        body, name=f"{tag}_bwd", grid=(n // rows, ns), in_specs=[row, row, gain, hid, hid, w_col, w_col, w_row],
        out_specs=[row, hid, hid, gain],
        out_shape=[jax.ShapeDtypeStruct((n, d), F32), hid_shape, hid_shape, jax.ShapeDtypeStruct((1, d), F32)],
        scratch_shapes=[pltpu.VMEM((rows, d), F32)], compiler_params=_params())(dout, h, g, a, b, w1, w3, w2)
    plan.after(f"{tag}_bwd")
    plan.before(f"{tag}_gw2")
    gw2 = _mm(hm, dout, ta=True, shard='m', alpha=0.5, name=f"{tag}_gw2")
    plan.after(f"{tag}_gw2")
    gw1 = _mm(n1, da, ta=True, shard='n', name=f"{tag}_gw1")
    gw3 = _mm(n1, db, ta=True, shard='n', name=f"{tag}_gw3")
    return dh, dg, gw1, gw3, gw2


def _local_step(x, tgt, plan):
    n = x.shape[0]
    grads = plan.grads

    def f(name):
        w = plan.get(name)
        return w.reshape(1, D_MODEL) if name.endswith('_norm') and name != 'gla_out_norm' else w

    def carried(tag, fn, *args, **kw):
        plan.before(tag)
        out = fn(*args, **kw)
        plan.after(tag)
        return out

    h1, ffn1 = _ffn_fwd(x, f('ffn1_norm'), f('ffn1_w1'), f('ffn1_w3'), f('ffn1_w2'), "ffn1", plan)
    u = carried("mix_rms", _rms_fwd, h1, f('mix_norm'), "mix_rms")
    w_in = f('w_in')
    w_a = w_in[:, :2048]
    w_al = jnp.pad(w_in[:, 2048:2048 + GLA_RANK], ((0, 0), (0, LANE - GLA_RANK)))
    w_g = w_in[:, 2048 + GLA_RANK:]
    za = _mm(u, w_a, name="in_a")
    zg = _mm(u, w_g, name="in_g")
    al = _mm(u, w_al, name="in_al")
    ar, ai, bbar_re, bbar_im = _s5_discretize(f('s5_lambda_re'), f('s5_lambda_im'), f('s5_log_dt'), f('s5_b_re'), f('s5_b_im'))
    b_blk = jnp.concatenate([_block_diag(bbar_re.transpose(0, 2, 1)), _block_diag(bbar_im.transpose(0, 2, 1))], axis=1)
    c_blk = jnp.concatenate([_block_diag(f('s5_c_re').transpose(0, 2, 1)), -_block_diag(f('s5_c_im').transpose(0, 2, 1))], axis=0)
    b_blk, c_blk = b_blk.astype(BF16), c_blk.astype(BF16)
    ar8 = jnp.broadcast_to(ar.reshape(1, S5_GP), (SEG, S5_GP))
    ai8 = jnp.broadcast_to(ai.reshape(1, S5_GP), (SEG, S5_GP))
    pw_r, pw_i = _segment_powers(ar, ai, n // SEG)
    dskip = f('s5_d').reshape(1, S5_W)
    u_s5 = _permute_rows(za[:, :S5_W])
    bu = _mm(u_s5, b_blk, name="s5_bu")
    xs = _s5_scan(bu, ar8, ai8, pw_r, pw_i, "s5_scan")
    ys_p = _mm(xs, c_blk, res=_scale_rows(u_s5, dskip, "s5_skip"), name="s5_y")
    ys = _unpermute_rows(ys_p)
    zgelu = _gelu_fwd(ys, "s5_gelu")
    t_glu = _mm(zgelu, f('s5_glu_w'), bias=f('s5_glu_b').reshape(1, S5_W), name="s5_glu_t")
    y_s5 = _glu_fwd(zgelu, t_glu, "s5_glu")
    q, k = _heads(za[:, 512:768], GLA_DK), _heads(za[:, 768:1024], GLA_DK)
    v, r = _heads(za[:, 1024:1536], GLA_DV), _heads(za[:, 1536:2048], GLA_DV)
    wup = jnp.pad(f('gla_a_up_w'), ((0, LANE - GLA_RANK), (0, 0)))
    wup_h = wup.reshape(LANE, GLA_HEADS, GLA_DK).transpose(1, 0, 2)
    bup_h = f('gla_a_up_b').reshape(GLA_HEADS, 1, GLA_DK)
    gn_h = f('gla_out_norm').reshape(GLA_HEADS, 1, GLA_DV)
    y_gla_h, s_prev = carried("gla_fwd", _gla_fwd, q, k, v, r, al, wup_h, bup_h, gn_h, "gla_fwd")
    y_gla = _unheads(y_gla_h).astype(BF16)
    ps = _mm(y_s5, f('proj_s5'), name="proj_s5")
    pg = carried("proj_gla", _mm, y_gla, f('proj_gla'), name="proj_gla")
    merged = _merge_fwd(zg, ps, pg, "merge")
    h2 = _mm(merged, f('w_out'), res=h1, name="w_out")
    h3, ffn2 = _ffn_fwd(h2, f('ffn2_norm'), f('ffn2_w1'), f('ffn2_w3'), f('ffn2_w2'), "ffn2", plan)
    loss, dh3, g_final = _final_loss(h3, f('final_norm').reshape(1, D_MODEL), tgt, "loss")
    grads['final_norm'] = g_final.reshape(D_MODEL)
    dh2, grads['ffn2_norm'], grads['ffn2_w1'], grads['ffn2_w3'], grads['ffn2_w2'] = _ffn_bwd(
        dh3, ffn2, f('ffn2_norm'), f('ffn2_w1'), f('ffn2_w3'), f('ffn2_w2'), "ffn2", plan)
    dh2b = dh2.astype(BF16)
    dm = _mm(dh2b, f('w_out'), tb=True, name="d_merged")
    grads['w_out'] = _mm(merged, dh2b, ta=True, name="g_w_out")
    dps, dpg, dzg = carried("d_merge", _merge_bwd, dm, zg, ps, pg, "d_merge")
    grads['proj_s5'] = _mm(y_s5, dps, ta=True, name="g_proj_s5")
    grads['proj_gla'] = _mm(y_gla, dpg, ta=True, name="g_proj_gla")
    dy_s5 = _mm(dps, f('proj_s5'), tb=True, name="d_y_s5")
    dy_gla = _mm(dpg, f('proj_gla'), tb=True, name="d_y_gla")
    dzgelu, dt_glu, g_glu_b = _glu_bwd1(dy_s5, zgelu, t_glu, "d_glu")
    grads['s5_glu_b'] = g_glu_b.reshape(S5_W)
    grads['s5_glu_w'] = _mm(zgelu, dt_glu, ta=True, name="g_glu_w")
    dzgelu = _mm(dt_glu, f('s5_glu_w'), tb=True, res=dzgelu, name="d_gelu")
    dys, du_skip, g_d = _glu_bwd2(_permute_rows(dzgelu), ys_p, u_s5, dskip, "d_s5_y")
    grads['s5_d'] = g_d.reshape(S5_G, S5_H)
    gx = _mm(dys, c_blk, tb=True, name="s5_gx")
    lam, da8 = _s5_scan_bwd(gx, xs, ar8, ai8, pw_r, pw_i, "s5_scan_bwd")
    g_c = _mm(dys, xs, ta=True, name="g_s5_c")
    grads['s5_c_re'] = _diag_blocks(g_c[:, :S5_GP], S5_H, S5_P)
    grads['s5_c_im'] = -_diag_blocks(g_c[:, S5_GP:], S5_H, S5_P)
    g_b = _mm(lam, u_s5, ta=True, name="g_s5_b")
    g_bbar_re = _diag_blocks(g_b[:S5_GP], S5_P, S5_H)
    g_bbar_im = _diag_blocks(g_b[S5_GP:], S5_P, S5_H)
    da = jnp.sum(da8, axis=0)
    g_ar, g_ai = da[:S5_GP].reshape(S5_G, S5_P), da[S5_GP:].reshape(S5_G, S5_P)
    _, disc_vjp = jax.vjp(_s5_discretize, f('s5_lambda_re'), f('s5_lambda_im'), f('s5_log_dt'), f('s5_b_re'), f('s5_b_im'))
    (grads['s5_lambda_re'], grads['s5_lambda_im'], grads['s5_log_dt'], grads['s5_b_re'],
     grads['s5_b_im']) = disc_vjp((g_ar, g_ai, g_bbar_re, g_bbar_im))
    du_s5 = _unpermute_rows(_mm(lam, b_blk, tb=True, res=du_skip, name="d_s5_u"))
    dq, dk, dv, dr, dz, dgn, dbup = carried("gla_bwd", _gla_bwd, q, k, v, r, al, wup_h, bup_h, gn_h, s_prev,
                                            _heads(dy_gla, GLA_DV), "gla_bwd")
    grads['gla_out_norm'] = dgn.reshape(GLA_HEADS * GLA_DV)
    grads['gla_a_up_b'] = dbup.reshape(GLA_HEADS * GLA_DK)
    dz = _unheads(dz)
    grads['gla_a_up_w'] = _mm(al, dz, ta=True, name="g_a_up")[:GLA_RANK]
    dal = _mm(dz, wup, tb=True, name="d_a_low")
    dza = jnp.concatenate([du_s5, _unheads(dq), _unheads(dk), _unheads(dv), _unheads(dr)], axis=1)
    g_wa = _mm(u, dza, ta=True, name="g_in_a")
    g_wg = _mm(u, dzg, ta=True, name="g_in_g")
    g_wal = _mm(u, dal, ta=True, name="g_in_al")
    grads['w_in'] = jnp.concatenate([g_wa, g_wal[:, :GLA_RANK], g_wg], axis=1)
    du = carried("d_u_a", _mm, dza, w_a, tb=True, name="d_u_a")
    du = _mm(dzg, w_g, tb=True, res=du, name="d_u_g")
    du = _mm(dal, w_al, tb=True, res=du, name="d_u_al")
    dh1, g_mix = carried("d_mix_rms", _rms_bwd, h1, f('mix_norm'), du, dh2, "d_mix_rms")
    grads['mix_norm'] = g_mix
    dx, grads['ffn1_norm'], grads['ffn1_w1'], grads['ffn1_w3'], grads['ffn1_w2'] = _ffn_bwd(
        dh1, ffn1, f('ffn1_norm'), f('ffn1_w1'), f('ffn1_w3'), f('ffn1_w2'), "ffn1", plan)
    return loss[0, 0], dx


MIXER_WEIGHTS = ['w_in', 's5_glu_w', 'proj_s5', 'proj_gla', 'w_out', 'gla_a_up_w']
FFN1_WEIGHTS, FFN2_WEIGHTS = FFN_WEIGHTS[:3], FFN_WEIGHTS[3:]
GRAD_GROUPS = {'ffn2': FFN2_WEIGHTS, 'mixer': ['w_out', 'proj_s5', 'proj_gla', 's5_glu_w', 'w_in'], 'ffn1': FFN1_WEIGHTS}


class _Plan:
    def __init__(self, a, c_arr, s_arr):
        self.a, self.c_arr, self.s_arr = a, c_arr, s_arr
        self.grads, self.weights, self.riding = {}, {}, {}
        self.g4s, self.chip_sums, self.halves, self.sib_halves = {}, {}, {}, {}
        for nm in SMALL:
            if nm != 'gla_a_up_w':
                self.weights[nm] = a[nm] if nm == 'final_norm' else a[nm][0]
        ici = _side_gather_ici(self._shards(FFN1_WEIGHTS))
        _run_side(ici, "gather_ffn1_ici")
        self._gathered(FFN1_WEIGHTS, _run_side(_side_gather_d2d(ici.outs), "gather_ffn1_d2d"))

    def _shards(self, names):
        return [self.a[nm][0].astype(F32 if nm == 'gla_a_up_w' else BF16) for nm in names]

    def _gathered(self, names, arrs):
        for nm, g4 in zip(names, arrs):
            if nm in FFN_WEIGHTS:
                self.weights[nm] = g4
            elif nm in COL_SHARDED:
                self.weights[nm] = jnp.concatenate([g4[s] for s in range(4)], axis=1)
            else:
                self.weights[nm] = g4.reshape(4 * g4.shape[1], g4.shape[2])

    def get(self, name):
        return self.weights[name]

    def _shard_major(self, nm):
        g = self.grads[nm]
        if nm in FFN_WEIGHTS:
            return g
        if nm in COL_SHARDED:
            return jnp.stack(jnp.split(g, 4, axis=1))
        return g.reshape(4, g.shape[0] // 4, g.shape[1])

    def _schedule(self, tag):
        grp = GRAD_GROUPS
        if tag == "ffn1_fwd":
            return _side_gather_ici(self._shards(MIXER_WEIGHTS)), lambda outs: self.riding.update(mixer_ici=outs)
        if tag == "mix_rms":
            return _side_gather_d2d(self.riding['mixer_ici']), lambda outs: self._gathered(MIXER_WEIGHTS, outs)
        if tag == "gla_fwd":
            return _side_gather_ici(self._shards(FFN2_WEIGHTS)), lambda outs: self.riding.update(ffn2_ici=outs)
        if tag == "proj_gla":
            return _side_gather_d2d(self.riding['ffn2_ici']), lambda outs: self._gathered(FFN2_WEIGHTS, outs)
        steps = {"d_merge": ('ffn2', 0), "gla_bwd": ('ffn2', 1), "d_mix_rms": ('ffn2', 2),
                 "d_u_a": ('mixer', 0), "ffn1_bwd": ('mixer', 1), "ffn1_gw2": ('mixer', 2)}
        if tag in steps:
            group, stage = steps[tag]
            return self._reduce_stage(grp[group], stage)
        return None

    def _reduce_stage(self, names, stage):
        if stage == 0:
            for nm in names:
                self.g4s[nm] = self._shard_major(nm)

            def done(outs):
                for nm, r in zip(names, outs):
                    self.chip_sums[nm] = _chip_sum(self.g4s[nm], r, self.c_arr, f"chip_sum_{nm}")
            return _side_swap_halves([self.g4s[nm] for nm in names]), done
        if stage == 1:
            def done(outs):
                for nm, o in zip(names, outs):
                    self.halves[nm] = _owner_sum(self.chip_sums[nm], o, self.s_arr, f"owner_sum_{nm}")
            return _side_scatter([self.chip_sums[nm] for nm in names]), done

        def done(outs):
            self.sib_halves.update(zip(names, outs))
        return _side_swap_reduced([self.halves[nm] for nm in names]), done

    def before(self, tag):
        entry = self._schedule(tag)
        if entry is not None:
            side, done = entry
            self.riding[tag] = (side, done)
            _RIDER.append(side)

    def after(self, tag):
        if tag in self.riding:
            side, done = self.riding.pop(tag)
            assert not _RIDER and side.outs is not None, tag
            done(side.outs)

    def finish(self):
        names = GRAD_GROUPS['ffn1']
        for stage in range(3):
            side, done = self._reduce_stage(names, stage)
            done(_run_side(side, f"grad_ffn1_stage{stage}"))


def _train_step(a):
    x = a['x'][0]
    tgt = a['loss_target'][0]
    xi, yi, ci = lax.axis_index("x"), lax.axis_index("y"), lax.axis_index("c")
    c_arr = jnp.reshape(ci, (1,)).astype(jnp.int32)
    s_arr = jnp.reshape(2 * xi + yi, (1,)).astype(jnp.int32)
    plan = _Plan(a, c_arr, s_arr)
    loss, dx = _local_step(x, tgt, plan)
    plan.finish()
    grads = plan.grads
    loss = lax.psum(loss, ("x", "y", "c"))
    halves = [plan.halves[nm] for nm in SHARDED]
    sib_halves = [plan.sib_halves[nm] for nm in SHARDED]
    red = {}
    small_parts = [grads[nm].reshape(a[nm].shape) for nm in SMALL if nm != 'gla_a_up_w'] + [grads['gla_a_up_w']]
    small_sum = _unpack_small(_allreduce_small(_pack_small(small_parts), "allreduce_small"), small_parts)
    small_names = [nm for nm in SMALL if nm != 'gla_a_up_w']
    for nm, g in zip(small_names, small_sum[:-1]):
        red[nm] = g
    g_up = small_sum[-1]
    red['gla_a_up_w'] = lax.dynamic_slice(g_up, (0, (2 * xi + yi) * GLA_DK), (GLA_RANK, GLA_DK))
    out_g, out_d, out_m, out_v = {}, {}, {}, {}
    for nm, own, sib in zip(SHARDED, halves, sib_halves):
        g, d, nm_, nv_ = _adamw_halves(a[nm][0], own, sib, a['m_' + nm][0], a['v_' + nm][0], c_arr, f"adamw_{nm}")
        shape = a[nm].shape
        out_g[nm], out_d[nm], out_m[nm], out_v[nm] = (t.reshape(shape) for t in (g, d, nm_, nv_))
    rest = [nm for nm in WEIGHTS if nm not in SHARDED]
    pk = lambda pre: _pack_small([a[pre + nm] for nm in rest])
    d, nm_, nv_ = _adamw(pk(''), _pack_small([red[nm] for nm in rest]), pk('m_'), pk('v_'), "adamw_small")
    like = [a[nm] for nm in rest]
    for nm, g, dd, mm_, vv_ in zip(rest, [red[nm].reshape(a[nm].shape) for nm in rest], _unpack_small(d, like),
                                   _unpack_small(nm_, like), _unpack_small(nv_, like)):
        out_g[nm], out_d[nm], out_m[nm], out_v[nm] = g, dd, mm_, vv_
    return (loss, dx[None], *[out_g[nm] for nm in WEIGHTS], *[out_d[nm] for nm in WEIGHTS],
            *[out_m[nm] for nm in WEIGHTS], *[out_v[nm] for nm in WEIGHTS])


def kernel(x, ffn1_norm, ffn1_w1, ffn1_w3, ffn1_w2, mix_norm, w_in, s5_lambda_re, s5_lambda_im, s5_log_dt, s5_b_re, s5_b_im, s5_c_re, s5_c_im, s5_d, s5_glu_w, s5_glu_b, gla_a_up_w, gla_a_up_b, gla_out_norm, proj_s5, proj_gla, w_out, ffn2_norm, ffn2_w1, ffn2_w3, ffn2_w2, final_norm, loss_target, m_ffn1_norm, m_ffn1_w1, m_ffn1_w3, m_ffn1_w2, m_mix_norm, m_w_in, m_s5_lambda_re, m_s5_lambda_im, m_s5_log_dt, m_s5_b_re, m_s5_b_im, m_s5_c_re, m_s5_c_im, m_s5_d, m_s5_glu_w, m_s5_glu_b, m_gla_a_up_w, m_gla_a_up_b, m_gla_out_norm, m_proj_s5, m_proj_gla, m_w_out, m_ffn2_norm, m_ffn2_w1, m_ffn2_w3, m_ffn2_w2, m_final_norm, v_ffn1_norm, v_ffn1_w1, v_ffn1_w3, v_ffn1_w2, v_mix_norm, v_w_in, v_s5_lambda_re, v_s5_lambda_im, v_s5_log_dt, v_s5_b_re, v_s5_b_im, v_s5_c_re, v_s5_c_im, v_s5_d, v_s5_glu_w, v_s5_glu_b, v_gla_a_up_w, v_gla_a_up_b, v_gla_out_norm, v_proj_s5, v_proj_gla, v_w_out, v_ffn2_norm, v_ffn2_w1, v_ffn2_w3, v_ffn2_w2, v_final_norm):
    return _train_step(dict(locals()))
```

```python
import functools

import jax
import jax.numpy as jnp
from jax import lax
from jax.experimental import pallas as pl
from jax.experimental.pallas import tpu as pltpu

F32 = jnp.float32
BF16 = jnp.bfloat16
HI = lax.Precision.HIGHEST
MESH_ID = pl.DeviceIdType.MESH

D_MODEL = 1024
EPS = 1e-6
S5_G, S5_P, S5_H = 32, 64, 16
S5_W = S5_G * S5_H
S5_GP = S5_G * S5_P
SEG = 8
SCAN_ROWS = 256
GLA_HEADS, GLA_DK, GLA_DV = 4, 64, 128
GLA_CHUNK = 64
GLA_TAU = 16.0
GLA_RANK = 16
ADAM_LR, ADAM_B1, ADAM_B2, ADAM_EPS, ADAM_WD, ADAM_STEP = 0.001, 0.9, 0.999, 1e-08, 0.01, 10
V7X_VMEM_LIMIT = 56 * 1024 * 1024
LANE = 128

WEIGHTS = ['ffn1_norm', 'ffn1_w1', 'ffn1_w3', 'ffn1_w2', 'mix_norm', 'w_in', 's5_lambda_re', 's5_lambda_im',
           's5_log_dt', 's5_b_re', 's5_b_im', 's5_c_re', 's5_c_im', 's5_d', 's5_glu_w', 's5_glu_b', 'gla_a_up_w',
           'gla_a_up_b', 'gla_out_norm', 'proj_s5', 'proj_gla', 'w_out', 'ffn2_norm', 'ffn2_w1', 'ffn2_w3',
           'ffn2_w2', 'final_norm']
SHARDED = ['ffn1_w1', 'ffn1_w3', 'ffn1_w2', 'w_in', 's5_glu_w', 'proj_s5', 'proj_gla', 'w_out',
           'ffn2_w1', 'ffn2_w3', 'ffn2_w2']
COL_SHARDED = ['ffn1_w1', 'ffn1_w3', 'w_in', 'proj_s5', 'proj_gla', 'ffn2_w1', 'ffn2_w3', 'gla_a_up_w']
SMALL = [n for n in WEIGHTS if n not in SHARDED]
FFN_WEIGHTS = ['ffn1_w1', 'ffn1_w3', 'ffn1_w2', 'ffn2_w1', 'ffn2_w3', 'ffn2_w2']


def _params(**kw):
    return pltpu.CompilerParams(vmem_limit_bytes=V7X_VMEM_LIMIT, **kw)


class _Side:
    def __init__(self, ins, out_shapes, nsem, copies, aliased=False):
        self.ins, self.out_shapes, self.nsem, self.copies, self.aliased = list(ins), list(out_shapes), nsem, copies, aliased
        self.outs = None


_RIDER = []


def _pcall(body, **kw):
    if _RIDER:
        return _carry(body, _RIDER.pop(), **kw)
    return pl.pallas_call(body, **kw)


def _carry(body, side, *, name, grid, in_specs, out_specs, out_shape, scratch_shapes=(), compiler_params=None):
    del compiler_params
    single = not isinstance(out_shape, (list, tuple))
    out_specs = [out_specs] if single else list(out_specs)
    out_shape = [out_shape] if single else list(out_shape)
    n_in, n_out, n_scr = len(in_specs), len(out_shape), len(scratch_shapes)
    s_in, s_out = len(side.ins), len(side.out_shapes)
    any_spec = pl.BlockSpec(memory_space=pl.ANY)

    def wrapped(*refs):
        cuts = [n_in, s_in, n_out, s_out, n_scr]
        parts, pos = [], 0
        for c in cuts:
            parts.append(refs[pos:pos + c])
            pos += c
        ins, sins, outs, souts, scr = parts
        ssem, rsem = refs[pos], refs[pos + 1]
        first = last = None
        for d, g in enumerate(grid):
            i = pl.program_id(d)
            first = (i == 0) if first is None else first & (i == 0)
            last = (i == g - 1) if last is None else last & (i == g - 1)

        @pl.when(first)
        def _():
            for cp in side.copies(sins, souts, ssem, rsem):
                cp.start()

        body(*ins, *outs, *scr)

        @pl.when(last)
        def _():
            for cp in side.copies(sins, souts, ssem, rsem):
                cp.wait()

    call = pl.pallas_call(
        wrapped, name=name, grid=grid, in_specs=list(in_specs) + [any_spec] * s_in,
        out_specs=out_specs + [any_spec] * s_out, out_shape=out_shape + side.out_shapes,
        scratch_shapes=list(scratch_shapes) + [pltpu.SemaphoreType.DMA((side.nsem,)), pltpu.SemaphoreType.DMA((side.nsem,))],
        input_output_aliases={n_in + j: n_out + j for j in range(s_in)} if side.aliased else {},
        compiler_params=_params(has_side_effects=True))

    def run(*args):
        res = call(*args, *side.ins)
        side.outs = list(res[n_out:])
        return res[0] if single else list(res[:n_out])

    return run


def _run_side(side, name):
    s_in, s_out = len(side.ins), len(side.out_shapes)
    any_spec = pl.BlockSpec(memory_space=pl.ANY)

    def body(*refs):
        sins, souts = refs[:s_in], refs[s_in:s_in + s_out]
        ssem, rsem = refs[s_in + s_out:]
        cps = side.copies(sins, souts, ssem, rsem)
        for cp in cps:
            cp.start()
        for cp in cps:
            cp.wait()

    side.outs = list(pl.pallas_call(
        body, name=name, in_specs=[any_spec] * s_in, out_specs=[any_spec] * s_out, out_shape=side.out_shapes,
        scratch_shapes=[pltpu.SemaphoreType.DMA((side.nsem,)), pltpu.SemaphoreType.DMA((side.nsem,))],
        input_output_aliases={j: j for j in range(s_in)} if side.aliased else {},
        compiler_params=pltpu.CompilerParams(has_side_effects=True))(*side.ins))
    return side.outs


def _pick(n, cap, quantum):
    if n <= cap:
        return n
    best = None
    for t in range(quantum, cap + 1, quantum):
        if n % t == 0:
            best = t
    assert best is not None, (n, cap, quantum)
    return best


def _sigmoid(x):
    return jax.nn.sigmoid(x)


def _mm(a, b, *, name, ta=False, tb=False, out_dtype=F32, alpha=1.0, res=None, bias=None, exact=False, shard=None):
    ns = 4
    (k_a, m) = a.shape[-2:] if ta else a.shape[-2:][::-1]
    (k_b, n) = b.shape[-2:][::-1] if tb else b.shape[-2:]
    assert k_a == k_b, (a.shape, b.shape, ta, tb)
    assert (a.ndim == 3) == (shard in ('k', 'm')) and (b.ndim == 3) == (shard in ('n', 'k'))
    k = k_a
    tm = _pick(m, 1024, 128)
    tn = _pick(n, 1024, 128)
    tk = _pick(k, 1024, 128)
    pm, pn, pk = m // tm, n // tn, k // tk
    gm = pm * (ns if shard == 'm' else 1)
    gn = pn * (ns if shard == 'n' else 1)
    gk = pk * (ns if shard == 'k' else 1)
    dims = (((0,) if ta else (1,), (1,) if tb else (0,)), ((), ()))
    op_dtype = F32 if exact else BF16

    def body(*refs):
        a_ref, b_ref = refs[0], refs[1]
        pos = 2
        res_ref = bias_ref = None
        if res is not None:
            res_ref = refs[pos]
            pos += 1
        if bias is not None:
            bias_ref = refs[pos]
            pos += 1
        o_ref, acc_ref = refs[pos], refs[pos + 1]
        kk = pl.program_id(2)

        @pl.when(kk == 0)
        def _():
            acc_ref[...] = jnp.zeros_like(acc_ref)

        acc_ref[...] += lax.dot_general(a_ref[...].astype(op_dtype), b_ref[...].astype(op_dtype), dims,
                                        precision=HI if exact else None, preferred_element_type=F32)

        @pl.when(kk == gk - 1)
        def _():
            o = acc_ref[...]
            if alpha != 1.0:
                o = o * alpha
            if bias_ref is not None:
                o = o + bias_ref[...]
            if res_ref is not None:
                o = o + res_ref[...]
            o_ref[...] = o.astype(out_dtype)

    def spec(block, sharded_on, order):
        per = {'m': pm, 'n': pn, 'k': pk}

        def index(i, j, kk):
            g = {'m': i, 'n': j, 'k': kk}
            r, c = order(i % pm if shard == 'm' else i, j % pn if shard == 'n' else j, kk % pk if shard == 'k' else kk)
            if sharded_on is None:
                return (r, c)
            return (g[sharded_on] // per[sharded_on], r, c)

        return pl.BlockSpec(block if sharded_on is None else (None,) + block, index)

    a_sh = shard if shard in ('k', 'm') else None
    b_sh = shard if shard in ('n', 'k') else None
    o_sh = shard if shard in ('n', 'm') else None
    a_spec = spec((tk, tm), a_sh, lambda i, j, kk: (kk, i)) if ta else spec((tm, tk), a_sh, lambda i, j, kk: (i, kk))
    b_spec = spec((tn, tk), b_sh, lambda i, j, kk: (j, kk)) if tb else spec((tk, tn), b_sh, lambda i, j, kk: (kk, j))
    ins, in_specs = [a, b], [a_spec, b_spec]
    if res is not None:
        assert o_sh is None
        ins.append(res)
        in_specs.append(pl.BlockSpec((tm, tn), lambda i, j, kk: (i, j)))
    if bias is not None:
        assert o_sh is None
        ins.append(bias)
        in_specs.append(pl.BlockSpec((1, tn), lambda i, j, kk: (0, j)))
    out_shape = (m, n) if o_sh is None else (ns, m, n)
    return _pcall(body, name=name, grid=(gm, gn, gk), in_specs=in_specs,
                  out_specs=spec((tm, tn), o_sh, lambda i, j, kk: (i, j)),
                  out_shape=jax.ShapeDtypeStruct(out_shape, out_dtype),
                  scratch_shapes=[pltpu.VMEM((tm, tn), F32)], compiler_params=_params())(*ins)


def _rows(body, ins, outs, *, n, name, tm=256):
    tm = _pick(n, tm, 16)
    in_specs = []
    for arr, kind in ins:
        if kind == 'r':
            in_specs.append(pl.BlockSpec((tm, arr.shape[1]), lambda i: (i, 0)))
        else:
            in_specs.append(pl.BlockSpec(arr.shape, lambda i: (0, 0)))
    out_specs, out_shape = [], []
    for cols, dtype, kind in outs:
        if kind == 'r':
            out_specs.append(pl.BlockSpec((tm, cols), lambda i: (i, 0)))
            out_shape.append(jax.ShapeDtypeStruct((n, cols), dtype))
        else:
            out_specs.append(pl.BlockSpec((1, cols), lambda i: (0, 0)))
            out_shape.append(jax.ShapeDtypeStruct((1, cols), dtype))
    n_in = len(ins)
    acc_ids = [j for j, o in enumerate(outs) if o[2] == 'a']

    def wrapped(*refs):
        if acc_ids:
            @pl.when(pl.program_id(0) == 0)
            def _():
                for j in acc_ids:
                    refs[n_in + j][...] = jnp.zeros_like(refs[n_in + j])
        body(*refs)

    res = _pcall(wrapped, name=name, grid=(n // tm,), in_specs=in_specs, out_specs=out_specs, out_shape=out_shape,
                 compiler_params=_params())(*[a for a, _ in ins])
    return res


def _rms_fwd(x, g, name):
    def body(x_ref, g_ref, o_ref):
        xv = x_ref[...]
        rstd = lax.rsqrt(jnp.mean(xv * xv, axis=-1, keepdims=True) + EPS)
        o_ref[...] = (xv * rstd * g_ref[...]).astype(BF16)
    return _rows(body, [(x, 'r'), (g, 'f')], [(x.shape[1], BF16, 'r')], n=x.shape[0], name=name)[0]


def _rms_bwd(x, g, dn, dres, name):
    def body(x_ref, g_ref, dn_ref, dres_ref, dx_ref, dg_ref):
        xv = x_ref[...]
        rstd = lax.rsqrt(jnp.mean(xv * xv, axis=-1, keepdims=True) + EPS)
        xh = xv * rstd
        dn = dn_ref[...]
        dg_ref[...] += jnp.sum(dn * xh, axis=0, keepdims=True)
        dxh = dn * g_ref[...]
        dx_ref[...] = dres_ref[...] + rstd * (dxh - xh * jnp.mean(dxh * xh, axis=-1, keepdims=True))
    d = x.shape[1]
    return _rows(body, [(x, 'r'), (g, 'f'), (dn, 'r'), (dres, 'r')], [(d, F32, 'r'), (d, F32, 'a')],
                 n=x.shape[0], name=name)


def _gelu_parts(y):
    c0 = 0.7978845608028654
    inner = c0 * (y + 0.044715 * y * y * y)
    th = jnp.tanh(inner)
    return th, c0 * (1.0 + 3.0 * 0.044715 * y * y)


def _gelu_fwd(y, name):
    def body(y_ref, o_ref):
        yv = y_ref[...]
        th, _ = _gelu_parts(yv)
        o_ref[...] = 0.5 * yv * (1.0 + th)
    return _rows(body, [(y, 'r')], [(y.shape[1], F32, 'r')], n=y.shape[0], name=name)[0]


def _glu_fwd(zg, t, name):
    def body(z_ref, t_ref, o_ref):
        o_ref[...] = (z_ref[...] * _sigmoid(t_ref[...])).astype(BF16)
    return _rows(body, [(zg, 'r'), (t, 'r')], [(zg.shape[1], BF16, 'r')], n=zg.shape[0], name=name)[0]


def _glu_bwd1(dy, zg, t, name):
    def body(dy_ref, z_ref, t_ref, dz_ref, dt_ref, db_ref):
        dyv, zv = dy_ref[...], z_ref[...]
        sg = _sigmoid(t_ref[...])
        dz_ref[...] = dyv * sg
        dt = dyv * zv * sg * (1.0 - sg)
        dt_ref[...] = dt.astype(BF16)
        db_ref[...] += jnp.sum(dt, axis=0, keepdims=True)
    w = zg.shape[1]
    return _rows(body, [(dy, 'r'), (zg, 'r'), (t, 'r')], [(w, F32, 'r'), (w, BF16, 'r'), (w, F32, 'a')],
                 n=zg.shape[0], name=name)


def _glu_bwd2(dzg, ys, u, dskip, name):
    def body(dz_ref, y_ref, u_ref, d_ref, dy_ref, du_ref, dd_ref):
        yv = y_ref[...]
        th, dinner = _gelu_parts(yv)
        dy = dz_ref[...] * (0.5 * (1.0 + th) + 0.5 * yv * (1.0 - th * th) * dinner)
        dy_ref[...] = dy
        du_ref[...] = dy * d_ref[...]
        dd_ref[...] += jnp.sum(dy * u_ref[...], axis=0, keepdims=True)
    w = ys.shape[1]
    return _rows(body, [(dzg, 'r'), (ys, 'r'), (u, 'r'), (dskip, 'f')], [(w, F32, 'r'), (w, F32, 'r'), (w, F32, 'a')],
                 n=ys.shape[0], name=name)


def _scale_rows(u, dskip, name):
    def body(u_ref, d_ref, o_ref):
        o_ref[...] = u_ref[...] * d_ref[...]
    return _rows(body, [(u, 'r'), (dskip, 'f')], [(u.shape[1], F32, 'r')], n=u.shape[0], name=name)[0]


def _merge_fwd(zg, ps, pg, name):
    def body(z_ref, ps_ref, pg_ref, o_ref):
        zv = z_ref[...]
        o_ref[...] = (_sigmoid(zv[:, :D_MODEL]) * ps_ref[...] + _sigmoid(zv[:, D_MODEL:]) * pg_ref[...]).astype(BF16)
    return _rows(body, [(zg, 'r'), (ps, 'r'), (pg, 'r')], [(D_MODEL, BF16, 'r')], n=zg.shape[0], name=name)[0]


def _merge_bwd(dm, zg, ps, pg, name):
    def body(dm_ref, z_ref, ps_ref, pg_ref, dps_ref, dpg_ref, dz_ref):
        dmv, zv = dm_ref[...], z_ref[...]
        s1, s2 = _sigmoid(zv[:, :D_MODEL]), _sigmoid(zv[:, D_MODEL:])
        dps_ref[...] = (dmv * s1).astype(BF16)
        dpg_ref[...] = (dmv * s2).astype(BF16)
        dz_ref[:, :D_MODEL] = dmv * ps_ref[...] * s1 * (1.0 - s1)
        dz_ref[:, D_MODEL:] = dmv * pg_ref[...] * s2 * (1.0 - s2)
    return _rows(body, [(dm, 'r'), (zg, 'r'), (ps, 'r'), (pg, 'r')],
                 [(D_MODEL, BF16, 'r'), (D_MODEL, BF16, 'r'), (2 * D_MODEL, F32, 'r')], n=zg.shape[0], name=name)


def _final_loss(h, g, tgt, name):
    def body(h_ref, g_ref, t_ref, loss_ref, dh_ref, dg_ref):
        hv = h_ref[...]
        rstd = lax.rsqrt(jnp.mean(hv * hv, axis=-1, keepdims=True) + EPS)
        xh = hv * rstd
        err = xh * g_ref[...] - t_ref[...]
        part = 0.5 * jnp.sum(jnp.mean(err * err, axis=-1, keepdims=True), axis=0, keepdims=True)
        loss_ref[...] += jnp.broadcast_to(part, loss_ref.shape)
        dout = err * (1.0 / hv.shape[1])
        dg_ref[...] += jnp.sum(dout * xh, axis=0, keepdims=True)
        dxh = dout * g_ref[...]
        dh_ref[...] = rstd * (dxh - xh * jnp.mean(dxh * xh, axis=-1, keepdims=True))
    d = h.shape[1]
    return _rows(body, [(h, 'r'), (g, 'f'), (tgt, 'r')], [(LANE, F32, 'a'), (d, F32, 'r'), (d, F32, 'a')],
                 n=h.shape[0], name=name)


def _adamw_math(wv, gv, mv, vv):
    nm = ADAM_B1 * mv + (1.0 - ADAM_B1) * gv
    nv = ADAM_B2 * vv + (1.0 - ADAM_B2) * (gv * gv)
    m_hat = nm / (1.0 - ADAM_B1 ** ADAM_STEP)
    v_hat = nv / (1.0 - ADAM_B2 ** ADAM_STEP)
    return -ADAM_LR * (m_hat / (jnp.sqrt(v_hat) + ADAM_EPS) + ADAM_WD * wv), nm, nv


def _adamw(w, g, m, v, name):
    def body(w_ref, g_ref, m_ref, v_ref, d_ref, nm_ref, nv_ref):
        d_ref[...], nm_ref[...], nv_ref[...] = _adamw_math(w_ref[...], g_ref[...], m_ref[...], v_ref[...])
    c = w.shape[1]
    return _rows(body, [(w, 'r'), (g, 'r'), (m, 'r'), (v, 'r')], [(c, F32, 'r')] * 3, n=w.shape[0], name=name)


def _adamw_halves(w, g_own, g_sib, m, v, c_arr, name):
    r, cols = w.shape
    h = r // 2
    tr = _pick(h, 256, 8)
    per = h // tr

    def body(c_ref, w_ref, go_ref, gs_ref, m_ref, v_ref, g_ref, d_ref, nm_ref, nv_ref):
        mine = (pl.program_id(0) // per) == c_ref[0]
        gv = jnp.where(mine, go_ref[...], gs_ref[...])
        g_ref[...] = gv
        d_ref[...], nm_ref[...], nv_ref[...] = _adamw_math(w_ref[...], gv, m_ref[...], v_ref[...])

    full = pl.BlockSpec((tr, cols), lambda i, c_ref: (i, 0))
    half = pl.BlockSpec((tr, cols), lambda i, c_ref: (i % per, 0))
    grid_spec = pltpu.PrefetchScalarGridSpec(num_scalar_prefetch=1, grid=(2 * per,),
                                             in_specs=[full, half, half, full, full], out_specs=[full] * 4)
    return _pcall(body, name=name, grid_spec=grid_spec, out_shape=[jax.ShapeDtypeStruct((r, cols), F32)] * 4,
                  compiler_params=_params())(c_arr, w, g_own, g_sib, m, v)


def _shift_rows(v, sh, down):
    rolled = pltpu.roll(v, sh if down else v.shape[0] - sh, axis=0)
    row = lax.broadcasted_iota(jnp.int32, v.shape, 0)
    keep = (row >= sh) if down else (row < v.shape[0] - sh)
    return jnp.where(keep, rolled, 0.0)


def _chain_segments(st_r, st_i, pw_r_ref, pw_i_ref, conj, down):
    vr, vi = st_r[...], st_i[...]
    sh, k = 1, 0
    while sh < SEG:
        pr, pi = pw_r_ref[k:k + 1, :], pw_i_ref[k:k + 1, :]
        if conj:
            pi = -pi
        sr, si = _shift_rows(vr, sh, down), _shift_rows(vi, sh, down)
        vr, vi = vr + pr * sr - pi * si, vi + pr * si + pi * sr
        sh, k = sh * 2, k + 1
    st_r[...] = _shift_rows(vr, 1, down)
    st_i[...] = _shift_rows(vi, 1, down)


def _s5_scan(bu, ar8, ai8, pw_r, pw_i, name):
    n = bu.shape[0]
    rb = SCAN_ROWS
    nb, steps, lc = n // rb, rb // SEG, 512

    def body(bu_ref, ar_ref, ai_ref, pwr_ref, pwi_ref, x_ref, st_r, st_i):
        ph, b = pl.program_id(0), pl.program_id(1)

        @pl.when((ph == 0) & (b == 0))
        def _():
            st_r[...] = jnp.zeros_like(st_r)
            st_i[...] = jnp.zeros_like(st_i)

        def scan(store):
            for c in range(S5_GP // lc):
                re, im = slice(c * lc, (c + 1) * lc), slice(S5_GP + c * lc, S5_GP + (c + 1) * lc)
                a_r, a_i = ar_ref[:, re], ai_ref[:, re]

                def step(s, carry):
                    xr, xi = carry
                    rows = pl.ds(pl.multiple_of(s * SEG, SEG), SEG)
                    nr = a_r * xr - a_i * xi + bu_ref[rows, re]
                    ni = a_r * xi + a_i * xr + bu_ref[rows, im]
                    if store:
                        x_ref[rows, re] = nr
                        x_ref[rows, im] = ni
                    return nr, ni

                xr, xi = lax.fori_loop(0, steps, step, (st_r[:, re], st_i[:, re]), unroll=4)
                st_r[:, re] = xr
                st_i[:, re] = xi

        @pl.when(ph == 0)
        def _():
            scan(False)

        @pl.when((ph == 0) & (b == nb - 1))
        def _():
            _chain_segments(st_r, st_i, pwr_ref, pwi_ref, conj=False, down=True)

        @pl.when(ph == 1)
        def _():
            scan(True)

    full = lambda a: pl.BlockSpec(a.shape, lambda ph, b: (0, 0))
    return _pcall(body, name=name, grid=(2, nb),
                  in_specs=[pl.BlockSpec((rb, 2 * S5_GP), lambda ph, b: (b, 0)), full(ar8), full(ai8), full(pw_r), full(pw_i)],
                  out_specs=pl.BlockSpec((rb, 2 * S5_GP), lambda ph, b: (b * ph, 0)),
                  out_shape=jax.ShapeDtypeStruct((n, 2 * S5_GP), F32),
                  scratch_shapes=[pltpu.VMEM((SEG, S5_GP), F32), pltpu.VMEM((SEG, S5_GP), F32)],
                  compiler_params=_params())(bu, ar8, ai8, pw_r, pw_i)


def _s5_scan_bwd(gx, xs, ar8, ai8, pw_r, pw_i, name):
    n = gx.shape[0]
    rb = SCAN_ROWS
    nb, steps, lc = n // rb, rb // SEG, 256

    def body(gx_ref, x_ref, ar_ref, ai_ref, pwr_ref, pwi_ref, lam_ref, da_ref, st_r, st_i):
        ph, b = pl.program_id(0), pl.program_id(1)

        @pl.when((ph == 0) & (b == 0))
        def _():
            st_r[...] = jnp.zeros_like(st_r)
            st_i[...] = jnp.zeros_like(st_i)
            da_ref[...] = jnp.zeros_like(da_ref)

        def scan(store):
            for c in range(S5_GP // lc):
                re, im = slice(c * lc, (c + 1) * lc), slice(S5_GP + c * lc, S5_GP + (c + 1) * lc)
                a_r, a_i = ar_ref[:, re], ai_ref[:, re]

                def step(s, carry):
                    rows = pl.ds(pl.multiple_of((steps - 1 - s) * SEG, SEG), SEG)
                    if store:
                        lr, li, dr, di = carry
                        xr, xi = x_ref[rows, re], x_ref[rows, im]
                        dr = dr + lr * xr + li * xi
                        di = di + li * xr - lr * xi
                    else:
                        lr, li = carry
                    nr = a_r * lr + a_i * li + gx_ref[rows, re]
                    ni = a_r * li - a_i * lr + gx_ref[rows, im]
                    if store:
                        lam_ref[rows, re] = nr
                        lam_ref[rows, im] = ni
                        return nr, ni, dr, di
                    return nr, ni

                if store:
                    lr, li, dr, di = lax.fori_loop(0, steps, step, (st_r[:, re], st_i[:, re], da_ref[:, re], da_ref[:, im]),
                                                   unroll=4)
                    da_ref[:, re] = dr
                    da_ref[:, im] = di
                else:
                    lr, li = lax.fori_loop(0, steps, step, (st_r[:, re], st_i[:, re]), unroll=4)
                st_r[:, re] = lr
                st_i[:, re] = li

        @pl.when(ph == 0)
        def _():
            scan(False)

        @pl.when((ph == 0) & (b == nb - 1))
        def _():
            _chain_segments(st_r, st_i, pwr_ref, pwi_ref, conj=True, down=False)

        @pl.when(ph == 1)
        def _():
            scan(True)

    full = lambda a: pl.BlockSpec(a.shape, lambda ph, b: (0, 0))
    rev = lambda ph, b: (nb - 1 - b, 0)
    return _pcall(body, name=name, grid=(2, nb),
                  in_specs=[pl.BlockSpec((rb, 2 * S5_GP), rev), pl.BlockSpec((rb, 2 * S5_GP), lambda ph, b: ((nb - 1 - b) * ph, 0)),
                            full(ar8), full(ai8), full(pw_r), full(pw_i)],
                  out_specs=[pl.BlockSpec((rb, 2 * S5_GP), lambda ph, b: (nb - 1 - b * ph, 0)),
                             pl.BlockSpec((SEG, 2 * S5_GP), lambda ph, b: (0, 0))],
                  out_shape=[jax.ShapeDtypeStruct((n, 2 * S5_GP), F32), jax.ShapeDtypeStruct((SEG, 2 * S5_GP), F32)],
                  scratch_shapes=[pltpu.VMEM((SEG, S5_GP), F32), pltpu.VMEM((SEG, S5_GP), F32)],
                  compiler_params=_params())(gx, xs, ar8, ai8, pw_r, pw_i)


def _s5_discretize(lam_re, lam_im, log_dt, b_re, b_im):
    dt = jnp.exp(log_dt)[:, None]
    mag = jnp.exp(lam_re * dt)
    ar = mag * jnp.cos(lam_im * dt)
    ai = mag * jnp.sin(lam_im * dt)
    den = lam_re * lam_re + lam_im * lam_im
    nr = ar - 1.0
    fr = (nr * lam_re + ai * lam_im) / den
    fi = (ai * lam_re - nr * lam_im) / den
    bbar_re = fr[:, :, None] * b_re - fi[:, :, None] * b_im
    bbar_im = fr[:, :, None] * b_im + fi[:, :, None] * b_re
    return ar, ai, bbar_re, bbar_im


def _block_diag(t):
    g, a, b = t.shape
    eye = jnp.eye(g, dtype=t.dtype)
    return (t[:, :, None, :] * eye[:, None, :, None]).reshape(g * a, g * b)


def _diag_blocks(m, a, b):
    g = S5_G
    return jnp.einsum('gagb->gab', m.reshape(g, a, g, b))


def _permute_rows(t):
    n = t.shape[0]
    return t.reshape(SEG, n // SEG, t.shape[1]).transpose(1, 0, 2).reshape(n, t.shape[1])


def _unpermute_rows(t):
    n = t.shape[0]
    return t.reshape(n // SEG, SEG, t.shape[1]).transpose(1, 0, 2).reshape(n, t.shape[1])


def _segment_powers(ar, ai, seg_steps):
    pr, pi = ar.reshape(1, S5_GP), ai.reshape(1, S5_GP)
    e = 1
    while e < seg_steps:
        pr, pi = pr * pr - pi * pi, 2.0 * pr * pi
        e *= 2
    assert e == seg_steps, "segment length must be a power of two"
    rows_r, rows_i = [], []
    for _ in range(3):
        rows_r.append(pr)
        rows_i.append(pi)
        pr, pi = pr * pr - pi * pi, 2.0 * pr * pi
    pad = jnp.zeros((SEG - 3, S5_GP), F32)
    return jnp.concatenate(rows_r + [pad], axis=0), jnp.concatenate(rows_i + [pad], axis=0)


NT = (((1,), (1,)), ((), ()))
TN = (((0,), (0,)), ((), ()))


def _dot(a, b, dims=None, exact=False):
    dims = (((1,), (0,)), ((), ())) if dims is None else dims
    if exact:
        return lax.dot_general(a, b, dims, precision=HI, preferred_element_type=F32)
    return lax.dot_general(a.astype(BF16), b.astype(BF16), dims, preferred_element_type=F32)


def _dot01(a, b, dims=None, ones_first=True):
    x = b if ones_first else a
    hi = x.astype(BF16)
    r1 = x - hi.astype(F32)
    mid = r1.astype(BF16)
    lo = (r1 - mid.astype(F32)).astype(BF16)
    parts = [(_dot(a, p, dims) if ones_first else _dot(p, b, dims)) for p in (lo, mid, hi)]
    return (parts[0] + parts[1]) + parts[2]


HEADS = range(4)


def _gla_chunk_fwd(qc, kc, vc, al, wup, bup, s_prev, tril):
    ones = jnp.ones((GLA_CHUNK, GLA_DV), F32)
    z = [_dot(al, wup[h]) + bup[h] for h in HEADS]
    la = [(jnp.minimum(z[h], 0.0) - jnp.log(1.0 + jnp.exp(-jnp.abs(z[h])))) * (1.0 / GLA_TAU) for h in HEADS]
    bc = [_dot01(tril, la[h]) for h in HEADS]
    blb = [_dot01(la[h], ones, TN, ones_first=False) for h in HEADS]
    bl = [bc[h][GLA_CHUNK - 1:GLA_CHUNK, :] for h in HEADS]
    ebc = [jnp.exp(bc[h]) for h in HEADS]
    qt = [qc[h] * (GLA_DK ** -0.5) * ebc[h] for h in HEADS]
    kt = [kc[h] * jnp.exp(-bc[h]) for h in HEADS]
    ke = [kc[h] * jnp.exp(bl[h] - bc[h]) for h in HEADS]
    sc = [_dot(qt[h], kt[h], NT) * tril for h in HEADS]
    oi = [_dot(sc[h], vc[h]) for h in HEADS]
    oo = [_dot(qt[h], s_prev[h]) for h in HEADS]
    o = [oi[h] + oo[h] for h in HEADS]
    return z, bc, bl, blb, ebc, qt, kt, ke, sc, o


GLA_ROWS = 512
GLA_CPB = GLA_ROWS // GLA_CHUNK


def _gla_in_specs(arrs, blk):
    specs = []
    for a in arrs:
        if a.ndim == 3 and a.shape[1] > LANE:
            specs.append(pl.BlockSpec((GLA_HEADS, GLA_ROWS, a.shape[2]), lambda j: (0, blk(j), 0)))
        elif a.ndim == 3:
            specs.append(pl.BlockSpec(a.shape, lambda j: (0, 0, 0)))
        else:
            specs.append(pl.BlockSpec((GLA_ROWS, a.shape[1]), lambda j: (blk(j), 0)))
    return specs


def _tri(lower):
    ri = lax.broadcasted_iota(jnp.int32, (GLA_CHUNK, GLA_CHUNK), 0)
    ci = lax.broadcasted_iota(jnp.int32, (GLA_CHUNK, GLA_CHUNK), 1)
    return ((ri >= ci) if lower else (ri <= ci)).astype(F32)


def _gla_fwd(q, k, v, r, al, wup, bup, gn, name):
    n = q.shape[1]
    nc = n // GLA_CHUNK

    def body(q_ref, k_ref, v_ref, r_ref, al_ref, wup_ref, bup_ref, gn_ref, y_ref, sp_ref, s_ref):
        @pl.when(pl.program_id(0) == 0)
        def _():
            s_ref[...] = jnp.zeros_like(s_ref)

        tril = _tri(True)

        def chunk(c, carry):
            rows = pl.ds(pl.multiple_of(c * GLA_CHUNK, GLA_CHUNK), GLA_CHUNK)
            alc = al_ref[rows, :]
            vc = [v_ref[h, rows, :] for h in HEADS]
            s_prev = [s_ref[h] for h in HEADS]
            _, _, _, blb, _, _, _, ke, _, o = _gla_chunk_fwd(
                [q_ref[h, rows, :] for h in HEADS], [k_ref[h, rows, :] for h in HEADS], vc, alc,
                [wup_ref[h] for h in HEADS], [bup_ref[h] for h in HEADS], s_prev, tril)
            ds = [_dot(ke[h], vc[h], TN) for h in HEADS]
            for h in HEADS:
                rc = r_ref[h, rows, :]
                sp_ref[h, c] = s_prev[h]
                rstd = lax.rsqrt(jnp.mean(o[h] * o[h], axis=-1, keepdims=True) + EPS)
                y_ref[h, rows, :] = o[h] * rstd * gn_ref[h] * (rc * _sigmoid(rc))
                s_ref[h] = jnp.exp(blb[h]) * s_prev[h] + ds[h]
            return carry

        lax.fori_loop(0, GLA_CPB, chunk, 0)

    ins = [q, k, v, r, al, wup, bup, gn]
    return _pcall(body, name=name, grid=(n // GLA_ROWS,), in_specs=_gla_in_specs(ins, lambda j: j),
                  out_specs=[pl.BlockSpec((GLA_HEADS, GLA_ROWS, GLA_DV), lambda j: (0, j, 0)),
                             pl.BlockSpec((GLA_HEADS, GLA_CPB, GLA_DK, GLA_DV), lambda j: (0, j, 0, 0))],
                  out_shape=[jax.ShapeDtypeStruct((GLA_HEADS, n, GLA_DV), F32),
                             jax.ShapeDtypeStruct((GLA_HEADS, nc, GLA_DK, GLA_DV), F32)],
                  scratch_shapes=[pltpu.VMEM((GLA_HEADS, GLA_DK, GLA_DV), F32)],
                  compiler_params=_params())(*ins)


def _gla_bwd(q, k, v, r, al, wup, bup, gn, sp, dy, name):
    n = q.shape[1]
    nc = n // GLA_CHUNK

    nb = n // GLA_ROWS

    def body(q_ref, k_ref, v_ref, r_ref, al_ref, wup_ref, bup_ref, gn_ref, dy_ref, sp_ref,
             dq_ref, dk_ref, dv_ref, dr_ref, dz_ref, dgn_ref, dbup_ref, ds_ref):
        @pl.when(pl.program_id(0) == 0)
        def _():
            ds_ref[...] = jnp.zeros_like(ds_ref)
            dgn_ref[...] = jnp.zeros_like(dgn_ref)
            dbup_ref[...] = jnp.zeros_like(dbup_ref)

        tril, triu = _tri(True), _tri(False)

        def chunk(i, carry):
            c = GLA_CPB - 1 - i
            rows = pl.ds(pl.multiple_of(c * GLA_CHUNK, GLA_CHUNK), GLA_CHUNK)
            alc = al_ref[rows, :]
            qc = [q_ref[h, rows, :] for h in HEADS]
            kc = [k_ref[h, rows, :] for h in HEADS]
            vc = [v_ref[h, rows, :] for h in HEADS]
            s_prev = [sp_ref[h, c] for h in HEADS]
            ds = [ds_ref[h] for h in HEADS]
            z, bc, bl, blb, ebc, qt, kt, ke, sc, o = _gla_chunk_fwd(
                qc, kc, vc, alc, [wup_ref[h] for h in HEADS], [bup_ref[h] for h in HEADS], s_prev, tril)
            do, rstd = [], []
            for h in HEADS:
                rc = r_ref[h, rows, :]
                rs = lax.rsqrt(jnp.mean(o[h] * o[h], axis=-1, keepdims=True) + EPS)
                on = o[h] * rs
                sr = _sigmoid(rc)
                sil = rc * sr
                dyv, gnv = dy_ref[h, rows, :], gn_ref[h]
                dgn_ref[h] += jnp.sum(dyv * on * sil, axis=0, keepdims=True)
                dr_ref[h, rows, :] = dyv * on * gnv * (sr * (1.0 + rc * (1.0 - sr)))
                don = dyv * gnv * sil
                do.append(rs * (don - on * jnp.mean(don * on, axis=-1, keepdims=True)))
            dp = [_dot(do[h], vc[h], NT) * tril for h in HEADS]
            dv1 = [_dot(sc[h], do[h], TN) for h in HEADS]
            dv2 = [_dot(ke[h], ds[h]) for h in HEADS]
            dq2 = [_dot(do[h], s_prev[h], NT) for h in HEADS]
            dke = [_dot(vc[h], ds[h], NT) for h in HEADS]
            ddec = [_dot01(jnp.ones((8, GLA_DV), F32), ds[h] * s_prev[h], NT)[0:1, :] for h in HEADS]
            dsn = [_dot(qt[h], do[h], TN) for h in HEADS]
            dq1 = [_dot(dp[h], kt[h]) for h in HEADS]
            dkt = [_dot(dp[h], qt[h], TN) for h in HEADS]
            dbc, dbl = [], []
            for h in HEADS:
                dqt = dq1[h] + dq2[h]
                dv_ref[h, rows, :] = dv1[h] + dv2[h]
                ds_ref[h] = jnp.exp(blb[h]) * ds[h] + dsn[h]
                dq_ref[h, rows, :] = dqt * (GLA_DK ** -0.5) * ebc[h]
                dk_ref[h, rows, :] = dkt[h] * jnp.exp(-bc[h]) + dke[h] * jnp.exp(bl[h] - bc[h])
                dbc.append(dqt * qt[h] - dkt[h] * kt[h] - dke[h] * ke[h])
                dbl.append(jnp.sum(dke[h] * ke[h], axis=0, keepdims=True) + ddec[h] * jnp.exp(bl[h]))
            dla = [_dot01(triu, dbc[h]) + dbl[h] for h in HEADS]
            for h in HEADS:
                dz = dla[h] * (1.0 - _sigmoid(z[h])) * (1.0 / GLA_TAU)
                dz_ref[h, rows, :] = dz
                dbup_ref[h] += jnp.sum(dz, axis=0, keepdims=True)
            return carry

        lax.fori_loop(0, GLA_CPB, chunk, 0)

    rev = lambda j: nb - 1 - j
    ins = [q, k, v, r, al, wup, bup, gn, dy, sp]
    in_specs = _gla_in_specs(ins[:9], rev) + [pl.BlockSpec((GLA_HEADS, GLA_CPB, GLA_DK, GLA_DV), lambda j: (0, rev(j), 0, 0))]
    hs = lambda w: pl.BlockSpec((GLA_HEADS, GLA_ROWS, w), lambda j: (0, rev(j), 0))
    h1 = lambda w: pl.BlockSpec((GLA_HEADS, 1, w), lambda j: (0, 0, 0))
    sh = lambda w: jax.ShapeDtypeStruct((GLA_HEADS, n, w), F32)
    s1 = lambda w: jax.ShapeDtypeStruct((GLA_HEADS, 1, w), F32)
    return _pcall(body, name=name, grid=(nb,), in_specs=in_specs,
                  out_specs=[hs(GLA_DK), hs(GLA_DK), hs(GLA_DV), hs(GLA_DV), hs(GLA_DK), h1(GLA_DV), h1(GLA_DK)],
                  out_shape=[sh(GLA_DK), sh(GLA_DK), sh(GLA_DV), sh(GLA_DV), sh(GLA_DK), s1(GLA_DV), s1(GLA_DK)],
                  scratch_shapes=[pltpu.VMEM((GLA_HEADS, GLA_DK, GLA_DV), F32)],
                  compiler_params=_params())(*ins)


def _heads(t, w):
    return t.reshape(t.shape[0], GLA_HEADS, w).transpose(1, 0, 2)


def _unheads(t):
    return t.transpose(1, 0, 2).reshape(t.shape[1], GLA_HEADS * t.shape[2])


ANY = pl.BlockSpec(memory_space=pl.ANY)


def _place():
    x, y, c = lax.axis_index("x"), lax.axis_index("y"), lax.axis_index("c")
    chips = [(1 - x, y), (x, 1 - y), (1 - x, 1 - y)]
    return x, y, c, chips


def _remote(src, dst, ssem, rsem, dev):
    return pltpu.make_async_remote_copy(src_ref=src, dst_ref=dst, send_sem=ssem, recv_sem=rsem, device_id=dev,
                                        device_id_type=MESH_ID)


def _half(c, rows):
    h = rows // 2
    return pl.ds(pl.multiple_of(c * h, 8), h)


def _side_gather_ici(shards):
    def copies(ins, outs, ssem, rsem):
        x, y, c, chips = _place()
        mine = 2 * x + y
        cps = []
        for w in range(len(ins)):
            half = _half(c, ins[w].shape[0])
            cps.append(_remote(ins[w], outs[w].at[mine], ssem.at[4 * w], rsem.at[4 * w], (x, y, 1 - c)))
            for k, (px, py) in enumerate(chips):
                cps.append(_remote(ins[w].at[half], outs[w].at[mine, half], ssem.at[4 * w + 1 + k], rsem.at[4 * w + 1 + k],
                                   (px, py, c)))
        return cps

    return _Side(shards, [jax.ShapeDtypeStruct((4,) + s.shape, s.dtype) for s in shards], 4 * len(shards), copies)


def _side_gather_d2d(gathered):
    def copies(ins, outs, ssem, rsem):
        x, y, c, chips = _place()
        cps = []
        for w in range(len(outs)):
            half = _half(c, outs[w].shape[1])
            for k, (px, py) in enumerate(chips):
                theirs = outs[w].at[2 * px + py, half]
                cps.append(_remote(theirs, theirs, ssem.at[3 * w + k], rsem.at[3 * w + k], (x, y, 1 - c)))
        return cps

    return _Side(gathered, [jax.ShapeDtypeStruct(g.shape, g.dtype) for g in gathered], 3 * len(gathered), copies,
                 aliased=True)


def _side_swap_halves(grads):
    def copies(ins, outs, ssem, rsem):
        x, y, c, _ = _place()
        return [_remote(ins[w].at[:, _half(1 - c, ins[w].shape[1]), :], outs[w], ssem.at[w], rsem.at[w], (x, y, 1 - c))
                for w in range(len(ins))]

    return _Side(grads, [jax.ShapeDtypeStruct((4, g.shape[1] // 2, g.shape[2]), g.dtype) for g in grads], len(grads), copies)


def _side_scatter(sums):
    def copies(ins, outs, ssem, rsem):
        x, y, c, chips = _place()
        return [_remote(ins[w].at[2 * px + py], outs[w].at[k], ssem.at[3 * w + k], rsem.at[3 * w + k], (px, py, c))
                for w in range(len(ins)) for k, (px, py) in enumerate(chips)]

    return _Side(sums, [jax.ShapeDtypeStruct((3,) + s.shape[1:], s.dtype) for s in sums], 3 * len(sums), copies)


def _side_swap_reduced(halves):
    def copies(ins, outs, ssem, rsem):
        x, y, c, _ = _place()
        return [_remote(ins[w], outs[w], ssem.at[w], rsem.at[w], (x, y, 1 - c)) for w in range(len(ins))]

    return _Side(halves, [jax.ShapeDtypeStruct(h.shape, h.dtype) for h in halves], len(halves), copies)


def _chip_sum(g, recv, c_arr, name):
    _, r, cols = g.shape
    h = r // 2
    tr = _pick(h, 256, 16)
    g4 = g.reshape(4, 2, h, cols)

    def body(c_ref, g_ref, r_ref, o_ref):
        o_ref[...] = (g_ref[...] + r_ref[...]).astype(BF16)

    grid_spec = pltpu.PrefetchScalarGridSpec(
        num_scalar_prefetch=1, grid=(4, h // tr),
        in_specs=[pl.BlockSpec((None, None, tr, cols), lambda s, i, c_ref: (s, c_ref[0], i, 0)),
                  pl.BlockSpec((None, tr, cols), lambda s, i, c_ref: (s, i, 0))],
        out_specs=pl.BlockSpec((None, tr, cols), lambda s, i, c_ref: (s, i, 0)))
    return _pcall(body, name=name, grid_spec=grid_spec, out_shape=jax.ShapeDtypeStruct((4, h, cols), BF16),
                  compiler_params=_params())(c_arr, g4, recv)


def _owner_sum(sums, others, s_arr, name):
    _, h, cols = sums.shape
    tr = _pick(h, 256, 16)

    def body(s_ref, a_ref, o_ref, out_ref):
        f = lambda v: v.astype(F32)
        out_ref[...] = (f(a_ref[...]) + f(o_ref[0])) + (f(o_ref[1]) + f(o_ref[2]))

    grid_spec = pltpu.PrefetchScalarGridSpec(
        num_scalar_prefetch=1, grid=(h // tr,),
        in_specs=[pl.BlockSpec((None, tr, cols), lambda i, s_ref: (s_ref[0], i, 0)),
                  pl.BlockSpec((3, tr, cols), lambda i, s_ref: (0, i, 0))],
        out_specs=pl.BlockSpec((tr, cols), lambda i, s_ref: (i, 0)))
    return _pcall(body, name=name, grid_spec=grid_spec, out_shape=jax.ShapeDtypeStruct((h, cols), F32),
                  compiler_params=_params())(s_arr, sums, others)


def _allreduce_small(v, name):
    def body(v_ref, o_ref, r0, r1, ssem, rsem):
        x, y, c, chips = _place()
        cp = _remote(v_ref, r0, ssem.at[0], rsem.at[0], (x, y, 1 - c))
        cp.start()
        cp.wait()
        o_ref[...] = v_ref[...] + r0[...]
        cps = []
        for k, (px, py) in enumerate(chips):
            cp = _remote(o_ref, r1.at[k], ssem.at[1 + k], rsem.at[1 + k], (px, py, c))
            cp.start()
            cps.append(cp)
        for cp in cps:
            cp.wait()
        o_ref[...] = (o_ref[...] + r1[0]) + (r1[1] + r1[2])

    vm = pl.BlockSpec(memory_space=pltpu.VMEM)
    return _pcall(body, name=name, in_specs=[vm], out_specs=vm, out_shape=jax.ShapeDtypeStruct(v.shape, F32),
                  scratch_shapes=[pltpu.VMEM(v.shape, F32), pltpu.VMEM((3,) + v.shape, F32),
                                  pltpu.SemaphoreType.DMA((4,)), pltpu.SemaphoreType.DMA((4,))],
                  compiler_params=_params(has_side_effects=True))(v)


def _pack_small(parts):
    flat = jnp.concatenate([p.reshape(-1).astype(F32) for p in parts])
    pad = (-flat.shape[0]) % (64 * LANE)
    return jnp.pad(flat, (0, pad)).reshape(-1, LANE)


def _unpack_small(packed, like):
    flat, out, pos = packed.reshape(-1), [], 0
    for p in like:
        out.append(flat[pos:pos + p.size].reshape(p.shape))
        pos += p.size
    return out


FFN_FWD_ROWS, FFN_BWD_ROWS = 1024, 512
FFN_SUB_ROWS = 256


def _ffn_specs(n, d, fs, cap):
    rows = _pick(n, cap, 16)
    row = pl.BlockSpec((rows, d), lambda i, s: (i, 0))
    gain = pl.BlockSpec((1, d), lambda i, s: (0, 0))
    w_col = pl.BlockSpec((None, d, fs), lambda i, s: (s, 0, 0))
    w_row = pl.BlockSpec((None, fs, d), lambda i, s: (s, 0, 0))
    hid = pl.BlockSpec((None, rows, fs), lambda i, s: (s, i, 0))
    return rows, row, gain, w_col, w_row, hid


def _ffn_fwd(h, g, w1, w3, w2, tag, plan):
    n, d = h.shape
    ns, _, fs = w1.shape
    rows, row, gain, w_col, w_row, hid = _ffn_specs(n, d, fs, FFN_FWD_ROWS)
    sub = _pick(rows, FFN_SUB_ROWS, 16)

    def body(h_ref, g_ref, w1_ref, w3_ref, w2_ref, out_ref, n1_ref, a_ref, b_ref, hm_ref, acc_ref):
        s = pl.program_id(1)

        @pl.when(s == 0)
        def _():
            xv = h_ref[...]
            rstd = lax.rsqrt(jnp.mean(xv * xv, axis=-1, keepdims=True) + EPS)
            n1_ref[...] = (xv * rstd * g_ref[...]).astype(BF16)
            acc_ref[...] = jnp.zeros_like(acc_ref)

        def up(j):
            n1 = n1_ref[j * sub:(j + 1) * sub, :]
            return _dot(n1, w1_ref[...]), _dot(n1, w3_ref[...])

        cur = up(0)
        for j in range(rows // sub):
            nxt = up(j + 1) if (j + 1) * sub < rows else None
            a, b = cur
            r = slice(j * sub, (j + 1) * sub)
            hm = (a * _sigmoid(a) * b).astype(BF16)
            a_ref[r, :] = a.astype(BF16)
            b_ref[r, :] = b.astype(BF16)
            hm_ref[r, :] = hm
            acc_ref[r, :] += _dot(hm, w2_ref[...])
            cur = nxt

        @pl.when(s == ns - 1)
        def _():
            out_ref[...] = h_ref[...] + 0.5 * acc_ref[...]

    hid_shape = jax.ShapeDtypeStruct((ns, n, fs), BF16)
    plan.before(f"{tag}_fwd")
    out, n1, a, b, hm = _pcall(
        body, name=f"{tag}_fwd", grid=(n // rows, ns), in_specs=[row, gain, w_col, w_col, w_row],
        out_specs=[row, row, hid, hid, hid],
        out_shape=[jax.ShapeDtypeStruct((n, d), F32), jax.ShapeDtypeStruct((n, d), BF16), hid_shape, hid_shape, hid_shape],
        scratch_shapes=[pltpu.VMEM((rows, d), F32)], compiler_params=_params())(h, g, w1, w3, w2)
    plan.after(f"{tag}_fwd")
    return out, (h, n1, a, b, hm)


def _ffn_bwd(dout, saved, g, w1, w3, w2, tag, plan):
    h, n1, a, b, hm = saved
    n, d = h.shape
    ns, _, fs = w1.shape
    rows, row, gain, w_col, w_row, hid = _ffn_specs(n, d, fs, FFN_BWD_ROWS)
    sub = _pick(rows, FFN_SUB_ROWS, 16)

    def body(do_ref, h_ref, g_ref, a_ref, b_ref, w1_ref, w3_ref, w2_ref, dh_ref, da_ref, db_ref, dg_ref, acc_ref):
        i, s = pl.program_id(0), pl.program_id(1)

        @pl.when(s == 0)
        def _():
            acc_ref[...] = jnp.zeros_like(acc_ref)

        @pl.when((s == 0) & (i == 0))
        def _():
            dg_ref[...] = jnp.zeros_like(dg_ref)

        def up(j):
            return _dot(0.5 * do_ref[j * sub:(j + 1) * sub, :], w2_ref[...], NT)

        cur = up(0)
        for j in range(rows // sub):
            nxt = up(j + 1) if (j + 1) * sub < rows else None
            r = slice(j * sub, (j + 1) * sub)
            av, bv = a_ref[r, :].astype(F32), b_ref[r, :].astype(F32)
            sg = _sigmoid(av)
            da = (cur * bv * (sg * (1.0 + av * (1.0 - sg)))).astype(BF16)
            db = (cur * av * sg).astype(BF16)
            da_ref[r, :] = da
            db_ref[r, :] = db
            acc_ref[r, :] += _dot(da, w1_ref[...], NT) + _dot(db, w3_ref[...], NT)
            cur = nxt

        @pl.when(s == ns - 1)
        def _():
            xv, dn = h_ref[...], acc_ref[...]
            rstd = lax.rsqrt(jnp.mean(xv * xv, axis=-1, keepdims=True) + EPS)
            xh = xv * rstd
            dg_ref[...] += jnp.sum(dn * xh, axis=0, keepdims=True)
            dxh = dn * g_ref[...]
            dh_ref[...] = do_ref[...] + rstd * (dxh - xh * jnp.mean(dxh * xh, axis=-1, keepdims=True))

    hid_shape = jax.ShapeDtypeStruct((ns, n, fs), BF16)
    plan.before(f"{tag}_bwd")
    dh, da, db, dg = _pcall(
        body, name=f"{tag}_bwd", grid=(n // rows, ns), in_specs=[row, row, gain, hid, hid, w_col, w_col, w_row],
        out_specs=[row, hid, hid, gain],
        out_shape=[jax.ShapeDtypeStruct((n, d), F32), hid_shape, hid_shape, jax.ShapeDtypeStruct((1, d), F32)],
        scratch_shapes=[pltpu.VMEM((rows, d), F32)], compiler_params=_params())(dout, h, g, a, b, w1, w3, w2)
    plan.after(f"{tag}_bwd")
    plan.before(f"{tag}_gw2")
    gw2 = _mm(hm, dout, ta=True, shard='m', alpha=0.5, name=f"{tag}_gw2")
    plan.after(f"{tag}_gw2")
    gw1 = _mm(n1, da, ta=True, shard='n', name=f"{tag}_gw1")
    gw3 = _mm(n1, db, ta=True, shard='n', name=f"{tag}_gw3")
    return dh, dg, gw1, gw3, gw2


def _local_step(x, tgt, plan):
    n = x.shape[0]
    grads = plan.grads

    def f(name):
        w = plan.get(name)
        return w.reshape(1, D_MODEL) if name.endswith('_norm') and name != 'gla_out_norm' else w

    def carried(tag, fn, *args, **kw):
        plan.before(tag)
        out = fn(*args, **kw)
        plan.after(tag)
        return out

    h1, ffn1 = _ffn_fwd(x, f('ffn1_norm'), f('ffn1_w1'), f('ffn1_w3'), f('ffn1_w2'), "ffn1", plan)
    u = carried("mix_rms", _rms_fwd, h1, f('mix_norm'), "mix_rms")
    w_in = f('w_in')
    w_a = w_in[:, :2048]
    w_al = jnp.pad(w_in[:, 2048:2048 + GLA_RANK], ((0, 0), (0, LANE - GLA_RANK)))
    w_g = w_in[:, 2048 + GLA_RANK:]
    za = _mm(u, w_a, name="in_a")
    zg = _mm(u, w_g, name="in_g")
    al = _mm(u, w_al, name="in_al")
    ar, ai, bbar_re, bbar_im = _s5_discretize(f('s5_lambda_re'), f('s5_lambda_im'), f('s5_log_dt'), f('s5_b_re'), f('s5_b_im'))
    b_blk = jnp.concatenate([_block_diag(bbar_re.transpose(0, 2, 1)), _block_diag(bbar_im.transpose(0, 2, 1))], axis=1)
    c_blk = jnp.concatenate([_block_diag(f('s5_c_re').transpose(0, 2, 1)), -_block_diag(f('s5_c_im').transpose(0, 2, 1))], axis=0)
    b_blk, c_blk = b_blk.astype(BF16), c_blk.astype(BF16)
    ar8 = jnp.broadcast_to(ar.reshape(1, S5_GP), (SEG, S5_GP))
    ai8 = jnp.broadcast_to(ai.reshape(1, S5_GP), (SEG, S5_GP))
    pw_r, pw_i = _segment_powers(ar, ai, n // SEG)
    dskip = f('s5_d').reshape(1, S5_W)
    u_s5 = _permute_rows(za[:, :S5_W])
    bu = _mm(u_s5, b_blk, name="s5_bu")
    xs = _s5_scan(bu, ar8, ai8, pw_r, pw_i, "s5_scan")
    ys_p = _mm(xs, c_blk, res=_scale_rows(u_s5, dskip, "s5_skip"), name="s5_y")
    ys = _unpermute_rows(ys_p)
    zgelu = _gelu_fwd(ys, "s5_gelu")
    t_glu = _mm(zgelu, f('s5_glu_w'), bias=f('s5_glu_b').reshape(1, S5_W), name="s5_glu_t")
    y_s5 = _glu_fwd(zgelu, t_glu, "s5_glu")
    q, k = _heads(za[:, 512:768], GLA_DK), _heads(za[:, 768:1024], GLA_DK)
    v, r = _heads(za[:, 1024:1536], GLA_DV), _heads(za[:, 1536:2048], GLA_DV)
    wup = jnp.pad(f('gla_a_up_w'), ((0, LANE - GLA_RANK), (0, 0)))
    wup_h = wup.reshape(LANE, GLA_HEADS, GLA_DK).transpose(1, 0, 2)
    bup_h = f('gla_a_up_b').reshape(GLA_HEADS, 1, GLA_DK)
    gn_h = f('gla_out_norm').reshape(GLA_HEADS, 1, GLA_DV)
    y_gla_h, s_prev = carried("gla_fwd", _gla_fwd, q, k, v, r, al, wup_h, bup_h, gn_h, "gla_fwd")
    y_gla = _unheads(y_gla_h).astype(BF16)
    ps = _mm(y_s5, f('proj_s5'), name="proj_s5")
    pg = carried("proj_gla", _mm, y_gla, f('proj_gla'), name="proj_gla")
    merged = _merge_fwd(zg, ps, pg, "merge")
    h2 = _mm(merged, f('w_out'), res=h1, name="w_out")
    h3, ffn2 = _ffn_fwd(h2, f('ffn2_norm'), f('ffn2_w1'), f('ffn2_w3'), f('ffn2_w2'), "ffn2", plan)
    loss, dh3, g_final = _final_loss(h3, f('final_norm').reshape(1, D_MODEL), tgt, "loss")
    grads['final_norm'] = g_final.reshape(D_MODEL)
    dh2, grads['ffn2_norm'], grads['ffn2_w1'], grads['ffn2_w3'], grads['ffn2_w2'] = _ffn_bwd(
        dh3, ffn2, f('ffn2_norm'), f('ffn2_w1'), f('ffn2_w3'), f('ffn2_w2'), "ffn2", plan)
    dm = _mm(dh2, f('w_out'), tb=True, name="d_merged")
    grads['w_out'] = _mm(merged, dh2, ta=True, name="g_w_out")
    dps, dpg, dzg = carried("d_merge", _merge_bwd, dm, zg, ps, pg, "d_merge")
    grads['proj_s5'] = _mm(y_s5, dps, ta=True, name="g_proj_s5")
    grads['proj_gla'] = _mm(y_gla, dpg, ta=True, name="g_proj_gla")
    dy_s5 = _mm(dps, f('proj_s5'), tb=True, name="d_y_s5")
    dy_gla = _mm(dpg, f('proj_gla'), tb=True, name="d_y_gla")
    dzgelu, dt_glu, g_glu_b = _glu_bwd1(dy_s5, zgelu, t_glu, "d_glu")
    grads['s5_glu_b'] = g_glu_b.reshape(S5_W)
    grads['s5_glu_w'] = _mm(zgelu, dt_glu, ta=True, name="g_glu_w")
    dzgelu = _mm(dt_glu, f('s5_glu_w'), tb=True, res=dzgelu, name="d_gelu")
    dys, du_skip, g_d = _glu_bwd2(_permute_rows(dzgelu), ys_p, u_s5, dskip, "d_s5_y")
    grads['s5_d'] = g_d.reshape(S5_G, S5_H)
    gx = _mm(dys, c_blk, tb=True, name="s5_gx")
    lam, da8 = _s5_scan_bwd(gx, xs, ar8, ai8, pw_r, pw_i, "s5_scan_bwd")
    g_c = _mm(dys, xs, ta=True, name="g_s5_c")
    grads['s5_c_re'] = _diag_blocks(g_c[:, :S5_GP], S5_H, S5_P)
    grads['s5_c_im'] = -_diag_blocks(g_c[:, S5_GP:], S5_H, S5_P)
    g_b = _mm(lam, u_s5, ta=True, name="g_s5_b")
    g_bbar_re = _diag_blocks(g_b[:S5_GP], S5_P, S5_H)
    g_bbar_im = _diag_blocks(g_b[S5_GP:], S5_P, S5_H)
    da = jnp.sum(da8, axis=0)
    g_ar, g_ai = da[:S5_GP].reshape(S5_G, S5_P), da[S5_GP:].reshape(S5_G, S5_P)
    _, disc_vjp = jax.vjp(_s5_discretize, f('s5_lambda_re'), f('s5_lambda_im'), f('s5_log_dt'), f('s5_b_re'), f('s5_b_im'))
    (grads['s5_lambda_re'], grads['s5_lambda_im'], grads['s5_log_dt'], grads['s5_b_re'],
     grads['s5_b_im']) = disc_vjp((g_ar, g_ai, g_bbar_re, g_bbar_im))
    du_s5 = _unpermute_rows(_mm(lam, b_blk, tb=True, res=du_skip, name="d_s5_u"))
    dq, dk, dv, dr, dz, dgn, dbup = carried("gla_bwd", _gla_bwd, q, k, v, r, al, wup_h, bup_h, gn_h, s_prev,
                                            _heads(dy_gla, GLA_DV), "gla_bwd")
    grads['gla_out_norm'] = dgn.reshape(GLA_HEADS * GLA_DV)
    grads['gla_a_up_b'] = dbup.reshape(GLA_HEADS * GLA_DK)
    dz = _unheads(dz)
    grads['gla_a_up_w'] = _mm(al, dz, ta=True, name="g_a_up")[:GLA_RANK]
    dal = _mm(dz, wup, tb=True, name="d_a_low")
    dza = jnp.concatenate([du_s5, _unheads(dq), _unheads(dk), _unheads(dv), _unheads(dr)], axis=1)
    g_wa = _mm(u, dza, ta=True, name="g_in_a")
    g_wg = _mm(u, dzg, ta=True, name="g_in_g")
    g_wal = _mm(u, dal, ta=True, name="g_in_al")
    grads['w_in'] = jnp.concatenate([g_wa, g_wal[:, :GLA_RANK], g_wg], axis=1)
    du = carried("d_u_a", _mm, dza, w_a, tb=True, name="d_u_a")
    du = _mm(dzg, w_g, tb=True, res=du, name="d_u_g")
    du = _mm(dal, w_al, tb=True, res=du, name="d_u_al")
    dh1, g_mix = carried("d_mix_rms", _rms_bwd, h1, f('mix_norm'), du, dh2, "d_mix_rms")
    grads['mix_norm'] = g_mix
    dx, grads['ffn1_norm'], grads['ffn1_w1'], grads['ffn1_w3'], grads['ffn1_w2'] = _ffn_bwd(
        dh1, ffn1, f('ffn1_norm'), f('ffn1_w1'), f('ffn1_w3'), f('ffn1_w2'), "ffn1", plan)
    return loss[0, 0], dx


MIXER_WEIGHTS = ['w_in', 's5_glu_w', 'proj_s5', 'proj_gla', 'w_out', 'gla_a_up_w']
FFN1_WEIGHTS, FFN2_WEIGHTS = FFN_WEIGHTS[:3], FFN_WEIGHTS[3:]
GRAD_GROUPS = {'ffn2': FFN2_WEIGHTS, 'mixer': ['w_out', 'proj_s5', 'proj_gla', 's5_glu_w', 'w_in'], 'ffn1': FFN1_WEIGHTS}


class _Plan:
    def __init__(self, a, c_arr, s_arr):
        self.a, self.c_arr, self.s_arr = a, c_arr, s_arr
        self.grads, self.weights, self.riding = {}, {}, {}
        self.g4s, self.chip_sums, self.halves, self.sib_halves = {}, {}, {}, {}
        for nm in SMALL:
            if nm != 'gla_a_up_w':
                self.weights[nm] = a[nm] if nm == 'final_norm' else a[nm][0]
        ici = _side_gather_ici(self._shards(FFN1_WEIGHTS))
        _run_side(ici, "gather_ffn1_ici")
        self._gathered(FFN1_WEIGHTS, _run_side(_side_gather_d2d(ici.outs), "gather_ffn1_d2d"))

    def _shards(self, names):
        return [self.a[nm][0].astype(F32 if nm == 'gla_a_up_w' else BF16) for nm in names]

    def _gathered(self, names, arrs):
        for nm, g4 in zip(names, arrs):
            if nm in FFN_WEIGHTS:
                self.weights[nm] = g4
            elif nm in COL_SHARDED:
                self.weights[nm] = jnp.concatenate([g4[s] for s in range(4)], axis=1)
            else:
                self.weights[nm] = g4.reshape(4 * g4.shape[1], g4.shape[2])

    def get(self, name):
        return self.weights[name]

    def _shard_major(self, nm):
        g = self.grads[nm]
        if nm in FFN_WEIGHTS:
            return g
        if nm in COL_SHARDED:
            return jnp.stack(jnp.split(g, 4, axis=1))
        return g.reshape(4, g.shape[0] // 4, g.shape[1])

    def _schedule(self, tag):
        grp = GRAD_GROUPS
        if tag == "ffn1_fwd":
            return _side_gather_ici(self._shards(MIXER_WEIGHTS)), lambda outs: self.riding.update(mixer_ici=outs)
        if tag == "mix_rms":
            return _side_gather_d2d(self.riding['mixer_ici']), lambda outs: self._gathered(MIXER_WEIGHTS, outs)
        if tag == "gla_fwd":
            return _side_gather_ici(self._shards(FFN2_WEIGHTS)), lambda outs: self.riding.update(ffn2_ici=outs)
        if tag == "proj_gla":
            return _side_gather_d2d(self.riding['ffn2_ici']), lambda outs: self._gathered(FFN2_WEIGHTS, outs)
        steps = {"d_merge": ('ffn2', 0), "gla_bwd": ('ffn2', 1), "d_mix_rms": ('ffn2', 2),
                 "d_u_a": ('mixer', 0), "ffn1_bwd": ('mixer', 1), "ffn1_gw2": ('mixer', 2)}
        if tag in steps:
            group, stage = steps[tag]
            return self._reduce_stage(grp[group], stage)
        return None

    def _reduce_stage(self, names, stage):
        if stage == 0:
            for nm in names:
                self.g4s[nm] = self._shard_major(nm)

            def done(outs):
                for nm, r in zip(names, outs):
                    self.chip_sums[nm] = _chip_sum(self.g4s[nm], r, self.c_arr, f"chip_sum_{nm}")
            return _side_swap_halves([self.g4s[nm] for nm in names]), done
        if stage == 1:
            def done(outs):
                for nm, o in zip(names, outs):
                    self.halves[nm] = _owner_sum(self.chip_sums[nm], o, self.s_arr, f"owner_sum_{nm}")
            return _side_scatter([self.chip_sums[nm] for nm in names]), done

        def done(outs):
            self.sib_halves.update(zip(names, outs))
        return _side_swap_reduced([self.halves[nm] for nm in names]), done

    def before(self, tag):
        entry = self._schedule(tag)
        if entry is not None:
            side, done = entry
            self.riding[tag] = (side, done)
            _RIDER.append(side)

    def after(self, tag):
        if tag in self.riding:
            side, done = self.riding.pop(tag)
            assert not _RIDER and side.outs is not None, tag
            done(side.outs)

    def finish(self):
        names = GRAD_GROUPS['ffn1']
        for stage in range(3):
            side, done = self._reduce_stage(names, stage)
            done(_run_side(side, f"grad_ffn1_stage{stage}"))


def _train_step(a):
    x = a['x'][0]
    tgt = a['loss_target'][0]
    xi, yi, ci = lax.axis_index("x"), lax.axis_index("y"), lax.axis_index("c")
    c_arr = jnp.reshape(ci, (1,)).astype(jnp.int32)
    s_arr = jnp.reshape(2 * xi + yi, (1,)).astype(jnp.int32)
    plan = _Plan(a, c_arr, s_arr)
    loss, dx = _local_step(x, tgt, plan)
    plan.finish()
    grads = plan.grads
    loss = lax.psum(loss, ("x", "y", "c"))
    halves = [plan.halves[nm] for nm in SHARDED]
    sib_halves = [plan.sib_halves[nm] for nm in SHARDED]
    red = {}
    small_parts = [grads[nm].reshape(a[nm].shape) for nm in SMALL if nm != 'gla_a_up_w'] + [grads['gla_a_up_w']]
    small_sum = _unpack_small(_allreduce_small(_pack_small(small_parts), "allreduce_small"), small_parts)
    small_names = [nm for nm in SMALL if nm != 'gla_a_up_w']
    for nm, g in zip(small_names, small_sum[:-1]):
        red[nm] = g
    g_up = small_sum[-1]
    red['gla_a_up_w'] = lax.dynamic_slice(g_up, (0, (2 * xi + yi) * GLA_DK), (GLA_RANK, GLA_DK))
    out_g, out_d, out_m, out_v = {}, {}, {}, {}
    for nm, own, sib in zip(SHARDED, halves, sib_halves):
        g, d, nm_, nv_ = _adamw_halves(a[nm][0], own, sib, a['m_' + nm][0], a['v_' + nm][0], c_arr, f"adamw_{nm}")
        shape = a[nm].shape
        out_g[nm], out_d[nm], out_m[nm], out_v[nm] = (t.reshape(shape) for t in (g, d, nm_, nv_))
    rest = [nm for nm in WEIGHTS if nm not in SHARDED]
    pk = lambda pre: _pack_small([a[pre + nm] for nm in rest])
    d, nm_, nv_ = _adamw(pk(''), _pack_small([red[nm] for nm in rest]), pk('m_'), pk('v_'), "adamw_small")
    like = [a[nm] for nm in rest]
    for nm, g, dd, mm_, vv_ in zip(rest, [red[nm].reshape(a[nm].shape) for nm in rest], _unpack_small(d, like),
                                   _unpack_small(nm_, like), _unpack_small(nv_, like)):
        out_g[nm], out_d[nm], out_m[nm], out_v[nm] = g, dd, mm_, vv_
    return (loss, dx[None], *[out_g[nm] for nm in WEIGHTS], *[out_d[nm] for nm in WEIGHTS],
            *[out_m[nm] for nm in WEIGHTS], *[out_v[nm] for nm in WEIGHTS])


def kernel(x, ffn1_norm, ffn1_w1, ffn1_w3, ffn1_w2, mix_norm, w_in, s5_lambda_re, s5_lambda_im, s5_log_dt, s5_b_re, s5_b_im, s5_c_re, s5_c_im, s5_d, s5_glu_w, s5_glu_b, gla_a_up_w, gla_a_up_b, gla_out_norm, proj_s5, proj_gla, w_out, ffn2_norm, ffn2_w1, ffn2_w3, ffn2_w2, final_norm, loss_target, m_ffn1_norm, m_ffn1_w1, m_ffn1_w3, m_ffn1_w2, m_mix_norm, m_w_in, m_s5_lambda_re, m_s5_lambda_im, m_s5_log_dt, m_s5_b_re, m_s5_b_im, m_s5_c_re, m_s5_c_im, m_s5_d, m_s5_glu_w, m_s5_glu_b, m_gla_a_up_w, m_gla_a_up_b, m_gla_out_norm, m_proj_s5, m_proj_gla, m_w_out, m_ffn2_norm, m_ffn2_w1, m_ffn2_w3, m_ffn2_w2, m_final_norm, v_ffn1_norm, v_ffn1_w1, v_ffn1_w3, v_ffn1_w2, v_mix_norm, v_w_in, v_s5_lambda_re, v_s5_lambda_im, v_s5_log_dt, v_s5_b_re, v_s5_b_im, v_s5_c_re, v_s5_c_im, v_s5_d, v_s5_glu_w, v_s5_glu_b, v_gla_a_up_w, v_gla_a_up_b, v_gla_out_norm, v_proj_s5, v_proj_gla, v_w_out, v_ffn2_norm, v_ffn2_w1, v_ffn2_w3, v_ffn2_w2, v_final_norm):
    return _train_step(dict(locals()))
```

```python
import functools

import jax
import jax.numpy as jnp
from jax import lax
from jax.experimental import pallas as pl
from jax.experimental.pallas import tpu as pltpu

F32 = jnp.float32
BF16 = jnp.bfloat16
HI = lax.Precision.HIGHEST
MESH_ID = pl.DeviceIdType.MESH

D_MODEL = 1024
EPS = 1e-6
S5_G, S5_P, S5_H = 32, 64, 16
S5_W = S5_G * S5_H
S5_GP = S5_G * S5_P
SEG = 8
SCAN_ROWS = 256
GLA_HEADS, GLA_DK, GLA_DV = 4, 64, 128
GLA_CHUNK = 64
GLA_TAU = 16.0
GLA_RANK = 16
ADAM_LR, ADAM_B1, ADAM_B2, ADAM_EPS, ADAM_WD, ADAM_STEP = 0.001, 0.9, 0.999, 1e-08, 0.01, 10
V7X_VMEM_LIMIT = 56 * 1024 * 1024
LANE = 128

WEIGHTS = ['ffn1_norm', 'ffn1_w1', 'ffn1_w3', 'ffn1_w2', 'mix_norm', 'w_in', 's5_lambda_re', 's5_lambda_im',
           's5_log_dt', 's5_b_re', 's5_b_im', 's5_c_re', 's5_c_im', 's5_d', 's5_glu_w', 's5_glu_b', 'gla_a_up_w',
           'gla_a_up_b', 'gla_out_norm', 'proj_s5', 'proj_gla', 'w_out', 'ffn2_norm', 'ffn2_w1', 'ffn2_w3',
           'ffn2_w2', 'final_norm']
SHARDED = ['ffn1_w1', 'ffn1_w3', 'ffn1_w2', 'w_in', 's5_glu_w', 'proj_s5', 'proj_gla', 'w_out',
           'ffn2_w1', 'ffn2_w3', 'ffn2_w2']
COL_SHARDED = ['ffn1_w1', 'ffn1_w3', 'w_in', 'proj_s5', 'proj_gla', 'ffn2_w1', 'ffn2_w3', 'gla_a_up_w']
SMALL = [n for n in WEIGHTS if n not in SHARDED]
FFN_WEIGHTS = ['ffn1_w1', 'ffn1_w3', 'ffn1_w2', 'ffn2_w1', 'ffn2_w3', 'ffn2_w2']


def _params(**kw):
    return pltpu.CompilerParams(vmem_limit_bytes=V7X_VMEM_LIMIT, **kw)


class _Side:
    def __init__(self, ins, out_shapes, nsem, copies, aliased=False):
        self.ins, self.out_shapes, self.nsem, self.copies, self.aliased = list(ins), list(out_shapes), nsem, copies, aliased
        self.outs = None


_RIDER = []


def _pcall(body, **kw):
    if _RIDER:
        return _carry(body, _RIDER.pop(), **kw)
    return pl.pallas_call(body, **kw)


def _carry(body, side, *, name, grid, in_specs, out_specs, out_shape, scratch_shapes=(), compiler_params=None):
    del compiler_params
    single = not isinstance(out_shape, (list, tuple))
    out_specs = [out_specs] if single else list(out_specs)
    out_shape = [out_shape] if single else list(out_shape)
    n_in, n_out, n_scr = len(in_specs), len(out_shape), len(scratch_shapes)
    s_in, s_out = len(side.ins), len(side.out_shapes)
    any_spec = pl.BlockSpec(memory_space=pl.ANY)

    def wrapped(*refs):
        cuts = [n_in, s_in, n_out, s_out, n_scr]
        parts, pos = [], 0
        for c in cuts:
            parts.append(refs[pos:pos + c])
            pos += c
        ins, sins, outs, souts, scr = parts
        ssem, rsem = refs[pos], refs[pos + 1]
        first = last = None
        for d, g in enumerate(grid):
            i = pl.program_id(d)
            first = (i == 0) if first is None else first & (i == 0)
            last = (i == g - 1) if last is None else last & (i == g - 1)

        @pl.when(first)
        def _():
            for cp in side.copies(sins, souts, ssem, rsem):
                cp.start()

        body(*ins, *outs, *scr)

        @pl.when(last)
        def _():
            for cp in side.copies(sins, souts, ssem, rsem):
                cp.wait()

    call = pl.pallas_call(
        wrapped, name=name, grid=grid, in_specs=list(in_specs) + [any_spec] * s_in,
        out_specs=out_specs + [any_spec] * s_out, out_shape=out_shape + side.out_shapes,
        scratch_shapes=list(scratch_shapes) + [pltpu.SemaphoreType.DMA((side.nsem,)), pltpu.SemaphoreType.DMA((side.nsem,))],
        input_output_aliases={n_in + j: n_out + j for j in range(s_in)} if side.aliased else {},
        compiler_params=_params(has_side_effects=True))

    def run(*args):
        res = call(*args, *side.ins)
        side.outs = list(res[n_out:])
        return res[0] if single else list(res[:n_out])

    return run


def _run_side(side, name):
    s_in, s_out = len(side.ins), len(side.out_shapes)
    any_spec = pl.BlockSpec(memory_space=pl.ANY)

    def body(*refs):
        sins, souts = refs[:s_in], refs[s_in:s_in + s_out]
        ssem, rsem = refs[s_in + s_out:]
        cps = side.copies(sins, souts, ssem, rsem)
        for cp in cps:
            cp.start()
        for cp in cps:
            cp.wait()

    side.outs = list(pl.pallas_call(
        body, name=name, in_specs=[any_spec] * s_in, out_specs=[any_spec] * s_out, out_shape=side.out_shapes,
        scratch_shapes=[pltpu.SemaphoreType.DMA((side.nsem,)), pltpu.SemaphoreType.DMA((side.nsem,))],
        input_output_aliases={j: j for j in range(s_in)} if side.aliased else {},
        compiler_params=pltpu.CompilerParams(has_side_effects=True))(*side.ins))
    return side.outs


def _pick(n, cap, quantum):
    if n <= cap:
        return n
    best = None
    for t in range(quantum, cap + 1, quantum):
        if n % t == 0:
            best = t
    assert best is not None, (n, cap, quantum)
    return best


def _sigmoid(x):
    return jax.nn.sigmoid(x)


def _mm(a, b, *, name, ta=False, tb=False, out_dtype=F32, alpha=1.0, res=None, bias=None, exact=False, shard=None):
    ns = 4
    (k_a, m) = a.shape[-2:] if ta else a.shape[-2:][::-1]
    (k_b, n) = b.shape[-2:][::-1] if tb else b.shape[-2:]
    assert k_a == k_b, (a.shape, b.shape, ta, tb)
    assert (a.ndim == 3) == (shard in ('k', 'm')) and (b.ndim == 3) == (shard in ('n', 'k'))
    k = k_a
    tm = _pick(m, 1024, 128)
    tn = _pick(n, 1024, 128)
    tk = _pick(k, 1024, 128)
    pm, pn, pk = m // tm, n // tn, k // tk
    gm = pm * (ns if shard == 'm' else 1)
    gn = pn * (ns if shard == 'n' else 1)
    gk = pk * (ns if shard == 'k' else 1)
    dims = (((0,) if ta else (1,), (1,) if tb else (0,)), ((), ()))
    op_dtype = F32 if exact else BF16

    def body(*refs):
        a_ref, b_ref = refs[0], refs[1]
        pos = 2
        res_ref = bias_ref = None
        if res is not None:
            res_ref = refs[pos]
            pos += 1
        if bias is not None:
            bias_ref = refs[pos]
            pos += 1
        o_ref, acc_ref = refs[pos], refs[pos + 1]
        kk = pl.program_id(2)

        @pl.when(kk == 0)
        def _():
            acc_ref[...] = jnp.zeros_like(acc_ref)

        acc_ref[...] += lax.dot_general(a_ref[...].astype(op_dtype), b_ref[...].astype(op_dtype), dims,
                                        precision=HI if exact else None, preferred_element_type=F32)

        @pl.when(kk == gk - 1)
        def _():
            o = acc_ref[...]
            if alpha != 1.0:
                o = o * alpha
            if bias_ref is not None:
                o = o + bias_ref[...]
            if res_ref is not None:
                o = o + res_ref[...]
            o_ref[...] = o.astype(out_dtype)

    def spec(block, sharded_on, order):
        per = {'m': pm, 'n': pn, 'k': pk}

        def index(i, j, kk):
            g = {'m': i, 'n': j, 'k': kk}
            r, c = order(i % pm if shard == 'm' else i, j % pn if shard == 'n' else j, kk % pk if shard == 'k' else kk)
            if sharded_on is None:
                return (r, c)
            return (g[sharded_on] // per[sharded_on], r, c)

        return pl.BlockSpec(block if sharded_on is None else (None,) + block, index)

    a_sh = shard if shard in ('k', 'm') else None
    b_sh = shard if shard in ('n', 'k') else None
    o_sh = shard if shard in ('n', 'm') else None
    a_spec = spec((tk, tm), a_sh, lambda i, j, kk: (kk, i)) if ta else spec((tm, tk), a_sh, lambda i, j, kk: (i, kk))
    b_spec = spec((tn, tk), b_sh, lambda i, j, kk: (j, kk)) if tb else spec((tk, tn), b_sh, lambda i, j, kk: (kk, j))
    ins, in_specs = [a, b], [a_spec, b_spec]
    if res is not None:
        assert o_sh is None
        ins.append(res)
        in_specs.append(pl.BlockSpec((tm, tn), lambda i, j, kk: (i, j)))
    if bias is not None:
        assert o_sh is None
        ins.append(bias)
        in_specs.append(pl.BlockSpec((1, tn), lambda i, j, kk: (0, j)))
    out_shape = (m, n) if o_sh is None else (ns, m, n)
    return _pcall(body, name=name, grid=(gm, gn, gk), in_specs=in_specs,
                  out_specs=spec((tm, tn), o_sh, lambda i, j, kk: (i, j)),
                  out_shape=jax.ShapeDtypeStruct(out_shape, out_dtype),
                  scratch_shapes=[pltpu.VMEM((tm, tn), F32)], compiler_params=_params())(*ins)


def _rows(body, ins, outs, *, n, name, tm=256):
    tm = _pick(n, tm, 16)
    in_specs = []
    for arr, kind in ins:
        if kind == 'r':
            in_specs.append(pl.BlockSpec((tm, arr.shape[1]), lambda i: (i, 0)))
        else:
            in_specs.append(pl.BlockSpec(arr.shape, lambda i: (0, 0)))
    out_specs, out_shape = [], []
    for cols, dtype, kind in outs:
        if kind == 'r':
            out_specs.append(pl.BlockSpec((tm, cols), lambda i: (i, 0)))
            out_shape.append(jax.ShapeDtypeStruct((n, cols), dtype))
        else:
            out_specs.append(pl.BlockSpec((1, cols), lambda i: (0, 0)))
            out_shape.append(jax.ShapeDtypeStruct((1, cols), dtype))
    n_in = len(ins)
    acc_ids = [j for j, o in enumerate(outs) if o[2] == 'a']

    def wrapped(*refs):
        if acc_ids:
            @pl.when(pl.program_id(0) == 0)
            def _():
                for j in acc_ids:
                    refs[n_in + j][...] = jnp.zeros_like(refs[n_in + j])
        body(*refs)

    res = _pcall(wrapped, name=name, grid=(n // tm,), in_specs=in_specs, out_specs=out_specs, out_shape=out_shape,
                 compiler_params=_params())(*[a for a, _ in ins])
    return res


def _rms_fwd(x, g, name):
    def body(x_ref, g_ref, o_ref):
        xv = x_ref[...]
        rstd = lax.rsqrt(jnp.mean(xv * xv, axis=-1, keepdims=True) + EPS)
        o_ref[...] = (xv * rstd * g_ref[...]).astype(BF16)
    return _rows(body, [(x, 'r'), (g, 'f')], [(x.shape[1], BF16, 'r')], n=x.shape[0], name=name)[0]


def _rms_bwd(x, g, dn, dres, name):
    def body(x_ref, g_ref, dn_ref, dres_ref, dx_ref, dg_ref):
        xv = x_ref[...]
        rstd = lax.rsqrt(jnp.mean(xv * xv, axis=-1, keepdims=True) + EPS)
        xh = xv * rstd
        dn = dn_ref[...]
        dg_ref[...] += jnp.sum(dn * xh, axis=0, keepdims=True)
        dxh = dn * g_ref[...]
        dx_ref[...] = dres_ref[...] + rstd * (dxh - xh * jnp.mean(dxh * xh, axis=-1, keepdims=True))
    d = x.shape[1]
    return _rows(body, [(x, 'r'), (g, 'f'), (dn, 'r'), (dres, 'r')], [(d, F32, 'r'), (d, F32, 'a')],
                 n=x.shape[0], name=name)


def _gelu_parts(y):
    c0 = 0.7978845608028654
    inner = c0 * (y + 0.044715 * y * y * y)
    th = jnp.tanh(inner)
    return th, c0 * (1.0 + 3.0 * 0.044715 * y * y)


def _gelu_fwd(y, name):
    def body(y_ref, o_ref):
        yv = y_ref[...]
        th, _ = _gelu_parts(yv)
        o_ref[...] = 0.5 * yv * (1.0 + th)
    return _rows(body, [(y, 'r')], [(y.shape[1], F32, 'r')], n=y.shape[0], name=name)[0]


def _glu_fwd(zg, t, name):
    def body(z_ref, t_ref, o_ref):
        o_ref[...] = (z_ref[...] * _sigmoid(t_ref[...])).astype(BF16)
    return _rows(body, [(zg, 'r'), (t, 'r')], [(zg.shape[1], BF16, 'r')], n=zg.shape[0], name=name)[0]


def _glu_bwd1(dy, zg, t, name):
    def body(dy_ref, z_ref, t_ref, dz_ref, dt_ref, db_ref):
        dyv, zv = dy_ref[...], z_ref[...]
        sg = _sigmoid(t_ref[...])
        dz_ref[...] = dyv * sg
        dt = dyv * zv * sg * (1.0 - sg)
        dt_ref[...] = dt.astype(BF16)
        db_ref[...] += jnp.sum(dt, axis=0, keepdims=True)
    w = zg.shape[1]
    return _rows(body, [(dy, 'r'), (zg, 'r'), (t, 'r')], [(w, F32, 'r'), (w, BF16, 'r'), (w, F32, 'a')],
                 n=zg.shape[0], name=name)


def _glu_bwd2(dzg, ys, u, dskip, name):
    def body(dz_ref, y_ref, u_ref, d_ref, dy_ref, du_ref, dd_ref):
        yv = y_ref[...]
        th, dinner = _gelu_parts(yv)
        dy = dz_ref[...] * (0.5 * (1.0 + th) + 0.5 * yv * (1.0 - th * th) * dinner)
        dy_ref[...] = dy
        du_ref[...] = dy * d_ref[...]
        dd_ref[...] += jnp.sum(dy * u_ref[...], axis=0, keepdims=True)
    w = ys.shape[1]
    return _rows(body, [(dzg, 'r'), (ys, 'r'), (u, 'r'), (dskip, 'f')], [(w, F32, 'r'), (w, F32, 'r'), (w, F32, 'a')],
                 n=ys.shape[0], name=name)


def _scale_rows(u, dskip, name):
    def body(u_ref, d_ref, o_ref):
        o_ref[...] = u_ref[...] * d_ref[...]
    return _rows(body, [(u, 'r'), (dskip, 'f')], [(u.shape[1], F32, 'r')], n=u.shape[0], name=name)[0]


def _merge_fwd(zg, ps, pg, name):
    def body(z_ref, ps_ref, pg_ref, o_ref):
        zv = z_ref[...]
        o_ref[...] = (_sigmoid(zv[:, :D_MODEL]) * ps_ref[...] + _sigmoid(zv[:, D_MODEL:]) * pg_ref[...]).astype(BF16)
    return _rows(body, [(zg, 'r'), (ps, 'r'), (pg, 'r')], [(D_MODEL, BF16, 'r')], n=zg.shape[0], name=name)[0]


def _merge_bwd(dm, zg, ps, pg, name):
    def body(dm_ref, z_ref, ps_ref, pg_ref, dps_ref, dpg_ref, dz_ref):
        dmv, zv = dm_ref[...], z_ref[...]
        s1, s2 = _sigmoid(zv[:, :D_MODEL]), _sigmoid(zv[:, D_MODEL:])
        dps_ref[...] = (dmv * s1).astype(BF16)
        dpg_ref[...] = (dmv * s2).astype(BF16)
        dz_ref[:, :D_MODEL] = dmv * ps_ref[...] * s1 * (1.0 - s1)
        dz_ref[:, D_MODEL:] = dmv * pg_ref[...] * s2 * (1.0 - s2)
    return _rows(body, [(dm, 'r'), (zg, 'r'), (ps, 'r'), (pg, 'r')],
                 [(D_MODEL, BF16, 'r'), (D_MODEL, BF16, 'r'), (2 * D_MODEL, F32, 'r')], n=zg.shape[0], name=name)


def _final_loss(h, g, tgt, name):
    def body(h_ref, g_ref, t_ref, loss_ref, dh_ref, dg_ref):
        hv = h_ref[...]
        rstd = lax.rsqrt(jnp.mean(hv * hv, axis=-1, keepdims=True) + EPS)
        xh = hv * rstd
        err = xh * g_ref[...] - t_ref[...]
        part = 0.5 * jnp.sum(jnp.mean(err * err, axis=-1, keepdims=True), axis=0, keepdims=True)
        loss_ref[...] += jnp.broadcast_to(part, loss_ref.shape)
        dout = err * (1.0 / hv.shape[1])
        dg_ref[...] += jnp.sum(dout * xh, axis=0, keepdims=True)
        dxh = dout * g_ref[...]
        dh_ref[...] = rstd * (dxh - xh * jnp.mean(dxh * xh, axis=-1, keepdims=True))
    d = h.shape[1]
    return _rows(body, [(h, 'r'), (g, 'f'), (tgt, 'r')], [(LANE, F32, 'a'), (d, F32, 'r'), (d, F32, 'a')],
                 n=h.shape[0], name=name)


def _adamw_math(wv, gv, mv, vv):
    nm = ADAM_B1 * mv + (1.0 - ADAM_B1) * gv
    nv = ADAM_B2 * vv + (1.0 - ADAM_B2) * (gv * gv)
    m_hat = nm / (1.0 - ADAM_B1 ** ADAM_STEP)
    v_hat = nv / (1.0 - ADAM_B2 ** ADAM_STEP)
    return -ADAM_LR * (m_hat / (jnp.sqrt(v_hat) + ADAM_EPS) + ADAM_WD * wv), nm, nv


def _adamw(w, g, m, v, name):
    def body(w_ref, g_ref, m_ref, v_ref, d_ref, nm_ref, nv_ref):
        d_ref[...], nm_ref[...], nv_ref[...] = _adamw_math(w_ref[...], g_ref[...], m_ref[...], v_ref[...])
    c = w.shape[1]
    return _rows(body, [(w, 'r'), (g, 'r'), (m, 'r'), (v, 'r')], [(c, F32, 'r')] * 3, n=w.shape[0], name=name)


def _adamw_halves(w, g_own, g_sib, m, v, c_arr, name):
    r, cols = w.shape
    h = r // 2
    tr = _pick(h, 256, 8)
    per = h // tr

    def body(c_ref, w_ref, go_ref, gs_ref, m_ref, v_ref, g_ref, d_ref, nm_ref, nv_ref):
        mine = (pl.program_id(0) // per) == c_ref[0]
        gv = jnp.where(mine, go_ref[...], gs_ref[...])
        g_ref[...] = gv
        d_ref[...], nm_ref[...], nv_ref[...] = _adamw_math(w_ref[...], gv, m_ref[...], v_ref[...])

    full = pl.BlockSpec((tr, cols), lambda i, c_ref: (i, 0))
    half = pl.BlockSpec((tr, cols), lambda i, c_ref: (i % per, 0))
    grid_spec = pltpu.PrefetchScalarGridSpec(num_scalar_prefetch=1, grid=(2 * per,),
                                             in_specs=[full, half, half, full, full], out_specs=[full] * 4)
    return _pcall(body, name=name, grid_spec=grid_spec, out_shape=[jax.ShapeDtypeStruct((r, cols), F32)] * 4,
                  compiler_params=_params())(c_arr, w, g_own, g_sib, m, v)


def _shift_rows(v, sh, down):
    rolled = pltpu.roll(v, sh if down else v.shape[0] - sh, axis=0)
    row = lax.broadcasted_iota(jnp.int32, v.shape, 0)
    keep = (row >= sh) if down else (row < v.shape[0] - sh)
    return jnp.where(keep, rolled, 0.0)


def _chain_segments(st_r, st_i, pw_r_ref, pw_i_ref, conj, down):
    vr, vi = st_r[...], st_i[...]
    sh, k = 1, 0
    while sh < SEG:
        pr, pi = pw_r_ref[k:k + 1, :], pw_i_ref[k:k + 1, :]
        if conj:
            pi = -pi
        sr, si = _shift_rows(vr, sh, down), _shift_rows(vi, sh, down)
        vr, vi = vr + pr * sr - pi * si, vi + pr * si + pi * sr
        sh, k = sh * 2, k + 1
    st_r[...] = _shift_rows(vr, 1, down)
    st_i[...] = _shift_rows(vi, 1, down)


def _s5_scan(bu, ar8, ai8, pw_r, pw_i, name):
    n = bu.shape[0]
    rb = SCAN_ROWS
    nb, steps, lc = n // rb, rb // SEG, 512

    def body(bu_ref, ar_ref, ai_ref, pwr_ref, pwi_ref, x_ref, st_r, st_i):
        ph, b = pl.program_id(0), pl.program_id(1)

        @pl.when((ph == 0) & (b == 0))
        def _():
            st_r[...] = jnp.zeros_like(st_r)
            st_i[...] = jnp.zeros_like(st_i)

        def scan(store):
            for c in range(S5_GP // lc):
                re, im = slice(c * lc, (c + 1) * lc), slice(S5_GP + c * lc, S5_GP + (c + 1) * lc)
                a_r, a_i = ar_ref[:, re], ai_ref[:, re]

                def step(s, carry):
                    xr, xi = carry
                    rows = pl.ds(pl.multiple_of(s * SEG, SEG), SEG)
                    nr = a_r * xr - a_i * xi + bu_ref[rows, re]
                    ni = a_r * xi + a_i * xr + bu_ref[rows, im]
                    if store:
                        x_ref[rows, re] = nr
                        x_ref[rows, im] = ni
                    return nr, ni

                xr, xi = lax.fori_loop(0, steps, step, (st_r[:, re], st_i[:, re]), unroll=4)
                st_r[:, re] = xr
                st_i[:, re] = xi

        @pl.when(ph == 0)
        def _():
            scan(False)

        @pl.when((ph == 0) & (b == nb - 1))
        def _():
            _chain_segments(st_r, st_i, pwr_ref, pwi_ref, conj=False, down=True)

        @pl.when(ph == 1)
        def _():
            scan(True)

    full = lambda a: pl.BlockSpec(a.shape, lambda ph, b: (0, 0))
    return _pcall(body, name=name, grid=(2, nb),
                  in_specs=[pl.BlockSpec((rb, 2 * S5_GP), lambda ph, b: (b, 0)), full(ar8), full(ai8), full(pw_r), full(pw_i)],
                  out_specs=pl.BlockSpec((rb, 2 * S5_GP), lambda ph, b: (b * ph, 0)),
                  out_shape=jax.ShapeDtypeStruct((n, 2 * S5_GP), F32),
                  scratch_shapes=[pltpu.VMEM((SEG, S5_GP), F32), pltpu.VMEM((SEG, S5_GP), F32)],
                  compiler_params=_params())(bu, ar8, ai8, pw_r, pw_i)


def _s5_scan_bwd(gx, xs, ar8, ai8, pw_r, pw_i, name):
    n = gx.shape[0]
    rb = SCAN_ROWS
    nb, steps, lc = n // rb, rb // SEG, 256

    def body(gx_ref, x_ref, ar_ref, ai_ref, pwr_ref, pwi_ref, lam_ref, da_ref, st_r, st_i):
        ph, b = pl.program_id(0), pl.program_id(1)

        @pl.when((ph == 0) & (b == 0))
        def _():
            st_r[...] = jnp.zeros_like(st_r)
            st_i[...] = jnp.zeros_like(st_i)
            da_ref[...] = jnp.zeros_like(da_ref)

        def scan(store):
            for c in range(S5_GP // lc):
                re, im = slice(c * lc, (c + 1) * lc), slice(S5_GP + c * lc, S5_GP + (c + 1) * lc)
                a_r, a_i = ar_ref[:, re], ai_ref[:, re]

                def step(s, carry):
                    rows = pl.ds(pl.multiple_of((steps - 1 - s) * SEG, SEG), SEG)
                    if store:
                        lr, li, dr, di = carry
                        xr, xi = x_ref[rows, re], x_ref[rows, im]
                        dr = dr + lr * xr + li * xi
                        di = di + li * xr - lr * xi
                    else:
                        lr, li = carry
                    nr = a_r * lr + a_i * li + gx_ref[rows, re]
                    ni = a_r * li - a_i * lr + gx_ref[rows, im]
                    if store:
                        lam_ref[rows, re] = nr
                        lam_ref[rows, im] = ni
                        return nr, ni, dr, di
                    return nr, ni

                if store:
                    lr, li, dr, di = lax.fori_loop(0, steps, step, (st_r[:, re], st_i[:, re], da_ref[:, re], da_ref[:, im]),
                                                   unroll=4)
                    da_ref[:, re] = dr
                    da_ref[:, im] = di
                else:
                    lr, li = lax.fori_loop(0, steps, step, (st_r[:, re], st_i[:, re]), unroll=4)
                st_r[:, re] = lr
                st_i[:, re] = li

        @pl.when(ph == 0)
        def _():
            scan(False)

        @pl.when((ph == 0) & (b == nb - 1))
        def _():
            _chain_segments(st_r, st_i, pwr_ref, pwi_ref, conj=True, down=False)

        @pl.when(ph == 1)
        def _():
            scan(True)

    full = lambda a: pl.BlockSpec(a.shape, lambda ph, b: (0, 0))
    rev = lambda ph, b: (nb - 1 - b, 0)
    return _pcall(body, name=name, grid=(2, nb),
                  in_specs=[pl.BlockSpec((rb, 2 * S5_GP), rev), pl.BlockSpec((rb, 2 * S5_GP), lambda ph, b: ((nb - 1 - b) * ph, 0)),
                            full(ar8), full(ai8), full(pw_r), full(pw_i)],
                  out_specs=[pl.BlockSpec((rb, 2 * S5_GP), lambda ph, b: (nb - 1 - b * ph, 0)),
                             pl.BlockSpec((SEG, 2 * S5_GP), lambda ph, b: (0, 0))],
                  out_shape=[jax.ShapeDtypeStruct((n, 2 * S5_GP), F32), jax.ShapeDtypeStruct((SEG, 2 * S5_GP), F32)],
                  scratch_shapes=[pltpu.VMEM((SEG, S5_GP), F32), pltpu.VMEM((SEG, S5_GP), F32)],
                  compiler_params=_params())(gx, xs, ar8, ai8, pw_r, pw_i)


def _s5_discretize(lam_re, lam_im, log_dt, b_re, b_im):
    dt = jnp.exp(log_dt)[:, None]
    mag = jnp.exp(lam_re * dt)
    ar = mag * jnp.cos(lam_im * dt)
    ai = mag * jnp.sin(lam_im * dt)
    den = lam_re * lam_re + lam_im * lam_im
    nr = ar - 1.0
    fr = (nr * lam_re + ai * lam_im) / den
    fi = (ai * lam_re - nr * lam_im) / den
    bbar_re = fr[:, :, None] * b_re - fi[:, :, None] * b_im
    bbar_im = fr[:, :, None] * b_im + fi[:, :, None] * b_re
    return ar, ai, bbar_re, bbar_im


def _block_diag(t):
    g, a, b = t.shape
    eye = jnp.eye(g, dtype=t.dtype)
    return (t[:, :, None, :] * eye[:, None, :, None]).reshape(g * a, g * b)


def _diag_blocks(m, a, b):
    g = S5_G
    return jnp.einsum('gagb->gab', m.reshape(g, a, g, b))


def _permute_rows(t):
    n = t.shape[0]
    return t.reshape(SEG, n // SEG, t.shape[1]).transpose(1, 0, 2).reshape(n, t.shape[1])


def _unpermute_rows(t):
    n = t.shape[0]
    return t.reshape(n // SEG, SEG, t.shape[1]).transpose(1, 0, 2).reshape(n, t.shape[1])


def _segment_powers(ar, ai, seg_steps):
    pr, pi = ar.reshape(1, S5_GP), ai.reshape(1, S5_GP)
    e = 1
    while e < seg_steps:
        pr, pi = pr * pr - pi * pi, 2.0 * pr * pi
        e *= 2
    assert e == seg_steps, "segment length must be a power of two"
    rows_r, rows_i = [], []
    for _ in range(3):
        rows_r.append(pr)
        rows_i.append(pi)
        pr, pi = pr * pr - pi * pi, 2.0 * pr * pi
    pad = jnp.zeros((SEG - 3, S5_GP), F32)
    return jnp.concatenate(rows_r + [pad], axis=0), jnp.concatenate(rows_i + [pad], axis=0)


NT = (((1,), (1,)), ((), ()))
TN = (((0,), (0,)), ((), ()))


def _dot(a, b, dims=None, exact=False):
    dims = (((1,), (0,)), ((), ())) if dims is None else dims
    if exact:
        return lax.dot_general(a, b, dims, precision=HI, preferred_element_type=F32)
    return lax.dot_general(a.astype(BF16), b.astype(BF16), dims, preferred_element_type=F32)


def _dot01(a, b, dims=None, ones_first=True):
    x = b if ones_first else a
    hi = x.astype(BF16)
    r1 = x - hi.astype(F32)
    mid = r1.astype(BF16)
    lo = (r1 - mid.astype(F32)).astype(BF16)
    parts = [(_dot(a, p, dims) if ones_first else _dot(p, b, dims)) for p in (lo, mid, hi)]
    return (parts[0] + parts[1]) + parts[2]


HEADS = range(4)


def _gla_chunk_fwd(qc, kc, vc, al, wup, bup, s_prev, tril):
    ones = jnp.ones((GLA_CHUNK, GLA_DV), F32)
    z = [_dot(al, wup[h]) + bup[h] for h in HEADS]
    la = [(jnp.minimum(z[h], 0.0) - jnp.log(1.0 + jnp.exp(-jnp.abs(z[h])))) * (1.0 / GLA_TAU) for h in HEADS]
    bc = [_dot01(tril, la[h]) for h in HEADS]
    blb = [_dot01(la[h], ones, TN, ones_first=False) for h in HEADS]
    bl = [bc[h][GLA_CHUNK - 1:GLA_CHUNK, :] for h in HEADS]
    ebc = [jnp.exp(bc[h]) for h in HEADS]
    qt = [qc[h] * (GLA_DK ** -0.5) * ebc[h] for h in HEADS]
    kt = [kc[h] * jnp.exp(-bc[h]) for h in HEADS]
    ke = [kc[h] * jnp.exp(bl[h] - bc[h]) for h in HEADS]
    sc = [_dot(qt[h], kt[h], NT) * tril for h in HEADS]
    oi = [_dot(sc[h], vc[h]) for h in HEADS]
    oo = [_dot(qt[h], s_prev[h]) for h in HEADS]
    o = [oi[h] + oo[h] for h in HEADS]
    return z, bc, bl, blb, ebc, qt, kt, ke, sc, o


GLA_ROWS = 512
GLA_CPB = GLA_ROWS // GLA_CHUNK


def _gla_in_specs(arrs, blk):
    specs = []
    for a in arrs:
        if a.ndim == 3 and a.shape[1] > LANE:
            specs.append(pl.BlockSpec((GLA_HEADS, GLA_ROWS, a.shape[2]), lambda j: (0, blk(j), 0)))
        elif a.ndim == 3:
            specs.append(pl.BlockSpec(a.shape, lambda j: (0, 0, 0)))
        else:
            specs.append(pl.BlockSpec((GLA_ROWS, a.shape[1]), lambda j: (blk(j), 0)))
    return specs


def _tri(lower):
    ri = lax.broadcasted_iota(jnp.int32, (GLA_CHUNK, GLA_CHUNK), 0)
    ci = lax.broadcasted_iota(jnp.int32, (GLA_CHUNK, GLA_CHUNK), 1)
    return ((ri >= ci) if lower else (ri <= ci)).astype(F32)


def _gla_fwd(q, k, v, r, al, wup, bup, gn, name):
    n = q.shape[1]
    nc = n // GLA_CHUNK

    def body(q_ref, k_ref, v_ref, r_ref, al_ref, wup_ref, bup_ref, gn_ref, y_ref, sp_ref, s_ref):
        @pl.when(pl.program_id(0) == 0)
        def _():
            s_ref[...] = jnp.zeros_like(s_ref)

        tril = _tri(True)

        def chunk(c, carry):
            rows = pl.ds(pl.multiple_of(c * GLA_CHUNK, GLA_CHUNK), GLA_CHUNK)
            alc = al_ref[rows, :]
            vc = [v_ref[h, rows, :] for h in HEADS]
            s_prev = [s_ref[h] for h in HEADS]
            _, _, _, blb, _, _, _, ke, _, o = _gla_chunk_fwd(
                [q_ref[h, rows, :] for h in HEADS], [k_ref[h, rows, :] for h in HEADS], vc, alc,
                [wup_ref[h] for h in HEADS], [bup_ref[h] for h in HEADS], s_prev, tril)
            ds = [_dot(ke[h], vc[h], TN) for h in HEADS]
            for h in HEADS:
                rc = r_ref[h, rows, :]
                sp_ref[h, c] = s_prev[h]
                rstd = lax.rsqrt(jnp.mean(o[h] * o[h], axis=-1, keepdims=True) + EPS)
                y_ref[h, rows, :] = o[h] * rstd * gn_ref[h] * (rc * _sigmoid(rc))
                s_ref[h] = jnp.exp(blb[h]) * s_prev[h] + ds[h]
            return carry

        lax.fori_loop(0, GLA_CPB, chunk, 0)

    ins = [q, k, v, r, al, wup, bup, gn]
    return _pcall(body, name=name, grid=(n // GLA_ROWS,), in_specs=_gla_in_specs(ins, lambda j: j),
                  out_specs=[pl.BlockSpec((GLA_HEADS, GLA_ROWS, GLA_DV), lambda j: (0, j, 0)),
                             pl.BlockSpec((GLA_HEADS, GLA_CPB, GLA_DK, GLA_DV), lambda j: (0, j, 0, 0))],
                  out_shape=[jax.ShapeDtypeStruct((GLA_HEADS, n, GLA_DV), F32),
                             jax.ShapeDtypeStruct((GLA_HEADS, nc, GLA_DK, GLA_DV), F32)],
                  scratch_shapes=[pltpu.VMEM((GLA_HEADS, GLA_DK, GLA_DV), F32)],
                  compiler_params=_params())(*ins)


def _gla_bwd(q, k, v, r, al, wup, bup, gn, sp, dy, name):
    n = q.shape[1]
    nc = n // GLA_CHUNK

    nb = n // GLA_ROWS

    def body(q_ref, k_ref, v_ref, r_ref, al_ref, wup_ref, bup_ref, gn_ref, dy_ref, sp_ref,
             dq_ref, dk_ref, dv_ref, dr_ref, dz_ref, dgn_ref, dbup_ref, ds_ref):
        @pl.when(pl.program_id(0) == 0)
        def _():
            ds_ref[...] = jnp.zeros_like(ds_ref)
            dgn_ref[...] = jnp.zeros_like(dgn_ref)
            dbup_ref[...] = jnp.zeros_like(dbup_ref)

        tril, triu = _tri(True), _tri(False)

        def chunk(i, carry):
            c = GLA_CPB - 1 - i
            rows = pl.ds(pl.multiple_of(c * GLA_CHUNK, GLA_CHUNK), GLA_CHUNK)
            alc = al_ref[rows, :]
            qc = [q_ref[h, rows, :] for h in HEADS]
            kc = [k_ref[h, rows, :] for h in HEADS]
            vc = [v_ref[h, rows, :] for h in HEADS]
            s_prev = [sp_ref[h, c] for h in HEADS]
            ds = [ds_ref[h] for h in HEADS]
            z, bc, bl, blb, ebc, qt, kt, ke, sc, o = _gla_chunk_fwd(
                qc, kc, vc, alc, [wup_ref[h] for h in HEADS], [bup_ref[h] for h in HEADS], s_prev, tril)
            do, rstd = [], []
            for h in HEADS:
                rc = r_ref[h, rows, :]
                rs = lax.rsqrt(jnp.mean(o[h] * o[h], axis=-1, keepdims=True) + EPS)
                on = o[h] * rs
                sr = _sigmoid(rc)
                sil = rc * sr
                dyv, gnv = dy_ref[h, rows, :], gn_ref[h]
                dgn_ref[h] += jnp.sum(dyv * on * sil, axis=0, keepdims=True)
                dr_ref[h, rows, :] = dyv * on * gnv * (sr * (1.0 + rc * (1.0 - sr)))
                don = dyv * gnv * sil
                do.append(rs * (don - on * jnp.mean(don * on, axis=-1, keepdims=True)))
            dp = [_dot(do[h], vc[h], NT) * tril for h in HEADS]
            dv1 = [_dot(sc[h], do[h], TN) for h in HEADS]
            dv2 = [_dot(ke[h], ds[h]) for h in HEADS]
            dq2 = [_dot(do[h], s_prev[h], NT) for h in HEADS]
            dke = [_dot(vc[h], ds[h], NT) for h in HEADS]
            ddec = [_dot01(jnp.ones((8, GLA_DV), F32), ds[h] * s_prev[h], NT)[0:1, :] for h in HEADS]
            dsn = [_dot(qt[h], do[h], TN) for h in HEADS]
            dq1 = [_dot(dp[h], kt[h]) for h in HEADS]
            dkt = [_dot(dp[h], qt[h], TN) for h in HEADS]
            dbc, dbl = [], []
            for h in HEADS:
                dqt = dq1[h] + dq2[h]
                dv_ref[h, rows, :] = dv1[h] + dv2[h]
                ds_ref[h] = jnp.exp(blb[h]) * ds[h] + dsn[h]
                dq_ref[h, rows, :] = dqt * (GLA_DK ** -0.5) * ebc[h]
                dk_ref[h, rows, :] = dkt[h] * jnp.exp(-bc[h]) + dke[h] * jnp.exp(bl[h] - bc[h])
                dbc.append(dqt * qt[h] - dkt[h] * kt[h] - dke[h] * ke[h])
                dbl.append(jnp.sum(dke[h] * ke[h], axis=0, keepdims=True) + ddec[h] * jnp.exp(bl[h]))
            dla = [_dot01(triu, dbc[h]) + dbl[h] for h in HEADS]
            for h in HEADS:
                dz = dla[h] * (1.0 - _sigmoid(z[h])) * (1.0 / GLA_TAU)
                dz_ref[h, rows, :] = dz
                dbup_ref[h] += jnp.sum(dz, axis=0, keepdims=True)
            return carry

        lax.fori_loop(0, GLA_CPB, chunk, 0)

    rev = lambda j: nb - 1 - j
    ins = [q, k, v, r, al, wup, bup, gn, dy, sp]
    in_specs = _gla_in_specs(ins[:9], rev) + [pl.BlockSpec((GLA_HEADS, GLA_CPB, GLA_DK, GLA_DV), lambda j: (0, rev(j), 0, 0))]
    hs = lambda w: pl.BlockSpec((GLA_HEADS, GLA_ROWS, w), lambda j: (0, rev(j), 0))
    h1 = lambda w: pl.BlockSpec((GLA_HEADS, 1, w), lambda j: (0, 0, 0))
    sh = lambda w: jax.ShapeDtypeStruct((GLA_HEADS, n, w), F32)
    s1 = lambda w: jax.ShapeDtypeStruct((GLA_HEADS, 1, w), F32)
    return _pcall(body, name=name, grid=(nb,), in_specs=in_specs,
                  out_specs=[hs(GLA_DK), hs(GLA_DK), hs(GLA_DV), hs(GLA_DV), hs(GLA_DK), h1(GLA_DV), h1(GLA_DK)],
                  out_shape=[sh(GLA_DK), sh(GLA_DK), sh(GLA_DV), sh(GLA_DV), sh(GLA_DK), s1(GLA_DV), s1(GLA_DK)],
                  scratch_shapes=[pltpu.VMEM((GLA_HEADS, GLA_DK, GLA_DV), F32)],
                  compiler_params=_params())(*ins)


def _heads(t, w):
    return t.reshape(t.shape[0], GLA_HEADS, w).transpose(1, 0, 2)


def _unheads(t):
    return t.transpose(1, 0, 2).reshape(t.shape[1], GLA_HEADS * t.shape[2])


ANY = pl.BlockSpec(memory_space=pl.ANY)


def _place():
    x, y, c = lax.axis_index("x"), lax.axis_index("y"), lax.axis_index("c")
    chips = [(1 - x, y), (x, 1 - y), (1 - x, 1 - y)]
    return x, y, c, chips


def _remote(src, dst, ssem, rsem, dev):
    return pltpu.make_async_remote_copy(src_ref=src, dst_ref=dst, send_sem=ssem, recv_sem=rsem, device_id=dev,
                                        device_id_type=MESH_ID)


def _half(c, rows):
    h = rows // 2
    return pl.ds(pl.multiple_of(c * h, 8), h)


def _side_gather_ici(shards):
    def copies(ins, outs, ssem, rsem):
        x, y, c, chips = _place()
        mine = 2 * x + y
        cps = []
        for w in range(len(ins)):
            half = _half(c, ins[w].shape[0])
            cps.append(_remote(ins[w], outs[w].at[mine], ssem.at[4 * w], rsem.at[4 * w], (x, y, 1 - c)))
            for k, (px, py) in enumerate(chips):
                cps.append(_remote(ins[w].at[half], outs[w].at[mine, half], ssem.at[4 * w + 1 + k], rsem.at[4 * w + 1 + k],
                                   (px, py, c)))
        return cps

    return _Side(shards, [jax.ShapeDtypeStruct((4,) + s.shape, s.dtype) for s in shards], 4 * len(shards), copies)


def _side_gather_d2d(gathered):
    def copies(ins, outs, ssem, rsem):
        x, y, c, chips = _place()
        cps = []
        for w in range(len(outs)):
            half = _half(c, outs[w].shape[1])
            for k, (px, py) in enumerate(chips):
                theirs = outs[w].at[2 * px + py, half]
                cps.append(_remote(theirs, theirs, ssem.at[3 * w + k], rsem.at[3 * w + k], (x, y, 1 - c)))
        return cps

    return _Side(gathered, [jax.ShapeDtypeStruct(g.shape, g.dtype) for g in gathered], 3 * len(gathered), copies,
                 aliased=True)


def _side_swap_halves(grads):
    def copies(ins, outs, ssem, rsem):
        x, y, c, _ = _place()
        return [_remote(ins[w].at[:, _half(1 - c, ins[w].shape[1]), :], outs[w], ssem.at[w], rsem.at[w], (x, y, 1 - c))
                for w in range(len(ins))]

    return _Side(grads, [jax.ShapeDtypeStruct((4, g.shape[1] // 2, g.shape[2]), g.dtype) for g in grads], len(grads), copies)


def _side_scatter(sums):
    def copies(ins, outs, ssem, rsem):
        x, y, c, chips = _place()
        return [_remote(ins[w].at[2 * px + py], outs[w].at[k], ssem.at[3 * w + k], rsem.at[3 * w + k], (px, py, c))
                for w in range(len(ins)) for k, (px, py) in enumerate(chips)]

    return _Side(sums, [jax.ShapeDtypeStruct((3,) + s.shape[1:], s.dtype) for s in sums], 3 * len(sums), copies)


def _side_swap_reduced(halves):
    def copies(ins, outs, ssem, rsem):
        x, y, c, _ = _place()
        return [_remote(ins[w], outs[w], ssem.at[w], rsem.at[w], (x, y, 1 - c)) for w in range(len(ins))]

    return _Side(halves, [jax.ShapeDtypeStruct(h.shape, h.dtype) for h in halves], len(halves), copies)


def _chip_sum(g, recv, c_arr, name):
    _, r, cols = g.shape
    h = r // 2
    tr = _pick(h, 256, 16)
    g4 = g.reshape(4, 2, h, cols)

    def body(c_ref, g_ref, r_ref, o_ref):
        o_ref[...] = (g_ref[...] + r_ref[...]).astype(BF16)

    grid_spec = pltpu.PrefetchScalarGridSpec(
        num_scalar_prefetch=1, grid=(4, h // tr),
        in_specs=[pl.BlockSpec((None, None, tr, cols), lambda s, i, c_ref: (s, c_ref[0], i, 0)),
                  pl.BlockSpec((None, tr, cols), lambda s, i, c_ref: (s, i, 0))],
        out_specs=pl.BlockSpec((None, tr, cols), lambda s, i, c_ref: (s, i, 0)))
    return _pcall(body, name=name, grid_spec=grid_spec, out_shape=jax.ShapeDtypeStruct((4, h, cols), BF16),
                  compiler_params=_params())(c_arr, g4, recv)


def _owner_sum(sums, others, s_arr, name):
    _, h, cols = sums.shape
    tr = _pick(h, 256, 16)

    def body(s_ref, a_ref, o_ref, out_ref):
        f = lambda v: v.astype(F32)
        out_ref[...] = (f(a_ref[...]) + f(o_ref[0])) + (f(o_ref[1]) + f(o_ref[2]))

    grid_spec = pltpu.PrefetchScalarGridSpec(
        num_scalar_prefetch=1, grid=(h // tr,),
        in_specs=[pl.BlockSpec((None, tr, cols), lambda i, s_ref: (s_ref[0], i, 0)),
                  pl.BlockSpec((3, tr, cols), lambda i, s_ref: (0, i, 0))],
        out_specs=pl.BlockSpec((tr, cols), lambda i, s_ref: (i, 0)))
    return _pcall(body, name=name, grid_spec=grid_spec, out_shape=jax.ShapeDtypeStruct((h, cols), F32),
                  compiler_params=_params())(s_arr, sums, others)


def _allreduce_small(v, name):
    def body(v_ref, o_ref, r0, r1, ssem, rsem):
        x, y, c, chips = _place()
        cp = _remote(v_ref, r0, ssem.at[0], rsem.at[0], (x, y, 1 - c))
        cp.start()
        cp.wait()
        o_ref[...] = v_ref[...] + r0[...]
        cps = []
        for k, (px, py) in enumerate(chips):
            cp = _remote(o_ref, r1.at[k], ssem.at[1 + k], rsem.at[1 + k], (px, py, c))
            cp.start()
            cps.append(cp)
        for cp in cps:
            cp.wait()
        o_ref[...] = (o_ref[...] + r1[0]) + (r1[1] + r1[2])

    vm = pl.BlockSpec(memory_space=pltpu.VMEM)
    return _pcall(body, name=name, in_specs=[vm], out_specs=vm, out_shape=jax.ShapeDtypeStruct(v.shape, F32),
                  scratch_shapes=[pltpu.VMEM(v.shape, F32), pltpu.VMEM((3,) + v.shape, F32),
                                  pltpu.SemaphoreType.DMA((4,)), pltpu.SemaphoreType.DMA((4,))],
                  compiler_params=_params(has_side_effects=True))(v)


def _pack_small(parts):
    flat = jnp.concatenate([p.reshape(-1).astype(F32) for p in parts])
    pad = (-flat.shape[0]) % (64 * LANE)
    return jnp.pad(flat, (0, pad)).reshape(-1, LANE)


def _unpack_small(packed, like):
    flat, out, pos = packed.reshape(-1), [], 0
    for p in like:
        out.append(flat[pos:pos + p.size].reshape(p.shape))
        pos += p.size
    return out


FFN_FWD_ROWS, FFN_BWD_ROWS = 1024, 512
FFN_SUB_ROWS = 256


def _ffn_specs(n, d, fs, cap):
    rows = _pick(n, cap, 16)
    row = pl.BlockSpec((rows, d), lambda i, s: (i, 0))
    gain = pl.BlockSpec((1, d), lambda i, s: (0, 0))
    w_row = pl.BlockSpec((None, fs, d), lambda i, s: (s, 0, 0))
    hid = pl.BlockSpec((None, rows, fs), lambda i, s: (s, i, 0))
    return rows, row, gain, w_row, hid


def _ffn_fwd(h, g, w1t, w3t, w2, tag, plan):
    n, d = h.shape
    ns, fs, _ = w2.shape
    rows, row, gain, w_row, hid = _ffn_specs(n, d, fs, FFN_FWD_ROWS)
    sub = rows

    def body(h_ref, g_ref, w1_ref, w3_ref, w2_ref, out_ref, n1_ref, a_ref, b_ref, hm_ref, acc_ref):
        s = pl.program_id(1)

        @pl.when(s == 0)
        def _():
            xv = h_ref[...]
            rstd = lax.rsqrt(jnp.mean(xv * xv, axis=-1, keepdims=True) + EPS)
            n1_ref[...] = (xv * rstd * g_ref[...]).astype(BF16)
            acc_ref[...] = jnp.zeros_like(acc_ref)

        def up(j):
            n1 = n1_ref[j * sub:(j + 1) * sub, :]
            return _dot(n1, w1_ref[...], NT), _dot(n1, w3_ref[...], NT)

        cur = up(0)
        for j in range(rows // sub):
            nxt = up(j + 1) if (j + 1) * sub < rows else None
            a, b = cur
            r = slice(j * sub, (j + 1) * sub)
            hm = (a * _sigmoid(a) * b).astype(BF16)
            a_ref[r, :] = a.astype(BF16)
            b_ref[r, :] = b.astype(BF16)
            hm_ref[r, :] = hm
            acc_ref[r, :] += _dot(hm, w2_ref[...])
            cur = nxt

        @pl.when(s == ns - 1)
        def _():
            out_ref[...] = h_ref[...] + 0.5 * acc_ref[...]

    hid_shape = jax.ShapeDtypeStruct((ns, n, fs), BF16)
    plan.before(f"{tag}_fwd")
    out, n1, a, b, hm = _pcall(
        body, name=f"{tag}_fwd", grid=(n // rows, ns), in_specs=[row, gain, w_row, w_row, w_row],
        out_specs=[row, row, hid, hid, hid],
        out_shape=[jax.ShapeDtypeStruct((n, d), F32), jax.ShapeDtypeStruct((n, d), BF16), hid_shape, hid_shape, hid_shape],
        scratch_shapes=[pltpu.VMEM((rows, d), F32)], compiler_params=_params())(h, g, w1t, w3t, w2)
    plan.after(f"{tag}_fwd")
    return out, (h, n1, a, b, hm)


def _ffn_bwd(dout, saved, g, w1, w3, w2, tag, plan):
    h, n1, a, b, hm = saved
    n, d = h.shape
    ns, fs, _ = w2.shape
    rows, row, gain, w_row, hid = _ffn_specs(n, d, fs, FFN_BWD_ROWS)
    sub = _pick(rows, FFN_SUB_ROWS, 16)

    def body(do_ref, h_ref, g_ref, a_ref, b_ref, w1_ref, w3_ref, w2_ref, dh_ref, da_ref, db_ref, dg_ref, acc_ref):
        i, s = pl.program_id(0), pl.program_id(1)

        @pl.when(s == 0)
        def _():
            acc_ref[...] = jnp.zeros_like(acc_ref)

        @pl.when((s == 0) & (i == 0))
        def _():
            dg_ref[...] = jnp.zeros_like(dg_ref)

        def up(j):
            return _dot(0.5 * do_ref[j * sub:(j + 1) * sub, :], w2_ref[...], NT)

        cur = up(0)
        for j in range(rows // sub):
            nxt = up(j + 1) if (j + 1) * sub < rows else None
            r = slice(j * sub, (j + 1) * sub)
            av, bv = a_ref[r, :].astype(F32), b_ref[r, :].astype(F32)
            sg = _sigmoid(av)
            da = (cur * bv * (sg * (1.0 + av * (1.0 - sg)))).astype(BF16)
            db = (cur * av * sg).astype(BF16)
            da_ref[r, :] = da
            db_ref[r, :] = db
            acc_ref[r, :] += _dot(da, w1_ref[...]) + _dot(db, w3_ref[...])
            cur = nxt

        @pl.when(s == ns - 1)
        def _():
            xv, dn = h_ref[...], acc_ref[...]
            rstd = lax.rsqrt(jnp.mean(xv * xv, axis=-1, keepdims=True) + EPS)
            xh = xv * rstd
            dg_ref[...] += jnp.sum(dn * xh, axis=0, keepdims=True)
            dxh = dn * g_ref[...]
            dh_ref[...] = do_ref[...] + rstd * (dxh - xh * jnp.mean(dxh * xh, axis=-1, keepdims=True))

    hid_shape = jax.ShapeDtypeStruct((ns, n, fs), BF16)
    plan.before(f"{tag}_bwd")
    dh, da, db, dg = _pcall(
        body, name=f"{tag}_bwd", grid=(n // rows, ns), in_specs=[row, row, gain, hid, hid, w_row, w_row, w_row],
        out_specs=[row, hid, hid, gain],
        out_shape=[jax.ShapeDtypeStruct((n, d), F32), hid_shape, hid_shape, jax.ShapeDtypeStruct((1, d), F32)],
        scratch_shapes=[pltpu.VMEM((rows, d), F32)], compiler_params=_params())(dout, h, g, a, b, w1, w3, w2)
    plan.after(f"{tag}_bwd")
    plan.before(f"{tag}_gw2")
    gw2 = _mm(hm, dout, ta=True, shard='m', alpha=0.5, name=f"{tag}_gw2")
    plan.after(f"{tag}_gw2")
    gw1 = _mm(da, n1, ta=True, shard='m', name=f"{tag}_gw1")
    gw3 = _mm(db, n1, ta=True, shard='m', name=f"{tag}_gw3")
    return dh, dg, gw1, gw3, gw2


def _local_step(x, tgt, plan):
    n = x.shape[0]
    grads = plan.grads

    def f(name):
        w = plan.get(name)
        return w.reshape(1, D_MODEL) if name.endswith('_norm') and name != 'gla_out_norm' else w

    def carried(tag, fn, *args, **kw):
        plan.before(tag)
        out = fn(*args, **kw)
        plan.after(tag)
        return out

    h1, ffn1 = _ffn_fwd(x, f('ffn1_norm'), f('ffn1_w1'), f('ffn1_w3'), f('ffn1_w2'), "ffn1", plan)
    u = carried("mix_rms", _rms_fwd, h1, f('mix_norm'), "mix_rms")
    w_in = f('w_in')
    w_a = w_in[:, :2048]
    w_al = jnp.pad(w_in[:, 2048:2048 + GLA_RANK], ((0, 0), (0, LANE - GLA_RANK)))
    w_g = w_in[:, 2048 + GLA_RANK:]
    za = _mm(u, w_a, name="in_a")
    zg = _mm(u, w_g, name="in_g")
    al = _mm(u, w_al, name="in_al")
    ar, ai, bbar_re, bbar_im = _s5_discretize(f('s5_lambda_re'), f('s5_lambda_im'), f('s5_log_dt'), f('s5_b_re'), f('s5_b_im'))
    b_blk = jnp.concatenate([_block_diag(bbar_re.transpose(0, 2, 1)), _block_diag(bbar_im.transpose(0, 2, 1))], axis=1)
    c_blk = jnp.concatenate([_block_diag(f('s5_c_re').transpose(0, 2, 1)), -_block_diag(f('s5_c_im').transpose(0, 2, 1))], axis=0)
    b_blk, c_blk = b_blk.astype(BF16), c_blk.astype(BF16)
    ar8 = jnp.broadcast_to(ar.reshape(1, S5_GP), (SEG, S5_GP))
    ai8 = jnp.broadcast_to(ai.reshape(1, S5_GP), (SEG, S5_GP))
    pw_r, pw_i = _segment_powers(ar, ai, n // SEG)
    dskip = f('s5_d').reshape(1, S5_W)
    u_s5 = _permute_rows(za[:, :S5_W])
    bu = _mm(u_s5, b_blk, name="s5_bu")
    xs = _s5_scan(bu, ar8, ai8, pw_r, pw_i, "s5_scan")
    ys_p = _mm(xs, c_blk, res=_scale_rows(u_s5, dskip, "s5_skip"), name="s5_y")
    ys = _unpermute_rows(ys_p)
    zgelu = _gelu_fwd(ys, "s5_gelu")
    t_glu = _mm(zgelu, f('s5_glu_w'), bias=f('s5_glu_b').reshape(1, S5_W), name="s5_glu_t")
    y_s5 = _glu_fwd(zgelu, t_glu, "s5_glu")
    q, k = _heads(za[:, 512:768], GLA_DK), _heads(za[:, 768:1024], GLA_DK)
    v, r = _heads(za[:, 1024:1536], GLA_DV), _heads(za[:, 1536:2048], GLA_DV)
    wup = jnp.pad(f('gla_a_up_w'), ((0, LANE - GLA_RANK), (0, 0)))
    wup_h = wup.reshape(LANE, GLA_HEADS, GLA_DK).transpose(1, 0, 2)
    bup_h = f('gla_a_up_b').reshape(GLA_HEADS, 1, GLA_DK)
    gn_h = f('gla_out_norm').reshape(GLA_HEADS, 1, GLA_DV)
    y_gla_h, s_prev = carried("gla_fwd", _gla_fwd, q, k, v, r, al, wup_h, bup_h, gn_h, "gla_fwd")
    y_gla = _unheads(y_gla_h).astype(BF16)
    ps = _mm(y_s5, f('proj_s5'), name="proj_s5")
    pg = carried("proj_gla", _mm, y_gla, f('proj_gla'), name="proj_gla")
    merged = _merge_fwd(zg, ps, pg, "merge")
    h2 = _mm(merged, f('w_out'), res=h1, name="w_out")
    h3, ffn2 = _ffn_fwd(h2, f('ffn2_norm'), f('ffn2_w1'), f('ffn2_w3'), f('ffn2_w2'), "ffn2", plan)
    loss, dh3, g_final = _final_loss(h3, f('final_norm').reshape(1, D_MODEL), tgt, "loss")
    grads['final_norm'] = g_final.reshape(D_MODEL)
    dh2, grads['ffn2_norm'], grads['ffn2_w1'], grads['ffn2_w3'], grads['ffn2_w2'] = _ffn_bwd(
        dh3, ffn2, f('ffn2_norm'), f('ffn2_w1'), f('ffn2_w3'), f('ffn2_w2'), "ffn2", plan)
    dm = _mm(dh2, f('w_out'), tb=True, name="d_merged")
    grads['w_out'] = _mm(merged, dh2, ta=True, name="g_w_out")
    dps, dpg, dzg = carried("d_merge", _merge_bwd, dm, zg, ps, pg, "d_merge")
    grads['proj_s5'] = _mm(y_s5, dps, ta=True, name="g_proj_s5")
    grads['proj_gla'] = _mm(y_gla, dpg, ta=True, name="g_proj_gla")
    dy_s5 = _mm(dps, f('proj_s5'), tb=True, name="d_y_s5")
    dy_gla = _mm(dpg, f('proj_gla'), tb=True, name="d_y_gla")
    dzgelu, dt_glu, g_glu_b = _glu_bwd1(dy_s5, zgelu, t_glu, "d_glu")
    grads['s5_glu_b'] = g_glu_b.reshape(S5_W)
    grads['s5_glu_w'] = _mm(zgelu, dt_glu, ta=True, name="g_glu_w")
    dzgelu = _mm(dt_glu, f('s5_glu_w'), tb=True, res=dzgelu, name="d_gelu")
    dys, du_skip, g_d = _glu_bwd2(_permute_rows(dzgelu), ys_p, u_s5, dskip, "d_s5_y")
    grads['s5_d'] = g_d.reshape(S5_G, S5_H)
    gx = _mm(dys, c_blk, tb=True, name="s5_gx")
    lam, da8 = _s5_scan_bwd(gx, xs, ar8, ai8, pw_r, pw_i, "s5_scan_bwd")
    g_c = _mm(dys, xs, ta=True, name="g_s5_c")
    grads['s5_c_re'] = _diag_blocks(g_c[:, :S5_GP], S5_H, S5_P)
    grads['s5_c_im'] = -_diag_blocks(g_c[:, S5_GP:], S5_H, S5_P)
    g_b = _mm(lam, u_s5, ta=True, name="g_s5_b")
    g_bbar_re = _diag_blocks(g_b[:S5_GP], S5_P, S5_H)
    g_bbar_im = _diag_blocks(g_b[S5_GP:], S5_P, S5_H)
    da = jnp.sum(da8, axis=0)
    g_ar, g_ai = da[:S5_GP].reshape(S5_G, S5_P), da[S5_GP:].reshape(S5_G, S5_P)
    _, disc_vjp = jax.vjp(_s5_discretize, f('s5_lambda_re'), f('s5_lambda_im'), f('s5_log_dt'), f('s5_b_re'), f('s5_b_im'))
    (grads['s5_lambda_re'], grads['s5_lambda_im'], grads['s5_log_dt'], grads['s5_b_re'],
     grads['s5_b_im']) = disc_vjp((g_ar, g_ai, g_bbar_re, g_bbar_im))
    du_s5 = _unpermute_rows(_mm(lam, b_blk, tb=True, res=du_skip, name="d_s5_u"))
    dq, dk, dv, dr, dz, dgn, dbup = carried("gla_bwd", _gla_bwd, q, k, v, r, al, wup_h, bup_h, gn_h, s_prev,
                                            _heads(dy_gla, GLA_DV), "gla_bwd")
    grads['gla_out_norm'] = dgn.reshape(GLA_HEADS * GLA_DV)
    grads['gla_a_up_b'] = dbup.reshape(GLA_HEADS * GLA_DK)
    dz = _unheads(dz)
    grads['gla_a_up_w'] = _mm(al, dz, ta=True, name="g_a_up")[:GLA_RANK]
    dal = _mm(dz, wup, tb=True, name="d_a_low")
    dza = jnp.concatenate([du_s5, _unheads(dq), _unheads(dk), _unheads(dv), _unheads(dr)], axis=1)
    g_wa = _mm(u, dza, ta=True, name="g_in_a")
    g_wg = _mm(u, dzg, ta=True, name="g_in_g")
    g_wal = _mm(u, dal, ta=True, name="g_in_al")
    grads['w_in'] = jnp.concatenate([g_wa, g_wal[:, :GLA_RANK], g_wg], axis=1)
    du = carried("d_u_a", _mm, dza, w_a, tb=True, name="d_u_a")
    du = _mm(dzg, w_g, tb=True, res=du, name="d_u_g")
    du = _mm(dal, w_al, tb=True, res=du, name="d_u_al")
    dh1, g_mix = carried("d_mix_rms", _rms_bwd, h1, f('mix_norm'), du, dh2, "d_mix_rms")
    grads['mix_norm'] = g_mix
    dx, grads['ffn1_norm'], grads['ffn1_w1'], grads['ffn1_w3'], grads['ffn1_w2'] = _ffn_bwd(
        dh1, ffn1, f('ffn1_norm'), f('ffn1_w1'), f('ffn1_w3'), f('ffn1_w2'), "ffn1", plan)
    return loss[0, 0], dx


MIXER_WEIGHTS = ['w_in', 's5_glu_w', 'proj_s5', 'proj_gla', 'w_out', 'gla_a_up_w']
FFN1_WEIGHTS, FFN2_WEIGHTS = FFN_WEIGHTS[:3], FFN_WEIGHTS[3:]
TRANSPOSED = ['ffn1_w1', 'ffn1_w3', 'ffn2_w1', 'ffn2_w3']


def _local_shard(w, nm):
    return jnp.swapaxes(w, 1, 2)[0] if nm in TRANSPOSED else w[0]
GRAD_GROUPS = {'ffn2': FFN2_WEIGHTS, 'mixer': ['w_out', 'proj_s5', 'proj_gla', 's5_glu_w', 'w_in'], 'ffn1': FFN1_WEIGHTS}


class _Plan:
    def __init__(self, a, c_arr, s_arr):
        self.a, self.c_arr, self.s_arr = a, c_arr, s_arr
        self.grads, self.weights, self.riding = {}, {}, {}
        self.g4s, self.chip_sums, self.halves, self.sib_halves = {}, {}, {}, {}
        for nm in SMALL:
            if nm != 'gla_a_up_w':
                self.weights[nm] = a[nm] if nm == 'final_norm' else a[nm][0]
        ici = _side_gather_ici(self._shards(FFN1_WEIGHTS))
        _run_side(ici, "gather_ffn1_ici")
        self._gathered(FFN1_WEIGHTS, _run_side(_side_gather_d2d(ici.outs), "gather_ffn1_d2d"))

    def _shards(self, names):
        return [_local_shard(self.a[nm], nm).astype(F32 if nm == 'gla_a_up_w' else BF16) for nm in names]

    def _gathered(self, names, arrs):
        for nm, g4 in zip(names, arrs):
            if nm in FFN_WEIGHTS:
                self.weights[nm] = g4
            elif nm in COL_SHARDED:
                self.weights[nm] = jnp.concatenate([g4[s] for s in range(4)], axis=1)
            else:
                self.weights[nm] = g4.reshape(4 * g4.shape[1], g4.shape[2])

    def get(self, name):
        return self.weights[name]

    def _shard_major(self, nm):
        g = self.grads[nm]
        if nm in FFN_WEIGHTS:
            return g
        if nm in COL_SHARDED:
            return jnp.stack(jnp.split(g, 4, axis=1))
        return g.reshape(4, g.shape[0] // 4, g.shape[1])

    def _schedule(self, tag):
        grp = GRAD_GROUPS
        if tag == "ffn1_fwd":
            return _side_gather_ici(self._shards(MIXER_WEIGHTS)), lambda outs: self.riding.update(mixer_ici=outs)
        if tag == "mix_rms":
            return _side_gather_d2d(self.riding['mixer_ici']), lambda outs: self._gathered(MIXER_WEIGHTS, outs)
        if tag == "gla_fwd":
            return _side_gather_ici(self._shards(FFN2_WEIGHTS)), lambda outs: self.riding.update(ffn2_ici=outs)
        if tag == "proj_gla":
            return _side_gather_d2d(self.riding['ffn2_ici']), lambda outs: self._gathered(FFN2_WEIGHTS, outs)
        steps = {"d_merge": ('ffn2', 0), "gla_bwd": ('ffn2', 1), "d_mix_rms": ('ffn2', 2),
                 "d_u_a": ('mixer', 0), "ffn1_bwd": ('mixer', 1), "ffn1_gw2": ('mixer', 2)}
        if tag in steps:
            group, stage = steps[tag]
            return self._reduce_stage(grp[group], stage)
        return None

    def _reduce_stage(self, names, stage):
        if stage == 0:
            for nm in names:
                self.g4s[nm] = self._shard_major(nm)

            def done(outs):
                for nm, r in zip(names, outs):
                    self.chip_sums[nm] = _chip_sum(self.g4s[nm], r, self.c_arr, f"chip_sum_{nm}")
            return _side_swap_halves([self.g4s[nm] for nm in names]), done
        if stage == 1:
            def done(outs):
                for nm, o in zip(names, outs):
                    self.halves[nm] = _owner_sum(self.chip_sums[nm], o, self.s_arr, f"owner_sum_{nm}")
            return _side_scatter([self.chip_sums[nm] for nm in names]), done

        def done(outs):
            self.sib_halves.update(zip(names, outs))
        return _side_swap_reduced([self.halves[nm] for nm in names]), done

    def before(self, tag):
        entry = self._schedule(tag)
        if entry is not None:
            side, done = entry
            self.riding[tag] = (side, done)
            _RIDER.append(side)

    def after(self, tag):
        if tag in self.riding:
            side, done = self.riding.pop(tag)
            assert not _RIDER and side.outs is not None, tag
            done(side.outs)

    def finish(self):
        names = GRAD_GROUPS['ffn1']
        for stage in range(3):
            side, done = self._reduce_stage(names, stage)
            done(_run_side(side, f"grad_ffn1_stage{stage}"))


def _train_step(a):
    x = a['x'][0]
    tgt = a['loss_target'][0]
    xi, yi, ci = lax.axis_index("x"), lax.axis_index("y"), lax.axis_index("c")
    c_arr = jnp.reshape(ci, (1,)).astype(jnp.int32)
    s_arr = jnp.reshape(2 * xi + yi, (1,)).astype(jnp.int32)
    plan = _Plan(a, c_arr, s_arr)
    loss, dx = _local_step(x, tgt, plan)
    plan.finish()
    grads = plan.grads
    loss = lax.psum(loss, ("x", "y", "c"))
    halves = [plan.halves[nm] for nm in SHARDED]
    sib_halves = [plan.sib_halves[nm] for nm in SHARDED]
    red = {}
    small_parts = [grads[nm].reshape(a[nm].shape) for nm in SMALL if nm != 'gla_a_up_w'] + [grads['gla_a_up_w']]
    small_sum = _unpack_small(_allreduce_small(_pack_small(small_parts), "allreduce_small"), small_parts)
    small_names = [nm for nm in SMALL if nm != 'gla_a_up_w']
    for nm, g in zip(small_names, small_sum[:-1]):
        red[nm] = g
    g_up = small_sum[-1]
    red['gla_a_up_w'] = lax.dynamic_slice(g_up, (0, (2 * xi + yi) * GLA_DK), (GLA_RANK, GLA_DK))
    out_g, out_d, out_m, out_v = {}, {}, {}, {}
    for nm, own, sib in zip(SHARDED, halves, sib_halves):
        loc = lambda pre: _local_shard(a[pre + nm], nm)
        res = _adamw_halves(loc(''), own, sib, loc('m_'), loc('v_'), c_arr, f"adamw_{nm}")
        back = (lambda t: jnp.swapaxes(t[None], 1, 2)) if nm in TRANSPOSED else (lambda t: t[None])
        out_g[nm], out_d[nm], out_m[nm], out_v[nm] = (back(t) for t in res)
    rest = [nm for nm in WEIGHTS if nm not in SHARDED]
    pk = lambda pre: _pack_small([a[pre + nm] for nm in rest])
    d, nm_, nv_ = _adamw(pk(''), _pack_small([red[nm] for nm in rest]), pk('m_'), pk('v_'), "adamw_small")
    like = [a[nm] for nm in rest]
    for nm, g, dd, mm_, vv_ in zip(rest, [red[nm].reshape(a[nm].shape) for nm in rest], _unpack_small(d, like),
                                   _unpack_small(nm_, like), _unpack_small(nv_, like)):
        out_g[nm], out_d[nm], out_m[nm], out_v[nm] = g, dd, mm_, vv_
    return (loss, dx[None], *[out_g[nm] for nm in WEIGHTS], *[out_d[nm] for nm in WEIGHTS],
            *[out_m[nm] for nm in WEIGHTS], *[out_v[nm] for nm in WEIGHTS])


def kernel(x, ffn1_norm, ffn1_w1, ffn1_w3, ffn1_w2, mix_norm, w_in, s5_lambda_re, s5_lambda_im, s5_log_dt, s5_b_re, s5_b_im, s5_c_re, s5_c_im, s5_d, s5_glu_w, s5_glu_b, gla_a_up_w, gla_a_up_b, gla_out_norm, proj_s5, proj_gla, w_out, ffn2_norm, ffn2_w1, ffn2_w3, ffn2_w2, final_norm, loss_target, m_ffn1_norm, m_ffn1_w1, m_ffn1_w3, m_ffn1_w2, m_mix_norm, m_w_in, m_s5_lambda_re, m_s5_lambda_im, m_s5_log_dt, m_s5_b_re, m_s5_b_im, m_s5_c_re, m_s5_c_im, m_s5_d, m_s5_glu_w, m_s5_glu_b, m_gla_a_up_w, m_gla_a_up_b, m_gla_out_norm, m_proj_s5, m_proj_gla, m_w_out, m_ffn2_norm, m_ffn2_w1, m_ffn2_w3, m_ffn2_w2, m_final_norm, v_ffn1_norm, v_ffn1_w1, v_ffn1_w3, v_ffn1_w2, v_mix_norm, v_w_in, v_s5_lambda_re, v_s5_lambda_im, v_s5_log_dt, v_s5_b_re, v_s5_b_im, v_s5_c_re, v_s5_c_im, v_s5_d, v_s5_glu_w, v_s5_glu_b, v_gla_a_up_w, v_gla_a_up_b, v_gla_out_norm, v_proj_s5, v_proj_gla, v_w_out, v_ffn2_norm, v_ffn2_w1, v_ffn2_w3, v_ffn2_w2, v_final_norm):
    return _train_step(dict(locals()))
```

```python
import functools

import jax
import jax.numpy as jnp
from jax import lax
from jax.experimental import pallas as pl
from jax.experimental.pallas import tpu as pltpu

F32 = jnp.float32
BF16 = jnp.bfloat16
HI = lax.Precision.HIGHEST
MESH_ID = pl.DeviceIdType.MESH

D_MODEL = 1024
EPS = 1e-6
S5_G, S5_P, S5_H = 32, 64, 16
S5_W = S5_G * S5_H
S5_GP = S5_G * S5_P
SEG = 8
SCAN_ROWS = 256
GLA_HEADS, GLA_DK, GLA_DV = 4, 64, 128
GLA_CHUNK = 64
GLA_TAU = 16.0
GLA_RANK = 16
ADAM_LR, ADAM_B1, ADAM_B2, ADAM_EPS, ADAM_WD, ADAM_STEP = 0.001, 0.9, 0.999, 1e-08, 0.01, 10
V7X_VMEM_LIMIT = 56 * 1024 * 1024
LANE = 128

WEIGHTS = ['ffn1_norm', 'ffn1_w1', 'ffn1_w3', 'ffn1_w2', 'mix_norm', 'w_in', 's5_lambda_re', 's5_lambda_im',
           's5_log_dt', 's5_b_re', 's5_b_im', 's5_c_re', 's5_c_im', 's5_d', 's5_glu_w', 's5_glu_b', 'gla_a_up_w',
           'gla_a_up_b', 'gla_out_norm', 'proj_s5', 'proj_gla', 'w_out', 'ffn2_norm', 'ffn2_w1', 'ffn2_w3',
           'ffn2_w2', 'final_norm']
SHARDED = ['ffn1_w1', 'ffn1_w3', 'ffn1_w2', 'w_in', 's5_glu_w', 'proj_s5', 'proj_gla', 'w_out',
           'ffn2_w1', 'ffn2_w3', 'ffn2_w2']
COL_SHARDED = ['ffn1_w1', 'ffn1_w3', 'w_in', 'proj_s5', 'proj_gla', 'ffn2_w1', 'ffn2_w3', 'gla_a_up_w']
SMALL = [n for n in WEIGHTS if n not in SHARDED]
FFN_WEIGHTS = ['ffn1_w1', 'ffn1_w3', 'ffn1_w2', 'ffn2_w1', 'ffn2_w3', 'ffn2_w2']


def _params(**kw):
    return pltpu.CompilerParams(vmem_limit_bytes=V7X_VMEM_LIMIT, **kw)


class _Side:
    def __init__(self, ins, out_shapes, nsem, copies, aliased=False):
        self.ins, self.out_shapes, self.nsem, self.copies, self.aliased = list(ins), list(out_shapes), nsem, copies, aliased
        self.outs = None


_RIDER = []


def _pcall(body, **kw):
    if _RIDER:
        return _carry(body, _RIDER.pop(), **kw)
    return pl.pallas_call(body, **kw)


def _carry(body, side, *, name, grid, in_specs, out_specs, out_shape, scratch_shapes=(), compiler_params=None):
    del compiler_params
    single = not isinstance(out_shape, (list, tuple))
    out_specs = [out_specs] if single else list(out_specs)
    out_shape = [out_shape] if single else list(out_shape)
    n_in, n_out, n_scr = len(in_specs), len(out_shape), len(scratch_shapes)
    s_in, s_out = len(side.ins), len(side.out_shapes)
    any_spec = pl.BlockSpec(memory_space=pl.ANY)

    def wrapped(*refs):
        cuts = [n_in, s_in, n_out, s_out, n_scr]
        parts, pos = [], 0
        for c in cuts:
            parts.append(refs[pos:pos + c])
            pos += c
        ins, sins, outs, souts, scr = parts
        ssem, rsem = refs[pos], refs[pos + 1]
        first = last = None
        for d, g in enumerate(grid):
            i = pl.program_id(d)
            first = (i == 0) if first is None else first & (i == 0)
            last = (i == g - 1) if last is None else last & (i == g - 1)

        @pl.when(first)
        def _():
            for cp in side.copies(sins, souts, ssem, rsem):
                cp.start()

        body(*ins, *outs, *scr)

        @pl.when(last)
        def _():
            for cp in side.copies(sins, souts, ssem, rsem):
                cp.wait()

    call = pl.pallas_call(
        wrapped, name=name, grid=grid, in_specs=list(in_specs) + [any_spec] * s_in,
        out_specs=out_specs + [any_spec] * s_out, out_shape=out_shape + side.out_shapes,
        scratch_shapes=list(scratch_shapes) + [pltpu.SemaphoreType.DMA((side.nsem,)), pltpu.SemaphoreType.DMA((side.nsem,))],
        input_output_aliases={n_in + j: n_out + j for j in range(s_in)} if side.aliased else {},
        compiler_params=_params(has_side_effects=True))

    def run(*args):
        res = call(*args, *side.ins)
        side.outs = list(res[n_out:])
        return res[0] if single else list(res[:n_out])

    return run


def _run_side(side, name):
    s_in, s_out = len(side.ins), len(side.out_shapes)
    any_spec = pl.BlockSpec(memory_space=pl.ANY)

    def body(*refs):
        sins, souts = refs[:s_in], refs[s_in:s_in + s_out]
        ssem, rsem = refs[s_in + s_out:]
        cps = side.copies(sins, souts, ssem, rsem)
        for cp in cps:
            cp.start()
        for cp in cps:
            cp.wait()

    side.outs = list(pl.pallas_call(
        body, name=name, in_specs=[any_spec] * s_in, out_specs=[any_spec] * s_out, out_shape=side.out_shapes,
        scratch_shapes=[pltpu.SemaphoreType.DMA((side.nsem,)), pltpu.SemaphoreType.DMA((side.nsem,))],
        input_output_aliases={j: j for j in range(s_in)} if side.aliased else {},
        compiler_params=pltpu.CompilerParams(has_side_effects=True))(*side.ins))
    return side.outs


def _pick(n, cap, quantum):
    if n <= cap:
        return n
    best = None
    for t in range(quantum, cap + 1, quantum):
        if n % t == 0:
            best = t
    assert best is not None, (n, cap, quantum)
    return best


def _sigmoid(x):
    return jax.nn.sigmoid(x)


def _mm(a, b, *, name, ta=False, tb=False, out_dtype=F32, alpha=1.0, res=None, bias=None, exact=False, shard=None):
    ns = 4
    (k_a, m) = a.shape[-2:] if ta else a.shape[-2:][::-1]
    (k_b, n) = b.shape[-2:][::-1] if tb else b.shape[-2:]
    assert k_a == k_b, (a.shape, b.shape, ta, tb)
    assert (a.ndim == 3) == (shard in ('k', 'm')) and (b.ndim == 3) == (shard in ('n', 'k'))
    k = k_a
    tm = _pick(m, 1024, 128)
    tn = _pick(n, 1024, 128)
    tk = _pick(k, 1024, 128)
    pm, pn, pk = m // tm, n // tn, k // tk
    gm = pm * (ns if shard == 'm' else 1)
    gn = pn * (ns if shard == 'n' else 1)
    gk = pk * (ns if shard == 'k' else 1)
    dims = (((0,) if ta else (1,), (1,) if tb else (0,)), ((), ()))
    op_dtype = F32 if exact else BF16

    def body(*refs):
        a_ref, b_ref = refs[0], refs[1]
        pos = 2
        res_ref = bias_ref = None
        if res is not None:
            res_ref = refs[pos]
            pos += 1
        if bias is not None:
            bias_ref = refs[pos]
            pos += 1
        o_ref, acc_ref = refs[pos], refs[pos + 1]
        kk = pl.program_id(2)

        @pl.when(kk == 0)
        def _():
            acc_ref[...] = jnp.zeros_like(acc_ref)

        acc_ref[...] += lax.dot_general(a_ref[...].astype(op_dtype), b_ref[...].astype(op_dtype), dims,
                                        precision=HI if exact else None, preferred_element_type=F32)

        @pl.when(kk == gk - 1)
        def _():
            o = acc_ref[...]
            if alpha != 1.0:
                o = o * alpha
            if bias_ref is not None:
                o = o + bias_ref[...]
            if res_ref is not None:
                o = o + res_ref[...]
            o_ref[...] = o.astype(out_dtype)

    def spec(block, sharded_on, order):
        per = {'m': pm, 'n': pn, 'k': pk}

        def index(i, j, kk):
            g = {'m': i, 'n': j, 'k': kk}
            r, c = order(i % pm if shard == 'm' else i, j % pn if shard == 'n' else j, kk % pk if shard == 'k' else kk)
            if sharded_on is None:
                return (r, c)
            return (g[sharded_on] // per[sharded_on], r, c)

        return pl.BlockSpec(block if sharded_on is None else (None,) + block, index)

    a_sh = shard if shard in ('k', 'm') else None
    b_sh = shard if shard in ('n', 'k') else None
    o_sh = shard if shard in ('n', 'm') else None
    a_spec = spec((tk, tm), a_sh, lambda i, j, kk: (kk, i)) if ta else spec((tm, tk), a_sh, lambda i, j, kk: (i, kk))
    b_spec = spec((tn, tk), b_sh, lambda i, j, kk: (j, kk)) if tb else spec((tk, tn), b_sh, lambda i, j, kk: (kk, j))
    ins, in_specs = [a, b], [a_spec, b_spec]
    if res is not None:
        assert o_sh is None
        ins.append(res)
        in_specs.append(pl.BlockSpec((tm, tn), lambda i, j, kk: (i, j)))
    if bias is not None:
        assert o_sh is None
        ins.append(bias)
        in_specs.append(pl.BlockSpec((1, tn), lambda i, j, kk: (0, j)))
    out_shape = (m, n) if o_sh is None else (ns, m, n)
    return _pcall(body, name=name, grid=(gm, gn, gk), in_specs=in_specs,
                  out_specs=spec((tm, tn), o_sh, lambda i, j, kk: (i, j)),
                  out_shape=jax.ShapeDtypeStruct(out_shape, out_dtype),
                  scratch_shapes=[pltpu.VMEM((tm, tn), F32)], compiler_params=_params())(*ins)


def _rows(body, ins, outs, *, n, name, tm=256):
    tm = _pick(n, tm, 16)
    in_specs = []
    for arr, kind in ins:
        if kind == 'r':
            in_specs.append(pl.BlockSpec((tm, arr.shape[1]), lambda i: (i, 0)))
        else:
            in_specs.append(pl.BlockSpec(arr.shape, lambda i: (0, 0)))
    out_specs, out_shape = [], []
    for cols, dtype, kind in outs:
        if kind == 'r':
            out_specs.append(pl.BlockSpec((tm, cols), lambda i: (i, 0)))
            out_shape.append(jax.ShapeDtypeStruct((n, cols), dtype))
        else:
            out_specs.append(pl.BlockSpec((1, cols), lambda i: (0, 0)))
            out_shape.append(jax.ShapeDtypeStruct((1, cols), dtype))
    n_in = len(ins)
    acc_ids = [j for j, o in enumerate(outs) if o[2] == 'a']

    def wrapped(*refs):
        if acc_ids:
            @pl.when(pl.program_id(0) == 0)
            def _():
                for j in acc_ids:
                    refs[n_in + j][...] = jnp.zeros_like(refs[n_in + j])
        body(*refs)

    res = _pcall(wrapped, name=name, grid=(n // tm,), in_specs=in_specs, out_specs=out_specs, out_shape=out_shape,
                 compiler_params=_params())(*[a for a, _ in ins])
    return res


def _rms_fwd(x, g, name):
    def body(x_ref, g_ref, o_ref):
        xv = x_ref[...]
        rstd = lax.rsqrt(jnp.mean(xv * xv, axis=-1, keepdims=True) + EPS)
        o_ref[...] = (xv * rstd * g_ref[...]).astype(BF16)
    return _rows(body, [(x, 'r'), (g, 'f')], [(x.shape[1], BF16, 'r')], n=x.shape[0], name=name)[0]


def _rms_bwd(x, g, dn, dres, name):
    def body(x_ref, g_ref, dn_ref, dres_ref, dx_ref, dg_ref):
        xv = x_ref[...]
        rstd = lax.rsqrt(jnp.mean(xv * xv, axis=-1, keepdims=True) + EPS)
        xh = xv * rstd
        dn = dn_ref[...]
        dg_ref[...] += jnp.sum(dn * xh, axis=0, keepdims=True)
        dxh = dn * g_ref[...]
        dx_ref[...] = dres_ref[...] + rstd * (dxh - xh * jnp.mean(dxh * xh, axis=-1, keepdims=True))
    d = x.shape[1]
    return _rows(body, [(x, 'r'), (g, 'f'), (dn, 'r'), (dres, 'r')], [(d, F32, 'r'), (d, F32, 'a')],
                 n=x.shape[0], name=name)


def _gelu_parts(y):
    c0 = 0.7978845608028654
    inner = c0 * (y + 0.044715 * y * y * y)
    th = jnp.tanh(inner)
    return th, c0 * (1.0 + 3.0 * 0.044715 * y * y)


def _gelu_fwd(y, name):
    def body(y_ref, o_ref):
        yv = y_ref[...]
        th, _ = _gelu_parts(yv)
        o_ref[...] = 0.5 * yv * (1.0 + th)
    return _rows(body, [(y, 'r')], [(y.shape[1], F32, 'r')], n=y.shape[0], name=name)[0]


def _glu_fwd(zg, t, name):
    def body(z_ref, t_ref, o_ref):
        o_ref[...] = (z_ref[...] * _sigmoid(t_ref[...])).astype(BF16)
    return _rows(body, [(zg, 'r'), (t, 'r')], [(zg.shape[1], BF16, 'r')], n=zg.shape[0], name=name)[0]


def _glu_bwd1(dy, zg, t, name):
    def body(dy_ref, z_ref, t_ref, dz_ref, dt_ref, db_ref):
        dyv, zv = dy_ref[...], z_ref[...]
        sg = _sigmoid(t_ref[...])
        dz_ref[...] = dyv * sg
        dt = dyv * zv * sg * (1.0 - sg)
        dt_ref[...] = dt.astype(BF16)
        db_ref[...] += jnp.sum(dt, axis=0, keepdims=True)
    w = zg.shape[1]
    return _rows(body, [(dy, 'r'), (zg, 'r'), (t, 'r')], [(w, F32, 'r'), (w, BF16, 'r'), (w, F32, 'a')],
                 n=zg.shape[0], name=name)


def _glu_bwd2(dzg, ys, u, dskip, name):
    def body(dz_ref, y_ref, u_ref, d_ref, dy_ref, du_ref, dd_ref):
        yv = y_ref[...]
        th, dinner = _gelu_parts(yv)
        dy = dz_ref[...] * (0.5 * (1.0 + th) + 0.5 * yv * (1.0 - th * th) * dinner)
        dy_ref[...] = dy
        du_ref[...] = dy * d_ref[...]
        dd_ref[...] += jnp.sum(dy * u_ref[...], axis=0, keepdims=True)
    w = ys.shape[1]
    return _rows(body, [(dzg, 'r'), (ys, 'r'), (u, 'r'), (dskip, 'f')], [(w, F32, 'r'), (w, F32, 'r'), (w, F32, 'a')],
                 n=ys.shape[0], name=name)


def _scale_rows(u, dskip, name):
    def body(u_ref, d_ref, o_ref):
        o_ref[...] = u_ref[...] * d_ref[...]
    return _rows(body, [(u, 'r'), (dskip, 'f')], [(u.shape[1], F32, 'r')], n=u.shape[0], name=name)[0]


def _merge_fwd(zg, ps, pg, name):
    def body(z_ref, ps_ref, pg_ref, o_ref):
        zv = z_ref[...]
        o_ref[...] = (_sigmoid(zv[:, :D_MODEL]) * ps_ref[...] + _sigmoid(zv[:, D_MODEL:]) * pg_ref[...]).astype(BF16)
    return _rows(body, [(zg, 'r'), (ps, 'r'), (pg, 'r')], [(D_MODEL, BF16, 'r')], n=zg.shape[0], name=name)[0]


def _merge_bwd(dm, zg, ps, pg, name):
    def body(dm_ref, z_ref, ps_ref, pg_ref, dps_ref, dpg_ref, dz_ref):
        dmv, zv = dm_ref[...], z_ref[...]
        s1, s2 = _sigmoid(zv[:, :D_MODEL]), _sigmoid(zv[:, D_MODEL:])
        dps_ref[...] = (dmv * s1).astype(BF16)
        dpg_ref[...] = (dmv * s2).astype(BF16)
        dz_ref[:, :D_MODEL] = dmv * ps_ref[...] * s1 * (1.0 - s1)
        dz_ref[:, D_MODEL:] = dmv * pg_ref[...] * s2 * (1.0 - s2)
    return _rows(body, [(dm, 'r'), (zg, 'r'), (ps, 'r'), (pg, 'r')],
                 [(D_MODEL, BF16, 'r'), (D_MODEL, BF16, 'r'), (2 * D_MODEL, F32, 'r')], n=zg.shape[0], name=name)


def _final_loss(h, g, tgt, name):
    def body(h_ref, g_ref, t_ref, loss_ref, dh_ref, dg_ref):
        hv = h_ref[...]
        rstd = lax.rsqrt(jnp.mean(hv * hv, axis=-1, keepdims=True) + EPS)
        xh = hv * rstd
        err = xh * g_ref[...] - t_ref[...]
        part = 0.5 * jnp.sum(jnp.mean(err * err, axis=-1, keepdims=True), axis=0, keepdims=True)
        loss_ref[...] += jnp.broadcast_to(part, loss_ref.shape)
        dout = err * (1.0 / hv.shape[1])
        dg_ref[...] += jnp.sum(dout * xh, axis=0, keepdims=True)
        dxh = dout * g_ref[...]
        dh_ref[...] = rstd * (dxh - xh * jnp.mean(dxh * xh, axis=-1, keepdims=True))
    d = h.shape[1]
    return _rows(body, [(h, 'r'), (g, 'f'), (tgt, 'r')], [(LANE, F32, 'a'), (d, F32, 'r'), (d, F32, 'a')],
                 n=h.shape[0], name=name)


def _adamw_math(wv, gv, mv, vv):
    nm = ADAM_B1 * mv + (1.0 - ADAM_B1) * gv
    nv = ADAM_B2 * vv + (1.0 - ADAM_B2) * (gv * gv)
    m_hat = nm / (1.0 - ADAM_B1 ** ADAM_STEP)
    v_hat = nv / (1.0 - ADAM_B2 ** ADAM_STEP)
    return -ADAM_LR * (m_hat / (jnp.sqrt(v_hat) + ADAM_EPS) + ADAM_WD * wv), nm, nv


def _adamw(w, g, m, v, name):
    def body(w_ref, g_ref, m_ref, v_ref, d_ref, nm_ref, nv_ref):
        d_ref[...], nm_ref[...], nv_ref[...] = _adamw_math(w_ref[...], g_ref[...], m_ref[...], v_ref[...])
    c = w.shape[1]
    return _rows(body, [(w, 'r'), (g, 'r'), (m, 'r'), (v, 'r')], [(c, F32, 'r')] * 3, n=w.shape[0], name=name)


def _adamw_halves(w, g_own, g_sib, m, v, c_arr, name):
    r, cols = w.shape
    h = r // 2
    tr = _pick(h, 256, 8)
    per = h // tr

    def body(c_ref, w_ref, go_ref, gs_ref, m_ref, v_ref, g_ref, d_ref, nm_ref, nv_ref):
        mine = (pl.program_id(0) // per) == c_ref[0]
        gv = jnp.where(mine, go_ref[...], gs_ref[...])
        g_ref[...] = gv
        d_ref[...], nm_ref[...], nv_ref[...] = _adamw_math(w_ref[...], gv, m_ref[...], v_ref[...])

    full = pl.BlockSpec((tr, cols), lambda i, c_ref: (i, 0))
    half = pl.BlockSpec((tr, cols), lambda i, c_ref: (i % per, 0))
    grid_spec = pltpu.PrefetchScalarGridSpec(num_scalar_prefetch=1, grid=(2 * per,),
                                             in_specs=[full, half, half, full, full], out_specs=[full] * 4)
    return _pcall(body, name=name, grid_spec=grid_spec, out_shape=[jax.ShapeDtypeStruct((r, cols), F32)] * 4,
                  compiler_params=_params())(c_arr, w, g_own, g_sib, m, v)


def _shift_rows(v, sh, down):
    rolled = pltpu.roll(v, sh if down else v.shape[0] - sh, axis=0)
    row = lax.broadcasted_iota(jnp.int32, v.shape, 0)
    keep = (row >= sh) if down else (row < v.shape[0] - sh)
    return jnp.where(keep, rolled, 0.0)


def _chain_segments(st_r, st_i, pw_r_ref, pw_i_ref, conj, down):
    vr, vi = st_r[...], st_i[...]
    sh, k = 1, 0
    while sh < SEG:
        pr, pi = pw_r_ref[k:k + 1, :], pw_i_ref[k:k + 1, :]
        if conj:
            pi = -pi
        sr, si = _shift_rows(vr, sh, down), _shift_rows(vi, sh, down)
        vr, vi = vr + pr * sr - pi * si, vi + pr * si + pi * sr
        sh, k = sh * 2, k + 1
    st_r[...] = _shift_rows(vr, 1, down)
    st_i[...] = _shift_rows(vi, 1, down)


def _s5_scan(bu, ar8, ai8, pw_r, pw_i, name):
    n = bu.shape[0]
    rb = SCAN_ROWS
    nb, steps, lc = n // rb, rb // SEG, 512

    def body(bu_ref, ar_ref, ai_ref, pwr_ref, pwi_ref, x_ref, st_r, st_i):
        ph, b = pl.program_id(0), pl.program_id(1)

        @pl.when((ph == 0) & (b == 0))
        def _():
            st_r[...] = jnp.zeros_like(st_r)
            st_i[...] = jnp.zeros_like(st_i)

        def scan(store):
            for c in range(S5_GP // lc):
                re, im = slice(c * lc, (c + 1) * lc), slice(S5_GP + c * lc, S5_GP + (c + 1) * lc)
                a_r, a_i = ar_ref[:, re], ai_ref[:, re]

                def step(s, carry):
                    xr, xi = carry
                    rows = pl.ds(pl.multiple_of(s * SEG, SEG), SEG)
                    nr = a_r * xr - a_i * xi + bu_ref[rows, re]
                    ni = a_r * xi + a_i * xr + bu_ref[rows, im]
                    if store:
                        x_ref[rows, re] = nr
                        x_ref[rows, im] = ni
                    return nr, ni

                xr, xi = lax.fori_loop(0, steps, step, (st_r[:, re], st_i[:, re]), unroll=4)
                st_r[:, re] = xr
                st_i[:, re] = xi

        @pl.when(ph == 0)
        def _():
            scan(False)

        @pl.when((ph == 0) & (b == nb - 1))
        def _():
            _chain_segments(st_r, st_i, pwr_ref, pwi_ref, conj=False, down=True)

        @pl.when(ph == 1)
        def _():
            scan(True)

    full = lambda a: pl.BlockSpec(a.shape, lambda ph, b: (0, 0))
    return _pcall(body, name=name, grid=(2, nb),
                  in_specs=[pl.BlockSpec((rb, 2 * S5_GP), lambda ph, b: (b, 0)), full(ar8), full(ai8), full(pw_r), full(pw_i)],
                  out_specs=pl.BlockSpec((rb, 2 * S5_GP), lambda ph, b: (b * ph, 0)),
                  out_shape=jax.ShapeDtypeStruct((n, 2 * S5_GP), F32),
                  scratch_shapes=[pltpu.VMEM((SEG, S5_GP), F32), pltpu.VMEM((SEG, S5_GP), F32)],
                  compiler_params=_params())(bu, ar8, ai8, pw_r, pw_i)


def _s5_scan_bwd(gx, xs, ar8, ai8, pw_r, pw_i, name):
    n = gx.shape[0]
    rb = SCAN_ROWS
    nb, steps, lc = n // rb, rb // SEG, 256

    def body(gx_ref, x_ref, ar_ref, ai_ref, pwr_ref, pwi_ref, lam_ref, da_ref, st_r, st_i):
        ph, b = pl.program_id(0), pl.program_id(1)

        @pl.when((ph == 0) & (b == 0))
        def _():
            st_r[...] = jnp.zeros_like(st_r)
            st_i[...] = jnp.zeros_like(st_i)
            da_ref[...] = jnp.zeros_like(da_ref)

        def scan(store):
            for c in range(S5_GP // lc):
                re, im = slice(c * lc, (c + 1) * lc), slice(S5_GP + c * lc, S5_GP + (c + 1) * lc)
                a_r, a_i = ar_ref[:, re], ai_ref[:, re]

                def step(s, carry):
                    rows = pl.ds(pl.multiple_of((steps - 1 - s) * SEG, SEG), SEG)
                    if store:
                        lr, li, dr, di = carry
                        xr, xi = x_ref[rows, re], x_ref[rows, im]
                        dr = dr + lr * xr + li * xi
                        di = di + li * xr - lr * xi
                    else:
                        lr, li = carry
                    nr = a_r * lr + a_i * li + gx_ref[rows, re]
                    ni = a_r * li - a_i * lr + gx_ref[rows, im]
                    if store:
                        lam_ref[rows, re] = nr
                        lam_ref[rows, im] = ni
                        return nr, ni, dr, di
                    return nr, ni

                if store:
                    lr, li, dr, di = lax.fori_loop(0, steps, step, (st_r[:, re], st_i[:, re], da_ref[:, re], da_ref[:, im]),
                                                   unroll=4)
                    da_ref[:, re] = dr
                    da_ref[:, im] = di
                else:
                    lr, li = lax.fori_loop(0, steps, step, (st_r[:, re], st_i[:, re]), unroll=4)
                st_r[:, re] = lr
                st_i[:, re] = li

        @pl.when(ph == 0)
        def _():
            scan(False)

        @pl.when((ph == 0) & (b == nb - 1))
        def _():
            _chain_segments(st_r, st_i, pwr_ref, pwi_ref, conj=True, down=False)

        @pl.when(ph == 1)
        def _():
            scan(True)

    full = lambda a: pl.BlockSpec(a.shape, lambda ph, b: (0, 0))
    rev = lambda ph, b: (nb - 1 - b, 0)
    return _pcall(body, name=name, grid=(2, nb),
                  in_specs=[pl.BlockSpec((rb, 2 * S5_GP), rev), pl.BlockSpec((rb, 2 * S5_GP), lambda ph, b: ((nb - 1 - b) * ph, 0)),
                            full(ar8), full(ai8), full(pw_r), full(pw_i)],
                  out_specs=[pl.BlockSpec((rb, 2 * S5_GP), lambda ph, b: (nb - 1 - b * ph, 0)),
                             pl.BlockSpec((SEG, 2 * S5_GP), lambda ph, b: (0, 0))],
                  out_shape=[jax.ShapeDtypeStruct((n, 2 * S5_GP), F32), jax.ShapeDtypeStruct((SEG, 2 * S5_GP), F32)],
                  scratch_shapes=[pltpu.VMEM((SEG, S5_GP), F32), pltpu.VMEM((SEG, S5_GP), F32)],
                  compiler_params=_params())(gx, xs, ar8, ai8, pw_r, pw_i)


def _s5_discretize(lam_re, lam_im, log_dt, b_re, b_im):
    dt = jnp.exp(log_dt)[:, None]
    mag = jnp.exp(lam_re * dt)
    ar = mag * jnp.cos(lam_im * dt)
    ai = mag * jnp.sin(lam_im * dt)
    den = lam_re * lam_re + lam_im * lam_im
    nr = ar - 1.0
    fr = (nr * lam_re + ai * lam_im) / den
    fi = (ai * lam_re - nr * lam_im) / den
    bbar_re = fr[:, :, None] * b_re - fi[:, :, None] * b_im
    bbar_im = fr[:, :, None] * b_im + fi[:, :, None] * b_re
    return ar, ai, bbar_re, bbar_im


def _block_diag(t):
    g, a, b = t.shape
    eye = jnp.eye(g, dtype=t.dtype)
    return (t[:, :, None, :] * eye[:, None, :, None]).reshape(g * a, g * b)


def _diag_blocks(m, a, b):
    g = S5_G
    return jnp.einsum('gagb->gab', m.reshape(g, a, g, b))


def _permute_rows(t):
    n = t.shape[0]
    return t.reshape(SEG, n // SEG, t.shape[1]).transpose(1, 0, 2).reshape(n, t.shape[1])


def _unpermute_rows(t):
    n = t.shape[0]
    return t.reshape(n // SEG, SEG, t.shape[1]).transpose(1, 0, 2).reshape(n, t.shape[1])


def _segment_powers(ar, ai, seg_steps):
    pr, pi = ar.reshape(1, S5_GP), ai.reshape(1, S5_GP)
    e = 1
    while e < seg_steps:
        pr, pi = pr * pr - pi * pi, 2.0 * pr * pi
        e *= 2
    assert e == seg_steps, "segment length must be a power of two"
    rows_r, rows_i = [], []
    for _ in range(3):
        rows_r.append(pr)
        rows_i.append(pi)
        pr, pi = pr * pr - pi * pi, 2.0 * pr * pi
    pad = jnp.zeros((SEG - 3, S5_GP), F32)
    return jnp.concatenate(rows_r + [pad], axis=0), jnp.concatenate(rows_i + [pad], axis=0)


NT = (((1,), (1,)), ((), ()))
TN = (((0,), (0,)), ((), ()))


def _dot(a, b, dims=None, exact=False):
    dims = (((1,), (0,)), ((), ())) if dims is None else dims
    if exact:
        return lax.dot_general(a, b, dims, precision=HI, preferred_element_type=F32)
    return lax.dot_general(a.astype(BF16), b.astype(BF16), dims, preferred_element_type=F32)


def _dot01(a, b, dims=None, ones_first=True):
    x = b if ones_first else a
    hi = x.astype(BF16)
    r1 = x - hi.astype(F32)
    mid = r1.astype(BF16)
    lo = (r1 - mid.astype(F32)).astype(BF16)
    parts = [(_dot(a, p, dims) if ones_first else _dot(p, b, dims)) for p in (lo, mid, hi)]
    return (parts[0] + parts[1]) + parts[2]


HEADS = range(4)


def _gla_chunk_fwd(qc, kc, vc, al, wup, bup, s_prev, tril):
    ones = jnp.ones((GLA_CHUNK, GLA_DV), F32)
    z = [_dot(al, wup[h]) + bup[h] for h in HEADS]
    la = [(jnp.minimum(z[h], 0.0) - jnp.log(1.0 + jnp.exp(-jnp.abs(z[h])))) * (1.0 / GLA_TAU) for h in HEADS]
    bc = [_dot01(tril, la[h]) for h in HEADS]
    blb = [_dot01(la[h], ones, TN, ones_first=False) for h in HEADS]
    bl = [bc[h][GLA_CHUNK - 1:GLA_CHUNK, :] for h in HEADS]
    ebc = [jnp.exp(bc[h]) for h in HEADS]
    qt = [qc[h] * (GLA_DK ** -0.5) * ebc[h] for h in HEADS]
    kt = [kc[h] * jnp.exp(-bc[h]) for h in HEADS]
    ke = [kc[h] * jnp.exp(bl[h] - bc[h]) for h in HEADS]
    sc = [_dot(qt[h], kt[h], NT) * tril for h in HEADS]
    oi = [_dot(sc[h], vc[h]) for h in HEADS]
    oo = [_dot(qt[h], s_prev[h]) for h in HEADS]
    o = [oi[h] + oo[h] for h in HEADS]
    return z, bc, bl, blb, ebc, qt, kt, ke, sc, o


GLA_ROWS = 512
GLA_CPB = GLA_ROWS // GLA_CHUNK


ZA_COLS = 5 * 512
SLOT = 128


def _pad_heads(w):
    r = w.shape[0]
    return jnp.pad(w.reshape(r, GLA_HEADS, GLA_DK), ((0, 0), (0, 0), (0, SLOT - GLA_DK))).reshape(r, GLA_HEADS * SLOT)


def _unpad_heads(w):
    r = w.shape[0]
    return w.reshape(r, GLA_HEADS, SLOT)[:, :, :GLA_DK].reshape(r, GLA_HEADS * GLA_DK)


def _gla_token_specs(blk):
    col = lambda cb: pl.BlockSpec((GLA_ROWS, 512), lambda j: (blk(j), cb))
    whole = lambda a: pl.BlockSpec(a.shape, lambda j: (0,) * a.ndim)
    return col, whole


def _head_ds(h, width):
    return pl.ds(h * SLOT, width)


def _tri(lower):
    ri = lax.broadcasted_iota(jnp.int32, (GLA_CHUNK, GLA_CHUNK), 0)
    ci = lax.broadcasted_iota(jnp.int32, (GLA_CHUNK, GLA_CHUNK), 1)
    return ((ri >= ci) if lower else (ri <= ci)).astype(F32)


def _gla_fwd(za, al, wup, bup, gn, name):
    n = za.shape[0]
    nc = n // GLA_CHUNK

    def body(q_ref, k_ref, v_ref, r_ref, al_ref, wup_ref, bup_ref, gn_ref, y_ref, sp_ref, s_ref):
        @pl.when(pl.program_id(0) == 0)
        def _():
            s_ref[...] = jnp.zeros_like(s_ref)

        tril = _tri(True)

        def chunk(c, carry):
            rows = pl.ds(pl.multiple_of(c * GLA_CHUNK, GLA_CHUNK), GLA_CHUNK)
            alc = al_ref[rows, :]
            vc = [v_ref[rows, _head_ds(h, GLA_DV)] for h in HEADS]
            s_prev = [s_ref[h] for h in HEADS]
            _, _, _, blb, _, _, _, ke, _, o = _gla_chunk_fwd(
                [q_ref[rows, _head_ds(h, GLA_DK)] for h in HEADS], [k_ref[rows, _head_ds(h, GLA_DK)] for h in HEADS],
                vc, alc, [wup_ref[h] for h in HEADS], [bup_ref[h] for h in HEADS], s_prev, tril)
            ds = [_dot(ke[h], vc[h], TN) for h in HEADS]
            for h in HEADS:
                rc = r_ref[rows, _head_ds(h, GLA_DV)]
                sp_ref[h, c] = s_prev[h]
                rstd = lax.rsqrt(jnp.mean(o[h] * o[h], axis=-1, keepdims=True) + EPS)
                y_ref[rows, _head_ds(h, GLA_DV)] = (o[h] * rstd * gn_ref[h] * (rc * _sigmoid(rc))).astype(BF16)
                s_ref[h] = jnp.exp(blb[h]) * s_prev[h] + ds[h]
            return carry

        lax.fori_loop(0, GLA_CPB, chunk, 0)

    col, whole = _gla_token_specs(lambda j: j)
    return _pcall(body, name=name, grid=(n // GLA_ROWS,),
                  in_specs=[col(1), col(2), col(3), col(4), pl.BlockSpec((GLA_ROWS, LANE), lambda j: (j, 0)),
                            whole(wup), whole(bup), whole(gn)],
                  out_specs=[pl.BlockSpec((GLA_ROWS, GLA_HEADS * GLA_DV), lambda j: (j, 0)),
                             pl.BlockSpec((GLA_HEADS, GLA_CPB, GLA_DK, GLA_DV), lambda j: (0, j, 0, 0))],
                  out_shape=[jax.ShapeDtypeStruct((n, GLA_HEADS * GLA_DV), BF16),
                             jax.ShapeDtypeStruct((GLA_HEADS, nc, GLA_DK, GLA_DV), F32)],
                  scratch_shapes=[pltpu.VMEM((GLA_HEADS, GLA_DK, GLA_DV), F32)],
                  compiler_params=_params())(za, za, za, za, al, wup, bup, gn)


def _gla_bwd(za, al, wup, bup, gn, sp, dy, du_s5, name):
    n = za.shape[0]
    nb = n // GLA_ROWS

    def body(q_ref, k_ref, v_ref, r_ref, al_ref, wup_ref, bup_ref, gn_ref, dy_ref, dus_ref, sp_ref,
             dza_ref, dz_ref, dgn_ref, dbup_ref, ds_ref):
        @pl.when(pl.program_id(0) == 0)
        def _():
            ds_ref[...] = jnp.zeros_like(ds_ref)
            dgn_ref[...] = jnp.zeros_like(dgn_ref)
            dbup_ref[...] = jnp.zeros_like(dbup_ref)

        tril, triu = _tri(True), _tri(False)
        dza_ref[:, 0:512] = dus_ref[...]
        dza_ref[:, 512:1536] = jnp.zeros((GLA_ROWS, 1024), F32)
        dz_ref[...] = jnp.zeros_like(dz_ref)

        def chunk(i, carry):
            c = GLA_CPB - 1 - i
            rows = pl.ds(pl.multiple_of(c * GLA_CHUNK, GLA_CHUNK), GLA_CHUNK)
            alc = al_ref[rows, :]
            qc = [q_ref[rows, _head_ds(h, GLA_DK)] for h in HEADS]
            kc = [k_ref[rows, _head_ds(h, GLA_DK)] for h in HEADS]
            vc = [v_ref[rows, _head_ds(h, GLA_DV)] for h in HEADS]
            s_prev = [sp_ref[h, c] for h in HEADS]
            ds = [ds_ref[h] for h in HEADS]
            z, bc, bl, blb, ebc, qt, kt, ke, sc, o = _gla_chunk_fwd(
                qc, kc, vc, alc, [wup_ref[h] for h in HEADS], [bup_ref[h] for h in HEADS], s_prev, tril)
            do = []
            for h in HEADS:
                rc = r_ref[rows, _head_ds(h, GLA_DV)]
                rs = lax.rsqrt(jnp.mean(o[h] * o[h], axis=-1, keepdims=True) + EPS)
                on = o[h] * rs
                sr = _sigmoid(rc)
                sil = rc * sr
                dyv, gnv = dy_ref[rows, _head_ds(h, GLA_DV)], gn_ref[h]
                dgn_ref[h] += jnp.sum(dyv * on * sil, axis=0, keepdims=True)
                dza_ref[rows, pl.ds(2048 + h * SLOT, GLA_DV)] = dyv * on * gnv * (sr * (1.0 + rc * (1.0 - sr)))
                don = dyv * gnv * sil
                do.append(rs * (don - on * jnp.mean(don * on, axis=-1, keepdims=True)))
            dp = [_dot(do[h], vc[h], NT) * tril for h in HEADS]
            dv1 = [_dot(sc[h], do[h], TN) for h in HEADS]
            dv2 = [_dot(ke[h], ds[h]) for h in HEADS]
            dq2 = [_dot(do[h], s_prev[h], NT) for h in HEADS]
            dke = [_dot(vc[h], ds[h], NT) for h in HEADS]
            ddec = [_dot01(jnp.ones((8, GLA_DV), F32), ds[h] * s_prev[h], NT)[0:1, :] for h in HEADS]
            dsn = [_dot(qt[h], do[h], TN) for h in HEADS]
            dq1 = [_dot(dp[h], kt[h]) for h in HEADS]
            dkt = [_dot(dp[h], qt[h], TN) for h in HEADS]
            dbc, dbl = [], []
            for h in HEADS:
                dqt = dq1[h] + dq2[h]
                dza_ref[rows, pl.ds(1536 + h * SLOT, GLA_DV)] = dv1[h] + dv2[h]
                ds_ref[h] = jnp.exp(blb[h]) * ds[h] + dsn[h]
                dza_ref[rows, pl.ds(512 + h * SLOT, GLA_DK)] = dqt * (GLA_DK ** -0.5) * ebc[h]
                dza_ref[rows, pl.ds(1024 + h * SLOT, GLA_DK)] = dkt[h] * jnp.exp(-bc[h]) + dke[h] * jnp.exp(bl[h] - bc[h])
                dbc.append(dqt * qt[h] - dkt[h] * kt[h] - dke[h] * ke[h])
                dbl.append(jnp.sum(dke[h] * ke[h], axis=0, keepdims=True) + ddec[h] * jnp.exp(bl[h]))
            dla = [_dot01(triu, dbc[h]) + dbl[h] for h in HEADS]
            for h in HEADS:
                dz = dla[h] * (1.0 - _sigmoid(z[h])) * (1.0 / GLA_TAU)
                dz_ref[rows, _head_ds(h, GLA_DK)] = dz
                dbup_ref[h] += jnp.sum(dz, axis=0, keepdims=True)
            return carry

        lax.fori_loop(0, GLA_CPB, chunk, 0)

    rev = lambda j: nb - 1 - j
    col, whole = _gla_token_specs(rev)
    tok = lambda w: pl.BlockSpec((GLA_ROWS, w), lambda j: (rev(j), 0))
    h1 = lambda w: pl.BlockSpec((GLA_HEADS, 1, w), lambda j: (0, 0, 0))
    s1 = lambda w: jax.ShapeDtypeStruct((GLA_HEADS, 1, w), F32)
    return _pcall(body, name=name, grid=(nb,),
                  in_specs=[col(1), col(2), col(3), col(4), tok(LANE), whole(wup), whole(bup), whole(gn), tok(512), tok(512),
                            pl.BlockSpec((GLA_HEADS, GLA_CPB, GLA_DK, GLA_DV), lambda j: (0, rev(j), 0, 0))],
                  out_specs=[tok(ZA_COLS), tok(GLA_HEADS * SLOT), h1(GLA_DV), h1(GLA_DK)],
                  out_shape=[jax.ShapeDtypeStruct((n, ZA_COLS), F32), jax.ShapeDtypeStruct((n, GLA_HEADS * SLOT), F32),
                             s1(GLA_DV), s1(GLA_DK)],
                  scratch_shapes=[pltpu.VMEM((GLA_HEADS, GLA_DK, GLA_DV), F32)],
                  compiler_params=_params())(za, za, za, za, al, wup, bup, gn, dy, du_s5, sp)


ANY = pl.BlockSpec(memory_space=pl.ANY)


def _place():
    x, y, c = lax.axis_index("x"), lax.axis_index("y"), lax.axis_index("c")
    chips = [(1 - x, y), (x, 1 - y), (1 - x, 1 - y)]
    return x, y, c, chips


def _remote(src, dst, ssem, rsem, dev):
    return pltpu.make_async_remote_copy(src_ref=src, dst_ref=dst, send_sem=ssem, recv_sem=rsem, device_id=dev,
                                        device_id_type=MESH_ID)


def _half(c, rows):
    h = rows // 2
    return pl.ds(pl.multiple_of(c * h, 8), h)


def _side_gather_ici(shards):
    def copies(ins, outs, ssem, rsem):
        x, y, c, chips = _place()
        mine = 2 * x + y
        cps = []
        for w in range(len(ins)):
            half = _half(c, ins[w].shape[0])
            cps.append(_remote(ins[w], outs[w].at[mine], ssem.at[4 * w], rsem.at[4 * w], (x, y, 1 - c)))
            for k, (px, py) in enumerate(chips):
                cps.append(_remote(ins[w].at[half], outs[w].at[mine, half], ssem.at[4 * w + 1 + k], rsem.at[4 * w + 1 + k],
                                   (px, py, c)))
        return cps

    return _Side(shards, [jax.ShapeDtypeStruct((4,) + s.shape, s.dtype) for s in shards], 4 * len(shards), copies)


def _side_gather_d2d(gathered):
    def copies(ins, outs, ssem, rsem):
        x, y, c, chips = _place()
        cps = []
        for w in range(len(outs)):
            half = _half(c, outs[w].shape[1])
            for k, (px, py) in enumerate(chips):
                theirs = outs[w].at[2 * px + py, half]
                cps.append(_remote(theirs, theirs, ssem.at[3 * w + k], rsem.at[3 * w + k], (x, y, 1 - c)))
        return cps

    return _Side(gathered, [jax.ShapeDtypeStruct(g.shape, g.dtype) for g in gathered], 3 * len(gathered), copies,
                 aliased=True)


def _side_swap_halves(grads):
    def copies(ins, outs, ssem, rsem):
        x, y, c, _ = _place()
        return [_remote(ins[w].at[:, _half(1 - c, ins[w].shape[1]), :], outs[w], ssem.at[w], rsem.at[w], (x, y, 1 - c))
                for w in range(len(ins))]

    return _Side(grads, [jax.ShapeDtypeStruct((4, g.shape[1] // 2, g.shape[2]), g.dtype) for g in grads], len(grads), copies)


def _side_scatter(sums):
    def copies(ins, outs, ssem, rsem):
        x, y, c, chips = _place()
        return [_remote(ins[w].at[2 * px + py], outs[w].at[k], ssem.at[3 * w + k], rsem.at[3 * w + k], (px, py, c))
                for w in range(len(ins)) for k, (px, py) in enumerate(chips)]

    return _Side(sums, [jax.ShapeDtypeStruct((3,) + s.shape[1:], s.dtype) for s in sums], 3 * len(sums), copies)


def _side_swap_reduced(halves):
    def copies(ins, outs, ssem, rsem):
        x, y, c, _ = _place()
        return [_remote(ins[w], outs[w], ssem.at[w], rsem.at[w], (x, y, 1 - c)) for w in range(len(ins))]

    return _Side(halves, [jax.ShapeDtypeStruct(h.shape, h.dtype) for h in halves], len(halves), copies)


def _chip_sum(g, recv, c_arr, name):
    _, r, cols = g.shape
    h = r // 2
    tr = _pick(h, 256, 16)
    g4 = g.reshape(4, 2, h, cols)

    def body(c_ref, g_ref, r_ref, o_ref):
        o_ref[...] = (g_ref[...] + r_ref[...]).astype(BF16)

    grid_spec = pltpu.PrefetchScalarGridSpec(
        num_scalar_prefetch=1, grid=(4, h // tr),
        in_specs=[pl.BlockSpec((None, None, tr, cols), lambda s, i, c_ref: (s, c_ref[0], i, 0)),
                  pl.BlockSpec((None, tr, cols), lambda s, i, c_ref: (s, i, 0))],
        out_specs=pl.BlockSpec((None, tr, cols), lambda s, i, c_ref: (s, i, 0)))
    return _pcall(body, name=name, grid_spec=grid_spec, out_shape=jax.ShapeDtypeStruct((4, h, cols), BF16),
                  compiler_params=_params())(c_arr, g4, recv)


def _owner_sum(sums, others, s_arr, name):
    _, h, cols = sums.shape
    tr = _pick(h, 256, 16)

    def body(s_ref, a_ref, o_ref, out_ref):
        f = lambda v: v.astype(F32)
        out_ref[...] = (f(a_ref[...]) + f(o_ref[0])) + (f(o_ref[1]) + f(o_ref[2]))

    grid_spec = pltpu.PrefetchScalarGridSpec(
        num_scalar_prefetch=1, grid=(h // tr,),
        in_specs=[pl.BlockSpec((None, tr, cols), lambda i, s_ref: (s_ref[0], i, 0)),
                  pl.BlockSpec((3, tr, cols), lambda i, s_ref: (0, i, 0))],
        out_specs=pl.BlockSpec((tr, cols), lambda i, s_ref: (i, 0)))
    return _pcall(body, name=name, grid_spec=grid_spec, out_shape=jax.ShapeDtypeStruct((h, cols), F32),
                  compiler_params=_params())(s_arr, sums, others)


def _allreduce_small(v, name):
    def body(v_ref, o_ref, r0, r1, ssem, rsem):
        x, y, c, chips = _place()
        cp = _remote(v_ref, r0, ssem.at[0], rsem.at[0], (x, y, 1 - c))
        cp.start()
        cp.wait()
        o_ref[...] = v_ref[...] + r0[...]
        cps = []
        for k, (px, py) in enumerate(chips):
            cp = _remote(o_ref, r1.at[k], ssem.at[1 + k], rsem.at[1 + k], (px, py, c))
            cp.start()
            cps.append(cp)
        for cp in cps:
            cp.wait()
        o_ref[...] = (o_ref[...] + r1[0]) + (r1[1] + r1[2])

    vm = pl.BlockSpec(memory_space=pltpu.VMEM)
    return _pcall(body, name=name, in_specs=[vm], out_specs=vm, out_shape=jax.ShapeDtypeStruct(v.shape, F32),
                  scratch_shapes=[pltpu.VMEM(v.shape, F32), pltpu.VMEM((3,) + v.shape, F32),
                                  pltpu.SemaphoreType.DMA((4,)), pltpu.SemaphoreType.DMA((4,))],
                  compiler_params=_params(has_side_effects=True))(v)


def _tile_rows(size):
    return -(-size // (8 * LANE)) * 8


def _pack_small(parts):
    pieces = []
    for p in parts:
        flat = p.reshape(-1).astype(F32)
        pieces.append(jnp.pad(flat, (0, _tile_rows(p.size) * LANE - p.size)).reshape(-1, LANE))
    rows = sum(x.shape[0] for x in pieces)
    pieces.append(jnp.zeros(((-rows) % 64, LANE), F32))
    return jnp.concatenate(pieces, axis=0)


def _unpack_small(packed, like):
    out, pos = [], 0
    for p in like:
        rows = _tile_rows(p.size)
        out.append(packed[pos:pos + rows].reshape(-1)[:p.size].reshape(p.shape))
        pos += rows
    return out


FFN_FWD_ROWS, FFN_BWD_ROWS = 1024, 512
FFN_SUB_ROWS = 256


def _ffn_specs(n, d, fs, cap):
    rows = _pick(n, cap, 16)
    row = pl.BlockSpec((rows, d), lambda i, s: (i, 0))
    gain = pl.BlockSpec((1, d), lambda i, s: (0, 0))
    w_row = pl.BlockSpec((None, fs, d), lambda i, s: (s, 0, 0))
    hid = pl.BlockSpec((None, rows, fs), lambda i, s: (s, i, 0))
    return rows, row, gain, w_row, hid


def _ffn_fwd(h, g, w1t, w3t, w2, tag, plan):
    n, d = h.shape
    ns, fs, _ = w2.shape
    rows, row, gain, w_row, hid = _ffn_specs(n, d, fs, FFN_FWD_ROWS)
    sub = rows

    def body(h_ref, g_ref, w1_ref, w3_ref, w2_ref, out_ref, n1_ref, a_ref, b_ref, hm_ref, acc_ref):
        s = pl.program_id(1)

        @pl.when(s == 0)
        def _():
            xv = h_ref[...]
            rstd = lax.rsqrt(jnp.mean(xv * xv, axis=-1, keepdims=True) + EPS)
            n1_ref[...] = (xv * rstd * g_ref[...]).astype(BF16)
            acc_ref[...] = jnp.zeros_like(acc_ref)

        def up(j):
            n1 = n1_ref[j * sub:(j + 1) * sub, :]
            return _dot(n1, w1_ref[...], NT), _dot(n1, w3_ref[...], NT)

        cur = up(0)
        for j in range(rows // sub):
            nxt = up(j + 1) if (j + 1) * sub < rows else None
            a, b = cur
            r = slice(j * sub, (j + 1) * sub)
            hm = (a * _sigmoid(a) * b).astype(BF16)
            a_ref[r, :] = a.astype(BF16)
            b_ref[r, :] = b.astype(BF16)
            hm_ref[r, :] = hm
            acc_ref[r, :] += _dot(hm, w2_ref[...])
            cur = nxt

        @pl.when(s == ns - 1)
        def _():
            out_ref[...] = h_ref[...] + 0.5 * acc_ref[...]

    hid_shape = jax.ShapeDtypeStruct((ns, n, fs), BF16)
    plan.before(f"{tag}_fwd")
    out, n1, a, b, hm = _pcall(
        body, name=f"{tag}_fwd", grid=(n // rows, ns), in_specs=[row, gain, w_row, w_row, w_row],
        out_specs=[row, row, hid, hid, hid],
        out_shape=[jax.ShapeDtypeStruct((n, d), F32), jax.ShapeDtypeStruct((n, d), BF16), hid_shape, hid_shape, hid_shape],
        scratch_shapes=[pltpu.VMEM((rows, d), F32)], compiler_params=_params())(h, g, w1t, w3t, w2)
    plan.after(f"{tag}_fwd")
    return out, (h, n1, a, b, hm)


def _ffn_bwd(dout, saved, g, w1, w3, w2, tag, plan):
    h, n1, a, b, hm = saved
    n, d = h.shape
    ns, fs, _ = w2.shape
    rows, row, gain, w_row, hid = _ffn_specs(n, d, fs, FFN_BWD_ROWS)
    sub = _pick(rows, FFN_SUB_ROWS, 16)

    def body(do_ref, h_ref, g_ref, a_ref, b_ref, w1_ref, w3_ref, w2_ref, dh_ref, da_ref, db_ref, dg_ref, acc_ref):
        i, s = pl.program_id(0), pl.program_id(1)

        @pl.when(s == 0)
        def _():
            acc_ref[...] = jnp.zeros_like(acc_ref)

        @pl.when((s == 0) & (i == 0))
        def _():
            dg_ref[...] = jnp.zeros_like(dg_ref)

        def up(j):
            return _dot(0.5 * do_ref[j * sub:(j + 1) * sub, :], w2_ref[...], NT)

        cur = up(0)
        for j in range(rows // sub):
            nxt = up(j + 1) if (j + 1) * sub < rows else None
            r = slice(j * sub, (j + 1) * sub)
            av, bv = a_ref[r, :].astype(F32), b_ref[r, :].astype(F32)
            sg = _sigmoid(av)
            da = (cur * bv * (sg * (1.0 + av * (1.0 - sg)))).astype(BF16)
            db = (cur * av * sg).astype(BF16)
            da_ref[r, :] = da
            db_ref[r, :] = db
            acc_ref[r, :] += _dot(da, w1_ref[...]) + _dot(db, w3_ref[...])
            cur = nxt

        @pl.when(s == ns - 1)
        def _():
            xv, dn = h_ref[...], acc_ref[...]
            rstd = lax.rsqrt(jnp.mean(xv * xv, axis=-1, keepdims=True) + EPS)
            xh = xv * rstd
            dg_ref[...] += jnp.sum(dn * xh, axis=0, keepdims=True)
            dxh = dn * g_ref[...]
            dh_ref[...] = do_ref[...] + rstd * (dxh - xh * jnp.mean(dxh * xh, axis=-1, keepdims=True))

    hid_shape = jax.ShapeDtypeStruct((ns, n, fs), BF16)
    plan.before(f"{tag}_bwd")
    dh, da, db, dg = _pcall(
        body, name=f"{tag}_bwd", grid=(n // rows, ns), in_specs=[row, row, gain, hid, hid, w_row, w_row, w_row],
        out_specs=[row, hid, hid, gain],
        out_shape=[jax.ShapeDtypeStruct((n, d), F32), hid_shape, hid_shape, jax.ShapeDtypeStruct((1, d), F32)],
        scratch_shapes=[pltpu.VMEM((rows, d), F32)], compiler_params=_params())(dout, h, g, a, b, w1, w3, w2)
    plan.after(f"{tag}_bwd")
    plan.before(f"{tag}_gw2")
    gw2 = _mm(hm, dout, ta=True, shard='m', alpha=0.5, name=f"{tag}_gw2")
    plan.after(f"{tag}_gw2")
    gw1 = _mm(da, n1, ta=True, shard='m', name=f"{tag}_gw1")
    gw3 = _mm(db, n1, ta=True, shard='m', name=f"{tag}_gw3")
    return dh, dg, gw1, gw3, gw2


def _local_step(x, tgt, plan):
    n = x.shape[0]
    grads = plan.grads

    def f(name):
        w = plan.get(name)
        return w.reshape(1, D_MODEL) if name.endswith('_norm') and name != 'gla_out_norm' else w

    def carried(tag, fn, *args, **kw):
        plan.before(tag)
        out = fn(*args, **kw)
        plan.after(tag)
        return out

    h1, ffn1 = _ffn_fwd(x, f('ffn1_norm'), f('ffn1_w1'), f('ffn1_w3'), f('ffn1_w2'), "ffn1", plan)
    u = carried("mix_rms", _rms_fwd, h1, f('mix_norm'), "mix_rms")
    w_in = f('w_in')
    w_a = jnp.concatenate([w_in[:, :512], _pad_heads(w_in[:, 512:768]), _pad_heads(w_in[:, 768:1024]), w_in[:, 1024:2048]],
                          axis=1)
    w_al = jnp.pad(w_in[:, 2048:2048 + GLA_RANK], ((0, 0), (0, LANE - GLA_RANK)))
    w_g = w_in[:, 2048 + GLA_RANK:]
    za = _mm(u, w_a, name="in_a")
    zg = _mm(u, w_g, name="in_g")
    al = _mm(u, w_al, name="in_al")
    ar, ai, bbar_re, bbar_im = _s5_discretize(f('s5_lambda_re'), f('s5_lambda_im'), f('s5_log_dt'), f('s5_b_re'), f('s5_b_im'))
    b_blk = jnp.concatenate([_block_diag(bbar_re.transpose(0, 2, 1)), _block_diag(bbar_im.transpose(0, 2, 1))], axis=1)
    c_blk = jnp.concatenate([_block_diag(f('s5_c_re').transpose(0, 2, 1)), -_block_diag(f('s5_c_im').transpose(0, 2, 1))], axis=0)
    b_blk, c_blk = b_blk.astype(BF16), c_blk.astype(BF16)
    ar8 = jnp.broadcast_to(ar.reshape(1, S5_GP), (SEG, S5_GP))
    ai8 = jnp.broadcast_to(ai.reshape(1, S5_GP), (SEG, S5_GP))
    pw_r, pw_i = _segment_powers(ar, ai, n // SEG)
    dskip = f('s5_d').reshape(1, S5_W)
    u_s5 = _permute_rows(za[:, :S5_W])
    bu = _mm(u_s5, b_blk, name="s5_bu")
    xs = _s5_scan(bu, ar8, ai8, pw_r, pw_i, "s5_scan")
    ys_p = _mm(xs, c_blk, res=_scale_rows(u_s5, dskip, "s5_skip"), name="s5_y")
    ys = _unpermute_rows(ys_p)
    zgelu = _gelu_fwd(ys, "s5_gelu")
    t_glu = _mm(zgelu, f('s5_glu_w'), bias=f('s5_glu_b').reshape(1, S5_W), name="s5_glu_t")
    y_s5 = _glu_fwd(zgelu, t_glu, "s5_glu")
    wup = jnp.pad(f('gla_a_up_w'), ((0, LANE - GLA_RANK), (0, 0)))
    wup_h = wup.reshape(LANE, GLA_HEADS, GLA_DK).transpose(1, 0, 2)
    bup_h = f('gla_a_up_b').reshape(GLA_HEADS, 1, GLA_DK)
    gn_h = f('gla_out_norm').reshape(GLA_HEADS, 1, GLA_DV)
    y_gla, s_prev = carried("gla_fwd", _gla_fwd, za, al, wup_h, bup_h, gn_h, "gla_fwd")
    ps = _mm(y_s5, f('proj_s5'), name="proj_s5")
    pg = carried("proj_gla", _mm, y_gla, f('proj_gla'), name="proj_gla")
    merged = _merge_fwd(zg, ps, pg, "merge")
    h2 = _mm(merged, f('w_out'), res=h1, name="w_out")
    h3, ffn2 = _ffn_fwd(h2, f('ffn2_norm'), f('ffn2_w1'), f('ffn2_w3'), f('ffn2_w2'), "ffn2", plan)
    loss, dh3, g_final = _final_loss(h3, f('final_norm').reshape(1, D_MODEL), tgt, "loss")
    grads['final_norm'] = g_final.reshape(D_MODEL)
    dh2, grads['ffn2_norm'], grads['ffn2_w1'], grads['ffn2_w3'], grads['ffn2_w2'] = _ffn_bwd(
        dh3, ffn2, f('ffn2_norm'), f('ffn2_w1'), f('ffn2_w3'), f('ffn2_w2'), "ffn2", plan)
    dm = _mm(dh2, f('w_out'), tb=True, name="d_merged")
    grads['w_out'] = _mm(merged, dh2, ta=True, name="g_w_out")
    dps, dpg, dzg = carried("d_merge", _merge_bwd, dm, zg, ps, pg, "d_merge")
    grads['proj_s5'] = _mm(y_s5, dps, ta=True, name="g_proj_s5")
    grads['proj_gla'] = _mm(y_gla, dpg, ta=True, name="g_proj_gla")
    dy_s5 = _mm(dps, f('proj_s5'), tb=True, name="d_y_s5")
    dy_gla = _mm(dpg, f('proj_gla'), tb=True, name="d_y_gla")
    dzgelu, dt_glu, g_glu_b = _glu_bwd1(dy_s5, zgelu, t_glu, "d_glu")
    grads['s5_glu_b'] = g_glu_b.reshape(S5_W)
    grads['s5_glu_w'] = _mm(zgelu, dt_glu, ta=True, name="g_glu_w")
    dzgelu = _mm(dt_glu, f('s5_glu_w'), tb=True, res=dzgelu, name="d_gelu")
    dys, du_skip, g_d = _glu_bwd2(_permute_rows(dzgelu), ys_p, u_s5, dskip, "d_s5_y")
    grads['s5_d'] = g_d.reshape(S5_G, S5_H)
    gx = _mm(dys, c_blk, tb=True, name="s5_gx")
    lam, da8 = _s5_scan_bwd(gx, xs, ar8, ai8, pw_r, pw_i, "s5_scan_bwd")
    g_c = _mm(dys, xs, ta=True, name="g_s5_c")
    grads['s5_c_re'] = _diag_blocks(g_c[:, :S5_GP], S5_H, S5_P)
    grads['s5_c_im'] = -_diag_blocks(g_c[:, S5_GP:], S5_H, S5_P)
    g_b = _mm(lam, u_s5, ta=True, name="g_s5_b")
    g_bbar_re = _diag_blocks(g_b[:S5_GP], S5_P, S5_H)
    g_bbar_im = _diag_blocks(g_b[S5_GP:], S5_P, S5_H)
    da = jnp.sum(da8, axis=0)
    g_ar, g_ai = da[:S5_GP].reshape(S5_G, S5_P), da[S5_GP:].reshape(S5_G, S5_P)
    _, disc_vjp = jax.vjp(_s5_discretize, f('s5_lambda_re'), f('s5_lambda_im'), f('s5_log_dt'), f('s5_b_re'), f('s5_b_im'))
    (grads['s5_lambda_re'], grads['s5_lambda_im'], grads['s5_log_dt'], grads['s5_b_re'],
     grads['s5_b_im']) = disc_vjp((g_ar, g_ai, g_bbar_re, g_bbar_im))
    du_s5 = _unpermute_rows(_mm(lam, b_blk, tb=True, res=du_skip, name="d_s5_u"))
    dza, dz, dgn, dbup = carried("gla_bwd", _gla_bwd, za, al, wup_h, bup_h, gn_h, s_prev, dy_gla, du_s5, "gla_bwd")
    grads['gla_out_norm'] = dgn.reshape(GLA_HEADS * GLA_DV)
    grads['gla_a_up_b'] = dbup.reshape(GLA_HEADS * GLA_DK)
    grads['gla_a_up_w'] = _unpad_heads(_mm(al, dz, ta=True, name="g_a_up")[:GLA_RANK])
    dal = _mm(dz, _pad_heads(wup), tb=True, name="d_a_low")
    g_wa = _mm(u, dza, ta=True, name="g_in_a")
    g_wg = _mm(u, dzg, ta=True, name="g_in_g")
    g_wal = _mm(u, dal, ta=True, name="g_in_al")
    grads['w_in'] = jnp.concatenate([g_wa[:, :512], _unpad_heads(g_wa[:, 512:1024]), _unpad_heads(g_wa[:, 1024:1536]),
                                     g_wa[:, 1536:], g_wal[:, :GLA_RANK], g_wg], axis=1)
    du = carried("d_u_a", _mm, dza, w_a, tb=True, name="d_u_a")
    du = _mm(dzg, w_g, tb=True, res=du, name="d_u_g")
    du = _mm(dal, w_al, tb=True, res=du, name="d_u_al")
    dh1, g_mix = carried("d_mix_rms", _rms_bwd, h1, f('mix_norm'), du, dh2, "d_mix_rms")
    grads['mix_norm'] = g_mix
    dx, grads['ffn1_norm'], grads['ffn1_w1'], grads['ffn1_w3'], grads['ffn1_w2'] = _ffn_bwd(
        dh1, ffn1, f('ffn1_norm'), f('ffn1_w1'), f('ffn1_w3'), f('ffn1_w2'), "ffn1", plan)
    return loss[0, 0], dx


MIXER_WEIGHTS = ['w_in', 's5_glu_w', 'proj_s5', 'proj_gla', 'w_out', 'gla_a_up_w']
FFN1_WEIGHTS, FFN2_WEIGHTS = FFN_WEIGHTS[:3], FFN_WEIGHTS[3:]
TRANSPOSED = ['ffn1_w1', 'ffn1_w3', 'ffn2_w1', 'ffn2_w3']


def _local_shard(w, nm):
    return jnp.swapaxes(w, 1, 2)[0] if nm in TRANSPOSED else w[0]
GRAD_GROUPS = {'ffn2': FFN2_WEIGHTS, 'mixer': ['w_out', 'proj_s5', 'proj_gla', 's5_glu_w', 'w_in'], 'ffn1': FFN1_WEIGHTS}


class _Plan:
    def __init__(self, a, c_arr, s_arr):
        self.a, self.c_arr, self.s_arr = a, c_arr, s_arr
        self.grads, self.weights, self.riding = {}, {}, {}
        self.g4s, self.chip_sums, self.halves, self.sib_halves = {}, {}, {}, {}
        for nm in SMALL:
            if nm != 'gla_a_up_w':
                self.weights[nm] = a[nm] if nm == 'final_norm' else a[nm][0]
        ici = _side_gather_ici(self._shards(FFN1_WEIGHTS))
        _run_side(ici, "gather_ffn1_ici")
        self._gathered(FFN1_WEIGHTS, _run_side(_side_gather_d2d(ici.outs), "gather_ffn1_d2d"))

    def _shards(self, names):
        return [_local_shard(self.a[nm], nm).astype(F32 if nm == 'gla_a_up_w' else BF16) for nm in names]

    def _gathered(self, names, arrs):
        for nm, g4 in zip(names, arrs):
            if nm in FFN_WEIGHTS:
                self.weights[nm] = g4
            elif nm in COL_SHARDED:
                self.weights[nm] = jnp.concatenate([g4[s] for s in range(4)], axis=1)
            else:
                self.weights[nm] = g4.reshape(4 * g4.shape[1], g4.shape[2])

    def get(self, name):
        return self.weights[name]

    def _shard_major(self, nm):
        g = self.grads[nm]
        if nm in FFN_WEIGHTS:
            return g
        if nm in COL_SHARDED:
            return jnp.stack(jnp.split(g, 4, axis=1))
        return g.reshape(4, g.shape[0] // 4, g.shape[1])

    def _schedule(self, tag):
        grp = GRAD_GROUPS
        if tag == "ffn1_fwd":
            return _side_gather_ici(self._shards(MIXER_WEIGHTS)), lambda outs: self.riding.update(mixer_ici=outs)
        if tag == "mix_rms":
            return _side_gather_d2d(self.riding['mixer_ici']), lambda outs: self._gathered(MIXER_WEIGHTS, outs)
        if tag == "gla_fwd":
            return _side_gather_ici(self._shards(FFN2_WEIGHTS)), lambda outs: self.riding.update(ffn2_ici=outs)
        if tag == "proj_gla":
            return _side_gather_d2d(self.riding['ffn2_ici']), lambda outs: self._gathered(FFN2_WEIGHTS, outs)
        steps = {"d_merge": ('ffn2', 0), "gla_bwd": ('ffn2', 1), "d_mix_rms": ('ffn2', 2),
                 "d_u_a": ('mixer', 0), "ffn1_bwd": ('mixer', 1), "ffn1_gw2": ('mixer', 2)}
        if tag in steps:
            group, stage = steps[tag]
            return self._reduce_stage(grp[group], stage)
        return None

    def _reduce_stage(self, names, stage):
        if stage == 0:
            for nm in names:
                self.g4s[nm] = self._shard_major(nm)

            def done(outs):
                for nm, r in zip(names, outs):
                    self.chip_sums[nm] = _chip_sum(self.g4s[nm], r, self.c_arr, f"chip_sum_{nm}")
            return _side_swap_halves([self.g4s[nm] for nm in names]), done
        if stage == 1:
            def done(outs):
                for nm, o in zip(names, outs):
                    self.halves[nm] = _owner_sum(self.chip_sums[nm], o, self.s_arr, f"owner_sum_{nm}")
            return _side_scatter([self.chip_sums[nm] for nm in names]), done

        def done(outs):
            self.sib_halves.update(zip(names, outs))
        return _side_swap_reduced([self.halves[nm] for nm in names]), done

    def before(self, tag):
        entry = self._schedule(tag)
        if entry is not None:
            side, done = entry
            self.riding[tag] = (side, done)
            _RIDER.append(side)

    def after(self, tag):
        if tag in self.riding:
            side, done = self.riding.pop(tag)
            assert not _RIDER and side.outs is not None, tag
            done(side.outs)

    def finish(self):
        names = GRAD_GROUPS['ffn1']
        for stage in range(3):
            side, done = self._reduce_stage(names, stage)
            done(_run_side(side, f"grad_ffn1_stage{stage}"))


def _train_step(a):
    x = a['x'][0]
    tgt = a['loss_target'][0]
    xi, yi, ci = lax.axis_index("x"), lax.axis_index("y"), lax.axis_index("c")
    c_arr = jnp.reshape(ci, (1,)).astype(jnp.int32)
    s_arr = jnp.reshape(2 * xi + yi, (1,)).astype(jnp.int32)
    plan = _Plan(a, c_arr, s_arr)
    loss, dx = _local_step(x, tgt, plan)
    plan.finish()
    grads = plan.grads
    loss = lax.psum(loss, ("x", "y", "c"))
    halves = [plan.halves[nm] for nm in SHARDED]
    sib_halves = [plan.sib_halves[nm] for nm in SHARDED]
    red = {}
    small_parts = [grads[nm].reshape(a[nm].shape) for nm in SMALL if nm != 'gla_a_up_w'] + [grads['gla_a_up_w']]
    small_sum = _unpack_small(_allreduce_small(_pack_small(small_parts), "allreduce_small"), small_parts)
    small_names = [nm for nm in SMALL if nm != 'gla_a_up_w']
    for nm, g in zip(small_names, small_sum[:-1]):
        red[nm] = g
    g_up = small_sum[-1]
    red['gla_a_up_w'] = lax.dynamic_slice(g_up, (0, (2 * xi + yi) * GLA_DK), (GLA_RANK, GLA_DK))
    out_g, out_d, out_m, out_v = {}, {}, {}, {}
    for nm, own, sib in zip(SHARDED, halves, sib_halves):
        loc = lambda pre: _local_shard(a[pre + nm], nm)
        res = _adamw_halves(loc(''), own, sib, loc('m_'), loc('v_'), c_arr, f"adamw_{nm}")
        back = (lambda t: jnp.swapaxes(t[None], 1, 2)) if nm in TRANSPOSED else (lambda t: t[None])
        out_g[nm], out_d[nm], out_m[nm], out_v[nm] = (back(t) for t in res)
    rest = [nm for nm in WEIGHTS if nm not in SHARDED]
    pk = lambda pre: _pack_small([a[pre + nm] for nm in rest])
    d, nm_, nv_ = _adamw(pk(''), _pack_small([red[nm] for nm in rest]), pk('m_'), pk('v_'), "adamw_small")
    like = [a[nm] for nm in rest]
    for nm, g, dd, mm_, vv_ in zip(rest, [red[nm].reshape(a[nm].shape) for nm in rest], _unpack_small(d, like),
                                   _unpack_small(nm_, like), _unpack_small(nv_, like)):
        out_g[nm], out_d[nm], out_m[nm], out_v[nm] = g, dd, mm_, vv_
    return (loss, dx[None], *[out_g[nm] for nm in WEIGHTS], *[out_d[nm] for nm in WEIGHTS],
            *[out_m[nm] for nm in WEIGHTS], *[out_v[nm] for nm in WEIGHTS])


def kernel(x, ffn1_norm, ffn1_w1, ffn1_w3, ffn1_w2, mix_norm, w_in, s5_lambda_re, s5_lambda_im, s5_log_dt, s5_b_re, s5_b_im, s5_c_re, s5_c_im, s5_d, s5_glu_w, s5_glu_b, gla_a_up_w, gla_a_up_b, gla_out_norm, proj_s5, proj_gla, w_out, ffn2_norm, ffn2_w1, ffn2_w3, ffn2_w2, final_norm, loss_target, m_ffn1_norm, m_ffn1_w1, m_ffn1_w3, m_ffn1_w2, m_mix_norm, m_w_in, m_s5_lambda_re, m_s5_lambda_im, m_s5_log_dt, m_s5_b_re, m_s5_b_im, m_s5_c_re, m_s5_c_im, m_s5_d, m_s5_glu_w, m_s5_glu_b, m_gla_a_up_w, m_gla_a_up_b, m_gla_out_norm, m_proj_s5, m_proj_gla, m_w_out, m_ffn2_norm, m_ffn2_w1, m_ffn2_w3, m_ffn2_w2, m_final_norm, v_ffn1_norm, v_ffn1_w1, v_ffn1_w3, v_ffn1_w2, v_mix_norm, v_w_in, v_s5_lambda_re, v_s5_lambda_im, v_s5_log_dt, v_s5_b_re, v_s5_b_im, v_s5_c_re, v_s5_c_im, v_s5_d, v_s5_glu_w, v_s5_glu_b, v_gla_a_up_w, v_gla_a_up_b, v_gla_out_norm, v_proj_s5, v_proj_gla, v_w_out, v_ffn2_norm, v_ffn2_w1, v_ffn2_w3, v_ffn2_w2, v_final_norm):
    return _train_step(dict(locals()))
```

```python
import functools

import jax
import jax.numpy as jnp
from jax import lax
from jax.experimental import pallas as pl
from jax.experimental.pallas import tpu as pltpu

F32 = jnp.float32
BF16 = jnp.bfloat16
HI = lax.Precision.HIGHEST
MESH_ID = pl.DeviceIdType.MESH

D_MODEL = 1024
EPS = 1e-6
S5_G, S5_P, S5_H = 32, 64, 16
S5_W = S5_G * S5_H
S5_GP = S5_G * S5_P
SEG = 8
SCAN_ROWS = 256
GLA_HEADS, GLA_DK, GLA_DV = 4, 64, 128
GLA_CHUNK = 64
GLA_TAU = 16.0
GLA_RANK = 16
ADAM_LR, ADAM_B1, ADAM_B2, ADAM_EPS, ADAM_WD, ADAM_STEP = 0.001, 0.9, 0.999, 1e-08, 0.01, 10
V7X_VMEM_LIMIT = 56 * 1024 * 1024
LANE = 128

WEIGHTS = ['ffn1_norm', 'ffn1_w1', 'ffn1_w3', 'ffn1_w2', 'mix_norm', 'w_in', 's5_lambda_re', 's5_lambda_im',
           's5_log_dt', 's5_b_re', 's5_b_im', 's5_c_re', 's5_c_im', 's5_d', 's5_glu_w', 's5_glu_b', 'gla_a_up_w',
           'gla_a_up_b', 'gla_out_norm', 'proj_s5', 'proj_gla', 'w_out', 'ffn2_norm', 'ffn2_w1', 'ffn2_w3',
           'ffn2_w2', 'final_norm']
SHARDED = ['ffn1_w1', 'ffn1_w3', 'ffn1_w2', 'w_in', 's5_glu_w', 'proj_s5', 'proj_gla', 'w_out',
           'ffn2_w1', 'ffn2_w3', 'ffn2_w2']
COL_SHARDED = ['ffn1_w1', 'ffn1_w3', 'w_in', 'proj_s5', 'proj_gla', 'ffn2_w1', 'ffn2_w3', 'gla_a_up_w']
SMALL = [n for n in WEIGHTS if n not in SHARDED]
FFN_WEIGHTS = ['ffn1_w1', 'ffn1_w3', 'ffn1_w2', 'ffn2_w1', 'ffn2_w3', 'ffn2_w2']


def _params(**kw):
    return pltpu.CompilerParams(vmem_limit_bytes=V7X_VMEM_LIMIT, **kw)


class _Side:
    def __init__(self, ins, out_shapes, nsem, copies, aliased=False):
        self.ins, self.out_shapes, self.nsem, self.copies, self.aliased = list(ins), list(out_shapes), nsem, copies, aliased
        self.outs = None


_RIDER = []


def _pcall(body, **kw):
    if _RIDER:
        return _carry(body, _RIDER.pop(), **kw)
    return pl.pallas_call(body, **kw)


def _carry(body, side, *, name, grid, in_specs, out_specs, out_shape, scratch_shapes=(), compiler_params=None):
    del compiler_params
    single = not isinstance(out_shape, (list, tuple))
    out_specs = [out_specs] if single else list(out_specs)
    out_shape = [out_shape] if single else list(out_shape)
    n_in, n_out, n_scr = len(in_specs), len(out_shape), len(scratch_shapes)
    s_in, s_out = len(side.ins), len(side.out_shapes)
    any_spec = pl.BlockSpec(memory_space=pl.ANY)

    def wrapped(*refs):
        cuts = [n_in, s_in, n_out, s_out, n_scr]
        parts, pos = [], 0
        for c in cuts:
            parts.append(refs[pos:pos + c])
            pos += c
        ins, sins, outs, souts, scr = parts
        ssem, rsem = refs[pos], refs[pos + 1]
        first = last = None
        for d, g in enumerate(grid):
            i = pl.program_id(d)
            first = (i == 0) if first is None else first & (i == 0)
            last = (i == g - 1) if last is None else last & (i == g - 1)

        @pl.when(first)
        def _():
            for cp in side.copies(sins, souts, ssem, rsem):
                cp.start()

        body(*ins, *outs, *scr)

        @pl.when(last)
        def _():
            for cp in side.copies(sins, souts, ssem, rsem):
                cp.wait()

    call = pl.pallas_call(
        wrapped, name=name, grid=grid, in_specs=list(in_specs) + [any_spec] * s_in,
        out_specs=out_specs + [any_spec] * s_out, out_shape=out_shape + side.out_shapes,
        scratch_shapes=list(scratch_shapes) + [pltpu.SemaphoreType.DMA((side.nsem,)), pltpu.SemaphoreType.DMA((side.nsem,))],
        input_output_aliases={n_in + j: n_out + j for j in range(s_in)} if side.aliased else {},
        compiler_params=_params(has_side_effects=True))

    def run(*args):
        res = call(*args, *side.ins)
        side.outs = list(res[n_out:])
        return res[0] if single else list(res[:n_out])

    return run


def _run_side(side, name):
    s_in, s_out = len(side.ins), len(side.out_shapes)
    any_spec = pl.BlockSpec(memory_space=pl.ANY)

    def body(*refs):
        sins, souts = refs[:s_in], refs[s_in:s_in + s_out]
        ssem, rsem = refs[s_in + s_out:]
        cps = side.copies(sins, souts, ssem, rsem)
        for cp in cps:
            cp.start()
        for cp in cps:
            cp.wait()

    side.outs = list(pl.pallas_call(
        body, name=name, in_specs=[any_spec] * s_in, out_specs=[any_spec] * s_out, out_shape=side.out_shapes,
        scratch_shapes=[pltpu.SemaphoreType.DMA((side.nsem,)), pltpu.SemaphoreType.DMA((side.nsem,))],
        input_output_aliases={j: j for j in range(s_in)} if side.aliased else {},
        compiler_params=pltpu.CompilerParams(has_side_effects=True))(*side.ins))
    return side.outs


def _pick(n, cap, quantum):
    if n <= cap:
        return n
    best = None
    for t in range(quantum, cap + 1, quantum):
        if n % t == 0:
            best = t
    assert best is not None, (n, cap, quantum)
    return best


def _sigmoid(x):
    return jax.nn.sigmoid(x)


def _mm(a, b, *, name, ta=False, tb=False, out_dtype=F32, alpha=1.0, res=None, bias=None, exact=False, shard=None):
    ns = 4
    (k_a, m) = a.shape[-2:] if ta else a.shape[-2:][::-1]
    (k_b, n) = b.shape[-2:][::-1] if tb else b.shape[-2:]
    assert k_a == k_b, (a.shape, b.shape, ta, tb)
    assert (a.ndim == 3) == (shard in ('k', 'm')) and (b.ndim == 3) == (shard in ('n', 'k'))
    k = k_a
    tm = _pick(m, 1024, 128)
    tn = _pick(n, 1024, 128)
    tk = _pick(k, 1024, 128)
    pm, pn, pk = m // tm, n // tn, k // tk
    gm = pm * (ns if shard == 'm' else 1)
    gn = pn * (ns if shard == 'n' else 1)
    gk = pk * (ns if shard == 'k' else 1)
    dims = (((0,) if ta else (1,), (1,) if tb else (0,)), ((), ()))
    op_dtype = F32 if exact else BF16

    def body(*refs):
        a_ref, b_ref = refs[0], refs[1]
        pos = 2
        res_ref = bias_ref = None
        if res is not None:
            res_ref = refs[pos]
            pos += 1
        if bias is not None:
            bias_ref = refs[pos]
            pos += 1
        o_ref, acc_ref = refs[pos], refs[pos + 1]
        kk = pl.program_id(2)

        @pl.when(kk == 0)
        def _():
            acc_ref[...] = jnp.zeros_like(acc_ref)

        acc_ref[...] += lax.dot_general(a_ref[...].astype(op_dtype), b_ref[...].astype(op_dtype), dims,
                                        precision=HI if exact else None, preferred_element_type=F32)

        @pl.when(kk == gk - 1)
        def _():
            o = acc_ref[...]
            if alpha != 1.0:
                o = o * alpha
            if bias_ref is not None:
                o = o + bias_ref[...]
            if res_ref is not None:
                o = o + res_ref[...]
            o_ref[...] = o.astype(out_dtype)

    def spec(block, sharded_on, order):
        per = {'m': pm, 'n': pn, 'k': pk}

        def index(i, j, kk):
            g = {'m': i, 'n': j, 'k': kk}
            r, c = order(i % pm if shard == 'm' else i, j % pn if shard == 'n' else j, kk % pk if shard == 'k' else kk)
            if sharded_on is None:
                return (r, c)
            return (g[sharded_on] // per[sharded_on], r, c)

        return pl.BlockSpec(block if sharded_on is None else (None,) + block, index)

    a_sh = shard if shard in ('k', 'm') else None
    b_sh = shard if shard in ('n', 'k') else None
    o_sh = shard if shard in ('n', 'm') else None
    a_spec = spec((tk, tm), a_sh, lambda i, j, kk: (kk, i)) if ta else spec((tm, tk), a_sh, lambda i, j, kk: (i, kk))
    b_spec = spec((tn, tk), b_sh, lambda i, j, kk: (j, kk)) if tb else spec((tk, tn), b_sh, lambda i, j, kk: (kk, j))
    ins, in_specs = [a, b], [a_spec, b_spec]
    if res is not None:
        assert o_sh is None
        ins.append(res)
        in_specs.append(pl.BlockSpec((tm, tn), lambda i, j, kk: (i, j)))
    if bias is not None:
        assert o_sh is None
        ins.append(bias)
        in_specs.append(pl.BlockSpec((1, tn), lambda i, j, kk: (0, j)))
    out_shape = (m, n) if o_sh is None else (ns, m, n)
    return _pcall(body, name=name, grid=(gm, gn, gk), in_specs=in_specs,
                  out_specs=spec((tm, tn), o_sh, lambda i, j, kk: (i, j)),
                  out_shape=jax.ShapeDtypeStruct(out_shape, out_dtype),
                  scratch_shapes=[pltpu.VMEM((tm, tn), F32)], compiler_params=_params())(*ins)


def _rows(body, ins, outs, *, n, name, tm=256):
    tm = _pick(n, tm, 16)
    in_specs = []
    for arr, kind in ins:
        if kind == 'r':
            in_specs.append(pl.BlockSpec((tm, arr.shape[1]), lambda i: (i, 0)))
        else:
            in_specs.append(pl.BlockSpec(arr.shape, lambda i: (0, 0)))
    out_specs, out_shape = [], []
    for cols, dtype, kind in outs:
        if kind == 'r':
            out_specs.append(pl.BlockSpec((tm, cols), lambda i: (i, 0)))
            out_shape.append(jax.ShapeDtypeStruct((n, cols), dtype))
        else:
            out_specs.append(pl.BlockSpec((1, cols), lambda i: (0, 0)))
            out_shape.append(jax.ShapeDtypeStruct((1, cols), dtype))
    n_in = len(ins)
    acc_ids = [j for j, o in enumerate(outs) if o[2] == 'a']

    def wrapped(*refs):
        if acc_ids:
            @pl.when(pl.program_id(0) == 0)
            def _():
                for j in acc_ids:
                    refs[n_in + j][...] = jnp.zeros_like(refs[n_in + j])
        body(*refs)

    res = _pcall(wrapped, name=name, grid=(n // tm,), in_specs=in_specs, out_specs=out_specs, out_shape=out_shape,
                 compiler_params=_params())(*[a for a, _ in ins])
    return res


def _rms_fwd(x, g, name):
    def body(x_ref, g_ref, o_ref):
        xv = x_ref[...]
        rstd = lax.rsqrt(jnp.mean(xv * xv, axis=-1, keepdims=True) + EPS)
        o_ref[...] = (xv * rstd * g_ref[...]).astype(BF16)
    return _rows(body, [(x, 'r'), (g, 'f')], [(x.shape[1], BF16, 'r')], n=x.shape[0], name=name)[0]


def _rms_bwd(x, g, dn, dres, name):
    def body(x_ref, g_ref, dn_ref, dres_ref, dx_ref, dg_ref):
        xv = x_ref[...]
        rstd = lax.rsqrt(jnp.mean(xv * xv, axis=-1, keepdims=True) + EPS)
        xh = xv * rstd
        dn = dn_ref[...]
        dg_ref[...] += jnp.sum(dn * xh, axis=0, keepdims=True)
        dxh = dn * g_ref[...]
        dx_ref[...] = dres_ref[...] + rstd * (dxh - xh * jnp.mean(dxh * xh, axis=-1, keepdims=True))
    d = x.shape[1]
    return _rows(body, [(x, 'r'), (g, 'f'), (dn, 'r'), (dres, 'r')], [(d, F32, 'r'), (d, F32, 'a')],
                 n=x.shape[0], name=name)


def _gelu_parts(y):
    c0 = 0.7978845608028654
    inner = c0 * (y + 0.044715 * y * y * y)
    th = jnp.tanh(inner)
    return th, c0 * (1.0 + 3.0 * 0.044715 * y * y)


def _gelu_fwd(y, name):
    def body(y_ref, o_ref):
        yv = y_ref[...]
        th, _ = _gelu_parts(yv)
        o_ref[...] = 0.5 * yv * (1.0 + th)
    return _rows(body, [(y, 'r')], [(y.shape[1], F32, 'r')], n=y.shape[0], name=name)[0]


def _glu_fwd(zg, t, name):
    def body(z_ref, t_ref, o_ref):
        o_ref[...] = (z_ref[...] * _sigmoid(t_ref[...])).astype(BF16)
    return _rows(body, [(zg, 'r'), (t, 'r')], [(zg.shape[1], BF16, 'r')], n=zg.shape[0], name=name)[0]


def _glu_bwd1(dy, zg, t, name):
    def body(dy_ref, z_ref, t_ref, dz_ref, dt_ref, db_ref):
        dyv, zv = dy_ref[...], z_ref[...]
        sg = _sigmoid(t_ref[...])
        dz_ref[...] = dyv * sg
        dt = dyv * zv * sg * (1.0 - sg)
        dt_ref[...] = dt.astype(BF16)
        db_ref[...] += jnp.sum(dt, axis=0, keepdims=True)
    w = zg.shape[1]
    return _rows(body, [(dy, 'r'), (zg, 'r'), (t, 'r')], [(w, F32, 'r'), (w, BF16, 'r'), (w, F32, 'a')],
                 n=zg.shape[0], name=name)


def _glu_bwd2(dzg, ys, u, dskip, name):
    def body(dz_ref, y_ref, u_ref, d_ref, dy_ref, du_ref, dd_ref):
        yv = y_ref[...]
        th, dinner = _gelu_parts(yv)
        dy = dz_ref[...] * (0.5 * (1.0 + th) + 0.5 * yv * (1.0 - th * th) * dinner)
        dy_ref[...] = dy
        du_ref[...] = dy * d_ref[...]
        dd_ref[...] += jnp.sum(dy * u_ref[...], axis=0, keepdims=True)
    w = ys.shape[1]
    return _rows(body, [(dzg, 'r'), (ys, 'r'), (u, 'r'), (dskip, 'f')], [(w, F32, 'r'), (w, F32, 'r'), (w, F32, 'a')],
                 n=ys.shape[0], name=name)


def _scale_rows(u, dskip, name):
    def body(u_ref, d_ref, o_ref):
        o_ref[...] = u_ref[...] * d_ref[...]
    return _rows(body, [(u, 'r'), (dskip, 'f')], [(u.shape[1], F32, 'r')], n=u.shape[0], name=name)[0]


def _merge_fwd(zg, ps, pg, name):
    def body(z_ref, ps_ref, pg_ref, o_ref):
        zv = z_ref[...]
        o_ref[...] = (_sigmoid(zv[:, :D_MODEL]) * ps_ref[...] + _sigmoid(zv[:, D_MODEL:]) * pg_ref[...]).astype(BF16)
    return _rows(body, [(zg, 'r'), (ps, 'r'), (pg, 'r')], [(D_MODEL, BF16, 'r')], n=zg.shape[0], name=name)[0]


def _merge_bwd(dm, zg, ps, pg, name):
    def body(dm_ref, z_ref, ps_ref, pg_ref, dps_ref, dpg_ref, dz_ref):
        dmv, zv = dm_ref[...], z_ref[...]
        s1, s2 = _sigmoid(zv[:, :D_MODEL]), _sigmoid(zv[:, D_MODEL:])
        dps_ref[...] = (dmv * s1).astype(BF16)
        dpg_ref[...] = (dmv * s2).astype(BF16)
        dz_ref[:, :D_MODEL] = dmv * ps_ref[...] * s1 * (1.0 - s1)
        dz_ref[:, D_MODEL:] = dmv * pg_ref[...] * s2 * (1.0 - s2)
    return _rows(body, [(dm, 'r'), (zg, 'r'), (ps, 'r'), (pg, 'r')],
                 [(D_MODEL, BF16, 'r'), (D_MODEL, BF16, 'r'), (2 * D_MODEL, F32, 'r')], n=zg.shape[0], name=name)


def _final_loss(h, g, tgt, name):
    def body(h_ref, g_ref, t_ref, loss_ref, dh_ref, dg_ref):
        hv = h_ref[...]
        rstd = lax.rsqrt(jnp.mean(hv * hv, axis=-1, keepdims=True) + EPS)
        xh = hv * rstd
        err = xh * g_ref[...] - t_ref[...]
        part = 0.5 * jnp.sum(jnp.mean(err * err, axis=-1, keepdims=True), axis=0, keepdims=True)
        loss_ref[...] += jnp.broadcast_to(part, loss_ref.shape)
        dout = err * (1.0 / hv.shape[1])
        dg_ref[...] += jnp.sum(dout * xh, axis=0, keepdims=True)
        dxh = dout * g_ref[...]
        dh_ref[...] = rstd * (dxh - xh * jnp.mean(dxh * xh, axis=-1, keepdims=True))
    d = h.shape[1]
    return _rows(body, [(h, 'r'), (g, 'f'), (tgt, 'r')], [(LANE, F32, 'a'), (d, F32, 'r'), (d, F32, 'a')],
                 n=h.shape[0], name=name)


def _adamw_math(wv, gv, mv, vv):
    nm = ADAM_B1 * mv + (1.0 - ADAM_B1) * gv
    nv = ADAM_B2 * vv + (1.0 - ADAM_B2) * (gv * gv)
    m_hat = nm / (1.0 - ADAM_B1 ** ADAM_STEP)
    v_hat = nv / (1.0 - ADAM_B2 ** ADAM_STEP)
    return -ADAM_LR * (m_hat / (jnp.sqrt(v_hat) + ADAM_EPS) + ADAM_WD * wv), nm, nv


def _adamw(w, g, m, v, name):
    def body(w_ref, g_ref, m_ref, v_ref, d_ref, nm_ref, nv_ref):
        d_ref[...], nm_ref[...], nv_ref[...] = _adamw_math(w_ref[...], g_ref[...], m_ref[...], v_ref[...])
    c = w.shape[1]
    return _rows(body, [(w, 'r'), (g, 'r'), (m, 'r'), (v, 'r')], [(c, F32, 'r')] * 3, n=w.shape[0], name=name)


def _adamw_halves(w, g_own, g_sib, m, v, c_arr, name):
    r, cols = w.shape
    h = r // 2
    tr = _pick(h, 256, 8)
    per = h // tr

    def body(c_ref, w_ref, go_ref, gs_ref, m_ref, v_ref, g_ref, d_ref, nm_ref, nv_ref):
        mine = (pl.program_id(0) // per) == c_ref[0]
        gv = jnp.where(mine, go_ref[...], gs_ref[...])
        g_ref[...] = gv
        d_ref[...], nm_ref[...], nv_ref[...] = _adamw_math(w_ref[...], gv, m_ref[...], v_ref[...])

    full = pl.BlockSpec((tr, cols), lambda i, c_ref: (i, 0))
    half = pl.BlockSpec((tr, cols), lambda i, c_ref: (i % per, 0))
    grid_spec = pltpu.PrefetchScalarGridSpec(num_scalar_prefetch=1, grid=(2 * per,),
                                             in_specs=[full, half, half, full, full], out_specs=[full] * 4)
    return _pcall(body, name=name, grid_spec=grid_spec, out_shape=[jax.ShapeDtypeStruct((r, cols), F32)] * 4,
                  compiler_params=_params())(c_arr, w, g_own, g_sib, m, v)


def _shift_rows(v, sh, down):
    rolled = pltpu.roll(v, sh if down else v.shape[0] - sh, axis=0)
    row = lax.broadcasted_iota(jnp.int32, v.shape, 0)
    keep = (row >= sh) if down else (row < v.shape[0] - sh)
    return jnp.where(keep, rolled, 0.0)


def _chain_segments(st_r, st_i, pw_r_ref, pw_i_ref, conj, down):
    vr, vi = st_r[...], st_i[...]
    sh, k = 1, 0
    while sh < SEG:
        pr, pi = pw_r_ref[k:k + 1, :], pw_i_ref[k:k + 1, :]
        if conj:
            pi = -pi
        sr, si = _shift_rows(vr, sh, down), _shift_rows(vi, sh, down)
        vr, vi = vr + pr * sr - pi * si, vi + pr * si + pi * sr
        sh, k = sh * 2, k + 1
    st_r[...] = _shift_rows(vr, 1, down)
    st_i[...] = _shift_rows(vi, 1, down)


def _s5_scan(bu, ar8, ai8, pw_r, pw_i, name):
    n = bu.shape[0]
    rb = SCAN_ROWS
    nb, steps, lc = n // rb, rb // SEG, 512

    def body(bu_ref, ar_ref, ai_ref, pwr_ref, pwi_ref, x_ref, st_r, st_i):
        ph, b = pl.program_id(0), pl.program_id(1)

        @pl.when((ph == 0) & (b == 0))
        def _():
            st_r[...] = jnp.zeros_like(st_r)
            st_i[...] = jnp.zeros_like(st_i)

        def scan(store):
            for c in range(S5_GP // lc):
                re, im = slice(c * lc, (c + 1) * lc), slice(S5_GP + c * lc, S5_GP + (c + 1) * lc)
                a_r, a_i = ar_ref[:, re], ai_ref[:, re]

                def step(s, carry):
                    xr, xi = carry
                    rows = pl.ds(pl.multiple_of(s * SEG, SEG), SEG)
                    nr = a_r * xr - a_i * xi + bu_ref[rows, re]
                    ni = a_r * xi + a_i * xr + bu_ref[rows, im]
                    if store:
                        x_ref[rows, re] = nr
                        x_ref[rows, im] = ni
                    return nr, ni

                xr, xi = lax.fori_loop(0, steps, step, (st_r[:, re], st_i[:, re]), unroll=4)
                st_r[:, re] = xr
                st_i[:, re] = xi

        @pl.when(ph == 0)
        def _():
            scan(False)

        @pl.when((ph == 0) & (b == nb - 1))
        def _():
            _chain_segments(st_r, st_i, pwr_ref, pwi_ref, conj=False, down=True)

        @pl.when(ph == 1)
        def _():
            scan(True)

    full = lambda a: pl.BlockSpec(a.shape, lambda ph, b: (0, 0))
    return _pcall(body, name=name, grid=(2, nb),
                  in_specs=[pl.BlockSpec((rb, 2 * S5_GP), lambda ph, b: (b, 0)), full(ar8), full(ai8), full(pw_r), full(pw_i)],
                  out_specs=pl.BlockSpec((rb, 2 * S5_GP), lambda ph, b: (b * ph, 0)),
                  out_shape=jax.ShapeDtypeStruct((n, 2 * S5_GP), F32),
                  scratch_shapes=[pltpu.VMEM((SEG, S5_GP), F32), pltpu.VMEM((SEG, S5_GP), F32)],
                  compiler_params=_params())(bu, ar8, ai8, pw_r, pw_i)


def _s5_scan_bwd(gx, xs, ar8, ai8, pw_r, pw_i, name):
    n = gx.shape[0]
    rb = SCAN_ROWS
    nb, steps, lc = n // rb, rb // SEG, 256

    def body(gx_ref, x_ref, ar_ref, ai_ref, pwr_ref, pwi_ref, lam_ref, da_ref, st_r, st_i):
        ph, b = pl.program_id(0), pl.program_id(1)

        @pl.when((ph == 0) & (b == 0))
        def _():
            st_r[...] = jnp.zeros_like(st_r)
            st_i[...] = jnp.zeros_like(st_i)
            da_ref[...] = jnp.zeros_like(da_ref)

        def scan(store):
            for c in range(S5_GP // lc):
                re, im = slice(c * lc, (c + 1) * lc), slice(S5_GP + c * lc, S5_GP + (c + 1) * lc)
                a_r, a_i = ar_ref[:, re], ai_ref[:, re]

                def step(s, carry):
                    rows = pl.ds(pl.multiple_of((steps - 1 - s) * SEG, SEG), SEG)
                    if store:
                        lr, li, dr, di = carry
                        xr, xi = x_ref[rows, re], x_ref[rows, im]
                        dr = dr + lr * xr + li * xi
                        di = di + li * xr - lr * xi
                    else:
                        lr, li = carry
                    nr = a_r * lr + a_i * li + gx_ref[rows, re]
                    ni = a_r * li - a_i * lr + gx_ref[rows, im]
                    if store:
                        lam_ref[rows, re] = nr
                        lam_ref[rows, im] = ni
                        return nr, ni, dr, di
                    return nr, ni

                if store:
                    lr, li, dr, di = lax.fori_loop(0, steps, step, (st_r[:, re], st_i[:, re], da_ref[:, re], da_ref[:, im]),
                                                   unroll=4)
                    da_ref[:, re] = dr
                    da_ref[:, im] = di
                else:
                    lr, li = lax.fori_loop(0, steps, step, (st_r[:, re], st_i[:, re]), unroll=4)
                st_r[:, re] = lr
                st_i[:, re] = li

        @pl.when(ph == 0)
        def _():
            scan(False)

        @pl.when((ph == 0) & (b == nb - 1))
        def _():
            _chain_segments(st_r, st_i, pwr_ref, pwi_ref, conj=True, down=False)

        @pl.when(ph == 1)
        def _():
            scan(True)

    full = lambda a: pl.BlockSpec(a.shape, lambda ph, b: (0, 0))
    rev = lambda ph, b: (nb - 1 - b, 0)
    return _pcall(body, name=name, grid=(2, nb),
                  in_specs=[pl.BlockSpec((rb, 2 * S5_GP), rev), pl.BlockSpec((rb, 2 * S5_GP), lambda ph, b: ((nb - 1 - b) * ph, 0)),
                            full(ar8), full(ai8), full(pw_r), full(pw_i)],
                  out_specs=[pl.BlockSpec((rb, 2 * S5_GP), lambda ph, b: (nb - 1 - b * ph, 0)),
                             pl.BlockSpec((SEG, 2 * S5_GP), lambda ph, b: (0, 0))],
                  out_shape=[jax.ShapeDtypeStruct((n, 2 * S5_GP), F32), jax.ShapeDtypeStruct((SEG, 2 * S5_GP), F32)],
                  scratch_shapes=[pltpu.VMEM((SEG, S5_GP), F32), pltpu.VMEM((SEG, S5_GP), F32)],
                  compiler_params=_params())(gx, xs, ar8, ai8, pw_r, pw_i)


def _s5_discretize(lam_re, lam_im, log_dt, b_re, b_im):
    dt = jnp.exp(log_dt)[:, None]
    mag = jnp.exp(lam_re * dt)
    ar = mag * jnp.cos(lam_im * dt)
    ai = mag * jnp.sin(lam_im * dt)
    den = lam_re * lam_re + lam_im * lam_im
    nr = ar - 1.0
    fr = (nr * lam_re + ai * lam_im) / den
    fi = (ai * lam_re - nr * lam_im) / den
    bbar_re = fr[:, :, None] * b_re - fi[:, :, None] * b_im
    bbar_im = fr[:, :, None] * b_im + fi[:, :, None] * b_re
    return ar, ai, bbar_re, bbar_im


BD_TILES, BD_CH, BD_ST, BD_GROUPS = 8, 128, 512, 8


def _bd_tiles(re, im):
    eye = jnp.eye(BD_GROUPS, dtype=re.dtype)

    def tiles(t):
        t = t.reshape(S5_G // BD_GROUPS, BD_GROUPS, S5_H, S5_P)
        return (t[:, :, :, None, :] * eye[None, :, None, :, None]).reshape(S5_G // BD_GROUPS, BD_CH, BD_ST)

    return jnp.concatenate([tiles(re), tiles(im)], axis=0)


def _bd_blocks(t):
    t = t.reshape(2, S5_G // BD_GROUPS, BD_GROUPS, S5_H, BD_GROUPS, S5_P)
    return jnp.einsum('rkahap->rkahp', t).reshape(2, S5_G, S5_H, S5_P)


def _bd_expand(a, t, name):
    n = a.shape[0]
    tm = _pick(n, 1024, 16)

    def body(a_ref, t_ref, o_ref):
        o_ref[...] = _dot(a_ref[...], t_ref[...])

    return _pcall(body, name=name, grid=(n // tm, BD_TILES),
                  in_specs=[pl.BlockSpec((tm, BD_CH), lambda i, j: (i, j % 4)), pl.BlockSpec((None, BD_CH, BD_ST), lambda i, j: (j, 0, 0))],
                  out_specs=pl.BlockSpec((tm, BD_ST), lambda i, j: (i, j)),
                  out_shape=jax.ShapeDtypeStruct((n, BD_TILES * BD_ST), F32), compiler_params=_params())(a, t)


def _bd_reduce(x, t, res, name):
    n = x.shape[0]
    tm = _pick(n, 1024, 16)

    def body(x_ref, t_ref, r_ref, o_ref):
        part = _dot(x_ref[...], t_ref[...], NT)

        @pl.when(pl.program_id(2) == 0)
        def _():
            o_ref[...] = r_ref[...] + part

        @pl.when(pl.program_id(2) == 1)
        def _():
            o_ref[...] += part

    return _pcall(body, name=name, grid=(n // tm, 4, 2),
                  in_specs=[pl.BlockSpec((tm, BD_ST), lambda i, k, r: (i, k + 4 * r)),
                            pl.BlockSpec((None, BD_CH, BD_ST), lambda i, k, r: (k + 4 * r, 0, 0)),
                            pl.BlockSpec((tm, BD_CH), lambda i, k, r: (i, k))],
                  out_specs=pl.BlockSpec((tm, BD_CH), lambda i, k, r: (i, k)),
                  out_shape=jax.ShapeDtypeStruct((n, S5_W), F32), compiler_params=_params())(x, t, res)


def _bd_outer(a, x, name):
    n = a.shape[0]
    tk = _pick(n, 1024, 16)
    nk = n // tk

    def body(a_ref, x_ref, o_ref):
        part = _dot(a_ref[...], x_ref[...], TN)

        @pl.when(pl.program_id(1) == 0)
        def _():
            o_ref[...] = part

        @pl.when(pl.program_id(1) > 0)
        def _():
            o_ref[...] += part

    return _pcall(body, name=name, grid=(BD_TILES, nk),
                  in_specs=[pl.BlockSpec((tk, BD_CH), lambda j, kk: (kk, j % 4)), pl.BlockSpec((tk, BD_ST), lambda j, kk: (kk, j))],
                  out_specs=pl.BlockSpec((None, BD_CH, BD_ST), lambda j, kk: (j, 0, 0)),
                  out_shape=jax.ShapeDtypeStruct((BD_TILES, BD_CH, BD_ST), F32), compiler_params=_params())(a, x)


def _permute_rows(t):
    n = t.shape[0]
    return t.reshape(SEG, n // SEG, t.shape[1]).transpose(1, 0, 2).reshape(n, t.shape[1])


def _unpermute_rows(t):
    n = t.shape[0]
    return t.reshape(n // SEG, SEG, t.shape[1]).transpose(1, 0, 2).reshape(n, t.shape[1])


def _segment_powers(ar, ai, seg_steps):
    pr, pi = ar.reshape(1, S5_GP), ai.reshape(1, S5_GP)
    e = 1
    while e < seg_steps:
        pr, pi = pr * pr - pi * pi, 2.0 * pr * pi
        e *= 2
    assert e == seg_steps, "segment length must be a power of two"
    rows_r, rows_i = [], []
    for _ in range(3):
        rows_r.append(pr)
        rows_i.append(pi)
        pr, pi = pr * pr - pi * pi, 2.0 * pr * pi
    pad = jnp.zeros((SEG - 3, S5_GP), F32)
    return jnp.concatenate(rows_r + [pad], axis=0), jnp.concatenate(rows_i + [pad], axis=0)


NT = (((1,), (1,)), ((), ()))
TN = (((0,), (0,)), ((), ()))


def _dot(a, b, dims=None, exact=False):
    dims = (((1,), (0,)), ((), ())) if dims is None else dims
    if exact:
        return lax.dot_general(a, b, dims, precision=HI, preferred_element_type=F32)
    return lax.dot_general(a.astype(BF16), b.astype(BF16), dims, preferred_element_type=F32)


def _dot01(a, b, dims=None, ones_first=True):
    x = b if ones_first else a
    hi = x.astype(BF16)
    r1 = x - hi.astype(F32)
    mid = r1.astype(BF16)
    lo = (r1 - mid.astype(F32)).astype(BF16)
    parts = [(_dot(a, p, dims) if ones_first else _dot(p, b, dims)) for p in (lo, mid, hi)]
    return (parts[0] + parts[1]) + parts[2]


HEADS = range(4)


def _gla_chunk_fwd(qc, kc, vc, al, wup, bup, s_prev, tril):
    ones = jnp.ones((GLA_CHUNK, GLA_DV), F32)
    z = [_dot(al, wup[h]) + bup[h] for h in HEADS]
    la = [(jnp.minimum(z[h], 0.0) - jnp.log(1.0 + jnp.exp(-jnp.abs(z[h])))) * (1.0 / GLA_TAU) for h in HEADS]
    bc = [_dot01(tril, la[h]) for h in HEADS]
    blb = [_dot01(la[h], ones, TN, ones_first=False) for h in HEADS]
    bl = [bc[h][GLA_CHUNK - 1:GLA_CHUNK, :] for h in HEADS]
    ebc = [jnp.exp(bc[h]) for h in HEADS]
    qt = [qc[h] * (GLA_DK ** -0.5) * ebc[h] for h in HEADS]
    kt = [kc[h] * jnp.exp(-bc[h]) for h in HEADS]
    ke = [kc[h] * jnp.exp(bl[h] - bc[h]) for h in HEADS]
    sc = [_dot(qt[h], kt[h], NT) * tril for h in HEADS]
    oi = [_dot(sc[h], vc[h]) for h in HEADS]
    oo = [_dot(qt[h], s_prev[h]) for h in HEADS]
    o = [oi[h] + oo[h] for h in HEADS]
    return z, bc, bl, blb, ebc, qt, kt, ke, sc, o


GLA_ROWS = 512
GLA_CPB = GLA_ROWS // GLA_CHUNK


ZA_COLS = 5 * 512
SLOT = 128


def _pad_heads(w):
    r = w.shape[0]
    return jnp.pad(w.reshape(r, GLA_HEADS, GLA_DK), ((0, 0), (0, 0), (0, SLOT - GLA_DK))).reshape(r, GLA_HEADS * SLOT)


def _unpad_heads(w):
    r = w.shape[0]
    return w.reshape(r, GLA_HEADS, SLOT)[:, :, :GLA_DK].reshape(r, GLA_HEADS * GLA_DK)


def _gla_token_specs(blk):
    col = lambda cb: pl.BlockSpec((GLA_ROWS, 512), lambda j: (blk(j), cb))
    whole = lambda a: pl.BlockSpec(a.shape, lambda j: (0,) * a.ndim)
    return col, whole


def _head_ds(h, width):
    return pl.ds(h * SLOT, width)


def _tri(lower):
    ri = lax.broadcasted_iota(jnp.int32, (GLA_CHUNK, GLA_CHUNK), 0)
    ci = lax.broadcasted_iota(jnp.int32, (GLA_CHUNK, GLA_CHUNK), 1)
    return ((ri >= ci) if lower else (ri <= ci)).astype(F32)


def _gla_fwd(za, al, wup, bup, gn, name):
    n = za.shape[0]
    nc = n // GLA_CHUNK

    def body(q_ref, k_ref, v_ref, r_ref, al_ref, wup_ref, bup_ref, gn_ref, y_ref, sp_ref, s_ref):
        @pl.when(pl.program_id(0) == 0)
        def _():
            s_ref[...] = jnp.zeros_like(s_ref)

        tril = _tri(True)

        def chunk(c, carry):
            rows = pl.ds(pl.multiple_of(c * GLA_CHUNK, GLA_CHUNK), GLA_CHUNK)
            alc = al_ref[rows, :]
            vc = [v_ref[rows, _head_ds(h, GLA_DV)] for h in HEADS]
            s_prev = [s_ref[h] for h in HEADS]
            _, _, _, blb, _, _, _, ke, _, o = _gla_chunk_fwd(
                [q_ref[rows, _head_ds(h, GLA_DK)] for h in HEADS], [k_ref[rows, _head_ds(h, GLA_DK)] for h in HEADS],
                vc, alc, [wup_ref[h] for h in HEADS], [bup_ref[h] for h in HEADS], s_prev, tril)
            ds = [_dot(ke[h], vc[h], TN) for h in HEADS]
            for h in HEADS:
                rc = r_ref[rows, _head_ds(h, GLA_DV)]
                sp_ref[h, c] = s_prev[h]
                rstd = lax.rsqrt(jnp.mean(o[h] * o[h], axis=-1, keepdims=True) + EPS)
                y_ref[rows, _head_ds(h, GLA_DV)] = (o[h] * rstd * gn_ref[h] * (rc * _sigmoid(rc))).astype(BF16)
                s_ref[h] = jnp.exp(blb[h]) * s_prev[h] + ds[h]
            return carry

        lax.fori_loop(0, GLA_CPB, chunk, 0)

    col, whole = _gla_token_specs(lambda j: j)
    return _pcall(body, name=name, grid=(n // GLA_ROWS,),
                  in_specs=[col(1), col(2), col(3), col(4), pl.BlockSpec((GLA_ROWS, LANE), lambda j: (j, 0)),
                            whole(wup), whole(bup), whole(gn)],
                  out_specs=[pl.BlockSpec((GLA_ROWS, GLA_HEADS * GLA_DV), lambda j: (j, 0)),
                             pl.BlockSpec((GLA_HEADS, GLA_CPB, GLA_DK, GLA_DV), lambda j: (0, j, 0, 0))],
                  out_shape=[jax.ShapeDtypeStruct((n, GLA_HEADS * GLA_DV), BF16),
                             jax.ShapeDtypeStruct((GLA_HEADS, nc, GLA_DK, GLA_DV), F32)],
                  scratch_shapes=[pltpu.VMEM((GLA_HEADS, GLA_DK, GLA_DV), F32)],
                  compiler_params=_params())(za, za, za, za, al, wup, bup, gn)


def _gla_bwd(za, al, wup, bup, gn, sp, dy, du_s5, name):
    n = za.shape[0]
    nb = n // GLA_ROWS

    def body(q_ref, k_ref, v_ref, r_ref, al_ref, wup_ref, bup_ref, gn_ref, dy_ref, dus_ref, sp_ref,
             dza_ref, dz_ref, dgn_ref, dbup_ref, ds_ref):
        @pl.when(pl.program_id(0) == 0)
        def _():
            ds_ref[...] = jnp.zeros_like(ds_ref)
            dgn_ref[...] = jnp.zeros_like(dgn_ref)
            dbup_ref[...] = jnp.zeros_like(dbup_ref)

        tril, triu = _tri(True), _tri(False)
        dza_ref[:, 0:512] = dus_ref[...]
        dza_ref[:, 512:1536] = jnp.zeros((GLA_ROWS, 1024), F32)
        dz_ref[...] = jnp.zeros_like(dz_ref)

        def chunk(i, carry):
            c = GLA_CPB - 1 - i
            rows = pl.ds(pl.multiple_of(c * GLA_CHUNK, GLA_CHUNK), GLA_CHUNK)
            alc = al_ref[rows, :]
            qc = [q_ref[rows, _head_ds(h, GLA_DK)] for h in HEADS]
            kc = [k_ref[rows, _head_ds(h, GLA_DK)] for h in HEADS]
            vc = [v_ref[rows, _head_ds(h, GLA_DV)] for h in HEADS]
            s_prev = [sp_ref[h, c] for h in HEADS]
            ds = [ds_ref[h] for h in HEADS]
            z, bc, bl, blb, ebc, qt, kt, ke, sc, o = _gla_chunk_fwd(
                qc, kc, vc, alc, [wup_ref[h] for h in HEADS], [bup_ref[h] for h in HEADS], s_prev, tril)
            do = []
            for h in HEADS:
                rc = r_ref[rows, _head_ds(h, GLA_DV)]
                rs = lax.rsqrt(jnp.mean(o[h] * o[h], axis=-1, keepdims=True) + EPS)
                on = o[h] * rs
                sr = _sigmoid(rc)
                sil = rc * sr
                dyv, gnv = dy_ref[rows, _head_ds(h, GLA_DV)], gn_ref[h]
                dgn_ref[h] += jnp.sum(dyv * on * sil, axis=0, keepdims=True)
                dza_ref[rows, pl.ds(2048 + h * SLOT, GLA_DV)] = dyv * on * gnv * (sr * (1.0 + rc * (1.0 - sr)))
                don = dyv * gnv * sil
                do.append(rs * (don - on * jnp.mean(don * on, axis=-1, keepdims=True)))
            dp = [_dot(do[h], vc[h], NT) * tril for h in HEADS]
            dv1 = [_dot(sc[h], do[h], TN) for h in HEADS]
            dv2 = [_dot(ke[h], ds[h]) for h in HEADS]
            dq2 = [_dot(do[h], s_prev[h], NT) for h in HEADS]
            dke = [_dot(vc[h], ds[h], NT) for h in HEADS]
            ddec = [_dot01(jnp.ones((8, GLA_DV), F32), ds[h] * s_prev[h], NT)[0:1, :] for h in HEADS]
            dsn = [_dot(qt[h], do[h], TN) for h in HEADS]
            dq1 = [_dot(dp[h], kt[h]) for h in HEADS]
            dkt = [_dot(dp[h], qt[h], TN) for h in HEADS]
            dbc, dbl = [], []
            for h in HEADS:
                dqt = dq1[h] + dq2[h]
                dza_ref[rows, pl.ds(1536 + h * SLOT, GLA_DV)] = dv1[h] + dv2[h]
                ds_ref[h] = jnp.exp(blb[h]) * ds[h] + dsn[h]
                dza_ref[rows, pl.ds(512 + h * SLOT, GLA_DK)] = dqt * (GLA_DK ** -0.5) * ebc[h]
                dza_ref[rows, pl.ds(1024 + h * SLOT, GLA_DK)] = dkt[h] * jnp.exp(-bc[h]) + dke[h] * jnp.exp(bl[h] - bc[h])
                dbc.append(dqt * qt[h] - dkt[h] * kt[h] - dke[h] * ke[h])
                dbl.append(jnp.sum(dke[h] * ke[h], axis=0, keepdims=True) + ddec[h] * jnp.exp(bl[h]))
            dla = [_dot01(triu, dbc[h]) + dbl[h] for h in HEADS]
            for h in HEADS:
                dz = dla[h] * (1.0 - _sigmoid(z[h])) * (1.0 / GLA_TAU)
                dz_ref[rows, _head_ds(h, GLA_DK)] = dz
                dbup_ref[h] += jnp.sum(dz, axis=0, keepdims=True)
            return carry

        lax.fori_loop(0, GLA_CPB, chunk, 0)

    rev = lambda j: nb - 1 - j
    col, whole = _gla_token_specs(rev)
    tok = lambda w: pl.BlockSpec((GLA_ROWS, w), lambda j: (rev(j), 0))
    h1 = lambda w: pl.BlockSpec((GLA_HEADS, 1, w), lambda j: (0, 0, 0))
    s1 = lambda w: jax.ShapeDtypeStruct((GLA_HEADS, 1, w), F32)
    return _pcall(body, name=name, grid=(nb,),
                  in_specs=[col(1), col(2), col(3), col(4), tok(LANE), whole(wup), whole(bup), whole(gn), tok(512), tok(512),
                            pl.BlockSpec((GLA_HEADS, GLA_CPB, GLA_DK, GLA_DV), lambda j: (0, rev(j), 0, 0))],
                  out_specs=[tok(ZA_COLS), tok(GLA_HEADS * SLOT), h1(GLA_DV), h1(GLA_DK)],
                  out_shape=[jax.ShapeDtypeStruct((n, ZA_COLS), F32), jax.ShapeDtypeStruct((n, GLA_HEADS * SLOT), F32),
                             s1(GLA_DV), s1(GLA_DK)],
                  scratch_shapes=[pltpu.VMEM((GLA_HEADS, GLA_DK, GLA_DV), F32)],
                  compiler_params=_params())(za, za, za, za, al, wup, bup, gn, dy, du_s5, sp)


ANY = pl.BlockSpec(memory_space=pl.ANY)


def _place():
    x, y, c = lax.axis_index("x"), lax.axis_index("y"), lax.axis_index("c")
    chips = [(1 - x, y), (x, 1 - y), (1 - x, 1 - y)]
    return x, y, c, chips


def _remote(src, dst, ssem, rsem, dev):
    return pltpu.make_async_remote_copy(src_ref=src, dst_ref=dst, send_sem=ssem, recv_sem=rsem, device_id=dev,
                                        device_id_type=MESH_ID)


def _half(c, rows):
    h = rows // 2
    return pl.ds(pl.multiple_of(c * h, 8), h)


def _side_gather_ici(shards):
    def copies(ins, outs, ssem, rsem):
        x, y, c, chips = _place()
        mine = 2 * x + y
        cps = []
        for w in range(len(ins)):
            half = _half(c, ins[w].shape[0])
            cps.append(_remote(ins[w], outs[w].at[mine], ssem.at[4 * w], rsem.at[4 * w], (x, y, 1 - c)))
            for k, (px, py) in enumerate(chips):
                cps.append(_remote(ins[w].at[half], outs[w].at[mine, half], ssem.at[4 * w + 1 + k], rsem.at[4 * w + 1 + k],
                                   (px, py, c)))
        return cps

    return _Side(shards, [jax.ShapeDtypeStruct((4,) + s.shape, s.dtype) for s in shards], 4 * len(shards), copies)


def _side_gather_d2d(gathered):
    def copies(ins, outs, ssem, rsem):
        x, y, c, chips = _place()
        cps = []
        for w in range(len(outs)):
            half = _half(c, outs[w].shape[1])
            for k, (px, py) in enumerate(chips):
                theirs = outs[w].at[2 * px + py, half]
                cps.append(_remote(theirs, theirs, ssem.at[3 * w + k], rsem.at[3 * w + k], (x, y, 1 - c)))
        return cps

    return _Side(gathered, [jax.ShapeDtypeStruct(g.shape, g.dtype) for g in gathered], 3 * len(gathered), copies,
                 aliased=True)


def _side_swap_halves(grads):
    def copies(ins, outs, ssem, rsem):
        x, y, c, _ = _place()
        return [_remote(ins[w].at[:, _half(1 - c, ins[w].shape[1]), :], outs[w], ssem.at[w], rsem.at[w], (x, y, 1 - c))
                for w in range(len(ins))]

    return _Side(grads, [jax.ShapeDtypeStruct((4, g.shape[1] // 2, g.shape[2]), g.dtype) for g in grads], len(grads), copies)


def _side_scatter(sums):
    def copies(ins, outs, ssem, rsem):
        x, y, c, chips = _place()
        return [_remote(ins[w].at[2 * px + py], outs[w].at[k], ssem.at[3 * w + k], rsem.at[3 * w + k], (px, py, c))
                for w in range(len(ins)) for k, (px, py) in enumerate(chips)]

    return _Side(sums, [jax.ShapeDtypeStruct((3,) + s.shape[1:], s.dtype) for s in sums], 3 * len(sums), copies)


def _side_swap_reduced(halves):
    def copies(ins, outs, ssem, rsem):
        x, y, c, _ = _place()
        return [_remote(ins[w], outs[w], ssem.at[w], rsem.at[w], (x, y, 1 - c)) for w in range(len(ins))]

    return _Side(halves, [jax.ShapeDtypeStruct(h.shape, h.dtype) for h in halves], len(halves), copies)


def _chip_sum(g, recv, c_arr, name):
    _, r, cols = g.shape
    h = r // 2
    tr = _pick(h, 256, 16)
    g4 = g.reshape(4, 2, h, cols)

    def body(c_ref, g_ref, r_ref, o_ref):
        o_ref[...] = (g_ref[...] + r_ref[...]).astype(BF16)

    grid_spec = pltpu.PrefetchScalarGridSpec(
        num_scalar_prefetch=1, grid=(4, h // tr),
        in_specs=[pl.BlockSpec((None, None, tr, cols), lambda s, i, c_ref: (s, c_ref[0], i, 0)),
                  pl.BlockSpec((None, tr, cols), lambda s, i, c_ref: (s, i, 0))],
        out_specs=pl.BlockSpec((None, tr, cols), lambda s, i, c_ref: (s, i, 0)))
    return _pcall(body, name=name, grid_spec=grid_spec, out_shape=jax.ShapeDtypeStruct((4, h, cols), BF16),
                  compiler_params=_params())(c_arr, g4, recv)


def _owner_sum(sums, others, s_arr, name):
    _, h, cols = sums.shape
    tr = _pick(h, 256, 16)

    def body(s_ref, a_ref, o_ref, out_ref):
        f = lambda v: v.astype(F32)
        out_ref[...] = (f(a_ref[...]) + f(o_ref[0])) + (f(o_ref[1]) + f(o_ref[2]))

    grid_spec = pltpu.PrefetchScalarGridSpec(
        num_scalar_prefetch=1, grid=(h // tr,),
        in_specs=[pl.BlockSpec((None, tr, cols), lambda i, s_ref: (s_ref[0], i, 0)),
                  pl.BlockSpec((3, tr, cols), lambda i, s_ref: (0, i, 0))],
        out_specs=pl.BlockSpec((tr, cols), lambda i, s_ref: (i, 0)))
    return _pcall(body, name=name, grid_spec=grid_spec, out_shape=jax.ShapeDtypeStruct((h, cols), F32),
                  compiler_params=_params())(s_arr, sums, others)


def _allreduce_small(v, name):
    def body(v_ref, o_ref, r0, r1, ssem, rsem):
        x, y, c, chips = _place()
        cp = _remote(v_ref, r0, ssem.at[0], rsem.at[0], (x, y, 1 - c))
        cp.start()
        cp.wait()
        o_ref[...] = v_ref[...] + r0[...]
        cps = []
        for k, (px, py) in enumerate(chips):
            cp = _remote(o_ref, r1.at[k], ssem.at[1 + k], rsem.at[1 + k], (px, py, c))
            cp.start()
            cps.append(cp)
        for cp in cps:
            cp.wait()
        o_ref[...] = (o_ref[...] + r1[0]) + (r1[1] + r1[2])

    vm = pl.BlockSpec(memory_space=pltpu.VMEM)
    return _pcall(body, name=name, in_specs=[vm], out_specs=vm, out_shape=jax.ShapeDtypeStruct(v.shape, F32),
                  scratch_shapes=[pltpu.VMEM(v.shape, F32), pltpu.VMEM((3,) + v.shape, F32),
                                  pltpu.SemaphoreType.DMA((4,)), pltpu.SemaphoreType.DMA((4,))],
                  compiler_params=_params(has_side_effects=True))(v)


def _tile_rows(size):
    return -(-size // (8 * LANE)) * 8


def _pack_small(parts):
    pieces = []
    for p in parts:
        flat = p.reshape(-1).astype(F32)
        pieces.append(jnp.pad(flat, (0, _tile_rows(p.size) * LANE - p.size)).reshape(-1, LANE))
    rows = sum(x.shape[0] for x in pieces)
    pieces.append(jnp.zeros(((-rows) % 64, LANE), F32))
    return jnp.concatenate(pieces, axis=0)


def _unpack_small(packed, like):
    out, pos = [], 0
    for p in like:
        rows = _tile_rows(p.size)
        out.append(packed[pos:pos + rows].reshape(-1)[:p.size].reshape(p.shape))
        pos += rows
    return out


FFN_FWD_ROWS, FFN_BWD_ROWS = 1024, 512
FFN_SUB_ROWS = 256


def _ffn_specs(n, d, fs, cap):
    rows = _pick(n, cap, 16)
    row = pl.BlockSpec((rows, d), lambda i, s: (i, 0))
    gain = pl.BlockSpec((1, d), lambda i, s: (0, 0))
    w_row = pl.BlockSpec((None, fs, d), lambda i, s: (s, 0, 0))
    hid = pl.BlockSpec((None, rows, fs), lambda i, s: (s, i, 0))
    return rows, row, gain, w_row, hid


def _ffn_fwd(h, g, w1t, w3t, w2, tag, plan):
    n, d = h.shape
    ns, fs, _ = w2.shape
    rows, row, gain, w_row, hid = _ffn_specs(n, d, fs, FFN_FWD_ROWS)
    sub = rows

    def body(h_ref, g_ref, w1_ref, w3_ref, w2_ref, out_ref, n1_ref, a_ref, b_ref, hm_ref, acc_ref):
        s = pl.program_id(1)

        @pl.when(s == 0)
        def _():
            xv = h_ref[...]
            rstd = lax.rsqrt(jnp.mean(xv * xv, axis=-1, keepdims=True) + EPS)
            n1_ref[...] = (xv * rstd * g_ref[...]).astype(BF16)
            acc_ref[...] = jnp.zeros_like(acc_ref)

        def up(j):
            n1 = n1_ref[j * sub:(j + 1) * sub, :]
            return _dot(n1, w1_ref[...], NT), _dot(n1, w3_ref[...], NT)

        cur = up(0)
        for j in range(rows // sub):
            nxt = up(j + 1) if (j + 1) * sub < rows else None
            a, b = cur
            r = slice(j * sub, (j + 1) * sub)
            hm = (a * _sigmoid(a) * b).astype(BF16)
            a_ref[r, :] = a.astype(BF16)
            b_ref[r, :] = b.astype(BF16)
            hm_ref[r, :] = hm
            acc_ref[r, :] += _dot(hm, w2_ref[...])
            cur = nxt

        @pl.when(s == ns - 1)
        def _():
            out_ref[...] = h_ref[...] + 0.5 * acc_ref[...]

    hid_shape = jax.ShapeDtypeStruct((ns, n, fs), BF16)
    plan.before(f"{tag}_fwd")
    out, n1, a, b, hm = _pcall(
        body, name=f"{tag}_fwd", grid=(n // rows, ns), in_specs=[row, gain, w_row, w_row, w_row],
        out_specs=[row, row, hid, hid, hid],
        out_shape=[jax.ShapeDtypeStruct((n, d), F32), jax.ShapeDtypeStruct((n, d), BF16), hid_shape, hid_shape, hid_shape],
        scratch_shapes=[pltpu.VMEM((rows, d), F32)], compiler_params=_params())(h, g, w1t, w3t, w2)
    plan.after(f"{tag}_fwd")
    return out, (h, n1, a, b, hm)


def _ffn_bwd(dout, saved, g, w1, w3, w2, tag, plan):
    h, n1, a, b, hm = saved
    n, d = h.shape
    ns, fs, _ = w2.shape
    rows, row, gain, w_row, hid = _ffn_specs(n, d, fs, FFN_BWD_ROWS)
    sub = _pick(rows, FFN_SUB_ROWS, 16)

    def body(do_ref, h_ref, g_ref, a_ref, b_ref, w1_ref, w3_ref, w2_ref, dh_ref, da_ref, db_ref, dg_ref, acc_ref):
        i, s = pl.program_id(0), pl.program_id(1)

        @pl.when(s == 0)
        def _():
            acc_ref[...] = jnp.zeros_like(acc_ref)

        @pl.when((s == 0) & (i == 0))
        def _():
            dg_ref[...] = jnp.zeros_like(dg_ref)

        def up(j):
            return _dot(0.5 * do_ref[j * sub:(j + 1) * sub, :], w2_ref[...], NT)

        cur = up(0)
        for j in range(rows // sub):
            nxt = up(j + 1) if (j + 1) * sub < rows else None
            r = slice(j * sub, (j + 1) * sub)
            av, bv = a_ref[r, :].astype(F32), b_ref[r, :].astype(F32)
            sg = _sigmoid(av)
            da = (cur * bv * (sg * (1.0 + av * (1.0 - sg)))).astype(BF16)
            db = (cur * av * sg).astype(BF16)
            da_ref[r, :] = da
            db_ref[r, :] = db
            acc_ref[r, :] += _dot(da, w1_ref[...]) + _dot(db, w3_ref[...])
            cur = nxt

        @pl.when(s == ns - 1)
        def _():
            xv, dn = h_ref[...], acc_ref[...]
            rstd = lax.rsqrt(jnp.mean(xv * xv, axis=-1, keepdims=True) + EPS)
            xh = xv * rstd
            dg_ref[...] += jnp.sum(dn * xh, axis=0, keepdims=True)
            dxh = dn * g_ref[...]
            dh_ref[...] = do_ref[...] + rstd * (dxh - xh * jnp.mean(dxh * xh, axis=-1, keepdims=True))

    hid_shape = jax.ShapeDtypeStruct((ns, n, fs), BF16)
    plan.before(f"{tag}_bwd")
    dh, da, db, dg = _pcall(
        body, name=f"{tag}_bwd", grid=(n // rows, ns), in_specs=[row, row, gain, hid, hid, w_row, w_row, w_row],
        out_specs=[row, hid, hid, gain],
        out_shape=[jax.ShapeDtypeStruct((n, d), F32), hid_shape, hid_shape, jax.ShapeDtypeStruct((1, d), F32)],
        scratch_shapes=[pltpu.VMEM((rows, d), F32)], compiler_params=_params())(dout, h, g, a, b, w1, w3, w2)
    plan.after(f"{tag}_bwd")
    plan.before(f"{tag}_gw2")
    gw2 = _mm(hm, dout, ta=True, shard='m', alpha=0.5, name=f"{tag}_gw2")
    plan.after(f"{tag}_gw2")
    gw1 = _mm(da, n1, ta=True, shard='m', name=f"{tag}_gw1")
    gw3 = _mm(db, n1, ta=True, shard='m', name=f"{tag}_gw3")
    return dh, dg, gw1, gw3, gw2


def _local_step(x, tgt, plan):
    n = x.shape[0]
    grads = plan.grads

    def f(name):
        w = plan.get(name)
        return w.reshape(1, D_MODEL) if name.endswith('_norm') and name != 'gla_out_norm' else w

    def carried(tag, fn, *args, **kw):
        plan.before(tag)
        out = fn(*args, **kw)
        plan.after(tag)
        return out

    h1, ffn1 = _ffn_fwd(x, f('ffn1_norm'), f('ffn1_w1'), f('ffn1_w3'), f('ffn1_w2'), "ffn1", plan)
    u = carried("mix_rms", _rms_fwd, h1, f('mix_norm'), "mix_rms")
    w_in = f('w_in')
    w_a = jnp.concatenate([w_in[:, :512], _pad_heads(w_in[:, 512:768]), _pad_heads(w_in[:, 768:1024]), w_in[:, 1024:2048]],
                          axis=1)
    w_al = jnp.pad(w_in[:, 2048:2048 + GLA_RANK], ((0, 0), (0, LANE - GLA_RANK)))
    w_g = w_in[:, 2048 + GLA_RANK:]
    za = _mm(u, w_a, name="in_a")
    zg = _mm(u, w_g, name="in_g")
    al = _mm(u, w_al, name="in_al")
    ar, ai, bbar_re, bbar_im = _s5_discretize(f('s5_lambda_re'), f('s5_lambda_im'), f('s5_log_dt'), f('s5_b_re'), f('s5_b_im'))
    t_b = _bd_tiles(bbar_re.transpose(0, 2, 1), bbar_im.transpose(0, 2, 1)).astype(BF16)
    t_c = _bd_tiles(f('s5_c_re'), -f('s5_c_im')).astype(BF16)
    ar8 = jnp.broadcast_to(ar.reshape(1, S5_GP), (SEG, S5_GP))
    ai8 = jnp.broadcast_to(ai.reshape(1, S5_GP), (SEG, S5_GP))
    pw_r, pw_i = _segment_powers(ar, ai, n // SEG)
    dskip = f('s5_d').reshape(1, S5_W)
    u_s5 = _permute_rows(za[:, :S5_W])
    bu = _bd_expand(u_s5, t_b, "s5_bu")
    xs = _s5_scan(bu, ar8, ai8, pw_r, pw_i, "s5_scan")
    ys_p = _bd_reduce(xs, t_c, _scale_rows(u_s5, dskip, "s5_skip"), "s5_y")
    ys = _unpermute_rows(ys_p)
    zgelu = _gelu_fwd(ys, "s5_gelu")
    t_glu = _mm(zgelu, f('s5_glu_w'), bias=f('s5_glu_b').reshape(1, S5_W), name="s5_glu_t")
    y_s5 = _glu_fwd(zgelu, t_glu, "s5_glu")
    wup = jnp.pad(f('gla_a_up_w'), ((0, LANE - GLA_RANK), (0, 0)))
    wup_h = wup.reshape(LANE, GLA_HEADS, GLA_DK).transpose(1, 0, 2)
    bup_h = f('gla_a_up_b').reshape(GLA_HEADS, 1, GLA_DK)
    gn_h = f('gla_out_norm').reshape(GLA_HEADS, 1, GLA_DV)
    y_gla, s_prev = carried("gla_fwd", _gla_fwd, za, al, wup_h, bup_h, gn_h, "gla_fwd")
    ps = _mm(y_s5, f('proj_s5'), name="proj_s5")
    pg = carried("proj_gla", _mm, y_gla, f('proj_gla'), name="proj_gla")
    merged = _merge_fwd(zg, ps, pg, "merge")
    h2 = _mm(merged, f('w_out'), res=h1, name="w_out")
    h3, ffn2 = _ffn_fwd(h2, f('ffn2_norm'), f('ffn2_w1'), f('ffn2_w3'), f('ffn2_w2'), "ffn2", plan)
    loss, dh3, g_final = _final_loss(h3, f('final_norm').reshape(1, D_MODEL), tgt, "loss")
    grads['final_norm'] = g_final.reshape(D_MODEL)
    dh2, grads['ffn2_norm'], grads['ffn2_w1'], grads['ffn2_w3'], grads['ffn2_w2'] = _ffn_bwd(
        dh3, ffn2, f('ffn2_norm'), f('ffn2_w1'), f('ffn2_w3'), f('ffn2_w2'), "ffn2", plan)
    dm = _mm(dh2, f('w_out'), tb=True, name="d_merged")
    grads['w_out'] = _mm(merged, dh2, ta=True, name="g_w_out")
    dps, dpg, dzg = carried("d_merge", _merge_bwd, dm, zg, ps, pg, "d_merge")
    grads['proj_s5'] = _mm(y_s5, dps, ta=True, name="g_proj_s5")
    grads['proj_gla'] = _mm(y_gla, dpg, ta=True, name="g_proj_gla")
    dy_s5 = _mm(dps, f('proj_s5'), tb=True, name="d_y_s5")
    dy_gla = _mm(dpg, f('proj_gla'), tb=True, name="d_y_gla")
    dzgelu, dt_glu, g_glu_b = _glu_bwd1(dy_s5, zgelu, t_glu, "d_glu")
    grads['s5_glu_b'] = g_glu_b.reshape(S5_W)
    grads['s5_glu_w'] = _mm(zgelu, dt_glu, ta=True, name="g_glu_w")
    dzgelu = _mm(dt_glu, f('s5_glu_w'), tb=True, res=dzgelu, name="d_gelu")
    dys, du_skip, g_d = _glu_bwd2(_permute_rows(dzgelu), ys_p, u_s5, dskip, "d_s5_y")
    grads['s5_d'] = g_d.reshape(S5_G, S5_H)
    gx = _bd_expand(dys, t_c, "s5_gx")
    lam, da8 = _s5_scan_bwd(gx, xs, ar8, ai8, pw_r, pw_i, "s5_scan_bwd")
    g_c = _bd_blocks(_bd_outer(dys, xs, "g_s5_c"))
    grads['s5_c_re'], grads['s5_c_im'] = g_c[0], -g_c[1]
    g_b = _bd_blocks(_bd_outer(u_s5, lam, "g_s5_b")).transpose(0, 1, 3, 2)
    g_bbar_re, g_bbar_im = g_b[0], g_b[1]
    da = jnp.sum(da8, axis=0)
    g_ar, g_ai = da[:S5_GP].reshape(S5_G, S5_P), da[S5_GP:].reshape(S5_G, S5_P)
    _, disc_vjp = jax.vjp(_s5_discretize, f('s5_lambda_re'), f('s5_lambda_im'), f('s5_log_dt'), f('s5_b_re'), f('s5_b_im'))
    (grads['s5_lambda_re'], grads['s5_lambda_im'], grads['s5_log_dt'], grads['s5_b_re'],
     grads['s5_b_im']) = disc_vjp((g_ar, g_ai, g_bbar_re, g_bbar_im))
    du_s5 = _unpermute_rows(_bd_reduce(lam, t_b, du_skip, "d_s5_u"))
    dza, dz, dgn, dbup = carried("gla_bwd", _gla_bwd, za, al, wup_h, bup_h, gn_h, s_prev, dy_gla, du_s5, "gla_bwd")
    grads['gla_out_norm'] = dgn.reshape(GLA_HEADS * GLA_DV)
    grads['gla_a_up_b'] = dbup.reshape(GLA_HEADS * GLA_DK)
    grads['gla_a_up_w'] = _unpad_heads(_mm(al, dz, ta=True, name="g_a_up")[:GLA_RANK])
    dal = _mm(dz, _pad_heads(wup), tb=True, name="d_a_low")
    g_wa = _mm(u, dza, ta=True, name="g_in_a")
    g_wg = _mm(u, dzg, ta=True, name="g_in_g")
    g_wal = _mm(u, dal, ta=True, name="g_in_al")
    grads['w_in'] = jnp.concatenate([g_wa[:, :512], _unpad_heads(g_wa[:, 512:1024]), _unpad_heads(g_wa[:, 1024:1536]),
                                     g_wa[:, 1536:], g_wal[:, :GLA_RANK], g_wg], axis=1)
    du = carried("d_u_a", _mm, dza, w_a, tb=True, name="d_u_a")
    du = _mm(dzg, w_g, tb=True, res=du, name="d_u_g")
    du = _mm(dal, w_al, tb=True, res=du, name="d_u_al")
    dh1, g_mix = carried("d_mix_rms", _rms_bwd, h1, f('mix_norm'), du, dh2, "d_mix_rms")
    grads['mix_norm'] = g_mix
    dx, grads['ffn1_norm'], grads['ffn1_w1'], grads['ffn1_w3'], grads['ffn1_w2'] = _ffn_bwd(
        dh1, ffn1, f('ffn1_norm'), f('ffn1_w1'), f('ffn1_w3'), f('ffn1_w2'), "ffn1", plan)
    return loss[0, 0], dx


MIXER_WEIGHTS = ['w_in', 's5_glu_w', 'proj_s5', 'proj_gla', 'w_out', 'gla_a_up_w']
FFN1_WEIGHTS, FFN2_WEIGHTS = FFN_WEIGHTS[:3], FFN_WEIGHTS[3:]
TRANSPOSED = ['ffn1_w1', 'ffn1_w3', 'ffn2_w1', 'ffn2_w3']


def _local_shard(w, nm):
    return jnp.swapaxes(w, 1, 2)[0] if nm in TRANSPOSED else w[0]
GRAD_GROUPS = {'ffn2': FFN2_WEIGHTS, 'mixer': ['w_out', 'proj_s5', 'proj_gla', 's5_glu_w', 'w_in'], 'ffn1': FFN1_WEIGHTS}


class _Plan:
    def __init__(self, a, c_arr, s_arr):
        self.a, self.c_arr, self.s_arr = a, c_arr, s_arr
        self.grads, self.weights, self.riding = {}, {}, {}
        self.g4s, self.chip_sums, self.halves, self.sib_halves = {}, {}, {}, {}
        for nm in SMALL:
            if nm != 'gla_a_up_w':
                self.weights[nm] = a[nm] if nm == 'final_norm' else a[nm][0]
        ici = _side_gather_ici(self._shards(FFN1_WEIGHTS))
        _run_side(ici, "gather_ffn1_ici")
        self._gathered(FFN1_WEIGHTS, _run_side(_side_gather_d2d(ici.outs), "gather_ffn1_d2d"))

    def _shards(self, names):
        return [_local_shard(self.a[nm], nm).astype(F32 if nm == 'gla_a_up_w' else BF16) for nm in names]

    def _gathered(self, names, arrs):
        for nm, g4 in zip(names, arrs):
            if nm in FFN_WEIGHTS:
                self.weights[nm] = g4
            elif nm in COL_SHARDED:
                self.weights[nm] = jnp.concatenate([g4[s] for s in range(4)], axis=1)
            else:
                self.weights[nm] = g4.reshape(4 * g4.shape[1], g4.shape[2])

    def get(self, name):
        return self.weights[name]

    def _shard_major(self, nm):
        g = self.grads[nm]
        if nm in FFN_WEIGHTS:
            return g
        if nm in COL_SHARDED:
            return jnp.stack(jnp.split(g, 4, axis=1))
        return g.reshape(4, g.shape[0] // 4, g.shape[1])

    def _schedule(self, tag):
        grp = GRAD_GROUPS
        if tag == "ffn1_fwd":
            return _side_gather_ici(self._shards(MIXER_WEIGHTS)), lambda outs: self.riding.update(mixer_ici=outs)
        if tag == "mix_rms":
            return _side_gather_d2d(self.riding['mixer_ici']), lambda outs: self._gathered(MIXER_WEIGHTS, outs)
        if tag == "gla_fwd":
            return _side_gather_ici(self._shards(FFN2_WEIGHTS)), lambda outs: self.riding.update(ffn2_ici=outs)
        if tag == "proj_gla":
            return _side_gather_d2d(self.riding['ffn2_ici']), lambda outs: self._gathered(FFN2_WEIGHTS, outs)
        steps = {"d_merge": ('ffn2', 0), "gla_bwd": ('ffn2', 1), "d_mix_rms": ('ffn2', 2),
                 "d_u_a": ('mixer', 0), "ffn1_bwd": ('mixer', 1), "ffn1_gw2": ('mixer', 2)}
        if tag in steps:
            group, stage = steps[tag]
            return self._reduce_stage(grp[group], stage)
        return None

    def _reduce_stage(self, names, stage):
        if stage == 0:
            for nm in names:
                self.g4s[nm] = self._shard_major(nm)

            def done(outs):
                for nm, r in zip(names, outs):
                    self.chip_sums[nm] = _chip_sum(self.g4s[nm], r, self.c_arr, f"chip_sum_{nm}")
            return _side_swap_halves([self.g4s[nm] for nm in names]), done
        if stage == 1:
            def done(outs):
                for nm, o in zip(names, outs):
                    self.halves[nm] = _owner_sum(self.chip_sums[nm], o, self.s_arr, f"owner_sum_{nm}")
            return _side_scatter([self.chip_sums[nm] for nm in names]), done

        def done(outs):
            self.sib_halves.update(zip(names, outs))
        return _side_swap_reduced([self.halves[nm] for nm in names]), done

    def before(self, tag):
        entry = self._schedule(tag)
        if entry is not None:
            side, done = entry
            self.riding[tag] = (side, done)
            _RIDER.append(side)

    def after(self, tag):
        if tag in self.riding:
            side, done = self.riding.pop(tag)
            assert not _RIDER and side.outs is not None, tag
            done(side.outs)

    def finish(self):
        names = GRAD_GROUPS['ffn1']
        for stage in range(3):
            side, done = self._reduce_stage(names, stage)
            done(_run_side(side, f"grad_ffn1_stage{stage}"))


def _train_step(a):
    x = a['x'][0]
    tgt = a['loss_target'][0]
    xi, yi, ci = lax.axis_index("x"), lax.axis_index("y"), lax.axis_index("c")
    c_arr = jnp.reshape(ci, (1,)).astype(jnp.int32)
    s_arr = jnp.reshape(2 * xi + yi, (1,)).astype(jnp.int32)
    plan = _Plan(a, c_arr, s_arr)
    loss, dx = _local_step(x, tgt, plan)
    plan.finish()
    grads = plan.grads
    loss = lax.psum(loss, ("x", "y", "c"))
    halves = [plan.halves[nm] for nm in SHARDED]
    sib_halves = [plan.sib_halves[nm] for nm in SHARDED]
    red = {}
    small_parts = [grads[nm].reshape(a[nm].shape) for nm in SMALL if nm != 'gla_a_up_w'] + [grads['gla_a_up_w']]
    small_sum = _unpack_small(_allreduce_small(_pack_small(small_parts), "allreduce_small"), small_parts)
    small_names = [nm for nm in SMALL if nm != 'gla_a_up_w']
    for nm, g in zip(small_names, small_sum[:-1]):
        red[nm] = g
    g_up = small_sum[-1]
    red['gla_a_up_w'] = lax.dynamic_slice(g_up, (0, (2 * xi + yi) * GLA_DK), (GLA_RANK, GLA_DK))
    out_g, out_d, out_m, out_v = {}, {}, {}, {}
    for nm, own, sib in zip(SHARDED, halves, sib_halves):
        loc = lambda pre: _local_shard(a[pre + nm], nm)
        res = _adamw_halves(loc(''), own, sib, loc('m_'), loc('v_'), c_arr, f"adamw_{nm}")
        back = (lambda t: jnp.swapaxes(t[None], 1, 2)) if nm in TRANSPOSED else (lambda t: t[None])
        out_g[nm], out_d[nm], out_m[nm], out_v[nm] = (back(t) for t in res)
    rest = [nm for nm in WEIGHTS if nm not in SHARDED]
    pk = lambda pre: _pack_small([a[pre + nm] for nm in rest])
    d, nm_, nv_ = _adamw(pk(''), _pack_small([red[nm] for nm in rest]), pk('m_'), pk('v_'), "adamw_small")
    like = [a[nm] for nm in rest]
    for nm, g, dd, mm_, vv_ in zip(rest, [red[nm].reshape(a[nm].shape) for nm in rest], _unpack_small(d, like),
                                   _unpack_small(nm_, like), _unpack_small(nv_, like)):
        out_g[nm], out_d[nm], out_m[nm], out_v[nm] = g, dd, mm_, vv_
    return (loss, dx[None], *[out_g[nm] for nm in WEIGHTS], *[out_d[nm] for nm in WEIGHTS],
            *[out_m[nm] for nm in WEIGHTS], *[out_v[nm] for nm in WEIGHTS])


def kernel(x, ffn1_norm, ffn1_w1, ffn1_w3, ffn1_w2, mix_norm, w_in, s5_lambda_re, s5_lambda_im, s5_log_dt, s5_b_re, s5_b_im, s5_c_re, s5_c_im, s5_d, s5_glu_w, s5_glu_b, gla_a_up_w, gla_a_up_b, gla_out_norm, proj_s5, proj_gla, w_out, ffn2_norm, ffn2_w1, ffn2_w3, ffn2_w2, final_norm, loss_target, m_ffn1_norm, m_ffn1_w1, m_ffn1_w3, m_ffn1_w2, m_mix_norm, m_w_in, m_s5_lambda_re, m_s5_lambda_im, m_s5_log_dt, m_s5_b_re, m_s5_b_im, m_s5_c_re, m_s5_c_im, m_s5_d, m_s5_glu_w, m_s5_glu_b, m_gla_a_up_w, m_gla_a_up_b, m_gla_out_norm, m_proj_s5, m_proj_gla, m_w_out, m_ffn2_norm, m_ffn2_w1, m_ffn2_w3, m_ffn2_w2, m_final_norm, v_ffn1_norm, v_ffn1_w1, v_ffn1_w3, v_ffn1_w2, v_mix_norm, v_w_in, v_s5_lambda_re, v_s5_lambda_im, v_s5_log_dt, v_s5_b_re, v_s5_b_im, v_s5_c_re, v_s5_c_im, v_s5_d, v_s5_glu_w, v_s5_glu_b, v_gla_a_up_w, v_gla_a_up_b, v_gla_out_norm, v_proj_s5, v_proj_gla, v_w_out, v_ffn2_norm, v_ffn2_w1, v_ffn2_w3, v_ffn2_w2, v_final_norm):
    return _train_step(dict(locals()))
```

```python
import functools

import jax
import jax.numpy as jnp
from jax import lax
from jax.experimental import pallas as pl
from jax.experimental.pallas import tpu as pltpu

F32 = jnp.float32
BF16 = jnp.bfloat16
HI = lax.Precision.HIGHEST
MESH_ID = pl.DeviceIdType.MESH

D_MODEL = 1024
EPS = 1e-6
S5_G, S5_P, S5_H = 32, 64, 16
S5_W = S5_G * S5_H
S5_GP = S5_G * S5_P
SEG = 8
SCAN_ROWS = 256
GLA_HEADS, GLA_DK, GLA_DV = 4, 64, 128
GLA_CHUNK = 64
GLA_TAU = 16.0
GLA_RANK = 16
ADAM_LR, ADAM_B1, ADAM_B2, ADAM_EPS, ADAM_WD, ADAM_STEP = 0.001, 0.9, 0.999, 1e-08, 0.01, 10
V7X_VMEM_LIMIT = 56 * 1024 * 1024
LANE = 128

WEIGHTS = ['ffn1_norm', 'ffn1_w1', 'ffn1_w3', 'ffn1_w2', 'mix_norm', 'w_in', 's5_lambda_re', 's5_lambda_im',
           's5_log_dt', 's5_b_re', 's5_b_im', 's5_c_re', 's5_c_im', 's5_d', 's5_glu_w', 's5_glu_b', 'gla_a_up_w',
           'gla_a_up_b', 'gla_out_norm', 'proj_s5', 'proj_gla', 'w_out', 'ffn2_norm', 'ffn2_w1', 'ffn2_w3',
           'ffn2_w2', 'final_norm']
SHARDED = ['ffn1_w1', 'ffn1_w3', 'ffn1_w2', 'w_in', 's5_glu_w', 'proj_s5', 'proj_gla', 'w_out',
           'ffn2_w1', 'ffn2_w3', 'ffn2_w2']
COL_SHARDED = ['ffn1_w1', 'ffn1_w3', 'w_in', 'proj_s5', 'proj_gla', 'ffn2_w1', 'ffn2_w3', 'gla_a_up_w']
SMALL = [n for n in WEIGHTS if n not in SHARDED]
FFN_WEIGHTS = ['ffn1_w1', 'ffn1_w3', 'ffn1_w2', 'ffn2_w1', 'ffn2_w3', 'ffn2_w2']


def _params(**kw):
    return pltpu.CompilerParams(vmem_limit_bytes=V7X_VMEM_LIMIT, **kw)


class _Side:
    def __init__(self, ins, out_shapes, nsem, copies, aliased=False):
        self.ins, self.out_shapes, self.nsem, self.copies, self.aliased = list(ins), list(out_shapes), nsem, copies, aliased
        self.outs = None


_RIDER = []


def _pcall(body, **kw):
    if _RIDER:
        return _carry(body, _RIDER.pop(), **kw)
    return pl.pallas_call(body, **kw)


def _carry(body, side, *, name, grid, in_specs, out_specs, out_shape, scratch_shapes=(), compiler_params=None):
    del compiler_params
    single = not isinstance(out_shape, (list, tuple))
    out_specs = [out_specs] if single else list(out_specs)
    out_shape = [out_shape] if single else list(out_shape)
    n_in, n_out, n_scr = len(in_specs), len(out_shape), len(scratch_shapes)
    s_in, s_out = len(side.ins), len(side.out_shapes)
    any_spec = pl.BlockSpec(memory_space=pl.ANY)

    def wrapped(*refs):
        cuts = [n_in, s_in, n_out, s_out, n_scr]
        parts, pos = [], 0
        for c in cuts:
            parts.append(refs[pos:pos + c])
            pos += c
        ins, sins, outs, souts, scr = parts
        ssem, rsem = refs[pos], refs[pos + 1]
        first = last = None
        for d, g in enumerate(grid):
            i = pl.program_id(d)
            first = (i == 0) if first is None else first & (i == 0)
            last = (i == g - 1) if last is None else last & (i == g - 1)

        @pl.when(first)
        def _():
            for cp in side.copies(sins, souts, ssem, rsem):
                cp.start()

        body(*ins, *outs, *scr)

        @pl.when(last)
        def _():
            for cp in side.copies(sins, souts, ssem, rsem):
                cp.wait()

    call = pl.pallas_call(
        wrapped, name=name, grid=grid, in_specs=list(in_specs) + [any_spec] * s_in,
        out_specs=out_specs + [any_spec] * s_out, out_shape=out_shape + side.out_shapes,
        scratch_shapes=list(scratch_shapes) + [pltpu.SemaphoreType.DMA((side.nsem,)), pltpu.SemaphoreType.DMA((side.nsem,))],
        input_output_aliases={n_in + j: n_out + j for j in range(s_in)} if side.aliased else {},
        compiler_params=_params(has_side_effects=True))

    def run(*args):
        res = call(*args, *side.ins)
        side.outs = list(res[n_out:])
        return res[0] if single else list(res[:n_out])

    return run


def _run_side(side, name):
    s_in, s_out = len(side.ins), len(side.out_shapes)
    any_spec = pl.BlockSpec(memory_space=pl.ANY)

    def body(*refs):
        sins, souts = refs[:s_in], refs[s_in:s_in + s_out]
        ssem, rsem = refs[s_in + s_out:]
        cps = side.copies(sins, souts, ssem, rsem)
        for cp in cps:
            cp.start()
        for cp in cps:
            cp.wait()

    side.outs = list(pl.pallas_call(
        body, name=name, in_specs=[any_spec] * s_in, out_specs=[any_spec] * s_out, out_shape=side.out_shapes,
        scratch_shapes=[pltpu.SemaphoreType.DMA((side.nsem,)), pltpu.SemaphoreType.DMA((side.nsem,))],
        input_output_aliases={j: j for j in range(s_in)} if side.aliased else {},
        compiler_params=pltpu.CompilerParams(has_side_effects=True))(*side.ins))
    return side.outs


def _pick(n, cap, quantum):
    if n <= cap:
        return n
    best = None
    for t in range(quantum, cap + 1, quantum):
        if n % t == 0:
            best = t
    assert best is not None, (n, cap, quantum)
    return best


def _sigmoid(x):
    return jax.nn.sigmoid(x)


def _mm(a, b, *, name, ta=False, tb=False, out_dtype=F32, alpha=1.0, res=None, bias=None, exact=False, shard=None):
    ns = 4
    (k_a, m) = a.shape[-2:] if ta else a.shape[-2:][::-1]
    (k_b, n) = b.shape[-2:][::-1] if tb else b.shape[-2:]
    assert k_a == k_b, (a.shape, b.shape, ta, tb)
    assert (a.ndim == 3) == (shard in ('k', 'm')) and (b.ndim == 3) == (shard in ('n', 'k'))
    k = k_a
    tm = _pick(m, 1024, 128)
    tn = _pick(n, 1024, 128)
    tk = _pick(k, 1024, 128)
    pm, pn, pk = m // tm, n // tn, k // tk
    gm = pm * (ns if shard == 'm' else 1)
    gn = pn * (ns if shard == 'n' else 1)
    gk = pk * (ns if shard == 'k' else 1)
    dims = (((0,) if ta else (1,), (1,) if tb else (0,)), ((), ()))
    op_dtype = F32 if exact else BF16

    def body(*refs):
        a_ref, b_ref = refs[0], refs[1]
        pos = 2
        res_ref = bias_ref = None
        if res is not None:
            res_ref = refs[pos]
            pos += 1
        if bias is not None:
            bias_ref = refs[pos]
            pos += 1
        o_ref, acc_ref = refs[pos], refs[pos + 1]
        kk = pl.program_id(2)

        @pl.when(kk == 0)
        def _():
            acc_ref[...] = jnp.zeros_like(acc_ref)

        acc_ref[...] += lax.dot_general(a_ref[...].astype(op_dtype), b_ref[...].astype(op_dtype), dims,
                                        precision=HI if exact else None, preferred_element_type=F32)

        @pl.when(kk == gk - 1)
        def _():
            o = acc_ref[...]
            if alpha != 1.0:
                o = o * alpha
            if bias_ref is not None:
                o = o + bias_ref[...]
            if res_ref is not None:
                o = o + res_ref[...]
            o_ref[...] = o.astype(out_dtype)

    def spec(block, sharded_on, order):
        per = {'m': pm, 'n': pn, 'k': pk}

        def index(i, j, kk):
            g = {'m': i, 'n': j, 'k': kk}
            r, c = order(i % pm if shard == 'm' else i, j % pn if shard == 'n' else j, kk % pk if shard == 'k' else kk)
            if sharded_on is None:
                return (r, c)
            return (g[sharded_on] // per[sharded_on], r, c)

        return pl.BlockSpec(block if sharded_on is None else (None,) + block, index)

    a_sh = shard if shard in ('k', 'm') else None
    b_sh = shard if shard in ('n', 'k') else None
    o_sh = shard if shard in ('n', 'm') else None
    a_spec = spec((tk, tm), a_sh, lambda i, j, kk: (kk, i)) if ta else spec((tm, tk), a_sh, lambda i, j, kk: (i, kk))
    b_spec = spec((tn, tk), b_sh, lambda i, j, kk: (j, kk)) if tb else spec((tk, tn), b_sh, lambda i, j, kk: (kk, j))
    ins, in_specs = [a, b], [a_spec, b_spec]
    if res is not None:
        assert o_sh is None
        ins.append(res)
        in_specs.append(pl.BlockSpec((tm, tn), lambda i, j, kk: (i, j)))
    if bias is not None:
        assert o_sh is None
        ins.append(bias)
        in_specs.append(pl.BlockSpec((1, tn), lambda i, j, kk: (0, j)))
    out_shape = (m, n) if o_sh is None else (ns, m, n)
    return _pcall(body, name=name, grid=(gm, gn, gk), in_specs=in_specs,
                  out_specs=spec((tm, tn), o_sh, lambda i, j, kk: (i, j)),
                  out_shape=jax.ShapeDtypeStruct(out_shape, out_dtype),
                  scratch_shapes=[pltpu.VMEM((tm, tn), F32)], compiler_params=_params())(*ins)


ROWS_VMEM_BUDGET = 24 * 1024 * 1024


def _rows(body, ins, outs, *, n, name):
    cols = sum(a.shape[1] for a, kind in ins if kind == 'r') + sum(c for c, _, kind in outs if kind == 'r')
    cap = 256
    while cap < 2048 and 2 * 4 * cols * (2 * cap) <= ROWS_VMEM_BUDGET:
        cap *= 2
    tm = _pick(n, cap, 16)
    in_specs = []
    for arr, kind in ins:
        if kind == 'r':
            in_specs.append(pl.BlockSpec((tm, arr.shape[1]), lambda i: (i, 0)))
        else:
            in_specs.append(pl.BlockSpec(arr.shape, lambda i: (0, 0)))
    out_specs, out_shape = [], []
    for cols, dtype, kind in outs:
        if kind == 'r':
            out_specs.append(pl.BlockSpec((tm, cols), lambda i: (i, 0)))
            out_shape.append(jax.ShapeDtypeStruct((n, cols), dtype))
        else:
            out_specs.append(pl.BlockSpec((1, cols), lambda i: (0, 0)))
            out_shape.append(jax.ShapeDtypeStruct((1, cols), dtype))
    n_in = len(ins)
    acc_ids = [j for j, o in enumerate(outs) if o[2] == 'a']

    def wrapped(*refs):
        if acc_ids:
            @pl.when(pl.program_id(0) == 0)
            def _():
                for j in acc_ids:
                    refs[n_in + j][...] = jnp.zeros_like(refs[n_in + j])
        body(*refs)

    res = _pcall(wrapped, name=name, grid=(n // tm,), in_specs=in_specs, out_specs=out_specs, out_shape=out_shape,
                 compiler_params=_params())(*[a for a, _ in ins])
    return res


def _rms_fwd(x, g, name):
    def body(x_ref, g_ref, o_ref):
        xv = x_ref[...]
        rstd = lax.rsqrt(jnp.mean(xv * xv, axis=-1, keepdims=True) + EPS)
        o_ref[...] = (xv * rstd * g_ref[...]).astype(BF16)
    return _rows(body, [(x, 'r'), (g, 'f')], [(x.shape[1], BF16, 'r')], n=x.shape[0], name=name)[0]


def _rms_bwd(x, g, dn, dres, name):
    def body(x_ref, g_ref, dn_ref, dres_ref, dx_ref, dg_ref):
        xv = x_ref[...]
        rstd = lax.rsqrt(jnp.mean(xv * xv, axis=-1, keepdims=True) + EPS)
        xh = xv * rstd
        dn = dn_ref[...]
        dg_ref[...] += jnp.sum(dn * xh, axis=0, keepdims=True)
        dxh = dn * g_ref[...]
        dx_ref[...] = dres_ref[...] + rstd * (dxh - xh * jnp.mean(dxh * xh, axis=-1, keepdims=True))
    d = x.shape[1]
    return _rows(body, [(x, 'r'), (g, 'f'), (dn, 'r'), (dres, 'r')], [(d, F32, 'r'), (d, F32, 'a')],
                 n=x.shape[0], name=name)


def _gelu_parts(y):
    c0 = 0.7978845608028654
    inner = c0 * (y + 0.044715 * y * y * y)
    th = jnp.tanh(inner)
    return th, c0 * (1.0 + 3.0 * 0.044715 * y * y)


def _gelu_fwd(y, name):
    def body(y_ref, o_ref):
        yv = y_ref[...]
        th, _ = _gelu_parts(yv)
        o_ref[...] = 0.5 * yv * (1.0 + th)
    return _rows(body, [(y, 'r')], [(y.shape[1], F32, 'r')], n=y.shape[0], name=name)[0]


def _glu_fwd(zg, t, name):
    def body(z_ref, t_ref, o_ref):
        o_ref[...] = (z_ref[...] * _sigmoid(t_ref[...])).astype(BF16)
    return _rows(body, [(zg, 'r'), (t, 'r')], [(zg.shape[1], BF16, 'r')], n=zg.shape[0], name=name)[0]


def _glu_bwd1(dy, zg, t, name):
    def body(dy_ref, z_ref, t_ref, dz_ref, dt_ref, db_ref):
        dyv, zv = dy_ref[...], z_ref[...]
        sg = _sigmoid(t_ref[...])
        dz_ref[...] = dyv * sg
        dt = dyv * zv * sg * (1.0 - sg)
        dt_ref[...] = dt.astype(BF16)
        db_ref[...] += jnp.sum(dt, axis=0, keepdims=True)
    w = zg.shape[1]
    return _rows(body, [(dy, 'r'), (zg, 'r'), (t, 'r')], [(w, F32, 'r'), (w, BF16, 'r'), (w, F32, 'a')],
                 n=zg.shape[0], name=name)


def _glu_bwd2(dzg, ys, u, dskip, name):
    def body(dz_ref, y_ref, u_ref, d_ref, dy_ref, du_ref, dd_ref):
        yv = y_ref[...]
        th, dinner = _gelu_parts(yv)
        dy = dz_ref[...] * (0.5 * (1.0 + th) + 0.5 * yv * (1.0 - th * th) * dinner)
        dy_ref[...] = dy
        du_ref[...] = dy * d_ref[...]
        dd_ref[...] += jnp.sum(dy * u_ref[...], axis=0, keepdims=True)
    w = ys.shape[1]
    return _rows(body, [(dzg, 'r'), (ys, 'r'), (u, 'r'), (dskip, 'f')], [(w, F32, 'r'), (w, F32, 'r'), (w, F32, 'a')],
                 n=ys.shape[0], name=name)


def _scale_rows(u, dskip, name):
    def body(u_ref, d_ref, o_ref):
        o_ref[...] = u_ref[...] * d_ref[...]
    return _rows(body, [(u, 'r'), (dskip, 'f')], [(u.shape[1], F32, 'r')], n=u.shape[0], name=name)[0]


def _merge_fwd(zg, ps, pg, name):
    def body(z_ref, ps_ref, pg_ref, o_ref):
        zv = z_ref[...]
        o_ref[...] = (_sigmoid(zv[:, :D_MODEL]) * ps_ref[...] + _sigmoid(zv[:, D_MODEL:]) * pg_ref[...]).astype(BF16)
    return _rows(body, [(zg, 'r'), (ps, 'r'), (pg, 'r')], [(D_MODEL, BF16, 'r')], n=zg.shape[0], name=name)[0]


def _merge_bwd(dm, zg, ps, pg, name):
    def body(dm_ref, z_ref, ps_ref, pg_ref, dps_ref, dpg_ref, dz_ref):
        dmv, zv = dm_ref[...], z_ref[...]
        s1, s2 = _sigmoid(zv[:, :D_MODEL]), _sigmoid(zv[:, D_MODEL:])
        dps_ref[...] = (dmv * s1).astype(BF16)
        dpg_ref[...] = (dmv * s2).astype(BF16)
        dz_ref[:, :D_MODEL] = dmv * ps_ref[...] * s1 * (1.0 - s1)
        dz_ref[:, D_MODEL:] = dmv * pg_ref[...] * s2 * (1.0 - s2)
    return _rows(body, [(dm, 'r'), (zg, 'r'), (ps, 'r'), (pg, 'r')],
                 [(D_MODEL, BF16, 'r'), (D_MODEL, BF16, 'r'), (2 * D_MODEL, F32, 'r')], n=zg.shape[0], name=name)


def _final_loss(h, g, tgt, name):
    def body(h_ref, g_ref, t_ref, loss_ref, dh_ref, dg_ref):
        hv = h_ref[...]
        rstd = lax.rsqrt(jnp.mean(hv * hv, axis=-1, keepdims=True) + EPS)
        xh = hv * rstd
        err = xh * g_ref[...] - t_ref[...]
        part = 0.5 * jnp.sum(jnp.mean(err * err, axis=-1, keepdims=True), axis=0, keepdims=True)
        loss_ref[...] += jnp.broadcast_to(part, loss_ref.shape)
        dout = err * (1.0 / hv.shape[1])
        dg_ref[...] += jnp.sum(dout * xh, axis=0, keepdims=True)
        dxh = dout * g_ref[...]
        dh_ref[...] = rstd * (dxh - xh * jnp.mean(dxh * xh, axis=-1, keepdims=True))
    d = h.shape[1]
    return _rows(body, [(h, 'r'), (g, 'f'), (tgt, 'r')], [(LANE, F32, 'a'), (d, F32, 'r'), (d, F32, 'a')],
                 n=h.shape[0], name=name)


def _adamw_math(wv, gv, mv, vv):
    nm = ADAM_B1 * mv + (1.0 - ADAM_B1) * gv
    nv = ADAM_B2 * vv + (1.0 - ADAM_B2) * (gv * gv)
    m_hat = nm / (1.0 - ADAM_B1 ** ADAM_STEP)
    v_hat = nv / (1.0 - ADAM_B2 ** ADAM_STEP)
    return -ADAM_LR * (m_hat / (jnp.sqrt(v_hat) + ADAM_EPS) + ADAM_WD * wv), nm, nv


def _adamw(w, g, m, v, name):
    def body(w_ref, g_ref, m_ref, v_ref, d_ref, nm_ref, nv_ref):
        d_ref[...], nm_ref[...], nv_ref[...] = _adamw_math(w_ref[...], g_ref[...], m_ref[...], v_ref[...])
    c = w.shape[1]
    return _rows(body, [(w, 'r'), (g, 'r'), (m, 'r'), (v, 'r')], [(c, F32, 'r')] * 3, n=w.shape[0], name=name)


def _adamw_halves(w, g_own, g_sib, m, v, c_arr, name):
    r, cols = w.shape
    h = r // 2
    tr = _pick(h, 512, 8)
    per = h // tr

    def body(c_ref, w_ref, go_ref, gs_ref, m_ref, v_ref, g_ref, d_ref, nm_ref, nv_ref):
        mine = (pl.program_id(0) // per) == c_ref[0]
        gv = jnp.where(mine, go_ref[...], gs_ref[...])
        g_ref[...] = gv
        d_ref[...], nm_ref[...], nv_ref[...] = _adamw_math(w_ref[...], gv, m_ref[...], v_ref[...])

    full = pl.BlockSpec((tr, cols), lambda i, c_ref: (i, 0))
    half = pl.BlockSpec((tr, cols), lambda i, c_ref: (i % per, 0))
    grid_spec = pltpu.PrefetchScalarGridSpec(num_scalar_prefetch=1, grid=(2 * per,),
                                             in_specs=[full, half, half, full, full], out_specs=[full] * 4)
    return _pcall(body, name=name, grid_spec=grid_spec, out_shape=[jax.ShapeDtypeStruct((r, cols), F32)] * 4,
                  compiler_params=_params())(c_arr, w, g_own, g_sib, m, v)


def _shift_rows(v, sh, down):
    rolled = pltpu.roll(v, sh if down else v.shape[0] - sh, axis=0)
    row = lax.broadcasted_iota(jnp.int32, v.shape, 0)
    keep = (row >= sh) if down else (row < v.shape[0] - sh)
    return jnp.where(keep, rolled, 0.0)


def _chain_segments(st_r, st_i, pw_r_ref, pw_i_ref, conj, down):
    vr, vi = st_r[...], st_i[...]
    sh, k = 1, 0
    while sh < SEG:
        pr, pi = pw_r_ref[k:k + 1, :], pw_i_ref[k:k + 1, :]
        if conj:
            pi = -pi
        sr, si = _shift_rows(vr, sh, down), _shift_rows(vi, sh, down)
        vr, vi = vr + pr * sr - pi * si, vi + pr * si + pi * sr
        sh, k = sh * 2, k + 1
    st_r[...] = _shift_rows(vr, 1, down)
    st_i[...] = _shift_rows(vi, 1, down)


def _s5_scan(bu, ar8, ai8, pw_r, pw_i, name):
    n = bu.shape[0]
    rb = SCAN_ROWS
    nb, steps, lc = n // rb, rb // SEG, 512

    def body(bu_ref, ar_ref, ai_ref, pwr_ref, pwi_ref, x_ref, st_r, st_i):
        ph, b = pl.program_id(0), pl.program_id(1)

        @pl.when((ph == 0) & (b == 0))
        def _():
            st_r[...] = jnp.zeros_like(st_r)
            st_i[...] = jnp.zeros_like(st_i)

        def scan(store):
            for c in range(S5_GP // lc):
                re, im = slice(c * lc, (c + 1) * lc), slice(S5_GP + c * lc, S5_GP + (c + 1) * lc)
                a_r, a_i = ar_ref[:, re], ai_ref[:, re]

                def step(s, carry):
                    xr, xi = carry
                    rows = pl.ds(pl.multiple_of(s * SEG, SEG), SEG)
                    nr = a_r * xr - a_i * xi + bu_ref[rows, re]
                    ni = a_r * xi + a_i * xr + bu_ref[rows, im]
                    if store:
                        x_ref[rows, re] = nr
                        x_ref[rows, im] = ni
                    return nr, ni

                xr, xi = lax.fori_loop(0, steps, step, (st_r[:, re], st_i[:, re]), unroll=4)
                st_r[:, re] = xr
                st_i[:, re] = xi

        @pl.when(ph == 0)
        def _():
            scan(False)

        @pl.when((ph == 0) & (b == nb - 1))
        def _():
            _chain_segments(st_r, st_i, pwr_ref, pwi_ref, conj=False, down=True)

        @pl.when(ph == 1)
        def _():
            scan(True)

    full = lambda a: pl.BlockSpec(a.shape, lambda ph, b: (0, 0))
    return _pcall(body, name=name, grid=(2, nb),
                  in_specs=[pl.BlockSpec((rb, 2 * S5_GP), lambda ph, b: (b, 0)), full(ar8), full(ai8), full(pw_r), full(pw_i)],
                  out_specs=pl.BlockSpec((rb, 2 * S5_GP), lambda ph, b: (b * ph, 0)),
                  out_shape=jax.ShapeDtypeStruct((n, 2 * S5_GP), F32),
                  scratch_shapes=[pltpu.VMEM((SEG, S5_GP), F32), pltpu.VMEM((SEG, S5_GP), F32)],
                  compiler_params=_params())(bu, ar8, ai8, pw_r, pw_i)


def _s5_scan_bwd(gx, xs, ar8, ai8, pw_r, pw_i, name):
    n = gx.shape[0]
    rb = SCAN_ROWS
    nb, steps, lc = n // rb, rb // SEG, 256

    def body(gx_ref, x_ref, ar_ref, ai_ref, pwr_ref, pwi_ref, lam_ref, da_ref, st_r, st_i):
        ph, b = pl.program_id(0), pl.program_id(1)

        @pl.when((ph == 0) & (b == 0))
        def _():
            st_r[...] = jnp.zeros_like(st_r)
            st_i[...] = jnp.zeros_like(st_i)
            da_ref[...] = jnp.zeros_like(da_ref)

        def scan(store):
            for c in range(S5_GP // lc):
                re, im = slice(c * lc, (c + 1) * lc), slice(S5_GP + c * lc, S5_GP + (c + 1) * lc)
                a_r, a_i = ar_ref[:, re], ai_ref[:, re]

                def step(s, carry):
                    rows = pl.ds(pl.multiple_of((steps - 1 - s) * SEG, SEG), SEG)
                    if store:
                        lr, li, dr, di = carry
                        xr, xi = x_ref[rows, re], x_ref[rows, im]
                        dr = dr + lr * xr + li * xi
                        di = di + li * xr - lr * xi
                    else:
                        lr, li = carry
                    nr = a_r * lr + a_i * li + gx_ref[rows, re]
                    ni = a_r * li - a_i * lr + gx_ref[rows, im]
                    if store:
                        lam_ref[rows, re] = nr
                        lam_ref[rows, im] = ni
                        return nr, ni, dr, di
                    return nr, ni

                if store:
                    lr, li, dr, di = lax.fori_loop(0, steps, step, (st_r[:, re], st_i[:, re], da_ref[:, re], da_ref[:, im]),
                                                   unroll=4)
                    da_ref[:, re] = dr
                    da_ref[:, im] = di
                else:
                    lr, li = lax.fori_loop(0, steps, step, (st_r[:, re], st_i[:, re]), unroll=4)
                st_r[:, re] = lr
                st_i[:, re] = li

        @pl.when(ph == 0)
        def _():
            scan(False)

        @pl.when((ph == 0) & (b == nb - 1))
        def _():
            _chain_segments(st_r, st_i, pwr_ref, pwi_ref, conj=True, down=False)

        @pl.when(ph == 1)
        def _():
            scan(True)

    full = lambda a: pl.BlockSpec(a.shape, lambda ph, b: (0, 0))
    rev = lambda ph, b: (nb - 1 - b, 0)
    return _pcall(body, name=name, grid=(2, nb),
                  in_specs=[pl.BlockSpec((rb, 2 * S5_GP), rev), pl.BlockSpec((rb, 2 * S5_GP), lambda ph, b: ((nb - 1 - b) * ph, 0)),
                            full(ar8), full(ai8), full(pw_r), full(pw_i)],
                  out_specs=[pl.BlockSpec((rb, 2 * S5_GP), lambda ph, b: (nb - 1 - b * ph, 0)),
                             pl.BlockSpec((SEG, 2 * S5_GP), lambda ph, b: (0, 0))],
                  out_shape=[jax.ShapeDtypeStruct((n, 2 * S5_GP), F32), jax.ShapeDtypeStruct((SEG, 2 * S5_GP), F32)],
                  scratch_shapes=[pltpu.VMEM((SEG, S5_GP), F32), pltpu.VMEM((SEG, S5_GP), F32)],
                  compiler_params=_params())(gx, xs, ar8, ai8, pw_r, pw_i)


def _s5_discretize(lam_re, lam_im, log_dt, b_re, b_im):
    dt = jnp.exp(log_dt)[:, None]
    mag = jnp.exp(lam_re * dt)
    ar = mag * jnp.cos(lam_im * dt)
    ai = mag * jnp.sin(lam_im * dt)
    den = lam_re * lam_re + lam_im * lam_im
    nr = ar - 1.0
    fr = (nr * lam_re + ai * lam_im) / den
    fi = (ai * lam_re - nr * lam_im) / den
    bbar_re = fr[:, :, None] * b_re - fi[:, :, None] * b_im
    bbar_im = fr[:, :, None] * b_im + fi[:, :, None] * b_re
    return ar, ai, bbar_re, bbar_im


BD_TILES, BD_CH, BD_ST, BD_GROUPS = 8, 128, 512, 8
BD_ROWS = 4096


def _bd_tiles(re, im):
    eye = jnp.eye(BD_GROUPS, dtype=re.dtype)

    def tiles(t):
        t = t.reshape(S5_G // BD_GROUPS, BD_GROUPS, S5_H, S5_P)
        return (t[:, :, :, None, :] * eye[None, :, None, :, None]).reshape(S5_G // BD_GROUPS, BD_CH, BD_ST)

    return jnp.concatenate([tiles(re), tiles(im)], axis=0)


def _bd_blocks(t):
    t = t.reshape(2, S5_G // BD_GROUPS, BD_GROUPS, S5_H, BD_GROUPS, S5_P)
    return jnp.einsum('rkahap->rkahp', t).reshape(2, S5_G, S5_H, S5_P)


def _bd_expand(a, t, name):
    n = a.shape[0]
    tm = _pick(n, BD_ROWS, 16)

    def body(a_ref, t_ref, o_ref):
        o_ref[...] = _dot(a_ref[...], t_ref[...])

    return _pcall(body, name=name, grid=(n // tm, BD_TILES),
                  in_specs=[pl.BlockSpec((tm, BD_CH), lambda i, j: (i, j % 4)), pl.BlockSpec((None, BD_CH, BD_ST), lambda i, j: (j, 0, 0))],
                  out_specs=pl.BlockSpec((tm, BD_ST), lambda i, j: (i, j)),
                  out_shape=jax.ShapeDtypeStruct((n, BD_TILES * BD_ST), F32), compiler_params=_params())(a, t)


def _bd_reduce(x, t, res, name):
    n = x.shape[0]
    tm = _pick(n, BD_ROWS, 16)

    def body(x_ref, t_ref, r_ref, o_ref):
        part = _dot(x_ref[...], t_ref[...], NT)

        @pl.when(pl.program_id(2) == 0)
        def _():
            o_ref[...] = r_ref[...] + part

        @pl.when(pl.program_id(2) == 1)
        def _():
            o_ref[...] += part

    return _pcall(body, name=name, grid=(n // tm, 4, 2),
                  in_specs=[pl.BlockSpec((tm, BD_ST), lambda i, k, r: (i, k + 4 * r)),
                            pl.BlockSpec((None, BD_CH, BD_ST), lambda i, k, r: (k + 4 * r, 0, 0)),
                            pl.BlockSpec((tm, BD_CH), lambda i, k, r: (i, k))],
                  out_specs=pl.BlockSpec((tm, BD_CH), lambda i, k, r: (i, k)),
                  out_shape=jax.ShapeDtypeStruct((n, S5_W), F32), compiler_params=_params())(x, t, res)


def _bd_outer(a, x, name):
    n = a.shape[0]
    tk = _pick(n, BD_ROWS, 16)
    nk = n // tk

    def body(a_ref, x_ref, o_ref):
        part = _dot(a_ref[...], x_ref[...], TN)

        @pl.when(pl.program_id(1) == 0)
        def _():
            o_ref[...] = part

        @pl.when(pl.program_id(1) > 0)
        def _():
            o_ref[...] += part

    return _pcall(body, name=name, grid=(BD_TILES, nk),
                  in_specs=[pl.BlockSpec((tk, BD_CH), lambda j, kk: (kk, j % 4)), pl.BlockSpec((tk, BD_ST), lambda j, kk: (kk, j))],
                  out_specs=pl.BlockSpec((None, BD_CH, BD_ST), lambda j, kk: (j, 0, 0)),
                  out_shape=jax.ShapeDtypeStruct((BD_TILES, BD_CH, BD_ST), F32), compiler_params=_params())(a, x)


def _permute_rows(t):
    n = t.shape[0]
    return t.reshape(SEG, n // SEG, t.shape[1]).transpose(1, 0, 2).reshape(n, t.shape[1])


def _unpermute_rows(t):
    n = t.shape[0]
    return t.reshape(n // SEG, SEG, t.shape[1]).transpose(1, 0, 2).reshape(n, t.shape[1])


def _segment_powers(ar, ai, seg_steps):
    pr, pi = ar.reshape(1, S5_GP), ai.reshape(1, S5_GP)
    e = 1
    while e < seg_steps:
        pr, pi = pr * pr - pi * pi, 2.0 * pr * pi
        e *= 2
    assert e == seg_steps, "segment length must be a power of two"
    rows_r, rows_i = [], []
    for _ in range(3):
        rows_r.append(pr)
        rows_i.append(pi)
        pr, pi = pr * pr - pi * pi, 2.0 * pr * pi
    pad = jnp.zeros((SEG - 3, S5_GP), F32)
    return jnp.concatenate(rows_r + [pad], axis=0), jnp.concatenate(rows_i + [pad], axis=0)


NT = (((1,), (1,)), ((), ()))
TN = (((0,), (0,)), ((), ()))


def _dot(a, b, dims=None, exact=False):
    dims = (((1,), (0,)), ((), ())) if dims is None else dims
    if exact:
        return lax.dot_general(a, b, dims, precision=HI, preferred_element_type=F32)
    return lax.dot_general(a.astype(BF16), b.astype(BF16), dims, preferred_element_type=F32)


def _dot01(a, b, dims=None, ones_first=True):
    x = b if ones_first else a
    hi = x.astype(BF16)
    r1 = x - hi.astype(F32)
    mid = r1.astype(BF16)
    lo = (r1 - mid.astype(F32)).astype(BF16)
    parts = [(_dot(a, p, dims) if ones_first else _dot(p, b, dims)) for p in (lo, mid, hi)]
    return (parts[0] + parts[1]) + parts[2]


HEADS = range(4)


def _gla_chunk_fwd(qc, kc, vc, al, wup, bup, s_prev, tril):
    ones = jnp.ones((GLA_CHUNK, GLA_DV), F32)
    z = [_dot(al, wup[h]) + bup[h] for h in HEADS]
    la = [(jnp.minimum(z[h], 0.0) - jnp.log(1.0 + jnp.exp(-jnp.abs(z[h])))) * (1.0 / GLA_TAU) for h in HEADS]
    bc = [_dot01(tril, la[h]) for h in HEADS]
    blb = [_dot01(la[h], ones, TN, ones_first=False) for h in HEADS]
    bl = [bc[h][GLA_CHUNK - 1:GLA_CHUNK, :] for h in HEADS]
    ebc = [jnp.exp(bc[h]) for h in HEADS]
    qt = [qc[h] * (GLA_DK ** -0.5) * ebc[h] for h in HEADS]
    kt = [kc[h] * jnp.exp(-bc[h]) for h in HEADS]
    ke = [kc[h] * jnp.exp(bl[h] - bc[h]) for h in HEADS]
    sc = [_dot(qt[h], kt[h], NT) * tril for h in HEADS]
    oi = [_dot(sc[h], vc[h]) for h in HEADS]
    oo = [_dot(qt[h], s_prev[h]) for h in HEADS]
    o = [oi[h] + oo[h] for h in HEADS]
    return z, bc, bl, blb, ebc, qt, kt, ke, sc, o


GLA_ROWS = 512
GLA_CPB = GLA_ROWS // GLA_CHUNK


ZA_COLS = 5 * 512
SLOT = 128


def _pad_heads(w):
    r = w.shape[0]
    return jnp.pad(w.reshape(r, GLA_HEADS, GLA_DK), ((0, 0), (0, 0), (0, SLOT - GLA_DK))).reshape(r, GLA_HEADS * SLOT)


def _unpad_heads(w):
    r = w.shape[0]
    return w.reshape(r, GLA_HEADS, SLOT)[:, :, :GLA_DK].reshape(r, GLA_HEADS * GLA_DK)


def _gla_token_specs(blk):
    col = lambda cb: pl.BlockSpec((GLA_ROWS, 512), lambda j: (blk(j), cb))
    whole = lambda a: pl.BlockSpec(a.shape, lambda j: (0,) * a.ndim)
    return col, whole


def _head_ds(h, width):
    return pl.ds(h * SLOT, width)


def _tri(lower):
    ri = lax.broadcasted_iota(jnp.int32, (GLA_CHUNK, GLA_CHUNK), 0)
    ci = lax.broadcasted_iota(jnp.int32, (GLA_CHUNK, GLA_CHUNK), 1)
    return ((ri >= ci) if lower else (ri <= ci)).astype(F32)


def _gla_fwd(za, al, wup, bup, gn, name):
    n = za.shape[0]
    nc = n // GLA_CHUNK

    def body(q_ref, k_ref, v_ref, r_ref, al_ref, wup_ref, bup_ref, gn_ref, y_ref, sp_ref, s_ref):
        @pl.when(pl.program_id(0) == 0)
        def _():
            s_ref[...] = jnp.zeros_like(s_ref)

        tril = _tri(True)

        def chunk(c, carry):
            rows = pl.ds(pl.multiple_of(c * GLA_CHUNK, GLA_CHUNK), GLA_CHUNK)
            alc = al_ref[rows, :]
            vc = [v_ref[rows, _head_ds(h, GLA_DV)] for h in HEADS]
            s_prev = [s_ref[h] for h in HEADS]
            _, _, _, blb, _, _, _, ke, _, o = _gla_chunk_fwd(
                [q_ref[rows, _head_ds(h, GLA_DK)] for h in HEADS], [k_ref[rows, _head_ds(h, GLA_DK)] for h in HEADS],
                vc, alc, [wup_ref[h] for h in HEADS], [bup_ref[h] for h in HEADS], s_prev, tril)
            ds = [_dot(ke[h], vc[h], TN) for h in HEADS]
            for h in HEADS:
                rc = r_ref[rows, _head_ds(h, GLA_DV)]
                sp_ref[h, c] = s_prev[h]
                rstd = lax.rsqrt(jnp.mean(o[h] * o[h], axis=-1, keepdims=True) + EPS)
                y_ref[rows, _head_ds(h, GLA_DV)] = (o[h] * rstd * gn_ref[h] * (rc * _sigmoid(rc))).astype(BF16)
                s_ref[h] = jnp.exp(blb[h]) * s_prev[h] + ds[h]
            return carry

        lax.fori_loop(0, GLA_CPB, chunk, 0)

    col, whole = _gla_token_specs(lambda j: j)
    return _pcall(body, name=name, grid=(n // GLA_ROWS,),
                  in_specs=[col(1), col(2), col(3), col(4), pl.BlockSpec((GLA_ROWS, LANE), lambda j: (j, 0)),
                            whole(wup), whole(bup), whole(gn)],
                  out_specs=[pl.BlockSpec((GLA_ROWS, GLA_HEADS * GLA_DV), lambda j: (j, 0)),
                             pl.BlockSpec((GLA_HEADS, GLA_CPB, GLA_DK, GLA_DV), lambda j: (0, j, 0, 0))],
                  out_shape=[jax.ShapeDtypeStruct((n, GLA_HEADS * GLA_DV), BF16),
                             jax.ShapeDtypeStruct((GLA_HEADS, nc, GLA_DK, GLA_DV), F32)],
                  scratch_shapes=[pltpu.VMEM((GLA_HEADS, GLA_DK, GLA_DV), F32)],
                  compiler_params=_params())(za, za, za, za, al, wup, bup, gn)


def _gla_bwd(za, al, wup, bup, gn, sp, dy, du_s5, name):
    n = za.shape[0]
    nb = n // GLA_ROWS

    def body(q_ref, k_ref, v_ref, r_ref, al_ref, wup_ref, bup_ref, gn_ref, dy_ref, dus_ref, sp_ref,
             dza_ref, dz_ref, dgn_ref, dbup_ref, ds_ref):
        @pl.when(pl.program_id(0) == 0)
        def _():
            ds_ref[...] = jnp.zeros_like(ds_ref)
            dgn_ref[...] = jnp.zeros_like(dgn_ref)
            dbup_ref[...] = jnp.zeros_like(dbup_ref)

        tril, triu = _tri(True), _tri(False)
        dza_ref[:, 0:512] = dus_ref[...]
        dza_ref[:, 512:1536] = jnp.zeros((GLA_ROWS, 1024), F32)
        dz_ref[...] = jnp.zeros_like(dz_ref)

        def chunk(i, carry):
            c = GLA_CPB - 1 - i
            rows = pl.ds(pl.multiple_of(c * GLA_CHUNK, GLA_CHUNK), GLA_CHUNK)
            alc = al_ref[rows, :]
            qc = [q_ref[rows, _head_ds(h, GLA_DK)] for h in HEADS]
            kc = [k_ref[rows, _head_ds(h, GLA_DK)] for h in HEADS]
            vc = [v_ref[rows, _head_ds(h, GLA_DV)] for h in HEADS]
            s_prev = [sp_ref[h, c] for h in HEADS]
            ds = [ds_ref[h] for h in HEADS]
            z, bc, bl, blb, ebc, qt, kt, ke, sc, o = _gla_chunk_fwd(
                qc, kc, vc, alc, [wup_ref[h] for h in HEADS], [bup_ref[h] for h in HEADS], s_prev, tril)
            do = []
            for h in HEADS:
                rc = r_ref[rows, _head_ds(h, GLA_DV)]
                rs = lax.rsqrt(jnp.mean(o[h] * o[h], axis=-1, keepdims=True) + EPS)
                on = o[h] * rs
                sr = _sigmoid(rc)
                sil = rc * sr
                dyv, gnv = dy_ref[rows, _head_ds(h, GLA_DV)], gn_ref[h]
                dgn_ref[h] += jnp.sum(dyv * on * sil, axis=0, keepdims=True)
                dza_ref[rows, pl.ds(2048 + h * SLOT, GLA_DV)] = dyv * on * gnv * (sr * (1.0 + rc * (1.0 - sr)))
                don = dyv * gnv * sil
                do.append(rs * (don - on * jnp.mean(don * on, axis=-1, keepdims=True)))
            dp = [_dot(do[h], vc[h], NT) * tril for h in HEADS]
            dv1 = [_dot(sc[h], do[h], TN) for h in HEADS]
            dv2 = [_dot(ke[h], ds[h]) for h in HEADS]
            dq2 = [_dot(do[h], s_prev[h], NT) for h in HEADS]
            dke = [_dot(vc[h], ds[h], NT) for h in HEADS]
            ddec = [_dot01(jnp.ones((8, GLA_DV), F32), ds[h] * s_prev[h], NT)[0:1, :] for h in HEADS]
            dsn = [_dot(qt[h], do[h], TN) for h in HEADS]
            dq1 = [_dot(dp[h], kt[h]) for h in HEADS]
            dkt = [_dot(dp[h], qt[h], TN) for h in HEADS]
            dbc, dbl = [], []
            for h in HEADS:
                dqt = dq1[h] + dq2[h]
                dza_ref[rows, pl.ds(1536 + h * SLOT, GLA_DV)] = dv1[h] + dv2[h]
                ds_ref[h] = jnp.exp(blb[h]) * ds[h] + dsn[h]
                dza_ref[rows, pl.ds(512 + h * SLOT, GLA_DK)] = dqt * (GLA_DK ** -0.5) * ebc[h]
                dza_ref[rows, pl.ds(1024 + h * SLOT, GLA_DK)] = dkt[h] * jnp.exp(-bc[h]) + dke[h] * jnp.exp(bl[h] - bc[h])
                dbc.append(dqt * qt[h] - dkt[h] * kt[h] - dke[h] * ke[h])
                dbl.append(jnp.sum(dke[h] * ke[h], axis=0, keepdims=True) + ddec[h] * jnp.exp(bl[h]))
            dla = [_dot01(triu, dbc[h]) + dbl[h] for h in HEADS]
            for h in HEADS:
                dz = dla[h] * (1.0 - _sigmoid(z[h])) * (1.0 / GLA_TAU)
                dz_ref[rows, _head_ds(h, GLA_DK)] = dz
                dbup_ref[h] += jnp.sum(dz, axis=0, keepdims=True)
            return carry

        lax.fori_loop(0, GLA_CPB, chunk, 0)

    rev = lambda j: nb - 1 - j
    col, whole = _gla_token_specs(rev)
    tok = lambda w: pl.BlockSpec((GLA_ROWS, w), lambda j: (rev(j), 0))
    h1 = lambda w: pl.BlockSpec((GLA_HEADS, 1, w), lambda j: (0, 0, 0))
    s1 = lambda w: jax.ShapeDtypeStruct((GLA_HEADS, 1, w), F32)
    return _pcall(body, name=name, grid=(nb,),
                  in_specs=[col(1), col(2), col(3), col(4), tok(LANE), whole(wup), whole(bup), whole(gn), tok(512), tok(512),
                            pl.BlockSpec((GLA_HEADS, GLA_CPB, GLA_DK, GLA_DV), lambda j: (0, rev(j), 0, 0))],
                  out_specs=[tok(ZA_COLS), tok(GLA_HEADS * SLOT), h1(GLA_DV), h1(GLA_DK)],
                  out_shape=[jax.ShapeDtypeStruct((n, ZA_COLS), F32), jax.ShapeDtypeStruct((n, GLA_HEADS * SLOT), F32),
                             s1(GLA_DV), s1(GLA_DK)],
                  scratch_shapes=[pltpu.VMEM((GLA_HEADS, GLA_DK, GLA_DV), F32)],
                  compiler_params=_params())(za, za, za, za, al, wup, bup, gn, dy, du_s5, sp)


ANY = pl.BlockSpec(memory_space=pl.ANY)


def _place():
    x, y, c = lax.axis_index("x"), lax.axis_index("y"), lax.axis_index("c")
    chips = [(1 - x, y), (x, 1 - y), (1 - x, 1 - y)]
    return x, y, c, chips


def _remote(src, dst, ssem, rsem, dev):
    return pltpu.make_async_remote_copy(src_ref=src, dst_ref=dst, send_sem=ssem, recv_sem=rsem, device_id=dev,
                                        device_id_type=MESH_ID)


def _half(c, rows):
    h = rows // 2
    return pl.ds(pl.multiple_of(c * h, 8), h)


def _side_gather_ici(shards):
    def copies(ins, outs, ssem, rsem):
        x, y, c, chips = _place()
        mine = 2 * x + y
        cps = []
        for w in range(len(ins)):
            half = _half(c, ins[w].shape[0])
            cps.append(_remote(ins[w], outs[w].at[mine], ssem.at[4 * w], rsem.at[4 * w], (x, y, 1 - c)))
            for k, (px, py) in enumerate(chips):
                cps.append(_remote(ins[w].at[half], outs[w].at[mine, half], ssem.at[4 * w + 1 + k], rsem.at[4 * w + 1 + k],
                                   (px, py, c)))
        return cps

    return _Side(shards, [jax.ShapeDtypeStruct((4,) + s.shape, s.dtype) for s in shards], 4 * len(shards), copies)


def _side_gather_d2d(gathered):
    def copies(ins, outs, ssem, rsem):
        x, y, c, chips = _place()
        cps = []
        for w in range(len(outs)):
            half = _half(c, outs[w].shape[1])
            for k, (px, py) in enumerate(chips):
                theirs = outs[w].at[2 * px + py, half]
                cps.append(_remote(theirs, theirs, ssem.at[3 * w + k], rsem.at[3 * w + k], (x, y, 1 - c)))
        return cps

    return _Side(gathered, [jax.ShapeDtypeStruct(g.shape, g.dtype) for g in gathered], 3 * len(gathered), copies,
                 aliased=True)


def _side_swap_halves(grads):
    def copies(ins, outs, ssem, rsem):
        x, y, c, _ = _place()
        return [_remote(ins[w].at[:, _half(1 - c, ins[w].shape[1]), :], outs[w], ssem.at[w], rsem.at[w], (x, y, 1 - c))
                for w in range(len(ins))]

    return _Side(grads, [jax.ShapeDtypeStruct((4, g.shape[1] // 2, g.shape[2]), g.dtype) for g in grads], len(grads), copies)


def _side_scatter(sums):
    def copies(ins, outs, ssem, rsem):
        x, y, c, chips = _place()
        return [_remote(ins[w].at[2 * px + py], outs[w].at[k], ssem.at[3 * w + k], rsem.at[3 * w + k], (px, py, c))
                for w in range(len(ins)) for k, (px, py) in enumerate(chips)]

    return _Side(sums, [jax.ShapeDtypeStruct((3,) + s.shape[1:], s.dtype) for s in sums], 3 * len(sums), copies)


def _side_swap_reduced(halves):
    def copies(ins, outs, ssem, rsem):
        x, y, c, _ = _place()
        return [_remote(ins[w], outs[w], ssem.at[w], rsem.at[w], (x, y, 1 - c)) for w in range(len(ins))]

    return _Side(halves, [jax.ShapeDtypeStruct(h.shape, h.dtype) for h in halves], len(halves), copies)


def _chip_sum(g, recv, c_arr, name):
    _, r, cols = g.shape
    h = r // 2
    tr = _pick(h, 512, 16)
    g4 = g.reshape(4, 2, h, cols)

    def body(c_ref, g_ref, r_ref, o_ref):
        o_ref[...] = (g_ref[...] + r_ref[...]).astype(BF16)

    grid_spec = pltpu.PrefetchScalarGridSpec(
        num_scalar_prefetch=1, grid=(4, h // tr),
        in_specs=[pl.BlockSpec((None, None, tr, cols), lambda s, i, c_ref: (s, c_ref[0], i, 0)),
                  pl.BlockSpec((None, tr, cols), lambda s, i, c_ref: (s, i, 0))],
        out_specs=pl.BlockSpec((None, tr, cols), lambda s, i, c_ref: (s, i, 0)))
    return _pcall(body, name=name, grid_spec=grid_spec, out_shape=jax.ShapeDtypeStruct((4, h, cols), BF16),
                  compiler_params=_params())(c_arr, g4, recv)


def _owner_sum(sums, others, s_arr, name):
    _, h, cols = sums.shape
    tr = _pick(h, 512, 16)

    def body(s_ref, a_ref, o_ref, out_ref):
        f = lambda v: v.astype(F32)
        out_ref[...] = (f(a_ref[...]) + f(o_ref[0])) + (f(o_ref[1]) + f(o_ref[2]))

    grid_spec = pltpu.PrefetchScalarGridSpec(
        num_scalar_prefetch=1, grid=(h // tr,),
        in_specs=[pl.BlockSpec((None, tr, cols), lambda i, s_ref: (s_ref[0], i, 0)),
                  pl.BlockSpec((3, tr, cols), lambda i, s_ref: (0, i, 0))],
        out_specs=pl.BlockSpec((tr, cols), lambda i, s_ref: (i, 0)))
    return _pcall(body, name=name, grid_spec=grid_spec, out_shape=jax.ShapeDtypeStruct((h, cols), F32),
                  compiler_params=_params())(s_arr, sums, others)


def _allreduce_small(v, name):
    def body(v_ref, o_ref, r0, r1, ssem, rsem):
        x, y, c, chips = _place()
        cp = _remote(v_ref, r0, ssem.at[0], rsem.at[0], (x, y, 1 - c))
        cp.start()
        cp.wait()
        o_ref[...] = v_ref[...] + r0[...]
        cps = []
        for k, (px, py) in enumerate(chips):
            cp = _remote(o_ref, r1.at[k], ssem.at[1 + k], rsem.at[1 + k], (px, py, c))
            cp.start()
            cps.append(cp)
        for cp in cps:
            cp.wait()
        o_ref[...] = (o_ref[...] + r1[0]) + (r1[1] + r1[2])

    vm = pl.BlockSpec(memory_space=pltpu.VMEM)
    return _pcall(body, name=name, in_specs=[vm], out_specs=vm, out_shape=jax.ShapeDtypeStruct(v.shape, F32),
                  scratch_shapes=[pltpu.VMEM(v.shape, F32), pltpu.VMEM((3,) + v.shape, F32),
                                  pltpu.SemaphoreType.DMA((4,)), pltpu.SemaphoreType.DMA((4,))],
                  compiler_params=_params(has_side_effects=True))(v)


def _tile_rows(size):
    return -(-size // (8 * LANE)) * 8


def _pack_small(parts):
    pieces = []
    for p in parts:
        flat = p.reshape(-1).astype(F32)
        pieces.append(jnp.pad(flat, (0, _tile_rows(p.size) * LANE - p.size)).reshape(-1, LANE))
    rows = sum(x.shape[0] for x in pieces)
    pieces.append(jnp.zeros(((-rows) % 64, LANE), F32))
    return jnp.concatenate(pieces, axis=0)


def _unpack_small(packed, like):
    out, pos = [], 0
    for p in like:
        rows = _tile_rows(p.size)
        out.append(packed[pos:pos + rows].reshape(-1)[:p.size].reshape(p.shape))
        pos += rows
    return out


FFN_FWD_ROWS, FFN_BWD_ROWS = 1024, 512
FFN_SUB_ROWS = 256


def _ffn_specs(n, d, fs, cap):
    rows = _pick(n, cap, 16)
    row = pl.BlockSpec((rows, d), lambda i, s: (i, 0))
    gain = pl.BlockSpec((1, d), lambda i, s: (0, 0))
    w_row = pl.BlockSpec((None, fs, d), lambda i, s: (s, 0, 0))
    hid = pl.BlockSpec((None, rows, fs), lambda i, s: (s, i, 0))
    return rows, row, gain, w_row, hid


def _ffn_fwd(h, g, w1t, w3t, w2, tag, plan):
    n, d = h.shape
    ns, fs, _ = w2.shape
    rows, row, gain, w_row, hid = _ffn_specs(n, d, fs, FFN_FWD_ROWS)
    sub = rows

    def body(h_ref, g_ref, w1_ref, w3_ref, w2_ref, out_ref, n1_ref, a_ref, b_ref, hm_ref, acc_ref):
        s = pl.program_id(1)

        @pl.when(s == 0)
        def _():
            xv = h_ref[...]
            rstd = lax.rsqrt(jnp.mean(xv * xv, axis=-1, keepdims=True) + EPS)
            n1_ref[...] = (xv * rstd * g_ref[...]).astype(BF16)
            acc_ref[...] = jnp.zeros_like(acc_ref)

        def up(j):
            n1 = n1_ref[j * sub:(j + 1) * sub, :]
            return _dot(n1, w1_ref[...], NT), _dot(n1, w3_ref[...], NT)

        cur = up(0)
        for j in range(rows // sub):
            nxt = up(j + 1) if (j + 1) * sub < rows else None
            a, b = cur
            r = slice(j * sub, (j + 1) * sub)
            hm = (a * _sigmoid(a) * b).astype(BF16)
            a_ref[r, :] = a.astype(BF16)
            b_ref[r, :] = b.astype(BF16)
            hm_ref[r, :] = hm
            acc_ref[r, :] += _dot(hm, w2_ref[...])
            cur = nxt

        @pl.when(s == ns - 1)
        def _():
            out_ref[...] = h_ref[...] + 0.5 * acc_ref[...]

    hid_shape = jax.ShapeDtypeStruct((ns, n, fs), BF16)
    plan.before(f"{tag}_fwd")
    out, n1, a, b, hm = _pcall(
        body, name=f"{tag}_fwd", grid=(n // rows, ns), in_specs=[row, gain, w_row, w_row, w_row],
        out_specs=[row, row, hid, hid, hid],
        out_shape=[jax.ShapeDtypeStruct((n, d), F32), jax.ShapeDtypeStruct((n, d), BF16), hid_shape, hid_shape, hid_shape],
        scratch_shapes=[pltpu.VMEM((rows, d), F32)], compiler_params=_params())(h, g, w1t, w3t, w2)
    plan.after(f"{tag}_fwd")
    return out, (h, n1, a, b, hm)


def _ffn_bwd(dout, saved, g, w1, w3, w2, tag, plan):
    h, n1, a, b, hm = saved
    n, d = h.shape
    ns, fs, _ = w2.shape
    rows, row, gain, w_row, hid = _ffn_specs(n, d, fs, FFN_BWD_ROWS)
    sub = _pick(rows, FFN_SUB_ROWS, 16)

    def body(do_ref, h_ref, g_ref, a_ref, b_ref, w1_ref, w3_ref, w2_ref, dh_ref, da_ref, db_ref, dg_ref, acc_ref):
        i, s = pl.program_id(0), pl.program_id(1)

        @pl.when(s == 0)
        def _():
            acc_ref[...] = jnp.zeros_like(acc_ref)

        @pl.when((s == 0) & (i == 0))
        def _():
            dg_ref[...] = jnp.zeros_like(dg_ref)

        def up(j):
            return _dot(0.5 * do_ref[j * sub:(j + 1) * sub, :], w2_ref[...], NT)

        cur = up(0)
        for j in range(rows // sub):
            nxt = up(j + 1) if (j + 1) * sub < rows else None
            r = slice(j * sub, (j + 1) * sub)
            av, bv = a_ref[r, :].astype(F32), b_ref[r, :].astype(F32)
            sg = _sigmoid(av)
            da = (cur * bv * (sg * (1.0 + av * (1.0 - sg)))).astype(BF16)
            db = (cur * av * sg).astype(BF16)
            da_ref[r, :] = da
            db_ref[r, :] = db
            acc_ref[r, :] += _dot(da, w1_ref[...]) + _dot(db, w3_ref[...])
            cur = nxt

        @pl.when(s == ns - 1)
        def _():
            xv, dn = h_ref[...], acc_ref[...]
            rstd = lax.rsqrt(jnp.mean(xv * xv, axis=-1, keepdims=True) + EPS)
            xh = xv * rstd
            dg_ref[...] += jnp.sum(dn * xh, axis=0, keepdims=True)
            dxh = dn * g_ref[...]
            dh_ref[...] = do_ref[...] + rstd * (dxh - xh * jnp.mean(dxh * xh, axis=-1, keepdims=True))

    hid_shape = jax.ShapeDtypeStruct((ns, n, fs), BF16)
    plan.before(f"{tag}_bwd")
    dh, da, db, dg = _pcall(
        body, name=f"{tag}_bwd", grid=(n // rows, ns), in_specs=[row, row, gain, hid, hid, w_row, w_row, w_row],
        out_specs=[row, hid, hid, gain],
        out_shape=[jax.ShapeDtypeStruct((n, d), F32), hid_shape, hid_shape, jax.ShapeDtypeStruct((1, d), F32)],
        scratch_shapes=[pltpu.VMEM((rows, d), F32)], compiler_params=_params())(dout, h, g, a, b, w1, w3, w2)
    plan.after(f"{tag}_bwd")
    plan.before(f"{tag}_gw2")
    gw2 = _mm(hm, dout, ta=True, shard='m', alpha=0.5, name=f"{tag}_gw2")
    plan.after(f"{tag}_gw2")
    gw1 = _mm(da, n1, ta=True, shard='m', name=f"{tag}_gw1")
    gw3 = _mm(db, n1, ta=True, shard='m', name=f"{tag}_gw3")
    return dh, dg, gw1, gw3, gw2


def _local_step(x, tgt, plan):
    n = x.shape[0]
    grads = plan.grads

    def f(name):
        w = plan.get(name)
        return w.reshape(1, D_MODEL) if name.endswith('_norm') and name != 'gla_out_norm' else w

    def carried(tag, fn, *args, **kw):
        plan.before(tag)
        out = fn(*args, **kw)
        plan.after(tag)
        return out

    h1, ffn1 = _ffn_fwd(x, f('ffn1_norm'), f('ffn1_w1'), f('ffn1_w3'), f('ffn1_w2'), "ffn1", plan)
    u = carried("mix_rms", _rms_fwd, h1, f('mix_norm'), "mix_rms")
    w_in = f('w_in')
    w_a = jnp.concatenate([w_in[:, :512], _pad_heads(w_in[:, 512:768]), _pad_heads(w_in[:, 768:1024]), w_in[:, 1024:2048]],
                          axis=1)
    w_al = jnp.pad(w_in[:, 2048:2048 + GLA_RANK], ((0, 0), (0, LANE - GLA_RANK)))
    w_g = w_in[:, 2048 + GLA_RANK:]
    za = _mm(u, w_a, name="in_a")
    zg = _mm(u, w_g, name="in_g")
    al = _mm(u, w_al, name="in_al")
    ar, ai, bbar_re, bbar_im = _s5_discretize(f('s5_lambda_re'), f('s5_lambda_im'), f('s5_log_dt'), f('s5_b_re'), f('s5_b_im'))
    t_b = _bd_tiles(bbar_re.transpose(0, 2, 1), bbar_im.transpose(0, 2, 1)).astype(BF16)
    t_c = _bd_tiles(f('s5_c_re'), -f('s5_c_im')).astype(BF16)
    ar8 = jnp.broadcast_to(ar.reshape(1, S5_GP), (SEG, S5_GP))
    ai8 = jnp.broadcast_to(ai.reshape(1, S5_GP), (SEG, S5_GP))
    pw_r, pw_i = _segment_powers(ar, ai, n // SEG)
    dskip = f('s5_d').reshape(1, S5_W)
    u_s5 = _permute_rows(za[:, :S5_W])
    bu = _bd_expand(u_s5, t_b, "s5_bu")
    xs = _s5_scan(bu, ar8, ai8, pw_r, pw_i, "s5_scan")
    ys_p = _bd_reduce(xs, t_c, _scale_rows(u_s5, dskip, "s5_skip"), "s5_y")
    ys = _unpermute_rows(ys_p)
    zgelu = _gelu_fwd(ys, "s5_gelu")
    t_glu = _mm(zgelu, f('s5_glu_w'), bias=f('s5_glu_b').reshape(1, S5_W), name="s5_glu_t")
    y_s5 = _glu_fwd(zgelu, t_glu, "s5_glu")
    wup = jnp.pad(f('gla_a_up_w'), ((0, LANE - GLA_RANK), (0, 0)))
    wup_h = wup.reshape(LANE, GLA_HEADS, GLA_DK).transpose(1, 0, 2)
    bup_h = f('gla_a_up_b').reshape(GLA_HEADS, 1, GLA_DK)
    gn_h = f('gla_out_norm').reshape(GLA_HEADS, 1, GLA_DV)
    y_gla, s_prev = carried("gla_fwd", _gla_fwd, za, al, wup_h, bup_h, gn_h, "gla_fwd")
    ps = _mm(y_s5, f('proj_s5'), name="proj_s5")
    pg = carried("proj_gla", _mm, y_gla, f('proj_gla'), name="proj_gla")
    merged = _merge_fwd(zg, ps, pg, "merge")
    h2 = _mm(merged, f('w_out'), res=h1, name="w_out")
    h3, ffn2 = _ffn_fwd(h2, f('ffn2_norm'), f('ffn2_w1'), f('ffn2_w3'), f('ffn2_w2'), "ffn2", plan)
    loss, dh3, g_final = _final_loss(h3, f('final_norm').reshape(1, D_MODEL), tgt, "loss")
    grads['final_norm'] = g_final.reshape(D_MODEL)
    dh2, grads['ffn2_norm'], grads['ffn2_w1'], grads['ffn2_w3'], grads['ffn2_w2'] = _ffn_bwd(
        dh3, ffn2, f('ffn2_norm'), f('ffn2_w1'), f('ffn2_w3'), f('ffn2_w2'), "ffn2", plan)
    dm = _mm(dh2, f('w_out'), tb=True, name="d_merged")
    grads['w_out'] = _mm(merged, dh2, ta=True, name="g_w_out")
    dps, dpg, dzg = carried("d_merge", _merge_bwd, dm, zg, ps, pg, "d_merge")
    grads['proj_s5'] = _mm(y_s5, dps, ta=True, name="g_proj_s5")
    grads['proj_gla'] = _mm(y_gla, dpg, ta=True, name="g_proj_gla")
    dy_s5 = _mm(dps, f('proj_s5'), tb=True, name="d_y_s5")
    dy_gla = _mm(dpg, f('proj_gla'), tb=True, name="d_y_gla")
    dzgelu, dt_glu, g_glu_b = _glu_bwd1(dy_s5, zgelu, t_glu, "d_glu")
    grads['s5_glu_b'] = g_glu_b.reshape(S5_W)
    grads['s5_glu_w'] = _mm(zgelu, dt_glu, ta=True, name="g_glu_w")
    dzgelu = _mm(dt_glu, f('s5_glu_w'), tb=True, res=dzgelu, name="d_gelu")
    dys, du_skip, g_d = _glu_bwd2(_permute_rows(dzgelu), ys_p, u_s5, dskip, "d_s5_y")
    grads['s5_d'] = g_d.reshape(S5_G, S5_H)
    gx = _bd_expand(dys, t_c, "s5_gx")
    lam, da8 = _s5_scan_bwd(gx, xs, ar8, ai8, pw_r, pw_i, "s5_scan_bwd")
    g_c = _bd_blocks(_bd_outer(dys, xs, "g_s5_c"))
    grads['s5_c_re'], grads['s5_c_im'] = g_c[0], -g_c[1]
    g_b = _bd_blocks(_bd_outer(u_s5, lam, "g_s5_b")).transpose(0, 1, 3, 2)
    g_bbar_re, g_bbar_im = g_b[0], g_b[1]
    da = jnp.sum(da8, axis=0)
    g_ar, g_ai = da[:S5_GP].reshape(S5_G, S5_P), da[S5_GP:].reshape(S5_G, S5_P)
    _, disc_vjp = jax.vjp(_s5_discretize, f('s5_lambda_re'), f('s5_lambda_im'), f('s5_log_dt'), f('s5_b_re'), f('s5_b_im'))
    (grads['s5_lambda_re'], grads['s5_lambda_im'], grads['s5_log_dt'], grads['s5_b_re'],
     grads['s5_b_im']) = disc_vjp((g_ar, g_ai, g_bbar_re, g_bbar_im))
    du_s5 = _unpermute_rows(_bd_reduce(lam, t_b, du_skip, "d_s5_u"))
    dza, dz, dgn, dbup = carried("gla_bwd", _gla_bwd, za, al, wup_h, bup_h, gn_h, s_prev, dy_gla, du_s5, "gla_bwd")
    grads['gla_out_norm'] = dgn.reshape(GLA_HEADS * GLA_DV)
    grads['gla_a_up_b'] = dbup.reshape(GLA_HEADS * GLA_DK)
    grads['gla_a_up_w'] = _unpad_heads(_mm(al, dz, ta=True, name="g_a_up")[:GLA_RANK])
    dal = _mm(dz, _pad_heads(wup), tb=True, name="d_a_low")
    g_wa = _mm(u, dza, ta=True, name="g_in_a")
    g_wg = _mm(u, dzg, ta=True, name="g_in_g")
    g_wal = _mm(u, dal, ta=True, name="g_in_al")
    grads['w_in'] = jnp.concatenate([g_wa[:, :512], _unpad_heads(g_wa[:, 512:1024]), _unpad_heads(g_wa[:, 1024:1536]),
                                     g_wa[:, 1536:], g_wal[:, :GLA_RANK], g_wg], axis=1)
    du = carried("d_u_a", _mm, dza, w_a, tb=True, name="d_u_a")
    du = _mm(dzg, w_g, tb=True, res=du, name="d_u_g")
    du = _mm(dal, w_al, tb=True, res=du, name="d_u_al")
    dh1, g_mix = carried("d_mix_rms", _rms_bwd, h1, f('mix_norm'), du, dh2, "d_mix_rms")
    grads['mix_norm'] = g_mix
    dx, grads['ffn1_norm'], grads['ffn1_w1'], grads['ffn1_w3'], grads['ffn1_w2'] = _ffn_bwd(
        dh1, ffn1, f('ffn1_norm'), f('ffn1_w1'), f('ffn1_w3'), f('ffn1_w2'), "ffn1", plan)
    return loss[0, 0], dx


MIXER_WEIGHTS = ['w_in', 's5_glu_w', 'proj_s5', 'proj_gla', 'w_out', 'gla_a_up_w']
FFN1_WEIGHTS, FFN2_WEIGHTS = FFN_WEIGHTS[:3], FFN_WEIGHTS[3:]
TRANSPOSED = ['ffn1_w1', 'ffn1_w3', 'ffn2_w1', 'ffn2_w3']


def _local_shard(w, nm):
    return jnp.swapaxes(w, 1, 2)[0] if nm in TRANSPOSED else w[0]
GRAD_GROUPS = {'ffn2': FFN2_WEIGHTS, 'mixer': ['w_out', 'proj_s5', 'proj_gla', 's5_glu_w', 'w_in'], 'ffn1': FFN1_WEIGHTS}


class _Plan:
    def __init__(self, a, c_arr, s_arr):
        self.a, self.c_arr, self.s_arr = a, c_arr, s_arr
        self.grads, self.weights, self.riding = {}, {}, {}
        self.g4s, self.chip_sums, self.halves, self.sib_halves = {}, {}, {}, {}
        for nm in SMALL:
            if nm != 'gla_a_up_w':
                self.weights[nm] = a[nm] if nm == 'final_norm' else a[nm][0]
        ici = _side_gather_ici(self._shards(FFN1_WEIGHTS))
        _run_side(ici, "gather_ffn1_ici")
        self._gathered(FFN1_WEIGHTS, _run_side(_side_gather_d2d(ici.outs), "gather_ffn1_d2d"))

    def _shards(self, names):
        return [_local_shard(self.a[nm], nm).astype(F32 if nm == 'gla_a_up_w' else BF16) for nm in names]

    def _gathered(self, names, arrs):
        for nm, g4 in zip(names, arrs):
            if nm in FFN_WEIGHTS:
                self.weights[nm] = g4
            elif nm in COL_SHARDED:
                self.weights[nm] = jnp.concatenate([g4[s] for s in range(4)], axis=1)
            else:
                self.weights[nm] = g4.reshape(4 * g4.shape[1], g4.shape[2])

    def get(self, name):
        return self.weights[name]

    def _shard_major(self, nm):
        g = self.grads[nm]
        if nm in FFN_WEIGHTS:
            return g
        if nm in COL_SHARDED:
            return jnp.stack(jnp.split(g, 4, axis=1))
        return g.reshape(4, g.shape[0] // 4, g.shape[1])

    def _schedule(self, tag):
        grp = GRAD_GROUPS
        if tag == "ffn1_fwd":
            return _side_gather_ici(self._shards(MIXER_WEIGHTS)), lambda outs: self.riding.update(mixer_ici=outs)
        if tag == "mix_rms":
            return _side_gather_d2d(self.riding['mixer_ici']), lambda outs: self._gathered(MIXER_WEIGHTS, outs)
        if tag == "gla_fwd":
            return _side_gather_ici(self._shards(FFN2_WEIGHTS)), lambda outs: self.riding.update(ffn2_ici=outs)
        if tag == "proj_gla":
            return _side_gather_d2d(self.riding['ffn2_ici']), lambda outs: self._gathered(FFN2_WEIGHTS, outs)
        steps = {"d_merge": ('ffn2', 0), "gla_bwd": ('ffn2', 1), "d_mix_rms": ('ffn2', 2),
                 "d_u_a": ('mixer', 0), "ffn1_bwd": ('mixer', 1), "ffn1_gw2": ('mixer', 2)}
        if tag in steps:
            group, stage = steps[tag]
            return self._reduce_stage(grp[group], stage)
        return None

    def _reduce_stage(self, names, stage):
        if stage == 0:
            for nm in names:
                self.g4s[nm] = self._shard_major(nm)

            def done(outs):
                for nm, r in zip(names, outs):
                    self.chip_sums[nm] = _chip_sum(self.g4s[nm], r, self.c_arr, f"chip_sum_{nm}")
            return _side_swap_halves([self.g4s[nm] for nm in names]), done
        if stage == 1:
            def done(outs):
                for nm, o in zip(names, outs):
                    self.halves[nm] = _owner_sum(self.chip_sums[nm], o, self.s_arr, f"owner_sum_{nm}")
            return _side_scatter([self.chip_sums[nm] for nm in names]), done

        def done(outs):
            self.sib_halves.update(zip(names, outs))
        return _side_swap_reduced([self.halves[nm] for nm in names]), done

    def before(self, tag):
        entry = self._schedule(tag)
        if entry is not None:
            side, done = entry
            self.riding[tag] = (side, done)
            _RIDER.append(side)

    def after(self, tag):
        if tag in self.riding:
            side, done = self.riding.pop(tag)
            assert not _RIDER and side.outs is not None, tag
            done(side.outs)

    def finish(self):
        names = GRAD_GROUPS['ffn1']
        for stage in range(3):
            side, done = self._reduce_stage(names, stage)
            done(_run_side(side, f"grad_ffn1_stage{stage}"))


def _train_step(a):
    x = a['x'][0]
    tgt = a['loss_target'][0]
    xi, yi, ci = lax.axis_index("x"), lax.axis_index("y"), lax.axis_index("c")
    c_arr = jnp.reshape(ci, (1,)).astype(jnp.int32)
    s_arr = jnp.reshape(2 * xi + yi, (1,)).astype(jnp.int32)
    plan = _Plan(a, c_arr, s_arr)
    loss, dx = _local_step(x, tgt, plan)
    plan.finish()
    grads = plan.grads
    loss = lax.psum(loss, ("x", "y", "c"))
    halves = [plan.halves[nm] for nm in SHARDED]
    sib_halves = [plan.sib_halves[nm] for nm in SHARDED]
    red = {}
    small_parts = [grads[nm].reshape(a[nm].shape) for nm in SMALL if nm != 'gla_a_up_w'] + [grads['gla_a_up_w']]
    small_sum = _unpack_small(_allreduce_small(_pack_small(small_parts), "allreduce_small"), small_parts)
    small_names = [nm for nm in SMALL if nm != 'gla_a_up_w']
    for nm, g in zip(small_names, small_sum[:-1]):
        red[nm] = g
    g_up = small_sum[-1]
    red['gla_a_up_w'] = lax.dynamic_slice(g_up, (0, (2 * xi + yi) * GLA_DK), (GLA_RANK, GLA_DK))
    out_g, out_d, out_m, out_v = {}, {}, {}, {}
    for nm, own, sib in zip(SHARDED, halves, sib_halves):
        loc = lambda pre: _local_shard(a[pre + nm], nm)
        res = _adamw_halves(loc(''), own, sib, loc('m_'), loc('v_'), c_arr, f"adamw_{nm}")
        back = (lambda t: jnp.swapaxes(t[None], 1, 2)) if nm in TRANSPOSED else (lambda t: t[None])
        out_g[nm], out_d[nm], out_m[nm], out_v[nm] = (back(t) for t in res)
    rest = [nm for nm in WEIGHTS if nm not in SHARDED]
    pk = lambda pre: _pack_small([a[pre + nm] for nm in rest])
    d, nm_, nv_ = _adamw(pk(''), _pack_small([red[nm] for nm in rest]), pk('m_'), pk('v_'), "adamw_small")
    like = [a[nm] for nm in rest]
    for nm, g, dd, mm_, vv_ in zip(rest, [red[nm].reshape(a[nm].shape) for nm in rest], _unpack_small(d, like),
                                   _unpack_small(nm_, like), _unpack_small(nv_, like)):
        out_g[nm], out_d[nm], out_m[nm], out_v[nm] = g, dd, mm_, vv_
    return (loss, dx[None], *[out_g[nm] for nm in WEIGHTS], *[out_d[nm] for nm in WEIGHTS],
            *[out_m[nm] for nm in WEIGHTS], *[out_v[nm] for nm in WEIGHTS])


def kernel(x, ffn1_norm, ffn1_w1, ffn1_w3, ffn1_w2, mix_norm, w_in, s5_lambda_re, s5_lambda_im, s5_log_dt, s5_b_re, s5_b_im, s5_c_re, s5_c_im, s5_d, s5_glu_w, s5_glu_b, gla_a_up_w, gla_a_up_b, gla_out_norm, proj_s5, proj_gla, w_out, ffn2_norm, ffn2_w1, ffn2_w3, ffn2_w2, final_norm, loss_target, m_ffn1_norm, m_ffn1_w1, m_ffn1_w3, m_ffn1_w2, m_mix_norm, m_w_in, m_s5_lambda_re, m_s5_lambda_im, m_s5_log_dt, m_s5_b_re, m_s5_b_im, m_s5_c_re, m_s5_c_im, m_s5_d, m_s5_glu_w, m_s5_glu_b, m_gla_a_up_w, m_gla_a_up_b, m_gla_out_norm, m_proj_s5, m_proj_gla, m_w_out, m_ffn2_norm, m_ffn2_w1, m_ffn2_w3, m_ffn2_w2, m_final_norm, v_ffn1_norm, v_ffn1_w1, v_ffn1_w3, v_ffn1_w2, v_mix_norm, v_w_in, v_s5_lambda_re, v_s5_lambda_im, v_s5_log_dt, v_s5_b_re, v_s5_b_im, v_s5_c_re, v_s5_c_im, v_s5_d, v_s5_glu_w, v_s5_glu_b, v_gla_a_up_w, v_gla_a_up_b, v_gla_out_norm, v_proj_s5, v_proj_gla, v_w_out, v_ffn2_norm, v_ffn2_w1, v_ffn2_w3, v_ffn2_w2, v_final_norm):
    return _train_step(dict(locals()))
```

```python
import functools

import jax
import jax.numpy as jnp
from jax import lax
from jax.experimental import pallas as pl
from jax.experimental.pallas import tpu as pltpu

F32 = jnp.float32
BF16 = jnp.bfloat16
HI = lax.Precision.HIGHEST
MESH_ID = pl.DeviceIdType.MESH

D_MODEL = 1024
EPS = 1e-6
S5_G, S5_P, S5_H = 32, 64, 16
S5_W = S5_G * S5_H
S5_GP = S5_G * S5_P
SEG = 8
SCAN_ROWS = 256
GLA_HEADS, GLA_DK, GLA_DV = 4, 64, 128
GLA_CHUNK = 64
GLA_TAU = 16.0
GLA_RANK = 16
ADAM_LR, ADAM_B1, ADAM_B2, ADAM_EPS, ADAM_WD, ADAM_STEP = 0.001, 0.9, 0.999, 1e-08, 0.01, 10
V7X_VMEM_LIMIT = 56 * 1024 * 1024
LANE = 128

WEIGHTS = ['ffn1_norm', 'ffn1_w1', 'ffn1_w3', 'ffn1_w2', 'mix_norm', 'w_in', 's5_lambda_re', 's5_lambda_im',
           's5_log_dt', 's5_b_re', 's5_b_im', 's5_c_re', 's5_c_im', 's5_d', 's5_glu_w', 's5_glu_b', 'gla_a_up_w',
           'gla_a_up_b', 'gla_out_norm', 'proj_s5', 'proj_gla', 'w_out', 'ffn2_norm', 'ffn2_w1', 'ffn2_w3',
           'ffn2_w2', 'final_norm']
SHARDED = ['ffn1_w1', 'ffn1_w3', 'ffn1_w2', 'w_in', 's5_glu_w', 'proj_s5', 'proj_gla', 'w_out',
           'ffn2_w1', 'ffn2_w3', 'ffn2_w2']
COL_SHARDED = ['ffn1_w1', 'ffn1_w3', 'w_in', 'proj_s5', 'proj_gla', 'ffn2_w1', 'ffn2_w3', 'gla_a_up_w']
SMALL = [n for n in WEIGHTS if n not in SHARDED]
FFN_WEIGHTS = ['ffn1_w1', 'ffn1_w3', 'ffn1_w2', 'ffn2_w1', 'ffn2_w3', 'ffn2_w2']


def _params(**kw):
    return pltpu.CompilerParams(vmem_limit_bytes=V7X_VMEM_LIMIT, **kw)


class _Side:
    def __init__(self, ins, out_shapes, nsem, copies, aliased=False):
        self.ins, self.out_shapes, self.nsem, self.copies, self.aliased = list(ins), list(out_shapes), nsem, copies, aliased
        self.outs = None


_RIDER = []


def _pcall(body, **kw):
    if _RIDER:
        return _carry(body, _RIDER.pop(), **kw)
    return pl.pallas_call(body, **kw)


def _carry(body, side, *, name, grid, in_specs, out_specs, out_shape, scratch_shapes=(), compiler_params=None):
    del compiler_params
    single = not isinstance(out_shape, (list, tuple))
    out_specs = [out_specs] if single else list(out_specs)
    out_shape = [out_shape] if single else list(out_shape)
    n_in, n_out, n_scr = len(in_specs), len(out_shape), len(scratch_shapes)
    s_in, s_out = len(side.ins), len(side.out_shapes)
    any_spec = pl.BlockSpec(memory_space=pl.ANY)

    def wrapped(*refs):
        cuts = [n_in, s_in, n_out, s_out, n_scr]
        parts, pos = [], 0
        for c in cuts:
            parts.append(refs[pos:pos + c])
            pos += c
        ins, sins, outs, souts, scr = parts
        ssem, rsem = refs[pos], refs[pos + 1]
        first = last = None
        for d, g in enumerate(grid):
            i = pl.program_id(d)
            first = (i == 0) if first is None else first & (i == 0)
            last = (i == g - 1) if last is None else last & (i == g - 1)

        @pl.when(first)
        def _():
            for cp in side.copies(sins, souts, ssem, rsem):
                cp.start()

        body(*ins, *outs, *scr)

        @pl.when(last)
        def _():
            for cp in side.copies(sins, souts, ssem, rsem):
                cp.wait()

    call = pl.pallas_call(
        wrapped, name=name, grid=grid, in_specs=list(in_specs) + [any_spec] * s_in,
        out_specs=out_specs + [any_spec] * s_out, out_shape=out_shape + side.out_shapes,
        scratch_shapes=list(scratch_shapes) + [pltpu.SemaphoreType.DMA((side.nsem,)), pltpu.SemaphoreType.DMA((side.nsem,))],
        input_output_aliases={n_in + j: n_out + j for j in range(s_in)} if side.aliased else {},
        compiler_params=_params(has_side_effects=True))

    def run(*args):
        res = call(*args, *side.ins)
        side.outs = list(res[n_out:])
        return res[0] if single else list(res[:n_out])

    return run


def _run_side(side, name):
    s_in, s_out = len(side.ins), len(side.out_shapes)
    any_spec = pl.BlockSpec(memory_space=pl.ANY)

    def body(*refs):
        sins, souts = refs[:s_in], refs[s_in:s_in + s_out]
        ssem, rsem = refs[s_in + s_out:]
        cps = side.copies(sins, souts, ssem, rsem)
        for cp in cps:
            cp.start()
        for cp in cps:
            cp.wait()

    side.outs = list(pl.pallas_call(
        body, name=name, in_specs=[any_spec] * s_in, out_specs=[any_spec] * s_out, out_shape=side.out_shapes,
        scratch_shapes=[pltpu.SemaphoreType.DMA((side.nsem,)), pltpu.SemaphoreType.DMA((side.nsem,))],
        input_output_aliases={j: j for j in range(s_in)} if side.aliased else {},
        compiler_params=pltpu.CompilerParams(has_side_effects=True))(*side.ins))
    return side.outs


def _pick(n, cap, quantum):
    if n <= cap:
        return n
    best = None
    for t in range(quantum, cap + 1, quantum):
        if n % t == 0:
            best = t
    assert best is not None, (n, cap, quantum)
    return best


def _sigmoid(x):
    return jax.nn.sigmoid(x)


def _mm(a, b, *, name, ta=False, tb=False, out_dtype=F32, alpha=1.0, res=None, bias=None, exact=False, shard=None):
    ns = 4
    (k_a, m) = a.shape[-2:] if ta else a.shape[-2:][::-1]
    (k_b, n) = b.shape[-2:][::-1] if tb else b.shape[-2:]
    assert k_a == k_b, (a.shape, b.shape, ta, tb)
    assert (a.ndim == 3) == (shard in ('k', 'm')) and (b.ndim == 3) == (shard in ('n', 'k'))
    k = k_a
    tm = _pick(m, 1024, 128)
    tn = _pick(n, 1024, 128)
    tk = _pick(k, 1024, 128)
    pm, pn, pk = m // tm, n // tn, k // tk
    gm = pm * (ns if shard == 'm' else 1)
    gn = pn * (ns if shard == 'n' else 1)
    gk = pk * (ns if shard == 'k' else 1)
    dims = (((0,) if ta else (1,), (1,) if tb else (0,)), ((), ()))
    op_dtype = F32 if exact else BF16

    def body(*refs):
        a_ref, b_ref = refs[0], refs[1]
        pos = 2
        res_ref = bias_ref = None
        if res is not None:
            res_ref = refs[pos]
            pos += 1
        if bias is not None:
            bias_ref = refs[pos]
            pos += 1
        o_ref, acc_ref = refs[pos], refs[pos + 1]
        kk = pl.program_id(2)

        @pl.when(kk == 0)
        def _():
            acc_ref[...] = jnp.zeros_like(acc_ref)

        acc_ref[...] += lax.dot_general(a_ref[...].astype(op_dtype), b_ref[...].astype(op_dtype), dims,
                                        precision=HI if exact else None, preferred_element_type=F32)

        @pl.when(kk == gk - 1)
        def _():
            o = acc_ref[...]
            if alpha != 1.0:
                o = o * alpha
            if bias_ref is not None:
                o = o + bias_ref[...]
            if res_ref is not None:
                o = o + res_ref[...]
            o_ref[...] = o.astype(out_dtype)

    def spec(block, sharded_on, order):
        per = {'m': pm, 'n': pn, 'k': pk}

        def index(i, j, kk):
            g = {'m': i, 'n': j, 'k': kk}
            r, c = order(i % pm if shard == 'm' else i, j % pn if shard == 'n' else j, kk % pk if shard == 'k' else kk)
            if sharded_on is None:
                return (r, c)
            return (g[sharded_on] // per[sharded_on], r, c)

        return pl.BlockSpec(block if sharded_on is None else (None,) + block, index)

    a_sh = shard if shard in ('k', 'm') else None
    b_sh = shard if shard in ('n', 'k') else None
    o_sh = shard if shard in ('n', 'm') else None
    a_spec = spec((tk, tm), a_sh, lambda i, j, kk: (kk, i)) if ta else spec((tm, tk), a_sh, lambda i, j, kk: (i, kk))
    b_spec = spec((tn, tk), b_sh, lambda i, j, kk: (j, kk)) if tb else spec((tk, tn), b_sh, lambda i, j, kk: (kk, j))
    ins, in_specs = [a, b], [a_spec, b_spec]
    if res is not None:
        assert o_sh is None
        ins.append(res)
        in_specs.append(pl.BlockSpec((tm, tn), lambda i, j, kk: (i, j)))
    if bias is not None:
        assert o_sh is None
        ins.append(bias)
        in_specs.append(pl.BlockSpec((1, tn), lambda i, j, kk: (0, j)))
    out_shape = (m, n) if o_sh is None else (ns, m, n)
    return _pcall(body, name=name, grid=(gm, gn, gk), in_specs=in_specs,
                  out_specs=spec((tm, tn), o_sh, lambda i, j, kk: (i, j)),
                  out_shape=jax.ShapeDtypeStruct(out_shape, out_dtype),
                  scratch_shapes=[pltpu.VMEM((tm, tn), F32)], compiler_params=_params())(*ins)


ROWS_VMEM_BUDGET = 24 * 1024 * 1024


def _rows(body, ins, outs, *, n, name):
    cols = sum(a.shape[1] for a, kind in ins if kind == 'r') + sum(c for c, _, kind in outs if kind == 'r')
    cap = 256
    while cap < 2048 and 2 * 4 * cols * (2 * cap) <= ROWS_VMEM_BUDGET:
        cap *= 2
    tm = _pick(n, cap, 16)
    in_specs = []
    for arr, kind in ins:
        if kind == 'r':
            in_specs.append(pl.BlockSpec((tm, arr.shape[1]), lambda i: (i, 0)))
        else:
            in_specs.append(pl.BlockSpec(arr.shape, lambda i: (0, 0)))
    out_specs, out_shape = [], []
    for cols, dtype, kind in outs:
        if kind == 'r':
            out_specs.append(pl.BlockSpec((tm, cols), lambda i: (i, 0)))
            out_shape.append(jax.ShapeDtypeStruct((n, cols), dtype))
        else:
            out_specs.append(pl.BlockSpec((1, cols), lambda i: (0, 0)))
            out_shape.append(jax.ShapeDtypeStruct((1, cols), dtype))
    n_in = len(ins)
    acc_ids = [j for j, o in enumerate(outs) if o[2] == 'a']

    def wrapped(*refs):
        if acc_ids:
            @pl.when(pl.program_id(0) == 0)
            def _():
                for j in acc_ids:
                    refs[n_in + j][...] = jnp.zeros_like(refs[n_in + j])
        body(*refs)

    res = _pcall(wrapped, name=name, grid=(n // tm,), in_specs=in_specs, out_specs=out_specs, out_shape=out_shape,
                 compiler_params=_params())(*[a for a, _ in ins])
    return res


def _rms_fwd(x, g, name):
    def body(x_ref, g_ref, o_ref):
        xv = x_ref[...]
        rstd = lax.rsqrt(jnp.mean(xv * xv, axis=-1, keepdims=True) + EPS)
        o_ref[...] = (xv * rstd * g_ref[...]).astype(BF16)
    return _rows(body, [(x, 'r'), (g, 'f')], [(x.shape[1], BF16, 'r')], n=x.shape[0], name=name)[0]


def _rms_bwd(x, g, dn, dres, name):
    def body(x_ref, g_ref, dn_ref, dres_ref, dx_ref, dg_ref):
        xv = x_ref[...]
        rstd = lax.rsqrt(jnp.mean(xv * xv, axis=-1, keepdims=True) + EPS)
        xh = xv * rstd
        dn = dn_ref[...]
        dg_ref[...] += jnp.sum(dn * xh, axis=0, keepdims=True)
        dxh = dn * g_ref[...]
        dx_ref[...] = dres_ref[...] + rstd * (dxh - xh * jnp.mean(dxh * xh, axis=-1, keepdims=True))
    d = x.shape[1]
    return _rows(body, [(x, 'r'), (g, 'f'), (dn, 'r'), (dres, 'r')], [(d, F32, 'r'), (d, F32, 'a')],
                 n=x.shape[0], name=name)


def _gelu_parts(y):
    c0 = 0.7978845608028654
    inner = c0 * (y + 0.044715 * y * y * y)
    th = jnp.tanh(inner)
    return th, c0 * (1.0 + 3.0 * 0.044715 * y * y)


def _gelu_fwd(y, name):
    def body(y_ref, o_ref):
        yv = y_ref[...]
        th, _ = _gelu_parts(yv)
        o_ref[...] = 0.5 * yv * (1.0 + th)
    return _rows(body, [(y, 'r')], [(y.shape[1], F32, 'r')], n=y.shape[0], name=name)[0]


def _glu_fwd(zg, t, name):
    def body(z_ref, t_ref, o_ref):
        o_ref[...] = (z_ref[...] * _sigmoid(t_ref[...])).astype(BF16)
    return _rows(body, [(zg, 'r'), (t, 'r')], [(zg.shape[1], BF16, 'r')], n=zg.shape[0], name=name)[0]


def _glu_bwd1(dy, zg, t, name):
    def body(dy_ref, z_ref, t_ref, dz_ref, dt_ref, db_ref):
        dyv, zv = dy_ref[...], z_ref[...]
        sg = _sigmoid(t_ref[...])
        dz_ref[...] = dyv * sg
        dt = dyv * zv * sg * (1.0 - sg)
        dt_ref[...] = dt.astype(BF16)
        db_ref[...] += jnp.sum(dt, axis=0, keepdims=True)
    w = zg.shape[1]
    return _rows(body, [(dy, 'r'), (zg, 'r'), (t, 'r')], [(w, F32, 'r'), (w, BF16, 'r'), (w, F32, 'a')],
                 n=zg.shape[0], name=name)


def _glu_bwd2(dzg, ys, u, dskip, name):
    def body(dz_ref, y_ref, u_ref, d_ref, dy_ref, du_ref, dd_ref):
        yv = y_ref[...]
        th, dinner = _gelu_parts(yv)
        dy = dz_ref[...] * (0.5 * (1.0 + th) + 0.5 * yv * (1.0 - th * th) * dinner)
        dy_ref[...] = dy
        du_ref[...] = dy * d_ref[...]
        dd_ref[...] += jnp.sum(dy * u_ref[...], axis=0, keepdims=True)
    w = ys.shape[1]
    return _rows(body, [(dzg, 'r'), (ys, 'r'), (u, 'r'), (dskip, 'f')], [(w, F32, 'r'), (w, F32, 'r'), (w, F32, 'a')],
                 n=ys.shape[0], name=name)


def _scale_rows(u, dskip, name):
    def body(u_ref, d_ref, o_ref):
        o_ref[...] = u_ref[...] * d_ref[...]
    return _rows(body, [(u, 'r'), (dskip, 'f')], [(u.shape[1], F32, 'r')], n=u.shape[0], name=name)[0]


def _merge_fwd(zg, ps, pg, name):
    def body(z_ref, ps_ref, pg_ref, o_ref):
        zv = z_ref[...]
        o_ref[...] = (_sigmoid(zv[:, :D_MODEL]) * ps_ref[...] + _sigmoid(zv[:, D_MODEL:]) * pg_ref[...]).astype(BF16)
    return _rows(body, [(zg, 'r'), (ps, 'r'), (pg, 'r')], [(D_MODEL, BF16, 'r')], n=zg.shape[0], name=name)[0]


def _merge_bwd(dm, zg, ps, pg, name):
    def body(dm_ref, z_ref, ps_ref, pg_ref, dps_ref, dpg_ref, dz_ref):
        dmv, zv = dm_ref[...], z_ref[...]
        s1, s2 = _sigmoid(zv[:, :D_MODEL]), _sigmoid(zv[:, D_MODEL:])
        dps_ref[...] = (dmv * s1).astype(BF16)
        dpg_ref[...] = (dmv * s2).astype(BF16)
        dz_ref[:, :D_MODEL] = dmv * ps_ref[...] * s1 * (1.0 - s1)
        dz_ref[:, D_MODEL:] = dmv * pg_ref[...] * s2 * (1.0 - s2)
    return _rows(body, [(dm, 'r'), (zg, 'r'), (ps, 'r'), (pg, 'r')],
                 [(D_MODEL, BF16, 'r'), (D_MODEL, BF16, 'r'), (2 * D_MODEL, F32, 'r')], n=zg.shape[0], name=name)


def _final_loss(h, g, tgt, name):
    def body(h_ref, g_ref, t_ref, loss_ref, dh_ref, dg_ref):
        hv = h_ref[...]
        rstd = lax.rsqrt(jnp.mean(hv * hv, axis=-1, keepdims=True) + EPS)
        xh = hv * rstd
        err = xh * g_ref[...] - t_ref[...]
        part = 0.5 * jnp.sum(jnp.mean(err * err, axis=-1, keepdims=True), axis=0, keepdims=True)
        loss_ref[...] += jnp.broadcast_to(part, loss_ref.shape)
        dout = err * (1.0 / hv.shape[1])
        dg_ref[...] += jnp.sum(dout * xh, axis=0, keepdims=True)
        dxh = dout * g_ref[...]
        dh_ref[...] = rstd * (dxh - xh * jnp.mean(dxh * xh, axis=-1, keepdims=True))
    d = h.shape[1]
    return _rows(body, [(h, 'r'), (g, 'f'), (tgt, 'r')], [(LANE, F32, 'a'), (d, F32, 'r'), (d, F32, 'a')],
                 n=h.shape[0], name=name)


def _adamw_math(wv, gv, mv, vv):
    nm = ADAM_B1 * mv + (1.0 - ADAM_B1) * gv
    nv = ADAM_B2 * vv + (1.0 - ADAM_B2) * (gv * gv)
    m_hat = nm / (1.0 - ADAM_B1 ** ADAM_STEP)
    v_hat = nv / (1.0 - ADAM_B2 ** ADAM_STEP)
    return -ADAM_LR * (m_hat / (jnp.sqrt(v_hat) + ADAM_EPS) + ADAM_WD * wv), nm, nv


def _adamw(w, g, m, v, name):
    def body(w_ref, g_ref, m_ref, v_ref, d_ref, nm_ref, nv_ref):
        d_ref[...], nm_ref[...], nv_ref[...] = _adamw_math(w_ref[...], g_ref[...], m_ref[...], v_ref[...])
    c = w.shape[1]
    return _rows(body, [(w, 'r'), (g, 'r'), (m, 'r'), (v, 'r')], [(c, F32, 'r')] * 3, n=w.shape[0], name=name)


def _adamw_halves(w, g_own, g_sib, m, v, c_arr, name):
    r, cols = w.shape
    h = r // 2
    tr = _pick(h, 512, 8)
    per = h // tr

    def body(c_ref, w_ref, go_ref, gs_ref, m_ref, v_ref, g_ref, d_ref, nm_ref, nv_ref):
        mine = (pl.program_id(0) // per) == c_ref[0]
        gv = jnp.where(mine, go_ref[...], gs_ref[...])
        g_ref[...] = gv
        d_ref[...], nm_ref[...], nv_ref[...] = _adamw_math(w_ref[...], gv, m_ref[...], v_ref[...])

    full = pl.BlockSpec((tr, cols), lambda i, c_ref: (i, 0))
    half = pl.BlockSpec((tr, cols), lambda i, c_ref: (i % per, 0))
    grid_spec = pltpu.PrefetchScalarGridSpec(num_scalar_prefetch=1, grid=(2 * per,),
                                             in_specs=[full, half, half, full, full], out_specs=[full] * 4)
    return _pcall(body, name=name, grid_spec=grid_spec, out_shape=[jax.ShapeDtypeStruct((r, cols), F32)] * 4,
                  compiler_params=_params())(c_arr, w, g_own, g_sib, m, v)


def _shift_rows(v, sh, down):
    rolled = pltpu.roll(v, sh if down else v.shape[0] - sh, axis=0)
    row = lax.broadcasted_iota(jnp.int32, v.shape, 0)
    keep = (row >= sh) if down else (row < v.shape[0] - sh)
    return jnp.where(keep, rolled, 0.0)


def _chain_segments(st_r, st_i, pw_r_ref, pw_i_ref, conj, down):
    vr, vi = st_r[...], st_i[...]
    sh, k = 1, 0
    while sh < SEG:
        pr, pi = pw_r_ref[k:k + 1, :], pw_i_ref[k:k + 1, :]
        if conj:
            pi = -pi
        sr, si = _shift_rows(vr, sh, down), _shift_rows(vi, sh, down)
        vr, vi = vr + pr * sr - pi * si, vi + pr * si + pi * sr
        sh, k = sh * 2, k + 1
    st_r[...] = _shift_rows(vr, 1, down)
    st_i[...] = _shift_rows(vi, 1, down)


def _expand_block(u_ref, t_ref, bu_ref):
    for j in range(BD_TILES):
        k = j % 4
        bu_ref[:, j * BD_ST:(j + 1) * BD_ST] = _dot(u_ref[:, k * BD_CH:(k + 1) * BD_CH], t_ref[j])


def _s5_scan(u, tiles, ar8, ai8, pw_r, pw_i, name):
    n = u.shape[0]
    rb = SCAN_ROWS
    nb, steps, lc = n // rb, rb // SEG, 512

    def body(u_ref, t_ref, ar_ref, ai_ref, pwr_ref, pwi_ref, x_ref, st_r, st_i, bu_ref):
        ph, b = pl.program_id(0), pl.program_id(1)

        @pl.when((ph == 0) & (b == 0))
        def _():
            st_r[...] = jnp.zeros_like(st_r)
            st_i[...] = jnp.zeros_like(st_i)

        _expand_block(u_ref, t_ref, bu_ref)

        def scan(store):
            for c in range(S5_GP // lc):
                re, im = slice(c * lc, (c + 1) * lc), slice(S5_GP + c * lc, S5_GP + (c + 1) * lc)
                a_r, a_i = ar_ref[:, re], ai_ref[:, re]

                def step(s, carry):
                    xr, xi = carry
                    rows = pl.ds(pl.multiple_of(s * SEG, SEG), SEG)
                    nr = a_r * xr - a_i * xi + bu_ref[rows, re]
                    ni = a_r * xi + a_i * xr + bu_ref[rows, im]
                    if store:
                        x_ref[rows, re] = nr
                        x_ref[rows, im] = ni
                    return nr, ni

                xr, xi = lax.fori_loop(0, steps, step, (st_r[:, re], st_i[:, re]), unroll=4)
                st_r[:, re] = xr
                st_i[:, re] = xi

        @pl.when(ph == 0)
        def _():
            scan(False)

        @pl.when((ph == 0) & (b == nb - 1))
        def _():
            _chain_segments(st_r, st_i, pwr_ref, pwi_ref, conj=False, down=True)

        @pl.when(ph == 1)
        def _():
            scan(True)

    full = lambda a: pl.BlockSpec(a.shape, lambda ph, b: (0, 0))
    return _pcall(body, name=name, grid=(2, nb),
                  in_specs=[pl.BlockSpec((rb, S5_W), lambda ph, b: (b, 0)), pl.BlockSpec(tiles.shape, lambda ph, b: (0, 0, 0)),
                            full(ar8), full(ai8), full(pw_r), full(pw_i)],
                  out_specs=pl.BlockSpec((rb, 2 * S5_GP), lambda ph, b: (b * ph, 0)),
                  out_shape=jax.ShapeDtypeStruct((n, 2 * S5_GP), F32),
                  scratch_shapes=[pltpu.VMEM((SEG, S5_GP), F32), pltpu.VMEM((SEG, S5_GP), F32),
                                  pltpu.VMEM((rb, 2 * S5_GP), F32)],
                  compiler_params=_params())(u, tiles, ar8, ai8, pw_r, pw_i)


def _s5_scan_bwd(dy, tiles, xs, ar8, ai8, pw_r, pw_i, name):
    n = dy.shape[0]
    rb = SCAN_ROWS
    nb, steps, lc = n // rb, rb // SEG, 256

    def body(dy_ref, t_ref, x_ref, ar_ref, ai_ref, pwr_ref, pwi_ref, lam_ref, da_ref, st_r, st_i, gx_ref):
        ph, b = pl.program_id(0), pl.program_id(1)

        @pl.when((ph == 0) & (b == 0))
        def _():
            st_r[...] = jnp.zeros_like(st_r)
            st_i[...] = jnp.zeros_like(st_i)
            da_ref[...] = jnp.zeros_like(da_ref)

        _expand_block(dy_ref, t_ref, gx_ref)

        def scan(store):
            for c in range(S5_GP // lc):
                re, im = slice(c * lc, (c + 1) * lc), slice(S5_GP + c * lc, S5_GP + (c + 1) * lc)
                a_r, a_i = ar_ref[:, re], ai_ref[:, re]

                def step(s, carry):
                    rows = pl.ds(pl.multiple_of((steps - 1 - s) * SEG, SEG), SEG)
                    if store:
                        lr, li, dr, di = carry
                        xr, xi = x_ref[rows, re], x_ref[rows, im]
                        dr = dr + lr * xr + li * xi
                        di = di + li * xr - lr * xi
                    else:
                        lr, li = carry
                    nr = a_r * lr + a_i * li + gx_ref[rows, re]
                    ni = a_r * li - a_i * lr + gx_ref[rows, im]
                    if store:
                        lam_ref[rows, re] = nr
                        lam_ref[rows, im] = ni
                        return nr, ni, dr, di
                    return nr, ni

                if store:
                    lr, li, dr, di = lax.fori_loop(0, steps, step, (st_r[:, re], st_i[:, re], da_ref[:, re], da_ref[:, im]),
                                                   unroll=4)
                    da_ref[:, re] = dr
                    da_ref[:, im] = di
                else:
                    lr, li = lax.fori_loop(0, steps, step, (st_r[:, re], st_i[:, re]), unroll=4)
                st_r[:, re] = lr
                st_i[:, re] = li

        @pl.when(ph == 0)
        def _():
            scan(False)

        @pl.when((ph == 0) & (b == nb - 1))
        def _():
            _chain_segments(st_r, st_i, pwr_ref, pwi_ref, conj=True, down=False)

        @pl.when(ph == 1)
        def _():
            scan(True)

    full = lambda a: pl.BlockSpec(a.shape, lambda ph, b: (0, 0))
    rev = lambda ph, b: (nb - 1 - b, 0)
    return _pcall(body, name=name, grid=(2, nb),
                  in_specs=[pl.BlockSpec((rb, S5_W), rev), pl.BlockSpec(tiles.shape, lambda ph, b: (0, 0, 0)),
                            pl.BlockSpec((rb, 2 * S5_GP), lambda ph, b: ((nb - 1 - b) * ph, 0)),
                            full(ar8), full(ai8), full(pw_r), full(pw_i)],
                  out_specs=[pl.BlockSpec((rb, 2 * S5_GP), lambda ph, b: (nb - 1 - b * ph, 0)),
                             pl.BlockSpec((SEG, 2 * S5_GP), lambda ph, b: (0, 0))],
                  out_shape=[jax.ShapeDtypeStruct((n, 2 * S5_GP), F32), jax.ShapeDtypeStruct((SEG, 2 * S5_GP), F32)],
                  scratch_shapes=[pltpu.VMEM((SEG, S5_GP), F32), pltpu.VMEM((SEG, S5_GP), F32),
                                  pltpu.VMEM((rb, 2 * S5_GP), F32)],
                  compiler_params=_params())(dy, tiles, xs, ar8, ai8, pw_r, pw_i)


def _s5_discretize(lam_re, lam_im, log_dt, b_re, b_im):
    dt = jnp.exp(log_dt)[:, None]
    mag = jnp.exp(lam_re * dt)
    ar = mag * jnp.cos(lam_im * dt)
    ai = mag * jnp.sin(lam_im * dt)
    den = lam_re * lam_re + lam_im * lam_im
    nr = ar - 1.0
    fr = (nr * lam_re + ai * lam_im) / den
    fi = (ai * lam_re - nr * lam_im) / den
    bbar_re = fr[:, :, None] * b_re - fi[:, :, None] * b_im
    bbar_im = fr[:, :, None] * b_im + fi[:, :, None] * b_re
    return ar, ai, bbar_re, bbar_im


BD_TILES, BD_CH, BD_ST, BD_GROUPS = 8, 128, 512, 8
BD_ROWS = 4096


def _bd_tiles(re, im):
    eye = jnp.eye(BD_GROUPS, dtype=re.dtype)

    def tiles(t):
        t = t.reshape(S5_G // BD_GROUPS, BD_GROUPS, S5_H, S5_P)
        return (t[:, :, :, None, :] * eye[None, :, None, :, None]).reshape(S5_G // BD_GROUPS, BD_CH, BD_ST)

    return jnp.concatenate([tiles(re), tiles(im)], axis=0)


def _bd_blocks(t):
    t = t.reshape(2, S5_G // BD_GROUPS, BD_GROUPS, S5_H, BD_GROUPS, S5_P)
    return jnp.einsum('rkahap->rkahp', t).reshape(2, S5_G, S5_H, S5_P)


def _bd_reduce(x, t, res, name):
    n = x.shape[0]
    tm = _pick(n, BD_ROWS, 16)

    def body(x_ref, t_ref, r_ref, o_ref):
        part = _dot(x_ref[...], t_ref[...], NT)

        @pl.when(pl.program_id(2) == 0)
        def _():
            o_ref[...] = r_ref[...] + part

        @pl.when(pl.program_id(2) == 1)
        def _():
            o_ref[...] += part

    return _pcall(body, name=name, grid=(n // tm, 4, 2),
                  in_specs=[pl.BlockSpec((tm, BD_ST), lambda i, k, r: (i, k + 4 * r)),
                            pl.BlockSpec((None, BD_CH, BD_ST), lambda i, k, r: (k + 4 * r, 0, 0)),
                            pl.BlockSpec((tm, BD_CH), lambda i, k, r: (i, k))],
                  out_specs=pl.BlockSpec((tm, BD_CH), lambda i, k, r: (i, k)),
                  out_shape=jax.ShapeDtypeStruct((n, S5_W), F32), compiler_params=_params())(x, t, res)


def _bd_outer(a, x, name):
    n = a.shape[0]
    tk = _pick(n, BD_ROWS, 16)
    nk = n // tk

    def body(a_ref, x_ref, o_ref):
        part = _dot(a_ref[...], x_ref[...], TN)

        @pl.when(pl.program_id(1) == 0)
        def _():
            o_ref[...] = part

        @pl.when(pl.program_id(1) > 0)
        def _():
            o_ref[...] += part

    return _pcall(body, name=name, grid=(BD_TILES, nk),
                  in_specs=[pl.BlockSpec((tk, BD_CH), lambda j, kk: (kk, j % 4)), pl.BlockSpec((tk, BD_ST), lambda j, kk: (kk, j))],
                  out_specs=pl.BlockSpec((None, BD_CH, BD_ST), lambda j, kk: (j, 0, 0)),
                  out_shape=jax.ShapeDtypeStruct((BD_TILES, BD_CH, BD_ST), F32), compiler_params=_params())(a, x)


def _permute_rows(t):
    n = t.shape[0]
    return t.reshape(SEG, n // SEG, t.shape[1]).transpose(1, 0, 2).reshape(n, t.shape[1])


def _unpermute_rows(t):
    n = t.shape[0]
    return t.reshape(n // SEG, SEG, t.shape[1]).transpose(1, 0, 2).reshape(n, t.shape[1])


def _segment_powers(ar, ai, seg_steps):
    pr, pi = ar.reshape(1, S5_GP), ai.reshape(1, S5_GP)
    e = 1
    while e < seg_steps:
        pr, pi = pr * pr - pi * pi, 2.0 * pr * pi
        e *= 2
    assert e == seg_steps, "segment length must be a power of two"
    rows_r, rows_i = [], []
    for _ in range(3):
        rows_r.append(pr)
        rows_i.append(pi)
        pr, pi = pr * pr - pi * pi, 2.0 * pr * pi
    pad = jnp.zeros((SEG - 3, S5_GP), F32)
    return jnp.concatenate(rows_r + [pad], axis=0), jnp.concatenate(rows_i + [pad], axis=0)


NT = (((1,), (1,)), ((), ()))
TN = (((0,), (0,)), ((), ()))


def _dot(a, b, dims=None, exact=False):
    dims = (((1,), (0,)), ((), ())) if dims is None else dims
    if exact:
        return lax.dot_general(a, b, dims, precision=HI, preferred_element_type=F32)
    return lax.dot_general(a.astype(BF16), b.astype(BF16), dims, preferred_element_type=F32)


def _dot01(a, b, dims=None, ones_first=True):
    x = b if ones_first else a
    hi = x.astype(BF16)
    r1 = x - hi.astype(F32)
    mid = r1.astype(BF16)
    lo = (r1 - mid.astype(F32)).astype(BF16)
    parts = [(_dot(a, p, dims) if ones_first else _dot(p, b, dims)) for p in (lo, mid, hi)]
    return (parts[0] + parts[1]) + parts[2]


HEADS = range(4)


def _gla_chunk_fwd(qc, kc, vc, al, wup, bup, s_prev, tril):
    ones = jnp.ones((GLA_CHUNK, GLA_DV), F32)
    z = [_dot(al, wup[h]) + bup[h] for h in HEADS]
    la = [(jnp.minimum(z[h], 0.0) - jnp.log(1.0 + jnp.exp(-jnp.abs(z[h])))) * (1.0 / GLA_TAU) for h in HEADS]
    bc = [_dot01(tril, la[h]) for h in HEADS]
    blb = [_dot01(la[h], ones, TN, ones_first=False) for h in HEADS]
    bl = [bc[h][GLA_CHUNK - 1:GLA_CHUNK, :] for h in HEADS]
    ebc = [jnp.exp(bc[h]) for h in HEADS]
    qt = [qc[h] * (GLA_DK ** -0.5) * ebc[h] for h in HEADS]
    kt = [kc[h] * jnp.exp(-bc[h]) for h in HEADS]
    ke = [kc[h] * jnp.exp(bl[h] - bc[h]) for h in HEADS]
    sc = [_dot(qt[h], kt[h], NT) * tril for h in HEADS]
    oi = [_dot(sc[h], vc[h]) for h in HEADS]
    oo = [_dot(qt[h], s_prev[h]) for h in HEADS]
    o = [oi[h] + oo[h] for h in HEADS]
    return z, bc, bl, blb, ebc, qt, kt, ke, sc, o


GLA_ROWS = 512
GLA_CPB = GLA_ROWS // GLA_CHUNK


ZA_COLS = 5 * 512
SLOT = 128


def _pad_heads(w):
    r = w.shape[0]
    return jnp.pad(w.reshape(r, GLA_HEADS, GLA_DK), ((0, 0), (0, 0), (0, SLOT - GLA_DK))).reshape(r, GLA_HEADS * SLOT)


def _unpad_heads(w):
    r = w.shape[0]
    return w.reshape(r, GLA_HEADS, SLOT)[:, :, :GLA_DK].reshape(r, GLA_HEADS * GLA_DK)


def _gla_token_specs(blk):
    col = lambda cb: pl.BlockSpec((GLA_ROWS, 512), lambda j: (blk(j), cb))
    whole = lambda a: pl.BlockSpec(a.shape, lambda j: (0,) * a.ndim)
    return col, whole


def _head_ds(h, width):
    return pl.ds(h * SLOT, width)


def _tri(lower):
    ri = lax.broadcasted_iota(jnp.int32, (GLA_CHUNK, GLA_CHUNK), 0)
    ci = lax.broadcasted_iota(jnp.int32, (GLA_CHUNK, GLA_CHUNK), 1)
    return ((ri >= ci) if lower else (ri <= ci)).astype(F32)


def _gla_fwd(za, al, wup, bup, gn, name):
    n = za.shape[0]
    nc = n // GLA_CHUNK

    def body(q_ref, k_ref, v_ref, r_ref, al_ref, wup_ref, bup_ref, gn_ref, y_ref, sp_ref, s_ref):
        @pl.when(pl.program_id(0) == 0)
        def _():
            s_ref[...] = jnp.zeros_like(s_ref)

        tril = _tri(True)

        def chunk(c, carry):
            rows = pl.ds(pl.multiple_of(c * GLA_CHUNK, GLA_CHUNK), GLA_CHUNK)
            alc = al_ref[rows, :]
            vc = [v_ref[rows, _head_ds(h, GLA_DV)] for h in HEADS]
            s_prev = [s_ref[h] for h in HEADS]
            _, _, _, blb, _, _, _, ke, _, o = _gla_chunk_fwd(
                [q_ref[rows, _head_ds(h, GLA_DK)] for h in HEADS], [k_ref[rows, _head_ds(h, GLA_DK)] for h in HEADS],
                vc, alc, [wup_ref[h] for h in HEADS], [bup_ref[h] for h in HEADS], s_prev, tril)
            ds = [_dot(ke[h], vc[h], TN) for h in HEADS]
            for h in HEADS:
                rc = r_ref[rows, _head_ds(h, GLA_DV)]
                sp_ref[h, c] = s_prev[h]
                rstd = lax.rsqrt(jnp.mean(o[h] * o[h], axis=-1, keepdims=True) + EPS)
                y_ref[rows, _head_ds(h, GLA_DV)] = (o[h] * rstd * gn_ref[h] * (rc * _sigmoid(rc))).astype(BF16)
                s_ref[h] = jnp.exp(blb[h]) * s_prev[h] + ds[h]
            return carry

        lax.fori_loop(0, GLA_CPB, chunk, 0)

    col, whole = _gla_token_specs(lambda j: j)
    return _pcall(body, name=name, grid=(n // GLA_ROWS,),
                  in_specs=[col(1), col(2), col(3), col(4), pl.BlockSpec((GLA_ROWS, LANE), lambda j: (j, 0)),
                            whole(wup), whole(bup), whole(gn)],
                  out_specs=[pl.BlockSpec((GLA_ROWS, GLA_HEADS * GLA_DV), lambda j: (j, 0)),
                             pl.BlockSpec((GLA_HEADS, GLA_CPB, GLA_DK, GLA_DV), lambda j: (0, j, 0, 0))],
                  out_shape=[jax.ShapeDtypeStruct((n, GLA_HEADS * GLA_DV), BF16),
                             jax.ShapeDtypeStruct((GLA_HEADS, nc, GLA_DK, GLA_DV), F32)],
                  scratch_shapes=[pltpu.VMEM((GLA_HEADS, GLA_DK, GLA_DV), F32)],
                  compiler_params=_params())(za, za, za, za, al, wup, bup, gn)


def _gla_bwd(za, al, wup, bup, gn, sp, dy, du_s5, name):
    n = za.shape[0]
    nb = n // GLA_ROWS

    def body(q_ref, k_ref, v_ref, r_ref, al_ref, wup_ref, bup_ref, gn_ref, dy_ref, dus_ref, sp_ref,
             dza_ref, dz_ref, dgn_ref, dbup_ref, ds_ref):
        @pl.when(pl.program_id(0) == 0)
        def _():
            ds_ref[...] = jnp.zeros_like(ds_ref)
            dgn_ref[...] = jnp.zeros_like(dgn_ref)
            dbup_ref[...] = jnp.zeros_like(dbup_ref)

        tril, triu = _tri(True), _tri(False)
        dza_ref[:, 0:512] = dus_ref[...]
        dza_ref[:, 512:1536] = jnp.zeros((GLA_ROWS, 1024), F32)
        dz_ref[...] = jnp.zeros_like(dz_ref)

        def chunk(i, carry):
            c = GLA_CPB - 1 - i
            rows = pl.ds(pl.multiple_of(c * GLA_CHUNK, GLA_CHUNK), GLA_CHUNK)
            alc = al_ref[rows, :]
            qc = [q_ref[rows, _head_ds(h, GLA_DK)] for h in HEADS]
            kc = [k_ref[rows, _head_ds(h, GLA_DK)] for h in HEADS]
            vc = [v_ref[rows, _head_ds(h, GLA_DV)] for h in HEADS]
            s_prev = [sp_ref[h, c] for h in HEADS]
            ds = [ds_ref[h] for h in HEADS]
            z, bc, bl, blb, ebc, qt, kt, ke, sc, o = _gla_chunk_fwd(
                qc, kc, vc, alc, [wup_ref[h] for h in HEADS], [bup_ref[h] for h in HEADS], s_prev, tril)
            do = []
            for h in HEADS:
                rc = r_ref[rows, _head_ds(h, GLA_DV)]
                rs = lax.rsqrt(jnp.mean(o[h] * o[h], axis=-1, keepdims=True) + EPS)
                on = o[h] * rs
                sr = _sigmoid(rc)
                sil = rc * sr
                dyv, gnv = dy_ref[rows, _head_ds(h, GLA_DV)], gn_ref[h]
                dgn_ref[h] += jnp.sum(dyv * on * sil, axis=0, keepdims=True)
                dza_ref[rows, pl.ds(2048 + h * SLOT, GLA_DV)] = dyv * on * gnv * (sr * (1.0 + rc * (1.0 - sr)))
                don = dyv * gnv * sil
                do.append(rs * (don - on * jnp.mean(don * on, axis=-1, keepdims=True)))
            dp = [_dot(do[h], vc[h], NT) * tril for h in HEADS]
            dv1 = [_dot(sc[h], do[h], TN) for h in HEADS]
            dv2 = [_dot(ke[h], ds[h]) for h in HEADS]
            dq2 = [_dot(do[h], s_prev[h], NT) for h in HEADS]
            dke = [_dot(vc[h], ds[h], NT) for h in HEADS]
            ddec = [_dot01(jnp.ones((8, GLA_DV), F32), ds[h] * s_prev[h], NT)[0:1, :] for h in HEADS]
            dsn = [_dot(qt[h], do[h], TN) for h in HEADS]
            dq1 = [_dot(dp[h], kt[h]) for h in HEADS]
            dkt = [_dot(dp[h], qt[h], TN) for h in HEADS]
            dbc, dbl = [], []
            for h in HEADS:
                dqt = dq1[h] + dq2[h]
                dza_ref[rows, pl.ds(1536 + h * SLOT, GLA_DV)] = dv1[h] + dv2[h]
                ds_ref[h] = jnp.exp(blb[h]) * ds[h] + dsn[h]
                dza_ref[rows, pl.ds(512 + h * SLOT, GLA_DK)] = dqt * (GLA_DK ** -0.5) * ebc[h]
                dza_ref[rows, pl.ds(1024 + h * SLOT, GLA_DK)] = dkt[h] * jnp.exp(-bc[h]) + dke[h] * jnp.exp(bl[h] - bc[h])
                dbc.append(dqt * qt[h] - dkt[h] * kt[h] - dke[h] * ke[h])
                dbl.append(jnp.sum(dke[h] * ke[h], axis=0, keepdims=True) + ddec[h] * jnp.exp(bl[h]))
            dla = [_dot01(triu, dbc[h]) + dbl[h] for h in HEADS]
            for h in HEADS:
                dz = dla[h] * (1.0 - _sigmoid(z[h])) * (1.0 / GLA_TAU)
                dz_ref[rows, _head_ds(h, GLA_DK)] = dz
                dbup_ref[h] += jnp.sum(dz, axis=0, keepdims=True)
            return carry

        lax.fori_loop(0, GLA_CPB, chunk, 0)

    rev = lambda j: nb - 1 - j
    col, whole = _gla_token_specs(rev)
    tok = lambda w: pl.BlockSpec((GLA_ROWS, w), lambda j: (rev(j), 0))
    h1 = lambda w: pl.BlockSpec((GLA_HEADS, 1, w), lambda j: (0, 0, 0))
    s1 = lambda w: jax.ShapeDtypeStruct((GLA_HEADS, 1, w), F32)
    return _pcall(body, name=name, grid=(nb,),
                  in_specs=[col(1), col(2), col(3), col(4), tok(LANE), whole(wup), whole(bup), whole(gn), tok(512), tok(512),
                            pl.BlockSpec((GLA_HEADS, GLA_CPB, GLA_DK, GLA_DV), lambda j: (0, rev(j), 0, 0))],
                  out_specs=[tok(ZA_COLS), tok(GLA_HEADS * SLOT), h1(GLA_DV), h1(GLA_DK)],
                  out_shape=[jax.ShapeDtypeStruct((n, ZA_COLS), F32), jax.ShapeDtypeStruct((n, GLA_HEADS * SLOT), F32),
                             s1(GLA_DV), s1(GLA_DK)],
                  scratch_shapes=[pltpu.VMEM((GLA_HEADS, GLA_DK, GLA_DV), F32)],
                  compiler_params=_params())(za, za, za, za, al, wup, bup, gn, dy, du_s5, sp)


ANY = pl.BlockSpec(memory_space=pl.ANY)


def _place():
    x, y, c = lax.axis_index("x"), lax.axis_index("y"), lax.axis_index("c")
    chips = [(1 - x, y), (x, 1 - y), (1 - x, 1 - y)]
    return x, y, c, chips


def _remote(src, dst, ssem, rsem, dev):
    return pltpu.make_async_remote_copy(src_ref=src, dst_ref=dst, send_sem=ssem, recv_sem=rsem, device_id=dev,
                                        device_id_type=MESH_ID)


def _half(c, rows):
    h = rows // 2
    return pl.ds(pl.multiple_of(c * h, 8), h)


def _side_gather_ici(shards):
    def copies(ins, outs, ssem, rsem):
        x, y, c, chips = _place()
        mine = 2 * x + y
        cps = []
        for w in range(len(ins)):
            half = _half(c, ins[w].shape[0])
            cps.append(_remote(ins[w], outs[w].at[mine], ssem.at[4 * w], rsem.at[4 * w], (x, y, 1 - c)))
            for k, (px, py) in enumerate(chips):
                cps.append(_remote(ins[w].at[half], outs[w].at[mine, half], ssem.at[4 * w + 1 + k], rsem.at[4 * w + 1 + k],
                                   (px, py, c)))
        return cps

    return _Side(shards, [jax.ShapeDtypeStruct((4,) + s.shape, s.dtype) for s in shards], 4 * len(shards), copies)


def _side_gather_d2d(gathered):
    def copies(ins, outs, ssem, rsem):
        x, y, c, chips = _place()
        cps = []
        for w in range(len(outs)):
            half = _half(c, outs[w].shape[1])
            for k, (px, py) in enumerate(chips):
                theirs = outs[w].at[2 * px + py, half]
                cps.append(_remote(theirs, theirs, ssem.at[3 * w + k], rsem.at[3 * w + k], (x, y, 1 - c)))
        return cps

    return _Side(gathered, [jax.ShapeDtypeStruct(g.shape, g.dtype) for g in gathered], 3 * len(gathered), copies,
                 aliased=True)


def _side_swap_halves(grads):
    def copies(ins, outs, ssem, rsem):
        x, y, c, _ = _place()
        return [_remote(ins[w].at[:, _half(1 - c, ins[w].shape[1]), :], outs[w], ssem.at[w], rsem.at[w], (x, y, 1 - c))
                for w in range(len(ins))]

    return _Side(grads, [jax.ShapeDtypeStruct((4, g.shape[1] // 2, g.shape[2]), g.dtype) for g in grads], len(grads), copies)


def _side_scatter(sums):
    def copies(ins, outs, ssem, rsem):
        x, y, c, chips = _place()
        return [_remote(ins[w].at[2 * px + py], outs[w].at[k], ssem.at[3 * w + k], rsem.at[3 * w + k], (px, py, c))
                for w in range(len(ins)) for k, (px, py) in enumerate(chips)]

    return _Side(sums, [jax.ShapeDtypeStruct((3,) + s.shape[1:], s.dtype) for s in sums], 3 * len(sums), copies)


def _side_swap_reduced(halves):
    def copies(ins, outs, ssem, rsem):
        x, y, c, _ = _place()
        return [_remote(ins[w], outs[w], ssem.at[w], rsem.at[w], (x, y, 1 - c)) for w in range(len(ins))]

    return _Side(halves, [jax.ShapeDtypeStruct(h.shape, h.dtype) for h in halves], len(halves), copies)


def _chip_sum(g, recv, c_arr, name):
    _, r, cols = g.shape
    h = r // 2
    tr = _pick(h, 512, 16)
    g4 = g.reshape(4, 2, h, cols)

    def body(c_ref, g_ref, r_ref, o_ref):
        o_ref[...] = (g_ref[...] + r_ref[...]).astype(BF16)

    grid_spec = pltpu.PrefetchScalarGridSpec(
        num_scalar_prefetch=1, grid=(4, h // tr),
        in_specs=[pl.BlockSpec((None, None, tr, cols), lambda s, i, c_ref: (s, c_ref[0], i, 0)),
                  pl.BlockSpec((None, tr, cols), lambda s, i, c_ref: (s, i, 0))],
        out_specs=pl.BlockSpec((None, tr, cols), lambda s, i, c_ref: (s, i, 0)))
    return _pcall(body, name=name, grid_spec=grid_spec, out_shape=jax.ShapeDtypeStruct((4, h, cols), BF16),
                  compiler_params=_params())(c_arr, g4, recv)


def _owner_sum(sums, others, s_arr, name):
    _, h, cols = sums.shape
    tr = _pick(h, 512, 16)

    def body(s_ref, a_ref, o_ref, out_ref):
        f = lambda v: v.astype(F32)
        out_ref[...] = (f(a_ref[...]) + f(o_ref[0])) + (f(o_ref[1]) + f(o_ref[2]))

    grid_spec = pltpu.PrefetchScalarGridSpec(
        num_scalar_prefetch=1, grid=(h // tr,),
        in_specs=[pl.BlockSpec((None, tr, cols), lambda i, s_ref: (s_ref[0], i, 0)),
                  pl.BlockSpec((3, tr, cols), lambda i, s_ref: (0, i, 0))],
        out_specs=pl.BlockSpec((tr, cols), lambda i, s_ref: (i, 0)))
    return _pcall(body, name=name, grid_spec=grid_spec, out_shape=jax.ShapeDtypeStruct((h, cols), F32),
                  compiler_params=_params())(s_arr, sums, others)


def _allreduce_small(v, name):
    def body(v_ref, o_ref, r0, r1, ssem, rsem):
        x, y, c, chips = _place()
        cp = _remote(v_ref, r0, ssem.at[0], rsem.at[0], (x, y, 1 - c))
        cp.start()
        cp.wait()
        o_ref[...] = v_ref[...] + r0[...]
        cps = []
        for k, (px, py) in enumerate(chips):
            cp = _remote(o_ref, r1.at[k], ssem.at[1 + k], rsem.at[1 + k], (px, py, c))
            cp.start()
            cps.append(cp)
        for cp in cps:
            cp.wait()
        o_ref[...] = (o_ref[...] + r1[0]) + (r1[1] + r1[2])

    vm = pl.BlockSpec(memory_space=pltpu.VMEM)
    return _pcall(body, name=name, in_specs=[vm], out_specs=vm, out_shape=jax.ShapeDtypeStruct(v.shape, F32),
                  scratch_shapes=[pltpu.VMEM(v.shape, F32), pltpu.VMEM((3,) + v.shape, F32),
                                  pltpu.SemaphoreType.DMA((4,)), pltpu.SemaphoreType.DMA((4,))],
                  compiler_params=_params(has_side_effects=True))(v)


def _tile_rows(size):
    return -(-size // (8 * LANE)) * 8


def _pack_small(parts):
    pieces = []
    for p in parts:
        flat = p.reshape(-1).astype(F32)
        pieces.append(jnp.pad(flat, (0, _tile_rows(p.size) * LANE - p.size)).reshape(-1, LANE))
    rows = sum(x.shape[0] for x in pieces)
    pieces.append(jnp.zeros(((-rows) % 64, LANE), F32))
    return jnp.concatenate(pieces, axis=0)


def _unpack_small(packed, like):
    out, pos = [], 0
    for p in like:
        rows = _tile_rows(p.size)
        out.append(packed[pos:pos + rows].reshape(-1)[:p.size].reshape(p.shape))
        pos += rows
    return out


FFN_FWD_ROWS, FFN_BWD_ROWS = 1024, 512
FFN_SUB_ROWS = 256


def _ffn_specs(n, d, fs, cap):
    rows = _pick(n, cap, 16)
    row = pl.BlockSpec((rows, d), lambda i, s: (i, 0))
    gain = pl.BlockSpec((1, d), lambda i, s: (0, 0))
    w_row = pl.BlockSpec((None, fs, d), lambda i, s: (s, 0, 0))
    hid = pl.BlockSpec((None, rows, fs), lambda i, s: (s, i, 0))
    return rows, row, gain, w_row, hid


def _ffn_fwd(h, g, w1t, w3t, w2, tag, plan):
    n, d = h.shape
    ns, fs, _ = w2.shape
    rows, row, gain, w_row, hid = _ffn_specs(n, d, fs, FFN_FWD_ROWS)
    sub = rows

    def body(h_ref, g_ref, w1_ref, w3_ref, w2_ref, out_ref, n1_ref, a_ref, b_ref, hm_ref, acc_ref):
        s = pl.program_id(1)

        @pl.when(s == 0)
        def _():
            xv = h_ref[...]
            rstd = lax.rsqrt(jnp.mean(xv * xv, axis=-1, keepdims=True) + EPS)
            n1_ref[...] = (xv * rstd * g_ref[...]).astype(BF16)
            acc_ref[...] = jnp.zeros_like(acc_ref)

        def up(j):
            n1 = n1_ref[j * sub:(j + 1) * sub, :]
            return _dot(n1, w1_ref[...], NT), _dot(n1, w3_ref[...], NT)

        cur = up(0)
        for j in range(rows // sub):
            nxt = up(j + 1) if (j + 1) * sub < rows else None
            a, b = cur
            r = slice(j * sub, (j + 1) * sub)
            hm = (a * _sigmoid(a) * b).astype(BF16)
            a_ref[r, :] = a.astype(BF16)
            b_ref[r, :] = b.astype(BF16)
            hm_ref[r, :] = hm
            acc_ref[r, :] += _dot(hm, w2_ref[...])
            cur = nxt

        @pl.when(s == ns - 1)
        def _():
            out_ref[...] = h_ref[...] + 0.5 * acc_ref[...]

    hid_shape = jax.ShapeDtypeStruct((ns, n, fs), BF16)
    plan.before(f"{tag}_fwd")
    out, n1, a, b, hm = _pcall(
        body, name=f"{tag}_fwd", grid=(n // rows, ns), in_specs=[row, gain, w_row, w_row, w_row],
        out_specs=[row, row, hid, hid, hid],
        out_shape=[jax.ShapeDtypeStruct((n, d), F32), jax.ShapeDtypeStruct((n, d), BF16), hid_shape, hid_shape, hid_shape],
        scratch_shapes=[pltpu.VMEM((rows, d), F32)], compiler_params=_params())(h, g, w1t, w3t, w2)
    plan.after(f"{tag}_fwd")
    return out, (h, n1, a, b, hm)


def _ffn_bwd(dout, saved, g, w1, w3, w2, tag, plan):
    h, n1, a, b, hm = saved
    n, d = h.shape
    ns, fs, _ = w2.shape
    rows, row, gain, w_row, hid = _ffn_specs(n, d, fs, FFN_BWD_ROWS)
    sub = _pick(rows, FFN_SUB_ROWS, 16)

    def body(do_ref, h_ref, g_ref, a_ref, b_ref, w1_ref, w3_ref, w2_ref, dh_ref, da_ref, db_ref, dg_ref, acc_ref):
        i, s = pl.program_id(0), pl.program_id(1)

        @pl.when(s == 0)
        def _():
            acc_ref[...] = jnp.zeros_like(acc_ref)

        @pl.when((s == 0) & (i == 0))
        def _():
            dg_ref[...] = jnp.zeros_like(dg_ref)

        def up(j):
            return _dot(0.5 * do_ref[j * sub:(j + 1) * sub, :], w2_ref[...], NT)

        cur = up(0)
        for j in range(rows // sub):
            nxt = up(j + 1) if (j + 1) * sub < rows else None
            r = slice(j * sub, (j + 1) * sub)
            av, bv = a_ref[r, :].astype(F32), b_ref[r, :].astype(F32)
            sg = _sigmoid(av)
            da = (cur * bv * (sg * (1.0 + av * (1.0 - sg)))).astype(BF16)
            db = (cur * av * sg).astype(BF16)
            da_ref[r, :] = da
            db_ref[r, :] = db
            acc_ref[r, :] += _dot(da, w1_ref[...]) + _dot(db, w3_ref[...])
            cur = nxt

        @pl.when(s == ns - 1)
        def _():
            xv, dn = h_ref[...], acc_ref[...]
            rstd = lax.rsqrt(jnp.mean(xv * xv, axis=-1, keepdims=True) + EPS)
            xh = xv * rstd
            dg_ref[...] += jnp.sum(dn * xh, axis=0, keepdims=True)
            dxh = dn * g_ref[...]
            dh_ref[...] = do_ref[...] + rstd * (dxh - xh * jnp.mean(dxh * xh, axis=-1, keepdims=True))

    hid_shape = jax.ShapeDtypeStruct((ns, n, fs), BF16)
    plan.before(f"{tag}_bwd")
    dh, da, db, dg = _pcall(
        body, name=f"{tag}_bwd", grid=(n // rows, ns), in_specs=[row, row, gain, hid, hid, w_row, w_row, w_row],
        out_specs=[row, hid, hid, gain],
        out_shape=[jax.ShapeDtypeStruct((n, d), F32), hid_shape, hid_shape, jax.ShapeDtypeStruct((1, d), F32)],
        scratch_shapes=[pltpu.VMEM((rows, d), F32)], compiler_params=_params())(dout, h, g, a, b, w1, w3, w2)
    plan.after(f"{tag}_bwd")
    plan.before(f"{tag}_gw2")
    gw2 = _mm(hm, dout, ta=True, shard='m', alpha=0.5, name=f"{tag}_gw2")
    plan.after(f"{tag}_gw2")
    gw1 = _mm(da, n1, ta=True, shard='m', name=f"{tag}_gw1")
    gw3 = _mm(db, n1, ta=True, shard='m', name=f"{tag}_gw3")
    return dh, dg, gw1, gw3, gw2


def _local_step(x, tgt, plan):
    n = x.shape[0]
    grads = plan.grads

    def f(name):
        w = plan.get(name)
        return w.reshape(1, D_MODEL) if name.endswith('_norm') and name != 'gla_out_norm' else w

    def carried(tag, fn, *args, **kw):
        plan.before(tag)
        out = fn(*args, **kw)
        plan.after(tag)
        return out

    h1, ffn1 = _ffn_fwd(x, f('ffn1_norm'), f('ffn1_w1'), f('ffn1_w3'), f('ffn1_w2'), "ffn1", plan)
    u = carried("mix_rms", _rms_fwd, h1, f('mix_norm'), "mix_rms")
    w_in = f('w_in')
    w_a = jnp.concatenate([w_in[:, :512], _pad_heads(w_in[:, 512:768]), _pad_heads(w_in[:, 768:1024]), w_in[:, 1024:2048]],
                          axis=1)
    w_al = jnp.pad(w_in[:, 2048:2048 + GLA_RANK], ((0, 0), (0, LANE - GLA_RANK)))
    w_g = w_in[:, 2048 + GLA_RANK:]
    za = _mm(u, w_a, name="in_a")
    zg = _mm(u, w_g, name="in_g")
    al = _mm(u, w_al, name="in_al")
    ar, ai, bbar_re, bbar_im = _s5_discretize(f('s5_lambda_re'), f('s5_lambda_im'), f('s5_log_dt'), f('s5_b_re'), f('s5_b_im'))
    t_b = _bd_tiles(bbar_re.transpose(0, 2, 1), bbar_im.transpose(0, 2, 1)).astype(BF16)
    t_c = _bd_tiles(f('s5_c_re'), -f('s5_c_im')).astype(BF16)
    ar8 = jnp.broadcast_to(ar.reshape(1, S5_GP), (SEG, S5_GP))
    ai8 = jnp.broadcast_to(ai.reshape(1, S5_GP), (SEG, S5_GP))
    pw_r, pw_i = _segment_powers(ar, ai, n // SEG)
    dskip = f('s5_d').reshape(1, S5_W)
    u_s5 = _permute_rows(za[:, :S5_W])
    xs = _s5_scan(u_s5, t_b, ar8, ai8, pw_r, pw_i, "s5_scan")
    ys_p = _bd_reduce(xs, t_c, _scale_rows(u_s5, dskip, "s5_skip"), "s5_y")
    ys = _unpermute_rows(ys_p)
    zgelu = _gelu_fwd(ys, "s5_gelu")
    t_glu = _mm(zgelu, f('s5_glu_w'), bias=f('s5_glu_b').reshape(1, S5_W), name="s5_glu_t")
    y_s5 = _glu_fwd(zgelu, t_glu, "s5_glu")
    wup = jnp.pad(f('gla_a_up_w'), ((0, LANE - GLA_RANK), (0, 0)))
    wup_h = wup.reshape(LANE, GLA_HEADS, GLA_DK).transpose(1, 0, 2)
    bup_h = f('gla_a_up_b').reshape(GLA_HEADS, 1, GLA_DK)
    gn_h = f('gla_out_norm').reshape(GLA_HEADS, 1, GLA_DV)
    y_gla, s_prev = carried("gla_fwd", _gla_fwd, za, al, wup_h, bup_h, gn_h, "gla_fwd")
    ps = _mm(y_s5, f('proj_s5'), name="proj_s5")
    pg = carried("proj_gla", _mm, y_gla, f('proj_gla'), name="proj_gla")
    merged = _merge_fwd(zg, ps, pg, "merge")
    h2 = _mm(merged, f('w_out'), res=h1, name="w_out")
    h3, ffn2 = _ffn_fwd(h2, f('ffn2_norm'), f('ffn2_w1'), f('ffn2_w3'), f('ffn2_w2'), "ffn2", plan)
    loss, dh3, g_final = _final_loss(h3, f('final_norm').reshape(1, D_MODEL), tgt, "loss")
    grads['final_norm'] = g_final.reshape(D_MODEL)
    dh2, grads['ffn2_norm'], grads['ffn2_w1'], grads['ffn2_w3'], grads['ffn2_w2'] = _ffn_bwd(
        dh3, ffn2, f('ffn2_norm'), f('ffn2_w1'), f('ffn2_w3'), f('ffn2_w2'), "ffn2", plan)
    dm = _mm(dh2, f('w_out'), tb=True, name="d_merged")
    grads['w_out'] = _mm(merged, dh2, ta=True, name="g_w_out")
    dps, dpg, dzg = carried("d_merge", _merge_bwd, dm, zg, ps, pg, "d_merge")
    grads['proj_s5'] = _mm(y_s5, dps, ta=True, name="g_proj_s5")
    grads['proj_gla'] = _mm(y_gla, dpg, ta=True, name="g_proj_gla")
    dy_s5 = _mm(dps, f('proj_s5'), tb=True, name="d_y_s5")
    dy_gla = _mm(dpg, f('proj_gla'), tb=True, name="d_y_gla")
    dzgelu, dt_glu, g_glu_b = _glu_bwd1(dy_s5, zgelu, t_glu, "d_glu")
    grads['s5_glu_b'] = g_glu_b.reshape(S5_W)
    grads['s5_glu_w'] = _mm(zgelu, dt_glu, ta=True, name="g_glu_w")
    dzgelu = _mm(dt_glu, f('s5_glu_w'), tb=True, res=dzgelu, name="d_gelu")
    dys, du_skip, g_d = _glu_bwd2(_permute_rows(dzgelu), ys_p, u_s5, dskip, "d_s5_y")
    grads['s5_d'] = g_d.reshape(S5_G, S5_H)
    lam, da8 = _s5_scan_bwd(dys, t_c, xs, ar8, ai8, pw_r, pw_i, "s5_scan_bwd")
    g_c = _bd_blocks(_bd_outer(dys, xs, "g_s5_c"))
    grads['s5_c_re'], grads['s5_c_im'] = g_c[0], -g_c[1]
    g_b = _bd_blocks(_bd_outer(u_s5, lam, "g_s5_b")).transpose(0, 1, 3, 2)
    g_bbar_re, g_bbar_im = g_b[0], g_b[1]
    da = jnp.sum(da8, axis=0)
    g_ar, g_ai = da[:S5_GP].reshape(S5_G, S5_P), da[S5_GP:].reshape(S5_G, S5_P)
    _, disc_vjp = jax.vjp(_s5_discretize, f('s5_lambda_re'), f('s5_lambda_im'), f('s5_log_dt'), f('s5_b_re'), f('s5_b_im'))
    (grads['s5_lambda_re'], grads['s5_lambda_im'], grads['s5_log_dt'], grads['s5_b_re'],
     grads['s5_b_im']) = disc_vjp((g_ar, g_ai, g_bbar_re, g_bbar_im))
    du_s5 = _unpermute_rows(_bd_reduce(lam, t_b, du_skip, "d_s5_u"))
    dza, dz, dgn, dbup = carried("gla_bwd", _gla_bwd, za, al, wup_h, bup_h, gn_h, s_prev, dy_gla, du_s5, "gla_bwd")
    grads['gla_out_norm'] = dgn.reshape(GLA_HEADS * GLA_DV)
    grads['gla_a_up_b'] = dbup.reshape(GLA_HEADS * GLA_DK)
    grads['gla_a_up_w'] = _unpad_heads(_mm(al, dz, ta=True, name="g_a_up")[:GLA_RANK])
    dal = _mm(dz, _pad_heads(wup), tb=True, name="d_a_low")
    g_wa = _mm(u, dza, ta=True, name="g_in_a")
    g_wg = _mm(u, dzg, ta=True, name="g_in_g")
    g_wal = _mm(u, dal, ta=True, name="g_in_al")
    grads['w_in'] = jnp.concatenate([g_wa[:, :512], _unpad_heads(g_wa[:, 512:1024]), _unpad_heads(g_wa[:, 1024:1536]),
                                     g_wa[:, 1536:], g_wal[:, :GLA_RANK], g_wg], axis=1)
    du = carried("d_u_a", _mm, dza, w_a, tb=True, name="d_u_a")
    du = _mm(dzg, w_g, tb=True, res=du, name="d_u_g")
    du = _mm(dal, w_al, tb=True, res=du, name="d_u_al")
    dh1, g_mix = carried("d_mix_rms", _rms_bwd, h1, f('mix_norm'), du, dh2, "d_mix_rms")
    grads['mix_norm'] = g_mix
    dx, grads['ffn1_norm'], grads['ffn1_w1'], grads['ffn1_w3'], grads['ffn1_w2'] = _ffn_bwd(
        dh1, ffn1, f('ffn1_norm'), f('ffn1_w1'), f('ffn1_w3'), f('ffn1_w2'), "ffn1", plan)
    return loss[0, 0], dx


MIXER_WEIGHTS = ['w_in', 's5_glu_w', 'proj_s5', 'proj_gla', 'w_out', 'gla_a_up_w']
FFN1_WEIGHTS, FFN2_WEIGHTS = FFN_WEIGHTS[:3], FFN_WEIGHTS[3:]
TRANSPOSED = ['ffn1_w1', 'ffn1_w3', 'ffn2_w1', 'ffn2_w3']


def _local_shard(w, nm):
    return jnp.swapaxes(w, 1, 2)[0] if nm in TRANSPOSED else w[0]
GRAD_GROUPS = {'ffn2': FFN2_WEIGHTS, 'mixer': ['w_out', 'proj_s5', 'proj_gla', 's5_glu_w', 'w_in'], 'ffn1': FFN1_WEIGHTS}


class _Plan:
    def __init__(self, a, c_arr, s_arr):
        self.a, self.c_arr, self.s_arr = a, c_arr, s_arr
        self.grads, self.weights, self.riding = {}, {}, {}
        self.g4s, self.chip_sums, self.halves, self.sib_halves = {}, {}, {}, {}
        for nm in SMALL:
            if nm != 'gla_a_up_w':
                self.weights[nm] = a[nm] if nm == 'final_norm' else a[nm][0]
        ici = _side_gather_ici(self._shards(FFN1_WEIGHTS))
        _run_side(ici, "gather_ffn1_ici")
        self._gathered(FFN1_WEIGHTS, _run_side(_side_gather_d2d(ici.outs), "gather_ffn1_d2d"))

    def _shards(self, names):
        return [_local_shard(self.a[nm], nm).astype(F32 if nm == 'gla_a_up_w' else BF16) for nm in names]

    def _gathered(self, names, arrs):
        for nm, g4 in zip(names, arrs):
            if nm in FFN_WEIGHTS:
                self.weights[nm] = g4
            elif nm in COL_SHARDED:
                self.weights[nm] = jnp.concatenate([g4[s] for s in range(4)], axis=1)
            else:
                self.weights[nm] = g4.reshape(4 * g4.shape[1], g4.shape[2])

    def get(self, name):
        return self.weights[name]

    def _shard_major(self, nm):
        g = self.grads[nm]
        if nm in FFN_WEIGHTS:
            return g
        if nm in COL_SHARDED:
            return jnp.stack(jnp.split(g, 4, axis=1))
        return g.reshape(4, g.shape[0] // 4, g.shape[1])

    def _schedule(self, tag):
        grp = GRAD_GROUPS
        if tag == "ffn1_fwd":
            return _side_gather_ici(self._shards(MIXER_WEIGHTS)), lambda outs: self.riding.update(mixer_ici=outs)
        if tag == "mix_rms":
            return _side_gather_d2d(self.riding['mixer_ici']), lambda outs: self._gathered(MIXER_WEIGHTS, outs)
        if tag == "gla_fwd":
            return _side_gather_ici(self._shards(FFN2_WEIGHTS)), lambda outs: self.riding.update(ffn2_ici=outs)
        if tag == "proj_gla":
            return _side_gather_d2d(self.riding['ffn2_ici']), lambda outs: self._gathered(FFN2_WEIGHTS, outs)
        steps = {"d_merge": ('ffn2', 0), "gla_bwd": ('ffn2', 1), "d_mix_rms": ('ffn2', 2),
                 "d_u_a": ('mixer', 0), "ffn1_bwd": ('mixer', 1), "ffn1_gw2": ('mixer', 2)}
        if tag in steps:
            group, stage = steps[tag]
            return self._reduce_stage(grp[group], stage)
        return None

    def _reduce_stage(self, names, stage):
        if stage == 0:
            for nm in names:
                self.g4s[nm] = self._shard_major(nm)

            def done(outs):
                for nm, r in zip(names, outs):
                    self.chip_sums[nm] = _chip_sum(self.g4s[nm], r, self.c_arr, f"chip_sum_{nm}")
            return _side_swap_halves([self.g4s[nm] for nm in names]), done
        if stage == 1:
            def done(outs):
                for nm, o in zip(names, outs):
                    self.halves[nm] = _owner_sum(self.chip_sums[nm], o, self.s_arr, f"owner_sum_{nm}")
            return _side_scatter([self.chip_sums[nm] for nm in names]), done

        def done(outs):
            self.sib_halves.update(zip(names, outs))
        return _side_swap_reduced([self.halves[nm] for nm in names]), done

    def before(self, tag):
        entry = self._schedule(tag)
        if entry is not None:
            side, done = entry
            self.riding[tag] = (side, done)
            _RIDER.append(side)

    def after(self, tag):
        if tag in self.riding:
            side, done = self.riding.pop(tag)
            assert not _RIDER and side.outs is not None, tag
            done(side.outs)

    def finish(self):
        names = GRAD_GROUPS['ffn1']
        for stage in range(3):
            side, done = self._reduce_stage(names, stage)
            done(_run_side(side, f"grad_ffn1_stage{stage}"))


def _train_step(a):
    x = a['x'][0]
    tgt = a['loss_target'][0]
    xi, yi, ci = lax.axis_index("x"), lax.axis_index("y"), lax.axis_index("c")
    c_arr = jnp.reshape(ci, (1,)).astype(jnp.int32)
    s_arr = jnp.reshape(2 * xi + yi, (1,)).astype(jnp.int32)
    plan = _Plan(a, c_arr, s_arr)
    loss, dx = _local_step(x, tgt, plan)
    plan.finish()
    grads = plan.grads
    loss = lax.psum(loss, ("x", "y", "c"))
    halves = [plan.halves[nm] for nm in SHARDED]
    sib_halves = [plan.sib_halves[nm] for nm in SHARDED]
    red = {}
    small_parts = [grads[nm].reshape(a[nm].shape) for nm in SMALL if nm != 'gla_a_up_w'] + [grads['gla_a_up_w']]
    small_sum = _unpack_small(_allreduce_small(_pack_small(small_parts), "allreduce_small"), small_parts)
    small_names = [nm for nm in SMALL if nm != 'gla_a_up_w']
    for nm, g in zip(small_names, small_sum[:-1]):
        red[nm] = g
    g_up = small_sum[-1]
    red['gla_a_up_w'] = lax.dynamic_slice(g_up, (0, (2 * xi + yi) * GLA_DK), (GLA_RANK, GLA_DK))
    out_g, out_d, out_m, out_v = {}, {}, {}, {}
    for nm, own, sib in zip(SHARDED, halves, sib_halves):
        loc = lambda pre: _local_shard(a[pre + nm], nm)
        res = _adamw_halves(loc(''), own, sib, loc('m_'), loc('v_'), c_arr, f"adamw_{nm}")
        back = (lambda t: jnp.swapaxes(t[None], 1, 2)) if nm in TRANSPOSED else (lambda t: t[None])
        out_g[nm], out_d[nm], out_m[nm], out_v[nm] = (back(t) for t in res)
    rest = [nm for nm in WEIGHTS if nm not in SHARDED]
    pk = lambda pre: _pack_small([a[pre + nm] for nm in rest])
    d, nm_, nv_ = _adamw(pk(''), _pack_small([red[nm] for nm in rest]), pk('m_'), pk('v_'), "adamw_small")
    like = [a[nm] for nm in rest]
    for nm, g, dd, mm_, vv_ in zip(rest, [red[nm].reshape(a[nm].shape) for nm in rest], _unpack_small(d, like),
                                   _unpack_small(nm_, like), _unpack_small(nv_, like)):
        out_g[nm], out_d[nm], out_m[nm], out_v[nm] = g, dd, mm_, vv_
    return (loss, dx[None], *[out_g[nm] for nm in WEIGHTS], *[out_d[nm] for nm in WEIGHTS],
            *[out_m[nm] for nm in WEIGHTS], *[out_v[nm] for nm in WEIGHTS])


def kernel(x, ffn1_norm, ffn1_w1, ffn1_w3, ffn1_w2, mix_norm, w_in, s5_lambda_re, s5_lambda_im, s5_log_dt, s5_b_re, s5_b_im, s5_c_re, s5_c_im, s5_d, s5_glu_w, s5_glu_b, gla_a_up_w, gla_a_up_b, gla_out_norm, proj_s5, proj_gla, w_out, ffn2_norm, ffn2_w1, ffn2_w3, ffn2_w2, final_norm, loss_target, m_ffn1_norm, m_ffn1_w1, m_ffn1_w3, m_ffn1_w2, m_mix_norm, m_w_in, m_s5_lambda_re, m_s5_lambda_im, m_s5_log_dt, m_s5_b_re, m_s5_b_im, m_s5_c_re, m_s5_c_im, m_s5_d, m_s5_glu_w, m_s5_glu_b, m_gla_a_up_w, m_gla_a_up_b, m_gla_out_norm, m_proj_s5, m_proj_gla, m_w_out, m_ffn2_norm, m_ffn2_w1, m_ffn2_w3, m_ffn2_w2, m_final_norm, v_ffn1_norm, v_ffn1_w1, v_ffn1_w3, v_ffn1_w2, v_mix_norm, v_w_in, v_s5_lambda_re, v_s5_lambda_im, v_s5_log_dt, v_s5_b_re, v_s5_b_im, v_s5_c_re, v_s5_c_im, v_s5_d, v_s5_glu_w, v_s5_glu_b, v_gla_a_up_w, v_gla_a_up_b, v_gla_out_norm, v_proj_s5, v_proj_gla, v_w_out, v_ffn2_norm, v_ffn2_w1, v_ffn2_w3, v_ffn2_w2, v_final_norm):
    return _train_step(dict(locals()))
```

```python
import functools

import jax
import jax.numpy as jnp
from jax import lax
from jax.experimental import pallas as pl
from jax.experimental.pallas import tpu as pltpu

F32 = jnp.float32
BF16 = jnp.bfloat16
HI = lax.Precision.HIGHEST
MESH_ID = pl.DeviceIdType.MESH

D_MODEL = 1024
EPS = 1e-6
S5_G, S5_P, S5_H = 32, 64, 16
S5_W = S5_G * S5_H
S5_GP = S5_G * S5_P
SEG = 8
SCAN_ROWS = 256
GLA_HEADS, GLA_DK, GLA_DV = 4, 64, 128
GLA_CHUNK = 64
GLA_TAU = 16.0
GLA_RANK = 16
ADAM_LR, ADAM_B1, ADAM_B2, ADAM_EPS, ADAM_WD, ADAM_STEP = 0.001, 0.9, 0.999, 1e-08, 0.01, 10
V7X_VMEM_LIMIT = 56 * 1024 * 1024
LANE = 128

WEIGHTS = ['ffn1_norm', 'ffn1_w1', 'ffn1_w3', 'ffn1_w2', 'mix_norm', 'w_in', 's5_lambda_re', 's5_lambda_im',
           's5_log_dt', 's5_b_re', 's5_b_im', 's5_c_re', 's5_c_im', 's5_d', 's5_glu_w', 's5_glu_b', 'gla_a_up_w',
           'gla_a_up_b', 'gla_out_norm', 'proj_s5', 'proj_gla', 'w_out', 'ffn2_norm', 'ffn2_w1', 'ffn2_w3',
           'ffn2_w2', 'final_norm']
SHARDED = ['ffn1_w1', 'ffn1_w3', 'ffn1_w2', 'w_in', 's5_glu_w', 'proj_s5', 'proj_gla', 'w_out',
           'ffn2_w1', 'ffn2_w3', 'ffn2_w2']
COL_SHARDED = ['ffn1_w1', 'ffn1_w3', 'w_in', 'proj_s5', 'proj_gla', 'ffn2_w1', 'ffn2_w3', 'gla_a_up_w']
SMALL = [n for n in WEIGHTS if n not in SHARDED]
FFN_WEIGHTS = ['ffn1_w1', 'ffn1_w3', 'ffn1_w2', 'ffn2_w1', 'ffn2_w3', 'ffn2_w2']


def _params(**kw):
    return pltpu.CompilerParams(vmem_limit_bytes=V7X_VMEM_LIMIT, **kw)


class _Side:
    def __init__(self, ins, out_shapes, nsem, copies, aliased=False):
        self.ins, self.out_shapes, self.nsem, self.copies, self.aliased = list(ins), list(out_shapes), nsem, copies, aliased
        self.outs = None


_RIDER = []


def _pcall(body, **kw):
    if _RIDER:
        return _carry(body, _RIDER.pop(), **kw)
    return pl.pallas_call(body, **kw)


def _carry(body, side, *, name, grid, in_specs, out_specs, out_shape, scratch_shapes=(), compiler_params=None):
    del compiler_params
    single = not isinstance(out_shape, (list, tuple))
    out_specs = [out_specs] if single else list(out_specs)
    out_shape = [out_shape] if single else list(out_shape)
    n_in, n_out, n_scr = len(in_specs), len(out_shape), len(scratch_shapes)
    s_in, s_out = len(side.ins), len(side.out_shapes)
    any_spec = pl.BlockSpec(memory_space=pl.ANY)

    def wrapped(*refs):
        cuts = [n_in, s_in, n_out, s_out, n_scr]
        parts, pos = [], 0
        for c in cuts:
            parts.append(refs[pos:pos + c])
            pos += c
        ins, sins, outs, souts, scr = parts
        ssem, rsem = refs[pos], refs[pos + 1]
        first = last = None
        for d, g in enumerate(grid):
            i = pl.program_id(d)
            first = (i == 0) if first is None else first & (i == 0)
            last = (i == g - 1) if last is None else last & (i == g - 1)

        @pl.when(first)
        def _():
            for cp in side.copies(sins, souts, ssem, rsem):
                cp.start()

        body(*ins, *outs, *scr)

        @pl.when(last)
        def _():
            for cp in side.copies(sins, souts, ssem, rsem):
                cp.wait()

    call = pl.pallas_call(
        wrapped, name=name, grid=grid, in_specs=list(in_specs) + [any_spec] * s_in,
        out_specs=out_specs + [any_spec] * s_out, out_shape=out_shape + side.out_shapes,
        scratch_shapes=list(scratch_shapes) + [pltpu.SemaphoreType.DMA((side.nsem,)), pltpu.SemaphoreType.DMA((side.nsem,))],
        input_output_aliases={n_in + j: n_out + j for j in range(s_in)} if side.aliased else {},
        compiler_params=_params(has_side_effects=True))

    def run(*args):
        res = call(*args, *side.ins)
        side.outs = list(res[n_out:])
        return res[0] if single else list(res[:n_out])

    return run


def _run_side(side, name):
    s_in, s_out = len(side.ins), len(side.out_shapes)
    any_spec = pl.BlockSpec(memory_space=pl.ANY)

    def body(*refs):
        sins, souts = refs[:s_in], refs[s_in:s_in + s_out]
        ssem, rsem = refs[s_in + s_out:]
        cps = side.copies(sins, souts, ssem, rsem)
        for cp in cps:
            cp.start()
        for cp in cps:
            cp.wait()

    side.outs = list(pl.pallas_call(
        body, name=name, in_specs=[any_spec] * s_in, out_specs=[any_spec] * s_out, out_shape=side.out_shapes,
        scratch_shapes=[pltpu.SemaphoreType.DMA((side.nsem,)), pltpu.SemaphoreType.DMA((side.nsem,))],
        input_output_aliases={j: j for j in range(s_in)} if side.aliased else {},
        compiler_params=pltpu.CompilerParams(has_side_effects=True))(*side.ins))
    return side.outs


def _pick(n, cap, quantum):
    if n <= cap:
        return n
    best = None
    for t in range(quantum, cap + 1, quantum):
        if n % t == 0:
            best = t
    assert best is not None, (n, cap, quantum)
    return best


def _sigmoid(x):
    return jax.nn.sigmoid(x)


def _mm(a, b, *, name, ta=False, tb=False, out_dtype=F32, alpha=1.0, res=None, bias=None, exact=False, shard=None):
    ns = 4
    (k_a, m) = a.shape[-2:] if ta else a.shape[-2:][::-1]
    (k_b, n) = b.shape[-2:][::-1] if tb else b.shape[-2:]
    assert k_a == k_b, (a.shape, b.shape, ta, tb)
    assert (a.ndim == 3) == (shard in ('k', 'm')) and (b.ndim == 3) == (shard in ('n', 'k'))
    k = k_a
    tm = _pick(m, 1024, 128)
    tn = _pick(n, 1024, 128)
    tk = _pick(k, 1024, 128)
    pm, pn, pk = m // tm, n // tn, k // tk
    gm = pm * (ns if shard == 'm' else 1)
    gn = pn * (ns if shard == 'n' else 1)
    gk = pk * (ns if shard == 'k' else 1)
    dims = (((0,) if ta else (1,), (1,) if tb else (0,)), ((), ()))
    op_dtype = F32 if exact else BF16

    def body(*refs):
        a_ref, b_ref = refs[0], refs[1]
        pos = 2
        res_ref = bias_ref = None
        if res is not None:
            res_ref = refs[pos]
            pos += 1
        if bias is not None:
            bias_ref = refs[pos]
            pos += 1
        o_ref, acc_ref = refs[pos], refs[pos + 1]
        kk = pl.program_id(2)

        @pl.when(kk == 0)
        def _():
            acc_ref[...] = jnp.zeros_like(acc_ref)

        acc_ref[...] += lax.dot_general(a_ref[...].astype(op_dtype), b_ref[...].astype(op_dtype), dims,
                                        precision=HI if exact else None, preferred_element_type=F32)

        @pl.when(kk == gk - 1)
        def _():
            o = acc_ref[...]
            if alpha != 1.0:
                o = o * alpha
            if bias_ref is not None:
                o = o + bias_ref[...]
            if res_ref is not None:
                o = o + res_ref[...]
            o_ref[...] = o.astype(out_dtype)

    def spec(block, sharded_on, order):
        per = {'m': pm, 'n': pn, 'k': pk}

        def index(i, j, kk):
            g = {'m': i, 'n': j, 'k': kk}
            r, c = order(i % pm if shard == 'm' else i, j % pn if shard == 'n' else j, kk % pk if shard == 'k' else kk)
            if sharded_on is None:
                return (r, c)
            return (g[sharded_on] // per[sharded_on], r, c)

        return pl.BlockSpec(block if sharded_on is None else (None,) + block, index)

    a_sh = shard if shard in ('k', 'm') else None
    b_sh = shard if shard in ('n', 'k') else None
    o_sh = shard if shard in ('n', 'm') else None
    a_spec = spec((tk, tm), a_sh, lambda i, j, kk: (kk, i)) if ta else spec((tm, tk), a_sh, lambda i, j, kk: (i, kk))
    b_spec = spec((tn, tk), b_sh, lambda i, j, kk: (j, kk)) if tb else spec((tk, tn), b_sh, lambda i, j, kk: (kk, j))
    ins, in_specs = [a, b], [a_spec, b_spec]
    if res is not None:
        assert o_sh is None
        ins.append(res)
        in_specs.append(pl.BlockSpec((tm, tn), lambda i, j, kk: (i, j)))
    if bias is not None:
        assert o_sh is None
        ins.append(bias)
        in_specs.append(pl.BlockSpec((1, tn), lambda i, j, kk: (0, j)))
    out_shape = (m, n) if o_sh is None else (ns, m, n)
    return _pcall(body, name=name, grid=(gm, gn, gk), in_specs=in_specs,
                  out_specs=spec((tm, tn), o_sh, lambda i, j, kk: (i, j)),
                  out_shape=jax.ShapeDtypeStruct(out_shape, out_dtype),
                  scratch_shapes=[pltpu.VMEM((tm, tn), F32)], compiler_params=_params())(*ins)


ROWS_VMEM_BUDGET = 24 * 1024 * 1024


def _rows(body, ins, outs, *, n, name):
    cols = sum(a.shape[1] for a, kind in ins if kind == 'r') + sum(c for c, _, kind in outs if kind == 'r')
    cap = 256
    while cap < 2048 and 2 * 4 * cols * (2 * cap) <= ROWS_VMEM_BUDGET:
        cap *= 2
    tm = _pick(n, cap, 16)
    in_specs = []
    for arr, kind in ins:
        if kind == 'r':
            in_specs.append(pl.BlockSpec((tm, arr.shape[1]), lambda i: (i, 0)))
        else:
            in_specs.append(pl.BlockSpec(arr.shape, lambda i: (0, 0)))
    out_specs, out_shape = [], []
    for cols, dtype, kind in outs:
        if kind == 'r':
            out_specs.append(pl.BlockSpec((tm, cols), lambda i: (i, 0)))
            out_shape.append(jax.ShapeDtypeStruct((n, cols), dtype))
        else:
            out_specs.append(pl.BlockSpec((1, cols), lambda i: (0, 0)))
            out_shape.append(jax.ShapeDtypeStruct((1, cols), dtype))
    n_in = len(ins)
    acc_ids = [j for j, o in enumerate(outs) if o[2] == 'a']

    def wrapped(*refs):
        if acc_ids:
            @pl.when(pl.program_id(0) == 0)
            def _():
                for j in acc_ids:
                    refs[n_in + j][...] = jnp.zeros_like(refs[n_in + j])
        body(*refs)

    res = _pcall(wrapped, name=name, grid=(n // tm,), in_specs=in_specs, out_specs=out_specs, out_shape=out_shape,
                 compiler_params=_params())(*[a for a, _ in ins])
    return res


def _rms_fwd(x, g, name):
    def body(x_ref, g_ref, o_ref):
        xv = x_ref[...]
        rstd = lax.rsqrt(jnp.mean(xv * xv, axis=-1, keepdims=True) + EPS)
        o_ref[...] = (xv * rstd * g_ref[...]).astype(BF16)
    return _rows(body, [(x, 'r'), (g, 'f')], [(x.shape[1], BF16, 'r')], n=x.shape[0], name=name)[0]


def _rms_bwd(x, g, dn, dres, name):
    def body(x_ref, g_ref, dn_ref, dres_ref, dx_ref, dg_ref):
        xv = x_ref[...]
        rstd = lax.rsqrt(jnp.mean(xv * xv, axis=-1, keepdims=True) + EPS)
        xh = xv * rstd
        dn = dn_ref[...]
        dg_ref[...] += jnp.sum(dn * xh, axis=0, keepdims=True)
        dxh = dn * g_ref[...]
        dx_ref[...] = dres_ref[...] + rstd * (dxh - xh * jnp.mean(dxh * xh, axis=-1, keepdims=True))
    d = x.shape[1]
    return _rows(body, [(x, 'r'), (g, 'f'), (dn, 'r'), (dres, 'r')], [(d, F32, 'r'), (d, F32, 'a')],
                 n=x.shape[0], name=name)


def _gelu_parts(y):
    c0 = 0.7978845608028654
    inner = c0 * (y + 0.044715 * y * y * y)
    th = jnp.tanh(inner)
    return th, c0 * (1.0 + 3.0 * 0.044715 * y * y)


def _gelu_fwd(y, name):
    def body(y_ref, o_ref):
        yv = y_ref[...]
        th, _ = _gelu_parts(yv)
        o_ref[...] = 0.5 * yv * (1.0 + th)
    return _rows(body, [(y, 'r')], [(y.shape[1], F32, 'r')], n=y.shape[0], name=name)[0]


def _glu_fwd(zg, t, name):
    def body(z_ref, t_ref, o_ref):
        o_ref[...] = (z_ref[...] * _sigmoid(t_ref[...])).astype(BF16)
    return _rows(body, [(zg, 'r'), (t, 'r')], [(zg.shape[1], BF16, 'r')], n=zg.shape[0], name=name)[0]


def _glu_bwd1(dy, zg, t, name):
    def body(dy_ref, z_ref, t_ref, dz_ref, dt_ref, db_ref):
        dyv, zv = dy_ref[...], z_ref[...]
        sg = _sigmoid(t_ref[...])
        dz_ref[...] = dyv * sg
        dt = dyv * zv * sg * (1.0 - sg)
        dt_ref[...] = dt.astype(BF16)
        db_ref[...] += jnp.sum(dt, axis=0, keepdims=True)
    w = zg.shape[1]
    return _rows(body, [(dy, 'r'), (zg, 'r'), (t, 'r')], [(w, F32, 'r'), (w, BF16, 'r'), (w, F32, 'a')],
                 n=zg.shape[0], name=name)


def _glu_bwd2(dzg, ys, u, dskip, name):
    def body(dz_ref, y_ref, u_ref, d_ref, dy_ref, du_ref, dd_ref):
        yv = y_ref[...]
        th, dinner = _gelu_parts(yv)
        dy = dz_ref[...] * (0.5 * (1.0 + th) + 0.5 * yv * (1.0 - th * th) * dinner)
        dy_ref[...] = dy
        du_ref[...] = dy * d_ref[...]
        dd_ref[...] += jnp.sum(dy * u_ref[...], axis=0, keepdims=True)
    w = ys.shape[1]
    return _rows(body, [(dzg, 'r'), (ys, 'r'), (u, 'r'), (dskip, 'f')], [(w, F32, 'r'), (w, F32, 'r'), (w, F32, 'a')],
                 n=ys.shape[0], name=name)


def _scale_rows(u, dskip, name):
    def body(u_ref, d_ref, o_ref):
        o_ref[...] = u_ref[...] * d_ref[...]
    return _rows(body, [(u, 'r'), (dskip, 'f')], [(u.shape[1], F32, 'r')], n=u.shape[0], name=name)[0]


def _merge_fwd(zg, ps, pg, name):
    def body(z_ref, ps_ref, pg_ref, o_ref):
        zv = z_ref[...]
        o_ref[...] = (_sigmoid(zv[:, :D_MODEL]) * ps_ref[...] + _sigmoid(zv[:, D_MODEL:]) * pg_ref[...]).astype(BF16)
    return _rows(body, [(zg, 'r'), (ps, 'r'), (pg, 'r')], [(D_MODEL, BF16, 'r')], n=zg.shape[0], name=name)[0]


def _merge_bwd(dm, zg, ps, pg, name):
    def body(dm_ref, z_ref, ps_ref, pg_ref, dps_ref, dpg_ref, dz_ref):
        dmv, zv = dm_ref[...], z_ref[...]
        s1, s2 = _sigmoid(zv[:, :D_MODEL]), _sigmoid(zv[:, D_MODEL:])
        dps_ref[...] = (dmv * s1).astype(BF16)
        dpg_ref[...] = (dmv * s2).astype(BF16)
        dz_ref[:, :D_MODEL] = dmv * ps_ref[...] * s1 * (1.0 - s1)
        dz_ref[:, D_MODEL:] = dmv * pg_ref[...] * s2 * (1.0 - s2)
    return _rows(body, [(dm, 'r'), (zg, 'r'), (ps, 'r'), (pg, 'r')],
                 [(D_MODEL, BF16, 'r'), (D_MODEL, BF16, 'r'), (2 * D_MODEL, F32, 'r')], n=zg.shape[0], name=name)


def _final_loss(h, g, tgt, name):
    def body(h_ref, g_ref, t_ref, loss_ref, dh_ref, dg_ref):
        hv = h_ref[...]
        rstd = lax.rsqrt(jnp.mean(hv * hv, axis=-1, keepdims=True) + EPS)
        xh = hv * rstd
        err = xh * g_ref[...] - t_ref[...]
        part = 0.5 * jnp.sum(jnp.mean(err * err, axis=-1, keepdims=True), axis=0, keepdims=True)
        loss_ref[...] += jnp.broadcast_to(part, loss_ref.shape)
        dout = err * (1.0 / hv.shape[1])
        dg_ref[...] += jnp.sum(dout * xh, axis=0, keepdims=True)
        dxh = dout * g_ref[...]
        dh_ref[...] = rstd * (dxh - xh * jnp.mean(dxh * xh, axis=-1, keepdims=True))
    d = h.shape[1]
    return _rows(body, [(h, 'r'), (g, 'f'), (tgt, 'r')], [(LANE, F32, 'a'), (d, F32, 'r'), (d, F32, 'a')],
                 n=h.shape[0], name=name)


def _adamw_math(wv, gv, mv, vv):
    nm = ADAM_B1 * mv + (1.0 - ADAM_B1) * gv
    nv = ADAM_B2 * vv + (1.0 - ADAM_B2) * (gv * gv)
    m_hat = nm / (1.0 - ADAM_B1 ** ADAM_STEP)
    v_hat = nv / (1.0 - ADAM_B2 ** ADAM_STEP)
    return -ADAM_LR * (m_hat / (jnp.sqrt(v_hat) + ADAM_EPS) + ADAM_WD * wv), nm, nv


def _adamw(w, g, m, v, name):
    def body(w_ref, g_ref, m_ref, v_ref, d_ref, nm_ref, nv_ref):
        d_ref[...], nm_ref[...], nv_ref[...] = _adamw_math(w_ref[...], g_ref[...], m_ref[...], v_ref[...])
    c = w.shape[1]
    return _rows(body, [(w, 'r'), (g, 'r'), (m, 'r'), (v, 'r')], [(c, F32, 'r')] * 3, n=w.shape[0], name=name)


def _adamw_halves(w, g_own, g_sib, m, v, c_arr, name):
    r, cols = w.shape
    h = r // 2
    tr = _pick(h, 512, 8)
    per = h // tr

    def body(c_ref, w_ref, go_ref, gs_ref, m_ref, v_ref, g_ref, d_ref, nm_ref, nv_ref):
        mine = (pl.program_id(0) // per) == c_ref[0]
        gv = jnp.where(mine, go_ref[...], gs_ref[...])
        g_ref[...] = gv
        d_ref[...], nm_ref[...], nv_ref[...] = _adamw_math(w_ref[...], gv, m_ref[...], v_ref[...])

    full = pl.BlockSpec((tr, cols), lambda i, c_ref: (i, 0))
    half = pl.BlockSpec((tr, cols), lambda i, c_ref: (i % per, 0))
    grid_spec = pltpu.PrefetchScalarGridSpec(num_scalar_prefetch=1, grid=(2 * per,),
                                             in_specs=[full, half, half, full, full], out_specs=[full] * 4)
    return _pcall(body, name=name, grid_spec=grid_spec, out_shape=[jax.ShapeDtypeStruct((r, cols), F32)] * 4,
                  compiler_params=_params())(c_arr, w, g_own, g_sib, m, v)


def _shift_rows(v, sh, down):
    rolled = pltpu.roll(v, sh if down else v.shape[0] - sh, axis=0)
    row = lax.broadcasted_iota(jnp.int32, v.shape, 0)
    keep = (row >= sh) if down else (row < v.shape[0] - sh)
    return jnp.where(keep, rolled, 0.0)


def _chain_segments(st_r, st_i, pw_r_ref, pw_i_ref, conj, down):
    vr, vi = st_r[...], st_i[...]
    sh, k = 1, 0
    while sh < SEG:
        pr, pi = pw_r_ref[k:k + 1, :], pw_i_ref[k:k + 1, :]
        if conj:
            pi = -pi
        sr, si = _shift_rows(vr, sh, down), _shift_rows(vi, sh, down)
        vr, vi = vr + pr * sr - pi * si, vi + pr * si + pi * sr
        sh, k = sh * 2, k + 1
    st_r[...] = _shift_rows(vr, 1, down)
    st_i[...] = _shift_rows(vi, 1, down)


def _expand_block(u_ref, t_ref, bu_ref):
    for j in range(BD_TILES):
        k = j % 4
        bu_ref[:, j * BD_ST:(j + 1) * BD_ST] = _dot(u_ref[:, k * BD_CH:(k + 1) * BD_CH], t_ref[j])


def _s5_scan(u, tiles, ar8, ai8, pw_r, pw_i, name):
    n = u.shape[0]
    rb = SCAN_ROWS
    nb, steps, lc = n // rb, rb // SEG, 512

    def body(u_ref, t_ref, ar_ref, ai_ref, pwr_ref, pwi_ref, x_ref, st_r, st_i, bu_ref):
        ph, b = pl.program_id(0), pl.program_id(1)

        @pl.when((ph == 0) & (b == 0))
        def _():
            st_r[...] = jnp.zeros_like(st_r)
            st_i[...] = jnp.zeros_like(st_i)

        _expand_block(u_ref, t_ref, bu_ref)

        def scan(store):
            for c in range(S5_GP // lc):
                re, im = slice(c * lc, (c + 1) * lc), slice(S5_GP + c * lc, S5_GP + (c + 1) * lc)
                a_r, a_i = ar_ref[:, re], ai_ref[:, re]

                def step(s, carry):
                    xr, xi = carry
                    rows = pl.ds(pl.multiple_of(s * SEG, SEG), SEG)
                    nr = a_r * xr - a_i * xi + bu_ref[rows, re]
                    ni = a_r * xi + a_i * xr + bu_ref[rows, im]
                    if store:
                        x_ref[rows, re] = nr
                        x_ref[rows, im] = ni
                    return nr, ni

                xr, xi = lax.fori_loop(0, steps, step, (st_r[:, re], st_i[:, re]), unroll=4)
                st_r[:, re] = xr
                st_i[:, re] = xi

        @pl.when(ph == 0)
        def _():
            scan(False)

        @pl.when((ph == 0) & (b == nb - 1))
        def _():
            _chain_segments(st_r, st_i, pwr_ref, pwi_ref, conj=False, down=True)

        @pl.when(ph == 1)
        def _():
            scan(True)

    full = lambda a: pl.BlockSpec(a.shape, lambda ph, b: (0, 0))
    return _pcall(body, name=name, grid=(2, nb),
                  in_specs=[pl.BlockSpec((rb, S5_W), lambda ph, b: (b, 0)), pl.BlockSpec(tiles.shape, lambda ph, b: (0, 0, 0)),
                            full(ar8), full(ai8), full(pw_r), full(pw_i)],
                  out_specs=pl.BlockSpec((rb, 2 * S5_GP), lambda ph, b: (b * ph, 0)),
                  out_shape=jax.ShapeDtypeStruct((n, 2 * S5_GP), F32),
                  scratch_shapes=[pltpu.VMEM((SEG, S5_GP), F32), pltpu.VMEM((SEG, S5_GP), F32),
                                  pltpu.VMEM((rb, 2 * S5_GP), F32)],
                  compiler_params=_params())(u, tiles, ar8, ai8, pw_r, pw_i)


def _s5_scan_bwd(dy, tiles, xs, ar8, ai8, pw_r, pw_i, name):
    n = dy.shape[0]
    rb = SCAN_ROWS
    nb, steps, lc = n // rb, rb // SEG, 256

    def body(dy_ref, t_ref, x_ref, ar_ref, ai_ref, pwr_ref, pwi_ref, lam_ref, da_ref, st_r, st_i, gx_ref):
        ph, b = pl.program_id(0), pl.program_id(1)

        @pl.when((ph == 0) & (b == 0))
        def _():
            st_r[...] = jnp.zeros_like(st_r)
            st_i[...] = jnp.zeros_like(st_i)
            da_ref[...] = jnp.zeros_like(da_ref)

        _expand_block(dy_ref, t_ref, gx_ref)

        def scan(store):
            for c in range(S5_GP // lc):
                re, im = slice(c * lc, (c + 1) * lc), slice(S5_GP + c * lc, S5_GP + (c + 1) * lc)
                a_r, a_i = ar_ref[:, re], ai_ref[:, re]

                def step(s, carry):
                    rows = pl.ds(pl.multiple_of((steps - 1 - s) * SEG, SEG), SEG)
                    if store:
                        lr, li, dr, di = carry
                        xr, xi = x_ref[rows, re], x_ref[rows, im]
                        dr = dr + lr * xr + li * xi
                        di = di + li * xr - lr * xi
                    else:
                        lr, li = carry
                    nr = a_r * lr + a_i * li + gx_ref[rows, re]
                    ni = a_r * li - a_i * lr + gx_ref[rows, im]
                    if store:
                        lam_ref[rows, re] = nr
                        lam_ref[rows, im] = ni
                        return nr, ni, dr, di
                    return nr, ni

                if store:
                    lr, li, dr, di = lax.fori_loop(0, steps, step, (st_r[:, re], st_i[:, re], da_ref[:, re], da_ref[:, im]),
                                                   unroll=4)
                    da_ref[:, re] = dr
                    da_ref[:, im] = di
                else:
                    lr, li = lax.fori_loop(0, steps, step, (st_r[:, re], st_i[:, re]), unroll=4)
                st_r[:, re] = lr
                st_i[:, re] = li

        @pl.when(ph == 0)
        def _():
            scan(False)

        @pl.when((ph == 0) & (b == nb - 1))
        def _():
            _chain_segments(st_r, st_i, pwr_ref, pwi_ref, conj=True, down=False)

        @pl.when(ph == 1)
        def _():
            scan(True)

    full = lambda a: pl.BlockSpec(a.shape, lambda ph, b: (0, 0))
    rev = lambda ph, b: (nb - 1 - b, 0)
    return _pcall(body, name=name, grid=(2, nb),
                  in_specs=[pl.BlockSpec((rb, S5_W), rev), pl.BlockSpec(tiles.shape, lambda ph, b: (0, 0, 0)),
                            pl.BlockSpec((rb, 2 * S5_GP), lambda ph, b: ((nb - 1 - b) * ph, 0)),
                            full(ar8), full(ai8), full(pw_r), full(pw_i)],
                  out_specs=[pl.BlockSpec((rb, 2 * S5_GP), lambda ph, b: (nb - 1 - b * ph, 0)),
                             pl.BlockSpec((SEG, 2 * S5_GP), lambda ph, b: (0, 0))],
                  out_shape=[jax.ShapeDtypeStruct((n, 2 * S5_GP), F32), jax.ShapeDtypeStruct((SEG, 2 * S5_GP), F32)],
                  scratch_shapes=[pltpu.VMEM((SEG, S5_GP), F32), pltpu.VMEM((SEG, S5_GP), F32),
                                  pltpu.VMEM((rb, 2 * S5_GP), F32)],
                  compiler_params=_params())(dy, tiles, xs, ar8, ai8, pw_r, pw_i)


def _s5_discretize(lam_re, lam_im, log_dt, b_re, b_im):
    dt = jnp.exp(log_dt)[:, None]
    mag = jnp.exp(lam_re * dt)
    ar = mag * jnp.cos(lam_im * dt)
    ai = mag * jnp.sin(lam_im * dt)
    den = lam_re * lam_re + lam_im * lam_im
    nr = ar - 1.0
    fr = (nr * lam_re + ai * lam_im) / den
    fi = (ai * lam_re - nr * lam_im) / den
    bbar_re = fr[:, :, None] * b_re - fi[:, :, None] * b_im
    bbar_im = fr[:, :, None] * b_im + fi[:, :, None] * b_re
    return ar, ai, bbar_re, bbar_im


BD_TILES, BD_CH, BD_ST, BD_GROUPS = 8, 128, 512, 8
BD_ROWS = 4096


def _bd_tiles(re, im):
    eye = jnp.eye(BD_GROUPS, dtype=re.dtype)

    def tiles(t):
        t = t.reshape(S5_G // BD_GROUPS, BD_GROUPS, S5_H, S5_P)
        return (t[:, :, :, None, :] * eye[None, :, None, :, None]).reshape(S5_G // BD_GROUPS, BD_CH, BD_ST)

    return jnp.concatenate([tiles(re), tiles(im)], axis=0)


def _bd_blocks(t):
    t = t.reshape(2, S5_G // BD_GROUPS, BD_GROUPS, S5_H, BD_GROUPS, S5_P)
    return jnp.einsum('rkahap->rkahp', t).reshape(2, S5_G, S5_H, S5_P)


def _bd_reduce(x, t, res, name):
    n = x.shape[0]
    tm = _pick(n, BD_ROWS, 16)

    def body(x_ref, t_ref, r_ref, o_ref):
        part = _dot(x_ref[...], t_ref[...], NT)

        @pl.when(pl.program_id(2) == 0)
        def _():
            o_ref[...] = r_ref[...] + part

        @pl.when(pl.program_id(2) == 1)
        def _():
            o_ref[...] += part

    return _pcall(body, name=name, grid=(n // tm, 4, 2),
                  in_specs=[pl.BlockSpec((tm, BD_ST), lambda i, k, r: (i, k + 4 * r)),
                            pl.BlockSpec((None, BD_CH, BD_ST), lambda i, k, r: (k + 4 * r, 0, 0)),
                            pl.BlockSpec((tm, BD_CH), lambda i, k, r: (i, k))],
                  out_specs=pl.BlockSpec((tm, BD_CH), lambda i, k, r: (i, k)),
                  out_shape=jax.ShapeDtypeStruct((n, S5_W), F32), compiler_params=_params())(x, t, res)


def _bd_outer(a, x, name):
    n = a.shape[0]
    tk = _pick(n, BD_ROWS, 16)
    nk = n // tk

    def body(a_ref, x_ref, o_ref):
        part = _dot(a_ref[...], x_ref[...], TN)

        @pl.when(pl.program_id(1) == 0)
        def _():
            o_ref[...] = part

        @pl.when(pl.program_id(1) > 0)
        def _():
            o_ref[...] += part

    return _pcall(body, name=name, grid=(BD_TILES, nk),
                  in_specs=[pl.BlockSpec((tk, BD_CH), lambda j, kk: (kk, j % 4)), pl.BlockSpec((tk, BD_ST), lambda j, kk: (kk, j))],
                  out_specs=pl.BlockSpec((None, BD_CH, BD_ST), lambda j, kk: (j, 0, 0)),
                  out_shape=jax.ShapeDtypeStruct((BD_TILES, BD_CH, BD_ST), F32), compiler_params=_params())(a, x)


def _permute_rows(t):
    n = t.shape[0]
    return t.reshape(SEG, n // SEG, t.shape[1]).transpose(1, 0, 2).reshape(n, t.shape[1])


def _unpermute_rows(t):
    n = t.shape[0]
    return t.reshape(n // SEG, SEG, t.shape[1]).transpose(1, 0, 2).reshape(n, t.shape[1])


def _segment_powers(ar, ai, seg_steps):
    pr, pi = ar.reshape(1, S5_GP), ai.reshape(1, S5_GP)
    e = 1
    while e < seg_steps:
        pr, pi = pr * pr - pi * pi, 2.0 * pr * pi
        e *= 2
    assert e == seg_steps, "segment length must be a power of two"
    rows_r, rows_i = [], []
    for _ in range(3):
        rows_r.append(pr)
        rows_i.append(pi)
        pr, pi = pr * pr - pi * pi, 2.0 * pr * pi
    pad = jnp.zeros((SEG - 3, S5_GP), F32)
    return jnp.concatenate(rows_r + [pad], axis=0), jnp.concatenate(rows_i + [pad], axis=0)


NT = (((1,), (1,)), ((), ()))
TN = (((0,), (0,)), ((), ()))


def _dot(a, b, dims=None, exact=False):
    dims = (((1,), (0,)), ((), ())) if dims is None else dims
    if exact:
        return lax.dot_general(a, b, dims, precision=HI, preferred_element_type=F32)
    return lax.dot_general(a.astype(BF16), b.astype(BF16), dims, preferred_element_type=F32)


def _dot01(a, b, dims=None, ones_first=True):
    x = b if ones_first else a
    hi = x.astype(BF16)
    r1 = x - hi.astype(F32)
    mid = r1.astype(BF16)
    lo = (r1 - mid.astype(F32)).astype(BF16)
    parts = [(_dot(a, p, dims) if ones_first else _dot(p, b, dims)) for p in (lo, mid, hi)]
    return (parts[0] + parts[1]) + parts[2]


HEADS = range(4)


def _gla_chunk_fwd(qc, kc, vc, al, wup, bup, s_prev, tril):
    ones = jnp.ones((GLA_CHUNK, GLA_DV), F32)
    z = [_dot(al, wup[h]) + bup[h] for h in HEADS]
    la = [(jnp.minimum(z[h], 0.0) - jnp.log(1.0 + jnp.exp(-jnp.abs(z[h])))) * (1.0 / GLA_TAU) for h in HEADS]
    bc = [_dot01(tril, la[h]) for h in HEADS]
    blb = [_dot01(la[h], ones, TN, ones_first=False) for h in HEADS]
    bl = [bc[h][GLA_CHUNK - 1:GLA_CHUNK, :] for h in HEADS]
    ebc = [jnp.exp(bc[h]) for h in HEADS]
    qt = [qc[h] * (GLA_DK ** -0.5) * ebc[h] for h in HEADS]
    kt = [kc[h] * jnp.exp(-bc[h]) for h in HEADS]
    ke = [kc[h] * jnp.exp(bl[h] - bc[h]) for h in HEADS]
    sc = [_dot(qt[h], kt[h], NT) * tril for h in HEADS]
    oi = [_dot(sc[h], vc[h]) for h in HEADS]
    oo = [_dot(qt[h], s_prev[h]) for h in HEADS]
    o = [oi[h] + oo[h] for h in HEADS]
    return z, bc, bl, blb, ebc, qt, kt, ke, sc, o


GLA_ROWS = 512
GLA_CPB = GLA_ROWS // GLA_CHUNK


ZA_COLS = 5 * 512
SLOT = 128


def _pad_heads(w):
    r = w.shape[0]
    return jnp.pad(w.reshape(r, GLA_HEADS, GLA_DK), ((0, 0), (0, 0), (0, SLOT - GLA_DK))).reshape(r, GLA_HEADS * SLOT)


def _unpad_heads(w):
    r = w.shape[0]
    return w.reshape(r, GLA_HEADS, SLOT)[:, :, :GLA_DK].reshape(r, GLA_HEADS * GLA_DK)


def _gla_token_specs(blk):
    col = lambda cb: pl.BlockSpec((GLA_ROWS, 512), lambda j: (blk(j), cb))
    whole = lambda a: pl.BlockSpec(a.shape, lambda j: (0,) * a.ndim)
    return col, whole


def _head_ds(h, width):
    return pl.ds(h * SLOT, width)


def _tri(lower):
    ri = lax.broadcasted_iota(jnp.int32, (GLA_CHUNK, GLA_CHUNK), 0)
    ci = lax.broadcasted_iota(jnp.int32, (GLA_CHUNK, GLA_CHUNK), 1)
    return ((ri >= ci) if lower else (ri <= ci)).astype(F32)


def _gla_fwd(za, al, wup, bup, gn, name):
    n = za.shape[0]
    nc = n // GLA_CHUNK

    def body(q_ref, k_ref, v_ref, r_ref, al_ref, wup_ref, bup_ref, gn_ref, y_ref, sp_ref, s_ref):
        @pl.when(pl.program_id(0) == 0)
        def _():
            s_ref[...] = jnp.zeros_like(s_ref)

        tril = _tri(True)

        def chunk(c, carry):
            rows = pl.ds(pl.multiple_of(c * GLA_CHUNK, GLA_CHUNK), GLA_CHUNK)
            alc = al_ref[rows, :]
            vc = [v_ref[rows, _head_ds(h, GLA_DV)] for h in HEADS]
            s_prev = [s_ref[h] for h in HEADS]
            _, _, _, blb, _, _, _, ke, _, o = _gla_chunk_fwd(
                [q_ref[rows, _head_ds(h, GLA_DK)] for h in HEADS], [k_ref[rows, _head_ds(h, GLA_DK)] for h in HEADS],
                vc, alc, [wup_ref[h] for h in HEADS], [bup_ref[h] for h in HEADS], s_prev, tril)
            ds = [_dot(ke[h], vc[h], TN) for h in HEADS]
            for h in HEADS:
                rc = r_ref[rows, _head_ds(h, GLA_DV)]
                sp_ref[h, c] = s_prev[h]
                rstd = lax.rsqrt(jnp.mean(o[h] * o[h], axis=-1, keepdims=True) + EPS)
                y_ref[rows, _head_ds(h, GLA_DV)] = (o[h] * rstd * gn_ref[h] * (rc * _sigmoid(rc))).astype(BF16)
                s_ref[h] = jnp.exp(blb[h]) * s_prev[h] + ds[h]
            return carry

        lax.fori_loop(0, GLA_CPB, chunk, 0)

    col, whole = _gla_token_specs(lambda j: j)
    return _pcall(body, name=name, grid=(n // GLA_ROWS,),
                  in_specs=[col(1), col(2), col(3), col(4), pl.BlockSpec((GLA_ROWS, LANE), lambda j: (j, 0)),
                            whole(wup), whole(bup), whole(gn)],
                  out_specs=[pl.BlockSpec((GLA_ROWS, GLA_HEADS * GLA_DV), lambda j: (j, 0)),
                             pl.BlockSpec((GLA_HEADS, GLA_CPB, GLA_DK, GLA_DV), lambda j: (0, j, 0, 0))],
                  out_shape=[jax.ShapeDtypeStruct((n, GLA_HEADS * GLA_DV), BF16),
                             jax.ShapeDtypeStruct((GLA_HEADS, nc, GLA_DK, GLA_DV), F32)],
                  scratch_shapes=[pltpu.VMEM((GLA_HEADS, GLA_DK, GLA_DV), F32)],
                  compiler_params=_params())(za, za, za, za, al, wup, bup, gn)


def _gla_bwd(za, al, wup, bup, gn, sp, dy, du_s5, name):
    n = za.shape[0]
    nb = n // GLA_ROWS

    def body(q_ref, k_ref, v_ref, r_ref, al_ref, wup_ref, bup_ref, gn_ref, dy_ref, dus_ref, sp_ref,
             dza_ref, dz_ref, dgn_ref, dbup_ref, ds_ref):
        @pl.when(pl.program_id(0) == 0)
        def _():
            ds_ref[...] = jnp.zeros_like(ds_ref)
            dgn_ref[...] = jnp.zeros_like(dgn_ref)
            dbup_ref[...] = jnp.zeros_like(dbup_ref)

        tril, triu = _tri(True), _tri(False)
        dza_ref[:, 0:512] = dus_ref[...]
        dza_ref[:, 512:1536] = jnp.zeros((GLA_ROWS, 1024), F32)
        dz_ref[...] = jnp.zeros_like(dz_ref)

        def chunk(i, carry):
            c = GLA_CPB - 1 - i
            rows = pl.ds(pl.multiple_of(c * GLA_CHUNK, GLA_CHUNK), GLA_CHUNK)
            alc = al_ref[rows, :]
            qc = [q_ref[rows, _head_ds(h, GLA_DK)] for h in HEADS]
            kc = [k_ref[rows, _head_ds(h, GLA_DK)] for h in HEADS]
            vc = [v_ref[rows, _head_ds(h, GLA_DV)] for h in HEADS]
            s_prev = [sp_ref[h, c] for h in HEADS]
            ds = [ds_ref[h] for h in HEADS]
            z, bc, bl, blb, ebc, qt, kt, ke, sc, o = _gla_chunk_fwd(
                qc, kc, vc, alc, [wup_ref[h] for h in HEADS], [bup_ref[h] for h in HEADS], s_prev, tril)
            do = []
            for h in HEADS:
                rc = r_ref[rows, _head_ds(h, GLA_DV)]
                rs = lax.rsqrt(jnp.mean(o[h] * o[h], axis=-1, keepdims=True) + EPS)
                on = o[h] * rs
                sr = _sigmoid(rc)
                sil = rc * sr
                dyv, gnv = dy_ref[rows, _head_ds(h, GLA_DV)], gn_ref[h]
                dgn_ref[h] += jnp.sum(dyv * on * sil, axis=0, keepdims=True)
                dza_ref[rows, pl.ds(2048 + h * SLOT, GLA_DV)] = dyv * on * gnv * (sr * (1.0 + rc * (1.0 - sr)))
                don = dyv * gnv * sil
                do.append(rs * (don - on * jnp.mean(don * on, axis=-1, keepdims=True)))
            dp = [_dot(do[h], vc[h], NT) * tril for h in HEADS]
            dv1 = [_dot(sc[h], do[h], TN) for h in HEADS]
            dv2 = [_dot(ke[h], ds[h]) for h in HEADS]
            dq2 = [_dot(do[h], s_prev[h], NT) for h in HEADS]
            dke = [_dot(vc[h], ds[h], NT) for h in HEADS]
            ddec = [_dot01(jnp.ones((8, GLA_DV), F32), ds[h] * s_prev[h], NT)[0:1, :] for h in HEADS]
            dsn = [_dot(qt[h], do[h], TN) for h in HEADS]
            dq1 = [_dot(dp[h], kt[h]) for h in HEADS]
            dkt = [_dot(dp[h], qt[h], TN) for h in HEADS]
            dbc, dbl = [], []
            for h in HEADS:
                dqt = dq1[h] + dq2[h]
                dza_ref[rows, pl.ds(1536 + h * SLOT, GLA_DV)] = dv1[h] + dv2[h]
                ds_ref[h] = jnp.exp(blb[h]) * ds[h] + dsn[h]
                dza_ref[rows, pl.ds(512 + h * SLOT, GLA_DK)] = dqt * (GLA_DK ** -0.5) * ebc[h]
                dza_ref[rows, pl.ds(1024 + h * SLOT, GLA_DK)] = dkt[h] * jnp.exp(-bc[h]) + dke[h] * jnp.exp(bl[h] - bc[h])
                dbc.append(dqt * qt[h] - dkt[h] * kt[h] - dke[h] * ke[h])
                dbl.append(jnp.sum(dke[h] * ke[h], axis=0, keepdims=True) + ddec[h] * jnp.exp(bl[h]))
            dla = [_dot01(triu, dbc[h]) + dbl[h] for h in HEADS]
            for h in HEADS:
                dz = dla[h] * (1.0 - _sigmoid(z[h])) * (1.0 / GLA_TAU)
                dz_ref[rows, _head_ds(h, GLA_DK)] = dz
                dbup_ref[h] += jnp.sum(dz, axis=0, keepdims=True)
            return carry

        lax.fori_loop(0, GLA_CPB, chunk, 0)

    rev = lambda j: nb - 1 - j
    col, whole = _gla_token_specs(rev)
    tok = lambda w: pl.BlockSpec((GLA_ROWS, w), lambda j: (rev(j), 0))
    h1 = lambda w: pl.BlockSpec((GLA_HEADS, 1, w), lambda j: (0, 0, 0))
    s1 = lambda w: jax.ShapeDtypeStruct((GLA_HEADS, 1, w), F32)
    return _pcall(body, name=name, grid=(nb,),
                  in_specs=[col(1), col(2), col(3), col(4), tok(LANE), whole(wup), whole(bup), whole(gn), tok(512), tok(512),
                            pl.BlockSpec((GLA_HEADS, GLA_CPB, GLA_DK, GLA_DV), lambda j: (0, rev(j), 0, 0))],
                  out_specs=[tok(ZA_COLS), tok(GLA_HEADS * SLOT), h1(GLA_DV), h1(GLA_DK)],
                  out_shape=[jax.ShapeDtypeStruct((n, ZA_COLS), F32), jax.ShapeDtypeStruct((n, GLA_HEADS * SLOT), F32),
                             s1(GLA_DV), s1(GLA_DK)],
                  scratch_shapes=[pltpu.VMEM((GLA_HEADS, GLA_DK, GLA_DV), F32)],
                  compiler_params=_params())(za, za, za, za, al, wup, bup, gn, dy, du_s5, sp)


ANY = pl.BlockSpec(memory_space=pl.ANY)


def _place():
    x, y, c = lax.axis_index("x"), lax.axis_index("y"), lax.axis_index("c")
    chips = [(1 - x, y), (x, 1 - y), (1 - x, 1 - y)]
    return x, y, c, chips


def _remote(src, dst, ssem, rsem, dev):
    return pltpu.make_async_remote_copy(src_ref=src, dst_ref=dst, send_sem=ssem, recv_sem=rsem, device_id=dev,
                                        device_id_type=MESH_ID)


def _half(c, rows):
    h = rows // 2
    return pl.ds(pl.multiple_of(c * h, 8), h)


def _side_gather_ici(shards):
    def copies(ins, outs, ssem, rsem):
        x, y, c, chips = _place()
        mine = 2 * x + y
        cps = []
        for w in range(len(ins)):
            half = _half(c, ins[w].shape[0])
            cps.append(_remote(ins[w], outs[w].at[mine], ssem.at[4 * w], rsem.at[4 * w], (x, y, 1 - c)))
            for k, (px, py) in enumerate(chips):
                cps.append(_remote(ins[w].at[half], outs[w].at[mine, half], ssem.at[4 * w + 1 + k], rsem.at[4 * w + 1 + k],
                                   (px, py, c)))
        return cps

    return _Side(shards, [jax.ShapeDtypeStruct((4,) + s.shape, s.dtype) for s in shards], 4 * len(shards), copies)


def _side_gather_d2d(gathered):
    def copies(ins, outs, ssem, rsem):
        x, y, c, chips = _place()
        cps = []
        for w in range(len(outs)):
            half = _half(c, outs[w].shape[1])
            for k, (px, py) in enumerate(chips):
                theirs = outs[w].at[2 * px + py, half]
                cps.append(_remote(theirs, theirs, ssem.at[3 * w + k], rsem.at[3 * w + k], (x, y, 1 - c)))
        return cps

    return _Side(gathered, [jax.ShapeDtypeStruct(g.shape, g.dtype) for g in gathered], 3 * len(gathered), copies,
                 aliased=True)


def _side_swap_halves(grads):
    def copies(ins, outs, ssem, rsem):
        x, y, c, _ = _place()
        return [_remote(ins[w].at[:, _half(1 - c, ins[w].shape[1]), :], outs[w], ssem.at[w], rsem.at[w], (x, y, 1 - c))
                for w in range(len(ins))]

    return _Side(grads, [jax.ShapeDtypeStruct((4, g.shape[1] // 2, g.shape[2]), g.dtype) for g in grads], len(grads), copies)


def _side_scatter(sums):
    def copies(ins, outs, ssem, rsem):
        x, y, c, chips = _place()
        return [_remote(ins[w].at[2 * px + py], outs[w].at[k], ssem.at[3 * w + k], rsem.at[3 * w + k], (px, py, c))
                for w in range(len(ins)) for k, (px, py) in enumerate(chips)]

    return _Side(sums, [jax.ShapeDtypeStruct((3,) + s.shape[1:], s.dtype) for s in sums], 3 * len(sums), copies)


def _side_swap_reduced(halves):
    def copies(ins, outs, ssem, rsem):
        x, y, c, _ = _place()
        return [_remote(ins[w], outs[w], ssem.at[w], rsem.at[w], (x, y, 1 - c)) for w in range(len(ins))]

    return _Side(halves, [jax.ShapeDtypeStruct(h.shape, h.dtype) for h in halves], len(halves), copies)


def _chip_sum(g, recv, c_arr, name):
    _, r, cols = g.shape
    h = r // 2
    tr = _pick(h, 512, 16)
    g4 = g.reshape(4, 2, h, cols)

    def body(c_ref, g_ref, r_ref, o_ref):
        o_ref[...] = (g_ref[...] + r_ref[...]).astype(BF16)

    grid_spec = pltpu.PrefetchScalarGridSpec(
        num_scalar_prefetch=1, grid=(4, h // tr),
        in_specs=[pl.BlockSpec((None, None, tr, cols), lambda s, i, c_ref: (s, c_ref[0], i, 0)),
                  pl.BlockSpec((None, tr, cols), lambda s, i, c_ref: (s, i, 0))],
        out_specs=pl.BlockSpec((None, tr, cols), lambda s, i, c_ref: (s, i, 0)))
    return _pcall(body, name=name, grid_spec=grid_spec, out_shape=jax.ShapeDtypeStruct((4, h, cols), BF16),
                  compiler_params=_params())(c_arr, g4, recv)


def _owner_sum(sums, others, s_arr, name):
    _, h, cols = sums.shape
    tr = _pick(h, 512, 16)

    def body(s_ref, a_ref, o_ref, out_ref):
        f = lambda v: v.astype(F32)
        out_ref[...] = (f(a_ref[...]) + f(o_ref[0])) + (f(o_ref[1]) + f(o_ref[2]))

    grid_spec = pltpu.PrefetchScalarGridSpec(
        num_scalar_prefetch=1, grid=(h // tr,),
        in_specs=[pl.BlockSpec((None, tr, cols), lambda i, s_ref: (s_ref[0], i, 0)),
                  pl.BlockSpec((3, tr, cols), lambda i, s_ref: (0, i, 0))],
        out_specs=pl.BlockSpec((tr, cols), lambda i, s_ref: (i, 0)))
    return _pcall(body, name=name, grid_spec=grid_spec, out_shape=jax.ShapeDtypeStruct((h, cols), F32),
                  compiler_params=_params())(s_arr, sums, others)


def _allreduce_small(v, name):
    def body(v_ref, o_ref, r0, r1, ssem, rsem):
        x, y, c, chips = _place()
        cp = _remote(v_ref, r0, ssem.at[0], rsem.at[0], (x, y, 1 - c))
        cp.start()
        cp.wait()
        o_ref[...] = v_ref[...] + r0[...]
        cps = []
        for k, (px, py) in enumerate(chips):
            cp = _remote(o_ref, r1.at[k], ssem.at[1 + k], rsem.at[1 + k], (px, py, c))
            cp.start()
            cps.append(cp)
        for cp in cps:
            cp.wait()
        o_ref[...] = (o_ref[...] + r1[0]) + (r1[1] + r1[2])

    vm = pl.BlockSpec(memory_space=pltpu.VMEM)
    return _pcall(body, name=name, in_specs=[vm], out_specs=vm, out_shape=jax.ShapeDtypeStruct(v.shape, F32),
                  scratch_shapes=[pltpu.VMEM(v.shape, F32), pltpu.VMEM((3,) + v.shape, F32),
                                  pltpu.SemaphoreType.DMA((4,)), pltpu.SemaphoreType.DMA((4,))],
                  compiler_params=_params(has_side_effects=True))(v)


def _tile_rows(size):
    return -(-size // (8 * LANE)) * 8


def _pack_small(parts):
    pieces = []
    for p in parts:
        flat = p.reshape(-1).astype(F32)
        pieces.append(jnp.pad(flat, (0, _tile_rows(p.size) * LANE - p.size)).reshape(-1, LANE))
    rows = sum(x.shape[0] for x in pieces)
    pieces.append(jnp.zeros(((-rows) % 64, LANE), F32))
    return jnp.concatenate(pieces, axis=0)


def _unpack_small(packed, like):
    out, pos = [], 0
    for p in like:
        rows = _tile_rows(p.size)
        out.append(packed[pos:pos + rows].reshape(-1)[:p.size].reshape(p.shape))
        pos += rows
    return out


FFN_FWD_ROWS, FFN_BWD_ROWS = 1024, 512
FFN_SUB_ROWS = 256


def _ffn_specs(n, d, fs, cap):
    rows = _pick(n, cap, 16)
    row = pl.BlockSpec((rows, d), lambda i, s: (i, 0))
    gain = pl.BlockSpec((1, d), lambda i, s: (0, 0))
    w_row = pl.BlockSpec((None, fs, d), lambda i, s: (s, 0, 0))
    hid = pl.BlockSpec((None, rows, fs), lambda i, s: (s, i, 0))
    return rows, row, gain, w_row, hid


def _ffn_fwd(h, g, w1t, w3t, w2, tag, plan):
    n, d = h.shape
    ns, fs, _ = w2.shape
    rows, row, gain, w_row, hid = _ffn_specs(n, d, fs, FFN_FWD_ROWS)
    sub = rows

    def body(h_ref, g_ref, w1_ref, w3_ref, w2_ref, out_ref, n1_ref, a_ref, b_ref, hm_ref, acc_ref):
        s = pl.program_id(1)

        @pl.when(s == 0)
        def _():
            xv = h_ref[...]
            rstd = lax.rsqrt(jnp.mean(xv * xv, axis=-1, keepdims=True) + EPS)
            n1_ref[...] = (xv * rstd * g_ref[...]).astype(BF16)
            acc_ref[...] = jnp.zeros_like(acc_ref)

        def up(j):
            n1 = n1_ref[j * sub:(j + 1) * sub, :]
            return _dot(n1, w1_ref[...], NT), _dot(n1, w3_ref[...], NT)

        cur = up(0)
        for j in range(rows // sub):
            nxt = up(j + 1) if (j + 1) * sub < rows else None
            a, b = cur
            r = slice(j * sub, (j + 1) * sub)
            hm = (a * _sigmoid(a) * b).astype(BF16)
            a_ref[r, :] = a.astype(BF16)
            b_ref[r, :] = b.astype(BF16)
            hm_ref[r, :] = hm
            acc_ref[r, :] += _dot(hm, w2_ref[...])
            cur = nxt

        @pl.when(s == ns - 1)
        def _():
            out_ref[...] = h_ref[...] + 0.5 * acc_ref[...]

    hid_shape = jax.ShapeDtypeStruct((ns, n, fs), BF16)
    plan.before(f"{tag}_fwd")
    out, n1, a, b, hm = _pcall(
        body, name=f"{tag}_fwd", grid=(n // rows, ns), in_specs=[row, gain, w_row, w_row, w_row],
        out_specs=[row, row, hid, hid, hid],
        out_shape=[jax.ShapeDtypeStruct((n, d), F32), jax.ShapeDtypeStruct((n, d), BF16), hid_shape, hid_shape, hid_shape],
        scratch_shapes=[pltpu.VMEM((rows, d), F32)], compiler_params=_params())(h, g, w1t, w3t, w2)
    plan.after(f"{tag}_fwd")
    return out, (h, n1, a, b, hm)


def _ffn_bwd(dout, saved, g, w1, w3, w2, tag, plan):
    h, n1, a, b, hm = saved
    n, d = h.shape
    ns, fs, _ = w2.shape
    rows, row, gain, w_row, hid = _ffn_specs(n, d, fs, FFN_BWD_ROWS)
    sub = _pick(rows, FFN_SUB_ROWS, 16)

    def body(do_ref, h_ref, g_ref, a_ref, b_ref, w1_ref, w3_ref, w2_ref, dh_ref, da_ref, db_ref, dg_ref, acc_ref):
        i, s = pl.program_id(0), pl.program_id(1)

        @pl.when(s == 0)
        def _():
            acc_ref[...] = jnp.zeros_like(acc_ref)

        @pl.when((s == 0) & (i == 0))
        def _():
            dg_ref[...] = jnp.zeros_like(dg_ref)

        def up(j):
            return _dot(0.5 * do_ref[j * sub:(j + 1) * sub, :], w2_ref[...], NT)

        cur = up(0)
        for j in range(rows // sub):
            nxt = up(j + 1) if (j + 1) * sub < rows else None
            r = slice(j * sub, (j + 1) * sub)
            av, bv = a_ref[r, :].astype(F32), b_ref[r, :].astype(F32)
            sg = _sigmoid(av)
            da = (cur * bv * (sg * (1.0 + av * (1.0 - sg)))).astype(BF16)
            db = (cur * av * sg).astype(BF16)
            da_ref[r, :] = da
            db_ref[r, :] = db
            acc_ref[r, :] += _dot(da, w1_ref[...]) + _dot(db, w3_ref[...])
            cur = nxt

        @pl.when(s == ns - 1)
        def _():
            xv, dn = h_ref[...], acc_ref[...]
            rstd = lax.rsqrt(jnp.mean(xv * xv, axis=-1, keepdims=True) + EPS)
            xh = xv * rstd
            dg_ref[...] += jnp.sum(dn * xh, axis=0, keepdims=True)
            dxh = dn * g_ref[...]
            dh_ref[...] = do_ref[...] + rstd * (dxh - xh * jnp.mean(dxh * xh, axis=-1, keepdims=True))

    hid_shape = jax.ShapeDtypeStruct((ns, n, fs), BF16)
    plan.before(f"{tag}_bwd")
    dh, da, db, dg = _pcall(
        body, name=f"{tag}_bwd", grid=(n // rows, ns), in_specs=[row, row, gain, hid, hid, w_row, w_row, w_row],
        out_specs=[row, hid, hid, gain],
        out_shape=[jax.ShapeDtypeStruct((n, d), F32), hid_shape, hid_shape, jax.ShapeDtypeStruct((1, d), F32)],
        scratch_shapes=[pltpu.VMEM((rows, d), F32)], compiler_params=_params())(dout, h, g, a, b, w1, w3, w2)
    plan.after(f"{tag}_bwd")
    plan.before(f"{tag}_gw2")
    gw2 = _mm(hm, dout, ta=True, shard='m', alpha=0.5, name=f"{tag}_gw2")
    plan.after(f"{tag}_gw2")
    plan.grads[f"{tag}_w2"] = gw2
    plan.before(f"{tag}_gw1")
    gw1 = _mm(da, n1, ta=True, shard='m', name=f"{tag}_gw1")
    plan.after(f"{tag}_gw1")
    plan.before(f"{tag}_gw3")
    gw3 = _mm(db, n1, ta=True, shard='m', name=f"{tag}_gw3")
    plan.after(f"{tag}_gw3")
    return dh, dg, gw1, gw3, gw2


def _local_step(x, tgt, plan):
    n = x.shape[0]
    grads = plan.grads

    def f(name):
        w = plan.get(name)
        return w.reshape(1, D_MODEL) if name.endswith('_norm') and name != 'gla_out_norm' else w

    def carried(tag, fn, *args, **kw):
        plan.before(tag)
        out = fn(*args, **kw)
        plan.after(tag)
        return out

    h1, ffn1 = _ffn_fwd(x, f('ffn1_norm'), f('ffn1_w1'), f('ffn1_w3'), f('ffn1_w2'), "ffn1", plan)
    u = carried("mix_rms", _rms_fwd, h1, f('mix_norm'), "mix_rms")
    w_in = f('w_in')
    w_a = jnp.concatenate([w_in[:, :512], _pad_heads(w_in[:, 512:768]), _pad_heads(w_in[:, 768:1024]), w_in[:, 1024:2048]],
                          axis=1)
    w_al = jnp.pad(w_in[:, 2048:2048 + GLA_RANK], ((0, 0), (0, LANE - GLA_RANK)))
    w_g = w_in[:, 2048 + GLA_RANK:]
    za = carried("in_a", _mm, u, w_a, name="in_a")
    zg = carried("in_g", _mm, u, w_g, name="in_g")
    al = _mm(u, w_al, name="in_al")
    ar, ai, bbar_re, bbar_im = _s5_discretize(f('s5_lambda_re'), f('s5_lambda_im'), f('s5_log_dt'), f('s5_b_re'), f('s5_b_im'))
    t_b = _bd_tiles(bbar_re.transpose(0, 2, 1), bbar_im.transpose(0, 2, 1)).astype(BF16)
    t_c = _bd_tiles(f('s5_c_re'), -f('s5_c_im')).astype(BF16)
    ar8 = jnp.broadcast_to(ar.reshape(1, S5_GP), (SEG, S5_GP))
    ai8 = jnp.broadcast_to(ai.reshape(1, S5_GP), (SEG, S5_GP))
    pw_r, pw_i = _segment_powers(ar, ai, n // SEG)
    dskip = f('s5_d').reshape(1, S5_W)
    u_s5 = _permute_rows(za[:, :S5_W])
    xs = _s5_scan(u_s5, t_b, ar8, ai8, pw_r, pw_i, "s5_scan")
    ys_p = _bd_reduce(xs, t_c, _scale_rows(u_s5, dskip, "s5_skip"), "s5_y")
    ys = _unpermute_rows(ys_p)
    zgelu = _gelu_fwd(ys, "s5_gelu")
    t_glu = _mm(zgelu, f('s5_glu_w'), bias=f('s5_glu_b').reshape(1, S5_W), name="s5_glu_t")
    y_s5 = _glu_fwd(zgelu, t_glu, "s5_glu")
    wup = jnp.pad(f('gla_a_up_w'), ((0, LANE - GLA_RANK), (0, 0)))
    wup_h = wup.reshape(LANE, GLA_HEADS, GLA_DK).transpose(1, 0, 2)
    bup_h = f('gla_a_up_b').reshape(GLA_HEADS, 1, GLA_DK)
    gn_h = f('gla_out_norm').reshape(GLA_HEADS, 1, GLA_DV)
    y_gla, s_prev = carried("gla_fwd", _gla_fwd, za, al, wup_h, bup_h, gn_h, "gla_fwd")
    ps = _mm(y_s5, f('proj_s5'), name="proj_s5")
    pg = carried("proj_gla", _mm, y_gla, f('proj_gla'), name="proj_gla")
    merged = _merge_fwd(zg, ps, pg, "merge")
    h2 = _mm(merged, f('w_out'), res=h1, name="w_out")
    h3, ffn2 = _ffn_fwd(h2, f('ffn2_norm'), f('ffn2_w1'), f('ffn2_w3'), f('ffn2_w2'), "ffn2", plan)
    loss, dh3, g_final = _final_loss(h3, f('final_norm').reshape(1, D_MODEL), tgt, "loss")
    grads['final_norm'] = g_final.reshape(D_MODEL)
    dh2, grads['ffn2_norm'], grads['ffn2_w1'], grads['ffn2_w3'], grads['ffn2_w2'] = _ffn_bwd(
        dh3, ffn2, f('ffn2_norm'), f('ffn2_w1'), f('ffn2_w3'), f('ffn2_w2'), "ffn2", plan)
    dm = _mm(dh2, f('w_out'), tb=True, name="d_merged")
    grads['w_out'] = _mm(merged, dh2, ta=True, name="g_w_out")
    dps, dpg, dzg = carried("d_merge", _merge_bwd, dm, zg, ps, pg, "d_merge")
    grads['proj_s5'] = _mm(y_s5, dps, ta=True, name="g_proj_s5")
    grads['proj_gla'] = _mm(y_gla, dpg, ta=True, name="g_proj_gla")
    dy_s5 = _mm(dps, f('proj_s5'), tb=True, name="d_y_s5")
    dy_gla = _mm(dpg, f('proj_gla'), tb=True, name="d_y_gla")
    dzgelu, dt_glu, g_glu_b = _glu_bwd1(dy_s5, zgelu, t_glu, "d_glu")
    grads['s5_glu_b'] = g_glu_b.reshape(S5_W)
    grads['s5_glu_w'] = _mm(zgelu, dt_glu, ta=True, name="g_glu_w")
    dzgelu = _mm(dt_glu, f('s5_glu_w'), tb=True, res=dzgelu, name="d_gelu")
    dys, du_skip, g_d = _glu_bwd2(_permute_rows(dzgelu), ys_p, u_s5, dskip, "d_s5_y")
    grads['s5_d'] = g_d.reshape(S5_G, S5_H)
    lam, da8 = _s5_scan_bwd(dys, t_c, xs, ar8, ai8, pw_r, pw_i, "s5_scan_bwd")
    g_c = _bd_blocks(_bd_outer(dys, xs, "g_s5_c"))
    grads['s5_c_re'], grads['s5_c_im'] = g_c[0], -g_c[1]
    g_b = _bd_blocks(_bd_outer(u_s5, lam, "g_s5_b")).transpose(0, 1, 3, 2)
    g_bbar_re, g_bbar_im = g_b[0], g_b[1]
    da = jnp.sum(da8, axis=0)
    g_ar, g_ai = da[:S5_GP].reshape(S5_G, S5_P), da[S5_GP:].reshape(S5_G, S5_P)
    _, disc_vjp = jax.vjp(_s5_discretize, f('s5_lambda_re'), f('s5_lambda_im'), f('s5_log_dt'), f('s5_b_re'), f('s5_b_im'))
    (grads['s5_lambda_re'], grads['s5_lambda_im'], grads['s5_log_dt'], grads['s5_b_re'],
     grads['s5_b_im']) = disc_vjp((g_ar, g_ai, g_bbar_re, g_bbar_im))
    du_s5 = _unpermute_rows(_bd_reduce(lam, t_b, du_skip, "d_s5_u"))
    dza, dz, dgn, dbup = carried("gla_bwd", _gla_bwd, za, al, wup_h, bup_h, gn_h, s_prev, dy_gla, du_s5, "gla_bwd")
    grads['gla_out_norm'] = dgn.reshape(GLA_HEADS * GLA_DV)
    grads['gla_a_up_b'] = dbup.reshape(GLA_HEADS * GLA_DK)
    grads['gla_a_up_w'] = _unpad_heads(_mm(al, dz, ta=True, name="g_a_up")[:GLA_RANK])
    dal = _mm(dz, _pad_heads(wup), tb=True, name="d_a_low")
    g_wa = _mm(u, dza, ta=True, name="g_in_a")
    g_wg = _mm(u, dzg, ta=True, name="g_in_g")
    g_wal = _mm(u, dal, ta=True, name="g_in_al")
    grads['w_in'] = jnp.concatenate([g_wa[:, :512], _unpad_heads(g_wa[:, 512:1024]), _unpad_heads(g_wa[:, 1024:1536]),
                                     g_wa[:, 1536:], g_wal[:, :GLA_RANK], g_wg], axis=1)
    du = carried("d_u_a", _mm, dza, w_a, tb=True, name="d_u_a")
    du = _mm(dzg, w_g, tb=True, res=du, name="d_u_g")
    du = _mm(dal, w_al, tb=True, res=du, name="d_u_al")
    dh1, g_mix = carried("d_mix_rms", _rms_bwd, h1, f('mix_norm'), du, dh2, "d_mix_rms")
    grads['mix_norm'] = g_mix
    dx, grads['ffn1_norm'], grads['ffn1_w1'], grads['ffn1_w3'], grads['ffn1_w2'] = _ffn_bwd(
        dh1, ffn1, f('ffn1_norm'), f('ffn1_w1'), f('ffn1_w3'), f('ffn1_w2'), "ffn1", plan)
    return loss[0, 0], dx


MIXER_WEIGHTS = ['w_in', 's5_glu_w', 'proj_s5', 'proj_gla', 'w_out', 'gla_a_up_w']
FFN1_WEIGHTS, FFN2_WEIGHTS = FFN_WEIGHTS[:3], FFN_WEIGHTS[3:]
TRANSPOSED = ['ffn1_w1', 'ffn1_w3', 'ffn2_w1', 'ffn2_w3']


def _local_shard(w, nm):
    return jnp.swapaxes(w, 1, 2)[0] if nm in TRANSPOSED else w[0]
FFN1_EARLY = ['ffn1_w2']
GRAD_GROUPS = {'ffn2': FFN2_WEIGHTS, 'mixer': ['w_out', 'proj_s5', 'proj_gla', 's5_glu_w', 'w_in'], 'ffn1': FFN1_WEIGHTS}


class _Plan:
    def __init__(self, a, c_arr, s_arr):
        self.a, self.c_arr, self.s_arr = a, c_arr, s_arr
        self.grads, self.weights, self.riding = {}, {}, {}
        self.g4s, self.chip_sums, self.halves, self.sib_halves = {}, {}, {}, {}
        for nm in SMALL:
            if nm != 'gla_a_up_w':
                self.weights[nm] = a[nm] if nm == 'final_norm' else a[nm][0]
        ici = _side_gather_ici(self._shards(FFN1_WEIGHTS))
        _run_side(ici, "gather_ffn1_ici")
        self._gathered(FFN1_WEIGHTS, _run_side(_side_gather_d2d(ici.outs), "gather_ffn1_d2d"))

    def _shards(self, names):
        return [_local_shard(self.a[nm], nm).astype(F32 if nm == 'gla_a_up_w' else BF16) for nm in names]

    def _gathered(self, names, arrs):
        for nm, g4 in zip(names, arrs):
            if nm in FFN_WEIGHTS:
                self.weights[nm] = g4
            elif nm in COL_SHARDED:
                self.weights[nm] = jnp.concatenate([g4[s] for s in range(4)], axis=1)
            else:
                self.weights[nm] = g4.reshape(4 * g4.shape[1], g4.shape[2])

    def get(self, name):
        return self.weights[name]

    def _shard_major(self, nm):
        g = self.grads[nm]
        if nm in FFN_WEIGHTS:
            return g
        if nm in COL_SHARDED:
            return jnp.stack(jnp.split(g, 4, axis=1))
        return g.reshape(4, g.shape[0] // 4, g.shape[1])

    def _schedule(self, tag):
        grp = GRAD_GROUPS
        gathers = {"ffn1_fwd": ('ici', MIXER_WEIGHTS), "mix_rms": ('d2d', MIXER_WEIGHTS),
                   "in_a": ('ici', FFN2_WEIGHTS[:1]), "in_g": ('d2d', FFN2_WEIGHTS[:1]),
                   "gla_fwd": ('ici', FFN2_WEIGHTS[1:]), "proj_gla": ('d2d', FFN2_WEIGHTS[1:])}
        if tag in gathers:
            kind, names = gathers[tag]
            key = tuple(names)
            if kind == 'ici':
                return _side_gather_ici(self._shards(names)), lambda outs: self.riding.update({key: outs})
            return _side_gather_d2d(self.riding[key]), lambda outs: self._gathered(names, outs)
        steps = {"d_merge": (grp['ffn2'], 0), "gla_bwd": (grp['ffn2'], 1), "d_mix_rms": (grp['ffn2'], 2),
                 "d_u_a": (grp['mixer'], 0), "ffn1_bwd": (grp['mixer'], 1), "ffn1_gw2": (grp['mixer'], 2),
                 "ffn1_gw1": (FFN1_EARLY, 0), "ffn1_gw3": (FFN1_EARLY, 1)}
        if tag in steps:
            return self._reduce_stage(*steps[tag])
        return None

    def _reduce_stage(self, names, stage):
        if stage == 0:
            for nm in names:
                self.g4s[nm] = self._shard_major(nm)

            def done(outs):
                for nm, r in zip(names, outs):
                    self.chip_sums[nm] = _chip_sum(self.g4s[nm], r, self.c_arr, f"chip_sum_{nm}")
            return _side_swap_halves([self.g4s[nm] for nm in names]), done
        if stage == 1:
            def done(outs):
                for nm, o in zip(names, outs):
                    self.halves[nm] = _owner_sum(self.chip_sums[nm], o, self.s_arr, f"owner_sum_{nm}")
            return _side_scatter([self.chip_sums[nm] for nm in names]), done

        def done(outs):
            self.sib_halves.update(zip(names, outs))
        return _side_swap_reduced([self.halves[nm] for nm in names]), done

    def before(self, tag):
        entry = self._schedule(tag)
        if entry is not None:
            side, done = entry
            self.riding[tag] = (side, done)
            _RIDER.append(side)

    def after(self, tag):
        if tag in self.riding:
            side, done = self.riding.pop(tag)
            assert not _RIDER and side.outs is not None, tag
            done(side.outs)

    def finish(self):
        late = [nm for nm in GRAD_GROUPS['ffn1'] if nm not in FFN1_EARLY]
        for stage, names in ((0, late), (1, late), (2, GRAD_GROUPS['ffn1'])):
            side, done = self._reduce_stage(names, stage)
            done(_run_side(side, f"grad_ffn1_stage{stage}"))


def _train_step(a):
    x = a['x'][0]
    tgt = a['loss_target'][0]
    xi, yi, ci = lax.axis_index("x"), lax.axis_index("y"), lax.axis_index("c")
    c_arr = jnp.reshape(ci, (1,)).astype(jnp.int32)
    s_arr = jnp.reshape(2 * xi + yi, (1,)).astype(jnp.int32)
    plan = _Plan(a, c_arr, s_arr)
    loss, dx = _local_step(x, tgt, plan)
    plan.finish()
    grads = plan.grads
    loss = lax.psum(loss, ("x", "y", "c"))
    halves = [plan.halves[nm] for nm in SHARDED]
    sib_halves = [plan.sib_halves[nm] for nm in SHARDED]
    red = {}
    small_parts = [grads[nm].reshape(a[nm].shape) for nm in SMALL if nm != 'gla_a_up_w'] + [grads['gla_a_up_w']]
    small_sum = _unpack_small(_allreduce_small(_pack_small(small_parts), "allreduce_small"), small_parts)
    small_names = [nm for nm in SMALL if nm != 'gla_a_up_w']
    for nm, g in zip(small_names, small_sum[:-1]):
        red[nm] = g
    g_up = small_sum[-1]
    red['gla_a_up_w'] = lax.dynamic_slice(g_up, (0, (2 * xi + yi) * GLA_DK), (GLA_RANK, GLA_DK))
    out_g, out_d, out_m, out_v = {}, {}, {}, {}
    for nm, own, sib in zip(SHARDED, halves, sib_halves):
        loc = lambda pre: _local_shard(a[pre + nm], nm)
        res = _adamw_halves(loc(''), own, sib, loc('m_'), loc('v_'), c_arr, f"adamw_{nm}")
        back = (lambda t: jnp.swapaxes(t[None], 1, 2)) if nm in TRANSPOSED else (lambda t: t[None])
        out_g[nm], out_d[nm], out_m[nm], out_v[nm] = (back(t) for t in res)
    rest = [nm for nm in WEIGHTS if nm not in SHARDED]
    pk = lambda pre: _pack_small([a[pre + nm] for nm in rest])
    d, nm_, nv_ = _adamw(pk(''), _pack_small([red[nm] for nm in rest]), pk('m_'), pk('v_'), "adamw_small")
    like = [a[nm] for nm in rest]
    for nm, g, dd, mm_, vv_ in zip(rest, [red[nm].reshape(a[nm].shape) for nm in rest], _unpack_small(d, like),
                                   _unpack_small(nm_, like), _unpack_small(nv_, like)):
        out_g[nm], out_d[nm], out_m[nm], out_v[nm] = g, dd, mm_, vv_
    return (loss, dx[None], *[out_g[nm] for nm in WEIGHTS], *[out_d[nm] for nm in WEIGHTS],
            *[out_m[nm] for nm in WEIGHTS], *[out_v[nm] for nm in WEIGHTS])


def kernel(x, ffn1_norm, ffn1_w1, ffn1_w3, ffn1_w2, mix_norm, w_in, s5_lambda_re, s5_lambda_im, s5_log_dt, s5_b_re, s5_b_im, s5_c_re, s5_c_im, s5_d, s5_glu_w, s5_glu_b, gla_a_up_w, gla_a_up_b, gla_out_norm, proj_s5, proj_gla, w_out, ffn2_norm, ffn2_w1, ffn2_w3, ffn2_w2, final_norm, loss_target, m_ffn1_norm, m_ffn1_w1, m_ffn1_w3, m_ffn1_w2, m_mix_norm, m_w_in, m_s5_lambda_re, m_s5_lambda_im, m_s5_log_dt, m_s5_b_re, m_s5_b_im, m_s5_c_re, m_s5_c_im, m_s5_d, m_s5_glu_w, m_s5_glu_b, m_gla_a_up_w, m_gla_a_up_b, m_gla_out_norm, m_proj_s5, m_proj_gla, m_w_out, m_ffn2_norm, m_ffn2_w1, m_ffn2_w3, m_ffn2_w2, m_final_norm, v_ffn1_norm, v_ffn1_w1, v_ffn1_w3, v_ffn1_w2, v_mix_norm, v_w_in, v_s5_lambda_re, v_s5_lambda_im, v_s5_log_dt, v_s5_b_re, v_s5_b_im, v_s5_c_re, v_s5_c_im, v_s5_d, v_s5_glu_w, v_s5_glu_b, v_gla_a_up_w, v_gla_a_up_b, v_gla_out_norm, v_proj_s5, v_proj_gla, v_w_out, v_ffn2_norm, v_ffn2_w1, v_ffn2_w3, v_ffn2_w2, v_final_norm):
    return _train_step(dict(locals()))
```

```python
import functools

import jax
import jax.numpy as jnp
from jax import lax
from jax.experimental import pallas as pl
from jax.experimental.pallas import tpu as pltpu

F32 = jnp.float32
BF16 = jnp.bfloat16
HI = lax.Precision.HIGHEST
MESH_ID = pl.DeviceIdType.MESH

D_MODEL = 1024
EPS = 1e-6
S5_G, S5_P, S5_H = 32, 64, 16
S5_W = S5_G * S5_H
S5_GP = S5_G * S5_P
SEG = 8
SCAN_ROWS = 256
GLA_HEADS, GLA_DK, GLA_DV = 4, 64, 128
GLA_CHUNK = 64
GLA_TAU = 16.0
GLA_RANK = 16
ADAM_LR, ADAM_B1, ADAM_B2, ADAM_EPS, ADAM_WD, ADAM_STEP = 0.001, 0.9, 0.999, 1e-08, 0.01, 10
V7X_VMEM_LIMIT = 56 * 1024 * 1024
LANE = 128

WEIGHTS = ['ffn1_norm', 'ffn1_w1', 'ffn1_w3', 'ffn1_w2', 'mix_norm', 'w_in', 's5_lambda_re', 's5_lambda_im',
           's5_log_dt', 's5_b_re', 's5_b_im', 's5_c_re', 's5_c_im', 's5_d', 's5_glu_w', 's5_glu_b', 'gla_a_up_w',
           'gla_a_up_b', 'gla_out_norm', 'proj_s5', 'proj_gla', 'w_out', 'ffn2_norm', 'ffn2_w1', 'ffn2_w3',
           'ffn2_w2', 'final_norm']
SHARDED = ['ffn1_w1', 'ffn1_w3', 'ffn1_w2', 'w_in', 's5_glu_w', 'proj_s5', 'proj_gla', 'w_out',
           'ffn2_w1', 'ffn2_w3', 'ffn2_w2']
COL_SHARDED = ['ffn1_w1', 'ffn1_w3', 'w_in', 'proj_s5', 'proj_gla', 'ffn2_w1', 'ffn2_w3', 'gla_a_up_w']
SMALL = [n for n in WEIGHTS if n not in SHARDED]
FFN_WEIGHTS = ['ffn1_w1', 'ffn1_w3', 'ffn1_w2', 'ffn2_w1', 'ffn2_w3', 'ffn2_w2']


def _params(**kw):
    return pltpu.CompilerParams(vmem_limit_bytes=V7X_VMEM_LIMIT, **kw)


class _Side:
    def __init__(self, ins, out_shapes, nsem, copies, aliased=False):
        self.ins, self.out_shapes, self.nsem, self.copies, self.aliased = list(ins), list(out_shapes), nsem, copies, aliased
        self.outs = None


_RIDER = []


def _pcall(body, **kw):
    if _RIDER:
        return _carry(body, _RIDER.pop(), **kw)
    return pl.pallas_call(body, **kw)


def _carry(body, side, *, name, grid, in_specs, out_specs, out_shape, scratch_shapes=(), compiler_params=None):
    del compiler_params
    single = not isinstance(out_shape, (list, tuple))
    out_specs = [out_specs] if single else list(out_specs)
    out_shape = [out_shape] if single else list(out_shape)
    n_in, n_out, n_scr = len(in_specs), len(out_shape), len(scratch_shapes)
    s_in, s_out = len(side.ins), len(side.out_shapes)
    any_spec = pl.BlockSpec(memory_space=pl.ANY)

    def wrapped(*refs):
        cuts = [n_in, s_in, n_out, s_out, n_scr]
        parts, pos = [], 0
        for c in cuts:
            parts.append(refs[pos:pos + c])
            pos += c
        ins, sins, outs, souts, scr = parts
        ssem, rsem = refs[pos], refs[pos + 1]
        first = last = None
        for d, g in enumerate(grid):
            i = pl.program_id(d)
            first = (i == 0) if first is None else first & (i == 0)
            last = (i == g - 1) if last is None else last & (i == g - 1)

        @pl.when(first)
        def _():
            for cp in side.copies(sins, souts, ssem, rsem):
                cp.start()

        body(*ins, *outs, *scr)

        @pl.when(last)
        def _():
            for cp in side.copies(sins, souts, ssem, rsem):
                cp.wait()

    call = pl.pallas_call(
        wrapped, name=name, grid=grid, in_specs=list(in_specs) + [any_spec] * s_in,
        out_specs=out_specs + [any_spec] * s_out, out_shape=out_shape + side.out_shapes,
        scratch_shapes=list(scratch_shapes) + [pltpu.SemaphoreType.DMA((side.nsem,)), pltpu.SemaphoreType.DMA((side.nsem,))],
        input_output_aliases={n_in + j: n_out + j for j in range(s_in)} if side.aliased else {},
        compiler_params=_params(has_side_effects=True))

    def run(*args):
        res = call(*args, *side.ins)
        side.outs = list(res[n_out:])
        return res[0] if single else list(res[:n_out])

    return run


def _run_side(side, name):
    s_in, s_out = len(side.ins), len(side.out_shapes)
    any_spec = pl.BlockSpec(memory_space=pl.ANY)

    def body(*refs):
        sins, souts = refs[:s_in], refs[s_in:s_in + s_out]
        ssem, rsem = refs[s_in + s_out:]
        cps = side.copies(sins, souts, ssem, rsem)
        for cp in cps:
            cp.start()
        for cp in cps:
            cp.wait()

    side.outs = list(pl.pallas_call(
        body, name=name, in_specs=[any_spec] * s_in, out_specs=[any_spec] * s_out, out_shape=side.out_shapes,
        scratch_shapes=[pltpu.SemaphoreType.DMA((side.nsem,)), pltpu.SemaphoreType.DMA((side.nsem,))],
        input_output_aliases={j: j for j in range(s_in)} if side.aliased else {},
        compiler_params=pltpu.CompilerParams(has_side_effects=True))(*side.ins))
    return side.outs


def _pick(n, cap, quantum):
    if n <= cap:
        return n
    best = None
    for t in range(quantum, cap + 1, quantum):
        if n % t == 0:
            best = t
    assert best is not None, (n, cap, quantum)
    return best


def _sigmoid(x):
    return jax.nn.sigmoid(x)


def _mm(a, b, *, name, ta=False, tb=False, out_dtype=F32, alpha=1.0, res=None, bias=None, exact=False, shard=None):
    ns = 4
    (k_a, m) = a.shape[-2:] if ta else a.shape[-2:][::-1]
    (k_b, n) = b.shape[-2:][::-1] if tb else b.shape[-2:]
    assert k_a == k_b, (a.shape, b.shape, ta, tb)
    assert (a.ndim == 3) == (shard in ('k', 'm')) and (b.ndim == 3) == (shard in ('n', 'k'))
    k = k_a
    tm = _pick(m, 1024, 128)
    tn = _pick(n, 1024, 128)
    tk = _pick(k, 1024, 128)
    pm, pn, pk = m // tm, n // tn, k // tk
    gm = pm * (ns if shard == 'm' else 1)
    gn = pn * (ns if shard == 'n' else 1)
    gk = pk * (ns if shard == 'k' else 1)
    dims = (((0,) if ta else (1,), (1,) if tb else (0,)), ((), ()))
    op_dtype = F32 if exact else BF16

    def body(*refs):
        a_ref, b_ref = refs[0], refs[1]
        pos = 2
        res_ref = bias_ref = None
        if res is not None:
            res_ref = refs[pos]
            pos += 1
        if bias is not None:
            bias_ref = refs[pos]
            pos += 1
        o_ref, acc_ref = refs[pos], refs[pos + 1]
        kk = pl.program_id(2)

        @pl.when(kk == 0)
        def _():
            acc_ref[...] = jnp.zeros_like(acc_ref)

        acc_ref[...] += lax.dot_general(a_ref[...].astype(op_dtype), b_ref[...].astype(op_dtype), dims,
                                        precision=HI if exact else None, preferred_element_type=F32)

        @pl.when(kk == gk - 1)
        def _():
            o = acc_ref[...]
            if alpha != 1.0:
                o = o * alpha
            if bias_ref is not None:
                o = o + bias_ref[...]
            if res_ref is not None:
                o = o + res_ref[...]
            o_ref[...] = o.astype(out_dtype)

    def spec(block, sharded_on, order):
        per = {'m': pm, 'n': pn, 'k': pk}

        def index(i, j, kk):
            g = {'m': i, 'n': j, 'k': kk}
            r, c = order(i % pm if shard == 'm' else i, j % pn if shard == 'n' else j, kk % pk if shard == 'k' else kk)
            if sharded_on is None:
                return (r, c)
            return (g[sharded_on] // per[sharded_on], r, c)

        return pl.BlockSpec(block if sharded_on is None else (None,) + block, index)

    a_sh = shard if shard in ('k', 'm') else None
    b_sh = shard if shard in ('n', 'k') else None
    o_sh = shard if shard in ('n', 'm') else None
    a_spec = spec((tk, tm), a_sh, lambda i, j, kk: (kk, i)) if ta else spec((tm, tk), a_sh, lambda i, j, kk: (i, kk))
    b_spec = spec((tn, tk), b_sh, lambda i, j, kk: (j, kk)) if tb else spec((tk, tn), b_sh, lambda i, j, kk: (kk, j))
    ins, in_specs = [a, b], [a_spec, b_spec]
    if res is not None:
        assert o_sh is None
        ins.append(res)
        in_specs.append(pl.BlockSpec((tm, tn), lambda i, j, kk: (i, j)))
    if bias is not None:
        assert o_sh is None
        ins.append(bias)
        in_specs.append(pl.BlockSpec((1, tn), lambda i, j, kk: (0, j)))
    out_shape = (m, n) if o_sh is None else (ns, m, n)
    return _pcall(body, name=name, grid=(gm, gn, gk), in_specs=in_specs,
                  out_specs=spec((tm, tn), o_sh, lambda i, j, kk: (i, j)),
                  out_shape=jax.ShapeDtypeStruct(out_shape, out_dtype),
                  scratch_shapes=[pltpu.VMEM((tm, tn), F32)], compiler_params=_params())(*ins)


ROWS_VMEM_BUDGET = 24 * 1024 * 1024


def _rows(body, ins, outs, *, n, name):
    cols = sum(a.shape[1] for a, kind in ins if kind == 'r') + sum(c for c, _, kind in outs if kind == 'r')
    cap = 256
    while cap < 2048 and 2 * 4 * cols * (2 * cap) <= ROWS_VMEM_BUDGET:
        cap *= 2
    tm = _pick(n, cap, 16)
    in_specs = []
    for arr, kind in ins:
        if kind == 'r':
            in_specs.append(pl.BlockSpec((tm, arr.shape[1]), lambda i: (i, 0)))
        else:
            in_specs.append(pl.BlockSpec(arr.shape, lambda i: (0, 0)))
    out_specs, out_shape = [], []
    for cols, dtype, kind in outs:
        if kind == 'r':
            out_specs.append(pl.BlockSpec((tm, cols), lambda i: (i, 0)))
            out_shape.append(jax.ShapeDtypeStruct((n, cols), dtype))
        else:
            out_specs.append(pl.BlockSpec((1, cols), lambda i: (0, 0)))
            out_shape.append(jax.ShapeDtypeStruct((1, cols), dtype))
    n_in = len(ins)
    acc_ids = [j for j, o in enumerate(outs) if o[2] == 'a']

    def wrapped(*refs):
        if acc_ids:
            @pl.when(pl.program_id(0) == 0)
            def _():
                for j in acc_ids:
                    refs[n_in + j][...] = jnp.zeros_like(refs[n_in + j])
        body(*refs)

    res = _pcall(wrapped, name=name, grid=(n // tm,), in_specs=in_specs, out_specs=out_specs, out_shape=out_shape,
                 compiler_params=_params())(*[a for a, _ in ins])
    return res


def _rms_fwd(x, g, name):
    def body(x_ref, g_ref, o_ref):
        xv = x_ref[...]
        rstd = lax.rsqrt(jnp.mean(xv * xv, axis=-1, keepdims=True) + EPS)
        o_ref[...] = (xv * rstd * g_ref[...]).astype(BF16)
    return _rows(body, [(x, 'r'), (g, 'f')], [(x.shape[1], BF16, 'r')], n=x.shape[0], name=name)[0]


def _rms_bwd(x, g, dn, dres, name):
    def body(x_ref, g_ref, dn_ref, dres_ref, dx_ref, dg_ref):
        xv = x_ref[...]
        rstd = lax.rsqrt(jnp.mean(xv * xv, axis=-1, keepdims=True) + EPS)
        xh = xv * rstd
        dn = dn_ref[...]
        dg_ref[...] += jnp.sum(dn * xh, axis=0, keepdims=True)
        dxh = dn * g_ref[...]
        dx_ref[...] = dres_ref[...] + rstd * (dxh - xh * jnp.mean(dxh * xh, axis=-1, keepdims=True))
    d = x.shape[1]
    return _rows(body, [(x, 'r'), (g, 'f'), (dn, 'r'), (dres, 'r')], [(d, F32, 'r'), (d, F32, 'a')],
                 n=x.shape[0], name=name)


def _gelu_parts(y):
    c0 = 0.7978845608028654
    inner = c0 * (y + 0.044715 * y * y * y)
    th = jnp.tanh(inner)
    return th, c0 * (1.0 + 3.0 * 0.044715 * y * y)


def _gelu_fwd(y, name):
    def body(y_ref, o_ref):
        yv = y_ref[...]
        th, _ = _gelu_parts(yv)
        o_ref[...] = 0.5 * yv * (1.0 + th)
    return _rows(body, [(y, 'r')], [(y.shape[1], F32, 'r')], n=y.shape[0], name=name)[0]


def _glu_fwd(zg, t, name):
    def body(z_ref, t_ref, o_ref):
        o_ref[...] = (z_ref[...] * _sigmoid(t_ref[...])).astype(BF16)
    return _rows(body, [(zg, 'r'), (t, 'r')], [(zg.shape[1], BF16, 'r')], n=zg.shape[0], name=name)[0]


def _glu_bwd1(dy, zg, t, name):
    def body(dy_ref, z_ref, t_ref, dz_ref, dt_ref, db_ref):
        dyv, zv = dy_ref[...], z_ref[...]
        sg = _sigmoid(t_ref[...])
        dz_ref[...] = dyv * sg
        dt = dyv * zv * sg * (1.0 - sg)
        dt_ref[...] = dt.astype(BF16)
        db_ref[...] += jnp.sum(dt, axis=0, keepdims=True)
    w = zg.shape[1]
    return _rows(body, [(dy, 'r'), (zg, 'r'), (t, 'r')], [(w, F32, 'r'), (w, BF16, 'r'), (w, F32, 'a')],
                 n=zg.shape[0], name=name)


def _glu_bwd2(dzg, ys, u, dskip, name):
    def body(dz_ref, y_ref, u_ref, d_ref, dy_ref, du_ref, dd_ref):
        yv = y_ref[...]
        th, dinner = _gelu_parts(yv)
        dy = dz_ref[...] * (0.5 * (1.0 + th) + 0.5 * yv * (1.0 - th * th) * dinner)
        dy_ref[...] = dy
        du_ref[...] = dy * d_ref[...]
        dd_ref[...] += jnp.sum(dy * u_ref[...], axis=0, keepdims=True)
    w = ys.shape[1]
    return _rows(body, [(dzg, 'r'), (ys, 'r'), (u, 'r'), (dskip, 'f')], [(w, F32, 'r'), (w, F32, 'r'), (w, F32, 'a')],
                 n=ys.shape[0], name=name)


def _scale_rows(u, dskip, name):
    def body(u_ref, d_ref, o_ref):
        o_ref[...] = u_ref[...] * d_ref[...]
    return _rows(body, [(u, 'r'), (dskip, 'f')], [(u.shape[1], F32, 'r')], n=u.shape[0], name=name)[0]


def _merge_fwd(zg, ps, pg, name):
    def body(z_ref, ps_ref, pg_ref, o_ref):
        zv = z_ref[...]
        o_ref[...] = (_sigmoid(zv[:, :D_MODEL]) * ps_ref[...] + _sigmoid(zv[:, D_MODEL:]) * pg_ref[...]).astype(BF16)
    return _rows(body, [(zg, 'r'), (ps, 'r'), (pg, 'r')], [(D_MODEL, BF16, 'r')], n=zg.shape[0], name=name)[0]


def _merge_bwd(dm, zg, ps, pg, name):
    def body(dm_ref, z_ref, ps_ref, pg_ref, dps_ref, dpg_ref, dz_ref):
        dmv, zv = dm_ref[...], z_ref[...]
        s1, s2 = _sigmoid(zv[:, :D_MODEL]), _sigmoid(zv[:, D_MODEL:])
        dps_ref[...] = (dmv * s1).astype(BF16)
        dpg_ref[...] = (dmv * s2).astype(BF16)
        dz_ref[:, :D_MODEL] = dmv * ps_ref[...] * s1 * (1.0 - s1)
        dz_ref[:, D_MODEL:] = dmv * pg_ref[...] * s2 * (1.0 - s2)
    return _rows(body, [(dm, 'r'), (zg, 'r'), (ps, 'r'), (pg, 'r')],
                 [(D_MODEL, BF16, 'r'), (D_MODEL, BF16, 'r'), (2 * D_MODEL, F32, 'r')], n=zg.shape[0], name=name)


def _final_loss(h, g, tgt, name):
    def body(h_ref, g_ref, t_ref, loss_ref, dh_ref, dg_ref):
        hv = h_ref[...]
        rstd = lax.rsqrt(jnp.mean(hv * hv, axis=-1, keepdims=True) + EPS)
        xh = hv * rstd
        err = xh * g_ref[...] - t_ref[...]
        part = 0.5 * jnp.sum(jnp.mean(err * err, axis=-1, keepdims=True), axis=0, keepdims=True)
        loss_ref[...] += jnp.broadcast_to(part, loss_ref.shape)
        dout = err * (1.0 / hv.shape[1])
        dg_ref[...] += jnp.sum(dout * xh, axis=0, keepdims=True)
        dxh = dout * g_ref[...]
        dh_ref[...] = rstd * (dxh - xh * jnp.mean(dxh * xh, axis=-1, keepdims=True))
    d = h.shape[1]
    return _rows(body, [(h, 'r'), (g, 'f'), (tgt, 'r')], [(LANE, F32, 'a'), (d, F32, 'r'), (d, F32, 'a')],
                 n=h.shape[0], name=name)


def _adamw_math(wv, gv, mv, vv):
    nm = ADAM_B1 * mv + (1.0 - ADAM_B1) * gv
    nv = ADAM_B2 * vv + (1.0 - ADAM_B2) * (gv * gv)
    m_hat = nm / (1.0 - ADAM_B1 ** ADAM_STEP)
    v_hat = nv / (1.0 - ADAM_B2 ** ADAM_STEP)
    return -ADAM_LR * (m_hat / (jnp.sqrt(v_hat) + ADAM_EPS) + ADAM_WD * wv), nm, nv


def _adamw(w, g, m, v, name):
    def body(w_ref, g_ref, m_ref, v_ref, d_ref, nm_ref, nv_ref):
        d_ref[...], nm_ref[...], nv_ref[...] = _adamw_math(w_ref[...], g_ref[...], m_ref[...], v_ref[...])
    c = w.shape[1]
    return _rows(body, [(w, 'r'), (g, 'r'), (m, 'r'), (v, 'r')], [(c, F32, 'r')] * 3, n=w.shape[0], name=name)


def _adamw_halves(w, g_own, g_sib, m, v, c_arr, name):
    r, cols = w.shape
    h = r // 2
    tr = _pick(h, 512, 8)
    per = h // tr

    def body(c_ref, w_ref, go_ref, gs_ref, m_ref, v_ref, g_ref, d_ref, nm_ref, nv_ref):
        mine = (pl.program_id(0) // per) == c_ref[0]
        gv = jnp.where(mine, go_ref[...], gs_ref[...])
        g_ref[...] = gv
        d_ref[...], nm_ref[...], nv_ref[...] = _adamw_math(w_ref[...], gv, m_ref[...], v_ref[...])

    full = pl.BlockSpec((tr, cols), lambda i, c_ref: (i, 0))
    half = pl.BlockSpec((tr, cols), lambda i, c_ref: (i % per, 0))
    grid_spec = pltpu.PrefetchScalarGridSpec(num_scalar_prefetch=1, grid=(2 * per,),
                                             in_specs=[full, half, half, full, full], out_specs=[full] * 4)
    return _pcall(body, name=name, grid_spec=grid_spec, out_shape=[jax.ShapeDtypeStruct((r, cols), F32)] * 4,
                  compiler_params=_params())(c_arr, w, g_own, g_sib, m, v)


def _shift_rows(v, sh, down):
    rolled = pltpu.roll(v, sh if down else v.shape[0] - sh, axis=0)
    row = lax.broadcasted_iota(jnp.int32, v.shape, 0)
    keep = (row >= sh) if down else (row < v.shape[0] - sh)
    return jnp.where(keep, rolled, 0.0)


def _chain_segments(st_r, st_i, pw_r_ref, pw_i_ref, conj, down):
    vr, vi = st_r[...], st_i[...]
    sh, k = 1, 0
    while sh < SEG:
        pr, pi = pw_r_ref[k:k + 1, :], pw_i_ref[k:k + 1, :]
        if conj:
            pi = -pi
        sr, si = _shift_rows(vr, sh, down), _shift_rows(vi, sh, down)
        vr, vi = vr + pr * sr - pi * si, vi + pr * si + pi * sr
        sh, k = sh * 2, k + 1
    st_r[...] = _shift_rows(vr, 1, down)
    st_i[...] = _shift_rows(vi, 1, down)


def _expand_block(u_ref, t_ref, bu_ref):
    for j in range(BD_TILES):
        k = j % 4
        bu_ref[:, j * BD_ST:(j + 1) * BD_ST] = _dot(u_ref[:, k * BD_CH:(k + 1) * BD_CH], t_ref[j])


def _s5_scan(u, tiles, ar8, ai8, pw_r, pw_i, name):
    n = u.shape[0]
    rb = SCAN_ROWS
    nb, steps, lc = n // rb, rb // SEG, 512

    def body(u_ref, t_ref, ar_ref, ai_ref, pwr_ref, pwi_ref, x_ref, st_r, st_i, bu_ref):
        ph, b = pl.program_id(0), pl.program_id(1)

        @pl.when((ph == 0) & (b == 0))
        def _():
            st_r[...] = jnp.zeros_like(st_r)
            st_i[...] = jnp.zeros_like(st_i)

        _expand_block(u_ref, t_ref, bu_ref)

        def scan(store):
            for c in range(S5_GP // lc):
                re, im = slice(c * lc, (c + 1) * lc), slice(S5_GP + c * lc, S5_GP + (c + 1) * lc)
                a_r, a_i = ar_ref[:, re], ai_ref[:, re]

                def step(s, carry):
                    xr, xi = carry
                    rows = pl.ds(pl.multiple_of(s * SEG, SEG), SEG)
                    nr = a_r * xr - a_i * xi + bu_ref[rows, re]
                    ni = a_r * xi + a_i * xr + bu_ref[rows, im]
                    if store:
                        x_ref[rows, re] = nr
                        x_ref[rows, im] = ni
                    return nr, ni

                xr, xi = lax.fori_loop(0, steps, step, (st_r[:, re], st_i[:, re]), unroll=4)
                st_r[:, re] = xr
                st_i[:, re] = xi

        @pl.when(ph == 0)
        def _():
            scan(False)

        @pl.when((ph == 0) & (b == nb - 1))
        def _():
            _chain_segments(st_r, st_i, pwr_ref, pwi_ref, conj=False, down=True)

        @pl.when(ph == 1)
        def _():
            scan(True)

    full = lambda a: pl.BlockSpec(a.shape, lambda ph, b: (0, 0))
    return _pcall(body, name=name, grid=(2, nb),
                  in_specs=[pl.BlockSpec((rb, S5_W), lambda ph, b: (b, 0)), pl.BlockSpec(tiles.shape, lambda ph, b: (0, 0, 0)),
                            full(ar8), full(ai8), full(pw_r), full(pw_i)],
                  out_specs=pl.BlockSpec((rb, 2 * S5_GP), lambda ph, b: (b * ph, 0)),
                  out_shape=jax.ShapeDtypeStruct((n, 2 * S5_GP), F32),
                  scratch_shapes=[pltpu.VMEM((SEG, S5_GP), F32), pltpu.VMEM((SEG, S5_GP), F32),
                                  pltpu.VMEM((rb, 2 * S5_GP), F32)],
                  compiler_params=_params())(u, tiles, ar8, ai8, pw_r, pw_i)


def _s5_scan_bwd(dy, tiles, xs, ar8, ai8, pw_r, pw_i, name):
    n = dy.shape[0]
    rb = SCAN_ROWS
    nb, steps, lc = n // rb, rb // SEG, 256

    def body(dy_ref, t_ref, x_ref, ar_ref, ai_ref, pwr_ref, pwi_ref, lam_ref, da_ref, st_r, st_i, gx_ref):
        ph, b = pl.program_id(0), pl.program_id(1)

        @pl.when((ph == 0) & (b == 0))
        def _():
            st_r[...] = jnp.zeros_like(st_r)
            st_i[...] = jnp.zeros_like(st_i)
            da_ref[...] = jnp.zeros_like(da_ref)

        _expand_block(dy_ref, t_ref, gx_ref)

        def scan(store):
            for c in range(S5_GP // lc):
                re, im = slice(c * lc, (c + 1) * lc), slice(S5_GP + c * lc, S5_GP + (c + 1) * lc)
                a_r, a_i = ar_ref[:, re], ai_ref[:, re]

                def step(s, carry):
                    rows = pl.ds(pl.multiple_of((steps - 1 - s) * SEG, SEG), SEG)
                    if store:
                        lr, li, dr, di = carry
                        xr, xi = x_ref[rows, re], x_ref[rows, im]
                        dr = dr + lr * xr + li * xi
                        di = di + li * xr - lr * xi
                    else:
                        lr, li = carry
                    nr = a_r * lr + a_i * li + gx_ref[rows, re]
                    ni = a_r * li - a_i * lr + gx_ref[rows, im]
                    if store:
                        lam_ref[rows, re] = nr
                        lam_ref[rows, im] = ni
                        return nr, ni, dr, di
                    return nr, ni

                if store:
                    lr, li, dr, di = lax.fori_loop(0, steps, step, (st_r[:, re], st_i[:, re], da_ref[:, re], da_ref[:, im]),
                                                   unroll=4)
                    da_ref[:, re] = dr
                    da_ref[:, im] = di
                else:
                    lr, li = lax.fori_loop(0, steps, step, (st_r[:, re], st_i[:, re]), unroll=4)
                st_r[:, re] = lr
                st_i[:, re] = li

        @pl.when(ph == 0)
        def _():
            scan(False)

        @pl.when((ph == 0) & (b == nb - 1))
        def _():
            _chain_segments(st_r, st_i, pwr_ref, pwi_ref, conj=True, down=False)

        @pl.when(ph == 1)
        def _():
            scan(True)

    full = lambda a: pl.BlockSpec(a.shape, lambda ph, b: (0, 0))
    rev = lambda ph, b: (nb - 1 - b, 0)
    return _pcall(body, name=name, grid=(2, nb),
                  in_specs=[pl.BlockSpec((rb, S5_W), rev), pl.BlockSpec(tiles.shape, lambda ph, b: (0, 0, 0)),
                            pl.BlockSpec((rb, 2 * S5_GP), lambda ph, b: ((nb - 1 - b) * ph, 0)),
                            full(ar8), full(ai8), full(pw_r), full(pw_i)],
                  out_specs=[pl.BlockSpec((rb, 2 * S5_GP), lambda ph, b: (nb - 1 - b * ph, 0)),
                             pl.BlockSpec((SEG, 2 * S5_GP), lambda ph, b: (0, 0))],
                  out_shape=[jax.ShapeDtypeStruct((n, 2 * S5_GP), F32), jax.ShapeDtypeStruct((SEG, 2 * S5_GP), F32)],
                  scratch_shapes=[pltpu.VMEM((SEG, S5_GP), F32), pltpu.VMEM((SEG, S5_GP), F32),
                                  pltpu.VMEM((rb, 2 * S5_GP), F32)],
                  compiler_params=_params())(dy, tiles, xs, ar8, ai8, pw_r, pw_i)


def _s5_discretize(lam_re, lam_im, log_dt, b_re, b_im):
    dt = jnp.exp(log_dt)[:, None]
    mag = jnp.exp(lam_re * dt)
    ar = mag * jnp.cos(lam_im * dt)
    ai = mag * jnp.sin(lam_im * dt)
    den = lam_re * lam_re + lam_im * lam_im
    nr = ar - 1.0
    fr = (nr * lam_re + ai * lam_im) / den
    fi = (ai * lam_re - nr * lam_im) / den
    bbar_re = fr[:, :, None] * b_re - fi[:, :, None] * b_im
    bbar_im = fr[:, :, None] * b_im + fi[:, :, None] * b_re
    return ar, ai, bbar_re, bbar_im


BD_TILES, BD_CH, BD_ST, BD_GROUPS = 8, 128, 512, 8
BD_ROWS = 4096


def _bd_tiles(re, im):
    eye = jnp.eye(BD_GROUPS, dtype=re.dtype)

    def tiles(t):
        t = t.reshape(S5_G // BD_GROUPS, BD_GROUPS, S5_H, S5_P)
        return (t[:, :, :, None, :] * eye[None, :, None, :, None]).reshape(S5_G // BD_GROUPS, BD_CH, BD_ST)

    return jnp.concatenate([tiles(re), tiles(im)], axis=0)


def _bd_blocks(t):
    t = t.reshape(2, S5_G // BD_GROUPS, BD_GROUPS, S5_H, BD_GROUPS, S5_P)
    return jnp.einsum('rkahap->rkahp', t).reshape(2, S5_G, S5_H, S5_P)


def _bd_reduce(x, t, res, name):
    n = x.shape[0]
    tm = _pick(n, BD_ROWS, 16)

    def body(x_ref, t_ref, r_ref, o_ref):
        part = _dot(x_ref[...], t_ref[...], NT)

        @pl.when(pl.program_id(2) == 0)
        def _():
            o_ref[...] = r_ref[...] + part

        @pl.when(pl.program_id(2) == 1)
        def _():
            o_ref[...] += part

    return _pcall(body, name=name, grid=(n // tm, 4, 2),
                  in_specs=[pl.BlockSpec((tm, BD_ST), lambda i, k, r: (i, k + 4 * r)),
                            pl.BlockSpec((None, BD_CH, BD_ST), lambda i, k, r: (k + 4 * r, 0, 0)),
                            pl.BlockSpec((tm, BD_CH), lambda i, k, r: (i, k))],
                  out_specs=pl.BlockSpec((tm, BD_CH), lambda i, k, r: (i, k)),
                  out_shape=jax.ShapeDtypeStruct((n, S5_W), F32), compiler_params=_params())(x, t, res)


def _bd_outer(a, x, name):
    n = a.shape[0]
    tk = _pick(n, BD_ROWS, 16)
    nk = n // tk

    def body(a_ref, x_ref, o_ref):
        part = _dot(a_ref[...], x_ref[...], TN)

        @pl.when(pl.program_id(1) == 0)
        def _():
            o_ref[...] = part

        @pl.when(pl.program_id(1) > 0)
        def _():
            o_ref[...] += part

    return _pcall(body, name=name, grid=(BD_TILES, nk),
                  in_specs=[pl.BlockSpec((tk, BD_CH), lambda j, kk: (kk, j % 4)), pl.BlockSpec((tk, BD_ST), lambda j, kk: (kk, j))],
                  out_specs=pl.BlockSpec((None, BD_CH, BD_ST), lambda j, kk: (j, 0, 0)),
                  out_shape=jax.ShapeDtypeStruct((BD_TILES, BD_CH, BD_ST), F32), compiler_params=_params())(a, x)


def _permute_rows(t):
    n = t.shape[0]
    return t.reshape(SEG, n // SEG, t.shape[1]).transpose(1, 0, 2).reshape(n, t.shape[1])


def _unpermute_rows(t):
    n = t.shape[0]
    return t.reshape(n // SEG, SEG, t.shape[1]).transpose(1, 0, 2).reshape(n, t.shape[1])


def _segment_powers(ar, ai, seg_steps):
    pr, pi = ar.reshape(1, S5_GP), ai.reshape(1, S5_GP)
    e = 1
    while e < seg_steps:
        pr, pi = pr * pr - pi * pi, 2.0 * pr * pi
        e *= 2
    assert e == seg_steps, "segment length must be a power of two"
    rows_r, rows_i = [], []
    for _ in range(3):
        rows_r.append(pr)
        rows_i.append(pi)
        pr, pi = pr * pr - pi * pi, 2.0 * pr * pi
    pad = jnp.zeros((SEG - 3, S5_GP), F32)
    return jnp.concatenate(rows_r + [pad], axis=0), jnp.concatenate(rows_i + [pad], axis=0)


NT = (((1,), (1,)), ((), ()))
TN = (((0,), (0,)), ((), ()))


def _dot(a, b, dims=None, exact=False):
    dims = (((1,), (0,)), ((), ())) if dims is None else dims
    if exact:
        return lax.dot_general(a, b, dims, precision=HI, preferred_element_type=F32)
    return lax.dot_general(a.astype(BF16), b.astype(BF16), dims, preferred_element_type=F32)


def _dot01(a, b, dims=None, ones_first=True):
    x = b if ones_first else a
    hi = x.astype(BF16)
    r1 = x - hi.astype(F32)
    mid = r1.astype(BF16)
    lo = (r1 - mid.astype(F32)).astype(BF16)
    parts = [(_dot(a, p, dims) if ones_first else _dot(p, b, dims)) for p in (lo, mid, hi)]
    return (parts[0] + parts[1]) + parts[2]


HEADS = range(4)


def _gla_chunk_fwd(qc, kc, vc, al, wup, bup, s_prev, tril):
    ones = jnp.ones((GLA_CHUNK, GLA_DV), F32)
    z = [_dot(al, wup[h]) + bup[h] for h in HEADS]
    la = [(jnp.minimum(z[h], 0.0) - jnp.log(1.0 + jnp.exp(-jnp.abs(z[h])))) * (1.0 / GLA_TAU) for h in HEADS]
    bc = [_dot01(tril, la[h]) for h in HEADS]
    blb = [_dot01(la[h], ones, TN, ones_first=False) for h in HEADS]
    bl = [bc[h][GLA_CHUNK - 1:GLA_CHUNK, :] for h in HEADS]
    ebc = [jnp.exp(bc[h]) for h in HEADS]
    qt = [qc[h] * (GLA_DK ** -0.5) * ebc[h] for h in HEADS]
    kt = [kc[h] * jnp.exp(-bc[h]) for h in HEADS]
    ke = [kc[h] * jnp.exp(bl[h] - bc[h]) for h in HEADS]
    sc = [_dot(qt[h], kt[h], NT) * tril for h in HEADS]
    oi = [_dot(sc[h], vc[h]) for h in HEADS]
    oo = [_dot(qt[h], s_prev[h]) for h in HEADS]
    o = [oi[h] + oo[h] for h in HEADS]
    return z, bc, bl, blb, ebc, qt, kt, ke, sc, o


GLA_ROWS = 512
GLA_CPB = GLA_ROWS // GLA_CHUNK


ZA_COLS = 5 * 512
SLOT = 128


def _pad_heads(w):
    r = w.shape[0]
    return jnp.pad(w.reshape(r, GLA_HEADS, GLA_DK), ((0, 0), (0, 0), (0, SLOT - GLA_DK))).reshape(r, GLA_HEADS * SLOT)


def _unpad_heads(w):
    r = w.shape[0]
    return w.reshape(r, GLA_HEADS, SLOT)[:, :, :GLA_DK].reshape(r, GLA_HEADS * GLA_DK)


def _gla_token_specs(blk):
    col = lambda cb: pl.BlockSpec((GLA_ROWS, 512), lambda j: (blk(j), cb))
    whole = lambda a: pl.BlockSpec(a.shape, lambda j: (0,) * a.ndim)
    return col, whole


def _head_ds(h, width):
    return pl.ds(h * SLOT, width)


def _tri(lower):
    ri = lax.broadcasted_iota(jnp.int32, (GLA_CHUNK, GLA_CHUNK), 0)
    ci = lax.broadcasted_iota(jnp.int32, (GLA_CHUNK, GLA_CHUNK), 1)
    return ((ri >= ci) if lower else (ri <= ci)).astype(F32)


def _gla_fwd(za, al, wup, bup, gn, name):
    n = za.shape[0]
    nc = n // GLA_CHUNK

    def body(q_ref, k_ref, v_ref, r_ref, al_ref, wup_ref, bup_ref, gn_ref, y_ref, sp_ref, s_ref):
        @pl.when(pl.program_id(0) == 0)
        def _():
            s_ref[...] = jnp.zeros_like(s_ref)

        tril = _tri(True)

        def chunk(c, carry):
            rows = pl.ds(pl.multiple_of(c * GLA_CHUNK, GLA_CHUNK), GLA_CHUNK)
            alc = al_ref[rows, :]
            vc = [v_ref[rows, _head_ds(h, GLA_DV)] for h in HEADS]
            s_prev = [s_ref[h] for h in HEADS]
            _, _, _, blb, _, _, _, ke, _, o = _gla_chunk_fwd(
                [q_ref[rows, _head_ds(h, GLA_DK)] for h in HEADS], [k_ref[rows, _head_ds(h, GLA_DK)] for h in HEADS],
                vc, alc, [wup_ref[h] for h in HEADS], [bup_ref[h] for h in HEADS], s_prev, tril)
            ds = [_dot(ke[h], vc[h], TN) for h in HEADS]
            for h in HEADS:
                rc = r_ref[rows, _head_ds(h, GLA_DV)]
                sp_ref[h, c] = s_prev[h]
                rstd = lax.rsqrt(jnp.mean(o[h] * o[h], axis=-1, keepdims=True) + EPS)
                y_ref[rows, _head_ds(h, GLA_DV)] = (o[h] * rstd * gn_ref[h] * (rc * _sigmoid(rc))).astype(BF16)
                s_ref[h] = jnp.exp(blb[h]) * s_prev[h] + ds[h]
            return carry

        lax.fori_loop(0, GLA_CPB, chunk, 0)

    col, whole = _gla_token_specs(lambda j: j)
    return _pcall(body, name=name, grid=(n // GLA_ROWS,),
                  in_specs=[col(1), col(2), col(3), col(4), pl.BlockSpec((GLA_ROWS, LANE), lambda j: (j, 0)),
                            whole(wup), whole(bup), whole(gn)],
                  out_specs=[pl.BlockSpec((GLA_ROWS, GLA_HEADS * GLA_DV), lambda j: (j, 0)),
                             pl.BlockSpec((GLA_HEADS, GLA_CPB, GLA_DK, GLA_DV), lambda j: (0, j, 0, 0))],
                  out_shape=[jax.ShapeDtypeStruct((n, GLA_HEADS * GLA_DV), BF16),
                             jax.ShapeDtypeStruct((GLA_HEADS, nc, GLA_DK, GLA_DV), F32)],
                  scratch_shapes=[pltpu.VMEM((GLA_HEADS, GLA_DK, GLA_DV), F32)],
                  compiler_params=_params())(za, za, za, za, al, wup, bup, gn)


def _gla_bwd(za, al, wup, bup, gn, sp, dy, du_s5, name):
    n = za.shape[0]
    nb = n // GLA_ROWS

    def body(q_ref, k_ref, v_ref, r_ref, al_ref, wup_ref, bup_ref, gn_ref, dy_ref, dus_ref, sp_ref,
             dza_ref, dz_ref, dgn_ref, dbup_ref, ds_ref):
        @pl.when(pl.program_id(0) == 0)
        def _():
            ds_ref[...] = jnp.zeros_like(ds_ref)
            dgn_ref[...] = jnp.zeros_like(dgn_ref)
            dbup_ref[...] = jnp.zeros_like(dbup_ref)

        tril, triu = _tri(True), _tri(False)
        dza_ref[:, 0:512] = dus_ref[...]
        dza_ref[:, 512:1536] = jnp.zeros((GLA_ROWS, 1024), F32)
        dz_ref[...] = jnp.zeros_like(dz_ref)

        def chunk(i, carry):
            c = GLA_CPB - 1 - i
            rows = pl.ds(pl.multiple_of(c * GLA_CHUNK, GLA_CHUNK), GLA_CHUNK)
            alc = al_ref[rows, :]
            qc = [q_ref[rows, _head_ds(h, GLA_DK)] for h in HEADS]
            kc = [k_ref[rows, _head_ds(h, GLA_DK)] for h in HEADS]
            vc = [v_ref[rows, _head_ds(h, GLA_DV)] for h in HEADS]
            s_prev = [sp_ref[h, c] for h in HEADS]
            ds = [ds_ref[h] for h in HEADS]
            z, bc, bl, blb, ebc, qt, kt, ke, sc, o = _gla_chunk_fwd(
                qc, kc, vc, alc, [wup_ref[h] for h in HEADS], [bup_ref[h] for h in HEADS], s_prev, tril)
            do = []
            for h in HEADS:
                rc = r_ref[rows, _head_ds(h, GLA_DV)]
                rs = lax.rsqrt(jnp.mean(o[h] * o[h], axis=-1, keepdims=True) + EPS)
                on = o[h] * rs
                sr = _sigmoid(rc)
                sil = rc * sr
                dyv, gnv = dy_ref[rows, _head_ds(h, GLA_DV)], gn_ref[h]
                dgn_ref[h] += jnp.sum(dyv * on * sil, axis=0, keepdims=True)
                dza_ref[rows, pl.ds(2048 + h * SLOT, GLA_DV)] = dyv * on * gnv * (sr * (1.0 + rc * (1.0 - sr)))
                don = dyv * gnv * sil
                do.append(rs * (don - on * jnp.mean(don * on, axis=-1, keepdims=True)))
            dp = [_dot(do[h], vc[h], NT) * tril for h in HEADS]
            dv1 = [_dot(sc[h], do[h], TN) for h in HEADS]
            dv2 = [_dot(ke[h], ds[h]) for h in HEADS]
            dq2 = [_dot(do[h], s_prev[h], NT) for h in HEADS]
            dke = [_dot(vc[h], ds[h], NT) for h in HEADS]
            ddec = [_dot01(jnp.ones((8, GLA_DV), F32), ds[h] * s_prev[h], NT)[0:1, :] for h in HEADS]
            dsn = [_dot(qt[h], do[h], TN) for h in HEADS]
            dq1 = [_dot(dp[h], kt[h]) for h in HEADS]
            dkt = [_dot(dp[h], qt[h], TN) for h in HEADS]
            dbc, dbl = [], []
            for h in HEADS:
                dqt = dq1[h] + dq2[h]
                dza_ref[rows, pl.ds(1536 + h * SLOT, GLA_DV)] = dv1[h] + dv2[h]
                ds_ref[h] = jnp.exp(blb[h]) * ds[h] + dsn[h]
                dza_ref[rows, pl.ds(512 + h * SLOT, GLA_DK)] = dqt * (GLA_DK ** -0.5) * ebc[h]
                dza_ref[rows, pl.ds(1024 + h * SLOT, GLA_DK)] = dkt[h] * jnp.exp(-bc[h]) + dke[h] * jnp.exp(bl[h] - bc[h])
                dbc.append(dqt * qt[h] - dkt[h] * kt[h] - dke[h] * ke[h])
                dbl.append(jnp.sum(dke[h] * ke[h], axis=0, keepdims=True) + ddec[h] * jnp.exp(bl[h]))
            dla = [_dot01(triu, dbc[h]) + dbl[h] for h in HEADS]
            for h in HEADS:
                dz = dla[h] * (1.0 - _sigmoid(z[h])) * (1.0 / GLA_TAU)
                dz_ref[rows, _head_ds(h, GLA_DK)] = dz
                dbup_ref[h] += jnp.sum(dz, axis=0, keepdims=True)
            return carry

        lax.fori_loop(0, GLA_CPB, chunk, 0)

    rev = lambda j: nb - 1 - j
    col, whole = _gla_token_specs(rev)
    tok = lambda w: pl.BlockSpec((GLA_ROWS, w), lambda j: (rev(j), 0))
    h1 = lambda w: pl.BlockSpec((GLA_HEADS, 1, w), lambda j: (0, 0, 0))
    s1 = lambda w: jax.ShapeDtypeStruct((GLA_HEADS, 1, w), F32)
    return _pcall(body, name=name, grid=(nb,),
                  in_specs=[col(1), col(2), col(3), col(4), tok(LANE), whole(wup), whole(bup), whole(gn), tok(512), tok(512),
                            pl.BlockSpec((GLA_HEADS, GLA_CPB, GLA_DK, GLA_DV), lambda j: (0, rev(j), 0, 0))],
                  out_specs=[tok(ZA_COLS), tok(GLA_HEADS * SLOT), h1(GLA_DV), h1(GLA_DK)],
                  out_shape=[jax.ShapeDtypeStruct((n, ZA_COLS), F32), jax.ShapeDtypeStruct((n, GLA_HEADS * SLOT), F32),
                             s1(GLA_DV), s1(GLA_DK)],
                  scratch_shapes=[pltpu.VMEM((GLA_HEADS, GLA_DK, GLA_DV), F32)],
                  compiler_params=_params())(za, za, za, za, al, wup, bup, gn, dy, du_s5, sp)


ANY = pl.BlockSpec(memory_space=pl.ANY)


def _place():
    x, y, c = lax.axis_index("x"), lax.axis_index("y"), lax.axis_index("c")
    chips = [(1 - x, y), (x, 1 - y), (1 - x, 1 - y)]
    return x, y, c, chips


def _remote(src, dst, ssem, rsem, dev):
    return pltpu.make_async_remote_copy(src_ref=src, dst_ref=dst, send_sem=ssem, recv_sem=rsem, device_id=dev,
                                        device_id_type=MESH_ID)


def _half(c, rows):
    h = rows // 2
    return pl.ds(pl.multiple_of(c * h, 8), h)


def _side_gather_ici(shards):
    def copies(ins, outs, ssem, rsem):
        x, y, c, chips = _place()
        mine = 2 * x + y
        cps = []
        for w in range(len(ins)):
            half = _half(c, ins[w].shape[0])
            cps.append(_remote(ins[w], outs[w].at[mine], ssem.at[4 * w], rsem.at[4 * w], (x, y, 1 - c)))
            for k, (px, py) in enumerate(chips):
                cps.append(_remote(ins[w].at[half], outs[w].at[mine, half], ssem.at[4 * w + 1 + k], rsem.at[4 * w + 1 + k],
                                   (px, py, c)))
        return cps

    return _Side(shards, [jax.ShapeDtypeStruct((4,) + s.shape, s.dtype) for s in shards], 4 * len(shards), copies)


def _side_gather_d2d(gathered):
    def copies(ins, outs, ssem, rsem):
        x, y, c, chips = _place()
        cps = []
        for w in range(len(outs)):
            half = _half(c, outs[w].shape[1])
            for k, (px, py) in enumerate(chips):
                theirs = outs[w].at[2 * px + py, half]
                cps.append(_remote(theirs, theirs, ssem.at[3 * w + k], rsem.at[3 * w + k], (x, y, 1 - c)))
        return cps

    return _Side(gathered, [jax.ShapeDtypeStruct(g.shape, g.dtype) for g in gathered], 3 * len(gathered), copies,
                 aliased=True)


def _side_swap_halves(grads):
    def copies(ins, outs, ssem, rsem):
        x, y, c, _ = _place()
        return [_remote(ins[w].at[:, _half(1 - c, ins[w].shape[1]), :], outs[w], ssem.at[w], rsem.at[w], (x, y, 1 - c))
                for w in range(len(ins))]

    return _Side(grads, [jax.ShapeDtypeStruct((4, g.shape[1] // 2, g.shape[2]), g.dtype) for g in grads], len(grads), copies)


def _side_scatter(sums):
    def copies(ins, outs, ssem, rsem):
        x, y, c, chips = _place()
        return [_remote(ins[w].at[2 * px + py], outs[w].at[k], ssem.at[3 * w + k], rsem.at[3 * w + k], (px, py, c))
                for w in range(len(ins)) for k, (px, py) in enumerate(chips)]

    return _Side(sums, [jax.ShapeDtypeStruct((3,) + s.shape[1:], s.dtype) for s in sums], 3 * len(sums), copies)


def _side_swap_reduced(halves):
    def copies(ins, outs, ssem, rsem):
        x, y, c, _ = _place()
        return [_remote(ins[w], outs[w], ssem.at[w], rsem.at[w], (x, y, 1 - c)) for w in range(len(ins))]

    return _Side(halves, [jax.ShapeDtypeStruct(h.shape, h.dtype) for h in halves], len(halves), copies)


def _chip_sum(g, recv, c_arr, name):
    _, r, cols = g.shape
    h = r // 2
    tr = _pick(h, 512, 16)
    g4 = g.reshape(4, 2, h, cols)

    def body(c_ref, g_ref, r_ref, o_ref):
        o_ref[...] = (g_ref[...] + r_ref[...]).astype(BF16)

    grid_spec = pltpu.PrefetchScalarGridSpec(
        num_scalar_prefetch=1, grid=(4, h // tr),
        in_specs=[pl.BlockSpec((None, None, tr, cols), lambda s, i, c_ref: (s, c_ref[0], i, 0)),
                  pl.BlockSpec((None, tr, cols), lambda s, i, c_ref: (s, i, 0))],
        out_specs=pl.BlockSpec((None, tr, cols), lambda s, i, c_ref: (s, i, 0)))
    return _pcall(body, name=name, grid_spec=grid_spec, out_shape=jax.ShapeDtypeStruct((4, h, cols), BF16),
                  compiler_params=_params())(c_arr, g4, recv)


def _owner_sum(sums, others, s_arr, name):
    _, h, cols = sums.shape
    tr = _pick(h, 512, 16)

    def body(s_ref, a_ref, o_ref, out_ref):
        f = lambda v: v.astype(F32)
        out_ref[...] = (f(a_ref[...]) + f(o_ref[0])) + (f(o_ref[1]) + f(o_ref[2]))

    grid_spec = pltpu.PrefetchScalarGridSpec(
        num_scalar_prefetch=1, grid=(h // tr,),
        in_specs=[pl.BlockSpec((None, tr, cols), lambda i, s_ref: (s_ref[0], i, 0)),
                  pl.BlockSpec((3, tr, cols), lambda i, s_ref: (0, i, 0))],
        out_specs=pl.BlockSpec((tr, cols), lambda i, s_ref: (i, 0)))
    return _pcall(body, name=name, grid_spec=grid_spec, out_shape=jax.ShapeDtypeStruct((h, cols), F32),
                  compiler_params=_params())(s_arr, sums, others)


def _side_small_sibling(v):
    def copies(ins, outs, ssem, rsem):
        x, y, c, _ = _place()
        return [_remote(ins[0], outs[0], ssem.at[0], rsem.at[0], (x, y, 1 - c))]

    return _Side([v], [jax.ShapeDtypeStruct(v.shape, F32)], 1, copies)


def _side_small_chips(v):
    def copies(ins, outs, ssem, rsem):
        x, y, c, chips = _place()
        return [_remote(ins[0], outs[0].at[k], ssem.at[k], rsem.at[k], (px, py, c)) for k, (px, py) in enumerate(chips)]

    return _Side([v], [jax.ShapeDtypeStruct((3,) + v.shape, F32)], 3, copies)


def _small_add(v, r, name):
    def body(v_ref, r_ref, o_ref):
        if r.ndim == 2:
            o_ref[...] = v_ref[...] + r_ref[...]
        else:
            o_ref[...] = (v_ref[...] + r_ref[0]) + (r_ref[1] + r_ref[2])

    vm = pl.BlockSpec(memory_space=pltpu.VMEM)
    return _pcall(body, name=name, in_specs=[vm, vm], out_specs=vm, out_shape=jax.ShapeDtypeStruct(v.shape, F32),
                  compiler_params=_params())(v, r)


def _merge_sides(sides):
    if len(sides) == 1:
        return sides[0]

    def copies(in_refs, out_refs, ssem, rsem):
        cps, i, o, q = [], 0, 0, 0
        for s in sides:
            ni, no = len(s.ins), len(s.out_shapes)
            cps += s.copies(in_refs[i:i + ni], out_refs[o:o + no], ssem.at[pl.ds(q, s.nsem)], rsem.at[pl.ds(q, s.nsem)])
            i, o, q = i + ni, o + no, q + s.nsem
        return cps

    assert not any(s.aliased for s in sides)
    return _Side(sum((s.ins for s in sides), []), sum((s.out_shapes for s in sides), []), sum(s.nsem for s in sides), copies)


def _tile_rows(size):
    return -(-size // (8 * LANE)) * 8


def _pack_small(parts):
    pieces = []
    for p in parts:
        flat = p.reshape(-1).astype(F32)
        pieces.append(jnp.pad(flat, (0, _tile_rows(p.size) * LANE - p.size)).reshape(-1, LANE))
    rows = sum(x.shape[0] for x in pieces)
    pieces.append(jnp.zeros(((-rows) % 64, LANE), F32))
    return jnp.concatenate(pieces, axis=0)


def _unpack_small(packed, like):
    out, pos = [], 0
    for p in like:
        rows = _tile_rows(p.size)
        out.append(packed[pos:pos + rows].reshape(-1)[:p.size].reshape(p.shape))
        pos += rows
    return out


FFN_FWD_ROWS, FFN_BWD_ROWS = 1024, 512
FFN_SUB_ROWS = 256


def _ffn_specs(n, d, fs, cap):
    rows = _pick(n, cap, 16)
    row = pl.BlockSpec((rows, d), lambda i, s: (i, 0))
    gain = pl.BlockSpec((1, d), lambda i, s: (0, 0))
    w_row = pl.BlockSpec((None, fs, d), lambda i, s: (s, 0, 0))
    hid = pl.BlockSpec((None, rows, fs), lambda i, s: (s, i, 0))
    return rows, row, gain, w_row, hid


def _ffn_fwd(h, g, w1t, w3t, w2, tag, plan):
    n, d = h.shape
    ns, fs, _ = w2.shape
    rows, row, gain, w_row, hid = _ffn_specs(n, d, fs, FFN_FWD_ROWS)
    sub = rows

    def body(h_ref, g_ref, w1_ref, w3_ref, w2_ref, out_ref, n1_ref, a_ref, b_ref, hm_ref, acc_ref):
        s = pl.program_id(1)

        @pl.when(s == 0)
        def _():
            xv = h_ref[...]
            rstd = lax.rsqrt(jnp.mean(xv * xv, axis=-1, keepdims=True) + EPS)
            n1_ref[...] = (xv * rstd * g_ref[...]).astype(BF16)
            acc_ref[...] = jnp.zeros_like(acc_ref)

        def up(j):
            n1 = n1_ref[j * sub:(j + 1) * sub, :]
            return _dot(n1, w1_ref[...], NT), _dot(n1, w3_ref[...], NT)

        cur = up(0)
        for j in range(rows // sub):
            nxt = up(j + 1) if (j + 1) * sub < rows else None
            a, b = cur
            r = slice(j * sub, (j + 1) * sub)
            hm = (a * _sigmoid(a) * b).astype(BF16)
            a_ref[r, :] = a.astype(BF16)
            b_ref[r, :] = b.astype(BF16)
            hm_ref[r, :] = hm
            acc_ref[r, :] += _dot(hm, w2_ref[...])
            cur = nxt

        @pl.when(s == ns - 1)
        def _():
            out_ref[...] = h_ref[...] + 0.5 * acc_ref[...]

    hid_shape = jax.ShapeDtypeStruct((ns, n, fs), BF16)
    plan.before(f"{tag}_fwd")
    out, n1, a, b, hm = _pcall(
        body, name=f"{tag}_fwd", grid=(n // rows, ns), in_specs=[row, gain, w_row, w_row, w_row],
        out_specs=[row, row, hid, hid, hid],
        out_shape=[jax.ShapeDtypeStruct((n, d), F32), jax.ShapeDtypeStruct((n, d), BF16), hid_shape, hid_shape, hid_shape],
        scratch_shapes=[pltpu.VMEM((rows, d), F32)], compiler_params=_params())(h, g, w1t, w3t, w2)
    plan.after(f"{tag}_fwd")
    return out, (h, n1, a, b, hm)


def _ffn_bwd(dout, saved, g, w1, w3, w2, tag, plan):
    h, n1, a, b, hm = saved
    n, d = h.shape
    ns, fs, _ = w2.shape
    rows, row, gain, w_row, hid = _ffn_specs(n, d, fs, FFN_BWD_ROWS)
    sub = _pick(rows, FFN_SUB_ROWS, 16)

    def body(do_ref, h_ref, g_ref, a_ref, b_ref, w1_ref, w3_ref, w2_ref, dh_ref, da_ref, db_ref, dg_ref, acc_ref):
        i, s = pl.program_id(0), pl.program_id(1)

        @pl.when(s == 0)
        def _():
            acc_ref[...] = jnp.zeros_like(acc_ref)

        @pl.when((s == 0) & (i == 0))
        def _():
            dg_ref[...] = jnp.zeros_like(dg_ref)

        def up(j):
            return _dot(0.5 * do_ref[j * sub:(j + 1) * sub, :], w2_ref[...], NT)

        cur = up(0)
        for j in range(rows // sub):
            nxt = up(j + 1) if (j + 1) * sub < rows else None
            r = slice(j * sub, (j + 1) * sub)
            av, bv = a_ref[r, :].astype(F32), b_ref[r, :].astype(F32)
            sg = _sigmoid(av)
            da = (cur * bv * (sg * (1.0 + av * (1.0 - sg)))).astype(BF16)
            db = (cur * av * sg).astype(BF16)
            da_ref[r, :] = da
            db_ref[r, :] = db
            acc_ref[r, :] += _dot(da, w1_ref[...]) + _dot(db, w3_ref[...])
            cur = nxt

        @pl.when(s == ns - 1)
        def _():
            xv, dn = h_ref[...], acc_ref[...]
            rstd = lax.rsqrt(jnp.mean(xv * xv, axis=-1, keepdims=True) + EPS)
            xh = xv * rstd
            dg_ref[...] += jnp.sum(dn * xh, axis=0, keepdims=True)
            dxh = dn * g_ref[...]
            dh_ref[...] = do_ref[...] + rstd * (dxh - xh * jnp.mean(dxh * xh, axis=-1, keepdims=True))

    hid_shape = jax.ShapeDtypeStruct((ns, n, fs), BF16)
    plan.before(f"{tag}_bwd")
    dh, da, db, dg = _pcall(
        body, name=f"{tag}_bwd", grid=(n // rows, ns), in_specs=[row, row, gain, hid, hid, w_row, w_row, w_row],
        out_specs=[row, hid, hid, gain],
        out_shape=[jax.ShapeDtypeStruct((n, d), F32), hid_shape, hid_shape, jax.ShapeDtypeStruct((1, d), F32)],
        scratch_shapes=[pltpu.VMEM((rows, d), F32)], compiler_params=_params())(dout, h, g, a, b, w1, w3, w2)
    plan.after(f"{tag}_bwd")
    plan.grads[f"{tag}_norm"] = dg
    plan.before(f"{tag}_gw2")
    gw2 = _mm(hm, dout, ta=True, shard='m', alpha=0.5, name=f"{tag}_gw2")
    plan.after(f"{tag}_gw2")
    plan.grads[f"{tag}_w2"] = gw2
    plan.before(f"{tag}_gw1")
    gw1 = _mm(da, n1, ta=True, shard='m', name=f"{tag}_gw1")
    plan.after(f"{tag}_gw1")
    plan.before(f"{tag}_gw3")
    gw3 = _mm(db, n1, ta=True, shard='m', name=f"{tag}_gw3")
    plan.after(f"{tag}_gw3")
    return dh, dg, gw1, gw3, gw2


def _local_step(x, tgt, plan):
    n = x.shape[0]
    grads = plan.grads

    def f(name):
        w = plan.get(name)
        return w.reshape(1, D_MODEL) if name.endswith('_norm') and name != 'gla_out_norm' else w

    def carried(tag, fn, *args, **kw):
        plan.before(tag)
        out = fn(*args, **kw)
        plan.after(tag)
        return out

    h1, ffn1 = _ffn_fwd(x, f('ffn1_norm'), f('ffn1_w1'), f('ffn1_w3'), f('ffn1_w2'), "ffn1", plan)
    u = carried("mix_rms", _rms_fwd, h1, f('mix_norm'), "mix_rms")
    w_in = f('w_in')
    w_a = jnp.concatenate([w_in[:, :512], _pad_heads(w_in[:, 512:768]), _pad_heads(w_in[:, 768:1024]), w_in[:, 1024:2048]],
                          axis=1)
    w_al = jnp.pad(w_in[:, 2048:2048 + GLA_RANK], ((0, 0), (0, LANE - GLA_RANK)))
    w_g = w_in[:, 2048 + GLA_RANK:]
    za = carried("in_a", _mm, u, w_a, name="in_a")
    zg = carried("in_g", _mm, u, w_g, name="in_g")
    al = _mm(u, w_al, name="in_al")
    ar, ai, bbar_re, bbar_im = _s5_discretize(f('s5_lambda_re'), f('s5_lambda_im'), f('s5_log_dt'), f('s5_b_re'), f('s5_b_im'))
    t_b = _bd_tiles(bbar_re.transpose(0, 2, 1), bbar_im.transpose(0, 2, 1)).astype(BF16)
    t_c = _bd_tiles(f('s5_c_re'), -f('s5_c_im')).astype(BF16)
    ar8 = jnp.broadcast_to(ar.reshape(1, S5_GP), (SEG, S5_GP))
    ai8 = jnp.broadcast_to(ai.reshape(1, S5_GP), (SEG, S5_GP))
    pw_r, pw_i = _segment_powers(ar, ai, n // SEG)
    dskip = f('s5_d').reshape(1, S5_W)
    u_s5 = _permute_rows(za[:, :S5_W])
    xs = _s5_scan(u_s5, t_b, ar8, ai8, pw_r, pw_i, "s5_scan")
    ys_p = _bd_reduce(xs, t_c, _scale_rows(u_s5, dskip, "s5_skip"), "s5_y")
    ys = _unpermute_rows(ys_p)
    zgelu = _gelu_fwd(ys, "s5_gelu")
    t_glu = _mm(zgelu, f('s5_glu_w'), bias=f('s5_glu_b').reshape(1, S5_W), name="s5_glu_t")
    y_s5 = _glu_fwd(zgelu, t_glu, "s5_glu")
    wup = jnp.pad(f('gla_a_up_w'), ((0, LANE - GLA_RANK), (0, 0)))
    wup_h = wup.reshape(LANE, GLA_HEADS, GLA_DK).transpose(1, 0, 2)
    bup_h = f('gla_a_up_b').reshape(GLA_HEADS, 1, GLA_DK)
    gn_h = f('gla_out_norm').reshape(GLA_HEADS, 1, GLA_DV)
    y_gla, s_prev = carried("gla_fwd", _gla_fwd, za, al, wup_h, bup_h, gn_h, "gla_fwd")
    ps = _mm(y_s5, f('proj_s5'), name="proj_s5")
    pg = carried("proj_gla", _mm, y_gla, f('proj_gla'), name="proj_gla")
    merged = _merge_fwd(zg, ps, pg, "merge")
    h2 = _mm(merged, f('w_out'), res=h1, name="w_out")
    h3, ffn2 = _ffn_fwd(h2, f('ffn2_norm'), f('ffn2_w1'), f('ffn2_w3'), f('ffn2_w2'), "ffn2", plan)
    loss, dh3, g_final = _final_loss(h3, f('final_norm').reshape(1, D_MODEL), tgt, "loss")
    grads['final_norm'] = g_final.reshape(D_MODEL)
    dh2, grads['ffn2_norm'], grads['ffn2_w1'], grads['ffn2_w3'], grads['ffn2_w2'] = _ffn_bwd(
        dh3, ffn2, f('ffn2_norm'), f('ffn2_w1'), f('ffn2_w3'), f('ffn2_w2'), "ffn2", plan)
    dm = _mm(dh2, f('w_out'), tb=True, name="d_merged")
    grads['w_out'] = _mm(merged, dh2, ta=True, name="g_w_out")
    dps, dpg, dzg = carried("d_merge", _merge_bwd, dm, zg, ps, pg, "d_merge")
    grads['proj_s5'] = _mm(y_s5, dps, ta=True, name="g_proj_s5")
    grads['proj_gla'] = _mm(y_gla, dpg, ta=True, name="g_proj_gla")
    dy_s5 = _mm(dps, f('proj_s5'), tb=True, name="d_y_s5")
    dy_gla = _mm(dpg, f('proj_gla'), tb=True, name="d_y_gla")
    dzgelu, dt_glu, g_glu_b = _glu_bwd1(dy_s5, zgelu, t_glu, "d_glu")
    grads['s5_glu_b'] = g_glu_b.reshape(S5_W)
    grads['s5_glu_w'] = _mm(zgelu, dt_glu, ta=True, name="g_glu_w")
    dzgelu = _mm(dt_glu, f('s5_glu_w'), tb=True, res=dzgelu, name="d_gelu")
    dys, du_skip, g_d = _glu_bwd2(_permute_rows(dzgelu), ys_p, u_s5, dskip, "d_s5_y")
    grads['s5_d'] = g_d.reshape(S5_G, S5_H)
    lam, da8 = carried("s5_scan_bwd", _s5_scan_bwd, dys, t_c, xs, ar8, ai8, pw_r, pw_i, "s5_scan_bwd")
    g_c = _bd_blocks(_bd_outer(dys, xs, "g_s5_c"))
    grads['s5_c_re'], grads['s5_c_im'] = g_c[0], -g_c[1]
    g_b = _bd_blocks(_bd_outer(u_s5, lam, "g_s5_b")).transpose(0, 1, 3, 2)
    g_bbar_re, g_bbar_im = g_b[0], g_b[1]
    da = jnp.sum(da8, axis=0)
    g_ar, g_ai = da[:S5_GP].reshape(S5_G, S5_P), da[S5_GP:].reshape(S5_G, S5_P)
    _, disc_vjp = jax.vjp(_s5_discretize, f('s5_lambda_re'), f('s5_lambda_im'), f('s5_log_dt'), f('s5_b_re'), f('s5_b_im'))
    (grads['s5_lambda_re'], grads['s5_lambda_im'], grads['s5_log_dt'], grads['s5_b_re'],
     grads['s5_b_im']) = disc_vjp((g_ar, g_ai, g_bbar_re, g_bbar_im))
    du_s5 = _unpermute_rows(_bd_reduce(lam, t_b, du_skip, "d_s5_u"))
    dza, dz, dgn, dbup = carried("gla_bwd", _gla_bwd, za, al, wup_h, bup_h, gn_h, s_prev, dy_gla, du_s5, "gla_bwd")
    grads['gla_out_norm'] = dgn.reshape(GLA_HEADS * GLA_DV)
    grads['gla_a_up_b'] = dbup.reshape(GLA_HEADS * GLA_DK)
    grads['gla_a_up_w'] = _unpad_heads(_mm(al, dz, ta=True, name="g_a_up")[:GLA_RANK])
    dal = _mm(dz, _pad_heads(wup), tb=True, name="d_a_low")
    g_wa = _mm(u, dza, ta=True, name="g_in_a")
    g_wg = _mm(u, dzg, ta=True, name="g_in_g")
    g_wal = _mm(u, dal, ta=True, name="g_in_al")
    grads['w_in'] = jnp.concatenate([g_wa[:, :512], _unpad_heads(g_wa[:, 512:1024]), _unpad_heads(g_wa[:, 1024:1536]),
                                     g_wa[:, 1536:], g_wal[:, :GLA_RANK], g_wg], axis=1)
    du = carried("d_u_a", _mm, dza, w_a, tb=True, name="d_u_a")
    du = _mm(dzg, w_g, tb=True, res=du, name="d_u_g")
    du = _mm(dal, w_al, tb=True, res=du, name="d_u_al")
    dh1, g_mix = carried("d_mix_rms", _rms_bwd, h1, f('mix_norm'), du, dh2, "d_mix_rms")
    grads['mix_norm'] = g_mix
    dx, grads['ffn1_norm'], grads['ffn1_w1'], grads['ffn1_w3'], grads['ffn1_w2'] = _ffn_bwd(
        dh1, ffn1, f('ffn1_norm'), f('ffn1_w1'), f('ffn1_w3'), f('ffn1_w2'), "ffn1", plan)
    return loss[0, 0], dx


MIXER_WEIGHTS = ['w_in', 's5_glu_w', 'proj_s5', 'proj_gla', 'w_out', 'gla_a_up_w']
FFN1_WEIGHTS, FFN2_WEIGHTS = FFN_WEIGHTS[:3], FFN_WEIGHTS[3:]
TRANSPOSED = ['ffn1_w1', 'ffn1_w3', 'ffn2_w1', 'ffn2_w3']


def _local_shard(w, nm):
    return jnp.swapaxes(w, 1, 2)[0] if nm in TRANSPOSED else w[0]
FFN1_EARLY = ['ffn1_w2']
GRAD_GROUPS = {'ffn2': FFN2_WEIGHTS, 'mixer': ['w_out', 'proj_s5', 'proj_gla', 's5_glu_w', 'w_in'], 'ffn1': FFN1_WEIGHTS}


class _Plan:
    def __init__(self, a, c_arr, s_arr):
        self.a, self.c_arr, self.s_arr = a, c_arr, s_arr
        self.grads, self.weights, self.riding = {}, {}, {}
        self.g4s, self.chip_sums, self.halves, self.sib_halves = {}, {}, {}, {}
        for nm in SMALL:
            if nm != 'gla_a_up_w':
                self.weights[nm] = a[nm] if nm == 'final_norm' else a[nm][0]
        ici = _side_gather_ici(self._shards(FFN1_WEIGHTS))
        _run_side(ici, "gather_ffn1_ici")
        self._gathered(FFN1_WEIGHTS, _run_side(_side_gather_d2d(ici.outs), "gather_ffn1_d2d"))

    def _shards(self, names):
        return [_local_shard(self.a[nm], nm).astype(F32 if nm == 'gla_a_up_w' else BF16) for nm in names]

    def _gathered(self, names, arrs):
        for nm, g4 in zip(names, arrs):
            if nm in FFN_WEIGHTS:
                self.weights[nm] = g4
            elif nm in COL_SHARDED:
                self.weights[nm] = jnp.concatenate([g4[s] for s in range(4)], axis=1)
            else:
                self.weights[nm] = g4.reshape(4 * g4.shape[1], g4.shape[2])

    def get(self, name):
        return self.weights[name]

    def _shard_major(self, nm):
        g = self.grads[nm]
        if nm in FFN_WEIGHTS:
            return g
        if nm in COL_SHARDED:
            return jnp.stack(jnp.split(g, 4, axis=1))
        return g.reshape(4, g.shape[0] // 4, g.shape[1])

    def _schedule(self, tag):
        grp = GRAD_GROUPS
        gathers = {"ffn1_fwd": ('ici', MIXER_WEIGHTS), "mix_rms": ('d2d', MIXER_WEIGHTS),
                   "in_a": ('ici', FFN2_WEIGHTS[:1]), "in_g": ('d2d', FFN2_WEIGHTS[:1]),
                   "gla_fwd": ('ici', FFN2_WEIGHTS[1:]), "proj_gla": ('d2d', FFN2_WEIGHTS[1:])}
        if tag in gathers:
            kind, names = gathers[tag]
            key = tuple(names)
            if kind == 'ici':
                return [(_side_gather_ici(self._shards(names)), lambda outs: self.riding.update({key: outs}))]
            return [(_side_gather_d2d(self.riding[key]), lambda outs: self._gathered(names, outs))]
        steps = {"d_merge": (grp['ffn2'], 0), "s5_scan_bwd": (FFN2_WEIGHTS[2:], 1), "gla_bwd": (FFN2_WEIGHTS[:2], 1),
                 "d_mix_rms": (grp['ffn2'], 2),
                 "d_u_a": (grp['mixer'], 0), "ffn1_bwd": (grp['mixer'], 1), "ffn1_gw2": (grp['mixer'], 2),
                 "ffn1_gw1": (FFN1_EARLY, 0), "ffn1_gw3": (FFN1_EARLY, 1)}
        entries = [self._reduce_stage(*steps[tag])] if tag in steps else []
        if tag == "ffn1_gw1":
            entries.append(self._small_stage(0))
        if tag == "ffn1_gw3":
            entries.append(self._small_stage(1))
        return entries

    def _small_stage(self, stage):
        if stage == 0:
            a, grads = self.a, self.grads
            self.small_parts = [grads[nm].reshape(a[nm].shape) for nm in SMALL if nm != 'gla_a_up_w'] + [grads['gla_a_up_w']]
            packed = _pack_small(self.small_parts)

            def done(outs):
                self.small_pair = _small_add(packed, outs[0], "small_sum_pair")
            return _side_small_sibling(packed), done

        def done(outs):
            self.small_total = _small_add(self.small_pair, outs[0], "small_sum_chips")
        return _side_small_chips(self.small_pair), done

    def _reduce_stage(self, names, stage):
        if stage == 0:
            for nm in names:
                self.g4s[nm] = self._shard_major(nm)

            def done(outs):
                for nm, r in zip(names, outs):
                    self.chip_sums[nm] = _chip_sum(self.g4s[nm], r, self.c_arr, f"chip_sum_{nm}")
            return _side_swap_halves([self.g4s[nm] for nm in names]), done
        if stage == 1:
            def done(outs):
                for nm, o in zip(names, outs):
                    self.halves[nm] = _owner_sum(self.chip_sums[nm], o, self.s_arr, f"owner_sum_{nm}")
            return _side_scatter([self.chip_sums[nm] for nm in names]), done

        def done(outs):
            self.sib_halves.update(zip(names, outs))
        return _side_swap_reduced([self.halves[nm] for nm in names]), done

    def before(self, tag):
        entries = self._schedule(tag)
        if entries:
            merged = _merge_sides([side for side, _ in entries])
            self.riding[tag] = (merged, entries)
            _RIDER.append(merged)

    def after(self, tag):
        if tag in self.riding:
            merged, entries = self.riding.pop(tag)
            assert not _RIDER and merged.outs is not None, tag
            pos = 0
            for side, done in entries:
                done(merged.outs[pos:pos + len(side.out_shapes)])
                pos += len(side.out_shapes)

    def finish(self):
        late = [nm for nm in GRAD_GROUPS['ffn1'] if nm not in FFN1_EARLY]
        for stage, names in ((0, late), (1, late), (2, GRAD_GROUPS['ffn1'])):
            side, done = self._reduce_stage(names, stage)
            done(_run_side(side, f"grad_ffn1_stage{stage}"))


def _train_step(a):
    x = a['x'][0]
    tgt = a['loss_target'][0]
    xi, yi, ci = lax.axis_index("x"), lax.axis_index("y"), lax.axis_index("c")
    c_arr = jnp.reshape(ci, (1,)).astype(jnp.int32)
    s_arr = jnp.reshape(2 * xi + yi, (1,)).astype(jnp.int32)
    plan = _Plan(a, c_arr, s_arr)
    loss, dx = _local_step(x, tgt, plan)
    plan.finish()
    grads = plan.grads
    loss = lax.psum(loss, ("x", "y", "c"))
    halves = [plan.halves[nm] for nm in SHARDED]
    sib_halves = [plan.sib_halves[nm] for nm in SHARDED]
    red = {}
    small_sum = _unpack_small(plan.small_total, plan.small_parts)
    small_names = [nm for nm in SMALL if nm != 'gla_a_up_w']
    for nm, g in zip(small_names, small_sum[:-1]):
        red[nm] = g
    g_up = small_sum[-1]
    red['gla_a_up_w'] = lax.dynamic_slice(g_up, (0, (2 * xi + yi) * GLA_DK), (GLA_RANK, GLA_DK))
    out_g, out_d, out_m, out_v = {}, {}, {}, {}
    for nm, own, sib in zip(SHARDED, halves, sib_halves):
        loc = lambda pre: _local_shard(a[pre + nm], nm)
        res = _adamw_halves(loc(''), own, sib, loc('m_'), loc('v_'), c_arr, f"adamw_{nm}")
        back = (lambda t: jnp.swapaxes(t[None], 1, 2)) if nm in TRANSPOSED else (lambda t: t[None])
        out_g[nm], out_d[nm], out_m[nm], out_v[nm] = (back(t) for t in res)
    rest = [nm for nm in WEIGHTS if nm not in SHARDED]
    pk = lambda pre: _pack_small([a[pre + nm] for nm in rest])
    d, nm_, nv_ = _adamw(pk(''), _pack_small([red[nm] for nm in rest]), pk('m_'), pk('v_'), "adamw_small")
    like = [a[nm] for nm in rest]
    for nm, g, dd, mm_, vv_ in zip(rest, [red[nm].reshape(a[nm].shape) for nm in rest], _unpack_small(d, like),
                                   _unpack_small(nm_, like), _unpack_small(nv_, like)):
        out_g[nm], out_d[nm], out_m[nm], out_v[nm] = g, dd, mm_, vv_
    return (loss, dx[None], *[out_g[nm] for nm in WEIGHTS], *[out_d[nm] for nm in WEIGHTS],
            *[out_m[nm] for nm in WEIGHTS], *[out_v[nm] for nm in WEIGHTS])


def kernel(x, ffn1_norm, ffn1_w1, ffn1_w3, ffn1_w2, mix_norm, w_in, s5_lambda_re, s5_lambda_im, s5_log_dt, s5_b_re, s5_b_im, s5_c_re, s5_c_im, s5_d, s5_glu_w, s5_glu_b, gla_a_up_w, gla_a_up_b, gla_out_norm, proj_s5, proj_gla, w_out, ffn2_norm, ffn2_w1, ffn2_w3, ffn2_w2, final_norm, loss_target, m_ffn1_norm, m_ffn1_w1, m_ffn1_w3, m_ffn1_w2, m_mix_norm, m_w_in, m_s5_lambda_re, m_s5_lambda_im, m_s5_log_dt, m_s5_b_re, m_s5_b_im, m_s5_c_re, m_s5_c_im, m_s5_d, m_s5_glu_w, m_s5_glu_b, m_gla_a_up_w, m_gla_a_up_b, m_gla_out_norm, m_proj_s5, m_proj_gla, m_w_out, m_ffn2_norm, m_ffn2_w1, m_ffn2_w3, m_ffn2_w2, m_final_norm, v_ffn1_norm, v_ffn1_w1, v_ffn1_w3, v_ffn1_w2, v_mix_norm, v_w_in, v_s5_lambda_re, v_s5_lambda_im, v_s5_log_dt, v_s5_b_re, v_s5_b_im, v_s5_c_re, v_s5_c_im, v_s5_d, v_s5_glu_w, v_s5_glu_b, v_gla_a_up_w, v_gla_a_up_b, v_gla_out_norm, v_proj_s5, v_proj_gla, v_w_out, v_ffn2_norm, v_ffn2_w1, v_ffn2_w3, v_ffn2_w2, v_final_norm):
    return _train_step(dict(locals()))
```

```python
import functools

import jax
import jax.numpy as jnp
from jax import lax
from jax.experimental import pallas as pl
from jax.experimental.pallas import tpu as pltpu

F32 = jnp.float32
BF16 = jnp.bfloat16
HI = lax.Precision.HIGHEST
MESH_ID = pl.DeviceIdType.MESH

D_MODEL = 1024
EPS = 1e-6
S5_G, S5_P, S5_H = 32, 64, 16
S5_W = S5_G * S5_H
S5_GP = S5_G * S5_P
SEG = 8
SCAN_ROWS = 256
GLA_HEADS, GLA_DK, GLA_DV = 4, 64, 128
GLA_CHUNK = 64
GLA_TAU = 16.0
GLA_RANK = 16
ADAM_LR, ADAM_B1, ADAM_B2, ADAM_EPS, ADAM_WD, ADAM_STEP = 0.001, 0.9, 0.999, 1e-08, 0.01, 10
V7X_VMEM_LIMIT = 56 * 1024 * 1024
LANE = 128

WEIGHTS = ['ffn1_norm', 'ffn1_w1', 'ffn1_w3', 'ffn1_w2', 'mix_norm', 'w_in', 's5_lambda_re', 's5_lambda_im',
           's5_log_dt', 's5_b_re', 's5_b_im', 's5_c_re', 's5_c_im', 's5_d', 's5_glu_w', 's5_glu_b', 'gla_a_up_w',
           'gla_a_up_b', 'gla_out_norm', 'proj_s5', 'proj_gla', 'w_out', 'ffn2_norm', 'ffn2_w1', 'ffn2_w3',
           'ffn2_w2', 'final_norm']
SHARDED = ['ffn1_w1', 'ffn1_w3', 'ffn1_w2', 'w_in', 's5_glu_w', 'proj_s5', 'proj_gla', 'w_out',
           'ffn2_w1', 'ffn2_w3', 'ffn2_w2']
COL_SHARDED = ['ffn1_w1', 'ffn1_w3', 'w_in', 'proj_s5', 'proj_gla', 'ffn2_w1', 'ffn2_w3', 'gla_a_up_w']
SMALL = [n for n in WEIGHTS if n not in SHARDED]
FFN_WEIGHTS = ['ffn1_w1', 'ffn1_w3', 'ffn1_w2', 'ffn2_w1', 'ffn2_w3', 'ffn2_w2']


def _params(**kw):
    return pltpu.CompilerParams(vmem_limit_bytes=V7X_VMEM_LIMIT, **kw)


class _Side:
    def __init__(self, ins, out_shapes, nsem, copies, aliased=False):
        self.ins, self.out_shapes, self.nsem, self.copies, self.aliased = list(ins), list(out_shapes), nsem, copies, aliased
        self.outs = None


_RIDER = []


def _pcall(body, **kw):
    if _RIDER:
        return _carry(body, _RIDER.pop(), **kw)
    return pl.pallas_call(body, **kw)


def _carry(body, side, *, name, grid, in_specs, out_specs, out_shape, scratch_shapes=(), compiler_params=None):
    del compiler_params
    single = not isinstance(out_shape, (list, tuple))
    out_specs = [out_specs] if single else list(out_specs)
    out_shape = [out_shape] if single else list(out_shape)
    n_in, n_out, n_scr = len(in_specs), len(out_shape), len(scratch_shapes)
    s_in, s_out = len(side.ins), len(side.out_shapes)
    any_spec = pl.BlockSpec(memory_space=pl.ANY)

    def wrapped(*refs):
        cuts = [n_in, s_in, n_out, s_out, n_scr]
        parts, pos = [], 0
        for c in cuts:
            parts.append(refs[pos:pos + c])
            pos += c
        ins, sins, outs, souts, scr = parts
        ssem, rsem = refs[pos], refs[pos + 1]
        first = last = None
        for d, g in enumerate(grid):
            i = pl.program_id(d)
            first = (i == 0) if first is None else first & (i == 0)
            last = (i == g - 1) if last is None else last & (i == g - 1)

        @pl.when(first)
        def _():
            for cp in side.copies(sins, souts, ssem, rsem):
                cp.start()

        body(*ins, *outs, *scr)

        @pl.when(last)
        def _():
            for cp in side.copies(sins, souts, ssem, rsem):
                cp.wait()

    call = pl.pallas_call(
        wrapped, name=name, grid=grid, in_specs=list(in_specs) + [any_spec] * s_in,
        out_specs=out_specs + [any_spec] * s_out, out_shape=out_shape + side.out_shapes,
        scratch_shapes=list(scratch_shapes) + [pltpu.SemaphoreType.DMA((side.nsem,)), pltpu.SemaphoreType.DMA((side.nsem,))],
        input_output_aliases={n_in + j: n_out + j for j in range(s_in)} if side.aliased else {},
        compiler_params=_params(has_side_effects=True))

    def run(*args):
        res = call(*args, *side.ins)
        side.outs = list(res[n_out:])
        return res[0] if single else list(res[:n_out])

    return run


def _run_side(side, name):
    s_in, s_out = len(side.ins), len(side.out_shapes)
    any_spec = pl.BlockSpec(memory_space=pl.ANY)

    def body(*refs):
        sins, souts = refs[:s_in], refs[s_in:s_in + s_out]
        ssem, rsem = refs[s_in + s_out:]
        cps = side.copies(sins, souts, ssem, rsem)
        for cp in cps:
            cp.start()
        for cp in cps:
            cp.wait()

    side.outs = list(pl.pallas_call(
        body, name=name, in_specs=[any_spec] * s_in, out_specs=[any_spec] * s_out, out_shape=side.out_shapes,
        scratch_shapes=[pltpu.SemaphoreType.DMA((side.nsem,)), pltpu.SemaphoreType.DMA((side.nsem,))],
        input_output_aliases={j: j for j in range(s_in)} if side.aliased else {},
        compiler_params=pltpu.CompilerParams(has_side_effects=True))(*side.ins))
    return side.outs


def _pick(n, cap, quantum):
    if n <= cap:
        return n
    best = None
    for t in range(quantum, cap + 1, quantum):
        if n % t == 0:
            best = t
    assert best is not None, (n, cap, quantum)
    return best


def _sigmoid(x):
    return jax.nn.sigmoid(x)


def _mm(a, b, *, name, ta=False, tb=False, out_dtype=F32, alpha=1.0, res=None, bias=None, exact=False, shard=None):
    ns = 4
    (k_a, m) = a.shape[-2:] if ta else a.shape[-2:][::-1]
    (k_b, n) = b.shape[-2:][::-1] if tb else b.shape[-2:]
    assert k_a == k_b, (a.shape, b.shape, ta, tb)
    assert (a.ndim == 3) == (shard in ('k', 'm')) and (b.ndim == 3) == (shard in ('n', 'k'))
    k = k_a
    tm = _pick(m, 1024, 128)
    tn = _pick(n, 1024, 128)
    tk = _pick(k, 1024, 128)
    pm, pn, pk = m // tm, n // tn, k // tk
    gm = pm * (ns if shard == 'm' else 1)
    gn = pn * (ns if shard == 'n' else 1)
    gk = pk * (ns if shard == 'k' else 1)
    dims = (((0,) if ta else (1,), (1,) if tb else (0,)), ((), ()))
    op_dtype = F32 if exact else BF16

    def body(*refs):
        a_ref, b_ref = refs[0], refs[1]
        pos = 2
        res_ref = bias_ref = None
        if res is not None:
            res_ref = refs[pos]
            pos += 1
        if bias is not None:
            bias_ref = refs[pos]
            pos += 1
        o_ref, acc_ref = refs[pos], refs[pos + 1]
        kk = pl.program_id(2)

        @pl.when(kk == 0)
        def _():
            acc_ref[...] = jnp.zeros_like(acc_ref)

        acc_ref[...] += lax.dot_general(a_ref[...].astype(op_dtype), b_ref[...].astype(op_dtype), dims,
                                        precision=HI if exact else None, preferred_element_type=F32)

        @pl.when(kk == gk - 1)
        def _():
            o = acc_ref[...]
            if alpha != 1.0:
                o = o * alpha
            if bias_ref is not None:
                o = o + bias_ref[...]
            if res_ref is not None:
                o = o + res_ref[...]
            o_ref[...] = o.astype(out_dtype)

    def spec(block, sharded_on, order):
        per = {'m': pm, 'n': pn, 'k': pk}

        def index(i, j, kk):
            g = {'m': i, 'n': j, 'k': kk}
            r, c = order(i % pm if shard == 'm' else i, j % pn if shard == 'n' else j, kk % pk if shard == 'k' else kk)
            if sharded_on is None:
                return (r, c)
            return (g[sharded_on] // per[sharded_on], r, c)

        return pl.BlockSpec(block if sharded_on is None else (None,) + block, index)

    a_sh = shard if shard in ('k', 'm') else None
    b_sh = shard if shard in ('n', 'k') else None
    o_sh = shard if shard in ('n', 'm') else None
    a_spec = spec((tk, tm), a_sh, lambda i, j, kk: (kk, i)) if ta else spec((tm, tk), a_sh, lambda i, j, kk: (i, kk))
    b_spec = spec((tn, tk), b_sh, lambda i, j, kk: (j, kk)) if tb else spec((tk, tn), b_sh, lambda i, j, kk: (kk, j))
    ins, in_specs = [a, b], [a_spec, b_spec]
    if res is not None:
        assert o_sh is None
        ins.append(res)
        in_specs.append(pl.BlockSpec((tm, tn), lambda i, j, kk: (i, j)))
    if bias is not None:
        assert o_sh is None
        ins.append(bias)
        in_specs.append(pl.BlockSpec((1, tn), lambda i, j, kk: (0, j)))
    out_shape = (m, n) if o_sh is None else (ns, m, n)
    return _pcall(body, name=name, grid=(gm, gn, gk), in_specs=in_specs,
                  out_specs=spec((tm, tn), o_sh, lambda i, j, kk: (i, j)),
                  out_shape=jax.ShapeDtypeStruct(out_shape, out_dtype),
                  scratch_shapes=[pltpu.VMEM((tm, tn), F32)], compiler_params=_params())(*ins)


ROWS_VMEM_BUDGET = 24 * 1024 * 1024


def _rows(body, ins, outs, *, n, name):
    cols = sum(a.shape[1] for a, kind in ins if kind == 'r') + sum(c for c, _, kind in outs if kind == 'r')
    cap = 256
    while cap < 2048 and 2 * 4 * cols * (2 * cap) <= ROWS_VMEM_BUDGET:
        cap *= 2
    tm = _pick(n, cap, 16)
    in_specs = []
    for arr, kind in ins:
        if kind == 'r':
            in_specs.append(pl.BlockSpec((tm, arr.shape[1]), lambda i: (i, 0)))
        else:
            in_specs.append(pl.BlockSpec(arr.shape, lambda i: (0, 0)))
    out_specs, out_shape = [], []
    for cols, dtype, kind in outs:
        if kind == 'r':
            out_specs.append(pl.BlockSpec((tm, cols), lambda i: (i, 0)))
            out_shape.append(jax.ShapeDtypeStruct((n, cols), dtype))
        else:
            out_specs.append(pl.BlockSpec((1, cols), lambda i: (0, 0)))
            out_shape.append(jax.ShapeDtypeStruct((1, cols), dtype))
    n_in = len(ins)
    acc_ids = [j for j, o in enumerate(outs) if o[2] == 'a']

    def wrapped(*refs):
        if acc_ids:
            @pl.when(pl.program_id(0) == 0)
            def _():
                for j in acc_ids:
                    refs[n_in + j][...] = jnp.zeros_like(refs[n_in + j])
        body(*refs)

    res = _pcall(wrapped, name=name, grid=(n // tm,), in_specs=in_specs, out_specs=out_specs, out_shape=out_shape,
                 compiler_params=_params())(*[a for a, _ in ins])
    return res


def _rms_fwd(x, g, name):
    def body(x_ref, g_ref, o_ref):
        xv = x_ref[...]
        rstd = lax.rsqrt(jnp.mean(xv * xv, axis=-1, keepdims=True) + EPS)
        o_ref[...] = (xv * rstd * g_ref[...]).astype(BF16)
    return _rows(body, [(x, 'r'), (g, 'f')], [(x.shape[1], BF16, 'r')], n=x.shape[0], name=name)[0]


def _rms_bwd(x, g, dn, dres, name):
    def body(x_ref, g_ref, dn_ref, dres_ref, dx_ref, dg_ref):
        xv = x_ref[...]
        rstd = lax.rsqrt(jnp.mean(xv * xv, axis=-1, keepdims=True) + EPS)
        xh = xv * rstd
        dn = dn_ref[...]
        dg_ref[...] += jnp.sum(dn * xh, axis=0, keepdims=True)
        dxh = dn * g_ref[...]
        dx_ref[...] = dres_ref[...] + rstd * (dxh - xh * jnp.mean(dxh * xh, axis=-1, keepdims=True))
    d = x.shape[1]
    return _rows(body, [(x, 'r'), (g, 'f'), (dn, 'r'), (dres, 'r')], [(d, F32, 'r'), (d, F32, 'a')],
                 n=x.shape[0], name=name)


def _gelu_parts(y):
    c0 = 0.7978845608028654
    inner = c0 * (y + 0.044715 * y * y * y)
    th = jnp.tanh(inner)
    return th, c0 * (1.0 + 3.0 * 0.044715 * y * y)


def _gelu_fwd(y, name):
    def body(y_ref, o_ref):
        yv = y_ref[...]
        th, _ = _gelu_parts(yv)
        o_ref[...] = 0.5 * yv * (1.0 + th)
    return _rows(body, [(y, 'r')], [(y.shape[1], F32, 'r')], n=y.shape[0], name=name)[0]


def _glu_fwd(zg, t, name):
    def body(z_ref, t_ref, o_ref):
        o_ref[...] = (z_ref[...] * _sigmoid(t_ref[...])).astype(BF16)
    return _rows(body, [(zg, 'r'), (t, 'r')], [(zg.shape[1], BF16, 'r')], n=zg.shape[0], name=name)[0]


def _glu_bwd1(dy, zg, t, name):
    def body(dy_ref, z_ref, t_ref, dz_ref, dt_ref, db_ref):
        dyv, zv = dy_ref[...], z_ref[...]
        sg = _sigmoid(t_ref[...])
        dz_ref[...] = dyv * sg
        dt = dyv * zv * sg * (1.0 - sg)
        dt_ref[...] = dt.astype(BF16)
        db_ref[...] += jnp.sum(dt, axis=0, keepdims=True)
    w = zg.shape[1]
    return _rows(body, [(dy, 'r'), (zg, 'r'), (t, 'r')], [(w, F32, 'r'), (w, BF16, 'r'), (w, F32, 'a')],
                 n=zg.shape[0], name=name)


def _glu_bwd2(dzg, ys, u, dskip, name):
    def body(dz_ref, y_ref, u_ref, d_ref, dy_ref, du_ref, dd_ref):
        yv = y_ref[...]
        th, dinner = _gelu_parts(yv)
        dy = dz_ref[...] * (0.5 * (1.0 + th) + 0.5 * yv * (1.0 - th * th) * dinner)
        dy_ref[...] = dy
        du_ref[...] = dy * d_ref[...]
        dd_ref[...] += jnp.sum(dy * u_ref[...], axis=0, keepdims=True)
    w = ys.shape[1]
    return _rows(body, [(dzg, 'r'), (ys, 'r'), (u, 'r'), (dskip, 'f')], [(w, F32, 'r'), (w, F32, 'r'), (w, F32, 'a')],
                 n=ys.shape[0], name=name)


def _scale_rows(u, dskip, name):
    def body(u_ref, d_ref, o_ref):
        o_ref[...] = u_ref[...] * d_ref[...]
    return _rows(body, [(u, 'r'), (dskip, 'f')], [(u.shape[1], F32, 'r')], n=u.shape[0], name=name)[0]


def _merge_fwd(zg, ps, pg, name):
    def body(z_ref, ps_ref, pg_ref, o_ref):
        zv = z_ref[...]
        o_ref[...] = (_sigmoid(zv[:, :D_MODEL]) * ps_ref[...] + _sigmoid(zv[:, D_MODEL:]) * pg_ref[...]).astype(BF16)
    return _rows(body, [(zg, 'r'), (ps, 'r'), (pg, 'r')], [(D_MODEL, BF16, 'r')], n=zg.shape[0], name=name)[0]


def _merge_bwd(dm, zg, ps, pg, name):
    def body(dm_ref, z_ref, ps_ref, pg_ref, dps_ref, dpg_ref, dz_ref):
        dmv, zv = dm_ref[...], z_ref[...]
        s1, s2 = _sigmoid(zv[:, :D_MODEL]), _sigmoid(zv[:, D_MODEL:])
        dps_ref[...] = (dmv * s1).astype(BF16)
        dpg_ref[...] = (dmv * s2).astype(BF16)
        dz_ref[:, :D_MODEL] = dmv * ps_ref[...] * s1 * (1.0 - s1)
        dz_ref[:, D_MODEL:] = dmv * pg_ref[...] * s2 * (1.0 - s2)
    return _rows(body, [(dm, 'r'), (zg, 'r'), (ps, 'r'), (pg, 'r')],
                 [(D_MODEL, BF16, 'r'), (D_MODEL, BF16, 'r'), (2 * D_MODEL, F32, 'r')], n=zg.shape[0], name=name)


def _final_loss(h, g, tgt, name):
    def body(h_ref, g_ref, t_ref, loss_ref, dh_ref, dg_ref):
        hv = h_ref[...]
        rstd = lax.rsqrt(jnp.mean(hv * hv, axis=-1, keepdims=True) + EPS)
        xh = hv * rstd
        err = xh * g_ref[...] - t_ref[...]
        part = 0.5 * jnp.sum(jnp.mean(err * err, axis=-1, keepdims=True), axis=0, keepdims=True)
        loss_ref[...] += jnp.broadcast_to(part, loss_ref.shape)
        dout = err * (1.0 / hv.shape[1])
        dg_ref[...] += jnp.sum(dout * xh, axis=0, keepdims=True)
        dxh = dout * g_ref[...]
        dh_ref[...] = rstd * (dxh - xh * jnp.mean(dxh * xh, axis=-1, keepdims=True))
    d = h.shape[1]
    return _rows(body, [(h, 'r'), (g, 'f'), (tgt, 'r')], [(LANE, F32, 'a'), (d, F32, 'r'), (d, F32, 'a')],
                 n=h.shape[0], name=name)


def _adamw_math(wv, gv, mv, vv):
    nm = ADAM_B1 * mv + (1.0 - ADAM_B1) * gv
    nv = ADAM_B2 * vv + (1.0 - ADAM_B2) * (gv * gv)
    m_hat = nm / (1.0 - ADAM_B1 ** ADAM_STEP)
    v_hat = nv / (1.0 - ADAM_B2 ** ADAM_STEP)
    return -ADAM_LR * (m_hat / (jnp.sqrt(v_hat) + ADAM_EPS) + ADAM_WD * wv), nm, nv


def _adamw(w, g, m, v, name):
    def body(w_ref, g_ref, m_ref, v_ref, d_ref, nm_ref, nv_ref):
        d_ref[...], nm_ref[...], nv_ref[...] = _adamw_math(w_ref[...], g_ref[...], m_ref[...], v_ref[...])
    c = w.shape[1]
    return _rows(body, [(w, 'r'), (g, 'r'), (m, 'r'), (v, 'r')], [(c, F32, 'r')] * 3, n=w.shape[0], name=name)


def _adamw_halves(w, g_own, g_sib, m, v, c_arr, name):
    r, cols = w.shape
    h = r // 2
    tr = _pick(h, 512, 8)
    per = h // tr

    def body(c_ref, w_ref, go_ref, gs_ref, m_ref, v_ref, g_ref, d_ref, nm_ref, nv_ref):
        mine = (pl.program_id(0) // per) == c_ref[0]
        gv = jnp.where(mine, go_ref[...], gs_ref[...])
        g_ref[...] = gv
        d_ref[...], nm_ref[...], nv_ref[...] = _adamw_math(w_ref[...], gv, m_ref[...], v_ref[...])

    full = pl.BlockSpec((tr, cols), lambda i, c_ref: (i, 0))
    half = pl.BlockSpec((tr, cols), lambda i, c_ref: (i % per, 0))
    grid_spec = pltpu.PrefetchScalarGridSpec(num_scalar_prefetch=1, grid=(2 * per,),
                                             in_specs=[full, half, half, full, full], out_specs=[full] * 4)
    return _pcall(body, name=name, grid_spec=grid_spec, out_shape=[jax.ShapeDtypeStruct((r, cols), F32)] * 4,
                  compiler_params=_params())(c_arr, w, g_own, g_sib, m, v)


def _shift_rows(v, sh, down):
    rolled = pltpu.roll(v, sh if down else v.shape[0] - sh, axis=0)
    row = lax.broadcasted_iota(jnp.int32, v.shape, 0)
    keep = (row >= sh) if down else (row < v.shape[0] - sh)
    return jnp.where(keep, rolled, 0.0)


def _chain_segments(st_r, st_i, pw_r_ref, pw_i_ref, conj, down):
    vr, vi = st_r[...], st_i[...]
    sh, k = 1, 0
    while sh < SEG:
        pr, pi = pw_r_ref[k:k + 1, :], pw_i_ref[k:k + 1, :]
        if conj:
            pi = -pi
        sr, si = _shift_rows(vr, sh, down), _shift_rows(vi, sh, down)
        vr, vi = vr + pr * sr - pi * si, vi + pr * si + pi * sr
        sh, k = sh * 2, k + 1
    st_r[...] = _shift_rows(vr, 1, down)
    st_i[...] = _shift_rows(vi, 1, down)


def _expand_block(u_ref, t_ref, bu_ref):
    for j in range(BD_TILES):
        k = j % 4
        bu_ref[:, j * BD_ST:(j + 1) * BD_ST] = _dot(u_ref[:, k * BD_CH:(k + 1) * BD_CH], t_ref[j])


def _s5_scan(u, tiles, ar8, ai8, pw_r, pw_i, name):
    n = u.shape[0]
    rb = SCAN_ROWS
    nb, steps, lc = n // rb, rb // SEG, 512

    def body(u_ref, t_ref, ar_ref, ai_ref, pwr_ref, pwi_ref, x_ref, st_r, st_i, bu_ref):
        ph, b = pl.program_id(0), pl.program_id(1)

        @pl.when((ph == 0) & (b == 0))
        def _():
            st_r[...] = jnp.zeros_like(st_r)
            st_i[...] = jnp.zeros_like(st_i)

        _expand_block(u_ref, t_ref, bu_ref)

        def scan(store):
            for c in range(S5_GP // lc):
                re, im = slice(c * lc, (c + 1) * lc), slice(S5_GP + c * lc, S5_GP + (c + 1) * lc)
                a_r, a_i = ar_ref[:, re], ai_ref[:, re]

                def step(s, carry):
                    xr, xi = carry
                    rows = pl.ds(pl.multiple_of(s * SEG, SEG), SEG)
                    nr = a_r * xr - a_i * xi + bu_ref[rows, re]
                    ni = a_r * xi + a_i * xr + bu_ref[rows, im]
                    if store:
                        x_ref[rows, re] = nr
                        x_ref[rows, im] = ni
                    return nr, ni

                xr, xi = lax.fori_loop(0, steps, step, (st_r[:, re], st_i[:, re]), unroll=4)
                st_r[:, re] = xr
                st_i[:, re] = xi

        @pl.when(ph == 0)
        def _():
            scan(False)

        @pl.when((ph == 0) & (b == nb - 1))
        def _():
            _chain_segments(st_r, st_i, pwr_ref, pwi_ref, conj=False, down=True)

        @pl.when(ph == 1)
        def _():
            scan(True)

    full = lambda a: pl.BlockSpec(a.shape, lambda ph, b: (0, 0))
    return _pcall(body, name=name, grid=(2, nb),
                  in_specs=[pl.BlockSpec((rb, S5_W), lambda ph, b: (b, 0)), pl.BlockSpec(tiles.shape, lambda ph, b: (0, 0, 0)),
                            full(ar8), full(ai8), full(pw_r), full(pw_i)],
                  out_specs=pl.BlockSpec((rb, 2 * S5_GP), lambda ph, b: (b * ph, 0)),
                  out_shape=jax.ShapeDtypeStruct((n, 2 * S5_GP), F32),
                  scratch_shapes=[pltpu.VMEM((SEG, S5_GP), F32), pltpu.VMEM((SEG, S5_GP), F32),
                                  pltpu.VMEM((rb, 2 * S5_GP), F32)],
                  compiler_params=_params())(u, tiles, ar8, ai8, pw_r, pw_i)


def _s5_scan_bwd(dy, tiles, xs, ar8, ai8, pw_r, pw_i, name):
    n = dy.shape[0]
    rb = SCAN_ROWS
    nb, steps, lc = n // rb, rb // SEG, 256

    def body(dy_ref, t_ref, x_ref, ar_ref, ai_ref, pwr_ref, pwi_ref, lam_ref, da_ref, st_r, st_i, gx_ref):
        ph, b = pl.program_id(0), pl.program_id(1)

        @pl.when((ph == 0) & (b == 0))
        def _():
            st_r[...] = jnp.zeros_like(st_r)
            st_i[...] = jnp.zeros_like(st_i)
            da_ref[...] = jnp.zeros_like(da_ref)

        _expand_block(dy_ref, t_ref, gx_ref)

        def scan(store):
            for c in range(S5_GP // lc):
                re, im = slice(c * lc, (c + 1) * lc), slice(S5_GP + c * lc, S5_GP + (c + 1) * lc)
                a_r, a_i = ar_ref[:, re], ai_ref[:, re]

                def step(s, carry):
                    rows = pl.ds(pl.multiple_of((steps - 1 - s) * SEG, SEG), SEG)
                    if store:
                        lr, li, dr, di = carry
                        xr, xi = x_ref[rows, re], x_ref[rows, im]
                        dr = dr + lr * xr + li * xi
                        di = di + li * xr - lr * xi
                    else:
                        lr, li = carry
                    nr = a_r * lr + a_i * li + gx_ref[rows, re]
                    ni = a_r * li - a_i * lr + gx_ref[rows, im]
                    if store:
                        lam_ref[rows, re] = nr
                        lam_ref[rows, im] = ni
                        return nr, ni, dr, di
                    return nr, ni

                if store:
                    lr, li, dr, di = lax.fori_loop(0, steps, step, (st_r[:, re], st_i[:, re], da_ref[:, re], da_ref[:, im]),
                                                   unroll=4)
                    da_ref[:, re] = dr
                    da_ref[:, im] = di
                else:
                    lr, li = lax.fori_loop(0, steps, step, (st_r[:, re], st_i[:, re]), unroll=4)
                st_r[:, re] = lr
                st_i[:, re] = li

        @pl.when(ph == 0)
        def _():
            scan(False)

        @pl.when((ph == 0) & (b == nb - 1))
        def _():
            _chain_segments(st_r, st_i, pwr_ref, pwi_ref, conj=True, down=False)

        @pl.when(ph == 1)
        def _():
            scan(True)

    full = lambda a: pl.BlockSpec(a.shape, lambda ph, b: (0, 0))
    rev = lambda ph, b: (nb - 1 - b, 0)
    return _pcall(body, name=name, grid=(2, nb),
                  in_specs=[pl.BlockSpec((rb, S5_W), rev), pl.BlockSpec(tiles.shape, lambda ph, b: (0, 0, 0)),
                            pl.BlockSpec((rb, 2 * S5_GP), lambda ph, b: ((nb - 1 - b) * ph, 0)),
                            full(ar8), full(ai8), full(pw_r), full(pw_i)],
                  out_specs=[pl.BlockSpec((rb, 2 * S5_GP), lambda ph, b: (nb - 1 - b * ph, 0)),
                             pl.BlockSpec((SEG, 2 * S5_GP), lambda ph, b: (0, 0))],
                  out_shape=[jax.ShapeDtypeStruct((n, 2 * S5_GP), F32), jax.ShapeDtypeStruct((SEG, 2 * S5_GP), F32)],
                  scratch_shapes=[pltpu.VMEM((SEG, S5_GP), F32), pltpu.VMEM((SEG, S5_GP), F32),
                                  pltpu.VMEM((rb, 2 * S5_GP), F32)],
                  compiler_params=_params())(dy, tiles, xs, ar8, ai8, pw_r, pw_i)


def _s5_discretize(lam_re, lam_im, log_dt, b_re, b_im):
    dt = jnp.exp(log_dt)[:, None]
    mag = jnp.exp(lam_re * dt)
    ar = mag * jnp.cos(lam_im * dt)
    ai = mag * jnp.sin(lam_im * dt)
    den = lam_re * lam_re + lam_im * lam_im
    nr = ar - 1.0
    fr = (nr * lam_re + ai * lam_im) / den
    fi = (ai * lam_re - nr * lam_im) / den
    bbar_re = fr[:, :, None] * b_re - fi[:, :, None] * b_im
    bbar_im = fr[:, :, None] * b_im + fi[:, :, None] * b_re
    return ar, ai, bbar_re, bbar_im


BD_TILES, BD_CH, BD_ST, BD_GROUPS = 8, 128, 512, 8
BD_ROWS = 4096


def _bd_tiles(re, im):
    eye = jnp.eye(BD_GROUPS, dtype=re.dtype)

    def tiles(t):
        t = t.reshape(S5_G // BD_GROUPS, BD_GROUPS, S5_H, S5_P)
        return (t[:, :, :, None, :] * eye[None, :, None, :, None]).reshape(S5_G // BD_GROUPS, BD_CH, BD_ST)

    return jnp.concatenate([tiles(re), tiles(im)], axis=0)


def _bd_blocks(t):
    t = t.reshape(2, S5_G // BD_GROUPS, BD_GROUPS, S5_H, BD_GROUPS, S5_P)
    return jnp.einsum('rkahap->rkahp', t).reshape(2, S5_G, S5_H, S5_P)


def _bd_reduce(x, t, res, name):
    n = x.shape[0]
    tm = _pick(n, BD_ROWS, 16)

    def body(x_ref, t_ref, r_ref, o_ref):
        part = _dot(x_ref[...], t_ref[...], NT)

        @pl.when(pl.program_id(2) == 0)
        def _():
            o_ref[...] = r_ref[...] + part

        @pl.when(pl.program_id(2) == 1)
        def _():
            o_ref[...] += part

    return _pcall(body, name=name, grid=(n // tm, 4, 2),
                  in_specs=[pl.BlockSpec((tm, BD_ST), lambda i, k, r: (i, k + 4 * r)),
                            pl.BlockSpec((None, BD_CH, BD_ST), lambda i, k, r: (k + 4 * r, 0, 0)),
                            pl.BlockSpec((tm, BD_CH), lambda i, k, r: (i, k))],
                  out_specs=pl.BlockSpec((tm, BD_CH), lambda i, k, r: (i, k)),
                  out_shape=jax.ShapeDtypeStruct((n, S5_W), F32), compiler_params=_params())(x, t, res)


def _bd_outer(a, x, name):
    n = a.shape[0]
    tk = _pick(n, BD_ROWS, 16)
    nk = n // tk

    def body(a_ref, x_ref, o_ref):
        part = _dot(a_ref[...], x_ref[...], TN)

        @pl.when(pl.program_id(1) == 0)
        def _():
            o_ref[...] = part

        @pl.when(pl.program_id(1) > 0)
        def _():
            o_ref[...] += part

    return _pcall(body, name=name, grid=(BD_TILES, nk),
                  in_specs=[pl.BlockSpec((tk, BD_CH), lambda j, kk: (kk, j % 4)), pl.BlockSpec((tk, BD_ST), lambda j, kk: (kk, j))],
                  out_specs=pl.BlockSpec((None, BD_CH, BD_ST), lambda j, kk: (j, 0, 0)),
                  out_shape=jax.ShapeDtypeStruct((BD_TILES, BD_CH, BD_ST), F32), compiler_params=_params())(a, x)


def _permute_rows(t):
    n = t.shape[0]
    return t.reshape(SEG, n // SEG, t.shape[1]).transpose(1, 0, 2).reshape(n, t.shape[1])


def _unpermute_rows(t):
    n = t.shape[0]
    return t.reshape(n // SEG, SEG, t.shape[1]).transpose(1, 0, 2).reshape(n, t.shape[1])


def _segment_powers(ar, ai, seg_steps):
    pr, pi = ar.reshape(1, S5_GP), ai.reshape(1, S5_GP)
    e = 1
    while e < seg_steps:
        pr, pi = pr * pr - pi * pi, 2.0 * pr * pi
        e *= 2
    assert e == seg_steps, "segment length must be a power of two"
    rows_r, rows_i = [], []
    for _ in range(3):
        rows_r.append(pr)
        rows_i.append(pi)
        pr, pi = pr * pr - pi * pi, 2.0 * pr * pi
    pad = jnp.zeros((SEG - 3, S5_GP), F32)
    return jnp.concatenate(rows_r + [pad], axis=0), jnp.concatenate(rows_i + [pad], axis=0)


NT = (((1,), (1,)), ((), ()))
TN = (((0,), (0,)), ((), ()))


def _dot(a, b, dims=None, exact=False):
    dims = (((1,), (0,)), ((), ())) if dims is None else dims
    if exact:
        return lax.dot_general(a, b, dims, precision=HI, preferred_element_type=F32)
    return lax.dot_general(a.astype(BF16), b.astype(BF16), dims, preferred_element_type=F32)


def _dot01(a, b, dims=None, ones_first=True):
    x = b if ones_first else a
    hi = x.astype(BF16)
    r1 = x - hi.astype(F32)
    mid = r1.astype(BF16)
    lo = (r1 - mid.astype(F32)).astype(BF16)
    parts = [(_dot(a, p, dims) if ones_first else _dot(p, b, dims)) for p in (lo, mid, hi)]
    return (parts[0] + parts[1]) + parts[2]


HEADS = range(4)


def _gla_chunk_fwd(qc, kc, vc, al, wup, bup, s_prev, tril):
    ones = jnp.ones((GLA_CHUNK, GLA_DV), F32)
    z = [_dot(al, wup[h]) + bup[h] for h in HEADS]
    la = [(jnp.minimum(z[h], 0.0) - jnp.log(1.0 + jnp.exp(-jnp.abs(z[h])))) * (1.0 / GLA_TAU) for h in HEADS]
    bc = [_dot01(tril, la[h]) for h in HEADS]
    blb = [_dot01(la[h], ones, TN, ones_first=False) for h in HEADS]
    bl = [bc[h][GLA_CHUNK - 1:GLA_CHUNK, :] for h in HEADS]
    ebc = [jnp.exp(bc[h]) for h in HEADS]
    qt = [qc[h] * (GLA_DK ** -0.5) * ebc[h] for h in HEADS]
    kt = [kc[h] * jnp.exp(-bc[h]) for h in HEADS]
    ke = [kc[h] * jnp.exp(bl[h] - bc[h]) for h in HEADS]
    sc = [_dot(qt[h], kt[h], NT) * tril for h in HEADS]
    oi = [_dot(sc[h], vc[h]) for h in HEADS]
    oo = [_dot(qt[h], s_prev[h]) for h in HEADS]
    o = [oi[h] + oo[h] for h in HEADS]
    return z, bc, bl, blb, ebc, qt, kt, ke, sc, o


GLA_ROWS = 512
GLA_CPB = GLA_ROWS // GLA_CHUNK


ZA_COLS = 5 * 512
SLOT = 128


def _pad_heads(w):
    r = w.shape[0]
    return jnp.pad(w.reshape(r, GLA_HEADS, GLA_DK), ((0, 0), (0, 0), (0, SLOT - GLA_DK))).reshape(r, GLA_HEADS * SLOT)


def _unpad_heads(w):
    r = w.shape[0]
    return w.reshape(r, GLA_HEADS, SLOT)[:, :, :GLA_DK].reshape(r, GLA_HEADS * GLA_DK)


def _gla_token_specs(blk):
    col = lambda cb: pl.BlockSpec((GLA_ROWS, 512), lambda j: (blk(j), cb))
    whole = lambda a: pl.BlockSpec(a.shape, lambda j: (0,) * a.ndim)
    return col, whole


def _head_ds(h, width):
    return pl.ds(h * SLOT, width)


def _tri(lower):
    ri = lax.broadcasted_iota(jnp.int32, (GLA_CHUNK, GLA_CHUNK), 0)
    ci = lax.broadcasted_iota(jnp.int32, (GLA_CHUNK, GLA_CHUNK), 1)
    return ((ri >= ci) if lower else (ri <= ci)).astype(F32)


def _gla_fwd(za, al, wup, bup, gn, name):
    n = za.shape[0]
    nc = n // GLA_CHUNK

    def body(q_ref, k_ref, v_ref, r_ref, al_ref, wup_ref, bup_ref, gn_ref, y_ref, sp_ref, s_ref):
        @pl.when(pl.program_id(0) == 0)
        def _():
            s_ref[...] = jnp.zeros_like(s_ref)

        tril = _tri(True)

        def chunk(c, carry):
            rows = pl.ds(pl.multiple_of(c * GLA_CHUNK, GLA_CHUNK), GLA_CHUNK)
            alc = al_ref[rows, :]
            vc = [v_ref[rows, _head_ds(h, GLA_DV)] for h in HEADS]
            s_prev = [s_ref[h] for h in HEADS]
            _, _, _, blb, _, _, _, ke, _, o = _gla_chunk_fwd(
                [q_ref[rows, _head_ds(h, GLA_DK)] for h in HEADS], [k_ref[rows, _head_ds(h, GLA_DK)] for h in HEADS],
                vc, alc, [wup_ref[h] for h in HEADS], [bup_ref[h] for h in HEADS], s_prev, tril)
            ds = [_dot(ke[h], vc[h], TN) for h in HEADS]
            for h in HEADS:
                rc = r_ref[rows, _head_ds(h, GLA_DV)]
                sp_ref[h, c] = s_prev[h]
                rstd = lax.rsqrt(jnp.mean(o[h] * o[h], axis=-1, keepdims=True) + EPS)
                y_ref[rows, _head_ds(h, GLA_DV)] = (o[h] * rstd * gn_ref[h] * (rc * _sigmoid(rc))).astype(BF16)
                s_ref[h] = jnp.exp(blb[h]) * s_prev[h] + ds[h]
            return carry

        lax.fori_loop(0, GLA_CPB, chunk, 0)

    col, whole = _gla_token_specs(lambda j: j)
    return _pcall(body, name=name, grid=(n // GLA_ROWS,),
                  in_specs=[col(1), col(2), col(3), col(4), pl.BlockSpec((GLA_ROWS, LANE), lambda j: (j, 0)),
                            whole(wup), whole(bup), whole(gn)],
                  out_specs=[pl.BlockSpec((GLA_ROWS, GLA_HEADS * GLA_DV), lambda j: (j, 0)),
                             pl.BlockSpec((GLA_HEADS, GLA_CPB, GLA_DK, GLA_DV), lambda j: (0, j, 0, 0))],
                  out_shape=[jax.ShapeDtypeStruct((n, GLA_HEADS * GLA_DV), BF16),
                             jax.ShapeDtypeStruct((GLA_HEADS, nc, GLA_DK, GLA_DV), F32)],
                  scratch_shapes=[pltpu.VMEM((GLA_HEADS, GLA_DK, GLA_DV), F32)],
                  compiler_params=_params())(za, za, za, za, al, wup, bup, gn)


def _gla_bwd(za, al, wup, bup, gn, sp, dy, du_s5, name):
    n = za.shape[0]
    nb = n // GLA_ROWS

    def body(q_ref, k_ref, v_ref, r_ref, al_ref, wup_ref, bup_ref, gn_ref, dy_ref, dus_ref, sp_ref,
             dza_ref, dz_ref, dgn_ref, dbup_ref, ds_ref):
        @pl.when(pl.program_id(0) == 0)
        def _():
            ds_ref[...] = jnp.zeros_like(ds_ref)
            dgn_ref[...] = jnp.zeros_like(dgn_ref)
            dbup_ref[...] = jnp.zeros_like(dbup_ref)

        tril, triu = _tri(True), _tri(False)
        dza_ref[:, 0:512] = dus_ref[...]
        dza_ref[:, 512:1536] = jnp.zeros((GLA_ROWS, 1024), F32)
        dz_ref[...] = jnp.zeros_like(dz_ref)

        def chunk(i, carry):
            c = GLA_CPB - 1 - i
            rows = pl.ds(pl.multiple_of(c * GLA_CHUNK, GLA_CHUNK), GLA_CHUNK)
            alc = al_ref[rows, :]
            qc = [q_ref[rows, _head_ds(h, GLA_DK)] for h in HEADS]
            kc = [k_ref[rows, _head_ds(h, GLA_DK)] for h in HEADS]
            vc = [v_ref[rows, _head_ds(h, GLA_DV)] for h in HEADS]
            s_prev = [sp_ref[h, c] for h in HEADS]
            ds = [ds_ref[h] for h in HEADS]
            z, bc, bl, blb, ebc, qt, kt, ke, sc, o = _gla_chunk_fwd(
                qc, kc, vc, alc, [wup_ref[h] for h in HEADS], [bup_ref[h] for h in HEADS], s_prev, tril)
            do = []
            for h in HEADS:
                rc = r_ref[rows, _head_ds(h, GLA_DV)]
                rs = lax.rsqrt(jnp.mean(o[h] * o[h], axis=-1, keepdims=True) + EPS)
                on = o[h] * rs
                sr = _sigmoid(rc)
                sil = rc * sr
                dyv, gnv = dy_ref[rows, _head_ds(h, GLA_DV)], gn_ref[h]
                dgn_ref[h] += jnp.sum(dyv * on * sil, axis=0, keepdims=True)
                dza_ref[rows, pl.ds(2048 + h * SLOT, GLA_DV)] = dyv * on * gnv * (sr * (1.0 + rc * (1.0 - sr)))
                don = dyv * gnv * sil
                do.append(rs * (don - on * jnp.mean(don * on, axis=-1, keepdims=True)))
            dp = [_dot(do[h], vc[h], NT) * tril for h in HEADS]
            dv1 = [_dot(sc[h], do[h], TN) for h in HEADS]
            dv2 = [_dot(ke[h], ds[h]) for h in HEADS]
            dq2 = [_dot(do[h], s_prev[h], NT) for h in HEADS]
            dke = [_dot(vc[h], ds[h], NT) for h in HEADS]
            ddec = [_dot01(jnp.ones((8, GLA_DV), F32), ds[h] * s_prev[h], NT)[0:1, :] for h in HEADS]
            dsn = [_dot(qt[h], do[h], TN) for h in HEADS]
            dq1 = [_dot(dp[h], kt[h]) for h in HEADS]
            dkt = [_dot(dp[h], qt[h], TN) for h in HEADS]
            dbc, dbl = [], []
            for h in HEADS:
                dqt = dq1[h] + dq2[h]
                dza_ref[rows, pl.ds(1536 + h * SLOT, GLA_DV)] = dv1[h] + dv2[h]
                ds_ref[h] = jnp.exp(blb[h]) * ds[h] + dsn[h]
                dza_ref[rows, pl.ds(512 + h * SLOT, GLA_DK)] = dqt * (GLA_DK ** -0.5) * ebc[h]
                dza_ref[rows, pl.ds(1024 + h * SLOT, GLA_DK)] = dkt[h] * jnp.exp(-bc[h]) + dke[h] * jnp.exp(bl[h] - bc[h])
                dbc.append(dqt * qt[h] - dkt[h] * kt[h] - dke[h] * ke[h])
                dbl.append(jnp.sum(dke[h] * ke[h], axis=0, keepdims=True) + ddec[h] * jnp.exp(bl[h]))
            dla = [_dot01(triu, dbc[h]) + dbl[h] for h in HEADS]
            for h in HEADS:
                dz = dla[h] * (1.0 - _sigmoid(z[h])) * (1.0 / GLA_TAU)
                dz_ref[rows, _head_ds(h, GLA_DK)] = dz
                dbup_ref[h] += jnp.sum(dz, axis=0, keepdims=True)
            return carry

        lax.fori_loop(0, GLA_CPB, chunk, 0)

    rev = lambda j: nb - 1 - j
    col, whole = _gla_token_specs(rev)
    tok = lambda w: pl.BlockSpec((GLA_ROWS, w), lambda j: (rev(j), 0))
    h1 = lambda w: pl.BlockSpec((GLA_HEADS, 1, w), lambda j: (0, 0, 0))
    s1 = lambda w: jax.ShapeDtypeStruct((GLA_HEADS, 1, w), F32)
    return _pcall(body, name=name, grid=(nb,),
                  in_specs=[col(1), col(2), col(3), col(4), tok(LANE), whole(wup), whole(bup), whole(gn), tok(512), tok(512),
                            pl.BlockSpec((GLA_HEADS, GLA_CPB, GLA_DK, GLA_DV), lambda j: (0, rev(j), 0, 0))],
                  out_specs=[tok(ZA_COLS), tok(GLA_HEADS * SLOT), h1(GLA_DV), h1(GLA_DK)],
                  out_shape=[jax.ShapeDtypeStruct((n, ZA_COLS), F32), jax.ShapeDtypeStruct((n, GLA_HEADS * SLOT), F32),
                             s1(GLA_DV), s1(GLA_DK)],
                  scratch_shapes=[pltpu.VMEM((GLA_HEADS, GLA_DK, GLA_DV), F32)],
                  compiler_params=_params())(za, za, za, za, al, wup, bup, gn, dy, du_s5, sp)


ANY = pl.BlockSpec(memory_space=pl.ANY)


def _place():
    x, y, c = lax.axis_index("x"), lax.axis_index("y"), lax.axis_index("c")
    chips = [(1 - x, y), (x, 1 - y), (1 - x, 1 - y)]
    return x, y, c, chips


def _remote(src, dst, ssem, rsem, dev):
    return pltpu.make_async_remote_copy(src_ref=src, dst_ref=dst, send_sem=ssem, recv_sem=rsem, device_id=dev,
                                        device_id_type=MESH_ID)


def _half(c, rows):
    h = rows // 2
    return pl.ds(pl.multiple_of(c * h, 8), h)


def _side_gather_ici(shards):
    def copies(ins, outs, ssem, rsem):
        x, y, c, chips = _place()
        mine = 2 * x + y
        cps = []
        for w in range(len(ins)):
            half = _half(c, ins[w].shape[0])
            cps.append(_remote(ins[w], outs[w].at[mine], ssem.at[4 * w], rsem.at[4 * w], (x, y, 1 - c)))
            for k, (px, py) in enumerate(chips):
                cps.append(_remote(ins[w].at[half], outs[w].at[mine, half], ssem.at[4 * w + 1 + k], rsem.at[4 * w + 1 + k],
                                   (px, py, c)))
        return cps

    return _Side(shards, [jax.ShapeDtypeStruct((4,) + s.shape, s.dtype) for s in shards], 4 * len(shards), copies)


def _side_gather_d2d(gathered):
    def copies(ins, outs, ssem, rsem):
        x, y, c, chips = _place()
        cps = []
        for w in range(len(outs)):
            half = _half(c, outs[w].shape[1])
            for k, (px, py) in enumerate(chips):
                theirs = outs[w].at[2 * px + py, half]
                cps.append(_remote(theirs, theirs, ssem.at[3 * w + k], rsem.at[3 * w + k], (x, y, 1 - c)))
        return cps

    return _Side(gathered, [jax.ShapeDtypeStruct(g.shape, g.dtype) for g in gathered], 3 * len(gathered), copies,
                 aliased=True)


def _side_swap_halves(grads):
    def copies(ins, outs, ssem, rsem):
        x, y, c, _ = _place()
        return [_remote(ins[w].at[:, _half(1 - c, ins[w].shape[1]), :], outs[w], ssem.at[w], rsem.at[w], (x, y, 1 - c))
                for w in range(len(ins))]

    return _Side(grads, [jax.ShapeDtypeStruct((4, g.shape[1] // 2, g.shape[2]), g.dtype) for g in grads], len(grads), copies)


def _side_scatter(sums):
    def copies(ins, outs, ssem, rsem):
        x, y, c, chips = _place()
        return [_remote(ins[w].at[2 * px + py], outs[w].at[k], ssem.at[3 * w + k], rsem.at[3 * w + k], (px, py, c))
                for w in range(len(ins)) for k, (px, py) in enumerate(chips)]

    return _Side(sums, [jax.ShapeDtypeStruct((3,) + s.shape[1:], s.dtype) for s in sums], 3 * len(sums), copies)


def _side_swap_reduced(halves):
    def copies(ins, outs, ssem, rsem):
        x, y, c, _ = _place()
        return [_remote(ins[w], outs[w], ssem.at[w], rsem.at[w], (x, y, 1 - c)) for w in range(len(ins))]

    return _Side(halves, [jax.ShapeDtypeStruct(h.shape, h.dtype) for h in halves], len(halves), copies)


def _chip_sum(g, recv, c_arr, name):
    _, r, cols = g.shape
    h = r // 2
    tr = _pick(h, 512, 16)
    g4 = g.reshape(4, 2, h, cols)

    def body(c_ref, g_ref, r_ref, o_ref):
        o_ref[...] = (g_ref[...] + r_ref[...]).astype(BF16)

    grid_spec = pltpu.PrefetchScalarGridSpec(
        num_scalar_prefetch=1, grid=(4, h // tr),
        in_specs=[pl.BlockSpec((None, None, tr, cols), lambda s, i, c_ref: (s, c_ref[0], i, 0)),
                  pl.BlockSpec((None, tr, cols), lambda s, i, c_ref: (s, i, 0))],
        out_specs=pl.BlockSpec((None, tr, cols), lambda s, i, c_ref: (s, i, 0)))
    return _pcall(body, name=name, grid_spec=grid_spec, out_shape=jax.ShapeDtypeStruct((4, h, cols), BF16),
                  compiler_params=_params())(c_arr, g4, recv)


def _owner_sum(sums, others, s_arr, name):
    _, h, cols = sums.shape
    tr = _pick(h, 512, 16)

    def body(s_ref, a_ref, o_ref, out_ref):
        f = lambda v: v.astype(F32)
        out_ref[...] = (f(a_ref[...]) + f(o_ref[0])) + (f(o_ref[1]) + f(o_ref[2]))

    grid_spec = pltpu.PrefetchScalarGridSpec(
        num_scalar_prefetch=1, grid=(h // tr,),
        in_specs=[pl.BlockSpec((None, tr, cols), lambda i, s_ref: (s_ref[0], i, 0)),
                  pl.BlockSpec((3, tr, cols), lambda i, s_ref: (0, i, 0))],
        out_specs=pl.BlockSpec((tr, cols), lambda i, s_ref: (i, 0)))
    return _pcall(body, name=name, grid_spec=grid_spec, out_shape=jax.ShapeDtypeStruct((h, cols), F32),
                  compiler_params=_params())(s_arr, sums, others)


def _side_small_sibling(v):
    def copies(ins, outs, ssem, rsem):
        x, y, c, _ = _place()
        return [_remote(ins[0], outs[0], ssem.at[0], rsem.at[0], (x, y, 1 - c))]

    return _Side([v], [jax.ShapeDtypeStruct(v.shape, F32)], 1, copies)


def _side_small_chips(v):
    def copies(ins, outs, ssem, rsem):
        x, y, c, chips = _place()
        return [_remote(ins[0], outs[0].at[k], ssem.at[k], rsem.at[k], (px, py, c)) for k, (px, py) in enumerate(chips)]

    return _Side([v], [jax.ShapeDtypeStruct((3,) + v.shape, F32)], 3, copies)


def _small_add(v, r, name):
    def body(v_ref, r_ref, o_ref):
        if r.ndim == 2:
            o_ref[...] = v_ref[...] + r_ref[...]
        else:
            o_ref[...] = (v_ref[...] + r_ref[0]) + (r_ref[1] + r_ref[2])

    vm = pl.BlockSpec(memory_space=pltpu.VMEM)
    return _pcall(body, name=name, in_specs=[vm, vm], out_specs=vm, out_shape=jax.ShapeDtypeStruct(v.shape, F32),
                  compiler_params=_params())(v, r)


def _merge_sides(sides):
    if len(sides) == 1:
        return sides[0]

    def copies(in_refs, out_refs, ssem, rsem):
        cps, i, o, q = [], 0, 0, 0
        for s in sides:
            ni, no = len(s.ins), len(s.out_shapes)
            cps += s.copies(in_refs[i:i + ni], out_refs[o:o + no], ssem.at[pl.ds(q, s.nsem)], rsem.at[pl.ds(q, s.nsem)])
            i, o, q = i + ni, o + no, q + s.nsem
        return cps

    assert not any(s.aliased for s in sides)
    return _Side(sum((s.ins for s in sides), []), sum((s.out_shapes for s in sides), []), sum(s.nsem for s in sides), copies)


def _tile_rows(size):
    return -(-size // (8 * LANE)) * 8


def _pack_small(parts):
    pieces = []
    for p in parts:
        flat = p.reshape(-1).astype(F32)
        pieces.append(jnp.pad(flat, (0, _tile_rows(p.size) * LANE - p.size)).reshape(-1, LANE))
    rows = sum(x.shape[0] for x in pieces)
    pieces.append(jnp.zeros(((-rows) % 64, LANE), F32))
    return jnp.concatenate(pieces, axis=0)


def _unpack_small(packed, like):
    out, pos = [], 0
    for p in like:
        rows = _tile_rows(p.size)
        out.append(packed[pos:pos + rows].reshape(-1)[:p.size].reshape(p.shape))
        pos += rows
    return out


FFN_FWD_ROWS, FFN_BWD_ROWS = 1024, 512
FFN_SUB_ROWS = 256


def _ffn_specs(n, d, fs, cap):
    rows = _pick(n, cap, 16)
    row = pl.BlockSpec((rows, d), lambda i, s: (i, 0))
    gain = pl.BlockSpec((1, d), lambda i, s: (0, 0))
    w_row = pl.BlockSpec((None, fs, d), lambda i, s: (s, 0, 0))
    hid = pl.BlockSpec((None, rows, fs), lambda i, s: (s, i, 0))
    return rows, row, gain, w_row, hid


def _ffn_fwd(h, g, w1t, w3t, w2, tag, plan):
    n, d = h.shape
    ns, fs, _ = w2.shape
    rows, row, gain, w_row, hid = _ffn_specs(n, d, fs, FFN_FWD_ROWS)
    sub = rows

    def body(h_ref, g_ref, w1_ref, w3_ref, w2_ref, out_ref, n1_ref, a_ref, b_ref, hm_ref, acc_ref):
        s = pl.program_id(1)

        @pl.when(s == 0)
        def _():
            xv = h_ref[...]
            rstd = lax.rsqrt(jnp.mean(xv * xv, axis=-1, keepdims=True) + EPS)
            n1_ref[...] = (xv * rstd * g_ref[...]).astype(BF16)
            acc_ref[...] = jnp.zeros_like(acc_ref)

        def up(j):
            n1 = n1_ref[j * sub:(j + 1) * sub, :]
            return _dot(n1, w1_ref[...], NT), _dot(n1, w3_ref[...], NT)

        cur = up(0)
        for j in range(rows // sub):
            nxt = up(j + 1) if (j + 1) * sub < rows else None
            a, b = cur
            r = slice(j * sub, (j + 1) * sub)
            hm = (a * _sigmoid(a) * b).astype(BF16)
            a_ref[r, :] = a.astype(BF16)
            b_ref[r, :] = b.astype(BF16)
            hm_ref[r, :] = hm
            acc_ref[r, :] += _dot(hm, w2_ref[...])
            cur = nxt

        @pl.when(s == ns - 1)
        def _():
            out_ref[...] = h_ref[...] + 0.5 * acc_ref[...]

    hid_shape = jax.ShapeDtypeStruct((ns, n, fs), BF16)
    plan.before(f"{tag}_fwd")
    out, n1, a, b, hm = _pcall(
        body, name=f"{tag}_fwd", grid=(n // rows, ns), in_specs=[row, gain, w_row, w_row, w_row],
        out_specs=[row, row, hid, hid, hid],
        out_shape=[jax.ShapeDtypeStruct((n, d), F32), jax.ShapeDtypeStruct((n, d), BF16), hid_shape, hid_shape, hid_shape],
        scratch_shapes=[pltpu.VMEM((rows, d), F32)], compiler_params=_params())(h, g, w1t, w3t, w2)
    plan.after(f"{tag}_fwd")
    return out, (h, n1, a, b, hm)


def _ffn_bwd(dout, saved, g, w1, w3, w2, tag, plan):
    h, n1, a, b, hm = saved
    n, d = h.shape
    ns, fs, _ = w2.shape
    rows, row, gain, w_row, hid = _ffn_specs(n, d, fs, FFN_BWD_ROWS)
    sub = _pick(rows, FFN_SUB_ROWS, 16)

    def body(do_ref, h_ref, g_ref, a_ref, b_ref, w1_ref, w3_ref, w2_ref, dh_ref, da_ref, db_ref, dg_ref, acc_ref):
        i, s = pl.program_id(0), pl.program_id(1)

        @pl.when(s == 0)
        def _():
            acc_ref[...] = jnp.zeros_like(acc_ref)

        @pl.when((s == 0) & (i == 0))
        def _():
            dg_ref[...] = jnp.zeros_like(dg_ref)

        def up(j):
            return _dot(0.5 * do_ref[j * sub:(j + 1) * sub, :], w2_ref[...], NT)

        cur = up(0)
        for j in range(rows // sub):
            nxt = up(j + 1) if (j + 1) * sub < rows else None
            r = slice(j * sub, (j + 1) * sub)
            av, bv = a_ref[r, :].astype(F32), b_ref[r, :].astype(F32)
            sg = _sigmoid(av)
            da = (cur * bv * (sg * (1.0 + av * (1.0 - sg)))).astype(BF16)
            db = (cur * av * sg).astype(BF16)
            da_ref[r, :] = da
            db_ref[r, :] = db
            acc_ref[r, :] += _dot(da, w1_ref[...]) + _dot(db, w3_ref[...])
            cur = nxt

        @pl.when(s == ns - 1)
        def _():
            xv, dn = h_ref[...], acc_ref[...]
            rstd = lax.rsqrt(jnp.mean(xv * xv, axis=-1, keepdims=True) + EPS)
            xh = xv * rstd
            dg_ref[...] += jnp.sum(dn * xh, axis=0, keepdims=True)
            dxh = dn * g_ref[...]
            dh_ref[...] = do_ref[...] + rstd * (dxh - xh * jnp.mean(dxh * xh, axis=-1, keepdims=True))

    hid_shape = jax.ShapeDtypeStruct((ns, n, fs), BF16)
    plan.before(f"{tag}_bwd")
    dh, da, db, dg = _pcall(
        body, name=f"{tag}_bwd", grid=(n // rows, ns), in_specs=[row, row, gain, hid, hid, w_row, w_row, w_row],
        out_specs=[row, hid, hid, gain],
        out_shape=[jax.ShapeDtypeStruct((n, d), F32), hid_shape, hid_shape, jax.ShapeDtypeStruct((1, d), F32)],
        scratch_shapes=[pltpu.VMEM((rows, d), F32)], compiler_params=_params())(dout, h, g, a, b, w1, w3, w2)
    plan.after(f"{tag}_bwd")
    plan.grads[f"{tag}_norm"] = dg
    plan.before(f"{tag}_gw2")
    gw2 = _mm(hm, dout, ta=True, shard='m', alpha=0.5, name=f"{tag}_gw2")
    plan.after(f"{tag}_gw2")
    plan.grads[f"{tag}_w2"] = gw2
    plan.before(f"{tag}_gw1")
    gw1 = _mm(da, n1, ta=True, shard='m', name=f"{tag}_gw1")
    plan.after(f"{tag}_gw1")
    plan.before(f"{tag}_gw3")
    gw3 = _mm(db, n1, ta=True, shard='m', name=f"{tag}_gw3")
    plan.after(f"{tag}_gw3")
    return dh, dg, gw1, gw3, gw2


def _local_step(x, tgt, plan):
    n = x.shape[0]
    grads = plan.grads

    def f(name):
        w = plan.get(name)
        return w.reshape(1, D_MODEL) if name.endswith('_norm') and name != 'gla_out_norm' else w

    def carried(tag, fn, *args, **kw):
        plan.before(tag)
        out = fn(*args, **kw)
        plan.after(tag)
        return out

    h1, ffn1 = _ffn_fwd(x, f('ffn1_norm'), f('ffn1_w1'), f('ffn1_w3'), f('ffn1_w2'), "ffn1", plan)
    u = carried("mix_rms", _rms_fwd, h1, f('mix_norm'), "mix_rms")
    w_in = f('w_in')
    w_a = jnp.concatenate([w_in[:, :512], _pad_heads(w_in[:, 512:768]), _pad_heads(w_in[:, 768:1024]), w_in[:, 1024:2048]],
                          axis=1)
    w_al = jnp.pad(w_in[:, 2048:2048 + GLA_RANK], ((0, 0), (0, LANE - GLA_RANK)))
    w_g = w_in[:, 2048 + GLA_RANK:]
    za = carried("in_a", _mm, u, w_a, name="in_a")
    zg = carried("in_g", _mm, u, w_g, name="in_g")
    al = _mm(u, w_al, name="in_al")
    ar, ai, bbar_re, bbar_im = _s5_discretize(f('s5_lambda_re'), f('s5_lambda_im'), f('s5_log_dt'), f('s5_b_re'), f('s5_b_im'))
    t_b = _bd_tiles(bbar_re.transpose(0, 2, 1), bbar_im.transpose(0, 2, 1)).astype(BF16)
    t_c = _bd_tiles(f('s5_c_re'), -f('s5_c_im')).astype(BF16)
    ar8 = jnp.broadcast_to(ar.reshape(1, S5_GP), (SEG, S5_GP))
    ai8 = jnp.broadcast_to(ai.reshape(1, S5_GP), (SEG, S5_GP))
    pw_r, pw_i = _segment_powers(ar, ai, n // SEG)
    dskip = f('s5_d').reshape(1, S5_W)
    u_s5 = _permute_rows(za[:, :S5_W])
    xs = _s5_scan(u_s5, t_b, ar8, ai8, pw_r, pw_i, "s5_scan")
    ys_p = _bd_reduce(xs, t_c, _scale_rows(u_s5, dskip, "s5_skip"), "s5_y")
    ys = _unpermute_rows(ys_p)
    zgelu = _gelu_fwd(ys, "s5_gelu")
    t_glu = _mm(zgelu, f('s5_glu_w'), bias=f('s5_glu_b').reshape(1, S5_W), name="s5_glu_t")
    y_s5 = _glu_fwd(zgelu, t_glu, "s5_glu")
    wup = jnp.pad(f('gla_a_up_w'), ((0, LANE - GLA_RANK), (0, 0)))
    wup_h = wup.reshape(LANE, GLA_HEADS, GLA_DK).transpose(1, 0, 2)
    bup_h = f('gla_a_up_b').reshape(GLA_HEADS, 1, GLA_DK)
    gn_h = f('gla_out_norm').reshape(GLA_HEADS, 1, GLA_DV)
    y_gla, s_prev = carried("gla_fwd", _gla_fwd, za, al, wup_h, bup_h, gn_h, "gla_fwd")
    ps = _mm(y_s5, f('proj_s5'), name="proj_s5")
    pg = carried("proj_gla", _mm, y_gla, f('proj_gla'), name="proj_gla")
    merged = _merge_fwd(zg, ps, pg, "merge")
    h2 = _mm(merged, f('w_out'), res=h1, name="w_out")
    h3, ffn2 = _ffn_fwd(h2, f('ffn2_norm'), f('ffn2_w1'), f('ffn2_w3'), f('ffn2_w2'), "ffn2", plan)
    loss, dh3, g_final = _final_loss(h3, f('final_norm').reshape(1, D_MODEL), tgt, "loss")
    grads['final_norm'] = g_final.reshape(D_MODEL)
    dh2, grads['ffn2_norm'], grads['ffn2_w1'], grads['ffn2_w3'], grads['ffn2_w2'] = _ffn_bwd(
        dh3, ffn2, f('ffn2_norm'), f('ffn2_w1'), f('ffn2_w3'), f('ffn2_w2'), "ffn2", plan)
    dm = _mm(dh2, f('w_out'), tb=True, name="d_merged")
    grads['w_out'] = _mm(merged, dh2, ta=True, name="g_w_out")
    dps, dpg, dzg = carried("d_merge", _merge_bwd, dm, zg, ps, pg, "d_merge")
    grads['proj_s5'] = _mm(y_s5, dps, ta=True, name="g_proj_s5")
    grads['proj_gla'] = _mm(y_gla, dpg, ta=True, name="g_proj_gla")
    dy_s5 = _mm(dps, f('proj_s5'), tb=True, name="d_y_s5")
    dy_gla = _mm(dpg, f('proj_gla'), tb=True, name="d_y_gla")
    dzgelu, dt_glu, g_glu_b = _glu_bwd1(dy_s5, zgelu, t_glu, "d_glu")
    grads['s5_glu_b'] = g_glu_b.reshape(S5_W)
    grads['s5_glu_w'] = _mm(zgelu, dt_glu, ta=True, name="g_glu_w")
    dzgelu = _mm(dt_glu, f('s5_glu_w'), tb=True, res=dzgelu, name="d_gelu")
    dys, du_skip, g_d = _glu_bwd2(_permute_rows(dzgelu), ys_p, u_s5, dskip, "d_s5_y")
    grads['s5_d'] = g_d.reshape(S5_G, S5_H)
    lam, da8 = _s5_scan_bwd(dys, t_c, xs, ar8, ai8, pw_r, pw_i, "s5_scan_bwd")
    g_c = _bd_blocks(_bd_outer(dys, xs, "g_s5_c"))
    grads['s5_c_re'], grads['s5_c_im'] = g_c[0], -g_c[1]
    g_b = _bd_blocks(_bd_outer(u_s5, lam, "g_s5_b")).transpose(0, 1, 3, 2)
    g_bbar_re, g_bbar_im = g_b[0], g_b[1]
    da = jnp.sum(da8, axis=0)
    g_ar, g_ai = da[:S5_GP].reshape(S5_G, S5_P), da[S5_GP:].reshape(S5_G, S5_P)
    _, disc_vjp = jax.vjp(_s5_discretize, f('s5_lambda_re'), f('s5_lambda_im'), f('s5_log_dt'), f('s5_b_re'), f('s5_b_im'))
    (grads['s5_lambda_re'], grads['s5_lambda_im'], grads['s5_log_dt'], grads['s5_b_re'],
     grads['s5_b_im']) = disc_vjp((g_ar, g_ai, g_bbar_re, g_bbar_im))
    du_s5 = _unpermute_rows(_bd_reduce(lam, t_b, du_skip, "d_s5_u"))
    dza, dz, dgn, dbup = carried("gla_bwd", _gla_bwd, za, al, wup_h, bup_h, gn_h, s_prev, dy_gla, du_s5, "gla_bwd")
    grads['gla_out_norm'] = dgn.reshape(GLA_HEADS * GLA_DV)
    grads['gla_a_up_b'] = dbup.reshape(GLA_HEADS * GLA_DK)
    grads['gla_a_up_w'] = _unpad_heads(_mm(al, dz, ta=True, name="g_a_up")[:GLA_RANK])
    dal = _mm(dz, _pad_heads(wup), tb=True, name="d_a_low")
    g_wa = _mm(u, dza, ta=True, name="g_in_a")
    g_wg = _mm(u, dzg, ta=True, name="g_in_g")
    g_wal = _mm(u, dal, ta=True, name="g_in_al")
    grads['w_in'] = jnp.concatenate([g_wa[:, :512], _unpad_heads(g_wa[:, 512:1024]), _unpad_heads(g_wa[:, 1024:1536]),
                                     g_wa[:, 1536:], g_wal[:, :GLA_RANK], g_wg], axis=1)
    du = carried("d_u_a", _mm, dza, w_a, tb=True, name="d_u_a")
    du = _mm(dzg, w_g, tb=True, res=du, name="d_u_g")
    du = _mm(dal, w_al, tb=True, res=du, name="d_u_al")
    dh1, g_mix = carried("d_mix_rms", _rms_bwd, h1, f('mix_norm'), du, dh2, "d_mix_rms")
    grads['mix_norm'] = g_mix
    dx, grads['ffn1_norm'], grads['ffn1_w1'], grads['ffn1_w3'], grads['ffn1_w2'] = _ffn_bwd(
        dh1, ffn1, f('ffn1_norm'), f('ffn1_w1'), f('ffn1_w3'), f('ffn1_w2'), "ffn1", plan)
    return loss[0, 0], dx


MIXER_WEIGHTS = ['w_in', 's5_glu_w', 'proj_s5', 'proj_gla', 'w_out', 'gla_a_up_w']
FFN1_WEIGHTS, FFN2_WEIGHTS = FFN_WEIGHTS[:3], FFN_WEIGHTS[3:]
TRANSPOSED = ['ffn1_w1', 'ffn1_w3', 'ffn2_w1', 'ffn2_w3']


def _local_shard(w, nm):
    return jnp.swapaxes(w, 1, 2)[0] if nm in TRANSPOSED else w[0]
FFN1_EARLY = ['ffn1_w2']
GRAD_GROUPS = {'ffn2': FFN2_WEIGHTS, 'mixer': ['w_out', 'proj_s5', 'proj_gla', 's5_glu_w', 'w_in'], 'ffn1': FFN1_WEIGHTS}


class _Plan:
    def __init__(self, a, c_arr, s_arr):
        self.a, self.c_arr, self.s_arr = a, c_arr, s_arr
        self.grads, self.weights, self.riding = {}, {}, {}
        self.g4s, self.chip_sums, self.halves, self.sib_halves = {}, {}, {}, {}
        for nm in SMALL:
            if nm != 'gla_a_up_w':
                self.weights[nm] = a[nm] if nm == 'final_norm' else a[nm][0]
        ici = _side_gather_ici(self._shards(FFN1_WEIGHTS))
        _run_side(ici, "gather_ffn1_ici")
        self._gathered(FFN1_WEIGHTS, _run_side(_side_gather_d2d(ici.outs), "gather_ffn1_d2d"))

    def _shards(self, names):
        return [_local_shard(self.a[nm], nm).astype(F32 if nm == 'gla_a_up_w' else BF16) for nm in names]

    def _gathered(self, names, arrs):
        for nm, g4 in zip(names, arrs):
            if nm in FFN_WEIGHTS:
                self.weights[nm] = g4
            elif nm in COL_SHARDED:
                self.weights[nm] = jnp.concatenate([g4[s] for s in range(4)], axis=1)
            else:
                self.weights[nm] = g4.reshape(4 * g4.shape[1], g4.shape[2])

    def get(self, name):
        return self.weights[name]

    def _shard_major(self, nm):
        g = self.grads[nm]
        if nm in FFN_WEIGHTS:
            return g
        if nm in COL_SHARDED:
            return jnp.stack(jnp.split(g, 4, axis=1))
        return g.reshape(4, g.shape[0] // 4, g.shape[1])

    def _schedule(self, tag):
        grp = GRAD_GROUPS
        gathers = {"ffn1_fwd": ('ici', MIXER_WEIGHTS), "mix_rms": ('d2d', MIXER_WEIGHTS),
                   "in_a": ('ici', FFN2_WEIGHTS[:1]), "in_g": ('d2d', FFN2_WEIGHTS[:1]),
                   "gla_fwd": ('ici', FFN2_WEIGHTS[1:]), "proj_gla": ('d2d', FFN2_WEIGHTS[1:])}
        if tag in gathers:
            kind, names = gathers[tag]
            key = tuple(names)
            if kind == 'ici':
                return [(_side_gather_ici(self._shards(names)), lambda outs: self.riding.update({key: outs}))]
            return [(_side_gather_d2d(self.riding[key]), lambda outs: self._gathered(names, outs))]
        steps = {"d_merge": (grp['ffn2'], 0), "gla_bwd": (grp['ffn2'], 1), "d_mix_rms": (grp['ffn2'], 2),
                 "d_u_a": (grp['mixer'], 0), "ffn1_bwd": (grp['mixer'], 1), "ffn1_gw2": (grp['mixer'], 2),
                 "ffn1_gw1": (FFN1_EARLY, 0), "ffn1_gw3": (FFN1_EARLY, 1)}
        entries = [self._reduce_stage(*steps[tag])] if tag in steps else []
        if tag == "ffn1_gw2":
            entries.append(self._small_stage(0))
        if tag == "ffn1_gw1":
            entries.append(self._small_stage(1))
        return entries

    def _small_stage(self, stage):
        if stage == 0:
            a, grads = self.a, self.grads
            self.small_parts = [grads[nm].reshape(a[nm].shape) for nm in SMALL if nm != 'gla_a_up_w'] + [grads['gla_a_up_w']]
            packed = _pack_small(self.small_parts)

            def done(outs):
                self.small_pair = _small_add(packed, outs[0], "small_sum_pair")
            return _side_small_sibling(packed), done

        def done(outs):
            self.small_total = _small_add(self.small_pair, outs[0], "small_sum_chips")
        return _side_small_chips(self.small_pair), done

    def _reduce_stage(self, names, stage):
        if stage == 0:
            for nm in names:
                self.g4s[nm] = self._shard_major(nm)

            def done(outs):
                for nm, r in zip(names, outs):
                    self.chip_sums[nm] = _chip_sum(self.g4s[nm], r, self.c_arr, f"chip_sum_{nm}")
            return _side_swap_halves([self.g4s[nm] for nm in names]), done
        if stage == 1:
            def done(outs):
                for nm, o in zip(names, outs):
                    self.halves[nm] = _owner_sum(self.chip_sums[nm], o, self.s_arr, f"owner_sum_{nm}")
            return _side_scatter([self.chip_sums[nm] for nm in names]), done

        def done(outs):
            self.sib_halves.update(zip(names, outs))
        return _side_swap_reduced([self.halves[nm] for nm in names]), done

    def before(self, tag):
        entries = self._schedule(tag)
        if entries:
            merged = _merge_sides([side for side, _ in entries])
            self.riding[tag] = (merged, entries)
            _RIDER.append(merged)

    def after(self, tag):
        if tag in self.riding:
            merged, entries = self.riding.pop(tag)
            assert not _RIDER and merged.outs is not None, tag
            pos = 0
            for side, done in entries:
                done(merged.outs[pos:pos + len(side.out_shapes)])
                pos += len(side.out_shapes)

    def finish(self):
        late = [nm for nm in GRAD_GROUPS['ffn1'] if nm not in FFN1_EARLY]
        for stage, names in ((0, late), (1, late), (2, GRAD_GROUPS['ffn1'])):
            side, done = self._reduce_stage(names, stage)
            done(_run_side(side, f"grad_ffn1_stage{stage}"))


def _train_step(a):
    x = a['x'][0]
    tgt = a['loss_target'][0]
    xi, yi, ci = lax.axis_index("x"), lax.axis_index("y"), lax.axis_index("c")
    c_arr = jnp.reshape(ci, (1,)).astype(jnp.int32)
    s_arr = jnp.reshape(2 * xi + yi, (1,)).astype(jnp.int32)
    plan = _Plan(a, c_arr, s_arr)
    loss, dx = _local_step(x, tgt, plan)
    plan.finish()
    grads = plan.grads
    loss = lax.psum(loss, ("x", "y", "c"))
    halves = [plan.halves[nm] for nm in SHARDED]
    sib_halves = [plan.sib_halves[nm] for nm in SHARDED]
    red = {}
    small_sum = _unpack_small(plan.small_total, plan.small_parts)
    small_names = [nm for nm in SMALL if nm != 'gla_a_up_w']
    for nm, g in zip(small_names, small_sum[:-1]):
        red[nm] = g
    g_up = small_sum[-1]
    red['gla_a_up_w'] = lax.dynamic_slice(g_up, (0, (2 * xi + yi) * GLA_DK), (GLA_RANK, GLA_DK))
    out_g, out_d, out_m, out_v = {}, {}, {}, {}
    for nm, own, sib in zip(SHARDED, halves, sib_halves):
        loc = lambda pre: _local_shard(a[pre + nm], nm)
        res = _adamw_halves(loc(''), own, sib, loc('m_'), loc('v_'), c_arr, f"adamw_{nm}")
        back = (lambda t: jnp.swapaxes(t[None], 1, 2)) if nm in TRANSPOSED else (lambda t: t[None])
        out_g[nm], out_d[nm], out_m[nm], out_v[nm] = (back(t) for t in res)
    rest = [nm for nm in WEIGHTS if nm not in SHARDED]
    pk = lambda pre: _pack_small([a[pre + nm] for nm in rest])
    d, nm_, nv_ = _adamw(pk(''), _pack_small([red[nm] for nm in rest]), pk('m_'), pk('v_'), "adamw_small")
    like = [a[nm] for nm in rest]
    for nm, g, dd, mm_, vv_ in zip(rest, [red[nm].reshape(a[nm].shape) for nm in rest], _unpack_small(d, like),
                                   _unpack_small(nm_, like), _unpack_small(nv_, like)):
        out_g[nm], out_d[nm], out_m[nm], out_v[nm] = g, dd, mm_, vv_
    return (loss, dx[None], *[out_g[nm] for nm in WEIGHTS], *[out_d[nm] for nm in WEIGHTS],
            *[out_m[nm] for nm in WEIGHTS], *[out_v[nm] for nm in WEIGHTS])


def kernel(x, ffn1_norm, ffn1_w1, ffn1_w3, ffn1_w2, mix_norm, w_in, s5_lambda_re, s5_lambda_im, s5_log_dt, s5_b_re, s5_b_im, s5_c_re, s5_c_im, s5_d, s5_glu_w, s5_glu_b, gla_a_up_w, gla_a_up_b, gla_out_norm, proj_s5, proj_gla, w_out, ffn2_norm, ffn2_w1, ffn2_w3, ffn2_w2, final_norm, loss_target, m_ffn1_norm, m_ffn1_w1, m_ffn1_w3, m_ffn1_w2, m_mix_norm, m_w_in, m_s5_lambda_re, m_s5_lambda_im, m_s5_log_dt, m_s5_b_re, m_s5_b_im, m_s5_c_re, m_s5_c_im, m_s5_d, m_s5_glu_w, m_s5_glu_b, m_gla_a_up_w, m_gla_a_up_b, m_gla_out_norm, m_proj_s5, m_proj_gla, m_w_out, m_ffn2_norm, m_ffn2_w1, m_ffn2_w3, m_ffn2_w2, m_final_norm, v_ffn1_norm, v_ffn1_w1, v_ffn1_w3, v_ffn1_w2, v_mix_norm, v_w_in, v_s5_lambda_re, v_s5_lambda_im, v_s5_log_dt, v_s5_b_re, v_s5_b_im, v_s5_c_re, v_s5_c_im, v_s5_d, v_s5_glu_w, v_s5_glu_b, v_gla_a_up_w, v_gla_a_up_b, v_gla_out_norm, v_proj_s5, v_proj_gla, v_w_out, v_ffn2_norm, v_ffn2_w1, v_ffn2_w3, v_ffn2_w2, v_final_norm):
    return _train_step(dict(locals()))
```

```python
import functools

import jax
import jax.numpy as jnp
from jax import lax
from jax.experimental import pallas as pl
from jax.experimental.pallas import tpu as pltpu

F32 = jnp.float32
BF16 = jnp.bfloat16
HI = lax.Precision.HIGHEST
MESH_ID = pl.DeviceIdType.MESH

D_MODEL = 1024
EPS = 1e-6
S5_G, S5_P, S5_H = 32, 64, 16
S5_W = S5_G * S5_H
S5_GP = S5_G * S5_P
SEG = 8
SCAN_ROWS = 256
GLA_HEADS, GLA_DK, GLA_DV = 4, 64, 128
GLA_CHUNK = 64
GLA_TAU = 16.0
GLA_RANK = 16
ADAM_LR, ADAM_B1, ADAM_B2, ADAM_EPS, ADAM_WD, ADAM_STEP = 0.001, 0.9, 0.999, 1e-08, 0.01, 10
V7X_VMEM_LIMIT = 56 * 1024 * 1024
LANE = 128

WEIGHTS = ['ffn1_norm', 'ffn1_w1', 'ffn1_w3', 'ffn1_w2', 'mix_norm', 'w_in', 's5_lambda_re', 's5_lambda_im',
           's5_log_dt', 's5_b_re', 's5_b_im', 's5_c_re', 's5_c_im', 's5_d', 's5_glu_w', 's5_glu_b', 'gla_a_up_w',
           'gla_a_up_b', 'gla_out_norm', 'proj_s5', 'proj_gla', 'w_out', 'ffn2_norm', 'ffn2_w1', 'ffn2_w3',
           'ffn2_w2', 'final_norm']
SHARDED = ['ffn1_w1', 'ffn1_w3', 'ffn1_w2', 'w_in', 's5_glu_w', 'proj_s5', 'proj_gla', 'w_out',
           'ffn2_w1', 'ffn2_w3', 'ffn2_w2']
COL_SHARDED = ['ffn1_w1', 'ffn1_w3', 'w_in', 'proj_s5', 'proj_gla', 'ffn2_w1', 'ffn2_w3', 'gla_a_up_w']
SMALL = [n for n in WEIGHTS if n not in SHARDED]
FFN_WEIGHTS = ['ffn1_w1', 'ffn1_w3', 'ffn1_w2', 'ffn2_w1', 'ffn2_w3', 'ffn2_w2']


def _params(**kw):
    return pltpu.CompilerParams(vmem_limit_bytes=V7X_VMEM_LIMIT, **kw)


class _Side:
    def __init__(self, ins, out_shapes, nsem, copies, aliased=False):
        self.ins, self.out_shapes, self.nsem, self.copies, self.aliased = list(ins), list(out_shapes), nsem, copies, aliased
        self.outs = None


_RIDER = []


def _pcall(body, **kw):
    if _RIDER:
        return _carry(body, _RIDER.pop(), **kw)
    return pl.pallas_call(body, **kw)


def _carry(body, side, *, name, grid, in_specs, out_specs, out_shape, scratch_shapes=(), compiler_params=None):
    del compiler_params
    single = not isinstance(out_shape, (list, tuple))
    out_specs = [out_specs] if single else list(out_specs)
    out_shape = [out_shape] if single else list(out_shape)
    n_in, n_out, n_scr = len(in_specs), len(out_shape), len(scratch_shapes)
    s_in, s_out = len(side.ins), len(side.out_shapes)
    any_spec = pl.BlockSpec(memory_space=pl.ANY)

    def wrapped(*refs):
        cuts = [n_in, s_in, n_out, s_out, n_scr]
        parts, pos = [], 0
        for c in cuts:
            parts.append(refs[pos:pos + c])
            pos += c
        ins, sins, outs, souts, scr = parts
        ssem, rsem = refs[pos], refs[pos + 1]
        first = last = None
        for d, g in enumerate(grid):
            i = pl.program_id(d)
            first = (i == 0) if first is None else first & (i == 0)
            last = (i == g - 1) if last is None else last & (i == g - 1)

        @pl.when(first)
        def _():
            for cp in side.copies(sins, souts, ssem, rsem):
                cp.start()

        body(*ins, *outs, *scr)

        @pl.when(last)
        def _():
            for cp in side.copies(sins, souts, ssem, rsem):
                cp.wait()

    call = pl.pallas_call(
        wrapped, name=name, grid=grid, in_specs=list(in_specs) + [any_spec] * s_in,
        out_specs=out_specs + [any_spec] * s_out, out_shape=out_shape + side.out_shapes,
        scratch_shapes=list(scratch_shapes) + [pltpu.SemaphoreType.DMA((side.nsem,)), pltpu.SemaphoreType.DMA((side.nsem,))],
        input_output_aliases={n_in + j: n_out + j for j in range(s_in)} if side.aliased else {},
        compiler_params=_params(has_side_effects=True))

    def run(*args):
        res = call(*args, *side.ins)
        side.outs = list(res[n_out:])
        return res[0] if single else list(res[:n_out])

    return run


def _run_side(side, name):
    s_in, s_out = len(side.ins), len(side.out_shapes)
    any_spec = pl.BlockSpec(memory_space=pl.ANY)

    def body(*refs):
        sins, souts = refs[:s_in], refs[s_in:s_in + s_out]
        ssem, rsem = refs[s_in + s_out:]
        cps = side.copies(sins, souts, ssem, rsem)
        for cp in cps:
            cp.start()
        for cp in cps:
            cp.wait()

    side.outs = list(pl.pallas_call(
        body, name=name, in_specs=[any_spec] * s_in, out_specs=[any_spec] * s_out, out_shape=side.out_shapes,
        scratch_shapes=[pltpu.SemaphoreType.DMA((side.nsem,)), pltpu.SemaphoreType.DMA((side.nsem,))],
        input_output_aliases={j: j for j in range(s_in)} if side.aliased else {},
        compiler_params=pltpu.CompilerParams(has_side_effects=True))(*side.ins))
    return side.outs


def _pick(n, cap, quantum):
    if n <= cap:
        return n
    best = None
    for t in range(quantum, cap + 1, quantum):
        if n % t == 0:
            best = t
    assert best is not None, (n, cap, quantum)
    return best


def _sigmoid(x):
    return jax.nn.sigmoid(x)


def _mm(a, b, *, name, ta=False, tb=False, out_dtype=F32, alpha=1.0, res=None, bias=None, exact=False, shard=None):
    ns = 4
    (k_a, m) = a.shape[-2:] if ta else a.shape[-2:][::-1]
    (k_b, n) = b.shape[-2:][::-1] if tb else b.shape[-2:]
    assert k_a == k_b, (a.shape, b.shape, ta, tb)
    assert (a.ndim == 3) == (shard in ('k', 'm')) and (b.ndim == 3) == (shard in ('n', 'k'))
    k = k_a
    tm = _pick(m, 1024, 128)
    tn = _pick(n, 1024, 128)
    tk = _pick(k, 1024, 128)
    pm, pn, pk = m // tm, n // tn, k // tk
    gm = pm * (ns if shard == 'm' else 1)
    gn = pn * (ns if shard == 'n' else 1)
    gk = pk * (ns if shard == 'k' else 1)
    dims = (((0,) if ta else (1,), (1,) if tb else (0,)), ((), ()))
    op_dtype = F32 if exact else BF16

    def body(*refs):
        a_ref, b_ref = refs[0], refs[1]
        pos = 2
        res_ref = bias_ref = None
        if res is not None:
            res_ref = refs[pos]
            pos += 1
        if bias is not None:
            bias_ref = refs[pos]
            pos += 1
        o_ref, acc_ref = refs[pos], refs[pos + 1]
        kk = pl.program_id(2)

        @pl.when(kk == 0)
        def _():
            acc_ref[...] = jnp.zeros_like(acc_ref)

        acc_ref[...] += lax.dot_general(a_ref[...].astype(op_dtype), b_ref[...].astype(op_dtype), dims,
                                        precision=HI if exact else None, preferred_element_type=F32)

        @pl.when(kk == gk - 1)
        def _():
            o = acc_ref[...]
            if alpha != 1.0:
                o = o * alpha
            if bias_ref is not None:
                o = o + bias_ref[...]
            if res_ref is not None:
                o = o + res_ref[...]
            o_ref[...] = o.astype(out_dtype)

    def spec(block, sharded_on, order):
        per = {'m': pm, 'n': pn, 'k': pk}

        def index(i, j, kk):
            g = {'m': i, 'n': j, 'k': kk}
            r, c = order(i % pm if shard == 'm' else i, j % pn if shard == 'n' else j, kk % pk if shard == 'k' else kk)
            if sharded_on is None:
                return (r, c)
            return (g[sharded_on] // per[sharded_on], r, c)

        return pl.BlockSpec(block if sharded_on is None else (None,) + block, index)

    a_sh = shard if shard in ('k', 'm') else None
    b_sh = shard if shard in ('n', 'k') else None
    o_sh = shard if shard in ('n', 'm') else None
    a_spec = spec((tk, tm), a_sh, lambda i, j, kk: (kk, i)) if ta else spec((tm, tk), a_sh, lambda i, j, kk: (i, kk))
    b_spec = spec((tn, tk), b_sh, lambda i, j, kk: (j, kk)) if tb else spec((tk, tn), b_sh, lambda i, j, kk: (kk, j))
    ins, in_specs = [a, b], [a_spec, b_spec]
    if res is not None:
        assert o_sh is None
        ins.append(res)
        in_specs.append(pl.BlockSpec((tm, tn), lambda i, j, kk: (i, j)))
    if bias is not None:
        assert o_sh is None
        ins.append(bias)
        in_specs.append(pl.BlockSpec((1, tn), lambda i, j, kk: (0, j)))
    out_shape = (m, n) if o_sh is None else (ns, m, n)
    return _pcall(body, name=name, grid=(gm, gn, gk), in_specs=in_specs,
                  out_specs=spec((tm, tn), o_sh, lambda i, j, kk: (i, j)),
                  out_shape=jax.ShapeDtypeStruct(out_shape, out_dtype),
                  scratch_shapes=[pltpu.VMEM((tm, tn), F32)], compiler_params=_params())(*ins)


ROWS_VMEM_BUDGET = 24 * 1024 * 1024


def _rows(body, ins, outs, *, n, name):
    cols = sum(a.shape[1] for a, kind in ins if kind == 'r') + sum(c for c, _, kind in outs if kind == 'r')
    cap = 256
    while cap < 2048 and 2 * 4 * cols * (2 * cap) <= ROWS_VMEM_BUDGET:
        cap *= 2
    tm = _pick(n, cap, 16)
    in_specs = []
    for arr, kind in ins:
        if kind == 'r':
            in_specs.append(pl.BlockSpec((tm, arr.shape[1]), lambda i: (i, 0)))
        else:
            in_specs.append(pl.BlockSpec(arr.shape, lambda i: (0, 0)))
    out_specs, out_shape = [], []
    for cols, dtype, kind in outs:
        if kind == 'r':
            out_specs.append(pl.BlockSpec((tm, cols), lambda i: (i, 0)))
            out_shape.append(jax.ShapeDtypeStruct((n, cols), dtype))
        else:
            out_specs.append(pl.BlockSpec((1, cols), lambda i: (0, 0)))
            out_shape.append(jax.ShapeDtypeStruct((1, cols), dtype))
    n_in = len(ins)
    acc_ids = [j for j, o in enumerate(outs) if o[2] == 'a']

    def wrapped(*refs):
        if acc_ids:
            @pl.when(pl.program_id(0) == 0)
            def _():
                for j in acc_ids:
                    refs[n_in + j][...] = jnp.zeros_like(refs[n_in + j])
        body(*refs)

    res = _pcall(wrapped, name=name, grid=(n // tm,), in_specs=in_specs, out_specs=out_specs, out_shape=out_shape,
                 compiler_params=_params())(*[a for a, _ in ins])
    return res


def _rms_fwd(x, g, name):
    def body(x_ref, g_ref, o_ref):
        xv = x_ref[...]
        rstd = lax.rsqrt(jnp.mean(xv * xv, axis=-1, keepdims=True) + EPS)
        o_ref[...] = (xv * rstd * g_ref[...]).astype(BF16)
    return _rows(body, [(x, 'r'), (g, 'f')], [(x.shape[1], BF16, 'r')], n=x.shape[0], name=name)[0]


def _rms_bwd(x, g, dn, dres, name):
    def body(x_ref, g_ref, dn_ref, dres_ref, dx_ref, dg_ref):
        xv = x_ref[...]
        rstd = lax.rsqrt(jnp.mean(xv * xv, axis=-1, keepdims=True) + EPS)
        xh = xv * rstd
        dn = dn_ref[...]
        dg_ref[...] += jnp.sum(dn * xh, axis=0, keepdims=True)
        dxh = dn * g_ref[...]
        dx_ref[...] = dres_ref[...] + rstd * (dxh - xh * jnp.mean(dxh * xh, axis=-1, keepdims=True))
    d = x.shape[1]
    return _rows(body, [(x, 'r'), (g, 'f'), (dn, 'r'), (dres, 'r')], [(d, F32, 'r'), (d, F32, 'a')],
                 n=x.shape[0], name=name)


def _gelu_parts(y):
    c0 = 0.7978845608028654
    inner = c0 * (y + 0.044715 * y * y * y)
    th = jnp.tanh(inner)
    return th, c0 * (1.0 + 3.0 * 0.044715 * y * y)


def _gelu_fwd(y, name):
    def body(y_ref, o_ref):
        yv = y_ref[...]
        th, _ = _gelu_parts(yv)
        o_ref[...] = 0.5 * yv * (1.0 + th)
    return _rows(body, [(y, 'r')], [(y.shape[1], F32, 'r')], n=y.shape[0], name=name)[0]


def _glu_fwd(zg, t, name):
    def body(z_ref, t_ref, o_ref):
        o_ref[...] = (z_ref[...] * _sigmoid(t_ref[...])).astype(BF16)
    return _rows(body, [(zg, 'r'), (t, 'r')], [(zg.shape[1], BF16, 'r')], n=zg.shape[0], name=name)[0]


def _glu_bwd1(dy, zg, t, name):
    def body(dy_ref, z_ref, t_ref, dz_ref, dt_ref, db_ref):
        dyv, zv = dy_ref[...], z_ref[...]
        sg = _sigmoid(t_ref[...])
        dz_ref[...] = dyv * sg
        dt = dyv * zv * sg * (1.0 - sg)
        dt_ref[...] = dt.astype(BF16)
        db_ref[...] += jnp.sum(dt, axis=0, keepdims=True)
    w = zg.shape[1]
    return _rows(body, [(dy, 'r'), (zg, 'r'), (t, 'r')], [(w, F32, 'r'), (w, BF16, 'r'), (w, F32, 'a')],
                 n=zg.shape[0], name=name)


def _glu_bwd2(dzg, ys, u, dskip, name):
    def body(dz_ref, y_ref, u_ref, d_ref, dy_ref, du_ref, dd_ref):
        yv = y_ref[...]
        th, dinner = _gelu_parts(yv)
        dy = dz_ref[...] * (0.5 * (1.0 + th) + 0.5 * yv * (1.0 - th * th) * dinner)
        dy_ref[...] = dy
        du_ref[...] = dy * d_ref[...]
        dd_ref[...] += jnp.sum(dy * u_ref[...], axis=0, keepdims=True)
    w = ys.shape[1]
    return _rows(body, [(dzg, 'r'), (ys, 'r'), (u, 'r'), (dskip, 'f')], [(w, F32, 'r'), (w, F32, 'r'), (w, F32, 'a')],
                 n=ys.shape[0], name=name)


def _scale_rows(u, dskip, name):
    def body(u_ref, d_ref, o_ref):
        o_ref[...] = u_ref[...] * d_ref[...]
    return _rows(body, [(u, 'r'), (dskip, 'f')], [(u.shape[1], F32, 'r')], n=u.shape[0], name=name)[0]


def _merge_fwd(zg, ps, pg, name):
    def body(z_ref, ps_ref, pg_ref, o_ref):
        zv = z_ref[...]
        o_ref[...] = (_sigmoid(zv[:, :D_MODEL]) * ps_ref[...] + _sigmoid(zv[:, D_MODEL:]) * pg_ref[...]).astype(BF16)
    return _rows(body, [(zg, 'r'), (ps, 'r'), (pg, 'r')], [(D_MODEL, BF16, 'r')], n=zg.shape[0], name=name)[0]


def _merge_bwd(dm, zg, ps, pg, name):
    def body(dm_ref, z_ref, ps_ref, pg_ref, dps_ref, dpg_ref, dz_ref):
        dmv, zv = dm_ref[...], z_ref[...]
        s1, s2 = _sigmoid(zv[:, :D_MODEL]), _sigmoid(zv[:, D_MODEL:])
        dps_ref[...] = (dmv * s1).astype(BF16)
        dpg_ref[...] = (dmv * s2).astype(BF16)
        dz_ref[:, :D_MODEL] = dmv * ps_ref[...] * s1 * (1.0 - s1)
        dz_ref[:, D_MODEL:] = dmv * pg_ref[...] * s2 * (1.0 - s2)
    return _rows(body, [(dm, 'r'), (zg, 'r'), (ps, 'r'), (pg, 'r')],
                 [(D_MODEL, BF16, 'r'), (D_MODEL, BF16, 'r'), (2 * D_MODEL, F32, 'r')], n=zg.shape[0], name=name)


def _final_loss(h, g, tgt, name):
    def body(h_ref, g_ref, t_ref, loss_ref, dh_ref, dg_ref):
        hv = h_ref[...]
        rstd = lax.rsqrt(jnp.mean(hv * hv, axis=-1, keepdims=True) + EPS)
        xh = hv * rstd
        err = xh * g_ref[...] - t_ref[...]
        part = 0.5 * jnp.sum(jnp.mean(err * err, axis=-1, keepdims=True), axis=0, keepdims=True)
        loss_ref[...] += jnp.broadcast_to(part, loss_ref.shape)
        dout = err * (1.0 / hv.shape[1])
        dg_ref[...] += jnp.sum(dout * xh, axis=0, keepdims=True)
        dxh = dout * g_ref[...]
        dh_ref[...] = rstd * (dxh - xh * jnp.mean(dxh * xh, axis=-1, keepdims=True))
    d = h.shape[1]
    return _rows(body, [(h, 'r'), (g, 'f'), (tgt, 'r')], [(LANE, F32, 'a'), (d, F32, 'r'), (d, F32, 'a')],
                 n=h.shape[0], name=name)


def _adamw_math(wv, gv, mv, vv):
    nm = ADAM_B1 * mv + (1.0 - ADAM_B1) * gv
    nv = ADAM_B2 * vv + (1.0 - ADAM_B2) * (gv * gv)
    m_hat = nm / (1.0 - ADAM_B1 ** ADAM_STEP)
    v_hat = nv / (1.0 - ADAM_B2 ** ADAM_STEP)
    return -ADAM_LR * (m_hat / (jnp.sqrt(v_hat) + ADAM_EPS) + ADAM_WD * wv), nm, nv


def _adamw(w, g, m, v, name):
    def body(w_ref, g_ref, m_ref, v_ref, d_ref, nm_ref, nv_ref):
        d_ref[...], nm_ref[...], nv_ref[...] = _adamw_math(w_ref[...], g_ref[...], m_ref[...], v_ref[...])
    c = w.shape[1]
    return _rows(body, [(w, 'r'), (g, 'r'), (m, 'r'), (v, 'r')], [(c, F32, 'r')] * 3, n=w.shape[0], name=name)


def _adamw_cols(w, g, m, v, name):
    r, cols = w.shape
    tc = _pick(cols, 256, LANE)

    def body(w_ref, g_ref, m_ref, v_ref, d_ref, nm_ref, nv_ref):
        d_ref[...], nm_ref[...], nv_ref[...] = _adamw_math(w_ref[...], g_ref[...], m_ref[...], v_ref[...])

    blk = pl.BlockSpec((r, tc), lambda j: (0, j))
    return _pcall(body, name=name, grid=(cols // tc,), in_specs=[blk] * 4, out_specs=[blk] * 3,
                  out_shape=[jax.ShapeDtypeStruct((r, cols), F32)] * 3, compiler_params=_params())(w, g, m, v)


def _adamw_halves(w, g_own, g_sib, m, v, c_arr, name):
    r, cols = w.shape
    h = r // 2
    tr = _pick(h, 512, 8)
    per = h // tr

    def body(c_ref, w_ref, go_ref, gs_ref, m_ref, v_ref, g_ref, d_ref, nm_ref, nv_ref):
        mine = (pl.program_id(0) // per) == c_ref[0]
        gv = jnp.where(mine, go_ref[...], gs_ref[...])
        g_ref[...] = gv
        d_ref[...], nm_ref[...], nv_ref[...] = _adamw_math(w_ref[...], gv, m_ref[...], v_ref[...])

    full = pl.BlockSpec((tr, cols), lambda i, c_ref: (i, 0))
    half = pl.BlockSpec((tr, cols), lambda i, c_ref: (i % per, 0))
    grid_spec = pltpu.PrefetchScalarGridSpec(num_scalar_prefetch=1, grid=(2 * per,),
                                             in_specs=[full, half, half, full, full], out_specs=[full] * 4)
    return _pcall(body, name=name, grid_spec=grid_spec, out_shape=[jax.ShapeDtypeStruct((r, cols), F32)] * 4,
                  compiler_params=_params())(c_arr, w, g_own, g_sib, m, v)


def _shift_rows(v, sh, down):
    rolled = pltpu.roll(v, sh if down else v.shape[0] - sh, axis=0)
    row = lax.broadcasted_iota(jnp.int32, v.shape, 0)
    keep = (row >= sh) if down else (row < v.shape[0] - sh)
    return jnp.where(keep, rolled, 0.0)


def _chain_segments(st_r, st_i, pw_r_ref, pw_i_ref, conj, down):
    vr, vi = st_r[...], st_i[...]
    sh, k = 1, 0
    while sh < SEG:
        pr, pi = pw_r_ref[k:k + 1, :], pw_i_ref[k:k + 1, :]
        if conj:
            pi = -pi
        sr, si = _shift_rows(vr, sh, down), _shift_rows(vi, sh, down)
        vr, vi = vr + pr * sr - pi * si, vi + pr * si + pi * sr
        sh, k = sh * 2, k + 1
    st_r[...] = _shift_rows(vr, 1, down)
    st_i[...] = _shift_rows(vi, 1, down)


def _expand_block(u_ref, t_ref, bu_ref):
    for j in range(BD_TILES):
        k = j % 4
        bu_ref[:, j * BD_ST:(j + 1) * BD_ST] = _dot(u_ref[:, k * BD_CH:(k + 1) * BD_CH], t_ref[j])


def _s5_scan(u, tiles, ar8, ai8, pw_r, pw_i, name):
    n = u.shape[0]
    rb = SCAN_ROWS
    nb, steps, lc = n // rb, rb // SEG, 512

    def body(u_ref, t_ref, ar_ref, ai_ref, pwr_ref, pwi_ref, x_ref, st_r, st_i, bu_ref):
        ph, b = pl.program_id(0), pl.program_id(1)

        @pl.when((ph == 0) & (b == 0))
        def _():
            st_r[...] = jnp.zeros_like(st_r)
            st_i[...] = jnp.zeros_like(st_i)

        _expand_block(u_ref, t_ref, bu_ref)

        def scan(store):
            for c in range(S5_GP // lc):
                re, im = slice(c * lc, (c + 1) * lc), slice(S5_GP + c * lc, S5_GP + (c + 1) * lc)
                a_r, a_i = ar_ref[:, re], ai_ref[:, re]

                def step(s, carry):
                    xr, xi = carry
                    rows = pl.ds(pl.multiple_of(s * SEG, SEG), SEG)
                    nr = a_r * xr - a_i * xi + bu_ref[rows, re]
                    ni = a_r * xi + a_i * xr + bu_ref[rows, im]
                    if store:
                        x_ref[rows, re] = nr
                        x_ref[rows, im] = ni
                    return nr, ni

                xr, xi = lax.fori_loop(0, steps, step, (st_r[:, re], st_i[:, re]), unroll=4)
                st_r[:, re] = xr
                st_i[:, re] = xi

        @pl.when(ph == 0)
        def _():
            scan(False)

        @pl.when((ph == 0) & (b == nb - 1))
        def _():
            _chain_segments(st_r, st_i, pwr_ref, pwi_ref, conj=False, down=True)

        @pl.when(ph == 1)
        def _():
            scan(True)

    full = lambda a: pl.BlockSpec(a.shape, lambda ph, b: (0, 0))
    return _pcall(body, name=name, grid=(2, nb),
                  in_specs=[pl.BlockSpec((rb, S5_W), lambda ph, b: (b, 0)), pl.BlockSpec(tiles.shape, lambda ph, b: (0, 0, 0)),
                            full(ar8), full(ai8), full(pw_r), full(pw_i)],
                  out_specs=pl.BlockSpec((rb, 2 * S5_GP), lambda ph, b: (b * ph, 0)),
                  out_shape=jax.ShapeDtypeStruct((n, 2 * S5_GP), F32),
                  scratch_shapes=[pltpu.VMEM((SEG, S5_GP), F32), pltpu.VMEM((SEG, S5_GP), F32),
                                  pltpu.VMEM((rb, 2 * S5_GP), F32)],
                  compiler_params=_params())(u, tiles, ar8, ai8, pw_r, pw_i)


def _s5_scan_bwd(dy, tiles, xs, ar8, ai8, pw_r, pw_i, name):
    n = dy.shape[0]
    rb = SCAN_ROWS
    nb, steps, lc = n // rb, rb // SEG, 256

    def body(dy_ref, t_ref, x_ref, ar_ref, ai_ref, pwr_ref, pwi_ref, lam_ref, da_ref, st_r, st_i, gx_ref):
        ph, b = pl.program_id(0), pl.program_id(1)

        @pl.when((ph == 0) & (b == 0))
        def _():
            st_r[...] = jnp.zeros_like(st_r)
            st_i[...] = jnp.zeros_like(st_i)
            da_ref[...] = jnp.zeros_like(da_ref)

        _expand_block(dy_ref, t_ref, gx_ref)

        def scan(store):
            for c in range(S5_GP // lc):
                re, im = slice(c * lc, (c + 1) * lc), slice(S5_GP + c * lc, S5_GP + (c + 1) * lc)
                a_r, a_i = ar_ref[:, re], ai_ref[:, re]

                def step(s, carry):
                    rows = pl.ds(pl.multiple_of((steps - 1 - s) * SEG, SEG), SEG)
                    if store:
                        lr, li, dr, di = carry
                        xr, xi = x_ref[rows, re], x_ref[rows, im]
                        dr = dr + lr * xr + li * xi
                        di = di + li * xr - lr * xi
                    else:
                        lr, li = carry
                    nr = a_r * lr + a_i * li + gx_ref[rows, re]
                    ni = a_r * li - a_i * lr + gx_ref[rows, im]
                    if store:
                        lam_ref[rows, re] = nr
                        lam_ref[rows, im] = ni
                        return nr, ni, dr, di
                    return nr, ni

                if store:
                    lr, li, dr, di = lax.fori_loop(0, steps, step, (st_r[:, re], st_i[:, re], da_ref[:, re], da_ref[:, im]),
                                                   unroll=4)
                    da_ref[:, re] = dr
                    da_ref[:, im] = di
                else:
                    lr, li = lax.fori_loop(0, steps, step, (st_r[:, re], st_i[:, re]), unroll=4)
                st_r[:, re] = lr
                st_i[:, re] = li

        @pl.when(ph == 0)
        def _():
            scan(False)

        @pl.when((ph == 0) & (b == nb - 1))
        def _():
            _chain_segments(st_r, st_i, pwr_ref, pwi_ref, conj=True, down=False)

        @pl.when(ph == 1)
        def _():
            scan(True)

    full = lambda a: pl.BlockSpec(a.shape, lambda ph, b: (0, 0))
    rev = lambda ph, b: (nb - 1 - b, 0)
    return _pcall(body, name=name, grid=(2, nb),
                  in_specs=[pl.BlockSpec((rb, S5_W), rev), pl.BlockSpec(tiles.shape, lambda ph, b: (0, 0, 0)),
                            pl.BlockSpec((rb, 2 * S5_GP), lambda ph, b: ((nb - 1 - b) * ph, 0)),
                            full(ar8), full(ai8), full(pw_r), full(pw_i)],
                  out_specs=[pl.BlockSpec((rb, 2 * S5_GP), lambda ph, b: (nb - 1 - b * ph, 0)),
                             pl.BlockSpec((SEG, 2 * S5_GP), lambda ph, b: (0, 0))],
                  out_shape=[jax.ShapeDtypeStruct((n, 2 * S5_GP), F32), jax.ShapeDtypeStruct((SEG, 2 * S5_GP), F32)],
                  scratch_shapes=[pltpu.VMEM((SEG, S5_GP), F32), pltpu.VMEM((SEG, S5_GP), F32),
                                  pltpu.VMEM((rb, 2 * S5_GP), F32)],
                  compiler_params=_params())(dy, tiles, xs, ar8, ai8, pw_r, pw_i)


def _s5_discretize(lam_re, lam_im, log_dt, b_re, b_im):
    dt = jnp.exp(log_dt)[:, None]
    mag = jnp.exp(lam_re * dt)
    ar = mag * jnp.cos(lam_im * dt)
    ai = mag * jnp.sin(lam_im * dt)
    den = lam_re * lam_re + lam_im * lam_im
    nr = ar - 1.0
    fr = (nr * lam_re + ai * lam_im) / den
    fi = (ai * lam_re - nr * lam_im) / den
    bbar_re = fr[:, :, None] * b_re - fi[:, :, None] * b_im
    bbar_im = fr[:, :, None] * b_im + fi[:, :, None] * b_re
    return ar, ai, bbar_re, bbar_im


BD_TILES, BD_CH, BD_ST, BD_GROUPS = 8, 128, 512, 8
BD_ROWS = 4096


def _bd_tiles(re, im):
    eye = jnp.eye(BD_GROUPS, dtype=re.dtype)

    def tiles(t):
        t = t.reshape(S5_G // BD_GROUPS, BD_GROUPS, S5_H, S5_P)
        return (t[:, :, :, None, :] * eye[None, :, None, :, None]).reshape(S5_G // BD_GROUPS, BD_CH, BD_ST)

    return jnp.concatenate([tiles(re), tiles(im)], axis=0)


def _bd_blocks(t):
    t = t.reshape(2, S5_G // BD_GROUPS, BD_GROUPS, S5_H, BD_GROUPS, S5_P)
    return jnp.einsum('rkahap->rkahp', t).reshape(2, S5_G, S5_H, S5_P)


def _bd_reduce(x, t, res, name):
    n = x.shape[0]
    tm = _pick(n, BD_ROWS, 16)

    def body(x_ref, t_ref, r_ref, o_ref):
        part = _dot(x_ref[...], t_ref[...], NT)

        @pl.when(pl.program_id(2) == 0)
        def _():
            o_ref[...] = r_ref[...] + part

        @pl.when(pl.program_id(2) == 1)
        def _():
            o_ref[...] += part

    return _pcall(body, name=name, grid=(n // tm, 4, 2),
                  in_specs=[pl.BlockSpec((tm, BD_ST), lambda i, k, r: (i, k + 4 * r)),
                            pl.BlockSpec((None, BD_CH, BD_ST), lambda i, k, r: (k + 4 * r, 0, 0)),
                            pl.BlockSpec((tm, BD_CH), lambda i, k, r: (i, k))],
                  out_specs=pl.BlockSpec((tm, BD_CH), lambda i, k, r: (i, k)),
                  out_shape=jax.ShapeDtypeStruct((n, S5_W), F32), compiler_params=_params())(x, t, res)


def _bd_outer(a, x, name):
    n = a.shape[0]
    tk = _pick(n, BD_ROWS, 16)
    nk = n // tk

    def body(a_ref, x_ref, o_ref):
        part = _dot(a_ref[...], x_ref[...], TN)

        @pl.when(pl.program_id(1) == 0)
        def _():
            o_ref[...] = part

        @pl.when(pl.program_id(1) > 0)
        def _():
            o_ref[...] += part

    return _pcall(body, name=name, grid=(BD_TILES, nk),
                  in_specs=[pl.BlockSpec((tk, BD_CH), lambda j, kk: (kk, j % 4)), pl.BlockSpec((tk, BD_ST), lambda j, kk: (kk, j))],
                  out_specs=pl.BlockSpec((None, BD_CH, BD_ST), lambda j, kk: (j, 0, 0)),
                  out_shape=jax.ShapeDtypeStruct((BD_TILES, BD_CH, BD_ST), F32), compiler_params=_params())(a, x)


def _permute_rows(t):
    n = t.shape[0]
    return t.reshape(SEG, n // SEG, t.shape[1]).transpose(1, 0, 2).reshape(n, t.shape[1])


def _unpermute_rows(t):
    n = t.shape[0]
    return t.reshape(n // SEG, SEG, t.shape[1]).transpose(1, 0, 2).reshape(n, t.shape[1])


def _segment_powers(ar, ai, seg_steps):
    pr, pi = ar.reshape(1, S5_GP), ai.reshape(1, S5_GP)
    e = 1
    while e < seg_steps:
        pr, pi = pr * pr - pi * pi, 2.0 * pr * pi
        e *= 2
    assert e == seg_steps, "segment length must be a power of two"
    rows_r, rows_i = [], []
    for _ in range(3):
        rows_r.append(pr)
        rows_i.append(pi)
        pr, pi = pr * pr - pi * pi, 2.0 * pr * pi
    pad = jnp.zeros((SEG - 3, S5_GP), F32)
    return jnp.concatenate(rows_r + [pad], axis=0), jnp.concatenate(rows_i + [pad], axis=0)


NT = (((1,), (1,)), ((), ()))
TN = (((0,), (0,)), ((), ()))


def _dot(a, b, dims=None, exact=False):
    dims = (((1,), (0,)), ((), ())) if dims is None else dims
    if exact:
        return lax.dot_general(a, b, dims, precision=HI, preferred_element_type=F32)
    return lax.dot_general(a.astype(BF16), b.astype(BF16), dims, preferred_element_type=F32)


def _dot01(a, b, dims=None, ones_first=True):
    x = b if ones_first else a
    hi = x.astype(BF16)
    r1 = x - hi.astype(F32)
    mid = r1.astype(BF16)
    lo = (r1 - mid.astype(F32)).astype(BF16)
    parts = [(_dot(a, p, dims) if ones_first else _dot(p, b, dims)) for p in (lo, mid, hi)]
    return (parts[0] + parts[1]) + parts[2]


HEADS = range(4)


def _gla_chunk_fwd(qc, kc, vc, al, wup, bup, s_prev, tril):
    ones = jnp.ones((GLA_CHUNK, GLA_DV), F32)
    z = [_dot(al, wup[h]) + bup[h] for h in HEADS]
    la = [(jnp.minimum(z[h], 0.0) - jnp.log(1.0 + jnp.exp(-jnp.abs(z[h])))) * (1.0 / GLA_TAU) for h in HEADS]
    bc = [_dot01(tril, la[h]) for h in HEADS]
    blb = [_dot01(la[h], ones, TN, ones_first=False) for h in HEADS]
    bl = [bc[h][GLA_CHUNK - 1:GLA_CHUNK, :] for h in HEADS]
    ebc = [jnp.exp(bc[h]) for h in HEADS]
    qt = [qc[h] * (GLA_DK ** -0.5) * ebc[h] for h in HEADS]
    kt = [kc[h] * jnp.exp(-bc[h]) for h in HEADS]
    ke = [kc[h] * jnp.exp(bl[h] - bc[h]) for h in HEADS]
    sc = [_dot(qt[h], kt[h], NT) * tril for h in HEADS]
    oi = [_dot(sc[h], vc[h]) for h in HEADS]
    oo = [_dot(qt[h], s_prev[h]) for h in HEADS]
    o = [oi[h] + oo[h] for h in HEADS]
    return z, bc, bl, blb, ebc, qt, kt, ke, sc, o


GLA_ROWS = 512
GLA_CPB = GLA_ROWS // GLA_CHUNK


ZA_COLS = 5 * 512
SLOT = 128


def _pad_heads(w):
    r = w.shape[0]
    return jnp.pad(w.reshape(r, GLA_HEADS, GLA_DK), ((0, 0), (0, 0), (0, SLOT - GLA_DK))).reshape(r, GLA_HEADS * SLOT)


def _unpad_heads(w):
    r = w.shape[0]
    return w.reshape(r, GLA_HEADS, SLOT)[:, :, :GLA_DK].reshape(r, GLA_HEADS * GLA_DK)


def _gla_token_specs(blk):
    col = lambda cb: pl.BlockSpec((GLA_ROWS, 512), lambda j: (blk(j), cb))
    whole = lambda a: pl.BlockSpec(a.shape, lambda j: (0,) * a.ndim)
    return col, whole


def _head_ds(h, width):
    return pl.ds(h * SLOT, width)


def _tri(lower):
    ri = lax.broadcasted_iota(jnp.int32, (GLA_CHUNK, GLA_CHUNK), 0)
    ci = lax.broadcasted_iota(jnp.int32, (GLA_CHUNK, GLA_CHUNK), 1)
    return ((ri >= ci) if lower else (ri <= ci)).astype(F32)


def _gla_fwd(za, al, wup, bup, gn, name):
    n = za.shape[0]
    nc = n // GLA_CHUNK

    def body(q_ref, k_ref, v_ref, r_ref, al_ref, wup_ref, bup_ref, gn_ref, y_ref, sp_ref, s_ref):
        @pl.when(pl.program_id(0) == 0)
        def _():
            s_ref[...] = jnp.zeros_like(s_ref)

        tril = _tri(True)

        def chunk(c, carry):
            rows = pl.ds(pl.multiple_of(c * GLA_CHUNK, GLA_CHUNK), GLA_CHUNK)
            alc = al_ref[rows, :]
            vc = [v_ref[rows, _head_ds(h, GLA_DV)] for h in HEADS]
            s_prev = [s_ref[h] for h in HEADS]
            _, _, _, blb, _, _, _, ke, _, o = _gla_chunk_fwd(
                [q_ref[rows, _head_ds(h, GLA_DK)] for h in HEADS], [k_ref[rows, _head_ds(h, GLA_DK)] for h in HEADS],
                vc, alc, [wup_ref[h] for h in HEADS], [bup_ref[h] for h in HEADS], s_prev, tril)
            ds = [_dot(ke[h], vc[h], TN) for h in HEADS]
            for h in HEADS:
                rc = r_ref[rows, _head_ds(h, GLA_DV)]
                sp_ref[h, c] = s_prev[h]
                rstd = lax.rsqrt(jnp.mean(o[h] * o[h], axis=-1, keepdims=True) + EPS)
                y_ref[rows, _head_ds(h, GLA_DV)] = (o[h] * rstd * gn_ref[h] * (rc * _sigmoid(rc))).astype(BF16)
                s_ref[h] = jnp.exp(blb[h]) * s_prev[h] + ds[h]
            return carry

        lax.fori_loop(0, GLA_CPB, chunk, 0)

    col, whole = _gla_token_specs(lambda j: j)
    return _pcall(body, name=name, grid=(n // GLA_ROWS,),
                  in_specs=[col(1), col(2), col(3), col(4), pl.BlockSpec((GLA_ROWS, LANE), lambda j: (j, 0)),
                            whole(wup), whole(bup), whole(gn)],
                  out_specs=[pl.BlockSpec((GLA_ROWS, GLA_HEADS * GLA_DV), lambda j: (j, 0)),
                             pl.BlockSpec((GLA_HEADS, GLA_CPB, GLA_DK, GLA_DV), lambda j: (0, j, 0, 0))],
                  out_shape=[jax.ShapeDtypeStruct((n, GLA_HEADS * GLA_DV), BF16),
                             jax.ShapeDtypeStruct((GLA_HEADS, nc, GLA_DK, GLA_DV), F32)],
                  scratch_shapes=[pltpu.VMEM((GLA_HEADS, GLA_DK, GLA_DV), F32)],
                  compiler_params=_params())(za, za, za, za, al, wup, bup, gn)


def _gla_bwd(za, al, wup, bup, gn, sp, dy, du_s5, name):
    n = za.shape[0]
    nb = n // GLA_ROWS

    def body(q_ref, k_ref, v_ref, r_ref, al_ref, wup_ref, bup_ref, gn_ref, dy_ref, dus_ref, sp_ref,
             dza_ref, dz_ref, dgn_ref, dbup_ref, ds_ref):
        @pl.when(pl.program_id(0) == 0)
        def _():
            ds_ref[...] = jnp.zeros_like(ds_ref)
            dgn_ref[...] = jnp.zeros_like(dgn_ref)
            dbup_ref[...] = jnp.zeros_like(dbup_ref)

        tril, triu = _tri(True), _tri(False)
        dza_ref[:, 0:512] = dus_ref[...]
        dza_ref[:, 512:1536] = jnp.zeros((GLA_ROWS, 1024), F32)
        dz_ref[...] = jnp.zeros_like(dz_ref)

        def chunk(i, carry):
            c = GLA_CPB - 1 - i
            rows = pl.ds(pl.multiple_of(c * GLA_CHUNK, GLA_CHUNK), GLA_CHUNK)
            alc = al_ref[rows, :]
            qc = [q_ref[rows, _head_ds(h, GLA_DK)] for h in HEADS]
            kc = [k_ref[rows, _head_ds(h, GLA_DK)] for h in HEADS]
            vc = [v_ref[rows, _head_ds(h, GLA_DV)] for h in HEADS]
            s_prev = [sp_ref[h, c] for h in HEADS]
            ds = [ds_ref[h] for h in HEADS]
            z, bc, bl, blb, ebc, qt, kt, ke, sc, o = _gla_chunk_fwd(
                qc, kc, vc, alc, [wup_ref[h] for h in HEADS], [bup_ref[h] for h in HEADS], s_prev, tril)
            do = []
            for h in HEADS:
                rc = r_ref[rows, _head_ds(h, GLA_DV)]
                rs = lax.rsqrt(jnp.mean(o[h] * o[h], axis=-1, keepdims=True) + EPS)
                on = o[h] * rs
                sr = _sigmoid(rc)
                sil = rc * sr
                dyv, gnv = dy_ref[rows, _head_ds(h, GLA_DV)], gn_ref[h]
                dgn_ref[h] += jnp.sum(dyv * on * sil, axis=0, keepdims=True)
                dza_ref[rows, pl.ds(2048 + h * SLOT, GLA_DV)] = dyv * on * gnv * (sr * (1.0 + rc * (1.0 - sr)))
                don = dyv * gnv * sil
                do.append(rs * (don - on * jnp.mean(don * on, axis=-1, keepdims=True)))
            dp = [_dot(do[h], vc[h], NT) * tril for h in HEADS]
            dv1 = [_dot(sc[h], do[h], TN) for h in HEADS]
            dv2 = [_dot(ke[h], ds[h]) for h in HEADS]
            dq2 = [_dot(do[h], s_prev[h], NT) for h in HEADS]
            dke = [_dot(vc[h], ds[h], NT) for h in HEADS]
            ddec = [_dot01(jnp.ones((8, GLA_DV), F32), ds[h] * s_prev[h], NT)[0:1, :] for h in HEADS]
            dsn = [_dot(qt[h], do[h], TN) for h in HEADS]
            dq1 = [_dot(dp[h], kt[h]) for h in HEADS]
            dkt = [_dot(dp[h], qt[h], TN) for h in HEADS]
            dbc, dbl = [], []
            for h in HEADS:
                dqt = dq1[h] + dq2[h]
                dza_ref[rows, pl.ds(1536 + h * SLOT, GLA_DV)] = dv1[h] + dv2[h]
                ds_ref[h] = jnp.exp(blb[h]) * ds[h] + dsn[h]
                dza_ref[rows, pl.ds(512 + h * SLOT, GLA_DK)] = dqt * (GLA_DK ** -0.5) * ebc[h]
                dza_ref[rows, pl.ds(1024 + h * SLOT, GLA_DK)] = dkt[h] * jnp.exp(-bc[h]) + dke[h] * jnp.exp(bl[h] - bc[h])
                dbc.append(dqt * qt[h] - dkt[h] * kt[h] - dke[h] * ke[h])
                dbl.append(jnp.sum(dke[h] * ke[h], axis=0, keepdims=True) + ddec[h] * jnp.exp(bl[h]))
            dla = [_dot01(triu, dbc[h]) + dbl[h] for h in HEADS]
            for h in HEADS:
                dz = dla[h] * (1.0 - _sigmoid(z[h])) * (1.0 / GLA_TAU)
                dz_ref[rows, _head_ds(h, GLA_DK)] = dz
                dbup_ref[h] += jnp.sum(dz, axis=0, keepdims=True)
            return carry

        lax.fori_loop(0, GLA_CPB, chunk, 0)

    rev = lambda j: nb - 1 - j
    col, whole = _gla_token_specs(rev)
    tok = lambda w: pl.BlockSpec((GLA_ROWS, w), lambda j: (rev(j), 0))
    h1 = lambda w: pl.BlockSpec((GLA_HEADS, 1, w), lambda j: (0, 0, 0))
    s1 = lambda w: jax.ShapeDtypeStruct((GLA_HEADS, 1, w), F32)
    return _pcall(body, name=name, grid=(nb,),
                  in_specs=[col(1), col(2), col(3), col(4), tok(LANE), whole(wup), whole(bup), whole(gn), tok(512), tok(512),
                            pl.BlockSpec((GLA_HEADS, GLA_CPB, GLA_DK, GLA_DV), lambda j: (0, rev(j), 0, 0))],
                  out_specs=[tok(ZA_COLS), tok(GLA_HEADS * SLOT), h1(GLA_DV), h1(GLA_DK)],
                  out_shape=[jax.ShapeDtypeStruct((n, ZA_COLS), F32), jax.ShapeDtypeStruct((n, GLA_HEADS * SLOT), F32),
                             s1(GLA_DV), s1(GLA_DK)],
                  scratch_shapes=[pltpu.VMEM((GLA_HEADS, GLA_DK, GLA_DV), F32)],
                  compiler_params=_params())(za, za, za, za, al, wup, bup, gn, dy, du_s5, sp)


ANY = pl.BlockSpec(memory_space=pl.ANY)


def _place():
    x, y, c = lax.axis_index("x"), lax.axis_index("y"), lax.axis_index("c")
    chips = [(1 - x, y), (x, 1 - y), (1 - x, 1 - y)]
    return x, y, c, chips


def _remote(src, dst, ssem, rsem, dev):
    return pltpu.make_async_remote_copy(src_ref=src, dst_ref=dst, send_sem=ssem, recv_sem=rsem, device_id=dev,
                                        device_id_type=MESH_ID)


def _half(c, rows):
    h = rows // 2
    return pl.ds(pl.multiple_of(c * h, 8), h)


def _side_gather_ici(shards):
    def copies(ins, outs, ssem, rsem):
        x, y, c, chips = _place()
        mine = 2 * x + y
        cps = []
        for w in range(len(ins)):
            half = _half(c, ins[w].shape[0])
            cps.append(_remote(ins[w], outs[w].at[mine], ssem.at[4 * w], rsem.at[4 * w], (x, y, 1 - c)))
            for k, (px, py) in enumerate(chips):
                cps.append(_remote(ins[w].at[half], outs[w].at[mine, half], ssem.at[4 * w + 1 + k], rsem.at[4 * w + 1 + k],
                                   (px, py, c)))
        return cps

    return _Side(shards, [jax.ShapeDtypeStruct((4,) + s.shape, s.dtype) for s in shards], 4 * len(shards), copies)


def _side_gather_d2d(gathered):
    def copies(ins, outs, ssem, rsem):
        x, y, c, chips = _place()
        cps = []
        for w in range(len(outs)):
            half = _half(c, outs[w].shape[1])
            for k, (px, py) in enumerate(chips):
                theirs = outs[w].at[2 * px + py, half]
                cps.append(_remote(theirs, theirs, ssem.at[3 * w + k], rsem.at[3 * w + k], (x, y, 1 - c)))
        return cps

    return _Side(gathered, [jax.ShapeDtypeStruct(g.shape, g.dtype) for g in gathered], 3 * len(gathered), copies,
                 aliased=True)


def _side_swap_halves(grads):
    def copies(ins, outs, ssem, rsem):
        x, y, c, _ = _place()
        return [_remote(ins[w].at[:, _half(1 - c, ins[w].shape[1]), :], outs[w], ssem.at[w], rsem.at[w], (x, y, 1 - c))
                for w in range(len(ins))]

    return _Side(grads, [jax.ShapeDtypeStruct((4, g.shape[1] // 2, g.shape[2]), g.dtype) for g in grads], len(grads), copies)


def _side_scatter(sums):
    def copies(ins, outs, ssem, rsem):
        x, y, c, chips = _place()
        return [_remote(ins[w].at[2 * px + py], outs[w].at[k], ssem.at[3 * w + k], rsem.at[3 * w + k], (px, py, c))
                for w in range(len(ins)) for k, (px, py) in enumerate(chips)]

    return _Side(sums, [jax.ShapeDtypeStruct((3,) + s.shape[1:], s.dtype) for s in sums], 3 * len(sums), copies)


def _side_swap_reduced(halves):
    def copies(ins, outs, ssem, rsem):
        x, y, c, _ = _place()
        return [_remote(ins[w], outs[w], ssem.at[w], rsem.at[w], (x, y, 1 - c)) for w in range(len(ins))]

    return _Side(halves, [jax.ShapeDtypeStruct(h.shape, h.dtype) for h in halves], len(halves), copies)


def _chip_sum(g, recv, c_arr, name):
    _, r, cols = g.shape
    h = r // 2
    tr = _pick(h, 512, 16)
    g4 = g.reshape(4, 2, h, cols)

    def body(c_ref, g_ref, r_ref, o_ref):
        o_ref[...] = (g_ref[...] + r_ref[...]).astype(BF16)

    grid_spec = pltpu.PrefetchScalarGridSpec(
        num_scalar_prefetch=1, grid=(4, h // tr),
        in_specs=[pl.BlockSpec((None, None, tr, cols), lambda s, i, c_ref: (s, c_ref[0], i, 0)),
                  pl.BlockSpec((None, tr, cols), lambda s, i, c_ref: (s, i, 0))],
        out_specs=pl.BlockSpec((None, tr, cols), lambda s, i, c_ref: (s, i, 0)))
    return _pcall(body, name=name, grid_spec=grid_spec, out_shape=jax.ShapeDtypeStruct((4, h, cols), BF16),
                  compiler_params=_params())(c_arr, g4, recv)


def _owner_sum(sums, others, s_arr, name):
    _, h, cols = sums.shape
    tr = _pick(h, 512, 16)

    def body(s_ref, a_ref, o_ref, out_ref):
        f = lambda v: v.astype(F32)
        out_ref[...] = (f(a_ref[...]) + f(o_ref[0])) + (f(o_ref[1]) + f(o_ref[2]))

    grid_spec = pltpu.PrefetchScalarGridSpec(
        num_scalar_prefetch=1, grid=(h // tr,),
        in_specs=[pl.BlockSpec((None, tr, cols), lambda i, s_ref: (s_ref[0], i, 0)),
                  pl.BlockSpec((3, tr, cols), lambda i, s_ref: (0, i, 0))],
        out_specs=pl.BlockSpec((tr, cols), lambda i, s_ref: (i, 0)))
    return _pcall(body, name=name, grid_spec=grid_spec, out_shape=jax.ShapeDtypeStruct((h, cols), F32),
                  compiler_params=_params())(s_arr, sums, others)


def _side_small_sibling(v):
    def copies(ins, outs, ssem, rsem):
        x, y, c, _ = _place()
        return [_remote(ins[0], outs[0], ssem.at[0], rsem.at[0], (x, y, 1 - c))]

    return _Side([v], [jax.ShapeDtypeStruct(v.shape, F32)], 1, copies)


def _side_small_chips(v):
    def copies(ins, outs, ssem, rsem):
        x, y, c, chips = _place()
        return [_remote(ins[0], outs[0].at[k], ssem.at[k], rsem.at[k], (px, py, c)) for k, (px, py) in enumerate(chips)]

    return _Side([v], [jax.ShapeDtypeStruct((3,) + v.shape, F32)], 3, copies)


def _small_add(v, r, name):
    def body(v_ref, r_ref, o_ref):
        if r.ndim == 2:
            o_ref[...] = v_ref[...] + r_ref[...]
        else:
            o_ref[...] = (v_ref[...] + r_ref[0]) + (r_ref[1] + r_ref[2])

    vm = pl.BlockSpec(memory_space=pltpu.VMEM)
    return _pcall(body, name=name, in_specs=[vm, vm], out_specs=vm, out_shape=jax.ShapeDtypeStruct(v.shape, F32),
                  compiler_params=_params())(v, r)


def _merge_sides(sides):
    if len(sides) == 1:
        return sides[0]

    def copies(in_refs, out_refs, ssem, rsem):
        cps, i, o, q = [], 0, 0, 0
        for s in sides:
            ni, no = len(s.ins), len(s.out_shapes)
            cps += s.copies(in_refs[i:i + ni], out_refs[o:o + no], ssem.at[pl.ds(q, s.nsem)], rsem.at[pl.ds(q, s.nsem)])
            i, o, q = i + ni, o + no, q + s.nsem
        return cps

    assert not any(s.aliased for s in sides)
    return _Side(sum((s.ins for s in sides), []), sum((s.out_shapes for s in sides), []), sum(s.nsem for s in sides), copies)


def _tile_rows(size):
    return -(-size // (8 * LANE)) * 8


def _pack_small(parts):
    pieces = []
    for p in parts:
        flat = p.reshape(-1).astype(F32)
        pieces.append(jnp.pad(flat, (0, _tile_rows(p.size) * LANE - p.size)).reshape(-1, LANE))
    rows = sum(x.shape[0] for x in pieces)
    pieces.append(jnp.zeros(((-rows) % 64, LANE), F32))
    return jnp.concatenate(pieces, axis=0)


def _unpack_small(packed, like):
    out, pos = [], 0
    for p in like:
        rows = _tile_rows(p.size)
        out.append(packed[pos:pos + rows].reshape(-1)[:p.size].reshape(p.shape))
        pos += rows
    return out


FFN_FWD_ROWS, FFN_BWD_ROWS = 1024, 512
FFN_SUB_ROWS = 256


def _ffn_specs(n, d, fs, cap):
    rows = _pick(n, cap, 16)
    row = pl.BlockSpec((rows, d), lambda i, s: (i, 0))
    gain = pl.BlockSpec((1, d), lambda i, s: (0, 0))
    w_row = pl.BlockSpec((None, fs, d), lambda i, s: (s, 0, 0))
    hid = pl.BlockSpec((None, rows, fs), lambda i, s: (s, i, 0))
    return rows, row, gain, w_row, hid


def _ffn_fwd(h, g, w1t, w3t, w2, tag, plan):
    n, d = h.shape
    ns, fs, _ = w2.shape
    rows, row, gain, w_row, hid = _ffn_specs(n, d, fs, FFN_FWD_ROWS)
    sub = rows

    def body(h_ref, g_ref, w1_ref, w3_ref, w2_ref, out_ref, n1_ref, a_ref, b_ref, hm_ref, acc_ref):
        s = pl.program_id(1)

        @pl.when(s == 0)
        def _():
            xv = h_ref[...]
            rstd = lax.rsqrt(jnp.mean(xv * xv, axis=-1, keepdims=True) + EPS)
            n1_ref[...] = (xv * rstd * g_ref[...]).astype(BF16)
            acc_ref[...] = jnp.zeros_like(acc_ref)

        def up(j):
            n1 = n1_ref[j * sub:(j + 1) * sub, :]
            return _dot(n1, w1_ref[...], NT), _dot(n1, w3_ref[...], NT)

        cur = up(0)
        for j in range(rows // sub):
            nxt = up(j + 1) if (j + 1) * sub < rows else None
            a, b = cur
            r = slice(j * sub, (j + 1) * sub)
            hm = (a * _sigmoid(a) * b).astype(BF16)
            a_ref[r, :] = a.astype(BF16)
            b_ref[r, :] = b.astype(BF16)
            hm_ref[r, :] = hm
            acc_ref[r, :] += _dot(hm, w2_ref[...])
            cur = nxt

        @pl.when(s == ns - 1)
        def _():
            out_ref[...] = h_ref[...] + 0.5 * acc_ref[...]

    hid_shape = jax.ShapeDtypeStruct((ns, n, fs), BF16)
    plan.before(f"{tag}_fwd")
    out, n1, a, b, hm = _pcall(
        body, name=f"{tag}_fwd", grid=(n // rows, ns), in_specs=[row, gain, w_row, w_row, w_row],
        out_specs=[row, row, hid, hid, hid],
        out_shape=[jax.ShapeDtypeStruct((n, d), F32), jax.ShapeDtypeStruct((n, d), BF16), hid_shape, hid_shape, hid_shape],
        scratch_shapes=[pltpu.VMEM((rows, d), F32)], compiler_params=_params())(h, g, w1t, w3t, w2)
    plan.after(f"{tag}_fwd")
    return out, (h, n1, a, b, hm)


def _ffn_bwd(dout, saved, g, w1, w3, w2, tag, plan):
    h, n1, a, b, hm = saved
    n, d = h.shape
    ns, fs, _ = w2.shape
    rows, row, gain, w_row, hid = _ffn_specs(n, d, fs, FFN_BWD_ROWS)
    sub = _pick(rows, FFN_SUB_ROWS, 16)

    def body(do_ref, h_ref, g_ref, a_ref, b_ref, w1_ref, w3_ref, w2_ref, dh_ref, da_ref, db_ref, dg_ref, acc_ref):
        i, s = pl.program_id(0), pl.program_id(1)

        @pl.when(s == 0)
        def _():
            acc_ref[...] = jnp.zeros_like(acc_ref)

        @pl.when((s == 0) & (i == 0))
        def _():
            dg_ref[...] = jnp.zeros_like(dg_ref)

        def up(j):
            return _dot(0.5 * do_ref[j * sub:(j + 1) * sub, :], w2_ref[...], NT)

        cur = up(0)
        for j in range(rows // sub):
            nxt = up(j + 1) if (j + 1) * sub < rows else None
            r = slice(j * sub, (j + 1) * sub)
            av, bv = a_ref[r, :].astype(F32), b_ref[r, :].astype(F32)
            sg = _sigmoid(av)
            da = (cur * bv * (sg * (1.0 + av * (1.0 - sg)))).astype(BF16)
            db = (cur * av * sg).astype(BF16)
            da_ref[r, :] = da
            db_ref[r, :] = db
            acc_ref[r, :] += _dot(da, w1_ref[...]) + _dot(db, w3_ref[...])
            cur = nxt

        @pl.when(s == ns - 1)
        def _():
            xv, dn = h_ref[...], acc_ref[...]
            rstd = lax.rsqrt(jnp.mean(xv * xv, axis=-1, keepdims=True) + EPS)
            xh = xv * rstd
            dg_ref[...] += jnp.sum(dn * xh, axis=0, keepdims=True)
            dxh = dn * g_ref[...]
            dh_ref[...] = do_ref[...] + rstd * (dxh - xh * jnp.mean(dxh * xh, axis=-1, keepdims=True))

    hid_shape = jax.ShapeDtypeStruct((ns, n, fs), BF16)
    plan.before(f"{tag}_bwd")
    dh, da, db, dg = _pcall(
        body, name=f"{tag}_bwd", grid=(n // rows, ns), in_specs=[row, row, gain, hid, hid, w_row, w_row, w_row],
        out_specs=[row, hid, hid, gain],
        out_shape=[jax.ShapeDtypeStruct((n, d), F32), hid_shape, hid_shape, jax.ShapeDtypeStruct((1, d), F32)],
        scratch_shapes=[pltpu.VMEM((rows, d), F32)], compiler_params=_params())(dout, h, g, a, b, w1, w3, w2)
    plan.after(f"{tag}_bwd")
    plan.grads[f"{tag}_norm"] = dg
    plan.before(f"{tag}_gw2")
    gw2 = _mm(hm, dout, ta=True, shard='m', alpha=0.5, name=f"{tag}_gw2")
    plan.after(f"{tag}_gw2")
    plan.grads[f"{tag}_w2"] = gw2
    plan.before(f"{tag}_gw1")
    gw1 = _mm(da, n1, ta=True, shard='m', name=f"{tag}_gw1")
    plan.after(f"{tag}_gw1")
    plan.before(f"{tag}_gw3")
    gw3 = _mm(db, n1, ta=True, shard='m', name=f"{tag}_gw3")
    plan.after(f"{tag}_gw3")
    return dh, dg, gw1, gw3, gw2


def _local_step(x, tgt, plan):
    n = x.shape[0]
    grads = plan.grads

    def f(name):
        w = plan.get(name)
        return w.reshape(1, D_MODEL) if name.endswith('_norm') and name != 'gla_out_norm' else w

    def carried(tag, fn, *args, **kw):
        plan.before(tag)
        out = fn(*args, **kw)
        plan.after(tag)
        return out

    h1, ffn1 = _ffn_fwd(x, f('ffn1_norm'), f('ffn1_w1'), f('ffn1_w3'), f('ffn1_w2'), "ffn1", plan)
    u = carried("mix_rms", _rms_fwd, h1, f('mix_norm'), "mix_rms")
    w_in = f('w_in')
    w_a = jnp.concatenate([w_in[:, :512], _pad_heads(w_in[:, 512:768]), _pad_heads(w_in[:, 768:1024]), w_in[:, 1024:2048]],
                          axis=1)
    w_al = jnp.pad(w_in[:, 2048:2048 + GLA_RANK], ((0, 0), (0, LANE - GLA_RANK)))
    w_g = w_in[:, 2048 + GLA_RANK:]
    za = carried("in_a", _mm, u, w_a, name="in_a")
    zg = carried("in_g", _mm, u, w_g, name="in_g")
    al = _mm(u, w_al, name="in_al")
    ar, ai, bbar_re, bbar_im = _s5_discretize(f('s5_lambda_re'), f('s5_lambda_im'), f('s5_log_dt'), f('s5_b_re'), f('s5_b_im'))
    t_b = _bd_tiles(bbar_re.transpose(0, 2, 1), bbar_im.transpose(0, 2, 1)).astype(BF16)
    t_c = _bd_tiles(f('s5_c_re'), -f('s5_c_im')).astype(BF16)
    ar8 = jnp.broadcast_to(ar.reshape(1, S5_GP), (SEG, S5_GP))
    ai8 = jnp.broadcast_to(ai.reshape(1, S5_GP), (SEG, S5_GP))
    pw_r, pw_i = _segment_powers(ar, ai, n // SEG)
    dskip = f('s5_d').reshape(1, S5_W)
    u_s5 = _permute_rows(za[:, :S5_W])
    xs = _s5_scan(u_s5, t_b, ar8, ai8, pw_r, pw_i, "s5_scan")
    ys_p = _bd_reduce(xs, t_c, _scale_rows(u_s5, dskip, "s5_skip"), "s5_y")
    ys = _unpermute_rows(ys_p)
    zgelu = _gelu_fwd(ys, "s5_gelu")
    t_glu = _mm(zgelu, f('s5_glu_w'), bias=f('s5_glu_b').reshape(1, S5_W), name="s5_glu_t")
    y_s5 = _glu_fwd(zgelu, t_glu, "s5_glu")
    wup = jnp.pad(f('gla_a_up_w'), ((0, LANE - GLA_RANK), (0, 0)))
    wup_h = wup.reshape(LANE, GLA_HEADS, GLA_DK).transpose(1, 0, 2)
    bup_h = f('gla_a_up_b').reshape(GLA_HEADS, 1, GLA_DK)
    gn_h = f('gla_out_norm').reshape(GLA_HEADS, 1, GLA_DV)
    y_gla, s_prev = carried("gla_fwd", _gla_fwd, za, al, wup_h, bup_h, gn_h, "gla_fwd")
    ps = _mm(y_s5, f('proj_s5'), name="proj_s5")
    pg = carried("proj_gla", _mm, y_gla, f('proj_gla'), name="proj_gla")
    merged = _merge_fwd(zg, ps, pg, "merge")
    h2 = _mm(merged, f('w_out'), res=h1, name="w_out")
    h3, ffn2 = _ffn_fwd(h2, f('ffn2_norm'), f('ffn2_w1'), f('ffn2_w3'), f('ffn2_w2'), "ffn2", plan)
    loss, dh3, g_final = _final_loss(h3, f('final_norm').reshape(1, D_MODEL), tgt, "loss")
    grads['final_norm'] = g_final.reshape(D_MODEL)
    dh2, grads['ffn2_norm'], grads['ffn2_w1'], grads['ffn2_w3'], grads['ffn2_w2'] = _ffn_bwd(
        dh3, ffn2, f('ffn2_norm'), f('ffn2_w1'), f('ffn2_w3'), f('ffn2_w2'), "ffn2", plan)
    dm = _mm(dh2, f('w_out'), tb=True, name="d_merged")
    grads['w_out'] = _mm(merged, dh2, ta=True, name="g_w_out")
    dps, dpg, dzg = carried("d_merge", _merge_bwd, dm, zg, ps, pg, "d_merge")
    grads['proj_s5'] = _mm(y_s5, dps, ta=True, name="g_proj_s5")
    grads['proj_gla'] = _mm(y_gla, dpg, ta=True, name="g_proj_gla")
    dy_s5 = _mm(dps, f('proj_s5'), tb=True, name="d_y_s5")
    dy_gla = _mm(dpg, f('proj_gla'), tb=True, name="d_y_gla")
    dzgelu, dt_glu, g_glu_b = _glu_bwd1(dy_s5, zgelu, t_glu, "d_glu")
    grads['s5_glu_b'] = g_glu_b.reshape(S5_W)
    grads['s5_glu_w'] = _mm(zgelu, dt_glu, ta=True, name="g_glu_w")
    dzgelu = _mm(dt_glu, f('s5_glu_w'), tb=True, res=dzgelu, name="d_gelu")
    dys, du_skip, g_d = _glu_bwd2(_permute_rows(dzgelu), ys_p, u_s5, dskip, "d_s5_y")
    grads['s5_d'] = g_d.reshape(S5_G, S5_H)
    lam, da8 = _s5_scan_bwd(dys, t_c, xs, ar8, ai8, pw_r, pw_i, "s5_scan_bwd")
    g_c = _bd_blocks(_bd_outer(dys, xs, "g_s5_c"))
    grads['s5_c_re'], grads['s5_c_im'] = g_c[0], -g_c[1]
    g_b = _bd_blocks(_bd_outer(u_s5, lam, "g_s5_b")).transpose(0, 1, 3, 2)
    g_bbar_re, g_bbar_im = g_b[0], g_b[1]
    da = jnp.sum(da8, axis=0)
    g_ar, g_ai = da[:S5_GP].reshape(S5_G, S5_P), da[S5_GP:].reshape(S5_G, S5_P)
    _, disc_vjp = jax.vjp(_s5_discretize, f('s5_lambda_re'), f('s5_lambda_im'), f('s5_log_dt'), f('s5_b_re'), f('s5_b_im'))
    (grads['s5_lambda_re'], grads['s5_lambda_im'], grads['s5_log_dt'], grads['s5_b_re'],
     grads['s5_b_im']) = disc_vjp((g_ar, g_ai, g_bbar_re, g_bbar_im))
    du_s5 = _unpermute_rows(_bd_reduce(lam, t_b, du_skip, "d_s5_u"))
    dza, dz, dgn, dbup = carried("gla_bwd", _gla_bwd, za, al, wup_h, bup_h, gn_h, s_prev, dy_gla, du_s5, "gla_bwd")
    grads['gla_out_norm'] = dgn.reshape(GLA_HEADS * GLA_DV)
    grads['gla_a_up_b'] = dbup.reshape(GLA_HEADS * GLA_DK)
    grads['gla_a_up_w'] = _unpad_heads(_mm(al, dz, ta=True, name="g_a_up")[:GLA_RANK])
    dal = _mm(dz, _pad_heads(wup), tb=True, name="d_a_low")
    g_wa = _mm(u, dza, ta=True, name="g_in_a")
    g_wg = _mm(u, dzg, ta=True, name="g_in_g")
    g_wal = _mm(u, dal, ta=True, name="g_in_al")
    grads['w_in'] = jnp.concatenate([g_wa[:, :512], _unpad_heads(g_wa[:, 512:1024]), _unpad_heads(g_wa[:, 1024:1536]),
                                     g_wa[:, 1536:], g_wal[:, :GLA_RANK], g_wg], axis=1)
    du = carried("d_u_a", _mm, dza, w_a, tb=True, name="d_u_a")
    du = _mm(dzg, w_g, tb=True, res=du, name="d_u_g")
    du = _mm(dal, w_al, tb=True, res=du, name="d_u_al")
    dh1, g_mix = carried("d_mix_rms", _rms_bwd, h1, f('mix_norm'), du, dh2, "d_mix_rms")
    grads['mix_norm'] = g_mix
    dx, grads['ffn1_norm'], grads['ffn1_w1'], grads['ffn1_w3'], grads['ffn1_w2'] = _ffn_bwd(
        dh1, ffn1, f('ffn1_norm'), f('ffn1_w1'), f('ffn1_w3'), f('ffn1_w2'), "ffn1", plan)
    return loss[0, 0], dx


MIXER_WEIGHTS = ['w_in', 's5_glu_w', 'proj_s5', 'proj_gla', 'w_out', 'gla_a_up_w']
FFN1_WEIGHTS, FFN2_WEIGHTS = FFN_WEIGHTS[:3], FFN_WEIGHTS[3:]
TRANSPOSED = ['ffn1_w1', 'ffn1_w3', 'ffn2_w1', 'ffn2_w3']


def _local_shard(w, nm):
    return jnp.swapaxes(w, 1, 2)[0] if nm in TRANSPOSED else w[0]
FFN1_EARLY = ['ffn1_w2']
GRAD_GROUPS = {'ffn2': FFN2_WEIGHTS, 'mixer': ['w_out', 'proj_s5', 'proj_gla', 's5_glu_w', 'w_in'], 'ffn1': FFN1_WEIGHTS}


class _Plan:
    def __init__(self, a, c_arr, s_arr):
        self.a, self.c_arr, self.s_arr = a, c_arr, s_arr
        self.grads, self.weights, self.riding = {}, {}, {}
        self.g4s, self.chip_sums, self.halves, self.sib_halves = {}, {}, {}, {}
        for nm in SMALL:
            if nm != 'gla_a_up_w':
                self.weights[nm] = a[nm] if nm == 'final_norm' else a[nm][0]
        ici = _side_gather_ici(self._shards(FFN1_WEIGHTS))
        _run_side(ici, "gather_ffn1_ici")
        self._gathered(FFN1_WEIGHTS, _run_side(_side_gather_d2d(ici.outs), "gather_ffn1_d2d"))

    def _shards(self, names):
        return [_local_shard(self.a[nm], nm).astype(F32 if nm == 'gla_a_up_w' else BF16) for nm in names]

    def _gathered(self, names, arrs):
        for nm, g4 in zip(names, arrs):
            if nm in FFN_WEIGHTS:
                self.weights[nm] = g4
            elif nm in COL_SHARDED:
                self.weights[nm] = jnp.concatenate([g4[s] for s in range(4)], axis=1)
            else:
                self.weights[nm] = g4.reshape(4 * g4.shape[1], g4.shape[2])

    def get(self, name):
        return self.weights[name]

    def _shard_major(self, nm):
        g = self.grads[nm]
        if nm in FFN_WEIGHTS:
            return g
        if nm in COL_SHARDED:
            return jnp.stack(jnp.split(g, 4, axis=1))
        return g.reshape(4, g.shape[0] // 4, g.shape[1])

    def _schedule(self, tag):
        grp = GRAD_GROUPS
        gathers = {"ffn1_fwd": ('ici', MIXER_WEIGHTS), "mix_rms": ('d2d', MIXER_WEIGHTS),
                   "in_a": ('ici', FFN2_WEIGHTS[:1]), "in_g": ('d2d', FFN2_WEIGHTS[:1]),
                   "gla_fwd": ('ici', FFN2_WEIGHTS[1:]), "proj_gla": ('d2d', FFN2_WEIGHTS[1:])}
        if tag in gathers:
            kind, names = gathers[tag]
            key = tuple(names)
            if kind == 'ici':
                return [(_side_gather_ici(self._shards(names)), lambda outs: self.riding.update({key: outs}))]
            return [(_side_gather_d2d(self.riding[key]), lambda outs: self._gathered(names, outs))]
        steps = {"d_merge": (grp['ffn2'], 0), "gla_bwd": (grp['ffn2'], 1), "d_mix_rms": (grp['ffn2'], 2),
                 "d_u_a": (grp['mixer'], 0), "ffn1_bwd": (grp['mixer'], 1), "ffn1_gw2": (grp['mixer'], 2),
                 "ffn1_gw1": (FFN1_EARLY, 0), "ffn1_gw3": (FFN1_EARLY, 1)}
        entries = [self._reduce_stage(*steps[tag])] if tag in steps else []
        if tag == "ffn1_gw2":
            entries.append(self._small_stage(0))
        if tag == "ffn1_gw1":
            entries.append(self._small_stage(1))
        return entries

    def _small_stage(self, stage):
        if stage == 0:
            a, grads = self.a, self.grads
            self.small_parts = [grads[nm].reshape(a[nm].shape) for nm in SMALL if nm != 'gla_a_up_w'] + [grads['gla_a_up_w']]
            packed = _pack_small(self.small_parts)

            def done(outs):
                self.small_pair = _small_add(packed, outs[0], "small_sum_pair")
            return _side_small_sibling(packed), done

        def done(outs):
            self.small_total = _small_add(self.small_pair, outs[0], "small_sum_chips")
        return _side_small_chips(self.small_pair), done

    def _reduce_stage(self, names, stage):
        if stage == 0:
            for nm in names:
                self.g4s[nm] = self._shard_major(nm)

            def done(outs):
                for nm, r in zip(names, outs):
                    self.chip_sums[nm] = _chip_sum(self.g4s[nm], r, self.c_arr, f"chip_sum_{nm}")
            return _side_swap_halves([self.g4s[nm] for nm in names]), done
        if stage == 1:
            def done(outs):
                for nm, o in zip(names, outs):
                    self.halves[nm] = _owner_sum(self.chip_sums[nm], o, self.s_arr, f"owner_sum_{nm}")
            return _side_scatter([self.chip_sums[nm] for nm in names]), done

        def done(outs):
            self.sib_halves.update(zip(names, outs))
        return _side_swap_reduced([self.halves[nm] for nm in names]), done

    def before(self, tag):
        entries = self._schedule(tag)
        if entries:
            merged = _merge_sides([side for side, _ in entries])
            self.riding[tag] = (merged, entries)
            _RIDER.append(merged)

    def after(self, tag):
        if tag in self.riding:
            merged, entries = self.riding.pop(tag)
            assert not _RIDER and merged.outs is not None, tag
            pos = 0
            for side, done in entries:
                done(merged.outs[pos:pos + len(side.out_shapes)])
                pos += len(side.out_shapes)

    def finish(self):
        late = [nm for nm in GRAD_GROUPS['ffn1'] if nm not in FFN1_EARLY]
        for stage, names in ((0, late), (1, late), (2, GRAD_GROUPS['ffn1'])):
            side, done = self._reduce_stage(names, stage)
            done(_run_side(side, f"grad_ffn1_stage{stage}"))


def _train_step(a):
    x = a['x'][0]
    tgt = a['loss_target'][0]
    xi, yi, ci = lax.axis_index("x"), lax.axis_index("y"), lax.axis_index("c")
    c_arr = jnp.reshape(ci, (1,)).astype(jnp.int32)
    s_arr = jnp.reshape(2 * xi + yi, (1,)).astype(jnp.int32)
    plan = _Plan(a, c_arr, s_arr)
    loss, dx = _local_step(x, tgt, plan)
    plan.finish()
    grads = plan.grads
    loss = lax.psum(loss, ("x", "y", "c"))
    halves = [plan.halves[nm] for nm in SHARDED]
    sib_halves = [plan.sib_halves[nm] for nm in SHARDED]
    red = {}
    small_sum = _unpack_small(plan.small_total, plan.small_parts)
    small_names = [nm for nm in SMALL if nm != 'gla_a_up_w']
    for nm, g in zip(small_names, small_sum[:-1]):
        red[nm] = g
    g_up = small_sum[-1]
    red['gla_a_up_w'] = lax.dynamic_slice(g_up, (0, (2 * xi + yi) * GLA_DK), (GLA_RANK, GLA_DK))
    out_g, out_d, out_m, out_v = {}, {}, {}, {}
    for nm, own, sib in zip(SHARDED, halves, sib_halves):
        if nm == 'w_in':
            g_nat = jnp.where(ci == 0, jnp.concatenate([own, sib], axis=0), jnp.concatenate([sib, own], axis=0))
            g_t = g_nat.T
            loc_t = lambda pre: jnp.swapaxes(a[pre + nm], 1, 2)[0]
            res = [g_t] + list(_adamw_cols(loc_t(''), g_t, loc_t('m_'), loc_t('v_'), f"adamw_{nm}"))
            out_g[nm], out_d[nm], out_m[nm], out_v[nm] = (jnp.swapaxes(t[None], 1, 2) for t in res)
            continue
        loc = lambda pre: _local_shard(a[pre + nm], nm)
        res = _adamw_halves(loc(''), own, sib, loc('m_'), loc('v_'), c_arr, f"adamw_{nm}")
        back = (lambda t: jnp.swapaxes(t[None], 1, 2)) if nm in TRANSPOSED else (lambda t: t[None])
        out_g[nm], out_d[nm], out_m[nm], out_v[nm] = (back(t) for t in res)
    rest = [nm for nm in WEIGHTS if nm not in SHARDED]
    pk = lambda pre: _pack_small([a[pre + nm] for nm in rest])
    d, nm_, nv_ = _adamw(pk(''), _pack_small([red[nm] for nm in rest]), pk('m_'), pk('v_'), "adamw_small")
    like = [a[nm] for nm in rest]
    for nm, g, dd, mm_, vv_ in zip(rest, [red[nm].reshape(a[nm].shape) for nm in rest], _unpack_small(d, like),
                                   _unpack_small(nm_, like), _unpack_small(nv_, like)):
        out_g[nm], out_d[nm], out_m[nm], out_v[nm] = g, dd, mm_, vv_
    return (loss, dx[None], *[out_g[nm] for nm in WEIGHTS], *[out_d[nm] for nm in WEIGHTS],
            *[out_m[nm] for nm in WEIGHTS], *[out_v[nm] for nm in WEIGHTS])


def kernel(x, ffn1_norm, ffn1_w1, ffn1_w3, ffn1_w2, mix_norm, w_in, s5_lambda_re, s5_lambda_im, s5_log_dt, s5_b_re, s5_b_im, s5_c_re, s5_c_im, s5_d, s5_glu_w, s5_glu_b, gla_a_up_w, gla_a_up_b, gla_out_norm, proj_s5, proj_gla, w_out, ffn2_norm, ffn2_w1, ffn2_w3, ffn2_w2, final_norm, loss_target, m_ffn1_norm, m_ffn1_w1, m_ffn1_w3, m_ffn1_w2, m_mix_norm, m_w_in, m_s5_lambda_re, m_s5_lambda_im, m_s5_log_dt, m_s5_b_re, m_s5_b_im, m_s5_c_re, m_s5_c_im, m_s5_d, m_s5_glu_w, m_s5_glu_b, m_gla_a_up_w, m_gla_a_up_b, m_gla_out_norm, m_proj_s5, m_proj_gla, m_w_out, m_ffn2_norm, m_ffn2_w1, m_ffn2_w3, m_ffn2_w2, m_final_norm, v_ffn1_norm, v_ffn1_w1, v_ffn1_w3, v_ffn1_w2, v_mix_norm, v_w_in, v_s5_lambda_re, v_s5_lambda_im, v_s5_log_dt, v_s5_b_re, v_s5_b_im, v_s5_c_re, v_s5_c_im, v_s5_d, v_s5_glu_w, v_s5_glu_b, v_gla_a_up_w, v_gla_a_up_b, v_gla_out_norm, v_proj_s5, v_proj_gla, v_w_out, v_ffn2_norm, v_ffn2_w1, v_ffn2_w3, v_ffn2_w2, v_final_norm):
    return _train_step(dict(locals()))
```

```python
import functools

import jax
import jax.numpy as jnp
from jax import lax
from jax.experimental import pallas as pl
from jax.experimental.pallas import tpu as pltpu

F32 = jnp.float32
BF16 = jnp.bfloat16
HI = lax.Precision.HIGHEST
MESH_ID = pl.DeviceIdType.MESH

D_MODEL = 1024
EPS = 1e-6
S5_G, S5_P, S5_H = 32, 64, 16
S5_W = S5_G * S5_H
S5_GP = S5_G * S5_P
SEG = 8
SCAN_ROWS = 256
GLA_HEADS, GLA_DK, GLA_DV = 4, 64, 128
GLA_CHUNK = 64
GLA_TAU = 16.0
GLA_RANK = 16
ADAM_LR, ADAM_B1, ADAM_B2, ADAM_EPS, ADAM_WD, ADAM_STEP = 0.001, 0.9, 0.999, 1e-08, 0.01, 10
V7X_VMEM_LIMIT = 56 * 1024 * 1024
LANE = 128

WEIGHTS = ['ffn1_norm', 'ffn1_w1', 'ffn1_w3', 'ffn1_w2', 'mix_norm', 'w_in', 's5_lambda_re', 's5_lambda_im',
           's5_log_dt', 's5_b_re', 's5_b_im', 's5_c_re', 's5_c_im', 's5_d', 's5_glu_w', 's5_glu_b', 'gla_a_up_w',
           'gla_a_up_b', 'gla_out_norm', 'proj_s5', 'proj_gla', 'w_out', 'ffn2_norm', 'ffn2_w1', 'ffn2_w3',
           'ffn2_w2', 'final_norm']
SHARDED = ['ffn1_w1', 'ffn1_w3', 'ffn1_w2', 'w_in', 's5_glu_w', 'proj_s5', 'proj_gla', 'w_out',
           'ffn2_w1', 'ffn2_w3', 'ffn2_w2']
COL_SHARDED = ['ffn1_w1', 'ffn1_w3', 'w_in', 'proj_s5', 'proj_gla', 'ffn2_w1', 'ffn2_w3', 'gla_a_up_w']
SMALL = [n for n in WEIGHTS if n not in SHARDED]
FFN_WEIGHTS = ['ffn1_w1', 'ffn1_w3', 'ffn1_w2', 'ffn2_w1', 'ffn2_w3', 'ffn2_w2']


def _params(**kw):
    return pltpu.CompilerParams(vmem_limit_bytes=V7X_VMEM_LIMIT, **kw)


class _Side:
    def __init__(self, ins, out_shapes, nsem, copies, aliased=False):
        self.ins, self.out_shapes, self.nsem, self.copies, self.aliased = list(ins), list(out_shapes), nsem, copies, aliased
        self.outs = None


_RIDER = []


def _pcall(body, **kw):
    if _RIDER:
        return _carry(body, _RIDER.pop(), **kw)
    return pl.pallas_call(body, **kw)


def _carry(body, side, *, name, grid, in_specs, out_specs, out_shape, scratch_shapes=(), compiler_params=None):
    del compiler_params
    single = not isinstance(out_shape, (list, tuple))
    out_specs = [out_specs] if single else list(out_specs)
    out_shape = [out_shape] if single else list(out_shape)
    n_in, n_out, n_scr = len(in_specs), len(out_shape), len(scratch_shapes)
    s_in, s_out = len(side.ins), len(side.out_shapes)
    any_spec = pl.BlockSpec(memory_space=pl.ANY)

    def wrapped(*refs):
        cuts = [n_in, s_in, n_out, s_out, n_scr]
        parts, pos = [], 0
        for c in cuts:
            parts.append(refs[pos:pos + c])
            pos += c
        ins, sins, outs, souts, scr = parts
        ssem, rsem = refs[pos], refs[pos + 1]
        first = last = None
        for d, g in enumerate(grid):
            i = pl.program_id(d)
            first = (i == 0) if first is None else first & (i == 0)
            last = (i == g - 1) if last is None else last & (i == g - 1)

        @pl.when(first)
        def _():
            for cp in side.copies(sins, souts, ssem, rsem):
                cp.start()

        body(*ins, *outs, *scr)

        @pl.when(last)
        def _():
            for cp in side.copies(sins, souts, ssem, rsem):
                cp.wait()

    call = pl.pallas_call(
        wrapped, name=name, grid=grid, in_specs=list(in_specs) + [any_spec] * s_in,
        out_specs=out_specs + [any_spec] * s_out, out_shape=out_shape + side.out_shapes,
        scratch_shapes=list(scratch_shapes) + [pltpu.SemaphoreType.DMA((side.nsem,)), pltpu.SemaphoreType.DMA((side.nsem,))],
        input_output_aliases={n_in + j: n_out + j for j in range(s_in)} if side.aliased else {},
        compiler_params=_params(has_side_effects=True))

    def run(*args):
        res = call(*args, *side.ins)
        side.outs = list(res[n_out:])
        return res[0] if single else list(res[:n_out])

    return run


def _run_side(side, name):
    s_in, s_out = len(side.ins), len(side.out_shapes)
    any_spec = pl.BlockSpec(memory_space=pl.ANY)

    def body(*refs):
        sins, souts = refs[:s_in], refs[s_in:s_in + s_out]
        ssem, rsem = refs[s_in + s_out:]
        cps = side.copies(sins, souts, ssem, rsem)
        for cp in cps:
            cp.start()
        for cp in cps:
            cp.wait()

    side.outs = list(pl.pallas_call(
        body, name=name, in_specs=[any_spec] * s_in, out_specs=[any_spec] * s_out, out_shape=side.out_shapes,
        scratch_shapes=[pltpu.SemaphoreType.DMA((side.nsem,)), pltpu.SemaphoreType.DMA((side.nsem,))],
        input_output_aliases={j: j for j in range(s_in)} if side.aliased else {},
        compiler_params=pltpu.CompilerParams(has_side_effects=True))(*side.ins))
    return side.outs


def _pick(n, cap, quantum):
    if n <= cap:
        return n
    best = None
    for t in range(quantum, cap + 1, quantum):
        if n % t == 0:
            best = t
    assert best is not None, (n, cap, quantum)
    return best


def _sigmoid(x):
    return jax.nn.sigmoid(x)


def _mm(a, b, *, name, ta=False, tb=False, out_dtype=F32, alpha=1.0, res=None, bias=None, exact=False, shard=None):
    ns = 4
    (k_a, m) = a.shape[-2:] if ta else a.shape[-2:][::-1]
    (k_b, n) = b.shape[-2:][::-1] if tb else b.shape[-2:]
    assert k_a == k_b, (a.shape, b.shape, ta, tb)
    assert (a.ndim == 3) == (shard in ('k', 'm')) and (b.ndim == 3) == (shard in ('n', 'k'))
    k = k_a
    tm = _pick(m, 1024, 128)
    tn = _pick(n, 1024, 128)
    tk = _pick(k, 1024, 128)
    pm, pn, pk = m // tm, n // tn, k // tk
    gm = pm * (ns if shard == 'm' else 1)
    gn = pn * (ns if shard == 'n' else 1)
    gk = pk * (ns if shard == 'k' else 1)
    dims = (((0,) if ta else (1,), (1,) if tb else (0,)), ((), ()))
    op_dtype = F32 if exact else BF16

    def body(*refs):
        a_ref, b_ref = refs[0], refs[1]
        pos = 2
        res_ref = bias_ref = None
        if res is not None:
            res_ref = refs[pos]
            pos += 1
        if bias is not None:
            bias_ref = refs[pos]
            pos += 1
        o_ref, acc_ref = refs[pos], refs[pos + 1]
        kk = pl.program_id(2)

        @pl.when(kk == 0)
        def _():
            acc_ref[...] = jnp.zeros_like(acc_ref)

        acc_ref[...] += lax.dot_general(a_ref[...].astype(op_dtype), b_ref[...].astype(op_dtype), dims,
                                        precision=HI if exact else None, preferred_element_type=F32)

        @pl.when(kk == gk - 1)
        def _():
            o = acc_ref[...]
            if alpha != 1.0:
                o = o * alpha
            if bias_ref is not None:
                o = o + bias_ref[...]
            if res_ref is not None:
                o = o + res_ref[...]
            o_ref[...] = o.astype(out_dtype)

    def spec(block, sharded_on, order):
        per = {'m': pm, 'n': pn, 'k': pk}

        def index(i, j, kk):
            g = {'m': i, 'n': j, 'k': kk}
            r, c = order(i % pm if shard == 'm' else i, j % pn if shard == 'n' else j, kk % pk if shard == 'k' else kk)
            if sharded_on is None:
                return (r, c)
            return (g[sharded_on] // per[sharded_on], r, c)

        return pl.BlockSpec(block if sharded_on is None else (None,) + block, index)

    a_sh = shard if shard in ('k', 'm') else None
    b_sh = shard if shard in ('n', 'k') else None
    o_sh = shard if shard in ('n', 'm') else None
    a_spec = spec((tk, tm), a_sh, lambda i, j, kk: (kk, i)) if ta else spec((tm, tk), a_sh, lambda i, j, kk: (i, kk))
    b_spec = spec((tn, tk), b_sh, lambda i, j, kk: (j, kk)) if tb else spec((tk, tn), b_sh, lambda i, j, kk: (kk, j))
    ins, in_specs = [a, b], [a_spec, b_spec]
    if res is not None:
        assert o_sh is None
        ins.append(res)
        in_specs.append(pl.BlockSpec((tm, tn), lambda i, j, kk: (i, j)))
    if bias is not None:
        assert o_sh is None
        ins.append(bias)
        in_specs.append(pl.BlockSpec((1, tn), lambda i, j, kk: (0, j)))
    out_shape = (m, n) if o_sh is None else (ns, m, n)
    return _pcall(body, name=name, grid=(gm, gn, gk), in_specs=in_specs,
                  out_specs=spec((tm, tn), o_sh, lambda i, j, kk: (i, j)),
                  out_shape=jax.ShapeDtypeStruct(out_shape, out_dtype),
                  scratch_shapes=[pltpu.VMEM((tm, tn), F32)], compiler_params=_params())(*ins)


ROWS_VMEM_BUDGET = 24 * 1024 * 1024


def _rows(body, ins, outs, *, n, name):
    cols = sum(a.shape[1] for a, kind in ins if kind == 'r') + sum(c for c, _, kind in outs if kind == 'r')
    cap = 256
    while cap < 2048 and 2 * 4 * cols * (2 * cap) <= ROWS_VMEM_BUDGET:
        cap *= 2
    tm = _pick(n, cap, 16)
    in_specs = []
    for arr, kind in ins:
        if kind == 'r':
            in_specs.append(pl.BlockSpec((tm, arr.shape[1]), lambda i: (i, 0)))
        else:
            in_specs.append(pl.BlockSpec(arr.shape, lambda i: (0, 0)))
    out_specs, out_shape = [], []
    for cols, dtype, kind in outs:
        if kind == 'r':
            out_specs.append(pl.BlockSpec((tm, cols), lambda i: (i, 0)))
            out_shape.append(jax.ShapeDtypeStruct((n, cols), dtype))
        else:
            out_specs.append(pl.BlockSpec((1, cols), lambda i: (0, 0)))
            out_shape.append(jax.ShapeDtypeStruct((1, cols), dtype))
    n_in = len(ins)
    acc_ids = [j for j, o in enumerate(outs) if o[2] == 'a']

    def wrapped(*refs):
        if acc_ids:
            @pl.when(pl.program_id(0) == 0)
            def _():
                for j in acc_ids:
                    refs[n_in + j][...] = jnp.zeros_like(refs[n_in + j])
        body(*refs)

    res = _pcall(wrapped, name=name, grid=(n // tm,), in_specs=in_specs, out_specs=out_specs, out_shape=out_shape,
                 compiler_params=_params())(*[a for a, _ in ins])
    return res


def _rms_fwd(x, g, name):
    def body(x_ref, g_ref, o_ref):
        xv = x_ref[...]
        rstd = lax.rsqrt(jnp.mean(xv * xv, axis=-1, keepdims=True) + EPS)
        o_ref[...] = (xv * rstd * g_ref[...]).astype(BF16)
    return _rows(body, [(x, 'r'), (g, 'f')], [(x.shape[1], BF16, 'r')], n=x.shape[0], name=name)[0]


def _rms_bwd(x, g, dn, dres, name):
    def body(x_ref, g_ref, dn_ref, dres_ref, dx_ref, dg_ref):
        xv = x_ref[...]
        rstd = lax.rsqrt(jnp.mean(xv * xv, axis=-1, keepdims=True) + EPS)
        xh = xv * rstd
        dn = dn_ref[...]
        dg_ref[...] += jnp.sum(dn * xh, axis=0, keepdims=True)
        dxh = dn * g_ref[...]
        dx_ref[...] = dres_ref[...] + rstd * (dxh - xh * jnp.mean(dxh * xh, axis=-1, keepdims=True))
    d = x.shape[1]
    return _rows(body, [(x, 'r'), (g, 'f'), (dn, 'r'), (dres, 'r')], [(d, F32, 'r'), (d, F32, 'a')],
                 n=x.shape[0], name=name)


def _gelu_parts(y):
    c0 = 0.7978845608028654
    inner = c0 * (y + 0.044715 * y * y * y)
    th = jnp.tanh(inner)
    return th, c0 * (1.0 + 3.0 * 0.044715 * y * y)


def _gelu_fwd(y, name):
    def body(y_ref, o_ref):
        yv = y_ref[...]
        th, _ = _gelu_parts(yv)
        o_ref[...] = 0.5 * yv * (1.0 + th)
    return _rows(body, [(y, 'r')], [(y.shape[1], F32, 'r')], n=y.shape[0], name=name)[0]


def _glu_fwd(zg, t, name):
    def body(z_ref, t_ref, o_ref):
        o_ref[...] = (z_ref[...] * _sigmoid(t_ref[...])).astype(BF16)
    return _rows(body, [(zg, 'r'), (t, 'r')], [(zg.shape[1], BF16, 'r')], n=zg.shape[0], name=name)[0]


def _glu_bwd1(dy, zg, t, name):
    def body(dy_ref, z_ref, t_ref, dz_ref, dt_ref, db_ref):
        dyv, zv = dy_ref[...], z_ref[...]
        sg = _sigmoid(t_ref[...])
        dz_ref[...] = dyv * sg
        dt = dyv * zv * sg * (1.0 - sg)
        dt_ref[...] = dt.astype(BF16)
        db_ref[...] += jnp.sum(dt, axis=0, keepdims=True)
    w = zg.shape[1]
    return _rows(body, [(dy, 'r'), (zg, 'r'), (t, 'r')], [(w, F32, 'r'), (w, BF16, 'r'), (w, F32, 'a')],
                 n=zg.shape[0], name=name)


def _glu_bwd2(dzg, ys, u, dskip, name):
    def body(dz_ref, y_ref, u_ref, d_ref, dy_ref, du_ref, dd_ref):
        yv = y_ref[...]
        th, dinner = _gelu_parts(yv)
        dy = dz_ref[...] * (0.5 * (1.0 + th) + 0.5 * yv * (1.0 - th * th) * dinner)
        dy_ref[...] = dy
        du_ref[...] = dy * d_ref[...]
        dd_ref[...] += jnp.sum(dy * u_ref[...], axis=0, keepdims=True)
    w = ys.shape[1]
    return _rows(body, [(dzg, 'r'), (ys, 'r'), (u, 'r'), (dskip, 'f')], [(w, F32, 'r'), (w, F32, 'r'), (w, F32, 'a')],
                 n=ys.shape[0], name=name)


def _scale_rows(u, dskip, name):
    def body(u_ref, d_ref, o_ref):
        o_ref[...] = u_ref[...] * d_ref[...]
    return _rows(body, [(u, 'r'), (dskip, 'f')], [(u.shape[1], F32, 'r')], n=u.shape[0], name=name)[0]


def _merge_fwd(zg, ps, pg, name):
    def body(z_ref, ps_ref, pg_ref, o_ref):
        zv = z_ref[...]
        o_ref[...] = (_sigmoid(zv[:, :D_MODEL]) * ps_ref[...] + _sigmoid(zv[:, D_MODEL:]) * pg_ref[...]).astype(BF16)
    return _rows(body, [(zg, 'r'), (ps, 'r'), (pg, 'r')], [(D_MODEL, BF16, 'r')], n=zg.shape[0], name=name)[0]


def _merge_bwd(dm, zg, ps, pg, name):
    def body(dm_ref, z_ref, ps_ref, pg_ref, dps_ref, dpg_ref, dz_ref):
        dmv, zv = dm_ref[...], z_ref[...]
        s1, s2 = _sigmoid(zv[:, :D_MODEL]), _sigmoid(zv[:, D_MODEL:])
        dps_ref[...] = (dmv * s1).astype(BF16)
        dpg_ref[...] = (dmv * s2).astype(BF16)
        dz_ref[:, :D_MODEL] = dmv * ps_ref[...] * s1 * (1.0 - s1)
        dz_ref[:, D_MODEL:] = dmv * pg_ref[...] * s2 * (1.0 - s2)
    return _rows(body, [(dm, 'r'), (zg, 'r'), (ps, 'r'), (pg, 'r')],
                 [(D_MODEL, BF16, 'r'), (D_MODEL, BF16, 'r'), (2 * D_MODEL, F32, 'r')], n=zg.shape[0], name=name)


def _final_loss(h, g, tgt, name):
    def body(h_ref, g_ref, t_ref, loss_ref, dh_ref, dg_ref):
        hv = h_ref[...]
        rstd = lax.rsqrt(jnp.mean(hv * hv, axis=-1, keepdims=True) + EPS)
        xh = hv * rstd
        err = xh * g_ref[...] - t_ref[...]
        part = 0.5 * jnp.sum(jnp.mean(err * err, axis=-1, keepdims=True), axis=0, keepdims=True)
        loss_ref[...] += jnp.broadcast_to(part, loss_ref.shape)
        dout = err * (1.0 / hv.shape[1])
        dg_ref[...] += jnp.sum(dout * xh, axis=0, keepdims=True)
        dxh = dout * g_ref[...]
        dh_ref[...] = rstd * (dxh - xh * jnp.mean(dxh * xh, axis=-1, keepdims=True))
    d = h.shape[1]
    return _rows(body, [(h, 'r'), (g, 'f'), (tgt, 'r')], [(LANE, F32, 'a'), (d, F32, 'r'), (d, F32, 'a')],
                 n=h.shape[0], name=name)


def _adamw_math(wv, gv, mv, vv):
    nm = ADAM_B1 * mv + (1.0 - ADAM_B1) * gv
    nv = ADAM_B2 * vv + (1.0 - ADAM_B2) * (gv * gv)
    m_hat = nm / (1.0 - ADAM_B1 ** ADAM_STEP)
    v_hat = nv / (1.0 - ADAM_B2 ** ADAM_STEP)
    return -ADAM_LR * (m_hat / (jnp.sqrt(v_hat) + ADAM_EPS) + ADAM_WD * wv), nm, nv


def _adamw(w, g, m, v, name):
    def body(w_ref, g_ref, m_ref, v_ref, d_ref, nm_ref, nv_ref):
        d_ref[...], nm_ref[...], nv_ref[...] = _adamw_math(w_ref[...], g_ref[...], m_ref[...], v_ref[...])
    c = w.shape[1]
    return _rows(body, [(w, 'r'), (g, 'r'), (m, 'r'), (v, 'r')], [(c, F32, 'r')] * 3, n=w.shape[0], name=name)


ADAMW_BLOCKS = 8


def _adamw_group(items, c_arr, name):
    per = ADAMW_BLOCKS // 2
    n = len(items)

    def body(c_ref, *refs):
        mine = (pl.program_id(0) // per) == c_ref[0]
        for k in range(n):
            w_ref, go_ref, gs_ref, m_ref, v_ref = refs[5 * k:5 * k + 5]
            g_ref, d_ref, nm_ref, nv_ref = refs[5 * n + 4 * k:5 * n + 4 * k + 4]
            gv = jnp.where(mine, go_ref[...], gs_ref[...])
            g_ref[...] = gv
            d_ref[...], nm_ref[...], nv_ref[...] = _adamw_math(w_ref[...], gv, m_ref[...], v_ref[...])

    in_specs, out_specs, out_shape, args = [pl.BlockSpec(memory_space=pltpu.SMEM)], [], [], [c_arr]
    for item in items:
        r, cols = item[0].shape
        assert r % (8 * ADAMW_BLOCKS) == 0, item[0].shape
        tr = r // ADAMW_BLOCKS
        full = pl.BlockSpec((tr, cols), lambda i: (i, 0))
        half = pl.BlockSpec((tr, cols), lambda i: (i % per, 0))
        in_specs += [full, half, half, full, full]
        out_specs += [full] * 4
        out_shape += [jax.ShapeDtypeStruct((r, cols), F32)] * 4
        args += list(item)
    return _pcall(body, name=name, grid=(ADAMW_BLOCKS,), in_specs=in_specs, out_specs=out_specs, out_shape=out_shape,
                  compiler_params=_params())(*args)


def _shift_rows(v, sh, down):
    rolled = pltpu.roll(v, sh if down else v.shape[0] - sh, axis=0)
    row = lax.broadcasted_iota(jnp.int32, v.shape, 0)
    keep = (row >= sh) if down else (row < v.shape[0] - sh)
    return jnp.where(keep, rolled, 0.0)


def _chain_segments(st_r, st_i, pw_r_ref, pw_i_ref, conj, down):
    vr, vi = st_r[...], st_i[...]
    sh, k = 1, 0
    while sh < SEG:
        pr, pi = pw_r_ref[k:k + 1, :], pw_i_ref[k:k + 1, :]
        if conj:
            pi = -pi
        sr, si = _shift_rows(vr, sh, down), _shift_rows(vi, sh, down)
        vr, vi = vr + pr * sr - pi * si, vi + pr * si + pi * sr
        sh, k = sh * 2, k + 1
    st_r[...] = _shift_rows(vr, 1, down)
    st_i[...] = _shift_rows(vi, 1, down)


def _expand_block(u_ref, t_ref, bu_ref):
    for j in range(BD_TILES):
        k = j % 4
        bu_ref[:, j * BD_ST:(j + 1) * BD_ST] = _dot(u_ref[:, k * BD_CH:(k + 1) * BD_CH], t_ref[j])


def _s5_scan(u, tiles, ar8, ai8, pw_r, pw_i, name):
    n = u.shape[0]
    rb = SCAN_ROWS
    nb, steps, lc = n // rb, rb // SEG, 512

    def body(u_ref, t_ref, ar_ref, ai_ref, pwr_ref, pwi_ref, x_ref, st_r, st_i, bu_ref):
        ph, b = pl.program_id(0), pl.program_id(1)

        @pl.when((ph == 0) & (b == 0))
        def _():
            st_r[...] = jnp.zeros_like(st_r)
            st_i[...] = jnp.zeros_like(st_i)

        _expand_block(u_ref, t_ref, bu_ref)

        def scan(store):
            for c in range(S5_GP // lc):
                re, im = slice(c * lc, (c + 1) * lc), slice(S5_GP + c * lc, S5_GP + (c + 1) * lc)
                a_r, a_i = ar_ref[:, re], ai_ref[:, re]

                def step(s, carry):
                    xr, xi = carry
                    rows = pl.ds(pl.multiple_of(s * SEG, SEG), SEG)
                    nr = a_r * xr - a_i * xi + bu_ref[rows, re]
                    ni = a_r * xi + a_i * xr + bu_ref[rows, im]
                    if store:
                        x_ref[rows, re] = nr
                        x_ref[rows, im] = ni
                    return nr, ni

                xr, xi = lax.fori_loop(0, steps, step, (st_r[:, re], st_i[:, re]), unroll=4)
                st_r[:, re] = xr
                st_i[:, re] = xi

        @pl.when(ph == 0)
        def _():
            scan(False)

        @pl.when((ph == 0) & (b == nb - 1))
        def _():
            _chain_segments(st_r, st_i, pwr_ref, pwi_ref, conj=False, down=True)

        @pl.when(ph == 1)
        def _():
            scan(True)

    full = lambda a: pl.BlockSpec(a.shape, lambda ph, b: (0, 0))
    return _pcall(body, name=name, grid=(2, nb),
                  in_specs=[pl.BlockSpec((rb, S5_W), lambda ph, b: (b, 0)), pl.BlockSpec(tiles.shape, lambda ph, b: (0, 0, 0)),
                            full(ar8), full(ai8), full(pw_r), full(pw_i)],
                  out_specs=pl.BlockSpec((rb, 2 * S5_GP), lambda ph, b: (b * ph, 0)),
                  out_shape=jax.ShapeDtypeStruct((n, 2 * S5_GP), F32),
                  scratch_shapes=[pltpu.VMEM((SEG, S5_GP), F32), pltpu.VMEM((SEG, S5_GP), F32),
                                  pltpu.VMEM((rb, 2 * S5_GP), F32)],
                  compiler_params=_params())(u, tiles, ar8, ai8, pw_r, pw_i)


def _s5_scan_bwd(dy, tiles, xs, ar8, ai8, pw_r, pw_i, name):
    n = dy.shape[0]
    rb = SCAN_ROWS
    nb, steps, lc = n // rb, rb // SEG, 256

    def body(dy_ref, t_ref, x_ref, ar_ref, ai_ref, pwr_ref, pwi_ref, lam_ref, da_ref, st_r, st_i, gx_ref):
        ph, b = pl.program_id(0), pl.program_id(1)

        @pl.when((ph == 0) & (b == 0))
        def _():
            st_r[...] = jnp.zeros_like(st_r)
            st_i[...] = jnp.zeros_like(st_i)
            da_ref[...] = jnp.zeros_like(da_ref)

        _expand_block(dy_ref, t_ref, gx_ref)

        def scan(store):
            for c in range(S5_GP // lc):
                re, im = slice(c * lc, (c + 1) * lc), slice(S5_GP + c * lc, S5_GP + (c + 1) * lc)
                a_r, a_i = ar_ref[:, re], ai_ref[:, re]

                def step(s, carry):
                    rows = pl.ds(pl.multiple_of((steps - 1 - s) * SEG, SEG), SEG)
                    if store:
                        lr, li, dr, di = carry
                        xr, xi = x_ref[rows, re], x_ref[rows, im]
                        dr = dr + lr * xr + li * xi
                        di = di + li * xr - lr * xi
                    else:
                        lr, li = carry
                    nr = a_r * lr + a_i * li + gx_ref[rows, re]
                    ni = a_r * li - a_i * lr + gx_ref[rows, im]
                    if store:
                        lam_ref[rows, re] = nr
                        lam_ref[rows, im] = ni
                        return nr, ni, dr, di
                    return nr, ni

                if store:
                    lr, li, dr, di = lax.fori_loop(0, steps, step, (st_r[:, re], st_i[:, re], da_ref[:, re], da_ref[:, im]),
                                                   unroll=4)
                    da_ref[:, re] = dr
                    da_ref[:, im] = di
                else:
                    lr, li = lax.fori_loop(0, steps, step, (st_r[:, re], st_i[:, re]), unroll=4)
                st_r[:, re] = lr
                st_i[:, re] = li

        @pl.when(ph == 0)
        def _():
            scan(False)

        @pl.when((ph == 0) & (b == nb - 1))
        def _():
            _chain_segments(st_r, st_i, pwr_ref, pwi_ref, conj=True, down=False)

        @pl.when(ph == 1)
        def _():
            scan(True)

    full = lambda a: pl.BlockSpec(a.shape, lambda ph, b: (0, 0))
    rev = lambda ph, b: (nb - 1 - b, 0)
    return _pcall(body, name=name, grid=(2, nb),
                  in_specs=[pl.BlockSpec((rb, S5_W), rev), pl.BlockSpec(tiles.shape, lambda ph, b: (0, 0, 0)),
                            pl.BlockSpec((rb, 2 * S5_GP), lambda ph, b: ((nb - 1 - b) * ph, 0)),
                            full(ar8), full(ai8), full(pw_r), full(pw_i)],
                  out_specs=[pl.BlockSpec((rb, 2 * S5_GP), lambda ph, b: (nb - 1 - b * ph, 0)),
                             pl.BlockSpec((SEG, 2 * S5_GP), lambda ph, b: (0, 0))],
                  out_shape=[jax.ShapeDtypeStruct((n, 2 * S5_GP), F32), jax.ShapeDtypeStruct((SEG, 2 * S5_GP), F32)],
                  scratch_shapes=[pltpu.VMEM((SEG, S5_GP), F32), pltpu.VMEM((SEG, S5_GP), F32),
                                  pltpu.VMEM((rb, 2 * S5_GP), F32)],
                  compiler_params=_params())(dy, tiles, xs, ar8, ai8, pw_r, pw_i)


def _s5_discretize(lam_re, lam_im, log_dt, b_re, b_im):
    dt = jnp.exp(log_dt)[:, None]
    mag = jnp.exp(lam_re * dt)
    ar = mag * jnp.cos(lam_im * dt)
    ai = mag * jnp.sin(lam_im * dt)
    den = lam_re * lam_re + lam_im * lam_im
    nr = ar - 1.0
    fr = (nr * lam_re + ai * lam_im) / den
    fi = (ai * lam_re - nr * lam_im) / den
    bbar_re = fr[:, :, None] * b_re - fi[:, :, None] * b_im
    bbar_im = fr[:, :, None] * b_im + fi[:, :, None] * b_re
    return ar, ai, bbar_re, bbar_im


BD_TILES, BD_CH, BD_ST, BD_GROUPS = 8, 128, 512, 8
BD_ROWS = 4096


def _bd_tiles(re, im):
    eye = jnp.eye(BD_GROUPS, dtype=re.dtype)

    def tiles(t):
        t = t.reshape(S5_G // BD_GROUPS, BD_GROUPS, S5_H, S5_P)
        return (t[:, :, :, None, :] * eye[None, :, None, :, None]).reshape(S5_G // BD_GROUPS, BD_CH, BD_ST)

    return jnp.concatenate([tiles(re), tiles(im)], axis=0)


def _bd_blocks(t):
    t = t.reshape(2, S5_G // BD_GROUPS, BD_GROUPS, S5_H, BD_GROUPS, S5_P)
    return jnp.einsum('rkahap->rkahp', t).reshape(2, S5_G, S5_H, S5_P)


def _bd_reduce(x, t, res, name):
    n = x.shape[0]
    tm = _pick(n, BD_ROWS, 16)

    def body(x_ref, t_ref, r_ref, o_ref):
        part = _dot(x_ref[...], t_ref[...], NT)

        @pl.when(pl.program_id(2) == 0)
        def _():
            o_ref[...] = r_ref[...] + part

        @pl.when(pl.program_id(2) == 1)
        def _():
            o_ref[...] += part

    return _pcall(body, name=name, grid=(n // tm, 4, 2),
                  in_specs=[pl.BlockSpec((tm, BD_ST), lambda i, k, r: (i, k + 4 * r)),
                            pl.BlockSpec((None, BD_CH, BD_ST), lambda i, k, r: (k + 4 * r, 0, 0)),
                            pl.BlockSpec((tm, BD_CH), lambda i, k, r: (i, k))],
                  out_specs=pl.BlockSpec((tm, BD_CH), lambda i, k, r: (i, k)),
                  out_shape=jax.ShapeDtypeStruct((n, S5_W), F32), compiler_params=_params())(x, t, res)


def _bd_outer(a, x, name):
    n = a.shape[0]
    tk = _pick(n, BD_ROWS, 16)
    nk = n // tk

    def body(a_ref, x_ref, o_ref):
        part = _dot(a_ref[...], x_ref[...], TN)

        @pl.when(pl.program_id(1) == 0)
        def _():
            o_ref[...] = part

        @pl.when(pl.program_id(1) > 0)
        def _():
            o_ref[...] += part

    return _pcall(body, name=name, grid=(BD_TILES, nk),
                  in_specs=[pl.BlockSpec((tk, BD_CH), lambda j, kk: (kk, j % 4)), pl.BlockSpec((tk, BD_ST), lambda j, kk: (kk, j))],
                  out_specs=pl.BlockSpec((None, BD_CH, BD_ST), lambda j, kk: (j, 0, 0)),
                  out_shape=jax.ShapeDtypeStruct((BD_TILES, BD_CH, BD_ST), F32), compiler_params=_params())(a, x)


def _permute_rows(t):
    n = t.shape[0]
    return t.reshape(SEG, n // SEG, t.shape[1]).transpose(1, 0, 2).reshape(n, t.shape[1])


def _unpermute_rows(t):
    n = t.shape[0]
    return t.reshape(n // SEG, SEG, t.shape[1]).transpose(1, 0, 2).reshape(n, t.shape[1])


def _segment_powers(ar, ai, seg_steps):
    pr, pi = ar.reshape(1, S5_GP), ai.reshape(1, S5_GP)
    e = 1
    while e < seg_steps:
        pr, pi = pr * pr - pi * pi, 2.0 * pr * pi
        e *= 2
    assert e == seg_steps, "segment length must be a power of two"
    rows_r, rows_i = [], []
    for _ in range(3):
        rows_r.append(pr)
        rows_i.append(pi)
        pr, pi = pr * pr - pi * pi, 2.0 * pr * pi
    pad = jnp.zeros((SEG - 3, S5_GP), F32)
    return jnp.concatenate(rows_r + [pad], axis=0), jnp.concatenate(rows_i + [pad], axis=0)


NT = (((1,), (1,)), ((), ()))
TN = (((0,), (0,)), ((), ()))


def _dot(a, b, dims=None, exact=False):
    dims = (((1,), (0,)), ((), ())) if dims is None else dims
    if exact:
        return lax.dot_general(a, b, dims, precision=HI, preferred_element_type=F32)
    return lax.dot_general(a.astype(BF16), b.astype(BF16), dims, preferred_element_type=F32)


def _dot01(a, b, dims=None, ones_first=True):
    x = b if ones_first else a
    hi = x.astype(BF16)
    r1 = x - hi.astype(F32)
    mid = r1.astype(BF16)
    lo = (r1 - mid.astype(F32)).astype(BF16)
    parts = [(_dot(a, p, dims) if ones_first else _dot(p, b, dims)) for p in (lo, mid, hi)]
    return (parts[0] + parts[1]) + parts[2]


HEADS = range(4)


def _gla_chunk_fwd(qc, kc, vc, al, wup, bup, s_prev, tril):
    ones = jnp.ones((GLA_CHUNK, GLA_DV), F32)
    z = [_dot(al, wup[h]) + bup[h] for h in HEADS]
    la = [(jnp.minimum(z[h], 0.0) - jnp.log(1.0 + jnp.exp(-jnp.abs(z[h])))) * (1.0 / GLA_TAU) for h in HEADS]
    bc = [_dot01(tril, la[h]) for h in HEADS]
    blb = [_dot01(la[h], ones, TN, ones_first=False) for h in HEADS]
    bl = [bc[h][GLA_CHUNK - 1:GLA_CHUNK, :] for h in HEADS]
    ebc = [jnp.exp(bc[h]) for h in HEADS]
    qt = [qc[h] * (GLA_DK ** -0.5) * ebc[h] for h in HEADS]
    kt = [kc[h] * jnp.exp(-bc[h]) for h in HEADS]
    ke = [kc[h] * jnp.exp(bl[h] - bc[h]) for h in HEADS]
    sc = [_dot(qt[h], kt[h], NT) * tril for h in HEADS]
    oi = [_dot(sc[h], vc[h]) for h in HEADS]
    oo = [_dot(qt[h], s_prev[h]) for h in HEADS]
    o = [oi[h] + oo[h] for h in HEADS]
    return z, bc, bl, blb, ebc, qt, kt, ke, sc, o


GLA_ROWS = 512
GLA_CPB = GLA_ROWS // GLA_CHUNK


ZA_COLS = 5 * 512
SLOT = 128


def _pad_heads(w):
    r = w.shape[0]
    return jnp.pad(w.reshape(r, GLA_HEADS, GLA_DK), ((0, 0), (0, 0), (0, SLOT - GLA_DK))).reshape(r, GLA_HEADS * SLOT)


def _unpad_heads(w):
    r = w.shape[0]
    return w.reshape(r, GLA_HEADS, SLOT)[:, :, :GLA_DK].reshape(r, GLA_HEADS * GLA_DK)


def _gla_token_specs(blk):
    col = lambda cb: pl.BlockSpec((GLA_ROWS, 512), lambda j: (blk(j), cb))
    whole = lambda a: pl.BlockSpec(a.shape, lambda j: (0,) * a.ndim)
    return col, whole


def _head_ds(h, width):
    return pl.ds(h * SLOT, width)


def _tri(lower):
    ri = lax.broadcasted_iota(jnp.int32, (GLA_CHUNK, GLA_CHUNK), 0)
    ci = lax.broadcasted_iota(jnp.int32, (GLA_CHUNK, GLA_CHUNK), 1)
    return ((ri >= ci) if lower else (ri <= ci)).astype(F32)


def _gla_fwd(za, al, wup, bup, gn, name):
    n = za.shape[0]
    nc = n // GLA_CHUNK

    def body(q_ref, k_ref, v_ref, r_ref, al_ref, wup_ref, bup_ref, gn_ref, y_ref, sp_ref, s_ref):
        @pl.when(pl.program_id(0) == 0)
        def _():
            s_ref[...] = jnp.zeros_like(s_ref)

        tril = _tri(True)

        def chunk(c, carry):
            rows = pl.ds(pl.multiple_of(c * GLA_CHUNK, GLA_CHUNK), GLA_CHUNK)
            alc = al_ref[rows, :]
            vc = [v_ref[rows, _head_ds(h, GLA_DV)] for h in HEADS]
            s_prev = [s_ref[h] for h in HEADS]
            _, _, _, blb, _, _, _, ke, _, o = _gla_chunk_fwd(
                [q_ref[rows, _head_ds(h, GLA_DK)] for h in HEADS], [k_ref[rows, _head_ds(h, GLA_DK)] for h in HEADS],
                vc, alc, [wup_ref[h] for h in HEADS], [bup_ref[h] for h in HEADS], s_prev, tril)
            ds = [_dot(ke[h], vc[h], TN) for h in HEADS]
            for h in HEADS:
                rc = r_ref[rows, _head_ds(h, GLA_DV)]
                sp_ref[h, c] = s_prev[h]
                rstd = lax.rsqrt(jnp.mean(o[h] * o[h], axis=-1, keepdims=True) + EPS)
                y_ref[rows, _head_ds(h, GLA_DV)] = (o[h] * rstd * gn_ref[h] * (rc * _sigmoid(rc))).astype(BF16)
                s_ref[h] = jnp.exp(blb[h]) * s_prev[h] + ds[h]
            return carry

        lax.fori_loop(0, GLA_CPB, chunk, 0)

    col, whole = _gla_token_specs(lambda j: j)
    return _pcall(body, name=name, grid=(n // GLA_ROWS,),
                  in_specs=[col(1), col(2), col(3), col(4), pl.BlockSpec((GLA_ROWS, LANE), lambda j: (j, 0)),
                            whole(wup), whole(bup), whole(gn)],
                  out_specs=[pl.BlockSpec((GLA_ROWS, GLA_HEADS * GLA_DV), lambda j: (j, 0)),
                             pl.BlockSpec((GLA_HEADS, GLA_CPB, GLA_DK, GLA_DV), lambda j: (0, j, 0, 0))],
                  out_shape=[jax.ShapeDtypeStruct((n, GLA_HEADS * GLA_DV), BF16),
                             jax.ShapeDtypeStruct((GLA_HEADS, nc, GLA_DK, GLA_DV), F32)],
                  scratch_shapes=[pltpu.VMEM((GLA_HEADS, GLA_DK, GLA_DV), F32)],
                  compiler_params=_params())(za, za, za, za, al, wup, bup, gn)


def _gla_bwd(za, al, wup, bup, gn, sp, dy, du_s5, name):
    n = za.shape[0]
    nb = n // GLA_ROWS

    def body(q_ref, k_ref, v_ref, r_ref, al_ref, wup_ref, bup_ref, gn_ref, dy_ref, dus_ref, sp_ref,
             dza_ref, dz_ref, dgn_ref, dbup_ref, ds_ref):
        @pl.when(pl.program_id(0) == 0)
        def _():
            ds_ref[...] = jnp.zeros_like(ds_ref)
            dgn_ref[...] = jnp.zeros_like(dgn_ref)
            dbup_ref[...] = jnp.zeros_like(dbup_ref)

        tril, triu = _tri(True), _tri(False)
        dza_ref[:, 0:512] = dus_ref[...]
        dza_ref[:, 512:1536] = jnp.zeros((GLA_ROWS, 1024), F32)
        dz_ref[...] = jnp.zeros_like(dz_ref)

        def chunk(i, carry):
            c = GLA_CPB - 1 - i
            rows = pl.ds(pl.multiple_of(c * GLA_CHUNK, GLA_CHUNK), GLA_CHUNK)
            alc = al_ref[rows, :]
            qc = [q_ref[rows, _head_ds(h, GLA_DK)] for h in HEADS]
            kc = [k_ref[rows, _head_ds(h, GLA_DK)] for h in HEADS]
            vc = [v_ref[rows, _head_ds(h, GLA_DV)] for h in HEADS]
            s_prev = [sp_ref[h, c] for h in HEADS]
            ds = [ds_ref[h] for h in HEADS]
            z, bc, bl, blb, ebc, qt, kt, ke, sc, o = _gla_chunk_fwd(
                qc, kc, vc, alc, [wup_ref[h] for h in HEADS], [bup_ref[h] for h in HEADS], s_prev, tril)
            do = []
            for h in HEADS:
                rc = r_ref[rows, _head_ds(h, GLA_DV)]
                rs = lax.rsqrt(jnp.mean(o[h] * o[h], axis=-1, keepdims=True) + EPS)
                on = o[h] * rs
                sr = _sigmoid(rc)
                sil = rc * sr
                dyv, gnv = dy_ref[rows, _head_ds(h, GLA_DV)], gn_ref[h]
                dgn_ref[h] += jnp.sum(dyv * on * sil, axis=0, keepdims=True)
                dza_ref[rows, pl.ds(2048 + h * SLOT, GLA_DV)] = dyv * on * gnv * (sr * (1.0 + rc * (1.0 - sr)))
                don = dyv * gnv * sil
                do.append(rs * (don - on * jnp.mean(don * on, axis=-1, keepdims=True)))
            dp = [_dot(do[h], vc[h], NT) * tril for h in HEADS]
            dv1 = [_dot(sc[h], do[h], TN) for h in HEADS]
            dv2 = [_dot(ke[h], ds[h]) for h in HEADS]
            dq2 = [_dot(do[h], s_prev[h], NT) for h in HEADS]
            dke = [_dot(vc[h], ds[h], NT) for h in HEADS]
            ddec = [_dot01(jnp.ones((8, GLA_DV), F32), ds[h] * s_prev[h], NT)[0:1, :] for h in HEADS]
            dsn = [_dot(qt[h], do[h], TN) for h in HEADS]
            dq1 = [_dot(dp[h], kt[h]) for h in HEADS]
            dkt = [_dot(dp[h], qt[h], TN) for h in HEADS]
            dbc, dbl = [], []
            for h in HEADS:
                dqt = dq1[h] + dq2[h]
                dza_ref[rows, pl.ds(1536 + h * SLOT, GLA_DV)] = dv1[h] + dv2[h]
                ds_ref[h] = jnp.exp(blb[h]) * ds[h] + dsn[h]
                dza_ref[rows, pl.ds(512 + h * SLOT, GLA_DK)] = dqt * (GLA_DK ** -0.5) * ebc[h]
                dza_ref[rows, pl.ds(1024 + h * SLOT, GLA_DK)] = dkt[h] * jnp.exp(-bc[h]) + dke[h] * jnp.exp(bl[h] - bc[h])
                dbc.append(dqt * qt[h] - dkt[h] * kt[h] - dke[h] * ke[h])
                dbl.append(jnp.sum(dke[h] * ke[h], axis=0, keepdims=True) + ddec[h] * jnp.exp(bl[h]))
            dla = [_dot01(triu, dbc[h]) + dbl[h] for h in HEADS]
            for h in HEADS:
                dz = dla[h] * (1.0 - _sigmoid(z[h])) * (1.0 / GLA_TAU)
                dz_ref[rows, _head_ds(h, GLA_DK)] = dz
                dbup_ref[h] += jnp.sum(dz, axis=0, keepdims=True)
            return carry

        lax.fori_loop(0, GLA_CPB, chunk, 0)

    rev = lambda j: nb - 1 - j
    col, whole = _gla_token_specs(rev)
    tok = lambda w: pl.BlockSpec((GLA_ROWS, w), lambda j: (rev(j), 0))
    h1 = lambda w: pl.BlockSpec((GLA_HEADS, 1, w), lambda j: (0, 0, 0))
    s1 = lambda w: jax.ShapeDtypeStruct((GLA_HEADS, 1, w), F32)
    return _pcall(body, name=name, grid=(nb,),
                  in_specs=[col(1), col(2), col(3), col(4), tok(LANE), whole(wup), whole(bup), whole(gn), tok(512), tok(512),
                            pl.BlockSpec((GLA_HEADS, GLA_CPB, GLA_DK, GLA_DV), lambda j: (0, rev(j), 0, 0))],
                  out_specs=[tok(ZA_COLS), tok(GLA_HEADS * SLOT), h1(GLA_DV), h1(GLA_DK)],
                  out_shape=[jax.ShapeDtypeStruct((n, ZA_COLS), F32), jax.ShapeDtypeStruct((n, GLA_HEADS * SLOT), F32),
                             s1(GLA_DV), s1(GLA_DK)],
                  scratch_shapes=[pltpu.VMEM((GLA_HEADS, GLA_DK, GLA_DV), F32)],
                  compiler_params=_params())(za, za, za, za, al, wup, bup, gn, dy, du_s5, sp)


ANY = pl.BlockSpec(memory_space=pl.ANY)


def _place():
    x, y, c = lax.axis_index("x"), lax.axis_index("y"), lax.axis_index("c")
    chips = [(1 - x, y), (x, 1 - y), (1 - x, 1 - y)]
    return x, y, c, chips


def _remote(src, dst, ssem, rsem, dev):
    return pltpu.make_async_remote_copy(src_ref=src, dst_ref=dst, send_sem=ssem, recv_sem=rsem, device_id=dev,
                                        device_id_type=MESH_ID)


def _half(c, rows):
    h = rows // 2
    return pl.ds(pl.multiple_of(c * h, 8), h)


def _side_gather_ici(shards):
    def copies(ins, outs, ssem, rsem):
        x, y, c, chips = _place()
        mine = 2 * x + y
        cps = []
        for w in range(len(ins)):
            half = _half(c, ins[w].shape[0])
            cps.append(_remote(ins[w], outs[w].at[mine], ssem.at[4 * w], rsem.at[4 * w], (x, y, 1 - c)))
            for k, (px, py) in enumerate(chips):
                cps.append(_remote(ins[w].at[half], outs[w].at[mine, half], ssem.at[4 * w + 1 + k], rsem.at[4 * w + 1 + k],
                                   (px, py, c)))
        return cps

    return _Side(shards, [jax.ShapeDtypeStruct((4,) + s.shape, s.dtype) for s in shards], 4 * len(shards), copies)


def _side_gather_d2d(gathered):
    def copies(ins, outs, ssem, rsem):
        x, y, c, chips = _place()
        cps = []
        for w in range(len(outs)):
            half = _half(c, outs[w].shape[1])
            for k, (px, py) in enumerate(chips):
                theirs = outs[w].at[2 * px + py, half]
                cps.append(_remote(theirs, theirs, ssem.at[3 * w + k], rsem.at[3 * w + k], (x, y, 1 - c)))
        return cps

    return _Side(gathered, [jax.ShapeDtypeStruct(g.shape, g.dtype) for g in gathered], 3 * len(gathered), copies,
                 aliased=True)


def _side_swap_halves(grads):
    def copies(ins, outs, ssem, rsem):
        x, y, c, _ = _place()
        return [_remote(ins[w].at[:, _half(1 - c, ins[w].shape[1]), :], outs[w], ssem.at[w], rsem.at[w], (x, y, 1 - c))
                for w in range(len(ins))]

    return _Side(grads, [jax.ShapeDtypeStruct((4, g.shape[1] // 2, g.shape[2]), g.dtype) for g in grads], len(grads), copies)


def _side_scatter(sums):
    def copies(ins, outs, ssem, rsem):
        x, y, c, chips = _place()
        return [_remote(ins[w].at[2 * px + py], outs[w].at[k], ssem.at[3 * w + k], rsem.at[3 * w + k], (px, py, c))
                for w in range(len(ins)) for k, (px, py) in enumerate(chips)]

    return _Side(sums, [jax.ShapeDtypeStruct((3,) + s.shape[1:], s.dtype) for s in sums], 3 * len(sums), copies)


def _side_swap_reduced(halves):
    def copies(ins, outs, ssem, rsem):
        x, y, c, _ = _place()
        return [_remote(ins[w], outs[w], ssem.at[w], rsem.at[w], (x, y, 1 - c)) for w in range(len(ins))]

    return _Side(halves, [jax.ShapeDtypeStruct(h.shape, h.dtype) for h in halves], len(halves), copies)


def _chip_sum(g, recv, c_arr, name):
    _, r, cols = g.shape
    h = r // 2
    tr = _pick(h, 512, 16)
    g4 = g.reshape(4, 2, h, cols)

    def body(c_ref, g_ref, r_ref, o_ref):
        o_ref[...] = (g_ref[...] + r_ref[...]).astype(BF16)

    grid_spec = pltpu.PrefetchScalarGridSpec(
        num_scalar_prefetch=1, grid=(4, h // tr),
        in_specs=[pl.BlockSpec((None, None, tr, cols), lambda s, i, c_ref: (s, c_ref[0], i, 0)),
                  pl.BlockSpec((None, tr, cols), lambda s, i, c_ref: (s, i, 0))],
        out_specs=pl.BlockSpec((None, tr, cols), lambda s, i, c_ref: (s, i, 0)))
    return _pcall(body, name=name, grid_spec=grid_spec, out_shape=jax.ShapeDtypeStruct((4, h, cols), BF16),
                  compiler_params=_params())(c_arr, g4, recv)


def _owner_sum(sums, others, s_arr, name):
    _, h, cols = sums.shape
    tr = _pick(h, 512, 16)

    def body(s_ref, a_ref, o_ref, out_ref):
        f = lambda v: v.astype(F32)
        out_ref[...] = (f(a_ref[...]) + f(o_ref[0])) + (f(o_ref[1]) + f(o_ref[2]))

    grid_spec = pltpu.PrefetchScalarGridSpec(
        num_scalar_prefetch=1, grid=(h // tr,),
        in_specs=[pl.BlockSpec((None, tr, cols), lambda i, s_ref: (s_ref[0], i, 0)),
                  pl.BlockSpec((3, tr, cols), lambda i, s_ref: (0, i, 0))],
        out_specs=pl.BlockSpec((tr, cols), lambda i, s_ref: (i, 0)))
    return _pcall(body, name=name, grid_spec=grid_spec, out_shape=jax.ShapeDtypeStruct((h, cols), F32),
                  compiler_params=_params())(s_arr, sums, others)


def _side_small_sibling(v):
    def copies(ins, outs, ssem, rsem):
        x, y, c, _ = _place()
        return [_remote(ins[0], outs[0], ssem.at[0], rsem.at[0], (x, y, 1 - c))]

    return _Side([v], [jax.ShapeDtypeStruct(v.shape, F32)], 1, copies)


def _side_small_chips(v):
    def copies(ins, outs, ssem, rsem):
        x, y, c, chips = _place()
        return [_remote(ins[0], outs[0].at[k], ssem.at[k], rsem.at[k], (px, py, c)) for k, (px, py) in enumerate(chips)]

    return _Side([v], [jax.ShapeDtypeStruct((3,) + v.shape, F32)], 3, copies)


def _small_add(v, r, name):
    def body(v_ref, r_ref, o_ref):
        if r.ndim == 2:
            o_ref[...] = v_ref[...] + r_ref[...]
        else:
            o_ref[...] = (v_ref[...] + r_ref[0]) + (r_ref[1] + r_ref[2])

    vm = pl.BlockSpec(memory_space=pltpu.VMEM)
    return _pcall(body, name=name, in_specs=[vm, vm], out_specs=vm, out_shape=jax.ShapeDtypeStruct(v.shape, F32),
                  compiler_params=_params())(v, r)


def _merge_sides(sides):
    if len(sides) == 1:
        return sides[0]

    def copies(in_refs, out_refs, ssem, rsem):
        cps, i, o, q = [], 0, 0, 0
        for s in sides:
            ni, no = len(s.ins), len(s.out_shapes)
            cps += s.copies(in_refs[i:i + ni], out_refs[o:o + no], ssem.at[pl.ds(q, s.nsem)], rsem.at[pl.ds(q, s.nsem)])
            i, o, q = i + ni, o + no, q + s.nsem
        return cps

    assert not any(s.aliased for s in sides)
    return _Side(sum((s.ins for s in sides), []), sum((s.out_shapes for s in sides), []), sum(s.nsem for s in sides), copies)


def _tile_rows(size):
    return -(-size // (8 * LANE)) * 8


def _pack_small(parts):
    pieces = []
    for p in parts:
        flat = p.reshape(-1).astype(F32)
        pieces.append(jnp.pad(flat, (0, _tile_rows(p.size) * LANE - p.size)).reshape(-1, LANE))
    rows = sum(x.shape[0] for x in pieces)
    pieces.append(jnp.zeros(((-rows) % 64, LANE), F32))
    return jnp.concatenate(pieces, axis=0)


def _unpack_small(packed, like):
    out, pos = [], 0
    for p in like:
        rows = _tile_rows(p.size)
        out.append(packed[pos:pos + rows].reshape(-1)[:p.size].reshape(p.shape))
        pos += rows
    return out


FFN_FWD_ROWS, FFN_BWD_ROWS = 1024, 512
FFN_SUB_ROWS = 256


def _ffn_specs(n, d, fs, cap):
    rows = _pick(n, cap, 16)
    row = pl.BlockSpec((rows, d), lambda i, s: (i, 0))
    gain = pl.BlockSpec((1, d), lambda i, s: (0, 0))
    w_row = pl.BlockSpec((None, fs, d), lambda i, s: (s, 0, 0))
    hid = pl.BlockSpec((None, rows, fs), lambda i, s: (s, i, 0))
    return rows, row, gain, w_row, hid


def _ffn_fwd(h, g, w1t, w3t, w2, tag, plan):
    n, d = h.shape
    ns, fs, _ = w2.shape
    rows, row, gain, w_row, hid = _ffn_specs(n, d, fs, FFN_FWD_ROWS)
    sub = rows

    def body(h_ref, g_ref, w1_ref, w3_ref, w2_ref, out_ref, n1_ref, a_ref, b_ref, hm_ref, acc_ref):
        s = pl.program_id(1)

        @pl.when(s == 0)
        def _():
            xv = h_ref[...]
            rstd = lax.rsqrt(jnp.mean(xv * xv, axis=-1, keepdims=True) + EPS)
            n1_ref[...] = (xv * rstd * g_ref[...]).astype(BF16)
            acc_ref[...] = jnp.zeros_like(acc_ref)

        def up(j):
            n1 = n1_ref[j * sub:(j + 1) * sub, :]
            return _dot(n1, w1_ref[...], NT), _dot(n1, w3_ref[...], NT)

        cur = up(0)
        for j in range(rows // sub):
            nxt = up(j + 1) if (j + 1) * sub < rows else None
            a, b = cur
            r = slice(j * sub, (j + 1) * sub)
            hm = (a * _sigmoid(a) * b).astype(BF16)
            a_ref[r, :] = a.astype(BF16)
            b_ref[r, :] = b.astype(BF16)
            hm_ref[r, :] = hm
            acc_ref[r, :] += _dot(hm, w2_ref[...])
            cur = nxt

        @pl.when(s == ns - 1)
        def _():
            out_ref[...] = h_ref[...] + 0.5 * acc_ref[...]

    hid_shape = jax.ShapeDtypeStruct((ns, n, fs), BF16)
    plan.before(f"{tag}_fwd")
    out, n1, a, b, hm = _pcall(
        body, name=f"{tag}_fwd", grid=(n // rows, ns), in_specs=[row, gain, w_row, w_row, w_row],
        out_specs=[row, row, hid, hid, hid],
        out_shape=[jax.ShapeDtypeStruct((n, d), F32), jax.ShapeDtypeStruct((n, d), BF16), hid_shape, hid_shape, hid_shape],
        scratch_shapes=[pltpu.VMEM((rows, d), F32)], compiler_params=_params())(h, g, w1t, w3t, w2)
    plan.after(f"{tag}_fwd")
    return out, (h, n1, a, b, hm)


def _ffn_bwd(dout, saved, g, w1, w3, w2, tag, plan):
    h, n1, a, b, hm = saved
    n, d = h.shape
    ns, fs, _ = w2.shape
    rows, row, gain, w_row, hid = _ffn_specs(n, d, fs, FFN_BWD_ROWS)
    sub = _pick(rows, FFN_SUB_ROWS, 16)

    def body(do_ref, h_ref, g_ref, a_ref, b_ref, w1_ref, w3_ref, w2_ref, dh_ref, da_ref, db_ref, dg_ref, acc_ref):
        i, s = pl.program_id(0), pl.program_id(1)

        @pl.when(s == 0)
        def _():
            acc_ref[...] = jnp.zeros_like(acc_ref)

        @pl.when((s == 0) & (i == 0))
        def _():
            dg_ref[...] = jnp.zeros_like(dg_ref)

        def up(j):
            return _dot(0.5 * do_ref[j * sub:(j + 1) * sub, :], w2_ref[...], NT)

        cur = up(0)
        for j in range(rows // sub):
            nxt = up(j + 1) if (j + 1) * sub < rows else None
            r = slice(j * sub, (j + 1) * sub)
            av, bv = a_ref[r, :].astype(F32), b_ref[r, :].astype(F32)
            sg = _sigmoid(av)
            da = (cur * bv * (sg * (1.0 + av * (1.0 - sg)))).astype(BF16)
            db = (cur * av * sg).astype(BF16)
            da_ref[r, :] = da
            db_ref[r, :] = db
            acc_ref[r, :] += _dot(da, w1_ref[...]) + _dot(db, w3_ref[...])
            cur = nxt

        @pl.when(s == ns - 1)
        def _():
            xv, dn = h_ref[...], acc_ref[...]
            rstd = lax.rsqrt(jnp.mean(xv * xv, axis=-1, keepdims=True) + EPS)
            xh = xv * rstd
            dg_ref[...] += jnp.sum(dn * xh, axis=0, keepdims=True)
            dxh = dn * g_ref[...]
            dh_ref[...] = do_ref[...] + rstd * (dxh - xh * jnp.mean(dxh * xh, axis=-1, keepdims=True))

    hid_shape = jax.ShapeDtypeStruct((ns, n, fs), BF16)
    plan.before(f"{tag}_bwd")
    dh, da, db, dg = _pcall(
        body, name=f"{tag}_bwd", grid=(n // rows, ns), in_specs=[row, row, gain, hid, hid, w_row, w_row, w_row],
        out_specs=[row, hid, hid, gain],
        out_shape=[jax.ShapeDtypeStruct((n, d), F32), hid_shape, hid_shape, jax.ShapeDtypeStruct((1, d), F32)],
        scratch_shapes=[pltpu.VMEM((rows, d), F32)], compiler_params=_params())(dout, h, g, a, b, w1, w3, w2)
    plan.after(f"{tag}_bwd")
    plan.grads[f"{tag}_norm"] = dg
    plan.before(f"{tag}_gw2")
    gw2 = _mm(hm, dout, ta=True, shard='m', alpha=0.5, name=f"{tag}_gw2")
    plan.after(f"{tag}_gw2")
    plan.grads[f"{tag}_w2"] = gw2
    plan.before(f"{tag}_gw1")
    gw1 = _mm(da, n1, ta=True, shard='m', name=f"{tag}_gw1")
    plan.after(f"{tag}_gw1")
    plan.before(f"{tag}_gw3")
    gw3 = _mm(db, n1, ta=True, shard='m', name=f"{tag}_gw3")
    plan.after(f"{tag}_gw3")
    return dh, dg, gw1, gw3, gw2


def _local_step(x, tgt, plan):
    n = x.shape[0]
    grads = plan.grads

    def f(name):
        w = plan.get(name)
        return w.reshape(1, D_MODEL) if name.endswith('_norm') and name != 'gla_out_norm' else w

    def carried(tag, fn, *args, **kw):
        plan.before(tag)
        out = fn(*args, **kw)
        plan.after(tag)
        return out

    h1, ffn1 = _ffn_fwd(x, f('ffn1_norm'), f('ffn1_w1'), f('ffn1_w3'), f('ffn1_w2'), "ffn1", plan)
    u = carried("mix_rms", _rms_fwd, h1, f('mix_norm'), "mix_rms")
    w_in = f('w_in')
    w_a = jnp.concatenate([w_in[:, :512], _pad_heads(w_in[:, 512:768]), _pad_heads(w_in[:, 768:1024]), w_in[:, 1024:2048]],
                          axis=1)
    w_al = jnp.pad(w_in[:, 2048:2048 + GLA_RANK], ((0, 0), (0, LANE - GLA_RANK)))
    w_g = w_in[:, 2048 + GLA_RANK:]
    za = carried("in_a", _mm, u, w_a, name="in_a")
    zg = carried("in_g", _mm, u, w_g, name="in_g")
    al = _mm(u, w_al, name="in_al")
    ar, ai, bbar_re, bbar_im = _s5_discretize(f('s5_lambda_re'), f('s5_lambda_im'), f('s5_log_dt'), f('s5_b_re'), f('s5_b_im'))
    t_b = _bd_tiles(bbar_re.transpose(0, 2, 1), bbar_im.transpose(0, 2, 1)).astype(BF16)
    t_c = _bd_tiles(f('s5_c_re'), -f('s5_c_im')).astype(BF16)
    ar8 = jnp.broadcast_to(ar.reshape(1, S5_GP), (SEG, S5_GP))
    ai8 = jnp.broadcast_to(ai.reshape(1, S5_GP), (SEG, S5_GP))
    pw_r, pw_i = _segment_powers(ar, ai, n // SEG)
    dskip = f('s5_d').reshape(1, S5_W)
    u_s5 = _permute_rows(za[:, :S5_W])
    xs = _s5_scan(u_s5, t_b, ar8, ai8, pw_r, pw_i, "s5_scan")
    ys_p = _bd_reduce(xs, t_c, _scale_rows(u_s5, dskip, "s5_skip"), "s5_y")
    ys = _unpermute_rows(ys_p)
    zgelu = _gelu_fwd(ys, "s5_gelu")
    t_glu = _mm(zgelu, f('s5_glu_w'), bias=f('s5_glu_b').reshape(1, S5_W), name="s5_glu_t")
    y_s5 = _glu_fwd(zgelu, t_glu, "s5_glu")
    wup = jnp.pad(f('gla_a_up_w'), ((0, LANE - GLA_RANK), (0, 0)))
    wup_h = wup.reshape(LANE, GLA_HEADS, GLA_DK).transpose(1, 0, 2)
    bup_h = f('gla_a_up_b').reshape(GLA_HEADS, 1, GLA_DK)
    gn_h = f('gla_out_norm').reshape(GLA_HEADS, 1, GLA_DV)
    y_gla, s_prev = carried("gla_fwd", _gla_fwd, za, al, wup_h, bup_h, gn_h, "gla_fwd")
    ps = _mm(y_s5, f('proj_s5'), name="proj_s5")
    pg = carried("proj_gla", _mm, y_gla, f('proj_gla'), name="proj_gla")
    merged = _merge_fwd(zg, ps, pg, "merge")
    h2 = _mm(merged, f('w_out'), res=h1, name="w_out")
    h3, ffn2 = _ffn_fwd(h2, f('ffn2_norm'), f('ffn2_w1'), f('ffn2_w3'), f('ffn2_w2'), "ffn2", plan)
    loss, dh3, g_final = _final_loss(h3, f('final_norm').reshape(1, D_MODEL), tgt, "loss")
    grads['final_norm'] = g_final.reshape(D_MODEL)
    dh2, grads['ffn2_norm'], grads['ffn2_w1'], grads['ffn2_w3'], grads['ffn2_w2'] = _ffn_bwd(
        dh3, ffn2, f('ffn2_norm'), f('ffn2_w1'), f('ffn2_w3'), f('ffn2_w2'), "ffn2", plan)
    dm = _mm(dh2, f('w_out'), tb=True, name="d_merged")
    grads['w_out'] = _mm(merged, dh2, ta=True, name="g_w_out")
    dps, dpg, dzg = carried("d_merge", _merge_bwd, dm, zg, ps, pg, "d_merge")
    grads['proj_s5'] = _mm(y_s5, dps, ta=True, name="g_proj_s5")
    grads['proj_gla'] = _mm(y_gla, dpg, ta=True, name="g_proj_gla")
    dy_s5 = _mm(dps, f('proj_s5'), tb=True, name="d_y_s5")
    dy_gla = _mm(dpg, f('proj_gla'), tb=True, name="d_y_gla")
    dzgelu, dt_glu, g_glu_b = _glu_bwd1(dy_s5, zgelu, t_glu, "d_glu")
    grads['s5_glu_b'] = g_glu_b.reshape(S5_W)
    grads['s5_glu_w'] = _mm(zgelu, dt_glu, ta=True, name="g_glu_w")
    dzgelu = _mm(dt_glu, f('s5_glu_w'), tb=True, res=dzgelu, name="d_gelu")
    dys, du_skip, g_d = _glu_bwd2(_permute_rows(dzgelu), ys_p, u_s5, dskip, "d_s5_y")
    grads['s5_d'] = g_d.reshape(S5_G, S5_H)
    lam, da8 = _s5_scan_bwd(dys, t_c, xs, ar8, ai8, pw_r, pw_i, "s5_scan_bwd")
    g_c = _bd_blocks(_bd_outer(dys, xs, "g_s5_c"))
    grads['s5_c_re'], grads['s5_c_im'] = g_c[0], -g_c[1]
    g_b = _bd_blocks(_bd_outer(u_s5, lam, "g_s5_b")).transpose(0, 1, 3, 2)
    g_bbar_re, g_bbar_im = g_b[0], g_b[1]
    da = jnp.sum(da8, axis=0)
    g_ar, g_ai = da[:S5_GP].reshape(S5_G, S5_P), da[S5_GP:].reshape(S5_G, S5_P)
    _, disc_vjp = jax.vjp(_s5_discretize, f('s5_lambda_re'), f('s5_lambda_im'), f('s5_log_dt'), f('s5_b_re'), f('s5_b_im'))
    (grads['s5_lambda_re'], grads['s5_lambda_im'], grads['s5_log_dt'], grads['s5_b_re'],
     grads['s5_b_im']) = disc_vjp((g_ar, g_ai, g_bbar_re, g_bbar_im))
    du_s5 = _unpermute_rows(_bd_reduce(lam, t_b, du_skip, "d_s5_u"))
    dza, dz, dgn, dbup = carried("gla_bwd", _gla_bwd, za, al, wup_h, bup_h, gn_h, s_prev, dy_gla, du_s5, "gla_bwd")
    grads['gla_out_norm'] = dgn.reshape(GLA_HEADS * GLA_DV)
    grads['gla_a_up_b'] = dbup.reshape(GLA_HEADS * GLA_DK)
    grads['gla_a_up_w'] = _unpad_heads(_mm(al, dz, ta=True, name="g_a_up")[:GLA_RANK])
    dal = _mm(dz, _pad_heads(wup), tb=True, name="d_a_low")
    g_wa = _mm(u, dza, ta=True, name="g_in_a")
    g_wg = _mm(u, dzg, ta=True, name="g_in_g")
    g_wal = _mm(u, dal, ta=True, name="g_in_al")
    grads['w_in'] = jnp.concatenate([g_wa[:, :512], _unpad_heads(g_wa[:, 512:1024]), _unpad_heads(g_wa[:, 1024:1536]),
                                     g_wa[:, 1536:], g_wal[:, :GLA_RANK], g_wg], axis=1)
    du = carried("d_u_a", _mm, dza, w_a, tb=True, name="d_u_a")
    du = _mm(dzg, w_g, tb=True, res=du, name="d_u_g")
    du = _mm(dal, w_al, tb=True, res=du, name="d_u_al")
    dh1, g_mix = carried("d_mix_rms", _rms_bwd, h1, f('mix_norm'), du, dh2, "d_mix_rms")
    grads['mix_norm'] = g_mix
    dx, grads['ffn1_norm'], grads['ffn1_w1'], grads['ffn1_w3'], grads['ffn1_w2'] = _ffn_bwd(
        dh1, ffn1, f('ffn1_norm'), f('ffn1_w1'), f('ffn1_w3'), f('ffn1_w2'), "ffn1", plan)
    return loss[0, 0], dx


MIXER_WEIGHTS = ['w_in', 's5_glu_w', 'proj_s5', 'proj_gla', 'w_out', 'gla_a_up_w']
FFN1_WEIGHTS, FFN2_WEIGHTS = FFN_WEIGHTS[:3], FFN_WEIGHTS[3:]
TRANSPOSED = ['ffn1_w1', 'ffn1_w3', 'ffn2_w1', 'ffn2_w3']


def _local_shard(w, nm):
    return jnp.swapaxes(w, 1, 2)[0] if nm in TRANSPOSED else w[0]
FFN1_EARLY = ['ffn1_w2']
FFN1_LATE = ['ffn1_w1', 'ffn1_w3']
GRAD_GROUPS = {'ffn2': FFN2_WEIGHTS, 'mixer': ['w_out', 'proj_s5', 'proj_gla', 's5_glu_w', 'w_in'], 'ffn1': FFN1_WEIGHTS}


class _Plan:
    def __init__(self, a, c_arr, s_arr):
        self.a, self.c_arr, self.s_arr = a, c_arr, s_arr
        self.grads, self.weights, self.riding = {}, {}, {}
        self.g4s, self.chip_sums, self.halves, self.sib_halves = {}, {}, {}, {}
        for nm in SMALL:
            if nm != 'gla_a_up_w':
                self.weights[nm] = a[nm] if nm == 'final_norm' else a[nm][0]
        ici = _side_gather_ici(self._shards(FFN1_WEIGHTS))
        _run_side(ici, "gather_ffn1_ici")
        self._gathered(FFN1_WEIGHTS, _run_side(_side_gather_d2d(ici.outs), "gather_ffn1_d2d"))

    def _shards(self, names):
        return [_local_shard(self.a[nm], nm).astype(F32 if nm == 'gla_a_up_w' else BF16) for nm in names]

    def _gathered(self, names, arrs):
        for nm, g4 in zip(names, arrs):
            if nm in FFN_WEIGHTS:
                self.weights[nm] = g4
            elif nm in COL_SHARDED:
                self.weights[nm] = jnp.concatenate([g4[s] for s in range(4)], axis=1)
            else:
                self.weights[nm] = g4.reshape(4 * g4.shape[1], g4.shape[2])

    def get(self, name):
        return self.weights[name]

    def _shard_major(self, nm):
        g = self.grads[nm]
        if nm in FFN_WEIGHTS:
            return g
        if nm in COL_SHARDED:
            return jnp.stack(jnp.split(g, 4, axis=1))
        return g.reshape(4, g.shape[0] // 4, g.shape[1])

    def _schedule(self, tag):
        grp = GRAD_GROUPS
        gathers = {"ffn1_fwd": ('ici', MIXER_WEIGHTS), "mix_rms": ('d2d', MIXER_WEIGHTS),
                   "in_a": ('ici', FFN2_WEIGHTS[:1]), "in_g": ('d2d', FFN2_WEIGHTS[:1]),
                   "gla_fwd": ('ici', FFN2_WEIGHTS[1:]), "proj_gla": ('d2d', FFN2_WEIGHTS[1:])}
        if tag in gathers:
            kind, names = gathers[tag]
            key = tuple(names)
            if kind == 'ici':
                return [(_side_gather_ici(self._shards(names)), lambda outs: self.riding.update({key: outs}))]
            return [(_side_gather_d2d(self.riding[key]), lambda outs: self._gathered(names, outs))]
        steps = {"d_merge": (grp['ffn2'], 0), "gla_bwd": (grp['ffn2'], 1), "d_mix_rms": (grp['ffn2'], 2),
                 "d_u_a": (grp['mixer'], 0), "ffn1_bwd": (grp['mixer'], 1), "ffn1_gw2": (grp['mixer'], 2),
                 "ffn1_gw1": (FFN1_EARLY, 0), "ffn1_gw3": (FFN1_EARLY, 1), "adamw_early": (FFN1_LATE, 1)}
        entries = [self._reduce_stage(*steps[tag])] if tag in steps else []
        if tag == "ffn1_gw2":
            entries.append(self._small_stage(0))
        if tag == "ffn1_gw1":
            entries.append(self._small_stage(1))
        return entries

    def _small_stage(self, stage):
        if stage == 0:
            a, grads = self.a, self.grads
            self.small_parts = [grads[nm].reshape(a[nm].shape) for nm in SMALL if nm != 'gla_a_up_w'] + [grads['gla_a_up_w']]
            packed = _pack_small(self.small_parts)

            def done(outs):
                self.small_pair = _small_add(packed, outs[0], "small_sum_pair")
            return _side_small_sibling(packed), done

        def done(outs):
            self.small_total = _small_add(self.small_pair, outs[0], "small_sum_chips")
        return _side_small_chips(self.small_pair), done

    def _reduce_stage(self, names, stage):
        if stage == 0:
            for nm in names:
                self.g4s[nm] = self._shard_major(nm)

            def done(outs):
                for nm, r in zip(names, outs):
                    self.chip_sums[nm] = _chip_sum(self.g4s[nm], r, self.c_arr, f"chip_sum_{nm}")
            return _side_swap_halves([self.g4s[nm] for nm in names]), done
        if stage == 1:
            def done(outs):
                for nm, o in zip(names, outs):
                    self.halves[nm] = _owner_sum(self.chip_sums[nm], o, self.s_arr, f"owner_sum_{nm}")
            return _side_scatter([self.chip_sums[nm] for nm in names]), done

        def done(outs):
            self.sib_halves.update(zip(names, outs))
        return _side_swap_reduced([self.halves[nm] for nm in names]), done

    def before(self, tag):
        entries = self._schedule(tag)
        if entries:
            merged = _merge_sides([side for side, _ in entries])
            self.riding[tag] = (merged, entries)
            _RIDER.append(merged)

    def after(self, tag):
        if tag in self.riding:
            merged, entries = self.riding.pop(tag)
            assert not _RIDER and merged.outs is not None, tag
            pos = 0
            for side, done in entries:
                done(merged.outs[pos:pos + len(side.out_shapes)])
                pos += len(side.out_shapes)

    def finish_alone(self, stage):
        names = FFN1_LATE if stage == 0 else GRAD_GROUPS['ffn1']
        side, done = self._reduce_stage(names, stage)
        done(_run_side(side, f"grad_ffn1_stage{stage}"))


def _train_step(a):
    x = a['x'][0]
    tgt = a['loss_target'][0]
    xi, yi, ci = lax.axis_index("x"), lax.axis_index("y"), lax.axis_index("c")
    c_arr = jnp.reshape(ci, (1,)).astype(jnp.int32)
    s_arr = jnp.reshape(2 * xi + yi, (1,)).astype(jnp.int32)
    plan = _Plan(a, c_arr, s_arr)
    loss, dx = _local_step(x, tgt, plan)
    loss = lax.psum(loss, ("x", "y", "c"))
    red = {}
    small_sum = _unpack_small(plan.small_total, plan.small_parts)
    small_names = [nm for nm in SMALL if nm != 'gla_a_up_w']
    for nm, g in zip(small_names, small_sum[:-1]):
        red[nm] = g
    g_up = small_sum[-1]
    red['gla_a_up_w'] = lax.dynamic_slice(g_up, (0, (2 * xi + yi) * GLA_DK), (GLA_RANK, GLA_DK))
    out_g, out_d, out_m, out_v = {}, {}, {}, {}

    def update(names, tag):
        items = [(_local_shard(a[nm], nm), plan.halves[nm], plan.sib_halves[nm], _local_shard(a['m_' + nm], nm),
                  _local_shard(a['v_' + nm], nm)) for nm in names]
        plan.before(tag)
        res = _adamw_group(items, c_arr, tag)
        plan.after(tag)
        for k, nm in enumerate(names):
            back = (lambda t: jnp.swapaxes(t[None], 1, 2)) if nm in TRANSPOSED else (lambda t: t[None])
            out_g[nm], out_d[nm], out_m[nm], out_v[nm] = (back(t) for t in res[4 * k:4 * k + 4])

    plan.finish_alone(0)
    update([nm for nm in SHARDED if nm not in GRAD_GROUPS['ffn1']], "adamw_early")
    plan.finish_alone(2)
    update(GRAD_GROUPS['ffn1'], "adamw_ffn1")
    rest = [nm for nm in WEIGHTS if nm not in SHARDED]
    pk = lambda pre: _pack_small([a[pre + nm] for nm in rest])
    d, nm_, nv_ = _adamw(pk(''), _pack_small([red[nm] for nm in rest]), pk('m_'), pk('v_'), "adamw_small")
    like = [a[nm] for nm in rest]
    for nm, g, dd, mm_, vv_ in zip(rest, [red[nm].reshape(a[nm].shape) for nm in rest], _unpack_small(d, like),
                                   _unpack_small(nm_, like), _unpack_small(nv_, like)):
        out_g[nm], out_d[nm], out_m[nm], out_v[nm] = g, dd, mm_, vv_
    return (loss, dx[None], *[out_g[nm] for nm in WEIGHTS], *[out_d[nm] for nm in WEIGHTS],
            *[out_m[nm] for nm in WEIGHTS], *[out_v[nm] for nm in WEIGHTS])


def kernel(x, ffn1_norm, ffn1_w1, ffn1_w3, ffn1_w2, mix_norm, w_in, s5_lambda_re, s5_lambda_im, s5_log_dt, s5_b_re, s5_b_im, s5_c_re, s5_c_im, s5_d, s5_glu_w, s5_glu_b, gla_a_up_w, gla_a_up_b, gla_out_norm, proj_s5, proj_gla, w_out, ffn2_norm, ffn2_w1, ffn2_w3, ffn2_w2, final_norm, loss_target, m_ffn1_norm, m_ffn1_w1, m_ffn1_w3, m_ffn1_w2, m_mix_norm, m_w_in, m_s5_lambda_re, m_s5_lambda_im, m_s5_log_dt, m_s5_b_re, m_s5_b_im, m_s5_c_re, m_s5_c_im, m_s5_d, m_s5_glu_w, m_s5_glu_b, m_gla_a_up_w, m_gla_a_up_b, m_gla_out_norm, m_proj_s5, m_proj_gla, m_w_out, m_ffn2_norm, m_ffn2_w1, m_ffn2_w3, m_ffn2_w2, m_final_norm, v_ffn1_norm, v_ffn1_w1, v_ffn1_w3, v_ffn1_w2, v_mix_norm, v_w_in, v_s5_lambda_re, v_s5_lambda_im, v_s5_log_dt, v_s5_b_re, v_s5_b_im, v_s5_c_re, v_s5_c_im, v_s5_d, v_s5_glu_w, v_s5_glu_b, v_gla_a_up_w, v_gla_a_up_b, v_gla_out_norm, v_proj_s5, v_proj_gla, v_w_out, v_ffn2_norm, v_ffn2_w1, v_ffn2_w3, v_ffn2_w2, v_final_norm):
    return _train_step(dict(locals()))
```

```python
import functools

import jax
import jax.numpy as jnp
from jax import lax
from jax.experimental import pallas as pl
from jax.experimental.pallas import tpu as pltpu

F32 = jnp.float32
BF16 = jnp.bfloat16
HI = lax.Precision.HIGHEST
MESH_ID = pl.DeviceIdType.MESH

D_MODEL = 1024
EPS = 1e-6
S5_G, S5_P, S5_H = 32, 64, 16
S5_W = S5_G * S5_H
S5_GP = S5_G * S5_P
SEG = 8
SCAN_ROWS = 256
GLA_HEADS, GLA_DK, GLA_DV = 4, 64, 128
GLA_CHUNK = 64
GLA_TAU = 16.0
GLA_RANK = 16
ADAM_LR, ADAM_B1, ADAM_B2, ADAM_EPS, ADAM_WD, ADAM_STEP = 0.001, 0.9, 0.999, 1e-08, 0.01, 10
V7X_VMEM_LIMIT = 56 * 1024 * 1024
LANE = 128

WEIGHTS = ['ffn1_norm', 'ffn1_w1', 'ffn1_w3', 'ffn1_w2', 'mix_norm', 'w_in', 's5_lambda_re', 's5_lambda_im',
           's5_log_dt', 's5_b_re', 's5_b_im', 's5_c_re', 's5_c_im', 's5_d', 's5_glu_w', 's5_glu_b', 'gla_a_up_w',
           'gla_a_up_b', 'gla_out_norm', 'proj_s5', 'proj_gla', 'w_out', 'ffn2_norm', 'ffn2_w1', 'ffn2_w3',
           'ffn2_w2', 'final_norm']
SHARDED = ['ffn1_w1', 'ffn1_w3', 'ffn1_w2', 'w_in', 's5_glu_w', 'proj_s5', 'proj_gla', 'w_out',
           'ffn2_w1', 'ffn2_w3', 'ffn2_w2']
COL_SHARDED = ['ffn1_w1', 'ffn1_w3', 'w_in', 'proj_s5', 'proj_gla', 'ffn2_w1', 'ffn2_w3', 'gla_a_up_w']
SMALL = [n for n in WEIGHTS if n not in SHARDED]
FFN_WEIGHTS = ['ffn1_w1', 'ffn1_w3', 'ffn1_w2', 'ffn2_w1', 'ffn2_w3', 'ffn2_w2']


def _params(**kw):
    return pltpu.CompilerParams(vmem_limit_bytes=V7X_VMEM_LIMIT, **kw)


class _Side:
    def __init__(self, ins, out_shapes, nsem, copies, aliased=False):
        self.ins, self.out_shapes, self.nsem, self.copies, self.aliased = list(ins), list(out_shapes), nsem, copies, aliased
        self.outs = None


_RIDER = []


def _pcall(body, **kw):
    if _RIDER:
        return _carry(body, _RIDER.pop(), **kw)
    return pl.pallas_call(body, **kw)


def _carry(body, side, *, name, grid, in_specs, out_specs, out_shape, scratch_shapes=(), compiler_params=None):
    del compiler_params
    single = not isinstance(out_shape, (list, tuple))
    out_specs = [out_specs] if single else list(out_specs)
    out_shape = [out_shape] if single else list(out_shape)
    n_in, n_out, n_scr = len(in_specs), len(out_shape), len(scratch_shapes)
    s_in, s_out = len(side.ins), len(side.out_shapes)
    any_spec = pl.BlockSpec(memory_space=pl.ANY)

    def wrapped(*refs):
        cuts = [n_in, s_in, n_out, s_out, n_scr]
        parts, pos = [], 0
        for c in cuts:
            parts.append(refs[pos:pos + c])
            pos += c
        ins, sins, outs, souts, scr = parts
        ssem, rsem = refs[pos], refs[pos + 1]
        first = last = None
        for d, g in enumerate(grid):
            i = pl.program_id(d)
            first = (i == 0) if first is None else first & (i == 0)
            last = (i == g - 1) if last is None else last & (i == g - 1)

        @pl.when(first)
        def _():
            for cp in side.copies(sins, souts, ssem, rsem):
                cp.start()

        body(*ins, *outs, *scr)

        @pl.when(last)
        def _():
            for cp in side.copies(sins, souts, ssem, rsem):
                cp.wait()

    call = pl.pallas_call(
        wrapped, name=name, grid=grid, in_specs=list(in_specs) + [any_spec] * s_in,
        out_specs=out_specs + [any_spec] * s_out, out_shape=out_shape + side.out_shapes,
        scratch_shapes=list(scratch_shapes) + [pltpu.SemaphoreType.DMA((side.nsem,)), pltpu.SemaphoreType.DMA((side.nsem,))],
        input_output_aliases={n_in + j: n_out + j for j in range(s_in)} if side.aliased else {},
        compiler_params=_params(has_side_effects=True))

    def run(*args):
        res = call(*args, *side.ins)
        side.outs = list(res[n_out:])
        return res[0] if single else list(res[:n_out])

    return run


def _run_side(side, name):
    s_in, s_out = len(side.ins), len(side.out_shapes)
    any_spec = pl.BlockSpec(memory_space=pl.ANY)

    def body(*refs):
        sins, souts = refs[:s_in], refs[s_in:s_in + s_out]
        ssem, rsem = refs[s_in + s_out:]
        cps = side.copies(sins, souts, ssem, rsem)
        for cp in cps:
            cp.start()
        for cp in cps:
            cp.wait()

    side.outs = list(pl.pallas_call(
        body, name=name, in_specs=[any_spec] * s_in, out_specs=[any_spec] * s_out, out_shape=side.out_shapes,
        scratch_shapes=[pltpu.SemaphoreType.DMA((side.nsem,)), pltpu.SemaphoreType.DMA((side.nsem,))],
        input_output_aliases={j: j for j in range(s_in)} if side.aliased else {},
        compiler_params=pltpu.CompilerParams(has_side_effects=True))(*side.ins))
    return side.outs


def _pick(n, cap, quantum):
    if n <= cap:
        return n
    best = None
    for t in range(quantum, cap + 1, quantum):
        if n % t == 0:
            best = t
    assert best is not None, (n, cap, quantum)
    return best


def _sigmoid(x):
    return jax.nn.sigmoid(x)


def _mm(a, b, *, name, ta=False, tb=False, out_dtype=F32, alpha=1.0, res=None, bias=None, exact=False, shard=None):
    ns = 4
    (k_a, m) = a.shape[-2:] if ta else a.shape[-2:][::-1]
    (k_b, n) = b.shape[-2:][::-1] if tb else b.shape[-2:]
    assert k_a == k_b, (a.shape, b.shape, ta, tb)
    assert (a.ndim == 3) == (shard in ('k', 'm')) and (b.ndim == 3) == (shard in ('n', 'k'))
    k = k_a
    tm = _pick(m, 1024, 128)
    tn = _pick(n, 1024, 128)
    tk = _pick(k, 1024, 128)
    pm, pn, pk = m // tm, n // tn, k // tk
    gm = pm * (ns if shard == 'm' else 1)
    gn = pn * (ns if shard == 'n' else 1)
    gk = pk * (ns if shard == 'k' else 1)
    dims = (((0,) if ta else (1,), (1,) if tb else (0,)), ((), ()))
    op_dtype = F32 if exact else BF16

    def body(*refs):
        a_ref, b_ref = refs[0], refs[1]
        pos = 2
        res_ref = bias_ref = None
        if res is not None:
            res_ref = refs[pos]
            pos += 1
        if bias is not None:
            bias_ref = refs[pos]
            pos += 1
        o_ref, acc_ref = refs[pos], refs[pos + 1]
        kk = pl.program_id(2)

        @pl.when(kk == 0)
        def _():
            acc_ref[...] = jnp.zeros_like(acc_ref)

        acc_ref[...] += lax.dot_general(a_ref[...].astype(op_dtype), b_ref[...].astype(op_dtype), dims,
                                        precision=HI if exact else None, preferred_element_type=F32)

        @pl.when(kk == gk - 1)
        def _():
            o = acc_ref[...]
            if alpha != 1.0:
                o = o * alpha
            if bias_ref is not None:
                o = o + bias_ref[...]
            if res_ref is not None:
                o = o + res_ref[...]
            o_ref[...] = o.astype(out_dtype)

    def spec(block, sharded_on, order):
        per = {'m': pm, 'n': pn, 'k': pk}

        def index(i, j, kk):
            g = {'m': i, 'n': j, 'k': kk}
            r, c = order(i % pm if shard == 'm' else i, j % pn if shard == 'n' else j, kk % pk if shard == 'k' else kk)
            if sharded_on is None:
                return (r, c)
            return (g[sharded_on] // per[sharded_on], r, c)

        return pl.BlockSpec(block if sharded_on is None else (None,) + block, index)

    a_sh = shard if shard in ('k', 'm') else None
    b_sh = shard if shard in ('n', 'k') else None
    o_sh = shard if shard in ('n', 'm') else None
    a_spec = spec((tk, tm), a_sh, lambda i, j, kk: (kk, i)) if ta else spec((tm, tk), a_sh, lambda i, j, kk: (i, kk))
    b_spec = spec((tn, tk), b_sh, lambda i, j, kk: (j, kk)) if tb else spec((tk, tn), b_sh, lambda i, j, kk: (kk, j))
    ins, in_specs = [a, b], [a_spec, b_spec]
    if res is not None:
        assert o_sh is None
        ins.append(res)
        in_specs.append(pl.BlockSpec((tm, tn), lambda i, j, kk: (i, j)))
    if bias is not None:
        assert o_sh is None
        ins.append(bias)
        in_specs.append(pl.BlockSpec((1, tn), lambda i, j, kk: (0, j)))
    out_shape = (m, n) if o_sh is None else (ns, m, n)
    return _pcall(body, name=name, grid=(gm, gn, gk), in_specs=in_specs,
                  out_specs=spec((tm, tn), o_sh, lambda i, j, kk: (i, j)),
                  out_shape=jax.ShapeDtypeStruct(out_shape, out_dtype),
                  scratch_shapes=[pltpu.VMEM((tm, tn), F32)], compiler_params=_params())(*ins)


ROWS_VMEM_BUDGET = 24 * 1024 * 1024


def _rows(body, ins, outs, *, n, name):
    cols = sum(a.shape[1] for a, kind in ins if kind == 'r') + sum(c for c, _, kind in outs if kind == 'r')
    cap = 256
    while cap < 2048 and 2 * 4 * cols * (2 * cap) <= ROWS_VMEM_BUDGET:
        cap *= 2
    tm = _pick(n, cap, 16)
    in_specs = []
    for arr, kind in ins:
        if kind == 'r':
            in_specs.append(pl.BlockSpec((tm, arr.shape[1]), lambda i: (i, 0)))
        else:
            in_specs.append(pl.BlockSpec(arr.shape, lambda i: (0, 0)))
    out_specs, out_shape = [], []
    for cols, dtype, kind in outs:
        if kind == 'r':
            out_specs.append(pl.BlockSpec((tm, cols), lambda i: (i, 0)))
            out_shape.append(jax.ShapeDtypeStruct((n, cols), dtype))
        else:
            out_specs.append(pl.BlockSpec((1, cols), lambda i: (0, 0)))
            out_shape.append(jax.ShapeDtypeStruct((1, cols), dtype))
    n_in = len(ins)
    acc_ids = [j for j, o in enumerate(outs) if o[2] == 'a']

    def wrapped(*refs):
        if acc_ids:
            @pl.when(pl.program_id(0) == 0)
            def _():
                for j in acc_ids:
                    refs[n_in + j][...] = jnp.zeros_like(refs[n_in + j])
        body(*refs)

    res = _pcall(wrapped, name=name, grid=(n // tm,), in_specs=in_specs, out_specs=out_specs, out_shape=out_shape,
                 compiler_params=_params())(*[a for a, _ in ins])
    return res


def _rms_fwd(x, g, name):
    def body(x_ref, g_ref, o_ref):
        xv = x_ref[...]
        rstd = lax.rsqrt(jnp.mean(xv * xv, axis=-1, keepdims=True) + EPS)
        o_ref[...] = (xv * rstd * g_ref[...]).astype(BF16)
    return _rows(body, [(x, 'r'), (g, 'f')], [(x.shape[1], BF16, 'r')], n=x.shape[0], name=name)[0]


def _rms_bwd(x, g, dn, dres, name):
    def body(x_ref, g_ref, dn_ref, dres_ref, dx_ref, dg_ref):
        xv = x_ref[...]
        rstd = lax.rsqrt(jnp.mean(xv * xv, axis=-1, keepdims=True) + EPS)
        xh = xv * rstd
        dn = dn_ref[...]
        dg_ref[...] += jnp.sum(dn * xh, axis=0, keepdims=True)
        dxh = dn * g_ref[...]
        dx_ref[...] = dres_ref[...] + rstd * (dxh - xh * jnp.mean(dxh * xh, axis=-1, keepdims=True))
    d = x.shape[1]
    return _rows(body, [(x, 'r'), (g, 'f'), (dn, 'r'), (dres, 'r')], [(d, F32, 'r'), (d, F32, 'a')],
                 n=x.shape[0], name=name)


def _gelu_parts(y):
    c0 = 0.7978845608028654
    inner = c0 * (y + 0.044715 * y * y * y)
    th = jnp.tanh(inner)
    return th, c0 * (1.0 + 3.0 * 0.044715 * y * y)


def _gelu_fwd(y, name):
    def body(y_ref, o_ref):
        yv = y_ref[...]
        th, _ = _gelu_parts(yv)
        o_ref[...] = 0.5 * yv * (1.0 + th)
    return _rows(body, [(y, 'r')], [(y.shape[1], F32, 'r')], n=y.shape[0], name=name)[0]


def _glu_fwd(zg, t, name):
    def body(z_ref, t_ref, o_ref):
        o_ref[...] = (z_ref[...] * _sigmoid(t_ref[...])).astype(BF16)
    return _rows(body, [(zg, 'r'), (t, 'r')], [(zg.shape[1], BF16, 'r')], n=zg.shape[0], name=name)[0]


def _glu_bwd1(dy, zg, t, name):
    def body(dy_ref, z_ref, t_ref, dz_ref, dt_ref, db_ref):
        dyv, zv = dy_ref[...], z_ref[...]
        sg = _sigmoid(t_ref[...])
        dz_ref[...] = dyv * sg
        dt = dyv * zv * sg * (1.0 - sg)
        dt_ref[...] = dt.astype(BF16)
        db_ref[...] += jnp.sum(dt, axis=0, keepdims=True)
    w = zg.shape[1]
    return _rows(body, [(dy, 'r'), (zg, 'r'), (t, 'r')], [(w, F32, 'r'), (w, BF16, 'r'), (w, F32, 'a')],
                 n=zg.shape[0], name=name)


def _glu_bwd2(dzg, ys, u, dskip, name):
    def body(dz_ref, y_ref, u_ref, d_ref, dy_ref, du_ref, dd_ref):
        yv = y_ref[...]
        th, dinner = _gelu_parts(yv)
        dy = dz_ref[...] * (0.5 * (1.0 + th) + 0.5 * yv * (1.0 - th * th) * dinner)
        dy_ref[...] = dy
        du_ref[...] = dy * d_ref[...]
        dd_ref[...] += jnp.sum(dy * u_ref[...], axis=0, keepdims=True)
    w = ys.shape[1]
    return _rows(body, [(dzg, 'r'), (ys, 'r'), (u, 'r'), (dskip, 'f')], [(w, F32, 'r'), (w, F32, 'r'), (w, F32, 'a')],
                 n=ys.shape[0], name=name)


def _scale_rows(u, dskip, name):
    def body(u_ref, d_ref, o_ref):
        o_ref[...] = u_ref[...] * d_ref[...]
    return _rows(body, [(u, 'r'), (dskip, 'f')], [(u.shape[1], F32, 'r')], n=u.shape[0], name=name)[0]


def _merge_fwd(zg, ps, pg, name):
    def body(z_ref, ps_ref, pg_ref, o_ref):
        zv = z_ref[...]
        o_ref[...] = (_sigmoid(zv[:, :D_MODEL]) * ps_ref[...] + _sigmoid(zv[:, D_MODEL:]) * pg_ref[...]).astype(BF16)
    return _rows(body, [(zg, 'r'), (ps, 'r'), (pg, 'r')], [(D_MODEL, BF16, 'r')], n=zg.shape[0], name=name)[0]


def _merge_bwd(dm, zg, ps, pg, name):
    def body(dm_ref, z_ref, ps_ref, pg_ref, dps_ref, dpg_ref, dz_ref):
        dmv, zv = dm_ref[...], z_ref[...]
        s1, s2 = _sigmoid(zv[:, :D_MODEL]), _sigmoid(zv[:, D_MODEL:])
        dps_ref[...] = (dmv * s1).astype(BF16)
        dpg_ref[...] = (dmv * s2).astype(BF16)
        dz_ref[:, :D_MODEL] = dmv * ps_ref[...] * s1 * (1.0 - s1)
        dz_ref[:, D_MODEL:] = dmv * pg_ref[...] * s2 * (1.0 - s2)
    return _rows(body, [(dm, 'r'), (zg, 'r'), (ps, 'r'), (pg, 'r')],
                 [(D_MODEL, BF16, 'r'), (D_MODEL, BF16, 'r'), (2 * D_MODEL, F32, 'r')], n=zg.shape[0], name=name)


def _final_loss(h, g, tgt, name):
    def body(h_ref, g_ref, t_ref, loss_ref, dh_ref, dg_ref):
        hv = h_ref[...]
        rstd = lax.rsqrt(jnp.mean(hv * hv, axis=-1, keepdims=True) + EPS)
        xh = hv * rstd
        err = xh * g_ref[...] - t_ref[...]
        part = 0.5 * jnp.sum(jnp.mean(err * err, axis=-1, keepdims=True), axis=0, keepdims=True)
        loss_ref[...] += jnp.broadcast_to(part, loss_ref.shape)
        dout = err * (1.0 / hv.shape[1])
        dg_ref[...] += jnp.sum(dout * xh, axis=0, keepdims=True)
        dxh = dout * g_ref[...]
        dh_ref[...] = rstd * (dxh - xh * jnp.mean(dxh * xh, axis=-1, keepdims=True))
    d = h.shape[1]
    return _rows(body, [(h, 'r'), (g, 'f'), (tgt, 'r')], [(LANE, F32, 'a'), (d, F32, 'r'), (d, F32, 'a')],
                 n=h.shape[0], name=name)


def _adamw_math(wv, gv, mv, vv):
    nm = ADAM_B1 * mv + (1.0 - ADAM_B1) * gv
    nv = ADAM_B2 * vv + (1.0 - ADAM_B2) * (gv * gv)
    m_hat = nm / (1.0 - ADAM_B1 ** ADAM_STEP)
    v_hat = nv / (1.0 - ADAM_B2 ** ADAM_STEP)
    return -ADAM_LR * (m_hat / (jnp.sqrt(v_hat) + ADAM_EPS) + ADAM_WD * wv), nm, nv


def _adamw(w, g, m, v, name):
    def body(w_ref, g_ref, m_ref, v_ref, d_ref, nm_ref, nv_ref):
        d_ref[...], nm_ref[...], nv_ref[...] = _adamw_math(w_ref[...], g_ref[...], m_ref[...], v_ref[...])
    c = w.shape[1]
    return _rows(body, [(w, 'r'), (g, 'r'), (m, 'r'), (v, 'r')], [(c, F32, 'r')] * 3, n=w.shape[0], name=name)


ADAMW_BLOCKS = 8


def _adamw_group(items, c_arr, name):
    per = ADAMW_BLOCKS // 2
    n = len(items)

    def body(c_ref, *refs):
        mine = (pl.program_id(0) // per) == c_ref[0]
        for k in range(n):
            w_ref, go_ref, gs_ref, m_ref, v_ref = refs[5 * k:5 * k + 5]
            g_ref, d_ref, nm_ref, nv_ref = refs[5 * n + 4 * k:5 * n + 4 * k + 4]
            gv = jnp.where(mine, go_ref[...], gs_ref[...])
            g_ref[...] = gv
            d_ref[...], nm_ref[...], nv_ref[...] = _adamw_math(w_ref[...], gv, m_ref[...], v_ref[...])

    in_specs, out_specs, out_shape, args = [pl.BlockSpec(memory_space=pltpu.SMEM)], [], [], [c_arr]
    for item in items:
        r, cols = item[0].shape
        assert r % (8 * ADAMW_BLOCKS) == 0, item[0].shape
        tr = r // ADAMW_BLOCKS
        full = pl.BlockSpec((tr, cols), lambda i: (i, 0))
        half = pl.BlockSpec((tr, cols), lambda i: (i % per, 0))
        in_specs += [full, half, half, full, full]
        out_specs += [full] * 4
        out_shape += [jax.ShapeDtypeStruct((r, cols), F32)] * 4
        args += list(item)
    return _pcall(body, name=name, grid=(ADAMW_BLOCKS,), in_specs=in_specs, out_specs=out_specs, out_shape=out_shape,
                  compiler_params=_params())(*args)


def _shift_rows(v, sh, down):
    rolled = pltpu.roll(v, sh if down else v.shape[0] - sh, axis=0)
    row = lax.broadcasted_iota(jnp.int32, v.shape, 0)
    keep = (row >= sh) if down else (row < v.shape[0] - sh)
    return jnp.where(keep, rolled, 0.0)


def _chain_segments(st_r, st_i, pw_r_ref, pw_i_ref, conj, down):
    vr, vi = st_r[...], st_i[...]
    sh, k = 1, 0
    while sh < SEG:
        pr, pi = pw_r_ref[k:k + 1, :], pw_i_ref[k:k + 1, :]
        if conj:
            pi = -pi
        sr, si = _shift_rows(vr, sh, down), _shift_rows(vi, sh, down)
        vr, vi = vr + pr * sr - pi * si, vi + pr * si + pi * sr
        sh, k = sh * 2, k + 1
    st_r[...] = _shift_rows(vr, 1, down)
    st_i[...] = _shift_rows(vi, 1, down)


def _expand_block(u_ref, t_ref, bu_ref):
    for j in range(BD_TILES):
        k = j % 4
        bu_ref[:, j * BD_ST:(j + 1) * BD_ST] = _dot(u_ref[:, k * BD_CH:(k + 1) * BD_CH], t_ref[j])


def _s5_scan(u, tiles, ar8, ai8, pw_r, pw_i, name):
    n = u.shape[0]
    rb = SCAN_ROWS
    nb, steps, lc = n // rb, rb // SEG, 512

    def body(u_ref, t_ref, ar_ref, ai_ref, pwr_ref, pwi_ref, x_ref, st_r, st_i, bu_ref):
        ph, b = pl.program_id(0), pl.program_id(1)

        @pl.when((ph == 0) & (b == 0))
        def _():
            st_r[...] = jnp.zeros_like(st_r)
            st_i[...] = jnp.zeros_like(st_i)

        _expand_block(u_ref, t_ref, bu_ref)

        def scan(store):
            for c in range(S5_GP // lc):
                re, im = slice(c * lc, (c + 1) * lc), slice(S5_GP + c * lc, S5_GP + (c + 1) * lc)
                a_r, a_i = ar_ref[:, re], ai_ref[:, re]

                def step(s, carry):
                    xr, xi = carry
                    rows = pl.ds(pl.multiple_of(s * SEG, SEG), SEG)
                    nr = a_r * xr - a_i * xi + bu_ref[rows, re]
                    ni = a_r * xi + a_i * xr + bu_ref[rows, im]
                    if store:
                        x_ref[rows, re] = nr
                        x_ref[rows, im] = ni
                    return nr, ni

                xr, xi = lax.fori_loop(0, steps, step, (st_r[:, re], st_i[:, re]), unroll=4)
                st_r[:, re] = xr
                st_i[:, re] = xi

        @pl.when(ph == 0)
        def _():
            scan(False)

        @pl.when((ph == 0) & (b == nb - 1))
        def _():
            _chain_segments(st_r, st_i, pwr_ref, pwi_ref, conj=False, down=True)

        @pl.when(ph == 1)
        def _():
            scan(True)

    full = lambda a: pl.BlockSpec(a.shape, lambda ph, b: (0, 0))
    return _pcall(body, name=name, grid=(2, nb),
                  in_specs=[pl.BlockSpec((rb, S5_W), lambda ph, b: (b, 0)), pl.BlockSpec(tiles.shape, lambda ph, b: (0, 0, 0)),
                            full(ar8), full(ai8), full(pw_r), full(pw_i)],
                  out_specs=pl.BlockSpec((rb, 2 * S5_GP), lambda ph, b: (b * ph, 0)),
                  out_shape=jax.ShapeDtypeStruct((n, 2 * S5_GP), F32),
                  scratch_shapes=[pltpu.VMEM((SEG, S5_GP), F32), pltpu.VMEM((SEG, S5_GP), F32),
                                  pltpu.VMEM((rb, 2 * S5_GP), F32)],
                  compiler_params=_params())(u, tiles, ar8, ai8, pw_r, pw_i)


def _s5_scan_bwd(dy, tiles, xs, ar8, ai8, pw_r, pw_i, name):
    n = dy.shape[0]
    rb = SCAN_ROWS
    nb, steps, lc = n // rb, rb // SEG, 256

    def body(dy_ref, t_ref, x_ref, ar_ref, ai_ref, pwr_ref, pwi_ref, lam_ref, da_ref, st_r, st_i, gx_ref):
        ph, b = pl.program_id(0), pl.program_id(1)

        @pl.when((ph == 0) & (b == 0))
        def _():
            st_r[...] = jnp.zeros_like(st_r)
            st_i[...] = jnp.zeros_like(st_i)
            da_ref[...] = jnp.zeros_like(da_ref)

        _expand_block(dy_ref, t_ref, gx_ref)

        def scan(store):
            for c in range(S5_GP // lc):
                re, im = slice(c * lc, (c + 1) * lc), slice(S5_GP + c * lc, S5_GP + (c + 1) * lc)
                a_r, a_i = ar_ref[:, re], ai_ref[:, re]

                def step(s, carry):
                    rows = pl.ds(pl.multiple_of((steps - 1 - s) * SEG, SEG), SEG)
                    if store:
                        lr, li, dr, di = carry
                        xr, xi = x_ref[rows, re], x_ref[rows, im]
                        dr = dr + lr * xr + li * xi
                        di = di + li * xr - lr * xi
                    else:
                        lr, li = carry
                    nr = a_r * lr + a_i * li + gx_ref[rows, re]
                    ni = a_r * li - a_i * lr + gx_ref[rows, im]
                    if store:
                        lam_ref[rows, re] = nr
                        lam_ref[rows, im] = ni
                        return nr, ni, dr, di
                    return nr, ni

                if store:
                    lr, li, dr, di = lax.fori_loop(0, steps, step, (st_r[:, re], st_i[:, re], da_ref[:, re], da_ref[:, im]),
                                                   unroll=4)
                    da_ref[:, re] = dr
                    da_ref[:, im] = di
                else:
                    lr, li = lax.fori_loop(0, steps, step, (st_r[:, re], st_i[:, re]), unroll=4)
                st_r[:, re] = lr
                st_i[:, re] = li

        @pl.when(ph == 0)
        def _():
            scan(False)

        @pl.when((ph == 0) & (b == nb - 1))
        def _():
            _chain_segments(st_r, st_i, pwr_ref, pwi_ref, conj=True, down=False)

        @pl.when(ph == 1)
        def _():
            scan(True)

    full = lambda a: pl.BlockSpec(a.shape, lambda ph, b: (0, 0))
    rev = lambda ph, b: (nb - 1 - b, 0)
    return _pcall(body, name=name, grid=(2, nb),
                  in_specs=[pl.BlockSpec((rb, S5_W), rev), pl.BlockSpec(tiles.shape, lambda ph, b: (0, 0, 0)),
                            pl.BlockSpec((rb, 2 * S5_GP), lambda ph, b: ((nb - 1 - b) * ph, 0)),
                            full(ar8), full(ai8), full(pw_r), full(pw_i)],
                  out_specs=[pl.BlockSpec((rb, 2 * S5_GP), lambda ph, b: (nb - 1 - b * ph, 0)),
                             pl.BlockSpec((SEG, 2 * S5_GP), lambda ph, b: (0, 0))],
                  out_shape=[jax.ShapeDtypeStruct((n, 2 * S5_GP), F32), jax.ShapeDtypeStruct((SEG, 2 * S5_GP), F32)],
                  scratch_shapes=[pltpu.VMEM((SEG, S5_GP), F32), pltpu.VMEM((SEG, S5_GP), F32),
                                  pltpu.VMEM((rb, 2 * S5_GP), F32)],
                  compiler_params=_params())(dy, tiles, xs, ar8, ai8, pw_r, pw_i)


def _s5_discretize(lam_re, lam_im, log_dt, b_re, b_im):
    dt = jnp.exp(log_dt)[:, None]
    mag = jnp.exp(lam_re * dt)
    ar = mag * jnp.cos(lam_im * dt)
    ai = mag * jnp.sin(lam_im * dt)
    den = lam_re * lam_re + lam_im * lam_im
    nr = ar - 1.0
    fr = (nr * lam_re + ai * lam_im) / den
    fi = (ai * lam_re - nr * lam_im) / den
    bbar_re = fr[:, :, None] * b_re - fi[:, :, None] * b_im
    bbar_im = fr[:, :, None] * b_im + fi[:, :, None] * b_re
    return ar, ai, bbar_re, bbar_im


BD_TILES, BD_CH, BD_ST, BD_GROUPS = 8, 128, 512, 8
BD_ROWS = 4096


def _bd_tiles(re, im):
    eye = jnp.eye(BD_GROUPS, dtype=re.dtype)

    def tiles(t):
        t = t.reshape(S5_G // BD_GROUPS, BD_GROUPS, S5_H, S5_P)
        return (t[:, :, :, None, :] * eye[None, :, None, :, None]).reshape(S5_G // BD_GROUPS, BD_CH, BD_ST)

    return jnp.concatenate([tiles(re), tiles(im)], axis=0)


def _bd_blocks(t):
    t = t.reshape(2, S5_G // BD_GROUPS, BD_GROUPS, S5_H, BD_GROUPS, S5_P)
    return jnp.einsum('rkahap->rkahp', t).reshape(2, S5_G, S5_H, S5_P)


def _bd_reduce(x, t, res, name):
    n = x.shape[0]
    tm = _pick(n, BD_ROWS, 16)

    def body(x_ref, t_ref, r_ref, o_ref):
        part = _dot(x_ref[...], t_ref[...], NT)

        @pl.when(pl.program_id(2) == 0)
        def _():
            o_ref[...] = r_ref[...] + part

        @pl.when(pl.program_id(2) == 1)
        def _():
            o_ref[...] += part

    return _pcall(body, name=name, grid=(n // tm, 4, 2),
                  in_specs=[pl.BlockSpec((tm, BD_ST), lambda i, k, r: (i, k + 4 * r)),
                            pl.BlockSpec((None, BD_CH, BD_ST), lambda i, k, r: (k + 4 * r, 0, 0)),
                            pl.BlockSpec((tm, BD_CH), lambda i, k, r: (i, k))],
                  out_specs=pl.BlockSpec((tm, BD_CH), lambda i, k, r: (i, k)),
                  out_shape=jax.ShapeDtypeStruct((n, S5_W), F32), compiler_params=_params())(x, t, res)


def _bd_outer(a, x, name):
    n = a.shape[0]
    tk = _pick(n, BD_ROWS, 16)
    nk = n // tk

    def body(a_ref, x_ref, o_ref):
        part = _dot(a_ref[...], x_ref[...], TN)

        @pl.when(pl.program_id(1) == 0)
        def _():
            o_ref[...] = part

        @pl.when(pl.program_id(1) > 0)
        def _():
            o_ref[...] += part

    return _pcall(body, name=name, grid=(BD_TILES, nk),
                  in_specs=[pl.BlockSpec((tk, BD_CH), lambda j, kk: (kk, j % 4)), pl.BlockSpec((tk, BD_ST), lambda j, kk: (kk, j))],
                  out_specs=pl.BlockSpec((None, BD_CH, BD_ST), lambda j, kk: (j, 0, 0)),
                  out_shape=jax.ShapeDtypeStruct((BD_TILES, BD_CH, BD_ST), F32), compiler_params=_params())(a, x)


def _permute_rows(t):
    n = t.shape[0]
    return t.reshape(SEG, n // SEG, t.shape[1]).transpose(1, 0, 2).reshape(n, t.shape[1])


def _unpermute_rows(t):
    n = t.shape[0]
    return t.reshape(n // SEG, SEG, t.shape[1]).transpose(1, 0, 2).reshape(n, t.shape[1])


def _segment_powers(ar, ai, seg_steps):
    pr, pi = ar.reshape(1, S5_GP), ai.reshape(1, S5_GP)
    e = 1
    while e < seg_steps:
        pr, pi = pr * pr - pi * pi, 2.0 * pr * pi
        e *= 2
    assert e == seg_steps, "segment length must be a power of two"
    rows_r, rows_i = [], []
    for _ in range(3):
        rows_r.append(pr)
        rows_i.append(pi)
        pr, pi = pr * pr - pi * pi, 2.0 * pr * pi
    pad = jnp.zeros((SEG - 3, S5_GP), F32)
    return jnp.concatenate(rows_r + [pad], axis=0), jnp.concatenate(rows_i + [pad], axis=0)


NT = (((1,), (1,)), ((), ()))
TN = (((0,), (0,)), ((), ()))


def _dot(a, b, dims=None, exact=False):
    dims = (((1,), (0,)), ((), ())) if dims is None else dims
    if exact:
        return lax.dot_general(a, b, dims, precision=HI, preferred_element_type=F32)
    return lax.dot_general(a.astype(BF16), b.astype(BF16), dims, preferred_element_type=F32)


def _dot01(a, b, dims=None, ones_first=True):
    x = b if ones_first else a
    hi = x.astype(BF16)
    lo = (x - hi.astype(F32)).astype(BF16)
    parts = [(_dot(a, p, dims) if ones_first else _dot(p, b, dims)) for p in (lo, hi)]
    return parts[0] + parts[1]


HEADS = range(4)


def _gla_chunk_fwd(qc, kc, vc, al, wup, bup, s_prev, tril):
    ones = jnp.ones((GLA_CHUNK, GLA_DV), F32)
    z = [_dot(al, wup[h]) + bup[h] for h in HEADS]
    la = [(jnp.minimum(z[h], 0.0) - jnp.log(1.0 + jnp.exp(-jnp.abs(z[h])))) * (1.0 / GLA_TAU) for h in HEADS]
    bc = [_dot01(tril, la[h]) for h in HEADS]
    blb = [_dot01(la[h], ones, TN, ones_first=False) for h in HEADS]
    bl = [bc[h][GLA_CHUNK - 1:GLA_CHUNK, :] for h in HEADS]
    ebc = [jnp.exp(bc[h]) for h in HEADS]
    qt = [qc[h] * (GLA_DK ** -0.5) * ebc[h] for h in HEADS]
    kt = [kc[h] * jnp.exp(-bc[h]) for h in HEADS]
    ke = [kc[h] * jnp.exp(bl[h] - bc[h]) for h in HEADS]
    sc = [_dot(qt[h], kt[h], NT) * tril for h in HEADS]
    oi = [_dot(sc[h], vc[h]) for h in HEADS]
    oo = [_dot(qt[h], s_prev[h]) for h in HEADS]
    o = [oi[h] + oo[h] for h in HEADS]
    return z, bc, bl, blb, ebc, qt, kt, ke, sc, o


GLA_ROWS = 512
GLA_CPB = GLA_ROWS // GLA_CHUNK


ZA_COLS = 5 * 512
SLOT = 128


def _pad_heads(w):
    r = w.shape[0]
    return jnp.pad(w.reshape(r, GLA_HEADS, GLA_DK), ((0, 0), (0, 0), (0, SLOT - GLA_DK))).reshape(r, GLA_HEADS * SLOT)


def _unpad_heads(w):
    r = w.shape[0]
    return w.reshape(r, GLA_HEADS, SLOT)[:, :, :GLA_DK].reshape(r, GLA_HEADS * GLA_DK)


def _gla_token_specs(blk):
    col = lambda cb: pl.BlockSpec((GLA_ROWS, 512), lambda j: (blk(j), cb))
    whole = lambda a: pl.BlockSpec(a.shape, lambda j: (0,) * a.ndim)
    return col, whole


def _head_ds(h, width):
    return pl.ds(h * SLOT, width)


def _tri(lower):
    ri = lax.broadcasted_iota(jnp.int32, (GLA_CHUNK, GLA_CHUNK), 0)
    ci = lax.broadcasted_iota(jnp.int32, (GLA_CHUNK, GLA_CHUNK), 1)
    return ((ri >= ci) if lower else (ri <= ci)).astype(F32)


def _gla_fwd(za, al, wup, bup, gn, name):
    n = za.shape[0]
    nc = n // GLA_CHUNK

    def body(q_ref, k_ref, v_ref, r_ref, al_ref, wup_ref, bup_ref, gn_ref, y_ref, sp_ref, s_ref):
        @pl.when(pl.program_id(0) == 0)
        def _():
            s_ref[...] = jnp.zeros_like(s_ref)

        tril = _tri(True)

        def chunk(c, carry):
            rows = pl.ds(pl.multiple_of(c * GLA_CHUNK, GLA_CHUNK), GLA_CHUNK)
            alc = al_ref[rows, :]
            vc = [v_ref[rows, _head_ds(h, GLA_DV)] for h in HEADS]
            s_prev = [s_ref[h] for h in HEADS]
            _, _, _, blb, _, _, _, ke, _, o = _gla_chunk_fwd(
                [q_ref[rows, _head_ds(h, GLA_DK)] for h in HEADS], [k_ref[rows, _head_ds(h, GLA_DK)] for h in HEADS],
                vc, alc, [wup_ref[h] for h in HEADS], [bup_ref[h] for h in HEADS], s_prev, tril)
            ds = [_dot(ke[h], vc[h], TN) for h in HEADS]
            for h in HEADS:
                rc = r_ref[rows, _head_ds(h, GLA_DV)]
                sp_ref[h, c] = s_prev[h]
                rstd = lax.rsqrt(jnp.mean(o[h] * o[h], axis=-1, keepdims=True) + EPS)
                y_ref[rows, _head_ds(h, GLA_DV)] = (o[h] * rstd * gn_ref[h] * (rc * _sigmoid(rc))).astype(BF16)
                s_ref[h] = jnp.exp(blb[h]) * s_prev[h] + ds[h]
            return carry

        lax.fori_loop(0, GLA_CPB, chunk, 0)

    col, whole = _gla_token_specs(lambda j: j)
    return _pcall(body, name=name, grid=(n // GLA_ROWS,),
                  in_specs=[col(1), col(2), col(3), col(4), pl.BlockSpec((GLA_ROWS, LANE), lambda j: (j, 0)),
                            whole(wup), whole(bup), whole(gn)],
                  out_specs=[pl.BlockSpec((GLA_ROWS, GLA_HEADS * GLA_DV), lambda j: (j, 0)),
                             pl.BlockSpec((GLA_HEADS, GLA_CPB, GLA_DK, GLA_DV), lambda j: (0, j, 0, 0))],
                  out_shape=[jax.ShapeDtypeStruct((n, GLA_HEADS * GLA_DV), BF16),
                             jax.ShapeDtypeStruct((GLA_HEADS, nc, GLA_DK, GLA_DV), F32)],
                  scratch_shapes=[pltpu.VMEM((GLA_HEADS, GLA_DK, GLA_DV), F32)],
                  compiler_params=_params())(za, za, za, za, al, wup, bup, gn)


def _gla_bwd(za, al, wup, bup, gn, sp, dy, du_s5, name):
    n = za.shape[0]
    nb = n // GLA_ROWS

    def body(q_ref, k_ref, v_ref, r_ref, al_ref, wup_ref, bup_ref, gn_ref, dy_ref, dus_ref, sp_ref,
             dza_ref, dz_ref, dgn_ref, dbup_ref, ds_ref):
        @pl.when(pl.program_id(0) == 0)
        def _():
            ds_ref[...] = jnp.zeros_like(ds_ref)
            dgn_ref[...] = jnp.zeros_like(dgn_ref)
            dbup_ref[...] = jnp.zeros_like(dbup_ref)

        tril, triu = _tri(True), _tri(False)
        dza_ref[:, 0:512] = dus_ref[...]
        dza_ref[:, 512:1536] = jnp.zeros((GLA_ROWS, 1024), F32)
        dz_ref[...] = jnp.zeros_like(dz_ref)

        def chunk(i, carry):
            c = GLA_CPB - 1 - i
            rows = pl.ds(pl.multiple_of(c * GLA_CHUNK, GLA_CHUNK), GLA_CHUNK)
            alc = al_ref[rows, :]
            qc = [q_ref[rows, _head_ds(h, GLA_DK)] for h in HEADS]
            kc = [k_ref[rows, _head_ds(h, GLA_DK)] for h in HEADS]
            vc = [v_ref[rows, _head_ds(h, GLA_DV)] for h in HEADS]
            s_prev = [sp_ref[h, c] for h in HEADS]
            ds = [ds_ref[h] for h in HEADS]
            z, bc, bl, blb, ebc, qt, kt, ke, sc, o = _gla_chunk_fwd(
                qc, kc, vc, alc, [wup_ref[h] for h in HEADS], [bup_ref[h] for h in HEADS], s_prev, tril)
            do = []
            for h in HEADS:
                rc = r_ref[rows, _head_ds(h, GLA_DV)]
                rs = lax.rsqrt(jnp.mean(o[h] * o[h], axis=-1, keepdims=True) + EPS)
                on = o[h] * rs
                sr = _sigmoid(rc)
                sil = rc * sr
                dyv, gnv = dy_ref[rows, _head_ds(h, GLA_DV)], gn_ref[h]
                dgn_ref[h] += jnp.sum(dyv * on * sil, axis=0, keepdims=True)
                dza_ref[rows, pl.ds(2048 + h * SLOT, GLA_DV)] = dyv * on * gnv * (sr * (1.0 + rc * (1.0 - sr)))
                don = dyv * gnv * sil
                do.append(rs * (don - on * jnp.mean(don * on, axis=-1, keepdims=True)))
            dp = [_dot(do[h], vc[h], NT) * tril for h in HEADS]
            dv1 = [_dot(sc[h], do[h], TN) for h in HEADS]
            dv2 = [_dot(ke[h], ds[h]) for h in HEADS]
            dq2 = [_dot(do[h], s_prev[h], NT) for h in HEADS]
            dke = [_dot(vc[h], ds[h], NT) for h in HEADS]
            ddec = [_dot01(jnp.ones((8, GLA_DV), F32), ds[h] * s_prev[h], NT)[0:1, :] for h in HEADS]
            dsn = [_dot(qt[h], do[h], TN) for h in HEADS]
            dq1 = [_dot(dp[h], kt[h]) for h in HEADS]
            dkt = [_dot(dp[h], qt[h], TN) for h in HEADS]
            dbc, dbl = [], []
            for h in HEADS:
                dqt = dq1[h] + dq2[h]
                dza_ref[rows, pl.ds(1536 + h * SLOT, GLA_DV)] = dv1[h] + dv2[h]
                ds_ref[h] = jnp.exp(blb[h]) * ds[h] + dsn[h]
                dza_ref[rows, pl.ds(512 + h * SLOT, GLA_DK)] = dqt * (GLA_DK ** -0.5) * ebc[h]
                dza_ref[rows, pl.ds(1024 + h * SLOT, GLA_DK)] = dkt[h] * jnp.exp(-bc[h]) + dke[h] * jnp.exp(bl[h] - bc[h])
                dbc.append(dqt * qt[h] - dkt[h] * kt[h] - dke[h] * ke[h])
                dbl.append(jnp.sum(dke[h] * ke[h], axis=0, keepdims=True) + ddec[h] * jnp.exp(bl[h]))
            dla = [_dot01(triu, dbc[h]) + dbl[h] for h in HEADS]
            for h in HEADS:
                dz = dla[h] * (1.0 - _sigmoid(z[h])) * (1.0 / GLA_TAU)
                dz_ref[rows, _head_ds(h, GLA_DK)] = dz
                dbup_ref[h] += jnp.sum(dz, axis=0, keepdims=True)
            return carry

        lax.fori_loop(0, GLA_CPB, chunk, 0)

    rev = lambda j: nb - 1 - j
    col, whole = _gla_token_specs(rev)
    tok = lambda w: pl.BlockSpec((GLA_ROWS, w), lambda j: (rev(j), 0))
    h1 = lambda w: pl.BlockSpec((GLA_HEADS, 1, w), lambda j: (0, 0, 0))
    s1 = lambda w: jax.ShapeDtypeStruct((GLA_HEADS, 1, w), F32)
    return _pcall(body, name=name, grid=(nb,),
                  in_specs=[col(1), col(2), col(3), col(4), tok(LANE), whole(wup), whole(bup), whole(gn), tok(512), tok(512),
                            pl.BlockSpec((GLA_HEADS, GLA_CPB, GLA_DK, GLA_DV), lambda j: (0, rev(j), 0, 0))],
                  out_specs=[tok(ZA_COLS), tok(GLA_HEADS * SLOT), h1(GLA_DV), h1(GLA_DK)],
                  out_shape=[jax.ShapeDtypeStruct((n, ZA_COLS), F32), jax.ShapeDtypeStruct((n, GLA_HEADS * SLOT), F32),
                             s1(GLA_DV), s1(GLA_DK)],
                  scratch_shapes=[pltpu.VMEM((GLA_HEADS, GLA_DK, GLA_DV), F32)],
                  compiler_params=_params())(za, za, za, za, al, wup, bup, gn, dy, du_s5, sp)


ANY = pl.BlockSpec(memory_space=pl.ANY)


def _place():
    x, y, c = lax.axis_index("x"), lax.axis_index("y"), lax.axis_index("c")
    chips = [(1 - x, y), (x, 1 - y), (1 - x, 1 - y)]
    return x, y, c, chips


def _remote(src, dst, ssem, rsem, dev):
    return pltpu.make_async_remote_copy(src_ref=src, dst_ref=dst, send_sem=ssem, recv_sem=rsem, device_id=dev,
                                        device_id_type=MESH_ID)


def _half(c, rows):
    h = rows // 2
    return pl.ds(pl.multiple_of(c * h, 8), h)


def _side_gather_ici(shards):
    def copies(ins, outs, ssem, rsem):
        x, y, c, chips = _place()
        mine = 2 * x + y
        cps = []
        for w in range(len(ins)):
            half = _half(c, ins[w].shape[0])
            cps.append(_remote(ins[w], outs[w].at[mine], ssem.at[4 * w], rsem.at[4 * w], (x, y, 1 - c)))
            for k, (px, py) in enumerate(chips):
                cps.append(_remote(ins[w].at[half], outs[w].at[mine, half], ssem.at[4 * w + 1 + k], rsem.at[4 * w + 1 + k],
                                   (px, py, c)))
        return cps

    return _Side(shards, [jax.ShapeDtypeStruct((4,) + s.shape, s.dtype) for s in shards], 4 * len(shards), copies)


def _side_gather_d2d(gathered):
    def copies(ins, outs, ssem, rsem):
        x, y, c, chips = _place()
        cps = []
        for w in range(len(outs)):
            half = _half(c, outs[w].shape[1])
            for k, (px, py) in enumerate(chips):
                theirs = outs[w].at[2 * px + py, half]
                cps.append(_remote(theirs, theirs, ssem.at[3 * w + k], rsem.at[3 * w + k], (x, y, 1 - c)))
        return cps

    return _Side(gathered, [jax.ShapeDtypeStruct(g.shape, g.dtype) for g in gathered], 3 * len(gathered), copies,
                 aliased=True)


def _side_swap_halves(grads):
    def copies(ins, outs, ssem, rsem):
        x, y, c, _ = _place()
        return [_remote(ins[w].at[:, _half(1 - c, ins[w].shape[1]), :], outs[w], ssem.at[w], rsem.at[w], (x, y, 1 - c))
                for w in range(len(ins))]

    return _Side(grads, [jax.ShapeDtypeStruct((4, g.shape[1] // 2, g.shape[2]), g.dtype) for g in grads], len(grads), copies)


def _side_scatter(sums):
    def copies(ins, outs, ssem, rsem):
        x, y, c, chips = _place()
        return [_remote(ins[w].at[2 * px + py], outs[w].at[k], ssem.at[3 * w + k], rsem.at[3 * w + k], (px, py, c))
                for w in range(len(ins)) for k, (px, py) in enumerate(chips)]

    return _Side(sums, [jax.ShapeDtypeStruct((3,) + s.shape[1:], s.dtype) for s in sums], 3 * len(sums), copies)


def _side_swap_reduced(halves):
    def copies(ins, outs, ssem, rsem):
        x, y, c, _ = _place()
        return [_remote(ins[w], outs[w], ssem.at[w], rsem.at[w], (x, y, 1 - c)) for w in range(len(ins))]

    return _Side(halves, [jax.ShapeDtypeStruct(h.shape, h.dtype) for h in halves], len(halves), copies)


def _chip_sum(g, recv, c_arr, name):
    _, r, cols = g.shape
    h = r // 2
    tr = _pick(h, 512, 16)
    g4 = g.reshape(4, 2, h, cols)

    def body(c_ref, g_ref, r_ref, o_ref):
        o_ref[...] = (g_ref[...] + r_ref[...]).astype(BF16)

    grid_spec = pltpu.PrefetchScalarGridSpec(
        num_scalar_prefetch=1, grid=(4, h // tr),
        in_specs=[pl.BlockSpec((None, None, tr, cols), lambda s, i, c_ref: (s, c_ref[0], i, 0)),
                  pl.BlockSpec((None, tr, cols), lambda s, i, c_ref: (s, i, 0))],
        out_specs=pl.BlockSpec((None, tr, cols), lambda s, i, c_ref: (s, i, 0)))
    return _pcall(body, name=name, grid_spec=grid_spec, out_shape=jax.ShapeDtypeStruct((4, h, cols), BF16),
                  compiler_params=_params())(c_arr, g4, recv)


def _owner_sum(sums, others, s_arr, name):
    _, h, cols = sums.shape
    tr = _pick(h, 512, 16)

    def body(s_ref, a_ref, o_ref, out_ref):
        f = lambda v: v.astype(F32)
        out_ref[...] = (f(a_ref[...]) + f(o_ref[0])) + (f(o_ref[1]) + f(o_ref[2]))

    grid_spec = pltpu.PrefetchScalarGridSpec(
        num_scalar_prefetch=1, grid=(h // tr,),
        in_specs=[pl.BlockSpec((None, tr, cols), lambda i, s_ref: (s_ref[0], i, 0)),
                  pl.BlockSpec((3, tr, cols), lambda i, s_ref: (0, i, 0))],
        out_specs=pl.BlockSpec((tr, cols), lambda i, s_ref: (i, 0)))
    return _pcall(body, name=name, grid_spec=grid_spec, out_shape=jax.ShapeDtypeStruct((h, cols), F32),
                  compiler_params=_params())(s_arr, sums, others)


def _side_small_sibling(v):
    def copies(ins, outs, ssem, rsem):
        x, y, c, _ = _place()
        return [_remote(ins[0], outs[0], ssem.at[0], rsem.at[0], (x, y, 1 - c))]

    return _Side([v], [jax.ShapeDtypeStruct(v.shape, F32)], 1, copies)


def _side_small_chips(v):
    def copies(ins, outs, ssem, rsem):
        x, y, c, chips = _place()
        return [_remote(ins[0], outs[0].at[k], ssem.at[k], rsem.at[k], (px, py, c)) for k, (px, py) in enumerate(chips)]

    return _Side([v], [jax.ShapeDtypeStruct((3,) + v.shape, F32)], 3, copies)


def _small_add(v, r, name):
    def body(v_ref, r_ref, o_ref):
        if r.ndim == 2:
            o_ref[...] = v_ref[...] + r_ref[...]
        else:
            o_ref[...] = (v_ref[...] + r_ref[0]) + (r_ref[1] + r_ref[2])

    vm = pl.BlockSpec(memory_space=pltpu.VMEM)
    return _pcall(body, name=name, in_specs=[vm, vm], out_specs=vm, out_shape=jax.ShapeDtypeStruct(v.shape, F32),
                  compiler_params=_params())(v, r)


def _merge_sides(sides):
    if len(sides) == 1:
        return sides[0]

    def copies(in_refs, out_refs, ssem, rsem):
        cps, i, o, q = [], 0, 0, 0
        for s in sides:
            ni, no = len(s.ins), len(s.out_shapes)
            cps += s.copies(in_refs[i:i + ni], out_refs[o:o + no], ssem.at[pl.ds(q, s.nsem)], rsem.at[pl.ds(q, s.nsem)])
            i, o, q = i + ni, o + no, q + s.nsem
        return cps

    assert not any(s.aliased for s in sides)
    return _Side(sum((s.ins for s in sides), []), sum((s.out_shapes for s in sides), []), sum(s.nsem for s in sides), copies)


def _tile_rows(size):
    return -(-size // (8 * LANE)) * 8


def _pack_small(parts):
    pieces = []
    for p in parts:
        flat = p.reshape(-1).astype(F32)
        pieces.append(jnp.pad(flat, (0, _tile_rows(p.size) * LANE - p.size)).reshape(-1, LANE))
    rows = sum(x.shape[0] for x in pieces)
    pieces.append(jnp.zeros(((-rows) % 64, LANE), F32))
    return jnp.concatenate(pieces, axis=0)


def _unpack_small(packed, like):
    out, pos = [], 0
    for p in like:
        rows = _tile_rows(p.size)
        out.append(packed[pos:pos + rows].reshape(-1)[:p.size].reshape(p.shape))
        pos += rows
    return out


FFN_FWD_ROWS, FFN_BWD_ROWS = 1024, 512
FFN_SUB_ROWS = 256


def _ffn_specs(n, d, fs, cap):
    rows = _pick(n, cap, 16)
    row = pl.BlockSpec((rows, d), lambda i, s: (i, 0))
    gain = pl.BlockSpec((1, d), lambda i, s: (0, 0))
    w_row = pl.BlockSpec((None, fs, d), lambda i, s: (s, 0, 0))
    hid = pl.BlockSpec((None, rows, fs), lambda i, s: (s, i, 0))
    return rows, row, gain, w_row, hid


def _ffn_fwd(h, g, w1t, w3t, w2, tag, plan):
    n, d = h.shape
    ns, fs, _ = w2.shape
    rows, row, gain, w_row, hid = _ffn_specs(n, d, fs, FFN_FWD_ROWS)
    sub = rows

    def body(h_ref, g_ref, w1_ref, w3_ref, w2_ref, out_ref, n1_ref, a_ref, b_ref, hm_ref, acc_ref):
        s = pl.program_id(1)

        @pl.when(s == 0)
        def _():
            xv = h_ref[...]
            rstd = lax.rsqrt(jnp.mean(xv * xv, axis=-1, keepdims=True) + EPS)
            n1_ref[...] = (xv * rstd * g_ref[...]).astype(BF16)
            acc_ref[...] = jnp.zeros_like(acc_ref)

        def up(j):
            n1 = n1_ref[j * sub:(j + 1) * sub, :]
            return _dot(n1, w1_ref[...], NT), _dot(n1, w3_ref[...], NT)

        cur = up(0)
        for j in range(rows // sub):
            nxt = up(j + 1) if (j + 1) * sub < rows else None
            a, b = cur
            r = slice(j * sub, (j + 1) * sub)
            hm = (a * _sigmoid(a) * b).astype(BF16)
            a_ref[r, :] = a.astype(BF16)
            b_ref[r, :] = b.astype(BF16)
            hm_ref[r, :] = hm
            acc_ref[r, :] += _dot(hm, w2_ref[...])
            cur = nxt

        @pl.when(s == ns - 1)
        def _():
            out_ref[...] = h_ref[...] + 0.5 * acc_ref[...]

    hid_shape = jax.ShapeDtypeStruct((ns, n, fs), BF16)
    plan.before(f"{tag}_fwd")
    out, n1, a, b, hm = _pcall(
        body, name=f"{tag}_fwd", grid=(n // rows, ns), in_specs=[row, gain, w_row, w_row, w_row],
        out_specs=[row, row, hid, hid, hid],
        out_shape=[jax.ShapeDtypeStruct((n, d), F32), jax.ShapeDtypeStruct((n, d), BF16), hid_shape, hid_shape, hid_shape],
        scratch_shapes=[pltpu.VMEM((rows, d), F32)], compiler_params=_params())(h, g, w1t, w3t, w2)
    plan.after(f"{tag}_fwd")
    return out, (h, n1, a, b, hm)


def _ffn_bwd(dout, saved, g, w1, w3, w2, tag, plan):
    h, n1, a, b, hm = saved
    n, d = h.shape
    ns, fs, _ = w2.shape
    rows, row, gain, w_row, hid = _ffn_specs(n, d, fs, FFN_BWD_ROWS)
    sub = _pick(rows, FFN_SUB_ROWS, 16)

    def body(do_ref, h_ref, g_ref, a_ref, b_ref, w1_ref, w3_ref, w2_ref, dh_ref, da_ref, db_ref, dg_ref, acc_ref):
        i, s = pl.program_id(0), pl.program_id(1)

        @pl.when(s == 0)
        def _():
            acc_ref[...] = jnp.zeros_like(acc_ref)

        @pl.when((s == 0) & (i == 0))
        def _():
            dg_ref[...] = jnp.zeros_like(dg_ref)

        def up(j):
            return _dot(0.5 * do_ref[j * sub:(j + 1) * sub, :], w2_ref[...], NT)

        cur = up(0)
        for j in range(rows // sub):
            nxt = up(j + 1) if (j + 1) * sub < rows else None
            r = slice(j * sub, (j + 1) * sub)
            av, bv = a_ref[r, :].astype(F32), b_ref[r, :].astype(F32)
            sg = _sigmoid(av)
            da = (cur * bv * (sg * (1.0 + av * (1.0 - sg)))).astype(BF16)
            db = (cur * av * sg).astype(BF16)
            da_ref[r, :] = da
            db_ref[r, :] = db
            acc_ref[r, :] += _dot(da, w1_ref[...]) + _dot(db, w3_ref[...])
            cur = nxt

        @pl.when(s == ns - 1)
        def _():
            xv, dn = h_ref[...], acc_ref[...]
            rstd = lax.rsqrt(jnp.mean(xv * xv, axis=-1, keepdims=True) + EPS)
            xh = xv * rstd
            dg_ref[...] += jnp.sum(dn * xh, axis=0, keepdims=True)
            dxh = dn * g_ref[...]
            dh_ref[...] = do_ref[...] + rstd * (dxh - xh * jnp.mean(dxh * xh, axis=-1, keepdims=True))

    hid_shape = jax.ShapeDtypeStruct((ns, n, fs), BF16)
    plan.before(f"{tag}_bwd")
    dh, da, db, dg = _pcall(
        body, name=f"{tag}_bwd", grid=(n // rows, ns), in_specs=[row, row, gain, hid, hid, w_row, w_row, w_row],
        out_specs=[row, hid, hid, gain],
        out_shape=[jax.ShapeDtypeStruct((n, d), F32), hid_shape, hid_shape, jax.ShapeDtypeStruct((1, d), F32)],
        scratch_shapes=[pltpu.VMEM((rows, d), F32)], compiler_params=_params())(dout, h, g, a, b, w1, w3, w2)
    plan.after(f"{tag}_bwd")
    plan.grads[f"{tag}_norm"] = dg
    plan.before(f"{tag}_gw2")
    gw2 = _mm(hm, dout, ta=True, shard='m', alpha=0.5, name=f"{tag}_gw2")
    plan.after(f"{tag}_gw2")
    plan.grads[f"{tag}_w2"] = gw2
    plan.before(f"{tag}_gw1")
    gw1 = _mm(da, n1, ta=True, shard='m', name=f"{tag}_gw1")
    plan.after(f"{tag}_gw1")
    plan.grads[f"{tag}_w1"] = gw1
    plan.before(f"{tag}_gw3")
    gw3 = _mm(db, n1, ta=True, shard='m', name=f"{tag}_gw3")
    plan.after(f"{tag}_gw3")
    return dh, dg, gw1, gw3, gw2


def _local_step(x, tgt, plan):
    n = x.shape[0]
    grads = plan.grads

    def f(name):
        w = plan.get(name)
        return w.reshape(1, D_MODEL) if name.endswith('_norm') and name != 'gla_out_norm' else w

    def carried(tag, fn, *args, **kw):
        plan.before(tag)
        out = fn(*args, **kw)
        plan.after(tag)
        return out

    h1, ffn1 = _ffn_fwd(x, f('ffn1_norm'), f('ffn1_w1'), f('ffn1_w3'), f('ffn1_w2'), "ffn1", plan)
    u = carried("mix_rms", _rms_fwd, h1, f('mix_norm'), "mix_rms")
    w_in = f('w_in')
    w_a = jnp.concatenate([w_in[:, :512], _pad_heads(w_in[:, 512:768]), _pad_heads(w_in[:, 768:1024]), w_in[:, 1024:2048]],
                          axis=1)
    w_al = jnp.pad(w_in[:, 2048:2048 + GLA_RANK], ((0, 0), (0, LANE - GLA_RANK)))
    w_g = w_in[:, 2048 + GLA_RANK:]
    za = carried("in_a", _mm, u, w_a, name="in_a")
    zg = carried("in_g", _mm, u, w_g, name="in_g")
    al = _mm(u, w_al, name="in_al")
    ar, ai, bbar_re, bbar_im = _s5_discretize(f('s5_lambda_re'), f('s5_lambda_im'), f('s5_log_dt'), f('s5_b_re'), f('s5_b_im'))
    t_b = _bd_tiles(bbar_re.transpose(0, 2, 1), bbar_im.transpose(0, 2, 1)).astype(BF16)
    t_c = _bd_tiles(f('s5_c_re'), -f('s5_c_im')).astype(BF16)
    ar8 = jnp.broadcast_to(ar.reshape(1, S5_GP), (SEG, S5_GP))
    ai8 = jnp.broadcast_to(ai.reshape(1, S5_GP), (SEG, S5_GP))
    pw_r, pw_i = _segment_powers(ar, ai, n // SEG)
    dskip = f('s5_d').reshape(1, S5_W)
    u_s5 = _permute_rows(za[:, :S5_W])
    xs = _s5_scan(u_s5, t_b, ar8, ai8, pw_r, pw_i, "s5_scan")
    ys_p = _bd_reduce(xs, t_c, _scale_rows(u_s5, dskip, "s5_skip"), "s5_y")
    ys = _unpermute_rows(ys_p)
    zgelu = _gelu_fwd(ys, "s5_gelu")
    t_glu = _mm(zgelu, f('s5_glu_w'), bias=f('s5_glu_b').reshape(1, S5_W), name="s5_glu_t")
    y_s5 = _glu_fwd(zgelu, t_glu, "s5_glu")
    wup = jnp.pad(f('gla_a_up_w'), ((0, LANE - GLA_RANK), (0, 0)))
    wup_h = wup.reshape(LANE, GLA_HEADS, GLA_DK).transpose(1, 0, 2)
    bup_h = f('gla_a_up_b').reshape(GLA_HEADS, 1, GLA_DK)
    gn_h = f('gla_out_norm').reshape(GLA_HEADS, 1, GLA_DV)
    y_gla, s_prev = carried("gla_fwd", _gla_fwd, za, al, wup_h, bup_h, gn_h, "gla_fwd")
    ps = _mm(y_s5, f('proj_s5'), name="proj_s5")
    pg = carried("proj_gla", _mm, y_gla, f('proj_gla'), name="proj_gla")
    merged = _merge_fwd(zg, ps, pg, "merge")
    h2 = _mm(merged, f('w_out'), res=h1, name="w_out")
    h3, ffn2 = _ffn_fwd(h2, f('ffn2_norm'), f('ffn2_w1'), f('ffn2_w3'), f('ffn2_w2'), "ffn2", plan)
    loss, dh3, g_final = _final_loss(h3, f('final_norm').reshape(1, D_MODEL), tgt, "loss")
    plan.loss = loss[0, 0]
    grads['final_norm'] = g_final.reshape(D_MODEL)
    dh2, grads['ffn2_norm'], grads['ffn2_w1'], grads['ffn2_w3'], grads['ffn2_w2'] = _ffn_bwd(
        dh3, ffn2, f('ffn2_norm'), f('ffn2_w1'), f('ffn2_w3'), f('ffn2_w2'), "ffn2", plan)
    dm = carried("d_merged", _mm, dh2, f('w_out'), tb=True, name="d_merged")
    grads['w_out'] = _mm(merged, dh2, ta=True, name="g_w_out")
    dps, dpg, dzg = carried("d_merge", _merge_bwd, dm, zg, ps, pg, "d_merge")
    grads['proj_s5'] = _mm(y_s5, dps, ta=True, name="g_proj_s5")
    grads['proj_gla'] = _mm(y_gla, dpg, ta=True, name="g_proj_gla")
    dy_s5 = _mm(dps, f('proj_s5'), tb=True, name="d_y_s5")
    dy_gla = _mm(dpg, f('proj_gla'), tb=True, name="d_y_gla")
    dzgelu, dt_glu, g_glu_b = _glu_bwd1(dy_s5, zgelu, t_glu, "d_glu")
    grads['s5_glu_b'] = g_glu_b.reshape(S5_W)
    grads['s5_glu_w'] = _mm(zgelu, dt_glu, ta=True, name="g_glu_w")
    dzgelu = _mm(dt_glu, f('s5_glu_w'), tb=True, res=dzgelu, name="d_gelu")
    dys, du_skip, g_d = _glu_bwd2(_permute_rows(dzgelu), ys_p, u_s5, dskip, "d_s5_y")
    grads['s5_d'] = g_d.reshape(S5_G, S5_H)
    lam, da8 = _s5_scan_bwd(dys, t_c, xs, ar8, ai8, pw_r, pw_i, "s5_scan_bwd")
    g_c = _bd_blocks(_bd_outer(dys, xs, "g_s5_c"))
    grads['s5_c_re'], grads['s5_c_im'] = g_c[0], -g_c[1]
    g_b = _bd_blocks(_bd_outer(u_s5, lam, "g_s5_b")).transpose(0, 1, 3, 2)
    g_bbar_re, g_bbar_im = g_b[0], g_b[1]
    da = jnp.sum(da8, axis=0)
    g_ar, g_ai = da[:S5_GP].reshape(S5_G, S5_P), da[S5_GP:].reshape(S5_G, S5_P)
    _, disc_vjp = jax.vjp(_s5_discretize, f('s5_lambda_re'), f('s5_lambda_im'), f('s5_log_dt'), f('s5_b_re'), f('s5_b_im'))
    (grads['s5_lambda_re'], grads['s5_lambda_im'], grads['s5_log_dt'], grads['s5_b_re'],
     grads['s5_b_im']) = disc_vjp((g_ar, g_ai, g_bbar_re, g_bbar_im))
    du_s5 = _unpermute_rows(_bd_reduce(lam, t_b, du_skip, "d_s5_u"))
    dza, dz, dgn, dbup = carried("gla_bwd", _gla_bwd, za, al, wup_h, bup_h, gn_h, s_prev, dy_gla, du_s5, "gla_bwd")
    grads['gla_out_norm'] = dgn.reshape(GLA_HEADS * GLA_DV)
    grads['gla_a_up_b'] = dbup.reshape(GLA_HEADS * GLA_DK)
    grads['gla_a_up_w'] = _unpad_heads(_mm(al, dz, ta=True, name="g_a_up")[:GLA_RANK])
    dal = _mm(dz, _pad_heads(wup), tb=True, name="d_a_low")
    g_wa = _mm(u, dza, ta=True, name="g_in_a")
    g_wg = _mm(u, dzg, ta=True, name="g_in_g")
    g_wal = _mm(u, dal, ta=True, name="g_in_al")
    grads['w_in'] = jnp.concatenate([g_wa[:, :512], _unpad_heads(g_wa[:, 512:1024]), _unpad_heads(g_wa[:, 1024:1536]),
                                     g_wa[:, 1536:], g_wal[:, :GLA_RANK], g_wg], axis=1)
    du = carried("d_u_a", _mm, dza, w_a, tb=True, name="d_u_a")
    du = _mm(dzg, w_g, tb=True, res=du, name="d_u_g")
    du = _mm(dal, w_al, tb=True, res=du, name="d_u_al")
    dh1, g_mix = carried("d_mix_rms", _rms_bwd, h1, f('mix_norm'), du, dh2, "d_mix_rms")
    grads['mix_norm'] = g_mix
    dx, grads['ffn1_norm'], grads['ffn1_w1'], grads['ffn1_w3'], grads['ffn1_w2'] = _ffn_bwd(
        dh1, ffn1, f('ffn1_norm'), f('ffn1_w1'), f('ffn1_w3'), f('ffn1_w2'), "ffn1", plan)
    return loss[0, 0], dx


MIXER_WEIGHTS = ['w_in', 's5_glu_w', 'proj_s5', 'proj_gla', 'w_out', 'gla_a_up_w']
FFN1_WEIGHTS, FFN2_WEIGHTS = FFN_WEIGHTS[:3], FFN_WEIGHTS[3:]
TRANSPOSED = ['ffn1_w1', 'ffn1_w3', 'ffn2_w1', 'ffn2_w3']


def _local_shard(w, nm):
    return jnp.swapaxes(w, 1, 2)[0] if nm in TRANSPOSED else w[0]
FFN1_EARLY = ['ffn1_w2']
FFN1_LATE = ['ffn1_w1', 'ffn1_w3']
GRAD_GROUPS = {'ffn2': FFN2_WEIGHTS, 'mixer': ['w_out', 'proj_s5', 'proj_gla', 's5_glu_w', 'w_in'], 'ffn1': FFN1_WEIGHTS}


class _Plan:
    def __init__(self, a, c_arr, s_arr):
        self.a, self.c_arr, self.s_arr = a, c_arr, s_arr
        self.grads, self.weights, self.riding = {}, {}, {}
        self.g4s, self.chip_sums, self.halves, self.sib_halves = {}, {}, {}, {}
        for nm in SMALL:
            if nm != 'gla_a_up_w':
                self.weights[nm] = a[nm] if nm == 'final_norm' else a[nm][0]
        ici = _side_gather_ici(self._shards(FFN1_WEIGHTS))
        _run_side(ici, "gather_ffn1_ici")
        self._gathered(FFN1_WEIGHTS, _run_side(_side_gather_d2d(ici.outs), "gather_ffn1_d2d"))

    def _shards(self, names):
        return [_local_shard(self.a[nm], nm).astype(F32 if nm == 'gla_a_up_w' else BF16) for nm in names]

    def _gathered(self, names, arrs):
        for nm, g4 in zip(names, arrs):
            if nm in FFN_WEIGHTS:
                self.weights[nm] = g4
            elif nm in COL_SHARDED:
                self.weights[nm] = jnp.concatenate([g4[s] for s in range(4)], axis=1)
            else:
                self.weights[nm] = g4.reshape(4 * g4.shape[1], g4.shape[2])

    def get(self, name):
        return self.weights[name]

    def _shard_major(self, nm):
        g = self.grads[nm]
        if nm in FFN_WEIGHTS:
            return g
        if nm in COL_SHARDED:
            return jnp.stack(jnp.split(g, 4, axis=1))
        return g.reshape(4, g.shape[0] // 4, g.shape[1])

    def _schedule(self, tag):
        grp = GRAD_GROUPS
        gathers = {"ffn1_fwd": ('ici', MIXER_WEIGHTS), "mix_rms": ('d2d', MIXER_WEIGHTS),
                   "in_a": ('ici', FFN2_WEIGHTS[:1]), "in_g": ('d2d', FFN2_WEIGHTS[:1]),
                   "gla_fwd": ('ici', FFN2_WEIGHTS[1:]), "proj_gla": ('d2d', FFN2_WEIGHTS[1:])}
        if tag in gathers:
            kind, names = gathers[tag]
            key = tuple(names)
            if kind == 'ici':
                return [(_side_gather_ici(self._shards(names)), lambda outs: self.riding.update({key: outs}))]
            return [(_side_gather_d2d(self.riding[key]), lambda outs: self._gathered(names, outs))]
        steps = {"ffn2_gw1": (['ffn2_w2'], 0), "ffn2_gw3": (['ffn2_w1'], 0), "d_merged": (['ffn2_w3'], 0),
                 "gla_bwd": (grp['ffn2'], 1), "d_mix_rms": (grp['ffn2'], 2),
                 "d_u_a": (grp['mixer'], 0), "ffn1_bwd": (grp['mixer'], 1), "ffn1_gw2": (grp['mixer'], 2),
                 "ffn1_gw1": (FFN1_EARLY, 0), "ffn1_gw3": (FFN1_EARLY, 1), "adamw_early": (FFN1_LATE, 1)}
        entries = [self._reduce_stage(*steps[tag])] if tag in steps else []
        if tag == "ffn1_gw2":
            entries.append(self._small_stage(0))
        if tag == "ffn1_gw1":
            entries.append(self._small_stage(1))
        return entries

    def _small_stage(self, stage):
        if stage == 0:
            a, grads = self.a, self.grads
            self.small_parts = ([grads[nm].reshape(a[nm].shape) for nm in SMALL if nm != 'gla_a_up_w']
                                + [grads['gla_a_up_w'], self.loss.reshape(1)])
            packed = _pack_small(self.small_parts)

            def done(outs):
                self.small_pair = _small_add(packed, outs[0], "small_sum_pair")
            return _side_small_sibling(packed), done

        def done(outs):
            self.small_total = _small_add(self.small_pair, outs[0], "small_sum_chips")
        return _side_small_chips(self.small_pair), done

    def _reduce_stage(self, names, stage):
        if stage == 0:
            for nm in names:
                self.g4s[nm] = self._shard_major(nm)

            def done(outs):
                for nm, r in zip(names, outs):
                    self.chip_sums[nm] = _chip_sum(self.g4s[nm], r, self.c_arr, f"chip_sum_{nm}")
            return _side_swap_halves([self.g4s[nm] for nm in names]), done
        if stage == 1:
            def done(outs):
                for nm, o in zip(names, outs):
                    self.halves[nm] = _owner_sum(self.chip_sums[nm], o, self.s_arr, f"owner_sum_{nm}")
            return _side_scatter([self.chip_sums[nm] for nm in names]), done

        def done(outs):
            self.sib_halves.update(zip(names, outs))
        return _side_swap_reduced([self.halves[nm] for nm in names]), done

    def before(self, tag):
        entries = self._schedule(tag)
        if entries:
            merged = _merge_sides([side for side, _ in entries])
            self.riding[tag] = (merged, entries)
            _RIDER.append(merged)

    def after(self, tag):
        if tag in self.riding:
            merged, entries = self.riding.pop(tag)
            assert not _RIDER and merged.outs is not None, tag
            pos = 0
            for side, done in entries:
                done(merged.outs[pos:pos + len(side.out_shapes)])
                pos += len(side.out_shapes)

    def finish_alone(self, stage):
        names = FFN1_LATE if stage == 0 else GRAD_GROUPS['ffn1']
        side, done = self._reduce_stage(names, stage)
        done(_run_side(side, f"grad_ffn1_stage{stage}"))


def _train_step(a):
    x = a['x'][0]
    tgt = a['loss_target'][0]
    xi, yi, ci = lax.axis_index("x"), lax.axis_index("y"), lax.axis_index("c")
    c_arr = jnp.reshape(ci, (1,)).astype(jnp.int32)
    s_arr = jnp.reshape(2 * xi + yi, (1,)).astype(jnp.int32)
    plan = _Plan(a, c_arr, s_arr)
    loss, dx = _local_step(x, tgt, plan)
    red = {}
    small_sum = _unpack_small(plan.small_total, plan.small_parts)
    small_names = [nm for nm in SMALL if nm != 'gla_a_up_w']
    for nm, g in zip(small_names, small_sum[:-2]):
        red[nm] = g
    loss = small_sum[-1].reshape(())
    g_up = small_sum[-2]
    red['gla_a_up_w'] = lax.dynamic_slice(g_up, (0, (2 * xi + yi) * GLA_DK), (GLA_RANK, GLA_DK))
    out_g, out_d, out_m, out_v = {}, {}, {}, {}

    def update(names, tag):
        items = [(_local_shard(a[nm], nm), plan.halves[nm], plan.sib_halves[nm], _local_shard(a['m_' + nm], nm),
                  _local_shard(a['v_' + nm], nm)) for nm in names]
        plan.before(tag)
        res = _adamw_group(items, c_arr, tag)
        plan.after(tag)
        for k, nm in enumerate(names):
            back = (lambda t: jnp.swapaxes(t[None], 1, 2)) if nm in TRANSPOSED else (lambda t: t[None])
            out_g[nm], out_d[nm], out_m[nm], out_v[nm] = (back(t) for t in res[4 * k:4 * k + 4])

    plan.finish_alone(0)
    update([nm for nm in SHARDED if nm not in GRAD_GROUPS['ffn1']], "adamw_early")
    plan.finish_alone(2)
    update(GRAD_GROUPS['ffn1'], "adamw_ffn1")
    rest = [nm for nm in WEIGHTS if nm not in SHARDED]
    pk = lambda pre: _pack_small([a[pre + nm] for nm in rest])
    d, nm_, nv_ = _adamw(pk(''), _pack_small([red[nm] for nm in rest]), pk('m_'), pk('v_'), "adamw_small")
    like = [a[nm] for nm in rest]
    for nm, g, dd, mm_, vv_ in zip(rest, [red[nm].reshape(a[nm].shape) for nm in rest], _unpack_small(d, like),
                                   _unpack_small(nm_, like), _unpack_small(nv_, like)):
        out_g[nm], out_d[nm], out_m[nm], out_v[nm] = g, dd, mm_, vv_
    return (loss, dx[None], *[out_g[nm] for nm in WEIGHTS], *[out_d[nm] for nm in WEIGHTS],
            *[out_m[nm] for nm in WEIGHTS], *[out_v[nm] for nm in WEIGHTS])


def kernel(x, ffn1_norm, ffn1_w1, ffn1_w3, ffn1_w2, mix_norm, w_in, s5_lambda_re, s5_lambda_im, s5_log_dt, s5_b_re, s5_b_im, s5_c_re, s5_c_im, s5_d, s5_glu_w, s5_glu_b, gla_a_up_w, gla_a_up_b, gla_out_norm, proj_s5, proj_gla, w_out, ffn2_norm, ffn2_w1, ffn2_w3, ffn2_w2, final_norm, loss_target, m_ffn1_norm, m_ffn1_w1, m_ffn1_w3, m_ffn1_w2, m_mix_norm, m_w_in, m_s5_lambda_re, m_s5_lambda_im, m_s5_log_dt, m_s5_b_re, m_s5_b_im, m_s5_c_re, m_s5_c_im, m_s5_d, m_s5_glu_w, m_s5_glu_b, m_gla_a_up_w, m_gla_a_up_b, m_gla_out_norm, m_proj_s5, m_proj_gla, m_w_out, m_ffn2_norm, m_ffn2_w1, m_ffn2_w3, m_ffn2_w2, m_final_norm, v_ffn1_norm, v_ffn1_w1, v_ffn1_w3, v_ffn1_w2, v_mix_norm, v_w_in, v_s5_lambda_re, v_s5_lambda_im, v_s5_log_dt, v_s5_b_re, v_s5_b_im, v_s5_c_re, v_s5_c_im, v_s5_d, v_s5_glu_w, v_s5_glu_b, v_gla_a_up_w, v_gla_a_up_b, v_gla_out_norm, v_proj_s5, v_proj_gla, v_w_out, v_ffn2_norm, v_ffn2_w1, v_ffn2_w3, v_ffn2_w2, v_final_norm):
    return _train_step(dict(locals()))
```

```python
import functools

import jax
import jax.numpy as jnp
from jax import lax
from jax.experimental import pallas as pl
from jax.experimental.pallas import tpu as pltpu

F32 = jnp.float32
BF16 = jnp.bfloat16
HI = lax.Precision.HIGHEST
MESH_ID = pl.DeviceIdType.MESH

D_MODEL = 1024
EPS = 1e-6
S5_G, S5_P, S5_H = 32, 64, 16
S5_W = S5_G * S5_H
S5_GP = S5_G * S5_P
SEG = 8
SCAN_ROWS = 256
GLA_HEADS, GLA_DK, GLA_DV = 4, 64, 128
GLA_CHUNK = 64
GLA_TAU = 16.0
GLA_RANK = 16
ADAM_LR, ADAM_B1, ADAM_B2, ADAM_EPS, ADAM_WD, ADAM_STEP = 0.001, 0.9, 0.999, 1e-08, 0.01, 10
V7X_VMEM_LIMIT = 56 * 1024 * 1024
LANE = 128

WEIGHTS = ['ffn1_norm', 'ffn1_w1', 'ffn1_w3', 'ffn1_w2', 'mix_norm', 'w_in', 's5_lambda_re', 's5_lambda_im',
           's5_log_dt', 's5_b_re', 's5_b_im', 's5_c_re', 's5_c_im', 's5_d', 's5_glu_w', 's5_glu_b', 'gla_a_up_w',
           'gla_a_up_b', 'gla_out_norm', 'proj_s5', 'proj_gla', 'w_out', 'ffn2_norm', 'ffn2_w1', 'ffn2_w3',
           'ffn2_w2', 'final_norm']
SHARDED = ['ffn1_w1', 'ffn1_w3', 'ffn1_w2', 'w_in', 's5_glu_w', 'proj_s5', 'proj_gla', 'w_out',
           'ffn2_w1', 'ffn2_w3', 'ffn2_w2']
COL_SHARDED = ['ffn1_w1', 'ffn1_w3', 'w_in', 'proj_s5', 'proj_gla', 'ffn2_w1', 'ffn2_w3', 'gla_a_up_w']
SMALL = [n for n in WEIGHTS if n not in SHARDED]
FFN_WEIGHTS = ['ffn1_w1', 'ffn1_w3', 'ffn1_w2', 'ffn2_w1', 'ffn2_w3', 'ffn2_w2']


def _params(**kw):
    return pltpu.CompilerParams(vmem_limit_bytes=V7X_VMEM_LIMIT, **kw)


class _Side:
    def __init__(self, ins, out_shapes, nsem, copies, aliased=False):
        self.ins, self.out_shapes, self.nsem, self.copies, self.aliased = list(ins), list(out_shapes), nsem, copies, aliased
        self.outs = None


_RIDER = []


def _pcall(body, **kw):
    if _RIDER:
        return _carry(body, _RIDER.pop(), **kw)
    return pl.pallas_call(body, **kw)


def _carry(body, side, *, name, grid, in_specs, out_specs, out_shape, scratch_shapes=(), compiler_params=None):
    del compiler_params
    single = not isinstance(out_shape, (list, tuple))
    out_specs = [out_specs] if single else list(out_specs)
    out_shape = [out_shape] if single else list(out_shape)
    n_in, n_out, n_scr = len(in_specs), len(out_shape), len(scratch_shapes)
    s_in, s_out = len(side.ins), len(side.out_shapes)
    any_spec = pl.BlockSpec(memory_space=pl.ANY)

    def wrapped(*refs):
        cuts = [n_in, s_in, n_out, s_out, n_scr]
        parts, pos = [], 0
        for c in cuts:
            parts.append(refs[pos:pos + c])
            pos += c
        ins, sins, outs, souts, scr = parts
        ssem, rsem = refs[pos], refs[pos + 1]
        first = last = None
        for d, g in enumerate(grid):
            i = pl.program_id(d)
            first = (i == 0) if first is None else first & (i == 0)
            last = (i == g - 1) if last is None else last & (i == g - 1)

        @pl.when(first)
        def _():
            for cp in side.copies(sins, souts, ssem, rsem):
                cp.start()

        body(*ins, *outs, *scr)

        @pl.when(last)
        def _():
            for cp in side.copies(sins, souts, ssem, rsem):
                cp.wait()

    call = pl.pallas_call(
        wrapped, name=name, grid=grid, in_specs=list(in_specs) + [any_spec] * s_in,
        out_specs=out_specs + [any_spec] * s_out, out_shape=out_shape + side.out_shapes,
        scratch_shapes=list(scratch_shapes) + [pltpu.SemaphoreType.DMA((side.nsem,)), pltpu.SemaphoreType.DMA((side.nsem,))],
        input_output_aliases={n_in + j: n_out + j for j in range(s_in)} if side.aliased else {},
        compiler_params=_params(has_side_effects=True))

    def run(*args):
        res = call(*args, *side.ins)
        side.outs = list(res[n_out:])
        return res[0] if single else list(res[:n_out])

    return run


def _run_side(side, name):
    s_in, s_out = len(side.ins), len(side.out_shapes)
    any_spec = pl.BlockSpec(memory_space=pl.ANY)

    def body(*refs):
        sins, souts = refs[:s_in], refs[s_in:s_in + s_out]
        ssem, rsem = refs[s_in + s_out:]
        cps = side.copies(sins, souts, ssem, rsem)
        for cp in cps:
            cp.start()
        for cp in cps:
            cp.wait()

    side.outs = list(pl.pallas_call(
        body, name=name, in_specs=[any_spec] * s_in, out_specs=[any_spec] * s_out, out_shape=side.out_shapes,
        scratch_shapes=[pltpu.SemaphoreType.DMA((side.nsem,)), pltpu.SemaphoreType.DMA((side.nsem,))],
        input_output_aliases={j: j for j in range(s_in)} if side.aliased else {},
        compiler_params=pltpu.CompilerParams(has_side_effects=True))(*side.ins))
    return side.outs


def _pick(n, cap, quantum):
    if n <= cap:
        return n
    best = None
    for t in range(quantum, cap + 1, quantum):
        if n % t == 0:
            best = t
    assert best is not None, (n, cap, quantum)
    return best


def _sigmoid(x):
    return jax.nn.sigmoid(x)


def _mm(a, b, *, name, ta=False, tb=False, out_dtype=F32, alpha=1.0, res=None, bias=None, exact=False, shard=None):
    ns = 4
    (k_a, m) = a.shape[-2:] if ta else a.shape[-2:][::-1]
    (k_b, n) = b.shape[-2:][::-1] if tb else b.shape[-2:]
    assert k_a == k_b, (a.shape, b.shape, ta, tb)
    assert (a.ndim == 3) == (shard in ('k', 'm')) and (b.ndim == 3) == (shard in ('n', 'k'))
    k = k_a
    tm = _pick(m, 1024, 128)
    tn = _pick(n, 1024, 128)
    tk = _pick(k, 1024, 128)
    pm, pn, pk = m // tm, n // tn, k // tk
    gm = pm * (ns if shard == 'm' else 1)
    gn = pn * (ns if shard == 'n' else 1)
    gk = pk * (ns if shard == 'k' else 1)
    dims = (((0,) if ta else (1,), (1,) if tb else (0,)), ((), ()))
    op_dtype = F32 if exact else BF16

    def body(*refs):
        a_ref, b_ref = refs[0], refs[1]
        pos = 2
        res_ref = bias_ref = None
        if res is not None:
            res_ref = refs[pos]
            pos += 1
        if bias is not None:
            bias_ref = refs[pos]
            pos += 1
        o_ref, acc_ref = refs[pos], refs[pos + 1]
        kk = pl.program_id(2)

        @pl.when(kk == 0)
        def _():
            acc_ref[...] = jnp.zeros_like(acc_ref)

        acc_ref[...] += lax.dot_general(a_ref[...].astype(op_dtype), b_ref[...].astype(op_dtype), dims,
                                        precision=HI if exact else None, preferred_element_type=F32)

        @pl.when(kk == gk - 1)
        def _():
            o = acc_ref[...]
            if alpha != 1.0:
                o = o * alpha
            if bias_ref is not None:
                o = o + bias_ref[...]
            if res_ref is not None:
                o = o + res_ref[...]
            o_ref[...] = o.astype(out_dtype)

    def spec(block, sharded_on, order):
        per = {'m': pm, 'n': pn, 'k': pk}

        def index(i, j, kk):
            g = {'m': i, 'n': j, 'k': kk}
            r, c = order(i % pm if shard == 'm' else i, j % pn if shard == 'n' else j, kk % pk if shard == 'k' else kk)
            if sharded_on is None:
                return (r, c)
            return (g[sharded_on] // per[sharded_on], r, c)

        return pl.BlockSpec(block if sharded_on is None else (None,) + block, index)

    a_sh = shard if shard in ('k', 'm') else None
    b_sh = shard if shard in ('n', 'k') else None
    o_sh = shard if shard in ('n', 'm') else None
    a_spec = spec((tk, tm), a_sh, lambda i, j, kk: (kk, i)) if ta else spec((tm, tk), a_sh, lambda i, j, kk: (i, kk))
    b_spec = spec((tn, tk), b_sh, lambda i, j, kk: (j, kk)) if tb else spec((tk, tn), b_sh, lambda i, j, kk: (kk, j))
    ins, in_specs = [a, b], [a_spec, b_spec]
    if res is not None:
        assert o_sh is None
        ins.append(res)
        in_specs.append(pl.BlockSpec((tm, tn), lambda i, j, kk: (i, j)))
    if bias is not None:
        assert o_sh is None
        ins.append(bias)
        in_specs.append(pl.BlockSpec((1, tn), lambda i, j, kk: (0, j)))
    out_shape = (m, n) if o_sh is None else (ns, m, n)
    return _pcall(body, name=name, grid=(gm, gn, gk), in_specs=in_specs,
                  out_specs=spec((tm, tn), o_sh, lambda i, j, kk: (i, j)),
                  out_shape=jax.ShapeDtypeStruct(out_shape, out_dtype),
                  scratch_shapes=[pltpu.VMEM((tm, tn), F32)], compiler_params=_params())(*ins)


ROWS_VMEM_BUDGET = 24 * 1024 * 1024


def _rows(body, ins, outs, *, n, name):
    cols = sum(a.shape[1] for a, kind in ins if kind == 'r') + sum(c for c, _, kind in outs if kind == 'r')
    cap = 256
    while cap < 2048 and 2 * 4 * cols * (2 * cap) <= ROWS_VMEM_BUDGET:
        cap *= 2
    tm = _pick(n, cap, 16)
    in_specs = []
    for arr, kind in ins:
        if kind == 'r':
            in_specs.append(pl.BlockSpec((tm, arr.shape[1]), lambda i: (i, 0)))
        else:
            in_specs.append(pl.BlockSpec(arr.shape, lambda i: (0, 0)))
    out_specs, out_shape = [], []
    for cols, dtype, kind in outs:
        if kind == 'r':
            out_specs.append(pl.BlockSpec((tm, cols), lambda i: (i, 0)))
            out_shape.append(jax.ShapeDtypeStruct((n, cols), dtype))
        else:
            out_specs.append(pl.BlockSpec((1, cols), lambda i: (0, 0)))
            out_shape.append(jax.ShapeDtypeStruct((1, cols), dtype))
    n_in = len(ins)
    acc_ids = [j for j, o in enumerate(outs) if o[2] == 'a']

    def wrapped(*refs):
        if acc_ids:
            @pl.when(pl.program_id(0) == 0)
            def _():
                for j in acc_ids:
                    refs[n_in + j][...] = jnp.zeros_like(refs[n_in + j])
        body(*refs)

    res = _pcall(wrapped, name=name, grid=(n // tm,), in_specs=in_specs, out_specs=out_specs, out_shape=out_shape,
                 compiler_params=_params())(*[a for a, _ in ins])
    return res


def _rms_fwd(x, g, name):
    def body(x_ref, g_ref, o_ref):
        xv = x_ref[...]
        rstd = lax.rsqrt(jnp.mean(xv * xv, axis=-1, keepdims=True) + EPS)
        o_ref[...] = (xv * rstd * g_ref[...]).astype(BF16)
    return _rows(body, [(x, 'r'), (g, 'f')], [(x.shape[1], BF16, 'r')], n=x.shape[0], name=name)[0]


def _rms_bwd(x, g, dn, dres, name):
    def body(x_ref, g_ref, dn_ref, dres_ref, dx_ref, dg_ref):
        xv = x_ref[...]
        rstd = lax.rsqrt(jnp.mean(xv * xv, axis=-1, keepdims=True) + EPS)
        xh = xv * rstd
        dn = dn_ref[...]
        dg_ref[...] += jnp.sum(dn * xh, axis=0, keepdims=True)
        dxh = dn * g_ref[...]
        dx_ref[...] = dres_ref[...] + rstd * (dxh - xh * jnp.mean(dxh * xh, axis=-1, keepdims=True))
    d = x.shape[1]
    return _rows(body, [(x, 'r'), (g, 'f'), (dn, 'r'), (dres, 'r')], [(d, F32, 'r'), (d, F32, 'a')],
                 n=x.shape[0], name=name)


def _gelu_parts(y):
    c0 = 0.7978845608028654
    inner = c0 * (y + 0.044715 * y * y * y)
    th = jnp.tanh(inner)
    return th, c0 * (1.0 + 3.0 * 0.044715 * y * y)


def _gelu_fwd(y, name):
    def body(y_ref, o_ref):
        yv = y_ref[...]
        th, _ = _gelu_parts(yv)
        o_ref[...] = 0.5 * yv * (1.0 + th)
    return _rows(body, [(y, 'r')], [(y.shape[1], F32, 'r')], n=y.shape[0], name=name)[0]


def _glu_fwd(zg, t, name):
    def body(z_ref, t_ref, o_ref):
        o_ref[...] = (z_ref[...] * _sigmoid(t_ref[...])).astype(BF16)
    return _rows(body, [(zg, 'r'), (t, 'r')], [(zg.shape[1], BF16, 'r')], n=zg.shape[0], name=name)[0]


def _glu_bwd1(dy, zg, t, name):
    def body(dy_ref, z_ref, t_ref, dz_ref, dt_ref, db_ref):
        dyv, zv = dy_ref[...], z_ref[...]
        sg = _sigmoid(t_ref[...])
        dz_ref[...] = dyv * sg
        dt = dyv * zv * sg * (1.0 - sg)
        dt_ref[...] = dt.astype(BF16)
        db_ref[...] += jnp.sum(dt, axis=0, keepdims=True)
    w = zg.shape[1]
    return _rows(body, [(dy, 'r'), (zg, 'r'), (t, 'r')], [(w, F32, 'r'), (w, BF16, 'r'), (w, F32, 'a')],
                 n=zg.shape[0], name=name)


def _glu_bwd2(dzg, ys, u, dskip, name):
    def body(dz_ref, y_ref, u_ref, d_ref, dy_ref, du_ref, dd_ref):
        yv = y_ref[...]
        th, dinner = _gelu_parts(yv)
        dy = dz_ref[...] * (0.5 * (1.0 + th) + 0.5 * yv * (1.0 - th * th) * dinner)
        dy_ref[...] = dy
        du_ref[...] = dy * d_ref[...]
        dd_ref[...] += jnp.sum(dy * u_ref[...], axis=0, keepdims=True)
    w = ys.shape[1]
    return _rows(body, [(dzg, 'r'), (ys, 'r'), (u, 'r'), (dskip, 'f')], [(w, F32, 'r'), (w, F32, 'r'), (w, F32, 'a')],
                 n=ys.shape[0], name=name)


def _scale_rows(u, dskip, name):
    def body(u_ref, d_ref, o_ref):
        o_ref[...] = u_ref[...] * d_ref[...]
    return _rows(body, [(u, 'r'), (dskip, 'f')], [(u.shape[1], F32, 'r')], n=u.shape[0], name=name)[0]


def _merge_fwd(zg, ps, pg, name):
    def body(z_ref, ps_ref, pg_ref, o_ref):
        zv = z_ref[...]
        o_ref[...] = (_sigmoid(zv[:, :D_MODEL]) * ps_ref[...] + _sigmoid(zv[:, D_MODEL:]) * pg_ref[...]).astype(BF16)
    return _rows(body, [(zg, 'r'), (ps, 'r'), (pg, 'r')], [(D_MODEL, BF16, 'r')], n=zg.shape[0], name=name)[0]


def _merge_bwd(dm, zg, ps, pg, name):
    def body(dm_ref, z_ref, ps_ref, pg_ref, dps_ref, dpg_ref, dz_ref):
        dmv, zv = dm_ref[...], z_ref[...]
        s1, s2 = _sigmoid(zv[:, :D_MODEL]), _sigmoid(zv[:, D_MODEL:])
        dps_ref[...] = (dmv * s1).astype(BF16)
        dpg_ref[...] = (dmv * s2).astype(BF16)
        dz_ref[:, :D_MODEL] = dmv * ps_ref[...] * s1 * (1.0 - s1)
        dz_ref[:, D_MODEL:] = dmv * pg_ref[...] * s2 * (1.0 - s2)
    return _rows(body, [(dm, 'r'), (zg, 'r'), (ps, 'r'), (pg, 'r')],
                 [(D_MODEL, BF16, 'r'), (D_MODEL, BF16, 'r'), (2 * D_MODEL, F32, 'r')], n=zg.shape[0], name=name)


def _final_loss(h, g, tgt, name):
    def body(h_ref, g_ref, t_ref, loss_ref, dh_ref, dg_ref):
        hv = h_ref[...]
        rstd = lax.rsqrt(jnp.mean(hv * hv, axis=-1, keepdims=True) + EPS)
        xh = hv * rstd
        err = xh * g_ref[...] - t_ref[...]
        part = 0.5 * jnp.sum(jnp.mean(err * err, axis=-1, keepdims=True), axis=0, keepdims=True)
        loss_ref[...] += jnp.broadcast_to(part, loss_ref.shape)
        dout = err * (1.0 / hv.shape[1])
        dg_ref[...] += jnp.sum(dout * xh, axis=0, keepdims=True)
        dxh = dout * g_ref[...]
        dh_ref[...] = rstd * (dxh - xh * jnp.mean(dxh * xh, axis=-1, keepdims=True))
    d = h.shape[1]
    return _rows(body, [(h, 'r'), (g, 'f'), (tgt, 'r')], [(LANE, F32, 'a'), (d, F32, 'r'), (d, F32, 'a')],
                 n=h.shape[0], name=name)


def _adamw_math(wv, gv, mv, vv):
    nm = ADAM_B1 * mv + (1.0 - ADAM_B1) * gv
    nv = ADAM_B2 * vv + (1.0 - ADAM_B2) * (gv * gv)
    m_hat = nm / (1.0 - ADAM_B1 ** ADAM_STEP)
    v_hat = nv / (1.0 - ADAM_B2 ** ADAM_STEP)
    return -ADAM_LR * (m_hat / (jnp.sqrt(v_hat) + ADAM_EPS) + ADAM_WD * wv), nm, nv


def _adamw(w, g, m, v, name):
    def body(w_ref, g_ref, m_ref, v_ref, d_ref, nm_ref, nv_ref):
        d_ref[...], nm_ref[...], nv_ref[...] = _adamw_math(w_ref[...], g_ref[...], m_ref[...], v_ref[...])
    c = w.shape[1]
    return _rows(body, [(w, 'r'), (g, 'r'), (m, 'r'), (v, 'r')], [(c, F32, 'r')] * 3, n=w.shape[0], name=name)


ADAMW_BLOCKS = 8


def _adamw_group(items, c_arr, name):
    per = ADAMW_BLOCKS // 2
    n = len(items)

    def body(c_ref, *refs):
        mine = (pl.program_id(0) // per) == c_ref[0]
        for k in range(n):
            w_ref, go_ref, gs_ref, m_ref, v_ref = refs[5 * k:5 * k + 5]
            g_ref, d_ref, nm_ref, nv_ref = refs[5 * n + 4 * k:5 * n + 4 * k + 4]
            gv = jnp.where(mine, go_ref[...], gs_ref[...])
            g_ref[...] = gv
            d_ref[...], nm_ref[...], nv_ref[...] = _adamw_math(w_ref[...], gv, m_ref[...], v_ref[...])

    in_specs, out_specs, out_shape, args = [pl.BlockSpec(memory_space=pltpu.SMEM)], [], [], [c_arr]
    for item in items:
        r, cols = item[0].shape
        assert r % (8 * ADAMW_BLOCKS) == 0, item[0].shape
        tr = r // ADAMW_BLOCKS
        full = pl.BlockSpec((tr, cols), lambda i: (i, 0))
        half = pl.BlockSpec((tr, cols), lambda i: (i % per, 0))
        in_specs += [full, half, half, full, full]
        out_specs += [full] * 4
        out_shape += [jax.ShapeDtypeStruct((r, cols), F32)] * 4
        args += list(item)
    return _pcall(body, name=name, grid=(ADAMW_BLOCKS,), in_specs=in_specs, out_specs=out_specs, out_shape=out_shape,
                  compiler_params=_params())(*args)


def _shift_rows(v, sh, down):
    rolled = pltpu.roll(v, sh if down else v.shape[0] - sh, axis=0)
    row = lax.broadcasted_iota(jnp.int32, v.shape, 0)
    keep = (row >= sh) if down else (row < v.shape[0] - sh)
    return jnp.where(keep, rolled, 0.0)


def _chain_segments(st_r, st_i, pw_r_ref, pw_i_ref, conj, down):
    vr, vi = st_r[...], st_i[...]
    sh, k = 1, 0
    while sh < SEG:
        pr, pi = pw_r_ref[k:k + 1, :], pw_i_ref[k:k + 1, :]
        if conj:
            pi = -pi
        sr, si = _shift_rows(vr, sh, down), _shift_rows(vi, sh, down)
        vr, vi = vr + pr * sr - pi * si, vi + pr * si + pi * sr
        sh, k = sh * 2, k + 1
    st_r[...] = _shift_rows(vr, 1, down)
    st_i[...] = _shift_rows(vi, 1, down)


def _expand_block(u_ref, t_ref, bu_ref):
    for j in range(BD_TILES):
        k = j % 4
        bu_ref[:, j * BD_ST:(j + 1) * BD_ST] = _dot(u_ref[:, k * BD_CH:(k + 1) * BD_CH], t_ref[j])


def _s5_scan(u, tiles, ar8, ai8, pw_r, pw_i, name):
    n = u.shape[0]
    rb = SCAN_ROWS
    nb, steps, lc = n // rb, rb // SEG, 512

    def body(u_ref, t_ref, ar_ref, ai_ref, pwr_ref, pwi_ref, x_ref, st_r, st_i, bu_ref):
        ph, b = pl.program_id(0), pl.program_id(1)

        @pl.when((ph == 0) & (b == 0))
        def _():
            st_r[...] = jnp.zeros_like(st_r)
            st_i[...] = jnp.zeros_like(st_i)

        _expand_block(u_ref, t_ref, bu_ref)

        def scan(store):
            for c in range(S5_GP // lc):
                re, im = slice(c * lc, (c + 1) * lc), slice(S5_GP + c * lc, S5_GP + (c + 1) * lc)
                a_r, a_i = ar_ref[:, re], ai_ref[:, re]

                def step(s, carry):
                    xr, xi = carry
                    rows = pl.ds(pl.multiple_of(s * SEG, SEG), SEG)
                    nr = a_r * xr - a_i * xi + bu_ref[rows, re]
                    ni = a_r * xi + a_i * xr + bu_ref[rows, im]
                    if store:
                        x_ref[rows, re] = nr
                        x_ref[rows, im] = ni
                    return nr, ni

                xr, xi = lax.fori_loop(0, steps, step, (st_r[:, re], st_i[:, re]), unroll=4)
                st_r[:, re] = xr
                st_i[:, re] = xi

        @pl.when(ph == 0)
        def _():
            scan(False)

        @pl.when((ph == 0) & (b == nb - 1))
        def _():
            _chain_segments(st_r, st_i, pwr_ref, pwi_ref, conj=False, down=True)

        @pl.when(ph == 1)
        def _():
            scan(True)

    full = lambda a: pl.BlockSpec(a.shape, lambda ph, b: (0, 0))
    return _pcall(body, name=name, grid=(2, nb),
                  in_specs=[pl.BlockSpec((rb, S5_W), lambda ph, b: (b, 0)), pl.BlockSpec(tiles.shape, lambda ph, b: (0, 0, 0)),
                            full(ar8), full(ai8), full(pw_r), full(pw_i)],
                  out_specs=pl.BlockSpec((rb, 2 * S5_GP), lambda ph, b: (b * ph, 0)),
                  out_shape=jax.ShapeDtypeStruct((n, 2 * S5_GP), F32),
                  scratch_shapes=[pltpu.VMEM((SEG, S5_GP), F32), pltpu.VMEM((SEG, S5_GP), F32),
                                  pltpu.VMEM((rb, 2 * S5_GP), F32)],
                  compiler_params=_params())(u, tiles, ar8, ai8, pw_r, pw_i)


def _s5_scan_bwd(dy, tiles, xs, ar8, ai8, pw_r, pw_i, name):
    n = dy.shape[0]
    rb = SCAN_ROWS
    nb, steps, lc = n // rb, rb // SEG, 256

    def body(dy_ref, t_ref, x_ref, ar_ref, ai_ref, pwr_ref, pwi_ref, lam_ref, da_ref, st_r, st_i, gx_ref):
        ph, b = pl.program_id(0), pl.program_id(1)

        @pl.when((ph == 0) & (b == 0))
        def _():
            st_r[...] = jnp.zeros_like(st_r)
            st_i[...] = jnp.zeros_like(st_i)
            da_ref[...] = jnp.zeros_like(da_ref)

        _expand_block(dy_ref, t_ref, gx_ref)

        def scan(store):
            for c in range(S5_GP // lc):
                re, im = slice(c * lc, (c + 1) * lc), slice(S5_GP + c * lc, S5_GP + (c + 1) * lc)
                a_r, a_i = ar_ref[:, re], ai_ref[:, re]

                def step(s, carry):
                    rows = pl.ds(pl.multiple_of((steps - 1 - s) * SEG, SEG), SEG)
                    if store:
                        lr, li, dr, di = carry
                        xr, xi = x_ref[rows, re], x_ref[rows, im]
                        dr = dr + lr * xr + li * xi
                        di = di + li * xr - lr * xi
                    else:
                        lr, li = carry
                    nr = a_r * lr + a_i * li + gx_ref[rows, re]
                    ni = a_r * li - a_i * lr + gx_ref[rows, im]
                    if store:
                        lam_ref[rows, re] = nr
                        lam_ref[rows, im] = ni
                        return nr, ni, dr, di
                    return nr, ni

                if store:
                    lr, li, dr, di = lax.fori_loop(0, steps, step, (st_r[:, re], st_i[:, re], da_ref[:, re], da_ref[:, im]),
                                                   unroll=4)
                    da_ref[:, re] = dr
                    da_ref[:, im] = di
                else:
                    lr, li = lax.fori_loop(0, steps, step, (st_r[:, re], st_i[:, re]), unroll=4)
                st_r[:, re] = lr
                st_i[:, re] = li

        @pl.when(ph == 0)
        def _():
            scan(False)

        @pl.when((ph == 0) & (b == nb - 1))
        def _():
            _chain_segments(st_r, st_i, pwr_ref, pwi_ref, conj=True, down=False)

        @pl.when(ph == 1)
        def _():
            scan(True)

    full = lambda a: pl.BlockSpec(a.shape, lambda ph, b: (0, 0))
    rev = lambda ph, b: (nb - 1 - b, 0)
    return _pcall(body, name=name, grid=(2, nb),
                  in_specs=[pl.BlockSpec((rb, S5_W), rev), pl.BlockSpec(tiles.shape, lambda ph, b: (0, 0, 0)),
                            pl.BlockSpec((rb, 2 * S5_GP), lambda ph, b: ((nb - 1 - b) * ph, 0)),
                            full(ar8), full(ai8), full(pw_r), full(pw_i)],
                  out_specs=[pl.BlockSpec((rb, 2 * S5_GP), lambda ph, b: (nb - 1 - b * ph, 0)),
                             pl.BlockSpec((SEG, 2 * S5_GP), lambda ph, b: (0, 0))],
                  out_shape=[jax.ShapeDtypeStruct((n, 2 * S5_GP), F32), jax.ShapeDtypeStruct((SEG, 2 * S5_GP), F32)],
                  scratch_shapes=[pltpu.VMEM((SEG, S5_GP), F32), pltpu.VMEM((SEG, S5_GP), F32),
                                  pltpu.VMEM((rb, 2 * S5_GP), F32)],
                  compiler_params=_params())(dy, tiles, xs, ar8, ai8, pw_r, pw_i)


def _s5_discretize(lam_re, lam_im, log_dt, b_re, b_im):
    dt = jnp.exp(log_dt)[:, None]
    mag = jnp.exp(lam_re * dt)
    ar = mag * jnp.cos(lam_im * dt)
    ai = mag * jnp.sin(lam_im * dt)
    den = lam_re * lam_re + lam_im * lam_im
    nr = ar - 1.0
    fr = (nr * lam_re + ai * lam_im) / den
    fi = (ai * lam_re - nr * lam_im) / den
    bbar_re = fr[:, :, None] * b_re - fi[:, :, None] * b_im
    bbar_im = fr[:, :, None] * b_im + fi[:, :, None] * b_re
    return ar, ai, bbar_re, bbar_im


BD_TILES, BD_CH, BD_ST, BD_GROUPS = 8, 128, 512, 8
BD_ROWS = 4096


def _bd_tiles(re, im):
    eye = jnp.eye(BD_GROUPS, dtype=re.dtype)

    def tiles(t):
        t = t.reshape(S5_G // BD_GROUPS, BD_GROUPS, S5_H, S5_P)
        return (t[:, :, :, None, :] * eye[None, :, None, :, None]).reshape(S5_G // BD_GROUPS, BD_CH, BD_ST)

    return jnp.concatenate([tiles(re), tiles(im)], axis=0)


def _bd_blocks(t):
    t = t.reshape(2, S5_G // BD_GROUPS, BD_GROUPS, S5_H, BD_GROUPS, S5_P)
    return jnp.einsum('rkahap->rkahp', t).reshape(2, S5_G, S5_H, S5_P)


def _bd_reduce(x, t, res, name):
    n = x.shape[0]
    tm = _pick(n, BD_ROWS, 16)

    def body(x_ref, t_ref, r_ref, o_ref):
        part = _dot(x_ref[...], t_ref[...], NT)

        @pl.when(pl.program_id(2) == 0)
        def _():
            o_ref[...] = r_ref[...] + part

        @pl.when(pl.program_id(2) == 1)
        def _():
            o_ref[...] += part

    return _pcall(body, name=name, grid=(n // tm, 4, 2),
                  in_specs=[pl.BlockSpec((tm, BD_ST), lambda i, k, r: (i, k + 4 * r)),
                            pl.BlockSpec((None, BD_CH, BD_ST), lambda i, k, r: (k + 4 * r, 0, 0)),
                            pl.BlockSpec((tm, BD_CH), lambda i, k, r: (i, k))],
                  out_specs=pl.BlockSpec((tm, BD_CH), lambda i, k, r: (i, k)),
                  out_shape=jax.ShapeDtypeStruct((n, S5_W), F32), compiler_params=_params())(x, t, res)


def _bd_outer(a, x, name):
    n = a.shape[0]
    tk = _pick(n, BD_ROWS, 16)
    nk = n // tk

    def body(a_ref, x_ref, o_ref):
        part = _dot(a_ref[...], x_ref[...], TN)

        @pl.when(pl.program_id(1) == 0)
        def _():
            o_ref[...] = part

        @pl.when(pl.program_id(1) > 0)
        def _():
            o_ref[...] += part

    return _pcall(body, name=name, grid=(BD_TILES, nk),
                  in_specs=[pl.BlockSpec((tk, BD_CH), lambda j, kk: (kk, j % 4)), pl.BlockSpec((tk, BD_ST), lambda j, kk: (kk, j))],
                  out_specs=pl.BlockSpec((None, BD_CH, BD_ST), lambda j, kk: (j, 0, 0)),
                  out_shape=jax.ShapeDtypeStruct((BD_TILES, BD_CH, BD_ST), F32), compiler_params=_params())(a, x)


def _permute_rows(t):
    n = t.shape[0]
    return t.reshape(SEG, n // SEG, t.shape[1]).transpose(1, 0, 2).reshape(n, t.shape[1])


def _unpermute_rows(t):
    n = t.shape[0]
    return t.reshape(n // SEG, SEG, t.shape[1]).transpose(1, 0, 2).reshape(n, t.shape[1])


def _segment_powers(ar, ai, seg_steps):
    pr, pi = ar.reshape(1, S5_GP), ai.reshape(1, S5_GP)
    e = 1
    while e < seg_steps:
        pr, pi = pr * pr - pi * pi, 2.0 * pr * pi
        e *= 2
    assert e == seg_steps, "segment length must be a power of two"
    rows_r, rows_i = [], []
    for _ in range(3):
        rows_r.append(pr)
        rows_i.append(pi)
        pr, pi = pr * pr - pi * pi, 2.0 * pr * pi
    pad = jnp.zeros((SEG - 3, S5_GP), F32)
    return jnp.concatenate(rows_r + [pad], axis=0), jnp.concatenate(rows_i + [pad], axis=0)


NT = (((1,), (1,)), ((), ()))
TN = (((0,), (0,)), ((), ()))


def _dot(a, b, dims=None, exact=False):
    dims = (((1,), (0,)), ((), ())) if dims is None else dims
    if exact:
        return lax.dot_general(a, b, dims, precision=HI, preferred_element_type=F32)
    return lax.dot_general(a.astype(BF16), b.astype(BF16), dims, preferred_element_type=F32)


def _dot01(a, b, dims=None, ones_first=True):
    x = b if ones_first else a
    hi = x.astype(BF16)
    lo = (x - hi.astype(F32)).astype(BF16)
    parts = [(_dot(a, p, dims) if ones_first else _dot(p, b, dims)) for p in (lo, hi)]
    return parts[0] + parts[1]


HEADS = range(4)


def _gla_chunk_fwd(qc, kc, vc, al, wup, bup, s_prev, tril):
    ones = jnp.ones((GLA_CHUNK, GLA_DV), F32)
    z = [_dot(al, wup[h]) + bup[h] for h in HEADS]
    la = [(jnp.minimum(z[h], 0.0) - jnp.log(1.0 + jnp.exp(-jnp.abs(z[h])))) * (1.0 / GLA_TAU) for h in HEADS]
    bc = [_dot01(tril, la[h]) for h in HEADS]
    blb = [_dot01(la[h], ones, TN, ones_first=False) for h in HEADS]
    bl = [bc[h][GLA_CHUNK - 1:GLA_CHUNK, :] for h in HEADS]
    ebc = [jnp.exp(bc[h]) for h in HEADS]
    qt = [qc[h] * (GLA_DK ** -0.5) * ebc[h] for h in HEADS]
    kt = [kc[h] * jnp.exp(-bc[h]) for h in HEADS]
    ke = [kc[h] * jnp.exp(bl[h] - bc[h]) for h in HEADS]
    sc = [_dot(qt[h], kt[h], NT) * tril for h in HEADS]
    oi = [_dot(sc[h], vc[h]) for h in HEADS]
    oo = [_dot(qt[h], s_prev[h]) for h in HEADS]
    o = [oi[h] + oo[h] for h in HEADS]
    return z, bc, bl, blb, ebc, qt, kt, ke, sc, o


GLA_ROWS = 512
GLA_CPB = GLA_ROWS // GLA_CHUNK


ZA_COLS = 5 * 512
SLOT = 128


def _pad_heads(w):
    r = w.shape[0]
    return jnp.pad(w.reshape(r, GLA_HEADS, GLA_DK), ((0, 0), (0, 0), (0, SLOT - GLA_DK))).reshape(r, GLA_HEADS * SLOT)


def _unpad_heads(w):
    r = w.shape[0]
    return w.reshape(r, GLA_HEADS, SLOT)[:, :, :GLA_DK].reshape(r, GLA_HEADS * GLA_DK)


def _gla_token_specs(blk):
    col = lambda cb: pl.BlockSpec((GLA_ROWS, 512), lambda j: (blk(j), cb))
    whole = lambda a: pl.BlockSpec(a.shape, lambda j: (0,) * a.ndim)
    return col, whole


def _head_ds(h, width):
    return pl.ds(h * SLOT, width)


def _tri(lower):
    ri = lax.broadcasted_iota(jnp.int32, (GLA_CHUNK, GLA_CHUNK), 0)
    ci = lax.broadcasted_iota(jnp.int32, (GLA_CHUNK, GLA_CHUNK), 1)
    return ((ri >= ci) if lower else (ri <= ci)).astype(F32)


def _gla_fwd(za, al, wup, bup, gn, name):
    n = za.shape[0]
    nc = n // GLA_CHUNK

    def body(q_ref, k_ref, v_ref, r_ref, al_ref, wup_ref, bup_ref, gn_ref, y_ref, sp_ref, s_ref):
        @pl.when(pl.program_id(0) == 0)
        def _():
            s_ref[...] = jnp.zeros_like(s_ref)

        tril = _tri(True)

        def chunk(c, carry):
            rows = pl.ds(pl.multiple_of(c * GLA_CHUNK, GLA_CHUNK), GLA_CHUNK)
            alc = al_ref[rows, :]
            vc = [v_ref[rows, _head_ds(h, GLA_DV)] for h in HEADS]
            s_prev = [s_ref[h] for h in HEADS]
            _, _, _, blb, _, _, _, ke, _, o = _gla_chunk_fwd(
                [q_ref[rows, _head_ds(h, GLA_DK)] for h in HEADS], [k_ref[rows, _head_ds(h, GLA_DK)] for h in HEADS],
                vc, alc, [wup_ref[h] for h in HEADS], [bup_ref[h] for h in HEADS], s_prev, tril)
            ds = [_dot(ke[h], vc[h], TN) for h in HEADS]
            for h in HEADS:
                rc = r_ref[rows, _head_ds(h, GLA_DV)]
                sp_ref[h, c] = s_prev[h]
                rstd = lax.rsqrt(jnp.mean(o[h] * o[h], axis=-1, keepdims=True) + EPS)
                y_ref[rows, _head_ds(h, GLA_DV)] = (o[h] * rstd * gn_ref[h] * (rc * _sigmoid(rc))).astype(BF16)
                s_ref[h] = jnp.exp(blb[h]) * s_prev[h] + ds[h]
            return carry

        lax.fori_loop(0, GLA_CPB, chunk, 0)

    col, whole = _gla_token_specs(lambda j: j)
    return _pcall(body, name=name, grid=(n // GLA_ROWS,),
                  in_specs=[col(1), col(2), col(3), col(4), pl.BlockSpec((GLA_ROWS, LANE), lambda j: (j, 0)),
                            whole(wup), whole(bup), whole(gn)],
                  out_specs=[pl.BlockSpec((GLA_ROWS, GLA_HEADS * GLA_DV), lambda j: (j, 0)),
                             pl.BlockSpec((GLA_HEADS, GLA_CPB, GLA_DK, GLA_DV), lambda j: (0, j, 0, 0))],
                  out_shape=[jax.ShapeDtypeStruct((n, GLA_HEADS * GLA_DV), BF16),
                             jax.ShapeDtypeStruct((GLA_HEADS, nc, GLA_DK, GLA_DV), F32)],
                  scratch_shapes=[pltpu.VMEM((GLA_HEADS, GLA_DK, GLA_DV), F32)],
                  compiler_params=_params())(za, za, za, za, al, wup, bup, gn)


def _gla_bwd(za, al, wup, bup, gn, sp, dy, du_s5, name):
    n = za.shape[0]
    nb = n // GLA_ROWS

    def body(q_ref, k_ref, v_ref, r_ref, al_ref, wup_ref, bup_ref, gn_ref, dy_ref, dus_ref, sp_ref,
             dza_ref, dz_ref, dgn_ref, dbup_ref, ds_ref):
        @pl.when(pl.program_id(0) == 0)
        def _():
            ds_ref[...] = jnp.zeros_like(ds_ref)
            dgn_ref[...] = jnp.zeros_like(dgn_ref)
            dbup_ref[...] = jnp.zeros_like(dbup_ref)

        tril, triu = _tri(True), _tri(False)
        dza_ref[:, 0:512] = dus_ref[...]
        dza_ref[:, 512:1536] = jnp.zeros((GLA_ROWS, 1024), F32)
        dz_ref[...] = jnp.zeros_like(dz_ref)

        def chunk(i, carry):
            c = GLA_CPB - 1 - i
            rows = pl.ds(pl.multiple_of(c * GLA_CHUNK, GLA_CHUNK), GLA_CHUNK)
            alc = al_ref[rows, :]
            qc = [q_ref[rows, _head_ds(h, GLA_DK)] for h in HEADS]
            kc = [k_ref[rows, _head_ds(h, GLA_DK)] for h in HEADS]
            vc = [v_ref[rows, _head_ds(h, GLA_DV)] for h in HEADS]
            s_prev = [sp_ref[h, c] for h in HEADS]
            ds = [ds_ref[h] for h in HEADS]
            z, bc, bl, blb, ebc, qt, kt, ke, sc, o = _gla_chunk_fwd(
                qc, kc, vc, alc, [wup_ref[h] for h in HEADS], [bup_ref[h] for h in HEADS], s_prev, tril)
            do = []
            for h in HEADS:
                rc = r_ref[rows, _head_ds(h, GLA_DV)]
                rs = lax.rsqrt(jnp.mean(o[h] * o[h], axis=-1, keepdims=True) + EPS)
                on = o[h] * rs
                sr = _sigmoid(rc)
                sil = rc * sr
                dyv, gnv = dy_ref[rows, _head_ds(h, GLA_DV)], gn_ref[h]
                dgn_ref[h] += jnp.sum(dyv * on * sil, axis=0, keepdims=True)
                dza_ref[rows, pl.ds(2048 + h * SLOT, GLA_DV)] = dyv * on * gnv * (sr * (1.0 + rc * (1.0 - sr)))
                don = dyv * gnv * sil
                do.append(rs * (don - on * jnp.mean(don * on, axis=-1, keepdims=True)))
            dp = [_dot(do[h], vc[h], NT) * tril for h in HEADS]
            dv1 = [_dot(sc[h], do[h], TN) for h in HEADS]
            dv2 = [_dot(ke[h], ds[h]) for h in HEADS]
            dq2 = [_dot(do[h], s_prev[h], NT) for h in HEADS]
            dke = [_dot(vc[h], ds[h], NT) for h in HEADS]
            ddec = [_dot01(jnp.ones((8, GLA_DV), F32), ds[h] * s_prev[h], NT)[0:1, :] for h in HEADS]
            dsn = [_dot(qt[h], do[h], TN) for h in HEADS]
            dq1 = [_dot(dp[h], kt[h]) for h in HEADS]
            dkt = [_dot(dp[h], qt[h], TN) for h in HEADS]
            dbc, dbl = [], []
            for h in HEADS:
                dqt = dq1[h] + dq2[h]
                dza_ref[rows, pl.ds(1536 + h * SLOT, GLA_DV)] = dv1[h] + dv2[h]
                ds_ref[h] = jnp.exp(blb[h]) * ds[h] + dsn[h]
                dza_ref[rows, pl.ds(512 + h * SLOT, GLA_DK)] = dqt * (GLA_DK ** -0.5) * ebc[h]
                dza_ref[rows, pl.ds(1024 + h * SLOT, GLA_DK)] = dkt[h] * jnp.exp(-bc[h]) + dke[h] * jnp.exp(bl[h] - bc[h])
                dbc.append(dqt * qt[h] - dkt[h] * kt[h] - dke[h] * ke[h])
                dbl.append(jnp.sum(dke[h] * ke[h], axis=0, keepdims=True) + ddec[h] * jnp.exp(bl[h]))
            dla = [_dot01(triu, dbc[h]) + dbl[h] for h in HEADS]
            for h in HEADS:
                dz = dla[h] * (1.0 - _sigmoid(z[h])) * (1.0 / GLA_TAU)
                dz_ref[rows, _head_ds(h, GLA_DK)] = dz
                dbup_ref[h] += jnp.sum(dz, axis=0, keepdims=True)
            return carry

        lax.fori_loop(0, GLA_CPB, chunk, 0)

    rev = lambda j: nb - 1 - j
    col, whole = _gla_token_specs(rev)
    tok = lambda w: pl.BlockSpec((GLA_ROWS, w), lambda j: (rev(j), 0))
    h1 = lambda w: pl.BlockSpec((GLA_HEADS, 1, w), lambda j: (0, 0, 0))
    s1 = lambda w: jax.ShapeDtypeStruct((GLA_HEADS, 1, w), F32)
    return _pcall(body, name=name, grid=(nb,),
                  in_specs=[col(1), col(2), col(3), col(4), tok(LANE), whole(wup), whole(bup), whole(gn), tok(512), tok(512),
                            pl.BlockSpec((GLA_HEADS, GLA_CPB, GLA_DK, GLA_DV), lambda j: (0, rev(j), 0, 0))],
                  out_specs=[tok(ZA_COLS), tok(GLA_HEADS * SLOT), h1(GLA_DV), h1(GLA_DK)],
                  out_shape=[jax.ShapeDtypeStruct((n, ZA_COLS), F32), jax.ShapeDtypeStruct((n, GLA_HEADS * SLOT), F32),
                             s1(GLA_DV), s1(GLA_DK)],
                  scratch_shapes=[pltpu.VMEM((GLA_HEADS, GLA_DK, GLA_DV), F32)],
                  compiler_params=_params())(za, za, za, za, al, wup, bup, gn, dy, du_s5, sp)


ANY = pl.BlockSpec(memory_space=pl.ANY)


def _place():
    x, y, c = lax.axis_index("x"), lax.axis_index("y"), lax.axis_index("c")
    chips = [(1 - x, y), (x, 1 - y), (1 - x, 1 - y)]
    return x, y, c, chips


def _remote(src, dst, ssem, rsem, dev):
    return pltpu.make_async_remote_copy(src_ref=src, dst_ref=dst, send_sem=ssem, recv_sem=rsem, device_id=dev,
                                        device_id_type=MESH_ID)


def _half(c, rows):
    h = rows // 2
    return pl.ds(pl.multiple_of(c * h, 8), h)


def _side_gather_ici(shards):
    def copies(ins, outs, ssem, rsem):
        x, y, c, chips = _place()
        mine = 2 * x + y
        cps = []
        for w in range(len(ins)):
            half = _half(c, ins[w].shape[0])
            cps.append(_remote(ins[w], outs[w].at[mine], ssem.at[4 * w], rsem.at[4 * w], (x, y, 1 - c)))
            for k, (px, py) in enumerate(chips):
                cps.append(_remote(ins[w].at[half], outs[w].at[mine, half], ssem.at[4 * w + 1 + k], rsem.at[4 * w + 1 + k],
                                   (px, py, c)))
        return cps

    return _Side(shards, [jax.ShapeDtypeStruct((4,) + s.shape, s.dtype) for s in shards], 4 * len(shards), copies)


def _side_gather_d2d(gathered):
    def copies(ins, outs, ssem, rsem):
        x, y, c, chips = _place()
        cps = []
        for w in range(len(outs)):
            half = _half(c, outs[w].shape[1])
            for k, (px, py) in enumerate(chips):
                theirs = outs[w].at[2 * px + py, half]
                cps.append(_remote(theirs, theirs, ssem.at[3 * w + k], rsem.at[3 * w + k], (x, y, 1 - c)))
        return cps

    return _Side(gathered, [jax.ShapeDtypeStruct(g.shape, g.dtype) for g in gathered], 3 * len(gathered), copies,
                 aliased=True)


def _side_swap_halves(grads):
    def copies(ins, outs, ssem, rsem):
        x, y, c, _ = _place()
        return [_remote(ins[w].at[:, _half(1 - c, ins[w].shape[1]), :], outs[w], ssem.at[w], rsem.at[w], (x, y, 1 - c))
                for w in range(len(ins))]

    return _Side(grads, [jax.ShapeDtypeStruct((4, g.shape[1] // 2, g.shape[2]), g.dtype) for g in grads], len(grads), copies)


def _side_scatter(sums):
    def copies(ins, outs, ssem, rsem):
        x, y, c, chips = _place()
        return [_remote(ins[w].at[2 * px + py], outs[w].at[k], ssem.at[3 * w + k], rsem.at[3 * w + k], (px, py, c))
                for w in range(len(ins)) for k, (px, py) in enumerate(chips)]

    return _Side(sums, [jax.ShapeDtypeStruct((3,) + s.shape[1:], s.dtype) for s in sums], 3 * len(sums), copies)


def _side_swap_reduced(halves):
    def copies(ins, outs, ssem, rsem):
        x, y, c, _ = _place()
        return [_remote(ins[w], outs[w], ssem.at[w], rsem.at[w], (x, y, 1 - c)) for w in range(len(ins))]

    return _Side(halves, [jax.ShapeDtypeStruct(h.shape, h.dtype) for h in halves], len(halves), copies)


def _chip_sum(g, recv, c_arr, name):
    _, r, cols = g.shape
    h = r // 2
    tr = _pick(h, 512, 16)
    g4 = g.reshape(4, 2, h, cols)

    def body(c_ref, g_ref, r_ref, o_ref):
        o_ref[...] = (g_ref[...] + r_ref[...]).astype(BF16)

    grid_spec = pltpu.PrefetchScalarGridSpec(
        num_scalar_prefetch=1, grid=(4, h // tr),
        in_specs=[pl.BlockSpec((None, None, tr, cols), lambda s, i, c_ref: (s, c_ref[0], i, 0)),
                  pl.BlockSpec((None, tr, cols), lambda s, i, c_ref: (s, i, 0))],
        out_specs=pl.BlockSpec((None, tr, cols), lambda s, i, c_ref: (s, i, 0)))
    return _pcall(body, name=name, grid_spec=grid_spec, out_shape=jax.ShapeDtypeStruct((4, h, cols), BF16),
                  compiler_params=_params())(c_arr, g4, recv)


def _owner_sum(sums, others, s_arr, name):
    _, h, cols = sums.shape
    tr = _pick(h, 512, 16)

    def body(s_ref, a_ref, o_ref, out_ref):
        f = lambda v: v.astype(F32)
        out_ref[...] = (f(a_ref[...]) + f(o_ref[0])) + (f(o_ref[1]) + f(o_ref[2]))

    grid_spec = pltpu.PrefetchScalarGridSpec(
        num_scalar_prefetch=1, grid=(h // tr,),
        in_specs=[pl.BlockSpec((None, tr, cols), lambda i, s_ref: (s_ref[0], i, 0)),
                  pl.BlockSpec((3, tr, cols), lambda i, s_ref: (0, i, 0))],
        out_specs=pl.BlockSpec((tr, cols), lambda i, s_ref: (i, 0)))
    return _pcall(body, name=name, grid_spec=grid_spec, out_shape=jax.ShapeDtypeStruct((h, cols), F32),
                  compiler_params=_params())(s_arr, sums, others)


def _side_small_sibling(v):
    def copies(ins, outs, ssem, rsem):
        x, y, c, _ = _place()
        return [_remote(ins[0], outs[0], ssem.at[0], rsem.at[0], (x, y, 1 - c))]

    return _Side([v], [jax.ShapeDtypeStruct(v.shape, F32)], 1, copies)


def _side_small_chips(v):
    def copies(ins, outs, ssem, rsem):
        x, y, c, chips = _place()
        return [_remote(ins[0], outs[0].at[k], ssem.at[k], rsem.at[k], (px, py, c)) for k, (px, py) in enumerate(chips)]

    return _Side([v], [jax.ShapeDtypeStruct((3,) + v.shape, F32)], 3, copies)


def _small_add(v, r, name):
    def body(v_ref, r_ref, o_ref):
        if r.ndim == 2:
            o_ref[...] = v_ref[...] + r_ref[...]
        else:
            o_ref[...] = (v_ref[...] + r_ref[0]) + (r_ref[1] + r_ref[2])

    vm = pl.BlockSpec(memory_space=pltpu.VMEM)
    return _pcall(body, name=name, in_specs=[vm, vm], out_specs=vm, out_shape=jax.ShapeDtypeStruct(v.shape, F32),
                  compiler_params=_params())(v, r)


def _merge_sides(sides):
    if len(sides) == 1:
        return sides[0]

    def copies(in_refs, out_refs, ssem, rsem):
        cps, i, o, q = [], 0, 0, 0
        for s in sides:
            ni, no = len(s.ins), len(s.out_shapes)
            cps += s.copies(in_refs[i:i + ni], out_refs[o:o + no], ssem.at[pl.ds(q, s.nsem)], rsem.at[pl.ds(q, s.nsem)])
            i, o, q = i + ni, o + no, q + s.nsem
        return cps

    assert not any(s.aliased for s in sides)
    return _Side(sum((s.ins for s in sides), []), sum((s.out_shapes for s in sides), []), sum(s.nsem for s in sides), copies)


def _tile_rows(size):
    return -(-size // (8 * LANE)) * 8


def _pack_small(parts):
    pieces = []
    for p in parts:
        flat = p.reshape(-1).astype(F32)
        pieces.append(jnp.pad(flat, (0, _tile_rows(p.size) * LANE - p.size)).reshape(-1, LANE))
    rows = sum(x.shape[0] for x in pieces)
    pieces.append(jnp.zeros(((-rows) % 64, LANE), F32))
    return jnp.concatenate(pieces, axis=0)


def _unpack_small(packed, like):
    out, pos = [], 0
    for p in like:
        rows = _tile_rows(p.size)
        out.append(packed[pos:pos + rows].reshape(-1)[:p.size].reshape(p.shape))
        pos += rows
    return out


FFN_FWD_ROWS, FFN_BWD_ROWS = 1024, 512
FFN_SUB_ROWS = 256


def _ffn_specs(n, d, fs, cap):
    rows = _pick(n, cap, 16)
    row = pl.BlockSpec((rows, d), lambda i, s: (i, 0))
    gain = pl.BlockSpec((1, d), lambda i, s: (0, 0))
    w_row = pl.BlockSpec((None, fs, d), lambda i, s: (s, 0, 0))
    hid = pl.BlockSpec((None, rows, fs), lambda i, s: (s, i, 0))
    return rows, row, gain, w_row, hid


def _ffn_fwd(h, g, w1t, w3t, w2, tag, plan):
    n, d = h.shape
    ns, fs, _ = w2.shape
    rows, row, gain, w_row, hid = _ffn_specs(n, d, fs, FFN_FWD_ROWS)
    sub = rows

    def body(h_ref, g_ref, w1_ref, w3_ref, w2_ref, out_ref, n1_ref, a_ref, b_ref, hm_ref, acc_ref):
        s = pl.program_id(1)

        @pl.when(s == 0)
        def _():
            xv = h_ref[...]
            rstd = lax.rsqrt(jnp.mean(xv * xv, axis=-1, keepdims=True) + EPS)
            n1_ref[...] = (xv * rstd * g_ref[...]).astype(BF16)
            acc_ref[...] = jnp.zeros_like(acc_ref)

        def up(j):
            n1 = n1_ref[j * sub:(j + 1) * sub, :]
            return _dot(n1, w1_ref[...], NT), _dot(n1, w3_ref[...], NT)

        cur = up(0)
        for j in range(rows // sub):
            nxt = up(j + 1) if (j + 1) * sub < rows else None
            a, b = cur
            r = slice(j * sub, (j + 1) * sub)
            hm = (a * _sigmoid(a) * b).astype(BF16)
            a_ref[r, :] = a.astype(BF16)
            b_ref[r, :] = b.astype(BF16)
            hm_ref[r, :] = hm
            acc_ref[r, :] += _dot(hm, w2_ref[...])
            cur = nxt

        @pl.when(s == ns - 1)
        def _():
            out_ref[...] = h_ref[...] + 0.5 * acc_ref[...]

    hid_shape = jax.ShapeDtypeStruct((ns, n, fs), BF16)
    plan.before(f"{tag}_fwd")
    out, n1, a, b, hm = _pcall(
        body, name=f"{tag}_fwd", grid=(n // rows, ns), in_specs=[row, gain, w_row, w_row, w_row],
        out_specs=[row, row, hid, hid, hid],
        out_shape=[jax.ShapeDtypeStruct((n, d), F32), jax.ShapeDtypeStruct((n, d), BF16), hid_shape, hid_shape, hid_shape],
        scratch_shapes=[pltpu.VMEM((rows, d), F32)], compiler_params=_params())(h, g, w1t, w3t, w2)
    plan.after(f"{tag}_fwd")
    return out, (h, n1, a, b, hm)


def _wgrad(a3, b, name, alpha=1.0):
    ns, n, fs = a3.shape
    d = b.shape[1]
    tk = _pick(n, 1024, 16)

    def body(a_ref, b_ref, o_ref):
        @pl.when(pl.program_id(0) == 0)
        def _():
            o_ref[...] = jnp.zeros_like(o_ref)

        bv = b_ref[...].astype(BF16)
        for s in range(ns):
            part = _dot(a_ref[s], bv, TN)
            o_ref[s] += part if alpha == 1.0 else alpha * part

    return _pcall(body, name=name, grid=(n // tk,),
                  in_specs=[pl.BlockSpec((ns, tk, fs), lambda k: (0, k, 0)), pl.BlockSpec((tk, d), lambda k: (k, 0))],
                  out_specs=pl.BlockSpec((ns, fs, d), lambda k: (0, 0, 0)),
                  out_shape=jax.ShapeDtypeStruct((ns, fs, d), F32), compiler_params=_params())(a3, b)


def _ffn_bwd(dout, saved, g, w1, w3, w2, tag, plan):
    h, n1, a, b, hm = saved
    n, d = h.shape
    ns, fs, _ = w2.shape
    rows, row, gain, w_row, hid = _ffn_specs(n, d, fs, FFN_BWD_ROWS)
    sub = _pick(rows, FFN_SUB_ROWS, 16)

    def body(do_ref, h_ref, g_ref, a_ref, b_ref, w1_ref, w3_ref, w2_ref, dh_ref, da_ref, db_ref, dg_ref, acc_ref):
        i, s = pl.program_id(0), pl.program_id(1)

        @pl.when(s == 0)
        def _():
            acc_ref[...] = jnp.zeros_like(acc_ref)

        @pl.when((s == 0) & (i == 0))
        def _():
            dg_ref[...] = jnp.zeros_like(dg_ref)

        def up(j):
            return _dot(0.5 * do_ref[j * sub:(j + 1) * sub, :], w2_ref[...], NT)

        cur = up(0)
        for j in range(rows // sub):
            nxt = up(j + 1) if (j + 1) * sub < rows else None
            r = slice(j * sub, (j + 1) * sub)
            av, bv = a_ref[r, :].astype(F32), b_ref[r, :].astype(F32)
            sg = _sigmoid(av)
            da = (cur * bv * (sg * (1.0 + av * (1.0 - sg)))).astype(BF16)
            db = (cur * av * sg).astype(BF16)
            da_ref[r, :] = da
            db_ref[r, :] = db
            acc_ref[r, :] += _dot(da, w1_ref[...]) + _dot(db, w3_ref[...])
            cur = nxt

        @pl.when(s == ns - 1)
        def _():
            xv, dn = h_ref[...], acc_ref[...]
            rstd = lax.rsqrt(jnp.mean(xv * xv, axis=-1, keepdims=True) + EPS)
            xh = xv * rstd
            dg_ref[...] += jnp.sum(dn * xh, axis=0, keepdims=True)
            dxh = dn * g_ref[...]
            dh_ref[...] = do_ref[...] + rstd * (dxh - xh * jnp.mean(dxh * xh, axis=-1, keepdims=True))

    hid_shape = jax.ShapeDtypeStruct((ns, n, fs), BF16)
    plan.before(f"{tag}_bwd")
    dh, da, db, dg = _pcall(
        body, name=f"{tag}_bwd", grid=(n // rows, ns), in_specs=[row, row, gain, hid, hid, w_row, w_row, w_row],
        out_specs=[row, hid, hid, gain],
        out_shape=[jax.ShapeDtypeStruct((n, d), F32), hid_shape, hid_shape, jax.ShapeDtypeStruct((1, d), F32)],
        scratch_shapes=[pltpu.VMEM((rows, d), F32)], compiler_params=_params())(dout, h, g, a, b, w1, w3, w2)
    plan.after(f"{tag}_bwd")
    plan.grads[f"{tag}_norm"] = dg
    plan.before(f"{tag}_gw2")
    gw2 = _wgrad(hm, dout, f"{tag}_gw2", alpha=0.5)
    plan.after(f"{tag}_gw2")
    plan.grads[f"{tag}_w2"] = gw2
    plan.before(f"{tag}_gw1")
    gw1 = _wgrad(da, n1, f"{tag}_gw1")
    plan.after(f"{tag}_gw1")
    plan.grads[f"{tag}_w1"] = gw1
    plan.before(f"{tag}_gw3")
    gw3 = _wgrad(db, n1, f"{tag}_gw3")
    plan.after(f"{tag}_gw3")
    return dh, dg, gw1, gw3, gw2


def _local_step(x, tgt, plan):
    n = x.shape[0]
    grads = plan.grads

    def f(name):
        w = plan.get(name)
        return w.reshape(1, D_MODEL) if name.endswith('_norm') and name != 'gla_out_norm' else w

    def carried(tag, fn, *args, **kw):
        plan.before(tag)
        out = fn(*args, **kw)
        plan.after(tag)
        return out

    h1, ffn1 = _ffn_fwd(x, f('ffn1_norm'), f('ffn1_w1'), f('ffn1_w3'), f('ffn1_w2'), "ffn1", plan)
    u = carried("mix_rms", _rms_fwd, h1, f('mix_norm'), "mix_rms")
    w_in = f('w_in')
    w_a = jnp.concatenate([w_in[:, :512], _pad_heads(w_in[:, 512:768]), _pad_heads(w_in[:, 768:1024]), w_in[:, 1024:2048]],
                          axis=1)
    w_al = jnp.pad(w_in[:, 2048:2048 + GLA_RANK], ((0, 0), (0, LANE - GLA_RANK)))
    w_g = w_in[:, 2048 + GLA_RANK:]
    za = carried("in_a", _mm, u, w_a, name="in_a")
    zg = carried("in_g", _mm, u, w_g, name="in_g")
    al = _mm(u, w_al, name="in_al")
    ar, ai, bbar_re, bbar_im = _s5_discretize(f('s5_lambda_re'), f('s5_lambda_im'), f('s5_log_dt'), f('s5_b_re'), f('s5_b_im'))
    t_b = _bd_tiles(bbar_re.transpose(0, 2, 1), bbar_im.transpose(0, 2, 1)).astype(BF16)
    t_c = _bd_tiles(f('s5_c_re'), -f('s5_c_im')).astype(BF16)
    ar8 = jnp.broadcast_to(ar.reshape(1, S5_GP), (SEG, S5_GP))
    ai8 = jnp.broadcast_to(ai.reshape(1, S5_GP), (SEG, S5_GP))
    pw_r, pw_i = _segment_powers(ar, ai, n // SEG)
    dskip = f('s5_d').reshape(1, S5_W)
    u_s5 = _permute_rows(za[:, :S5_W])
    xs = _s5_scan(u_s5, t_b, ar8, ai8, pw_r, pw_i, "s5_scan")
    ys_p = _bd_reduce(xs, t_c, _scale_rows(u_s5, dskip, "s5_skip"), "s5_y")
    ys = _unpermute_rows(ys_p)
    zgelu = _gelu_fwd(ys, "s5_gelu")
    t_glu = _mm(zgelu, f('s5_glu_w'), bias=f('s5_glu_b').reshape(1, S5_W), name="s5_glu_t")
    y_s5 = _glu_fwd(zgelu, t_glu, "s5_glu")
    wup = jnp.pad(f('gla_a_up_w'), ((0, LANE - GLA_RANK), (0, 0)))
    wup_h = wup.reshape(LANE, GLA_HEADS, GLA_DK).transpose(1, 0, 2)
    bup_h = f('gla_a_up_b').reshape(GLA_HEADS, 1, GLA_DK)
    gn_h = f('gla_out_norm').reshape(GLA_HEADS, 1, GLA_DV)
    y_gla, s_prev = carried("gla_fwd", _gla_fwd, za, al, wup_h, bup_h, gn_h, "gla_fwd")
    ps = _mm(y_s5, f('proj_s5'), name="proj_s5")
    pg = carried("proj_gla", _mm, y_gla, f('proj_gla'), name="proj_gla")
    merged = _merge_fwd(zg, ps, pg, "merge")
    h2 = _mm(merged, f('w_out'), res=h1, name="w_out")
    h3, ffn2 = _ffn_fwd(h2, f('ffn2_norm'), f('ffn2_w1'), f('ffn2_w3'), f('ffn2_w2'), "ffn2", plan)
    loss, dh3, g_final = _final_loss(h3, f('final_norm').reshape(1, D_MODEL), tgt, "loss")
    plan.loss = loss[0, 0]
    grads['final_norm'] = g_final.reshape(D_MODEL)
    dh2, grads['ffn2_norm'], grads['ffn2_w1'], grads['ffn2_w3'], grads['ffn2_w2'] = _ffn_bwd(
        dh3, ffn2, f('ffn2_norm'), f('ffn2_w1'), f('ffn2_w3'), f('ffn2_w2'), "ffn2", plan)
    dm = carried("d_merged", _mm, dh2, f('w_out'), tb=True, name="d_merged")
    grads['w_out'] = _mm(merged, dh2, ta=True, name="g_w_out")
    dps, dpg, dzg = carried("d_merge", _merge_bwd, dm, zg, ps, pg, "d_merge")
    grads['proj_s5'] = _mm(y_s5, dps, ta=True, name="g_proj_s5")
    grads['proj_gla'] = _mm(y_gla, dpg, ta=True, name="g_proj_gla")
    dy_s5 = _mm(dps, f('proj_s5'), tb=True, name="d_y_s5")
    dy_gla = _mm(dpg, f('proj_gla'), tb=True, name="d_y_gla")
    dzgelu, dt_glu, g_glu_b = _glu_bwd1(dy_s5, zgelu, t_glu, "d_glu")
    grads['s5_glu_b'] = g_glu_b.reshape(S5_W)
    grads['s5_glu_w'] = _mm(zgelu, dt_glu, ta=True, name="g_glu_w")
    dzgelu = _mm(dt_glu, f('s5_glu_w'), tb=True, res=dzgelu, name="d_gelu")
    dys, du_skip, g_d = _glu_bwd2(_permute_rows(dzgelu), ys_p, u_s5, dskip, "d_s5_y")
    grads['s5_d'] = g_d.reshape(S5_G, S5_H)
    lam, da8 = _s5_scan_bwd(dys, t_c, xs, ar8, ai8, pw_r, pw_i, "s5_scan_bwd")
    g_c = _bd_blocks(_bd_outer(dys, xs, "g_s5_c"))
    grads['s5_c_re'], grads['s5_c_im'] = g_c[0], -g_c[1]
    g_b = _bd_blocks(_bd_outer(u_s5, lam, "g_s5_b")).transpose(0, 1, 3, 2)
    g_bbar_re, g_bbar_im = g_b[0], g_b[1]
    da = jnp.sum(da8, axis=0)
    g_ar, g_ai = da[:S5_GP].reshape(S5_G, S5_P), da[S5_GP:].reshape(S5_G, S5_P)
    _, disc_vjp = jax.vjp(_s5_discretize, f('s5_lambda_re'), f('s5_lambda_im'), f('s5_log_dt'), f('s5_b_re'), f('s5_b_im'))
    (grads['s5_lambda_re'], grads['s5_lambda_im'], grads['s5_log_dt'], grads['s5_b_re'],
     grads['s5_b_im']) = disc_vjp((g_ar, g_ai, g_bbar_re, g_bbar_im))
    du_s5 = _unpermute_rows(_bd_reduce(lam, t_b, du_skip, "d_s5_u"))
    dza, dz, dgn, dbup = carried("gla_bwd", _gla_bwd, za, al, wup_h, bup_h, gn_h, s_prev, dy_gla, du_s5, "gla_bwd")
    grads['gla_out_norm'] = dgn.reshape(GLA_HEADS * GLA_DV)
    grads['gla_a_up_b'] = dbup.reshape(GLA_HEADS * GLA_DK)
    grads['gla_a_up_w'] = _unpad_heads(_mm(al, dz, ta=True, name="g_a_up")[:GLA_RANK])
    dal = _mm(dz, _pad_heads(wup), tb=True, name="d_a_low")
    g_wa = _mm(u, dza, ta=True, name="g_in_a")
    g_wg = _mm(u, dzg, ta=True, name="g_in_g")
    g_wal = _mm(u, dal, ta=True, name="g_in_al")
    grads['w_in'] = jnp.concatenate([g_wa[:, :512], _unpad_heads(g_wa[:, 512:1024]), _unpad_heads(g_wa[:, 1024:1536]),
                                     g_wa[:, 1536:], g_wal[:, :GLA_RANK], g_wg], axis=1)
    du = carried("d_u_a", _mm, dza, w_a, tb=True, name="d_u_a")
    du = _mm(dzg, w_g, tb=True, res=du, name="d_u_g")
    du = _mm(dal, w_al, tb=True, res=du, name="d_u_al")
    dh1, g_mix = carried("d_mix_rms", _rms_bwd, h1, f('mix_norm'), du, dh2, "d_mix_rms")
    grads['mix_norm'] = g_mix
    dx, grads['ffn1_norm'], grads['ffn1_w1'], grads['ffn1_w3'], grads['ffn1_w2'] = _ffn_bwd(
        dh1, ffn1, f('ffn1_norm'), f('ffn1_w1'), f('ffn1_w3'), f('ffn1_w2'), "ffn1", plan)
    return loss[0, 0], dx


MIXER_WEIGHTS = ['w_in', 's5_glu_w', 'proj_s5', 'proj_gla', 'w_out', 'gla_a_up_w']
FFN1_WEIGHTS, FFN2_WEIGHTS = FFN_WEIGHTS[:3], FFN_WEIGHTS[3:]
TRANSPOSED = ['ffn1_w1', 'ffn1_w3', 'ffn2_w1', 'ffn2_w3']


def _local_shard(w, nm):
    return jnp.swapaxes(w, 1, 2)[0] if nm in TRANSPOSED else w[0]
FFN1_EARLY = ['ffn1_w2']
FFN1_LATE = ['ffn1_w1', 'ffn1_w3']
GRAD_GROUPS = {'ffn2': FFN2_WEIGHTS, 'mixer': ['w_out', 'proj_s5', 'proj_gla', 's5_glu_w', 'w_in'], 'ffn1': FFN1_WEIGHTS}


class _Plan:
    def __init__(self, a, c_arr, s_arr):
        self.a, self.c_arr, self.s_arr = a, c_arr, s_arr
        self.grads, self.weights, self.riding = {}, {}, {}
        self.g4s, self.chip_sums, self.halves, self.sib_halves = {}, {}, {}, {}
        for nm in SMALL:
            if nm != 'gla_a_up_w':
                self.weights[nm] = a[nm] if nm == 'final_norm' else a[nm][0]
        ici = _side_gather_ici(self._shards(FFN1_WEIGHTS))
        _run_side(ici, "gather_ffn1_ici")
        self._gathered(FFN1_WEIGHTS, _run_side(_side_gather_d2d(ici.outs), "gather_ffn1_d2d"))

    def _shards(self, names):
        return [_local_shard(self.a[nm], nm).astype(F32 if nm == 'gla_a_up_w' else BF16) for nm in names]

    def _gathered(self, names, arrs):
        for nm, g4 in zip(names, arrs):
            if nm in FFN_WEIGHTS:
                self.weights[nm] = g4
            elif nm in COL_SHARDED:
                self.weights[nm] = jnp.concatenate([g4[s] for s in range(4)], axis=1)
            else:
                self.weights[nm] = g4.reshape(4 * g4.shape[1], g4.shape[2])

    def get(self, name):
        return self.weights[name]

    def _shard_major(self, nm):
        g = self.grads[nm]
        if nm in FFN_WEIGHTS:
            return g
        if nm in COL_SHARDED:
            return jnp.stack(jnp.split(g, 4, axis=1))
        return g.reshape(4, g.shape[0] // 4, g.shape[1])

    def _schedule(self, tag):
        grp = GRAD_GROUPS
        gathers = {"ffn1_fwd": ('ici', MIXER_WEIGHTS), "mix_rms": ('d2d', MIXER_WEIGHTS),
                   "in_a": ('ici', FFN2_WEIGHTS[:1]), "in_g": ('d2d', FFN2_WEIGHTS[:1]),
                   "gla_fwd": ('ici', FFN2_WEIGHTS[1:]), "proj_gla": ('d2d', FFN2_WEIGHTS[1:])}
        if tag in gathers:
            kind, names = gathers[tag]
            key = tuple(names)
            if kind == 'ici':
                return [(_side_gather_ici(self._shards(names)), lambda outs: self.riding.update({key: outs}))]
            return [(_side_gather_d2d(self.riding[key]), lambda outs: self._gathered(names, outs))]
        steps = {"ffn2_gw1": (['ffn2_w2'], 0), "ffn2_gw3": (['ffn2_w1'], 0), "d_merged": (['ffn2_w3'], 0),
                 "gla_bwd": (grp['ffn2'], 1), "d_mix_rms": (grp['ffn2'], 2),
                 "d_u_a": (grp['mixer'], 0), "ffn1_bwd": (grp['mixer'], 1), "ffn1_gw2": (grp['mixer'], 2),
                 "ffn1_gw1": (FFN1_EARLY, 0), "ffn1_gw3": (FFN1_EARLY, 1), "adamw_early": (FFN1_LATE, 1)}
        entries = [self._reduce_stage(*steps[tag])] if tag in steps else []
        if tag == "ffn1_gw2":
            entries.append(self._small_stage(0))
        if tag == "ffn1_gw1":
            entries.append(self._small_stage(1))
        return entries

    def _small_stage(self, stage):
        if stage == 0:
            a, grads = self.a, self.grads
            self.small_parts = ([grads[nm].reshape(a[nm].shape) for nm in SMALL if nm != 'gla_a_up_w']
                                + [grads['gla_a_up_w'], self.loss.reshape(1)])
            packed = _pack_small(self.small_parts)

            def done(outs):
                self.small_pair = _small_add(packed, outs[0], "small_sum_pair")
            return _side_small_sibling(packed), done

        def done(outs):
            self.small_total = _small_add(self.small_pair, outs[0], "small_sum_chips")
        return _side_small_chips(self.small_pair), done

    def _reduce_stage(self, names, stage):
        if stage == 0:
            for nm in names:
                self.g4s[nm] = self._shard_major(nm)

            def done(outs):
                for nm, r in zip(names, outs):
                    self.chip_sums[nm] = _chip_sum(self.g4s[nm], r, self.c_arr, f"chip_sum_{nm}")
            return _side_swap_halves([self.g4s[nm] for nm in names]), done
        if stage == 1:
            def done(outs):
                for nm, o in zip(names, outs):
                    self.halves[nm] = _owner_sum(self.chip_sums[nm], o, self.s_arr, f"owner_sum_{nm}")
            return _side_scatter([self.chip_sums[nm] for nm in names]), done

        def done(outs):
            self.sib_halves.update(zip(names, outs))
        return _side_swap_reduced([self.halves[nm] for nm in names]), done

    def before(self, tag):
        entries = self._schedule(tag)
        if entries:
            merged = _merge_sides([side for side, _ in entries])
            self.riding[tag] = (merged, entries)
            _RIDER.append(merged)

    def after(self, tag):
        if tag in self.riding:
            merged, entries = self.riding.pop(tag)
            assert not _RIDER and merged.outs is not None, tag
            pos = 0
            for side, done in entries:
                done(merged.outs[pos:pos + len(side.out_shapes)])
                pos += len(side.out_shapes)

    def finish_alone(self, stage):
        names = FFN1_LATE if stage == 0 else GRAD_GROUPS['ffn1']
        side, done = self._reduce_stage(names, stage)
        done(_run_side(side, f"grad_ffn1_stage{stage}"))


def _train_step(a):
    x = a['x'][0]
    tgt = a['loss_target'][0]
    xi, yi, ci = lax.axis_index("x"), lax.axis_index("y"), lax.axis_index("c")
    c_arr = jnp.reshape(ci, (1,)).astype(jnp.int32)
    s_arr = jnp.reshape(2 * xi + yi, (1,)).astype(jnp.int32)
    plan = _Plan(a, c_arr, s_arr)
    loss, dx = _local_step(x, tgt, plan)
    red = {}
    small_sum = _unpack_small(plan.small_total, plan.small_parts)
    small_names = [nm for nm in SMALL if nm != 'gla_a_up_w']
    for nm, g in zip(small_names, small_sum[:-2]):
        red[nm] = g
    loss = small_sum[-1].reshape(())
    g_up = small_sum[-2]
    red['gla_a_up_w'] = lax.dynamic_slice(g_up, (0, (2 * xi + yi) * GLA_DK), (GLA_RANK, GLA_DK))
    out_g, out_d, out_m, out_v = {}, {}, {}, {}

    def update(names, tag):
        items = [(_local_shard(a[nm], nm), plan.halves[nm], plan.sib_halves[nm], _local_shard(a['m_' + nm], nm),
                  _local_shard(a['v_' + nm], nm)) for nm in names]
        plan.before(tag)
        res = _adamw_group(items, c_arr, tag)
        plan.after(tag)
        for k, nm in enumerate(names):
            back = (lambda t: jnp.swapaxes(t[None], 1, 2)) if nm in TRANSPOSED else (lambda t: t[None])
            out_g[nm], out_d[nm], out_m[nm], out_v[nm] = (back(t) for t in res[4 * k:4 * k + 4])

    plan.finish_alone(0)
    update([nm for nm in SHARDED if nm not in GRAD_GROUPS['ffn1']], "adamw_early")
    plan.finish_alone(2)
    update(GRAD_GROUPS['ffn1'], "adamw_ffn1")
    rest = [nm for nm in WEIGHTS if nm not in SHARDED]
    pk = lambda pre: _pack_small([a[pre + nm] for nm in rest])
    d, nm_, nv_ = _adamw(pk(''), _pack_small([red[nm] for nm in rest]), pk('m_'), pk('v_'), "adamw_small")
    like = [a[nm] for nm in rest]
    for nm, g, dd, mm_, vv_ in zip(rest, [red[nm].reshape(a[nm].shape) for nm in rest], _unpack_small(d, like),
                                   _unpack_small(nm_, like), _unpack_small(nv_, like)):
        out_g[nm], out_d[nm], out_m[nm], out_v[nm] = g, dd, mm_, vv_
    return (loss, dx[None], *[out_g[nm] for nm in WEIGHTS], *[out_d[nm] for nm in WEIGHTS],
            *[out_m[nm] for nm in WEIGHTS], *[out_v[nm] for nm in WEIGHTS])


def kernel(x, ffn1_norm, ffn1_w1, ffn1_w3, ffn1_w2, mix_norm, w_in, s5_lambda_re, s5_lambda_im, s5_log_dt, s5_b_re, s5_b_im, s5_c_re, s5_c_im, s5_d, s5_glu_w, s5_glu_b, gla_a_up_w, gla_a_up_b, gla_out_norm, proj_s5, proj_gla, w_out, ffn2_norm, ffn2_w1, ffn2_w3, ffn2_w2, final_norm, loss_target, m_ffn1_norm, m_ffn1_w1, m_ffn1_w3, m_ffn1_w2, m_mix_norm, m_w_in, m_s5_lambda_re, m_s5_lambda_im, m_s5_log_dt, m_s5_b_re, m_s5_b_im, m_s5_c_re, m_s5_c_im, m_s5_d, m_s5_glu_w, m_s5_glu_b, m_gla_a_up_w, m_gla_a_up_b, m_gla_out_norm, m_proj_s5, m_proj_gla, m_w_out, m_ffn2_norm, m_ffn2_w1, m_ffn2_w3, m_ffn2_w2, m_final_norm, v_ffn1_norm, v_ffn1_w1, v_ffn1_w3, v_ffn1_w2, v_mix_norm, v_w_in, v_s5_lambda_re, v_s5_lambda_im, v_s5_log_dt, v_s5_b_re, v_s5_b_im, v_s5_c_re, v_s5_c_im, v_s5_d, v_s5_glu_w, v_s5_glu_b, v_gla_a_up_w, v_gla_a_up_b, v_gla_out_norm, v_proj_s5, v_proj_gla, v_w_out, v_ffn2_norm, v_ffn2_w1, v_ffn2_w3, v_ffn2_w2, v_final_norm):
    return _train_step(dict(locals()))
```

```python
import functools

import jax
import jax.numpy as jnp
from jax import lax
from jax.experimental import pallas as pl
from jax.experimental.pallas import tpu as pltpu

F32 = jnp.float32
BF16 = jnp.bfloat16
HI = lax.Precision.HIGHEST
MESH_ID = pl.DeviceIdType.MESH

D_MODEL = 1024
EPS = 1e-6
S5_G, S5_P, S5_H = 32, 64, 16
S5_W = S5_G * S5_H
S5_GP = S5_G * S5_P
SEG = 8
SCAN_ROWS = 256
GLA_HEADS, GLA_DK, GLA_DV = 4, 64, 128
GLA_CHUNK = 64
GLA_TAU = 16.0
GLA_RANK = 16
ADAM_LR, ADAM_B1, ADAM_B2, ADAM_EPS, ADAM_WD, ADAM_STEP = 0.001, 0.9, 0.999, 1e-08, 0.01, 10
V7X_VMEM_LIMIT = 56 * 1024 * 1024
LANE = 128

WEIGHTS = ['ffn1_norm', 'ffn1_w1', 'ffn1_w3', 'ffn1_w2', 'mix_norm', 'w_in', 's5_lambda_re', 's5_lambda_im',
           's5_log_dt', 's5_b_re', 's5_b_im', 's5_c_re', 's5_c_im', 's5_d', 's5_glu_w', 's5_glu_b', 'gla_a_up_w',
           'gla_a_up_b', 'gla_out_norm', 'proj_s5', 'proj_gla', 'w_out', 'ffn2_norm', 'ffn2_w1', 'ffn2_w3',
           'ffn2_w2', 'final_norm']
SHARDED = ['ffn1_w1', 'ffn1_w3', 'ffn1_w2', 'w_in', 's5_glu_w', 'proj_s5', 'proj_gla', 'w_out',
           'ffn2_w1', 'ffn2_w3', 'ffn2_w2']
COL_SHARDED = ['ffn1_w1', 'ffn1_w3', 'w_in', 'proj_s5', 'proj_gla', 'ffn2_w1', 'ffn2_w3', 'gla_a_up_w']
SMALL = [n for n in WEIGHTS if n not in SHARDED]
FFN_WEIGHTS = ['ffn1_w1', 'ffn1_w3', 'ffn1_w2', 'ffn2_w1', 'ffn2_w3', 'ffn2_w2']


def _params(**kw):
    return pltpu.CompilerParams(vmem_limit_bytes=V7X_VMEM_LIMIT, **kw)


class _Side:
    def __init__(self, ins, out_shapes, nsem, copies, aliased=False):
        self.ins, self.out_shapes, self.nsem, self.copies, self.aliased = list(ins), list(out_shapes), nsem, copies, aliased
        self.outs = None


_RIDER = []


def _pcall(body, **kw):
    if _RIDER:
        return _carry(body, _RIDER.pop(), **kw)
    return pl.pallas_call(body, **kw)


def _carry(body, side, *, name, grid, in_specs, out_specs, out_shape, scratch_shapes=(), compiler_params=None):
    del compiler_params
    single = not isinstance(out_shape, (list, tuple))
    out_specs = [out_specs] if single else list(out_specs)
    out_shape = [out_shape] if single else list(out_shape)
    n_in, n_out, n_scr = len(in_specs), len(out_shape), len(scratch_shapes)
    s_in, s_out = len(side.ins), len(side.out_shapes)
    any_spec = pl.BlockSpec(memory_space=pl.ANY)

    def wrapped(*refs):
        cuts = [n_in, s_in, n_out, s_out, n_scr]
        parts, pos = [], 0
        for c in cuts:
            parts.append(refs[pos:pos + c])
            pos += c
        ins, sins, outs, souts, scr = parts
        ssem, rsem = refs[pos], refs[pos + 1]
        first = last = None
        for d, g in enumerate(grid):
            i = pl.program_id(d)
            first = (i == 0) if first is None else first & (i == 0)
            last = (i == g - 1) if last is None else last & (i == g - 1)

        @pl.when(first)
        def _():
            for cp in side.copies(sins, souts, ssem, rsem):
                cp.start()

        body(*ins, *outs, *scr)

        @pl.when(last)
        def _():
            for cp in side.copies(sins, souts, ssem, rsem):
                cp.wait()

    call = pl.pallas_call(
        wrapped, name=name, grid=grid, in_specs=list(in_specs) + [any_spec] * s_in,
        out_specs=out_specs + [any_spec] * s_out, out_shape=out_shape + side.out_shapes,
        scratch_shapes=list(scratch_shapes) + [pltpu.SemaphoreType.DMA((side.nsem,)), pltpu.SemaphoreType.DMA((side.nsem,))],
        input_output_aliases={n_in + j: n_out + j for j in range(s_in)} if side.aliased else {},
        compiler_params=_params(has_side_effects=True))

    def run(*args):
        res = call(*args, *side.ins)
        side.outs = list(res[n_out:])
        return res[0] if single else list(res[:n_out])

    return run


def _run_side(side, name):
    s_in, s_out = len(side.ins), len(side.out_shapes)
    any_spec = pl.BlockSpec(memory_space=pl.ANY)

    def body(*refs):
        sins, souts = refs[:s_in], refs[s_in:s_in + s_out]
        ssem, rsem = refs[s_in + s_out:]
        cps = side.copies(sins, souts, ssem, rsem)
        for cp in cps:
            cp.start()
        for cp in cps:
            cp.wait()

    side.outs = list(pl.pallas_call(
        body, name=name, in_specs=[any_spec] * s_in, out_specs=[any_spec] * s_out, out_shape=side.out_shapes,
        scratch_shapes=[pltpu.SemaphoreType.DMA((side.nsem,)), pltpu.SemaphoreType.DMA((side.nsem,))],
        input_output_aliases={j: j for j in range(s_in)} if side.aliased else {},
        compiler_params=pltpu.CompilerParams(has_side_effects=True))(*side.ins))
    return side.outs


def _pick(n, cap, quantum):
    if n <= cap:
        return n
    best = None
    for t in range(quantum, cap + 1, quantum):
        if n % t == 0:
            best = t
    assert best is not None, (n, cap, quantum)
    return best


def _sigmoid(x):
    return jax.nn.sigmoid(x)


def _mm(a, b, *, name, ta=False, tb=False, out_dtype=F32, alpha=1.0, res=None, bias=None, exact=False, shard=None):
    ns = 4
    (k_a, m) = a.shape[-2:] if ta else a.shape[-2:][::-1]
    (k_b, n) = b.shape[-2:][::-1] if tb else b.shape[-2:]
    assert k_a == k_b, (a.shape, b.shape, ta, tb)
    assert (a.ndim == 3) == (shard in ('k', 'm')) and (b.ndim == 3) == (shard in ('n', 'k'))
    k = k_a
    tm = _pick(m, 1024, 128)
    tn = _pick(n, 1024, 128)
    tk = _pick(k, 1024, 128)
    pm, pn, pk = m // tm, n // tn, k // tk
    gm = pm * (ns if shard == 'm' else 1)
    gn = pn * (ns if shard == 'n' else 1)
    gk = pk * (ns if shard == 'k' else 1)
    dims = (((0,) if ta else (1,), (1,) if tb else (0,)), ((), ()))
    op_dtype = F32 if exact else BF16

    def body(*refs):
        a_ref, b_ref = refs[0], refs[1]
        pos = 2
        res_ref = bias_ref = None
        if res is not None:
            res_ref = refs[pos]
            pos += 1
        if bias is not None:
            bias_ref = refs[pos]
            pos += 1
        o_ref, acc_ref = refs[pos], refs[pos + 1]
        kk = pl.program_id(2)

        @pl.when(kk == 0)
        def _():
            acc_ref[...] = jnp.zeros_like(acc_ref)

        acc_ref[...] += lax.dot_general(a_ref[...].astype(op_dtype), b_ref[...].astype(op_dtype), dims,
                                        precision=HI if exact else None, preferred_element_type=F32)

        @pl.when(kk == gk - 1)
        def _():
            o = acc_ref[...]
            if alpha != 1.0:
                o = o * alpha
            if bias_ref is not None:
                o = o + bias_ref[...]
            if res_ref is not None:
                o = o + res_ref[...]
            o_ref[...] = o.astype(out_dtype)

    def spec(block, sharded_on, order):
        per = {'m': pm, 'n': pn, 'k': pk}

        def index(i, j, kk):
            g = {'m': i, 'n': j, 'k': kk}
            r, c = order(i % pm if shard == 'm' else i, j % pn if shard == 'n' else j, kk % pk if shard == 'k' else kk)
            if sharded_on is None:
                return (r, c)
            return (g[sharded_on] // per[sharded_on], r, c)

        return pl.BlockSpec(block if sharded_on is None else (None,) + block, index)

    a_sh = shard if shard in ('k', 'm') else None
    b_sh = shard if shard in ('n', 'k') else None
    o_sh = shard if shard in ('n', 'm') else None
    a_spec = spec((tk, tm), a_sh, lambda i, j, kk: (kk, i)) if ta else spec((tm, tk), a_sh, lambda i, j, kk: (i, kk))
    b_spec = spec((tn, tk), b_sh, lambda i, j, kk: (j, kk)) if tb else spec((tk, tn), b_sh, lambda i, j, kk: (kk, j))
    ins, in_specs = [a, b], [a_spec, b_spec]
    if res is not None:
        assert o_sh is None
        ins.append(res)
        in_specs.append(pl.BlockSpec((tm, tn), lambda i, j, kk: (i, j)))
    if bias is not None:
        assert o_sh is None
        ins.append(bias)
        in_specs.append(pl.BlockSpec((1, tn), lambda i, j, kk: (0, j)))
    out_shape = (m, n) if o_sh is None else (ns, m, n)
    return _pcall(body, name=name, grid=(gm, gn, gk), in_specs=in_specs,
                  out_specs=spec((tm, tn), o_sh, lambda i, j, kk: (i, j)),
                  out_shape=jax.ShapeDtypeStruct(out_shape, out_dtype),
                  scratch_shapes=[pltpu.VMEM((tm, tn), F32)], compiler_params=_params())(*ins)


ROWS_VMEM_BUDGET = 24 * 1024 * 1024


def _rows(body, ins, outs, *, n, name):
    cols = sum(a.shape[1] for a, kind in ins if kind == 'r') + sum(c for c, _, kind in outs if kind == 'r')
    cap = 256
    while cap < 2048 and 2 * 4 * cols * (2 * cap) <= ROWS_VMEM_BUDGET:
        cap *= 2
    tm = _pick(n, cap, 16)
    in_specs = []
    for arr, kind in ins:
        if kind == 'r':
            in_specs.append(pl.BlockSpec((tm, arr.shape[1]), lambda i: (i, 0)))
        else:
            in_specs.append(pl.BlockSpec(arr.shape, lambda i: (0, 0)))
    out_specs, out_shape = [], []
    for cols, dtype, kind in outs:
        if kind == 'r':
            out_specs.append(pl.BlockSpec((tm, cols), lambda i: (i, 0)))
            out_shape.append(jax.ShapeDtypeStruct((n, cols), dtype))
        else:
            out_specs.append(pl.BlockSpec((1, cols), lambda i: (0, 0)))
            out_shape.append(jax.ShapeDtypeStruct((1, cols), dtype))
    n_in = len(ins)
    acc_ids = [j for j, o in enumerate(outs) if o[2] == 'a']

    def wrapped(*refs):
        if acc_ids:
            @pl.when(pl.program_id(0) == 0)
            def _():
                for j in acc_ids:
                    refs[n_in + j][...] = jnp.zeros_like(refs[n_in + j])
        body(*refs)

    res = _pcall(wrapped, name=name, grid=(n // tm,), in_specs=in_specs, out_specs=out_specs, out_shape=out_shape,
                 compiler_params=_params())(*[a for a, _ in ins])
    return res


def _rms_fwd(x, g, name):
    def body(x_ref, g_ref, o_ref):
        xv = x_ref[...]
        rstd = lax.rsqrt(jnp.mean(xv * xv, axis=-1, keepdims=True) + EPS)
        o_ref[...] = (xv * rstd * g_ref[...]).astype(BF16)
    return _rows(body, [(x, 'r'), (g, 'f')], [(x.shape[1], BF16, 'r')], n=x.shape[0], name=name)[0]


def _rms_bwd(x, g, dn, dres, name):
    def body(x_ref, g_ref, dn_ref, dres_ref, dx_ref, dg_ref):
        xv = x_ref[...]
        rstd = lax.rsqrt(jnp.mean(xv * xv, axis=-1, keepdims=True) + EPS)
        xh = xv * rstd
        dn = dn_ref[...]
        dg_ref[...] += jnp.sum(dn * xh, axis=0, keepdims=True)
        dxh = dn * g_ref[...]
        dx_ref[...] = dres_ref[...] + rstd * (dxh - xh * jnp.mean(dxh * xh, axis=-1, keepdims=True))
    d = x.shape[1]
    return _rows(body, [(x, 'r'), (g, 'f'), (dn, 'r'), (dres, 'r')], [(d, F32, 'r'), (d, F32, 'a')],
                 n=x.shape[0], name=name)


def _gelu_parts(y):
    c0 = 0.7978845608028654
    inner = c0 * (y + 0.044715 * y * y * y)
    th = jnp.tanh(inner)
    return th, c0 * (1.0 + 3.0 * 0.044715 * y * y)


def _gelu_fwd(y, name):
    def body(y_ref, o_ref):
        yv = y_ref[...]
        th, _ = _gelu_parts(yv)
        o_ref[...] = 0.5 * yv * (1.0 + th)
    return _rows(body, [(y, 'r')], [(y.shape[1], F32, 'r')], n=y.shape[0], name=name)[0]


def _glu_fwd(zg, t, name):
    def body(z_ref, t_ref, o_ref):
        o_ref[...] = (z_ref[...] * _sigmoid(t_ref[...])).astype(BF16)
    return _rows(body, [(zg, 'r'), (t, 'r')], [(zg.shape[1], BF16, 'r')], n=zg.shape[0], name=name)[0]


def _glu_bwd1(dy, zg, t, name):
    def body(dy_ref, z_ref, t_ref, dz_ref, dt_ref, db_ref):
        dyv, zv = dy_ref[...], z_ref[...]
        sg = _sigmoid(t_ref[...])
        dz_ref[...] = dyv * sg
        dt = dyv * zv * sg * (1.0 - sg)
        dt_ref[...] = dt.astype(BF16)
        db_ref[...] += jnp.sum(dt, axis=0, keepdims=True)
    w = zg.shape[1]
    return _rows(body, [(dy, 'r'), (zg, 'r'), (t, 'r')], [(w, F32, 'r'), (w, BF16, 'r'), (w, F32, 'a')],
                 n=zg.shape[0], name=name)


def _glu_bwd2(dzg, ys, u, dskip, name):
    def body(dz_ref, y_ref, u_ref, d_ref, dy_ref, du_ref, dd_ref):
        yv = y_ref[...]
        th, dinner = _gelu_parts(yv)
        dy = dz_ref[...] * (0.5 * (1.0 + th) + 0.5 * yv * (1.0 - th * th) * dinner)
        dy_ref[...] = dy
        du_ref[...] = dy * d_ref[...]
        dd_ref[...] += jnp.sum(dy * u_ref[...], axis=0, keepdims=True)
    w = ys.shape[1]
    return _rows(body, [(dzg, 'r'), (ys, 'r'), (u, 'r'), (dskip, 'f')], [(w, F32, 'r'), (w, F32, 'r'), (w, F32, 'a')],
                 n=ys.shape[0], name=name)


def _scale_rows(u, dskip, name):
    def body(u_ref, d_ref, o_ref):
        o_ref[...] = u_ref[...] * d_ref[...]
    return _rows(body, [(u, 'r'), (dskip, 'f')], [(u.shape[1], F32, 'r')], n=u.shape[0], name=name)[0]


def _merge_fwd(zg, ps, pg, name):
    def body(z_ref, ps_ref, pg_ref, o_ref):
        zv = z_ref[...]
        o_ref[...] = (_sigmoid(zv[:, :D_MODEL]) * ps_ref[...] + _sigmoid(zv[:, D_MODEL:]) * pg_ref[...]).astype(BF16)
    return _rows(body, [(zg, 'r'), (ps, 'r'), (pg, 'r')], [(D_MODEL, BF16, 'r')], n=zg.shape[0], name=name)[0]


def _merge_bwd(dm, zg, ps, pg, name):
    def body(dm_ref, z_ref, ps_ref, pg_ref, dps_ref, dpg_ref, dz_ref):
        dmv, zv = dm_ref[...], z_ref[...]
        s1, s2 = _sigmoid(zv[:, :D_MODEL]), _sigmoid(zv[:, D_MODEL:])
        dps_ref[...] = (dmv * s1).astype(BF16)
        dpg_ref[...] = (dmv * s2).astype(BF16)
        dz_ref[:, :D_MODEL] = dmv * ps_ref[...] * s1 * (1.0 - s1)
        dz_ref[:, D_MODEL:] = dmv * pg_ref[...] * s2 * (1.0 - s2)
    return _rows(body, [(dm, 'r'), (zg, 'r'), (ps, 'r'), (pg, 'r')],
                 [(D_MODEL, BF16, 'r'), (D_MODEL, BF16, 'r'), (2 * D_MODEL, F32, 'r')], n=zg.shape[0], name=name)


def _final_loss(h, g, tgt, name):
    def body(h_ref, g_ref, t_ref, loss_ref, dh_ref, dg_ref):
        hv = h_ref[...]
        rstd = lax.rsqrt(jnp.mean(hv * hv, axis=-1, keepdims=True) + EPS)
        xh = hv * rstd
        err = xh * g_ref[...] - t_ref[...]
        part = 0.5 * jnp.sum(jnp.mean(err * err, axis=-1, keepdims=True), axis=0, keepdims=True)
        loss_ref[...] += jnp.broadcast_to(part, loss_ref.shape)
        dout = err * (1.0 / hv.shape[1])
        dg_ref[...] += jnp.sum(dout * xh, axis=0, keepdims=True)
        dxh = dout * g_ref[...]
        dh_ref[...] = rstd * (dxh - xh * jnp.mean(dxh * xh, axis=-1, keepdims=True))
    d = h.shape[1]
    return _rows(body, [(h, 'r'), (g, 'f'), (tgt, 'r')], [(LANE, F32, 'a'), (d, F32, 'r'), (d, F32, 'a')],
                 n=h.shape[0], name=name)


def _adamw_math(wv, gv, mv, vv):
    nm = ADAM_B1 * mv + (1.0 - ADAM_B1) * gv
    nv = ADAM_B2 * vv + (1.0 - ADAM_B2) * (gv * gv)
    m_hat = nm / (1.0 - ADAM_B1 ** ADAM_STEP)
    v_hat = nv / (1.0 - ADAM_B2 ** ADAM_STEP)
    return -ADAM_LR * (m_hat / (jnp.sqrt(v_hat) + ADAM_EPS) + ADAM_WD * wv), nm, nv


def _adamw(w, g, m, v, name):
    def body(w_ref, g_ref, m_ref, v_ref, d_ref, nm_ref, nv_ref):
        d_ref[...], nm_ref[...], nv_ref[...] = _adamw_math(w_ref[...], g_ref[...], m_ref[...], v_ref[...])
    c = w.shape[1]
    return _rows(body, [(w, 'r'), (g, 'r'), (m, 'r'), (v, 'r')], [(c, F32, 'r')] * 3, n=w.shape[0], name=name)


ADAMW_BLOCKS = 8


def _adamw_group(items, c_arr, name):
    per = ADAMW_BLOCKS // 2
    n = len(items)

    def body(c_ref, *refs):
        mine = (pl.program_id(0) // per) == c_ref[0]
        for k in range(n):
            w_ref, go_ref, gs_ref, m_ref, v_ref = refs[5 * k:5 * k + 5]
            g_ref, d_ref, nm_ref, nv_ref = refs[5 * n + 4 * k:5 * n + 4 * k + 4]
            gv = jnp.where(mine, go_ref[...], gs_ref[...])
            g_ref[...] = gv
            d_ref[...], nm_ref[...], nv_ref[...] = _adamw_math(w_ref[...], gv, m_ref[...], v_ref[...])

    in_specs, out_specs, out_shape, args = [pl.BlockSpec(memory_space=pltpu.SMEM)], [], [], [c_arr]
    for item in items:
        r, cols = item[0].shape
        assert r % (8 * ADAMW_BLOCKS) == 0, item[0].shape
        tr = r // ADAMW_BLOCKS
        full = pl.BlockSpec((tr, cols), lambda i: (i, 0))
        half = pl.BlockSpec((tr, cols), lambda i: (i % per, 0))
        in_specs += [full, half, half, full, full]
        out_specs += [full] * 4
        out_shape += [jax.ShapeDtypeStruct((r, cols), F32)] * 4
        args += list(item)
    return _pcall(body, name=name, grid=(ADAMW_BLOCKS,), in_specs=in_specs, out_specs=out_specs, out_shape=out_shape,
                  compiler_params=_params())(*args)


def _shift_rows(v, sh, down):
    rolled = pltpu.roll(v, sh if down else v.shape[0] - sh, axis=0)
    row = lax.broadcasted_iota(jnp.int32, v.shape, 0)
    keep = (row >= sh) if down else (row < v.shape[0] - sh)
    return jnp.where(keep, rolled, 0.0)


def _chain_segments(st_r, st_i, pw_r_ref, pw_i_ref, conj, down):
    vr, vi = st_r[...], st_i[...]
    sh, k = 1, 0
    while sh < SEG:
        pr, pi = pw_r_ref[k:k + 1, :], pw_i_ref[k:k + 1, :]
        if conj:
            pi = -pi
        sr, si = _shift_rows(vr, sh, down), _shift_rows(vi, sh, down)
        vr, vi = vr + pr * sr - pi * si, vi + pr * si + pi * sr
        sh, k = sh * 2, k + 1
    st_r[...] = _shift_rows(vr, 1, down)
    st_i[...] = _shift_rows(vi, 1, down)


def _expand_block(u_ref, t_ref, bu_ref):
    for j in range(BD_TILES):
        k = j % 4
        bu_ref[:, j * BD_ST:(j + 1) * BD_ST] = _dot(u_ref[:, k * BD_CH:(k + 1) * BD_CH], t_ref[j])


def _s5_scan(u, tiles, ar8, ai8, pw_r, pw_i, name):
    n = u.shape[0]
    rb = SCAN_ROWS
    nb, steps, lc = n // rb, rb // SEG, 512

    def body(u_ref, t_ref, ar_ref, ai_ref, pwr_ref, pwi_ref, x_ref, st_r, st_i, bu_ref):
        ph, b = pl.program_id(0), pl.program_id(1)

        @pl.when((ph == 0) & (b == 0))
        def _():
            st_r[...] = jnp.zeros_like(st_r)
            st_i[...] = jnp.zeros_like(st_i)

        _expand_block(u_ref, t_ref, bu_ref)

        def scan(store):
            for c in range(S5_GP // lc):
                re, im = slice(c * lc, (c + 1) * lc), slice(S5_GP + c * lc, S5_GP + (c + 1) * lc)
                a_r, a_i = ar_ref[:, re], ai_ref[:, re]

                def step(s, carry):
                    xr, xi = carry
                    rows = pl.ds(pl.multiple_of(s * SEG, SEG), SEG)
                    nr = a_r * xr - a_i * xi + bu_ref[rows, re]
                    ni = a_r * xi + a_i * xr + bu_ref[rows, im]
                    if store:
                        x_ref[rows, re] = nr
                        x_ref[rows, im] = ni
                    return nr, ni

                xr, xi = lax.fori_loop(0, steps, step, (st_r[:, re], st_i[:, re]), unroll=4)
                st_r[:, re] = xr
                st_i[:, re] = xi

        @pl.when(ph == 0)
        def _():
            scan(False)

        @pl.when((ph == 0) & (b == nb - 1))
        def _():
            _chain_segments(st_r, st_i, pwr_ref, pwi_ref, conj=False, down=True)

        @pl.when(ph == 1)
        def _():
            scan(True)

    full = lambda a: pl.BlockSpec(a.shape, lambda ph, b: (0, 0))
    return _pcall(body, name=name, grid=(2, nb),
                  in_specs=[pl.BlockSpec((rb, S5_W), lambda ph, b: (b, 0)), pl.BlockSpec(tiles.shape, lambda ph, b: (0, 0, 0)),
                            full(ar8), full(ai8), full(pw_r), full(pw_i)],
                  out_specs=pl.BlockSpec((rb, 2 * S5_GP), lambda ph, b: (b * ph, 0)),
                  out_shape=jax.ShapeDtypeStruct((n, 2 * S5_GP), F32),
                  scratch_shapes=[pltpu.VMEM((SEG, S5_GP), F32), pltpu.VMEM((SEG, S5_GP), F32),
                                  pltpu.VMEM((rb, 2 * S5_GP), F32)],
                  compiler_params=_params())(u, tiles, ar8, ai8, pw_r, pw_i)


def _s5_scan_bwd(dy, tiles, xs, ar8, ai8, pw_r, pw_i, name):
    n = dy.shape[0]
    rb = SCAN_ROWS
    nb, steps, lc = n // rb, rb // SEG, 256

    def body(dy_ref, t_ref, x_ref, ar_ref, ai_ref, pwr_ref, pwi_ref, lam_ref, da_ref, st_r, st_i, gx_ref):
        ph, b = pl.program_id(0), pl.program_id(1)

        @pl.when((ph == 0) & (b == 0))
        def _():
            st_r[...] = jnp.zeros_like(st_r)
            st_i[...] = jnp.zeros_like(st_i)
            da_ref[...] = jnp.zeros_like(da_ref)

        _expand_block(dy_ref, t_ref, gx_ref)

        def scan(store):
            for c in range(S5_GP // lc):
                re, im = slice(c * lc, (c + 1) * lc), slice(S5_GP + c * lc, S5_GP + (c + 1) * lc)
                a_r, a_i = ar_ref[:, re], ai_ref[:, re]

                def step(s, carry):
                    rows = pl.ds(pl.multiple_of((steps - 1 - s) * SEG, SEG), SEG)
                    if store:
                        lr, li, dr, di = carry
                        xr, xi = x_ref[rows, re], x_ref[rows, im]
                        dr = dr + lr * xr + li * xi
                        di = di + li * xr - lr * xi
                    else:
                        lr, li = carry
                    nr = a_r * lr + a_i * li + gx_ref[rows, re]
                    ni = a_r * li - a_i * lr + gx_ref[rows, im]
                    if store:
                        lam_ref[rows, re] = nr
                        lam_ref[rows, im] = ni
                        return nr, ni, dr, di
                    return nr, ni

                if store:
                    lr, li, dr, di = lax.fori_loop(0, steps, step, (st_r[:, re], st_i[:, re], da_ref[:, re], da_ref[:, im]),
                                                   unroll=4)
                    da_ref[:, re] = dr
                    da_ref[:, im] = di
                else:
                    lr, li = lax.fori_loop(0, steps, step, (st_r[:, re], st_i[:, re]), unroll=4)
                st_r[:, re] = lr
                st_i[:, re] = li

        @pl.when(ph == 0)
        def _():
            scan(False)

        @pl.when((ph == 0) & (b == nb - 1))
        def _():
            _chain_segments(st_r, st_i, pwr_ref, pwi_ref, conj=True, down=False)

        @pl.when(ph == 1)
        def _():
            scan(True)

    full = lambda a: pl.BlockSpec(a.shape, lambda ph, b: (0, 0))
    rev = lambda ph, b: (nb - 1 - b, 0)
    return _pcall(body, name=name, grid=(2, nb),
                  in_specs=[pl.BlockSpec((rb, S5_W), rev), pl.BlockSpec(tiles.shape, lambda ph, b: (0, 0, 0)),
                            pl.BlockSpec((rb, 2 * S5_GP), lambda ph, b: ((nb - 1 - b) * ph, 0)),
                            full(ar8), full(ai8), full(pw_r), full(pw_i)],
                  out_specs=[pl.BlockSpec((rb, 2 * S5_GP), lambda ph, b: (nb - 1 - b * ph, 0)),
                             pl.BlockSpec((SEG, 2 * S5_GP), lambda ph, b: (0, 0))],
                  out_shape=[jax.ShapeDtypeStruct((n, 2 * S5_GP), F32), jax.ShapeDtypeStruct((SEG, 2 * S5_GP), F32)],
                  scratch_shapes=[pltpu.VMEM((SEG, S5_GP), F32), pltpu.VMEM((SEG, S5_GP), F32),
                                  pltpu.VMEM((rb, 2 * S5_GP), F32)],
                  compiler_params=_params())(dy, tiles, xs, ar8, ai8, pw_r, pw_i)


def _s5_discretize(lam_re, lam_im, log_dt, b_re, b_im):
    dt = jnp.exp(log_dt)[:, None]
    mag = jnp.exp(lam_re * dt)
    ar = mag * jnp.cos(lam_im * dt)
    ai = mag * jnp.sin(lam_im * dt)
    den = lam_re * lam_re + lam_im * lam_im
    nr = ar - 1.0
    fr = (nr * lam_re + ai * lam_im) / den
    fi = (ai * lam_re - nr * lam_im) / den
    bbar_re = fr[:, :, None] * b_re - fi[:, :, None] * b_im
    bbar_im = fr[:, :, None] * b_im + fi[:, :, None] * b_re
    return ar, ai, bbar_re, bbar_im


BD_TILES, BD_CH, BD_ST, BD_GROUPS = 8, 128, 512, 8
BD_ROWS = 4096


def _bd_tiles(re, im):
    eye = jnp.eye(BD_GROUPS, dtype=re.dtype)

    def tiles(t):
        t = t.reshape(S5_G // BD_GROUPS, BD_GROUPS, S5_H, S5_P)
        return (t[:, :, :, None, :] * eye[None, :, None, :, None]).reshape(S5_G // BD_GROUPS, BD_CH, BD_ST)

    return jnp.concatenate([tiles(re), tiles(im)], axis=0)


def _bd_blocks(t):
    t = t.reshape(2, S5_G // BD_GROUPS, BD_GROUPS, S5_H, BD_GROUPS, S5_P)
    return jnp.einsum('rkahap->rkahp', t).reshape(2, S5_G, S5_H, S5_P)


def _bd_reduce(x, t, res, name):
    n = x.shape[0]
    tm = _pick(n, BD_ROWS, 16)

    def body(x_ref, t_ref, r_ref, o_ref):
        part = _dot(x_ref[...], t_ref[...], NT)

        @pl.when(pl.program_id(2) == 0)
        def _():
            o_ref[...] = r_ref[...] + part

        @pl.when(pl.program_id(2) == 1)
        def _():
            o_ref[...] += part

    return _pcall(body, name=name, grid=(n // tm, 4, 2),
                  in_specs=[pl.BlockSpec((tm, BD_ST), lambda i, k, r: (i, k + 4 * r)),
                            pl.BlockSpec((None, BD_CH, BD_ST), lambda i, k, r: (k + 4 * r, 0, 0)),
                            pl.BlockSpec((tm, BD_CH), lambda i, k, r: (i, k))],
                  out_specs=pl.BlockSpec((tm, BD_CH), lambda i, k, r: (i, k)),
                  out_shape=jax.ShapeDtypeStruct((n, S5_W), F32), compiler_params=_params())(x, t, res)


def _bd_outer(a, x, name):
    n = a.shape[0]
    tk = _pick(n, BD_ROWS, 16)
    nk = n // tk

    def body(a_ref, x_ref, o_ref):
        part = _dot(a_ref[...], x_ref[...], TN)

        @pl.when(pl.program_id(1) == 0)
        def _():
            o_ref[...] = part

        @pl.when(pl.program_id(1) > 0)
        def _():
            o_ref[...] += part

    return _pcall(body, name=name, grid=(BD_TILES, nk),
                  in_specs=[pl.BlockSpec((tk, BD_CH), lambda j, kk: (kk, j % 4)), pl.BlockSpec((tk, BD_ST), lambda j, kk: (kk, j))],
                  out_specs=pl.BlockSpec((None, BD_CH, BD_ST), lambda j, kk: (j, 0, 0)),
                  out_shape=jax.ShapeDtypeStruct((BD_TILES, BD_CH, BD_ST), F32), compiler_params=_params())(a, x)


def _permute_rows(t):
    n = t.shape[0]
    return t.reshape(SEG, n // SEG, t.shape[1]).transpose(1, 0, 2).reshape(n, t.shape[1])


def _unpermute_rows(t):
    n = t.shape[0]
    return t.reshape(n // SEG, SEG, t.shape[1]).transpose(1, 0, 2).reshape(n, t.shape[1])


def _segment_powers(ar, ai, seg_steps):
    pr, pi = ar.reshape(1, S5_GP), ai.reshape(1, S5_GP)
    e = 1
    while e < seg_steps:
        pr, pi = pr * pr - pi * pi, 2.0 * pr * pi
        e *= 2
    assert e == seg_steps, "segment length must be a power of two"
    rows_r, rows_i = [], []
    for _ in range(3):
        rows_r.append(pr)
        rows_i.append(pi)
        pr, pi = pr * pr - pi * pi, 2.0 * pr * pi
    pad = jnp.zeros((SEG - 3, S5_GP), F32)
    return jnp.concatenate(rows_r + [pad], axis=0), jnp.concatenate(rows_i + [pad], axis=0)


NT = (((1,), (1,)), ((), ()))
TN = (((0,), (0,)), ((), ()))


def _dot(a, b, dims=None, exact=False):
    dims = (((1,), (0,)), ((), ())) if dims is None else dims
    if exact:
        return lax.dot_general(a, b, dims, precision=HI, preferred_element_type=F32)
    return lax.dot_general(a.astype(BF16), b.astype(BF16), dims, preferred_element_type=F32)


def _dot01(a, b, dims=None, ones_first=True):
    x = b if ones_first else a
    hi = x.astype(BF16)
    lo = (x - hi.astype(F32)).astype(BF16)
    parts = [(_dot(a, p, dims) if ones_first else _dot(p, b, dims)) for p in (lo, hi)]
    return parts[0] + parts[1]


HEADS = range(4)


def _gla_chunk_fwd(qc, kc, vc, al, wup, bup, s_prev, tril):
    ones = jnp.ones((GLA_CHUNK, GLA_DV), F32)
    z = [_dot(al, wup[h]) + bup[h] for h in HEADS]
    la = [(jnp.minimum(z[h], 0.0) - jnp.log(1.0 + jnp.exp(-jnp.abs(z[h])))) * (1.0 / GLA_TAU) for h in HEADS]
    bc = [_dot01(tril, la[h]) for h in HEADS]
    blb = [_dot01(la[h], ones, TN, ones_first=False) for h in HEADS]
    bl = [bc[h][GLA_CHUNK - 1:GLA_CHUNK, :] for h in HEADS]
    ebc = [jnp.exp(bc[h]) for h in HEADS]
    qt = [qc[h] * (GLA_DK ** -0.5) * ebc[h] for h in HEADS]
    kt = [kc[h] * jnp.exp(-bc[h]) for h in HEADS]
    ke = [kc[h] * jnp.exp(bl[h] - bc[h]) for h in HEADS]
    sc = [_dot(qt[h], kt[h], NT) * tril for h in HEADS]
    oi = [_dot(sc[h], vc[h]) for h in HEADS]
    oo = [_dot(qt[h], s_prev[h]) for h in HEADS]
    o = [oi[h] + oo[h] for h in HEADS]
    return z, bc, bl, blb, ebc, qt, kt, ke, sc, o


GLA_ROWS = 512
GLA_CPB = GLA_ROWS // GLA_CHUNK


ZA_COLS = 5 * 512
SLOT = 128


def _pad_heads(w):
    r = w.shape[0]
    return jnp.pad(w.reshape(r, GLA_HEADS, GLA_DK), ((0, 0), (0, 0), (0, SLOT - GLA_DK))).reshape(r, GLA_HEADS * SLOT)


def _unpad_heads(w):
    r = w.shape[0]
    return w.reshape(r, GLA_HEADS, SLOT)[:, :, :GLA_DK].reshape(r, GLA_HEADS * GLA_DK)


def _gla_token_specs(blk):
    col = lambda cb: pl.BlockSpec((GLA_ROWS, 512), lambda j: (blk(j), cb))
    whole = lambda a: pl.BlockSpec(a.shape, lambda j: (0,) * a.ndim)
    return col, whole


def _head_ds(h, width):
    return pl.ds(h * SLOT, width)


def _tri(lower):
    ri = lax.broadcasted_iota(jnp.int32, (GLA_CHUNK, GLA_CHUNK), 0)
    ci = lax.broadcasted_iota(jnp.int32, (GLA_CHUNK, GLA_CHUNK), 1)
    return ((ri >= ci) if lower else (ri <= ci)).astype(F32)


def _gla_fwd(za, al, wup, bup, gn, name):
    n = za.shape[0]
    nc = n // GLA_CHUNK

    def body(q_ref, k_ref, v_ref, r_ref, al_ref, wup_ref, bup_ref, gn_ref, y_ref, sp_ref, s_ref):
        @pl.when(pl.program_id(0) == 0)
        def _():
            s_ref[...] = jnp.zeros_like(s_ref)

        tril = _tri(True)

        def chunk(c, carry):
            rows = pl.ds(pl.multiple_of(c * GLA_CHUNK, GLA_CHUNK), GLA_CHUNK)
            alc = al_ref[rows, :]
            vc = [v_ref[rows, _head_ds(h, GLA_DV)] for h in HEADS]
            s_prev = [s_ref[h] for h in HEADS]
            _, _, _, blb, _, _, _, ke, _, o = _gla_chunk_fwd(
                [q_ref[rows, _head_ds(h, GLA_DK)] for h in HEADS], [k_ref[rows, _head_ds(h, GLA_DK)] for h in HEADS],
                vc, alc, [wup_ref[h] for h in HEADS], [bup_ref[h] for h in HEADS], s_prev, tril)
            ds = [_dot(ke[h], vc[h], TN) for h in HEADS]
            for h in HEADS:
                rc = r_ref[rows, _head_ds(h, GLA_DV)]
                sp_ref[h, c] = s_prev[h]
                rstd = lax.rsqrt(jnp.mean(o[h] * o[h], axis=-1, keepdims=True) + EPS)
                y_ref[rows, _head_ds(h, GLA_DV)] = (o[h] * rstd * gn_ref[h] * (rc * _sigmoid(rc))).astype(BF16)
                s_ref[h] = jnp.exp(blb[h]) * s_prev[h] + ds[h]
            return carry

        lax.fori_loop(0, GLA_CPB, chunk, 0)

    col, whole = _gla_token_specs(lambda j: j)
    return _pcall(body, name=name, grid=(n // GLA_ROWS,),
                  in_specs=[col(1), col(2), col(3), col(4), pl.BlockSpec((GLA_ROWS, LANE), lambda j: (j, 0)),
                            whole(wup), whole(bup), whole(gn)],
                  out_specs=[pl.BlockSpec((GLA_ROWS, GLA_HEADS * GLA_DV), lambda j: (j, 0)),
                             pl.BlockSpec((GLA_HEADS, GLA_CPB, GLA_DK, GLA_DV), lambda j: (0, j, 0, 0))],
                  out_shape=[jax.ShapeDtypeStruct((n, GLA_HEADS * GLA_DV), BF16),
                             jax.ShapeDtypeStruct((GLA_HEADS, nc, GLA_DK, GLA_DV), F32)],
                  scratch_shapes=[pltpu.VMEM((GLA_HEADS, GLA_DK, GLA_DV), F32)],
                  compiler_params=_params())(za, za, za, za, al, wup, bup, gn)


def _gla_bwd(za, al, wup, bup, gn, sp, dy, du_s5, name):
    n = za.shape[0]
    nb = n // GLA_ROWS

    def body(q_ref, k_ref, v_ref, r_ref, al_ref, wup_ref, bup_ref, gn_ref, dy_ref, dus_ref, sp_ref,
             dza_ref, dz_ref, dgn_ref, dbup_ref, ds_ref):
        @pl.when(pl.program_id(0) == 0)
        def _():
            ds_ref[...] = jnp.zeros_like(ds_ref)
            dgn_ref[...] = jnp.zeros_like(dgn_ref)
            dbup_ref[...] = jnp.zeros_like(dbup_ref)

        tril, triu = _tri(True), _tri(False)
        dza_ref[:, 0:512] = dus_ref[...]
        dza_ref[:, 512:1536] = jnp.zeros((GLA_ROWS, 1024), F32)
        dz_ref[...] = jnp.zeros_like(dz_ref)

        def chunk(i, carry):
            c = GLA_CPB - 1 - i
            rows = pl.ds(pl.multiple_of(c * GLA_CHUNK, GLA_CHUNK), GLA_CHUNK)
            alc = al_ref[rows, :]
            qc = [q_ref[rows, _head_ds(h, GLA_DK)] for h in HEADS]
            kc = [k_ref[rows, _head_ds(h, GLA_DK)] for h in HEADS]
            vc = [v_ref[rows, _head_ds(h, GLA_DV)] for h in HEADS]
            s_prev = [sp_ref[h, c] for h in HEADS]
            ds = [ds_ref[h] for h in HEADS]
            z, bc, bl, blb, ebc, qt, kt, ke, sc, o = _gla_chunk_fwd(
                qc, kc, vc, alc, [wup_ref[h] for h in HEADS], [bup_ref[h] for h in HEADS], s_prev, tril)
            do = []
            for h in HEADS:
                rc = r_ref[rows, _head_ds(h, GLA_DV)]
                rs = lax.rsqrt(jnp.mean(o[h] * o[h], axis=-1, keepdims=True) + EPS)
                on = o[h] * rs
                sr = _sigmoid(rc)
                sil = rc * sr
                dyv, gnv = dy_ref[rows, _head_ds(h, GLA_DV)], gn_ref[h]
                dgn_ref[h] += jnp.sum(dyv * on * sil, axis=0, keepdims=True)
                dza_ref[rows, pl.ds(2048 + h * SLOT, GLA_DV)] = dyv * on * gnv * (sr * (1.0 + rc * (1.0 - sr)))
                don = dyv * gnv * sil
                do.append(rs * (don - on * jnp.mean(don * on, axis=-1, keepdims=True)))
            dp = [_dot(do[h], vc[h], NT) * tril for h in HEADS]
            dv1 = [_dot(sc[h], do[h], TN) for h in HEADS]
            dv2 = [_dot(ke[h], ds[h]) for h in HEADS]
            dq2 = [_dot(do[h], s_prev[h], NT) for h in HEADS]
            dke = [_dot(vc[h], ds[h], NT) for h in HEADS]
            ddec = [_dot01(jnp.ones((8, GLA_DV), F32), ds[h] * s_prev[h], NT)[0:1, :] for h in HEADS]
            dsn = [_dot(qt[h], do[h], TN) for h in HEADS]
            dq1 = [_dot(dp[h], kt[h]) for h in HEADS]
            dkt = [_dot(dp[h], qt[h], TN) for h in HEADS]
            dbc, dbl = [], []
            for h in HEADS:
                dqt = dq1[h] + dq2[h]
                dza_ref[rows, pl.ds(1536 + h * SLOT, GLA_DV)] = dv1[h] + dv2[h]
                ds_ref[h] = jnp.exp(blb[h]) * ds[h] + dsn[h]
                dza_ref[rows, pl.ds(512 + h * SLOT, GLA_DK)] = dqt * (GLA_DK ** -0.5) * ebc[h]
                dza_ref[rows, pl.ds(1024 + h * SLOT, GLA_DK)] = dkt[h] * jnp.exp(-bc[h]) + dke[h] * jnp.exp(bl[h] - bc[h])
                dbc.append(dqt * qt[h] - dkt[h] * kt[h] - dke[h] * ke[h])
                dbl.append(jnp.sum(dke[h] * ke[h], axis=0, keepdims=True) + ddec[h] * jnp.exp(bl[h]))
            dla = [_dot01(triu, dbc[h]) + dbl[h] for h in HEADS]
            for h in HEADS:
                dz = dla[h] * (1.0 - _sigmoid(z[h])) * (1.0 / GLA_TAU)
                dz_ref[rows, _head_ds(h, GLA_DK)] = dz
                dbup_ref[h] += jnp.sum(dz, axis=0, keepdims=True)
            return carry

        lax.fori_loop(0, GLA_CPB, chunk, 0)

    rev = lambda j: nb - 1 - j
    col, whole = _gla_token_specs(rev)
    tok = lambda w: pl.BlockSpec((GLA_ROWS, w), lambda j: (rev(j), 0))
    h1 = lambda w: pl.BlockSpec((GLA_HEADS, 1, w), lambda j: (0, 0, 0))
    s1 = lambda w: jax.ShapeDtypeStruct((GLA_HEADS, 1, w), F32)
    return _pcall(body, name=name, grid=(nb,),
                  in_specs=[col(1), col(2), col(3), col(4), tok(LANE), whole(wup), whole(bup), whole(gn), tok(512), tok(512),
                            pl.BlockSpec((GLA_HEADS, GLA_CPB, GLA_DK, GLA_DV), lambda j: (0, rev(j), 0, 0))],
                  out_specs=[tok(ZA_COLS), tok(GLA_HEADS * SLOT), h1(GLA_DV), h1(GLA_DK)],
                  out_shape=[jax.ShapeDtypeStruct((n, ZA_COLS), F32), jax.ShapeDtypeStruct((n, GLA_HEADS * SLOT), F32),
                             s1(GLA_DV), s1(GLA_DK)],
                  scratch_shapes=[pltpu.VMEM((GLA_HEADS, GLA_DK, GLA_DV), F32)],
                  compiler_params=_params())(za, za, za, za, al, wup, bup, gn, dy, du_s5, sp)


ANY = pl.BlockSpec(memory_space=pl.ANY)


def _place():
    x, y, c = lax.axis_index("x"), lax.axis_index("y"), lax.axis_index("c")
    chips = [(1 - x, y), (x, 1 - y), (1 - x, 1 - y)]
    return x, y, c, chips


def _remote(src, dst, ssem, rsem, dev):
    return pltpu.make_async_remote_copy(src_ref=src, dst_ref=dst, send_sem=ssem, recv_sem=rsem, device_id=dev,
                                        device_id_type=MESH_ID)


def _half(c, rows):
    h = rows // 2
    return pl.ds(pl.multiple_of(c * h, 8), h)


def _side_gather_ici(shards):
    def copies(ins, outs, ssem, rsem):
        x, y, c, chips = _place()
        mine = 2 * x + y
        cps = []
        for w in range(len(ins)):
            half = _half(c, ins[w].shape[0])
            cps.append(_remote(ins[w], outs[w].at[mine], ssem.at[4 * w], rsem.at[4 * w], (x, y, 1 - c)))
            for k, (px, py) in enumerate(chips):
                cps.append(_remote(ins[w].at[half], outs[w].at[mine, half], ssem.at[4 * w + 1 + k], rsem.at[4 * w + 1 + k],
                                   (px, py, c)))
        return cps

    return _Side(shards, [jax.ShapeDtypeStruct((4,) + s.shape, s.dtype) for s in shards], 4 * len(shards), copies)


def _side_gather_d2d(gathered):
    def copies(ins, outs, ssem, rsem):
        x, y, c, chips = _place()
        cps = []
        for w in range(len(outs)):
            half = _half(c, outs[w].shape[1])
            for k, (px, py) in enumerate(chips):
                theirs = outs[w].at[2 * px + py, half]
                cps.append(_remote(theirs, theirs, ssem.at[3 * w + k], rsem.at[3 * w + k], (x, y, 1 - c)))
        return cps

    return _Side(gathered, [jax.ShapeDtypeStruct(g.shape, g.dtype) for g in gathered], 3 * len(gathered), copies,
                 aliased=True)


def _side_swap_halves(grads):
    def copies(ins, outs, ssem, rsem):
        x, y, c, _ = _place()
        return [_remote(ins[w].at[:, _half(1 - c, ins[w].shape[1]), :], outs[w], ssem.at[w], rsem.at[w], (x, y, 1 - c))
                for w in range(len(ins))]

    return _Side(grads, [jax.ShapeDtypeStruct((4, g.shape[1] // 2, g.shape[2]), g.dtype) for g in grads], len(grads), copies)


def _side_scatter(sums):
    def copies(ins, outs, ssem, rsem):
        x, y, c, chips = _place()
        return [_remote(ins[w].at[2 * px + py], outs[w].at[k], ssem.at[3 * w + k], rsem.at[3 * w + k], (px, py, c))
                for w in range(len(ins)) for k, (px, py) in enumerate(chips)]

    return _Side(sums, [jax.ShapeDtypeStruct((3,) + s.shape[1:], s.dtype) for s in sums], 3 * len(sums), copies)


def _side_swap_reduced(halves):
    def copies(ins, outs, ssem, rsem):
        x, y, c, _ = _place()
        return [_remote(ins[w], outs[w], ssem.at[w], rsem.at[w], (x, y, 1 - c)) for w in range(len(ins))]

    return _Side(halves, [jax.ShapeDtypeStruct(h.shape, h.dtype) for h in halves], len(halves), copies)


def _chip_sum(g, recv, c_arr, name):
    _, r, cols = g.shape
    h = r // 2
    tr = _pick(h, 512, 16)
    g4 = g.reshape(4, 2, h, cols)

    def body(c_ref, g_ref, r_ref, o_ref):
        o_ref[...] = (g_ref[...] + r_ref[...]).astype(BF16)

    grid_spec = pltpu.PrefetchScalarGridSpec(
        num_scalar_prefetch=1, grid=(4, h // tr),
        in_specs=[pl.BlockSpec((None, None, tr, cols), lambda s, i, c_ref: (s, c_ref[0], i, 0)),
                  pl.BlockSpec((None, tr, cols), lambda s, i, c_ref: (s, i, 0))],
        out_specs=pl.BlockSpec((None, tr, cols), lambda s, i, c_ref: (s, i, 0)))
    return _pcall(body, name=name, grid_spec=grid_spec, out_shape=jax.ShapeDtypeStruct((4, h, cols), BF16),
                  compiler_params=_params())(c_arr, g4, recv)


def _owner_sum(sums, others, s_arr, name):
    _, h, cols = sums.shape
    tr = _pick(h, 512, 16)

    def body(s_ref, a_ref, o_ref, out_ref):
        f = lambda v: v.astype(F32)
        out_ref[...] = (f(a_ref[...]) + f(o_ref[0])) + (f(o_ref[1]) + f(o_ref[2]))

    grid_spec = pltpu.PrefetchScalarGridSpec(
        num_scalar_prefetch=1, grid=(h // tr,),
        in_specs=[pl.BlockSpec((None, tr, cols), lambda i, s_ref: (s_ref[0], i, 0)),
                  pl.BlockSpec((3, tr, cols), lambda i, s_ref: (0, i, 0))],
        out_specs=pl.BlockSpec((tr, cols), lambda i, s_ref: (i, 0)))
    return _pcall(body, name=name, grid_spec=grid_spec, out_shape=jax.ShapeDtypeStruct((h, cols), F32),
                  compiler_params=_params())(s_arr, sums, others)


def _side_small_sibling(v):
    def copies(ins, outs, ssem, rsem):
        x, y, c, _ = _place()
        return [_remote(ins[0], outs[0], ssem.at[0], rsem.at[0], (x, y, 1 - c))]

    return _Side([v], [jax.ShapeDtypeStruct(v.shape, F32)], 1, copies)


def _side_small_chips(v):
    def copies(ins, outs, ssem, rsem):
        x, y, c, chips = _place()
        return [_remote(ins[0], outs[0].at[k], ssem.at[k], rsem.at[k], (px, py, c)) for k, (px, py) in enumerate(chips)]

    return _Side([v], [jax.ShapeDtypeStruct((3,) + v.shape, F32)], 3, copies)


def _small_add(v, r, name):
    def body(v_ref, r_ref, o_ref):
        if r.ndim == 2:
            o_ref[...] = v_ref[...] + r_ref[...]
        else:
            o_ref[...] = (v_ref[...] + r_ref[0]) + (r_ref[1] + r_ref[2])

    vm = pl.BlockSpec(memory_space=pltpu.VMEM)
    return _pcall(body, name=name, in_specs=[vm, vm], out_specs=vm, out_shape=jax.ShapeDtypeStruct(v.shape, F32),
                  compiler_params=_params())(v, r)


def _merge_sides(sides):
    if len(sides) == 1:
        return sides[0]

    def copies(in_refs, out_refs, ssem, rsem):
        cps, i, o, q = [], 0, 0, 0
        for s in sides:
            ni, no = len(s.ins), len(s.out_shapes)
            cps += s.copies(in_refs[i:i + ni], out_refs[o:o + no], ssem.at[pl.ds(q, s.nsem)], rsem.at[pl.ds(q, s.nsem)])
            i, o, q = i + ni, o + no, q + s.nsem
        return cps

    assert not any(s.aliased for s in sides)
    return _Side(sum((s.ins for s in sides), []), sum((s.out_shapes for s in sides), []), sum(s.nsem for s in sides), copies)


def _tile_rows(size):
    return -(-size // (8 * LANE)) * 8


def _pack_small(parts):
    pieces = []
    for p in parts:
        flat = p.reshape(-1).astype(F32)
        pieces.append(jnp.pad(flat, (0, _tile_rows(p.size) * LANE - p.size)).reshape(-1, LANE))
    rows = sum(x.shape[0] for x in pieces)
    pieces.append(jnp.zeros(((-rows) % 64, LANE), F32))
    return jnp.concatenate(pieces, axis=0)


def _unpack_small(packed, like):
    out, pos = [], 0
    for p in like:
        rows = _tile_rows(p.size)
        out.append(packed[pos:pos + rows].reshape(-1)[:p.size].reshape(p.shape))
        pos += rows
    return out


FFN_FWD_ROWS, FFN_BWD_ROWS = 1024, 512
FFN_SUB_ROWS = 256


def _ffn_specs(n, d, fs, cap):
    rows = _pick(n, cap, 16)
    row = pl.BlockSpec((rows, d), lambda i, s: (i, 0))
    gain = pl.BlockSpec((1, d), lambda i, s: (0, 0))
    w_row = pl.BlockSpec((None, fs, d), lambda i, s: (s, 0, 0))
    hid = pl.BlockSpec((None, rows, fs), lambda i, s: (s, i, 0))
    return rows, row, gain, w_row, hid


def _ffn_fwd(h, g, w1t, w3t, w2, tag, plan):
    n, d = h.shape
    ns, fs, _ = w2.shape
    rows, row, gain, w_row, hid = _ffn_specs(n, d, fs, FFN_FWD_ROWS)
    sub = rows

    def body(h_ref, g_ref, w1_ref, w3_ref, w2_ref, out_ref, n1_ref, a_ref, b_ref, hm_ref, acc_ref):
        s = pl.program_id(1)

        @pl.when(s == 0)
        def _():
            xv = h_ref[...]
            rstd = lax.rsqrt(jnp.mean(xv * xv, axis=-1, keepdims=True) + EPS)
            n1_ref[...] = (xv * rstd * g_ref[...]).astype(BF16)
            acc_ref[...] = jnp.zeros_like(acc_ref)

        def up(j):
            n1 = n1_ref[j * sub:(j + 1) * sub, :]
            return _dot(n1, w1_ref[...], NT), _dot(n1, w3_ref[...], NT)

        cur = up(0)
        for j in range(rows // sub):
            nxt = up(j + 1) if (j + 1) * sub < rows else None
            a, b = cur
            r = slice(j * sub, (j + 1) * sub)
            sg = _sigmoid(a)
            act = a * sg
            hm = (act * b).astype(BF16)
            a_ref[r, :] = (b * (sg * (1.0 + a * (1.0 - sg)))).astype(BF16)
            b_ref[r, :] = act.astype(BF16)
            hm_ref[r, :] = hm
            acc_ref[r, :] += _dot(hm, w2_ref[...])
            cur = nxt

        @pl.when(s == ns - 1)
        def _():
            out_ref[...] = h_ref[...] + 0.5 * acc_ref[...]

    hid_shape = jax.ShapeDtypeStruct((ns, n, fs), BF16)
    plan.before(f"{tag}_fwd")
    out, n1, a, b, hm = _pcall(
        body, name=f"{tag}_fwd", grid=(n // rows, ns), in_specs=[row, gain, w_row, w_row, w_row],
        out_specs=[row, row, hid, hid, hid],
        out_shape=[jax.ShapeDtypeStruct((n, d), F32), jax.ShapeDtypeStruct((n, d), BF16), hid_shape, hid_shape, hid_shape],
        scratch_shapes=[pltpu.VMEM((rows, d), F32)], compiler_params=_params())(h, g, w1t, w3t, w2)
    plan.after(f"{tag}_fwd")
    return out, (h, n1, a, b, hm)


def _wgrad(a3, b, name, alpha=1.0):
    ns, n, fs = a3.shape
    d = b.shape[1]
    tk = _pick(n, 1024, 16)

    def body(a_ref, b_ref, o_ref):
        @pl.when(pl.program_id(0) == 0)
        def _():
            o_ref[...] = jnp.zeros_like(o_ref)

        bv = b_ref[...].astype(BF16)
        for s in range(ns):
            part = _dot(a_ref[s], bv, TN)
            o_ref[s] += part if alpha == 1.0 else alpha * part

    return _pcall(body, name=name, grid=(n // tk,),
                  in_specs=[pl.BlockSpec((ns, tk, fs), lambda k: (0, k, 0)), pl.BlockSpec((tk, d), lambda k: (k, 0))],
                  out_specs=pl.BlockSpec((ns, fs, d), lambda k: (0, 0, 0)),
                  out_shape=jax.ShapeDtypeStruct((ns, fs, d), F32), compiler_params=_params())(a3, b)


def _ffn_bwd(dout, saved, g, w1, w3, w2, tag, plan):
    h, n1, a, b, hm = saved
    n, d = h.shape
    ns, fs, _ = w2.shape
    rows, row, gain, w_row, hid = _ffn_specs(n, d, fs, FFN_BWD_ROWS)
    sub = _pick(rows, FFN_SUB_ROWS, 16)

    def body(do_ref, h_ref, g_ref, a_ref, b_ref, w1_ref, w3_ref, w2_ref, dh_ref, da_ref, db_ref, dg_ref, acc_ref):
        i, s = pl.program_id(0), pl.program_id(1)

        @pl.when(s == 0)
        def _():
            acc_ref[...] = jnp.zeros_like(acc_ref)

        @pl.when((s == 0) & (i == 0))
        def _():
            dg_ref[...] = jnp.zeros_like(dg_ref)

        def up(j):
            return _dot(0.5 * do_ref[j * sub:(j + 1) * sub, :], w2_ref[...], NT)

        cur = up(0)
        for j in range(rows // sub):
            nxt = up(j + 1) if (j + 1) * sub < rows else None
            r = slice(j * sub, (j + 1) * sub)
            da = (cur * a_ref[r, :].astype(F32)).astype(BF16)
            db = (cur * b_ref[r, :].astype(F32)).astype(BF16)
            da_ref[r, :] = da
            db_ref[r, :] = db
            acc_ref[r, :] += _dot(da, w1_ref[...]) + _dot(db, w3_ref[...])
            cur = nxt

        @pl.when(s == ns - 1)
        def _():
            xv, dn = h_ref[...], acc_ref[...]
            rstd = lax.rsqrt(jnp.mean(xv * xv, axis=-1, keepdims=True) + EPS)
            xh = xv * rstd
            dg_ref[...] += jnp.sum(dn * xh, axis=0, keepdims=True)
            dxh = dn * g_ref[...]
            dh_ref[...] = do_ref[...] + rstd * (dxh - xh * jnp.mean(dxh * xh, axis=-1, keepdims=True))

    hid_shape = jax.ShapeDtypeStruct((ns, n, fs), BF16)
    plan.before(f"{tag}_bwd")
    dh, da, db, dg = _pcall(
        body, name=f"{tag}_bwd", grid=(n // rows, ns), in_specs=[row, row, gain, hid, hid, w_row, w_row, w_row],
        out_specs=[row, hid, hid, gain],
        out_shape=[jax.ShapeDtypeStruct((n, d), F32), hid_shape, hid_shape, jax.ShapeDtypeStruct((1, d), F32)],
        scratch_shapes=[pltpu.VMEM((rows, d), F32)], compiler_params=_params())(dout, h, g, a, b, w1, w3, w2)
    plan.after(f"{tag}_bwd")
    plan.grads[f"{tag}_norm"] = dg
    plan.before(f"{tag}_gw2")
    gw2 = _wgrad(hm, dout, f"{tag}_gw2", alpha=0.5)
    plan.after(f"{tag}_gw2")
    plan.grads[f"{tag}_w2"] = gw2
    plan.before(f"{tag}_gw1")
    gw1 = _wgrad(da, n1, f"{tag}_gw1")
    plan.after(f"{tag}_gw1")
    plan.grads[f"{tag}_w1"] = gw1
    plan.before(f"{tag}_gw3")
    gw3 = _wgrad(db, n1, f"{tag}_gw3")
    plan.after(f"{tag}_gw3")
    return dh, dg, gw1, gw3, gw2


def _local_step(x, tgt, plan):
    n = x.shape[0]
    grads = plan.grads

    def f(name):
        w = plan.get(name)
        return w.reshape(1, D_MODEL) if name.endswith('_norm') and name != 'gla_out_norm' else w

    def carried(tag, fn, *args, **kw):
        plan.before(tag)
        out = fn(*args, **kw)
        plan.after(tag)
        return out

    h1, ffn1 = _ffn_fwd(x, f('ffn1_norm'), f('ffn1_w1'), f('ffn1_w3'), f('ffn1_w2'), "ffn1", plan)
    u = carried("mix_rms", _rms_fwd, h1, f('mix_norm'), "mix_rms")
    w_in = f('w_in')
    w_a = jnp.concatenate([w_in[:, :512], _pad_heads(w_in[:, 512:768]), _pad_heads(w_in[:, 768:1024]), w_in[:, 1024:2048]],
                          axis=1)
    w_al = jnp.pad(w_in[:, 2048:2048 + GLA_RANK], ((0, 0), (0, LANE - GLA_RANK)))
    w_g = w_in[:, 2048 + GLA_RANK:]
    za = carried("in_a", _mm, u, w_a, name="in_a")
    zg = carried("in_g", _mm, u, w_g, name="in_g")
    al = _mm(u, w_al, name="in_al")
    ar, ai, bbar_re, bbar_im = _s5_discretize(f('s5_lambda_re'), f('s5_lambda_im'), f('s5_log_dt'), f('s5_b_re'), f('s5_b_im'))
    t_b = _bd_tiles(bbar_re.transpose(0, 2, 1), bbar_im.transpose(0, 2, 1)).astype(BF16)
    t_c = _bd_tiles(f('s5_c_re'), -f('s5_c_im')).astype(BF16)
    ar8 = jnp.broadcast_to(ar.reshape(1, S5_GP), (SEG, S5_GP))
    ai8 = jnp.broadcast_to(ai.reshape(1, S5_GP), (SEG, S5_GP))
    pw_r, pw_i = _segment_powers(ar, ai, n // SEG)
    dskip = f('s5_d').reshape(1, S5_W)
    u_s5 = _permute_rows(za[:, :S5_W])
    xs = _s5_scan(u_s5, t_b, ar8, ai8, pw_r, pw_i, "s5_scan")
    ys_p = _bd_reduce(xs, t_c, _scale_rows(u_s5, dskip, "s5_skip"), "s5_y")
    ys = _unpermute_rows(ys_p)
    zgelu = _gelu_fwd(ys, "s5_gelu")
    t_glu = _mm(zgelu, f('s5_glu_w'), bias=f('s5_glu_b').reshape(1, S5_W), name="s5_glu_t")
    y_s5 = _glu_fwd(zgelu, t_glu, "s5_glu")
    wup = jnp.pad(f('gla_a_up_w'), ((0, LANE - GLA_RANK), (0, 0)))
    wup_h = wup.reshape(LANE, GLA_HEADS, GLA_DK).transpose(1, 0, 2)
    bup_h = f('gla_a_up_b').reshape(GLA_HEADS, 1, GLA_DK)
    gn_h = f('gla_out_norm').reshape(GLA_HEADS, 1, GLA_DV)
    y_gla, s_prev = carried("gla_fwd", _gla_fwd, za, al, wup_h, bup_h, gn_h, "gla_fwd")
    ps = _mm(y_s5, f('proj_s5'), name="proj_s5")
    pg = carried("proj_gla", _mm, y_gla, f('proj_gla'), name="proj_gla")
    merged = _merge_fwd(zg, ps, pg, "merge")
    h2 = _mm(merged, f('w_out'), res=h1, name="w_out")
    h3, ffn2 = _ffn_fwd(h2, f('ffn2_norm'), f('ffn2_w1'), f('ffn2_w3'), f('ffn2_w2'), "ffn2", plan)
    loss, dh3, g_final = _final_loss(h3, f('final_norm').reshape(1, D_MODEL), tgt, "loss")
    plan.loss = loss[0, 0]
    grads['final_norm'] = g_final.reshape(D_MODEL)
    dh2, grads['ffn2_norm'], grads['ffn2_w1'], grads['ffn2_w3'], grads['ffn2_w2'] = _ffn_bwd(
        dh3, ffn2, f('ffn2_norm'), f('ffn2_w1'), f('ffn2_w3'), f('ffn2_w2'), "ffn2", plan)
    dm = carried("d_merged", _mm, dh2, f('w_out'), tb=True, name="d_merged")
    grads['w_out'] = _mm(merged, dh2, ta=True, name="g_w_out")
    dps, dpg, dzg = carried("d_merge", _merge_bwd, dm, zg, ps, pg, "d_merge")
    grads['proj_s5'] = _mm(y_s5, dps, ta=True, name="g_proj_s5")
    grads['proj_gla'] = _mm(y_gla, dpg, ta=True, name="g_proj_gla")
    dy_s5 = _mm(dps, f('proj_s5'), tb=True, name="d_y_s5")
    dy_gla = _mm(dpg, f('proj_gla'), tb=True, name="d_y_gla")
    dzgelu, dt_glu, g_glu_b = _glu_bwd1(dy_s5, zgelu, t_glu, "d_glu")
    grads['s5_glu_b'] = g_glu_b.reshape(S5_W)
    grads['s5_glu_w'] = _mm(zgelu, dt_glu, ta=True, name="g_glu_w")
    dzgelu = _mm(dt_glu, f('s5_glu_w'), tb=True, res=dzgelu, name="d_gelu")
    dys, du_skip, g_d = _glu_bwd2(_permute_rows(dzgelu), ys_p, u_s5, dskip, "d_s5_y")
    grads['s5_d'] = g_d.reshape(S5_G, S5_H)
    lam, da8 = _s5_scan_bwd(dys, t_c, xs, ar8, ai8, pw_r, pw_i, "s5_scan_bwd")
    g_c = _bd_blocks(_bd_outer(dys, xs, "g_s5_c"))
    grads['s5_c_re'], grads['s5_c_im'] = g_c[0], -g_c[1]
    g_b = _bd_blocks(_bd_outer(u_s5, lam, "g_s5_b")).transpose(0, 1, 3, 2)
    g_bbar_re, g_bbar_im = g_b[0], g_b[1]
    da = jnp.sum(da8, axis=0)
    g_ar, g_ai = da[:S5_GP].reshape(S5_G, S5_P), da[S5_GP:].reshape(S5_G, S5_P)
    _, disc_vjp = jax.vjp(_s5_discretize, f('s5_lambda_re'), f('s5_lambda_im'), f('s5_log_dt'), f('s5_b_re'), f('s5_b_im'))
    (grads['s5_lambda_re'], grads['s5_lambda_im'], grads['s5_log_dt'], grads['s5_b_re'],
     grads['s5_b_im']) = disc_vjp((g_ar, g_ai, g_bbar_re, g_bbar_im))
    du_s5 = _unpermute_rows(_bd_reduce(lam, t_b, du_skip, "d_s5_u"))
    dza, dz, dgn, dbup = carried("gla_bwd", _gla_bwd, za, al, wup_h, bup_h, gn_h, s_prev, dy_gla, du_s5, "gla_bwd")
    grads['gla_out_norm'] = dgn.reshape(GLA_HEADS * GLA_DV)
    grads['gla_a_up_b'] = dbup.reshape(GLA_HEADS * GLA_DK)
    grads['gla_a_up_w'] = _unpad_heads(_mm(al, dz, ta=True, name="g_a_up")[:GLA_RANK])
    dal = _mm(dz, _pad_heads(wup), tb=True, name="d_a_low")
    g_wa = _mm(u, dza, ta=True, name="g_in_a")
    g_wg = _mm(u, dzg, ta=True, name="g_in_g")
    g_wal = _mm(u, dal, ta=True, name="g_in_al")
    grads['w_in'] = jnp.concatenate([g_wa[:, :512], _unpad_heads(g_wa[:, 512:1024]), _unpad_heads(g_wa[:, 1024:1536]),
                                     g_wa[:, 1536:], g_wal[:, :GLA_RANK], g_wg], axis=1)
    du = carried("d_u_a", _mm, dza, w_a, tb=True, name="d_u_a")
    du = _mm(dzg, w_g, tb=True, res=du, name="d_u_g")
    du = _mm(dal, w_al, tb=True, res=du, name="d_u_al")
    dh1, g_mix = carried("d_mix_rms", _rms_bwd, h1, f('mix_norm'), du, dh2, "d_mix_rms")
    grads['mix_norm'] = g_mix
    dx, grads['ffn1_norm'], grads['ffn1_w1'], grads['ffn1_w3'], grads['ffn1_w2'] = _ffn_bwd(
        dh1, ffn1, f('ffn1_norm'), f('ffn1_w1'), f('ffn1_w3'), f('ffn1_w2'), "ffn1", plan)
    return loss[0, 0], dx


MIXER_WEIGHTS = ['w_in', 's5_glu_w', 'proj_s5', 'proj_gla', 'w_out', 'gla_a_up_w']
FFN1_WEIGHTS, FFN2_WEIGHTS = FFN_WEIGHTS[:3], FFN_WEIGHTS[3:]
TRANSPOSED = ['ffn1_w1', 'ffn1_w3', 'ffn2_w1', 'ffn2_w3']


def _local_shard(w, nm):
    return jnp.swapaxes(w, 1, 2)[0] if nm in TRANSPOSED else w[0]
FFN1_EARLY = ['ffn1_w2']
FFN1_LATE = ['ffn1_w1', 'ffn1_w3']
GRAD_GROUPS = {'ffn2': FFN2_WEIGHTS, 'mixer': ['w_out', 'proj_s5', 'proj_gla', 's5_glu_w', 'w_in'], 'ffn1': FFN1_WEIGHTS}


class _Plan:
    def __init__(self, a, c_arr, s_arr):
        self.a, self.c_arr, self.s_arr = a, c_arr, s_arr
        self.grads, self.weights, self.riding = {}, {}, {}
        self.g4s, self.chip_sums, self.halves, self.sib_halves = {}, {}, {}, {}
        for nm in SMALL:
            if nm != 'gla_a_up_w':
                self.weights[nm] = a[nm] if nm == 'final_norm' else a[nm][0]
        ici = _side_gather_ici(self._shards(FFN1_WEIGHTS))
        _run_side(ici, "gather_ffn1_ici")
        self._gathered(FFN1_WEIGHTS, _run_side(_side_gather_d2d(ici.outs), "gather_ffn1_d2d"))

    def _shards(self, names):
        return [_local_shard(self.a[nm], nm).astype(F32 if nm == 'gla_a_up_w' else BF16) for nm in names]

    def _gathered(self, names, arrs):
        for nm, g4 in zip(names, arrs):
            if nm in FFN_WEIGHTS:
                self.weights[nm] = g4
            elif nm in COL_SHARDED:
                self.weights[nm] = jnp.concatenate([g4[s] for s in range(4)], axis=1)
            else:
                self.weights[nm] = g4.reshape(4 * g4.shape[1], g4.shape[2])

    def get(self, name):
        return self.weights[name]

    def _shard_major(self, nm):
        g = self.grads[nm]
        if nm in FFN_WEIGHTS:
            return g
        if nm in COL_SHARDED:
            return jnp.stack(jnp.split(g, 4, axis=1))
        return g.reshape(4, g.shape[0] // 4, g.shape[1])

    def _schedule(self, tag):
        grp = GRAD_GROUPS
        gathers = {"ffn1_fwd": ('ici', MIXER_WEIGHTS), "mix_rms": ('d2d', MIXER_WEIGHTS),
                   "in_a": ('ici', FFN2_WEIGHTS[:1]), "in_g": ('d2d', FFN2_WEIGHTS[:1]),
                   "gla_fwd": ('ici', FFN2_WEIGHTS[1:]), "proj_gla": ('d2d', FFN2_WEIGHTS[1:])}
        if tag in gathers:
            kind, names = gathers[tag]
            key = tuple(names)
            if kind == 'ici':
                return [(_side_gather_ici(self._shards(names)), lambda outs: self.riding.update({key: outs}))]
            return [(_side_gather_d2d(self.riding[key]), lambda outs: self._gathered(names, outs))]
        steps = {"ffn2_gw1": (['ffn2_w2'], 0), "ffn2_gw3": (['ffn2_w1'], 0), "d_merged": (['ffn2_w3'], 0),
                 "gla_bwd": (grp['ffn2'], 1), "d_mix_rms": (grp['ffn2'], 2),
                 "d_u_a": (grp['mixer'], 0), "ffn1_bwd": (grp['mixer'], 1), "ffn1_gw2": (grp['mixer'], 2),
                 "ffn1_gw1": (FFN1_EARLY, 0), "ffn1_gw3": (FFN1_EARLY, 1), "adamw_early": (FFN1_LATE, 1)}
        entries = [self._reduce_stage(*steps[tag])] if tag in steps else []
        if tag == "ffn1_gw2":
            entries.append(self._small_stage(0))
        if tag == "ffn1_gw1":
            entries.append(self._small_stage(1))
        return entries

    def _small_stage(self, stage):
        if stage == 0:
            a, grads = self.a, self.grads
            self.small_parts = ([grads[nm].reshape(a[nm].shape) for nm in SMALL if nm != 'gla_a_up_w']
                                + [grads['gla_a_up_w'], self.loss.reshape(1)])
            packed = _pack_small(self.small_parts)

            def done(outs):
                self.small_pair = _small_add(packed, outs[0], "small_sum_pair")
            return _side_small_sibling(packed), done

        def done(outs):
            self.small_total = _small_add(self.small_pair, outs[0], "small_sum_chips")
        return _side_small_chips(self.small_pair), done

    def _reduce_stage(self, names, stage):
        if stage == 0:
            for nm in names:
                self.g4s[nm] = self._shard_major(nm)

            def done(outs):
                for nm, r in zip(names, outs):
                    self.chip_sums[nm] = _chip_sum(self.g4s[nm], r, self.c_arr, f"chip_sum_{nm}")
            return _side_swap_halves([self.g4s[nm] for nm in names]), done
        if stage == 1:
            def done(outs):
                for nm, o in zip(names, outs):
                    self.halves[nm] = _owner_sum(self.chip_sums[nm], o, self.s_arr, f"owner_sum_{nm}")
            return _side_scatter([self.chip_sums[nm] for nm in names]), done

        def done(outs):
            self.sib_halves.update(zip(names, outs))
        return _side_swap_reduced([self.halves[nm] for nm in names]), done

    def before(self, tag):
        entries = self._schedule(tag)
        if entries:
            merged = _merge_sides([side for side, _ in entries])
            self.riding[tag] = (merged, entries)
            _RIDER.append(merged)

    def after(self, tag):
        if tag in self.riding:
            merged, entries = self.riding.pop(tag)
            assert not _RIDER and merged.outs is not None, tag
            pos = 0
            for side, done in entries:
                done(merged.outs[pos:pos + len(side.out_shapes)])
                pos += len(side.out_shapes)

    def finish_alone(self, stage):
        names = FFN1_LATE if stage == 0 else GRAD_GROUPS['ffn1']
        side, done = self._reduce_stage(names, stage)
        done(_run_side(side, f"grad_ffn1_stage{stage}"))


def _train_step(a):
    x = a['x'][0]
    tgt = a['loss_target'][0]
    xi, yi, ci = lax.axis_index("x"), lax.axis_index("y"), lax.axis_index("c")
    c_arr = jnp.reshape(ci, (1,)).astype(jnp.int32)
    s_arr = jnp.reshape(2 * xi + yi, (1,)).astype(jnp.int32)
    plan = _Plan(a, c_arr, s_arr)
    loss, dx = _local_step(x, tgt, plan)
    red = {}
    small_sum = _unpack_small(plan.small_total, plan.small_parts)
    small_names = [nm for nm in SMALL if nm != 'gla_a_up_w']
    for nm, g in zip(small_names, small_sum[:-2]):
        red[nm] = g
    loss = small_sum[-1].reshape(())
    g_up = small_sum[-2]
    red['gla_a_up_w'] = lax.dynamic_slice(g_up, (0, (2 * xi + yi) * GLA_DK), (GLA_RANK, GLA_DK))
    out_g, out_d, out_m, out_v = {}, {}, {}, {}

    def update(names, tag):
        items = [(_local_shard(a[nm], nm), plan.halves[nm], plan.sib_halves[nm], _local_shard(a['m_' + nm], nm),
                  _local_shard(a['v_' + nm], nm)) for nm in names]
        plan.before(tag)
        res = _adamw_group(items, c_arr, tag)
        plan.after(tag)
        for k, nm in enumerate(names):
            back = (lambda t: jnp.swapaxes(t[None], 1, 2)) if nm in TRANSPOSED else (lambda t: t[None])
            out_g[nm], out_d[nm], out_m[nm], out_v[nm] = (back(t) for t in res[4 * k:4 * k + 4])

    plan.finish_alone(0)
    update([nm for nm in SHARDED if nm not in GRAD_GROUPS['ffn1']], "adamw_early")
    plan.finish_alone(2)
    update(GRAD_GROUPS['ffn1'], "adamw_ffn1")
    rest = [nm for nm in WEIGHTS if nm not in SHARDED]
    pk = lambda pre: _pack_small([a[pre + nm] for nm in rest])
    d, nm_, nv_ = _adamw(pk(''), _pack_small([red[nm] for nm in rest]), pk('m_'), pk('v_'), "adamw_small")
    like = [a[nm] for nm in rest]
    for nm, g, dd, mm_, vv_ in zip(rest, [red[nm].reshape(a[nm].shape) for nm in rest], _unpack_small(d, like),
                                   _unpack_small(nm_, like), _unpack_small(nv_, like)):
        out_g[nm], out_d[nm], out_m[nm], out_v[nm] = g, dd, mm_, vv_
    return (loss, dx[None], *[out_g[nm] for nm in WEIGHTS], *[out_d[nm] for nm in WEIGHTS],
            *[out_m[nm] for nm in WEIGHTS], *[out_v[nm] for nm in WEIGHTS])


def kernel(x, ffn1_norm, ffn1_w1, ffn1_w3, ffn1_w2, mix_norm, w_in, s5_lambda_re, s5_lambda_im, s5_log_dt, s5_b_re, s5_b_im, s5_c_re, s5_c_im, s5_d, s5_glu_w, s5_glu_b, gla_a_up_w, gla_a_up_b, gla_out_norm, proj_s5, proj_gla, w_out, ffn2_norm, ffn2_w1, ffn2_w3, ffn2_w2, final_norm, loss_target, m_ffn1_norm, m_ffn1_w1, m_ffn1_w3, m_ffn1_w2, m_mix_norm, m_w_in, m_s5_lambda_re, m_s5_lambda_im, m_s5_log_dt, m_s5_b_re, m_s5_b_im, m_s5_c_re, m_s5_c_im, m_s5_d, m_s5_glu_w, m_s5_glu_b, m_gla_a_up_w, m_gla_a_up_b, m_gla_out_norm, m_proj_s5, m_proj_gla, m_w_out, m_ffn2_norm, m_ffn2_w1, m_ffn2_w3, m_ffn2_w2, m_final_norm, v_ffn1_norm, v_ffn1_w1, v_ffn1_w3, v_ffn1_w2, v_mix_norm, v_w_in, v_s5_lambda_re, v_s5_lambda_im, v_s5_log_dt, v_s5_b_re, v_s5_b_im, v_s5_c_re, v_s5_c_im, v_s5_d, v_s5_glu_w, v_s5_glu_b, v_gla_a_up_w, v_gla_a_up_b, v_gla_out_norm, v_proj_s5, v_proj_gla, v_w_out, v_ffn2_norm, v_ffn2_w1, v_ffn2_w3, v_ffn2_w2, v_final_norm):
    return _train_step(dict(locals()))
```

```python
import functools

import jax
import jax.numpy as jnp
from jax import lax
from jax.experimental import pallas as pl
from jax.experimental.pallas import tpu as pltpu

F32 = jnp.float32
BF16 = jnp.bfloat16
HI = lax.Precision.HIGHEST
MESH_ID = pl.DeviceIdType.MESH

D_MODEL = 1024
EPS = 1e-6
S5_G, S5_P, S5_H = 32, 64, 16
S5_W = S5_G * S5_H
S5_GP = S5_G * S5_P
SEG = 8
SCAN_ROWS = 256
GLA_HEADS, GLA_DK, GLA_DV = 4, 64, 128
GLA_CHUNK = 64
GLA_TAU = 16.0
GLA_RANK = 16
ADAM_LR, ADAM_B1, ADAM_B2, ADAM_EPS, ADAM_WD, ADAM_STEP = 0.001, 0.9, 0.999, 1e-08, 0.01, 10
V7X_VMEM_LIMIT = 56 * 1024 * 1024
LANE = 128

WEIGHTS = ['ffn1_norm', 'ffn1_w1', 'ffn1_w3', 'ffn1_w2', 'mix_norm', 'w_in', 's5_lambda_re', 's5_lambda_im',
           's5_log_dt', 's5_b_re', 's5_b_im', 's5_c_re', 's5_c_im', 's5_d', 's5_glu_w', 's5_glu_b', 'gla_a_up_w',
           'gla_a_up_b', 'gla_out_norm', 'proj_s5', 'proj_gla', 'w_out', 'ffn2_norm', 'ffn2_w1', 'ffn2_w3',
           'ffn2_w2', 'final_norm']
SHARDED = ['ffn1_w1', 'ffn1_w3', 'ffn1_w2', 'w_in', 's5_glu_w', 'proj_s5', 'proj_gla', 'w_out',
           'ffn2_w1', 'ffn2_w3', 'ffn2_w2']
COL_SHARDED = ['ffn1_w1', 'ffn1_w3', 'w_in', 'proj_s5', 'proj_gla', 'ffn2_w1', 'ffn2_w3', 'gla_a_up_w']
SMALL = [n for n in WEIGHTS if n not in SHARDED]
FFN_WEIGHTS = ['ffn1_w1', 'ffn1_w3', 'ffn1_w2', 'ffn2_w1', 'ffn2_w3', 'ffn2_w2']


def _params(**kw):
    return pltpu.CompilerParams(vmem_limit_bytes=V7X_VMEM_LIMIT, **kw)


class _Side:
    def __init__(self, ins, out_shapes, nsem, copies, aliased=False):
        self.ins, self.out_shapes, self.nsem, self.copies, self.aliased = list(ins), list(out_shapes), nsem, copies, aliased
        self.outs = None


_RIDER = []


def _pcall(body, **kw):
    if _RIDER:
        return _carry(body, _RIDER.pop(), **kw)
    return pl.pallas_call(body, **kw)


def _carry(body, side, *, name, grid, in_specs, out_specs, out_shape, scratch_shapes=(), compiler_params=None):
    del compiler_params
    single = not isinstance(out_shape, (list, tuple))
    out_specs = [out_specs] if single else list(out_specs)
    out_shape = [out_shape] if single else list(out_shape)
    n_in, n_out, n_scr = len(in_specs), len(out_shape), len(scratch_shapes)
    s_in, s_out = len(side.ins), len(side.out_shapes)
    any_spec = pl.BlockSpec(memory_space=pl.ANY)

    def wrapped(*refs):
        cuts = [n_in, s_in, n_out, s_out, n_scr]
        parts, pos = [], 0
        for c in cuts:
            parts.append(refs[pos:pos + c])
            pos += c
        ins, sins, outs, souts, scr = parts
        ssem, rsem = refs[pos], refs[pos + 1]
        first = last = None
        for d, g in enumerate(grid):
            i = pl.program_id(d)
            first = (i == 0) if first is None else first & (i == 0)
            last = (i == g - 1) if last is None else last & (i == g - 1)

        @pl.when(first)
        def _():
            for cp in side.copies(sins, souts, ssem, rsem):
                cp.start()

        body(*ins, *outs, *scr)

        @pl.when(last)
        def _():
            for cp in side.copies(sins, souts, ssem, rsem):
                cp.wait()

    call = pl.pallas_call(
        wrapped, name=name, grid=grid, in_specs=list(in_specs) + [any_spec] * s_in,
        out_specs=out_specs + [any_spec] * s_out, out_shape=out_shape + side.out_shapes,
        scratch_shapes=list(scratch_shapes) + [pltpu.SemaphoreType.DMA((side.nsem,)), pltpu.SemaphoreType.DMA((side.nsem,))],
        input_output_aliases={n_in + j: n_out + j for j in range(s_in)} if side.aliased else {},
        compiler_params=_params(has_side_effects=True))

    def run(*args):
        res = call(*args, *side.ins)
        side.outs = list(res[n_out:])
        return res[0] if single else list(res[:n_out])

    return run


def _run_side(side, name):
    s_in, s_out = len(side.ins), len(side.out_shapes)
    any_spec = pl.BlockSpec(memory_space=pl.ANY)

    def body(*refs):
        sins, souts = refs[:s_in], refs[s_in:s_in + s_out]
        ssem, rsem = refs[s_in + s_out:]
        cps = side.copies(sins, souts, ssem, rsem)
        for cp in cps:
            cp.start()
        for cp in cps:
            cp.wait()

    side.outs = list(pl.pallas_call(
        body, name=name, in_specs=[any_spec] * s_in, out_specs=[any_spec] * s_out, out_shape=side.out_shapes,
        scratch_shapes=[pltpu.SemaphoreType.DMA((side.nsem,)), pltpu.SemaphoreType.DMA((side.nsem,))],
        input_output_aliases={j: j for j in range(s_in)} if side.aliased else {},
        compiler_params=pltpu.CompilerParams(has_side_effects=True))(*side.ins))
    return side.outs


def _pick(n, cap, quantum):
    if n <= cap:
        return n
    best = None
    for t in range(quantum, cap + 1, quantum):
        if n % t == 0:
            best = t
    assert best is not None, (n, cap, quantum)
    return best


def _sigmoid(x):
    return jax.nn.sigmoid(x)


def _mm(a, b, *, name, ta=False, tb=False, out_dtype=F32, alpha=1.0, res=None, bias=None, exact=False, shard=None):
    ns = 4
    (k_a, m) = a.shape[-2:] if ta else a.shape[-2:][::-1]
    (k_b, n) = b.shape[-2:][::-1] if tb else b.shape[-2:]
    assert k_a == k_b, (a.shape, b.shape, ta, tb)
    assert (a.ndim == 3) == (shard in ('k', 'm')) and (b.ndim == 3) == (shard in ('n', 'k'))
    k = k_a
    tm = _pick(m, 1024, 128)
    tn = _pick(n, 1024, 128)
    tk = _pick(k, 1024, 128)
    pm, pn, pk = m // tm, n // tn, k // tk
    gm = pm * (ns if shard == 'm' else 1)
    gn = pn * (ns if shard == 'n' else 1)
    gk = pk * (ns if shard == 'k' else 1)
    dims = (((0,) if ta else (1,), (1,) if tb else (0,)), ((), ()))
    op_dtype = F32 if exact else BF16

    def body(*refs):
        a_ref, b_ref = refs[0], refs[1]
        pos = 2
        res_ref = bias_ref = None
        if res is not None:
            res_ref = refs[pos]
            pos += 1
        if bias is not None:
            bias_ref = refs[pos]
            pos += 1
        o_ref, acc_ref = refs[pos], refs[pos + 1]
        kk = pl.program_id(2)

        @pl.when(kk == 0)
        def _():
            acc_ref[...] = jnp.zeros_like(acc_ref)

        acc_ref[...] += lax.dot_general(a_ref[...].astype(op_dtype), b_ref[...].astype(op_dtype), dims,
                                        precision=HI if exact else None, preferred_element_type=F32)

        @pl.when(kk == gk - 1)
        def _():
            o = acc_ref[...]
            if alpha != 1.0:
                o = o * alpha
            if bias_ref is not None:
                o = o + bias_ref[...]
            if res_ref is not None:
                o = o + res_ref[...]
            o_ref[...] = o.astype(out_dtype)

    def spec(block, sharded_on, order):
        per = {'m': pm, 'n': pn, 'k': pk}

        def index(i, j, kk):
            g = {'m': i, 'n': j, 'k': kk}
            r, c = order(i % pm if shard == 'm' else i, j % pn if shard == 'n' else j, kk % pk if shard == 'k' else kk)
            if sharded_on is None:
                return (r, c)
            return (g[sharded_on] // per[sharded_on], r, c)

        return pl.BlockSpec(block if sharded_on is None else (None,) + block, index)

    a_sh = shard if shard in ('k', 'm') else None
    b_sh = shard if shard in ('n', 'k') else None
    o_sh = shard if shard in ('n', 'm') else None
    a_spec = spec((tk, tm), a_sh, lambda i, j, kk: (kk, i)) if ta else spec((tm, tk), a_sh, lambda i, j, kk: (i, kk))
    b_spec = spec((tn, tk), b_sh, lambda i, j, kk: (j, kk)) if tb else spec((tk, tn), b_sh, lambda i, j, kk: (kk, j))
    ins, in_specs = [a, b], [a_spec, b_spec]
    if res is not None:
        assert o_sh is None
        ins.append(res)
        in_specs.append(pl.BlockSpec((tm, tn), lambda i, j, kk: (i, j)))
    if bias is not None:
        assert o_sh is None
        ins.append(bias)
        in_specs.append(pl.BlockSpec((1, tn), lambda i, j, kk: (0, j)))
    out_shape = (m, n) if o_sh is None else (ns, m, n)
    return _pcall(body, name=name, grid=(gm, gn, gk), in_specs=in_specs,
                  out_specs=spec((tm, tn), o_sh, lambda i, j, kk: (i, j)),
                  out_shape=jax.ShapeDtypeStruct(out_shape, out_dtype),
                  scratch_shapes=[pltpu.VMEM((tm, tn), F32)], compiler_params=_params())(*ins)


ROWS_VMEM_BUDGET = 24 * 1024 * 1024


def _rows(body, ins, outs, *, n, name):
    cols = sum(a.shape[1] for a, kind in ins if kind == 'r') + sum(c for c, _, kind in outs if kind == 'r')
    cap = 256
    while cap < 2048 and 2 * 4 * cols * (2 * cap) <= ROWS_VMEM_BUDGET:
        cap *= 2
    tm = _pick(n, cap, 16)
    in_specs = []
    for arr, kind in ins:
        if kind == 'r':
            in_specs.append(pl.BlockSpec((tm, arr.shape[1]), lambda i: (i, 0)))
        else:
            in_specs.append(pl.BlockSpec(arr.shape, lambda i: (0, 0)))
    out_specs, out_shape = [], []
    for cols, dtype, kind in outs:
        if kind == 'r':
            out_specs.append(pl.BlockSpec((tm, cols), lambda i: (i, 0)))
            out_shape.append(jax.ShapeDtypeStruct((n, cols), dtype))
        else:
            out_specs.append(pl.BlockSpec((1, cols), lambda i: (0, 0)))
            out_shape.append(jax.ShapeDtypeStruct((1, cols), dtype))
    n_in = len(ins)
    acc_ids = [j for j, o in enumerate(outs) if o[2] == 'a']

    def wrapped(*refs):
        if acc_ids:
            @pl.when(pl.program_id(0) == 0)
            def _():
                for j in acc_ids:
                    refs[n_in + j][...] = jnp.zeros_like(refs[n_in + j])
        body(*refs)

    res = _pcall(wrapped, name=name, grid=(n // tm,), in_specs=in_specs, out_specs=out_specs, out_shape=out_shape,
                 compiler_params=_params())(*[a for a, _ in ins])
    return res


def _rms_fwd(x, g, name):
    def body(x_ref, g_ref, o_ref):
        xv = x_ref[...]
        rstd = lax.rsqrt(jnp.mean(xv * xv, axis=-1, keepdims=True) + EPS)
        o_ref[...] = (xv * rstd * g_ref[...]).astype(BF16)
    return _rows(body, [(x, 'r'), (g, 'f')], [(x.shape[1], BF16, 'r')], n=x.shape[0], name=name)[0]


def _rms_bwd(x, g, dn, dres, name):
    def body(x_ref, g_ref, dn_ref, dres_ref, dx_ref, dg_ref):
        xv = x_ref[...]
        rstd = lax.rsqrt(jnp.mean(xv * xv, axis=-1, keepdims=True) + EPS)
        xh = xv * rstd
        dn = dn_ref[...]
        dg_ref[...] += jnp.sum(dn * xh, axis=0, keepdims=True)
        dxh = dn * g_ref[...]
        dx_ref[...] = dres_ref[...] + rstd * (dxh - xh * jnp.mean(dxh * xh, axis=-1, keepdims=True))
    d = x.shape[1]
    return _rows(body, [(x, 'r'), (g, 'f'), (dn, 'r'), (dres, 'r')], [(d, F32, 'r'), (d, F32, 'a')],
                 n=x.shape[0], name=name)


def _gelu_parts(y):
    c0 = 0.7978845608028654
    inner = c0 * (y + 0.044715 * y * y * y)
    th = jnp.tanh(inner)
    return th, c0 * (1.0 + 3.0 * 0.044715 * y * y)


def _gelu_fwd(y, name):
    def body(y_ref, o_ref):
        yv = y_ref[...]
        th, _ = _gelu_parts(yv)
        o_ref[...] = 0.5 * yv * (1.0 + th)
    return _rows(body, [(y, 'r')], [(y.shape[1], F32, 'r')], n=y.shape[0], name=name)[0]


def _glu_fwd(zg, t, name):
    def body(z_ref, t_ref, o_ref):
        o_ref[...] = (z_ref[...] * _sigmoid(t_ref[...])).astype(BF16)
    return _rows(body, [(zg, 'r'), (t, 'r')], [(zg.shape[1], BF16, 'r')], n=zg.shape[0], name=name)[0]


def _glu_bwd1(dy, zg, t, name):
    def body(dy_ref, z_ref, t_ref, dz_ref, dt_ref, db_ref):
        dyv, zv = dy_ref[...], z_ref[...]
        sg = _sigmoid(t_ref[...])
        dz_ref[...] = dyv * sg
        dt = dyv * zv * sg * (1.0 - sg)
        dt_ref[...] = dt.astype(BF16)
        db_ref[...] += jnp.sum(dt, axis=0, keepdims=True)
    w = zg.shape[1]
    return _rows(body, [(dy, 'r'), (zg, 'r'), (t, 'r')], [(w, F32, 'r'), (w, BF16, 'r'), (w, F32, 'a')],
                 n=zg.shape[0], name=name)


def _glu_bwd2(dzg, ys, u, dskip, name):
    def body(dz_ref, y_ref, u_ref, d_ref, dy_ref, du_ref, dd_ref):
        yv = y_ref[...]
        th, dinner = _gelu_parts(yv)
        dy = dz_ref[...] * (0.5 * (1.0 + th) + 0.5 * yv * (1.0 - th * th) * dinner)
        dy_ref[...] = dy
        du_ref[...] = dy * d_ref[...]
        dd_ref[...] += jnp.sum(dy * u_ref[...], axis=0, keepdims=True)
    w = ys.shape[1]
    return _rows(body, [(dzg, 'r'), (ys, 'r'), (u, 'r'), (dskip, 'f')], [(w, F32, 'r'), (w, F32, 'r'), (w, F32, 'a')],
                 n=ys.shape[0], name=name)


def _scale_rows(u, dskip, name):
    def body(u_ref, d_ref, o_ref):
        o_ref[...] = u_ref[...] * d_ref[...]
    return _rows(body, [(u, 'r'), (dskip, 'f')], [(u.shape[1], F32, 'r')], n=u.shape[0], name=name)[0]


def _merge_fwd(zg, ps, pg, name):
    def body(z_ref, ps_ref, pg_ref, o_ref):
        zv = z_ref[...]
        o_ref[...] = (_sigmoid(zv[:, :D_MODEL]) * ps_ref[...] + _sigmoid(zv[:, D_MODEL:]) * pg_ref[...]).astype(BF16)
    return _rows(body, [(zg, 'r'), (ps, 'r'), (pg, 'r')], [(D_MODEL, BF16, 'r')], n=zg.shape[0], name=name)[0]


def _merge_bwd(dm, zg, ps, pg, name):
    def body(dm_ref, z_ref, ps_ref, pg_ref, dps_ref, dpg_ref, dz_ref):
        dmv, zv = dm_ref[...], z_ref[...]
        s1, s2 = _sigmoid(zv[:, :D_MODEL]), _sigmoid(zv[:, D_MODEL:])
        dps_ref[...] = (dmv * s1).astype(BF16)
        dpg_ref[...] = (dmv * s2).astype(BF16)
        dz_ref[:, :D_MODEL] = dmv * ps_ref[...] * s1 * (1.0 - s1)
        dz_ref[:, D_MODEL:] = dmv * pg_ref[...] * s2 * (1.0 - s2)
    return _rows(body, [(dm, 'r'), (zg, 'r'), (ps, 'r'), (pg, 'r')],
                 [(D_MODEL, BF16, 'r'), (D_MODEL, BF16, 'r'), (2 * D_MODEL, F32, 'r')], n=zg.shape[0], name=name)


def _final_loss(h, g, tgt, name):
    def body(h_ref, g_ref, t_ref, loss_ref, dh_ref, dg_ref):
        hv = h_ref[...]
        rstd = lax.rsqrt(jnp.mean(hv * hv, axis=-1, keepdims=True) + EPS)
        xh = hv * rstd
        err = xh * g_ref[...] - t_ref[...]
        part = 0.5 * jnp.sum(jnp.mean(err * err, axis=-1, keepdims=True), axis=0, keepdims=True)
        loss_ref[...] += jnp.broadcast_to(part, loss_ref.shape)
        dout = err * (1.0 / hv.shape[1])
        dg_ref[...] += jnp.sum(dout * xh, axis=0, keepdims=True)
        dxh = dout * g_ref[...]
        dh_ref[...] = rstd * (dxh - xh * jnp.mean(dxh * xh, axis=-1, keepdims=True))
    d = h.shape[1]
    return _rows(body, [(h, 'r'), (g, 'f'), (tgt, 'r')], [(LANE, F32, 'a'), (d, F32, 'r'), (d, F32, 'a')],
                 n=h.shape[0], name=name)


def _adamw_math(wv, gv, mv, vv):
    nm = ADAM_B1 * mv + (1.0 - ADAM_B1) * gv
    nv = ADAM_B2 * vv + (1.0 - ADAM_B2) * (gv * gv)
    m_hat = nm / (1.0 - ADAM_B1 ** ADAM_STEP)
    v_hat = nv / (1.0 - ADAM_B2 ** ADAM_STEP)
    return -ADAM_LR * (m_hat / (jnp.sqrt(v_hat) + ADAM_EPS) + ADAM_WD * wv), nm, nv


def _adamw(w, g, m, v, name):
    def body(w_ref, g_ref, m_ref, v_ref, d_ref, nm_ref, nv_ref):
        d_ref[...], nm_ref[...], nv_ref[...] = _adamw_math(w_ref[...], g_ref[...], m_ref[...], v_ref[...])
    c = w.shape[1]
    return _rows(body, [(w, 'r'), (g, 'r'), (m, 'r'), (v, 'r')], [(c, F32, 'r')] * 3, n=w.shape[0], name=name)


ADAMW_BLOCKS = 8


def _adamw_group(items, c_arr, name):
    per = ADAMW_BLOCKS // 2
    n = len(items)

    def body(c_ref, *refs):
        mine = (pl.program_id(0) // per) == c_ref[0]
        for k in range(n):
            w_ref, go_ref, gs_ref, m_ref, v_ref = refs[5 * k:5 * k + 5]
            g_ref, d_ref, nm_ref, nv_ref = refs[5 * n + 4 * k:5 * n + 4 * k + 4]
            gv = jnp.where(mine, go_ref[...], gs_ref[...])
            g_ref[...] = gv
            d_ref[...], nm_ref[...], nv_ref[...] = _adamw_math(w_ref[...], gv, m_ref[...], v_ref[...])

    in_specs, out_specs, out_shape, args = [pl.BlockSpec(memory_space=pltpu.SMEM)], [], [], [c_arr]
    for item in items:
        r, cols = item[0].shape
        assert r % (8 * ADAMW_BLOCKS) == 0, item[0].shape
        tr = r // ADAMW_BLOCKS
        full = pl.BlockSpec((tr, cols), lambda i: (i, 0))
        half = pl.BlockSpec((tr, cols), lambda i: (i % per, 0))
        in_specs += [full, half, half, full, full]
        out_specs += [full] * 4
        out_shape += [jax.ShapeDtypeStruct((r, cols), F32)] * 4
        args += list(item)
    return _pcall(body, name=name, grid=(ADAMW_BLOCKS,), in_specs=in_specs, out_specs=out_specs, out_shape=out_shape,
                  compiler_params=_params())(*args)


def _shift_rows(v, sh, down):
    rolled = pltpu.roll(v, sh if down else v.shape[0] - sh, axis=0)
    row = lax.broadcasted_iota(jnp.int32, v.shape, 0)
    keep = (row >= sh) if down else (row < v.shape[0] - sh)
    return jnp.where(keep, rolled, 0.0)


def _chain_segments(st_r, st_i, pw_r_ref, pw_i_ref, conj, down):
    vr, vi = st_r[...], st_i[...]
    sh, k = 1, 0
    while sh < SEG:
        pr, pi = pw_r_ref[k:k + 1, :], pw_i_ref[k:k + 1, :]
        if conj:
            pi = -pi
        sr, si = _shift_rows(vr, sh, down), _shift_rows(vi, sh, down)
        vr, vi = vr + pr * sr - pi * si, vi + pr * si + pi * sr
        sh, k = sh * 2, k + 1
    st_r[...] = _shift_rows(vr, 1, down)
    st_i[...] = _shift_rows(vi, 1, down)


def _expand_block(u_ref, t_ref, bu_ref):
    for j in range(BD_TILES):
        k = j % 4
        bu_ref[:, j * BD_ST:(j + 1) * BD_ST] = _dot(u_ref[:, k * BD_CH:(k + 1) * BD_CH], t_ref[j])


def _s5_scan(u, tiles, ar8, ai8, pw_r, pw_i, name):
    n = u.shape[0]
    rb = SCAN_ROWS
    nb, steps, lc = n // rb, rb // SEG, 512

    def body(u_ref, t_ref, ar_ref, ai_ref, pwr_ref, pwi_ref, x_ref, st_r, st_i, bu_ref):
        ph, b = pl.program_id(0), pl.program_id(1)

        @pl.when((ph == 0) & (b == 0))
        def _():
            st_r[...] = jnp.zeros_like(st_r)
            st_i[...] = jnp.zeros_like(st_i)

        _expand_block(u_ref, t_ref, bu_ref)

        def scan(store):
            for c in range(S5_GP // lc):
                re, im = slice(c * lc, (c + 1) * lc), slice(S5_GP + c * lc, S5_GP + (c + 1) * lc)
                a_r, a_i = ar_ref[:, re], ai_ref[:, re]

                def step(s, carry):
                    xr, xi = carry
                    rows = pl.ds(pl.multiple_of(s * SEG, SEG), SEG)
                    nr = a_r * xr - a_i * xi + bu_ref[rows, re]
                    ni = a_r * xi + a_i * xr + bu_ref[rows, im]
                    if store:
                        x_ref[rows, re] = nr
                        x_ref[rows, im] = ni
                    return nr, ni

                xr, xi = lax.fori_loop(0, steps, step, (st_r[:, re], st_i[:, re]), unroll=4)
                st_r[:, re] = xr
                st_i[:, re] = xi

        @pl.when(ph == 0)
        def _():
            scan(False)

        @pl.when((ph == 0) & (b == nb - 1))
        def _():
            _chain_segments(st_r, st_i, pwr_ref, pwi_ref, conj=False, down=True)

        @pl.when(ph == 1)
        def _():
            scan(True)

    full = lambda a: pl.BlockSpec(a.shape, lambda ph, b: (0, 0))
    return _pcall(body, name=name, grid=(2, nb),
                  in_specs=[pl.BlockSpec((rb, S5_W), lambda ph, b: (b, 0)), pl.BlockSpec(tiles.shape, lambda ph, b: (0, 0, 0)),
                            full(ar8), full(ai8), full(pw_r), full(pw_i)],
                  out_specs=pl.BlockSpec((rb, 2 * S5_GP), lambda ph, b: (b * ph, 0)),
                  out_shape=jax.ShapeDtypeStruct((n, 2 * S5_GP), F32),
                  scratch_shapes=[pltpu.VMEM((SEG, S5_GP), F32), pltpu.VMEM((SEG, S5_GP), F32),
                                  pltpu.VMEM((rb, 2 * S5_GP), F32)],
                  compiler_params=_params())(u, tiles, ar8, ai8, pw_r, pw_i)


def _s5_scan_bwd(dy, tiles, xs, ar8, ai8, pw_r, pw_i, name):
    n = dy.shape[0]
    rb = SCAN_ROWS
    nb, steps, lc = n // rb, rb // SEG, 256

    def body(dy_ref, t_ref, x_ref, ar_ref, ai_ref, pwr_ref, pwi_ref, lam_ref, da_ref, st_r, st_i, gx_ref):
        ph, b = pl.program_id(0), pl.program_id(1)

        @pl.when((ph == 0) & (b == 0))
        def _():
            st_r[...] = jnp.zeros_like(st_r)
            st_i[...] = jnp.zeros_like(st_i)
            da_ref[...] = jnp.zeros_like(da_ref)

        _expand_block(dy_ref, t_ref, gx_ref)

        def scan(store):
            for c in range(S5_GP // lc):
                re, im = slice(c * lc, (c + 1) * lc), slice(S5_GP + c * lc, S5_GP + (c + 1) * lc)
                a_r, a_i = ar_ref[:, re], ai_ref[:, re]

                def step(s, carry):
                    rows = pl.ds(pl.multiple_of((steps - 1 - s) * SEG, SEG), SEG)
                    if store:
                        lr, li, dr, di = carry
                        xr, xi = x_ref[rows, re], x_ref[rows, im]
                        dr = dr + lr * xr + li * xi
                        di = di + li * xr - lr * xi
                    else:
                        lr, li = carry
                    nr = a_r * lr + a_i * li + gx_ref[rows, re]
                    ni = a_r * li - a_i * lr + gx_ref[rows, im]
                    if store:
                        lam_ref[rows, re] = nr
                        lam_ref[rows, im] = ni
                        return nr, ni, dr, di
                    return nr, ni

                if store:
                    lr, li, dr, di = lax.fori_loop(0, steps, step, (st_r[:, re], st_i[:, re], da_ref[:, re], da_ref[:, im]),
                                                   unroll=4)
                    da_ref[:, re] = dr
                    da_ref[:, im] = di
                else:
                    lr, li = lax.fori_loop(0, steps, step, (st_r[:, re], st_i[:, re]), unroll=4)
                st_r[:, re] = lr
                st_i[:, re] = li

        @pl.when(ph == 0)
        def _():
            scan(False)

        @pl.when((ph == 0) & (b == nb - 1))
        def _():
            _chain_segments(st_r, st_i, pwr_ref, pwi_ref, conj=True, down=False)

        @pl.when(ph == 1)
        def _():
            scan(True)

    full = lambda a: pl.BlockSpec(a.shape, lambda ph, b: (0, 0))
    rev = lambda ph, b: (nb - 1 - b, 0)
    return _pcall(body, name=name, grid=(2, nb),
                  in_specs=[pl.BlockSpec((rb, S5_W), rev), pl.BlockSpec(tiles.shape, lambda ph, b: (0, 0, 0)),
                            pl.BlockSpec((rb, 2 * S5_GP), lambda ph, b: ((nb - 1 - b) * ph, 0)),
                            full(ar8), full(ai8), full(pw_r), full(pw_i)],
                  out_specs=[pl.BlockSpec((rb, 2 * S5_GP), lambda ph, b: (nb - 1 - b * ph, 0)),
                             pl.BlockSpec((SEG, 2 * S5_GP), lambda ph, b: (0, 0))],
                  out_shape=[jax.ShapeDtypeStruct((n, 2 * S5_GP), F32), jax.ShapeDtypeStruct((SEG, 2 * S5_GP), F32)],
                  scratch_shapes=[pltpu.VMEM((SEG, S5_GP), F32), pltpu.VMEM((SEG, S5_GP), F32),
                                  pltpu.VMEM((rb, 2 * S5_GP), F32)],
                  compiler_params=_params())(dy, tiles, xs, ar8, ai8, pw_r, pw_i)


def _s5_discretize(lam_re, lam_im, log_dt, b_re, b_im):
    dt = jnp.exp(log_dt)[:, None]
    mag = jnp.exp(lam_re * dt)
    ar = mag * jnp.cos(lam_im * dt)
    ai = mag * jnp.sin(lam_im * dt)
    den = lam_re * lam_re + lam_im * lam_im
    nr = ar - 1.0
    fr = (nr * lam_re + ai * lam_im) / den
    fi = (ai * lam_re - nr * lam_im) / den
    bbar_re = fr[:, :, None] * b_re - fi[:, :, None] * b_im
    bbar_im = fr[:, :, None] * b_im + fi[:, :, None] * b_re
    return ar, ai, bbar_re, bbar_im


BD_TILES, BD_CH, BD_ST, BD_GROUPS = 8, 128, 512, 8
BD_ROWS = 4096


def _bd_tiles(re, im):
    eye = jnp.eye(BD_GROUPS, dtype=re.dtype)

    def tiles(t):
        t = t.reshape(S5_G // BD_GROUPS, BD_GROUPS, S5_H, S5_P)
        return (t[:, :, :, None, :] * eye[None, :, None, :, None]).reshape(S5_G // BD_GROUPS, BD_CH, BD_ST)

    return jnp.concatenate([tiles(re), tiles(im)], axis=0)


def _bd_blocks(t):
    t = t.reshape(2, S5_G // BD_GROUPS, BD_GROUPS, S5_H, BD_GROUPS, S5_P)
    return jnp.einsum('rkahap->rkahp', t).reshape(2, S5_G, S5_H, S5_P)


def _bd_reduce(x, t, res, name):
    n = x.shape[0]
    tm = _pick(n, BD_ROWS, 16)

    def body(x_ref, t_ref, r_ref, o_ref):
        part = _dot(x_ref[...], t_ref[...], NT)

        @pl.when(pl.program_id(2) == 0)
        def _():
            o_ref[...] = r_ref[...] + part

        @pl.when(pl.program_id(2) == 1)
        def _():
            o_ref[...] += part

    return _pcall(body, name=name, grid=(n // tm, 4, 2),
                  in_specs=[pl.BlockSpec((tm, BD_ST), lambda i, k, r: (i, k + 4 * r)),
                            pl.BlockSpec((None, BD_CH, BD_ST), lambda i, k, r: (k + 4 * r, 0, 0)),
                            pl.BlockSpec((tm, BD_CH), lambda i, k, r: (i, k))],
                  out_specs=pl.BlockSpec((tm, BD_CH), lambda i, k, r: (i, k)),
                  out_shape=jax.ShapeDtypeStruct((n, S5_W), F32), compiler_params=_params())(x, t, res)


def _bd_outer(a, x, name):
    n = a.shape[0]
    tk = _pick(n, BD_ROWS, 16)
    nk = n // tk

    def body(a_ref, x_ref, o_ref):
        part = _dot(a_ref[...], x_ref[...], TN)

        @pl.when(pl.program_id(1) == 0)
        def _():
            o_ref[...] = part

        @pl.when(pl.program_id(1) > 0)
        def _():
            o_ref[...] += part

    return _pcall(body, name=name, grid=(BD_TILES, nk),
                  in_specs=[pl.BlockSpec((tk, BD_CH), lambda j, kk: (kk, j % 4)), pl.BlockSpec((tk, BD_ST), lambda j, kk: (kk, j))],
                  out_specs=pl.BlockSpec((None, BD_CH, BD_ST), lambda j, kk: (j, 0, 0)),
                  out_shape=jax.ShapeDtypeStruct((BD_TILES, BD_CH, BD_ST), F32), compiler_params=_params())(a, x)


def _permute_rows(t):
    n = t.shape[0]
    return t.reshape(SEG, n // SEG, t.shape[1]).transpose(1, 0, 2).reshape(n, t.shape[1])


def _unpermute_rows(t):
    n = t.shape[0]
    return t.reshape(n // SEG, SEG, t.shape[1]).transpose(1, 0, 2).reshape(n, t.shape[1])


def _segment_powers(ar, ai, seg_steps):
    pr, pi = ar.reshape(1, S5_GP), ai.reshape(1, S5_GP)
    e = 1
    while e < seg_steps:
        pr, pi = pr * pr - pi * pi, 2.0 * pr * pi
        e *= 2
    assert e == seg_steps, "segment length must be a power of two"
    rows_r, rows_i = [], []
    for _ in range(3):
        rows_r.append(pr)
        rows_i.append(pi)
        pr, pi = pr * pr - pi * pi, 2.0 * pr * pi
    pad = jnp.zeros((SEG - 3, S5_GP), F32)
    return jnp.concatenate(rows_r + [pad], axis=0), jnp.concatenate(rows_i + [pad], axis=0)


NT = (((1,), (1,)), ((), ()))
TN = (((0,), (0,)), ((), ()))


def _dot(a, b, dims=None, exact=False):
    dims = (((1,), (0,)), ((), ())) if dims is None else dims
    if exact:
        return lax.dot_general(a, b, dims, precision=HI, preferred_element_type=F32)
    return lax.dot_general(a.astype(BF16), b.astype(BF16), dims, preferred_element_type=F32)


def _dot01(a, b, dims=None, ones_first=True):
    x = b if ones_first else a
    hi = x.astype(BF16)
    lo = (x - hi.astype(F32)).astype(BF16)
    parts = [(_dot(a, p, dims) if ones_first else _dot(p, b, dims)) for p in (lo, hi)]
    return parts[0] + parts[1]


HEADS = range(4)


def _gla_chunk_fwd(qc, kc, vc, al, wup, bup, s_prev, tril):
    ones = jnp.ones((GLA_CHUNK, GLA_DV), F32)
    z = [_dot(al, wup[h]) + bup[h] for h in HEADS]
    la = [(jnp.minimum(z[h], 0.0) - jnp.log(1.0 + jnp.exp(-jnp.abs(z[h])))) * (1.0 / GLA_TAU) for h in HEADS]
    bc = [_dot01(tril, la[h]) for h in HEADS]
    blb = [_dot01(la[h], ones, TN, ones_first=False) for h in HEADS]
    bl = [bc[h][GLA_CHUNK - 1:GLA_CHUNK, :] for h in HEADS]
    ebc = [jnp.exp(bc[h]) for h in HEADS]
    qt = [qc[h] * (GLA_DK ** -0.5) * ebc[h] for h in HEADS]
    kt = [kc[h] * jnp.exp(-bc[h]) for h in HEADS]
    ke = [kc[h] * jnp.exp(bl[h] - bc[h]) for h in HEADS]
    sc = [_dot(qt[h], kt[h], NT) * tril for h in HEADS]
    oi = [_dot(sc[h], vc[h]) for h in HEADS]
    oo = [_dot(qt[h], s_prev[h]) for h in HEADS]
    o = [oi[h] + oo[h] for h in HEADS]
    return z, bc, bl, blb, ebc, qt, kt, ke, sc, o


GLA_ROWS = 512
GLA_CPB = GLA_ROWS // GLA_CHUNK


ZA_COLS = 5 * 512
SLOT = 128


def _pad_heads(w):
    r = w.shape[0]
    return jnp.pad(w.reshape(r, GLA_HEADS, GLA_DK), ((0, 0), (0, 0), (0, SLOT - GLA_DK))).reshape(r, GLA_HEADS * SLOT)


def _unpad_heads(w):
    r = w.shape[0]
    return w.reshape(r, GLA_HEADS, SLOT)[:, :, :GLA_DK].reshape(r, GLA_HEADS * GLA_DK)


def _gla_token_specs(blk):
    col = lambda cb: pl.BlockSpec((GLA_ROWS, 512), lambda j: (blk(j), cb))
    whole = lambda a: pl.BlockSpec(a.shape, lambda j: (0,) * a.ndim)
    return col, whole


def _head_ds(h, width):
    return pl.ds(h * SLOT, width)


def _tri(lower):
    ri = lax.broadcasted_iota(jnp.int32, (GLA_CHUNK, GLA_CHUNK), 0)
    ci = lax.broadcasted_iota(jnp.int32, (GLA_CHUNK, GLA_CHUNK), 1)
    return ((ri >= ci) if lower else (ri <= ci)).astype(F32)


def _gla_fwd(za, al, wup, bup, gn, name):
    n = za.shape[0]
    nc = n // GLA_CHUNK

    def body(q_ref, k_ref, v_ref, r_ref, al_ref, wup_ref, bup_ref, gn_ref, y_ref, sp_ref, s_ref):
        @pl.when(pl.program_id(0) == 0)
        def _():
            s_ref[...] = jnp.zeros_like(s_ref)

        tril = _tri(True)

        def chunk(c, carry):
            rows = pl.ds(pl.multiple_of(c * GLA_CHUNK, GLA_CHUNK), GLA_CHUNK)
            alc = al_ref[rows, :]
            vc = [v_ref[rows, _head_ds(h, GLA_DV)] for h in HEADS]
            s_prev = [s_ref[h] for h in HEADS]
            _, _, _, blb, _, _, _, ke, _, o = _gla_chunk_fwd(
                [q_ref[rows, _head_ds(h, GLA_DK)] for h in HEADS], [k_ref[rows, _head_ds(h, GLA_DK)] for h in HEADS],
                vc, alc, [wup_ref[h] for h in HEADS], [bup_ref[h] for h in HEADS], s_prev, tril)
            ds = [_dot(ke[h], vc[h], TN) for h in HEADS]
            for h in HEADS:
                rc = r_ref[rows, _head_ds(h, GLA_DV)]
                sp_ref[h, c] = s_prev[h]
                rstd = lax.rsqrt(jnp.mean(o[h] * o[h], axis=-1, keepdims=True) + EPS)
                y_ref[rows, _head_ds(h, GLA_DV)] = (o[h] * rstd * gn_ref[h] * (rc * _sigmoid(rc))).astype(BF16)
                s_ref[h] = jnp.exp(blb[h]) * s_prev[h] + ds[h]
            return carry

        lax.fori_loop(0, GLA_CPB, chunk, 0)

    col, whole = _gla_token_specs(lambda j: j)
    return _pcall(body, name=name, grid=(n // GLA_ROWS,),
                  in_specs=[col(1), col(2), col(3), col(4), pl.BlockSpec((GLA_ROWS, LANE), lambda j: (j, 0)),
                            whole(wup), whole(bup), whole(gn)],
                  out_specs=[pl.BlockSpec((GLA_ROWS, GLA_HEADS * GLA_DV), lambda j: (j, 0)),
                             pl.BlockSpec((GLA_HEADS, GLA_CPB, GLA_DK, GLA_DV), lambda j: (0, j, 0, 0))],
                  out_shape=[jax.ShapeDtypeStruct((n, GLA_HEADS * GLA_DV), BF16),
                             jax.ShapeDtypeStruct((GLA_HEADS, nc, GLA_DK, GLA_DV), F32)],
                  scratch_shapes=[pltpu.VMEM((GLA_HEADS, GLA_DK, GLA_DV), F32)],
                  compiler_params=_params())(za, za, za, za, al, wup, bup, gn)


def _gla_bwd(za, al, wup, bup, gn, sp, dy, du_s5, name):
    n = za.shape[0]
    nb = n // GLA_ROWS

    def body(q_ref, k_ref, v_ref, r_ref, al_ref, wup_ref, bup_ref, gn_ref, dy_ref, dus_ref, sp_ref,
             dza_ref, dz_ref, dgn_ref, dbup_ref, ds_ref):
        @pl.when(pl.program_id(0) == 0)
        def _():
            ds_ref[...] = jnp.zeros_like(ds_ref)
            dgn_ref[...] = jnp.zeros_like(dgn_ref)
            dbup_ref[...] = jnp.zeros_like(dbup_ref)

        tril, triu = _tri(True), _tri(False)
        dza_ref[:, 0:512] = dus_ref[...]
        dza_ref[:, 512:1536] = jnp.zeros((GLA_ROWS, 1024), F32)
        dz_ref[...] = jnp.zeros_like(dz_ref)

        def chunk(i, carry):
            c = GLA_CPB - 1 - i
            rows = pl.ds(pl.multiple_of(c * GLA_CHUNK, GLA_CHUNK), GLA_CHUNK)
            alc = al_ref[rows, :]
            qc = [q_ref[rows, _head_ds(h, GLA_DK)] for h in HEADS]
            kc = [k_ref[rows, _head_ds(h, GLA_DK)] for h in HEADS]
            vc = [v_ref[rows, _head_ds(h, GLA_DV)] for h in HEADS]
            s_prev = [sp_ref[h, c] for h in HEADS]
            ds = [ds_ref[h] for h in HEADS]
            z, bc, bl, blb, ebc, qt, kt, ke, sc, o = _gla_chunk_fwd(
                qc, kc, vc, alc, [wup_ref[h] for h in HEADS], [bup_ref[h] for h in HEADS], s_prev, tril)
            do = []
            for h in HEADS:
                rc = r_ref[rows, _head_ds(h, GLA_DV)]
                rs = lax.rsqrt(jnp.mean(o[h] * o[h], axis=-1, keepdims=True) + EPS)
                on = o[h] * rs
                sr = _sigmoid(rc)
                sil = rc * sr
                dyv, gnv = dy_ref[rows, _head_ds(h, GLA_DV)], gn_ref[h]
                dgn_ref[h] += jnp.sum(dyv * on * sil, axis=0, keepdims=True)
                dza_ref[rows, pl.ds(2048 + h * SLOT, GLA_DV)] = dyv * on * gnv * (sr * (1.0 + rc * (1.0 - sr)))
                don = dyv * gnv * sil
                do.append(rs * (don - on * jnp.mean(don * on, axis=-1, keepdims=True)))
            dp = [_dot(do[h], vc[h], NT) * tril for h in HEADS]
            dv1 = [_dot(sc[h], do[h], TN) for h in HEADS]
            dv2 = [_dot(ke[h], ds[h]) for h in HEADS]
            dq2 = [_dot(do[h], s_prev[h], NT) for h in HEADS]
            dke = [_dot(vc[h], ds[h], NT) for h in HEADS]
            ddec = [_dot01(jnp.ones((8, GLA_DV), F32), ds[h] * s_prev[h], NT)[0:1, :] for h in HEADS]
            dsn = [_dot(qt[h], do[h], TN) for h in HEADS]
            dq1 = [_dot(dp[h], kt[h]) for h in HEADS]
            dkt = [_dot(dp[h], qt[h], TN) for h in HEADS]
            dbc, dbl = [], []
            for h in HEADS:
                dqt = dq1[h] + dq2[h]
                dza_ref[rows, pl.ds(1536 + h * SLOT, GLA_DV)] = dv1[h] + dv2[h]
                ds_ref[h] = jnp.exp(blb[h]) * ds[h] + dsn[h]
                dza_ref[rows, pl.ds(512 + h * SLOT, GLA_DK)] = dqt * (GLA_DK ** -0.5) * ebc[h]
                dza_ref[rows, pl.ds(1024 + h * SLOT, GLA_DK)] = dkt[h] * jnp.exp(-bc[h]) + dke[h] * jnp.exp(bl[h] - bc[h])
                dbc.append(dqt * qt[h] - dkt[h] * kt[h] - dke[h] * ke[h])
                dbl.append(jnp.sum(dke[h] * ke[h], axis=0, keepdims=True) + ddec[h] * jnp.exp(bl[h]))
            dla = [_dot01(triu, dbc[h]) + dbl[h] for h in HEADS]
            for h in HEADS:
                dz = dla[h] * (1.0 - _sigmoid(z[h])) * (1.0 / GLA_TAU)
                dz_ref[rows, _head_ds(h, GLA_DK)] = dz
                dbup_ref[h] += jnp.sum(dz, axis=0, keepdims=True)
            return carry

        lax.fori_loop(0, GLA_CPB, chunk, 0)

    rev = lambda j: nb - 1 - j
    col, whole = _gla_token_specs(rev)
    tok = lambda w: pl.BlockSpec((GLA_ROWS, w), lambda j: (rev(j), 0))
    h1 = lambda w: pl.BlockSpec((GLA_HEADS, 1, w), lambda j: (0, 0, 0))
    s1 = lambda w: jax.ShapeDtypeStruct((GLA_HEADS, 1, w), F32)
    return _pcall(body, name=name, grid=(nb,),
                  in_specs=[col(1), col(2), col(3), col(4), tok(LANE), whole(wup), whole(bup), whole(gn), tok(512), tok(512),
                            pl.BlockSpec((GLA_HEADS, GLA_CPB, GLA_DK, GLA_DV), lambda j: (0, rev(j), 0, 0))],
                  out_specs=[tok(ZA_COLS), tok(GLA_HEADS * SLOT), h1(GLA_DV), h1(GLA_DK)],
                  out_shape=[jax.ShapeDtypeStruct((n, ZA_COLS), F32), jax.ShapeDtypeStruct((n, GLA_HEADS * SLOT), F32),
                             s1(GLA_DV), s1(GLA_DK)],
                  scratch_shapes=[pltpu.VMEM((GLA_HEADS, GLA_DK, GLA_DV), F32)],
                  compiler_params=_params())(za, za, za, za, al, wup, bup, gn, dy, du_s5, sp)


ANY = pl.BlockSpec(memory_space=pl.ANY)


def _place():
    x, y, c = lax.axis_index("x"), lax.axis_index("y"), lax.axis_index("c")
    chips = [(1 - x, y), (x, 1 - y), (1 - x, 1 - y)]
    return x, y, c, chips


def _remote(src, dst, ssem, rsem, dev):
    return pltpu.make_async_remote_copy(src_ref=src, dst_ref=dst, send_sem=ssem, recv_sem=rsem, device_id=dev,
                                        device_id_type=MESH_ID)


def _half(c, rows):
    h = rows // 2
    return pl.ds(pl.multiple_of(c * h, 8), h)


def _side_gather_ici(shards):
    def copies(ins, outs, ssem, rsem):
        x, y, c, chips = _place()
        mine = 2 * x + y
        cps = []
        for w in range(len(ins)):
            half = _half(c, ins[w].shape[0])
            cps.append(_remote(ins[w], outs[w].at[mine], ssem.at[4 * w], rsem.at[4 * w], (x, y, 1 - c)))
            for k, (px, py) in enumerate(chips):
                cps.append(_remote(ins[w].at[half], outs[w].at[mine, half], ssem.at[4 * w + 1 + k], rsem.at[4 * w + 1 + k],
                                   (px, py, c)))
        return cps

    return _Side(shards, [jax.ShapeDtypeStruct((4,) + s.shape, s.dtype) for s in shards], 4 * len(shards), copies)


def _side_gather_d2d(gathered):
    def copies(ins, outs, ssem, rsem):
        x, y, c, chips = _place()
        cps = []
        for w in range(len(outs)):
            half = _half(c, outs[w].shape[1])
            for k, (px, py) in enumerate(chips):
                theirs = outs[w].at[2 * px + py, half]
                cps.append(_remote(theirs, theirs, ssem.at[3 * w + k], rsem.at[3 * w + k], (x, y, 1 - c)))
        return cps

    return _Side(gathered, [jax.ShapeDtypeStruct(g.shape, g.dtype) for g in gathered], 3 * len(gathered), copies,
                 aliased=True)


def _side_swap_halves(grads):
    def copies(ins, outs, ssem, rsem):
        x, y, c, _ = _place()
        return [_remote(ins[w].at[:, _half(1 - c, ins[w].shape[1]), :], outs[w], ssem.at[w], rsem.at[w], (x, y, 1 - c))
                for w in range(len(ins))]

    return _Side(grads, [jax.ShapeDtypeStruct((4, g.shape[1] // 2, g.shape[2]), g.dtype) for g in grads], len(grads), copies)


def _side_scatter(sums):
    def copies(ins, outs, ssem, rsem):
        x, y, c, chips = _place()
        return [_remote(ins[w].at[2 * px + py], outs[w].at[k], ssem.at[3 * w + k], rsem.at[3 * w + k], (px, py, c))
                for w in range(len(ins)) for k, (px, py) in enumerate(chips)]

    return _Side(sums, [jax.ShapeDtypeStruct((3,) + s.shape[1:], s.dtype) for s in sums], 3 * len(sums), copies)


def _side_swap_reduced(halves):
    def copies(ins, outs, ssem, rsem):
        x, y, c, _ = _place()
        return [_remote(ins[w], outs[w], ssem.at[w], rsem.at[w], (x, y, 1 - c)) for w in range(len(ins))]

    return _Side(halves, [jax.ShapeDtypeStruct(h.shape, h.dtype) for h in halves], len(halves), copies)


SUM_BLOCKS = 2


def _chip_sums(gs, recvs, c_arr, name):
    n = len(gs)

    def body(c_ref, *refs):
        for k in range(n):
            refs[2 * n + k][...] = (refs[2 * k][...] + refs[2 * k + 1][...]).astype(BF16)

    in_specs, out_specs, out_shape, args = [], [], [], []
    for g, recv in zip(gs, recvs):
        _, r, cols = g.shape
        h = r // 2
        assert h % (16 * SUM_BLOCKS) == 0, g.shape
        tr = h // SUM_BLOCKS
        in_specs += [pl.BlockSpec((None, None, tr, cols), lambda s, i, c_ref: (s, c_ref[0], i, 0)),
                     pl.BlockSpec((None, tr, cols), lambda s, i, c_ref: (s, i, 0))]
        out_specs.append(pl.BlockSpec((None, tr, cols), lambda s, i, c_ref: (s, i, 0)))
        out_shape.append(jax.ShapeDtypeStruct((4, h, cols), BF16))
        args += [g.reshape(4, 2, h, cols), recv]
    grid_spec = pltpu.PrefetchScalarGridSpec(num_scalar_prefetch=1, grid=(4, SUM_BLOCKS), in_specs=in_specs,
                                             out_specs=out_specs)
    return _pcall(body, name=name, grid_spec=grid_spec, out_shape=out_shape, compiler_params=_params())(c_arr, *args)


def _owner_sums(sums, others, s_arr, name):
    n = len(sums)

    def body(s_ref, *refs):
        f = lambda v: v.astype(F32)
        for k in range(n):
            a_ref, o_ref = refs[2 * k], refs[2 * k + 1]
            refs[2 * n + k][...] = (f(a_ref[...]) + f(o_ref[0])) + (f(o_ref[1]) + f(o_ref[2]))

    in_specs, out_specs, out_shape, args = [], [], [], []
    for sm, ot in zip(sums, others):
        _, h, cols = sm.shape
        tr = h // SUM_BLOCKS
        in_specs += [pl.BlockSpec((None, tr, cols), lambda i, s_ref: (s_ref[0], i, 0)),
                     pl.BlockSpec((3, tr, cols), lambda i, s_ref: (0, i, 0))]
        out_specs.append(pl.BlockSpec((tr, cols), lambda i, s_ref: (i, 0)))
        out_shape.append(jax.ShapeDtypeStruct((h, cols), F32))
        args += [sm, ot]
    grid_spec = pltpu.PrefetchScalarGridSpec(num_scalar_prefetch=1, grid=(SUM_BLOCKS,), in_specs=in_specs,
                                             out_specs=out_specs)
    return _pcall(body, name=name, grid_spec=grid_spec, out_shape=out_shape, compiler_params=_params())(s_arr, *args)


def _side_small_sibling(v):
    def copies(ins, outs, ssem, rsem):
        x, y, c, _ = _place()
        return [_remote(ins[0], outs[0], ssem.at[0], rsem.at[0], (x, y, 1 - c))]

    return _Side([v], [jax.ShapeDtypeStruct(v.shape, F32)], 1, copies)


def _side_small_chips(v):
    def copies(ins, outs, ssem, rsem):
        x, y, c, chips = _place()
        return [_remote(ins[0], outs[0].at[k], ssem.at[k], rsem.at[k], (px, py, c)) for k, (px, py) in enumerate(chips)]

    return _Side([v], [jax.ShapeDtypeStruct((3,) + v.shape, F32)], 3, copies)


def _small_add(v, r, name):
    def body(v_ref, r_ref, o_ref):
        if r.ndim == 2:
            o_ref[...] = v_ref[...] + r_ref[...]
        else:
            o_ref[...] = (v_ref[...] + r_ref[0]) + (r_ref[1] + r_ref[2])

    vm = pl.BlockSpec(memory_space=pltpu.VMEM)
    return _pcall(body, name=name, in_specs=[vm, vm], out_specs=vm, out_shape=jax.ShapeDtypeStruct(v.shape, F32),
                  compiler_params=_params())(v, r)


def _merge_sides(sides):
    if len(sides) == 1:
        return sides[0]

    def copies(in_refs, out_refs, ssem, rsem):
        cps, i, o, q = [], 0, 0, 0
        for s in sides:
            ni, no = len(s.ins), len(s.out_shapes)
            cps += s.copies(in_refs[i:i + ni], out_refs[o:o + no], ssem.at[pl.ds(q, s.nsem)], rsem.at[pl.ds(q, s.nsem)])
            i, o, q = i + ni, o + no, q + s.nsem
        return cps

    assert not any(s.aliased for s in sides)
    return _Side(sum((s.ins for s in sides), []), sum((s.out_shapes for s in sides), []), sum(s.nsem for s in sides), copies)


def _tile_rows(size):
    return -(-size // (8 * LANE)) * 8


def _pack_small(parts):
    pieces = []
    for p in parts:
        flat = p.reshape(-1).astype(F32)
        pieces.append(jnp.pad(flat, (0, _tile_rows(p.size) * LANE - p.size)).reshape(-1, LANE))
    rows = sum(x.shape[0] for x in pieces)
    pieces.append(jnp.zeros(((-rows) % 64, LANE), F32))
    return jnp.concatenate(pieces, axis=0)


def _unpack_small(packed, like):
    out, pos = [], 0
    for p in like:
        rows = _tile_rows(p.size)
        out.append(packed[pos:pos + rows].reshape(-1)[:p.size].reshape(p.shape))
        pos += rows
    return out


FFN_FWD_ROWS, FFN_BWD_ROWS = 1024, 512
FFN_SUB_ROWS = 256


def _ffn_specs(n, d, fs, cap):
    rows = _pick(n, cap, 16)
    row = pl.BlockSpec((rows, d), lambda i, s: (i, 0))
    gain = pl.BlockSpec((1, d), lambda i, s: (0, 0))
    w_row = pl.BlockSpec((None, fs, d), lambda i, s: (s, 0, 0))
    hid = pl.BlockSpec((None, rows, fs), lambda i, s: (s, i, 0))
    return rows, row, gain, w_row, hid


def _ffn_fwd(h, g, w1t, w3t, w2, tag, plan):
    n, d = h.shape
    ns, fs, _ = w2.shape
    rows, row, gain, w_row, hid = _ffn_specs(n, d, fs, FFN_FWD_ROWS)
    sub = rows

    def body(h_ref, g_ref, w1_ref, w3_ref, w2_ref, out_ref, n1_ref, a_ref, b_ref, hm_ref, acc_ref):
        s = pl.program_id(1)

        @pl.when(s == 0)
        def _():
            xv = h_ref[...]
            rstd = lax.rsqrt(jnp.mean(xv * xv, axis=-1, keepdims=True) + EPS)
            n1_ref[...] = (xv * rstd * g_ref[...]).astype(BF16)
            acc_ref[...] = jnp.zeros_like(acc_ref)

        def up(j):
            n1 = n1_ref[j * sub:(j + 1) * sub, :]
            return _dot(n1, w1_ref[...], NT), _dot(n1, w3_ref[...], NT)

        cur = up(0)
        for j in range(rows // sub):
            nxt = up(j + 1) if (j + 1) * sub < rows else None
            a, b = cur
            r = slice(j * sub, (j + 1) * sub)
            hm = (a * _sigmoid(a) * b).astype(BF16)
            a_ref[r, :] = a.astype(BF16)
            b_ref[r, :] = b.astype(BF16)
            hm_ref[r, :] = hm
            acc_ref[r, :] += _dot(hm, w2_ref[...])
            cur = nxt

        @pl.when(s == ns - 1)
        def _():
            out_ref[...] = h_ref[...] + 0.5 * acc_ref[...]

    hid_shape = jax.ShapeDtypeStruct((ns, n, fs), BF16)
    plan.before(f"{tag}_fwd")
    out, n1, a, b, hm = _pcall(
        body, name=f"{tag}_fwd", grid=(n // rows, ns), in_specs=[row, gain, w_row, w_row, w_row],
        out_specs=[row, row, hid, hid, hid],
        out_shape=[jax.ShapeDtypeStruct((n, d), F32), jax.ShapeDtypeStruct((n, d), BF16), hid_shape, hid_shape, hid_shape],
        scratch_shapes=[pltpu.VMEM((rows, d), F32)], compiler_params=_params())(h, g, w1t, w3t, w2)
    plan.after(f"{tag}_fwd")
    return out, (h, n1, a, b, hm)


def _wgrad(a3, b, name, alpha=1.0):
    ns, n, fs = a3.shape
    d = b.shape[1]
    tk = _pick(n, 1024, 16)

    def body(a_ref, b_ref, o_ref):
        @pl.when(pl.program_id(0) == 0)
        def _():
            o_ref[...] = jnp.zeros_like(o_ref)

        bv = b_ref[...].astype(BF16)
        for s in range(ns):
            part = _dot(a_ref[s], bv, TN)
            o_ref[s] += part if alpha == 1.0 else alpha * part

    return _pcall(body, name=name, grid=(n // tk,),
                  in_specs=[pl.BlockSpec((ns, tk, fs), lambda k: (0, k, 0)), pl.BlockSpec((tk, d), lambda k: (k, 0))],
                  out_specs=pl.BlockSpec((ns, fs, d), lambda k: (0, 0, 0)),
                  out_shape=jax.ShapeDtypeStruct((ns, fs, d), F32), compiler_params=_params())(a3, b)


def _ffn_bwd(dout, saved, g, w1, w3, w2, tag, plan):
    h, n1, a, b, hm = saved
    n, d = h.shape
    ns, fs, _ = w2.shape
    rows, row, gain, w_row, hid = _ffn_specs(n, d, fs, FFN_BWD_ROWS)
    sub = _pick(rows, FFN_SUB_ROWS, 16)

    def body(do_ref, h_ref, g_ref, a_ref, b_ref, w1_ref, w3_ref, w2_ref, dh_ref, da_ref, db_ref, dg_ref, acc_ref):
        i, s = pl.program_id(0), pl.program_id(1)

        @pl.when(s == 0)
        def _():
            acc_ref[...] = jnp.zeros_like(acc_ref)

        @pl.when((s == 0) & (i == 0))
        def _():
            dg_ref[...] = jnp.zeros_like(dg_ref)

        def up(j):
            return _dot(0.5 * do_ref[j * sub:(j + 1) * sub, :], w2_ref[...], NT)

        cur = up(0)
        for j in range(rows // sub):
            nxt = up(j + 1) if (j + 1) * sub < rows else None
            r = slice(j * sub, (j + 1) * sub)
            av, bv = a_ref[r, :].astype(F32), b_ref[r, :].astype(F32)
            sg = _sigmoid(av)
            da = (cur * bv * (sg * (1.0 + av * (1.0 - sg)))).astype(BF16)
            db = (cur * av * sg).astype(BF16)
            da_ref[r, :] = da
            db_ref[r, :] = db
            acc_ref[r, :] += _dot(da, w1_ref[...]) + _dot(db, w3_ref[...])
            cur = nxt

        @pl.when(s == ns - 1)
        def _():
            xv, dn = h_ref[...], acc_ref[...]
            rstd = lax.rsqrt(jnp.mean(xv * xv, axis=-1, keepdims=True) + EPS)
            xh = xv * rstd
            dg_ref[...] += jnp.sum(dn * xh, axis=0, keepdims=True)
            dxh = dn * g_ref[...]
            dh_ref[...] = do_ref[...] + rstd * (dxh - xh * jnp.mean(dxh * xh, axis=-1, keepdims=True))

    hid_shape = jax.ShapeDtypeStruct((ns, n, fs), BF16)
    plan.before(f"{tag}_bwd")
    dh, da, db, dg = _pcall(
        body, name=f"{tag}_bwd", grid=(n // rows, ns), in_specs=[row, row, gain, hid, hid, w_row, w_row, w_row],
        out_specs=[row, hid, hid, gain],
        out_shape=[jax.ShapeDtypeStruct((n, d), F32), hid_shape, hid_shape, jax.ShapeDtypeStruct((1, d), F32)],
        scratch_shapes=[pltpu.VMEM((rows, d), F32)], compiler_params=_params())(dout, h, g, a, b, w1, w3, w2)
    plan.after(f"{tag}_bwd")
    plan.grads[f"{tag}_norm"] = dg
    plan.before(f"{tag}_gw2")
    gw2 = _wgrad(hm, dout, f"{tag}_gw2", alpha=0.5)
    plan.after(f"{tag}_gw2")
    plan.grads[f"{tag}_w2"] = gw2
    plan.before(f"{tag}_gw1")
    gw1 = _wgrad(da, n1, f"{tag}_gw1")
    plan.after(f"{tag}_gw1")
    plan.grads[f"{tag}_w1"] = gw1
    plan.before(f"{tag}_gw3")
    gw3 = _wgrad(db, n1, f"{tag}_gw3")
    plan.after(f"{tag}_gw3")
    return dh, dg, gw1, gw3, gw2


def _local_step(x, tgt, plan):
    n = x.shape[0]
    grads = plan.grads

    def f(name):
        w = plan.get(name)
        return w.reshape(1, D_MODEL) if name.endswith('_norm') and name != 'gla_out_norm' else w

    def carried(tag, fn, *args, **kw):
        plan.before(tag)
        out = fn(*args, **kw)
        plan.after(tag)
        return out

    h1, ffn1 = _ffn_fwd(x, f('ffn1_norm'), f('ffn1_w1'), f('ffn1_w3'), f('ffn1_w2'), "ffn1", plan)
    u = carried("mix_rms", _rms_fwd, h1, f('mix_norm'), "mix_rms")
    w_in = f('w_in')
    w_a = jnp.concatenate([w_in[:, :512], _pad_heads(w_in[:, 512:768]), _pad_heads(w_in[:, 768:1024]), w_in[:, 1024:2048]],
                          axis=1)
    w_al = jnp.pad(w_in[:, 2048:2048 + GLA_RANK], ((0, 0), (0, LANE - GLA_RANK)))
    w_g = w_in[:, 2048 + GLA_RANK:]
    za = carried("in_a", _mm, u, w_a, name="in_a")
    zg = carried("in_g", _mm, u, w_g, name="in_g")
    al = _mm(u, w_al, name="in_al")
    ar, ai, bbar_re, bbar_im = _s5_discretize(f('s5_lambda_re'), f('s5_lambda_im'), f('s5_log_dt'), f('s5_b_re'), f('s5_b_im'))
    t_b = _bd_tiles(bbar_re.transpose(0, 2, 1), bbar_im.transpose(0, 2, 1)).astype(BF16)
    t_c = _bd_tiles(f('s5_c_re'), -f('s5_c_im')).astype(BF16)
    ar8 = jnp.broadcast_to(ar.reshape(1, S5_GP), (SEG, S5_GP))
    ai8 = jnp.broadcast_to(ai.reshape(1, S5_GP), (SEG, S5_GP))
    pw_r, pw_i = _segment_powers(ar, ai, n // SEG)
    dskip = f('s5_d').reshape(1, S5_W)
    u_s5 = _permute_rows(za[:, :S5_W])
    xs = _s5_scan(u_s5, t_b, ar8, ai8, pw_r, pw_i, "s5_scan")
    ys_p = _bd_reduce(xs, t_c, _scale_rows(u_s5, dskip, "s5_skip"), "s5_y")
    ys = _unpermute_rows(ys_p)
    zgelu = _gelu_fwd(ys, "s5_gelu")
    t_glu = _mm(zgelu, f('s5_glu_w'), bias=f('s5_glu_b').reshape(1, S5_W), name="s5_glu_t")
    y_s5 = _glu_fwd(zgelu, t_glu, "s5_glu")
    wup = jnp.pad(f('gla_a_up_w'), ((0, LANE - GLA_RANK), (0, 0)))
    wup_h = wup.reshape(LANE, GLA_HEADS, GLA_DK).transpose(1, 0, 2)
    bup_h = f('gla_a_up_b').reshape(GLA_HEADS, 1, GLA_DK)
    gn_h = f('gla_out_norm').reshape(GLA_HEADS, 1, GLA_DV)
    y_gla, s_prev = carried("gla_fwd", _gla_fwd, za, al, wup_h, bup_h, gn_h, "gla_fwd")
    ps = _mm(y_s5, f('proj_s5'), name="proj_s5")
    pg = carried("proj_gla", _mm, y_gla, f('proj_gla'), name="proj_gla")
    merged = _merge_fwd(zg, ps, pg, "merge")
    h2 = _mm(merged, f('w_out'), res=h1, name="w_out")
    h3, ffn2 = _ffn_fwd(h2, f('ffn2_norm'), f('ffn2_w1'), f('ffn2_w3'), f('ffn2_w2'), "ffn2", plan)
    loss, dh3, g_final = _final_loss(h3, f('final_norm').reshape(1, D_MODEL), tgt, "loss")
    plan.loss = loss[0, 0]
    grads['final_norm'] = g_final.reshape(D_MODEL)
    dh2, grads['ffn2_norm'], grads['ffn2_w1'], grads['ffn2_w3'], grads['ffn2_w2'] = _ffn_bwd(
        dh3, ffn2, f('ffn2_norm'), f('ffn2_w1'), f('ffn2_w3'), f('ffn2_w2'), "ffn2", plan)
    dm = carried("d_merged", _mm, dh2, f('w_out'), tb=True, name="d_merged")
    grads['w_out'] = _mm(merged, dh2, ta=True, name="g_w_out")
    dps, dpg, dzg = carried("d_merge", _merge_bwd, dm, zg, ps, pg, "d_merge")
    grads['proj_s5'] = _mm(y_s5, dps, ta=True, name="g_proj_s5")
    grads['proj_gla'] = _mm(y_gla, dpg, ta=True, name="g_proj_gla")
    dy_s5 = _mm(dps, f('proj_s5'), tb=True, name="d_y_s5")
    dy_gla = _mm(dpg, f('proj_gla'), tb=True, name="d_y_gla")
    dzgelu, dt_glu, g_glu_b = _glu_bwd1(dy_s5, zgelu, t_glu, "d_glu")
    grads['s5_glu_b'] = g_glu_b.reshape(S5_W)
    grads['s5_glu_w'] = _mm(zgelu, dt_glu, ta=True, name="g_glu_w")
    dzgelu = _mm(dt_glu, f('s5_glu_w'), tb=True, res=dzgelu, name="d_gelu")
    dys, du_skip, g_d = _glu_bwd2(_permute_rows(dzgelu), ys_p, u_s5, dskip, "d_s5_y")
    grads['s5_d'] = g_d.reshape(S5_G, S5_H)
    lam, da8 = _s5_scan_bwd(dys, t_c, xs, ar8, ai8, pw_r, pw_i, "s5_scan_bwd")
    g_c = _bd_blocks(_bd_outer(dys, xs, "g_s5_c"))
    grads['s5_c_re'], grads['s5_c_im'] = g_c[0], -g_c[1]
    g_b = _bd_blocks(_bd_outer(u_s5, lam, "g_s5_b")).transpose(0, 1, 3, 2)
    g_bbar_re, g_bbar_im = g_b[0], g_b[1]
    da = jnp.sum(da8, axis=0)
    g_ar, g_ai = da[:S5_GP].reshape(S5_G, S5_P), da[S5_GP:].reshape(S5_G, S5_P)
    _, disc_vjp = jax.vjp(_s5_discretize, f('s5_lambda_re'), f('s5_lambda_im'), f('s5_log_dt'), f('s5_b_re'), f('s5_b_im'))
    (grads['s5_lambda_re'], grads['s5_lambda_im'], grads['s5_log_dt'], grads['s5_b_re'],
     grads['s5_b_im']) = disc_vjp((g_ar, g_ai, g_bbar_re, g_bbar_im))
    du_s5 = _unpermute_rows(_bd_reduce(lam, t_b, du_skip, "d_s5_u"))
    dza, dz, dgn, dbup = carried("gla_bwd", _gla_bwd, za, al, wup_h, bup_h, gn_h, s_prev, dy_gla, du_s5, "gla_bwd")
    grads['gla_out_norm'] = dgn.reshape(GLA_HEADS * GLA_DV)
    grads['gla_a_up_b'] = dbup.reshape(GLA_HEADS * GLA_DK)
    grads['gla_a_up_w'] = _unpad_heads(_mm(al, dz, ta=True, name="g_a_up")[:GLA_RANK])
    dal = _mm(dz, _pad_heads(wup), tb=True, name="d_a_low")
    g_wa = _mm(u, dza, ta=True, name="g_in_a")
    g_wg = _mm(u, dzg, ta=True, name="g_in_g")
    g_wal = _mm(u, dal, ta=True, name="g_in_al")
    grads['w_in'] = jnp.concatenate([g_wa[:, :512], _unpad_heads(g_wa[:, 512:1024]), _unpad_heads(g_wa[:, 1024:1536]),
                                     g_wa[:, 1536:], g_wal[:, :GLA_RANK], g_wg], axis=1)
    du = carried("d_u_a", _mm, dza, w_a, tb=True, name="d_u_a")
    du = _mm(dzg, w_g, tb=True, res=du, name="d_u_g")
    du = _mm(dal, w_al, tb=True, res=du, name="d_u_al")
    dh1, g_mix = carried("d_mix_rms", _rms_bwd, h1, f('mix_norm'), du, dh2, "d_mix_rms")
    grads['mix_norm'] = g_mix
    dx, grads['ffn1_norm'], grads['ffn1_w1'], grads['ffn1_w3'], grads['ffn1_w2'] = _ffn_bwd(
        dh1, ffn1, f('ffn1_norm'), f('ffn1_w1'), f('ffn1_w3'), f('ffn1_w2'), "ffn1", plan)
    return loss[0, 0], dx


MIXER_WEIGHTS = ['w_in', 's5_glu_w', 'proj_s5', 'proj_gla', 'w_out', 'gla_a_up_w']
FFN1_WEIGHTS, FFN2_WEIGHTS = FFN_WEIGHTS[:3], FFN_WEIGHTS[3:]
TRANSPOSED = ['ffn1_w1', 'ffn1_w3', 'ffn2_w1', 'ffn2_w3']


def _local_shard(w, nm):
    return jnp.swapaxes(w, 1, 2)[0] if nm in TRANSPOSED else w[0]
FFN1_EARLY = ['ffn1_w2']
FFN1_LATE = ['ffn1_w1', 'ffn1_w3']
GRAD_GROUPS = {'ffn2': FFN2_WEIGHTS, 'mixer': ['w_out', 'proj_s5', 'proj_gla', 's5_glu_w', 'w_in'], 'ffn1': FFN1_WEIGHTS}


class _Plan:
    def __init__(self, a, c_arr, s_arr):
        self.a, self.c_arr, self.s_arr = a, c_arr, s_arr
        self.grads, self.weights, self.riding = {}, {}, {}
        self.g4s, self.chip_sums, self.halves, self.sib_halves = {}, {}, {}, {}
        for nm in SMALL:
            if nm != 'gla_a_up_w':
                self.weights[nm] = a[nm] if nm == 'final_norm' else a[nm][0]
        ici = _side_gather_ici(self._shards(FFN1_WEIGHTS))
        _run_side(ici, "gather_ffn1_ici")
        self._gathered(FFN1_WEIGHTS, _run_side(_side_gather_d2d(ici.outs), "gather_ffn1_d2d"))

    def _shards(self, names):
        return [_local_shard(self.a[nm], nm).astype(F32 if nm == 'gla_a_up_w' else BF16) for nm in names]

    def _gathered(self, names, arrs):
        for nm, g4 in zip(names, arrs):
            if nm in FFN_WEIGHTS:
                self.weights[nm] = g4
            elif nm in COL_SHARDED:
                self.weights[nm] = jnp.concatenate([g4[s] for s in range(4)], axis=1)
            else:
                self.weights[nm] = g4.reshape(4 * g4.shape[1], g4.shape[2])

    def get(self, name):
        return self.weights[name]

    def _shard_major(self, nm):
        g = self.grads[nm]
        if nm in FFN_WEIGHTS:
            return g
        if nm in COL_SHARDED:
            return jnp.stack(jnp.split(g, 4, axis=1))
        return g.reshape(4, g.shape[0] // 4, g.shape[1])

    def _schedule(self, tag):
        grp = GRAD_GROUPS
        gathers = {"ffn1_fwd": ('ici', MIXER_WEIGHTS), "mix_rms": ('d2d', MIXER_WEIGHTS),
                   "in_a": ('ici', FFN2_WEIGHTS[:1]), "in_g": ('d2d', FFN2_WEIGHTS[:1]),
                   "gla_fwd": ('ici', FFN2_WEIGHTS[1:]), "proj_gla": ('d2d', FFN2_WEIGHTS[1:])}
        if tag in gathers:
            kind, names = gathers[tag]
            key = tuple(names)
            if kind == 'ici':
                return [(_side_gather_ici(self._shards(names)), lambda outs: self.riding.update({key: outs}))]
            return [(_side_gather_d2d(self.riding[key]), lambda outs: self._gathered(names, outs))]
        steps = {"ffn2_gw1": (['ffn2_w2'], 0), "ffn2_gw3": (['ffn2_w1'], 0), "d_merged": (['ffn2_w3'], 0),
                 "gla_bwd": (grp['ffn2'], 1), "d_mix_rms": (grp['ffn2'], 2),
                 "d_u_a": (grp['mixer'], 0), "ffn1_bwd": (grp['mixer'], 1), "ffn1_gw2": (grp['mixer'], 2),
                 "ffn1_gw1": (FFN1_EARLY, 0), "ffn1_gw3": (FFN1_EARLY, 1), "adamw_early": (FFN1_LATE, 1)}
        entries = [self._reduce_stage(*steps[tag])] if tag in steps else []
        if tag == "ffn1_gw2":
            entries.append(self._small_stage(0))
        if tag == "ffn1_gw1":
            entries.append(self._small_stage(1))
        return entries

    def _small_stage(self, stage):
        if stage == 0:
            a, grads = self.a, self.grads
            self.small_parts = ([grads[nm].reshape(a[nm].shape) for nm in SMALL if nm != 'gla_a_up_w']
                                + [grads['gla_a_up_w'], self.loss.reshape(1)])
            packed = _pack_small(self.small_parts)

            def done(outs):
                self.small_pair = _small_add(packed, outs[0], "small_sum_pair")
            return _side_small_sibling(packed), done

        def done(outs):
            self.small_total = _small_add(self.small_pair, outs[0], "small_sum_chips")
        return _side_small_chips(self.small_pair), done

    def _reduce_stage(self, names, stage):
        if stage == 0:
            for nm in names:
                self.g4s[nm] = self._shard_major(nm)

            def done(outs):
                sums = _chip_sums([self.g4s[nm] for nm in names], outs, self.c_arr, f"chip_sum_{names[0]}")
                self.chip_sums.update(zip(names, sums))
            return _side_swap_halves([self.g4s[nm] for nm in names]), done
        if stage == 1:
            def done(outs):
                halves = _owner_sums([self.chip_sums[nm] for nm in names], outs, self.s_arr, f"owner_sum_{names[0]}")
                self.halves.update(zip(names, halves))
            return _side_scatter([self.chip_sums[nm] for nm in names]), done

        def done(outs):
            self.sib_halves.update(zip(names, outs))
        return _side_swap_reduced([self.halves[nm] for nm in names]), done

    def before(self, tag):
        entries = self._schedule(tag)
        if entries:
            merged = _merge_sides([side for side, _ in entries])
            self.riding[tag] = (merged, entries)
            _RIDER.append(merged)

    def after(self, tag):
        if tag in self.riding:
            merged, entries = self.riding.pop(tag)
            assert not _RIDER and merged.outs is not None, tag
            pos = 0
            for side, done in entries:
                done(merged.outs[pos:pos + len(side.out_shapes)])
                pos += len(side.out_shapes)

    def finish_alone(self, stage):
        names = FFN1_LATE if stage == 0 else GRAD_GROUPS['ffn1']
        side, done = self._reduce_stage(names, stage)
        done(_run_side(side, f"grad_ffn1_stage{stage}"))


def _train_step(a):
    x = a['x'][0]
    tgt = a['loss_target'][0]
    xi, yi, ci = lax.axis_index("x"), lax.axis_index("y"), lax.axis_index("c")
    c_arr = jnp.reshape(ci, (1,)).astype(jnp.int32)
    s_arr = jnp.reshape(2 * xi + yi, (1,)).astype(jnp.int32)
    plan = _Plan(a, c_arr, s_arr)
    loss, dx = _local_step(x, tgt, plan)
    red = {}
    small_sum = _unpack_small(plan.small_total, plan.small_parts)
    small_names = [nm for nm in SMALL if nm != 'gla_a_up_w']
    for nm, g in zip(small_names, small_sum[:-2]):
        red[nm] = g
    loss = small_sum[-1].reshape(())
    g_up = small_sum[-2]
    red['gla_a_up_w'] = lax.dynamic_slice(g_up, (0, (2 * xi + yi) * GLA_DK), (GLA_RANK, GLA_DK))
    out_g, out_d, out_m, out_v = {}, {}, {}, {}

    def update(names, tag):
        items = [(_local_shard(a[nm], nm), plan.halves[nm], plan.sib_halves[nm], _local_shard(a['m_' + nm], nm),
                  _local_shard(a['v_' + nm], nm)) for nm in names]
        plan.before(tag)
        res = _adamw_group(items, c_arr, tag)
        plan.after(tag)
        for k, nm in enumerate(names):
            back = (lambda t: jnp.swapaxes(t[None], 1, 2)) if nm in TRANSPOSED else (lambda t: t[None])
            out_g[nm], out_d[nm], out_m[nm], out_v[nm] = (back(t) for t in res[4 * k:4 * k + 4])

    plan.finish_alone(0)
    update([nm for nm in SHARDED if nm not in GRAD_GROUPS['ffn1']], "adamw_early")
    plan.finish_alone(2)
    update(GRAD_GROUPS['ffn1'], "adamw_ffn1")
    rest = [nm for nm in WEIGHTS if nm not in SHARDED]
    pk = lambda pre: _pack_small([a[pre + nm] for nm in rest])
    d, nm_, nv_ = _adamw(pk(''), _pack_small([red[nm] for nm in rest]), pk('m_'), pk('v_'), "adamw_small")
    like = [a[nm] for nm in rest]
    for nm, g, dd, mm_, vv_ in zip(rest, [red[nm].reshape(a[nm].shape) for nm in rest], _unpack_small(d, like),
                                   _unpack_small(nm_, like), _unpack_small(nv_, like)):
        out_g[nm], out_d[nm], out_m[nm], out_v[nm] = g, dd, mm_, vv_
    return (loss, dx[None], *[out_g[nm] for nm in WEIGHTS], *[out_d[nm] for nm in WEIGHTS],
            *[out_m[nm] for nm in WEIGHTS], *[out_v[nm] for nm in WEIGHTS])


def kernel(x, ffn1_norm, ffn1_w1, ffn1_w3, ffn1_w2, mix_norm, w_in, s5_lambda_re, s5_lambda_im, s5_log_dt, s5_b_re, s5_b_im, s5_c_re, s5_c_im, s5_d, s5_glu_w, s5_glu_b, gla_a_up_w, gla_a_up_b, gla_out_norm, proj_s5, proj_gla, w_out, ffn2_norm, ffn2_w1, ffn2_w3, ffn2_w2, final_norm, loss_target, m_ffn1_norm, m_ffn1_w1, m_ffn1_w3, m_ffn1_w2, m_mix_norm, m_w_in, m_s5_lambda_re, m_s5_lambda_im, m_s5_log_dt, m_s5_b_re, m_s5_b_im, m_s5_c_re, m_s5_c_im, m_s5_d, m_s5_glu_w, m_s5_glu_b, m_gla_a_up_w, m_gla_a_up_b, m_gla_out_norm, m_proj_s5, m_proj_gla, m_w_out, m_ffn2_norm, m_ffn2_w1, m_ffn2_w3, m_ffn2_w2, m_final_norm, v_ffn1_norm, v_ffn1_w1, v_ffn1_w3, v_ffn1_w2, v_mix_norm, v_w_in, v_s5_lambda_re, v_s5_lambda_im, v_s5_log_dt, v_s5_b_re, v_s5_b_im, v_s5_c_re, v_s5_c_im, v_s5_d, v_s5_glu_w, v_s5_glu_b, v_gla_a_up_w, v_gla_a_up_b, v_gla_out_norm, v_proj_s5, v_proj_gla, v_w_out, v_ffn2_norm, v_ffn2_w1, v_ffn2_w3, v_ffn2_w2, v_final_norm):
    return _train_step(dict(locals()))
```

```python
import functools

import jax
import jax.numpy as jnp
from jax import lax
from jax.experimental import pallas as pl
from jax.experimental.pallas import tpu as pltpu

F32 = jnp.float32
BF16 = jnp.bfloat16
HI = lax.Precision.HIGHEST
MESH_ID = pl.DeviceIdType.MESH

D_MODEL = 1024
EPS = 1e-6
S5_G, S5_P, S5_H = 32, 64, 16
S5_W = S5_G * S5_H
S5_GP = S5_G * S5_P
SEG = 8
SCAN_ROWS = 256
GLA_HEADS, GLA_DK, GLA_DV = 4, 64, 128
GLA_CHUNK = 64
GLA_TAU = 16.0
GLA_RANK = 16
ADAM_LR, ADAM_B1, ADAM_B2, ADAM_EPS, ADAM_WD, ADAM_STEP = 0.001, 0.9, 0.999, 1e-08, 0.01, 10
V7X_VMEM_LIMIT = 56 * 1024 * 1024
LANE = 128

WEIGHTS = ['ffn1_norm', 'ffn1_w1', 'ffn1_w3', 'ffn1_w2', 'mix_norm', 'w_in', 's5_lambda_re', 's5_lambda_im',
           's5_log_dt', 's5_b_re', 's5_b_im', 's5_c_re', 's5_c_im', 's5_d', 's5_glu_w', 's5_glu_b', 'gla_a_up_w',
           'gla_a_up_b', 'gla_out_norm', 'proj_s5', 'proj_gla', 'w_out', 'ffn2_norm', 'ffn2_w1', 'ffn2_w3',
           'ffn2_w2', 'final_norm']
SHARDED = ['ffn1_w1', 'ffn1_w3', 'ffn1_w2', 'w_in', 's5_glu_w', 'proj_s5', 'proj_gla', 'w_out',
           'ffn2_w1', 'ffn2_w3', 'ffn2_w2']
COL_SHARDED = ['ffn1_w1', 'ffn1_w3', 'w_in', 'proj_s5', 'proj_gla', 'ffn2_w1', 'ffn2_w3', 'gla_a_up_w']
SMALL = [n for n in WEIGHTS if n not in SHARDED]
FFN_WEIGHTS = ['ffn1_w1', 'ffn1_w3', 'ffn1_w2', 'ffn2_w1', 'ffn2_w3', 'ffn2_w2']


def _params(**kw):
    return pltpu.CompilerParams(vmem_limit_bytes=V7X_VMEM_LIMIT, **kw)


class _Side:
    def __init__(self, ins, out_shapes, nsem, copies, aliased=False):
        self.ins, self.out_shapes, self.nsem, self.copies, self.aliased = list(ins), list(out_shapes), nsem, copies, aliased
        self.outs = None


_RIDER = []


def _pcall(body, **kw):
    if _RIDER:
        return _carry(body, _RIDER.pop(), **kw)
    return pl.pallas_call(body, **kw)


def _carry(body, side, *, name, grid, in_specs, out_specs, out_shape, scratch_shapes=(), compiler_params=None):
    del compiler_params
    single = not isinstance(out_shape, (list, tuple))
    out_specs = [out_specs] if single else list(out_specs)
    out_shape = [out_shape] if single else list(out_shape)
    n_in, n_out, n_scr = len(in_specs), len(out_shape), len(scratch_shapes)
    s_in, s_out = len(side.ins), len(side.out_shapes)
    any_spec = pl.BlockSpec(memory_space=pl.ANY)

    def wrapped(*refs):
        cuts = [n_in, s_in, n_out, s_out, n_scr]
        parts, pos = [], 0
        for c in cuts:
            parts.append(refs[pos:pos + c])
            pos += c
        ins, sins, outs, souts, scr = parts
        ssem, rsem = refs[pos], refs[pos + 1]
        first = last = None
        for d, g in enumerate(grid):
            i = pl.program_id(d)
            first = (i == 0) if first is None else first & (i == 0)
            last = (i == g - 1) if last is None else last & (i == g - 1)

        @pl.when(first)
        def _():
            for cp in side.copies(sins, souts, ssem, rsem):
                cp.start()

        body(*ins, *outs, *scr)

        @pl.when(last)
        def _():
            for cp in side.copies(sins, souts, ssem, rsem):
                cp.wait()

    call = pl.pallas_call(
        wrapped, name=name, grid=grid, in_specs=list(in_specs) + [any_spec] * s_in,
        out_specs=out_specs + [any_spec] * s_out, out_shape=out_shape + side.out_shapes,
        scratch_shapes=list(scratch_shapes) + [pltpu.SemaphoreType.DMA((side.nsem,)), pltpu.SemaphoreType.DMA((side.nsem,))],
        input_output_aliases={n_in + j: n_out + j for j in range(s_in)} if side.aliased else {},
        compiler_params=_params(has_side_effects=True))

    def run(*args):
        res = call(*args, *side.ins)
        side.outs = list(res[n_out:])
        return res[0] if single else list(res[:n_out])

    return run


def _run_side(side, name):
    s_in, s_out = len(side.ins), len(side.out_shapes)
    any_spec = pl.BlockSpec(memory_space=pl.ANY)

    def body(*refs):
        sins, souts = refs[:s_in], refs[s_in:s_in + s_out]
        ssem, rsem = refs[s_in + s_out:]
        cps = side.copies(sins, souts, ssem, rsem)
        for cp in cps:
            cp.start()
        for cp in cps:
            cp.wait()

    side.outs = list(pl.pallas_call(
        body, name=name, in_specs=[any_spec] * s_in, out_specs=[any_spec] * s_out, out_shape=side.out_shapes,
        scratch_shapes=[pltpu.SemaphoreType.DMA((side.nsem,)), pltpu.SemaphoreType.DMA((side.nsem,))],
        input_output_aliases={j: j for j in range(s_in)} if side.aliased else {},
        compiler_params=pltpu.CompilerParams(has_side_effects=True))(*side.ins))
    return side.outs


def _pick(n, cap, quantum):
    if n <= cap:
        return n
    best = None
    for t in range(quantum, cap + 1, quantum):
        if n % t == 0:
            best = t
    assert best is not None, (n, cap, quantum)
    return best


def _sigmoid(x):
    return jax.nn.sigmoid(x)


def _mm(a, b, *, name, ta=False, tb=False, out_dtype=F32, alpha=1.0, res=None, bias=None, exact=False, shard=None):
    ns = 4
    (k_a, m) = a.shape[-2:] if ta else a.shape[-2:][::-1]
    (k_b, n) = b.shape[-2:][::-1] if tb else b.shape[-2:]
    assert k_a == k_b, (a.shape, b.shape, ta, tb)
    assert (a.ndim == 3) == (shard in ('k', 'm')) and (b.ndim == 3) == (shard in ('n', 'k'))
    k = k_a
    tm = _pick(m, 1024, 128)
    tn = _pick(n, 1024, 128)
    tk = _pick(k, 1024, 128)
    pm, pn, pk = m // tm, n // tn, k // tk
    gm = pm * (ns if shard == 'm' else 1)
    gn = pn * (ns if shard == 'n' else 1)
    gk = pk * (ns if shard == 'k' else 1)
    dims = (((0,) if ta else (1,), (1,) if tb else (0,)), ((), ()))
    op_dtype = F32 if exact else BF16

    def body(*refs):
        a_ref, b_ref = refs[0], refs[1]
        pos = 2
        res_ref = bias_ref = None
        if res is not None:
            res_ref = refs[pos]
            pos += 1
        if bias is not None:
            bias_ref = refs[pos]
            pos += 1
        o_ref, acc_ref = refs[pos], refs[pos + 1]
        kk = pl.program_id(2)

        @pl.when(kk == 0)
        def _():
            acc_ref[...] = jnp.zeros_like(acc_ref)

        acc_ref[...] += lax.dot_general(a_ref[...].astype(op_dtype), b_ref[...].astype(op_dtype), dims,
                                        precision=HI if exact else None, preferred_element_type=F32)

        @pl.when(kk == gk - 1)
        def _():
            o = acc_ref[...]
            if alpha != 1.0:
                o = o * alpha
            if bias_ref is not None:
                o = o + bias_ref[...]
            if res_ref is not None:
                o = o + res_ref[...]
            o_ref[...] = o.astype(out_dtype)

    def spec(block, sharded_on, order):
        per = {'m': pm, 'n': pn, 'k': pk}

        def index(i, j, kk):
            g = {'m': i, 'n': j, 'k': kk}
            r, c = order(i % pm if shard == 'm' else i, j % pn if shard == 'n' else j, kk % pk if shard == 'k' else kk)
            if sharded_on is None:
                return (r, c)
            return (g[sharded_on] // per[sharded_on], r, c)

        return pl.BlockSpec(block if sharded_on is None else (None,) + block, index)

    a_sh = shard if shard in ('k', 'm') else None
    b_sh = shard if shard in ('n', 'k') else None
    o_sh = shard if shard in ('n', 'm') else None
    a_spec = spec((tk, tm), a_sh, lambda i, j, kk: (kk, i)) if ta else spec((tm, tk), a_sh, lambda i, j, kk: (i, kk))
    b_spec = spec((tn, tk), b_sh, lambda i, j, kk: (j, kk)) if tb else spec((tk, tn), b_sh, lambda i, j, kk: (kk, j))
    ins, in_specs = [a, b], [a_spec, b_spec]
    if res is not None:
        assert o_sh is None
        ins.append(res)
        in_specs.append(pl.BlockSpec((tm, tn), lambda i, j, kk: (i, j)))
    if bias is not None:
        assert o_sh is None
        ins.append(bias)
        in_specs.append(pl.BlockSpec((1, tn), lambda i, j, kk: (0, j)))
    out_shape = (m, n) if o_sh is None else (ns, m, n)
    return _pcall(body, name=name, grid=(gm, gn, gk), in_specs=in_specs,
                  out_specs=spec((tm, tn), o_sh, lambda i, j, kk: (i, j)),
                  out_shape=jax.ShapeDtypeStruct(out_shape, out_dtype),
                  scratch_shapes=[pltpu.VMEM((tm, tn), F32)], compiler_params=_params())(*ins)


ROWS_VMEM_BUDGET = 24 * 1024 * 1024


def _rows(body, ins, outs, *, n, name):
    cols = sum(a.shape[1] for a, kind in ins if kind == 'r') + sum(c for c, _, kind in outs if kind == 'r')
    cap = 256
    while cap < 2048 and 2 * 4 * cols * (2 * cap) <= ROWS_VMEM_BUDGET:
        cap *= 2
    tm = _pick(n, cap, 16)
    in_specs = []
    for arr, kind in ins:
        if kind == 'r':
            in_specs.append(pl.BlockSpec((tm, arr.shape[1]), lambda i: (i, 0)))
        else:
            in_specs.append(pl.BlockSpec(arr.shape, lambda i: (0, 0)))
    out_specs, out_shape = [], []
    for cols, dtype, kind in outs:
        if kind == 'r':
            out_specs.append(pl.BlockSpec((tm, cols), lambda i: (i, 0)))
            out_shape.append(jax.ShapeDtypeStruct((n, cols), dtype))
        else:
            out_specs.append(pl.BlockSpec((1, cols), lambda i: (0, 0)))
            out_shape.append(jax.ShapeDtypeStruct((1, cols), dtype))
    n_in = len(ins)
    acc_ids = [j for j, o in enumerate(outs) if o[2] == 'a']

    def wrapped(*refs):
        if acc_ids:
            @pl.when(pl.program_id(0) == 0)
            def _():
                for j in acc_ids:
                    refs[n_in + j][...] = jnp.zeros_like(refs[n_in + j])
        body(*refs)

    res = _pcall(wrapped, name=name, grid=(n // tm,), in_specs=in_specs, out_specs=out_specs, out_shape=out_shape,
                 compiler_params=_params())(*[a for a, _ in ins])
    return res


def _rms_fwd(x, g, name):
    def body(x_ref, g_ref, o_ref):
        xv = x_ref[...]
        rstd = lax.rsqrt(jnp.mean(xv * xv, axis=-1, keepdims=True) + EPS)
        o_ref[...] = (xv * rstd * g_ref[...]).astype(BF16)
    return _rows(body, [(x, 'r'), (g, 'f')], [(x.shape[1], BF16, 'r')], n=x.shape[0], name=name)[0]


def _rms_bwd(x, g, dn, dres, name):
    def body(x_ref, g_ref, dn_ref, dres_ref, dx_ref, dg_ref):
        xv = x_ref[...]
        rstd = lax.rsqrt(jnp.mean(xv * xv, axis=-1, keepdims=True) + EPS)
        xh = xv * rstd
        dn = dn_ref[...]
        dg_ref[...] += jnp.sum(dn * xh, axis=0, keepdims=True)
        dxh = dn * g_ref[...]
        dx_ref[...] = dres_ref[...] + rstd * (dxh - xh * jnp.mean(dxh * xh, axis=-1, keepdims=True))
    d = x.shape[1]
    return _rows(body, [(x, 'r'), (g, 'f'), (dn, 'r'), (dres, 'r')], [(d, F32, 'r'), (d, F32, 'a')],
                 n=x.shape[0], name=name)


def _gelu_parts(y):
    c0 = 0.7978845608028654
    inner = c0 * (y + 0.044715 * y * y * y)
    th = jnp.tanh(inner)
    return th, c0 * (1.0 + 3.0 * 0.044715 * y * y)


def _gelu_fwd(y, name):
    def body(y_ref, o_ref):
        yv = y_ref[...]
        th, _ = _gelu_parts(yv)
        o_ref[...] = 0.5 * yv * (1.0 + th)
    return _rows(body, [(y, 'r')], [(y.shape[1], F32, 'r')], n=y.shape[0], name=name)[0]


def _glu_fwd(zg, t, name):
    def body(z_ref, t_ref, o_ref):
        o_ref[...] = (z_ref[...] * _sigmoid(t_ref[...])).astype(BF16)
    return _rows(body, [(zg, 'r'), (t, 'r')], [(zg.shape[1], BF16, 'r')], n=zg.shape[0], name=name)[0]


def _glu_bwd1(dy, zg, t, name):
    def body(dy_ref, z_ref, t_ref, dz_ref, dt_ref, db_ref):
        dyv, zv = dy_ref[...], z_ref[...]
        sg = _sigmoid(t_ref[...])
        dz_ref[...] = dyv * sg
        dt = dyv * zv * sg * (1.0 - sg)
        dt_ref[...] = dt.astype(BF16)
        db_ref[...] += jnp.sum(dt, axis=0, keepdims=True)
    w = zg.shape[1]
    return _rows(body, [(dy, 'r'), (zg, 'r'), (t, 'r')], [(w, F32, 'r'), (w, BF16, 'r'), (w, F32, 'a')],
                 n=zg.shape[0], name=name)


def _glu_bwd2(dzg, ys, u, dskip, name):
    def body(dz_ref, y_ref, u_ref, d_ref, dy_ref, du_ref, dd_ref):
        yv = y_ref[...]
        th, dinner = _gelu_parts(yv)
        dy = dz_ref[...] * (0.5 * (1.0 + th) + 0.5 * yv * (1.0 - th * th) * dinner)
        dy_ref[...] = dy
        du_ref[...] = dy * d_ref[...]
        dd_ref[...] += jnp.sum(dy * u_ref[...], axis=0, keepdims=True)
    w = ys.shape[1]
    return _rows(body, [(dzg, 'r'), (ys, 'r'), (u, 'r'), (dskip, 'f')], [(w, F32, 'r'), (w, F32, 'r'), (w, F32, 'a')],
                 n=ys.shape[0], name=name)


def _scale_rows(u, dskip, name):
    def body(u_ref, d_ref, o_ref):
        o_ref[...] = u_ref[...] * d_ref[...]
    return _rows(body, [(u, 'r'), (dskip, 'f')], [(u.shape[1], F32, 'r')], n=u.shape[0], name=name)[0]


def _merge_fwd(zg, ps, pg, name):
    def body(z_ref, ps_ref, pg_ref, o_ref):
        zv = z_ref[...]
        o_ref[...] = (_sigmoid(zv[:, :D_MODEL]) * ps_ref[...] + _sigmoid(zv[:, D_MODEL:]) * pg_ref[...]).astype(BF16)
    return _rows(body, [(zg, 'r'), (ps, 'r'), (pg, 'r')], [(D_MODEL, BF16, 'r')], n=zg.shape[0], name=name)[0]


def _merge_bwd(dm, zg, ps, pg, name):
    def body(dm_ref, z_ref, ps_ref, pg_ref, dps_ref, dpg_ref, dz_ref):
        dmv, zv = dm_ref[...], z_ref[...]
        s1, s2 = _sigmoid(zv[:, :D_MODEL]), _sigmoid(zv[:, D_MODEL:])
        dps_ref[...] = (dmv * s1).astype(BF16)
        dpg_ref[...] = (dmv * s2).astype(BF16)
        dz_ref[:, :D_MODEL] = dmv * ps_ref[...] * s1 * (1.0 - s1)
        dz_ref[:, D_MODEL:] = dmv * pg_ref[...] * s2 * (1.0 - s2)
    return _rows(body, [(dm, 'r'), (zg, 'r'), (ps, 'r'), (pg, 'r')],
                 [(D_MODEL, BF16, 'r'), (D_MODEL, BF16, 'r'), (2 * D_MODEL, F32, 'r')], n=zg.shape[0], name=name)


def _final_loss(h, g, tgt, name):
    def body(h_ref, g_ref, t_ref, loss_ref, dh_ref, dg_ref):
        hv = h_ref[...]
        rstd = lax.rsqrt(jnp.mean(hv * hv, axis=-1, keepdims=True) + EPS)
        xh = hv * rstd
        err = xh * g_ref[...] - t_ref[...]
        part = 0.5 * jnp.sum(jnp.mean(err * err, axis=-1, keepdims=True), axis=0, keepdims=True)
        loss_ref[...] += jnp.broadcast_to(part, loss_ref.shape)
        dout = err * (1.0 / hv.shape[1])
        dg_ref[...] += jnp.sum(dout * xh, axis=0, keepdims=True)
        dxh = dout * g_ref[...]
        dh_ref[...] = rstd * (dxh - xh * jnp.mean(dxh * xh, axis=-1, keepdims=True))
    d = h.shape[1]
    return _rows(body, [(h, 'r'), (g, 'f'), (tgt, 'r')], [(LANE, F32, 'a'), (d, F32, 'r'), (d, F32, 'a')],
                 n=h.shape[0], name=name)


def _adamw_math(wv, gv, mv, vv):
    nm = ADAM_B1 * mv + (1.0 - ADAM_B1) * gv
    nv = ADAM_B2 * vv + (1.0 - ADAM_B2) * (gv * gv)
    m_hat = nm / (1.0 - ADAM_B1 ** ADAM_STEP)
    v_hat = nv / (1.0 - ADAM_B2 ** ADAM_STEP)
    return -ADAM_LR * (m_hat / (jnp.sqrt(v_hat) + ADAM_EPS) + ADAM_WD * wv), nm, nv


def _adamw(w, g, m, v, name):
    def body(w_ref, g_ref, m_ref, v_ref, d_ref, nm_ref, nv_ref):
        d_ref[...], nm_ref[...], nv_ref[...] = _adamw_math(w_ref[...], g_ref[...], m_ref[...], v_ref[...])
    c = w.shape[1]
    return _rows(body, [(w, 'r'), (g, 'r'), (m, 'r'), (v, 'r')], [(c, F32, 'r')] * 3, n=w.shape[0], name=name)


ADAMW_BLOCKS = 8


def _adamw_group(items, c_arr, name):
    per = ADAMW_BLOCKS // 2
    n = len(items)

    def body(c_ref, *refs):
        mine = (pl.program_id(0) // per) == c_ref[0]
        for k in range(n):
            w_ref, go_ref, gs_ref, m_ref, v_ref = refs[5 * k:5 * k + 5]
            g_ref, d_ref, nm_ref, nv_ref = refs[5 * n + 4 * k:5 * n + 4 * k + 4]
            gv = jnp.where(mine, go_ref[...], gs_ref[...])
            g_ref[...] = gv
            d_ref[...], nm_ref[...], nv_ref[...] = _adamw_math(w_ref[...], gv, m_ref[...], v_ref[...])

    in_specs, out_specs, out_shape, args = [pl.BlockSpec(memory_space=pltpu.SMEM)], [], [], [c_arr]
    for item in items:
        r, cols = item[0].shape
        assert r % (8 * ADAMW_BLOCKS) == 0, item[0].shape
        tr = r // ADAMW_BLOCKS
        full = pl.BlockSpec((tr, cols), lambda i: (i, 0))
        half = pl.BlockSpec((tr, cols), lambda i: (i % per, 0))
        in_specs += [full, half, half, full, full]
        out_specs += [full] * 4
        out_shape += [jax.ShapeDtypeStruct((r, cols), F32)] * 4
        args += list(item)
    return _pcall(body, name=name, grid=(ADAMW_BLOCKS,), in_specs=in_specs, out_specs=out_specs, out_shape=out_shape,
                  compiler_params=_params())(*args)


def _shift_rows(v, sh, down):
    rolled = pltpu.roll(v, sh if down else v.shape[0] - sh, axis=0)
    row = lax.broadcasted_iota(jnp.int32, v.shape, 0)
    keep = (row >= sh) if down else (row < v.shape[0] - sh)
    return jnp.where(keep, rolled, 0.0)


def _chain_segments(st_r, st_i, pw_r_ref, pw_i_ref, conj, down):
    vr, vi = st_r[...], st_i[...]
    sh, k = 1, 0
    while sh < SEG:
        pr, pi = pw_r_ref[k:k + 1, :], pw_i_ref[k:k + 1, :]
        if conj:
            pi = -pi
        sr, si = _shift_rows(vr, sh, down), _shift_rows(vi, sh, down)
        vr, vi = vr + pr * sr - pi * si, vi + pr * si + pi * sr
        sh, k = sh * 2, k + 1
    st_r[...] = _shift_rows(vr, 1, down)
    st_i[...] = _shift_rows(vi, 1, down)


def _expand_block(u_ref, t_ref, bu_ref):
    for j in range(BD_TILES):
        k = j % 4
        bu_ref[:, j * BD_ST:(j + 1) * BD_ST] = _dot(u_ref[:, k * BD_CH:(k + 1) * BD_CH], t_ref[j])


def _s5_scan(u, tiles, ar8, ai8, pw_r, pw_i, name):
    n = u.shape[0]
    rb = SCAN_ROWS
    nb, steps, lc = n // rb, rb // SEG, 512

    def body(u_ref, t_ref, ar_ref, ai_ref, pwr_ref, pwi_ref, x_ref, st_r, st_i, bu_ref):
        ph, b = pl.program_id(0), pl.program_id(1)

        @pl.when((ph == 0) & (b == 0))
        def _():
            st_r[...] = jnp.zeros_like(st_r)
            st_i[...] = jnp.zeros_like(st_i)

        _expand_block(u_ref, t_ref, bu_ref)

        def scan(store):
            for c in range(S5_GP // lc):
                re, im = slice(c * lc, (c + 1) * lc), slice(S5_GP + c * lc, S5_GP + (c + 1) * lc)
                a_r, a_i = ar_ref[:, re], ai_ref[:, re]

                def step(s, carry):
                    xr, xi = carry
                    rows = pl.ds(pl.multiple_of(s * SEG, SEG), SEG)
                    nr = a_r * xr - a_i * xi + bu_ref[rows, re]
                    ni = a_r * xi + a_i * xr + bu_ref[rows, im]
                    if store:
                        x_ref[rows, re] = nr
                        x_ref[rows, im] = ni
                    return nr, ni

                xr, xi = lax.fori_loop(0, steps, step, (st_r[:, re], st_i[:, re]), unroll=4)
                st_r[:, re] = xr
                st_i[:, re] = xi

        @pl.when(ph == 0)
        def _():
            scan(False)

        @pl.when((ph == 0) & (b == nb - 1))
        def _():
            _chain_segments(st_r, st_i, pwr_ref, pwi_ref, conj=False, down=True)

        @pl.when(ph == 1)
        def _():
            scan(True)

    full = lambda a: pl.BlockSpec(a.shape, lambda ph, b: (0, 0))
    return _pcall(body, name=name, grid=(2, nb),
                  in_specs=[pl.BlockSpec((rb, S5_W), lambda ph, b: (b, 0)), pl.BlockSpec(tiles.shape, lambda ph, b: (0, 0, 0)),
                            full(ar8), full(ai8), full(pw_r), full(pw_i)],
                  out_specs=pl.BlockSpec((rb, 2 * S5_GP), lambda ph, b: (b * ph, 0)),
                  out_shape=jax.ShapeDtypeStruct((n, 2 * S5_GP), F32),
                  scratch_shapes=[pltpu.VMEM((SEG, S5_GP), F32), pltpu.VMEM((SEG, S5_GP), F32),
                                  pltpu.VMEM((rb, 2 * S5_GP), F32)],
                  compiler_params=_params())(u, tiles, ar8, ai8, pw_r, pw_i)


def _s5_scan_bwd(dy, tiles, xs, ar8, ai8, pw_r, pw_i, name):
    n = dy.shape[0]
    rb = SCAN_ROWS
    nb, steps, lc = n // rb, rb // SEG, 256

    def body(dy_ref, t_ref, x_ref, ar_ref, ai_ref, pwr_ref, pwi_ref, lam_ref, da_ref, st_r, st_i, gx_ref):
        ph, b = pl.program_id(0), pl.program_id(1)

        @pl.when((ph == 0) & (b == 0))
        def _():
            st_r[...] = jnp.zeros_like(st_r)
            st_i[...] = jnp.zeros_like(st_i)
            da_ref[...] = jnp.zeros_like(da_ref)

        _expand_block(dy_ref, t_ref, gx_ref)

        def scan(store):
            for c in range(S5_GP // lc):
                re, im = slice(c * lc, (c + 1) * lc), slice(S5_GP + c * lc, S5_GP + (c + 1) * lc)
                a_r, a_i = ar_ref[:, re], ai_ref[:, re]

                def step(s, carry):
                    rows = pl.ds(pl.multiple_of((steps - 1 - s) * SEG, SEG), SEG)
                    if store:
                        lr, li, dr, di = carry
                        xr, xi = x_ref[rows, re], x_ref[rows, im]
                        dr = dr + lr * xr + li * xi
                        di = di + li * xr - lr * xi
                    else:
                        lr, li = carry
                    nr = a_r * lr + a_i * li + gx_ref[rows, re]
                    ni = a_r * li - a_i * lr + gx_ref[rows, im]
                    if store:
                        lam_ref[rows, re] = nr
                        lam_ref[rows, im] = ni
                        return nr, ni, dr, di
                    return nr, ni

                if store:
                    lr, li, dr, di = lax.fori_loop(0, steps, step, (st_r[:, re], st_i[:, re], da_ref[:, re], da_ref[:, im]),
                                                   unroll=4)
                    da_ref[:, re] = dr
                    da_ref[:, im] = di
                else:
                    lr, li = lax.fori_loop(0, steps, step, (st_r[:, re], st_i[:, re]), unroll=4)
                st_r[:, re] = lr
                st_i[:, re] = li

        @pl.when(ph == 0)
        def _():
            scan(False)

        @pl.when((ph == 0) & (b == nb - 1))
        def _():
            _chain_segments(st_r, st_i, pwr_ref, pwi_ref, conj=True, down=False)

        @pl.when(ph == 1)
        def _():
            scan(True)

    full = lambda a: pl.BlockSpec(a.shape, lambda ph, b: (0, 0))
    rev = lambda ph, b: (nb - 1 - b, 0)
    return _pcall(body, name=name, grid=(2, nb),
                  in_specs=[pl.BlockSpec((rb, S5_W), rev), pl.BlockSpec(tiles.shape, lambda ph, b: (0, 0, 0)),
                            pl.BlockSpec((rb, 2 * S5_GP), lambda ph, b: ((nb - 1 - b) * ph, 0)),
                            full(ar8), full(ai8), full(pw_r), full(pw_i)],
                  out_specs=[pl.BlockSpec((rb, 2 * S5_GP), lambda ph, b: (nb - 1 - b * ph, 0)),
                             pl.BlockSpec((SEG, 2 * S5_GP), lambda ph, b: (0, 0))],
                  out_shape=[jax.ShapeDtypeStruct((n, 2 * S5_GP), F32), jax.ShapeDtypeStruct((SEG, 2 * S5_GP), F32)],
                  scratch_shapes=[pltpu.VMEM((SEG, S5_GP), F32), pltpu.VMEM((SEG, S5_GP), F32),
                                  pltpu.VMEM((rb, 2 * S5_GP), F32)],
                  compiler_params=_params())(dy, tiles, xs, ar8, ai8, pw_r, pw_i)


def _s5_discretize(lam_re, lam_im, log_dt, b_re, b_im):
    dt = jnp.exp(log_dt)[:, None]
    mag = jnp.exp(lam_re * dt)
    ar = mag * jnp.cos(lam_im * dt)
    ai = mag * jnp.sin(lam_im * dt)
    den = lam_re * lam_re + lam_im * lam_im
    nr = ar - 1.0
    fr = (nr * lam_re + ai * lam_im) / den
    fi = (ai * lam_re - nr * lam_im) / den
    bbar_re = fr[:, :, None] * b_re - fi[:, :, None] * b_im
    bbar_im = fr[:, :, None] * b_im + fi[:, :, None] * b_re
    return ar, ai, bbar_re, bbar_im


BD_TILES, BD_CH, BD_ST, BD_GROUPS = 8, 128, 512, 8
BD_ROWS = 4096


def _bd_tiles(re, im):
    eye = jnp.eye(BD_GROUPS, dtype=re.dtype)

    def tiles(t):
        t = t.reshape(S5_G // BD_GROUPS, BD_GROUPS, S5_H, S5_P)
        return (t[:, :, :, None, :] * eye[None, :, None, :, None]).reshape(S5_G // BD_GROUPS, BD_CH, BD_ST)

    return jnp.concatenate([tiles(re), tiles(im)], axis=0)


def _bd_blocks(t):
    t = t.reshape(2, S5_G // BD_GROUPS, BD_GROUPS, S5_H, BD_GROUPS, S5_P)
    return jnp.einsum('rkahap->rkahp', t).reshape(2, S5_G, S5_H, S5_P)


def _bd_reduce(x, t, res, name):
    n = x.shape[0]
    tm = _pick(n, BD_ROWS, 16)

    def body(x_ref, t_ref, r_ref, o_ref):
        part = _dot(x_ref[...], t_ref[...], NT)

        @pl.when(pl.program_id(2) == 0)
        def _():
            o_ref[...] = r_ref[...] + part

        @pl.when(pl.program_id(2) == 1)
        def _():
            o_ref[...] += part

    return _pcall(body, name=name, grid=(n // tm, 4, 2),
                  in_specs=[pl.BlockSpec((tm, BD_ST), lambda i, k, r: (i, k + 4 * r)),
                            pl.BlockSpec((None, BD_CH, BD_ST), lambda i, k, r: (k + 4 * r, 0, 0)),
                            pl.BlockSpec((tm, BD_CH), lambda i, k, r: (i, k))],
                  out_specs=pl.BlockSpec((tm, BD_CH), lambda i, k, r: (i, k)),
                  out_shape=jax.ShapeDtypeStruct((n, S5_W), F32), compiler_params=_params())(x, t, res)


def _bd_outer(a, x, name):
    n = a.shape[0]
    tk = _pick(n, BD_ROWS, 16)
    nk = n // tk

    def body(a_ref, x_ref, o_ref):
        part = _dot(a_ref[...], x_ref[...], TN)

        @pl.when(pl.program_id(1) == 0)
        def _():
            o_ref[...] = part

        @pl.when(pl.program_id(1) > 0)
        def _():
            o_ref[...] += part

    return _pcall(body, name=name, grid=(BD_TILES, nk),
                  in_specs=[pl.BlockSpec((tk, BD_CH), lambda j, kk: (kk, j % 4)), pl.BlockSpec((tk, BD_ST), lambda j, kk: (kk, j))],
                  out_specs=pl.BlockSpec((None, BD_CH, BD_ST), lambda j, kk: (j, 0, 0)),
                  out_shape=jax.ShapeDtypeStruct((BD_TILES, BD_CH, BD_ST), F32), compiler_params=_params())(a, x)


def _permute_rows(t):
    n = t.shape[0]
    return t.reshape(SEG, n // SEG, t.shape[1]).transpose(1, 0, 2).reshape(n, t.shape[1])


def _unpermute_rows(t):
    n = t.shape[0]
    return t.reshape(n // SEG, SEG, t.shape[1]).transpose(1, 0, 2).reshape(n, t.shape[1])


def _segment_powers(ar, ai, seg_steps):
    pr, pi = ar.reshape(1, S5_GP), ai.reshape(1, S5_GP)
    e = 1
    while e < seg_steps:
        pr, pi = pr * pr - pi * pi, 2.0 * pr * pi
        e *= 2
    assert e == seg_steps, "segment length must be a power of two"
    rows_r, rows_i = [], []
    for _ in range(3):
        rows_r.append(pr)
        rows_i.append(pi)
        pr, pi = pr * pr - pi * pi, 2.0 * pr * pi
    pad = jnp.zeros((SEG - 3, S5_GP), F32)
    return jnp.concatenate(rows_r + [pad], axis=0), jnp.concatenate(rows_i + [pad], axis=0)


NT = (((1,), (1,)), ((), ()))
TN = (((0,), (0,)), ((), ()))


def _dot(a, b, dims=None, exact=False):
    dims = (((1,), (0,)), ((), ())) if dims is None else dims
    if exact:
        return lax.dot_general(a, b, dims, precision=HI, preferred_element_type=F32)
    return lax.dot_general(a.astype(BF16), b.astype(BF16), dims, preferred_element_type=F32)


def _dot01(a, b, dims=None, ones_first=True):
    x = b if ones_first else a
    hi = x.astype(BF16)
    lo = (x - hi.astype(F32)).astype(BF16)
    parts = [(_dot(a, p, dims) if ones_first else _dot(p, b, dims)) for p in (lo, hi)]
    return parts[0] + parts[1]


HEADS = range(4)


def _gla_chunk_fwd(qc, kc, vc, al, wup, bup, s_prev, tril):
    ones = jnp.ones((GLA_CHUNK, GLA_DV), F32)
    z = [_dot(al, wup[h]) + bup[h] for h in HEADS]
    la = [(jnp.minimum(z[h], 0.0) - jnp.log(1.0 + jnp.exp(-jnp.abs(z[h])))) * (1.0 / GLA_TAU) for h in HEADS]
    bc = [_dot01(tril, la[h]) for h in HEADS]
    blb = [_dot01(la[h], ones, TN, ones_first=False) for h in HEADS]
    bl = [bc[h][GLA_CHUNK - 1:GLA_CHUNK, :] for h in HEADS]
    ebc = [jnp.exp(bc[h]) for h in HEADS]
    qt = [qc[h] * (GLA_DK ** -0.5) * ebc[h] for h in HEADS]
    kt = [kc[h] * jnp.exp(-bc[h]) for h in HEADS]
    ke = [kc[h] * jnp.exp(bl[h] - bc[h]) for h in HEADS]
    sc = [_dot(qt[h], kt[h], NT) * tril for h in HEADS]
    oi = [_dot(sc[h], vc[h]) for h in HEADS]
    oo = [_dot(qt[h], s_prev[h]) for h in HEADS]
    o = [oi[h] + oo[h] for h in HEADS]
    return z, bc, bl, blb, ebc, qt, kt, ke, sc, o


GLA_ROWS = 512
GLA_CPB = GLA_ROWS // GLA_CHUNK


ZA_COLS = 5 * 512
SLOT = 128


def _pad_heads(w):
    r = w.shape[0]
    return jnp.pad(w.reshape(r, GLA_HEADS, GLA_DK), ((0, 0), (0, 0), (0, SLOT - GLA_DK))).reshape(r, GLA_HEADS * SLOT)


def _unpad_heads(w):
    r = w.shape[0]
    return w.reshape(r, GLA_HEADS, SLOT)[:, :, :GLA_DK].reshape(r, GLA_HEADS * GLA_DK)


def _gla_token_specs(blk):
    col = lambda cb: pl.BlockSpec((GLA_ROWS, 512), lambda j: (blk(j), cb))
    whole = lambda a: pl.BlockSpec(a.shape, lambda j: (0,) * a.ndim)
    return col, whole


def _head_ds(h, width):
    return pl.ds(h * SLOT, width)


def _tri(lower):
    ri = lax.broadcasted_iota(jnp.int32, (GLA_CHUNK, GLA_CHUNK), 0)
    ci = lax.broadcasted_iota(jnp.int32, (GLA_CHUNK, GLA_CHUNK), 1)
    return ((ri >= ci) if lower else (ri <= ci)).astype(F32)


def _gla_fwd(za, al, wup, bup, gn, name):
    n = za.shape[0]
    nc = n // GLA_CHUNK

    def body(q_ref, k_ref, v_ref, r_ref, al_ref, wup_ref, bup_ref, gn_ref, y_ref, sp_ref, s_ref):
        @pl.when(pl.program_id(0) == 0)
        def _():
            s_ref[...] = jnp.zeros_like(s_ref)

        tril = _tri(True)

        def chunk(c, carry):
            rows = pl.ds(pl.multiple_of(c * GLA_CHUNK, GLA_CHUNK), GLA_CHUNK)
            alc = al_ref[rows, :]
            vc = [v_ref[rows, _head_ds(h, GLA_DV)] for h in HEADS]
            s_prev = [s_ref[h] for h in HEADS]
            _, _, _, blb, _, _, _, ke, _, o = _gla_chunk_fwd(
                [q_ref[rows, _head_ds(h, GLA_DK)] for h in HEADS], [k_ref[rows, _head_ds(h, GLA_DK)] for h in HEADS],
                vc, alc, [wup_ref[h] for h in HEADS], [bup_ref[h] for h in HEADS], s_prev, tril)
            ds = [_dot(ke[h], vc[h], TN) for h in HEADS]
            for h in HEADS:
                rc = r_ref[rows, _head_ds(h, GLA_DV)]
                sp_ref[h, c] = s_prev[h]
                rstd = lax.rsqrt(jnp.mean(o[h] * o[h], axis=-1, keepdims=True) + EPS)
                y_ref[rows, _head_ds(h, GLA_DV)] = (o[h] * rstd * gn_ref[h] * (rc * _sigmoid(rc))).astype(BF16)
                s_ref[h] = jnp.exp(blb[h]) * s_prev[h] + ds[h]
            return carry

        lax.fori_loop(0, GLA_CPB, chunk, 0)

    col, whole = _gla_token_specs(lambda j: j)
    return _pcall(body, name=name, grid=(n // GLA_ROWS,),
                  in_specs=[col(1), col(2), col(3), col(4), pl.BlockSpec((GLA_ROWS, LANE), lambda j: (j, 0)),
                            whole(wup), whole(bup), whole(gn)],
                  out_specs=[pl.BlockSpec((GLA_ROWS, GLA_HEADS * GLA_DV), lambda j: (j, 0)),
                             pl.BlockSpec((GLA_HEADS, GLA_CPB, GLA_DK, GLA_DV), lambda j: (0, j, 0, 0))],
                  out_shape=[jax.ShapeDtypeStruct((n, GLA_HEADS * GLA_DV), BF16),
                             jax.ShapeDtypeStruct((GLA_HEADS, nc, GLA_DK, GLA_DV), F32)],
                  scratch_shapes=[pltpu.VMEM((GLA_HEADS, GLA_DK, GLA_DV), F32)],
                  compiler_params=_params())(za, za, za, za, al, wup, bup, gn)


def _gla_bwd(za, al, wup, bup, gn, sp, dy, du_s5, name):
    n = za.shape[0]
    nb = n // GLA_ROWS

    def body(q_ref, k_ref, v_ref, r_ref, al_ref, wup_ref, bup_ref, gn_ref, dy_ref, dus_ref, sp_ref,
             dza_ref, dz_ref, dgn_ref, dbup_ref, ds_ref):
        @pl.when(pl.program_id(0) == 0)
        def _():
            ds_ref[...] = jnp.zeros_like(ds_ref)
            dgn_ref[...] = jnp.zeros_like(dgn_ref)
            dbup_ref[...] = jnp.zeros_like(dbup_ref)

        tril, triu = _tri(True), _tri(False)
        dza_ref[:, 0:512] = dus_ref[...]
        dza_ref[:, 512:1536] = jnp.zeros((GLA_ROWS, 1024), F32)
        dz_ref[...] = jnp.zeros_like(dz_ref)

        def chunk(i, carry):
            c = GLA_CPB - 1 - i
            rows = pl.ds(pl.multiple_of(c * GLA_CHUNK, GLA_CHUNK), GLA_CHUNK)
            alc = al_ref[rows, :]
            qc = [q_ref[rows, _head_ds(h, GLA_DK)] for h in HEADS]
            kc = [k_ref[rows, _head_ds(h, GLA_DK)] for h in HEADS]
            vc = [v_ref[rows, _head_ds(h, GLA_DV)] for h in HEADS]
            s_prev = [sp_ref[h, c] for h in HEADS]
            ds = [ds_ref[h] for h in HEADS]
            z, bc, bl, blb, ebc, qt, kt, ke, sc, o = _gla_chunk_fwd(
                qc, kc, vc, alc, [wup_ref[h] for h in HEADS], [bup_ref[h] for h in HEADS], s_prev, tril)
            do = []
            for h in HEADS:
                rc = r_ref[rows, _head_ds(h, GLA_DV)]
                rs = lax.rsqrt(jnp.mean(o[h] * o[h], axis=-1, keepdims=True) + EPS)
                on = o[h] * rs
                sr = _sigmoid(rc)
                sil = rc * sr
                dyv, gnv = dy_ref[rows, _head_ds(h, GLA_DV)], gn_ref[h]
                dgn_ref[h] += jnp.sum(dyv * on * sil, axis=0, keepdims=True)
                dza_ref[rows, pl.ds(2048 + h * SLOT, GLA_DV)] = dyv * on * gnv * (sr * (1.0 + rc * (1.0 - sr)))
                don = dyv * gnv * sil
                do.append(rs * (don - on * jnp.mean(don * on, axis=-1, keepdims=True)))
            dp = [_dot(do[h], vc[h], NT) * tril for h in HEADS]
            dv1 = [_dot(sc[h], do[h], TN) for h in HEADS]
            dv2 = [_dot(ke[h], ds[h]) for h in HEADS]
            dq2 = [_dot(do[h], s_prev[h], NT) for h in HEADS]
            dke = [_dot(vc[h], ds[h], NT) for h in HEADS]
            ddec = [_dot01(jnp.ones((8, GLA_DV), F32), ds[h] * s_prev[h], NT)[0:1, :] for h in HEADS]
            dsn = [_dot(qt[h], do[h], TN) for h in HEADS]
            dq1 = [_dot(dp[h], kt[h]) for h in HEADS]
            dkt = [_dot(dp[h], qt[h], TN) for h in HEADS]
            dbc, dbl = [], []
            for h in HEADS:
                dqt = dq1[h] + dq2[h]
                dza_ref[rows, pl.ds(1536 + h * SLOT, GLA_DV)] = dv1[h] + dv2[h]
                ds_ref[h] = jnp.exp(blb[h]) * ds[h] + dsn[h]
                dza_ref[rows, pl.ds(512 + h * SLOT, GLA_DK)] = dqt * (GLA_DK ** -0.5) * ebc[h]
                dza_ref[rows, pl.ds(1024 + h * SLOT, GLA_DK)] = dkt[h] * jnp.exp(-bc[h]) + dke[h] * jnp.exp(bl[h] - bc[h])
                dbc.append(dqt * qt[h] - dkt[h] * kt[h] - dke[h] * ke[h])
                dbl.append(jnp.sum(dke[h] * ke[h], axis=0, keepdims=True) + ddec[h] * jnp.exp(bl[h]))
            dla = [_dot01(triu, dbc[h]) + dbl[h] for h in HEADS]
            for h in HEADS:
                dz = dla[h] * (1.0 - _sigmoid(z[h])) * (1.0 / GLA_TAU)
                dz_ref[rows, _head_ds(h, GLA_DK)] = dz
                dbup_ref[h] += jnp.sum(dz, axis=0, keepdims=True)
            return carry

        lax.fori_loop(0, GLA_CPB, chunk, 0)

    rev = lambda j: nb - 1 - j
    col, whole = _gla_token_specs(rev)
    tok = lambda w: pl.BlockSpec((GLA_ROWS, w), lambda j: (rev(j), 0))
    h1 = lambda w: pl.BlockSpec((GLA_HEADS, 1, w), lambda j: (0, 0, 0))
    s1 = lambda w: jax.ShapeDtypeStruct((GLA_HEADS, 1, w), F32)
    return _pcall(body, name=name, grid=(nb,),
                  in_specs=[col(1), col(2), col(3), col(4), tok(LANE), whole(wup), whole(bup), whole(gn), tok(512), tok(512),
                            pl.BlockSpec((GLA_HEADS, GLA_CPB, GLA_DK, GLA_DV), lambda j: (0, rev(j), 0, 0))],
                  out_specs=[tok(ZA_COLS), tok(GLA_HEADS * SLOT), h1(GLA_DV), h1(GLA_DK)],
                  out_shape=[jax.ShapeDtypeStruct((n, ZA_COLS), F32), jax.ShapeDtypeStruct((n, GLA_HEADS * SLOT), F32),
                             s1(GLA_DV), s1(GLA_DK)],
                  scratch_shapes=[pltpu.VMEM((GLA_HEADS, GLA_DK, GLA_DV), F32)],
                  compiler_params=_params())(za, za, za, za, al, wup, bup, gn, dy, du_s5, sp)


ANY = pl.BlockSpec(memory_space=pl.ANY)


def _place():
    x, y, c = lax.axis_index("x"), lax.axis_index("y"), lax.axis_index("c")
    chips = [(1 - x, y), (x, 1 - y), (1 - x, 1 - y)]
    return x, y, c, chips


def _remote(src, dst, ssem, rsem, dev):
    return pltpu.make_async_remote_copy(src_ref=src, dst_ref=dst, send_sem=ssem, recv_sem=rsem, device_id=dev,
                                        device_id_type=MESH_ID)


def _half(c, rows):
    h = rows // 2
    return pl.ds(pl.multiple_of(c * h, 8), h)


def _side_gather_ici(shards):
    def copies(ins, outs, ssem, rsem):
        x, y, c, chips = _place()
        mine = 2 * x + y
        cps = []
        for w in range(len(ins)):
            half = _half(c, ins[w].shape[0])
            cps.append(_remote(ins[w], outs[w].at[mine], ssem.at[4 * w], rsem.at[4 * w], (x, y, 1 - c)))
            for k, (px, py) in enumerate(chips):
                cps.append(_remote(ins[w].at[half], outs[w].at[mine, half], ssem.at[4 * w + 1 + k], rsem.at[4 * w + 1 + k],
                                   (px, py, c)))
        return cps

    return _Side(shards, [jax.ShapeDtypeStruct((4,) + s.shape, s.dtype) for s in shards], 4 * len(shards), copies)


def _side_gather_d2d(gathered):
    def copies(ins, outs, ssem, rsem):
        x, y, c, chips = _place()
        cps = []
        for w in range(len(outs)):
            half = _half(c, outs[w].shape[1])
            for k, (px, py) in enumerate(chips):
                theirs = outs[w].at[2 * px + py, half]
                cps.append(_remote(theirs, theirs, ssem.at[3 * w + k], rsem.at[3 * w + k], (x, y, 1 - c)))
        return cps

    return _Side(gathered, [jax.ShapeDtypeStruct(g.shape, g.dtype) for g in gathered], 3 * len(gathered), copies,
                 aliased=True)


def _side_swap_halves(grads):
    def copies(ins, outs, ssem, rsem):
        x, y, c, _ = _place()
        return [_remote(ins[w].at[:, _half(1 - c, ins[w].shape[1]), :], outs[w], ssem.at[w], rsem.at[w], (x, y, 1 - c))
                for w in range(len(ins))]

    return _Side(grads, [jax.ShapeDtypeStruct((4, g.shape[1] // 2, g.shape[2]), g.dtype) for g in grads], len(grads), copies)


def _side_scatter(sums):
    def copies(ins, outs, ssem, rsem):
        x, y, c, chips = _place()
        return [_remote(ins[w].at[2 * px + py], outs[w].at[k], ssem.at[3 * w + k], rsem.at[3 * w + k], (px, py, c))
                for w in range(len(ins)) for k, (px, py) in enumerate(chips)]

    return _Side(sums, [jax.ShapeDtypeStruct((3,) + s.shape[1:], s.dtype) for s in sums], 3 * len(sums), copies)


def _side_swap_reduced(halves):
    def copies(ins, outs, ssem, rsem):
        x, y, c, _ = _place()
        return [_remote(ins[w], outs[w], ssem.at[w], rsem.at[w], (x, y, 1 - c)) for w in range(len(ins))]

    return _Side(halves, [jax.ShapeDtypeStruct(h.shape, h.dtype) for h in halves], len(halves), copies)


SUM_BLOCKS = 2


def _chip_sums(gs, recvs, c_arr, name):
    n = len(gs)

    def body(c_ref, *refs):
        for k in range(n):
            refs[2 * n + k][...] = (refs[2 * k][...] + refs[2 * k + 1][...]).astype(BF16)

    in_specs, out_specs, out_shape, args = [], [], [], []
    for g, recv in zip(gs, recvs):
        _, r, cols = g.shape
        h = r // 2
        assert h % (16 * SUM_BLOCKS) == 0, g.shape
        tr = h // SUM_BLOCKS
        in_specs += [pl.BlockSpec((None, None, tr, cols), lambda s, i, c_ref: (s, c_ref[0], i, 0)),
                     pl.BlockSpec((None, tr, cols), lambda s, i, c_ref: (s, i, 0))]
        out_specs.append(pl.BlockSpec((None, tr, cols), lambda s, i, c_ref: (s, i, 0)))
        out_shape.append(jax.ShapeDtypeStruct((4, h, cols), BF16))
        args += [g.reshape(4, 2, h, cols), recv]
    grid_spec = pltpu.PrefetchScalarGridSpec(num_scalar_prefetch=1, grid=(4, SUM_BLOCKS), in_specs=in_specs,
                                             out_specs=out_specs)
    return _pcall(body, name=name, grid_spec=grid_spec, out_shape=out_shape, compiler_params=_params())(c_arr, *args)


def _owner_sums(sums, others, s_arr, name):
    n = len(sums)

    def body(s_ref, *refs):
        f = lambda v: v.astype(F32)
        for k in range(n):
            a_ref, o_ref = refs[2 * k], refs[2 * k + 1]
            refs[2 * n + k][...] = (f(a_ref[...]) + f(o_ref[0])) + (f(o_ref[1]) + f(o_ref[2]))

    in_specs, out_specs, out_shape, args = [], [], [], []
    for sm, ot in zip(sums, others):
        _, h, cols = sm.shape
        tr = h // SUM_BLOCKS
        in_specs += [pl.BlockSpec((None, tr, cols), lambda i, s_ref: (s_ref[0], i, 0)),
                     pl.BlockSpec((3, tr, cols), lambda i, s_ref: (0, i, 0))]
        out_specs.append(pl.BlockSpec((tr, cols), lambda i, s_ref: (i, 0)))
        out_shape.append(jax.ShapeDtypeStruct((h, cols), F32))
        args += [sm, ot]
    grid_spec = pltpu.PrefetchScalarGridSpec(num_scalar_prefetch=1, grid=(SUM_BLOCKS,), in_specs=in_specs,
                                             out_specs=out_specs)
    return _pcall(body, name=name, grid_spec=grid_spec, out_shape=out_shape, compiler_params=_params())(s_arr, *args)


def _side_small_sibling(v):
    def copies(ins, outs, ssem, rsem):
        x, y, c, _ = _place()
        return [_remote(ins[0], outs[0], ssem.at[0], rsem.at[0], (x, y, 1 - c))]

    return _Side([v], [jax.ShapeDtypeStruct(v.shape, F32)], 1, copies)


def _side_small_chips(v):
    def copies(ins, outs, ssem, rsem):
        x, y, c, chips = _place()
        return [_remote(ins[0], outs[0].at[k], ssem.at[k], rsem.at[k], (px, py, c)) for k, (px, py) in enumerate(chips)]

    return _Side([v], [jax.ShapeDtypeStruct((3,) + v.shape, F32)], 3, copies)


def _small_add(v, r, name):
    def body(v_ref, r_ref, o_ref):
        if r.ndim == 2:
            o_ref[...] = v_ref[...] + r_ref[...]
        else:
            o_ref[...] = (v_ref[...] + r_ref[0]) + (r_ref[1] + r_ref[2])

    vm = pl.BlockSpec(memory_space=pltpu.VMEM)
    return _pcall(body, name=name, in_specs=[vm, vm], out_specs=vm, out_shape=jax.ShapeDtypeStruct(v.shape, F32),
                  compiler_params=_params())(v, r)


def _merge_sides(sides):
    if len(sides) == 1:
        return sides[0]

    def copies(in_refs, out_refs, ssem, rsem):
        cps, i, o, q = [], 0, 0, 0
        for s in sides:
            ni, no = len(s.ins), len(s.out_shapes)
            cps += s.copies(in_refs[i:i + ni], out_refs[o:o + no], ssem.at[pl.ds(q, s.nsem)], rsem.at[pl.ds(q, s.nsem)])
            i, o, q = i + ni, o + no, q + s.nsem
        return cps

    assert not any(s.aliased for s in sides)
    return _Side(sum((s.ins for s in sides), []), sum((s.out_shapes for s in sides), []), sum(s.nsem for s in sides), copies)


def _tile_rows(size):
    return -(-size // (8 * LANE)) * 8


def _pack_small(parts):
    pieces = []
    for p in parts:
        flat = p.reshape(-1).astype(F32)
        pieces.append(jnp.pad(flat, (0, _tile_rows(p.size) * LANE - p.size)).reshape(-1, LANE))
    rows = sum(x.shape[0] for x in pieces)
    pieces.append(jnp.zeros(((-rows) % 64, LANE), F32))
    return jnp.concatenate(pieces, axis=0)


def _unpack_small(packed, like):
    out, pos = [], 0
    for p in like:
        rows = _tile_rows(p.size)
        out.append(packed[pos:pos + rows].reshape(-1)[:p.size].reshape(p.shape))
        pos += rows
    return out


FFN_FWD_ROWS, FFN_BWD_ROWS = 1024, 512
FFN_SUB_ROWS = 256


def _ffn_specs(n, d, fs, cap):
    rows = _pick(n, cap, 16)
    row = pl.BlockSpec((rows, d), lambda i, s: (i, 0))
    gain = pl.BlockSpec((1, d), lambda i, s: (0, 0))
    w_row = pl.BlockSpec((None, fs, d), lambda i, s: (s, 0, 0))
    hid = pl.BlockSpec((None, rows, fs), lambda i, s: (s, i, 0))
    return rows, row, gain, w_row, hid


def _ffn_fwd(h, g, w1t, w3t, w2, tag, plan):
    n, d = h.shape
    ns, fs, _ = w2.shape
    rows, row, gain, w_row, hid = _ffn_specs(n, d, fs, FFN_FWD_ROWS)
    sub = rows

    def body(h_ref, g_ref, w1_ref, w3_ref, w2_ref, out_ref, n1_ref, a_ref, b_ref, hm_ref, acc_ref):
        s = pl.program_id(1)

        @pl.when(s == 0)
        def _():
            xv = h_ref[...]
            rstd = lax.rsqrt(jnp.mean(xv * xv, axis=-1, keepdims=True) + EPS)
            n1_ref[...] = (xv * rstd * g_ref[...]).astype(BF16)
            acc_ref[...] = jnp.zeros_like(acc_ref)

        def up(j):
            n1 = n1_ref[j * sub:(j + 1) * sub, :]
            return _dot(n1, w1_ref[...], NT), _dot(n1, w3_ref[...], NT)

        cur = up(0)
        for j in range(rows // sub):
            nxt = up(j + 1) if (j + 1) * sub < rows else None
            a, b = cur
            r = slice(j * sub, (j + 1) * sub)
            hm = (a * _sigmoid(a) * b).astype(BF16)
            a_ref[r, :] = a.astype(BF16)
            b_ref[r, :] = b.astype(BF16)
            hm_ref[r, :] = hm
            acc_ref[r, :] += _dot(hm, w2_ref[...])
            cur = nxt

        @pl.when(s == ns - 1)
        def _():
            out_ref[...] = h_ref[...] + 0.5 * acc_ref[...]

    hid_shape = jax.ShapeDtypeStruct((ns, n, fs), BF16)
    plan.before(f"{tag}_fwd")
    out, n1, a, b, hm = _pcall(
        body, name=f"{tag}_fwd", grid=(n // rows, ns), in_specs=[row, gain, w_row, w_row, w_row],
        out_specs=[row, row, hid, hid, hid],
        out_shape=[jax.ShapeDtypeStruct((n, d), F32), jax.ShapeDtypeStruct((n, d), BF16), hid_shape, hid_shape, hid_shape],
        scratch_shapes=[pltpu.VMEM((rows, d), F32)], compiler_params=_params())(h, g, w1t, w3t, w2)
    plan.after(f"{tag}_fwd")
    return out, (h, n1, a, b, hm)


def _wgrad(a3, b, name, alpha=1.0):
    ns, n, fs = a3.shape
    d = b.shape[1]
    tk = _pick(n, 1024, 16)

    def body(a_ref, b_ref, o_ref):
        @pl.when(pl.program_id(0) == 0)
        def _():
            o_ref[...] = jnp.zeros_like(o_ref)

        bv = b_ref[...].astype(BF16)
        for s in range(ns):
            part = _dot(a_ref[s], bv, TN)
            o_ref[s] += part if alpha == 1.0 else alpha * part

    return _pcall(body, name=name, grid=(n // tk,),
                  in_specs=[pl.BlockSpec((ns, tk, fs), lambda k: (0, k, 0)), pl.BlockSpec((tk, d), lambda k: (k, 0))],
                  out_specs=pl.BlockSpec((ns, fs, d), lambda k: (0, 0, 0)),
                  out_shape=jax.ShapeDtypeStruct((ns, fs, d), F32), compiler_params=_params())(a3, b)


def _ffn_bwd(dout, saved, g, w1, w3, w2, tag, plan):
    h, n1, a, b, hm = saved
    n, d = h.shape
    ns, fs, _ = w2.shape
    rows, row, gain, w_row, hid = _ffn_specs(n, d, fs, FFN_BWD_ROWS)
    sub = _pick(rows, FFN_SUB_ROWS, 16)

    def body(do_ref, h_ref, g_ref, a_ref, b_ref, w1_ref, w3_ref, w2_ref, dh_ref, da_ref, db_ref, dg_ref, acc_ref):
        i, s = pl.program_id(0), pl.program_id(1)

        @pl.when(s == 0)
        def _():
            acc_ref[...] = jnp.zeros_like(acc_ref)

        @pl.when((s == 0) & (i == 0))
        def _():
            dg_ref[...] = jnp.zeros_like(dg_ref)

        def up(j):
            return _dot(0.5 * do_ref[j * sub:(j + 1) * sub, :], w2_ref[...], NT)

        cur = up(0)
        for j in range(rows // sub):
            nxt = up(j + 1) if (j + 1) * sub < rows else None
            r = slice(j * sub, (j + 1) * sub)
            av, bv = a_ref[r, :].astype(F32), b_ref[r, :].astype(F32)
            sg = _sigmoid(av)
            da = (cur * bv * (sg * (1.0 + av * (1.0 - sg)))).astype(BF16)
            db = (cur * av * sg).astype(BF16)
            da_ref[r, :] = da
            db_ref[r, :] = db
            acc_ref[r, :] += _dot(da, w1_ref[...]) + _dot(db, w3_ref[...])
            cur = nxt

        @pl.when(s == ns - 1)
        def _():
            xv, dn = h_ref[...], acc_ref[...]
            rstd = lax.rsqrt(jnp.mean(xv * xv, axis=-1, keepdims=True) + EPS)
            xh = xv * rstd
            dg_ref[...] += jnp.sum(dn * xh, axis=0, keepdims=True)
            dxh = dn * g_ref[...]
            dh_ref[...] = do_ref[...] + rstd * (dxh - xh * jnp.mean(dxh * xh, axis=-1, keepdims=True))

    hid_shape = jax.ShapeDtypeStruct((ns, n, fs), BF16)
    plan.before(f"{tag}_bwd")
    dh, da, db, dg = _pcall(
        body, name=f"{tag}_bwd", grid=(n // rows, ns), in_specs=[row, row, gain, hid, hid, w_row, w_row, w_row],
        out_specs=[row, hid, hid, gain],
        out_shape=[jax.ShapeDtypeStruct((n, d), F32), hid_shape, hid_shape, jax.ShapeDtypeStruct((1, d), F32)],
        scratch_shapes=[pltpu.VMEM((rows, d), F32)], compiler_params=_params())(dout, h, g, a, b, w1, w3, w2)
    plan.after(f"{tag}_bwd")
    plan.grads[f"{tag}_norm"] = dg
    plan.before(f"{tag}_gw2")
    gw2 = _wgrad(hm, dout, f"{tag}_gw2", alpha=0.5)
    plan.after(f"{tag}_gw2")
    plan.grads[f"{tag}_w2"] = gw2
    plan.before(f"{tag}_gw1")
    gw1 = _wgrad(da, n1, f"{tag}_gw1")
    plan.after(f"{tag}_gw1")
    plan.grads[f"{tag}_w1"] = gw1
    plan.before(f"{tag}_gw3")
    gw3 = _wgrad(db, n1, f"{tag}_gw3")
    plan.after(f"{tag}_gw3")
    return dh, dg, gw1, gw3, gw2


def _local_step(x, tgt, plan):
    n = x.shape[0]
    grads = plan.grads

    def f(name):
        w = plan.get(name)
        return w.reshape(1, D_MODEL) if name.endswith('_norm') and name != 'gla_out_norm' else w

    def carried(tag, fn, *args, **kw):
        plan.before(tag)
        out = fn(*args, **kw)
        plan.after(tag)
        return out

    h1, ffn1 = _ffn_fwd(x, f('ffn1_norm'), f('ffn1_w1'), f('ffn1_w3'), f('ffn1_w2'), "ffn1", plan)
    u = carried("mix_rms", _rms_fwd, h1, f('mix_norm'), "mix_rms")
    w_in = f('w_in')
    w_a = jnp.concatenate([w_in[:, :512], _pad_heads(w_in[:, 512:768]), _pad_heads(w_in[:, 768:1024]), w_in[:, 1024:2048]],
                          axis=1)
    w_al = jnp.pad(w_in[:, 2048:2048 + GLA_RANK], ((0, 0), (0, LANE - GLA_RANK)))
    w_g = w_in[:, 2048 + GLA_RANK:]
    za = carried("in_a", _mm, u, w_a, name="in_a")
    zg = carried("in_g", _mm, u, w_g, name="in_g")
    al = _mm(u, w_al, name="in_al")
    ar, ai, bbar_re, bbar_im = _s5_discretize(f('s5_lambda_re'), f('s5_lambda_im'), f('s5_log_dt'), f('s5_b_re'), f('s5_b_im'))
    t_b = _bd_tiles(bbar_re.transpose(0, 2, 1), bbar_im.transpose(0, 2, 1)).astype(BF16)
    t_c = _bd_tiles(f('s5_c_re'), -f('s5_c_im')).astype(BF16)
    ar8 = jnp.broadcast_to(ar.reshape(1, S5_GP), (SEG, S5_GP))
    ai8 = jnp.broadcast_to(ai.reshape(1, S5_GP), (SEG, S5_GP))
    pw_r, pw_i = _segment_powers(ar, ai, n // SEG)
    dskip = f('s5_d').reshape(1, S5_W)
    u_s5 = _permute_rows(za[:, :S5_W])
    xs = _s5_scan(u_s5, t_b, ar8, ai8, pw_r, pw_i, "s5_scan")
    ys_p = carried("s5_y", _bd_reduce, xs, t_c, _scale_rows(u_s5, dskip, "s5_skip"), "s5_y")
    ys = _unpermute_rows(ys_p)
    zgelu = _gelu_fwd(ys, "s5_gelu")
    t_glu = carried("s5_glu_t", _mm, zgelu, f('s5_glu_w'), bias=f('s5_glu_b').reshape(1, S5_W), name="s5_glu_t")
    y_s5 = _glu_fwd(zgelu, t_glu, "s5_glu")
    wup = jnp.pad(f('gla_a_up_w'), ((0, LANE - GLA_RANK), (0, 0)))
    wup_h = wup.reshape(LANE, GLA_HEADS, GLA_DK).transpose(1, 0, 2)
    bup_h = f('gla_a_up_b').reshape(GLA_HEADS, 1, GLA_DK)
    gn_h = f('gla_out_norm').reshape(GLA_HEADS, 1, GLA_DV)
    y_gla, s_prev = carried("gla_fwd", _gla_fwd, za, al, wup_h, bup_h, gn_h, "gla_fwd")
    ps = _mm(y_s5, f('proj_s5'), name="proj_s5")
    pg = carried("proj_gla", _mm, y_gla, f('proj_gla'), name="proj_gla")
    merged = _merge_fwd(zg, ps, pg, "merge")
    h2 = _mm(merged, f('w_out'), res=h1, name="w_out")
    h3, ffn2 = _ffn_fwd(h2, f('ffn2_norm'), f('ffn2_w1'), f('ffn2_w3'), f('ffn2_w2'), "ffn2", plan)
    loss, dh3, g_final = _final_loss(h3, f('final_norm').reshape(1, D_MODEL), tgt, "loss")
    plan.loss = loss[0, 0]
    grads['final_norm'] = g_final.reshape(D_MODEL)
    dh2, grads['ffn2_norm'], grads['ffn2_w1'], grads['ffn2_w3'], grads['ffn2_w2'] = _ffn_bwd(
        dh3, ffn2, f('ffn2_norm'), f('ffn2_w1'), f('ffn2_w3'), f('ffn2_w2'), "ffn2", plan)
    dm = carried("d_merged", _mm, dh2, f('w_out'), tb=True, name="d_merged")
    grads['w_out'] = _mm(merged, dh2, ta=True, name="g_w_out")
    dps, dpg, dzg = carried("d_merge", _merge_bwd, dm, zg, ps, pg, "d_merge")
    grads['proj_s5'] = _mm(y_s5, dps, ta=True, name="g_proj_s5")
    grads['proj_gla'] = _mm(y_gla, dpg, ta=True, name="g_proj_gla")
    dy_s5 = _mm(dps, f('proj_s5'), tb=True, name="d_y_s5")
    dy_gla = _mm(dpg, f('proj_gla'), tb=True, name="d_y_gla")
    dzgelu, dt_glu, g_glu_b = _glu_bwd1(dy_s5, zgelu, t_glu, "d_glu")
    grads['s5_glu_b'] = g_glu_b.reshape(S5_W)
    grads['s5_glu_w'] = _mm(zgelu, dt_glu, ta=True, name="g_glu_w")
    dzgelu = _mm(dt_glu, f('s5_glu_w'), tb=True, res=dzgelu, name="d_gelu")
    dys, du_skip, g_d = _glu_bwd2(_permute_rows(dzgelu), ys_p, u_s5, dskip, "d_s5_y")
    grads['s5_d'] = g_d.reshape(S5_G, S5_H)
    lam, da8 = _s5_scan_bwd(dys, t_c, xs, ar8, ai8, pw_r, pw_i, "s5_scan_bwd")
    g_c = _bd_blocks(_bd_outer(dys, xs, "g_s5_c"))
    grads['s5_c_re'], grads['s5_c_im'] = g_c[0], -g_c[1]
    g_b = _bd_blocks(_bd_outer(u_s5, lam, "g_s5_b")).transpose(0, 1, 3, 2)
    g_bbar_re, g_bbar_im = g_b[0], g_b[1]
    da = jnp.sum(da8, axis=0)
    g_ar, g_ai = da[:S5_GP].reshape(S5_G, S5_P), da[S5_GP:].reshape(S5_G, S5_P)
    _, disc_vjp = jax.vjp(_s5_discretize, f('s5_lambda_re'), f('s5_lambda_im'), f('s5_log_dt'), f('s5_b_re'), f('s5_b_im'))
    (grads['s5_lambda_re'], grads['s5_lambda_im'], grads['s5_log_dt'], grads['s5_b_re'],
     grads['s5_b_im']) = disc_vjp((g_ar, g_ai, g_bbar_re, g_bbar_im))
    du_s5 = _unpermute_rows(_bd_reduce(lam, t_b, du_skip, "d_s5_u"))
    dza, dz, dgn, dbup = carried("gla_bwd", _gla_bwd, za, al, wup_h, bup_h, gn_h, s_prev, dy_gla, du_s5, "gla_bwd")
    grads['gla_out_norm'] = dgn.reshape(GLA_HEADS * GLA_DV)
    grads['gla_a_up_b'] = dbup.reshape(GLA_HEADS * GLA_DK)
    grads['gla_a_up_w'] = _unpad_heads(_mm(al, dz, ta=True, name="g_a_up")[:GLA_RANK])
    dal = _mm(dz, _pad_heads(wup), tb=True, name="d_a_low")
    g_wa = _mm(u, dza, ta=True, name="g_in_a")
    g_wg = _mm(u, dzg, ta=True, name="g_in_g")
    g_wal = _mm(u, dal, ta=True, name="g_in_al")
    grads['w_in'] = jnp.concatenate([g_wa[:, :512], _unpad_heads(g_wa[:, 512:1024]), _unpad_heads(g_wa[:, 1024:1536]),
                                     g_wa[:, 1536:], g_wal[:, :GLA_RANK], g_wg], axis=1)
    du = carried("d_u_a", _mm, dza, w_a, tb=True, name="d_u_a")
    du = _mm(dzg, w_g, tb=True, res=du, name="d_u_g")
    du = _mm(dal, w_al, tb=True, res=du, name="d_u_al")
    dh1, g_mix = carried("d_mix_rms", _rms_bwd, h1, f('mix_norm'), du, dh2, "d_mix_rms")
    grads['mix_norm'] = g_mix
    dx, grads['ffn1_norm'], grads['ffn1_w1'], grads['ffn1_w3'], grads['ffn1_w2'] = _ffn_bwd(
        dh1, ffn1, f('ffn1_norm'), f('ffn1_w1'), f('ffn1_w3'), f('ffn1_w2'), "ffn1", plan)
    return loss[0, 0], dx


MIXER_WEIGHTS = ['w_in', 's5_glu_w', 'proj_s5', 'proj_gla', 'w_out', 'gla_a_up_w']
FFN1_WEIGHTS, FFN2_WEIGHTS = FFN_WEIGHTS[:3], FFN_WEIGHTS[3:]
TRANSPOSED = ['ffn1_w1', 'ffn1_w3', 'ffn2_w1', 'ffn2_w3']


def _local_shard(w, nm):
    return jnp.swapaxes(w, 1, 2)[0] if nm in TRANSPOSED else w[0]
FFN1_EARLY = ['ffn1_w2']
FFN1_LATE = ['ffn1_w1', 'ffn1_w3']
GRAD_GROUPS = {'ffn2': FFN2_WEIGHTS, 'mixer': ['w_out', 'proj_s5', 'proj_gla', 's5_glu_w', 'w_in'], 'ffn1': FFN1_WEIGHTS}


class _Plan:
    def __init__(self, a, c_arr, s_arr):
        self.a, self.c_arr, self.s_arr = a, c_arr, s_arr
        self.grads, self.weights, self.riding = {}, {}, {}
        self.g4s, self.chip_sums, self.halves, self.sib_halves = {}, {}, {}, {}
        for nm in SMALL:
            if nm != 'gla_a_up_w':
                self.weights[nm] = a[nm] if nm == 'final_norm' else a[nm][0]
        ici = _side_gather_ici(self._shards(FFN1_WEIGHTS))
        _run_side(ici, "gather_ffn1_ici")
        self._gathered(FFN1_WEIGHTS, _run_side(_side_gather_d2d(ici.outs), "gather_ffn1_d2d"))

    def _shards(self, names):
        return [_local_shard(self.a[nm], nm).astype(F32 if nm == 'gla_a_up_w' else BF16) for nm in names]

    def _gathered(self, names, arrs):
        for nm, g4 in zip(names, arrs):
            if nm in FFN_WEIGHTS:
                self.weights[nm] = g4
            elif nm in COL_SHARDED:
                self.weights[nm] = jnp.concatenate([g4[s] for s in range(4)], axis=1)
            else:
                self.weights[nm] = g4.reshape(4 * g4.shape[1], g4.shape[2])

    def get(self, name):
        return self.weights[name]

    def _shard_major(self, nm):
        g = self.grads[nm]
        if nm in FFN_WEIGHTS:
            return g
        if nm in COL_SHARDED:
            return jnp.stack(jnp.split(g, 4, axis=1))
        return g.reshape(4, g.shape[0] // 4, g.shape[1])

    def _schedule(self, tag):
        grp = GRAD_GROUPS
        gathers = {"ffn1_fwd": ('ici', MIXER_WEIGHTS), "mix_rms": ('d2d', MIXER_WEIGHTS),
                   "in_a": ('ici', FFN2_WEIGHTS[:1]), "in_g": ('d2d', FFN2_WEIGHTS[:1]),
                   "s5_y": ('ici', FFN2_WEIGHTS[1:2]), "s5_glu_t": ('d2d', FFN2_WEIGHTS[1:2]),
                   "gla_fwd": ('ici', FFN2_WEIGHTS[2:]), "proj_gla": ('d2d', FFN2_WEIGHTS[2:])}
        if tag in gathers:
            kind, names = gathers[tag]
            key = tuple(names)
            if kind == 'ici':
                return [(_side_gather_ici(self._shards(names)), lambda outs: self.riding.update({key: outs}))]
            return [(_side_gather_d2d(self.riding[key]), lambda outs: self._gathered(names, outs))]
        steps = {"ffn2_gw1": (['ffn2_w2'], 0), "ffn2_gw3": (['ffn2_w1'], 0), "d_merged": (['ffn2_w3'], 0),
                 "gla_bwd": (grp['ffn2'], 1), "d_mix_rms": (grp['ffn2'], 2),
                 "d_u_a": (grp['mixer'], 0), "ffn1_bwd": (grp['mixer'], 1), "ffn1_gw2": (grp['mixer'], 2),
                 "ffn1_gw1": (FFN1_EARLY, 0), "ffn1_gw3": (FFN1_EARLY, 1), "adamw_early": (FFN1_LATE, 1)}
        entries = [self._reduce_stage(*steps[tag])] if tag in steps else []
        if tag == "ffn1_gw2":
            entries.append(self._small_stage(0))
        if tag == "ffn1_gw1":
            entries.append(self._small_stage(1))
        return entries

    def _small_stage(self, stage):
        if stage == 0:
            a, grads = self.a, self.grads
            self.small_parts = ([grads[nm].reshape(a[nm].shape) for nm in SMALL if nm != 'gla_a_up_w']
                                + [grads['gla_a_up_w'], self.loss.reshape(1)])
            packed = _pack_small(self.small_parts)

            def done(outs):
                self.small_pair = _small_add(packed, outs[0], "small_sum_pair")
            return _side_small_sibling(packed), done

        def done(outs):
            self.small_total = _small_add(self.small_pair, outs[0], "small_sum_chips")
        return _side_small_chips(self.small_pair), done

    def _reduce_stage(self, names, stage):
        if stage == 0:
            for nm in names:
                self.g4s[nm] = self._shard_major(nm)

            def done(outs):
                sums = _chip_sums([self.g4s[nm] for nm in names], outs, self.c_arr, f"chip_sum_{names[0]}")
                self.chip_sums.update(zip(names, sums))
            return _side_swap_halves([self.g4s[nm] for nm in names]), done
        if stage == 1:
            def done(outs):
                halves = _owner_sums([self.chip_sums[nm] for nm in names], outs, self.s_arr, f"owner_sum_{names[0]}")
                self.halves.update(zip(names, halves))
            return _side_scatter([self.chip_sums[nm] for nm in names]), done

        def done(outs):
            self.sib_halves.update(zip(names, outs))
        return _side_swap_reduced([self.halves[nm] for nm in names]), done

    def before(self, tag):
        entries = self._schedule(tag)
        if entries:
            merged = _merge_sides([side for side, _ in entries])
            self.riding[tag] = (merged, entries)
            _RIDER.append(merged)

    def after(self, tag):
        if tag in self.riding:
            merged, entries = self.riding.pop(tag)
            assert not _RIDER and merged.outs is not None, tag
            pos = 0
            for side, done in entries:
                done(merged.outs[pos:pos + len(side.out_shapes)])
                pos += len(side.out_shapes)

    def finish_alone(self, stage):
        names = FFN1_LATE if stage == 0 else GRAD_GROUPS['ffn1']
        side, done = self._reduce_stage(names, stage)
        done(_run_side(side, f"grad_ffn1_stage{stage}"))


def _train_step(a):
    x = a['x'][0]
    tgt = a['loss_target'][0]
    xi, yi, ci = lax.axis_index("x"), lax.axis_index("y"), lax.axis_index("c")
    c_arr = jnp.reshape(ci, (1,)).astype(jnp.int32)
    s_arr = jnp.reshape(2 * xi + yi, (1,)).astype(jnp.int32)
    plan = _Plan(a, c_arr, s_arr)
    loss, dx = _local_step(x, tgt, plan)
    red = {}
    small_sum = _unpack_small(plan.small_total, plan.small_parts)
    small_names = [nm for nm in SMALL if nm != 'gla_a_up_w']
    for nm, g in zip(small_names, small_sum[:-2]):
        red[nm] = g
    loss = small_sum[-1].reshape(())
    g_up = small_sum[-2]
    red['gla_a_up_w'] = lax.dynamic_slice(g_up, (0, (2 * xi + yi) * GLA_DK), (GLA_RANK, GLA_DK))
    out_g, out_d, out_m, out_v = {}, {}, {}, {}

    def update(names, tag):
        items = [(_local_shard(a[nm], nm), plan.halves[nm], plan.sib_halves[nm], _local_shard(a['m_' + nm], nm),
                  _local_shard(a['v_' + nm], nm)) for nm in names]
        plan.before(tag)
        res = _adamw_group(items, c_arr, tag)
        plan.after(tag)
        for k, nm in enumerate(names):
            back = (lambda t: jnp.swapaxes(t[None], 1, 2)) if nm in TRANSPOSED else (lambda t: t[None])
            out_g[nm], out_d[nm], out_m[nm], out_v[nm] = (back(t) for t in res[4 * k:4 * k + 4])

    plan.finish_alone(0)
    update([nm for nm in SHARDED if nm not in GRAD_GROUPS['ffn1']], "adamw_early")
    plan.finish_alone(2)
    update(GRAD_GROUPS['ffn1'], "adamw_ffn1")
    rest = [nm for nm in WEIGHTS if nm not in SHARDED]
    pk = lambda pre: _pack_small([a[pre + nm] for nm in rest])
    d, nm_, nv_ = _adamw(pk(''), _pack_small([red[nm] for nm in rest]), pk('m_'), pk('v_'), "adamw_small")
    like = [a[nm] for nm in rest]
    for nm, g, dd, mm_, vv_ in zip(rest, [red[nm].reshape(a[nm].shape) for nm in rest], _unpack_small(d, like),
                                   _unpack_small(nm_, like), _unpack_small(nv_, like)):
        out_g[nm], out_d[nm], out_m[nm], out_v[nm] = g, dd, mm_, vv_
    return (loss, dx[None], *[out_g[nm] for nm in WEIGHTS], *[out_d[nm] for nm in WEIGHTS],
            *[out_m[nm] for nm in WEIGHTS], *[out_v[nm] for nm in WEIGHTS])


def kernel(x, ffn1_norm, ffn1_w1, ffn1_w3, ffn1_w2, mix_norm, w_in, s5_lambda_re, s5_lambda_im, s5_log_dt, s5_b_re, s5_b_im, s5_c_re, s5_c_im, s5_d, s5_glu_w, s5_glu_b, gla_a_up_w, gla_a_up_b, gla_out_norm, proj_s5, proj_gla, w_out, ffn2_norm, ffn2_w1, ffn2_w3, ffn2_w2, final_norm, loss_target, m_ffn1_norm, m_ffn1_w1, m_ffn1_w3, m_ffn1_w2, m_mix_norm, m_w_in, m_s5_lambda_re, m_s5_lambda_im, m_s5_log_dt, m_s5_b_re, m_s5_b_im, m_s5_c_re, m_s5_c_im, m_s5_d, m_s5_glu_w, m_s5_glu_b, m_gla_a_up_w, m_gla_a_up_b, m_gla_out_norm, m_proj_s5, m_proj_gla, m_w_out, m_ffn2_norm, m_ffn2_w1, m_ffn2_w3, m_ffn2_w2, m_final_norm, v_ffn1_norm, v_ffn1_w1, v_ffn1_w3, v_ffn1_w2, v_mix_norm, v_w_in, v_s5_lambda_re, v_s5_lambda_im, v_s5_log_dt, v_s5_b_re, v_s5_b_im, v_s5_c_re, v_s5_c_im, v_s5_d, v_s5_glu_w, v_s5_glu_b, v_gla_a_up_w, v_gla_a_up_b, v_gla_out_norm, v_proj_s5, v_proj_gla, v_w_out, v_ffn2_norm, v_ffn2_w1, v_ffn2_w3, v_ffn2_w2, v_final_norm):
    return _train_step(dict(locals()))
```

```python
import functools

import jax
import jax.numpy as jnp
from jax import lax
from jax.experimental import pallas as pl
from jax.experimental.pallas import tpu as pltpu

F32 = jnp.float32
BF16 = jnp.bfloat16
HI = lax.Precision.HIGHEST
MESH_ID = pl.DeviceIdType.MESH

D_MODEL = 1024
EPS = 1e-6
S5_G, S5_P, S5_H = 32, 64, 16
S5_W = S5_G * S5_H
S5_GP = S5_G * S5_P
SEG = 8
SCAN_ROWS = 256
GLA_HEADS, GLA_DK, GLA_DV = 4, 64, 128
GLA_CHUNK = 64
GLA_TAU = 16.0
GLA_RANK = 16
ADAM_LR, ADAM_B1, ADAM_B2, ADAM_EPS, ADAM_WD, ADAM_STEP = 0.001, 0.9, 0.999, 1e-08, 0.01, 10
V7X_VMEM_LIMIT = 56 * 1024 * 1024
LANE = 128

WEIGHTS = ['ffn1_norm', 'ffn1_w1', 'ffn1_w3', 'ffn1_w2', 'mix_norm', 'w_in', 's5_lambda_re', 's5_lambda_im',
           's5_log_dt', 's5_b_re', 's5_b_im', 's5_c_re', 's5_c_im', 's5_d', 's5_glu_w', 's5_glu_b', 'gla_a_up_w',
           'gla_a_up_b', 'gla_out_norm', 'proj_s5', 'proj_gla', 'w_out', 'ffn2_norm', 'ffn2_w1', 'ffn2_w3',
           'ffn2_w2', 'final_norm']
SHARDED = ['ffn1_w1', 'ffn1_w3', 'ffn1_w2', 'w_in', 's5_glu_w', 'proj_s5', 'proj_gla', 'w_out',
           'ffn2_w1', 'ffn2_w3', 'ffn2_w2']
COL_SHARDED = ['ffn1_w1', 'ffn1_w3', 'w_in', 'proj_s5', 'proj_gla', 'ffn2_w1', 'ffn2_w3', 'gla_a_up_w']
SMALL = [n for n in WEIGHTS if n not in SHARDED]
FFN_WEIGHTS = ['ffn1_w1', 'ffn1_w3', 'ffn1_w2', 'ffn2_w1', 'ffn2_w3', 'ffn2_w2']


def _params(**kw):
    return pltpu.CompilerParams(vmem_limit_bytes=V7X_VMEM_LIMIT, **kw)


class _Side:
    def __init__(self, ins, out_shapes, nsem, copies, aliased=False):
        self.ins, self.out_shapes, self.nsem, self.copies, self.aliased = list(ins), list(out_shapes), nsem, copies, aliased
        self.outs = None


_RIDER = []


def _pcall(body, **kw):
    if _RIDER:
        return _carry(body, _RIDER.pop(), **kw)
    return pl.pallas_call(body, **kw)


def _carry(body, side, *, name, grid, in_specs, out_specs, out_shape, scratch_shapes=(), compiler_params=None):
    del compiler_params
    single = not isinstance(out_shape, (list, tuple))
    out_specs = [out_specs] if single else list(out_specs)
    out_shape = [out_shape] if single else list(out_shape)
    n_in, n_out, n_scr = len(in_specs), len(out_shape), len(scratch_shapes)
    s_in, s_out = len(side.ins), len(side.out_shapes)
    any_spec = pl.BlockSpec(memory_space=pl.ANY)

    def wrapped(*refs):
        cuts = [n_in, s_in, n_out, s_out, n_scr]
        parts, pos = [], 0
        for c in cuts:
            parts.append(refs[pos:pos + c])
            pos += c
        ins, sins, outs, souts, scr = parts
        ssem, rsem = refs[pos], refs[pos + 1]
        first = last = None
        for d, g in enumerate(grid):
            i = pl.program_id(d)
            first = (i == 0) if first is None else first & (i == 0)
            last = (i == g - 1) if last is None else last & (i == g - 1)

        @pl.when(first)
        def _():
            for cp in side.copies(sins, souts, ssem, rsem):
                cp.start()

        body(*ins, *outs, *scr)

        @pl.when(last)
        def _():
            for cp in side.copies(sins, souts, ssem, rsem):
                cp.wait()

    call = pl.pallas_call(
        wrapped, name=name, grid=grid, in_specs=list(in_specs) + [any_spec] * s_in,
        out_specs=out_specs + [any_spec] * s_out, out_shape=out_shape + side.out_shapes,
        scratch_shapes=list(scratch_shapes) + [pltpu.SemaphoreType.DMA((side.nsem,)), pltpu.SemaphoreType.DMA((side.nsem,))],
        input_output_aliases={n_in + j: n_out + j for j in range(s_in)} if side.aliased else {},
        compiler_params=_params(has_side_effects=True))

    def run(*args):
        res = call(*args, *side.ins)
        side.outs = list(res[n_out:])
        return res[0] if single else list(res[:n_out])

    return run


def _run_side(side, name):
    s_in, s_out = len(side.ins), len(side.out_shapes)
    any_spec = pl.BlockSpec(memory_space=pl.ANY)

    def body(*refs):
        sins, souts = refs[:s_in], refs[s_in:s_in + s_out]
        ssem, rsem = refs[s_in + s_out:]
        cps = side.copies(sins, souts, ssem, rsem)
        for cp in cps:
            cp.start()
        for cp in cps:
            cp.wait()

    side.outs = list(pl.pallas_call(
        body, name=name, in_specs=[any_spec] * s_in, out_specs=[any_spec] * s_out, out_shape=side.out_shapes,
        scratch_shapes=[pltpu.SemaphoreType.DMA((side.nsem,)), pltpu.SemaphoreType.DMA((side.nsem,))],
        input_output_aliases={j: j for j in range(s_in)} if side.aliased else {},
        compiler_params=pltpu.CompilerParams(has_side_effects=True))(*side.ins))
    return side.outs


def _pick(n, cap, quantum):
    if n <= cap:
        return n
    best = None
    for t in range(quantum, cap + 1, quantum):
        if n % t == 0:
            best = t
    assert best is not None, (n, cap, quantum)
    return best


def _sigmoid(x):
    return jax.nn.sigmoid(x)


def _mm(a, b, *, name, ta=False, tb=False, out_dtype=F32, alpha=1.0, res=None, bias=None, exact=False, shard=None):
    ns = 4
    (k_a, m) = a.shape[-2:] if ta else a.shape[-2:][::-1]
    (k_b, n) = b.shape[-2:][::-1] if tb else b.shape[-2:]
    assert k_a == k_b, (a.shape, b.shape, ta, tb)
    assert (a.ndim == 3) == (shard in ('k', 'm')) and (b.ndim == 3) == (shard in ('n', 'k'))
    k = k_a
    tm = _pick(m, 1024, 128)
    tn = _pick(n, 1024, 128)
    tk = _pick(k, 1024, 128)
    pm, pn, pk = m // tm, n // tn, k // tk
    gm = pm * (ns if shard == 'm' else 1)
    gn = pn * (ns if shard == 'n' else 1)
    gk = pk * (ns if shard == 'k' else 1)
    dims = (((0,) if ta else (1,), (1,) if tb else (0,)), ((), ()))
    op_dtype = F32 if exact else BF16

    def body(*refs):
        a_ref, b_ref = refs[0], refs[1]
        pos = 2
        res_ref = bias_ref = None
        if res is not None:
            res_ref = refs[pos]
            pos += 1
        if bias is not None:
            bias_ref = refs[pos]
            pos += 1
        o_ref, acc_ref = refs[pos], refs[pos + 1]
        kk = pl.program_id(2)

        @pl.when(kk == 0)
        def _():
            acc_ref[...] = jnp.zeros_like(acc_ref)

        acc_ref[...] += lax.dot_general(a_ref[...].astype(op_dtype), b_ref[...].astype(op_dtype), dims,
                                        precision=HI if exact else None, preferred_element_type=F32)

        @pl.when(kk == gk - 1)
        def _():
            o = acc_ref[...]
            if alpha != 1.0:
                o = o * alpha
            if bias_ref is not None:
                o = o + bias_ref[...]
            if res_ref is not None:
                o = o + res_ref[...]
            o_ref[...] = o.astype(out_dtype)

    def spec(block, sharded_on, order):
        per = {'m': pm, 'n': pn, 'k': pk}

        def index(i, j, kk):
            g = {'m': i, 'n': j, 'k': kk}
            r, c = order(i % pm if shard == 'm' else i, j % pn if shard == 'n' else j, kk % pk if shard == 'k' else kk)
            if sharded_on is None:
                return (r, c)
            return (g[sharded_on] // per[sharded_on], r, c)

        return pl.BlockSpec(block if sharded_on is None else (None,) + block, index)

    a_sh = shard if shard in ('k', 'm') else None
    b_sh = shard if shard in ('n', 'k') else None
    o_sh = shard if shard in ('n', 'm') else None
    a_spec = spec((tk, tm), a_sh, lambda i, j, kk: (kk, i)) if ta else spec((tm, tk), a_sh, lambda i, j, kk: (i, kk))
    b_spec = spec((tn, tk), b_sh, lambda i, j, kk: (j, kk)) if tb else spec((tk, tn), b_sh, lambda i, j, kk: (kk, j))
    ins, in_specs = [a, b], [a_spec, b_spec]
    if res is not None:
        assert o_sh is None
        ins.append(res)
        in_specs.append(pl.BlockSpec((tm, tn), lambda i, j, kk: (i, j)))
    if bias is not None:
        assert o_sh is None
        ins.append(bias)
        in_specs.append(pl.BlockSpec((1, tn), lambda i, j, kk: (0, j)))
    out_shape = (m, n) if o_sh is None else (ns, m, n)
    return _pcall(body, name=name, grid=(gm, gn, gk), in_specs=in_specs,
                  out_specs=spec((tm, tn), o_sh, lambda i, j, kk: (i, j)),
                  out_shape=jax.ShapeDtypeStruct(out_shape, out_dtype),
                  scratch_shapes=[pltpu.VMEM((tm, tn), F32)], compiler_params=_params())(*ins)


ROWS_VMEM_BUDGET = 24 * 1024 * 1024


def _rows(body, ins, outs, *, n, name):
    cols = sum(a.shape[1] for a, kind in ins if kind == 'r') + sum(c for c, _, kind in outs if kind == 'r')
    cap = 256
    while cap < 2048 and 2 * 4 * cols * (2 * cap) <= ROWS_VMEM_BUDGET:
        cap *= 2
    tm = _pick(n, cap, 16)
    in_specs = []
    for arr, kind in ins:
        if kind == 'r':
            in_specs.append(pl.BlockSpec((tm, arr.shape[1]), lambda i: (i, 0)))
        else:
            in_specs.append(pl.BlockSpec(arr.shape, lambda i: (0, 0)))
    out_specs, out_shape = [], []
    for cols, dtype, kind in outs:
        if kind == 'r':
            out_specs.append(pl.BlockSpec((tm, cols), lambda i: (i, 0)))
            out_shape.append(jax.ShapeDtypeStruct((n, cols), dtype))
        else:
            out_specs.append(pl.BlockSpec((1, cols), lambda i: (0, 0)))
            out_shape.append(jax.ShapeDtypeStruct((1, cols), dtype))
    n_in = len(ins)
    acc_ids = [j for j, o in enumerate(outs) if o[2] == 'a']

    def wrapped(*refs):
        if acc_ids:
            @pl.when(pl.program_id(0) == 0)
            def _():
                for j in acc_ids:
                    refs[n_in + j][...] = jnp.zeros_like(refs[n_in + j])
        body(*refs)

    res = _pcall(wrapped, name=name, grid=(n // tm,), in_specs=in_specs, out_specs=out_specs, out_shape=out_shape,
                 compiler_params=_params())(*[a for a, _ in ins])
    return res


def _rms_fwd(x, g, name):
    def body(x_ref, g_ref, o_ref):
        xv = x_ref[...]
        rstd = lax.rsqrt(jnp.mean(xv * xv, axis=-1, keepdims=True) + EPS)
        o_ref[...] = (xv * rstd * g_ref[...]).astype(BF16)
    return _rows(body, [(x, 'r'), (g, 'f')], [(x.shape[1], BF16, 'r')], n=x.shape[0], name=name)[0]


def _rms_bwd(x, g, dn, dres, name):
    def body(x_ref, g_ref, dn_ref, dres_ref, dx_ref, dg_ref):
        xv = x_ref[...]
        rstd = lax.rsqrt(jnp.mean(xv * xv, axis=-1, keepdims=True) + EPS)
        xh = xv * rstd
        dn = dn_ref[...]
        dg_ref[...] += jnp.sum(dn * xh, axis=0, keepdims=True)
        dxh = dn * g_ref[...]
        dx_ref[...] = dres_ref[...] + rstd * (dxh - xh * jnp.mean(dxh * xh, axis=-1, keepdims=True))
    d = x.shape[1]
    return _rows(body, [(x, 'r'), (g, 'f'), (dn, 'r'), (dres, 'r')], [(d, F32, 'r'), (d, F32, 'a')],
                 n=x.shape[0], name=name)


def _gelu_parts(y):
    c0 = 0.7978845608028654
    inner = c0 * (y + 0.044715 * y * y * y)
    th = jnp.tanh(inner)
    return th, c0 * (1.0 + 3.0 * 0.044715 * y * y)


def _gelu_fwd(y, name):
    def body(y_ref, o_ref):
        yv = y_ref[...]
        th, _ = _gelu_parts(yv)
        o_ref[...] = 0.5 * yv * (1.0 + th)
    return _rows(body, [(y, 'r')], [(y.shape[1], F32, 'r')], n=y.shape[0], name=name)[0]


def _glu_fwd(zg, t, name):
    def body(z_ref, t_ref, o_ref):
        o_ref[...] = (z_ref[...] * _sigmoid(t_ref[...])).astype(BF16)
    return _rows(body, [(zg, 'r'), (t, 'r')], [(zg.shape[1], BF16, 'r')], n=zg.shape[0], name=name)[0]


def _glu_bwd1(dy, zg, t, name):
    def body(dy_ref, z_ref, t_ref, dz_ref, dt_ref, db_ref):
        dyv, zv = dy_ref[...], z_ref[...]
        sg = _sigmoid(t_ref[...])
        dz_ref[...] = dyv * sg
        dt = dyv * zv * sg * (1.0 - sg)
        dt_ref[...] = dt.astype(BF16)
        db_ref[...] += jnp.sum(dt, axis=0, keepdims=True)
    w = zg.shape[1]
    return _rows(body, [(dy, 'r'), (zg, 'r'), (t, 'r')], [(w, F32, 'r'), (w, BF16, 'r'), (w, F32, 'a')],
                 n=zg.shape[0], name=name)


def _glu_bwd2(dzg, ys, u, dskip, name):
    def body(dz_ref, y_ref, u_ref, d_ref, dy_ref, du_ref, dd_ref):
        yv = y_ref[...]
        th, dinner = _gelu_parts(yv)
        dy = dz_ref[...] * (0.5 * (1.0 + th) + 0.5 * yv * (1.0 - th * th) * dinner)
        dy_ref[...] = dy
        du_ref[...] = dy * d_ref[...]
        dd_ref[...] += jnp.sum(dy * u_ref[...], axis=0, keepdims=True)
    w = ys.shape[1]
    return _rows(body, [(dzg, 'r'), (ys, 'r'), (u, 'r'), (dskip, 'f')], [(w, F32, 'r'), (w, F32, 'r'), (w, F32, 'a')],
                 n=ys.shape[0], name=name)


def _scale_rows(u, dskip, name):
    def body(u_ref, d_ref, o_ref):
        o_ref[...] = u_ref[...] * d_ref[...]
    return _rows(body, [(u, 'r'), (dskip, 'f')], [(u.shape[1], F32, 'r')], n=u.shape[0], name=name)[0]


def _merge_fwd(zg, ps, pg, name):
    def body(z_ref, ps_ref, pg_ref, o_ref):
        zv = z_ref[...]
        o_ref[...] = (_sigmoid(zv[:, :D_MODEL]) * ps_ref[...] + _sigmoid(zv[:, D_MODEL:]) * pg_ref[...]).astype(BF16)
    return _rows(body, [(zg, 'r'), (ps, 'r'), (pg, 'r')], [(D_MODEL, BF16, 'r')], n=zg.shape[0], name=name)[0]


def _merge_bwd(dm, zg, ps, pg, name):
    def body(dm_ref, z_ref, ps_ref, pg_ref, dps_ref, dpg_ref, dz_ref):
        dmv, zv = dm_ref[...], z_ref[...]
        s1, s2 = _sigmoid(zv[:, :D_MODEL]), _sigmoid(zv[:, D_MODEL:])
        dps_ref[...] = (dmv * s1).astype(BF16)
        dpg_ref[...] = (dmv * s2).astype(BF16)
        dz_ref[:, :D_MODEL] = (dmv * ps_ref[...] * s1 * (1.0 - s1)).astype(BF16)
        dz_ref[:, D_MODEL:] = (dmv * pg_ref[...] * s2 * (1.0 - s2)).astype(BF16)
    return _rows(body, [(dm, 'r'), (zg, 'r'), (ps, 'r'), (pg, 'r')],
                 [(D_MODEL, BF16, 'r'), (D_MODEL, BF16, 'r'), (2 * D_MODEL, BF16, 'r')], n=zg.shape[0], name=name)


def _final_loss(h, g, tgt, name):
    def body(h_ref, g_ref, t_ref, loss_ref, dh_ref, dg_ref):
        hv = h_ref[...]
        rstd = lax.rsqrt(jnp.mean(hv * hv, axis=-1, keepdims=True) + EPS)
        xh = hv * rstd
        err = xh * g_ref[...] - t_ref[...]
        part = 0.5 * jnp.sum(jnp.mean(err * err, axis=-1, keepdims=True), axis=0, keepdims=True)
        loss_ref[...] += jnp.broadcast_to(part, loss_ref.shape)
        dout = err * (1.0 / hv.shape[1])
        dg_ref[...] += jnp.sum(dout * xh, axis=0, keepdims=True)
        dxh = dout * g_ref[...]
        dh_ref[...] = rstd * (dxh - xh * jnp.mean(dxh * xh, axis=-1, keepdims=True))
    d = h.shape[1]
    return _rows(body, [(h, 'r'), (g, 'f'), (tgt, 'r')], [(LANE, F32, 'a'), (d, F32, 'r'), (d, F32, 'a')],
                 n=h.shape[0], name=name)


def _adamw_math(wv, gv, mv, vv):
    nm = ADAM_B1 * mv + (1.0 - ADAM_B1) * gv
    nv = ADAM_B2 * vv + (1.0 - ADAM_B2) * (gv * gv)
    m_hat = nm / (1.0 - ADAM_B1 ** ADAM_STEP)
    v_hat = nv / (1.0 - ADAM_B2 ** ADAM_STEP)
    return -ADAM_LR * (m_hat / (jnp.sqrt(v_hat) + ADAM_EPS) + ADAM_WD * wv), nm, nv


def _adamw(w, g, m, v, name):
    def body(w_ref, g_ref, m_ref, v_ref, d_ref, nm_ref, nv_ref):
        d_ref[...], nm_ref[...], nv_ref[...] = _adamw_math(w_ref[...], g_ref[...], m_ref[...], v_ref[...])
    c = w.shape[1]
    return _rows(body, [(w, 'r'), (g, 'r'), (m, 'r'), (v, 'r')], [(c, F32, 'r')] * 3, n=w.shape[0], name=name)


ADAMW_BLOCKS = 8


def _adamw_group(items, c_arr, name):
    per = ADAMW_BLOCKS // 2
    n = len(items)

    def body(c_ref, *refs):
        mine = (pl.program_id(0) // per) == c_ref[0]
        for k in range(n):
            w_ref, go_ref, gs_ref, m_ref, v_ref = refs[5 * k:5 * k + 5]
            g_ref, d_ref, nm_ref, nv_ref = refs[5 * n + 4 * k:5 * n + 4 * k + 4]
            gv = jnp.where(mine, go_ref[...], gs_ref[...])
            g_ref[...] = gv
            d_ref[...], nm_ref[...], nv_ref[...] = _adamw_math(w_ref[...], gv, m_ref[...], v_ref[...])

    in_specs, out_specs, out_shape, args = [pl.BlockSpec(memory_space=pltpu.SMEM)], [], [], [c_arr]
    for item in items:
        r, cols = item[0].shape
        assert r % (8 * ADAMW_BLOCKS) == 0, item[0].shape
        tr = r // ADAMW_BLOCKS
        full = pl.BlockSpec((tr, cols), lambda i: (i, 0))
        half = pl.BlockSpec((tr, cols), lambda i: (i % per, 0))
        in_specs += [full, half, half, full, full]
        out_specs += [full] * 4
        out_shape += [jax.ShapeDtypeStruct((r, cols), F32)] * 4
        args += list(item)
    return _pcall(body, name=name, grid=(ADAMW_BLOCKS,), in_specs=in_specs, out_specs=out_specs, out_shape=out_shape,
                  compiler_params=_params())(*args)


def _shift_rows(v, sh, down):
    rolled = pltpu.roll(v, sh if down else v.shape[0] - sh, axis=0)
    row = lax.broadcasted_iota(jnp.int32, v.shape, 0)
    keep = (row >= sh) if down else (row < v.shape[0] - sh)
    return jnp.where(keep, rolled, 0.0)


def _chain_segments(st_r, st_i, pw_r_ref, pw_i_ref, conj, down):
    vr, vi = st_r[...], st_i[...]
    sh, k = 1, 0
    while sh < SEG:
        pr, pi = pw_r_ref[k:k + 1, :], pw_i_ref[k:k + 1, :]
        if conj:
            pi = -pi
        sr, si = _shift_rows(vr, sh, down), _shift_rows(vi, sh, down)
        vr, vi = vr + pr * sr - pi * si, vi + pr * si + pi * sr
        sh, k = sh * 2, k + 1
    st_r[...] = _shift_rows(vr, 1, down)
    st_i[...] = _shift_rows(vi, 1, down)


def _expand_block(u_ref, t_ref, bu_ref):
    for j in range(BD_TILES):
        k = j % 4
        bu_ref[:, j * BD_ST:(j + 1) * BD_ST] = _dot(u_ref[:, k * BD_CH:(k + 1) * BD_CH], t_ref[j])


def _s5_scan(u, tiles, ar8, ai8, pw_r, pw_i, name):
    n = u.shape[0]
    rb = SCAN_ROWS
    nb, steps, lc = n // rb, rb // SEG, 512

    def body(u_ref, t_ref, ar_ref, ai_ref, pwr_ref, pwi_ref, x_ref, st_r, st_i, bu_ref):
        ph, b = pl.program_id(0), pl.program_id(1)

        @pl.when((ph == 0) & (b == 0))
        def _():
            st_r[...] = jnp.zeros_like(st_r)
            st_i[...] = jnp.zeros_like(st_i)

        _expand_block(u_ref, t_ref, bu_ref)

        def scan(store):
            for c in range(S5_GP // lc):
                re, im = slice(c * lc, (c + 1) * lc), slice(S5_GP + c * lc, S5_GP + (c + 1) * lc)
                a_r, a_i = ar_ref[:, re], ai_ref[:, re]

                def step(s, carry):
                    xr, xi = carry
                    rows = pl.ds(pl.multiple_of(s * SEG, SEG), SEG)
                    nr = a_r * xr - a_i * xi + bu_ref[rows, re]
                    ni = a_r * xi + a_i * xr + bu_ref[rows, im]
                    if store:
                        x_ref[rows, re] = nr
                        x_ref[rows, im] = ni
                    return nr, ni

                xr, xi = lax.fori_loop(0, steps, step, (st_r[:, re], st_i[:, re]), unroll=4)
                st_r[:, re] = xr
                st_i[:, re] = xi

        @pl.when(ph == 0)
        def _():
            scan(False)

        @pl.when((ph == 0) & (b == nb - 1))
        def _():
            _chain_segments(st_r, st_i, pwr_ref, pwi_ref, conj=False, down=True)

        @pl.when(ph == 1)
        def _():
            scan(True)

    full = lambda a: pl.BlockSpec(a.shape, lambda ph, b: (0, 0))
    return _pcall(body, name=name, grid=(2, nb),
                  in_specs=[pl.BlockSpec((rb, S5_W), lambda ph, b: (b, 0)), pl.BlockSpec(tiles.shape, lambda ph, b: (0, 0, 0)),
                            full(ar8), full(ai8), full(pw_r), full(pw_i)],
                  out_specs=pl.BlockSpec((rb, 2 * S5_GP), lambda ph, b: (b * ph, 0)),
                  out_shape=jax.ShapeDtypeStruct((n, 2 * S5_GP), F32),
                  scratch_shapes=[pltpu.VMEM((SEG, S5_GP), F32), pltpu.VMEM((SEG, S5_GP), F32),
                                  pltpu.VMEM((rb, 2 * S5_GP), F32)],
                  compiler_params=_params())(u, tiles, ar8, ai8, pw_r, pw_i)


def _s5_scan_bwd(dy, tiles, xs, ar8, ai8, pw_r, pw_i, name):
    n = dy.shape[0]
    rb = SCAN_ROWS
    nb, steps, lc = n // rb, rb // SEG, 256

    def body(dy_ref, t_ref, x_ref, ar_ref, ai_ref, pwr_ref, pwi_ref, lam_ref, da_ref, st_r, st_i, gx_ref):
        ph, b = pl.program_id(0), pl.program_id(1)

        @pl.when((ph == 0) & (b == 0))
        def _():
            st_r[...] = jnp.zeros_like(st_r)
            st_i[...] = jnp.zeros_like(st_i)
            da_ref[...] = jnp.zeros_like(da_ref)

        _expand_block(dy_ref, t_ref, gx_ref)

        def scan(store):
            for c in range(S5_GP // lc):
                re, im = slice(c * lc, (c + 1) * lc), slice(S5_GP + c * lc, S5_GP + (c + 1) * lc)
                a_r, a_i = ar_ref[:, re], ai_ref[:, re]

                def step(s, carry):
                    rows = pl.ds(pl.multiple_of((steps - 1 - s) * SEG, SEG), SEG)
                    if store:
                        lr, li, dr, di = carry
                        xr, xi = x_ref[rows, re], x_ref[rows, im]
                        dr = dr + lr * xr + li * xi
                        di = di + li * xr - lr * xi
                    else:
                        lr, li = carry
                    nr = a_r * lr + a_i * li + gx_ref[rows, re]
                    ni = a_r * li - a_i * lr + gx_ref[rows, im]
                    if store:
                        lam_ref[rows, re] = nr
                        lam_ref[rows, im] = ni
                        return nr, ni, dr, di
                    return nr, ni

                if store:
                    lr, li, dr, di = lax.fori_loop(0, steps, step, (st_r[:, re], st_i[:, re], da_ref[:, re], da_ref[:, im]),
                                                   unroll=4)
                    da_ref[:, re] = dr
                    da_ref[:, im] = di
                else:
                    lr, li = lax.fori_loop(0, steps, step, (st_r[:, re], st_i[:, re]), unroll=4)
                st_r[:, re] = lr
                st_i[:, re] = li

        @pl.when(ph == 0)
        def _():
            scan(False)

        @pl.when((ph == 0) & (b == nb - 1))
        def _():
            _chain_segments(st_r, st_i, pwr_ref, pwi_ref, conj=True, down=False)

        @pl.when(ph == 1)
        def _():
            scan(True)

    full = lambda a: pl.BlockSpec(a.shape, lambda ph, b: (0, 0))
    rev = lambda ph, b: (nb - 1 - b, 0)
    return _pcall(body, name=name, grid=(2, nb),
                  in_specs=[pl.BlockSpec((rb, S5_W), rev), pl.BlockSpec(tiles.shape, lambda ph, b: (0, 0, 0)),
                            pl.BlockSpec((rb, 2 * S5_GP), lambda ph, b: ((nb - 1 - b) * ph, 0)),
                            full(ar8), full(ai8), full(pw_r), full(pw_i)],
                  out_specs=[pl.BlockSpec((rb, 2 * S5_GP), lambda ph, b: (nb - 1 - b * ph, 0)),
                             pl.BlockSpec((SEG, 2 * S5_GP), lambda ph, b: (0, 0))],
                  out_shape=[jax.ShapeDtypeStruct((n, 2 * S5_GP), F32), jax.ShapeDtypeStruct((SEG, 2 * S5_GP), F32)],
                  scratch_shapes=[pltpu.VMEM((SEG, S5_GP), F32), pltpu.VMEM((SEG, S5_GP), F32),
                                  pltpu.VMEM((rb, 2 * S5_GP), F32)],
                  compiler_params=_params())(dy, tiles, xs, ar8, ai8, pw_r, pw_i)


def _s5_discretize(lam_re, lam_im, log_dt, b_re, b_im):
    dt = jnp.exp(log_dt)[:, None]
    mag = jnp.exp(lam_re * dt)
    ar = mag * jnp.cos(lam_im * dt)
    ai = mag * jnp.sin(lam_im * dt)
    den = lam_re * lam_re + lam_im * lam_im
    nr = ar - 1.0
    fr = (nr * lam_re + ai * lam_im) / den
    fi = (ai * lam_re - nr * lam_im) / den
    bbar_re = fr[:, :, None] * b_re - fi[:, :, None] * b_im
    bbar_im = fr[:, :, None] * b_im + fi[:, :, None] * b_re
    return ar, ai, bbar_re, bbar_im


BD_TILES, BD_CH, BD_ST, BD_GROUPS = 8, 128, 512, 8
BD_ROWS = 4096


def _bd_tiles(re, im):
    eye = jnp.eye(BD_GROUPS, dtype=re.dtype)

    def tiles(t):
        t = t.reshape(S5_G // BD_GROUPS, BD_GROUPS, S5_H, S5_P)
        return (t[:, :, :, None, :] * eye[None, :, None, :, None]).reshape(S5_G // BD_GROUPS, BD_CH, BD_ST)

    return jnp.concatenate([tiles(re), tiles(im)], axis=0)


def _bd_blocks(t):
    t = t.reshape(2, S5_G // BD_GROUPS, BD_GROUPS, S5_H, BD_GROUPS, S5_P)
    return jnp.einsum('rkahap->rkahp', t).reshape(2, S5_G, S5_H, S5_P)


def _bd_reduce(x, t, res, name):
    n = x.shape[0]
    tm = _pick(n, BD_ROWS, 16)

    def body(x_ref, t_ref, r_ref, o_ref):
        part = _dot(x_ref[...], t_ref[...], NT)

        @pl.when(pl.program_id(2) == 0)
        def _():
            o_ref[...] = r_ref[...] + part

        @pl.when(pl.program_id(2) == 1)
        def _():
            o_ref[...] += part

    return _pcall(body, name=name, grid=(n // tm, 4, 2),
                  in_specs=[pl.BlockSpec((tm, BD_ST), lambda i, k, r: (i, k + 4 * r)),
                            pl.BlockSpec((None, BD_CH, BD_ST), lambda i, k, r: (k + 4 * r, 0, 0)),
                            pl.BlockSpec((tm, BD_CH), lambda i, k, r: (i, k))],
                  out_specs=pl.BlockSpec((tm, BD_CH), lambda i, k, r: (i, k)),
                  out_shape=jax.ShapeDtypeStruct((n, S5_W), F32), compiler_params=_params())(x, t, res)


def _bd_outer(a, x, name):
    n = a.shape[0]
    tk = _pick(n, BD_ROWS, 16)
    nk = n // tk

    def body(a_ref, x_ref, o_ref):
        part = _dot(a_ref[...], x_ref[...], TN)

        @pl.when(pl.program_id(1) == 0)
        def _():
            o_ref[...] = part

        @pl.when(pl.program_id(1) > 0)
        def _():
            o_ref[...] += part

    return _pcall(body, name=name, grid=(BD_TILES, nk),
                  in_specs=[pl.BlockSpec((tk, BD_CH), lambda j, kk: (kk, j % 4)), pl.BlockSpec((tk, BD_ST), lambda j, kk: (kk, j))],
                  out_specs=pl.BlockSpec((None, BD_CH, BD_ST), lambda j, kk: (j, 0, 0)),
                  out_shape=jax.ShapeDtypeStruct((BD_TILES, BD_CH, BD_ST), F32), compiler_params=_params())(a, x)


def _permute_rows(t):
    n = t.shape[0]
    return t.reshape(SEG, n // SEG, t.shape[1]).transpose(1, 0, 2).reshape(n, t.shape[1])


def _unpermute_rows(t):
    n = t.shape[0]
    return t.reshape(n // SEG, SEG, t.shape[1]).transpose(1, 0, 2).reshape(n, t.shape[1])


def _segment_powers(ar, ai, seg_steps):
    pr, pi = ar.reshape(1, S5_GP), ai.reshape(1, S5_GP)
    e = 1
    while e < seg_steps:
        pr, pi = pr * pr - pi * pi, 2.0 * pr * pi
        e *= 2
    assert e == seg_steps, "segment length must be a power of two"
    rows_r, rows_i = [], []
    for _ in range(3):
        rows_r.append(pr)
        rows_i.append(pi)
        pr, pi = pr * pr - pi * pi, 2.0 * pr * pi
    pad = jnp.zeros((SEG - 3, S5_GP), F32)
    return jnp.concatenate(rows_r + [pad], axis=0), jnp.concatenate(rows_i + [pad], axis=0)


NT = (((1,), (1,)), ((), ()))
TN = (((0,), (0,)), ((), ()))


def _dot(a, b, dims=None, exact=False):
    dims = (((1,), (0,)), ((), ())) if dims is None else dims
    if exact:
        return lax.dot_general(a, b, dims, precision=HI, preferred_element_type=F32)
    return lax.dot_general(a.astype(BF16), b.astype(BF16), dims, preferred_element_type=F32)


def _dot01(a, b, dims=None, ones_first=True):
    x = b if ones_first else a
    hi = x.astype(BF16)
    lo = (x - hi.astype(F32)).astype(BF16)
    parts = [(_dot(a, p, dims) if ones_first else _dot(p, b, dims)) for p in (lo, hi)]
    return parts[0] + parts[1]


HEADS = range(4)


def _gla_chunk_fwd(qc, kc, vc, al, wup, bup, s_prev, tril):
    ones = jnp.ones((GLA_CHUNK, GLA_DV), F32)
    z = [_dot(al, wup[h]) + bup[h] for h in HEADS]
    la = [(jnp.minimum(z[h], 0.0) - jnp.log(1.0 + jnp.exp(-jnp.abs(z[h])))) * (1.0 / GLA_TAU) for h in HEADS]
    bc = [_dot01(tril, la[h]) for h in HEADS]
    blb = [_dot01(la[h], ones, TN, ones_first=False) for h in HEADS]
    bl = [bc[h][GLA_CHUNK - 1:GLA_CHUNK, :] for h in HEADS]
    ebc = [jnp.exp(bc[h]) for h in HEADS]
    qt = [qc[h] * (GLA_DK ** -0.5) * ebc[h] for h in HEADS]
    kt = [kc[h] * jnp.exp(-bc[h]) for h in HEADS]
    ke = [kc[h] * jnp.exp(bl[h] - bc[h]) for h in HEADS]
    sc = [_dot(qt[h], kt[h], NT) * tril for h in HEADS]
    oi = [_dot(sc[h], vc[h]) for h in HEADS]
    oo = [_dot(qt[h], s_prev[h]) for h in HEADS]
    o = [oi[h] + oo[h] for h in HEADS]
    return z, bc, bl, blb, ebc, qt, kt, ke, sc, o


GLA_ROWS = 512
GLA_CPB = GLA_ROWS // GLA_CHUNK


ZA_COLS = 5 * 512
SLOT = 128


def _pad_heads(w):
    r = w.shape[0]
    return jnp.pad(w.reshape(r, GLA_HEADS, GLA_DK), ((0, 0), (0, 0), (0, SLOT - GLA_DK))).reshape(r, GLA_HEADS * SLOT)


def _unpad_heads(w):
    r = w.shape[0]
    return w.reshape(r, GLA_HEADS, SLOT)[:, :, :GLA_DK].reshape(r, GLA_HEADS * GLA_DK)


def _gla_token_specs(blk):
    col = lambda cb: pl.BlockSpec((GLA_ROWS, 512), lambda j: (blk(j), cb))
    whole = lambda a: pl.BlockSpec(a.shape, lambda j: (0,) * a.ndim)
    return col, whole


def _head_ds(h, width):
    return pl.ds(h * SLOT, width)


def _tri(lower):
    ri = lax.broadcasted_iota(jnp.int32, (GLA_CHUNK, GLA_CHUNK), 0)
    ci = lax.broadcasted_iota(jnp.int32, (GLA_CHUNK, GLA_CHUNK), 1)
    return ((ri >= ci) if lower else (ri <= ci)).astype(F32)


def _gla_fwd(za, al, wup, bup, gn, name):
    n = za.shape[0]
    nc = n // GLA_CHUNK

    def body(q_ref, k_ref, v_ref, r_ref, al_ref, wup_ref, bup_ref, gn_ref, y_ref, sp_ref, s_ref):
        @pl.when(pl.program_id(0) == 0)
        def _():
            s_ref[...] = jnp.zeros_like(s_ref)

        tril = _tri(True)

        def chunk(c, carry):
            rows = pl.ds(pl.multiple_of(c * GLA_CHUNK, GLA_CHUNK), GLA_CHUNK)
            alc = al_ref[rows, :]
            vc = [v_ref[rows, _head_ds(h, GLA_DV)] for h in HEADS]
            s_prev = [s_ref[h] for h in HEADS]
            _, _, _, blb, _, _, _, ke, _, o = _gla_chunk_fwd(
                [q_ref[rows, _head_ds(h, GLA_DK)] for h in HEADS], [k_ref[rows, _head_ds(h, GLA_DK)] for h in HEADS],
                vc, alc, [wup_ref[h] for h in HEADS], [bup_ref[h] for h in HEADS], s_prev, tril)
            ds = [_dot(ke[h], vc[h], TN) for h in HEADS]
            for h in HEADS:
                rc = r_ref[rows, _head_ds(h, GLA_DV)]
                sp_ref[h, c] = s_prev[h]
                rstd = lax.rsqrt(jnp.mean(o[h] * o[h], axis=-1, keepdims=True) + EPS)
                y_ref[rows, _head_ds(h, GLA_DV)] = (o[h] * rstd * gn_ref[h] * (rc * _sigmoid(rc))).astype(BF16)
                s_ref[h] = jnp.exp(blb[h]) * s_prev[h] + ds[h]
            return carry

        lax.fori_loop(0, GLA_CPB, chunk, 0)

    col, whole = _gla_token_specs(lambda j: j)
    return _pcall(body, name=name, grid=(n // GLA_ROWS,),
                  in_specs=[col(1), col(2), col(3), col(4), pl.BlockSpec((GLA_ROWS, LANE), lambda j: (j, 0)),
                            whole(wup), whole(bup), whole(gn)],
                  out_specs=[pl.BlockSpec((GLA_ROWS, GLA_HEADS * GLA_DV), lambda j: (j, 0)),
                             pl.BlockSpec((GLA_HEADS, GLA_CPB, GLA_DK, GLA_DV), lambda j: (0, j, 0, 0))],
                  out_shape=[jax.ShapeDtypeStruct((n, GLA_HEADS * GLA_DV), BF16),
                             jax.ShapeDtypeStruct((GLA_HEADS, nc, GLA_DK, GLA_DV), F32)],
                  scratch_shapes=[pltpu.VMEM((GLA_HEADS, GLA_DK, GLA_DV), F32)],
                  compiler_params=_params())(za, za, za, za, al, wup, bup, gn)


def _gla_bwd(za, al, wup, bup, gn, sp, dy, du_s5, name):
    n = za.shape[0]
    nb = n // GLA_ROWS

    def body(q_ref, k_ref, v_ref, r_ref, al_ref, wup_ref, bup_ref, gn_ref, dy_ref, dus_ref, sp_ref,
             dza_ref, dz_ref, dgn_ref, dbup_ref, ds_ref):
        @pl.when(pl.program_id(0) == 0)
        def _():
            ds_ref[...] = jnp.zeros_like(ds_ref)
            dgn_ref[...] = jnp.zeros_like(dgn_ref)
            dbup_ref[...] = jnp.zeros_like(dbup_ref)

        tril, triu = _tri(True), _tri(False)
        dza_ref[:, 0:512] = dus_ref[...].astype(BF16)
        dza_ref[:, 512:1536] = jnp.zeros((GLA_ROWS, 1024), BF16)
        dz_ref[...] = jnp.zeros_like(dz_ref)

        def chunk(i, carry):
            c = GLA_CPB - 1 - i
            rows = pl.ds(pl.multiple_of(c * GLA_CHUNK, GLA_CHUNK), GLA_CHUNK)
            alc = al_ref[rows, :]
            qc = [q_ref[rows, _head_ds(h, GLA_DK)] for h in HEADS]
            kc = [k_ref[rows, _head_ds(h, GLA_DK)] for h in HEADS]
            vc = [v_ref[rows, _head_ds(h, GLA_DV)] for h in HEADS]
            s_prev = [sp_ref[h, c] for h in HEADS]
            ds = [ds_ref[h] for h in HEADS]
            z, bc, bl, blb, ebc, qt, kt, ke, sc, o = _gla_chunk_fwd(
                qc, kc, vc, alc, [wup_ref[h] for h in HEADS], [bup_ref[h] for h in HEADS], s_prev, tril)
            do = []
            for h in HEADS:
                rc = r_ref[rows, _head_ds(h, GLA_DV)]
                rs = lax.rsqrt(jnp.mean(o[h] * o[h], axis=-1, keepdims=True) + EPS)
                on = o[h] * rs
                sr = _sigmoid(rc)
                sil = rc * sr
                dyv, gnv = dy_ref[rows, _head_ds(h, GLA_DV)], gn_ref[h]
                dgn_ref[h] += jnp.sum(dyv * on * sil, axis=0, keepdims=True)
                dza_ref[rows, pl.ds(2048 + h * SLOT, GLA_DV)] = (dyv * on * gnv * (sr * (1.0 + rc * (1.0 - sr)))).astype(BF16)
                don = dyv * gnv * sil
                do.append(rs * (don - on * jnp.mean(don * on, axis=-1, keepdims=True)))
            dp = [_dot(do[h], vc[h], NT) * tril for h in HEADS]
            dv1 = [_dot(sc[h], do[h], TN) for h in HEADS]
            dv2 = [_dot(ke[h], ds[h]) for h in HEADS]
            dq2 = [_dot(do[h], s_prev[h], NT) for h in HEADS]
            dke = [_dot(vc[h], ds[h], NT) for h in HEADS]
            ddec = [_dot01(jnp.ones((8, GLA_DV), F32), ds[h] * s_prev[h], NT)[0:1, :] for h in HEADS]
            dsn = [_dot(qt[h], do[h], TN) for h in HEADS]
            dq1 = [_dot(dp[h], kt[h]) for h in HEADS]
            dkt = [_dot(dp[h], qt[h], TN) for h in HEADS]
            dbc, dbl = [], []
            for h in HEADS:
                dqt = dq1[h] + dq2[h]
                dza_ref[rows, pl.ds(1536 + h * SLOT, GLA_DV)] = (dv1[h] + dv2[h]).astype(BF16)
                ds_ref[h] = jnp.exp(blb[h]) * ds[h] + dsn[h]
                dza_ref[rows, pl.ds(512 + h * SLOT, GLA_DK)] = (dqt * (GLA_DK ** -0.5) * ebc[h]).astype(BF16)
                dza_ref[rows, pl.ds(1024 + h * SLOT, GLA_DK)] = (dkt[h] * jnp.exp(-bc[h])
                                                                 + dke[h] * jnp.exp(bl[h] - bc[h])).astype(BF16)
                dbc.append(dqt * qt[h] - dkt[h] * kt[h] - dke[h] * ke[h])
                dbl.append(jnp.sum(dke[h] * ke[h], axis=0, keepdims=True) + ddec[h] * jnp.exp(bl[h]))
            dla = [_dot01(triu, dbc[h]) + dbl[h] for h in HEADS]
            for h in HEADS:
                dz = dla[h] * (1.0 - _sigmoid(z[h])) * (1.0 / GLA_TAU)
                dz_ref[rows, _head_ds(h, GLA_DK)] = dz
                dbup_ref[h] += jnp.sum(dz, axis=0, keepdims=True)
            return carry

        lax.fori_loop(0, GLA_CPB, chunk, 0)

    rev = lambda j: nb - 1 - j
    col, whole = _gla_token_specs(rev)
    tok = lambda w: pl.BlockSpec((GLA_ROWS, w), lambda j: (rev(j), 0))
    h1 = lambda w: pl.BlockSpec((GLA_HEADS, 1, w), lambda j: (0, 0, 0))
    s1 = lambda w: jax.ShapeDtypeStruct((GLA_HEADS, 1, w), F32)
    return _pcall(body, name=name, grid=(nb,),
                  in_specs=[col(1), col(2), col(3), col(4), tok(LANE), whole(wup), whole(bup), whole(gn), tok(512), tok(512),
                            pl.BlockSpec((GLA_HEADS, GLA_CPB, GLA_DK, GLA_DV), lambda j: (0, rev(j), 0, 0))],
                  out_specs=[tok(ZA_COLS), tok(GLA_HEADS * SLOT), h1(GLA_DV), h1(GLA_DK)],
                  out_shape=[jax.ShapeDtypeStruct((n, ZA_COLS), BF16), jax.ShapeDtypeStruct((n, GLA_HEADS * SLOT), F32),
                             s1(GLA_DV), s1(GLA_DK)],
                  scratch_shapes=[pltpu.VMEM((GLA_HEADS, GLA_DK, GLA_DV), F32)],
                  compiler_params=_params())(za, za, za, za, al, wup, bup, gn, dy, du_s5, sp)


ANY = pl.BlockSpec(memory_space=pl.ANY)


def _place():
    x, y, c = lax.axis_index("x"), lax.axis_index("y"), lax.axis_index("c")
    chips = [(1 - x, y), (x, 1 - y), (1 - x, 1 - y)]
    return x, y, c, chips


def _remote(src, dst, ssem, rsem, dev):
    return pltpu.make_async_remote_copy(src_ref=src, dst_ref=dst, send_sem=ssem, recv_sem=rsem, device_id=dev,
                                        device_id_type=MESH_ID)


def _half(c, rows):
    h = rows // 2
    return pl.ds(pl.multiple_of(c * h, 8), h)


def _side_gather_ici(shards):
    def copies(ins, outs, ssem, rsem):
        x, y, c, chips = _place()
        mine = 2 * x + y
        cps = []
        for w in range(len(ins)):
            half = _half(c, ins[w].shape[0])
            cps.append(_remote(ins[w], outs[w].at[mine], ssem.at[4 * w], rsem.at[4 * w], (x, y, 1 - c)))
            for k, (px, py) in enumerate(chips):
                cps.append(_remote(ins[w].at[half], outs[w].at[mine, half], ssem.at[4 * w + 1 + k], rsem.at[4 * w + 1 + k],
                                   (px, py, c)))
        return cps

    return _Side(shards, [jax.ShapeDtypeStruct((4,) + s.shape, s.dtype) for s in shards], 4 * len(shards), copies)


def _side_gather_d2d(gathered):
    def copies(ins, outs, ssem, rsem):
        x, y, c, chips = _place()
        cps = []
        for w in range(len(outs)):
            half = _half(c, outs[w].shape[1])
            for k, (px, py) in enumerate(chips):
                theirs = outs[w].at[2 * px + py, half]
                cps.append(_remote(theirs, theirs, ssem.at[3 * w + k], rsem.at[3 * w + k], (x, y, 1 - c)))
        return cps

    return _Side(gathered, [jax.ShapeDtypeStruct(g.shape, g.dtype) for g in gathered], 3 * len(gathered), copies,
                 aliased=True)


def _side_swap_halves(grads):
    def copies(ins, outs, ssem, rsem):
        x, y, c, _ = _place()
        return [_remote(ins[w].at[:, _half(1 - c, ins[w].shape[1]), :], outs[w], ssem.at[w], rsem.at[w], (x, y, 1 - c))
                for w in range(len(ins))]

    return _Side(grads, [jax.ShapeDtypeStruct((4, g.shape[1] // 2, g.shape[2]), g.dtype) for g in grads], len(grads), copies)


def _side_scatter(sums):
    def copies(ins, outs, ssem, rsem):
        x, y, c, chips = _place()
        return [_remote(ins[w].at[2 * px + py], outs[w].at[k], ssem.at[3 * w + k], rsem.at[3 * w + k], (px, py, c))
                for w in range(len(ins)) for k, (px, py) in enumerate(chips)]

    return _Side(sums, [jax.ShapeDtypeStruct((3,) + s.shape[1:], s.dtype) for s in sums], 3 * len(sums), copies)


def _side_swap_reduced(halves):
    def copies(ins, outs, ssem, rsem):
        x, y, c, _ = _place()
        return [_remote(ins[w], outs[w], ssem.at[w], rsem.at[w], (x, y, 1 - c)) for w in range(len(ins))]

    return _Side(halves, [jax.ShapeDtypeStruct(h.shape, h.dtype) for h in halves], len(halves), copies)


SUM_BLOCKS = 2


def _chip_sums(gs, recvs, c_arr, name):
    n = len(gs)

    def body(c_ref, *refs):
        for k in range(n):
            refs[2 * n + k][...] = (refs[2 * k][...] + refs[2 * k + 1][...]).astype(BF16)

    in_specs, out_specs, out_shape, args = [], [], [], []
    for g, recv in zip(gs, recvs):
        _, r, cols = g.shape
        h = r // 2
        assert h % (16 * SUM_BLOCKS) == 0, g.shape
        tr = h // SUM_BLOCKS
        in_specs += [pl.BlockSpec((None, None, tr, cols), lambda s, i, c_ref: (s, c_ref[0], i, 0)),
                     pl.BlockSpec((None, tr, cols), lambda s, i, c_ref: (s, i, 0))]
        out_specs.append(pl.BlockSpec((None, tr, cols), lambda s, i, c_ref: (s, i, 0)))
        out_shape.append(jax.ShapeDtypeStruct((4, h, cols), BF16))
        args += [g.reshape(4, 2, h, cols), recv]
    grid_spec = pltpu.PrefetchScalarGridSpec(num_scalar_prefetch=1, grid=(4, SUM_BLOCKS), in_specs=in_specs,
                                             out_specs=out_specs)
    return _pcall(body, name=name, grid_spec=grid_spec, out_shape=out_shape, compiler_params=_params())(c_arr, *args)


def _owner_sums(sums, others, s_arr, name):
    n = len(sums)

    def body(s_ref, *refs):
        f = lambda v: v.astype(F32)
        for k in range(n):
            a_ref, o_ref = refs[2 * k], refs[2 * k + 1]
            refs[2 * n + k][...] = (f(a_ref[...]) + f(o_ref[0])) + (f(o_ref[1]) + f(o_ref[2]))

    in_specs, out_specs, out_shape, args = [], [], [], []
    for sm, ot in zip(sums, others):
        _, h, cols = sm.shape
        tr = h // SUM_BLOCKS
        in_specs += [pl.BlockSpec((None, tr, cols), lambda i, s_ref: (s_ref[0], i, 0)),
                     pl.BlockSpec((3, tr, cols), lambda i, s_ref: (0, i, 0))]
        out_specs.append(pl.BlockSpec((tr, cols), lambda i, s_ref: (i, 0)))
        out_shape.append(jax.ShapeDtypeStruct((h, cols), F32))
        args += [sm, ot]
    grid_spec = pltpu.PrefetchScalarGridSpec(num_scalar_prefetch=1, grid=(SUM_BLOCKS,), in_specs=in_specs,
                                             out_specs=out_specs)
    return _pcall(body, name=name, grid_spec=grid_spec, out_shape=out_shape, compiler_params=_params())(s_arr, *args)


def _side_small_sibling(v):
    def copies(ins, outs, ssem, rsem):
        x, y, c, _ = _place()
        return [_remote(ins[0], outs[0], ssem.at[0], rsem.at[0], (x, y, 1 - c))]

    return _Side([v], [jax.ShapeDtypeStruct(v.shape, F32)], 1, copies)


def _side_small_chips(v):
    def copies(ins, outs, ssem, rsem):
        x, y, c, chips = _place()
        return [_remote(ins[0], outs[0].at[k], ssem.at[k], rsem.at[k], (px, py, c)) for k, (px, py) in enumerate(chips)]

    return _Side([v], [jax.ShapeDtypeStruct((3,) + v.shape, F32)], 3, copies)


def _small_add(v, r, name):
    def body(v_ref, r_ref, o_ref):
        if r.ndim == 2:
            o_ref[...] = v_ref[...] + r_ref[...]
        else:
            o_ref[...] = (v_ref[...] + r_ref[0]) + (r_ref[1] + r_ref[2])

    vm = pl.BlockSpec(memory_space=pltpu.VMEM)
    return _pcall(body, name=name, in_specs=[vm, vm], out_specs=vm, out_shape=jax.ShapeDtypeStruct(v.shape, F32),
                  compiler_params=_params())(v, r)


def _merge_sides(sides):
    if len(sides) == 1:
        return sides[0]

    def copies(in_refs, out_refs, ssem, rsem):
        cps, i, o, q = [], 0, 0, 0
        for s in sides:
            ni, no = len(s.ins), len(s.out_shapes)
            cps += s.copies(in_refs[i:i + ni], out_refs[o:o + no], ssem.at[pl.ds(q, s.nsem)], rsem.at[pl.ds(q, s.nsem)])
            i, o, q = i + ni, o + no, q + s.nsem
        return cps

    assert not any(s.aliased for s in sides)
    return _Side(sum((s.ins for s in sides), []), sum((s.out_shapes for s in sides), []), sum(s.nsem for s in sides), copies)


def _tile_rows(size):
    return -(-size // (8 * LANE)) * 8


def _pack_small(parts):
    pieces = []
    for p in parts:
        flat = p.reshape(-1).astype(F32)
        pieces.append(jnp.pad(flat, (0, _tile_rows(p.size) * LANE - p.size)).reshape(-1, LANE))
    rows = sum(x.shape[0] for x in pieces)
    pieces.append(jnp.zeros(((-rows) % 64, LANE), F32))
    return jnp.concatenate(pieces, axis=0)


def _unpack_small(packed, like):
    out, pos = [], 0
    for p in like:
        rows = _tile_rows(p.size)
        out.append(packed[pos:pos + rows].reshape(-1)[:p.size].reshape(p.shape))
        pos += rows
    return out


FFN_FWD_ROWS, FFN_BWD_ROWS = 1024, 512
FFN_SUB_ROWS = 256


def _ffn_specs(n, d, fs, cap):
    rows = _pick(n, cap, 16)
    row = pl.BlockSpec((rows, d), lambda i, s: (i, 0))
    gain = pl.BlockSpec((1, d), lambda i, s: (0, 0))
    w_row = pl.BlockSpec((None, fs, d), lambda i, s: (s, 0, 0))
    hid = pl.BlockSpec((None, rows, fs), lambda i, s: (s, i, 0))
    return rows, row, gain, w_row, hid


def _ffn_fwd(h, g, w1t, w3t, w2, tag, plan):
    n, d = h.shape
    ns, fs, _ = w2.shape
    rows, row, gain, w_row, hid = _ffn_specs(n, d, fs, FFN_FWD_ROWS)
    sub = rows

    def body(h_ref, g_ref, w1_ref, w3_ref, w2_ref, out_ref, n1_ref, a_ref, b_ref, hm_ref, acc_ref):
        s = pl.program_id(1)

        @pl.when(s == 0)
        def _():
            xv = h_ref[...]
            rstd = lax.rsqrt(jnp.mean(xv * xv, axis=-1, keepdims=True) + EPS)
            n1_ref[...] = (xv * rstd * g_ref[...]).astype(BF16)
            acc_ref[...] = jnp.zeros_like(acc_ref)

        def up(j):
            n1 = n1_ref[j * sub:(j + 1) * sub, :]
            return _dot(n1, w1_ref[...], NT), _dot(n1, w3_ref[...], NT)

        cur = up(0)
        for j in range(rows // sub):
            nxt = up(j + 1) if (j + 1) * sub < rows else None
            a, b = cur
            r = slice(j * sub, (j + 1) * sub)
            hm = (a * _sigmoid(a) * b).astype(BF16)
            a_ref[r, :] = a.astype(BF16)
            b_ref[r, :] = b.astype(BF16)
            hm_ref[r, :] = hm
            acc_ref[r, :] += _dot(hm, w2_ref[...])
            cur = nxt

        @pl.when(s == ns - 1)
        def _():
            out_ref[...] = h_ref[...] + 0.5 * acc_ref[...]

    hid_shape = jax.ShapeDtypeStruct((ns, n, fs), BF16)
    plan.before(f"{tag}_fwd")
    out, n1, a, b, hm = _pcall(
        body, name=f"{tag}_fwd", grid=(n // rows, ns), in_specs=[row, gain, w_row, w_row, w_row],
        out_specs=[row, row, hid, hid, hid],
        out_shape=[jax.ShapeDtypeStruct((n, d), F32), jax.ShapeDtypeStruct((n, d), BF16), hid_shape, hid_shape, hid_shape],
        scratch_shapes=[pltpu.VMEM((rows, d), F32)], compiler_params=_params())(h, g, w1t, w3t, w2)
    plan.after(f"{tag}_fwd")
    return out, (h, n1, a, b, hm)


def _wgrad(a3, b, name, alpha=1.0):
    ns, n, fs = a3.shape
    d = b.shape[1]
    tk = _pick(n, 1024, 16)

    def body(a_ref, b_ref, o_ref):
        @pl.when(pl.program_id(0) == 0)
        def _():
            o_ref[...] = jnp.zeros_like(o_ref)

        bv = b_ref[...].astype(BF16)
        for s in range(ns):
            part = _dot(a_ref[s], bv, TN)
            o_ref[s] += part if alpha == 1.0 else alpha * part

    return _pcall(body, name=name, grid=(n // tk,),
                  in_specs=[pl.BlockSpec((ns, tk, fs), lambda k: (0, k, 0)), pl.BlockSpec((tk, d), lambda k: (k, 0))],
                  out_specs=pl.BlockSpec((ns, fs, d), lambda k: (0, 0, 0)),
                  out_shape=jax.ShapeDtypeStruct((ns, fs, d), F32), compiler_params=_params())(a3, b)


def _ffn_bwd(dout, saved, g, w1, w3, w2, tag, plan):
    h, n1, a, b, hm = saved
    n, d = h.shape
    ns, fs, _ = w2.shape
    rows, row, gain, w_row, hid = _ffn_specs(n, d, fs, FFN_BWD_ROWS)
    sub = _pick(rows, FFN_SUB_ROWS, 16)

    def body(do_ref, h_ref, g_ref, a_ref, b_ref, w1_ref, w3_ref, w2_ref, dh_ref, da_ref, db_ref, dg_ref, acc_ref):
        i, s = pl.program_id(0), pl.program_id(1)

        @pl.when(s == 0)
        def _():
            acc_ref[...] = jnp.zeros_like(acc_ref)

        @pl.when((s == 0) & (i == 0))
        def _():
            dg_ref[...] = jnp.zeros_like(dg_ref)

        def up(j):
            return _dot(0.5 * do_ref[j * sub:(j + 1) * sub, :], w2_ref[...], NT)

        cur = up(0)
        for j in range(rows // sub):
            nxt = up(j + 1) if (j + 1) * sub < rows else None
            r = slice(j * sub, (j + 1) * sub)
            av, bv = a_ref[r, :].astype(F32), b_ref[r, :].astype(F32)
            sg = _sigmoid(av)
            da = (cur * bv * (sg * (1.0 + av * (1.0 - sg)))).astype(BF16)
            db = (cur * av * sg).astype(BF16)
            da_ref[r, :] = da
            db_ref[r, :] = db
            acc_ref[r, :] += _dot(da, w1_ref[...]) + _dot(db, w3_ref[...])
            cur = nxt

        @pl.when(s == ns - 1)
        def _():
            xv, dn = h_ref[...], acc_ref[...]
            rstd = lax.rsqrt(jnp.mean(xv * xv, axis=-1, keepdims=True) + EPS)
            xh = xv * rstd
            dg_ref[...] += jnp.sum(dn * xh, axis=0, keepdims=True)
            dxh = dn * g_ref[...]
            dh_ref[...] = do_ref[...] + rstd * (dxh - xh * jnp.mean(dxh * xh, axis=-1, keepdims=True))

    hid_shape = jax.ShapeDtypeStruct((ns, n, fs), BF16)
    plan.before(f"{tag}_bwd")
    dh, da, db, dg = _pcall(
        body, name=f"{tag}_bwd", grid=(n // rows, ns), in_specs=[row, row, gain, hid, hid, w_row, w_row, w_row],
        out_specs=[row, hid, hid, gain],
        out_shape=[jax.ShapeDtypeStruct((n, d), F32), hid_shape, hid_shape, jax.ShapeDtypeStruct((1, d), F32)],
        scratch_shapes=[pltpu.VMEM((rows, d), F32)], compiler_params=_params())(dout, h, g, a, b, w1, w3, w2)
    plan.after(f"{tag}_bwd")
    plan.grads[f"{tag}_norm"] = dg
    plan.before(f"{tag}_gw2")
    gw2 = _wgrad(hm, dout, f"{tag}_gw2", alpha=0.5)
    plan.after(f"{tag}_gw2")
    plan.grads[f"{tag}_w2"] = gw2
    plan.before(f"{tag}_gw1")
    gw1 = _wgrad(da, n1, f"{tag}_gw1")
    plan.after(f"{tag}_gw1")
    plan.grads[f"{tag}_w1"] = gw1
    plan.before(f"{tag}_gw3")
    gw3 = _wgrad(db, n1, f"{tag}_gw3")
    plan.after(f"{tag}_gw3")
    return dh, dg, gw1, gw3, gw2


def _local_step(x, tgt, plan):
    n = x.shape[0]
    grads = plan.grads

    def f(name):
        w = plan.get(name)
        return w.reshape(1, D_MODEL) if name.endswith('_norm') and name != 'gla_out_norm' else w

    def carried(tag, fn, *args, **kw):
        plan.before(tag)
        out = fn(*args, **kw)
        plan.after(tag)
        return out

    h1, ffn1 = _ffn_fwd(x, f('ffn1_norm'), f('ffn1_w1'), f('ffn1_w3'), f('ffn1_w2'), "ffn1", plan)
    u = carried("mix_rms", _rms_fwd, h1, f('mix_norm'), "mix_rms")
    w_in = f('w_in')
    w_a = jnp.concatenate([w_in[:, :512], _pad_heads(w_in[:, 512:768]), _pad_heads(w_in[:, 768:1024]), w_in[:, 1024:2048]],
                          axis=1)
    w_al = jnp.pad(w_in[:, 2048:2048 + GLA_RANK], ((0, 0), (0, LANE - GLA_RANK)))
    w_g = w_in[:, 2048 + GLA_RANK:]
    za = carried("in_a", _mm, u, w_a, name="in_a")
    zg = carried("in_g", _mm, u, w_g, name="in_g")
    al = _mm(u, w_al, name="in_al")
    ar, ai, bbar_re, bbar_im = _s5_discretize(f('s5_lambda_re'), f('s5_lambda_im'), f('s5_log_dt'), f('s5_b_re'), f('s5_b_im'))
    t_b = _bd_tiles(bbar_re.transpose(0, 2, 1), bbar_im.transpose(0, 2, 1)).astype(BF16)
    t_c = _bd_tiles(f('s5_c_re'), -f('s5_c_im')).astype(BF16)
    ar8 = jnp.broadcast_to(ar.reshape(1, S5_GP), (SEG, S5_GP))
    ai8 = jnp.broadcast_to(ai.reshape(1, S5_GP), (SEG, S5_GP))
    pw_r, pw_i = _segment_powers(ar, ai, n // SEG)
    dskip = f('s5_d').reshape(1, S5_W)
    u_s5 = _permute_rows(za[:, :S5_W])
    xs = _s5_scan(u_s5, t_b, ar8, ai8, pw_r, pw_i, "s5_scan")
    ys_p = _bd_reduce(xs, t_c, _scale_rows(u_s5, dskip, "s5_skip"), "s5_y")
    ys = _unpermute_rows(ys_p)
    zgelu = _gelu_fwd(ys, "s5_gelu")
    t_glu = _mm(zgelu, f('s5_glu_w'), bias=f('s5_glu_b').reshape(1, S5_W), name="s5_glu_t")
    y_s5 = _glu_fwd(zgelu, t_glu, "s5_glu")
    wup = jnp.pad(f('gla_a_up_w'), ((0, LANE - GLA_RANK), (0, 0)))
    wup_h = wup.reshape(LANE, GLA_HEADS, GLA_DK).transpose(1, 0, 2)
    bup_h = f('gla_a_up_b').reshape(GLA_HEADS, 1, GLA_DK)
    gn_h = f('gla_out_norm').reshape(GLA_HEADS, 1, GLA_DV)
    y_gla, s_prev = carried("gla_fwd", _gla_fwd, za, al, wup_h, bup_h, gn_h, "gla_fwd")
    ps = _mm(y_s5, f('proj_s5'), name="proj_s5")
    pg = carried("proj_gla", _mm, y_gla, f('proj_gla'), name="proj_gla")
    merged = _merge_fwd(zg, ps, pg, "merge")
    h2 = _mm(merged, f('w_out'), res=h1, name="w_out")
    h3, ffn2 = _ffn_fwd(h2, f('ffn2_norm'), f('ffn2_w1'), f('ffn2_w3'), f('ffn2_w2'), "ffn2", plan)
    loss, dh3, g_final = _final_loss(h3, f('final_norm').reshape(1, D_MODEL), tgt, "loss")
    plan.loss = loss[0, 0]
    grads['final_norm'] = g_final.reshape(D_MODEL)
    dh2, grads['ffn2_norm'], grads['ffn2_w1'], grads['ffn2_w3'], grads['ffn2_w2'] = _ffn_bwd(
        dh3, ffn2, f('ffn2_norm'), f('ffn2_w1'), f('ffn2_w3'), f('ffn2_w2'), "ffn2", plan)
    dm = carried("d_merged", _mm, dh2, f('w_out'), tb=True, name="d_merged")
    grads['w_out'] = _mm(merged, dh2, ta=True, name="g_w_out")
    dps, dpg, dzg = carried("d_merge", _merge_bwd, dm, zg, ps, pg, "d_merge")
    grads['proj_s5'] = _mm(y_s5, dps, ta=True, name="g_proj_s5")
    grads['proj_gla'] = _mm(y_gla, dpg, ta=True, name="g_proj_gla")
    dy_s5 = _mm(dps, f('proj_s5'), tb=True, name="d_y_s5")
    dy_gla = _mm(dpg, f('proj_gla'), tb=True, name="d_y_gla")
    dzgelu, dt_glu, g_glu_b = _glu_bwd1(dy_s5, zgelu, t_glu, "d_glu")
    grads['s5_glu_b'] = g_glu_b.reshape(S5_W)
    grads['s5_glu_w'] = _mm(zgelu, dt_glu, ta=True, name="g_glu_w")
    dzgelu = _mm(dt_glu, f('s5_glu_w'), tb=True, res=dzgelu, name="d_gelu")
    dys, du_skip, g_d = _glu_bwd2(_permute_rows(dzgelu), ys_p, u_s5, dskip, "d_s5_y")
    grads['s5_d'] = g_d.reshape(S5_G, S5_H)
    lam, da8 = _s5_scan_bwd(dys, t_c, xs, ar8, ai8, pw_r, pw_i, "s5_scan_bwd")
    g_c = _bd_blocks(_bd_outer(dys, xs, "g_s5_c"))
    grads['s5_c_re'], grads['s5_c_im'] = g_c[0], -g_c[1]
    g_b = _bd_blocks(_bd_outer(u_s5, lam, "g_s5_b")).transpose(0, 1, 3, 2)
    g_bbar_re, g_bbar_im = g_b[0], g_b[1]
    da = jnp.sum(da8, axis=0)
    g_ar, g_ai = da[:S5_GP].reshape(S5_G, S5_P), da[S5_GP:].reshape(S5_G, S5_P)
    _, disc_vjp = jax.vjp(_s5_discretize, f('s5_lambda_re'), f('s5_lambda_im'), f('s5_log_dt'), f('s5_b_re'), f('s5_b_im'))
    (grads['s5_lambda_re'], grads['s5_lambda_im'], grads['s5_log_dt'], grads['s5_b_re'],
     grads['s5_b_im']) = disc_vjp((g_ar, g_ai, g_bbar_re, g_bbar_im))
    du_s5 = _unpermute_rows(_bd_reduce(lam, t_b, du_skip, "d_s5_u"))
    dza, dz, dgn, dbup = carried("gla_bwd", _gla_bwd, za, al, wup_h, bup_h, gn_h, s_prev, dy_gla, du_s5, "gla_bwd")
    grads['gla_out_norm'] = dgn.reshape(GLA_HEADS * GLA_DV)
    grads['gla_a_up_b'] = dbup.reshape(GLA_HEADS * GLA_DK)
    grads['gla_a_up_w'] = _unpad_heads(_mm(al, dz, ta=True, name="g_a_up")[:GLA_RANK])
    dal = _mm(dz, _pad_heads(wup), tb=True, name="d_a_low")
    g_wa = _mm(u, dza, ta=True, name="g_in_a")
    g_wg = _mm(u, dzg, ta=True, name="g_in_g")
    g_wal = _mm(u, dal, ta=True, name="g_in_al")
    grads['w_in'] = jnp.concatenate([g_wa[:, :512], _unpad_heads(g_wa[:, 512:1024]), _unpad_heads(g_wa[:, 1024:1536]),
                                     g_wa[:, 1536:], g_wal[:, :GLA_RANK], g_wg], axis=1)
    du = carried("d_u_a", _mm, dza, w_a, tb=True, name="d_u_a")
    du = _mm(dzg, w_g, tb=True, res=du, name="d_u_g")
    du = _mm(dal, w_al, tb=True, res=du, name="d_u_al")
    dh1, g_mix = carried("d_mix_rms", _rms_bwd, h1, f('mix_norm'), du, dh2, "d_mix_rms")
    grads['mix_norm'] = g_mix
    dx, grads['ffn1_norm'], grads['ffn1_w1'], grads['ffn1_w3'], grads['ffn1_w2'] = _ffn_bwd(
        dh1, ffn1, f('ffn1_norm'), f('ffn1_w1'), f('ffn1_w3'), f('ffn1_w2'), "ffn1", plan)
    return loss[0, 0], dx


MIXER_WEIGHTS = ['w_in', 's5_glu_w', 'proj_s5', 'proj_gla', 'w_out', 'gla_a_up_w']
FFN1_WEIGHTS, FFN2_WEIGHTS = FFN_WEIGHTS[:3], FFN_WEIGHTS[3:]
TRANSPOSED = ['ffn1_w1', 'ffn1_w3', 'ffn2_w1', 'ffn2_w3']


def _local_shard(w, nm):
    return jnp.swapaxes(w, 1, 2)[0] if nm in TRANSPOSED else w[0]
FFN1_EARLY = ['ffn1_w2']
FFN1_LATE = ['ffn1_w1', 'ffn1_w3']
GRAD_GROUPS = {'ffn2': FFN2_WEIGHTS, 'mixer': ['w_out', 'proj_s5', 'proj_gla', 's5_glu_w', 'w_in'], 'ffn1': FFN1_WEIGHTS}


class _Plan:
    def __init__(self, a, c_arr, s_arr):
        self.a, self.c_arr, self.s_arr = a, c_arr, s_arr
        self.grads, self.weights, self.riding = {}, {}, {}
        self.g4s, self.chip_sums, self.halves, self.sib_halves = {}, {}, {}, {}
        for nm in SMALL:
            if nm != 'gla_a_up_w':
                self.weights[nm] = a[nm] if nm == 'final_norm' else a[nm][0]
        ici = _side_gather_ici(self._shards(FFN1_WEIGHTS))
        _run_side(ici, "gather_ffn1_ici")
        self._gathered(FFN1_WEIGHTS, _run_side(_side_gather_d2d(ici.outs), "gather_ffn1_d2d"))

    def _shards(self, names):
        return [_local_shard(self.a[nm], nm).astype(F32 if nm == 'gla_a_up_w' else BF16) for nm in names]

    def _gathered(self, names, arrs):
        for nm, g4 in zip(names, arrs):
            if nm in FFN_WEIGHTS:
                self.weights[nm] = g4
            elif nm in COL_SHARDED:
                self.weights[nm] = jnp.concatenate([g4[s] for s in range(4)], axis=1)
            else:
                self.weights[nm] = g4.reshape(4 * g4.shape[1], g4.shape[2])

    def get(self, name):
        return self.weights[name]

    def _shard_major(self, nm):
        g = self.grads[nm]
        if nm in FFN_WEIGHTS:
            return g
        if nm in COL_SHARDED:
            return jnp.stack(jnp.split(g, 4, axis=1))
        return g.reshape(4, g.shape[0] // 4, g.shape[1])

    def _schedule(self, tag):
        grp = GRAD_GROUPS
        gathers = {"ffn1_fwd": ('ici', MIXER_WEIGHTS), "mix_rms": ('d2d', MIXER_WEIGHTS),
                   "in_a": ('ici', FFN2_WEIGHTS[:1]), "in_g": ('d2d', FFN2_WEIGHTS[:1]),
                   "gla_fwd": ('ici', FFN2_WEIGHTS[1:]), "proj_gla": ('d2d', FFN2_WEIGHTS[1:])}
        if tag in gathers:
            kind, names = gathers[tag]
            key = tuple(names)
            if kind == 'ici':
                return [(_side_gather_ici(self._shards(names)), lambda outs: self.riding.update({key: outs}))]
            return [(_side_gather_d2d(self.riding[key]), lambda outs: self._gathered(names, outs))]
        steps = {"ffn2_gw1": (['ffn2_w2'], 0), "ffn2_gw3": (['ffn2_w1'], 0), "d_merged": (['ffn2_w3'], 0),
                 "gla_bwd": (grp['ffn2'], 1), "d_mix_rms": (grp['ffn2'], 2),
                 "d_u_a": (grp['mixer'], 0), "ffn1_bwd": (grp['mixer'], 1), "ffn1_gw2": (grp['mixer'], 2),
                 "ffn1_gw1": (FFN1_EARLY, 0), "ffn1_gw3": (FFN1_EARLY, 1), "adamw_early": (FFN1_LATE, 1)}
        entries = [self._reduce_stage(*steps[tag])] if tag in steps else []
        if tag == "ffn1_gw2":
            entries.append(self._small_stage(0))
        if tag == "ffn1_gw1":
            entries.append(self._small_stage(1))
        return entries

    def _small_stage(self, stage):
        if stage == 0:
            a, grads = self.a, self.grads
            self.small_parts = ([grads[nm].reshape(a[nm].shape) for nm in SMALL if nm != 'gla_a_up_w']
                                + [grads['gla_a_up_w'], self.loss.reshape(1)])
            packed = _pack_small(self.small_parts)

            def done(outs):
                self.small_pair = _small_add(packed, outs[0], "small_sum_pair")
            return _side_small_sibling(packed), done

        def done(outs):
            self.small_total = _small_add(self.small_pair, outs[0], "small_sum_chips")
        return _side_small_chips(self.small_pair), done

    def _reduce_stage(self, names, stage):
        if stage == 0:
            for nm in names:
                self.g4s[nm] = self._shard_major(nm)

            def done(outs):
                sums = _chip_sums([self.g4s[nm] for nm in names], outs, self.c_arr, f"chip_sum_{names[0]}")
                self.chip_sums.update(zip(names, sums))
            return _side_swap_halves([self.g4s[nm] for nm in names]), done
        if stage == 1:
            def done(outs):
                halves = _owner_sums([self.chip_sums[nm] for nm in names], outs, self.s_arr, f"owner_sum_{names[0]}")
                self.halves.update(zip(names, halves))
            return _side_scatter([self.chip_sums[nm] for nm in names]), done

        def done(outs):
            self.sib_halves.update(zip(names, outs))
        return _side_swap_reduced([self.halves[nm] for nm in names]), done

    def before(self, tag):
        entries = self._schedule(tag)
        if entries:
            merged = _merge_sides([side for side, _ in entries])
            self.riding[tag] = (merged, entries)
            _RIDER.append(merged)

    def after(self, tag):
        if tag in self.riding:
            merged, entries = self.riding.pop(tag)
            assert not _RIDER and merged.outs is not None, tag
            pos = 0
            for side, done in entries:
                done(merged.outs[pos:pos + len(side.out_shapes)])
                pos += len(side.out_shapes)

    def finish_alone(self, stage):
        names = FFN1_LATE if stage == 0 else GRAD_GROUPS['ffn1']
        side, done = self._reduce_stage(names, stage)
        done(_run_side(side, f"grad_ffn1_stage{stage}"))


def _train_step(a):
    x = a['x'][0]
    tgt = a['loss_target'][0]
    xi, yi, ci = lax.axis_index("x"), lax.axis_index("y"), lax.axis_index("c")
    c_arr = jnp.reshape(ci, (1,)).astype(jnp.int32)
    s_arr = jnp.reshape(2 * xi + yi, (1,)).astype(jnp.int32)
    plan = _Plan(a, c_arr, s_arr)
    loss, dx = _local_step(x, tgt, plan)
    red = {}
    small_sum = _unpack_small(plan.small_total, plan.small_parts)
    small_names = [nm for nm in SMALL if nm != 'gla_a_up_w']
    for nm, g in zip(small_names, small_sum[:-2]):
        red[nm] = g
    loss = small_sum[-1].reshape(())
    g_up = small_sum[-2]
    red['gla_a_up_w'] = lax.dynamic_slice(g_up, (0, (2 * xi + yi) * GLA_DK), (GLA_RANK, GLA_DK))
    out_g, out_d, out_m, out_v = {}, {}, {}, {}

    def update(names, tag):
        items = [(_local_shard(a[nm], nm), plan.halves[nm], plan.sib_halves[nm], _local_shard(a['m_' + nm], nm),
                  _local_shard(a['v_' + nm], nm)) for nm in names]
        plan.before(tag)
        res = _adamw_group(items, c_arr, tag)
        plan.after(tag)
        for k, nm in enumerate(names):
            back = (lambda t: jnp.swapaxes(t[None], 1, 2)) if nm in TRANSPOSED else (lambda t: t[None])
            out_g[nm], out_d[nm], out_m[nm], out_v[nm] = (back(t) for t in res[4 * k:4 * k + 4])

    plan.finish_alone(0)
    update([nm for nm in SHARDED if nm not in GRAD_GROUPS['ffn1']], "adamw_early")
    plan.finish_alone(2)
    update(GRAD_GROUPS['ffn1'], "adamw_ffn1")
    rest = [nm for nm in WEIGHTS if nm not in SHARDED]
    pk = lambda pre: _pack_small([a[pre + nm] for nm in rest])
    d, nm_, nv_ = _adamw(pk(''), _pack_small([red[nm] for nm in rest]), pk('m_'), pk('v_'), "adamw_small")
    like = [a[nm] for nm in rest]
    for nm, g, dd, mm_, vv_ in zip(rest, [red[nm].reshape(a[nm].shape) for nm in rest], _unpack_small(d, like),
                                   _unpack_small(nm_, like), _unpack_small(nv_, like)):
        out_g[nm], out_d[nm], out_m[nm], out_v[nm] = g, dd, mm_, vv_
    return (loss, dx[None], *[out_g[nm] for nm in WEIGHTS], *[out_d[nm] for nm in WEIGHTS],
            *[out_m[nm] for nm in WEIGHTS], *[out_v[nm] for nm in WEIGHTS])


def kernel(x, ffn1_norm, ffn1_w1, ffn1_w3, ffn1_w2, mix_norm, w_in, s5_lambda_re, s5_lambda_im, s5_log_dt, s5_b_re, s5_b_im, s5_c_re, s5_c_im, s5_d, s5_glu_w, s5_glu_b, gla_a_up_w, gla_a_up_b, gla_out_norm, proj_s5, proj_gla, w_out, ffn2_norm, ffn2_w1, ffn2_w3, ffn2_w2, final_norm, loss_target, m_ffn1_norm, m_ffn1_w1, m_ffn1_w3, m_ffn1_w2, m_mix_norm, m_w_in, m_s5_lambda_re, m_s5_lambda_im, m_s5_log_dt, m_s5_b_re, m_s5_b_im, m_s5_c_re, m_s5_c_im, m_s5_d, m_s5_glu_w, m_s5_glu_b, m_gla_a_up_w, m_gla_a_up_b, m_gla_out_norm, m_proj_s5, m_proj_gla, m_w_out, m_ffn2_norm, m_ffn2_w1, m_ffn2_w3, m_ffn2_w2, m_final_norm, v_ffn1_norm, v_ffn1_w1, v_ffn1_w3, v_ffn1_w2, v_mix_norm, v_w_in, v_s5_lambda_re, v_s5_lambda_im, v_s5_log_dt, v_s5_b_re, v_s5_b_im, v_s5_c_re, v_s5_c_im, v_s5_d, v_s5_glu_w, v_s5_glu_b, v_gla_a_up_w, v_gla_a_up_b, v_gla_out_norm, v_proj_s5, v_proj_gla, v_w_out, v_ffn2_norm, v_ffn2_w1, v_ffn2_w3, v_ffn2_w2, v_final_norm):
    return _train_step(dict(locals()))
```

```python
import functools

import jax
import jax.numpy as jnp
from jax import lax
from jax.experimental import pallas as pl
from jax.experimental.pallas import tpu as pltpu

F32 = jnp.float32
BF16 = jnp.bfloat16
HI = lax.Precision.HIGHEST
MESH_ID = pl.DeviceIdType.MESH

D_MODEL = 1024
EPS = 1e-6
S5_G, S5_P, S5_H = 32, 64, 16
S5_W = S5_G * S5_H
S5_GP = S5_G * S5_P
SEG = 8
SCAN_ROWS = 256
GLA_HEADS, GLA_DK, GLA_DV = 4, 64, 128
GLA_CHUNK = 64
GLA_TAU = 16.0
GLA_RANK = 16
ADAM_LR, ADAM_B1, ADAM_B2, ADAM_EPS, ADAM_WD, ADAM_STEP = 0.001, 0.9, 0.999, 1e-08, 0.01, 10
V7X_VMEM_LIMIT = 56 * 1024 * 1024
LANE = 128

WEIGHTS = ['ffn1_norm', 'ffn1_w1', 'ffn1_w3', 'ffn1_w2', 'mix_norm', 'w_in', 's5_lambda_re', 's5_lambda_im',
           's5_log_dt', 's5_b_re', 's5_b_im', 's5_c_re', 's5_c_im', 's5_d', 's5_glu_w', 's5_glu_b', 'gla_a_up_w',
           'gla_a_up_b', 'gla_out_norm', 'proj_s5', 'proj_gla', 'w_out', 'ffn2_norm', 'ffn2_w1', 'ffn2_w3',
           'ffn2_w2', 'final_norm']
SHARDED = ['ffn1_w1', 'ffn1_w3', 'ffn1_w2', 'w_in', 's5_glu_w', 'proj_s5', 'proj_gla', 'w_out',
           'ffn2_w1', 'ffn2_w3', 'ffn2_w2']
COL_SHARDED = ['ffn1_w1', 'ffn1_w3', 'w_in', 'proj_s5', 'proj_gla', 'ffn2_w1', 'ffn2_w3', 'gla_a_up_w']
SMALL = [n for n in WEIGHTS if n not in SHARDED]
FFN_WEIGHTS = ['ffn1_w1', 'ffn1_w3', 'ffn1_w2', 'ffn2_w1', 'ffn2_w3', 'ffn2_w2']


def _params(**kw):
    return pltpu.CompilerParams(vmem_limit_bytes=V7X_VMEM_LIMIT, **kw)


class _Side:
    def __init__(self, ins, out_shapes, nsem, copies, aliased=False):
        self.ins, self.out_shapes, self.nsem, self.copies, self.aliased = list(ins), list(out_shapes), nsem, copies, aliased
        self.outs = None


_RIDER = []


def _pcall(body, **kw):
    if _RIDER:
        return _carry(body, _RIDER.pop(), **kw)
    return pl.pallas_call(body, **kw)


def _carry(body, side, *, name, grid, in_specs, out_specs, out_shape, scratch_shapes=(), compiler_params=None):
    del compiler_params
    single = not isinstance(out_shape, (list, tuple))
    out_specs = [out_specs] if single else list(out_specs)
    out_shape = [out_shape] if single else list(out_shape)
    n_in, n_out, n_scr = len(in_specs), len(out_shape), len(scratch_shapes)
    s_in, s_out = len(side.ins), len(side.out_shapes)
    any_spec = pl.BlockSpec(memory_space=pl.ANY)

    def wrapped(*refs):
        cuts = [n_in, s_in, n_out, s_out, n_scr]
        parts, pos = [], 0
        for c in cuts:
            parts.append(refs[pos:pos + c])
            pos += c
        ins, sins, outs, souts, scr = parts
        ssem, rsem = refs[pos], refs[pos + 1]
        first = last = None
        for d, g in enumerate(grid):
            i = pl.program_id(d)
            first = (i == 0) if first is None else first & (i == 0)
            last = (i == g - 1) if last is None else last & (i == g - 1)

        @pl.when(first)
        def _():
            for cp in side.copies(sins, souts, ssem, rsem):
                cp.start()

        body(*ins, *outs, *scr)

        @pl.when(last)
        def _():
            for cp in side.copies(sins, souts, ssem, rsem):
                cp.wait()

    call = pl.pallas_call(
        wrapped, name=name, grid=grid, in_specs=list(in_specs) + [any_spec] * s_in,
        out_specs=out_specs + [any_spec] * s_out, out_shape=out_shape + side.out_shapes,
        scratch_shapes=list(scratch_shapes) + [pltpu.SemaphoreType.DMA((side.nsem,)), pltpu.SemaphoreType.DMA((side.nsem,))],
        input_output_aliases={n_in + j: n_out + j for j in range(s_in)} if side.aliased else {},
        compiler_params=_params(has_side_effects=True))

    def run(*args):
        res = call(*args, *side.ins)
        side.outs = list(res[n_out:])
        return res[0] if single else list(res[:n_out])

    return run


def _run_side(side, name):
    s_in, s_out = len(side.ins), len(side.out_shapes)
    any_spec = pl.BlockSpec(memory_space=pl.ANY)

    def body(*refs):
        sins, souts = refs[:s_in], refs[s_in:s_in + s_out]
        ssem, rsem = refs[s_in + s_out:]
        cps = side.copies(sins, souts, ssem, rsem)
        for cp in cps:
            cp.start()
        for cp in cps:
            cp.wait()

    side.outs = list(pl.pallas_call(
        body, name=name, in_specs=[any_spec] * s_in, out_specs=[any_spec] * s_out, out_shape=side.out_shapes,
        scratch_shapes=[pltpu.SemaphoreType.DMA((side.nsem,)), pltpu.SemaphoreType.DMA((side.nsem,))],
        input_output_aliases={j: j for j in range(s_in)} if side.aliased else {},
        compiler_params=pltpu.CompilerParams(has_side_effects=True))(*side.ins))
    return side.outs


def _pick(n, cap, quantum):
    if n <= cap:
        return n
    best = None
    for t in range(quantum, cap + 1, quantum):
        if n % t == 0:
            best = t
    assert best is not None, (n, cap, quantum)
    return best


def _sigmoid(x):
    return jax.nn.sigmoid(x)


def _mm(a, b, *, name, ta=False, tb=False, out_dtype=F32, alpha=1.0, res=None, bias=None, exact=False, shard=None):
    ns = 4
    (k_a, m) = a.shape[-2:] if ta else a.shape[-2:][::-1]
    (k_b, n) = b.shape[-2:][::-1] if tb else b.shape[-2:]
    assert k_a == k_b, (a.shape, b.shape, ta, tb)
    assert (a.ndim == 3) == (shard in ('k', 'm')) and (b.ndim == 3) == (shard in ('n', 'k'))
    k = k_a
    tm = _pick(m, 1024, 128)
    tn = _pick(n, 1024, 128)
    tk = _pick(k, 1024, 128)
    pm, pn, pk = m // tm, n // tn, k // tk
    gm = pm * (ns if shard == 'm' else 1)
    gn = pn * (ns if shard == 'n' else 1)
    gk = pk * (ns if shard == 'k' else 1)
    dims = (((0,) if ta else (1,), (1,) if tb else (0,)), ((), ()))
    op_dtype = F32 if exact else BF16

    def body(*refs):
        a_ref, b_ref = refs[0], refs[1]
        pos = 2
        res_ref = bias_ref = None
        if res is not None:
            res_ref = refs[pos]
            pos += 1
        if bias is not None:
            bias_ref = refs[pos]
            pos += 1
        o_ref, acc_ref = refs[pos], refs[pos + 1]
        kk = pl.program_id(2)

        @pl.when(kk == 0)
        def _():
            acc_ref[...] = jnp.zeros_like(acc_ref)

        acc_ref[...] += lax.dot_general(a_ref[...].astype(op_dtype), b_ref[...].astype(op_dtype), dims,
                                        precision=HI if exact else None, preferred_element_type=F32)

        @pl.when(kk == gk - 1)
        def _():
            o = acc_ref[...]
            if alpha != 1.0:
                o = o * alpha
            if bias_ref is not None:
                o = o + bias_ref[...]
            if res_ref is not None:
                o = o + res_ref[...]
            o_ref[...] = o.astype(out_dtype)

    def spec(block, sharded_on, order):
        per = {'m': pm, 'n': pn, 'k': pk}

        def index(i, j, kk):
            g = {'m': i, 'n': j, 'k': kk}
            r, c = order(i % pm if shard == 'm' else i, j % pn if shard == 'n' else j, kk % pk if shard == 'k' else kk)
            if sharded_on is None:
                return (r, c)
            return (g[sharded_on] // per[sharded_on], r, c)

        return pl.BlockSpec(block if sharded_on is None else (None,) + block, index)

    a_sh = shard if shard in ('k', 'm') else None
    b_sh = shard if shard in ('n', 'k') else None
    o_sh = shard if shard in ('n', 'm') else None
    a_spec = spec((tk, tm), a_sh, lambda i, j, kk: (kk, i)) if ta else spec((tm, tk), a_sh, lambda i, j, kk: (i, kk))
    b_spec = spec((tn, tk), b_sh, lambda i, j, kk: (j, kk)) if tb else spec((tk, tn), b_sh, lambda i, j, kk: (kk, j))
    ins, in_specs = [a, b], [a_spec, b_spec]
    if res is not None:
        assert o_sh is None
        ins.append(res)
        in_specs.append(pl.BlockSpec((tm, tn), lambda i, j, kk: (i, j)))
    if bias is not None:
        assert o_sh is None
        ins.append(bias)
        in_specs.append(pl.BlockSpec((1, tn), lambda i, j, kk: (0, j)))
    out_shape = (m, n) if o_sh is None else (ns, m, n)
    return _pcall(body, name=name, grid=(gm, gn, gk), in_specs=in_specs,
                  out_specs=spec((tm, tn), o_sh, lambda i, j, kk: (i, j)),
                  out_shape=jax.ShapeDtypeStruct(out_shape, out_dtype),
                  scratch_shapes=[pltpu.VMEM((tm, tn), F32)], compiler_params=_params())(*ins)


ROWS_VMEM_BUDGET = 24 * 1024 * 1024


def _rows(body, ins, outs, *, n, name):
    cols = sum(a.shape[1] for a, kind in ins if kind == 'r') + sum(c for c, _, kind in outs if kind == 'r')
    cap = 256
    while cap < 2048 and 2 * 4 * cols * (2 * cap) <= ROWS_VMEM_BUDGET:
        cap *= 2
    tm = _pick(n, cap, 16)
    in_specs = []
    for arr, kind in ins:
        if kind == 'r':
            in_specs.append(pl.BlockSpec((tm, arr.shape[1]), lambda i: (i, 0)))
        else:
            in_specs.append(pl.BlockSpec(arr.shape, lambda i: (0, 0)))
    out_specs, out_shape = [], []
    for cols, dtype, kind in outs:
        if kind == 'r':
            out_specs.append(pl.BlockSpec((tm, cols), lambda i: (i, 0)))
            out_shape.append(jax.ShapeDtypeStruct((n, cols), dtype))
        else:
            out_specs.append(pl.BlockSpec((1, cols), lambda i: (0, 0)))
            out_shape.append(jax.ShapeDtypeStruct((1, cols), dtype))
    n_in = len(ins)
    acc_ids = [j for j, o in enumerate(outs) if o[2] == 'a']

    def wrapped(*refs):
        if acc_ids:
            @pl.when(pl.program_id(0) == 0)
            def _():
                for j in acc_ids:
                    refs[n_in + j][...] = jnp.zeros_like(refs[n_in + j])
        body(*refs)

    res = _pcall(wrapped, name=name, grid=(n // tm,), in_specs=in_specs, out_specs=out_specs, out_shape=out_shape,
                 compiler_params=_params())(*[a for a, _ in ins])
    return res


def _rms_fwd(x, g, name):
    def body(x_ref, g_ref, o_ref):
        xv = x_ref[...]
        rstd = lax.rsqrt(jnp.mean(xv * xv, axis=-1, keepdims=True) + EPS)
        o_ref[...] = (xv * rstd * g_ref[...]).astype(BF16)
    return _rows(body, [(x, 'r'), (g, 'f')], [(x.shape[1], BF16, 'r')], n=x.shape[0], name=name)[0]


def _rms_bwd(x, g, dn, dres, name):
    def body(x_ref, g_ref, dn_ref, dres_ref, dx_ref, dg_ref):
        xv = x_ref[...]
        rstd = lax.rsqrt(jnp.mean(xv * xv, axis=-1, keepdims=True) + EPS)
        xh = xv * rstd
        dn = dn_ref[...]
        dg_ref[...] += jnp.sum(dn * xh, axis=0, keepdims=True)
        dxh = dn * g_ref[...]
        dx_ref[...] = dres_ref[...] + rstd * (dxh - xh * jnp.mean(dxh * xh, axis=-1, keepdims=True))
    d = x.shape[1]
    return _rows(body, [(x, 'r'), (g, 'f'), (dn, 'r'), (dres, 'r')], [(d, F32, 'r'), (d, F32, 'a')],
                 n=x.shape[0], name=name)


def _gelu_parts(y):
    c0 = 0.7978845608028654
    inner = c0 * (y + 0.044715 * y * y * y)
    th = jnp.tanh(inner)
    return th, c0 * (1.0 + 3.0 * 0.044715 * y * y)


def _gelu_fwd(y, name):
    def body(y_ref, o_ref):
        yv = y_ref[...]
        th, _ = _gelu_parts(yv)
        o_ref[...] = 0.5 * yv * (1.0 + th)
    return _rows(body, [(y, 'r')], [(y.shape[1], F32, 'r')], n=y.shape[0], name=name)[0]


def _glu_fwd(zg, t, name):
    def body(z_ref, t_ref, o_ref):
        o_ref[...] = (z_ref[...] * _sigmoid(t_ref[...])).astype(BF16)
    return _rows(body, [(zg, 'r'), (t, 'r')], [(zg.shape[1], BF16, 'r')], n=zg.shape[0], name=name)[0]


def _glu_bwd1(dy, zg, t, name):
    def body(dy_ref, z_ref, t_ref, dz_ref, dt_ref, db_ref):
        dyv, zv = dy_ref[...], z_ref[...]
        sg = _sigmoid(t_ref[...])
        dz_ref[...] = dyv * sg
        dt = dyv * zv * sg * (1.0 - sg)
        dt_ref[...] = dt.astype(BF16)
        db_ref[...] += jnp.sum(dt, axis=0, keepdims=True)
    w = zg.shape[1]
    return _rows(body, [(dy, 'r'), (zg, 'r'), (t, 'r')], [(w, F32, 'r'), (w, BF16, 'r'), (w, F32, 'a')],
                 n=zg.shape[0], name=name)


def _glu_bwd2(dzg, ys, u, dskip, name):
    def body(dz_ref, y_ref, u_ref, d_ref, dy_ref, du_ref, dd_ref):
        yv = y_ref[...]
        th, dinner = _gelu_parts(yv)
        dy = dz_ref[...] * (0.5 * (1.0 + th) + 0.5 * yv * (1.0 - th * th) * dinner)
        dy_ref[...] = dy.astype(BF16)
        du_ref[...] = dy * d_ref[...]
        dd_ref[...] += jnp.sum(dy * u_ref[...], axis=0, keepdims=True)
    w = ys.shape[1]
    return _rows(body, [(dzg, 'r'), (ys, 'r'), (u, 'r'), (dskip, 'f')], [(w, BF16, 'r'), (w, F32, 'r'), (w, F32, 'a')],
                 n=ys.shape[0], name=name)


def _scale_rows(u, dskip, name):
    def body(u_ref, d_ref, o_ref):
        o_ref[...] = u_ref[...] * d_ref[...]
    return _rows(body, [(u, 'r'), (dskip, 'f')], [(u.shape[1], F32, 'r')], n=u.shape[0], name=name)[0]


def _merge_fwd(zg, ps, pg, name):
    def body(z_ref, ps_ref, pg_ref, o_ref):
        zv = z_ref[...]
        o_ref[...] = (_sigmoid(zv[:, :D_MODEL]) * ps_ref[...] + _sigmoid(zv[:, D_MODEL:]) * pg_ref[...]).astype(BF16)
    return _rows(body, [(zg, 'r'), (ps, 'r'), (pg, 'r')], [(D_MODEL, BF16, 'r')], n=zg.shape[0], name=name)[0]


def _merge_bwd(dm, zg, ps, pg, name):
    def body(dm_ref, z_ref, ps_ref, pg_ref, dps_ref, dpg_ref, dz_ref):
        dmv, zv = dm_ref[...], z_ref[...]
        s1, s2 = _sigmoid(zv[:, :D_MODEL]), _sigmoid(zv[:, D_MODEL:])
        dps_ref[...] = (dmv * s1).astype(BF16)
        dpg_ref[...] = (dmv * s2).astype(BF16)
        dz_ref[:, :D_MODEL] = (dmv * ps_ref[...] * s1 * (1.0 - s1)).astype(BF16)
        dz_ref[:, D_MODEL:] = (dmv * pg_ref[...] * s2 * (1.0 - s2)).astype(BF16)
    return _rows(body, [(dm, 'r'), (zg, 'r'), (ps, 'r'), (pg, 'r')],
                 [(D_MODEL, BF16, 'r'), (D_MODEL, BF16, 'r'), (2 * D_MODEL, BF16, 'r')], n=zg.shape[0], name=name)


def _final_loss(h, g, tgt, name):
    def body(h_ref, g_ref, t_ref, loss_ref, dh_ref, dg_ref):
        hv = h_ref[...]
        rstd = lax.rsqrt(jnp.mean(hv * hv, axis=-1, keepdims=True) + EPS)
        xh = hv * rstd
        err = xh * g_ref[...] - t_ref[...]
        part = 0.5 * jnp.sum(jnp.mean(err * err, axis=-1, keepdims=True), axis=0, keepdims=True)
        loss_ref[...] += jnp.broadcast_to(part, loss_ref.shape)
        dout = err * (1.0 / hv.shape[1])
        dg_ref[...] += jnp.sum(dout * xh, axis=0, keepdims=True)
        dxh = dout * g_ref[...]
        dh_ref[...] = rstd * (dxh - xh * jnp.mean(dxh * xh, axis=-1, keepdims=True))
    d = h.shape[1]
    return _rows(body, [(h, 'r'), (g, 'f'), (tgt, 'r')], [(LANE, F32, 'a'), (d, F32, 'r'), (d, F32, 'a')],
                 n=h.shape[0], name=name)


def _adamw_math(wv, gv, mv, vv):
    nm = ADAM_B1 * mv + (1.0 - ADAM_B1) * gv
    nv = ADAM_B2 * vv + (1.0 - ADAM_B2) * (gv * gv)
    m_hat = nm / (1.0 - ADAM_B1 ** ADAM_STEP)
    v_hat = nv / (1.0 - ADAM_B2 ** ADAM_STEP)
    return -ADAM_LR * (m_hat / (jnp.sqrt(v_hat) + ADAM_EPS) + ADAM_WD * wv), nm, nv


def _adamw(w, g, m, v, name):
    def body(w_ref, g_ref, m_ref, v_ref, d_ref, nm_ref, nv_ref):
        d_ref[...], nm_ref[...], nv_ref[...] = _adamw_math(w_ref[...], g_ref[...], m_ref[...], v_ref[...])
    c = w.shape[1]
    return _rows(body, [(w, 'r'), (g, 'r'), (m, 'r'), (v, 'r')], [(c, F32, 'r')] * 3, n=w.shape[0], name=name)


ADAMW_BLOCKS = 8


def _adamw_group(items, c_arr, name):
    per = ADAMW_BLOCKS // 2
    n = len(items)

    def body(c_ref, *refs):
        mine = (pl.program_id(0) // per) == c_ref[0]
        for k in range(n):
            w_ref, go_ref, gs_ref, m_ref, v_ref = refs[5 * k:5 * k + 5]
            g_ref, d_ref, nm_ref, nv_ref = refs[5 * n + 4 * k:5 * n + 4 * k + 4]
            gv = jnp.where(mine, go_ref[...], gs_ref[...])
            g_ref[...] = gv
            d_ref[...], nm_ref[...], nv_ref[...] = _adamw_math(w_ref[...], gv, m_ref[...], v_ref[...])

    in_specs, out_specs, out_shape, args = [pl.BlockSpec(memory_space=pltpu.SMEM)], [], [], [c_arr]
    for item in items:
        r, cols = item[0].shape
        assert r % (8 * ADAMW_BLOCKS) == 0, item[0].shape
        tr = r // ADAMW_BLOCKS
        full = pl.BlockSpec((tr, cols), lambda i: (i, 0))
        half = pl.BlockSpec((tr, cols), lambda i: (i % per, 0))
        in_specs += [full, half, half, full, full]
        out_specs += [full] * 4
        out_shape += [jax.ShapeDtypeStruct((r, cols), F32)] * 4
        args += list(item)
    return _pcall(body, name=name, grid=(ADAMW_BLOCKS,), in_specs=in_specs, out_specs=out_specs, out_shape=out_shape,
                  compiler_params=_params())(*args)


def _shift_rows(v, sh, down):
    rolled = pltpu.roll(v, sh if down else v.shape[0] - sh, axis=0)
    row = lax.broadcasted_iota(jnp.int32, v.shape, 0)
    keep = (row >= sh) if down else (row < v.shape[0] - sh)
    return jnp.where(keep, rolled, 0.0)


def _chain_segments(st_r, st_i, pw_r_ref, pw_i_ref, conj, down):
    vr, vi = st_r[...], st_i[...]
    sh, k = 1, 0
    while sh < SEG:
        pr, pi = pw_r_ref[k:k + 1, :], pw_i_ref[k:k + 1, :]
        if conj:
            pi = -pi
        sr, si = _shift_rows(vr, sh, down), _shift_rows(vi, sh, down)
        vr, vi = vr + pr * sr - pi * si, vi + pr * si + pi * sr
        sh, k = sh * 2, k + 1
    st_r[...] = _shift_rows(vr, 1, down)
    st_i[...] = _shift_rows(vi, 1, down)


def _expand_block(u_ref, t_ref, bu_ref):
    for j in range(BD_TILES):
        k = j % 4
        bu_ref[:, j * BD_ST:(j + 1) * BD_ST] = _dot(u_ref[:, k * BD_CH:(k + 1) * BD_CH], t_ref[j])


def _s5_scan(u, tiles, ar8, ai8, pw_r, pw_i, name):
    n = u.shape[0]
    rb = SCAN_ROWS
    nb, steps, lc = n // rb, rb // SEG, 512

    def body(u_ref, t_ref, ar_ref, ai_ref, pwr_ref, pwi_ref, x_ref, st_r, st_i, bu_ref):
        ph, b = pl.program_id(0), pl.program_id(1)

        @pl.when((ph == 0) & (b == 0))
        def _():
            st_r[...] = jnp.zeros_like(st_r)
            st_i[...] = jnp.zeros_like(st_i)

        _expand_block(u_ref, t_ref, bu_ref)

        def scan(store):
            for c in range(S5_GP // lc):
                re, im = slice(c * lc, (c + 1) * lc), slice(S5_GP + c * lc, S5_GP + (c + 1) * lc)
                a_r, a_i = ar_ref[:, re], ai_ref[:, re]

                def step(s, carry):
                    xr, xi = carry
                    rows = pl.ds(pl.multiple_of(s * SEG, SEG), SEG)
                    nr = a_r * xr - a_i * xi + bu_ref[rows, re]
                    ni = a_r * xi + a_i * xr + bu_ref[rows, im]
                    if store:
                        x_ref[rows, re] = nr
                        x_ref[rows, im] = ni
                    return nr, ni

                xr, xi = lax.fori_loop(0, steps, step, (st_r[:, re], st_i[:, re]), unroll=4)
                st_r[:, re] = xr
                st_i[:, re] = xi

        @pl.when(ph == 0)
        def _():
            scan(False)

        @pl.when((ph == 0) & (b == nb - 1))
        def _():
            _chain_segments(st_r, st_i, pwr_ref, pwi_ref, conj=False, down=True)

        @pl.when(ph == 1)
        def _():
            scan(True)

    full = lambda a: pl.BlockSpec(a.shape, lambda ph, b: (0, 0))
    return _pcall(body, name=name, grid=(2, nb),
                  in_specs=[pl.BlockSpec((rb, S5_W), lambda ph, b: (b, 0)), pl.BlockSpec(tiles.shape, lambda ph, b: (0, 0, 0)),
                            full(ar8), full(ai8), full(pw_r), full(pw_i)],
                  out_specs=pl.BlockSpec((rb, 2 * S5_GP), lambda ph, b: (b * ph, 0)),
                  out_shape=jax.ShapeDtypeStruct((n, 2 * S5_GP), F32),
                  scratch_shapes=[pltpu.VMEM((SEG, S5_GP), F32), pltpu.VMEM((SEG, S5_GP), F32),
                                  pltpu.VMEM((rb, 2 * S5_GP), F32)],
                  compiler_params=_params())(u, tiles, ar8, ai8, pw_r, pw_i)


def _s5_scan_bwd(dy, tiles, xs, ar8, ai8, pw_r, pw_i, name):
    n = dy.shape[0]
    rb = SCAN_ROWS
    nb, steps, lc = n // rb, rb // SEG, 256

    def body(dy_ref, t_ref, x_ref, ar_ref, ai_ref, pwr_ref, pwi_ref, lam_ref, da_ref, st_r, st_i, gx_ref):
        ph, b = pl.program_id(0), pl.program_id(1)

        @pl.when((ph == 0) & (b == 0))
        def _():
            st_r[...] = jnp.zeros_like(st_r)
            st_i[...] = jnp.zeros_like(st_i)
            da_ref[...] = jnp.zeros_like(da_ref)

        _expand_block(dy_ref, t_ref, gx_ref)

        def scan(store):
            for c in range(S5_GP // lc):
                re, im = slice(c * lc, (c + 1) * lc), slice(S5_GP + c * lc, S5_GP + (c + 1) * lc)
                a_r, a_i = ar_ref[:, re], ai_ref[:, re]

                def step(s, carry):
                    rows = pl.ds(pl.multiple_of((steps - 1 - s) * SEG, SEG), SEG)
                    if store:
                        lr, li, dr, di = carry
                        xr, xi = x_ref[rows, re], x_ref[rows, im]
                        dr = dr + lr * xr + li * xi
                        di = di + li * xr - lr * xi
                    else:
                        lr, li = carry
                    nr = a_r * lr + a_i * li + gx_ref[rows, re]
                    ni = a_r * li - a_i * lr + gx_ref[rows, im]
                    if store:
                        lam_ref[rows, re] = nr
                        lam_ref[rows, im] = ni
                        return nr, ni, dr, di
                    return nr, ni

                if store:
                    lr, li, dr, di = lax.fori_loop(0, steps, step, (st_r[:, re], st_i[:, re], da_ref[:, re], da_ref[:, im]),
                                                   unroll=4)
                    da_ref[:, re] = dr
                    da_ref[:, im] = di
                else:
                    lr, li = lax.fori_loop(0, steps, step, (st_r[:, re], st_i[:, re]), unroll=4)
                st_r[:, re] = lr
                st_i[:, re] = li

        @pl.when(ph == 0)
        def _():
            scan(False)

        @pl.when((ph == 0) & (b == nb - 1))
        def _():
            _chain_segments(st_r, st_i, pwr_ref, pwi_ref, conj=True, down=False)

        @pl.when(ph == 1)
        def _():
            scan(True)

    full = lambda a: pl.BlockSpec(a.shape, lambda ph, b: (0, 0))
    rev = lambda ph, b: (nb - 1 - b, 0)
    return _pcall(body, name=name, grid=(2, nb),
                  in_specs=[pl.BlockSpec((rb, S5_W), rev), pl.BlockSpec(tiles.shape, lambda ph, b: (0, 0, 0)),
                            pl.BlockSpec((rb, 2 * S5_GP), lambda ph, b: ((nb - 1 - b) * ph, 0)),
                            full(ar8), full(ai8), full(pw_r), full(pw_i)],
                  out_specs=[pl.BlockSpec((rb, 2 * S5_GP), lambda ph, b: (nb - 1 - b * ph, 0)),
                             pl.BlockSpec((SEG, 2 * S5_GP), lambda ph, b: (0, 0))],
                  out_shape=[jax.ShapeDtypeStruct((n, 2 * S5_GP), F32), jax.ShapeDtypeStruct((SEG, 2 * S5_GP), F32)],
                  scratch_shapes=[pltpu.VMEM((SEG, S5_GP), F32), pltpu.VMEM((SEG, S5_GP), F32),
                                  pltpu.VMEM((rb, 2 * S5_GP), F32)],
                  compiler_params=_params())(dy, tiles, xs, ar8, ai8, pw_r, pw_i)


def _s5_discretize(lam_re, lam_im, log_dt, b_re, b_im):
    dt = jnp.exp(log_dt)[:, None]
    mag = jnp.exp(lam_re * dt)
    ar = mag * jnp.cos(lam_im * dt)
    ai = mag * jnp.sin(lam_im * dt)
    den = lam_re * lam_re + lam_im * lam_im
    nr = ar - 1.0
    fr = (nr * lam_re + ai * lam_im) / den
    fi = (ai * lam_re - nr * lam_im) / den
    bbar_re = fr[:, :, None] * b_re - fi[:, :, None] * b_im
    bbar_im = fr[:, :, None] * b_im + fi[:, :, None] * b_re
    return ar, ai, bbar_re, bbar_im


BD_TILES, BD_CH, BD_ST, BD_GROUPS = 8, 128, 512, 8
BD_ROWS = 4096


def _bd_tiles(re, im):
    eye = jnp.eye(BD_GROUPS, dtype=re.dtype)

    def tiles(t):
        t = t.reshape(S5_G // BD_GROUPS, BD_GROUPS, S5_H, S5_P)
        return (t[:, :, :, None, :] * eye[None, :, None, :, None]).reshape(S5_G // BD_GROUPS, BD_CH, BD_ST)

    return jnp.concatenate([tiles(re), tiles(im)], axis=0)


def _bd_blocks(t):
    t = t.reshape(2, S5_G // BD_GROUPS, BD_GROUPS, S5_H, BD_GROUPS, S5_P)
    return jnp.einsum('rkahap->rkahp', t).reshape(2, S5_G, S5_H, S5_P)


def _bd_reduce(x, t, res, name):
    n = x.shape[0]
    tm = _pick(n, BD_ROWS, 16)

    def body(x_ref, t_ref, r_ref, o_ref):
        part = _dot(x_ref[...], t_ref[...], NT)

        @pl.when(pl.program_id(2) == 0)
        def _():
            o_ref[...] = r_ref[...] + part

        @pl.when(pl.program_id(2) == 1)
        def _():
            o_ref[...] += part

    return _pcall(body, name=name, grid=(n // tm, 4, 2),
                  in_specs=[pl.BlockSpec((tm, BD_ST), lambda i, k, r: (i, k + 4 * r)),
                            pl.BlockSpec((None, BD_CH, BD_ST), lambda i, k, r: (k + 4 * r, 0, 0)),
                            pl.BlockSpec((tm, BD_CH), lambda i, k, r: (i, k))],
                  out_specs=pl.BlockSpec((tm, BD_CH), lambda i, k, r: (i, k)),
                  out_shape=jax.ShapeDtypeStruct((n, S5_W), F32), compiler_params=_params())(x, t, res)


def _bd_outer(a, x, name):
    n = a.shape[0]
    tk = _pick(n, BD_ROWS, 16)
    nk = n // tk

    def body(a_ref, x_ref, o_ref):
        part = _dot(a_ref[...], x_ref[...], TN)

        @pl.when(pl.program_id(1) == 0)
        def _():
            o_ref[...] = part

        @pl.when(pl.program_id(1) > 0)
        def _():
            o_ref[...] += part

    return _pcall(body, name=name, grid=(BD_TILES, nk),
                  in_specs=[pl.BlockSpec((tk, BD_CH), lambda j, kk: (kk, j % 4)), pl.BlockSpec((tk, BD_ST), lambda j, kk: (kk, j))],
                  out_specs=pl.BlockSpec((None, BD_CH, BD_ST), lambda j, kk: (j, 0, 0)),
                  out_shape=jax.ShapeDtypeStruct((BD_TILES, BD_CH, BD_ST), F32), compiler_params=_params())(a, x)


def _permute_rows(t):
    n = t.shape[0]
    return t.reshape(SEG, n // SEG, t.shape[1]).transpose(1, 0, 2).reshape(n, t.shape[1])


def _unpermute_rows(t):
    n = t.shape[0]
    return t.reshape(n // SEG, SEG, t.shape[1]).transpose(1, 0, 2).reshape(n, t.shape[1])


def _segment_powers(ar, ai, seg_steps):
    pr, pi = ar.reshape(1, S5_GP), ai.reshape(1, S5_GP)
    e = 1
    while e < seg_steps:
        pr, pi = pr * pr - pi * pi, 2.0 * pr * pi
        e *= 2
    assert e == seg_steps, "segment length must be a power of two"
    rows_r, rows_i = [], []
    for _ in range(3):
        rows_r.append(pr)
        rows_i.append(pi)
        pr, pi = pr * pr - pi * pi, 2.0 * pr * pi
    pad = jnp.zeros((SEG - 3, S5_GP), F32)
    return jnp.concatenate(rows_r + [pad], axis=0), jnp.concatenate(rows_i + [pad], axis=0)


NT = (((1,), (1,)), ((), ()))
TN = (((0,), (0,)), ((), ()))


def _dot(a, b, dims=None, exact=False):
    dims = (((1,), (0,)), ((), ())) if dims is None else dims
    if exact:
        return lax.dot_general(a, b, dims, precision=HI, preferred_element_type=F32)
    return lax.dot_general(a.astype(BF16), b.astype(BF16), dims, preferred_element_type=F32)


def _dot01(a, b, dims=None, ones_first=True):
    x = b if ones_first else a
    hi = x.astype(BF16)
    lo = (x - hi.astype(F32)).astype(BF16)
    parts = [(_dot(a, p, dims) if ones_first else _dot(p, b, dims)) for p in (lo, hi)]
    return parts[0] + parts[1]


HEADS = range(4)


def _gla_chunk_fwd(qc, kc, vc, al, wup, bup, s_prev, tril):
    ones = jnp.ones((GLA_CHUNK, GLA_DV), F32)
    z = [_dot(al, wup[h]) + bup[h] for h in HEADS]
    la = [(jnp.minimum(z[h], 0.0) - jnp.log(1.0 + jnp.exp(-jnp.abs(z[h])))) * (1.0 / GLA_TAU) for h in HEADS]
    bc = [_dot01(tril, la[h]) for h in HEADS]
    blb = [_dot01(la[h], ones, TN, ones_first=False) for h in HEADS]
    bl = [bc[h][GLA_CHUNK - 1:GLA_CHUNK, :] for h in HEADS]
    ebc = [jnp.exp(bc[h]) for h in HEADS]
    qt = [qc[h] * (GLA_DK ** -0.5) * ebc[h] for h in HEADS]
    kt = [kc[h] * jnp.exp(-bc[h]) for h in HEADS]
    ke = [kc[h] * jnp.exp(bl[h] - bc[h]) for h in HEADS]
    sc = [_dot(qt[h], kt[h], NT) * tril for h in HEADS]
    oi = [_dot(sc[h], vc[h]) for h in HEADS]
    oo = [_dot(qt[h], s_prev[h]) for h in HEADS]
    o = [oi[h] + oo[h] for h in HEADS]
    return z, bc, bl, blb, ebc, qt, kt, ke, sc, o


GLA_ROWS = 512
GLA_CPB = GLA_ROWS // GLA_CHUNK


ZA_COLS = 5 * 512
SLOT = 128


def _pad_heads(w):
    r = w.shape[0]
    return jnp.pad(w.reshape(r, GLA_HEADS, GLA_DK), ((0, 0), (0, 0), (0, SLOT - GLA_DK))).reshape(r, GLA_HEADS * SLOT)


def _unpad_heads(w):
    r = w.shape[0]
    return w.reshape(r, GLA_HEADS, SLOT)[:, :, :GLA_DK].reshape(r, GLA_HEADS * GLA_DK)


def _gla_token_specs(blk):
    col = lambda cb: pl.BlockSpec((GLA_ROWS, 512), lambda j: (blk(j), cb))
    whole = lambda a: pl.BlockSpec(a.shape, lambda j: (0,) * a.ndim)
    return col, whole


def _head_ds(h, width):
    return pl.ds(h * SLOT, width)


def _tri(lower):
    ri = lax.broadcasted_iota(jnp.int32, (GLA_CHUNK, GLA_CHUNK), 0)
    ci = lax.broadcasted_iota(jnp.int32, (GLA_CHUNK, GLA_CHUNK), 1)
    return ((ri >= ci) if lower else (ri <= ci)).astype(F32)


def _gla_fwd(za, al, wup, bup, gn, name):
    n = za.shape[0]
    nc = n // GLA_CHUNK

    def body(q_ref, k_ref, v_ref, r_ref, al_ref, wup_ref, bup_ref, gn_ref, y_ref, sp_ref, s_ref):
        @pl.when(pl.program_id(0) == 0)
        def _():
            s_ref[...] = jnp.zeros_like(s_ref)

        tril = _tri(True)

        def chunk(c, carry):
            rows = pl.ds(pl.multiple_of(c * GLA_CHUNK, GLA_CHUNK), GLA_CHUNK)
            alc = al_ref[rows, :]
            vc = [v_ref[rows, _head_ds(h, GLA_DV)] for h in HEADS]
            s_prev = [s_ref[h] for h in HEADS]
            _, _, _, blb, _, _, _, ke, _, o = _gla_chunk_fwd(
                [q_ref[rows, _head_ds(h, GLA_DK)] for h in HEADS], [k_ref[rows, _head_ds(h, GLA_DK)] for h in HEADS],
                vc, alc, [wup_ref[h] for h in HEADS], [bup_ref[h] for h in HEADS], s_prev, tril)
            ds = [_dot(ke[h], vc[h], TN) for h in HEADS]
            for h in HEADS:
                rc = r_ref[rows, _head_ds(h, GLA_DV)]
                sp_ref[h, c] = s_prev[h]
                rstd = lax.rsqrt(jnp.mean(o[h] * o[h], axis=-1, keepdims=True) + EPS)
                y_ref[rows, _head_ds(h, GLA_DV)] = (o[h] * rstd * gn_ref[h] * (rc * _sigmoid(rc))).astype(BF16)
                s_ref[h] = jnp.exp(blb[h]) * s_prev[h] + ds[h]
            return carry

        lax.fori_loop(0, GLA_CPB, chunk, 0)

    col, whole = _gla_token_specs(lambda j: j)
    return _pcall(body, name=name, grid=(n // GLA_ROWS,),
                  in_specs=[col(1), col(2), col(3), col(4), pl.BlockSpec((GLA_ROWS, LANE), lambda j: (j, 0)),
                            whole(wup), whole(bup), whole(gn)],
                  out_specs=[pl.BlockSpec((GLA_ROWS, GLA_HEADS * GLA_DV), lambda j: (j, 0)),
                             pl.BlockSpec((GLA_HEADS, GLA_CPB, GLA_DK, GLA_DV), lambda j: (0, j, 0, 0))],
                  out_shape=[jax.ShapeDtypeStruct((n, GLA_HEADS * GLA_DV), BF16),
                             jax.ShapeDtypeStruct((GLA_HEADS, nc, GLA_DK, GLA_DV), F32)],
                  scratch_shapes=[pltpu.VMEM((GLA_HEADS, GLA_DK, GLA_DV), F32)],
                  compiler_params=_params())(za, za, za, za, al, wup, bup, gn)


def _gla_bwd(za, al, wup, bup, gn, sp, dy, du_s5, name):
    n = za.shape[0]
    nb = n // GLA_ROWS

    def body(q_ref, k_ref, v_ref, r_ref, al_ref, wup_ref, bup_ref, gn_ref, dy_ref, dus_ref, sp_ref,
             dza_ref, dz_ref, dgn_ref, dbup_ref, ds_ref):
        @pl.when(pl.program_id(0) == 0)
        def _():
            ds_ref[...] = jnp.zeros_like(ds_ref)
            dgn_ref[...] = jnp.zeros_like(dgn_ref)
            dbup_ref[...] = jnp.zeros_like(dbup_ref)

        tril, triu = _tri(True), _tri(False)
        dza_ref[:, 0:512] = dus_ref[...].astype(BF16)
        dza_ref[:, 512:1536] = jnp.zeros((GLA_ROWS, 1024), BF16)
        dz_ref[...] = jnp.zeros_like(dz_ref)

        def chunk(i, carry):
            c = GLA_CPB - 1 - i
            rows = pl.ds(pl.multiple_of(c * GLA_CHUNK, GLA_CHUNK), GLA_CHUNK)
            alc = al_ref[rows, :]
            qc = [q_ref[rows, _head_ds(h, GLA_DK)] for h in HEADS]
            kc = [k_ref[rows, _head_ds(h, GLA_DK)] for h in HEADS]
            vc = [v_ref[rows, _head_ds(h, GLA_DV)] for h in HEADS]
            s_prev = [sp_ref[h, c] for h in HEADS]
            ds = [ds_ref[h] for h in HEADS]
            z, bc, bl, blb, ebc, qt, kt, ke, sc, o = _gla_chunk_fwd(
                qc, kc, vc, alc, [wup_ref[h] for h in HEADS], [bup_ref[h] for h in HEADS], s_prev, tril)
            do = []
            for h in HEADS:
                rc = r_ref[rows, _head_ds(h, GLA_DV)]
                rs = lax.rsqrt(jnp.mean(o[h] * o[h], axis=-1, keepdims=True) + EPS)
                on = o[h] * rs
                sr = _sigmoid(rc)
                sil = rc * sr
                dyv, gnv = dy_ref[rows, _head_ds(h, GLA_DV)], gn_ref[h]
                dgn_ref[h] += jnp.sum(dyv * on * sil, axis=0, keepdims=True)
                dza_ref[rows, pl.ds(2048 + h * SLOT, GLA_DV)] = (dyv * on * gnv * (sr * (1.0 + rc * (1.0 - sr)))).astype(BF16)
                don = dyv * gnv * sil
                do.append(rs * (don - on * jnp.mean(don * on, axis=-1, keepdims=True)))
            dp = [_dot(do[h], vc[h], NT) * tril for h in HEADS]
            dv1 = [_dot(sc[h], do[h], TN) for h in HEADS]
            dv2 = [_dot(ke[h], ds[h]) for h in HEADS]
            dq2 = [_dot(do[h], s_prev[h], NT) for h in HEADS]
            dke = [_dot(vc[h], ds[h], NT) for h in HEADS]
            ddec = [_dot01(jnp.ones((8, GLA_DV), F32), ds[h] * s_prev[h], NT)[0:1, :] for h in HEADS]
            dsn = [_dot(qt[h], do[h], TN) for h in HEADS]
            dq1 = [_dot(dp[h], kt[h]) for h in HEADS]
            dkt = [_dot(dp[h], qt[h], TN) for h in HEADS]
            dbc, dbl = [], []
            for h in HEADS:
                dqt = dq1[h] + dq2[h]
                dza_ref[rows, pl.ds(1536 + h * SLOT, GLA_DV)] = (dv1[h] + dv2[h]).astype(BF16)
                ds_ref[h] = jnp.exp(blb[h]) * ds[h] + dsn[h]
                dza_ref[rows, pl.ds(512 + h * SLOT, GLA_DK)] = (dqt * (GLA_DK ** -0.5) * ebc[h]).astype(BF16)
                dza_ref[rows, pl.ds(1024 + h * SLOT, GLA_DK)] = (dkt[h] * jnp.exp(-bc[h])
                                                                 + dke[h] * jnp.exp(bl[h] - bc[h])).astype(BF16)
                dbc.append(dqt * qt[h] - dkt[h] * kt[h] - dke[h] * ke[h])
                dbl.append(jnp.sum(dke[h] * ke[h], axis=0, keepdims=True) + ddec[h] * jnp.exp(bl[h]))
            dla = [_dot01(triu, dbc[h]) + dbl[h] for h in HEADS]
            for h in HEADS:
                dz = dla[h] * (1.0 - _sigmoid(z[h])) * (1.0 / GLA_TAU)
                dz_ref[rows, _head_ds(h, GLA_DK)] = dz
                dbup_ref[h] += jnp.sum(dz, axis=0, keepdims=True)
            return carry

        lax.fori_loop(0, GLA_CPB, chunk, 0)

    rev = lambda j: nb - 1 - j
    col, whole = _gla_token_specs(rev)
    tok = lambda w: pl.BlockSpec((GLA_ROWS, w), lambda j: (rev(j), 0))
    h1 = lambda w: pl.BlockSpec((GLA_HEADS, 1, w), lambda j: (0, 0, 0))
    s1 = lambda w: jax.ShapeDtypeStruct((GLA_HEADS, 1, w), F32)
    return _pcall(body, name=name, grid=(nb,),
                  in_specs=[col(1), col(2), col(3), col(4), tok(LANE), whole(wup), whole(bup), whole(gn), tok(512), tok(512),
                            pl.BlockSpec((GLA_HEADS, GLA_CPB, GLA_DK, GLA_DV), lambda j: (0, rev(j), 0, 0))],
                  out_specs=[tok(ZA_COLS), tok(GLA_HEADS * SLOT), h1(GLA_DV), h1(GLA_DK)],
                  out_shape=[jax.ShapeDtypeStruct((n, ZA_COLS), BF16), jax.ShapeDtypeStruct((n, GLA_HEADS * SLOT), F32),
                             s1(GLA_DV), s1(GLA_DK)],
                  scratch_shapes=[pltpu.VMEM((GLA_HEADS, GLA_DK, GLA_DV), F32)],
                  compiler_params=_params())(za, za, za, za, al, wup, bup, gn, dy, du_s5, sp)


ANY = pl.BlockSpec(memory_space=pl.ANY)


def _place():
    x, y, c = lax.axis_index("x"), lax.axis_index("y"), lax.axis_index("c")
    chips = [(1 - x, y), (x, 1 - y), (1 - x, 1 - y)]
    return x, y, c, chips


def _remote(src, dst, ssem, rsem, dev):
    return pltpu.make_async_remote_copy(src_ref=src, dst_ref=dst, send_sem=ssem, recv_sem=rsem, device_id=dev,
                                        device_id_type=MESH_ID)


def _half(c, rows):
    h = rows // 2
    return pl.ds(pl.multiple_of(c * h, 8), h)


def _side_gather_ici(shards):
    def copies(ins, outs, ssem, rsem):
        x, y, c, chips = _place()
        mine = 2 * x + y
        cps = []
        for w in range(len(ins)):
            half = _half(c, ins[w].shape[0])
            cps.append(_remote(ins[w], outs[w].at[mine], ssem.at[4 * w], rsem.at[4 * w], (x, y, 1 - c)))
            for k, (px, py) in enumerate(chips):
                cps.append(_remote(ins[w].at[half], outs[w].at[mine, half], ssem.at[4 * w + 1 + k], rsem.at[4 * w + 1 + k],
                                   (px, py, c)))
        return cps

    return _Side(shards, [jax.ShapeDtypeStruct((4,) + s.shape, s.dtype) for s in shards], 4 * len(shards), copies)


def _side_gather_d2d(gathered):
    def copies(ins, outs, ssem, rsem):
        x, y, c, chips = _place()
        cps = []
        for w in range(len(outs)):
            half = _half(c, outs[w].shape[1])
            for k, (px, py) in enumerate(chips):
                theirs = outs[w].at[2 * px + py, half]
                cps.append(_remote(theirs, theirs, ssem.at[3 * w + k], rsem.at[3 * w + k], (x, y, 1 - c)))
        return cps

    return _Side(gathered, [jax.ShapeDtypeStruct(g.shape, g.dtype) for g in gathered], 3 * len(gathered), copies,
                 aliased=True)


def _side_swap_halves(grads):
    def copies(ins, outs, ssem, rsem):
        x, y, c, _ = _place()
        return [_remote(ins[w].at[:, _half(1 - c, ins[w].shape[1]), :], outs[w], ssem.at[w], rsem.at[w], (x, y, 1 - c))
                for w in range(len(ins))]

    return _Side(grads, [jax.ShapeDtypeStruct((4, g.shape[1] // 2, g.shape[2]), g.dtype) for g in grads], len(grads), copies)


def _side_scatter(sums):
    def copies(ins, outs, ssem, rsem):
        x, y, c, chips = _place()
        return [_remote(ins[w].at[2 * px + py], outs[w].at[k], ssem.at[3 * w + k], rsem.at[3 * w + k], (px, py, c))
                for w in range(len(ins)) for k, (px, py) in enumerate(chips)]

    return _Side(sums, [jax.ShapeDtypeStruct((3,) + s.shape[1:], s.dtype) for s in sums], 3 * len(sums), copies)


def _side_swap_reduced(halves):
    def copies(ins, outs, ssem, rsem):
        x, y, c, _ = _place()
        return [_remote(ins[w], outs[w], ssem.at[w], rsem.at[w], (x, y, 1 - c)) for w in range(len(ins))]

    return _Side(halves, [jax.ShapeDtypeStruct(h.shape, h.dtype) for h in halves], len(halves), copies)


SUM_BLOCKS = 2


def _chip_sums(gs, recvs, c_arr, name):
    n = len(gs)

    def body(c_ref, *refs):
        for k in range(n):
            refs[2 * n + k][...] = (refs[2 * k][...] + refs[2 * k + 1][...]).astype(BF16)

    in_specs, out_specs, out_shape, args = [], [], [], []
    for g, recv in zip(gs, recvs):
        _, r, cols = g.shape
        h = r // 2
        assert h % (16 * SUM_BLOCKS) == 0, g.shape
        tr = h // SUM_BLOCKS
        in_specs += [pl.BlockSpec((None, None, tr, cols), lambda s, i, c_ref: (s, c_ref[0], i, 0)),
                     pl.BlockSpec((None, tr, cols), lambda s, i, c_ref: (s, i, 0))]
        out_specs.append(pl.BlockSpec((None, tr, cols), lambda s, i, c_ref: (s, i, 0)))
        out_shape.append(jax.ShapeDtypeStruct((4, h, cols), BF16))
        args += [g.reshape(4, 2, h, cols), recv]
    grid_spec = pltpu.PrefetchScalarGridSpec(num_scalar_prefetch=1, grid=(4, SUM_BLOCKS), in_specs=in_specs,
                                             out_specs=out_specs)
    return _pcall(body, name=name, grid_spec=grid_spec, out_shape=out_shape, compiler_params=_params())(c_arr, *args)


def _owner_sums(sums, others, s_arr, name):
    n = len(sums)

    def body(s_ref, *refs):
        f = lambda v: v.astype(F32)
        for k in range(n):
            a_ref, o_ref = refs[2 * k], refs[2 * k + 1]
            refs[2 * n + k][...] = (f(a_ref[...]) + f(o_ref[0])) + (f(o_ref[1]) + f(o_ref[2]))

    in_specs, out_specs, out_shape, args = [], [], [], []
    for sm, ot in zip(sums, others):
        _, h, cols = sm.shape
        tr = h // SUM_BLOCKS
        in_specs += [pl.BlockSpec((None, tr, cols), lambda i, s_ref: (s_ref[0], i, 0)),
                     pl.BlockSpec((3, tr, cols), lambda i, s_ref: (0, i, 0))]
        out_specs.append(pl.BlockSpec((tr, cols), lambda i, s_ref: (i, 0)))
        out_shape.append(jax.ShapeDtypeStruct((h, cols), F32))
        args += [sm, ot]
    grid_spec = pltpu.PrefetchScalarGridSpec(num_scalar_prefetch=1, grid=(SUM_BLOCKS,), in_specs=in_specs,
                                             out_specs=out_specs)
    return _pcall(body, name=name, grid_spec=grid_spec, out_shape=out_shape, compiler_params=_params())(s_arr, *args)


def _side_small_sibling(v):
    def copies(ins, outs, ssem, rsem):
        x, y, c, _ = _place()
        return [_remote(ins[0], outs[0], ssem.at[0], rsem.at[0], (x, y, 1 - c))]

    return _Side([v], [jax.ShapeDtypeStruct(v.shape, F32)], 1, copies)


def _side_small_chips(v):
    def copies(ins, outs, ssem, rsem):
        x, y, c, chips = _place()
        return [_remote(ins[0], outs[0].at[k], ssem.at[k], rsem.at[k], (px, py, c)) for k, (px, py) in enumerate(chips)]

    return _Side([v], [jax.ShapeDtypeStruct((3,) + v.shape, F32)], 3, copies)


def _small_add(v, r, name):
    def body(v_ref, r_ref, o_ref):
        if r.ndim == 2:
            o_ref[...] = v_ref[...] + r_ref[...]
        else:
            o_ref[...] = (v_ref[...] + r_ref[0]) + (r_ref[1] + r_ref[2])

    vm = pl.BlockSpec(memory_space=pltpu.VMEM)
    return _pcall(body, name=name, in_specs=[vm, vm], out_specs=vm, out_shape=jax.ShapeDtypeStruct(v.shape, F32),
                  compiler_params=_params())(v, r)


def _merge_sides(sides):
    if len(sides) == 1:
        return sides[0]

    def copies(in_refs, out_refs, ssem, rsem):
        cps, i, o, q = [], 0, 0, 0
        for s in sides:
            ni, no = len(s.ins), len(s.out_shapes)
            cps += s.copies(in_refs[i:i + ni], out_refs[o:o + no], ssem.at[pl.ds(q, s.nsem)], rsem.at[pl.ds(q, s.nsem)])
            i, o, q = i + ni, o + no, q + s.nsem
        return cps

    assert not any(s.aliased for s in sides)
    return _Side(sum((s.ins for s in sides), []), sum((s.out_shapes for s in sides), []), sum(s.nsem for s in sides), copies)


def _tile_rows(size):
    return -(-size // (8 * LANE)) * 8


def _pack_small(parts):
    pieces = []
    for p in parts:
        flat = p.reshape(-1).astype(F32)
        pieces.append(jnp.pad(flat, (0, _tile_rows(p.size) * LANE - p.size)).reshape(-1, LANE))
    rows = sum(x.shape[0] for x in pieces)
    pieces.append(jnp.zeros(((-rows) % 64, LANE), F32))
    return jnp.concatenate(pieces, axis=0)


def _unpack_small(packed, like):
    out, pos = [], 0
    for p in like:
        rows = _tile_rows(p.size)
        out.append(packed[pos:pos + rows].reshape(-1)[:p.size].reshape(p.shape))
        pos += rows
    return out


FFN_FWD_ROWS, FFN_BWD_ROWS = 1024, 512
FFN_SUB_ROWS = 256


def _ffn_specs(n, d, fs, cap):
    rows = _pick(n, cap, 16)
    row = pl.BlockSpec((rows, d), lambda i, s: (i, 0))
    gain = pl.BlockSpec((1, d), lambda i, s: (0, 0))
    w_row = pl.BlockSpec((None, fs, d), lambda i, s: (s, 0, 0))
    hid = pl.BlockSpec((None, rows, fs), lambda i, s: (s, i, 0))
    return rows, row, gain, w_row, hid


def _ffn_fwd(h, g, w1t, w3t, w2, tag, plan):
    n, d = h.shape
    ns, fs, _ = w2.shape
    rows, row, gain, w_row, hid = _ffn_specs(n, d, fs, FFN_FWD_ROWS)
    sub = rows

    def body(h_ref, g_ref, w1_ref, w3_ref, w2_ref, out_ref, n1_ref, a_ref, b_ref, hm_ref, acc_ref):
        s = pl.program_id(1)

        @pl.when(s == 0)
        def _():
            xv = h_ref[...]
            rstd = lax.rsqrt(jnp.mean(xv * xv, axis=-1, keepdims=True) + EPS)
            n1_ref[...] = (xv * rstd * g_ref[...]).astype(BF16)
            acc_ref[...] = jnp.zeros_like(acc_ref)

        def up(j):
            n1 = n1_ref[j * sub:(j + 1) * sub, :]
            return _dot(n1, w1_ref[...], NT), _dot(n1, w3_ref[...], NT)

        cur = up(0)
        for j in range(rows // sub):
            nxt = up(j + 1) if (j + 1) * sub < rows else None
            a, b = cur
            r = slice(j * sub, (j + 1) * sub)
            hm = (a * _sigmoid(a) * b).astype(BF16)
            a_ref[r, :] = a.astype(BF16)
            b_ref[r, :] = b.astype(BF16)
            hm_ref[r, :] = hm
            acc_ref[r, :] += _dot(hm, w2_ref[...])
            cur = nxt

        @pl.when(s == ns - 1)
        def _():
            out_ref[...] = h_ref[...] + 0.5 * acc_ref[...]

    hid_shape = jax.ShapeDtypeStruct((ns, n, fs), BF16)
    plan.before(f"{tag}_fwd")
    out, n1, a, b, hm = _pcall(
        body, name=f"{tag}_fwd", grid=(n // rows, ns), in_specs=[row, gain, w_row, w_row, w_row],
        out_specs=[row, row, hid, hid, hid],
        out_shape=[jax.ShapeDtypeStruct((n, d), F32), jax.ShapeDtypeStruct((n, d), BF16), hid_shape, hid_shape, hid_shape],
        scratch_shapes=[pltpu.VMEM((rows, d), F32)], compiler_params=_params())(h, g, w1t, w3t, w2)
    plan.after(f"{tag}_fwd")
    return out, (h, n1, a, b, hm)


def _wgrad(a3, b, name, alpha=1.0):
    ns, n, fs = a3.shape
    d = b.shape[1]
    tk = _pick(n, 1024, 16)

    def body(a_ref, b_ref, o_ref):
        @pl.when(pl.program_id(0) == 0)
        def _():
            o_ref[...] = jnp.zeros_like(o_ref)

        bv = b_ref[...].astype(BF16)
        for s in range(ns):
            part = _dot(a_ref[s], bv, TN)
            o_ref[s] += part if alpha == 1.0 else alpha * part

    return _pcall(body, name=name, grid=(n // tk,),
                  in_specs=[pl.BlockSpec((ns, tk, fs), lambda k: (0, k, 0)), pl.BlockSpec((tk, d), lambda k: (k, 0))],
                  out_specs=pl.BlockSpec((ns, fs, d), lambda k: (0, 0, 0)),
                  out_shape=jax.ShapeDtypeStruct((ns, fs, d), F32), compiler_params=_params())(a3, b)


def _ffn_bwd(dout, saved, g, w1, w3, w2, tag, plan):
    h, n1, a, b, hm = saved
    n, d = h.shape
    ns, fs, _ = w2.shape
    rows, row, gain, w_row, hid = _ffn_specs(n, d, fs, FFN_BWD_ROWS)
    sub = _pick(rows, FFN_SUB_ROWS, 16)

    def body(do_ref, h_ref, g_ref, a_ref, b_ref, w1_ref, w3_ref, w2_ref, dh_ref, da_ref, db_ref, dg_ref, acc_ref):
        i, s = pl.program_id(0), pl.program_id(1)

        @pl.when(s == 0)
        def _():
            acc_ref[...] = jnp.zeros_like(acc_ref)

        @pl.when((s == 0) & (i == 0))
        def _():
            dg_ref[...] = jnp.zeros_like(dg_ref)

        def up(j):
            return _dot(0.5 * do_ref[j * sub:(j + 1) * sub, :], w2_ref[...], NT)

        cur = up(0)
        for j in range(rows // sub):
            nxt = up(j + 1) if (j + 1) * sub < rows else None
            r = slice(j * sub, (j + 1) * sub)
            av, bv = a_ref[r, :].astype(F32), b_ref[r, :].astype(F32)
            sg = _sigmoid(av)
            da = (cur * bv * (sg * (1.0 + av * (1.0 - sg)))).astype(BF16)
            db = (cur * av * sg).astype(BF16)
            da_ref[r, :] = da
            db_ref[r, :] = db
            acc_ref[r, :] += _dot(da, w1_ref[...]) + _dot(db, w3_ref[...])
            cur = nxt

        @pl.when(s == ns - 1)
        def _():
            xv, dn = h_ref[...], acc_ref[...]
            rstd = lax.rsqrt(jnp.mean(xv * xv, axis=-1, keepdims=True) + EPS)
            xh = xv * rstd
            dg_ref[...] += jnp.sum(dn * xh, axis=0, keepdims=True)
            dxh = dn * g_ref[...]
            dh_ref[...] = do_ref[...] + rstd * (dxh - xh * jnp.mean(dxh * xh, axis=-1, keepdims=True))

    hid_shape = jax.ShapeDtypeStruct((ns, n, fs), BF16)
    plan.before(f"{tag}_bwd")
    dh, da, db, dg = _pcall(
        body, name=f"{tag}_bwd", grid=(n // rows, ns), in_specs=[row, row, gain, hid, hid, w_row, w_row, w_row],
        out_specs=[row, hid, hid, gain],
        out_shape=[jax.ShapeDtypeStruct((n, d), F32), hid_shape, hid_shape, jax.ShapeDtypeStruct((1, d), F32)],
        scratch_shapes=[pltpu.VMEM((rows, d), F32)], compiler_params=_params())(dout, h, g, a, b, w1, w3, w2)
    plan.after(f"{tag}_bwd")
    plan.grads[f"{tag}_norm"] = dg
    plan.before(f"{tag}_gw2")
    gw2 = _wgrad(hm, dout, f"{tag}_gw2", alpha=0.5)
    plan.after(f"{tag}_gw2")
    plan.grads[f"{tag}_w2"] = gw2
    plan.before(f"{tag}_gw1")
    gw1 = _wgrad(da, n1, f"{tag}_gw1")
    plan.after(f"{tag}_gw1")
    plan.grads[f"{tag}_w1"] = gw1
    plan.before(f"{tag}_gw3")
    gw3 = _wgrad(db, n1, f"{tag}_gw3")
    plan.after(f"{tag}_gw3")
    return dh, dg, gw1, gw3, gw2


def _local_step(x, tgt, plan):
    n = x.shape[0]
    grads = plan.grads

    def f(name):
        w = plan.get(name)
        return w.reshape(1, D_MODEL) if name.endswith('_norm') and name != 'gla_out_norm' else w

    def carried(tag, fn, *args, **kw):
        plan.before(tag)
        out = fn(*args, **kw)
        plan.after(tag)
        return out

    h1, ffn1 = _ffn_fwd(x, f('ffn1_norm'), f('ffn1_w1'), f('ffn1_w3'), f('ffn1_w2'), "ffn1", plan)
    u = carried("mix_rms", _rms_fwd, h1, f('mix_norm'), "mix_rms")
    w_in = f('w_in')
    w_a = jnp.concatenate([w_in[:, :512], _pad_heads(w_in[:, 512:768]), _pad_heads(w_in[:, 768:1024]), w_in[:, 1024:2048]],
                          axis=1)
    w_al = jnp.pad(w_in[:, 2048:2048 + GLA_RANK], ((0, 0), (0, LANE - GLA_RANK)))
    w_g = w_in[:, 2048 + GLA_RANK:]
    za = carried("in_a", _mm, u, w_a, name="in_a")
    zg = carried("in_g", _mm, u, w_g, name="in_g")
    al = _mm(u, w_al, name="in_al")
    ar, ai, bbar_re, bbar_im = _s5_discretize(f('s5_lambda_re'), f('s5_lambda_im'), f('s5_log_dt'), f('s5_b_re'), f('s5_b_im'))
    t_b = _bd_tiles(bbar_re.transpose(0, 2, 1), bbar_im.transpose(0, 2, 1)).astype(BF16)
    t_c = _bd_tiles(f('s5_c_re'), -f('s5_c_im')).astype(BF16)
    ar8 = jnp.broadcast_to(ar.reshape(1, S5_GP), (SEG, S5_GP))
    ai8 = jnp.broadcast_to(ai.reshape(1, S5_GP), (SEG, S5_GP))
    pw_r, pw_i = _segment_powers(ar, ai, n // SEG)
    dskip = f('s5_d').reshape(1, S5_W)
    u_s5 = _permute_rows(za[:, :S5_W])
    xs = _s5_scan(u_s5, t_b, ar8, ai8, pw_r, pw_i, "s5_scan")
    ys_p = _bd_reduce(xs, t_c, _scale_rows(u_s5, dskip, "s5_skip"), "s5_y")
    ys = _unpermute_rows(ys_p)
    zgelu = _gelu_fwd(ys, "s5_gelu")
    t_glu = _mm(zgelu, f('s5_glu_w'), bias=f('s5_glu_b').reshape(1, S5_W), name="s5_glu_t")
    y_s5 = _glu_fwd(zgelu, t_glu, "s5_glu")
    wup = jnp.pad(f('gla_a_up_w'), ((0, LANE - GLA_RANK), (0, 0)))
    wup_h = wup.reshape(LANE, GLA_HEADS, GLA_DK).transpose(1, 0, 2)
    bup_h = f('gla_a_up_b').reshape(GLA_HEADS, 1, GLA_DK)
    gn_h = f('gla_out_norm').reshape(GLA_HEADS, 1, GLA_DV)
    y_gla, s_prev = carried("gla_fwd", _gla_fwd, za, al, wup_h, bup_h, gn_h, "gla_fwd")
    ps = _mm(y_s5, f('proj_s5'), name="proj_s5")
    pg = carried("proj_gla", _mm, y_gla, f('proj_gla'), name="proj_gla")
    merged = _merge_fwd(zg, ps, pg, "merge")
    h2 = _mm(merged, f('w_out'), res=h1, name="w_out")
    h3, ffn2 = _ffn_fwd(h2, f('ffn2_norm'), f('ffn2_w1'), f('ffn2_w3'), f('ffn2_w2'), "ffn2", plan)
    loss, dh3, g_final = _final_loss(h3, f('final_norm').reshape(1, D_MODEL), tgt, "loss")
    plan.loss = loss[0, 0]
    grads['final_norm'] = g_final.reshape(D_MODEL)
    dh2, grads['ffn2_norm'], grads['ffn2_w1'], grads['ffn2_w3'], grads['ffn2_w2'] = _ffn_bwd(
        dh3, ffn2, f('ffn2_norm'), f('ffn2_w1'), f('ffn2_w3'), f('ffn2_w2'), "ffn2", plan)
    dm = carried("d_merged", _mm, dh2, f('w_out'), tb=True, name="d_merged")
    grads['w_out'] = _mm(merged, dh2, ta=True, name="g_w_out")
    dps, dpg, dzg = carried("d_merge", _merge_bwd, dm, zg, ps, pg, "d_merge")
    grads['proj_s5'] = _mm(y_s5, dps, ta=True, name="g_proj_s5")
    grads['proj_gla'] = _mm(y_gla, dpg, ta=True, name="g_proj_gla")
    dy_s5 = _mm(dps, f('proj_s5'), tb=True, name="d_y_s5")
    dy_gla = _mm(dpg, f('proj_gla'), tb=True, name="d_y_gla")
    dzgelu, dt_glu, g_glu_b = _glu_bwd1(dy_s5, zgelu, t_glu, "d_glu")
    grads['s5_glu_b'] = g_glu_b.reshape(S5_W)
    grads['s5_glu_w'] = _mm(zgelu, dt_glu, ta=True, name="g_glu_w")
    dzgelu = _mm(dt_glu, f('s5_glu_w'), tb=True, res=dzgelu, name="d_gelu")
    dys, du_skip, g_d = _glu_bwd2(_permute_rows(dzgelu), ys_p, u_s5, dskip, "d_s5_y")
    grads['s5_d'] = g_d.reshape(S5_G, S5_H)
    lam, da8 = _s5_scan_bwd(dys, t_c, xs, ar8, ai8, pw_r, pw_i, "s5_scan_bwd")
    g_c = _bd_blocks(_bd_outer(dys, xs, "g_s5_c"))
    grads['s5_c_re'], grads['s5_c_im'] = g_c[0], -g_c[1]
    g_b = _bd_blocks(_bd_outer(u_s5, lam, "g_s5_b")).transpose(0, 1, 3, 2)
    g_bbar_re, g_bbar_im = g_b[0], g_b[1]
    da = jnp.sum(da8, axis=0)
    g_ar, g_ai = da[:S5_GP].reshape(S5_G, S5_P), da[S5_GP:].reshape(S5_G, S5_P)
    _, disc_vjp = jax.vjp(_s5_discretize, f('s5_lambda_re'), f('s5_lambda_im'), f('s5_log_dt'), f('s5_b_re'), f('s5_b_im'))
    (grads['s5_lambda_re'], grads['s5_lambda_im'], grads['s5_log_dt'], grads['s5_b_re'],
     grads['s5_b_im']) = disc_vjp((g_ar, g_ai, g_bbar_re, g_bbar_im))
    du_s5 = _unpermute_rows(_bd_reduce(lam, t_b, du_skip, "d_s5_u"))
    dza, dz, dgn, dbup = carried("gla_bwd", _gla_bwd, za, al, wup_h, bup_h, gn_h, s_prev, dy_gla, du_s5, "gla_bwd")
    grads['gla_out_norm'] = dgn.reshape(GLA_HEADS * GLA_DV)
    grads['gla_a_up_b'] = dbup.reshape(GLA_HEADS * GLA_DK)
    grads['gla_a_up_w'] = _unpad_heads(_mm(al, dz, ta=True, name="g_a_up")[:GLA_RANK])
    dal = _mm(dz, _pad_heads(wup), tb=True, out_dtype=BF16, name="d_a_low")
    g_wa = _mm(u, dza, ta=True, name="g_in_a")
    g_wg = _mm(u, dzg, ta=True, name="g_in_g")
    g_wal = _mm(u, dal, ta=True, name="g_in_al")
    grads['w_in'] = jnp.concatenate([g_wa[:, :512], _unpad_heads(g_wa[:, 512:1024]), _unpad_heads(g_wa[:, 1024:1536]),
                                     g_wa[:, 1536:], g_wal[:, :GLA_RANK], g_wg], axis=1)
    du = carried("d_u_a", _mm, dza, w_a, tb=True, name="d_u_a")
    du = _mm(dzg, w_g, tb=True, res=du, name="d_u_g")
    du = _mm(dal, w_al, tb=True, res=du, name="d_u_al")
    dh1, g_mix = carried("d_mix_rms", _rms_bwd, h1, f('mix_norm'), du, dh2, "d_mix_rms")
    grads['mix_norm'] = g_mix
    dx, grads['ffn1_norm'], grads['ffn1_w1'], grads['ffn1_w3'], grads['ffn1_w2'] = _ffn_bwd(
        dh1, ffn1, f('ffn1_norm'), f('ffn1_w1'), f('ffn1_w3'), f('ffn1_w2'), "ffn1", plan)
    return loss[0, 0], dx


MIXER_WEIGHTS = ['w_in', 's5_glu_w', 'proj_s5', 'proj_gla', 'w_out', 'gla_a_up_w']
FFN1_WEIGHTS, FFN2_WEIGHTS = FFN_WEIGHTS[:3], FFN_WEIGHTS[3:]
TRANSPOSED = ['ffn1_w1', 'ffn1_w3', 'ffn2_w1', 'ffn2_w3']


def _local_shard(w, nm):
    return jnp.swapaxes(w, 1, 2)[0] if nm in TRANSPOSED else w[0]
FFN1_EARLY = ['ffn1_w2']
FFN1_LATE = ['ffn1_w1', 'ffn1_w3']
GRAD_GROUPS = {'ffn2': FFN2_WEIGHTS, 'mixer': ['w_out', 'proj_s5', 'proj_gla', 's5_glu_w', 'w_in'], 'ffn1': FFN1_WEIGHTS}


class _Plan:
    def __init__(self, a, c_arr, s_arr):
        self.a, self.c_arr, self.s_arr = a, c_arr, s_arr
        self.grads, self.weights, self.riding = {}, {}, {}
        self.g4s, self.chip_sums, self.halves, self.sib_halves = {}, {}, {}, {}
        for nm in SMALL:
            if nm != 'gla_a_up_w':
                self.weights[nm] = a[nm] if nm == 'final_norm' else a[nm][0]
        ici = _side_gather_ici(self._shards(FFN1_WEIGHTS))
        _run_side(ici, "gather_ffn1_ici")
        self._gathered(FFN1_WEIGHTS, _run_side(_side_gather_d2d(ici.outs), "gather_ffn1_d2d"))

    def _shards(self, names):
        return [_local_shard(self.a[nm], nm).astype(F32 if nm == 'gla_a_up_w' else BF16) for nm in names]

    def _gathered(self, names, arrs):
        for nm, g4 in zip(names, arrs):
            if nm in FFN_WEIGHTS:
                self.weights[nm] = g4
            elif nm in COL_SHARDED:
                self.weights[nm] = jnp.concatenate([g4[s] for s in range(4)], axis=1)
            else:
                self.weights[nm] = g4.reshape(4 * g4.shape[1], g4.shape[2])

    def get(self, name):
        return self.weights[name]

    def _shard_major(self, nm):
        g = self.grads[nm]
        if nm in FFN_WEIGHTS:
            return g
        if nm in COL_SHARDED:
            return jnp.stack(jnp.split(g, 4, axis=1))
        return g.reshape(4, g.shape[0] // 4, g.shape[1])

    def _schedule(self, tag):
        grp = GRAD_GROUPS
        gathers = {"ffn1_fwd": ('ici', MIXER_WEIGHTS), "mix_rms": ('d2d', MIXER_WEIGHTS),
                   "in_a": ('ici', FFN2_WEIGHTS[:1]), "in_g": ('d2d', FFN2_WEIGHTS[:1]),
                   "gla_fwd": ('ici', FFN2_WEIGHTS[1:]), "proj_gla": ('d2d', FFN2_WEIGHTS[1:])}
        if tag in gathers:
            kind, names = gathers[tag]
            key = tuple(names)
            if kind == 'ici':
                return [(_side_gather_ici(self._shards(names)), lambda outs: self.riding.update({key: outs}))]
            return [(_side_gather_d2d(self.riding[key]), lambda outs: self._gathered(names, outs))]
        steps = {"ffn2_gw1": (['ffn2_w2'], 0), "ffn2_gw3": (['ffn2_w1'], 0), "d_merged": (['ffn2_w3'], 0),
                 "gla_bwd": (grp['ffn2'], 1), "d_mix_rms": (grp['ffn2'], 2),
                 "d_u_a": (grp['mixer'], 0), "ffn1_bwd": (grp['mixer'], 1), "ffn1_gw2": (grp['mixer'], 2),
                 "ffn1_gw1": (FFN1_EARLY, 0), "ffn1_gw3": (FFN1_EARLY, 1), "adamw_early": (FFN1_LATE, 1)}
        entries = [self._reduce_stage(*steps[tag])] if tag in steps else []
        if tag == "ffn1_gw2":
            entries.append(self._small_stage(0))
        if tag == "ffn1_gw1":
            entries.append(self._small_stage(1))
        return entries

    def _small_stage(self, stage):
        if stage == 0:
            a, grads = self.a, self.grads
            self.small_parts = ([grads[nm].reshape(a[nm].shape) for nm in SMALL if nm != 'gla_a_up_w']
                                + [grads['gla_a_up_w'], self.loss.reshape(1)])
            packed = _pack_small(self.small_parts)

            def done(outs):
                self.small_pair = _small_add(packed, outs[0], "small_sum_pair")
            return _side_small_sibling(packed), done

        def done(outs):
            self.small_total = _small_add(self.small_pair, outs[0], "small_sum_chips")
        return _side_small_chips(self.small_pair), done

    def _reduce_stage(self, names, stage):
        if stage == 0:
            for nm in names:
                self.g4s[nm] = self._shard_major(nm)

            def done(outs):
                sums = _chip_sums([self.g4s[nm] for nm in names], outs, self.c_arr, f"chip_sum_{names[0]}")
                self.chip_sums.update(zip(names, sums))
            return _side_swap_halves([self.g4s[nm] for nm in names]), done
        if stage == 1:
            def done(outs):
                halves = _owner_sums([self.chip_sums[nm] for nm in names], outs, self.s_arr, f"owner_sum_{names[0]}")
                self.halves.update(zip(names, halves))
            return _side_scatter([self.chip_sums[nm] for nm in names]), done

        def done(outs):
            self.sib_halves.update(zip(names, outs))
        return _side_swap_reduced([self.halves[nm] for nm in names]), done

    def before(self, tag):
        entries = self._schedule(tag)
        if entries:
            merged = _merge_sides([side for side, _ in entries])
            self.riding[tag] = (merged, entries)
            _RIDER.append(merged)

    def after(self, tag):
        if tag in self.riding:
            merged, entries = self.riding.pop(tag)
            assert not _RIDER and merged.outs is not None, tag
            pos = 0
            for side, done in entries:
                done(merged.outs[pos:pos + len(side.out_shapes)])
                pos += len(side.out_shapes)

    def finish_alone(self, stage):
        names = FFN1_LATE if stage == 0 else GRAD_GROUPS['ffn1']
        side, done = self._reduce_stage(names, stage)
        done(_run_side(side, f"grad_ffn1_stage{stage}"))


def _train_step(a):
    x = a['x'][0]
    tgt = a['loss_target'][0]
    xi, yi, ci = lax.axis_index("x"), lax.axis_index("y"), lax.axis_index("c")
    c_arr = jnp.reshape(ci, (1,)).astype(jnp.int32)
    s_arr = jnp.reshape(2 * xi + yi, (1,)).astype(jnp.int32)
    plan = _Plan(a, c_arr, s_arr)
    loss, dx = _local_step(x, tgt, plan)
    red = {}
    small_sum = _unpack_small(plan.small_total, plan.small_parts)
    small_names = [nm for nm in SMALL if nm != 'gla_a_up_w']
    for nm, g in zip(small_names, small_sum[:-2]):
        red[nm] = g
    loss = small_sum[-1].reshape(())
    g_up = small_sum[-2]
    red['gla_a_up_w'] = lax.dynamic_slice(g_up, (0, (2 * xi + yi) * GLA_DK), (GLA_RANK, GLA_DK))
    out_g, out_d, out_m, out_v = {}, {}, {}, {}

    def update(names, tag):
        items = [(_local_shard(a[nm], nm), plan.halves[nm], plan.sib_halves[nm], _local_shard(a['m_' + nm], nm),
                  _local_shard(a['v_' + nm], nm)) for nm in names]
        plan.before(tag)
        res = _adamw_group(items, c_arr, tag)
        plan.after(tag)
        for k, nm in enumerate(names):
            back = (lambda t: jnp.swapaxes(t[None], 1, 2)) if nm in TRANSPOSED else (lambda t: t[None])
            out_g[nm], out_d[nm], out_m[nm], out_v[nm] = (back(t) for t in res[4 * k:4 * k + 4])

    plan.finish_alone(0)
    update([nm for nm in SHARDED if nm not in GRAD_GROUPS['ffn1']], "adamw_early")
    plan.finish_alone(2)
    update(GRAD_GROUPS['ffn1'], "adamw_ffn1")
    rest = [nm for nm in WEIGHTS if nm not in SHARDED]
    pk = lambda pre: _pack_small([a[pre + nm] for nm in rest])
    d, nm_, nv_ = _adamw(pk(''), _pack_small([red[nm] for nm in rest]), pk('m_'), pk('v_'), "adamw_small")
    like = [a[nm] for nm in rest]
    for nm, g, dd, mm_, vv_ in zip(rest, [red[nm].reshape(a[nm].shape) for nm in rest], _unpack_small(d, like),
                                   _unpack_small(nm_, like), _unpack_small(nv_, like)):
        out_g[nm], out_d[nm], out_m[nm], out_v[nm] = g, dd, mm_, vv_
    return (loss, dx[None], *[out_g[nm] for nm in WEIGHTS], *[out_d[nm] for nm in WEIGHTS],
            *[out_m[nm] for nm in WEIGHTS], *[out_v[nm] for nm in WEIGHTS])


def kernel(x, ffn1_norm, ffn1_w1, ffn1_w3, ffn1_w2, mix_norm, w_in, s5_lambda_re, s5_lambda_im, s5_log_dt, s5_b_re, s5_b_im, s5_c_re, s5_c_im, s5_d, s5_glu_w, s5_glu_b, gla_a_up_w, gla_a_up_b, gla_out_norm, proj_s5, proj_gla, w_out, ffn2_norm, ffn2_w1, ffn2_w3, ffn2_w2, final_norm, loss_target, m_ffn1_norm, m_ffn1_w1, m_ffn1_w3, m_ffn1_w2, m_mix_norm, m_w_in, m_s5_lambda_re, m_s5_lambda_im, m_s5_log_dt, m_s5_b_re, m_s5_b_im, m_s5_c_re, m_s5_c_im, m_s5_d, m_s5_glu_w, m_s5_glu_b, m_gla_a_up_w, m_gla_a_up_b, m_gla_out_norm, m_proj_s5, m_proj_gla, m_w_out, m_ffn2_norm, m_ffn2_w1, m_ffn2_w3, m_ffn2_w2, m_final_norm, v_ffn1_norm, v_ffn1_w1, v_ffn1_w3, v_ffn1_w2, v_mix_norm, v_w_in, v_s5_lambda_re, v_s5_lambda_im, v_s5_log_dt, v_s5_b_re, v_s5_b_im, v_s5_c_re, v_s5_c_im, v_s5_d, v_s5_glu_w, v_s5_glu_b, v_gla_a_up_w, v_gla_a_up_b, v_gla_out_norm, v_proj_s5, v_proj_gla, v_w_out, v_ffn2_norm, v_ffn2_w1, v_ffn2_w3, v_ffn2_w2, v_final_norm):
    return _train_step(dict(locals()))
```

```python
import functools

import jax
import jax.numpy as jnp
from jax import lax
from jax.experimental import pallas as pl
from jax.experimental.pallas import tpu as pltpu

F32 = jnp.float32
BF16 = jnp.bfloat16
HI = lax.Precision.HIGHEST
MESH_ID = pl.DeviceIdType.MESH

D_MODEL = 1024
EPS = 1e-6
S5_G, S5_P, S5_H = 32, 64, 16
S5_W = S5_G * S5_H
S5_GP = S5_G * S5_P
SEG = 8
SCAN_ROWS = 256
GLA_HEADS, GLA_DK, GLA_DV = 4, 64, 128
GLA_CHUNK = 64
GLA_TAU = 16.0
GLA_RANK = 16
ADAM_LR, ADAM_B1, ADAM_B2, ADAM_EPS, ADAM_WD, ADAM_STEP = 0.001, 0.9, 0.999, 1e-08, 0.01, 10
V7X_VMEM_LIMIT = 56 * 1024 * 1024
LANE = 128

WEIGHTS = ['ffn1_norm', 'ffn1_w1', 'ffn1_w3', 'ffn1_w2', 'mix_norm', 'w_in', 's5_lambda_re', 's5_lambda_im',
           's5_log_dt', 's5_b_re', 's5_b_im', 's5_c_re', 's5_c_im', 's5_d', 's5_glu_w', 's5_glu_b', 'gla_a_up_w',
           'gla_a_up_b', 'gla_out_norm', 'proj_s5', 'proj_gla', 'w_out', 'ffn2_norm', 'ffn2_w1', 'ffn2_w3',
           'ffn2_w2', 'final_norm']
SHARDED = ['ffn1_w1', 'ffn1_w3', 'ffn1_w2', 'w_in', 's5_glu_w', 'proj_s5', 'proj_gla', 'w_out',
           'ffn2_w1', 'ffn2_w3', 'ffn2_w2']
COL_SHARDED = ['ffn1_w1', 'ffn1_w3', 'w_in', 'proj_s5', 'proj_gla', 'ffn2_w1', 'ffn2_w3', 'gla_a_up_w']
SMALL = [n for n in WEIGHTS if n not in SHARDED]
FFN_WEIGHTS = ['ffn1_w1', 'ffn1_w3', 'ffn1_w2', 'ffn2_w1', 'ffn2_w3', 'ffn2_w2']


def _params(**kw):
    return pltpu.CompilerParams(vmem_limit_bytes=V7X_VMEM_LIMIT, **kw)


class _Side:
    def __init__(self, ins, out_shapes, nsem, copies, aliased=False):
        self.ins, self.out_shapes, self.nsem, self.copies, self.aliased = list(ins), list(out_shapes), nsem, copies, aliased
        self.outs = None


_RIDER = []


def _pcall(body, **kw):
    if _RIDER:
        return _carry(body, _RIDER.pop(), **kw)
    return pl.pallas_call(body, **kw)


def _carry(body, side, *, name, grid, in_specs, out_specs, out_shape, scratch_shapes=(), compiler_params=None):
    del compiler_params
    single = not isinstance(out_shape, (list, tuple))
    out_specs = [out_specs] if single else list(out_specs)
    out_shape = [out_shape] if single else list(out_shape)
    n_in, n_out, n_scr = len(in_specs), len(out_shape), len(scratch_shapes)
    s_in, s_out = len(side.ins), len(side.out_shapes)
    any_spec = pl.BlockSpec(memory_space=pl.ANY)

    def wrapped(*refs):
        cuts = [n_in, s_in, n_out, s_out, n_scr]
        parts, pos = [], 0
        for c in cuts:
            parts.append(refs[pos:pos + c])
            pos += c
        ins, sins, outs, souts, scr = parts
        ssem, rsem = refs[pos], refs[pos + 1]
        first = last = None
        for d, g in enumerate(grid):
            i = pl.program_id(d)
            first = (i == 0) if first is None else first & (i == 0)
            last = (i == g - 1) if last is None else last & (i == g - 1)

        @pl.when(first)
        def _():
            for cp in side.copies(sins, souts, ssem, rsem):
                cp.start()

        body(*ins, *outs, *scr)

        @pl.when(last)
        def _():
            for cp in side.copies(sins, souts, ssem, rsem):
                cp.wait()

    call = pl.pallas_call(
        wrapped, name=name, grid=grid, in_specs=list(in_specs) + [any_spec] * s_in,
        out_specs=out_specs + [any_spec] * s_out, out_shape=out_shape + side.out_shapes,
        scratch_shapes=list(scratch_shapes) + [pltpu.SemaphoreType.DMA((side.nsem,)), pltpu.SemaphoreType.DMA((side.nsem,))],
        input_output_aliases={n_in + j: n_out + j for j in range(s_in)} if side.aliased else {},
        compiler_params=_params(has_side_effects=True))

    def run(*args):
        res = call(*args, *side.ins)
        side.outs = list(res[n_out:])
        return res[0] if single else list(res[:n_out])

    return run


def _run_side(side, name):
    s_in, s_out = len(side.ins), len(side.out_shapes)
    any_spec = pl.BlockSpec(memory_space=pl.ANY)

    def body(*refs):
        sins, souts = refs[:s_in], refs[s_in:s_in + s_out]
        ssem, rsem = refs[s_in + s_out:]
        cps = side.copies(sins, souts, ssem, rsem)
        for cp in cps:
            cp.start()
        for cp in cps:
            cp.wait()

    side.outs = list(pl.pallas_call(
        body, name=name, in_specs=[any_spec] * s_in, out_specs=[any_spec] * s_out, out_shape=side.out_shapes,
        scratch_shapes=[pltpu.SemaphoreType.DMA((side.nsem,)), pltpu.SemaphoreType.DMA((side.nsem,))],
        input_output_aliases={j: j for j in range(s_in)} if side.aliased else {},
        compiler_params=pltpu.CompilerParams(has_side_effects=True))(*side.ins))
    return side.outs


def _pick(n, cap, quantum):
    if n <= cap:
        return n
    best = None
    for t in range(quantum, cap + 1, quantum):
        if n % t == 0:
            best = t
    assert best is not None, (n, cap, quantum)
    return best


def _sigmoid(x):
    return jax.nn.sigmoid(x)


def _mm(a, b, *, name, ta=False, tb=False, out_dtype=F32, alpha=1.0, res=None, bias=None, exact=False, shard=None):
    ns = 4
    (k_a, m) = a.shape[-2:] if ta else a.shape[-2:][::-1]
    (k_b, n) = b.shape[-2:][::-1] if tb else b.shape[-2:]
    assert k_a == k_b, (a.shape, b.shape, ta, tb)
    assert (a.ndim == 3) == (shard in ('k', 'm')) and (b.ndim == 3) == (shard in ('n', 'k'))
    k = k_a
    tm = _pick(m, 1024, 128)
    tn = _pick(n, 1024, 128)
    tk = _pick(k, 1024, 128)
    pm, pn, pk = m // tm, n // tn, k // tk
    gm = pm * (ns if shard == 'm' else 1)
    gn = pn * (ns if shard == 'n' else 1)
    gk = pk * (ns if shard == 'k' else 1)
    dims = (((0,) if ta else (1,), (1,) if tb else (0,)), ((), ()))
    op_dtype = F32 if exact else BF16

    def body(*refs):
        a_ref, b_ref = refs[0], refs[1]
        pos = 2
        res_ref = bias_ref = None
        if res is not None:
            res_ref = refs[pos]
            pos += 1
        if bias is not None:
            bias_ref = refs[pos]
            pos += 1
        o_ref, acc_ref = refs[pos], refs[pos + 1]
        kk = pl.program_id(2)

        @pl.when(kk == 0)
        def _():
            acc_ref[...] = jnp.zeros_like(acc_ref)

        acc_ref[...] += lax.dot_general(a_ref[...].astype(op_dtype), b_ref[...].astype(op_dtype), dims,
                                        precision=HI if exact else None, preferred_element_type=F32)

        @pl.when(kk == gk - 1)
        def _():
            o = acc_ref[...]
            if alpha != 1.0:
                o = o * alpha
            if bias_ref is not None:
                o = o + bias_ref[...]
            if res_ref is not None:
                o = o + res_ref[...]
            o_ref[...] = o.astype(out_dtype)

    def spec(block, sharded_on, order):
        per = {'m': pm, 'n': pn, 'k': pk}

        def index(i, j, kk):
            g = {'m': i, 'n': j, 'k': kk}
            r, c = order(i % pm if shard == 'm' else i, j % pn if shard == 'n' else j, kk % pk if shard == 'k' else kk)
            if sharded_on is None:
                return (r, c)
            return (g[sharded_on] // per[sharded_on], r, c)

        return pl.BlockSpec(block if sharded_on is None else (None,) + block, index)

    a_sh = shard if shard in ('k', 'm') else None
    b_sh = shard if shard in ('n', 'k') else None
    o_sh = shard if shard in ('n', 'm') else None
    a_spec = spec((tk, tm), a_sh, lambda i, j, kk: (kk, i)) if ta else spec((tm, tk), a_sh, lambda i, j, kk: (i, kk))
    b_spec = spec((tn, tk), b_sh, lambda i, j, kk: (j, kk)) if tb else spec((tk, tn), b_sh, lambda i, j, kk: (kk, j))
    ins, in_specs = [a, b], [a_spec, b_spec]
    if res is not None:
        assert o_sh is None
        ins.append(res)
        in_specs.append(pl.BlockSpec((tm, tn), lambda i, j, kk: (i, j)))
    if bias is not None:
        assert o_sh is None
        ins.append(bias)
        in_specs.append(pl.BlockSpec((1, tn), lambda i, j, kk: (0, j)))
    out_shape = (m, n) if o_sh is None else (ns, m, n)
    return _pcall(body, name=name, grid=(gm, gn, gk), in_specs=in_specs,
                  out_specs=spec((tm, tn), o_sh, lambda i, j, kk: (i, j)),
                  out_shape=jax.ShapeDtypeStruct(out_shape, out_dtype),
                  scratch_shapes=[pltpu.VMEM((tm, tn), F32)], compiler_params=_params())(*ins)


ROWS_VMEM_BUDGET = 24 * 1024 * 1024


def _rows(body, ins, outs, *, n, name):
    cols = sum(a.shape[1] for a, kind in ins if kind == 'r') + sum(c for c, _, kind in outs if kind == 'r')
    cap = 256
    while cap < 2048 and 2 * 4 * cols * (2 * cap) <= ROWS_VMEM_BUDGET:
        cap *= 2
    tm = _pick(n, cap, 16)
    in_specs = []
    for arr, kind in ins:
        if kind == 'r':
            in_specs.append(pl.BlockSpec((tm, arr.shape[1]), lambda i: (i, 0)))
        else:
            in_specs.append(pl.BlockSpec(arr.shape, lambda i: (0, 0)))
    out_specs, out_shape = [], []
    for cols, dtype, kind in outs:
        if kind == 'r':
            out_specs.append(pl.BlockSpec((tm, cols), lambda i: (i, 0)))
            out_shape.append(jax.ShapeDtypeStruct((n, cols), dtype))
        else:
            out_specs.append(pl.BlockSpec((1, cols), lambda i: (0, 0)))
            out_shape.append(jax.ShapeDtypeStruct((1, cols), dtype))
    n_in = len(ins)
    acc_ids = [j for j, o in enumerate(outs) if o[2] == 'a']

    def wrapped(*refs):
        if acc_ids:
            @pl.when(pl.program_id(0) == 0)
            def _():
                for j in acc_ids:
                    refs[n_in + j][...] = jnp.zeros_like(refs[n_in + j])
        body(*refs)

    res = _pcall(wrapped, name=name, grid=(n // tm,), in_specs=in_specs, out_specs=out_specs, out_shape=out_shape,
                 compiler_params=_params())(*[a for a, _ in ins])
    return res


def _rms_fwd(x, g, name):
    def body(x_ref, g_ref, o_ref):
        xv = x_ref[...]
        rstd = lax.rsqrt(jnp.mean(xv * xv, axis=-1, keepdims=True) + EPS)
        o_ref[...] = (xv * rstd * g_ref[...]).astype(BF16)
    return _rows(body, [(x, 'r'), (g, 'f')], [(x.shape[1], BF16, 'r')], n=x.shape[0], name=name)[0]


def _rms_bwd(x, g, dn, dres, name):
    def body(x_ref, g_ref, dn_ref, dres_ref, dx_ref, dg_ref):
        xv = x_ref[...]
        rstd = lax.rsqrt(jnp.mean(xv * xv, axis=-1, keepdims=True) + EPS)
        xh = xv * rstd
        dn = dn_ref[...]
        dg_ref[...] += jnp.sum(dn * xh, axis=0, keepdims=True)
        dxh = dn * g_ref[...]
        dx_ref[...] = dres_ref[...] + rstd * (dxh - xh * jnp.mean(dxh * xh, axis=-1, keepdims=True))
    d = x.shape[1]
    return _rows(body, [(x, 'r'), (g, 'f'), (dn, 'r'), (dres, 'r')], [(d, F32, 'r'), (d, F32, 'a')],
                 n=x.shape[0], name=name)


def _gelu_parts(y):
    c0 = 0.7978845608028654
    inner = c0 * (y + 0.044715 * y * y * y)
    th = jnp.tanh(inner)
    return th, c0 * (1.0 + 3.0 * 0.044715 * y * y)


def _gelu_fwd(y, name):
    def body(y_ref, o_ref):
        yv = y_ref[...]
        th, _ = _gelu_parts(yv)
        o_ref[...] = 0.5 * yv * (1.0 + th)
    return _rows(body, [(y, 'r')], [(y.shape[1], F32, 'r')], n=y.shape[0], name=name)[0]


def _glu_fwd(zg, t, name):
    def body(z_ref, t_ref, o_ref):
        o_ref[...] = (z_ref[...] * _sigmoid(t_ref[...])).astype(BF16)
    return _rows(body, [(zg, 'r'), (t, 'r')], [(zg.shape[1], BF16, 'r')], n=zg.shape[0], name=name)[0]


def _glu_bwd1(dy, zg, t, name):
    def body(dy_ref, z_ref, t_ref, dz_ref, dt_ref, db_ref):
        dyv, zv = dy_ref[...], z_ref[...]
        sg = _sigmoid(t_ref[...])
        dz_ref[...] = dyv * sg
        dt = dyv * zv * sg * (1.0 - sg)
        dt_ref[...] = dt.astype(BF16)
        db_ref[...] += jnp.sum(dt, axis=0, keepdims=True)
    w = zg.shape[1]
    return _rows(body, [(dy, 'r'), (zg, 'r'), (t, 'r')], [(w, F32, 'r'), (w, BF16, 'r'), (w, F32, 'a')],
                 n=zg.shape[0], name=name)


def _glu_bwd2(dzg, ys, u, dskip, name):
    def body(dz_ref, y_ref, u_ref, d_ref, dy_ref, du_ref, dd_ref):
        yv = y_ref[...]
        th, dinner = _gelu_parts(yv)
        dy = dz_ref[...] * (0.5 * (1.0 + th) + 0.5 * yv * (1.0 - th * th) * dinner)
        dy_ref[...] = dy.astype(BF16)
        du_ref[...] = dy * d_ref[...]
        dd_ref[...] += jnp.sum(dy * u_ref[...], axis=0, keepdims=True)
    w = ys.shape[1]
    return _rows(body, [(dzg, 'r'), (ys, 'r'), (u, 'r'), (dskip, 'f')], [(w, BF16, 'r'), (w, F32, 'r'), (w, F32, 'a')],
                 n=ys.shape[0], name=name)


def _scale_rows(u, dskip, name):
    def body(u_ref, d_ref, o_ref):
        o_ref[...] = u_ref[...] * d_ref[...]
    return _rows(body, [(u, 'r'), (dskip, 'f')], [(u.shape[1], F32, 'r')], n=u.shape[0], name=name)[0]


def _merge_fwd(zg, ps, pg, name):
    def body(z_ref, ps_ref, pg_ref, o_ref):
        zv = z_ref[...]
        o_ref[...] = (_sigmoid(zv[:, :D_MODEL]) * ps_ref[...] + _sigmoid(zv[:, D_MODEL:]) * pg_ref[...]).astype(BF16)
    return _rows(body, [(zg, 'r'), (ps, 'r'), (pg, 'r')], [(D_MODEL, BF16, 'r')], n=zg.shape[0], name=name)[0]


def _merge_bwd(dm, zg, ps, pg, name):
    def body(dm_ref, z_ref, ps_ref, pg_ref, dps_ref, dpg_ref, dz_ref):
        dmv, zv = dm_ref[...], z_ref[...]
        s1, s2 = _sigmoid(zv[:, :D_MODEL]), _sigmoid(zv[:, D_MODEL:])
        dps_ref[...] = (dmv * s1).astype(BF16)
        dpg_ref[...] = (dmv * s2).astype(BF16)
        dz_ref[:, :D_MODEL] = (dmv * ps_ref[...] * s1 * (1.0 - s1)).astype(BF16)
        dz_ref[:, D_MODEL:] = (dmv * pg_ref[...] * s2 * (1.0 - s2)).astype(BF16)
    return _rows(body, [(dm, 'r'), (zg, 'r'), (ps, 'r'), (pg, 'r')],
                 [(D_MODEL, BF16, 'r'), (D_MODEL, BF16, 'r'), (2 * D_MODEL, BF16, 'r')], n=zg.shape[0], name=name)


def _final_loss(h, g, tgt, name):
    def body(h_ref, g_ref, t_ref, loss_ref, dh_ref, dg_ref):
        hv = h_ref[...]
        rstd = lax.rsqrt(jnp.mean(hv * hv, axis=-1, keepdims=True) + EPS)
        xh = hv * rstd
        err = xh * g_ref[...] - t_ref[...]
        part = 0.5 * jnp.sum(jnp.mean(err * err, axis=-1, keepdims=True), axis=0, keepdims=True)
        loss_ref[...] += jnp.broadcast_to(part, loss_ref.shape)
        dout = err * (1.0 / hv.shape[1])
        dg_ref[...] += jnp.sum(dout * xh, axis=0, keepdims=True)
        dxh = dout * g_ref[...]
        dh_ref[...] = rstd * (dxh - xh * jnp.mean(dxh * xh, axis=-1, keepdims=True))
    d = h.shape[1]
    return _rows(body, [(h, 'r'), (g, 'f'), (tgt, 'r')], [(LANE, F32, 'a'), (d, F32, 'r'), (d, F32, 'a')],
                 n=h.shape[0], name=name)


def _adamw_math(wv, gv, mv, vv):
    nm = ADAM_B1 * mv + (1.0 - ADAM_B1) * gv
    nv = ADAM_B2 * vv + (1.0 - ADAM_B2) * (gv * gv)
    m_hat = nm / (1.0 - ADAM_B1 ** ADAM_STEP)
    v_hat = nv / (1.0 - ADAM_B2 ** ADAM_STEP)
    return -ADAM_LR * (m_hat / (jnp.sqrt(v_hat) + ADAM_EPS) + ADAM_WD * wv), nm, nv


def _adamw(w, g, m, v, name):
    def body(w_ref, g_ref, m_ref, v_ref, d_ref, nm_ref, nv_ref):
        d_ref[...], nm_ref[...], nv_ref[...] = _adamw_math(w_ref[...], g_ref[...], m_ref[...], v_ref[...])
    c = w.shape[1]
    return _rows(body, [(w, 'r'), (g, 'r'), (m, 'r'), (v, 'r')], [(c, F32, 'r')] * 3, n=w.shape[0], name=name)


ADAMW_BLOCKS = 8


def _adamw_group(items, c_arr, name):
    per = ADAMW_BLOCKS // 2
    n = len(items)

    def body(c_ref, *refs):
        mine = (pl.program_id(0) // per) == c_ref[0]
        for k in range(n):
            w_ref, go_ref, gs_ref, m_ref, v_ref = refs[5 * k:5 * k + 5]
            g_ref, d_ref, nm_ref, nv_ref = refs[5 * n + 4 * k:5 * n + 4 * k + 4]
            gv = jnp.where(mine, go_ref[...], gs_ref[...])
            g_ref[...] = gv
            d_ref[...], nm_ref[...], nv_ref[...] = _adamw_math(w_ref[...], gv, m_ref[...], v_ref[...])

    in_specs, out_specs, out_shape, args = [pl.BlockSpec(memory_space=pltpu.SMEM)], [], [], [c_arr]
    for item in items:
        r, cols = item[0].shape
        assert r % (8 * ADAMW_BLOCKS) == 0, item[0].shape
        tr = r // ADAMW_BLOCKS
        full = pl.BlockSpec((tr, cols), lambda i: (i, 0))
        half = pl.BlockSpec((tr, cols), lambda i: (i % per, 0))
        in_specs += [full, half, half, full, full]
        out_specs += [full] * 4
        out_shape += [jax.ShapeDtypeStruct((r, cols), F32)] * 4
        args += list(item)
    return _pcall(body, name=name, grid=(ADAMW_BLOCKS,), in_specs=in_specs, out_specs=out_specs, out_shape=out_shape,
                  compiler_params=_params())(*args)


def _shift_rows(v, sh, down):
    rolled = pltpu.roll(v, sh if down else v.shape[0] - sh, axis=0)
    row = lax.broadcasted_iota(jnp.int32, v.shape, 0)
    keep = (row >= sh) if down else (row < v.shape[0] - sh)
    return jnp.where(keep, rolled, 0.0)


def _chain_segments(st_r, st_i, pw_r_ref, pw_i_ref, conj, down):
    vr, vi = st_r[...], st_i[...]
    sh, k = 1, 0
    while sh < SEG:
        pr, pi = pw_r_ref[k:k + 1, :], pw_i_ref[k:k + 1, :]
        if conj:
            pi = -pi
        sr, si = _shift_rows(vr, sh, down), _shift_rows(vi, sh, down)
        vr, vi = vr + pr * sr - pi * si, vi + pr * si + pi * sr
        sh, k = sh * 2, k + 1
    st_r[...] = _shift_rows(vr, 1, down)
    st_i[...] = _shift_rows(vi, 1, down)


def _expand_block(u_ref, t_ref, bu_ref):
    for j in range(BD_TILES):
        k = j % 4
        bu_ref[:, j * BD_ST:(j + 1) * BD_ST] = _dot(u_ref[:, k * BD_CH:(k + 1) * BD_CH], t_ref[j])


def _s5_scan(u, tiles, ar8, ai8, pw_r, pw_i, name):
    n = u.shape[0]
    rb = SCAN_ROWS
    nb, steps, lc = n // rb, rb // SEG, 512

    def body(u_ref, t_ref, ar_ref, ai_ref, pwr_ref, pwi_ref, x_ref, st_r, st_i, bu_ref):
        ph, b = pl.program_id(0), pl.program_id(1)

        @pl.when((ph == 0) & (b == 0))
        def _():
            st_r[...] = jnp.zeros_like(st_r)
            st_i[...] = jnp.zeros_like(st_i)

        _expand_block(u_ref, t_ref, bu_ref)

        def scan(store):
            for c in range(S5_GP // lc):
                re, im = slice(c * lc, (c + 1) * lc), slice(S5_GP + c * lc, S5_GP + (c + 1) * lc)
                a_r, a_i = ar_ref[:, re], ai_ref[:, re]

                def step(s, carry):
                    xr, xi = carry
                    rows = pl.ds(pl.multiple_of(s * SEG, SEG), SEG)
                    nr = a_r * xr - a_i * xi + bu_ref[rows, re]
                    ni = a_r * xi + a_i * xr + bu_ref[rows, im]
                    if store:
                        x_ref[rows, re] = nr
                        x_ref[rows, im] = ni
                    return nr, ni

                xr, xi = lax.fori_loop(0, steps, step, (st_r[:, re], st_i[:, re]), unroll=4)
                st_r[:, re] = xr
                st_i[:, re] = xi

        @pl.when(ph == 0)
        def _():
            scan(False)

        @pl.when((ph == 0) & (b == nb - 1))
        def _():
            _chain_segments(st_r, st_i, pwr_ref, pwi_ref, conj=False, down=True)

        @pl.when(ph == 1)
        def _():
            scan(True)

    full = lambda a: pl.BlockSpec(a.shape, lambda ph, b: (0, 0))
    return _pcall(body, name=name, grid=(2, nb),
                  in_specs=[pl.BlockSpec((rb, S5_W), lambda ph, b: (b, 0)), pl.BlockSpec(tiles.shape, lambda ph, b: (0, 0, 0)),
                            full(ar8), full(ai8), full(pw_r), full(pw_i)],
                  out_specs=pl.BlockSpec((rb, 2 * S5_GP), lambda ph, b: (b * ph, 0)),
                  out_shape=jax.ShapeDtypeStruct((n, 2 * S5_GP), F32),
                  scratch_shapes=[pltpu.VMEM((SEG, S5_GP), F32), pltpu.VMEM((SEG, S5_GP), F32),
                                  pltpu.VMEM((rb, 2 * S5_GP), F32)],
                  compiler_params=_params())(u, tiles, ar8, ai8, pw_r, pw_i)


def _s5_scan_bwd(dy, tiles, xs, ar8, ai8, pw_r, pw_i, name):
    n = dy.shape[0]
    rb = SCAN_ROWS
    nb, steps, lc = n // rb, rb // SEG, 256

    def body(dy_ref, t_ref, x_ref, ar_ref, ai_ref, pwr_ref, pwi_ref, lam_ref, da_ref, st_r, st_i, gx_ref):
        ph, b = pl.program_id(0), pl.program_id(1)

        @pl.when((ph == 0) & (b == 0))
        def _():
            st_r[...] = jnp.zeros_like(st_r)
            st_i[...] = jnp.zeros_like(st_i)
            da_ref[...] = jnp.zeros_like(da_ref)

        _expand_block(dy_ref, t_ref, gx_ref)

        def scan(store):
            for c in range(S5_GP // lc):
                re, im = slice(c * lc, (c + 1) * lc), slice(S5_GP + c * lc, S5_GP + (c + 1) * lc)
                a_r, a_i = ar_ref[:, re], ai_ref[:, re]

                def step(s, carry):
                    rows = pl.ds(pl.multiple_of((steps - 1 - s) * SEG, SEG), SEG)
                    if store:
                        lr, li, dr, di = carry
                        xr, xi = x_ref[rows, re], x_ref[rows, im]
                        dr = dr + lr * xr + li * xi
                        di = di + li * xr - lr * xi
                    else:
                        lr, li = carry
                    nr = a_r * lr + a_i * li + gx_ref[rows, re]
                    ni = a_r * li - a_i * lr + gx_ref[rows, im]
                    if store:
                        lam_ref[rows, re] = nr
                        lam_ref[rows, im] = ni
                        return nr, ni, dr, di
                    return nr, ni

                if store:
                    lr, li, dr, di = lax.fori_loop(0, steps, step, (st_r[:, re], st_i[:, re], da_ref[:, re], da_ref[:, im]),
                                                   unroll=4)
                    da_ref[:, re] = dr
                    da_ref[:, im] = di
                else:
                    lr, li = lax.fori_loop(0, steps, step, (st_r[:, re], st_i[:, re]), unroll=4)
                st_r[:, re] = lr
                st_i[:, re] = li

        @pl.when(ph == 0)
        def _():
            scan(False)

        @pl.when((ph == 0) & (b == nb - 1))
        def _():
            _chain_segments(st_r, st_i, pwr_ref, pwi_ref, conj=True, down=False)

        @pl.when(ph == 1)
        def _():
            scan(True)

    full = lambda a: pl.BlockSpec(a.shape, lambda ph, b: (0, 0))
    rev = lambda ph, b: (nb - 1 - b, 0)
    return _pcall(body, name=name, grid=(2, nb),
                  in_specs=[pl.BlockSpec((rb, S5_W), rev), pl.BlockSpec(tiles.shape, lambda ph, b: (0, 0, 0)),
                            pl.BlockSpec((rb, 2 * S5_GP), lambda ph, b: ((nb - 1 - b) * ph, 0)),
                            full(ar8), full(ai8), full(pw_r), full(pw_i)],
                  out_specs=[pl.BlockSpec((rb, 2 * S5_GP), lambda ph, b: (nb - 1 - b * ph, 0)),
                             pl.BlockSpec((SEG, 2 * S5_GP), lambda ph, b: (0, 0))],
                  out_shape=[jax.ShapeDtypeStruct((n, 2 * S5_GP), F32), jax.ShapeDtypeStruct((SEG, 2 * S5_GP), F32)],
                  scratch_shapes=[pltpu.VMEM((SEG, S5_GP), F32), pltpu.VMEM((SEG, S5_GP), F32),
                                  pltpu.VMEM((rb, 2 * S5_GP), F32)],
                  compiler_params=_params())(dy, tiles, xs, ar8, ai8, pw_r, pw_i)


def _s5_discretize(lam_re, lam_im, log_dt, b_re, b_im):
    dt = jnp.exp(log_dt)[:, None]
    mag = jnp.exp(lam_re * dt)
    ar = mag * jnp.cos(lam_im * dt)
    ai = mag * jnp.sin(lam_im * dt)
    den = lam_re * lam_re + lam_im * lam_im
    nr = ar - 1.0
    fr = (nr * lam_re + ai * lam_im) / den
    fi = (ai * lam_re - nr * lam_im) / den
    bbar_re = fr[:, :, None] * b_re - fi[:, :, None] * b_im
    bbar_im = fr[:, :, None] * b_im + fi[:, :, None] * b_re
    return ar, ai, bbar_re, bbar_im


BD_TILES, BD_CH, BD_ST, BD_GROUPS = 8, 128, 512, 8
BD_ROWS = 4096


def _bd_tiles(re, im):
    eye = jnp.eye(BD_GROUPS, dtype=re.dtype)

    def tiles(t):
        t = t.reshape(S5_G // BD_GROUPS, BD_GROUPS, S5_H, S5_P)
        return (t[:, :, :, None, :] * eye[None, :, None, :, None]).reshape(S5_G // BD_GROUPS, BD_CH, BD_ST)

    return jnp.concatenate([tiles(re), tiles(im)], axis=0)


def _bd_blocks(t):
    t = t.reshape(2, S5_G // BD_GROUPS, BD_GROUPS, S5_H, BD_GROUPS, S5_P)
    return jnp.einsum('rkahap->rkahp', t).reshape(2, S5_G, S5_H, S5_P)


def _bd_reduce(x, t, res, name):
    n = x.shape[0]
    tm = _pick(n, BD_ROWS, 16)

    def body(x_ref, t_ref, r_ref, o_ref):
        part = _dot(x_ref[...], t_ref[...], NT)

        @pl.when(pl.program_id(2) == 0)
        def _():
            o_ref[...] = r_ref[...] + part

        @pl.when(pl.program_id(2) == 1)
        def _():
            o_ref[...] += part

    return _pcall(body, name=name, grid=(n // tm, 4, 2),
                  in_specs=[pl.BlockSpec((tm, BD_ST), lambda i, k, r: (i, k + 4 * r)),
                            pl.BlockSpec((None, BD_CH, BD_ST), lambda i, k, r: (k + 4 * r, 0, 0)),
                            pl.BlockSpec((tm, BD_CH), lambda i, k, r: (i, k))],
                  out_specs=pl.BlockSpec((tm, BD_CH), lambda i, k, r: (i, k)),
                  out_shape=jax.ShapeDtypeStruct((n, S5_W), F32), compiler_params=_params())(x, t, res)


def _bd_outer(a, x, name):
    n = a.shape[0]
    tk = _pick(n, BD_ROWS, 16)
    nk = n // tk

    def body(a_ref, x_ref, o_ref):
        part = _dot(a_ref[...], x_ref[...], TN)

        @pl.when(pl.program_id(1) == 0)
        def _():
            o_ref[...] = part

        @pl.when(pl.program_id(1) > 0)
        def _():
            o_ref[...] += part

    return _pcall(body, name=name, grid=(BD_TILES, nk),
                  in_specs=[pl.BlockSpec((tk, BD_CH), lambda j, kk: (kk, j % 4)), pl.BlockSpec((tk, BD_ST), lambda j, kk: (kk, j))],
                  out_specs=pl.BlockSpec((None, BD_CH, BD_ST), lambda j, kk: (j, 0, 0)),
                  out_shape=jax.ShapeDtypeStruct((BD_TILES, BD_CH, BD_ST), F32), compiler_params=_params())(a, x)


def _permute_rows(t):
    n = t.shape[0]
    return t.reshape(SEG, n // SEG, t.shape[1]).transpose(1, 0, 2).reshape(n, t.shape[1])


def _unpermute_rows(t):
    n = t.shape[0]
    return t.reshape(n // SEG, SEG, t.shape[1]).transpose(1, 0, 2).reshape(n, t.shape[1])


def _segment_powers(ar, ai, seg_steps):
    pr, pi = ar.reshape(1, S5_GP), ai.reshape(1, S5_GP)
    e = 1
    while e < seg_steps:
        pr, pi = pr * pr - pi * pi, 2.0 * pr * pi
        e *= 2
    assert e == seg_steps, "segment length must be a power of two"
    rows_r, rows_i = [], []
    for _ in range(3):
        rows_r.append(pr)
        rows_i.append(pi)
        pr, pi = pr * pr - pi * pi, 2.0 * pr * pi
    pad = jnp.zeros((SEG - 3, S5_GP), F32)
    return jnp.concatenate(rows_r + [pad], axis=0), jnp.concatenate(rows_i + [pad], axis=0)


NT = (((1,), (1,)), ((), ()))
TN = (((0,), (0,)), ((), ()))


def _dot(a, b, dims=None, exact=False):
    dims = (((1,), (0,)), ((), ())) if dims is None else dims
    if exact:
        return lax.dot_general(a, b, dims, precision=HI, preferred_element_type=F32)
    return lax.dot_general(a.astype(BF16), b.astype(BF16), dims, preferred_element_type=F32)


def _dot01(a, b, dims=None, ones_first=True):
    x = b if ones_first else a
    hi = x.astype(BF16)
    lo = (x - hi.astype(F32)).astype(BF16)
    parts = [(_dot(a, p, dims) if ones_first else _dot(p, b, dims)) for p in (lo, hi)]
    return parts[0] + parts[1]


HEADS = range(4)


def _gla_chunk_fwd(qc, kc, vc, al, wup, bup, s_prev, tril):
    ones = jnp.ones((GLA_CHUNK, GLA_DV), F32)
    z = [_dot(al, wup[h]) + bup[h] for h in HEADS]
    la = [(jnp.minimum(z[h], 0.0) - jnp.log(1.0 + jnp.exp(-jnp.abs(z[h])))) * (1.0 / GLA_TAU) for h in HEADS]
    bc = [_dot01(tril, la[h]) for h in HEADS]
    blb = [_dot01(la[h], ones, TN, ones_first=False) for h in HEADS]
    bl = [bc[h][GLA_CHUNK - 1:GLA_CHUNK, :] for h in HEADS]
    ebc = [jnp.exp(bc[h]) for h in HEADS]
    qt = [qc[h] * (GLA_DK ** -0.5) * ebc[h] for h in HEADS]
    kt = [kc[h] * jnp.exp(-bc[h]) for h in HEADS]
    ke = [kc[h] * jnp.exp(bl[h] - bc[h]) for h in HEADS]
    sc = [_dot(qt[h], kt[h], NT) * tril for h in HEADS]
    oi = [_dot(sc[h], vc[h]) for h in HEADS]
    oo = [_dot(qt[h], s_prev[h]) for h in HEADS]
    o = [oi[h] + oo[h] for h in HEADS]
    return z, bc, bl, blb, ebc, qt, kt, ke, sc, o


GLA_ROWS = 512
GLA_CPB = GLA_ROWS // GLA_CHUNK


ZA_COLS = 5 * 512
SLOT = 128


def _pad_heads(w):
    r = w.shape[0]
    return jnp.pad(w.reshape(r, GLA_HEADS, GLA_DK), ((0, 0), (0, 0), (0, SLOT - GLA_DK))).reshape(r, GLA_HEADS * SLOT)


def _unpad_heads(w):
    r = w.shape[0]
    return w.reshape(r, GLA_HEADS, SLOT)[:, :, :GLA_DK].reshape(r, GLA_HEADS * GLA_DK)


def _gla_token_specs(blk):
    col = lambda cb: pl.BlockSpec((GLA_ROWS, 512), lambda j: (blk(j), cb))
    whole = lambda a: pl.BlockSpec(a.shape, lambda j: (0,) * a.ndim)
    return col, whole


def _head_ds(h, width):
    return pl.ds(h * SLOT, width)


def _tri(lower):
    ri = lax.broadcasted_iota(jnp.int32, (GLA_CHUNK, GLA_CHUNK), 0)
    ci = lax.broadcasted_iota(jnp.int32, (GLA_CHUNK, GLA_CHUNK), 1)
    return ((ri >= ci) if lower else (ri <= ci)).astype(F32)


def _gla_fwd(za, al, wup, bup, gn, name):
    n = za.shape[0]
    nc = n // GLA_CHUNK

    def body(q_ref, k_ref, v_ref, r_ref, al_ref, wup_ref, bup_ref, gn_ref, y_ref, sp_ref, s_ref):
        @pl.when(pl.program_id(0) == 0)
        def _():
            s_ref[...] = jnp.zeros_like(s_ref)

        tril = _tri(True)

        def chunk(c, carry):
            rows = pl.ds(pl.multiple_of(c * GLA_CHUNK, GLA_CHUNK), GLA_CHUNK)
            alc = al_ref[rows, :]
            vc = [v_ref[rows, _head_ds(h, GLA_DV)] for h in HEADS]
            s_prev = [s_ref[h] for h in HEADS]
            _, _, _, blb, _, _, _, ke, _, o = _gla_chunk_fwd(
                [q_ref[rows, _head_ds(h, GLA_DK)] for h in HEADS], [k_ref[rows, _head_ds(h, GLA_DK)] for h in HEADS],
                vc, alc, [wup_ref[h] for h in HEADS], [bup_ref[h] for h in HEADS], s_prev, tril)
            ds = [_dot(ke[h], vc[h], TN) for h in HEADS]
            for h in HEADS:
                rc = r_ref[rows, _head_ds(h, GLA_DV)]
                sp_ref[h, c] = s_prev[h]
                rstd = lax.rsqrt(jnp.mean(o[h] * o[h], axis=-1, keepdims=True) + EPS)
                y_ref[rows, _head_ds(h, GLA_DV)] = (o[h] * rstd * gn_ref[h] * (rc * _sigmoid(rc))).astype(BF16)
                s_ref[h] = jnp.exp(blb[h]) * s_prev[h] + ds[h]
            return carry

        lax.fori_loop(0, GLA_CPB, chunk, 0)

    col, whole = _gla_token_specs(lambda j: j)
    return _pcall(body, name=name, grid=(n // GLA_ROWS,),
                  in_specs=[col(1), col(2), col(3), col(4), pl.BlockSpec((GLA_ROWS, LANE), lambda j: (j, 0)),
                            whole(wup), whole(bup), whole(gn)],
                  out_specs=[pl.BlockSpec((GLA_ROWS, GLA_HEADS * GLA_DV), lambda j: (j, 0)),
                             pl.BlockSpec((GLA_HEADS, GLA_CPB, GLA_DK, GLA_DV), lambda j: (0, j, 0, 0))],
                  out_shape=[jax.ShapeDtypeStruct((n, GLA_HEADS * GLA_DV), BF16),
                             jax.ShapeDtypeStruct((GLA_HEADS, nc, GLA_DK, GLA_DV), F32)],
                  scratch_shapes=[pltpu.VMEM((GLA_HEADS, GLA_DK, GLA_DV), F32)],
                  compiler_params=_params())(za, za, za, za, al, wup, bup, gn)


def _gla_bwd(za, al, wup, bup, gn, sp, dy, du_s5, name):
    n = za.shape[0]
    nb = n // GLA_ROWS

    def body(q_ref, k_ref, v_ref, r_ref, al_ref, wup_ref, bup_ref, gn_ref, dy_ref, dus_ref, sp_ref,
             dza_ref, dz_ref, dgn_ref, dbup_ref, ds_ref):
        @pl.when(pl.program_id(0) == 0)
        def _():
            ds_ref[...] = jnp.zeros_like(ds_ref)
            dgn_ref[...] = jnp.zeros_like(dgn_ref)
            dbup_ref[...] = jnp.zeros_like(dbup_ref)

        tril, triu = _tri(True), _tri(False)
        dza_ref[:, 0:512] = dus_ref[...].astype(BF16)
        dza_ref[:, 512:1536] = jnp.zeros((GLA_ROWS, 1024), BF16)
        dz_ref[...] = jnp.zeros_like(dz_ref)

        def chunk(i, carry):
            c = GLA_CPB - 1 - i
            rows = pl.ds(pl.multiple_of(c * GLA_CHUNK, GLA_CHUNK), GLA_CHUNK)
            alc = al_ref[rows, :]
            qc = [q_ref[rows, _head_ds(h, GLA_DK)] for h in HEADS]
            kc = [k_ref[rows, _head_ds(h, GLA_DK)] for h in HEADS]
            vc = [v_ref[rows, _head_ds(h, GLA_DV)] for h in HEADS]
            s_prev = [sp_ref[h, c] for h in HEADS]
            ds = [ds_ref[h] for h in HEADS]
            z, bc, bl, blb, ebc, qt, kt, ke, sc, o = _gla_chunk_fwd(
                qc, kc, vc, alc, [wup_ref[h] for h in HEADS], [bup_ref[h] for h in HEADS], s_prev, tril)
            do = []
            for h in HEADS:
                rc = r_ref[rows, _head_ds(h, GLA_DV)]
                rs = lax.rsqrt(jnp.mean(o[h] * o[h], axis=-1, keepdims=True) + EPS)
                on = o[h] * rs
                sr = _sigmoid(rc)
                sil = rc * sr
                dyv, gnv = dy_ref[rows, _head_ds(h, GLA_DV)], gn_ref[h]
                dgn_ref[h] += jnp.sum(dyv * on * sil, axis=0, keepdims=True)
                dza_ref[rows, pl.ds(2048 + h * SLOT, GLA_DV)] = (dyv * on * gnv * (sr * (1.0 + rc * (1.0 - sr)))).astype(BF16)
                don = dyv * gnv * sil
                do.append(rs * (don - on * jnp.mean(don * on, axis=-1, keepdims=True)))
            dp = [_dot(do[h], vc[h], NT) * tril for h in HEADS]
            dv1 = [_dot(sc[h], do[h], TN) for h in HEADS]
            dv2 = [_dot(ke[h], ds[h]) for h in HEADS]
            dq2 = [_dot(do[h], s_prev[h], NT) for h in HEADS]
            dke = [_dot(vc[h], ds[h], NT) for h in HEADS]
            ddec = [_dot01(jnp.ones((8, GLA_DV), F32), ds[h] * s_prev[h], NT)[0:1, :] for h in HEADS]
            dsn = [_dot(qt[h], do[h], TN) for h in HEADS]
            dq1 = [_dot(dp[h], kt[h]) for h in HEADS]
            dkt = [_dot(dp[h], qt[h], TN) for h in HEADS]
            dbc, dbl = [], []
            for h in HEADS:
                dqt = dq1[h] + dq2[h]
                dza_ref[rows, pl.ds(1536 + h * SLOT, GLA_DV)] = (dv1[h] + dv2[h]).astype(BF16)
                ds_ref[h] = jnp.exp(blb[h]) * ds[h] + dsn[h]
                dza_ref[rows, pl.ds(512 + h * SLOT, GLA_DK)] = (dqt * (GLA_DK ** -0.5) * ebc[h]).astype(BF16)
                dza_ref[rows, pl.ds(1024 + h * SLOT, GLA_DK)] = (dkt[h] * jnp.exp(-bc[h])
                                                                 + dke[h] * jnp.exp(bl[h] - bc[h])).astype(BF16)
                dbc.append(dqt * qt[h] - dkt[h] * kt[h] - dke[h] * ke[h])
                dbl.append(jnp.sum(dke[h] * ke[h], axis=0, keepdims=True) + ddec[h] * jnp.exp(bl[h]))
            dla = [_dot01(triu, dbc[h]) + dbl[h] for h in HEADS]
            for h in HEADS:
                dz = dla[h] * (1.0 - _sigmoid(z[h])) * (1.0 / GLA_TAU)
                dz_ref[rows, _head_ds(h, GLA_DK)] = dz
                dbup_ref[h] += jnp.sum(dz, axis=0, keepdims=True)
            return carry

        lax.fori_loop(0, GLA_CPB, chunk, 0)

    rev = lambda j: nb - 1 - j
    col, whole = _gla_token_specs(rev)
    tok = lambda w: pl.BlockSpec((GLA_ROWS, w), lambda j: (rev(j), 0))
    h1 = lambda w: pl.BlockSpec((GLA_HEADS, 1, w), lambda j: (0, 0, 0))
    s1 = lambda w: jax.ShapeDtypeStruct((GLA_HEADS, 1, w), F32)
    return _pcall(body, name=name, grid=(nb,),
                  in_specs=[col(1), col(2), col(3), col(4), tok(LANE), whole(wup), whole(bup), whole(gn), tok(512), tok(512),
                            pl.BlockSpec((GLA_HEADS, GLA_CPB, GLA_DK, GLA_DV), lambda j: (0, rev(j), 0, 0))],
                  out_specs=[tok(ZA_COLS), tok(GLA_HEADS * SLOT), h1(GLA_DV), h1(GLA_DK)],
                  out_shape=[jax.ShapeDtypeStruct((n, ZA_COLS), BF16), jax.ShapeDtypeStruct((n, GLA_HEADS * SLOT), F32),
                             s1(GLA_DV), s1(GLA_DK)],
                  scratch_shapes=[pltpu.VMEM((GLA_HEADS, GLA_DK, GLA_DV), F32)],
                  compiler_params=_params())(za, za, za, za, al, wup, bup, gn, dy, du_s5, sp)


ANY = pl.BlockSpec(memory_space=pl.ANY)


def _place():
    x, y, c = lax.axis_index("x"), lax.axis_index("y"), lax.axis_index("c")
    chips = [(1 - x, y), (x, 1 - y), (1 - x, 1 - y)]
    return x, y, c, chips


def _remote(src, dst, ssem, rsem, dev):
    return pltpu.make_async_remote_copy(src_ref=src, dst_ref=dst, send_sem=ssem, recv_sem=rsem, device_id=dev,
                                        device_id_type=MESH_ID)


def _half(c, rows):
    h = rows // 2
    return pl.ds(pl.multiple_of(c * h, 8), h)


def _side_gather_ici(shards):
    def copies(ins, outs, ssem, rsem):
        x, y, c, chips = _place()
        mine = 2 * x + y
        cps = []
        for w in range(len(ins)):
            half = _half(c, ins[w].shape[0])
            cps.append(_remote(ins[w], outs[w].at[mine], ssem.at[4 * w], rsem.at[4 * w], (x, y, 1 - c)))
            for k, (px, py) in enumerate(chips):
                cps.append(_remote(ins[w].at[half], outs[w].at[mine, half], ssem.at[4 * w + 1 + k], rsem.at[4 * w + 1 + k],
                                   (px, py, c)))
        return cps

    return _Side(shards, [jax.ShapeDtypeStruct((4,) + s.shape, s.dtype) for s in shards], 4 * len(shards), copies)


def _side_gather_d2d(gathered):
    def copies(ins, outs, ssem, rsem):
        x, y, c, chips = _place()
        cps = []
        for w in range(len(outs)):
            half = _half(c, outs[w].shape[1])
            for k, (px, py) in enumerate(chips):
                theirs = outs[w].at[2 * px + py, half]
                cps.append(_remote(theirs, theirs, ssem.at[3 * w + k], rsem.at[3 * w + k], (x, y, 1 - c)))
        return cps

    return _Side(gathered, [jax.ShapeDtypeStruct(g.shape, g.dtype) for g in gathered], 3 * len(gathered), copies,
                 aliased=True)


def _side_swap_halves(grads):
    def copies(ins, outs, ssem, rsem):
        x, y, c, _ = _place()
        return [_remote(ins[w].at[:, _half(1 - c, ins[w].shape[1]), :], outs[w], ssem.at[w], rsem.at[w], (x, y, 1 - c))
                for w in range(len(ins))]

    return _Side(grads, [jax.ShapeDtypeStruct((4, g.shape[1] // 2, g.shape[2]), g.dtype) for g in grads], len(grads), copies)


def _side_scatter(sums):
    def copies(ins, outs, ssem, rsem):
        x, y, c, chips = _place()
        return [_remote(ins[w].at[2 * px + py], outs[w].at[k], ssem.at[3 * w + k], rsem.at[3 * w + k], (px, py, c))
                for w in range(len(ins)) for k, (px, py) in enumerate(chips)]

    return _Side(sums, [jax.ShapeDtypeStruct((3,) + s.shape[1:], s.dtype) for s in sums], 3 * len(sums), copies)


def _side_swap_reduced(halves):
    def copies(ins, outs, ssem, rsem):
        x, y, c, _ = _place()
        return [_remote(ins[w], outs[w], ssem.at[w], rsem.at[w], (x, y, 1 - c)) for w in range(len(ins))]

    return _Side(halves, [jax.ShapeDtypeStruct(h.shape, h.dtype) for h in halves], len(halves), copies)


SUM_BLOCKS = 2


def _chip_sums(gs, recvs, c_arr, name):
    n = len(gs)

    def body(c_ref, *refs):
        for k in range(n):
            refs[2 * n + k][...] = (refs[2 * k][...] + refs[2 * k + 1][...]).astype(BF16)

    in_specs, out_specs, out_shape, args = [], [], [], []
    for g, recv in zip(gs, recvs):
        _, r, cols = g.shape
        h = r // 2
        assert h % (16 * SUM_BLOCKS) == 0, g.shape
        tr = h // SUM_BLOCKS
        in_specs += [pl.BlockSpec((None, None, tr, cols), lambda s, i, c_ref: (s, c_ref[0], i, 0)),
                     pl.BlockSpec((None, tr, cols), lambda s, i, c_ref: (s, i, 0))]
        out_specs.append(pl.BlockSpec((None, tr, cols), lambda s, i, c_ref: (s, i, 0)))
        out_shape.append(jax.ShapeDtypeStruct((4, h, cols), BF16))
        args += [g.reshape(4, 2, h, cols), recv]
    grid_spec = pltpu.PrefetchScalarGridSpec(num_scalar_prefetch=1, grid=(4, SUM_BLOCKS), in_specs=in_specs,
                                             out_specs=out_specs)
    return _pcall(body, name=name, grid_spec=grid_spec, out_shape=out_shape, compiler_params=_params())(c_arr, *args)


def _owner_sums(sums, others, s_arr, name):
    n = len(sums)

    def body(s_ref, *refs):
        f = lambda v: v.astype(F32)
        for k in range(n):
            a_ref, o_ref = refs[2 * k], refs[2 * k + 1]
            refs[2 * n + k][...] = (f(a_ref[...]) + f(o_ref[0])) + (f(o_ref[1]) + f(o_ref[2]))

    in_specs, out_specs, out_shape, args = [], [], [], []
    for sm, ot in zip(sums, others):
        _, h, cols = sm.shape
        tr = h // SUM_BLOCKS
        in_specs += [pl.BlockSpec((None, tr, cols), lambda i, s_ref: (s_ref[0], i, 0)),
                     pl.BlockSpec((3, tr, cols), lambda i, s_ref: (0, i, 0))]
        out_specs.append(pl.BlockSpec((tr, cols), lambda i, s_ref: (i, 0)))
        out_shape.append(jax.ShapeDtypeStruct((h, cols), F32))
        args += [sm, ot]
    grid_spec = pltpu.PrefetchScalarGridSpec(num_scalar_prefetch=1, grid=(SUM_BLOCKS,), in_specs=in_specs,
                                             out_specs=out_specs)
    return _pcall(body, name=name, grid_spec=grid_spec, out_shape=out_shape, compiler_params=_params())(s_arr, *args)


def _side_small_sibling(v):
    def copies(ins, outs, ssem, rsem):
        x, y, c, _ = _place()
        return [_remote(ins[0], outs[0], ssem.at[0], rsem.at[0], (x, y, 1 - c))]

    return _Side([v], [jax.ShapeDtypeStruct(v.shape, F32)], 1, copies)


def _side_small_chips(v):
    def copies(ins, outs, ssem, rsem):
        x, y, c, chips = _place()
        return [_remote(ins[0], outs[0].at[k], ssem.at[k], rsem.at[k], (px, py, c)) for k, (px, py) in enumerate(chips)]

    return _Side([v], [jax.ShapeDtypeStruct((3,) + v.shape, F32)], 3, copies)


def _small_add(v, r, name):
    def body(v_ref, r_ref, o_ref):
        if r.ndim == 2:
            o_ref[...] = v_ref[...] + r_ref[...]
        else:
            o_ref[...] = (v_ref[...] + r_ref[0]) + (r_ref[1] + r_ref[2])

    vm = pl.BlockSpec(memory_space=pltpu.VMEM)
    return _pcall(body, name=name, in_specs=[vm, vm], out_specs=vm, out_shape=jax.ShapeDtypeStruct(v.shape, F32),
                  compiler_params=_params())(v, r)


def _merge_sides(sides):
    if len(sides) == 1:
        return sides[0]

    def copies(in_refs, out_refs, ssem, rsem):
        cps, i, o, q = [], 0, 0, 0
        for s in sides:
            ni, no = len(s.ins), len(s.out_shapes)
            cps += s.copies(in_refs[i:i + ni], out_refs[o:o + no], ssem.at[pl.ds(q, s.nsem)], rsem.at[pl.ds(q, s.nsem)])
            i, o, q = i + ni, o + no, q + s.nsem
        return cps

    assert not any(s.aliased for s in sides)
    return _Side(sum((s.ins for s in sides), []), sum((s.out_shapes for s in sides), []), sum(s.nsem for s in sides), copies)


def _tile_rows(size):
    return -(-size // (8 * LANE)) * 8


def _pack_small(parts):
    pieces = []
    for p in parts:
        flat = p.reshape(-1).astype(F32)
        pieces.append(jnp.pad(flat, (0, _tile_rows(p.size) * LANE - p.size)).reshape(-1, LANE))
    rows = sum(x.shape[0] for x in pieces)
    pieces.append(jnp.zeros(((-rows) % 64, LANE), F32))
    return jnp.concatenate(pieces, axis=0)


def _unpack_small(packed, like):
    out, pos = [], 0
    for p in like:
        rows = _tile_rows(p.size)
        out.append(packed[pos:pos + rows].reshape(-1)[:p.size].reshape(p.shape))
        pos += rows
    return out


FFN_FWD_ROWS, FFN_BWD_ROWS = 1024, 512
FFN_SUB_ROWS = 256


def _ffn_specs(n, d, fs, cap):
    rows = _pick(n, cap, 16)
    row = pl.BlockSpec((rows, d), lambda i, s: (i, 0))
    gain = pl.BlockSpec((1, d), lambda i, s: (0, 0))
    w_row = pl.BlockSpec((None, fs, d), lambda i, s: (s, 0, 0))
    hid = pl.BlockSpec((None, rows, fs), lambda i, s: (s, i, 0))
    return rows, row, gain, w_row, hid


def _ffn_fwd(h, g, w1t, w3t, w2, tag, plan):
    n, d = h.shape
    ns, fs, _ = w2.shape
    rows, row, gain, w_row, hid = _ffn_specs(n, d, fs, FFN_FWD_ROWS)
    sub = rows

    def body(h_ref, g_ref, w1_ref, w3_ref, w2_ref, out_ref, n1_ref, a_ref, b_ref, hm_ref, acc_ref):
        s = pl.program_id(1)

        @pl.when(s == 0)
        def _():
            xv = h_ref[...]
            rstd = lax.rsqrt(jnp.mean(xv * xv, axis=-1, keepdims=True) + EPS)
            n1_ref[...] = (xv * rstd * g_ref[...]).astype(BF16)
            acc_ref[...] = jnp.zeros_like(acc_ref)

        def up(j):
            n1 = n1_ref[j * sub:(j + 1) * sub, :]
            return _dot(n1, w1_ref[...], NT), _dot(n1, w3_ref[...], NT)

        cur = up(0)
        for j in range(rows // sub):
            nxt = up(j + 1) if (j + 1) * sub < rows else None
            a, b = cur
            r = slice(j * sub, (j + 1) * sub)
            hm = (a * _sigmoid(a) * b).astype(BF16)
            a_ref[r, :] = a.astype(BF16)
            b_ref[r, :] = b.astype(BF16)
            hm_ref[r, :] = hm
            acc_ref[r, :] += _dot(hm, w2_ref[...])
            cur = nxt

        @pl.when(s == ns - 1)
        def _():
            out_ref[...] = h_ref[...] + 0.5 * acc_ref[...]

    hid_shape = jax.ShapeDtypeStruct((ns, n, fs), BF16)
    plan.before(f"{tag}_fwd")
    out, n1, a, b, hm = _pcall(
        body, name=f"{tag}_fwd", grid=(n // rows, ns), in_specs=[row, gain, w_row, w_row, w_row],
        out_specs=[row, row, hid, hid, hid],
        out_shape=[jax.ShapeDtypeStruct((n, d), F32), jax.ShapeDtypeStruct((n, d), BF16), hid_shape, hid_shape, hid_shape],
        scratch_shapes=[pltpu.VMEM((rows, d), F32)], compiler_params=_params())(h, g, w1t, w3t, w2)
    plan.after(f"{tag}_fwd")
    return out, (h, n1, a, b, hm)


def _wgrad(a3, b, name, alpha=1.0):
    ns, n, fs = a3.shape
    d = b.shape[1]
    tk = _pick(n, 1024, 16)

    def body(a_ref, b_ref, o_ref):
        @pl.when(pl.program_id(0) == 0)
        def _():
            o_ref[...] = jnp.zeros_like(o_ref)

        bv = b_ref[...].astype(BF16)
        for s in range(ns):
            part = _dot(a_ref[s], bv, TN)
            o_ref[s] += part if alpha == 1.0 else alpha * part

    return _pcall(body, name=name, grid=(n // tk,),
                  in_specs=[pl.BlockSpec((ns, tk, fs), lambda k: (0, k, 0)), pl.BlockSpec((tk, d), lambda k: (k, 0))],
                  out_specs=pl.BlockSpec((ns, fs, d), lambda k: (0, 0, 0)),
                  out_shape=jax.ShapeDtypeStruct((ns, fs, d), F32), compiler_params=_params())(a3, b)


def _ffn_bwd(dout, saved, g, w1, w3, w2, tag, plan):
    h, n1, a, b, hm = saved
    n, d = h.shape
    ns, fs, _ = w2.shape
    rows, row, gain, w_row, hid = _ffn_specs(n, d, fs, FFN_BWD_ROWS)
    sub = _pick(rows, FFN_SUB_ROWS, 16)

    total = (n // rows) * ns

    def body(do_ref, h_ref, g_ref, a_ref, b_ref, w1_hbm, w3_hbm, w2_hbm, dh_ref, da_ref, db_ref, dg_ref, acc_ref,
             wbuf, wsem):
        i, s = pl.program_id(0), pl.program_id(1)
        t = i * ns + s

        def fetch(step, k):
            slot = step % 3
            return pltpu.make_async_copy((w1_hbm, w3_hbm, w2_hbm)[k].at[step % ns], wbuf.at[slot, k], wsem.at[slot, k])

        @pl.when(t == 0)
        def _():
            for k in range(3):
                fetch(0, k).start()
                fetch(1, k).start()

        for k in range(3):
            fetch(t, k).wait()

        @pl.when(t + 2 < total)
        def _():
            for k in range(3):
                fetch(t + 2, k).start()

        slot = t % 3
        w1_ref, w3_ref, w2_ref = wbuf.at[slot, 0], wbuf.at[slot, 1], wbuf.at[slot, 2]

        @pl.when(s == 0)
        def _():
            acc_ref[...] = jnp.zeros_like(acc_ref)

        @pl.when((s == 0) & (i == 0))
        def _():
            dg_ref[...] = jnp.zeros_like(dg_ref)

        def up(j):
            return _dot(0.5 * do_ref[j * sub:(j + 1) * sub, :], w2_ref[...], NT)

        cur = up(0)
        for j in range(rows // sub):
            nxt = up(j + 1) if (j + 1) * sub < rows else None
            r = slice(j * sub, (j + 1) * sub)
            av, bv = a_ref[r, :].astype(F32), b_ref[r, :].astype(F32)
            sg = _sigmoid(av)
            da = (cur * bv * (sg * (1.0 + av * (1.0 - sg)))).astype(BF16)
            db = (cur * av * sg).astype(BF16)
            da_ref[r, :] = da
            db_ref[r, :] = db
            acc_ref[r, :] += _dot(da, w1_ref[...]) + _dot(db, w3_ref[...])
            cur = nxt

        @pl.when(s == ns - 1)
        def _():
            xv, dn = h_ref[...], acc_ref[...]
            rstd = lax.rsqrt(jnp.mean(xv * xv, axis=-1, keepdims=True) + EPS)
            xh = xv * rstd
            dg_ref[...] += jnp.sum(dn * xh, axis=0, keepdims=True)
            dxh = dn * g_ref[...]
            dh_ref[...] = do_ref[...] + rstd * (dxh - xh * jnp.mean(dxh * xh, axis=-1, keepdims=True))

    hid_shape = jax.ShapeDtypeStruct((ns, n, fs), BF16)
    plan.before(f"{tag}_bwd")
    dh, da, db, dg = _pcall(
        body, name=f"{tag}_bwd", grid=(n // rows, ns), in_specs=[row, row, gain, hid, hid, ANY, ANY, ANY],
        out_specs=[row, hid, hid, gain],
        out_shape=[jax.ShapeDtypeStruct((n, d), F32), hid_shape, hid_shape, jax.ShapeDtypeStruct((1, d), F32)],
        scratch_shapes=[pltpu.VMEM((rows, d), F32), pltpu.VMEM((3, 3, fs, d), BF16), pltpu.SemaphoreType.DMA((3, 3))],
        compiler_params=_params())(dout, h, g, a, b, w1, w3, w2)
    plan.after(f"{tag}_bwd")
    plan.grads[f"{tag}_norm"] = dg
    plan.before(f"{tag}_gw2")
    gw2 = _wgrad(hm, dout, f"{tag}_gw2", alpha=0.5)
    plan.after(f"{tag}_gw2")
    plan.grads[f"{tag}_w2"] = gw2
    plan.before(f"{tag}_gw1")
    gw1 = _wgrad(da, n1, f"{tag}_gw1")
    plan.after(f"{tag}_gw1")
    plan.grads[f"{tag}_w1"] = gw1
    plan.before(f"{tag}_gw3")
    gw3 = _wgrad(db, n1, f"{tag}_gw3")
    plan.after(f"{tag}_gw3")
    return dh, dg, gw1, gw3, gw2


def _local_step(x, tgt, plan):
    n = x.shape[0]
    grads = plan.grads

    def f(name):
        w = plan.get(name)
        return w.reshape(1, D_MODEL) if name.endswith('_norm') and name != 'gla_out_norm' else w

    def carried(tag, fn, *args, **kw):
        plan.before(tag)
        out = fn(*args, **kw)
        plan.after(tag)
        return out

    h1, ffn1 = _ffn_fwd(x, f('ffn1_norm'), f('ffn1_w1'), f('ffn1_w3'), f('ffn1_w2'), "ffn1", plan)
    u = carried("mix_rms", _rms_fwd, h1, f('mix_norm'), "mix_rms")
    w_in = f('w_in')
    w_a = jnp.concatenate([w_in[:, :512], _pad_heads(w_in[:, 512:768]), _pad_heads(w_in[:, 768:1024]), w_in[:, 1024:2048]],
                          axis=1)
    w_al = jnp.pad(w_in[:, 2048:2048 + GLA_RANK], ((0, 0), (0, LANE - GLA_RANK)))
    w_g = w_in[:, 2048 + GLA_RANK:]
    za = carried("in_a", _mm, u, w_a, name="in_a")
    zg = carried("in_g", _mm, u, w_g, name="in_g")
    al = _mm(u, w_al, name="in_al")
    ar, ai, bbar_re, bbar_im = _s5_discretize(f('s5_lambda_re'), f('s5_lambda_im'), f('s5_log_dt'), f('s5_b_re'), f('s5_b_im'))
    t_b = _bd_tiles(bbar_re.transpose(0, 2, 1), bbar_im.transpose(0, 2, 1)).astype(BF16)
    t_c = _bd_tiles(f('s5_c_re'), -f('s5_c_im')).astype(BF16)
    ar8 = jnp.broadcast_to(ar.reshape(1, S5_GP), (SEG, S5_GP))
    ai8 = jnp.broadcast_to(ai.reshape(1, S5_GP), (SEG, S5_GP))
    pw_r, pw_i = _segment_powers(ar, ai, n // SEG)
    dskip = f('s5_d').reshape(1, S5_W)
    u_s5 = _permute_rows(za[:, :S5_W])
    xs = _s5_scan(u_s5, t_b, ar8, ai8, pw_r, pw_i, "s5_scan")
    ys_p = _bd_reduce(xs, t_c, _scale_rows(u_s5, dskip, "s5_skip"), "s5_y")
    ys = _unpermute_rows(ys_p)
    zgelu = _gelu_fwd(ys, "s5_gelu")
    t_glu = _mm(zgelu, f('s5_glu_w'), bias=f('s5_glu_b').reshape(1, S5_W), name="s5_glu_t")
    y_s5 = _glu_fwd(zgelu, t_glu, "s5_glu")
    wup = jnp.pad(f('gla_a_up_w'), ((0, LANE - GLA_RANK), (0, 0)))
    wup_h = wup.reshape(LANE, GLA_HEADS, GLA_DK).transpose(1, 0, 2)
    bup_h = f('gla_a_up_b').reshape(GLA_HEADS, 1, GLA_DK)
    gn_h = f('gla_out_norm').reshape(GLA_HEADS, 1, GLA_DV)
    y_gla, s_prev = carried("gla_fwd", _gla_fwd, za, al, wup_h, bup_h, gn_h, "gla_fwd")
    ps = _mm(y_s5, f('proj_s5'), name="proj_s5")
    pg = carried("proj_gla", _mm, y_gla, f('proj_gla'), name="proj_gla")
    merged = _merge_fwd(zg, ps, pg, "merge")
    h2 = _mm(merged, f('w_out'), res=h1, name="w_out")
    h3, ffn2 = _ffn_fwd(h2, f('ffn2_norm'), f('ffn2_w1'), f('ffn2_w3'), f('ffn2_w2'), "ffn2", plan)
    loss, dh3, g_final = _final_loss(h3, f('final_norm').reshape(1, D_MODEL), tgt, "loss")
    plan.loss = loss[0, 0]
    grads['final_norm'] = g_final.reshape(D_MODEL)
    dh2, grads['ffn2_norm'], grads['ffn2_w1'], grads['ffn2_w3'], grads['ffn2_w2'] = _ffn_bwd(
        dh3, ffn2, f('ffn2_norm'), f('ffn2_w1'), f('ffn2_w3'), f('ffn2_w2'), "ffn2", plan)
    dm = carried("d_merged", _mm, dh2, f('w_out'), tb=True, name="d_merged")
    grads['w_out'] = _mm(merged, dh2, ta=True, name="g_w_out")
    dps, dpg, dzg = carried("d_merge", _merge_bwd, dm, zg, ps, pg, "d_merge")
    grads['proj_s5'] = _mm(y_s5, dps, ta=True, name="g_proj_s5")
    grads['proj_gla'] = _mm(y_gla, dpg, ta=True, name="g_proj_gla")
    dy_s5 = _mm(dps, f('proj_s5'), tb=True, name="d_y_s5")
    dy_gla = _mm(dpg, f('proj_gla'), tb=True, name="d_y_gla")
    dzgelu, dt_glu, g_glu_b = _glu_bwd1(dy_s5, zgelu, t_glu, "d_glu")
    grads['s5_glu_b'] = g_glu_b.reshape(S5_W)
    grads['s5_glu_w'] = _mm(zgelu, dt_glu, ta=True, name="g_glu_w")
    dzgelu = _mm(dt_glu, f('s5_glu_w'), tb=True, res=dzgelu, name="d_gelu")
    dys, du_skip, g_d = _glu_bwd2(_permute_rows(dzgelu), ys_p, u_s5, dskip, "d_s5_y")
    grads['s5_d'] = g_d.reshape(S5_G, S5_H)
    lam, da8 = _s5_scan_bwd(dys, t_c, xs, ar8, ai8, pw_r, pw_i, "s5_scan_bwd")
    g_c = _bd_blocks(_bd_outer(dys, xs, "g_s5_c"))
    grads['s5_c_re'], grads['s5_c_im'] = g_c[0], -g_c[1]
    g_b = _bd_blocks(_bd_outer(u_s5, lam, "g_s5_b")).transpose(0, 1, 3, 2)
    g_bbar_re, g_bbar_im = g_b[0], g_b[1]
    da = jnp.sum(da8, axis=0)
    g_ar, g_ai = da[:S5_GP].reshape(S5_G, S5_P), da[S5_GP:].reshape(S5_G, S5_P)
    _, disc_vjp = jax.vjp(_s5_discretize, f('s5_lambda_re'), f('s5_lambda_im'), f('s5_log_dt'), f('s5_b_re'), f('s5_b_im'))
    (grads['s5_lambda_re'], grads['s5_lambda_im'], grads['s5_log_dt'], grads['s5_b_re'],
     grads['s5_b_im']) = disc_vjp((g_ar, g_ai, g_bbar_re, g_bbar_im))
    du_s5 = _unpermute_rows(_bd_reduce(lam, t_b, du_skip, "d_s5_u"))
    dza, dz, dgn, dbup = carried("gla_bwd", _gla_bwd, za, al, wup_h, bup_h, gn_h, s_prev, dy_gla, du_s5, "gla_bwd")
    grads['gla_out_norm'] = dgn.reshape(GLA_HEADS * GLA_DV)
    grads['gla_a_up_b'] = dbup.reshape(GLA_HEADS * GLA_DK)
    grads['gla_a_up_w'] = _unpad_heads(_mm(al, dz, ta=True, name="g_a_up")[:GLA_RANK])
    dal = _mm(dz, _pad_heads(wup), tb=True, out_dtype=BF16, name="d_a_low")
    g_wa = _mm(u, dza, ta=True, name="g_in_a")
    g_wg = _mm(u, dzg, ta=True, name="g_in_g")
    g_wal = _mm(u, dal, ta=True, name="g_in_al")
    grads['w_in'] = jnp.concatenate([g_wa[:, :512], _unpad_heads(g_wa[:, 512:1024]), _unpad_heads(g_wa[:, 1024:1536]),
                                     g_wa[:, 1536:], g_wal[:, :GLA_RANK], g_wg], axis=1)
    du = carried("d_u_a", _mm, dza, w_a, tb=True, name="d_u_a")
    du = _mm(dzg, w_g, tb=True, res=du, name="d_u_g")
    du = _mm(dal, w_al, tb=True, res=du, name="d_u_al")
    dh1, g_mix = carried("d_mix_rms", _rms_bwd, h1, f('mix_norm'), du, dh2, "d_mix_rms")
    grads['mix_norm'] = g_mix
    dx, grads['ffn1_norm'], grads['ffn1_w1'], grads['ffn1_w3'], grads['ffn1_w2'] = _ffn_bwd(
        dh1, ffn1, f('ffn1_norm'), f('ffn1_w1'), f('ffn1_w3'), f('ffn1_w2'), "ffn1", plan)
    return loss[0, 0], dx


MIXER_WEIGHTS = ['w_in', 's5_glu_w', 'proj_s5', 'proj_gla', 'w_out', 'gla_a_up_w']
FFN1_WEIGHTS, FFN2_WEIGHTS = FFN_WEIGHTS[:3], FFN_WEIGHTS[3:]
TRANSPOSED = ['ffn1_w1', 'ffn1_w3', 'ffn2_w1', 'ffn2_w3']


def _local_shard(w, nm):
    return jnp.swapaxes(w, 1, 2)[0] if nm in TRANSPOSED else w[0]
FFN1_EARLY = ['ffn1_w2']
FFN1_LATE = ['ffn1_w1', 'ffn1_w3']
GRAD_GROUPS = {'ffn2': FFN2_WEIGHTS, 'mixer': ['w_out', 'proj_s5', 'proj_gla', 's5_glu_w', 'w_in'], 'ffn1': FFN1_WEIGHTS}


class _Plan:
    def __init__(self, a, c_arr, s_arr):
        self.a, self.c_arr, self.s_arr = a, c_arr, s_arr
        self.grads, self.weights, self.riding = {}, {}, {}
        self.g4s, self.chip_sums, self.halves, self.sib_halves = {}, {}, {}, {}
        for nm in SMALL:
            if nm != 'gla_a_up_w':
                self.weights[nm] = a[nm] if nm == 'final_norm' else a[nm][0]
        ici = _side_gather_ici(self._shards(FFN1_WEIGHTS))
        _run_side(ici, "gather_ffn1_ici")
        self._gathered(FFN1_WEIGHTS, _run_side(_side_gather_d2d(ici.outs), "gather_ffn1_d2d"))

    def _shards(self, names):
        return [_local_shard(self.a[nm], nm).astype(F32 if nm == 'gla_a_up_w' else BF16) for nm in names]

    def _gathered(self, names, arrs):
        for nm, g4 in zip(names, arrs):
            if nm in FFN_WEIGHTS:
                self.weights[nm] = g4
            elif nm in COL_SHARDED:
                self.weights[nm] = jnp.concatenate([g4[s] for s in range(4)], axis=1)
            else:
                self.weights[nm] = g4.reshape(4 * g4.shape[1], g4.shape[2])

    def get(self, name):
        return self.weights[name]

    def _shard_major(self, nm):
        g = self.grads[nm]
        if nm in FFN_WEIGHTS:
            return g
        if nm in COL_SHARDED:
            return jnp.stack(jnp.split(g, 4, axis=1))
        return g.reshape(4, g.shape[0] // 4, g.shape[1])

    def _schedule(self, tag):
        grp = GRAD_GROUPS
        gathers = {"ffn1_fwd": ('ici', MIXER_WEIGHTS), "mix_rms": ('d2d', MIXER_WEIGHTS),
                   "in_a": ('ici', FFN2_WEIGHTS[:1]), "in_g": ('d2d', FFN2_WEIGHTS[:1]),
                   "gla_fwd": ('ici', FFN2_WEIGHTS[1:]), "proj_gla": ('d2d', FFN2_WEIGHTS[1:])}
        if tag in gathers:
            kind, names = gathers[tag]
            key = tuple(names)
            if kind == 'ici':
                return [(_side_gather_ici(self._shards(names)), lambda outs: self.riding.update({key: outs}))]
            return [(_side_gather_d2d(self.riding[key]), lambda outs: self._gathered(names, outs))]
        steps = {"ffn2_gw1": (['ffn2_w2'], 0), "ffn2_gw3": (['ffn2_w1'], 0), "d_merged": (['ffn2_w3'], 0),
                 "gla_bwd": (grp['ffn2'], 1), "d_mix_rms": (grp['ffn2'], 2),
                 "d_u_a": (grp['mixer'], 0), "ffn1_bwd": (grp['mixer'], 1), "ffn1_gw2": (grp['mixer'], 2),
                 "ffn1_gw1": (FFN1_EARLY, 0), "ffn1_gw3": (FFN1_EARLY, 1), "adamw_early": (FFN1_LATE, 1)}
        entries = [self._reduce_stage(*steps[tag])] if tag in steps else []
        if tag == "ffn1_gw2":
            entries.append(self._small_stage(0))
        if tag == "ffn1_gw1":
            entries.append(self._small_stage(1))
        return entries

    def _small_stage(self, stage):
        if stage == 0:
            a, grads = self.a, self.grads
            self.small_parts = ([grads[nm].reshape(a[nm].shape) for nm in SMALL if nm != 'gla_a_up_w']
                                + [grads['gla_a_up_w'], self.loss.reshape(1)])
            packed = _pack_small(self.small_parts)

            def done(outs):
                self.small_pair = _small_add(packed, outs[0], "small_sum_pair")
            return _side_small_sibling(packed), done

        def done(outs):
            self.small_total = _small_add(self.small_pair, outs[0], "small_sum_chips")
        return _side_small_chips(self.small_pair), done

    def _reduce_stage(self, names, stage):
        if stage == 0:
            for nm in names:
                self.g4s[nm] = self._shard_major(nm)

            def done(outs):
                sums = _chip_sums([self.g4s[nm] for nm in names], outs, self.c_arr, f"chip_sum_{names[0]}")
                self.chip_sums.update(zip(names, sums))
            return _side_swap_halves([self.g4s[nm] for nm in names]), done
        if stage == 1:
            def done(outs):
                halves = _owner_sums([self.chip_sums[nm] for nm in names], outs, self.s_arr, f"owner_sum_{names[0]}")
                self.halves.update(zip(names, halves))
            return _side_scatter([self.chip_sums[nm] for nm in names]), done

        def done(outs):
            self.sib_halves.update(zip(names, outs))
        return _side_swap_reduced([self.halves[nm] for nm in names]), done

    def before(self, tag):
        entries = self._schedule(tag)
        if entries:
            merged = _merge_sides([side for side, _ in entries])
            self.riding[tag] = (merged, entries)
            _RIDER.append(merged)

    def after(self, tag):
        if tag in self.riding:
            merged, entries = self.riding.pop(tag)
            assert not _RIDER and merged.outs is not None, tag
            pos = 0
            for side, done in entries:
                done(merged.outs[pos:pos + len(side.out_shapes)])
                pos += len(side.out_shapes)

    def finish_alone(self, stage):
        names = FFN1_LATE if stage == 0 else GRAD_GROUPS['ffn1']
        side, done = self._reduce_stage(names, stage)
        done(_run_side(side, f"grad_ffn1_stage{stage}"))


def _train_step(a):
    x = a['x'][0]
    tgt = a['loss_target'][0]
    xi, yi, ci = lax.axis_index("x"), lax.axis_index("y"), lax.axis_index("c")
    c_arr = jnp.reshape(ci, (1,)).astype(jnp.int32)
    s_arr = jnp.reshape(2 * xi + yi, (1,)).astype(jnp.int32)
    plan = _Plan(a, c_arr, s_arr)
    loss, dx = _local_step(x, tgt, plan)
    red = {}
    small_sum = _unpack_small(plan.small_total, plan.small_parts)
    small_names = [nm for nm in SMALL if nm != 'gla_a_up_w']
    for nm, g in zip(small_names, small_sum[:-2]):
        red[nm] = g
    loss = small_sum[-1].reshape(())
    g_up = small_sum[-2]
    red['gla_a_up_w'] = lax.dynamic_slice(g_up, (0, (2 * xi + yi) * GLA_DK), (GLA_RANK, GLA_DK))
    out_g, out_d, out_m, out_v = {}, {}, {}, {}

    def update(names, tag):
        items = [(_local_shard(a[nm], nm), plan.halves[nm], plan.sib_halves[nm], _local_shard(a['m_' + nm], nm),
                  _local_shard(a['v_' + nm], nm)) for nm in names]
        plan.before(tag)
        res = _adamw_group(items, c_arr, tag)
        plan.after(tag)
        for k, nm in enumerate(names):
            back = (lambda t: jnp.swapaxes(t[None], 1, 2)) if nm in TRANSPOSED else (lambda t: t[None])
            out_g[nm], out_d[nm], out_m[nm], out_v[nm] = (back(t) for t in res[4 * k:4 * k + 4])

    plan.finish_alone(0)
    update([nm for nm in SHARDED if nm not in GRAD_GROUPS['ffn1']], "adamw_early")
    plan.finish_alone(2)
    update(GRAD_GROUPS['ffn1'], "adamw_ffn1")
    rest = [nm for nm in WEIGHTS if nm not in SHARDED]
    pk = lambda pre: _pack_small([a[pre + nm] for nm in rest])
    d, nm_, nv_ = _adamw(pk(''), _pack_small([red[nm] for nm in rest]), pk('m_'), pk('v_'), "adamw_small")
    like = [a[nm] for nm in rest]
    for nm, g, dd, mm_, vv_ in zip(rest, [red[nm].reshape(a[nm].shape) for nm in rest], _unpack_small(d, like),
                                   _unpack_small(nm_, like), _unpack_small(nv_, like)):
        out_g[nm], out_d[nm], out_m[nm], out_v[nm] = g, dd, mm_, vv_
    return (loss, dx[None], *[out_g[nm] for nm in WEIGHTS], *[out_d[nm] for nm in WEIGHTS],
            *[out_m[nm] for nm in WEIGHTS], *[out_v[nm] for nm in WEIGHTS])


def kernel(x, ffn1_norm, ffn1_w1, ffn1_w3, ffn1_w2, mix_norm, w_in, s5_lambda_re, s5_lambda_im, s5_log_dt, s5_b_re, s5_b_im, s5_c_re, s5_c_im, s5_d, s5_glu_w, s5_glu_b, gla_a_up_w, gla_a_up_b, gla_out_norm, proj_s5, proj_gla, w_out, ffn2_norm, ffn2_w1, ffn2_w3, ffn2_w2, final_norm, loss_target, m_ffn1_norm, m_ffn1_w1, m_ffn1_w3, m_ffn1_w2, m_mix_norm, m_w_in, m_s5_lambda_re, m_s5_lambda_im, m_s5_log_dt, m_s5_b_re, m_s5_b_im, m_s5_c_re, m_s5_c_im, m_s5_d, m_s5_glu_w, m_s5_glu_b, m_gla_a_up_w, m_gla_a_up_b, m_gla_out_norm, m_proj_s5, m_proj_gla, m_w_out, m_ffn2_norm, m_ffn2_w1, m_ffn2_w3, m_ffn2_w2, m_final_norm, v_ffn1_norm, v_ffn1_w1, v_ffn1_w3, v_ffn1_w2, v_mix_norm, v_w_in, v_s5_lambda_re, v_s5_lambda_im, v_s5_log_dt, v_s5_b_re, v_s5_b_im, v_s5_c_re, v_s5_c_im, v_s5_d, v_s5_glu_w, v_s5_glu_b, v_gla_a_up_w, v_gla_a_up_b, v_gla_out_norm, v_proj_s5, v_proj_gla, v_w_out, v_ffn2_norm, v_ffn2_w1, v_ffn2_w3, v_ffn2_w2, v_final_norm):
    return _train_step(dict(locals()))
```

```python
import functools

import jax
import jax.numpy as jnp
from jax import lax
from jax.experimental import pallas as pl
from jax.experimental.pallas import tpu as pltpu

F32 = jnp.float32
BF16 = jnp.bfloat16
HI = lax.Precision.HIGHEST
MESH_ID = pl.DeviceIdType.MESH

D_MODEL = 1024
EPS = 1e-6
S5_G, S5_P, S5_H = 32, 64, 16
S5_W = S5_G * S5_H
S5_GP = S5_G * S5_P
SEG = 8
SCAN_ROWS = 256
GLA_HEADS, GLA_DK, GLA_DV = 4, 64, 128
GLA_CHUNK = 64
GLA_TAU = 16.0
GLA_RANK = 16
ADAM_LR, ADAM_B1, ADAM_B2, ADAM_EPS, ADAM_WD, ADAM_STEP = 0.001, 0.9, 0.999, 1e-08, 0.01, 10
V7X_VMEM_LIMIT = 56 * 1024 * 1024
LANE = 128

WEIGHTS = ['ffn1_norm', 'ffn1_w1', 'ffn1_w3', 'ffn1_w2', 'mix_norm', 'w_in', 's5_lambda_re', 's5_lambda_im',
           's5_log_dt', 's5_b_re', 's5_b_im', 's5_c_re', 's5_c_im', 's5_d', 's5_glu_w', 's5_glu_b', 'gla_a_up_w',
           'gla_a_up_b', 'gla_out_norm', 'proj_s5', 'proj_gla', 'w_out', 'ffn2_norm', 'ffn2_w1', 'ffn2_w3',
           'ffn2_w2', 'final_norm']
SHARDED = ['ffn1_w1', 'ffn1_w3', 'ffn1_w2', 'w_in', 's5_glu_w', 'proj_s5', 'proj_gla', 'w_out',
           'ffn2_w1', 'ffn2_w3', 'ffn2_w2']
COL_SHARDED = ['ffn1_w1', 'ffn1_w3', 'w_in', 'proj_s5', 'proj_gla', 'ffn2_w1', 'ffn2_w3', 'gla_a_up_w']
SMALL = [n for n in WEIGHTS if n not in SHARDED]
FFN_WEIGHTS = ['ffn1_w1', 'ffn1_w3', 'ffn1_w2', 'ffn2_w1', 'ffn2_w3', 'ffn2_w2']


def _params(**kw):
    return pltpu.CompilerParams(vmem_limit_bytes=V7X_VMEM_LIMIT, **kw)


class _Side:
    def __init__(self, ins, out_shapes, nsem, copies, aliased=False):
        self.ins, self.out_shapes, self.nsem, self.copies, self.aliased = list(ins), list(out_shapes), nsem, copies, aliased
        self.outs = None


_RIDER = []


def _pcall(body, **kw):
    if _RIDER:
        return _carry(body, _RIDER.pop(), **kw)
    return pl.pallas_call(body, **kw)


def _carry(body, side, *, name, grid, in_specs, out_specs, out_shape, scratch_shapes=(), compiler_params=None):
    del compiler_params
    single = not isinstance(out_shape, (list, tuple))
    out_specs = [out_specs] if single else list(out_specs)
    out_shape = [out_shape] if single else list(out_shape)
    n_in, n_out, n_scr = len(in_specs), len(out_shape), len(scratch_shapes)
    s_in, s_out = len(side.ins), len(side.out_shapes)
    any_spec = pl.BlockSpec(memory_space=pl.ANY)

    def wrapped(*refs):
        cuts = [n_in, s_in, n_out, s_out, n_scr]
        parts, pos = [], 0
        for c in cuts:
            parts.append(refs[pos:pos + c])
            pos += c
        ins, sins, outs, souts, scr = parts
        ssem, rsem = refs[pos], refs[pos + 1]
        first = last = None
        for d, g in enumerate(grid):
            i = pl.program_id(d)
            first = (i == 0) if first is None else first & (i == 0)
            last = (i == g - 1) if last is None else last & (i == g - 1)

        @pl.when(first)
        def _():
            for cp in side.copies(sins, souts, ssem, rsem):
                cp.start()

        body(*ins, *outs, *scr)

        @pl.when(last)
        def _():
            for cp in side.copies(sins, souts, ssem, rsem):
                cp.wait()

    call = pl.pallas_call(
        wrapped, name=name, grid=grid, in_specs=list(in_specs) + [any_spec] * s_in,
        out_specs=out_specs + [any_spec] * s_out, out_shape=out_shape + side.out_shapes,
        scratch_shapes=list(scratch_shapes) + [pltpu.SemaphoreType.DMA((side.nsem,)), pltpu.SemaphoreType.DMA((side.nsem,))],
        input_output_aliases={n_in + j: n_out + j for j in range(s_in)} if side.aliased else {},
        compiler_params=_params(has_side_effects=True))

    def run(*args):
        res = call(*args, *side.ins)
        side.outs = list(res[n_out:])
        return res[0] if single else list(res[:n_out])

    return run


def _run_side(side, name):
    s_in, s_out = len(side.ins), len(side.out_shapes)
    any_spec = pl.BlockSpec(memory_space=pl.ANY)

    def body(*refs):
        sins, souts = refs[:s_in], refs[s_in:s_in + s_out]
        ssem, rsem = refs[s_in + s_out:]
        cps = side.copies(sins, souts, ssem, rsem)
        for cp in cps:
            cp.start()
        for cp in cps:
            cp.wait()

    side.outs = list(pl.pallas_call(
        body, name=name, in_specs=[any_spec] * s_in, out_specs=[any_spec] * s_out, out_shape=side.out_shapes,
        scratch_shapes=[pltpu.SemaphoreType.DMA((side.nsem,)), pltpu.SemaphoreType.DMA((side.nsem,))],
        input_output_aliases={j: j for j in range(s_in)} if side.aliased else {},
        compiler_params=pltpu.CompilerParams(has_side_effects=True))(*side.ins))
    return side.outs


def _pick(n, cap, quantum):
    if n <= cap:
        return n
    best = None
    for t in range(quantum, cap + 1, quantum):
        if n % t == 0:
            best = t
    assert best is not None, (n, cap, quantum)
    return best


def _sigmoid(x):
    return jax.nn.sigmoid(x)


def _mm(a, b, *, name, ta=False, tb=False, out_dtype=F32, alpha=1.0, res=None, bias=None, exact=False, shard=None):
    ns = 4
    (k_a, m) = a.shape[-2:] if ta else a.shape[-2:][::-1]
    (k_b, n) = b.shape[-2:][::-1] if tb else b.shape[-2:]
    assert k_a == k_b, (a.shape, b.shape, ta, tb)
    assert (a.ndim == 3) == (shard in ('k', 'm')) and (b.ndim == 3) == (shard in ('n', 'k'))
    k = k_a
    tm = _pick(m, 1024, 128)
    tn = _pick(n, 1024, 128)
    tk = _pick(k, 1024, 128)
    pm, pn, pk = m // tm, n // tn, k // tk
    gm = pm * (ns if shard == 'm' else 1)
    gn = pn * (ns if shard == 'n' else 1)
    gk = pk * (ns if shard == 'k' else 1)
    dims = (((0,) if ta else (1,), (1,) if tb else (0,)), ((), ()))
    op_dtype = F32 if exact else BF16

    def body(*refs):
        a_ref, b_ref = refs[0], refs[1]
        pos = 2
        res_ref = bias_ref = None
        if res is not None:
            res_ref = refs[pos]
            pos += 1
        if bias is not None:
            bias_ref = refs[pos]
            pos += 1
        o_ref, acc_ref = refs[pos], refs[pos + 1]
        kk = pl.program_id(2)

        @pl.when(kk == 0)
        def _():
            acc_ref[...] = jnp.zeros_like(acc_ref)

        acc_ref[...] += lax.dot_general(a_ref[...].astype(op_dtype), b_ref[...].astype(op_dtype), dims,
                                        precision=HI if exact else None, preferred_element_type=F32)

        @pl.when(kk == gk - 1)
        def _():
            o = acc_ref[...]
            if alpha != 1.0:
                o = o * alpha
            if bias_ref is not None:
                o = o + bias_ref[...]
            if res_ref is not None:
                o = o + res_ref[...]
            o_ref[...] = o.astype(out_dtype)

    def spec(block, sharded_on, order):
        per = {'m': pm, 'n': pn, 'k': pk}

        def index(i, j, kk):
            g = {'m': i, 'n': j, 'k': kk}
            r, c = order(i % pm if shard == 'm' else i, j % pn if shard == 'n' else j, kk % pk if shard == 'k' else kk)
            if sharded_on is None:
                return (r, c)
            return (g[sharded_on] // per[sharded_on], r, c)

        return pl.BlockSpec(block if sharded_on is None else (None,) + block, index)

    a_sh = shard if shard in ('k', 'm') else None
    b_sh = shard if shard in ('n', 'k') else None
    o_sh = shard if shard in ('n', 'm') else None
    a_spec = spec((tk, tm), a_sh, lambda i, j, kk: (kk, i)) if ta else spec((tm, tk), a_sh, lambda i, j, kk: (i, kk))
    b_spec = spec((tn, tk), b_sh, lambda i, j, kk: (j, kk)) if tb else spec((tk, tn), b_sh, lambda i, j, kk: (kk, j))
    ins, in_specs = [a, b], [a_spec, b_spec]
    if res is not None:
        assert o_sh is None
        ins.append(res)
        in_specs.append(pl.BlockSpec((tm, tn), lambda i, j, kk: (i, j)))
    if bias is not None:
        assert o_sh is None
        ins.append(bias)
        in_specs.append(pl.BlockSpec((1, tn), lambda i, j, kk: (0, j)))
    out_shape = (m, n) if o_sh is None else (ns, m, n)
    return _pcall(body, name=name, grid=(gm, gn, gk), in_specs=in_specs,
                  out_specs=spec((tm, tn), o_sh, lambda i, j, kk: (i, j)),
                  out_shape=jax.ShapeDtypeStruct(out_shape, out_dtype),
                  scratch_shapes=[pltpu.VMEM((tm, tn), F32)], compiler_params=_params())(*ins)


ROWS_VMEM_BUDGET = 24 * 1024 * 1024


def _rows(body, ins, outs, *, n, name):
    cols = sum(a.shape[1] for a, kind in ins if kind == 'r') + sum(c for c, _, kind in outs if kind == 'r')
    cap = 256
    while cap < 2048 and 2 * 4 * cols * (2 * cap) <= ROWS_VMEM_BUDGET:
        cap *= 2
    tm = _pick(n, cap, 16)
    in_specs = []
    for arr, kind in ins:
        if kind == 'r':
            in_specs.append(pl.BlockSpec((tm, arr.shape[1]), lambda i: (i, 0)))
        else:
            in_specs.append(pl.BlockSpec(arr.shape, lambda i: (0, 0)))
    out_specs, out_shape = [], []
    for cols, dtype, kind in outs:
        if kind == 'r':
            out_specs.append(pl.BlockSpec((tm, cols), lambda i: (i, 0)))
            out_shape.append(jax.ShapeDtypeStruct((n, cols), dtype))
        else:
            out_specs.append(pl.BlockSpec((1, cols), lambda i: (0, 0)))
            out_shape.append(jax.ShapeDtypeStruct((1, cols), dtype))
    n_in = len(ins)
    acc_ids = [j for j, o in enumerate(outs) if o[2] == 'a']

    def wrapped(*refs):
        if acc_ids:
            @pl.when(pl.program_id(0) == 0)
            def _():
                for j in acc_ids:
                    refs[n_in + j][...] = jnp.zeros_like(refs[n_in + j])
        body(*refs)

    res = _pcall(wrapped, name=name, grid=(n // tm,), in_specs=in_specs, out_specs=out_specs, out_shape=out_shape,
                 compiler_params=_params())(*[a for a, _ in ins])
    return res


def _rms_fwd(x, g, name):
    def body(x_ref, g_ref, o_ref):
        xv = x_ref[...]
        rstd = lax.rsqrt(jnp.mean(xv * xv, axis=-1, keepdims=True) + EPS)
        o_ref[...] = (xv * rstd * g_ref[...]).astype(BF16)
    return _rows(body, [(x, 'r'), (g, 'f')], [(x.shape[1], BF16, 'r')], n=x.shape[0], name=name)[0]


def _rms_bwd(x, g, dn, dres, name):
    def body(x_ref, g_ref, dn_ref, dres_ref, dx_ref, dg_ref):
        xv = x_ref[...]
        rstd = lax.rsqrt(jnp.mean(xv * xv, axis=-1, keepdims=True) + EPS)
        xh = xv * rstd
        dn = dn_ref[...]
        dg_ref[...] += jnp.sum(dn * xh, axis=0, keepdims=True)
        dxh = dn * g_ref[...]
        dx_ref[...] = dres_ref[...] + rstd * (dxh - xh * jnp.mean(dxh * xh, axis=-1, keepdims=True))
    d = x.shape[1]
    return _rows(body, [(x, 'r'), (g, 'f'), (dn, 'r'), (dres, 'r')], [(d, F32, 'r'), (d, F32, 'a')],
                 n=x.shape[0], name=name)


def _gelu_parts(y):
    c0 = 0.7978845608028654
    inner = c0 * (y + 0.044715 * y * y * y)
    th = jnp.tanh(inner)
    return th, c0 * (1.0 + 3.0 * 0.044715 * y * y)


def _gelu_fwd(y, name):
    def body(y_ref, o_ref):
        yv = y_ref[...]
        th, _ = _gelu_parts(yv)
        o_ref[...] = 0.5 * yv * (1.0 + th)
    return _rows(body, [(y, 'r')], [(y.shape[1], F32, 'r')], n=y.shape[0], name=name)[0]


def _glu_fwd(zg, t, name):
    def body(z_ref, t_ref, o_ref):
        o_ref[...] = (z_ref[...] * _sigmoid(t_ref[...])).astype(BF16)
    return _rows(body, [(zg, 'r'), (t, 'r')], [(zg.shape[1], BF16, 'r')], n=zg.shape[0], name=name)[0]


def _glu_bwd1(dy, zg, t, name):
    def body(dy_ref, z_ref, t_ref, dz_ref, dt_ref, db_ref):
        dyv, zv = dy_ref[...], z_ref[...]
        sg = _sigmoid(t_ref[...])
        dz_ref[...] = dyv * sg
        dt = dyv * zv * sg * (1.0 - sg)
        dt_ref[...] = dt.astype(BF16)
        db_ref[...] += jnp.sum(dt, axis=0, keepdims=True)
    w = zg.shape[1]
    return _rows(body, [(dy, 'r'), (zg, 'r'), (t, 'r')], [(w, F32, 'r'), (w, BF16, 'r'), (w, F32, 'a')],
                 n=zg.shape[0], name=name)


def _glu_bwd2(dzg, ys, u, dskip, name):
    def body(dz_ref, y_ref, u_ref, d_ref, dy_ref, du_ref, dd_ref):
        yv = y_ref[...]
        th, dinner = _gelu_parts(yv)
        dy = dz_ref[...] * (0.5 * (1.0 + th) + 0.5 * yv * (1.0 - th * th) * dinner)
        dy_ref[...] = dy.astype(BF16)
        du_ref[...] = dy * d_ref[...]
        dd_ref[...] += jnp.sum(dy * u_ref[...], axis=0, keepdims=True)
    w = ys.shape[1]
    return _rows(body, [(dzg, 'r'), (ys, 'r'), (u, 'r'), (dskip, 'f')], [(w, BF16, 'r'), (w, F32, 'r'), (w, F32, 'a')],
                 n=ys.shape[0], name=name)


def _scale_rows(u, dskip, name):
    def body(u_ref, d_ref, o_ref):
        o_ref[...] = u_ref[...] * d_ref[...]
    return _rows(body, [(u, 'r'), (dskip, 'f')], [(u.shape[1], F32, 'r')], n=u.shape[0], name=name)[0]


def _merge_fwd(zg, ps, pg, name):
    def body(z_ref, ps_ref, pg_ref, o_ref):
        zv = z_ref[...]
        o_ref[...] = (_sigmoid(zv[:, :D_MODEL]) * ps_ref[...] + _sigmoid(zv[:, D_MODEL:]) * pg_ref[...]).astype(BF16)
    return _rows(body, [(zg, 'r'), (ps, 'r'), (pg, 'r')], [(D_MODEL, BF16, 'r')], n=zg.shape[0], name=name)[0]


def _merge_bwd(dm, zg, ps, pg, name):
    def body(dm_ref, z_ref, ps_ref, pg_ref, dps_ref, dpg_ref, dz_ref):
        dmv, zv = dm_ref[...], z_ref[...]
        s1, s2 = _sigmoid(zv[:, :D_MODEL]), _sigmoid(zv[:, D_MODEL:])
        dps_ref[...] = (dmv * s1).astype(BF16)
        dpg_ref[...] = (dmv * s2).astype(BF16)
        dz_ref[:, :D_MODEL] = (dmv * ps_ref[...] * s1 * (1.0 - s1)).astype(BF16)
        dz_ref[:, D_MODEL:] = (dmv * pg_ref[...] * s2 * (1.0 - s2)).astype(BF16)
    return _rows(body, [(dm, 'r'), (zg, 'r'), (ps, 'r'), (pg, 'r')],
                 [(D_MODEL, BF16, 'r'), (D_MODEL, BF16, 'r'), (2 * D_MODEL, BF16, 'r')], n=zg.shape[0], name=name)


def _final_loss(h, g, tgt, name):
    def body(h_ref, g_ref, t_ref, loss_ref, dh_ref, dg_ref):
        hv = h_ref[...]
        rstd = lax.rsqrt(jnp.mean(hv * hv, axis=-1, keepdims=True) + EPS)
        xh = hv * rstd
        err = xh * g_ref[...] - t_ref[...]
        part = 0.5 * jnp.sum(jnp.mean(err * err, axis=-1, keepdims=True), axis=0, keepdims=True)
        loss_ref[...] += jnp.broadcast_to(part, loss_ref.shape)
        dout = err * (1.0 / hv.shape[1])
        dg_ref[...] += jnp.sum(dout * xh, axis=0, keepdims=True)
        dxh = dout * g_ref[...]
        dh_ref[...] = rstd * (dxh - xh * jnp.mean(dxh * xh, axis=-1, keepdims=True))
    d = h.shape[1]
    return _rows(body, [(h, 'r'), (g, 'f'), (tgt, 'r')], [(LANE, F32, 'a'), (d, F32, 'r'), (d, F32, 'a')],
                 n=h.shape[0], name=name)


def _adamw_math(wv, gv, mv, vv):
    nm = ADAM_B1 * mv + (1.0 - ADAM_B1) * gv
    nv = ADAM_B2 * vv + (1.0 - ADAM_B2) * (gv * gv)
    m_hat = nm / (1.0 - ADAM_B1 ** ADAM_STEP)
    v_hat = nv / (1.0 - ADAM_B2 ** ADAM_STEP)
    return -ADAM_LR * (m_hat / (jnp.sqrt(v_hat) + ADAM_EPS) + ADAM_WD * wv), nm, nv


def _adamw(w, g, m, v, name):
    def body(w_ref, g_ref, m_ref, v_ref, d_ref, nm_ref, nv_ref):
        d_ref[...], nm_ref[...], nv_ref[...] = _adamw_math(w_ref[...], g_ref[...], m_ref[...], v_ref[...])
    c = w.shape[1]
    return _rows(body, [(w, 'r'), (g, 'r'), (m, 'r'), (v, 'r')], [(c, F32, 'r')] * 3, n=w.shape[0], name=name)


ADAMW_BLOCKS = 8


def _adamw_group(items, c_arr, name):
    per = ADAMW_BLOCKS // 2
    n = len(items)

    def body(c_ref, *refs):
        mine = (pl.program_id(0) // per) == c_ref[0]
        for k in range(n):
            w_ref, go_ref, gs_ref, m_ref, v_ref = refs[5 * k:5 * k + 5]
            g_ref, d_ref, nm_ref, nv_ref = refs[5 * n + 4 * k:5 * n + 4 * k + 4]
            gv = jnp.where(mine, go_ref[...], gs_ref[...])
            g_ref[...] = gv
            d_ref[...], nm_ref[...], nv_ref[...] = _adamw_math(w_ref[...], gv, m_ref[...], v_ref[...])

    in_specs, out_specs, out_shape, args = [pl.BlockSpec(memory_space=pltpu.SMEM)], [], [], [c_arr]
    for item in items:
        r, cols = item[0].shape
        assert r % (8 * ADAMW_BLOCKS) == 0, item[0].shape
        tr = r // ADAMW_BLOCKS
        full = pl.BlockSpec((tr, cols), lambda i: (i, 0))
        half = pl.BlockSpec((tr, cols), lambda i: (i % per, 0))
        in_specs += [full, half, half, full, full]
        out_specs += [full] * 4
        out_shape += [jax.ShapeDtypeStruct((r, cols), F32)] * 4
        args += list(item)
    return _pcall(body, name=name, grid=(ADAMW_BLOCKS,), in_specs=in_specs, out_specs=out_specs, out_shape=out_shape,
                  compiler_params=_params())(*args)


def _shift_rows(v, sh, down):
    rolled = pltpu.roll(v, sh if down else v.shape[0] - sh, axis=0)
    row = lax.broadcasted_iota(jnp.int32, v.shape, 0)
    keep = (row >= sh) if down else (row < v.shape[0] - sh)
    return jnp.where(keep, rolled, 0.0)


def _chain_segments(st_r, st_i, pw_r_ref, pw_i_ref, conj, down):
    vr, vi = st_r[...], st_i[...]
    sh, k = 1, 0
    while sh < SEG:
        pr, pi = pw_r_ref[k:k + 1, :], pw_i_ref[k:k + 1, :]
        if conj:
            pi = -pi
        sr, si = _shift_rows(vr, sh, down), _shift_rows(vi, sh, down)
        vr, vi = vr + pr * sr - pi * si, vi + pr * si + pi * sr
        sh, k = sh * 2, k + 1
    st_r[...] = _shift_rows(vr, 1, down)
    st_i[...] = _shift_rows(vi, 1, down)


def _expand_block(u_ref, t_ref, bu_ref):
    for j in range(BD_TILES):
        k = j % 4
        bu_ref[:, j * BD_ST:(j + 1) * BD_ST] = _dot(u_ref[:, k * BD_CH:(k + 1) * BD_CH], t_ref[j])


def _s5_scan(u, tiles, ar8, ai8, pw_r, pw_i, name):
    n = u.shape[0]
    rb = SCAN_ROWS
    nb, steps, lc = n // rb, rb // SEG, 512

    def body(u_ref, t_ref, ar_ref, ai_ref, pwr_ref, pwi_ref, x_ref, st_r, st_i, bu_ref):
        ph, b = pl.program_id(0), pl.program_id(1)

        @pl.when((ph == 0) & (b == 0))
        def _():
            st_r[...] = jnp.zeros_like(st_r)
            st_i[...] = jnp.zeros_like(st_i)

        _expand_block(u_ref, t_ref, bu_ref)

        def scan(store):
            for c in range(S5_GP // lc):
                re, im = slice(c * lc, (c + 1) * lc), slice(S5_GP + c * lc, S5_GP + (c + 1) * lc)
                a_r, a_i = ar_ref[:, re], ai_ref[:, re]

                def step(s, carry):
                    xr, xi = carry
                    rows = pl.ds(pl.multiple_of(s * SEG, SEG), SEG)
                    nr = a_r * xr - a_i * xi + bu_ref[rows, re]
                    ni = a_r * xi + a_i * xr + bu_ref[rows, im]
                    if store:
                        x_ref[rows, re] = nr
                        x_ref[rows, im] = ni
                    return nr, ni

                xr, xi = lax.fori_loop(0, steps, step, (st_r[:, re], st_i[:, re]), unroll=4)
                st_r[:, re] = xr
                st_i[:, re] = xi

        @pl.when(ph == 0)
        def _():
            scan(False)

        @pl.when((ph == 0) & (b == nb - 1))
        def _():
            _chain_segments(st_r, st_i, pwr_ref, pwi_ref, conj=False, down=True)

        @pl.when(ph == 1)
        def _():
            scan(True)

    full = lambda a: pl.BlockSpec(a.shape, lambda ph, b: (0, 0))
    return _pcall(body, name=name, grid=(2, nb),
                  in_specs=[pl.BlockSpec((rb, S5_W), lambda ph, b: (b, 0)), pl.BlockSpec(tiles.shape, lambda ph, b: (0, 0, 0)),
                            full(ar8), full(ai8), full(pw_r), full(pw_i)],
                  out_specs=pl.BlockSpec((rb, 2 * S5_GP), lambda ph, b: (b * ph, 0)),
                  out_shape=jax.ShapeDtypeStruct((n, 2 * S5_GP), F32),
                  scratch_shapes=[pltpu.VMEM((SEG, S5_GP), F32), pltpu.VMEM((SEG, S5_GP), F32),
                                  pltpu.VMEM((rb, 2 * S5_GP), F32)],
                  compiler_params=_params())(u, tiles, ar8, ai8, pw_r, pw_i)


def _s5_scan_bwd(dy, tiles, xs, ar8, ai8, pw_r, pw_i, name):
    n = dy.shape[0]
    rb = SCAN_ROWS
    nb, steps, lc = n // rb, rb // SEG, 256

    def body(dy_ref, t_ref, x_ref, ar_ref, ai_ref, pwr_ref, pwi_ref, lam_ref, da_ref, st_r, st_i, gx_ref):
        ph, b = pl.program_id(0), pl.program_id(1)

        @pl.when((ph == 0) & (b == 0))
        def _():
            st_r[...] = jnp.zeros_like(st_r)
            st_i[...] = jnp.zeros_like(st_i)
            da_ref[...] = jnp.zeros_like(da_ref)

        _expand_block(dy_ref, t_ref, gx_ref)

        def scan(store):
            for c in range(S5_GP // lc):
                re, im = slice(c * lc, (c + 1) * lc), slice(S5_GP + c * lc, S5_GP + (c + 1) * lc)
                a_r, a_i = ar_ref[:, re], ai_ref[:, re]

                def step(s, carry):
                    rows = pl.ds(pl.multiple_of((steps - 1 - s) * SEG, SEG), SEG)
                    if store:
                        lr, li, dr, di = carry
                        xr, xi = x_ref[rows, re], x_ref[rows, im]
                        dr = dr + lr * xr + li * xi
                        di = di + li * xr - lr * xi
                    else:
                        lr, li = carry
                    nr = a_r * lr + a_i * li + gx_ref[rows, re]
                    ni = a_r * li - a_i * lr + gx_ref[rows, im]
                    if store:
                        lam_ref[rows, re] = nr
                        lam_ref[rows, im] = ni
                        return nr, ni, dr, di
                    return nr, ni

                if store:
                    lr, li, dr, di = lax.fori_loop(0, steps, step, (st_r[:, re], st_i[:, re], da_ref[:, re], da_ref[:, im]),
                                                   unroll=4)
                    da_ref[:, re] = dr
                    da_ref[:, im] = di
                else:
                    lr, li = lax.fori_loop(0, steps, step, (st_r[:, re], st_i[:, re]), unroll=4)
                st_r[:, re] = lr
                st_i[:, re] = li

        @pl.when(ph == 0)
        def _():
            scan(False)

        @pl.when((ph == 0) & (b == nb - 1))
        def _():
            _chain_segments(st_r, st_i, pwr_ref, pwi_ref, conj=True, down=False)

        @pl.when(ph == 1)
        def _():
            scan(True)

    full = lambda a: pl.BlockSpec(a.shape, lambda ph, b: (0, 0))
    rev = lambda ph, b: (nb - 1 - b, 0)
    return _pcall(body, name=name, grid=(2, nb),
                  in_specs=[pl.BlockSpec((rb, S5_W), rev), pl.BlockSpec(tiles.shape, lambda ph, b: (0, 0, 0)),
                            pl.BlockSpec((rb, 2 * S5_GP), lambda ph, b: ((nb - 1 - b) * ph, 0)),
                            full(ar8), full(ai8), full(pw_r), full(pw_i)],
                  out_specs=[pl.BlockSpec((rb, 2 * S5_GP), lambda ph, b: (nb - 1 - b * ph, 0)),
                             pl.BlockSpec((SEG, 2 * S5_GP), lambda ph, b: (0, 0))],
                  out_shape=[jax.ShapeDtypeStruct((n, 2 * S5_GP), F32), jax.ShapeDtypeStruct((SEG, 2 * S5_GP), F32)],
                  scratch_shapes=[pltpu.VMEM((SEG, S5_GP), F32), pltpu.VMEM((SEG, S5_GP), F32),
                                  pltpu.VMEM((rb, 2 * S5_GP), F32)],
                  compiler_params=_params())(dy, tiles, xs, ar8, ai8, pw_r, pw_i)


def _s5_discretize(lam_re, lam_im, log_dt, b_re, b_im):
    dt = jnp.exp(log_dt)[:, None]
    mag = jnp.exp(lam_re * dt)
    ar = mag * jnp.cos(lam_im * dt)
    ai = mag * jnp.sin(lam_im * dt)
    den = lam_re * lam_re + lam_im * lam_im
    nr = ar - 1.0
    fr = (nr * lam_re + ai * lam_im) / den
    fi = (ai * lam_re - nr * lam_im) / den
    bbar_re = fr[:, :, None] * b_re - fi[:, :, None] * b_im
    bbar_im = fr[:, :, None] * b_im + fi[:, :, None] * b_re
    return ar, ai, bbar_re, bbar_im


BD_TILES, BD_CH, BD_ST, BD_GROUPS = 8, 128, 512, 8
BD_ROWS = 4096


def _bd_tiles(re, im):
    eye = jnp.eye(BD_GROUPS, dtype=re.dtype)

    def tiles(t):
        t = t.reshape(S5_G // BD_GROUPS, BD_GROUPS, S5_H, S5_P)
        return (t[:, :, :, None, :] * eye[None, :, None, :, None]).reshape(S5_G // BD_GROUPS, BD_CH, BD_ST)

    return jnp.concatenate([tiles(re), tiles(im)], axis=0)


def _bd_blocks(t):
    t = t.reshape(2, S5_G // BD_GROUPS, BD_GROUPS, S5_H, BD_GROUPS, S5_P)
    return jnp.einsum('rkahap->rkahp', t).reshape(2, S5_G, S5_H, S5_P)


def _bd_reduce(x, t, res, name):
    n = x.shape[0]
    tm = _pick(n, BD_ROWS, 16)

    def body(x_ref, t_ref, r_ref, o_ref):
        part = _dot(x_ref[...], t_ref[...], NT)

        @pl.when(pl.program_id(2) == 0)
        def _():
            o_ref[...] = r_ref[...] + part

        @pl.when(pl.program_id(2) == 1)
        def _():
            o_ref[...] += part

    return _pcall(body, name=name, grid=(n // tm, 4, 2),
                  in_specs=[pl.BlockSpec((tm, BD_ST), lambda i, k, r: (i, k + 4 * r)),
                            pl.BlockSpec((None, BD_CH, BD_ST), lambda i, k, r: (k + 4 * r, 0, 0)),
                            pl.BlockSpec((tm, BD_CH), lambda i, k, r: (i, k))],
                  out_specs=pl.BlockSpec((tm, BD_CH), lambda i, k, r: (i, k)),
                  out_shape=jax.ShapeDtypeStruct((n, S5_W), F32), compiler_params=_params())(x, t, res)


def _bd_outer(a, x, name):
    n = a.shape[0]
    tk = _pick(n, BD_ROWS, 16)
    nk = n // tk

    def body(a_ref, x_ref, o_ref):
        part = _dot(a_ref[...], x_ref[...], TN)

        @pl.when(pl.program_id(1) == 0)
        def _():
            o_ref[...] = part

        @pl.when(pl.program_id(1) > 0)
        def _():
            o_ref[...] += part

    return _pcall(body, name=name, grid=(BD_TILES, nk),
                  in_specs=[pl.BlockSpec((tk, BD_CH), lambda j, kk: (kk, j % 4)), pl.BlockSpec((tk, BD_ST), lambda j, kk: (kk, j))],
                  out_specs=pl.BlockSpec((None, BD_CH, BD_ST), lambda j, kk: (j, 0, 0)),
                  out_shape=jax.ShapeDtypeStruct((BD_TILES, BD_CH, BD_ST), F32), compiler_params=_params())(a, x)


def _permute_rows(t):
    n = t.shape[0]
    return t.reshape(SEG, n // SEG, t.shape[1]).transpose(1, 0, 2).reshape(n, t.shape[1])


def _unpermute_rows(t):
    n = t.shape[0]
    return t.reshape(n // SEG, SEG, t.shape[1]).transpose(1, 0, 2).reshape(n, t.shape[1])


def _segment_powers(ar, ai, seg_steps):
    pr, pi = ar.reshape(1, S5_GP), ai.reshape(1, S5_GP)
    e = 1
    while e < seg_steps:
        pr, pi = pr * pr - pi * pi, 2.0 * pr * pi
        e *= 2
    assert e == seg_steps, "segment length must be a power of two"
    rows_r, rows_i = [], []
    for _ in range(3):
        rows_r.append(pr)
        rows_i.append(pi)
        pr, pi = pr * pr - pi * pi, 2.0 * pr * pi
    pad = jnp.zeros((SEG - 3, S5_GP), F32)
    return jnp.concatenate(rows_r + [pad], axis=0), jnp.concatenate(rows_i + [pad], axis=0)


NT = (((1,), (1,)), ((), ()))
TN = (((0,), (0,)), ((), ()))


def _dot(a, b, dims=None, exact=False):
    dims = (((1,), (0,)), ((), ())) if dims is None else dims
    if exact:
        return lax.dot_general(a, b, dims, precision=HI, preferred_element_type=F32)
    return lax.dot_general(a.astype(BF16), b.astype(BF16), dims, preferred_element_type=F32)


def _dot01(a, b, dims=None, ones_first=True):
    x = b if ones_first else a
    hi = x.astype(BF16)
    lo = (x - hi.astype(F32)).astype(BF16)
    parts = [(_dot(a, p, dims) if ones_first else _dot(p, b, dims)) for p in (lo, hi)]
    return parts[0] + parts[1]


HEADS = range(4)


def _gla_chunk_fwd(qc, kc, vc, al, wup, bup, s_prev, tril):
    ones = jnp.ones((GLA_CHUNK, GLA_DV), F32)
    z = [_dot(al, wup[h]) + bup[h] for h in HEADS]
    la = [(jnp.minimum(z[h], 0.0) - jnp.log(1.0 + jnp.exp(-jnp.abs(z[h])))) * (1.0 / GLA_TAU) for h in HEADS]
    bc = [_dot01(tril, la[h]) for h in HEADS]
    blb = [_dot01(la[h], ones, TN, ones_first=False) for h in HEADS]
    bl = [bc[h][GLA_CHUNK - 1:GLA_CHUNK, :] for h in HEADS]
    ebc = [jnp.exp(bc[h]) for h in HEADS]
    qt = [qc[h] * (GLA_DK ** -0.5) * ebc[h] for h in HEADS]
    kt = [kc[h] * jnp.exp(-bc[h]) for h in HEADS]
    ke = [kc[h] * jnp.exp(bl[h] - bc[h]) for h in HEADS]
    sc = [_dot(qt[h], kt[h], NT) * tril for h in HEADS]
    oi = [_dot(sc[h], vc[h]) for h in HEADS]
    oo = [_dot(qt[h], s_prev[h]) for h in HEADS]
    o = [oi[h] + oo[h] for h in HEADS]
    return z, bc, bl, blb, ebc, qt, kt, ke, sc, o


GLA_ROWS = 512
GLA_CPB = GLA_ROWS // GLA_CHUNK


ZA_COLS = 5 * 512
SLOT = 128


def _pad_heads(w):
    r = w.shape[0]
    return jnp.pad(w.reshape(r, GLA_HEADS, GLA_DK), ((0, 0), (0, 0), (0, SLOT - GLA_DK))).reshape(r, GLA_HEADS * SLOT)


def _unpad_heads(w):
    r = w.shape[0]
    return w.reshape(r, GLA_HEADS, SLOT)[:, :, :GLA_DK].reshape(r, GLA_HEADS * GLA_DK)


def _gla_token_specs(blk):
    col = lambda cb: pl.BlockSpec((GLA_ROWS, 512), lambda j: (blk(j), cb))
    whole = lambda a: pl.BlockSpec(a.shape, lambda j: (0,) * a.ndim)
    return col, whole


def _head_ds(h, width):
    return pl.ds(h * SLOT, width)


def _tri(lower):
    ri = lax.broadcasted_iota(jnp.int32, (GLA_CHUNK, GLA_CHUNK), 0)
    ci = lax.broadcasted_iota(jnp.int32, (GLA_CHUNK, GLA_CHUNK), 1)
    return ((ri >= ci) if lower else (ri <= ci)).astype(F32)


def _gla_fwd(za, al, wup, bup, gn, name):
    n = za.shape[0]
    nc = n // GLA_CHUNK

    def body(q_ref, k_ref, v_ref, r_ref, al_ref, wup_ref, bup_ref, gn_ref, y_ref, sp_ref, s_ref):
        @pl.when(pl.program_id(0) == 0)
        def _():
            s_ref[...] = jnp.zeros_like(s_ref)

        tril = _tri(True)

        def chunk(c, carry):
            rows = pl.ds(pl.multiple_of(c * GLA_CHUNK, GLA_CHUNK), GLA_CHUNK)
            alc = al_ref[rows, :]
            vc = [v_ref[rows, _head_ds(h, GLA_DV)] for h in HEADS]
            s_prev = [s_ref[h] for h in HEADS]
            _, _, _, blb, _, _, _, ke, _, o = _gla_chunk_fwd(
                [q_ref[rows, _head_ds(h, GLA_DK)] for h in HEADS], [k_ref[rows, _head_ds(h, GLA_DK)] for h in HEADS],
                vc, alc, [wup_ref[h] for h in HEADS], [bup_ref[h] for h in HEADS], s_prev, tril)
            ds = [_dot(ke[h], vc[h], TN) for h in HEADS]
            for h in HEADS:
                rc = r_ref[rows, _head_ds(h, GLA_DV)]
                sp_ref[h, c] = s_prev[h]
                rstd = lax.rsqrt(jnp.mean(o[h] * o[h], axis=-1, keepdims=True) + EPS)
                y_ref[rows, _head_ds(h, GLA_DV)] = (o[h] * rstd * gn_ref[h] * (rc * _sigmoid(rc))).astype(BF16)
                s_ref[h] = jnp.exp(blb[h]) * s_prev[h] + ds[h]
            return carry

        lax.fori_loop(0, GLA_CPB, chunk, 0)

    col, whole = _gla_token_specs(lambda j: j)
    return _pcall(body, name=name, grid=(n // GLA_ROWS,),
                  in_specs=[col(1), col(2), col(3), col(4), pl.BlockSpec((GLA_ROWS, LANE), lambda j: (j, 0)),
                            whole(wup), whole(bup), whole(gn)],
                  out_specs=[pl.BlockSpec((GLA_ROWS, GLA_HEADS * GLA_DV), lambda j: (j, 0)),
                             pl.BlockSpec((GLA_HEADS, GLA_CPB, GLA_DK, GLA_DV), lambda j: (0, j, 0, 0))],
                  out_shape=[jax.ShapeDtypeStruct((n, GLA_HEADS * GLA_DV), BF16),
                             jax.ShapeDtypeStruct((GLA_HEADS, nc, GLA_DK, GLA_DV), F32)],
                  scratch_shapes=[pltpu.VMEM((GLA_HEADS, GLA_DK, GLA_DV), F32)],
                  compiler_params=_params())(za, za, za, za, al, wup, bup, gn)


def _gla_bwd(za, al, wup, bup, gn, sp, dy, du_s5, name):
    n = za.shape[0]
    nb = n // GLA_ROWS

    def body(q_ref, k_ref, v_ref, r_ref, al_ref, wup_ref, bup_ref, gn_ref, dy_ref, dus_ref, sp_ref,
             dza_ref, dz_ref, dgn_ref, dbup_ref, ds_ref):
        @pl.when(pl.program_id(0) == 0)
        def _():
            ds_ref[...] = jnp.zeros_like(ds_ref)
            dgn_ref[...] = jnp.zeros_like(dgn_ref)
            dbup_ref[...] = jnp.zeros_like(dbup_ref)

        tril, triu = _tri(True), _tri(False)
        dza_ref[:, 0:512] = dus_ref[...].astype(BF16)
        dza_ref[:, 512:1536] = jnp.zeros((GLA_ROWS, 1024), BF16)
        dz_ref[...] = jnp.zeros_like(dz_ref)

        def chunk(i, carry):
            c = GLA_CPB - 1 - i
            rows = pl.ds(pl.multiple_of(c * GLA_CHUNK, GLA_CHUNK), GLA_CHUNK)
            alc = al_ref[rows, :]
            qc = [q_ref[rows, _head_ds(h, GLA_DK)] for h in HEADS]
            kc = [k_ref[rows, _head_ds(h, GLA_DK)] for h in HEADS]
            vc = [v_ref[rows, _head_ds(h, GLA_DV)] for h in HEADS]
            s_prev = [sp_ref[h, c] for h in HEADS]
            ds = [ds_ref[h] for h in HEADS]
            z, bc, bl, blb, ebc, qt, kt, ke, sc, o = _gla_chunk_fwd(
                qc, kc, vc, alc, [wup_ref[h] for h in HEADS], [bup_ref[h] for h in HEADS], s_prev, tril)
            do = []
            for h in HEADS:
                rc = r_ref[rows, _head_ds(h, GLA_DV)]
                rs = lax.rsqrt(jnp.mean(o[h] * o[h], axis=-1, keepdims=True) + EPS)
                on = o[h] * rs
                sr = _sigmoid(rc)
                sil = rc * sr
                dyv, gnv = dy_ref[rows, _head_ds(h, GLA_DV)], gn_ref[h]
                dgn_ref[h] += jnp.sum(dyv * on * sil, axis=0, keepdims=True)
                dza_ref[rows, pl.ds(2048 + h * SLOT, GLA_DV)] = (dyv * on * gnv * (sr * (1.0 + rc * (1.0 - sr)))).astype(BF16)
                don = dyv * gnv * sil
                do.append(rs * (don - on * jnp.mean(don * on, axis=-1, keepdims=True)))
            dp = [_dot(do[h], vc[h], NT) * tril for h in HEADS]
            dv1 = [_dot(sc[h], do[h], TN) for h in HEADS]
            dv2 = [_dot(ke[h], ds[h]) for h in HEADS]
            dq2 = [_dot(do[h], s_prev[h], NT) for h in HEADS]
            dke = [_dot(vc[h], ds[h], NT) for h in HEADS]
            ddec = [_dot01(jnp.ones((8, GLA_DV), F32), ds[h] * s_prev[h], NT)[0:1, :] for h in HEADS]
            dsn = [_dot(qt[h], do[h], TN) for h in HEADS]
            dq1 = [_dot(dp[h], kt[h]) for h in HEADS]
            dkt = [_dot(dp[h], qt[h], TN) for h in HEADS]
            dbc, dbl = [], []
            for h in HEADS:
                dqt = dq1[h] + dq2[h]
                dza_ref[rows, pl.ds(1536 + h * SLOT, GLA_DV)] = (dv1[h] + dv2[h]).astype(BF16)
                ds_ref[h] = jnp.exp(blb[h]) * ds[h] + dsn[h]
                dza_ref[rows, pl.ds(512 + h * SLOT, GLA_DK)] = (dqt * (GLA_DK ** -0.5) * ebc[h]).astype(BF16)
                dza_ref[rows, pl.ds(1024 + h * SLOT, GLA_DK)] = (dkt[h] * jnp.exp(-bc[h])
                                                                 + dke[h] * jnp.exp(bl[h] - bc[h])).astype(BF16)
                dbc.append(dqt * qt[h] - dkt[h] * kt[h] - dke[h] * ke[h])
                dbl.append(jnp.sum(dke[h] * ke[h], axis=0, keepdims=True) + ddec[h] * jnp.exp(bl[h]))
            dla = [_dot01(triu, dbc[h]) + dbl[h] for h in HEADS]
            for h in HEADS:
                dz = dla[h] * (1.0 - _sigmoid(z[h])) * (1.0 / GLA_TAU)
                dz_ref[rows, _head_ds(h, GLA_DK)] = dz
                dbup_ref[h] += jnp.sum(dz, axis=0, keepdims=True)
            return carry

        lax.fori_loop(0, GLA_CPB, chunk, 0)

    rev = lambda j: nb - 1 - j
    col, whole = _gla_token_specs(rev)
    tok = lambda w: pl.BlockSpec((GLA_ROWS, w), lambda j: (rev(j), 0))
    h1 = lambda w: pl.BlockSpec((GLA_HEADS, 1, w), lambda j: (0, 0, 0))
    s1 = lambda w: jax.ShapeDtypeStruct((GLA_HEADS, 1, w), F32)
    return _pcall(body, name=name, grid=(nb,),
                  in_specs=[col(1), col(2), col(3), col(4), tok(LANE), whole(wup), whole(bup), whole(gn), tok(512), tok(512),
                            pl.BlockSpec((GLA_HEADS, GLA_CPB, GLA_DK, GLA_DV), lambda j: (0, rev(j), 0, 0))],
                  out_specs=[tok(ZA_COLS), tok(GLA_HEADS * SLOT), h1(GLA_DV), h1(GLA_DK)],
                  out_shape=[jax.ShapeDtypeStruct((n, ZA_COLS), BF16), jax.ShapeDtypeStruct((n, GLA_HEADS * SLOT), F32),
                             s1(GLA_DV), s1(GLA_DK)],
                  scratch_shapes=[pltpu.VMEM((GLA_HEADS, GLA_DK, GLA_DV), F32)],
                  compiler_params=_params())(za, za, za, za, al, wup, bup, gn, dy, du_s5, sp)


ANY = pl.BlockSpec(memory_space=pl.ANY)


def _place():
    x, y, c = lax.axis_index("x"), lax.axis_index("y"), lax.axis_index("c")
    chips = [(1 - x, y), (x, 1 - y), (1 - x, 1 - y)]
    return x, y, c, chips


def _remote(src, dst, ssem, rsem, dev):
    return pltpu.make_async_remote_copy(src_ref=src, dst_ref=dst, send_sem=ssem, recv_sem=rsem, device_id=dev,
                                        device_id_type=MESH_ID)


def _half(c, rows):
    h = rows // 2
    return pl.ds(pl.multiple_of(c * h, 8), h)


def _side_gather_ici(shards):
    def copies(ins, outs, ssem, rsem):
        x, y, c, chips = _place()
        mine = 2 * x + y
        cps = []
        for w in range(len(ins)):
            half = _half(c, ins[w].shape[0])
            cps.append(_remote(ins[w], outs[w].at[mine], ssem.at[4 * w], rsem.at[4 * w], (x, y, 1 - c)))
            for k, (px, py) in enumerate(chips):
                cps.append(_remote(ins[w].at[half], outs[w].at[mine, half], ssem.at[4 * w + 1 + k], rsem.at[4 * w + 1 + k],
                                   (px, py, c)))
        return cps

    return _Side(shards, [jax.ShapeDtypeStruct((4,) + s.shape, s.dtype) for s in shards], 4 * len(shards), copies)


def _side_gather_d2d(gathered):
    def copies(ins, outs, ssem, rsem):
        x, y, c, chips = _place()
        cps = []
        for w in range(len(outs)):
            half = _half(c, outs[w].shape[1])
            for k, (px, py) in enumerate(chips):
                theirs = outs[w].at[2 * px + py, half]
                cps.append(_remote(theirs, theirs, ssem.at[3 * w + k], rsem.at[3 * w + k], (x, y, 1 - c)))
        return cps

    return _Side(gathered, [jax.ShapeDtypeStruct(g.shape, g.dtype) for g in gathered], 3 * len(gathered), copies,
                 aliased=True)


def _side_swap_halves(grads):
    def copies(ins, outs, ssem, rsem):
        x, y, c, _ = _place()
        return [_remote(ins[w].at[:, _half(1 - c, ins[w].shape[1]), :], outs[w], ssem.at[w], rsem.at[w], (x, y, 1 - c))
                for w in range(len(ins))]

    return _Side(grads, [jax.ShapeDtypeStruct((4, g.shape[1] // 2, g.shape[2]), g.dtype) for g in grads], len(grads), copies)


def _side_scatter(sums):
    def copies(ins, outs, ssem, rsem):
        x, y, c, chips = _place()
        return [_remote(ins[w].at[2 * px + py], outs[w].at[k], ssem.at[3 * w + k], rsem.at[3 * w + k], (px, py, c))
                for w in range(len(ins)) for k, (px, py) in enumerate(chips)]

    return _Side(sums, [jax.ShapeDtypeStruct((3,) + s.shape[1:], s.dtype) for s in sums], 3 * len(sums), copies)


def _side_swap_reduced(halves):
    def copies(ins, outs, ssem, rsem):
        x, y, c, _ = _place()
        return [_remote(ins[w], outs[w], ssem.at[w], rsem.at[w], (x, y, 1 - c)) for w in range(len(ins))]

    return _Side(halves, [jax.ShapeDtypeStruct(h.shape, h.dtype) for h in halves], len(halves), copies)


SUM_BLOCKS = 2


def _chip_sums(gs, recvs, c_arr, name):
    n = len(gs)

    def body(c_ref, *refs):
        for k in range(n):
            refs[2 * n + k][...] = (refs[2 * k][...] + refs[2 * k + 1][...]).astype(BF16)

    in_specs, out_specs, out_shape, args = [], [], [], []
    for g, recv in zip(gs, recvs):
        _, r, cols = g.shape
        h = r // 2
        assert h % (16 * SUM_BLOCKS) == 0, g.shape
        tr = h // SUM_BLOCKS
        in_specs += [pl.BlockSpec((None, None, tr, cols), lambda s, i, c_ref: (s, c_ref[0], i, 0)),
                     pl.BlockSpec((None, tr, cols), lambda s, i, c_ref: (s, i, 0))]
        out_specs.append(pl.BlockSpec((None, tr, cols), lambda s, i, c_ref: (s, i, 0)))
        out_shape.append(jax.ShapeDtypeStruct((4, h, cols), BF16))
        args += [g.reshape(4, 2, h, cols), recv]
    grid_spec = pltpu.PrefetchScalarGridSpec(num_scalar_prefetch=1, grid=(4, SUM_BLOCKS), in_specs=in_specs,
                                             out_specs=out_specs)
    return _pcall(body, name=name, grid_spec=grid_spec, out_shape=out_shape, compiler_params=_params())(c_arr, *args)


def _owner_sums(sums, others, s_arr, name):
    n = len(sums)

    def body(s_ref, *refs):
        f = lambda v: v.astype(F32)
        for k in range(n):
            a_ref, o_ref = refs[2 * k], refs[2 * k + 1]
            refs[2 * n + k][...] = (f(a_ref[...]) + f(o_ref[0])) + (f(o_ref[1]) + f(o_ref[2]))

    in_specs, out_specs, out_shape, args = [], [], [], []
    for sm, ot in zip(sums, others):
        _, h, cols = sm.shape
        tr = h // SUM_BLOCKS
        in_specs += [pl.BlockSpec((None, tr, cols), lambda i, s_ref: (s_ref[0], i, 0)),
                     pl.BlockSpec((3, tr, cols), lambda i, s_ref: (0, i, 0))]
        out_specs.append(pl.BlockSpec((tr, cols), lambda i, s_ref: (i, 0)))
        out_shape.append(jax.ShapeDtypeStruct((h, cols), F32))
        args += [sm, ot]
    grid_spec = pltpu.PrefetchScalarGridSpec(num_scalar_prefetch=1, grid=(SUM_BLOCKS,), in_specs=in_specs,
                                             out_specs=out_specs)
    return _pcall(body, name=name, grid_spec=grid_spec, out_shape=out_shape, compiler_params=_params())(s_arr, *args)


def _side_small_sibling(v):
    def copies(ins, outs, ssem, rsem):
        x, y, c, _ = _place()
        return [_remote(ins[0], outs[0], ssem.at[0], rsem.at[0], (x, y, 1 - c))]

    return _Side([v], [jax.ShapeDtypeStruct(v.shape, F32)], 1, copies)


def _side_small_chips(v):
    def copies(ins, outs, ssem, rsem):
        x, y, c, chips = _place()
        return [_remote(ins[0], outs[0].at[k], ssem.at[k], rsem.at[k], (px, py, c)) for k, (px, py) in enumerate(chips)]

    return _Side([v], [jax.ShapeDtypeStruct((3,) + v.shape, F32)], 3, copies)


def _small_add(v, r, name):
    def body(v_ref, r_ref, o_ref):
        if r.ndim == 2:
            o_ref[...] = v_ref[...] + r_ref[...]
        else:
            o_ref[...] = (v_ref[...] + r_ref[0]) + (r_ref[1] + r_ref[2])

    vm = pl.BlockSpec(memory_space=pltpu.VMEM)
    return _pcall(body, name=name, in_specs=[vm, vm], out_specs=vm, out_shape=jax.ShapeDtypeStruct(v.shape, F32),
                  compiler_params=_params())(v, r)


def _merge_sides(sides):
    if len(sides) == 1:
        return sides[0]

    def copies(in_refs, out_refs, ssem, rsem):
        cps, i, o, q = [], 0, 0, 0
        for s in sides:
            ni, no = len(s.ins), len(s.out_shapes)
            cps += s.copies(in_refs[i:i + ni], out_refs[o:o + no], ssem.at[pl.ds(q, s.nsem)], rsem.at[pl.ds(q, s.nsem)])
            i, o, q = i + ni, o + no, q + s.nsem
        return cps

    assert not any(s.aliased for s in sides)
    return _Side(sum((s.ins for s in sides), []), sum((s.out_shapes for s in sides), []), sum(s.nsem for s in sides), copies)


def _tile_rows(size):
    return -(-size // (8 * LANE)) * 8


def _pack_small(parts):
    pieces = []
    for p in parts:
        flat = p.reshape(-1).astype(F32)
        pieces.append(jnp.pad(flat, (0, _tile_rows(p.size) * LANE - p.size)).reshape(-1, LANE))
    rows = sum(x.shape[0] for x in pieces)
    pieces.append(jnp.zeros(((-rows) % 64, LANE), F32))
    return jnp.concatenate(pieces, axis=0)


def _unpack_small(packed, like):
    out, pos = [], 0
    for p in like:
        rows = _tile_rows(p.size)
        out.append(packed[pos:pos + rows].reshape(-1)[:p.size].reshape(p.shape))
        pos += rows
    return out


FFN_FWD_ROWS, FFN_BWD_ROWS = 1024, 512
FFN_SUB_ROWS = 256


def _ffn_specs(n, d, fs, cap):
    rows = _pick(n, cap, 16)
    row = pl.BlockSpec((rows, d), lambda i, s: (i, 0))
    gain = pl.BlockSpec((1, d), lambda i, s: (0, 0))
    w_row = pl.BlockSpec((None, fs, d), lambda i, s: (s, 0, 0))
    hid = pl.BlockSpec((None, rows, fs), lambda i, s: (s, i, 0))
    return rows, row, gain, w_row, hid


def _ffn_fwd(h, g, w1t, w3t, w2, tag, plan):
    n, d = h.shape
    ns, fs, _ = w2.shape
    rows, row, gain, w_row, hid = _ffn_specs(n, d, fs, FFN_FWD_ROWS)
    sub = rows

    total = (n // rows) * ns

    def body(h_ref, g_ref, w1_hbm, w3_hbm, w2_hbm, out_ref, n1_ref, a_ref, b_ref, hm_ref, acc_ref, wbuf, wsem):
        s = pl.program_id(1)
        t = pl.program_id(0) * ns + s

        def fetch(step, k):
            slot = step % 3
            return pltpu.make_async_copy((w1_hbm, w3_hbm, w2_hbm)[k].at[step % ns], wbuf.at[slot, k], wsem.at[slot, k])

        @pl.when(t == 0)
        def _():
            for k in range(3):
                fetch(0, k).start()
                fetch(1, k).start()

        for k in range(3):
            fetch(t, k).wait()

        @pl.when(t + 2 < total)
        def _():
            for k in range(3):
                fetch(t + 2, k).start()

        slot = t % 3
        w1_ref, w3_ref, w2_ref = wbuf.at[slot, 0], wbuf.at[slot, 1], wbuf.at[slot, 2]

        @pl.when(s == 0)
        def _():
            xv = h_ref[...]
            rstd = lax.rsqrt(jnp.mean(xv * xv, axis=-1, keepdims=True) + EPS)
            n1_ref[...] = (xv * rstd * g_ref[...]).astype(BF16)
            acc_ref[...] = jnp.zeros_like(acc_ref)

        def up(j):
            n1 = n1_ref[j * sub:(j + 1) * sub, :]
            return _dot(n1, w1_ref[...], NT), _dot(n1, w3_ref[...], NT)

        cur = up(0)
        for j in range(rows // sub):
            nxt = up(j + 1) if (j + 1) * sub < rows else None
            a, b = cur
            r = slice(j * sub, (j + 1) * sub)
            hm = (a * _sigmoid(a) * b).astype(BF16)
            a_ref[r, :] = a.astype(BF16)
            b_ref[r, :] = b.astype(BF16)
            hm_ref[r, :] = hm
            acc_ref[r, :] += _dot(hm, w2_ref[...])
            cur = nxt

        @pl.when(s == ns - 1)
        def _():
            out_ref[...] = h_ref[...] + 0.5 * acc_ref[...]

    hid_shape = jax.ShapeDtypeStruct((ns, n, fs), BF16)
    plan.before(f"{tag}_fwd")
    out, n1, a, b, hm = _pcall(
        body, name=f"{tag}_fwd", grid=(n // rows, ns), in_specs=[row, gain, ANY, ANY, ANY],
        out_specs=[row, row, hid, hid, hid],
        out_shape=[jax.ShapeDtypeStruct((n, d), F32), jax.ShapeDtypeStruct((n, d), BF16), hid_shape, hid_shape, hid_shape],
        scratch_shapes=[pltpu.VMEM((rows, d), F32), pltpu.VMEM((3, 3, fs, d), BF16), pltpu.SemaphoreType.DMA((3, 3))],
        compiler_params=_params())(h, g, w1t, w3t, w2)
    plan.after(f"{tag}_fwd")
    return out, (h, n1, a, b, hm)


def _wgrad(a3, b, name, alpha=1.0):
    ns, n, fs = a3.shape
    d = b.shape[1]
    tk = _pick(n, 1024, 16)

    def body(a_ref, b_ref, o_ref):
        @pl.when(pl.program_id(0) == 0)
        def _():
            o_ref[...] = jnp.zeros_like(o_ref)

        bv = b_ref[...].astype(BF16)
        for s in range(ns):
            part = _dot(a_ref[s], bv, TN)
            o_ref[s] += part if alpha == 1.0 else alpha * part

    return _pcall(body, name=name, grid=(n // tk,),
                  in_specs=[pl.BlockSpec((ns, tk, fs), lambda k: (0, k, 0)), pl.BlockSpec((tk, d), lambda k: (k, 0))],
                  out_specs=pl.BlockSpec((ns, fs, d), lambda k: (0, 0, 0)),
                  out_shape=jax.ShapeDtypeStruct((ns, fs, d), F32), compiler_params=_params())(a3, b)


def _ffn_bwd(dout, saved, g, w1, w3, w2, tag, plan):
    h, n1, a, b, hm = saved
    n, d = h.shape
    ns, fs, _ = w2.shape
    rows, row, gain, w_row, hid = _ffn_specs(n, d, fs, FFN_BWD_ROWS)
    sub = _pick(rows, FFN_SUB_ROWS, 16)

    total = (n // rows) * ns

    def body(do_ref, h_ref, g_ref, a_ref, b_ref, w1_hbm, w3_hbm, w2_hbm, dh_ref, da_ref, db_ref, dg_ref, acc_ref,
             wbuf, wsem):
        i, s = pl.program_id(0), pl.program_id(1)
        t = i * ns + s

        def fetch(step, k):
            slot = step % 3
            return pltpu.make_async_copy((w1_hbm, w3_hbm, w2_hbm)[k].at[step % ns], wbuf.at[slot, k], wsem.at[slot, k])

        @pl.when(t == 0)
        def _():
            for k in range(3):
                fetch(0, k).start()
                fetch(1, k).start()

        for k in range(3):
            fetch(t, k).wait()

        @pl.when(t + 2 < total)
        def _():
            for k in range(3):
                fetch(t + 2, k).start()

        slot = t % 3
        w1_ref, w3_ref, w2_ref = wbuf.at[slot, 0], wbuf.at[slot, 1], wbuf.at[slot, 2]

        @pl.when(s == 0)
        def _():
            acc_ref[...] = jnp.zeros_like(acc_ref)

        @pl.when((s == 0) & (i == 0))
        def _():
            dg_ref[...] = jnp.zeros_like(dg_ref)

        def up(j):
            return _dot(0.5 * do_ref[j * sub:(j + 1) * sub, :], w2_ref[...], NT)

        cur = up(0)
        for j in range(rows // sub):
            nxt = up(j + 1) if (j + 1) * sub < rows else None
            r = slice(j * sub, (j + 1) * sub)
            av, bv = a_ref[r, :].astype(F32), b_ref[r, :].astype(F32)
            sg = _sigmoid(av)
            da = (cur * bv * (sg * (1.0 + av * (1.0 - sg)))).astype(BF16)
            db = (cur * av * sg).astype(BF16)
            da_ref[r, :] = da
            db_ref[r, :] = db
            acc_ref[r, :] += _dot(da, w1_ref[...]) + _dot(db, w3_ref[...])
            cur = nxt

        @pl.when(s == ns - 1)
        def _():
            xv, dn = h_ref[...], acc_ref[...]
            rstd = lax.rsqrt(jnp.mean(xv * xv, axis=-1, keepdims=True) + EPS)
            xh = xv * rstd
            dg_ref[...] += jnp.sum(dn * xh, axis=0, keepdims=True)
            dxh = dn * g_ref[...]
            dh_ref[...] = do_ref[...] + rstd * (dxh - xh * jnp.mean(dxh * xh, axis=-1, keepdims=True))

    hid_shape = jax.ShapeDtypeStruct((ns, n, fs), BF16)
    plan.before(f"{tag}_bwd")
    dh, da, db, dg = _pcall(
        body, name=f"{tag}_bwd", grid=(n // rows, ns), in_specs=[row, row, gain, hid, hid, ANY, ANY, ANY],
        out_specs=[row, hid, hid, gain],
        out_shape=[jax.ShapeDtypeStruct((n, d), F32), hid_shape, hid_shape, jax.ShapeDtypeStruct((1, d), F32)],
        scratch_shapes=[pltpu.VMEM((rows, d), F32), pltpu.VMEM((3, 3, fs, d), BF16), pltpu.SemaphoreType.DMA((3, 3))],
        compiler_params=_params())(dout, h, g, a, b, w1, w3, w2)
    plan.after(f"{tag}_bwd")
    plan.grads[f"{tag}_norm"] = dg
    plan.before(f"{tag}_gw2")
    gw2 = _wgrad(hm, dout, f"{tag}_gw2", alpha=0.5)
    plan.after(f"{tag}_gw2")
    plan.grads[f"{tag}_w2"] = gw2
    plan.before(f"{tag}_gw1")
    gw1 = _wgrad(da, n1, f"{tag}_gw1")
    plan.after(f"{tag}_gw1")
    plan.grads[f"{tag}_w1"] = gw1
    plan.before(f"{tag}_gw3")
    gw3 = _wgrad(db, n1, f"{tag}_gw3")
    plan.after(f"{tag}_gw3")
    return dh, dg, gw1, gw3, gw2


def _local_step(x, tgt, plan):
    n = x.shape[0]
    grads = plan.grads

    def f(name):
        w = plan.get(name)
        return w.reshape(1, D_MODEL) if name.endswith('_norm') and name != 'gla_out_norm' else w

    def carried(tag, fn, *args, **kw):
        plan.before(tag)
        out = fn(*args, **kw)
        plan.after(tag)
        return out

    h1, ffn1 = _ffn_fwd(x, f('ffn1_norm'), f('ffn1_w1'), f('ffn1_w3'), f('ffn1_w2'), "ffn1", plan)
    u = carried("mix_rms", _rms_fwd, h1, f('mix_norm'), "mix_rms")
    w_in = f('w_in')
    w_a = jnp.concatenate([w_in[:, :512], _pad_heads(w_in[:, 512:768]), _pad_heads(w_in[:, 768:1024]), w_in[:, 1024:2048]],
                          axis=1)
    w_al = jnp.pad(w_in[:, 2048:2048 + GLA_RANK], ((0, 0), (0, LANE - GLA_RANK)))
    w_g = w_in[:, 2048 + GLA_RANK:]
    za = carried("in_a", _mm, u, w_a, name="in_a")
    zg = carried("in_g", _mm, u, w_g, name="in_g")
    al = _mm(u, w_al, name="in_al")
    ar, ai, bbar_re, bbar_im = _s5_discretize(f('s5_lambda_re'), f('s5_lambda_im'), f('s5_log_dt'), f('s5_b_re'), f('s5_b_im'))
    t_b = _bd_tiles(bbar_re.transpose(0, 2, 1), bbar_im.transpose(0, 2, 1)).astype(BF16)
    t_c = _bd_tiles(f('s5_c_re'), -f('s5_c_im')).astype(BF16)
    ar8 = jnp.broadcast_to(ar.reshape(1, S5_GP), (SEG, S5_GP))
    ai8 = jnp.broadcast_to(ai.reshape(1, S5_GP), (SEG, S5_GP))
    pw_r, pw_i = _segment_powers(ar, ai, n // SEG)
    dskip = f('s5_d').reshape(1, S5_W)
    u_s5 = _permute_rows(za[:, :S5_W])
    xs = _s5_scan(u_s5, t_b, ar8, ai8, pw_r, pw_i, "s5_scan")
    ys_p = _bd_reduce(xs, t_c, _scale_rows(u_s5, dskip, "s5_skip"), "s5_y")
    ys = _unpermute_rows(ys_p)
    zgelu = _gelu_fwd(ys, "s5_gelu")
    t_glu = _mm(zgelu, f('s5_glu_w'), bias=f('s5_glu_b').reshape(1, S5_W), name="s5_glu_t")
    y_s5 = _glu_fwd(zgelu, t_glu, "s5_glu")
    wup = jnp.pad(f('gla_a_up_w'), ((0, LANE - GLA_RANK), (0, 0)))
    wup_h = wup.reshape(LANE, GLA_HEADS, GLA_DK).transpose(1, 0, 2)
    bup_h = f('gla_a_up_b').reshape(GLA_HEADS, 1, GLA_DK)
    gn_h = f('gla_out_norm').reshape(GLA_HEADS, 1, GLA_DV)
    y_gla, s_prev = carried("gla_fwd", _gla_fwd, za, al, wup_h, bup_h, gn_h, "gla_fwd")
    ps = _mm(y_s5, f('proj_s5'), name="proj_s5")
    pg = carried("proj_gla", _mm, y_gla, f('proj_gla'), name="proj_gla")
    merged = _merge_fwd(zg, ps, pg, "merge")
    h2 = _mm(merged, f('w_out'), res=h1, name="w_out")
    h3, ffn2 = _ffn_fwd(h2, f('ffn2_norm'), f('ffn2_w1'), f('ffn2_w3'), f('ffn2_w2'), "ffn2", plan)
    loss, dh3, g_final = _final_loss(h3, f('final_norm').reshape(1, D_MODEL), tgt, "loss")
    plan.loss = loss[0, 0]
    grads['final_norm'] = g_final.reshape(D_MODEL)
    dh2, grads['ffn2_norm'], grads['ffn2_w1'], grads['ffn2_w3'], grads['ffn2_w2'] = _ffn_bwd(
        dh3, ffn2, f('ffn2_norm'), f('ffn2_w1'), f('ffn2_w3'), f('ffn2_w2'), "ffn2", plan)
    dm = carried("d_merged", _mm, dh2, f('w_out'), tb=True, name="d_merged")
    grads['w_out'] = _mm(merged, dh2, ta=True, name="g_w_out")
    dps, dpg, dzg = carried("d_merge", _merge_bwd, dm, zg, ps, pg, "d_merge")
    grads['proj_s5'] = _mm(y_s5, dps, ta=True, name="g_proj_s5")
    grads['proj_gla'] = _mm(y_gla, dpg, ta=True, name="g_proj_gla")
    dy_s5 = _mm(dps, f('proj_s5'), tb=True, name="d_y_s5")
    dy_gla = _mm(dpg, f('proj_gla'), tb=True, name="d_y_gla")
    dzgelu, dt_glu, g_glu_b = _glu_bwd1(dy_s5, zgelu, t_glu, "d_glu")
    grads['s5_glu_b'] = g_glu_b.reshape(S5_W)
    grads['s5_glu_w'] = _mm(zgelu, dt_glu, ta=True, name="g_glu_w")
    dzgelu = _mm(dt_glu, f('s5_glu_w'), tb=True, res=dzgelu, name="d_gelu")
    dys, du_skip, g_d = _glu_bwd2(_permute_rows(dzgelu), ys_p, u_s5, dskip, "d_s5_y")
    grads['s5_d'] = g_d.reshape(S5_G, S5_H)
    lam, da8 = _s5_scan_bwd(dys, t_c, xs, ar8, ai8, pw_r, pw_i, "s5_scan_bwd")
    g_c = _bd_blocks(_bd_outer(dys, xs, "g_s5_c"))
    grads['s5_c_re'], grads['s5_c_im'] = g_c[0], -g_c[1]
    g_b = _bd_blocks(_bd_outer(u_s5, lam, "g_s5_b")).transpose(0, 1, 3, 2)
    g_bbar_re, g_bbar_im = g_b[0], g_b[1]
    da = jnp.sum(da8, axis=0)
    g_ar, g_ai = da[:S5_GP].reshape(S5_G, S5_P), da[S5_GP:].reshape(S5_G, S5_P)
    _, disc_vjp = jax.vjp(_s5_discretize, f('s5_lambda_re'), f('s5_lambda_im'), f('s5_log_dt'), f('s5_b_re'), f('s5_b_im'))
    (grads['s5_lambda_re'], grads['s5_lambda_im'], grads['s5_log_dt'], grads['s5_b_re'],
     grads['s5_b_im']) = disc_vjp((g_ar, g_ai, g_bbar_re, g_bbar_im))
    du_s5 = _unpermute_rows(_bd_reduce(lam, t_b, du_skip, "d_s5_u"))
    dza, dz, dgn, dbup = carried("gla_bwd", _gla_bwd, za, al, wup_h, bup_h, gn_h, s_prev, dy_gla, du_s5, "gla_bwd")
    grads['gla_out_norm'] = dgn.reshape(GLA_HEADS * GLA_DV)
    grads['gla_a_up_b'] = dbup.reshape(GLA_HEADS * GLA_DK)
    grads['gla_a_up_w'] = _unpad_heads(_mm(al, dz, ta=True, name="g_a_up")[:GLA_RANK])
    dal = _mm(dz, _pad_heads(wup), tb=True, out_dtype=BF16, name="d_a_low")
    g_wa = _mm(u, dza, ta=True, name="g_in_a")
    g_wg = _mm(u, dzg, ta=True, name="g_in_g")
    g_wal = _mm(u, dal, ta=True, name="g_in_al")
    grads['w_in'] = jnp.concatenate([g_wa[:, :512], _unpad_heads(g_wa[:, 512:1024]), _unpad_heads(g_wa[:, 1024:1536]),
                                     g_wa[:, 1536:], g_wal[:, :GLA_RANK], g_wg], axis=1)
    du = carried("d_u_a", _mm, dza, w_a, tb=True, name="d_u_a")
    du = _mm(dzg, w_g, tb=True, res=du, name="d_u_g")
    du = _mm(dal, w_al, tb=True, res=du, name="d_u_al")
    dh1, g_mix = carried("d_mix_rms", _rms_bwd, h1, f('mix_norm'), du, dh2, "d_mix_rms")
    grads['mix_norm'] = g_mix
    dx, grads['ffn1_norm'], grads['ffn1_w1'], grads['ffn1_w3'], grads['ffn1_w2'] = _ffn_bwd(
        dh1, ffn1, f('ffn1_norm'), f('ffn1_w1'), f('ffn1_w3'), f('ffn1_w2'), "ffn1", plan)
    return loss[0, 0], dx


MIXER_WEIGHTS = ['w_in', 's5_glu_w', 'proj_s5', 'proj_gla', 'w_out', 'gla_a_up_w']
FFN1_WEIGHTS, FFN2_WEIGHTS = FFN_WEIGHTS[:3], FFN_WEIGHTS[3:]
TRANSPOSED = ['ffn1_w1', 'ffn1_w3', 'ffn2_w1', 'ffn2_w3']


def _local_shard(w, nm):
    return jnp.swapaxes(w, 1, 2)[0] if nm in TRANSPOSED else w[0]
FFN1_EARLY = ['ffn1_w2']
FFN1_LATE = ['ffn1_w1', 'ffn1_w3']
GRAD_GROUPS = {'ffn2': FFN2_WEIGHTS, 'mixer': ['w_out', 'proj_s5', 'proj_gla', 's5_glu_w', 'w_in'], 'ffn1': FFN1_WEIGHTS}


class _Plan:
    def __init__(self, a, c_arr, s_arr):
        self.a, self.c_arr, self.s_arr = a, c_arr, s_arr
        self.grads, self.weights, self.riding = {}, {}, {}
        self.g4s, self.chip_sums, self.halves, self.sib_halves = {}, {}, {}, {}
        for nm in SMALL:
            if nm != 'gla_a_up_w':
                self.weights[nm] = a[nm] if nm == 'final_norm' else a[nm][0]
        ici = _side_gather_ici(self._shards(FFN1_WEIGHTS))
        _run_side(ici, "gather_ffn1_ici")
        self._gathered(FFN1_WEIGHTS, _run_side(_side_gather_d2d(ici.outs), "gather_ffn1_d2d"))

    def _shards(self, names):
        return [_local_shard(self.a[nm], nm).astype(F32 if nm == 'gla_a_up_w' else BF16) for nm in names]

    def _gathered(self, names, arrs):
        for nm, g4 in zip(names, arrs):
            if nm in FFN_WEIGHTS:
                self.weights[nm] = g4
            elif nm in COL_SHARDED:
                self.weights[nm] = jnp.concatenate([g4[s] for s in range(4)], axis=1)
            else:
                self.weights[nm] = g4.reshape(4 * g4.shape[1], g4.shape[2])

    def get(self, name):
        return self.weights[name]

    def _shard_major(self, nm):
        g = self.grads[nm]
        if nm in FFN_WEIGHTS:
            return g
        if nm in COL_SHARDED:
            return jnp.stack(jnp.split(g, 4, axis=1))
        return g.reshape(4, g.shape[0] // 4, g.shape[1])

    def _schedule(self, tag):
        grp = GRAD_GROUPS
        gathers = {"ffn1_fwd": ('ici', MIXER_WEIGHTS), "mix_rms": ('d2d', MIXER_WEIGHTS),
                   "in_a": ('ici', FFN2_WEIGHTS[:1]), "in_g": ('d2d', FFN2_WEIGHTS[:1]),
                   "gla_fwd": ('ici', FFN2_WEIGHTS[1:]), "proj_gla": ('d2d', FFN2_WEIGHTS[1:])}
        if tag in gathers:
            kind, names = gathers[tag]
            key = tuple(names)
            if kind == 'ici':
                return [(_side_gather_ici(self._shards(names)), lambda outs: self.riding.update({key: outs}))]
            return [(_side_gather_d2d(self.riding[key]), lambda outs: self._gathered(names, outs))]
        steps = {"ffn2_gw1": (['ffn2_w2'], 0), "ffn2_gw3": (['ffn2_w1'], 0), "d_merged": (['ffn2_w3'], 0),
                 "gla_bwd": (grp['ffn2'], 1), "d_mix_rms": (grp['ffn2'], 2),
                 "d_u_a": (grp['mixer'], 0), "ffn1_bwd": (grp['mixer'], 1), "ffn1_gw2": (grp['mixer'], 2),
                 "ffn1_gw1": (FFN1_EARLY, 0), "ffn1_gw3": (FFN1_EARLY, 1), "adamw_early": (FFN1_LATE, 1)}
        entries = [self._reduce_stage(*steps[tag])] if tag in steps else []
        if tag == "ffn1_gw2":
            entries.append(self._small_stage(0))
        if tag == "ffn1_gw1":
            entries.append(self._small_stage(1))
        return entries

    def _small_stage(self, stage):
        if stage == 0:
            a, grads = self.a, self.grads
            self.small_parts = ([grads[nm].reshape(a[nm].shape) for nm in SMALL if nm != 'gla_a_up_w']
                                + [grads['gla_a_up_w'], self.loss.reshape(1)])
            packed = _pack_small(self.small_parts)

            def done(outs):
                self.small_pair = _small_add(packed, outs[0], "small_sum_pair")
            return _side_small_sibling(packed), done

        def done(outs):
            self.small_total = _small_add(self.small_pair, outs[0], "small_sum_chips")
        return _side_small_chips(self.small_pair), done

    def _reduce_stage(self, names, stage):
        if stage == 0:
            for nm in names:
                self.g4s[nm] = self._shard_major(nm)

            def done(outs):
                sums = _chip_sums([self.g4s[nm] for nm in names], outs, self.c_arr, f"chip_sum_{names[0]}")
                self.chip_sums.update(zip(names, sums))
            return _side_swap_halves([self.g4s[nm] for nm in names]), done
        if stage == 1:
            def done(outs):
                halves = _owner_sums([self.chip_sums[nm] for nm in names], outs, self.s_arr, f"owner_sum_{names[0]}")
                self.halves.update(zip(names, halves))
            return _side_scatter([self.chip_sums[nm] for nm in names]), done

        def done(outs):
            self.sib_halves.update(zip(names, outs))
        return _side_swap_reduced([self.halves[nm] for nm in names]), done

    def before(self, tag):
        entries = self._schedule(tag)
        if entries:
            merged = _merge_sides([side for side, _ in entries])
            self.riding[tag] = (merged, entries)
            _RIDER.append(merged)

    def after(self, tag):
        if tag in self.riding:
            merged, entries = self.riding.pop(tag)
            assert not _RIDER and merged.outs is not None, tag
            pos = 0
            for side, done in entries:
                done(merged.outs[pos:pos + len(side.out_shapes)])
                pos += len(side.out_shapes)

    def finish_alone(self, stage):
        names = FFN1_LATE if stage == 0 else GRAD_GROUPS['ffn1']
        side, done = self._reduce_stage(names, stage)
        done(_run_side(side, f"grad_ffn1_stage{stage}"))


def _train_step(a):
    x = a['x'][0]
    tgt = a['loss_target'][0]
    xi, yi, ci = lax.axis_index("x"), lax.axis_index("y"), lax.axis_index("c")
    c_arr = jnp.reshape(ci, (1,)).astype(jnp.int32)
    s_arr = jnp.reshape(2 * xi + yi, (1,)).astype(jnp.int32)
    plan = _Plan(a, c_arr, s_arr)
    loss, dx = _local_step(x, tgt, plan)
    red = {}
    small_sum = _unpack_small(plan.small_total, plan.small_parts)
    small_names = [nm for nm in SMALL if nm != 'gla_a_up_w']
    for nm, g in zip(small_names, small_sum[:-2]):
        red[nm] = g
    loss = small_sum[-1].reshape(())
    g_up = small_sum[-2]
    red['gla_a_up_w'] = lax.dynamic_slice(g_up, (0, (2 * xi + yi) * GLA_DK), (GLA_RANK, GLA_DK))
    out_g, out_d, out_m, out_v = {}, {}, {}, {}

    def update(names, tag):
        items = [(_local_shard(a[nm], nm), plan.halves[nm], plan.sib_halves[nm], _local_shard(a['m_' + nm], nm),
                  _local_shard(a['v_' + nm], nm)) for nm in names]
        plan.before(tag)
        res = _adamw_group(items, c_arr, tag)
        plan.after(tag)
        for k, nm in enumerate(names):
            back = (lambda t: jnp.swapaxes(t[None], 1, 2)) if nm in TRANSPOSED else (lambda t: t[None])
            out_g[nm], out_d[nm], out_m[nm], out_v[nm] = (back(t) for t in res[4 * k:4 * k + 4])

    plan.finish_alone(0)
    update([nm for nm in SHARDED if nm not in GRAD_GROUPS['ffn1']], "adamw_early")
    plan.finish_alone(2)
    update(GRAD_GROUPS['ffn1'], "adamw_ffn1")
    rest = [nm for nm in WEIGHTS if nm not in SHARDED]
    pk = lambda pre: _pack_small([a[pre + nm] for nm in rest])
    d, nm_, nv_ = _adamw(pk(''), _pack_small([red[nm] for nm in rest]), pk('m_'), pk('v_'), "adamw_small")
    like = [a[nm] for nm in rest]
    for nm, g, dd, mm_, vv_ in zip(rest, [red[nm].reshape(a[nm].shape) for nm in rest], _unpack_small(d, like),
                                   _unpack_small(nm_, like), _unpack_small(nv_, like)):
        out_g[nm], out_d[nm], out_m[nm], out_v[nm] = g, dd, mm_, vv_
    return (loss, dx[None], *[out_g[nm] for nm in WEIGHTS], *[out_d[nm] for nm in WEIGHTS],
            *[out_m[nm] for nm in WEIGHTS], *[out_v[nm] for nm in WEIGHTS])


def kernel(x, ffn1_norm, ffn1_w1, ffn1_w3, ffn1_w2, mix_norm, w_in, s5_lambda_re, s5_lambda_im, s5_log_dt, s5_b_re, s5_b_im, s5_c_re, s5_c_im, s5_d, s5_glu_w, s5_glu_b, gla_a_up_w, gla_a_up_b, gla_out_norm, proj_s5, proj_gla, w_out, ffn2_norm, ffn2_w1, ffn2_w3, ffn2_w2, final_norm, loss_target, m_ffn1_norm, m_ffn1_w1, m_ffn1_w3, m_ffn1_w2, m_mix_norm, m_w_in, m_s5_lambda_re, m_s5_lambda_im, m_s5_log_dt, m_s5_b_re, m_s5_b_im, m_s5_c_re, m_s5_c_im, m_s5_d, m_s5_glu_w, m_s5_glu_b, m_gla_a_up_w, m_gla_a_up_b, m_gla_out_norm, m_proj_s5, m_proj_gla, m_w_out, m_ffn2_norm, m_ffn2_w1, m_ffn2_w3, m_ffn2_w2, m_final_norm, v_ffn1_norm, v_ffn1_w1, v_ffn1_w3, v_ffn1_w2, v_mix_norm, v_w_in, v_s5_lambda_re, v_s5_lambda_im, v_s5_log_dt, v_s5_b_re, v_s5_b_im, v_s5_c_re, v_s5_c_im, v_s5_d, v_s5_glu_w, v_s5_glu_b, v_gla_a_up_w, v_gla_a_up_b, v_gla_out_norm, v_proj_s5, v_proj_gla, v_w_out, v_ffn2_norm, v_ffn2_w1, v_ffn2_w3, v_ffn2_w2, v_final_norm):
    return _train_step(dict(locals()))
```
